```python
import math
import jax, jax.numpy as jnp
from jax import lax
import numpy as np

D_MODEL = 1024
BATCH = 8
SEQ = 8192
DEPTH = 2

MEM_LEN = 256
MIX_W = D_MODEL
GROUP_W = MIX_W // 4
N_FOX_HEADS = 4
N_SB_HEADS = 4
N_MLA_HEADS = 4
N_MEM_HEADS = 4
HEAD_DIM = GROUP_W // 4
MLA_Q_RANK = 256
MLA_KV_RANK = 128
MLA_NOPE = HEAD_DIM
MLA_ROPE = 32
MLA_V = GROUP_W // N_MLA_HEADS
ROPE_THETA = 10000.0
BLOCK_Q = 128
LN_EPS = 1e-5
RMS_EPS = 1e-6
FOX_FORGET_BIAS_INIT = 3.0
DEEPNORM_ALPHA = (2 * DEPTH) ** 0.25
DEEPNORM_BETA = (8 * DEPTH) ** -0.25
SPLIT_SIZES = (GROUP_W, GROUP_W, GROUP_W, N_FOX_HEADS,
               GROUP_W, GROUP_W, GROUP_W,
               MLA_Q_RANK, MLA_KV_RANK, MLA_ROPE,
               GROUP_W,
               MIX_W)
IN_COLS = sum(SPLIT_SIZES)

kernel_name = 'hybrid_fox_stickbreak_mla_memory_deepnorm'


def _layer_norm(x, g, b):
    xf = x.astype(jnp.float32)
    mu = jnp.mean(xf, axis=-1, keepdims=True)
    var = jnp.mean(jnp.square(xf - mu), axis=-1, keepdims=True)
    y = (xf - mu) * lax.rsqrt(var + LN_EPS) * g.astype(jnp.float32) + b.astype(jnp.float32)
    return y.astype(x.dtype)


def _rms_norm(x, g):
    xf = x.astype(jnp.float32)
    y = xf * lax.rsqrt(jnp.mean(jnp.square(xf), axis=-1, keepdims=True) + RMS_EPS)
    return (y * g.astype(jnp.float32)).astype(x.dtype)


def _heads(t, n):
    b, s, _ = t.shape
    return t.reshape(b, s, n, -1).transpose(0, 2, 1, 3)


def _merge(t):
    b, h, s, d = t.shape
    return t.transpose(0, 2, 1, 3).reshape(b, s, h * d)


def _rope(x, positions):
    half = x.shape[-1] // 2
    inv_freq = ROPE_THETA ** (-jnp.arange(half, dtype=jnp.float32) / half)
    ang = positions.astype(jnp.float32)[:, None] * inv_freq[None, :]
    ang = ang.reshape((ang.shape[0],) + (1,) * (x.ndim - 3) + (half,))
    cos, sin = jnp.cos(ang), jnp.sin(ang)
    xf = x.astype(jnp.float32)
    x1, x2 = xf[..., :half], xf[..., half:]
    return jnp.concatenate([x1 * cos - x2 * sin, x1 * sin + x2 * cos], axis=-1).astype(x.dtype)


def _sweep_query_blocks(block_fn, per_query):
    b, h, s = per_query[0].shape[:3]
    nb = s // BLOCK_Q
    blocks = tuple(jnp.moveaxis(a.reshape((b, h, nb, BLOCK_Q) + a.shape[3:]), 2, 0) for a in per_query)
    out = lax.map(lambda args: block_fn(args[0], *args[1:]), (jnp.arange(nb),) + blocks)
    out = jnp.moveaxis(out, 0, 2)
    return out.reshape(b, h, s, out.shape[-1])


def _causal_softmax_attention(q, k, v, scale, log_forget_cum=None):
    key_pos = jnp.arange(k.shape[2])

    def block(i, qb, *fq):
        q_pos = i * BLOCK_Q + jnp.arange(BLOCK_Q)
        logits = jnp.einsum('bhqd,bhkd->bhqk', qb, k).astype(jnp.float32) * scale
        if log_forget_cum is not None:
            logits = logits + fq[0][..., :, None] - log_forget_cum[:, :, None, :]
        mask = key_pos[None, :] <= q_pos[:, None]
        probs = jax.nn.softmax(jnp.where(mask, logits, -jnp.inf), axis=-1)
        return jnp.einsum('bhqk,bhkd->bhqd', probs.astype(v.dtype), v)

    per_query = (q,) if log_forget_cum is None else (q, log_forget_cum)
    return _sweep_query_blocks(block, per_query)


def _stick_breaking_attention(q, k, v, scale):
    key_pos = jnp.arange(k.shape[2])

    def block(i, qb):
        q_pos = i * BLOCK_Q + jnp.arange(BLOCK_Q)
        valid = key_pos[None, :] < q_pos[:, None]
        z = jnp.einsum('bhqd,bhkd->bhqk', qb, k).astype(jnp.float32) * scale
        log_keep = jnp.where(valid, jax.nn.log_sigmoid(-z), 0.0)
        log_tail = lax.cumsum(log_keep, axis=3, reverse=True) - log_keep
        w = jnp.where(valid, jnp.exp(jax.nn.log_sigmoid(z) + log_tail), 0.0)
        return jnp.einsum('bhqk,bhkd->bhqd', w.astype(v.dtype), v)

    return _sweep_query_blocks(block, (q,))


def _fwd_setup_inputs(seed: int = 0) -> dict:
    key = jax.random.key(seed)
    ks = jax.random.split(key, 18)
    f32 = jnp.float32
    nrm = lambda k, shape: jax.random.normal(k, shape, f32)
    return {
        'x': nrm(ks[0], (BATCH, SEQ, D_MODEL)),
        'mem': nrm(ks[1], (BATCH, MEM_LEN, D_MODEL)),
        'ln_in_g': 1.0 + 0.02 * nrm(ks[2], (D_MODEL,)),
        'ln_in_b': 0.02 * nrm(ks[3], (D_MODEL,)),
        'mem_ln_g': 1.0 + 0.02 * nrm(ks[4], (D_MODEL,)),
        'mem_ln_b': 0.02 * nrm(ks[5], (D_MODEL,)),
        'w_in': nrm(ks[6], (DEPTH, D_MODEL, IN_COLS)) * D_MODEL ** -0.5,
        'b_forget': FOX_FORGET_BIAS_INIT + 0.1 * nrm(ks[7], (DEPTH, N_FOX_HEADS)),
        'mla_q_norm_g': 1.0 + 0.02 * nrm(ks[8], (DEPTH, MLA_Q_RANK)),
        'w_mla_q_up': nrm(ks[9], (DEPTH, MLA_Q_RANK, N_MLA_HEADS * (MLA_NOPE + MLA_ROPE))) * MLA_Q_RANK ** -0.5,
        'mla_kv_norm_g': 1.0 + 0.02 * nrm(ks[10], (DEPTH, MLA_KV_RANK)),
        'w_mla_kv_up': nrm(ks[11], (DEPTH, MLA_KV_RANK, N_MLA_HEADS * (MLA_NOPE + MLA_V))) * MLA_KV_RANK ** -0.5,
        'w_mem_kv': nrm(ks[12], (DEPTH, D_MODEL, 2 * GROUP_W)) * D_MODEL ** -0.5,
        'w_out': nrm(ks[13], (DEPTH, MIX_W, D_MODEL)) * (MIX_W ** -0.5 * DEEPNORM_BETA),
        'ln_g': 1.0 + 0.02 * nrm(ks[14], (DEPTH, D_MODEL)),
        'ln_b': 0.02 * nrm(ks[15], (DEPTH, D_MODEL)),
    }


def _fwd_reference(x, mem, ln_in_g, ln_in_b, mem_ln_g, mem_ln_b, w_in, b_forget,
              mla_q_norm_g, w_mla_q_up, mla_kv_norm_g, w_mla_kv_up, w_mem_kv,
              w_out, ln_g, ln_b):
    b, s, _ = x.shape
    positions = jnp.arange(s)
    offsets = [int(o) for o in np.cumsum(SPLIT_SIZES)[:-1]]
    head_scale = HEAD_DIM ** -0.5
    mla_scale = (MLA_NOPE + MLA_ROPE) ** -0.5

    h_res = _layer_norm(x, ln_in_g, ln_in_b)
    mem_n = _layer_norm(mem, mem_ln_g, mem_ln_b)

    for l in range(DEPTH):
        proj = jnp.einsum('bsd,dc->bsc', h_res, w_in[l])
        (fq, fk, fv, f_logit, sq, sk, sv, c_q, c_kv, k_rot, mq, gate) = jnp.split(proj, offsets, axis=-1)

        log_f = jax.nn.log_sigmoid((f_logit + b_forget[l]).astype(jnp.float32))
        f_cum = jnp.cumsum(log_f, axis=1).transpose(0, 2, 1)
        out_fox = _causal_softmax_attention(_heads(fq, N_FOX_HEADS), _heads(fk, N_FOX_HEADS),
                                            _heads(fv, N_FOX_HEADS), head_scale, f_cum)

        out_sb = _stick_breaking_attention(_heads(sq, N_SB_HEADS), _heads(sk, N_SB_HEADS),
                                           _heads(sv, N_SB_HEADS), head_scale)

        q_mla = jnp.einsum('bsr,rc->bsc', _rms_norm(c_q, mla_q_norm_g[l]), w_mla_q_up[l])
        q_mla = q_mla.reshape(b, s, N_MLA_HEADS, MLA_NOPE + MLA_ROPE)
        q_full = jnp.concatenate([q_mla[..., :MLA_NOPE], _rope(q_mla[..., MLA_NOPE:], positions)], axis=-1)
        kv_mla = jnp.einsum('bsr,rc->bsc', _rms_norm(c_kv, mla_kv_norm_g[l]), w_mla_kv_up[l])
        kv_mla = kv_mla.reshape(b, s, N_MLA_HEADS, MLA_NOPE + MLA_V)
        k_rope = jnp.broadcast_to(_rope(k_rot, positions)[:, :, None, :], (b, s, N_MLA_HEADS, MLA_ROPE))
        k_full = jnp.concatenate([kv_mla[..., :MLA_NOPE], k_rope], axis=-1)
        v_mla = kv_mla[..., MLA_NOPE:]
        out_mla = _causal_softmax_attention(q_full.transpose(0, 2, 1, 3), k_full.transpose(0, 2, 1, 3),
                                            v_mla.transpose(0, 2, 1, 3), mla_scale)

        mkv = jnp.einsum('bmd,dc->bmc', mem_n, w_mem_kv[l])
        mk, mv = _heads(mkv[..., :GROUP_W], N_MEM_HEADS), _heads(mkv[..., GROUP_W:], N_MEM_HEADS)
        mem_logits = jnp.einsum('bhsd,bhmd->bhsm', _heads(mq, N_MEM_HEADS), mk).astype(jnp.float32) * head_scale
        mem_p = jax.nn.softmax(mem_logits, axis=-1)
        out_mem = jnp.einsum('bhsm,bhmd->bhsd', mem_p.astype(mv.dtype), mv)

        mixed = jnp.concatenate([_merge(out_fox), _merge(out_sb), _merge(out_mla), _merge(out_mem)], axis=-1)
        y = jnp.einsum('bsc,cd->bsd', mixed * jax.nn.silu(gate), w_out[l])

        h_res = _layer_norm(DEEPNORM_ALPHA * h_res + y, ln_g[l], ln_b[l])

    return h_res


import jax as _jax
import jax.numpy as _jnp

TWIN_FORMAT = 'train_step'
FWD_PARAMS = ['x', 'mem', 'ln_in_g', 'ln_in_b', 'mem_ln_g', 'mem_ln_b', 'w_in', 'b_forget', 'mla_q_norm_g', 'w_mla_q_up', 'mla_kv_norm_g', 'w_mla_kv_up', 'w_mem_kv', 'w_out', 'ln_g', 'ln_b']
TWIN_WEIGHTS = ['ln_in_g', 'ln_in_b', 'mem_ln_g', 'mem_ln_b', 'w_in', 'b_forget', 'mla_q_norm_g', 'w_mla_q_up', 'mla_kv_norm_g', 'w_mla_kv_up', 'w_mem_kv', 'w_out', 'ln_g', 'ln_b']
TWIN_DIFF_INPUT = 'x'
TWIN_INPUTS = ['x', 'mem', 'ln_in_g', 'ln_in_b', 'mem_ln_g', 'mem_ln_b', 'w_in', 'b_forget', 'mla_q_norm_g', 'w_mla_q_up', 'mla_kv_norm_g', 'w_mla_kv_up', 'w_mem_kv', 'w_out', 'ln_g', 'ln_b', 'loss_target', 'm_ln_in_g', 'm_ln_in_b', 'm_mem_ln_g', 'm_mem_ln_b', 'm_w_in', 'm_b_forget', 'm_mla_q_norm_g', 'm_w_mla_q_up', 'm_mla_kv_norm_g', 'm_w_mla_kv_up', 'm_w_mem_kv', 'm_w_out', 'm_ln_g', 'm_ln_b', 'v_ln_in_g', 'v_ln_in_b', 'v_mem_ln_g', 'v_mem_ln_b', 'v_w_in', 'v_b_forget', 'v_mla_q_norm_g', 'v_w_mla_q_up', 'v_mla_kv_norm_g', 'v_w_mla_kv_up', 'v_w_mem_kv', 'v_w_out', 'v_ln_g', 'v_ln_b']
TWIN_OUTPUTS = ['loss', 'grad_x', 'grad_ln_in_g', 'grad_ln_in_b', 'grad_mem_ln_g', 'grad_mem_ln_b', 'grad_w_in', 'grad_b_forget', 'grad_mla_q_norm_g', 'grad_w_mla_q_up', 'grad_mla_kv_norm_g', 'grad_w_mla_kv_up', 'grad_w_mem_kv', 'grad_w_out', 'grad_ln_g', 'grad_ln_b', 'delta_ln_in_g', 'delta_ln_in_b', 'delta_mem_ln_g', 'delta_mem_ln_b', 'delta_w_in', 'delta_b_forget', 'delta_mla_q_norm_g', 'delta_w_mla_q_up', 'delta_mla_kv_norm_g', 'delta_w_mla_kv_up', 'delta_w_mem_kv', 'delta_w_out', 'delta_ln_g', 'delta_ln_b', 'new_m_ln_in_g', 'new_m_ln_in_b', 'new_m_mem_ln_g', 'new_m_mem_ln_b', 'new_m_w_in', 'new_m_b_forget', 'new_m_mla_q_norm_g', 'new_m_w_mla_q_up', 'new_m_mla_kv_norm_g', 'new_m_w_mla_kv_up', 'new_m_w_mem_kv', 'new_m_w_out', 'new_m_ln_g', 'new_m_ln_b', 'new_v_ln_in_g', 'new_v_ln_in_b', 'new_v_mem_ln_g', 'new_v_mem_ln_b', 'new_v_w_in', 'new_v_b_forget', 'new_v_mla_q_norm_g', 'new_v_w_mla_q_up', 'new_v_mla_kv_norm_g', 'new_v_w_mla_kv_up', 'new_v_w_mem_kv', 'new_v_w_out', 'new_v_ln_g', 'new_v_ln_b']
TWIN_LEAF_KINDS = {'loss': 'loss', 'grad_x': 'grad_x', 'grad_ln_in_g': 'grad_w', 'grad_ln_in_b': 'grad_w', 'grad_mem_ln_g': 'grad_w', 'grad_mem_ln_b': 'grad_w', 'grad_w_in': 'grad_w', 'grad_b_forget': 'grad_w', 'grad_mla_q_norm_g': 'grad_w', 'grad_w_mla_q_up': 'grad_w', 'grad_mla_kv_norm_g': 'grad_w', 'grad_w_mla_kv_up': 'grad_w', 'grad_w_mem_kv': 'grad_w', 'grad_w_out': 'grad_w', 'grad_ln_g': 'grad_w', 'grad_ln_b': 'grad_w', 'delta_ln_in_g': 'delta_w', 'delta_ln_in_b': 'delta_w', 'delta_mem_ln_g': 'delta_w', 'delta_mem_ln_b': 'delta_w', 'delta_w_in': 'delta_w', 'delta_b_forget': 'delta_w', 'delta_mla_q_norm_g': 'delta_w', 'delta_w_mla_q_up': 'delta_w', 'delta_mla_kv_norm_g': 'delta_w', 'delta_w_mla_kv_up': 'delta_w', 'delta_w_mem_kv': 'delta_w', 'delta_w_out': 'delta_w', 'delta_ln_g': 'delta_w', 'delta_ln_b': 'delta_w', 'new_m_ln_in_g': 'new_m', 'new_m_ln_in_b': 'new_m', 'new_m_mem_ln_g': 'new_m', 'new_m_mem_ln_b': 'new_m', 'new_m_w_in': 'new_m', 'new_m_b_forget': 'new_m', 'new_m_mla_q_norm_g': 'new_m', 'new_m_w_mla_q_up': 'new_m', 'new_m_mla_kv_norm_g': 'new_m', 'new_m_w_mla_kv_up': 'new_m', 'new_m_w_mem_kv': 'new_m', 'new_m_w_out': 'new_m', 'new_m_ln_g': 'new_m', 'new_m_ln_b': 'new_m', 'new_v_ln_in_g': 'new_v', 'new_v_ln_in_b': 'new_v', 'new_v_mem_ln_g': 'new_v', 'new_v_mem_ln_b': 'new_v', 'new_v_w_in': 'new_v', 'new_v_b_forget': 'new_v', 'new_v_mla_q_norm_g': 'new_v', 'new_v_w_mla_q_up': 'new_v', 'new_v_mla_kv_norm_g': 'new_v', 'new_v_w_mla_kv_up': 'new_v', 'new_v_w_mem_kv': 'new_v', 'new_v_w_out': 'new_v', 'new_v_ln_g': 'new_v', 'new_v_ln_b': 'new_v'}


def _forward(args):
    return _fwd_reference(*[args[k] for k in FWD_PARAMS])


def _output_shape():
    def fwd():
        inp = _fwd_setup_inputs(0)
        return _fwd_reference(*[inp[k] for k in FWD_PARAMS])
    out = _jax.eval_shape(fwd)
    return out.shape, out.dtype

N_MICROBATCH = 1
ADAM_LR = 0.001
ADAM_B1 = 0.9
ADAM_B2 = 0.999
ADAM_EPS = 1e-08
ADAM_WD = 0.01
ADAM_STEP = 10
PER_EXAMPLE_BATCH_AXIS = {'x': 0, 'mem': 0, 'loss_target': 0}
SHARED_INPUTS = []
_WEIGHT_DTYPES = {'ln_in_g': _jnp.float32, 'ln_in_b': _jnp.float32, 'mem_ln_g': _jnp.float32, 'mem_ln_b': _jnp.float32, 'w_in': _jnp.float32, 'b_forget': _jnp.float32, 'mla_q_norm_g': _jnp.float32, 'w_mla_q_up': _jnp.float32, 'mla_kv_norm_g': _jnp.float32, 'w_mla_kv_up': _jnp.float32, 'w_mem_kv': _jnp.float32, 'w_out': _jnp.float32, 'ln_g': _jnp.float32, 'ln_b': _jnp.float32}
MOMENT_SCALE = {'ln_in_g': 2.355015e+00, 'ln_in_b': 9.909560e-01, 'mem_ln_g': 6.545368e-03, 'mem_ln_b': 5.318467e-02, 'w_in': 1.860584e-02, 'b_forget': 7.013646e-02, 'mla_q_norm_g': 1.048534e-02, 'w_mla_q_up': 8.087838e-03, 'mla_kv_norm_g': 2.280350e-02, 'w_mla_kv_up': 1.023816e-02, 'w_mem_kv': 5.740315e-03, 'w_out': 4.000417e-02, 'ln_g': 4.528938e+01, 'ln_b': 1.471242e+00}


def _to_microbatches(a, axis):
    t = _jnp.moveaxis(a, axis, 0)
    t = t.reshape((N_MICROBATCH, t.shape[0] // N_MICROBATCH) + t.shape[1:])
    return _jnp.moveaxis(t, 1, axis + 1)


def setup_inputs(seed: int = 0) -> dict:
    inp = _fwd_setup_inputs(seed)
    key = _jax.random.fold_in(_jax.random.key(seed), 7919)
    shape, _ = _output_shape()
    out = dict(inp)
    out["loss_target"] = _jax.random.normal(_jax.random.fold_in(key, 0), shape, _jnp.float32)
    for i, name in enumerate(TWIN_WEIGHTS):
        w = inp[name].astype(_jnp.float32)
        if MOMENT_SCALE is None:
            s = _jnp.sqrt(_jnp.mean(_jnp.square(w)) + 1e-30)
        else:
            s = MOMENT_SCALE[name]
        km, kv = _jax.random.split(_jax.random.fold_in(key, i + 1))
        out[name] = w
        out["m_" + name] = s * _jax.random.normal(km, w.shape, _jnp.float32)
        out["v_" + name] = (s * s) * _jax.random.uniform(kv, w.shape, _jnp.float32, 0.5, 1.5)
    if N_MICROBATCH > 1:
        for name, axis in PER_EXAMPLE_BATCH_AXIS.items():
            out[name] = _to_microbatches(out[name], axis)
    return {'x': out['x'], 'mem': out['mem'], 'ln_in_g': out['ln_in_g'], 'ln_in_b': out['ln_in_b'], 'mem_ln_g': out['mem_ln_g'], 'mem_ln_b': out['mem_ln_b'], 'w_in': out['w_in'], 'b_forget': out['b_forget'], 'mla_q_norm_g': out['mla_q_norm_g'], 'w_mla_q_up': out['w_mla_q_up'], 'mla_kv_norm_g': out['mla_kv_norm_g'], 'w_mla_kv_up': out['w_mla_kv_up'], 'w_mem_kv': out['w_mem_kv'], 'w_out': out['w_out'], 'ln_g': out['ln_g'], 'ln_b': out['ln_b'], 'loss_target': out['loss_target'], 'm_ln_in_g': out['m_ln_in_g'], 'm_ln_in_b': out['m_ln_in_b'], 'm_mem_ln_g': out['m_mem_ln_g'], 'm_mem_ln_b': out['m_mem_ln_b'], 'm_w_in': out['m_w_in'], 'm_b_forget': out['m_b_forget'], 'm_mla_q_norm_g': out['m_mla_q_norm_g'], 'm_w_mla_q_up': out['m_w_mla_q_up'], 'm_mla_kv_norm_g': out['m_mla_kv_norm_g'], 'm_w_mla_kv_up': out['m_w_mla_kv_up'], 'm_w_mem_kv': out['m_w_mem_kv'], 'm_w_out': out['m_w_out'], 'm_ln_g': out['m_ln_g'], 'm_ln_b': out['m_ln_b'], 'v_ln_in_g': out['v_ln_in_g'], 'v_ln_in_b': out['v_ln_in_b'], 'v_mem_ln_g': out['v_mem_ln_g'], 'v_mem_ln_b': out['v_mem_ln_b'], 'v_w_in': out['v_w_in'], 'v_b_forget': out['v_b_forget'], 'v_mla_q_norm_g': out['v_mla_q_norm_g'], 'v_w_mla_q_up': out['v_w_mla_q_up'], 'v_mla_kv_norm_g': out['v_mla_kv_norm_g'], 'v_w_mla_kv_up': out['v_w_mla_kv_up'], 'v_w_mem_kv': out['v_w_mem_kv'], 'v_w_out': out['v_w_out'], 'v_ln_g': out['v_ln_g'], 'v_ln_b': out['v_ln_b']}


def _loss(weights, diff, rest, loss_target):
    with _jax.named_scope("forward"):
        args = {**rest, TWIN_DIFF_INPUT: diff, **{k: w.astype(_WEIGHT_DTYPES[k]) for k, w in weights.items()}}
        y = _forward(args)
    with _jax.named_scope("loss_head"):
        err = _jnp.square(y.astype(_jnp.float32) - loss_target)
        return 0.5 * _jnp.sum(_jnp.mean(err, axis=-1)) if err.ndim else 0.5 * err


def _adamw(w, g, m, v):
    m = ADAM_B1 * m + (1.0 - ADAM_B1) * g
    v = ADAM_B2 * v + (1.0 - ADAM_B2) * _jnp.square(g)
    m_hat = m / (1.0 - ADAM_B1 ** ADAM_STEP)
    v_hat = v / (1.0 - ADAM_B2 ** ADAM_STEP)
    delta = -ADAM_LR * (m_hat / (_jnp.sqrt(v_hat) + ADAM_EPS) + ADAM_WD * w)
    return delta, m, v


def reference(x, mem, ln_in_g, ln_in_b, mem_ln_g, mem_ln_b, w_in, b_forget, mla_q_norm_g, w_mla_q_up, mla_kv_norm_g, w_mla_kv_up, w_mem_kv, w_out, ln_g, ln_b, loss_target, m_ln_in_g, m_ln_in_b, m_mem_ln_g, m_mem_ln_b, m_w_in, m_b_forget, m_mla_q_norm_g, m_w_mla_q_up, m_mla_kv_norm_g, m_w_mla_kv_up, m_w_mem_kv, m_w_out, m_ln_g, m_ln_b, v_ln_in_g, v_ln_in_b, v_mem_ln_g, v_mem_ln_b, v_w_in, v_b_forget, v_mla_q_norm_g, v_w_mla_q_up, v_mla_kv_norm_g, v_w_mla_kv_up, v_w_mem_kv, v_w_out, v_ln_g, v_ln_b):
    given = dict(x=x, mem=mem, ln_in_g=ln_in_g, ln_in_b=ln_in_b, mem_ln_g=mem_ln_g, mem_ln_b=mem_ln_b, w_in=w_in, b_forget=b_forget, mla_q_norm_g=mla_q_norm_g, w_mla_q_up=w_mla_q_up, mla_kv_norm_g=mla_kv_norm_g, w_mla_kv_up=w_mla_kv_up, w_mem_kv=w_mem_kv, w_out=w_out, ln_g=ln_g, ln_b=ln_b, loss_target=loss_target, m_ln_in_g=m_ln_in_g, m_ln_in_b=m_ln_in_b, m_mem_ln_g=m_mem_ln_g, m_mem_ln_b=m_mem_ln_b, m_w_in=m_w_in, m_b_forget=m_b_forget, m_mla_q_norm_g=m_mla_q_norm_g, m_w_mla_q_up=m_w_mla_q_up, m_mla_kv_norm_g=m_mla_kv_norm_g, m_w_mla_kv_up=m_w_mla_kv_up, m_w_mem_kv=m_w_mem_kv, m_w_out=m_w_out, m_ln_g=m_ln_g, m_ln_b=m_ln_b, v_ln_in_g=v_ln_in_g, v_ln_in_b=v_ln_in_b, v_mem_ln_g=v_mem_ln_g, v_mem_ln_b=v_mem_ln_b, v_w_in=v_w_in, v_b_forget=v_b_forget, v_mla_q_norm_g=v_mla_q_norm_g, v_w_mla_q_up=v_w_mla_q_up, v_mla_kv_norm_g=v_mla_kv_norm_g, v_w_mla_kv_up=v_w_mla_kv_up, v_w_mem_kv=v_w_mem_kv, v_w_out=v_w_out, v_ln_g=v_ln_g, v_ln_b=v_ln_b)
    weights = {n: given[n] for n in TWIN_WEIGHTS}
    shared = {n: given[n] for n in SHARED_INPUTS}
    per_example = {n: given[n] for n in ['x', 'mem']}
    grad_fn = _jax.value_and_grad(_loss, argnums=(0, 1))

    def one_microbatch(ex, loss_target):
        ex = dict(ex)
        diff = ex.pop(TWIN_DIFF_INPUT)
        return grad_fn(weights, diff, {**shared, **ex}, loss_target)

    if N_MICROBATCH == 1:
        loss, (grad_w, grad_x) = one_microbatch(per_example, given["loss_target"])
    else:
        def body(carry, xs):
            loss_sum, grad_sum = carry
            l_k, (gw_k, gx_k) = one_microbatch(xs[0], xs[1])
            with _jax.named_scope("update"):
                return (loss_sum + l_k, _jax.tree.map(_jnp.add, grad_sum, gw_k)), gx_k

        init = (_jnp.zeros((), _jnp.float32), _jax.tree.map(_jnp.zeros_like, weights))
        (loss, grad_w), grad_x = _jax.lax.scan(body, init, (per_example, given["loss_target"]))
    with _jax.named_scope("update"):
        delta_w, new_m, new_v = {}, {}, {}
        for n in TWIN_WEIGHTS:
            delta_w[n], new_m[n], new_v[n] = _adamw(weights[n], grad_w[n], given["m_" + n], given["v_" + n])
    return (loss, grad_x, *[grad_w[n] for n in TWIN_WEIGHTS], *[delta_w[n] for n in TWIN_WEIGHTS],
            *[new_m[n] for n in TWIN_WEIGHTS], *[new_v[n] for n in TWIN_WEIGHTS])
```

```python
import numpy as np
import jax
import jax.numpy as jnp
from jax import lax
from jax.experimental import pallas as pl
from jax.experimental.pallas import tpu as pltpu

F32 = jnp.float32
MXU_DTYPE = jnp.bfloat16

DEPTH = 2
D_MODEL = 1024
GROUP_W = 256
N_HEADS = 4
HEAD_DIM = 64
MLA_Q_RANK = 256
MLA_KV_RANK = 128
MLA_ROPE = 32
MLA_QK = HEAD_DIM + MLA_ROPE
ROPE_THETA = 10000.0
LN_EPS = 1e-5
RMS_EPS = 1e-6
ALPHA = (2 * DEPTH) ** 0.25
ADAM_LR, ADAM_B1, ADAM_B2, ADAM_EPS, ADAM_WD, ADAM_STEP = 0.001, 0.9, 0.999, 1e-08, 0.01, 10

_SPLIT = (256, 256, 256, 4, 256, 256, 256, 256, 128, 32, 256, 1024)
_OFF = [int(o) for o in np.cumsum((0,) + _SPLIT)]
IN_COLS = _OFF[-1]
PA_COLS = 7 * GROUP_W
PB_COLS = 1024 + 256 + 128 + 128
PB_CQ_BLK, PB_CKV_BLK, PB_MISC_BLK = 4, 10, 11
MISC_KROT = 4

LANES = 1024
PACK_ALIGN = 16
BIG_ROWS = 2560
SMALL_ROWS = 144
ROW_TILE = 256
ATT_TILE = 256
VMEM_LIMIT = 56 * 1024 * 1024
NEG = -1e30
MESH_AXES = ("x", "y", "c")


def _dot(a, b):
    return jnp.dot(a, b, preferred_element_type=F32)


def _dot_nt(a, b):
    return lax.dot_general(a, b, (((1,), (1,)), ((), ())), preferred_element_type=F32)


def _dot_tn(a, b):
    return lax.dot_general(a, b, (((0,), (0,)), ((), ())), preferred_element_type=F32)


def _split2(x):
    hi = x.astype(MXU_DTYPE)
    lo = (x - hi.astype(F32)).astype(MXU_DTYPE)
    return hi, lo


def _split3(x):
    hi = x.astype(MXU_DTYPE)
    r = x - hi.astype(F32)
    mid = r.astype(MXU_DTYPE)
    lo = (r - mid.astype(F32)).astype(MXU_DTYPE)
    return hi, mid, lo


def _dot_exact_r(x, pm):
    hi, mid, lo = _split3(x)
    return _dot(hi, pm) + _dot(mid, pm) + _dot(lo, pm)


def _dot_exact_l(pm, x):
    hi, mid, lo = _split3(x)
    return _dot(pm, hi) + _dot(pm, mid) + _dot(pm, lo)


def _pick(dim, prefs):
    for p in prefs:
        if dim % p == 0:
            return p
    return dim


def _softplus(z):
    return jnp.maximum(z, 0.0) + jnp.log(1.0 + jnp.exp(-jnp.abs(z)))


def _to_heads(a, nh=N_HEADS):
    s = a.shape[0]
    return a.reshape(s, nh, -1).transpose(1, 0, 2)


def _from_heads(a):
    nh, s, d = a.shape
    return a.transpose(1, 0, 2).reshape(s, nh * d)


def _matmul(a, b, out_dtype, name):
    m, k = a.shape
    _, n = b.shape
    tm = _pick(m, (512, 256, 128))
    tn = _pick(n, (512, 256, 128))
    tk = k if k <= 4096 else _pick(k, (1024, 512, 256, 128))
    nk = k // tk

    def kern(a_ref, b_ref, o_ref, acc_ref):
        kk = pl.program_id(2)

        @pl.when(kk == 0)
        def _():
            acc_ref[...] = jnp.zeros_like(acc_ref)

        acc_ref[...] += _dot(a_ref[...], b_ref[...])

        @pl.when(kk == nk - 1)
        def _():
            o_ref[...] = acc_ref[...].astype(o_ref.dtype)

    return pl.pallas_call(
        kern, name=name, grid=(m // tm, n // tn, nk),
        in_specs=[pl.BlockSpec((tm, tk), lambda i, j, kk: (i, kk)),
                  pl.BlockSpec((tk, tn), lambda i, j, kk: (kk, j))],
        out_specs=pl.BlockSpec((tm, tn), lambda i, j, kk: (i, j)),
        out_shape=jax.ShapeDtypeStruct((m, n), out_dtype),
        scratch_shapes=[pltpu.VMEM((tm, tn), F32)],
        compiler_params=pltpu.CompilerParams(
            dimension_semantics=("parallel", "parallel", "arbitrary"), vmem_limit_bytes=VMEM_LIMIT),
    )(a.astype(MXU_DTYPE), b.astype(MXU_DTYPE))


def _rowwise(body, name, rows, tile, row_ins, full_ins, row_outs, acc_outs=(), scratch=(),
             reverse=False, sequential=False):
    n = rows // tile

    def ridx(i):
        return (n - 1 - i) if reverse else i

    in_specs, args = [], []
    for arr, width, cb in row_ins:
        in_specs.append(pl.BlockSpec((tile, width), lambda i, cb=cb: (ridx(i), cb)))
        args.append(arr)
    for arr in full_ins:
        in_specs.append(pl.BlockSpec(arr.shape, lambda i, nd=arr.ndim: (0,) * nd))
        args.append(arr)
    out_shape = [jax.ShapeDtypeStruct((rows, w), dt) for w, dt in row_outs]
    out_shape += [jax.ShapeDtypeStruct(s, dt) for s, dt in acc_outs]
    out_specs = [pl.BlockSpec((tile, w), lambda i: (ridx(i), 0)) for w, dt in row_outs]
    out_specs += [pl.BlockSpec(s, lambda i, nd=len(s): (0,) * nd) for s, dt in acc_outs]

    def kern(*refs):
        body(pl.program_id(0), *refs)

    sem = "arbitrary" if (acc_outs or sequential) else "parallel"
    return pl.pallas_call(
        kern, name=name, grid=(n,), in_specs=in_specs, out_specs=out_specs, out_shape=out_shape,
        scratch_shapes=list(scratch),
        compiler_params=pltpu.CompilerParams(dimension_semantics=(sem,), vmem_limit_bytes=VMEM_LIMIT),
    )(*args)


def _ln_stats(u):
    mu = jnp.mean(u, axis=-1, keepdims=True)
    xc = u - mu
    var = jnp.mean(xc * xc, axis=-1, keepdims=True)
    return xc, lax.rsqrt(var + LN_EPS)


def _ln_fwd(a, b, g, beta, name):
    rows, d = a.shape
    has_b = b is not None

    def body(i, *refs):
        if has_b:
            a_ref, b_ref, g_ref, be_ref, h_ref, hb_ref = refs
            u = ALPHA * a_ref[...] + b_ref[...]
        else:
            a_ref, g_ref, be_ref, h_ref, hb_ref = refs
            u = a_ref[...]
        xc, rstd = _ln_stats(u)
        y = xc * rstd * g_ref[...] + be_ref[...]
        h_ref[...] = y
        hb_ref[...] = y.astype(hb_ref.dtype)

    row_ins = [(a, d, 0)] + ([(b, d, 0)] if has_b else [])
    return _rowwise(body, name, rows, min(ROW_TILE, rows), row_ins,
                    [g.reshape(1, d), beta.reshape(1, d)], [(d, F32), (d, MXU_DTYPE)])


def _ln_bwd(a, b, g, dy1, dy2, c1, name):
    rows, d = a.shape
    has_b = b is not None
    has_2 = dy2 is not None

    def body(i, *refs):
        refs = list(refs)
        a_ref = refs.pop(0)
        u = a_ref[...]
        if has_b:
            u = ALPHA * u + refs.pop(0)[...]
        dy = c1 * refs.pop(0)[...]
        if has_2:
            dy = dy + refs.pop(0)[...]
        g_ref, du_ref, dub_ref, dg_ref, db_ref = refs

        @pl.when(i == 0)
        def _():
            dg_ref[...] = jnp.zeros_like(dg_ref)
            db_ref[...] = jnp.zeros_like(db_ref)

        xc, rstd = _ln_stats(u)
        xhat = xc * rstd
        dxh = dy * g_ref[...]
        m1 = jnp.mean(dxh, axis=-1, keepdims=True)
        m2 = jnp.mean(dxh * xhat, axis=-1, keepdims=True)
        du = rstd * (dxh - m1 - xhat * m2)
        du_ref[...] = du
        dub_ref[...] = du.astype(dub_ref.dtype)
        dg_ref[...] += jnp.sum(dy * xhat, axis=0, keepdims=True)
        db_ref[...] += jnp.sum(dy, axis=0, keepdims=True)

    row_ins = [(a, d, 0)] + ([(b, d, 0)] if has_b else []) + [(dy1, d, 0)] + ([(dy2, d, 0)] if has_2 else [])
    return _rowwise(body, name, rows, min(ROW_TILE, rows), row_ins, [g.reshape(1, d)],
                    [(d, F32), (d, MXU_DTYPE)], [((1, d), F32), ((1, d), F32)])


def _loss_grad(h, target):
    rows, d = h.shape

    def body(i, h_ref, t_ref, dh_ref, acc_ref):
        @pl.when(i == 0)
        def _():
            acc_ref[...] = jnp.zeros_like(acc_ref)

        e = h_ref[...] - t_ref[...]
        dh_ref[...] = e * (1.0 / d)
        acc_ref[...] += jnp.sum(e * e, axis=0, keepdims=True)

    return _rowwise(body, "loss_grad", rows, ROW_TILE, [(h, d, 0), (target, d, 0)], [],
                    [(d, F32)], [((1, d), F32)])


def _gate_fwd(mixed, pb):
    rows, w = mixed.shape

    def body(i, m_ref, g_ref, o_ref):
        g = g_ref[...]
        o_ref[...] = (m_ref[...] * (g / (1.0 + jnp.exp(-g)))).astype(o_ref.dtype)

    return _rowwise(body, "gate_fwd", rows, ROW_TILE, [(mixed, w, 0), (pb, w, 0)], [], [(w, MXU_DTYPE)])[0]


def _gate_bwd(dgated, mixed, pb):
    rows, w = mixed.shape

    def body(i, dg_ref, m_ref, g_ref, dm_ref, dgate_ref):
        g = g_ref[...]
        dg = dg_ref[...]
        sig = 1.0 / (1.0 + jnp.exp(-g))
        dm_ref[...] = dg * (g * sig)
        dgate_ref[...] = (dg * m_ref[...] * (sig * (1.0 + g * (1.0 - sig)))).astype(dgate_ref.dtype)

    return _rowwise(body, "gate_bwd", rows, ROW_TILE, [(dgated, w, 0), (mixed, w, 0), (pb, w, 0)], [],
                    [(w, F32), (w, MXU_DTYPE)])


def _tri(n, kind):
    r = np.arange(n)[:, None]
    c = np.arange(n)[None, :]
    m = {"lower_incl": r >= c, "upper_incl": r <= c, "row_gt_col": r > c, "row_lt_col": r < c}[kind]
    return jnp.asarray(m.astype(np.float32), dtype=MXU_DTYPE)


def _forget_fwd(pb, bias_row):
    rows = pb.shape[0]
    tile = min(ROW_TILE, rows)

    def body(i, x_ref, b_ref, l_ref, o_ref, carry_ref):
        @pl.when(i == 0)
        def _():
            carry_ref[...] = jnp.zeros_like(carry_ref)

        xx = x_ref[...] + b_ref[...]
        lane = lax.broadcasted_iota(jnp.int32, xx.shape, 1)
        lf = jnp.where(lane < N_HEADS, -_softplus(-xx), 0.0)
        o_ref[...] = _dot_exact_l(l_ref[...], lf) + carry_ref[...]
        carry_ref[...] += jnp.sum(lf, axis=0, keepdims=True)

    return _rowwise(body, "forget_fwd", rows, tile, [(pb, 128, PB_MISC_BLK)],
                    [bias_row, _tri(tile, "lower_incl")], [(128, F32)],
                    scratch=[pltpu.VMEM((1, 128), F32)], sequential=True)[0]


def _forget_bwd(pb, bias_row, dfq, dfk):
    rows = pb.shape[0]
    tile = min(ROW_TILE, rows)

    def body(i, x_ref, dq_ref, dk_ref, b_ref, u_ref, o_ref, db_ref, carry_ref):
        @pl.when(i == 0)
        def _():
            carry_ref[...] = jnp.zeros_like(carry_ref)
            db_ref[...] = jnp.zeros_like(db_ref)

        df = dq_ref[...] + dk_ref[...]
        sfx = _dot_exact_l(u_ref[...], df) + carry_ref[...]
        carry_ref[...] += jnp.sum(df, axis=0, keepdims=True)
        xx = x_ref[...] + b_ref[...]
        lane = lax.broadcasted_iota(jnp.int32, xx.shape, 1)
        dl = jnp.where(lane < N_HEADS, sfx / (1.0 + jnp.exp(xx)), 0.0)
        o_ref[...] = dl
        db_ref[...] += jnp.sum(dl, axis=0, keepdims=True)

    return _rowwise(body, "forget_bwd", rows, tile,
                    [(pb, 128, PB_MISC_BLK), (dfq, 128, 0), (dfk, 128, 0)],
                    [bias_row, _tri(tile, "upper_incl")], [(128, F32)], [((1, 128), F32)],
                    scratch=[pltpu.VMEM((1, 128), F32)], reverse=True, sequential=True)


def _rope_tables(s):
    half = MLA_ROPE // 2
    inv_freq = ROPE_THETA ** (-jnp.arange(half, dtype=F32) / half)
    ang = jnp.arange(s).astype(F32)[:, None] * inv_freq[None, :]
    cos2 = jnp.tile(jnp.cos(ang), (1, 2))
    sin2 = jnp.tile(jnp.sin(ang), (1, 2))
    ones = jnp.ones((s, HEAD_DIM), F32)
    zeros = jnp.zeros((s, HEAD_DIM), F32)
    cq = jnp.tile(jnp.concatenate([ones, cos2], axis=1), (1, N_HEADS))
    sq = jnp.tile(jnp.concatenate([zeros, sin2], axis=1), (1, N_HEADS))
    pad = ((0, 0), (MISC_KROT, 128 - MISC_KROT - MLA_ROPE))
    ck = jnp.pad(cos2, pad)
    sk = jnp.pad(sin2, pad)
    cx = jnp.tile(cos2, (1, N_HEADS))
    sx = jnp.tile(sin2, (1, N_HEADS))
    return dict(cq=cq, sq=sq, ck=ck, sk=sk, cx=cx, sx=sx)


def _rot_matrix(width, bases):
    half = MLA_ROPE // 2
    p = np.zeros((width, width), np.float32)
    for b in bases:
        for i in range(half):
            p[b + half + i, b + i] = -1.0
            p[b + i, b + half + i] = 1.0
    return p


def _rope_matrices():
    pq = _rot_matrix(N_HEADS * MLA_QK, [h * MLA_QK + HEAD_DIM for h in range(N_HEADS)])
    pk = _rot_matrix(128, [MISC_KROT])
    p4 = _rot_matrix(128, [h * MLA_ROPE for h in range(N_HEADS)])
    a = np.zeros((128, 128), np.float32)
    for h in range(N_HEADS):
        for r in range(MLA_ROPE):
            a[h * MLA_ROPE + r, MISC_KROT + r] = 1.0
    cast = lambda m: jnp.asarray(m, dtype=MXU_DTYPE)
    return dict(pq=cast(pq), pqt=cast(pq.T), pk=cast(pk), xa=cast(a), xb=cast(p4.T @ a))


def _rms(c, g):
    r = lax.rsqrt(jnp.mean(c * c, axis=-1, keepdims=True) + RMS_EPS)
    return c * r * g


def _mla_q_fwd(pb, g, w_up, tabs, mats):
    rows = pb.shape[0]
    nq = N_HEADS * MLA_QK

    def body(i, c_ref, cos_ref, sin_ref, g_ref, w_ref, p_ref, q_ref, cn_ref):
        cn = _rms(c_ref[...], g_ref[...]).astype(cn_ref.dtype)
        cn_ref[...] = cn
        q = _dot(cn, w_ref[...])
        q_ref[...] = (q * cos_ref[...] + _dot_exact_r(q, p_ref[...]) * sin_ref[...]).astype(q_ref.dtype)

    return _rowwise(body, "mla_q_fwd", rows, ROW_TILE,
                    [(pb, MLA_Q_RANK, PB_CQ_BLK), (tabs["cq"], nq, 0), (tabs["sq"], nq, 0)],
                    [g.reshape(1, MLA_Q_RANK), w_up.astype(MXU_DTYPE), mats["pq"]],
                    [(nq, MXU_DTYPE), (MLA_Q_RANK, MXU_DTYPE)])


def _mla_kv_fwd(pb, g, w_up, tabs, mats):
    rows = pb.shape[0]
    nkv = N_HEADS * 2 * HEAD_DIM

    def body(i, c_ref, x_ref, cos_ref, sin_ref, g_ref, w_ref, p_ref, kv_ref, cn_ref, kr_ref):
        cn = _rms(c_ref[...], g_ref[...]).astype(cn_ref.dtype)
        cn_ref[...] = cn
        kv_ref[...] = _dot(cn, w_ref[...]).astype(kv_ref.dtype)
        xx = x_ref[...]
        kr_ref[...] = (xx * cos_ref[...] + _dot_exact_r(xx, p_ref[...]) * sin_ref[...]).astype(kr_ref.dtype)

    return _rowwise(body, "mla_kv_fwd", rows, ROW_TILE,
                    [(pb, MLA_KV_RANK, PB_CKV_BLK), (pb, 128, PB_MISC_BLK), (tabs["ck"], 128, 0), (tabs["sk"], 128, 0)],
                    [g.reshape(1, MLA_KV_RANK), w_up.astype(MXU_DTYPE), mats["pk"]],
                    [(nkv, MXU_DTYPE), (MLA_KV_RANK, MXU_DTYPE), (128, MXU_DTYPE)])


def _rope_q_bwd(dq_full, tabs, mats):
    rows, nq = dq_full.shape

    def body(i, d_ref, cos_ref, sin_ref, pt_ref, o_ref):
        d = d_ref[...]
        o_ref[...] = (d * cos_ref[...] + _dot_exact_r(d * sin_ref[...], pt_ref[...])).astype(o_ref.dtype)

    return _rowwise(body, "rope_q_bwd", rows, ROW_TILE,
                    [(dq_full, nq, 0), (tabs["cq"], nq, 0), (tabs["sq"], nq, 0)], [mats["pqt"]],
                    [(nq, MXU_DTYPE)])[0]


def _rope_k_bwd(dkr, tabs, mats):
    rows = dkr.shape[0]

    def body(i, d_ref, cos_ref, sin_ref, a_ref, b_ref, o_ref):
        d = d_ref[...]
        o_ref[...] = _dot_exact_r(d * cos_ref[...], a_ref[...]) + _dot_exact_r(d * sin_ref[...], b_ref[...])

    return _rowwise(body, "rope_k_bwd", rows, ROW_TILE,
                    [(dkr, 128, 0), (tabs["cx"], 128, 0), (tabs["sx"], 128, 0)], [mats["xa"], mats["xb"]],
                    [(128, F32)])[0]


def _rms_bwd(pb, width, col_blk, g, dy, name):
    rows = pb.shape[0]

    def body(i, c_ref, dy_ref, g_ref, dc_ref, dg_ref):
        @pl.when(i == 0)
        def _():
            dg_ref[...] = jnp.zeros_like(dg_ref)

        c = c_ref[...]
        dy = dy_ref[...]
        r = lax.rsqrt(jnp.mean(c * c, axis=-1, keepdims=True) + RMS_EPS)
        dyg = dy * g_ref[...]
        dc = r * dyg - c * (r * r * r) * jnp.mean(c * dyg, axis=-1, keepdims=True)
        dc_ref[...] = dc.astype(dc_ref.dtype)
        dg_ref[...] += jnp.sum(dy * c * r, axis=0, keepdims=True)

    return _rowwise(body, name, rows, ROW_TILE, [(pb, width, col_blk), (dy, width, 0)], [g.reshape(1, width)],
                    [(width, MXU_DTYPE)], [((1, width), F32)])


def _att_params():
    return pltpu.CompilerParams(dimension_semantics=("arbitrary", "arbitrary"), vmem_limit_bytes=VMEM_LIMIT)


def _blk_off(j, t):
    return j * t if isinstance(j, int) else pl.multiple_of(j * t, t)


def _causal_mask(t, strict):
    r = lax.broadcasted_iota(jnp.int32, (t, t), 0)
    c = lax.broadcasted_iota(jnp.int32, (t, t), 1)
    return (c < r) if strict else (c <= r)


def _softmax_fwd(q, k, v, scale, causal, fq, fk, name):
    nh, sq, dk = q.shape
    sk, dv = k.shape[1], v.shape[2]
    tq = min(ATT_TILE, sq)
    tk = tq if causal else min(ATT_TILE, sk)
    nkv = sk // tk
    bias = fq is not None

    def kern(*refs):
        if bias:
            q_ref, k_ref, v_ref, fq_ref, fk_ref, o_ref, lse_ref = refs
        else:
            q_ref, k_ref, v_ref, o_ref, lse_ref = refs
        i = pl.program_id(1)
        qb = q_ref[...]

        def block(j, carry, masked):
            m, l, acc = carry
            off = _blk_off(j, tk)
            kb = k_ref[pl.ds(off, tk), :]
            vb = v_ref[pl.ds(off, tk), :]
            s = _dot_nt(qb, kb) * scale
            if bias:
                s = s + fq_ref[...] - fk_ref[:, pl.ds(off, tk)]
            if masked:
                s = jnp.where(_causal_mask(tq, False), s, NEG)
            m_new = jnp.maximum(m, jnp.max(s, axis=1, keepdims=True))
            p = jnp.exp(s - m_new)
            a = jnp.exp(m - m_new)
            l = a * l + jnp.sum(p, axis=1, keepdims=True)
            acc = a * acc + _dot(p.astype(MXU_DTYPE), vb)
            return m_new, l, acc

        carry = (jnp.full((tq, 1), NEG, F32), jnp.zeros((tq, 1), F32), jnp.zeros((tq, dv), F32))
        if causal:
            carry = lax.fori_loop(0, i, lambda j, c: block(j, c, False), carry)
            carry = block(i, carry, True)
        else:
            for j in range(nkv):
                carry = block(j, carry, False)
        m, l, acc = carry
        o_ref[...] = acc / l
        lse_ref[...] = m + jnp.log(l)

    in_specs = [pl.BlockSpec((None, tq, dk), lambda h, i: (h, i, 0)),
                pl.BlockSpec((None, sk, dk), lambda h, i: (h, 0, 0)),
                pl.BlockSpec((None, sk, dv), lambda h, i: (h, 0, 0))]
    args = [q, k, v]
    if bias:
        in_specs += [pl.BlockSpec((None, tq, 1), lambda h, i: (h, i, 0)),
                     pl.BlockSpec((None, 1, sk), lambda h, i: (h, 0, 0))]
        args += [fq, fk]
    return pl.pallas_call(
        kern, name=name, grid=(nh, sq // tq), in_specs=in_specs,
        out_specs=[pl.BlockSpec((None, tq, dv), lambda h, i: (h, i, 0)),
                   pl.BlockSpec((None, tq, 1), lambda h, i: (h, i, 0))],
        out_shape=[jax.ShapeDtypeStruct((nh, sq, dv), F32), jax.ShapeDtypeStruct((nh, sq, 1), F32)],
        compiler_params=_att_params(),
    )(*args)


def _softmax_bwd(q, k, v, o, lse, do, scale, causal, fq, fk, name):
    nh, sq, dk = q.shape
    sk, dv = k.shape[1], v.shape[2]
    tq = min(ATT_TILE, sq)
    tk = tq if causal else min(ATT_TILE, sk)
    nkv = sk // tk
    bias = fq is not None

    def kern(*refs):
        if bias:
            (q_ref, k_ref, v_ref, o_ref, lse_ref, do_ref, fq_ref, fk_ref,
             dq_ref, dk_ref, dv_ref, dfq_ref, dfk_ref) = refs
        else:
            q_ref, k_ref, v_ref, o_ref, lse_ref, do_ref, dq_ref, dk_ref, dv_ref = refs
        i = pl.program_id(1)

        @pl.when(i == 0)
        def _():
            dk_ref[...] = jnp.zeros_like(dk_ref)
            dv_ref[...] = jnp.zeros_like(dv_ref)
            if bias:
                dfk_ref[...] = jnp.zeros_like(dfk_ref)

        qb = q_ref[...]
        do32 = do_ref[...]
        dob = do32.astype(MXU_DTYPE)
        lse_b = lse_ref[...]
        delta = jnp.sum(do32 * o_ref[...], axis=1, keepdims=True)

        def block(j, carry, masked):
            dq, dfq = carry
            off = _blk_off(j, tk)
            kb = k_ref[pl.ds(off, tk), :]
            vb = v_ref[pl.ds(off, tk), :]
            s = _dot_nt(qb, kb) * scale
            if bias:
                s = s + fq_ref[...] - fk_ref[:, pl.ds(off, tk)]
            if masked:
                s = jnp.where(_causal_mask(tq, False), s, NEG)
            p = jnp.exp(s - lse_b)
            dp = _dot_nt(dob, vb)
            ds = p * (dp - delta)
            dsb = (ds * scale).astype(MXU_DTYPE)
            dv_ref[pl.ds(off, tk), :] += _dot_tn(p.astype(MXU_DTYPE), dob)
            dk_ref[pl.ds(off, tk), :] += _dot_tn(dsb, qb)
            dq = dq + _dot(dsb, kb)
            if bias:
                dfq = dfq + jnp.sum(ds, axis=1, keepdims=True)
                dfk_ref[:, pl.ds(off, tk)] -= jnp.sum(ds, axis=0, keepdims=True)
            return dq, dfq

        carry = (jnp.zeros((tq, dk), F32), jnp.zeros((tq, 1), F32))
        if causal:
            carry = lax.fori_loop(0, i, lambda j, c: block(j, c, False), carry)
            carry = block(i, carry, True)
        else:
            for j in range(nkv):
                carry = block(j, carry, False)
        dq_ref[...] = carry[0]
        if bias:
            dfq_ref[...] = carry[1]

    qspec = lambda w: pl.BlockSpec((None, tq, w), lambda h, i: (h, i, 0))
    kspec = lambda w: pl.BlockSpec((None, sk, w), lambda h, i: (h, 0, 0))
    in_specs = [qspec(dk), kspec(dk), kspec(dv), qspec(dv), qspec(1), qspec(dv)]
    args = [q, k, v, o, lse, do]
    out_specs = [qspec(dk), kspec(dk), kspec(dv)]
    out_shape = [jax.ShapeDtypeStruct((nh, sq, dk), F32), jax.ShapeDtypeStruct((nh, sk, dk), F32),
                 jax.ShapeDtypeStruct((nh, sk, dv), F32)]
    if bias:
        fkspec = pl.BlockSpec((None, 1, sk), lambda h, i: (h, 0, 0))
        in_specs += [qspec(1), fkspec]
        args += [fq, fk]
        out_specs += [qspec(1), fkspec]
        out_shape += [jax.ShapeDtypeStruct((nh, sq, 1), F32), jax.ShapeDtypeStruct((nh, 1, sk), F32)]
    return pl.pallas_call(
        kern, name=name, grid=(nh, sq // tq), in_specs=in_specs, out_specs=out_specs, out_shape=out_shape,
        compiler_params=_att_params(),
    )(*args)


def _sb_logs(qb, kb, scale, masked, t):
    z = _dot_nt(qb, kb) * scale
    sp = _softplus(z)
    lk = -sp
    valid = _causal_mask(t, True) if masked else None
    if masked:
        lk = jnp.where(valid, lk, 0.0)
    return lk, z - sp, valid


def _sb_fwd(q, k, v, scale, name):
    nh, s, dk = q.shape
    dv = v.shape[2]
    t = min(ATT_TILE, s)

    def kern(q_ref, k_ref, v_ref, tri_ref, o_ref, rm_ref):
        i = pl.program_id(1)
        qb = q_ref[...]
        tri = tri_ref[...]
        lane = lax.broadcasted_iota(jnp.int32, (t, 128), 1)

        def block(j, carry, masked):
            acc, right, rm = carry
            off = _blk_off(j, t)
            kb = k_ref[pl.ds(off, t), :]
            vb = v_ref[pl.ds(off, t), :]
            lk, ls, valid = _sb_logs(qb, kb, scale, masked, t)
            hi, lo = _split2(lk)
            tail = _dot(hi, tri) + _dot(lo, tri)
            w = jnp.exp(ls + tail + right)
            if masked:
                w = jnp.where(valid, w, 0.0)
            acc = acc + _dot(w.astype(MXU_DTYPE), vb)
            rm = rm + jnp.where(lane == j, right, 0.0)
            right = right + jnp.sum(lk, axis=1, keepdims=True)
            return acc, right, rm

        carry = (jnp.zeros((t, dv), F32), jnp.zeros((t, 1), F32), jnp.zeros((t, 128), F32))
        carry = block(i, carry, True)
        carry = lax.fori_loop(0, i, lambda n, c: block(i - 1 - n, c, False), carry)
        o_ref[...] = carry[0]
        rm_ref[...] = carry[2]

    qspec = lambda w: pl.BlockSpec((None, t, w), lambda h, i: (h, i, 0))
    kspec = lambda w: pl.BlockSpec((None, s, w), lambda h, i: (h, 0, 0))
    return pl.pallas_call(
        kern, name=name, grid=(nh, s // t),
        in_specs=[qspec(dk), kspec(dk), kspec(dv), pl.BlockSpec((t, t), lambda h, i: (0, 0))],
        out_specs=[qspec(dv), qspec(128)],
        out_shape=[jax.ShapeDtypeStruct((nh, s, dv), F32), jax.ShapeDtypeStruct((nh, s, 128), F32)],
        compiler_params=_att_params(),
    )(q, k, v, _tri(t, "row_gt_col"))


def _sb_bwd(q, k, v, do, rm, scale, name):
    nh, s, dk = q.shape
    dv = v.shape[2]
    t = min(ATT_TILE, s)

    def kern(q_ref, k_ref, v_ref, do_ref, rm_ref, tri_ref, pre_ref, dq_ref, dk_ref, dv_ref):
        i = pl.program_id(1)

        @pl.when(i == 0)
        def _():
            dk_ref[...] = jnp.zeros_like(dk_ref)
            dv_ref[...] = jnp.zeros_like(dv_ref)

        qb = q_ref[...]
        dob = do_ref[...].astype(MXU_DTYPE)
        rmb = rm_ref[...]
        tri = tri_ref[...]
        pre = pre_ref[...]
        lane = lax.broadcasted_iota(jnp.int32, (t, 128), 1)

        def block(j, carry, masked):
            dq, left = carry
            off = _blk_off(j, t)
            kb = k_ref[pl.ds(off, t), :]
            vb = v_ref[pl.ds(off, t), :]
            lk, ls, valid = _sb_logs(qb, kb, scale, masked, t)
            sig = jnp.exp(ls)
            hi, lo = _split2(lk)
            right = jnp.sum(jnp.where(lane == j, rmb, 0.0), axis=1, keepdims=True)
            w = jnp.exp(ls + _dot(hi, tri) + _dot(lo, tri) + right)
            if masked:
                w = jnp.where(valid, w, 0.0)
            g = _dot_nt(dob, vb) * w
            dv_ref[pl.ds(off, t), :] += _dot_tn(w.astype(MXU_DTYPE), dob)
            ghi, glo = _split2(g)
            c = _dot(ghi, pre) + _dot(glo, pre) + left
            dz = g * (1.0 - sig) - sig * c
            if masked:
                dz = jnp.where(valid, dz, 0.0)
            dzb = (dz * scale).astype(MXU_DTYPE)
            dk_ref[pl.ds(off, t), :] += _dot_tn(dzb, qb)
            return dq + _dot(dzb, kb), left + jnp.sum(g, axis=1, keepdims=True)

        carry = (jnp.zeros((t, dk), F32), jnp.zeros((t, 1), F32))
        carry = lax.fori_loop(0, i, lambda j, c: block(j, c, False), carry)
        carry = block(i, carry, True)
        dq_ref[...] = carry[0]

    qspec = lambda w: pl.BlockSpec((None, t, w), lambda h, i: (h, i, 0))
    kspec = lambda w: pl.BlockSpec((None, s, w), lambda h, i: (h, 0, 0))
    mspec = pl.BlockSpec((t, t), lambda h, i: (0, 0))
    return pl.pallas_call(
        kern, name=name, grid=(nh, s // t),
        in_specs=[qspec(dk), kspec(dk), kspec(dv), qspec(dv), qspec(128), mspec, mspec],
        out_specs=[qspec(dk), kspec(dk), kspec(dv)],
        out_shape=[jax.ShapeDtypeStruct((nh, s, dk), F32), jax.ShapeDtypeStruct((nh, s, dk), F32),
                   jax.ShapeDtypeStruct((nh, s, dv), F32)],
        compiler_params=_att_params(),
    )(q, k, v, do, rm, _tri(t, "row_gt_col"), _tri(t, "row_lt_col"))


def _split_w_in(w):
    col = lambda n: w[:, _OFF[n]:_OFF[n + 1]]
    wa = jnp.concatenate([col(0), col(1), col(2), col(4), col(5), col(6), col(10)], axis=1)
    misc = jnp.concatenate([col(3), col(9), jnp.zeros((w.shape[0], 128 - 4 - MLA_ROPE), w.dtype)], axis=1)
    wb = jnp.concatenate([col(11), col(7), col(8), misc], axis=1)
    return wa, wb


def _merge_dw_in(dwp):
    a = lambda n: dwp[:, n * GROUP_W:(n + 1) * GROUP_W]
    b0 = PA_COLS
    gate = dwp[:, b0:b0 + 1024]
    cq = dwp[:, b0 + 1024:b0 + 1280]
    ckv = dwp[:, b0 + 1280:b0 + 1408]
    flog = dwp[:, b0 + 1408:b0 + 1412]
    krot = dwp[:, b0 + 1408 + MISC_KROT:b0 + 1408 + MISC_KROT + MLA_ROPE]
    return jnp.concatenate([a(0), a(1), a(2), flog, a(3), a(4), a(5), cq, ckv, krot, a(6), gate], axis=1)


def _pad_heads_cols(a):
    return jnp.pad(a, ((0, 0), (0, 128 - a.shape[1])))


def _local_step(x2, mem2, tgt, p):
    s = x2.shape[0]
    head_scale = HEAD_DIM ** -0.5
    mla_scale = MLA_QK ** -0.5
    tabs = _rope_tables(s)
    mats = _rope_matrices()

    h, hb = _ln_fwd(x2, None, p["ln_in_g"], p["ln_in_b"], "ln_in_fwd")
    _, memn_b = _ln_fwd(mem2, None, p["mem_ln_g"], p["mem_ln_b"], "ln_mem_fwd")

    saved = []
    for l in range(DEPTH):
        wa, wb = _split_w_in(p["w_in"][l])
        wp = jnp.concatenate([wa, wb], axis=1)
        bias_row = jnp.pad(p["b_forget"][l], (0, 128 - N_HEADS)).reshape(1, 128)
        pa = _matmul(hb, wa, MXU_DTYPE, "proj_a")
        pb = _matmul(hb, wb, F32, "proj_b")
        grp = lambda n: _to_heads(pa[:, n * GROUP_W:(n + 1) * GROUP_W])
        fq, fk, fv, sq, sk, sv, mq = (grp(n) for n in range(7))

        fc = _forget_fwd(pb, bias_row)
        ft = fc[:, :N_HEADS].T
        fqb, fkb = ft[:, :, None], ft[:, None, :]
        o_fox, lse_fox = _softmax_fwd(fq, fk, fv, head_scale, True, fqb, fkb, "fox_fwd")
        o_sb, rm_sb = _sb_fwd(sq, sk, sv, head_scale, "sb_fwd")

        qfull, cqn = _mla_q_fwd(pb, p["mla_q_norm_g"][l], p["w_mla_q_up"][l], tabs, mats)
        kv, ckvn, krope = _mla_kv_fwd(pb, p["mla_kv_norm_g"][l], p["w_mla_kv_up"][l], tabs, mats)
        qm = _to_heads(qfull)
        kv4 = kv.reshape(s, N_HEADS, 2 * HEAD_DIM)
        kr = jnp.broadcast_to(krope[:, None, MISC_KROT:MISC_KROT + MLA_ROPE], (s, N_HEADS, MLA_ROPE))
        km = jnp.concatenate([kv4[:, :, :HEAD_DIM], kr], axis=2).transpose(1, 0, 2)
        vm = kv4[:, :, HEAD_DIM:].transpose(1, 0, 2)
        o_mla, lse_mla = _softmax_fwd(qm, km, vm, mla_scale, True, None, None, "mla_fwd")

        mkv = _matmul(memn_b, p["w_mem_kv"][l], MXU_DTYPE, "mem_kv")
        mk, mv = _to_heads(mkv[:, :GROUP_W]), _to_heads(mkv[:, GROUP_W:])
        o_mem, lse_mem = _softmax_fwd(mq, mk, mv, head_scale, False, None, None, "mem_fwd")

        mixed = jnp.concatenate([_from_heads(o) for o in (o_fox, o_sb, o_mla, o_mem)], axis=1)
        gated = _gate_fwd(mixed, pb)
        y = _matmul(gated, p["w_out"][l], F32, "out_proj")
        saved.append(dict(h=h, hb=hb, y=y, wp=wp, bias_row=bias_row, pb=pb, fq=fq, fk=fk, fv=fv, sq=sq, sk=sk,
                          sv=sv, mq=mq, fqb=fqb, fkb=fkb, o_fox=o_fox, lse_fox=lse_fox, o_sb=o_sb, rm_sb=rm_sb,
                          cqn=cqn, ckvn=ckvn, qm=qm, km=km, vm=vm, o_mla=o_mla, lse_mla=lse_mla, mk=mk, mv=mv,
                          o_mem=o_mem, lse_mem=lse_mem, mixed=mixed, gated=gated))
        h, hb = _ln_fwd(h, y, p["ln_g"][l], p["ln_b"][l], "ln_fwd")

    dh, sq_cols = _loss_grad(h, tgt)
    loss_sum = jnp.sum(sq_cols)

    grads = {k: [None] * DEPTH for k in ("w_in", "b_forget", "mla_q_norm_g", "w_mla_q_up", "mla_kv_norm_g",
                                         "w_mla_kv_up", "w_mem_kv", "w_out", "ln_g", "ln_b")}
    dmemn = []
    dy1, dy2, c1 = dh, None, 1.0
    for l in reversed(range(DEPTH)):
        r = saved[l]
        pb = r["pb"]
        du, du_b, dg, db = _ln_bwd(r["h"], r["y"], p["ln_g"][l], dy1, dy2, c1, "ln_bwd")
        grads["ln_g"][l], grads["ln_b"][l] = dg[0], db[0]
        dgated = _matmul(du_b, p["w_out"][l].T, F32, "out_proj_dx")
        grads["w_out"][l] = _matmul(r["gated"].T, du_b, F32, "out_proj_dw")
        dmixed, dgate_b = _gate_bwd(dgated, r["mixed"], pb)
        dgrp = lambda n: _to_heads(dmixed[:, n * GROUP_W:(n + 1) * GROUP_W])

        dfq, dfk, dfv, dfqb, dfkb = _softmax_bwd(r["fq"], r["fk"], r["fv"], r["o_fox"], r["lse_fox"], dgrp(0),
                                                 head_scale, True, r["fqb"], r["fkb"], "fox_bwd")
        dmisc_f, dbf = _forget_bwd(pb, r["bias_row"], _pad_heads_cols(dfqb[:, :, 0].T),
                                   _pad_heads_cols(dfkb[:, 0, :].T))
        grads["b_forget"][l] = dbf[0, :N_HEADS]

        dsq, dsk, dsv = _sb_bwd(r["sq"], r["sk"], r["sv"], dgrp(1), r["rm_sb"], head_scale, "sb_bwd")

        dqm, dkm, dvm = _softmax_bwd(r["qm"], r["km"], r["vm"], r["o_mla"], r["lse_mla"], dgrp(2),
                                     mla_scale, True, None, None, "mla_bwd")
        dq_mla_b = _rope_q_bwd(_from_heads(dqm), tabs, mats)
        dcqn = _matmul(dq_mla_b, p["w_mla_q_up"][l].T, F32, "q_up_dx")
        grads["w_mla_q_up"][l] = _matmul(r["cqn"].T, dq_mla_b, F32, "q_up_dw")
        dcq_b, dgq = _rms_bwd(pb, MLA_Q_RANK, PB_CQ_BLK, p["mla_q_norm_g"][l], dcqn, "rms_q_bwd")
        grads["mla_q_norm_g"][l] = dgq[0]
        dkv_b = jnp.concatenate([dkm[:, :, :HEAD_DIM].transpose(1, 0, 2), dvm.transpose(1, 0, 2)],
                                axis=2).reshape(s, N_HEADS * 2 * HEAD_DIM).astype(MXU_DTYPE)
        dckvn = _matmul(dkv_b, p["w_mla_kv_up"][l].T, F32, "kv_up_dx")
        grads["w_mla_kv_up"][l] = _matmul(r["ckvn"].T, dkv_b, F32, "kv_up_dw")
        dckv_b, dgkv = _rms_bwd(pb, MLA_KV_RANK, PB_CKV_BLK, p["mla_kv_norm_g"][l], dckvn, "rms_kv_bwd")
        grads["mla_kv_norm_g"][l] = dgkv[0]
        dmisc_k = _rope_k_bwd(_from_heads(dkm[:, :, HEAD_DIM:]), tabs, mats)

        dmq, dmk, dmv = _softmax_bwd(r["mq"], r["mk"], r["mv"], r["o_mem"], r["lse_mem"], dgrp(3),
                                     head_scale, False, None, None, "mem_bwd")
        dmkv_b = jnp.concatenate([_from_heads(dmk), _from_heads(dmv)], axis=1).astype(MXU_DTYPE)
        grads["w_mem_kv"][l] = _matmul(memn_b.T, dmkv_b, F32, "mem_kv_dw")
        dmemn.append(_matmul(dmkv_b, p["w_mem_kv"][l].T, F32, "mem_kv_dx"))

        dmisc = jnp.concatenate([dmisc_f[:, :MISC_KROT], dmisc_k[:, MISC_KROT:MISC_KROT + MLA_ROPE],
                                 jnp.zeros((s, 128 - MISC_KROT - MLA_ROPE), F32)], axis=1)
        dp = jnp.concatenate([_from_heads(d).astype(MXU_DTYPE) for d in (dfq, dfk, dfv, dsq, dsk, dsv, dmq)]
                             + [dgate_b, dcq_b, dckv_b, dmisc.astype(MXU_DTYPE)], axis=1)
        dhproj = _matmul(dp, r["wp"].T, F32, "proj_dx")
        grads["w_in"][l] = _merge_dw_in(_matmul(r["hb"].T, dp, F32, "proj_dw"))
        dy1, dy2, c1 = du, dhproj, ALPHA

    dx, _, dg_in, db_in = _ln_bwd(x2, None, p["ln_in_g"], dy1, dy2, c1, "ln_in_bwd")
    _, _, dg_mem, db_mem = _ln_bwd(mem2, None, p["mem_ln_g"], dmemn[0], dmemn[1], 1.0, "ln_mem_bwd")
    out = {k: jnp.stack(v) for k, v in grads.items()}
    out.update(ln_in_g=dg_in[0], ln_in_b=db_in[0], mem_ln_g=dg_mem[0], mem_ln_b=db_mem[0])
    return loss_sum, dx, out


BIG_NAMES = ("w_in", "w_out", "w_mem_kv", "w_mla_q_up", "w_mla_kv_up")
BIG_AXIS = dict(w_in=2, w_out=1, w_mem_kv=1, w_mla_q_up=2, w_mla_kv_up=2)
SMALL_NAMES = ("ln_in_g", "ln_in_b", "mem_ln_g", "mem_ln_b", "ln_g", "ln_b", "b_forget", "mla_q_norm_g",
               "mla_kv_norm_g")
ALL_NAMES = ("ln_in_g", "ln_in_b", "mem_ln_g", "mem_ln_b", "w_in", "b_forget", "mla_q_norm_g", "w_mla_q_up",
             "mla_kv_norm_g", "w_mla_kv_up", "w_mem_kv", "w_out", "ln_g", "ln_b")
N_CHIPS = 4
N_DEV = 8


def _rows_of(shape):
    rows = -(-int(np.prod(shape)) // LANES)
    return -(-rows // PACK_ALIGN) * PACK_ALIGN


def _pack(arrs, rows):
    parts = []
    for a in arrs:
        f = a.reshape(-1)
        n = _rows_of(a.shape) * LANES
        parts.append(jnp.pad(f, (0, n - f.shape[0])).reshape(-1, LANES))
    used = sum(q.shape[0] for q in parts)
    if rows > used:
        parts.append(jnp.zeros((rows - used, LANES), parts[0].dtype))
    return jnp.concatenate(parts, axis=0)


def _unpack(buf, shapes):
    out, r = [], 0
    for shp in shapes:
        n = _rows_of(shp)
        out.append(buf[r:r + n].reshape(-1)[:int(np.prod(shp))].reshape(shp))
        r += n
    return out


HBM_SPEC = pl.BlockSpec(memory_space=pltpu.HBM)


def _gather_weights(shard):
    rows = shard.shape[0]

    def body(w_ref, out_ref, send_sems, recv_sems, local_sem):
        x, y, c = (lax.axis_index(a) for a in MESH_AXES)
        me = 2 * x + y
        chips = [(1 - x, y), (x, 1 - y), (1 - x, 1 - y)]

        def copy(k, block, to):
            return pltpu.make_async_remote_copy(
                src_ref=w_ref, dst_ref=out_ref.at[block], send_sem=send_sems.at[k], recv_sem=recv_sems.at[k],
                device_id=to, device_id_type=pl.DeviceIdType.MESH)

        mine = pltpu.make_async_copy(w_ref, out_ref.at[me], local_sem)
        mine.start()
        sends = [copy(k, me, (px, py, c)) for k, (px, py) in enumerate(chips)]
        for cp in sends:
            cp.start()
        for k, (px, py) in enumerate(chips):
            copy(k, 2 * px + py, (px, py, c)).wait_recv()
        for cp in sends:
            cp.wait_send()
        mine.wait()

    return pl.pallas_call(
        body, name="gather_weights", out_shape=jax.ShapeDtypeStruct((N_CHIPS, rows, LANES), shard.dtype),
        in_specs=[HBM_SPEC], out_specs=HBM_SPEC,
        scratch_shapes=[pltpu.SemaphoreType.DMA((3,)), pltpu.SemaphoreType.DMA((3,)), pltpu.SemaphoreType.DMA],
    )(shard)


def _exchange_grads(big, small):
    rb, rs = big.shape[1], small.shape[0]

    def body(big_ref, small_ref, obig_ref, osmall_ref, send_sems, recv_sems, local_sems):
        x, y, c = (lax.axis_index(a) for a in MESH_AXES)
        me = 4 * x + 2 * y + c
        flips = [(fx, fy, fc) for fx in (0, 1) for fy in (0, 1) for fc in (0, 1) if fx or fy or fc]
        peers = [(1 - x if fx else x, 1 - y if fy else y, 1 - c if fc else c) for fx, fy, fc in flips]

        def copy_big(k, src_chip, slot, to):
            return pltpu.make_async_remote_copy(
                src_ref=big_ref.at[src_chip], dst_ref=obig_ref.at[slot], send_sem=send_sems.at[k],
                recv_sem=recv_sems.at[k], device_id=to, device_id_type=pl.DeviceIdType.MESH)

        def copy_small(k, slot, to):
            return pltpu.make_async_remote_copy(
                src_ref=small_ref, dst_ref=osmall_ref.at[slot], send_sem=send_sems.at[7 + k],
                recv_sem=recv_sems.at[7 + k], device_id=to, device_id_type=pl.DeviceIdType.MESH)

        own_big = pltpu.make_async_copy(big_ref.at[2 * x + y], obig_ref.at[me], local_sems.at[0])
        own_small = pltpu.make_async_copy(small_ref, osmall_ref.at[me], local_sems.at[1])
        own_big.start()
        own_small.start()
        sends = []
        for k, (px, py, pc) in enumerate(peers):
            sends += [copy_big(k, 2 * px + py, me, (px, py, pc)), copy_small(k, me, (px, py, pc))]
        for cp in sends:
            cp.start()
        for k, (px, py, pc) in enumerate(peers):
            slot = 4 * px + 2 * py + pc
            copy_big(k, 2 * x + y, slot, (px, py, pc)).wait_recv()
            copy_small(k, slot, (px, py, pc)).wait_recv()
        for cp in sends:
            cp.wait_send()
        own_big.wait()
        own_small.wait()

    return pl.pallas_call(
        body, name="exchange_grads",
        out_shape=[jax.ShapeDtypeStruct((N_DEV, rb, LANES), big.dtype),
                   jax.ShapeDtypeStruct((N_DEV, rs, LANES), small.dtype)],
        in_specs=[HBM_SPEC, HBM_SPEC], out_specs=[HBM_SPEC, HBM_SPEC],
        scratch_shapes=[pltpu.SemaphoreType.DMA((14,)), pltpu.SemaphoreType.DMA((14,)),
                        pltpu.SemaphoreType.DMA((2,))],
    )(big, small)


def _adamw(parts, w, m, v, name):
    rows = w.shape[0]
    tile = _pick(rows, (128, 16, 8))
    bc1 = 1.0 - ADAM_B1 ** ADAM_STEP
    bc2 = 1.0 - ADAM_B2 ** ADAM_STEP

    def kern(p_ref, w_ref, m_ref, v_ref, g_ref, d_ref, nm_ref, nv_ref):
        g = p_ref[0]
        for d in range(1, N_DEV):
            g = g + p_ref[d]
        nm = ADAM_B1 * m_ref[...] + (1.0 - ADAM_B1) * g
        nv = ADAM_B2 * v_ref[...] + (1.0 - ADAM_B2) * (g * g)
        g_ref[...] = g
        nm_ref[...] = nm
        nv_ref[...] = nv
        d_ref[...] = -ADAM_LR * ((nm / bc1) / (jnp.sqrt(nv / bc2) + ADAM_EPS) + ADAM_WD * w_ref[...])

    spec = pl.BlockSpec((tile, LANES), lambda i: (i, 0))
    return pl.pallas_call(
        kern, name=name, grid=(rows // tile,),
        in_specs=[pl.BlockSpec((N_DEV, tile, LANES), lambda i: (0, i, 0)), spec, spec, spec],
        out_specs=[spec] * 4, out_shape=[jax.ShapeDtypeStruct((rows, LANES), F32)] * 4,
        compiler_params=pltpu.CompilerParams(dimension_semantics=("parallel",), vmem_limit_bytes=VMEM_LIMIT),
    )(parts, w, m, v)


def kernel(x, mem, ln_in_g, ln_in_b, mem_ln_g, mem_ln_b, w_in, b_forget, mla_q_norm_g, w_mla_q_up, mla_kv_norm_g, w_mla_kv_up, w_mem_kv, w_out, ln_g, ln_b, loss_target, m_ln_in_g, m_ln_in_b, m_mem_ln_g, m_mem_ln_b, m_w_in, m_b_forget, m_mla_q_norm_g, m_w_mla_q_up, m_mla_kv_norm_g, m_w_mla_kv_up, m_w_mem_kv, m_w_out, m_ln_g, m_ln_b, v_ln_in_g, v_ln_in_b, v_mem_ln_g, v_mem_ln_b, v_w_in, v_b_forget, v_mla_q_norm_g, v_w_mla_q_up, v_mla_kv_norm_g, v_w_mla_kv_up, v_w_mem_kv, v_w_out, v_ln_g, v_ln_b):
    w = dict(ln_in_g=ln_in_g, ln_in_b=ln_in_b, mem_ln_g=mem_ln_g, mem_ln_b=mem_ln_b, w_in=w_in, b_forget=b_forget,
             mla_q_norm_g=mla_q_norm_g, w_mla_q_up=w_mla_q_up, mla_kv_norm_g=mla_kv_norm_g,
             w_mla_kv_up=w_mla_kv_up, w_mem_kv=w_mem_kv, w_out=w_out, ln_g=ln_g, ln_b=ln_b)
    mo = dict(ln_in_g=m_ln_in_g, ln_in_b=m_ln_in_b, mem_ln_g=m_mem_ln_g, mem_ln_b=m_mem_ln_b, w_in=m_w_in,
              b_forget=m_b_forget, mla_q_norm_g=m_mla_q_norm_g, w_mla_q_up=m_w_mla_q_up,
              mla_kv_norm_g=m_mla_kv_norm_g, w_mla_kv_up=m_w_mla_kv_up, w_mem_kv=m_w_mem_kv, w_out=m_w_out,
              ln_g=m_ln_g, ln_b=m_ln_b)
    vo = dict(ln_in_g=v_ln_in_g, ln_in_b=v_ln_in_b, mem_ln_g=v_mem_ln_g, mem_ln_b=v_mem_ln_b, w_in=v_w_in,
              b_forget=v_b_forget, mla_q_norm_g=v_mla_q_norm_g, w_mla_q_up=v_w_mla_q_up,
              mla_kv_norm_g=v_mla_kv_norm_g, w_mla_kv_up=v_w_mla_kv_up, w_mem_kv=v_w_mem_kv, w_out=v_w_out,
              ln_g=v_ln_g, ln_b=v_ln_b)
    big_shapes = [w[n].shape for n in BIG_NAMES]
    small_shapes = [w[n].shape for n in SMALL_NAMES]

    gathered = _gather_weights(_pack([w[n].astype(MXU_DTYPE) for n in BIG_NAMES], BIG_ROWS))
    per_chip = [_unpack(gathered[j], big_shapes) for j in range(N_CHIPS)]
    full = dict(w)
    for idx, n in enumerate(BIG_NAMES):
        full[n] = jnp.concatenate([per_chip[j][idx] for j in range(N_CHIPS)], axis=BIG_AXIS[n])

    loss_sum, dx, g = _local_step(x[0], mem[0], loss_target[0], full)
    loss = lax.psum(loss_sum * (0.5 / D_MODEL), MESH_AXES)

    def shard_of(n, j):
        ax, size = BIG_AXIS[n], w[n].shape[BIG_AXIS[n]]
        return lax.slice_in_dim(g[n], j * size, (j + 1) * size, axis=ax)

    big = jnp.stack([_pack([shard_of(n, j) for n in BIG_NAMES], BIG_ROWS) for j in range(N_CHIPS)])
    small = _pack([g[n] for n in SMALL_NAMES], SMALL_ROWS)
    big_parts, small_parts = _exchange_grads(big, small)

    res_big = _adamw(big_parts, _pack([w[n] for n in BIG_NAMES], BIG_ROWS), _pack([mo[n] for n in BIG_NAMES], BIG_ROWS),
                     _pack([vo[n] for n in BIG_NAMES], BIG_ROWS), "adamw_sharded")
    res_small = _adamw(small_parts, _pack([w[n] for n in SMALL_NAMES], SMALL_ROWS),
                       _pack([mo[n] for n in SMALL_NAMES], SMALL_ROWS), _pack([vo[n] for n in SMALL_NAMES], SMALL_ROWS),
                       "adamw_replicated")
    outs = []
    for kind in range(4):
        vals = dict(zip(BIG_NAMES, _unpack(res_big[kind], big_shapes)))
        vals.update(zip(SMALL_NAMES, _unpack(res_small[kind], small_shapes)))
        outs += [vals[n] for n in ALL_NAMES]
    return (loss, dx[None], *outs)
```

```python
import numpy as np
import jax
import jax.numpy as jnp
from jax import lax
from jax.experimental import pallas as pl
from jax.experimental.pallas import tpu as pltpu

F32 = jnp.float32
MXU_DTYPE = jnp.bfloat16

DEPTH = 2
D_MODEL = 1024
GROUP_W = 256
N_HEADS = 4
HEAD_DIM = 64
MLA_Q_RANK = 256
MLA_KV_RANK = 128
MLA_ROPE = 32
MLA_Q_COLS = N_HEADS * (HEAD_DIM + MLA_ROPE)
MLA_KV_COLS = N_HEADS * 2 * HEAD_DIM
ROPE_THETA = 10000.0
LN_EPS = 1e-5
RMS_EPS = 1e-6
ALPHA = (2 * DEPTH) ** 0.25
ADAM_LR, ADAM_B1, ADAM_B2, ADAM_EPS, ADAM_WD, ADAM_STEP = 0.001, 0.9, 0.999, 1e-08, 0.01, 10

_SPLIT = (256, 256, 256, 4, 256, 256, 256, 256, 128, 32, 256, 1024)
_OFF = [int(o) for o in np.cumsum((0,) + _SPLIT)]
IN_COLS = _OFF[-1]
PA_COLS = 7 * GROUP_W
PB_COLS = 1024 + 256 + 128 + 128
PB_CQ_BLK, PB_CKV_BLK, PB_MISC_BLK = 4, 10, 11
MISC_KROT = 4

LANES = 1024
PACK_ALIGN = 16
BIG_ROWS = 2560
SMALL_ROWS = 144
ROW_TILE = 256
ATT_TILE = 256
PAIR = 128
SB_SLOT = PAIR // N_HEADS
VMEM_LIMIT = 56 * 1024 * 1024
NEG = -1e30
MESH_AXES = ("x", "y", "c")


def _dot(a, b):
    return jnp.dot(a, b, preferred_element_type=F32)


def _dot_nt(a, b):
    return lax.dot_general(a, b, (((1,), (1,)), ((), ())), preferred_element_type=F32)


def _dot_tn(a, b):
    return lax.dot_general(a, b, (((0,), (0,)), ((), ())), preferred_element_type=F32)


def _split2(x):
    hi = x.astype(MXU_DTYPE)
    lo = (x - hi.astype(F32)).astype(MXU_DTYPE)
    return hi, lo


def _split3(x):
    hi = x.astype(MXU_DTYPE)
    r = x - hi.astype(F32)
    mid = r.astype(MXU_DTYPE)
    lo = (r - mid.astype(F32)).astype(MXU_DTYPE)
    return hi, mid, lo


def _dot_exact_r(x, pm):
    hi, mid, lo = _split3(x)
    return _dot(hi, pm) + _dot(mid, pm) + _dot(lo, pm)


def _dot_exact_l(pm, x):
    hi, mid, lo = _split3(x)
    return _dot(pm, hi) + _dot(pm, mid) + _dot(pm, lo)


def _pick(dim, prefs):
    for p in prefs:
        if dim % p == 0:
            return p
    return dim


def _softplus(z):
    return jnp.maximum(z, 0.0) + jnp.log(1.0 + jnp.exp(-jnp.abs(z)))


def _matmul(a, b, out_dtype, name):
    m, k = a.shape
    _, n = b.shape
    tm = _pick(m, (512, 256, 128))
    tn = _pick(n, (512, 256, 128))
    tk = k if k <= 4096 else _pick(k, (1024, 512, 256, 128))
    nk = k // tk

    def kern(a_ref, b_ref, o_ref, acc_ref):
        kk = pl.program_id(2)

        @pl.when(kk == 0)
        def _():
            acc_ref[...] = jnp.zeros_like(acc_ref)

        acc_ref[...] += _dot(a_ref[...], b_ref[...])

        @pl.when(kk == nk - 1)
        def _():
            o_ref[...] = acc_ref[...].astype(o_ref.dtype)

    return pl.pallas_call(
        kern, name=name, grid=(m // tm, n // tn, nk),
        in_specs=[pl.BlockSpec((tm, tk), lambda i, j, kk: (i, kk)),
                  pl.BlockSpec((tk, tn), lambda i, j, kk: (kk, j))],
        out_specs=pl.BlockSpec((tm, tn), lambda i, j, kk: (i, j)),
        out_shape=jax.ShapeDtypeStruct((m, n), out_dtype),
        scratch_shapes=[pltpu.VMEM((tm, tn), F32)],
        compiler_params=pltpu.CompilerParams(
            dimension_semantics=("parallel", "parallel", "arbitrary"), vmem_limit_bytes=VMEM_LIMIT),
    )(a.astype(MXU_DTYPE), b.astype(MXU_DTYPE))


def _rowwise(body, name, rows, tile, row_ins, full_ins, row_outs, acc_outs=(), scratch=(),
             reverse=False, sequential=False):
    n = rows // tile

    def ridx(i):
        return (n - 1 - i) if reverse else i

    in_specs, args = [], []
    for arr, width, cb in row_ins:
        in_specs.append(pl.BlockSpec((tile, width), lambda i, cb=cb: (ridx(i), cb)))
        args.append(arr)
    for arr in full_ins:
        in_specs.append(pl.BlockSpec(arr.shape, lambda i, nd=arr.ndim: (0,) * nd))
        args.append(arr)
    out_shape = [jax.ShapeDtypeStruct((rows, w), dt) for w, dt in row_outs]
    out_shape += [jax.ShapeDtypeStruct(s, dt) for s, dt in acc_outs]
    out_specs = [pl.BlockSpec((tile, w), lambda i: (ridx(i), 0)) for w, dt in row_outs]
    out_specs += [pl.BlockSpec(s, lambda i, nd=len(s): (0,) * nd) for s, dt in acc_outs]

    def kern(*refs):
        body(pl.program_id(0), *refs)

    sem = "arbitrary" if (acc_outs or sequential) else "parallel"
    return pl.pallas_call(
        kern, name=name, grid=(n,), in_specs=in_specs, out_specs=out_specs, out_shape=out_shape,
        scratch_shapes=list(scratch),
        compiler_params=pltpu.CompilerParams(dimension_semantics=(sem,), vmem_limit_bytes=VMEM_LIMIT),
    )(*args)


def _ln_stats(u):
    mu = jnp.mean(u, axis=-1, keepdims=True)
    xc = u - mu
    var = jnp.mean(xc * xc, axis=-1, keepdims=True)
    return xc, lax.rsqrt(var + LN_EPS)


def _ln_fwd(a, b, g, beta, name):
    rows, d = a.shape
    has_b = b is not None

    def body(i, *refs):
        if has_b:
            a_ref, b_ref, g_ref, be_ref, h_ref, hb_ref = refs
            u = ALPHA * a_ref[...] + b_ref[...]
        else:
            a_ref, g_ref, be_ref, h_ref, hb_ref = refs
            u = a_ref[...]
        xc, rstd = _ln_stats(u)
        y = xc * rstd * g_ref[...] + be_ref[...]
        h_ref[...] = y
        hb_ref[...] = y.astype(hb_ref.dtype)

    row_ins = [(a, d, 0)] + ([(b, d, 0)] if has_b else [])
    return _rowwise(body, name, rows, min(ROW_TILE, rows), row_ins,
                    [g.reshape(1, d), beta.reshape(1, d)], [(d, F32), (d, MXU_DTYPE)])


def _ln_bwd(a, b, g, dy1, dy2, c1, name):
    rows, d = a.shape
    has_b = b is not None
    has_2 = dy2 is not None

    def body(i, *refs):
        refs = list(refs)
        a_ref = refs.pop(0)
        u = a_ref[...]
        if has_b:
            u = ALPHA * u + refs.pop(0)[...]
        dy = c1 * refs.pop(0)[...]
        if has_2:
            dy = dy + refs.pop(0)[...]
        g_ref, du_ref, dub_ref, dg_ref, db_ref = refs

        @pl.when(i == 0)
        def _():
            dg_ref[...] = jnp.zeros_like(dg_ref)
            db_ref[...] = jnp.zeros_like(db_ref)

        xc, rstd = _ln_stats(u)
        xhat = xc * rstd
        dxh = dy * g_ref[...]
        m1 = jnp.mean(dxh, axis=-1, keepdims=True)
        m2 = jnp.mean(dxh * xhat, axis=-1, keepdims=True)
        du = rstd * (dxh - m1 - xhat * m2)
        du_ref[...] = du
        dub_ref[...] = du.astype(dub_ref.dtype)
        dg_ref[...] += jnp.sum(dy * xhat, axis=0, keepdims=True)
        db_ref[...] += jnp.sum(dy, axis=0, keepdims=True)

    row_ins = [(a, d, 0)] + ([(b, d, 0)] if has_b else []) + [(dy1, d, 0)] + ([(dy2, d, 0)] if has_2 else [])
    return _rowwise(body, name, rows, min(ROW_TILE, rows), row_ins, [g.reshape(1, d)],
                    [(d, F32), (d, MXU_DTYPE)], [((1, d), F32), ((1, d), F32)])


def _loss_grad(h, target):
    rows, d = h.shape

    def body(i, h_ref, t_ref, dh_ref, acc_ref):
        @pl.when(i == 0)
        def _():
            acc_ref[...] = jnp.zeros_like(acc_ref)

        e = h_ref[...] - t_ref[...]
        dh_ref[...] = e * (1.0 / d)
        acc_ref[...] += jnp.sum(e * e, axis=0, keepdims=True)

    return _rowwise(body, "loss_grad", rows, ROW_TILE, [(h, d, 0), (target, d, 0)], [],
                    [(d, F32)], [((1, d), F32)])


def _gate_fwd(groups, pb):
    rows = pb.shape[0]
    w = GROUP_W * len(groups)

    def body(i, *refs):
        g = refs[4][...]
        mixed = jnp.concatenate([r[...] for r in refs[:4]], axis=1)
        refs[5][...] = (mixed * (g / (1.0 + jnp.exp(-g)))).astype(refs[5].dtype)

    return _rowwise(body, "gate_fwd", rows, ROW_TILE, [(o, GROUP_W, 0) for o in groups] + [(pb, w, 0)], [],
                    [(w, MXU_DTYPE)])[0]


def _gate_bwd(dgated, groups, pb):
    rows = pb.shape[0]
    w = GROUP_W * len(groups)

    def body(i, *refs):
        dg = refs[0][...]
        mixed = jnp.concatenate([r[...] for r in refs[1:5]], axis=1)
        g = refs[5][...]
        dm_ref, dgate_ref = refs[6], refs[7]
        sig = 1.0 / (1.0 + jnp.exp(-g))
        dm_ref[...] = dg * (g * sig)
        dgate_ref[...] = (dg * mixed * (sig * (1.0 + g * (1.0 - sig)))).astype(dgate_ref.dtype)

    return _rowwise(body, "gate_bwd", rows, ROW_TILE,
                    [(dgated, w, 0)] + [(o, GROUP_W, 0) for o in groups] + [(pb, w, 0)], [],
                    [(w, F32), (w, MXU_DTYPE)])


def _tri(n, kind):
    r = np.arange(n)[:, None]
    c = np.arange(n)[None, :]
    m = {"lower_incl": r >= c, "upper_incl": r <= c, "row_gt_col": r > c, "row_lt_col": r < c}[kind]
    return jnp.asarray(m.astype(np.float32), dtype=MXU_DTYPE)


def _forget_fwd(pb, bias_row):
    rows = pb.shape[0]
    tile = min(ROW_TILE, rows)

    def body(i, x_ref, b_ref, l_ref, o_ref, carry_ref):
        @pl.when(i == 0)
        def _():
            carry_ref[...] = jnp.zeros_like(carry_ref)

        xx = x_ref[...] + b_ref[...]
        lane = lax.broadcasted_iota(jnp.int32, xx.shape, 1)
        lf = jnp.where(lane < N_HEADS, -_softplus(-xx), 0.0)
        o_ref[...] = _dot_exact_l(l_ref[...], lf) + carry_ref[...]
        carry_ref[...] += jnp.sum(lf, axis=0, keepdims=True)

    return _rowwise(body, "forget_fwd", rows, tile, [(pb, 128, PB_MISC_BLK)],
                    [bias_row, _tri(tile, "lower_incl")], [(128, F32)],
                    scratch=[pltpu.VMEM((1, 128), F32)], sequential=True)[0]


def _forget_bwd(pb, bias_row, dfq, dfk):
    rows = pb.shape[0]
    tile = min(ROW_TILE, rows)

    def body(i, x_ref, dq_ref, dk_ref, b_ref, u_ref, o_ref, db_ref, carry_ref):
        @pl.when(i == 0)
        def _():
            carry_ref[...] = jnp.zeros_like(carry_ref)
            db_ref[...] = jnp.zeros_like(db_ref)

        df = dq_ref[...] + dk_ref[...]
        sfx = _dot_exact_l(u_ref[...], df) + carry_ref[...]
        carry_ref[...] += jnp.sum(df, axis=0, keepdims=True)
        xx = x_ref[...] + b_ref[...]
        lane = lax.broadcasted_iota(jnp.int32, xx.shape, 1)
        dl = jnp.where(lane < N_HEADS, sfx / (1.0 + jnp.exp(xx)), 0.0)
        o_ref[...] = dl
        db_ref[...] += jnp.sum(dl, axis=0, keepdims=True)

    return _rowwise(body, "forget_bwd", rows, tile,
                    [(pb, 128, PB_MISC_BLK), (dfq, 128, 0), (dfk, 128, 0)],
                    [bias_row, _tri(tile, "upper_incl")], [(128, F32)], [((1, 128), F32)],
                    scratch=[pltpu.VMEM((1, 128), F32)], reverse=True, sequential=True)


def _rope_tables(s):
    half = MLA_ROPE // 2
    inv_freq = ROPE_THETA ** (-jnp.arange(half, dtype=F32) / half)
    ang = jnp.arange(s).astype(F32)[:, None] * inv_freq[None, :]
    cos2 = jnp.tile(jnp.cos(ang), (1, 2))
    sin2 = jnp.tile(jnp.sin(ang), (1, 2))
    cx = jnp.tile(cos2, (1, N_HEADS))
    sx = jnp.tile(sin2, (1, N_HEADS))
    cq = jnp.concatenate([jnp.ones((s, GROUP_W), F32), cx], axis=1)
    sq = jnp.concatenate([jnp.zeros((s, GROUP_W), F32), sx], axis=1)
    pad = ((0, 0), (MISC_KROT, 128 - MISC_KROT - MLA_ROPE))
    ck = jnp.pad(cos2, pad)
    sk = jnp.pad(sin2, pad)
    return dict(cq=cq, sq=sq, ck=ck, sk=sk, cx=cx, sx=sx)


def _rot_matrix(width, bases):
    half = MLA_ROPE // 2
    p = np.zeros((width, width), np.float32)
    for b in bases:
        for i in range(half):
            p[b + half + i, b + i] = -1.0
            p[b + i, b + half + i] = 1.0
    return p


def _rope_matrices():
    pq = _rot_matrix(MLA_Q_COLS, [GROUP_W + h * MLA_ROPE for h in range(N_HEADS)])
    pk = _rot_matrix(128, [MISC_KROT])
    p4 = _rot_matrix(128, [h * MLA_ROPE for h in range(N_HEADS)])
    a = np.zeros((128, 128), np.float32)
    for h in range(N_HEADS):
        for r in range(MLA_ROPE):
            a[h * MLA_ROPE + r, MISC_KROT + r] = 1.0
    cast = lambda m: jnp.asarray(m, dtype=MXU_DTYPE)
    return dict(pq=cast(pq), pqt=cast(pq.T), pk=cast(pk), spread=cast(a.T), xa=cast(a), xb=cast(p4.T @ a))


def _rms(c, g):
    r = lax.rsqrt(jnp.mean(c * c, axis=-1, keepdims=True) + RMS_EPS)
    return c * r * g


def _mla_q_fwd(pb, g, w_up, tabs, mats):
    rows = pb.shape[0]

    def body(i, c_ref, cos_ref, sin_ref, g_ref, w_ref, p_ref, q_ref, cn_ref):
        cn = _rms(c_ref[...], g_ref[...]).astype(cn_ref.dtype)
        cn_ref[...] = cn
        q = _dot(cn, w_ref[...])
        q_ref[...] = (q * cos_ref[...] + _dot_exact_r(q, p_ref[...]) * sin_ref[...]).astype(q_ref.dtype)

    return _rowwise(body, "mla_q_fwd", rows, ROW_TILE,
                    [(pb, MLA_Q_RANK, PB_CQ_BLK), (tabs["cq"], MLA_Q_COLS, 0), (tabs["sq"], MLA_Q_COLS, 0)],
                    [g.reshape(1, MLA_Q_RANK), w_up.astype(MXU_DTYPE), mats["pq"]],
                    [(MLA_Q_COLS, MXU_DTYPE), (MLA_Q_RANK, MXU_DTYPE)])


def _mla_kv_fwd(pb, g, w_up, tabs, mats):
    rows = pb.shape[0]

    def body(i, c_ref, x_ref, cos_ref, sin_ref, g_ref, w_ref, p_ref, sp_ref, kv_ref, cn_ref, kr_ref):
        cn = _rms(c_ref[...], g_ref[...]).astype(cn_ref.dtype)
        cn_ref[...] = cn
        kv_ref[...] = _dot(cn, w_ref[...]).astype(kv_ref.dtype)
        xx = x_ref[...]
        kr = xx * cos_ref[...] + _dot_exact_r(xx, p_ref[...]) * sin_ref[...]
        kr_ref[...] = _dot_exact_r(kr, sp_ref[...]).astype(kr_ref.dtype)

    return _rowwise(body, "mla_kv_fwd", rows, ROW_TILE,
                    [(pb, MLA_KV_RANK, PB_CKV_BLK), (pb, 128, PB_MISC_BLK), (tabs["ck"], 128, 0), (tabs["sk"], 128, 0)],
                    [g.reshape(1, MLA_KV_RANK), w_up.astype(MXU_DTYPE), mats["pk"], mats["spread"]],
                    [(MLA_KV_COLS, MXU_DTYPE), (MLA_KV_RANK, MXU_DTYPE), (128, MXU_DTYPE)])


def _rope_q_bwd(dq_full, tabs, mats):
    rows, nq = dq_full.shape

    def body(i, d_ref, cos_ref, sin_ref, pt_ref, o_ref):
        d = d_ref[...]
        o_ref[...] = (d * cos_ref[...] + _dot_exact_r(d * sin_ref[...], pt_ref[...])).astype(o_ref.dtype)

    return _rowwise(body, "rope_q_bwd", rows, ROW_TILE,
                    [(dq_full, nq, 0), (tabs["cq"], nq, 0), (tabs["sq"], nq, 0)], [mats["pqt"]],
                    [(nq, MXU_DTYPE)])[0]


def _rope_k_bwd(dkr, tabs, mats):
    rows = dkr.shape[0]

    def body(i, d_ref, cos_ref, sin_ref, a_ref, b_ref, o_ref):
        d = d_ref[...]
        o_ref[...] = _dot_exact_r(d * cos_ref[...], a_ref[...]) + _dot_exact_r(d * sin_ref[...], b_ref[...])

    return _rowwise(body, "rope_k_bwd", rows, ROW_TILE,
                    [(dkr, 128, 0), (tabs["cx"], 128, 0), (tabs["sx"], 128, 0)], [mats["xa"], mats["xb"]],
                    [(128, F32)])[0]


def _rms_bwd(pb, width, col_blk, g, dy, name):
    rows = pb.shape[0]

    def body(i, c_ref, dy_ref, g_ref, dc_ref, dg_ref):
        @pl.when(i == 0)
        def _():
            dg_ref[...] = jnp.zeros_like(dg_ref)

        c = c_ref[...]
        dy = dy_ref[...]
        r = lax.rsqrt(jnp.mean(c * c, axis=-1, keepdims=True) + RMS_EPS)
        dyg = dy * g_ref[...]
        dc = r * dyg - c * (r * r * r) * jnp.mean(c * dyg, axis=-1, keepdims=True)
        dc_ref[...] = dc.astype(dc_ref.dtype)
        dg_ref[...] += jnp.sum(dy * c * r, axis=0, keepdims=True)

    return _rowwise(body, name, rows, ROW_TILE, [(pb, width, col_blk), (dy, width, 0)], [g.reshape(1, width)],
                    [(width, MXU_DTYPE)], [((1, width), F32)])


def _att_params(parallel):
    return pltpu.CompilerParams(dimension_semantics=("parallel" if parallel else "arbitrary",),
                                vmem_limit_bytes=VMEM_LIMIT)


def _blk_off(j, t):
    return j * t if isinstance(j, int) else pl.multiple_of(j * t, t)


def _causal_mask(t, strict):
    r = lax.broadcasted_iota(jnp.int32, (t, t), 0)
    c = lax.broadcasted_iota(jnp.int32, (t, t), 1)
    return (c < r) if strict else (c <= r)


def _lane_mask(kind, head, rows):
    lane = lax.broadcasted_iota(jnp.int32, (rows, PAIR), 1)
    if kind == "pair":
        return (lane < HEAD_DIM) if head % 2 == 0 else (lane >= HEAD_DIM)
    return (lane >= MLA_ROPE * head) & (lane < MLA_ROPE * (head + 1))


def _row_spec(t, cb, width=PAIR):
    return pl.BlockSpec((t, width), lambda i, cb=cb: (i, cb))


def _whole_spec(rows, cb, width=PAIR):
    return pl.BlockSpec((rows, width), lambda i, cb=cb: (0, cb), pipeline_mode=pl.Buffered(1))


def _masked_heads(blocks, kinds, pair, dtype):
    out = []
    for e in range(2):
        head = 2 * pair + e
        parts = [jnp.where(_lane_mask(k, head, b.shape[0]), b.astype(F32), 0.0).astype(dtype)
                 for b, k in zip(blocks, kinds)]
        out.append(parts[0] if len(parts) == 1 else jnp.concatenate(parts, axis=1))
    return out


def _col(block, idx):
    lane = lax.broadcasted_iota(jnp.int32, block.shape, 1)
    return jnp.sum(jnp.where(lane == idx, block, 0.0), axis=1, keepdims=True)


def _scatter_cols(cols, t):
    lane = lax.broadcasted_iota(jnp.int32, (t, PAIR), 1)
    out = jnp.zeros((t, PAIR), F32)
    for idx, c in cols.items():
        out = out + jnp.where(lane == idx, c, 0.0)
    return out


def _take_heads(per_head, pair):
    return jnp.where(_lane_mask("pair", 0, per_head[0].shape[0]), per_head[0], per_head[1])


class _Parts:
    def __init__(self, q_parts, k_parts, v_parts, tq, sk):
        self.kinds = [[kind for _, _, kind in q_parts[p]] for p in range(2)]
        self.nparts = len(q_parts[0])
        self.q_specs = [_row_spec(tq, cb) for p in range(2) for _, cb, _ in q_parts[p]]
        self.q_args = [a for p in range(2) for a, _, _ in q_parts[p]]
        self.k_specs = [_whole_spec(sk, cb) for p in range(2) for _, cb, _ in k_parts[p]]
        self.k_args = [a for p in range(2) for a, _, _ in k_parts[p]]
        self.v_specs = [_whole_spec(sk, cb) for _, cb in v_parts]
        self.v_args = [a for a, _ in v_parts]
        self.width = PAIR * self.nparts

    def split(self, refs):
        n = self.nparts
        refs = list(refs)
        q = [refs[p * n:(p + 1) * n] for p in range(2)]
        k = [refs[2 * n + p * n:2 * n + (p + 1) * n] for p in range(2)]
        v = refs[4 * n:4 * n + 2]
        return q, k, v, refs[4 * n + 2:]

    def k_block(self, k_refs, off, t):
        blks = [r[pl.ds(off, t), :] for r in k_refs]
        return blks[0] if len(blks) == 1 else jnp.concatenate(blks, axis=1)


def _softmax_fwd(q_parts, k_parts, v_parts, sq, sk, scale, causal, bias, name):
    tq = min(ATT_TILE, sq)
    tk = tq if causal else min(ATT_TILE, sk)
    nkv = sk // tk
    pp = _Parts(q_parts, k_parts, v_parts, tq, sk)

    def kern(*refs):
        q_refs, k_refs, v_refs, rest = pp.split(refs)
        if bias is not None:
            fc_ref, ft_ref, o_ref, lse_ref = rest
            fcb = fc_ref[...]
        else:
            o_ref, lse_ref = rest
        i = pl.program_id(0)
        qm = []
        for p in range(2):
            qm += _masked_heads([r[...] for r in q_refs[p]], pp.kinds[p], p, MXU_DTYPE)
        fq = [_col(fcb, h) for h in range(N_HEADS)] if bias is not None else None

        def block(j, carry, masked):
            off = _blk_off(j, tk)
            out = []
            for p in range(2):
                kb = pp.k_block(k_refs[p], off, tk)
                vb = v_refs[p][pl.ds(off, tk), :]
                for e in range(2):
                    h = 2 * p + e
                    m, l, acc = carry[h]
                    s = _dot_nt(qm[h], kb) * scale
                    if bias is not None:
                        s = s + fq[h] - ft_ref[h:h + 1, pl.ds(off, tk)]
                    if masked:
                        s = jnp.where(_causal_mask(tq, False), s, NEG)
                    m_new = jnp.maximum(m, jnp.max(s, axis=1, keepdims=True))
                    pr = jnp.exp(s - m_new)
                    a = jnp.exp(m - m_new)
                    out.append((m_new, a * l + jnp.sum(pr, axis=1, keepdims=True),
                                a * acc + _dot(pr.astype(MXU_DTYPE), vb)))
            return tuple(out)

        carry = tuple((jnp.full((tq, 1), NEG, F32), jnp.zeros((tq, 1), F32), jnp.zeros((tq, PAIR), F32))
                      for _ in range(N_HEADS))
        if causal:
            carry = lax.fori_loop(0, i, lambda j, c: block(j, c, False), carry)
            carry = block(i, carry, True)
        else:
            for j in range(nkv):
                carry = block(j, carry, False)
        for p in range(2):
            o_ref[:, p * PAIR:(p + 1) * PAIR] = _take_heads(
                [carry[2 * p + e][2] / carry[2 * p + e][1] for e in range(2)], p)
        lse_ref[...] = _scatter_cols({h: carry[h][0] + jnp.log(carry[h][1]) for h in range(N_HEADS)}, tq)

    in_specs = pp.q_specs + pp.k_specs + pp.v_specs
    args = pp.q_args + pp.k_args + pp.v_args
    if bias is not None:
        in_specs += [_row_spec(tq, 0), pl.BlockSpec((8, sk), lambda i: (0, 0), pipeline_mode=pl.Buffered(1))]
        args += list(bias)
    return pl.pallas_call(
        kern, name=name, grid=(sq // tq,), in_specs=in_specs,
        out_specs=[_row_spec(tq, 0, GROUP_W), _row_spec(tq, 0)],
        out_shape=[jax.ShapeDtypeStruct((sq, GROUP_W), F32), jax.ShapeDtypeStruct((sq, PAIR), F32)],
        compiler_params=_att_params(True),
    )(*args)


def _softmax_bwd(q_parts, k_parts, v_parts, o, lse, do, do_blk, sq, sk, scale, causal, bias, name):
    tq = min(ATT_TILE, sq)
    tk = tq if causal else min(ATT_TILE, sk)
    nkv = sk // tk
    pp = _Parts(q_parts, k_parts, v_parts, tq, sk)
    quad = pp.nparts == 2
    wq = GROUP_W + (PAIR if quad else 0)

    def kern(*refs):
        q_refs, k_refs, v_refs, rest = pp.split(refs)
        if bias is not None:
            o_ref, lse_ref, do_ref, fc_ref, ft_ref, dq_ref, dk_ref, dv_ref, dfq_ref, dfk_ref = rest
            fcb = fc_ref[...]
        else:
            o_ref, lse_ref, do_ref, dq_ref, dk_ref, dv_ref = rest
        i = pl.program_id(0)

        @pl.when(i == 0)
        def _():
            dk_ref[...] = jnp.zeros_like(dk_ref)
            dv_ref[...] = jnp.zeros_like(dv_ref)
            if bias is not None:
                dfk_ref[...] = jnp.zeros_like(dfk_ref)

        lse_b = lse_ref[...]
        qm, dom, delta, lse_h = [], [], [], []
        for p in range(2):
            qm += _masked_heads([r[...] for r in q_refs[p]], pp.kinds[p], p, MXU_DTYPE)
            do_p = do_ref[:, p * PAIR:(p + 1) * PAIR]
            dom += _masked_heads([do_p], ["pair"], p, MXU_DTYPE)
            prod = do_p * o_ref[:, p * PAIR:(p + 1) * PAIR]
            for e in range(2):
                h = 2 * p + e
                delta.append(jnp.sum(jnp.where(_lane_mask("pair", h, tq), prod, 0.0), axis=1, keepdims=True))
                lse_h.append(_col(lse_b, h))
        fq = [_col(fcb, h) for h in range(N_HEADS)] if bias is not None else None

        def block(j, carry, masked):
            off = _blk_off(j, tk)
            out = []
            for p in range(2):
                kb = pp.k_block(k_refs[p], off, tk)
                vb = v_refs[p][pl.ds(off, tk), :]
                dk_acc = jnp.zeros((tk, pp.width), F32)
                dv_acc = jnp.zeros((tk, PAIR), F32)
                for e in range(2):
                    h = 2 * p + e
                    dq, dfq = carry[h]
                    s = _dot_nt(qm[h], kb) * scale
                    if bias is not None:
                        s = s + fq[h] - ft_ref[h:h + 1, pl.ds(off, tk)]
                    if masked:
                        s = jnp.where(_causal_mask(tq, False), s, NEG)
                    pr = jnp.exp(s - lse_h[h])
                    ds = pr * (_dot_nt(dom[h], vb) - delta[h])
                    dsb = (ds * scale).astype(MXU_DTYPE)
                    dv_acc = dv_acc + _dot_tn(pr.astype(MXU_DTYPE), dom[h])
                    dk_acc = dk_acc + _dot_tn(dsb, qm[h])
                    dq = dq + _dot(dsb, kb)
                    if bias is not None:
                        dfq = dfq + jnp.sum(ds, axis=1, keepdims=True)
                        dfk_ref[h:h + 1, pl.ds(off, tk)] -= jnp.sum(ds, axis=0, keepdims=True)
                    out.append((dq, dfq))
                dv_ref[pl.ds(off, tk), p * PAIR:(p + 1) * PAIR] += dv_acc
                dk_ref[pl.ds(off, tk), p * PAIR:(p + 1) * PAIR] += dk_acc[:, :PAIR]
                if quad:
                    dk_ref[pl.ds(off, tk), GROUP_W:] += dk_acc[:, PAIR:]
            return tuple(out)

        carry = tuple((jnp.zeros((tq, pp.width), F32), jnp.zeros((tq, 1), F32)) for _ in range(N_HEADS))
        if causal:
            carry = lax.fori_loop(0, i, lambda j, c: block(j, c, False), carry)
            carry = block(i, carry, True)
        else:
            for j in range(nkv):
                carry = block(j, carry, False)
        for p in range(2):
            dq_ref[:, p * PAIR:(p + 1) * PAIR] = _take_heads([carry[2 * p + e][0][:, :PAIR] for e in range(2)], p)
        if quad:
            dq_ref[:, GROUP_W:] = sum(jnp.where(_lane_mask("quad", h, tq), carry[h][0][:, PAIR:], 0.0)
                                      for h in range(N_HEADS))
        if bias is not None:
            dfq_ref[...] = _scatter_cols({h: carry[h][1] for h in range(N_HEADS)}, tq)

    acc_spec = lambda rows, width: pl.BlockSpec((rows, width), lambda i: (0, 0), pipeline_mode=pl.Buffered(1))
    in_specs = pp.q_specs + pp.k_specs + pp.v_specs + [_row_spec(tq, 0, GROUP_W), _row_spec(tq, 0),
                                                       _row_spec(tq, do_blk, GROUP_W)]
    args = pp.q_args + pp.k_args + pp.v_args + [o, lse, do]
    out_specs = [_row_spec(tq, 0, wq), acc_spec(sk, wq), acc_spec(sk, GROUP_W)]
    out_shape = [jax.ShapeDtypeStruct((sq, wq), F32), jax.ShapeDtypeStruct((sk, wq), F32),
                 jax.ShapeDtypeStruct((sk, GROUP_W), F32)]
    if bias is not None:
        in_specs += [_row_spec(tq, 0), pl.BlockSpec((8, sk), lambda i: (0, 0), pipeline_mode=pl.Buffered(1))]
        args += list(bias)
        out_specs += [_row_spec(tq, 0), acc_spec(8, sk)]
        out_shape += [jax.ShapeDtypeStruct((sq, PAIR), F32), jax.ShapeDtypeStruct((8, sk), F32)]
    return pl.pallas_call(
        kern, name=name, grid=(sq // tq,), in_specs=in_specs, out_specs=out_specs, out_shape=out_shape,
        compiler_params=_att_params(False),
    )(*args)


def _sb_logs(qh, kb, scale, masked, t):
    z = _dot_nt(qh, kb) * scale
    sp = _softplus(z)
    lk = -sp
    valid = _causal_mask(t, True) if masked else None
    if masked:
        lk = jnp.where(valid, lk, 0.0)
    return lk, z - sp, valid


def _tri_sums(xs, tri):
    t = xs[0].shape[0]
    pieces = [_split2(x) for x in xs]
    hi = _dot(jnp.concatenate([pc[0] for pc in pieces], axis=0), tri)
    lo = _dot(jnp.concatenate([pc[1] for pc in pieces], axis=0), tri)
    return [hi[n * t:(n + 1) * t] + lo[n * t:(n + 1) * t] for n in range(len(xs))]


def _sb_fwd(src, q_blk, k_blk, v_blk, s, scale, name):
    t = min(ATT_TILE, s)
    pair = lambda blk: [[(src, blk + p, "pair")] for p in range(2)]
    pp = _Parts(pair(q_blk), pair(k_blk), [(src, v_blk + p) for p in range(2)], t, s)

    def kern(*refs):
        q_refs, k_refs, v_refs, (tri_ref, o_ref, rm_ref) = pp.split(refs)
        i = pl.program_id(0)
        tri = tri_ref[...]
        lane = lax.broadcasted_iota(jnp.int32, (t, PAIR), 1)
        qm = []
        for p in range(2):
            qm += _masked_heads([q_refs[p][0][...]], ["pair"], p, MXU_DTYPE)

        def block(j, carry, masked):
            accs, rights, rm = carry
            off = _blk_off(j, t)
            kbs = [k_refs[p][0][pl.ds(off, t), :] for p in range(2)]
            vbs = [v_refs[p][pl.ds(off, t), :] for p in range(2)]
            logs = [_sb_logs(qm[h], kbs[h // 2], scale, masked, t) for h in range(N_HEADS)]
            tails = _tri_sums([lg[0] for lg in logs], tri)
            new_acc, new_right = [], []
            for h in range(N_HEADS):
                lk, ls, valid = logs[h]
                w = jnp.exp(ls + tails[h] + rights[h])
                if masked:
                    w = jnp.where(valid, w, 0.0)
                new_acc.append(accs[h] + _dot(w.astype(MXU_DTYPE), vbs[h // 2]))
                rm = rm + jnp.where(lane == SB_SLOT * h + j, rights[h], 0.0)
                new_right.append(rights[h] + jnp.sum(lk, axis=1, keepdims=True))
            return tuple(new_acc), tuple(new_right), rm

        carry = (tuple(jnp.zeros((t, PAIR), F32) for _ in range(N_HEADS)),
                 tuple(jnp.zeros((t, 1), F32) for _ in range(N_HEADS)), jnp.zeros((t, PAIR), F32))
        carry = block(i, carry, True)
        carry = lax.fori_loop(0, i, lambda n, c: block(i - 1 - n, c, False), carry)
        for p in range(2):
            o_ref[:, p * PAIR:(p + 1) * PAIR] = _take_heads([carry[0][2 * p + e] for e in range(2)], p)
        rm_ref[...] = carry[2]

    return pl.pallas_call(
        kern, name=name, grid=(s // t,),
        in_specs=pp.q_specs + pp.k_specs + pp.v_specs + [pl.BlockSpec((t, t), lambda i: (0, 0))],
        out_specs=[_row_spec(t, 0, GROUP_W), _row_spec(t, 0)],
        out_shape=[jax.ShapeDtypeStruct((s, GROUP_W), F32), jax.ShapeDtypeStruct((s, PAIR), F32)],
        compiler_params=_att_params(True),
    )(*(pp.q_args + pp.k_args + pp.v_args + [_tri(t, "row_gt_col")]))


def _sb_bwd(src, q_blk, k_blk, v_blk, do, do_blk, rm, s, scale, name):
    t = min(ATT_TILE, s)
    pair = lambda blk: [[(src, blk + p, "pair")] for p in range(2)]
    pp = _Parts(pair(q_blk), pair(k_blk), [(src, v_blk + p) for p in range(2)], t, s)

    def kern(*refs):
        q_refs, k_refs, v_refs, (do_ref, rm_ref, tri_ref, pre_ref, dq_ref, dk_ref, dv_ref) = pp.split(refs)
        i = pl.program_id(0)

        @pl.when(i == 0)
        def _():
            dk_ref[...] = jnp.zeros_like(dk_ref)
            dv_ref[...] = jnp.zeros_like(dv_ref)

        rmb = rm_ref[...]
        tri = tri_ref[...]
        pre = pre_ref[...]
        qm, dom = [], []
        for p in range(2):
            qm += _masked_heads([q_refs[p][0][...]], ["pair"], p, MXU_DTYPE)
            dom += _masked_heads([do_ref[:, p * PAIR:(p + 1) * PAIR]], ["pair"], p, MXU_DTYPE)

        def block(j, carry, masked):
            dqs, lefts = carry
            off = _blk_off(j, t)
            kbs = [k_refs[p][0][pl.ds(off, t), :] for p in range(2)]
            vbs = [v_refs[p][pl.ds(off, t), :] for p in range(2)]
            logs = [_sb_logs(qm[h], kbs[h // 2], scale, masked, t) for h in range(N_HEADS)]
            tails = _tri_sums([lg[0] for lg in logs], tri)
            ws, gs = [], []
            for h in range(N_HEADS):
                lk, ls, valid = logs[h]
                w = jnp.exp(ls + tails[h] + _col(rmb, SB_SLOT * h + j))
                if masked:
                    w = jnp.where(valid, w, 0.0)
                ws.append(w)
                gs.append(_dot_nt(dom[h], vbs[h // 2]) * w)
            prefix = _tri_sums(gs, pre)
            new_dq, new_left = [], []
            dk_acc = [jnp.zeros((t, PAIR), F32) for _ in range(2)]
            dv_acc = [jnp.zeros((t, PAIR), F32) for _ in range(2)]
            for h in range(N_HEADS):
                lk, ls, valid = logs[h]
                sig = jnp.exp(ls)
                dz = gs[h] * (1.0 - sig) - sig * (prefix[h] + lefts[h])
                if masked:
                    dz = jnp.where(valid, dz, 0.0)
                dzb = (dz * scale).astype(MXU_DTYPE)
                dv_acc[h // 2] = dv_acc[h // 2] + _dot_tn(ws[h].astype(MXU_DTYPE), dom[h])
                dk_acc[h // 2] = dk_acc[h // 2] + _dot_tn(dzb, qm[h])
                new_dq.append(dqs[h] + _dot(dzb, kbs[h // 2]))
                new_left.append(lefts[h] + jnp.sum(gs[h], axis=1, keepdims=True))
            for p in range(2):
                dv_ref[pl.ds(off, t), p * PAIR:(p + 1) * PAIR] += dv_acc[p]
                dk_ref[pl.ds(off, t), p * PAIR:(p + 1) * PAIR] += dk_acc[p]
            return tuple(new_dq), tuple(new_left)

        carry = (tuple(jnp.zeros((t, PAIR), F32) for _ in range(N_HEADS)),
                 tuple(jnp.zeros((t, 1), F32) for _ in range(N_HEADS)))
        carry = lax.fori_loop(0, i, lambda j, c: block(j, c, False), carry)
        carry = block(i, carry, True)
        for p in range(2):
            dq_ref[:, p * PAIR:(p + 1) * PAIR] = _take_heads([carry[0][2 * p + e] for e in range(2)], p)

    mspec = pl.BlockSpec((t, t), lambda i: (0, 0))
    acc_spec = pl.BlockSpec((s, GROUP_W), lambda i: (0, 0), pipeline_mode=pl.Buffered(1))
    return pl.pallas_call(
        kern, name=name, grid=(s // t,),
        in_specs=pp.q_specs + pp.k_specs + pp.v_specs + [_row_spec(t, do_blk, GROUP_W), _row_spec(t, 0), mspec, mspec],
        out_specs=[_row_spec(t, 0, GROUP_W), acc_spec, acc_spec],
        out_shape=[jax.ShapeDtypeStruct((s, GROUP_W), F32)] * 3,
        compiler_params=_att_params(False),
    )(*(pp.q_args + pp.k_args + pp.v_args + [do, rm, _tri(t, "row_gt_col"), _tri(t, "row_lt_col")]))


def _split_w_in(w):
    col = lambda n: w[:, _OFF[n]:_OFF[n + 1]]
    wa = jnp.concatenate([col(0), col(1), col(2), col(4), col(5), col(6), col(10)], axis=1)
    misc = jnp.concatenate([col(3), col(9), jnp.zeros((w.shape[0], 128 - 4 - MLA_ROPE), w.dtype)], axis=1)
    wb = jnp.concatenate([col(11), col(7), col(8), misc], axis=1)
    return wa, wb


def _merge_dw_in(dwp):
    a = lambda n: dwp[:, n * GROUP_W:(n + 1) * GROUP_W]
    b0 = PA_COLS
    gate = dwp[:, b0:b0 + 1024]
    cq = dwp[:, b0 + 1024:b0 + 1280]
    ckv = dwp[:, b0 + 1280:b0 + 1408]
    flog = dwp[:, b0 + 1408:b0 + 1412]
    krot = dwp[:, b0 + 1408 + MISC_KROT:b0 + 1408 + MISC_KROT + MLA_ROPE]
    return jnp.concatenate([a(0), a(1), a(2), flog, a(3), a(4), a(5), cq, ckv, krot, a(6), gate], axis=1)


def _heads_first(w, per_head, first):
    r = w.shape[0]
    w3 = w.reshape(r, N_HEADS, per_head)
    return jnp.concatenate([w3[:, :, :first].reshape(r, -1), w3[:, :, first:].reshape(r, -1)], axis=1)


def _heads_interleaved(w, per_head, first):
    r = w.shape[0]
    a = w[:, :N_HEADS * first].reshape(r, N_HEADS, first)
    b = w[:, N_HEADS * first:].reshape(r, N_HEADS, per_head - first)
    return jnp.concatenate([a, b], axis=2).reshape(r, N_HEADS * per_head)


def _pad_rows8(a):
    return a[:, :8].T


def _local_step(x2, mem2, tgt, p):
    s = x2.shape[0]
    nm = mem2.shape[0]
    head_scale = HEAD_DIM ** -0.5
    mla_scale = (HEAD_DIM + MLA_ROPE) ** -0.5
    tabs = _rope_tables(s)
    mats = _rope_matrices()
    pairs = lambda arr, blk: [[(arr, blk + q, "pair")] for q in range(2)]
    vals = lambda arr, blk: [(arr, blk + q) for q in range(2)]

    h, hb = _ln_fwd(x2, None, p["ln_in_g"], p["ln_in_b"], "ln_in_fwd")
    _, memn_b = _ln_fwd(mem2, None, p["mem_ln_g"], p["mem_ln_b"], "ln_mem_fwd")

    saved = []
    for l in range(DEPTH):
        wa, wb = _split_w_in(p["w_in"][l])
        wp = jnp.concatenate([wa, wb], axis=1)
        wq_up = _heads_first(p["w_mla_q_up"][l], HEAD_DIM + MLA_ROPE, HEAD_DIM)
        wkv_up = _heads_first(p["w_mla_kv_up"][l], 2 * HEAD_DIM, HEAD_DIM)
        bias_row = jnp.pad(p["b_forget"][l], (0, 128 - N_HEADS)).reshape(1, 128)
        pa = _matmul(hb, wa, MXU_DTYPE, "proj_a")
        pb = _matmul(hb, wb, F32, "proj_b")

        fc = _forget_fwd(pb, bias_row)
        fbias = (fc, _pad_rows8(fc))
        o_fox, lse_fox = _softmax_fwd(pairs(pa, 0), pairs(pa, 2), vals(pa, 4), s, s, head_scale, True, fbias,
                                      "fox_fwd")
        o_sb, rm_sb = _sb_fwd(pa, 6, 8, 10, s, head_scale, "sb_fwd")

        qfull, cqn = _mla_q_fwd(pb, p["mla_q_norm_g"][l], wq_up, tabs, mats)
        kv, ckvn, kr4 = _mla_kv_fwd(pb, p["mla_kv_norm_g"][l], wkv_up, tabs, mats)
        mla_q = [[(qfull, q, "pair"), (qfull, 2, "quad")] for q in range(2)]
        mla_k = [[(kv, q, "pair"), (kr4, 0, "quad")] for q in range(2)]
        o_mla, lse_mla = _softmax_fwd(mla_q, mla_k, vals(kv, 2), s, s, mla_scale, True, None, "mla_fwd")

        mkv = _matmul(memn_b, p["w_mem_kv"][l], MXU_DTYPE, "mem_kv")
        o_mem, lse_mem = _softmax_fwd(pairs(pa, 12), pairs(mkv, 0), vals(mkv, 2), s, nm, head_scale, False, None,
                                      "mem_fwd")

        groups = (o_fox, o_sb, o_mla, o_mem)
        gated = _gate_fwd(groups, pb)
        y = _matmul(gated, p["w_out"][l], F32, "out_proj")
        saved.append(dict(h=h, hb=hb, y=y, wp=wp, wq_up=wq_up, wkv_up=wkv_up, bias_row=bias_row, pa=pa, pb=pb,
                          fbias=fbias, lse_fox=lse_fox, rm_sb=rm_sb, cqn=cqn, ckvn=ckvn, mla_q=mla_q, mla_k=mla_k,
                          kv=kv, lse_mla=lse_mla, mkv=mkv, lse_mem=lse_mem, groups=groups, gated=gated))
        h, hb = _ln_fwd(h, y, p["ln_g"][l], p["ln_b"][l], "ln_fwd")

    dh, sq_cols = _loss_grad(h, tgt)
    loss_sum = jnp.sum(sq_cols)

    grads = {k: [None] * DEPTH for k in ("w_in", "b_forget", "mla_q_norm_g", "w_mla_q_up", "mla_kv_norm_g",
                                         "w_mla_kv_up", "w_mem_kv", "w_out", "ln_g", "ln_b")}
    dmemn = []
    dy1, dy2, c1 = dh, None, 1.0
    for l in reversed(range(DEPTH)):
        r = saved[l]
        pa, pb = r["pa"], r["pb"]
        o_fox, o_sb, o_mla, o_mem = r["groups"]
        du, du_b, dg, db = _ln_bwd(r["h"], r["y"], p["ln_g"][l], dy1, dy2, c1, "ln_bwd")
        grads["ln_g"][l], grads["ln_b"][l] = dg[0], db[0]
        dgated = _matmul(du_b, p["w_out"][l].T, F32, "out_proj_dx")
        grads["w_out"][l] = _matmul(r["gated"].T, du_b, F32, "out_proj_dw")
        dmixed, dgate_b = _gate_bwd(dgated, r["groups"], pb)

        dfq, dfk, dfv, dfc_q, dfc_k = _softmax_bwd(pairs(pa, 0), pairs(pa, 2), vals(pa, 4), o_fox, r["lse_fox"],
                                                   dmixed, 0, s, s, head_scale, True, r["fbias"], "fox_bwd")
        dmisc_f, dbf = _forget_bwd(pb, r["bias_row"], dfc_q, jnp.pad(dfc_k.T, ((0, 0), (0, 128 - 8))))
        grads["b_forget"][l] = dbf[0, :N_HEADS]

        dsq, dsk, dsv = _sb_bwd(pa, 6, 8, 10, dmixed, 1, r["rm_sb"], s, head_scale, "sb_bwd")

        dqm, dkm, dvm = _softmax_bwd(r["mla_q"], r["mla_k"], vals(r["kv"], 2), o_mla, r["lse_mla"], dmixed, 2,
                                     s, s, mla_scale, True, None, "mla_bwd")
        dq_mla_b = _rope_q_bwd(dqm, tabs, mats)
        dcqn = _matmul(dq_mla_b, r["wq_up"].T, F32, "q_up_dx")
        grads["w_mla_q_up"][l] = _heads_interleaved(_matmul(r["cqn"].T, dq_mla_b, F32, "q_up_dw"),
                                                    HEAD_DIM + MLA_ROPE, HEAD_DIM)
        dcq_b, dgq = _rms_bwd(pb, MLA_Q_RANK, PB_CQ_BLK, p["mla_q_norm_g"][l], dcqn, "rms_q_bwd")
        grads["mla_q_norm_g"][l] = dgq[0]
        dkv_b = jnp.concatenate([dkm[:, :GROUP_W], dvm], axis=1).astype(MXU_DTYPE)
        dckvn = _matmul(dkv_b, r["wkv_up"].T, F32, "kv_up_dx")
        grads["w_mla_kv_up"][l] = _heads_interleaved(_matmul(r["ckvn"].T, dkv_b, F32, "kv_up_dw"),
                                                     2 * HEAD_DIM, HEAD_DIM)
        dckv_b, dgkv = _rms_bwd(pb, MLA_KV_RANK, PB_CKV_BLK, p["mla_kv_norm_g"][l], dckvn, "rms_kv_bwd")
        grads["mla_kv_norm_g"][l] = dgkv[0]
        dmisc_k = _rope_k_bwd(dkm[:, GROUP_W:], tabs, mats)

        dmq, dmk, dmv = _softmax_bwd(pairs(pa, 12), pairs(r["mkv"], 0), vals(r["mkv"], 2), o_mem, r["lse_mem"],
                                     dmixed, 3, s, nm, head_scale, False, None, "mem_bwd")
        dmkv_b = jnp.concatenate([dmk, dmv], axis=1).astype(MXU_DTYPE)
        grads["w_mem_kv"][l] = _matmul(memn_b.T, dmkv_b, F32, "mem_kv_dw")
        dmemn.append(_matmul(dmkv_b, p["w_mem_kv"][l].T, F32, "mem_kv_dx"))

        dmisc = jnp.concatenate([dmisc_f[:, :MISC_KROT], dmisc_k[:, MISC_KROT:MISC_KROT + MLA_ROPE],
                                 jnp.zeros((s, 128 - MISC_KROT - MLA_ROPE), F32)], axis=1)
        dp = jnp.concatenate([d.astype(MXU_DTYPE) for d in (dfq, dfk, dfv, dsq, dsk, dsv, dmq)]
                             + [dgate_b, dcq_b, dckv_b, dmisc.astype(MXU_DTYPE)], axis=1)
        dhproj = _matmul(dp, r["wp"].T, F32, "proj_dx")
        grads["w_in"][l] = _merge_dw_in(_matmul(r["hb"].T, dp, F32, "proj_dw"))
        dy1, dy2, c1 = du, dhproj, ALPHA

    dx, _, dg_in, db_in = _ln_bwd(x2, None, p["ln_in_g"], dy1, dy2, c1, "ln_in_bwd")
    _, _, dg_mem, db_mem = _ln_bwd(mem2, None, p["mem_ln_g"], dmemn[0], dmemn[1], 1.0, "ln_mem_bwd")
    out = {k: jnp.stack(v) for k, v in grads.items()}
    out.update(ln_in_g=dg_in[0], ln_in_b=db_in[0], mem_ln_g=dg_mem[0], mem_ln_b=db_mem[0])
    return loss_sum, dx, out


BIG_NAMES = ("w_in", "w_out", "w_mem_kv", "w_mla_q_up", "w_mla_kv_up")
BIG_AXIS = dict(w_in=2, w_out=1, w_mem_kv=1, w_mla_q_up=2, w_mla_kv_up=2)
SMALL_NAMES = ("ln_in_g", "ln_in_b", "mem_ln_g", "mem_ln_b", "ln_g", "ln_b", "b_forget", "mla_q_norm_g",
               "mla_kv_norm_g")
ALL_NAMES = ("ln_in_g", "ln_in_b", "mem_ln_g", "mem_ln_b", "w_in", "b_forget", "mla_q_norm_g", "w_mla_q_up",
             "mla_kv_norm_g", "w_mla_kv_up", "w_mem_kv", "w_out", "ln_g", "ln_b")
N_CHIPS = 4
N_DEV = 8


def _rows_of(shape):
    rows = -(-int(np.prod(shape)) // LANES)
    return -(-rows // PACK_ALIGN) * PACK_ALIGN


def _pack(arrs, rows):
    parts = []
    for a in arrs:
        f = a.reshape(-1)
        n = _rows_of(a.shape) * LANES
        parts.append(jnp.pad(f, (0, n - f.shape[0])).reshape(-1, LANES))
    used = sum(q.shape[0] for q in parts)
    if rows > used:
        parts.append(jnp.zeros((rows - used, LANES), parts[0].dtype))
    return jnp.concatenate(parts, axis=0)


def _unpack(buf, shapes):
    out, r = [], 0
    for shp in shapes:
        n = _rows_of(shp)
        out.append(buf[r:r + n].reshape(-1)[:int(np.prod(shp))].reshape(shp))
        r += n
    return out


HBM_SPEC = pl.BlockSpec(memory_space=pltpu.HBM)


def _gather_weights(shard):
    rows = shard.shape[0]

    def body(w_ref, out_ref, send_sems, recv_sems, local_sem):
        x, y, c = (lax.axis_index(a) for a in MESH_AXES)
        me = 2 * x + y
        chips = [(1 - x, y), (x, 1 - y), (1 - x, 1 - y)]

        def copy(k, block, to):
            return pltpu.make_async_remote_copy(
                src_ref=w_ref, dst_ref=out_ref.at[block], send_sem=send_sems.at[k], recv_sem=recv_sems.at[k],
                device_id=to, device_id_type=pl.DeviceIdType.MESH)

        mine = pltpu.make_async_copy(w_ref, out_ref.at[me], local_sem)
        mine.start()
        sends = [copy(k, me, (px, py, c)) for k, (px, py) in enumerate(chips)]
        for cp in sends:
            cp.start()
        for k, (px, py) in enumerate(chips):
            copy(k, 2 * px + py, (px, py, c)).wait_recv()
        for cp in sends:
            cp.wait_send()
        mine.wait()

    return pl.pallas_call(
        body, name="gather_weights", out_shape=jax.ShapeDtypeStruct((N_CHIPS, rows, LANES), shard.dtype),
        in_specs=[HBM_SPEC], out_specs=HBM_SPEC,
        scratch_shapes=[pltpu.SemaphoreType.DMA((3,)), pltpu.SemaphoreType.DMA((3,)), pltpu.SemaphoreType.DMA],
    )(shard)


def _exchange_grads(big, small):
    rb, rs = big.shape[1], small.shape[0]

    def body(big_ref, small_ref, obig_ref, osmall_ref, send_sems, recv_sems, local_sems):
        x, y, c = (lax.axis_index(a) for a in MESH_AXES)
        me = 4 * x + 2 * y + c
        flips = [(fx, fy, fc) for fx in (0, 1) for fy in (0, 1) for fc in (0, 1) if fx or fy or fc]
        peers = [(1 - x if fx else x, 1 - y if fy else y, 1 - c if fc else c) for fx, fy, fc in flips]

        def copy_big(k, src_chip, slot, to):
            return pltpu.make_async_remote_copy(
                src_ref=big_ref.at[src_chip], dst_ref=obig_ref.at[slot], send_sem=send_sems.at[k],
                recv_sem=recv_sems.at[k], device_id=to, device_id_type=pl.DeviceIdType.MESH)

        def copy_small(k, slot, to):
            return pltpu.make_async_remote_copy(
                src_ref=small_ref, dst_ref=osmall_ref.at[slot], send_sem=send_sems.at[7 + k],
                recv_sem=recv_sems.at[7 + k], device_id=to, device_id_type=pl.DeviceIdType.MESH)

        own_big = pltpu.make_async_copy(big_ref.at[2 * x + y], obig_ref.at[me], local_sems.at[0])
        own_small = pltpu.make_async_copy(small_ref, osmall_ref.at[me], local_sems.at[1])
        own_big.start()
        own_small.start()
        sends = []
        for k, (px, py, pc) in enumerate(peers):
            sends += [copy_big(k, 2 * px + py, me, (px, py, pc)), copy_small(k, me, (px, py, pc))]
        for cp in sends:
            cp.start()
        for k, (px, py, pc) in enumerate(peers):
            slot = 4 * px + 2 * py + pc
            copy_big(k, 2 * x + y, slot, (px, py, pc)).wait_recv()
            copy_small(k, slot, (px, py, pc)).wait_recv()
        for cp in sends:
            cp.wait_send()
        own_big.wait()
        own_small.wait()

    return pl.pallas_call(
        body, name="exchange_grads",
        out_shape=[jax.ShapeDtypeStruct((N_DEV, rb, LANES), big.dtype),
                   jax.ShapeDtypeStruct((N_DEV, rs, LANES), small.dtype)],
        in_specs=[HBM_SPEC, HBM_SPEC], out_specs=[HBM_SPEC, HBM_SPEC],
        scratch_shapes=[pltpu.SemaphoreType.DMA((14,)), pltpu.SemaphoreType.DMA((14,)),
                        pltpu.SemaphoreType.DMA((2,))],
    )(big, small)


def _adamw(parts, w, m, v, name):
    rows = w.shape[0]
    tile = _pick(rows, (128, 16, 8))
    bc1 = 1.0 - ADAM_B1 ** ADAM_STEP
    bc2 = 1.0 - ADAM_B2 ** ADAM_STEP

    def kern(p_ref, w_ref, m_ref, v_ref, g_ref, d_ref, nm_ref, nv_ref):
        g = p_ref[0]
        for d in range(1, N_DEV):
            g = g + p_ref[d]
        nm = ADAM_B1 * m_ref[...] + (1.0 - ADAM_B1) * g
        nv = ADAM_B2 * v_ref[...] + (1.0 - ADAM_B2) * (g * g)
        g_ref[...] = g
        nm_ref[...] = nm
        nv_ref[...] = nv
        d_ref[...] = -ADAM_LR * ((nm / bc1) / (jnp.sqrt(nv / bc2) + ADAM_EPS) + ADAM_WD * w_ref[...])

    spec = pl.BlockSpec((tile, LANES), lambda i: (i, 0))
    return pl.pallas_call(
        kern, name=name, grid=(rows // tile,),
        in_specs=[pl.BlockSpec((N_DEV, tile, LANES), lambda i: (0, i, 0)), spec, spec, spec],
        out_specs=[spec] * 4, out_shape=[jax.ShapeDtypeStruct((rows, LANES), F32)] * 4,
        compiler_params=pltpu.CompilerParams(dimension_semantics=("parallel",), vmem_limit_bytes=VMEM_LIMIT),
    )(parts, w, m, v)


def kernel(x, mem, ln_in_g, ln_in_b, mem_ln_g, mem_ln_b, w_in, b_forget, mla_q_norm_g, w_mla_q_up, mla_kv_norm_g, w_mla_kv_up, w_mem_kv, w_out, ln_g, ln_b, loss_target, m_ln_in_g, m_ln_in_b, m_mem_ln_g, m_mem_ln_b, m_w_in, m_b_forget, m_mla_q_norm_g, m_w_mla_q_up, m_mla_kv_norm_g, m_w_mla_kv_up, m_w_mem_kv, m_w_out, m_ln_g, m_ln_b, v_ln_in_g, v_ln_in_b, v_mem_ln_g, v_mem_ln_b, v_w_in, v_b_forget, v_mla_q_norm_g, v_w_mla_q_up, v_mla_kv_norm_g, v_w_mla_kv_up, v_w_mem_kv, v_w_out, v_ln_g, v_ln_b):
    w = dict(ln_in_g=ln_in_g, ln_in_b=ln_in_b, mem_ln_g=mem_ln_g, mem_ln_b=mem_ln_b, w_in=w_in, b_forget=b_forget,
             mla_q_norm_g=mla_q_norm_g, w_mla_q_up=w_mla_q_up, mla_kv_norm_g=mla_kv_norm_g,
             w_mla_kv_up=w_mla_kv_up, w_mem_kv=w_mem_kv, w_out=w_out, ln_g=ln_g, ln_b=ln_b)
    mo = dict(ln_in_g=m_ln_in_g, ln_in_b=m_ln_in_b, mem_ln_g=m_mem_ln_g, mem_ln_b=m_mem_ln_b, w_in=m_w_in,
              b_forget=m_b_forget, mla_q_norm_g=m_mla_q_norm_g, w_mla_q_up=m_w_mla_q_up,
              mla_kv_norm_g=m_mla_kv_norm_g, w_mla_kv_up=m_w_mla_kv_up, w_mem_kv=m_w_mem_kv, w_out=m_w_out,
              ln_g=m_ln_g, ln_b=m_ln_b)
    vo = dict(ln_in_g=v_ln_in_g, ln_in_b=v_ln_in_b, mem_ln_g=v_mem_ln_g, mem_ln_b=v_mem_ln_b, w_in=v_w_in,
              b_forget=v_b_forget, mla_q_norm_g=v_mla_q_norm_g, w_mla_q_up=v_w_mla_q_up,
              mla_kv_norm_g=v_mla_kv_norm_g, w_mla_kv_up=v_w_mla_kv_up, w_mem_kv=v_w_mem_kv, w_out=v_w_out,
              ln_g=v_ln_g, ln_b=v_ln_b)
    big_shapes = [w[n].shape for n in BIG_NAMES]
    small_shapes = [w[n].shape for n in SMALL_NAMES]

    gathered = _gather_weights(_pack([w[n].astype(MXU_DTYPE) for n in BIG_NAMES], BIG_ROWS))
    per_chip = [_unpack(gathered[j], big_shapes) for j in range(N_CHIPS)]
    full = dict(w)
    for idx, n in enumerate(BIG_NAMES):
        full[n] = jnp.concatenate([per_chip[j][idx] for j in range(N_CHIPS)], axis=BIG_AXIS[n])

    loss_sum, dx, g = _local_step(x[0], mem[0], loss_target[0], full)
    loss = lax.psum(loss_sum * (0.5 / D_MODEL), MESH_AXES)

    def shard_of(n, j):
        ax, size = BIG_AXIS[n], w[n].shape[BIG_AXIS[n]]
        return lax.slice_in_dim(g[n], j * size, (j + 1) * size, axis=ax)

    big = jnp.stack([_pack([shard_of(n, j) for n in BIG_NAMES], BIG_ROWS) for j in range(N_CHIPS)])
    small = _pack([g[n] for n in SMALL_NAMES], SMALL_ROWS)
    big_parts, small_parts = _exchange_grads(big, small)

    res_big = _adamw(big_parts, _pack([w[n] for n in BIG_NAMES], BIG_ROWS), _pack([mo[n] for n in BIG_NAMES], BIG_ROWS),
                     _pack([vo[n] for n in BIG_NAMES], BIG_ROWS), "adamw_sharded")
    res_small = _adamw(small_parts, _pack([w[n] for n in SMALL_NAMES], SMALL_ROWS),
                       _pack([mo[n] for n in SMALL_NAMES], SMALL_ROWS), _pack([vo[n] for n in SMALL_NAMES], SMALL_ROWS),
                       "adamw_replicated")
    outs = []
    for kind in range(4):
        vals = dict(zip(BIG_NAMES, _unpack(res_big[kind], big_shapes)))
        vals.update(zip(SMALL_NAMES, _unpack(res_small[kind], small_shapes)))
        outs += [vals[n] for n in ALL_NAMES]
    return (loss, dx[None], *outs)
```

```python
import numpy as np
import jax
import jax.numpy as jnp
from jax import lax
from jax.experimental import pallas as pl
from jax.experimental.pallas import tpu as pltpu

F32 = jnp.float32
MXU_DTYPE = jnp.bfloat16

DEPTH = 2
D_MODEL = 1024
GROUP_W = 256
N_HEADS = 4
HEAD_DIM = 64
MLA_Q_RANK = 256
MLA_KV_RANK = 128
MLA_ROPE = 32
MLA_Q_COLS = N_HEADS * (HEAD_DIM + MLA_ROPE)
MLA_KV_COLS = N_HEADS * 2 * HEAD_DIM
ROPE_THETA = 10000.0
LN_EPS = 1e-5
RMS_EPS = 1e-6
ALPHA = (2 * DEPTH) ** 0.25
ADAM_LR, ADAM_B1, ADAM_B2, ADAM_EPS, ADAM_WD, ADAM_STEP = 0.001, 0.9, 0.999, 1e-08, 0.01, 10

_SPLIT = (256, 256, 256, 4, 256, 256, 256, 256, 128, 32, 256, 1024)
_OFF = [int(o) for o in np.cumsum((0,) + _SPLIT)]
IN_COLS = _OFF[-1]
PA_COLS = 7 * GROUP_W
PB_COLS = 1024 + 256 + 128 + 128
PB_CQ_BLK, PB_CKV_BLK, PB_MISC_BLK = 4, 10, 11
MISC_KROT = 4

LANES = 1024
PACK_ALIGN = 16
BIG_ROWS = 2560
SMALL_ROWS = 144
ROW_TILE = 256
ATT_TILE = 256
SB_QUERY_TILE = 512
SOFTMAX_TILE = 512
PAIR = 128
SB_SLOT = PAIR // N_HEADS
VMEM_LIMIT = 56 * 1024 * 1024
NEG = -1e30
MESH_AXES = ("x", "y", "c")


def _dot(a, b):
    return jnp.dot(a, b, preferred_element_type=F32)


def _dot_nt(a, b):
    return lax.dot_general(a, b, (((1,), (1,)), ((), ())), preferred_element_type=F32)


def _dot_tn(a, b):
    return lax.dot_general(a, b, (((0,), (0,)), ((), ())), preferred_element_type=F32)


def _split2(x):
    hi = x.astype(MXU_DTYPE)
    lo = (x - hi.astype(F32)).astype(MXU_DTYPE)
    return hi, lo


def _split3(x):
    hi = x.astype(MXU_DTYPE)
    r = x - hi.astype(F32)
    mid = r.astype(MXU_DTYPE)
    lo = (r - mid.astype(F32)).astype(MXU_DTYPE)
    return hi, mid, lo


def _dot_exact_r(x, pm):
    hi, mid, lo = _split3(x)
    return _dot(hi, pm) + _dot(mid, pm) + _dot(lo, pm)


def _dot_exact_l(pm, x):
    hi, mid, lo = _split3(x)
    return _dot(pm, hi) + _dot(pm, mid) + _dot(pm, lo)


def _pick(dim, prefs):
    for p in prefs:
        if dim % p == 0:
            return p
    return dim


def _softplus(z):
    return jnp.maximum(z, 0.0) + jnp.log(1.0 + jnp.exp(-jnp.abs(z)))


def _matmul(a, b, out_dtype, name):
    m, k = a.shape
    _, n = b.shape
    tm = _pick(m, (512, 256, 128))
    tn = _pick(n, (512, 256, 128))
    tk = k if k <= 4096 else _pick(k, (1024, 512, 256, 128))
    nk = k // tk

    def kern(a_ref, b_ref, o_ref, acc_ref):
        kk = pl.program_id(2)

        @pl.when(kk == 0)
        def _():
            acc_ref[...] = jnp.zeros_like(acc_ref)

        acc_ref[...] += _dot(a_ref[...], b_ref[...])

        @pl.when(kk == nk - 1)
        def _():
            o_ref[...] = acc_ref[...].astype(o_ref.dtype)

    return pl.pallas_call(
        kern, name=name, grid=(m // tm, n // tn, nk),
        in_specs=[pl.BlockSpec((tm, tk), lambda i, j, kk: (i, kk)),
                  pl.BlockSpec((tk, tn), lambda i, j, kk: (kk, j))],
        out_specs=pl.BlockSpec((tm, tn), lambda i, j, kk: (i, j)),
        out_shape=jax.ShapeDtypeStruct((m, n), out_dtype),
        scratch_shapes=[pltpu.VMEM((tm, tn), F32)],
        compiler_params=pltpu.CompilerParams(
            dimension_semantics=("parallel", "parallel", "arbitrary"), vmem_limit_bytes=VMEM_LIMIT),
    )(a.astype(MXU_DTYPE), b.astype(MXU_DTYPE))


def _rowwise(body, name, rows, tile, row_ins, full_ins, row_outs, acc_outs=(), scratch=(),
             reverse=False, sequential=False):
    n = rows // tile

    def ridx(i):
        return (n - 1 - i) if reverse else i

    in_specs, args = [], []
    for arr, width, cb in row_ins:
        in_specs.append(pl.BlockSpec((tile, width), lambda i, cb=cb: (ridx(i), cb)))
        args.append(arr)
    for arr in full_ins:
        in_specs.append(pl.BlockSpec(arr.shape, lambda i, nd=arr.ndim: (0,) * nd))
        args.append(arr)
    out_shape = [jax.ShapeDtypeStruct((rows, w), dt) for w, dt in row_outs]
    out_shape += [jax.ShapeDtypeStruct(s, dt) for s, dt in acc_outs]
    out_specs = [pl.BlockSpec((tile, w), lambda i: (ridx(i), 0)) for w, dt in row_outs]
    out_specs += [pl.BlockSpec(s, lambda i, nd=len(s): (0,) * nd) for s, dt in acc_outs]

    def kern(*refs):
        body(pl.program_id(0), *refs)

    sem = "arbitrary" if (acc_outs or sequential) else "parallel"
    return pl.pallas_call(
        kern, name=name, grid=(n,), in_specs=in_specs, out_specs=out_specs, out_shape=out_shape,
        scratch_shapes=list(scratch),
        compiler_params=pltpu.CompilerParams(dimension_semantics=(sem,), vmem_limit_bytes=VMEM_LIMIT),
    )(*args)


def _ln_stats(u):
    mu = jnp.mean(u, axis=-1, keepdims=True)
    xc = u - mu
    var = jnp.mean(xc * xc, axis=-1, keepdims=True)
    return xc, lax.rsqrt(var + LN_EPS)


def _ln_fwd(a, b, g, beta, name):
    rows, d = a.shape
    has_b = b is not None

    def body(i, *refs):
        if has_b:
            a_ref, b_ref, g_ref, be_ref, h_ref, hb_ref = refs
            u = ALPHA * a_ref[...] + b_ref[...]
        else:
            a_ref, g_ref, be_ref, h_ref, hb_ref = refs
            u = a_ref[...]
        xc, rstd = _ln_stats(u)
        y = xc * rstd * g_ref[...] + be_ref[...]
        h_ref[...] = y
        hb_ref[...] = y.astype(hb_ref.dtype)

    row_ins = [(a, d, 0)] + ([(b, d, 0)] if has_b else [])
    return _rowwise(body, name, rows, min(ROW_TILE, rows), row_ins,
                    [g.reshape(1, d), beta.reshape(1, d)], [(d, F32), (d, MXU_DTYPE)])


def _ln_bwd(a, b, g, dy1, dy2, c1, name):
    rows, d = a.shape
    has_b = b is not None
    has_2 = dy2 is not None

    def body(i, *refs):
        refs = list(refs)
        a_ref = refs.pop(0)
        u = a_ref[...]
        if has_b:
            u = ALPHA * u + refs.pop(0)[...]
        dy = c1 * refs.pop(0)[...]
        if has_2:
            dy = dy + refs.pop(0)[...]
        g_ref, du_ref, dub_ref, dg_ref, db_ref = refs

        @pl.when(i == 0)
        def _():
            dg_ref[...] = jnp.zeros_like(dg_ref)
            db_ref[...] = jnp.zeros_like(db_ref)

        xc, rstd = _ln_stats(u)
        xhat = xc * rstd
        dxh = dy * g_ref[...]
        m1 = jnp.mean(dxh, axis=-1, keepdims=True)
        m2 = jnp.mean(dxh * xhat, axis=-1, keepdims=True)
        du = rstd * (dxh - m1 - xhat * m2)
        du_ref[...] = du
        dub_ref[...] = du.astype(dub_ref.dtype)
        dg_ref[...] += jnp.sum(dy * xhat, axis=0, keepdims=True)
        db_ref[...] += jnp.sum(dy, axis=0, keepdims=True)

    row_ins = [(a, d, 0)] + ([(b, d, 0)] if has_b else []) + [(dy1, d, 0)] + ([(dy2, d, 0)] if has_2 else [])
    return _rowwise(body, name, rows, min(ROW_TILE, rows), row_ins, [g.reshape(1, d)],
                    [(d, F32), (d, MXU_DTYPE)], [((1, d), F32), ((1, d), F32)])


def _loss_grad(h, target):
    rows, d = h.shape

    def body(i, h_ref, t_ref, dh_ref, acc_ref):
        @pl.when(i == 0)
        def _():
            acc_ref[...] = jnp.zeros_like(acc_ref)

        e = h_ref[...] - t_ref[...]
        dh_ref[...] = e * (1.0 / d)
        acc_ref[...] += jnp.sum(e * e, axis=0, keepdims=True)

    return _rowwise(body, "loss_grad", rows, ROW_TILE, [(h, d, 0), (target, d, 0)], [],
                    [(d, F32)], [((1, d), F32)])


def _gate_fwd(groups, pb):
    rows = pb.shape[0]
    w = GROUP_W * len(groups)

    def body(i, *refs):
        g = refs[4][...]
        mixed = jnp.concatenate([r[...] for r in refs[:4]], axis=1)
        refs[5][...] = (mixed * (g / (1.0 + jnp.exp(-g)))).astype(refs[5].dtype)

    return _rowwise(body, "gate_fwd", rows, ROW_TILE, [(o, GROUP_W, 0) for o in groups] + [(pb, w, 0)], [],
                    [(w, MXU_DTYPE)])[0]


def _gate_bwd(dgated, groups, pb):
    rows = pb.shape[0]
    w = GROUP_W * len(groups)

    def body(i, *refs):
        dg = refs[0][...]
        mixed = jnp.concatenate([r[...] for r in refs[1:5]], axis=1)
        g = refs[5][...]
        dm_ref, dgate_ref = refs[6], refs[7]
        sig = 1.0 / (1.0 + jnp.exp(-g))
        dm_ref[...] = dg * (g * sig)
        dgate_ref[...] = (dg * mixed * (sig * (1.0 + g * (1.0 - sig)))).astype(dgate_ref.dtype)

    return _rowwise(body, "gate_bwd", rows, ROW_TILE,
                    [(dgated, w, 0)] + [(o, GROUP_W, 0) for o in groups] + [(pb, w, 0)], [],
                    [(w, F32), (w, MXU_DTYPE)])


def _tri(n, kind):
    r = np.arange(n)[:, None]
    c = np.arange(n)[None, :]
    m = {"lower_incl": r >= c, "upper_incl": r <= c, "row_gt_col": r > c, "row_lt_col": r < c}[kind]
    return jnp.asarray(m.astype(np.float32), dtype=MXU_DTYPE)


def _forget_fwd(pb, bias_row):
    rows = pb.shape[0]
    tile = min(ROW_TILE, rows)

    def body(i, x_ref, b_ref, l_ref, o_ref, carry_ref):
        @pl.when(i == 0)
        def _():
            carry_ref[...] = jnp.zeros_like(carry_ref)

        xx = x_ref[...] + b_ref[...]
        lane = lax.broadcasted_iota(jnp.int32, xx.shape, 1)
        lf = jnp.where(lane < N_HEADS, -_softplus(-xx), 0.0)
        o_ref[...] = _dot_exact_l(l_ref[...], lf) + carry_ref[...]
        carry_ref[...] += jnp.sum(lf, axis=0, keepdims=True)

    return _rowwise(body, "forget_fwd", rows, tile, [(pb, 128, PB_MISC_BLK)],
                    [bias_row, _tri(tile, "lower_incl")], [(128, F32)],
                    scratch=[pltpu.VMEM((1, 128), F32)], sequential=True)[0]


def _forget_bwd(pb, bias_row, dfc):
    rows = pb.shape[0]
    tile = min(ROW_TILE, rows)

    def body(i, x_ref, df_ref, b_ref, u_ref, o_ref, db_ref, carry_ref):
        @pl.when(i == 0)
        def _():
            carry_ref[...] = jnp.zeros_like(carry_ref)
            db_ref[...] = jnp.zeros_like(db_ref)

        df = df_ref[...]
        sfx = _dot_exact_l(u_ref[...], df) + carry_ref[...]
        carry_ref[...] += jnp.sum(df, axis=0, keepdims=True)
        xx = x_ref[...] + b_ref[...]
        lane = lax.broadcasted_iota(jnp.int32, xx.shape, 1)
        dl = jnp.where(lane < N_HEADS, sfx / (1.0 + jnp.exp(xx)), 0.0)
        o_ref[...] = dl
        db_ref[...] += jnp.sum(dl, axis=0, keepdims=True)

    return _rowwise(body, "forget_bwd", rows, tile,
                    [(pb, 128, PB_MISC_BLK), (dfc, 128, 0)],
                    [bias_row, _tri(tile, "upper_incl")], [(128, F32)], [((1, 128), F32)],
                    scratch=[pltpu.VMEM((1, 128), F32)], reverse=True, sequential=True)


def _rope_tables(s):
    half = MLA_ROPE // 2
    inv_freq = ROPE_THETA ** (-jnp.arange(half, dtype=F32) / half)
    ang = jnp.arange(s).astype(F32)[:, None] * inv_freq[None, :]
    cos2 = jnp.tile(jnp.cos(ang), (1, 2))
    sin2 = jnp.tile(jnp.sin(ang), (1, 2))
    cx = jnp.tile(cos2, (1, N_HEADS))
    sx = jnp.tile(sin2, (1, N_HEADS))
    cq = jnp.concatenate([jnp.ones((s, GROUP_W), F32), cx], axis=1)
    sq = jnp.concatenate([jnp.zeros((s, GROUP_W), F32), sx], axis=1)
    pad = ((0, 0), (MISC_KROT, 128 - MISC_KROT - MLA_ROPE))
    ck = jnp.pad(cos2, pad)
    sk = jnp.pad(sin2, pad)
    return dict(cq=cq, sq=sq, ck=ck, sk=sk, cx=cx, sx=sx)


def _rot_matrix(width, bases):
    half = MLA_ROPE // 2
    p = np.zeros((width, width), np.float32)
    for b in bases:
        for i in range(half):
            p[b + half + i, b + i] = -1.0
            p[b + i, b + half + i] = 1.0
    return p


def _rope_matrices():
    pq = _rot_matrix(MLA_Q_COLS, [GROUP_W + h * MLA_ROPE for h in range(N_HEADS)])
    pk = _rot_matrix(128, [MISC_KROT])
    p4 = _rot_matrix(128, [h * MLA_ROPE for h in range(N_HEADS)])
    a = np.zeros((128, 128), np.float32)
    for h in range(N_HEADS):
        for r in range(MLA_ROPE):
            a[h * MLA_ROPE + r, MISC_KROT + r] = 1.0
    cast = lambda m: jnp.asarray(m, dtype=MXU_DTYPE)
    return dict(pq=cast(pq), pqt=cast(pq.T), pk=cast(pk), spread=cast(a.T), xa=cast(a), xb=cast(p4.T @ a))


def _rms(c, g):
    r = lax.rsqrt(jnp.mean(c * c, axis=-1, keepdims=True) + RMS_EPS)
    return c * r * g


def _mla_q_fwd(pb, g, w_up, tabs, mats):
    rows = pb.shape[0]

    def body(i, c_ref, cos_ref, sin_ref, g_ref, w_ref, p_ref, q_ref, cn_ref):
        cn = _rms(c_ref[...], g_ref[...]).astype(cn_ref.dtype)
        cn_ref[...] = cn
        q = _dot(cn, w_ref[...])
        q_ref[...] = (q * cos_ref[...] + _dot_exact_r(q, p_ref[...]) * sin_ref[...]).astype(q_ref.dtype)

    return _rowwise(body, "mla_q_fwd", rows, ROW_TILE,
                    [(pb, MLA_Q_RANK, PB_CQ_BLK), (tabs["cq"], MLA_Q_COLS, 0), (tabs["sq"], MLA_Q_COLS, 0)],
                    [g.reshape(1, MLA_Q_RANK), w_up.astype(MXU_DTYPE), mats["pq"]],
                    [(MLA_Q_COLS, MXU_DTYPE), (MLA_Q_RANK, MXU_DTYPE)])


def _mla_kv_fwd(pb, g, w_up, tabs, mats):
    rows = pb.shape[0]

    def body(i, c_ref, x_ref, cos_ref, sin_ref, g_ref, w_ref, p_ref, sp_ref, kv_ref, cn_ref, kr_ref):
        cn = _rms(c_ref[...], g_ref[...]).astype(cn_ref.dtype)
        cn_ref[...] = cn
        kv_ref[...] = _dot(cn, w_ref[...]).astype(kv_ref.dtype)
        xx = x_ref[...]
        kr = xx * cos_ref[...] + _dot_exact_r(xx, p_ref[...]) * sin_ref[...]
        kr_ref[...] = _dot_exact_r(kr, sp_ref[...]).astype(kr_ref.dtype)

    return _rowwise(body, "mla_kv_fwd", rows, ROW_TILE,
                    [(pb, MLA_KV_RANK, PB_CKV_BLK), (pb, 128, PB_MISC_BLK), (tabs["ck"], 128, 0), (tabs["sk"], 128, 0)],
                    [g.reshape(1, MLA_KV_RANK), w_up.astype(MXU_DTYPE), mats["pk"], mats["spread"]],
                    [(MLA_KV_COLS, MXU_DTYPE), (MLA_KV_RANK, MXU_DTYPE), (128, MXU_DTYPE)])


def _rope_q_bwd(dq_full, tabs, mats):
    rows, nq = dq_full.shape

    def body(i, d_ref, cos_ref, sin_ref, pt_ref, o_ref):
        d = d_ref[...]
        o_ref[...] = (d * cos_ref[...] + _dot_exact_r(d * sin_ref[...], pt_ref[...])).astype(o_ref.dtype)

    return _rowwise(body, "rope_q_bwd", rows, ROW_TILE,
                    [(dq_full, nq, 0), (tabs["cq"], nq, 0), (tabs["sq"], nq, 0)], [mats["pqt"]],
                    [(nq, MXU_DTYPE)])[0]


def _rope_k_bwd(dkr, tabs, mats):
    rows = dkr.shape[0]

    def body(i, d_ref, cos_ref, sin_ref, a_ref, b_ref, o_ref):
        d = d_ref[...]
        o_ref[...] = _dot_exact_r(d * cos_ref[...], a_ref[...]) + _dot_exact_r(d * sin_ref[...], b_ref[...])

    return _rowwise(body, "rope_k_bwd", rows, ROW_TILE,
                    [(dkr, 128, 0), (tabs["cx"], 128, 0), (tabs["sx"], 128, 0)], [mats["xa"], mats["xb"]],
                    [(128, F32)])[0]


def _rms_bwd(pb, width, col_blk, g, dy, name):
    rows = pb.shape[0]

    def body(i, c_ref, dy_ref, g_ref, dc_ref, dg_ref):
        @pl.when(i == 0)
        def _():
            dg_ref[...] = jnp.zeros_like(dg_ref)

        c = c_ref[...]
        dy = dy_ref[...]
        r = lax.rsqrt(jnp.mean(c * c, axis=-1, keepdims=True) + RMS_EPS)
        dyg = dy * g_ref[...]
        dc = r * dyg - c * (r * r * r) * jnp.mean(c * dyg, axis=-1, keepdims=True)
        dc_ref[...] = dc.astype(dc_ref.dtype)
        dg_ref[...] += jnp.sum(dy * c * r, axis=0, keepdims=True)

    return _rowwise(body, name, rows, ROW_TILE, [(pb, width, col_blk), (dy, width, 0)], [g.reshape(1, width)],
                    [(width, MXU_DTYPE)], [((1, width), F32)])


def _att_params(parallel):
    return pltpu.CompilerParams(dimension_semantics=("parallel" if parallel else "arbitrary",),
                                vmem_limit_bytes=VMEM_LIMIT)


def _blk_off(j, t):
    return j * t if isinstance(j, int) else pl.multiple_of(j * t, t)


def _causal_mask(t, strict):
    r = lax.broadcasted_iota(jnp.int32, (t, t), 0)
    c = lax.broadcasted_iota(jnp.int32, (t, t), 1)
    return (c < r) if strict else (c <= r)


def _lane_mask(kind, head, rows):
    lane = lax.broadcasted_iota(jnp.int32, (rows, PAIR), 1)
    if kind == "pair":
        return (lane < HEAD_DIM) if head % 2 == 0 else (lane >= HEAD_DIM)
    return (lane >= MLA_ROPE * head) & (lane < MLA_ROPE * (head + 1))


def _row_spec(t, cb, width=PAIR):
    return pl.BlockSpec((t, width), lambda i, cb=cb: (i, cb))


def _whole_spec(rows, cb, width=PAIR):
    return pl.BlockSpec((rows, width), lambda i, cb=cb: (0, cb), pipeline_mode=pl.Buffered(1))


def _is_pow2(x):
    return float(np.frexp(x)[0]) == 0.5


def _masked_heads(blocks, kinds, pair, dtype, scale=None):
    out = []
    for e in range(2):
        head = 2 * pair + e
        parts = [jnp.where(_lane_mask(k, head, b.shape[0]), b.astype(F32) * (1.0 if scale is None else scale),
                           0.0).astype(dtype)
                 for b, k in zip(blocks, kinds)]
        out.append(parts[0] if len(parts) == 1 else jnp.concatenate(parts, axis=1))
    return out


def _col(block, idx):
    lane = lax.broadcasted_iota(jnp.int32, block.shape, 1)
    return jnp.sum(jnp.where(lane == idx, block, 0.0), axis=1, keepdims=True)


def _scatter_cols(cols, t):
    lane = lax.broadcasted_iota(jnp.int32, (t, PAIR), 1)
    out = jnp.zeros((t, PAIR), F32)
    for idx, c in cols.items():
        out = out + jnp.where(lane == idx, c, 0.0)
    return out


def _take_heads(per_head, pair):
    return jnp.where(_lane_mask("pair", 0, per_head[0].shape[0]), per_head[0], per_head[1])


class _Parts:
    def __init__(self, q_parts, k_parts, v_parts, tq, sk):
        self.kinds = [[kind for _, _, kind in q_parts[p]] for p in range(2)]
        self.nparts = len(q_parts[0])
        self.q_specs = [_row_spec(tq, cb) for p in range(2) for _, cb, _ in q_parts[p]]
        self.q_args = [a for p in range(2) for a, _, _ in q_parts[p]]
        self.k_specs = [_whole_spec(sk, cb) for p in range(2) for _, cb, _ in k_parts[p]]
        self.k_args = [a for p in range(2) for a, _, _ in k_parts[p]]
        self.v_specs = [_whole_spec(sk, cb) for _, cb in v_parts]
        self.v_args = [a for a, _ in v_parts]
        self.width = PAIR * self.nparts

    def split(self, refs):
        n = self.nparts
        refs = list(refs)
        q = [refs[p * n:(p + 1) * n] for p in range(2)]
        k = [refs[2 * n + p * n:2 * n + (p + 1) * n] for p in range(2)]
        v = refs[4 * n:4 * n + 2]
        return q, k, v, refs[4 * n + 2:]

    def k_block(self, k_refs, off, t):
        blks = [r[pl.ds(off, t), :] for r in k_refs]
        return blks[0] if len(blks) == 1 else jnp.concatenate(blks, axis=1)


def _softmax_fwd(q_parts, k_parts, v_parts, sq, sk, scale, causal, bias, name):
    tq = min(SOFTMAX_TILE, sq)
    tk = tq if causal else min(SOFTMAX_TILE, sk)
    nkv = sk // tk
    pp = _Parts(q_parts, k_parts, v_parts, tq, sk)

    def kern(*refs):
        q_refs, k_refs, v_refs, rest = pp.split(refs)
        if bias is not None:
            fc_ref, ft_ref, o_ref, lse_ref = rest
            fcb = fc_ref[...]
        else:
            o_ref, lse_ref = rest
        i = pl.program_id(0)
        fold = _is_pow2(scale)
        head_on = [jnp.where(_lane_mask("pair", e, tk), 1.0, 0.0).astype(MXU_DTYPE) for e in range(2)]
        head_off = [jnp.where(_lane_mask("pair", e, tk), 0.0, 1.0).astype(MXU_DTYPE) for e in range(2)]
        lse_cols = {}
        for p in range(2):
            qm = _masked_heads([r[...] for r in q_refs[p]], pp.kinds[p], p, MXU_DTYPE, scale if fold else None)

            def block(j, carry, masked, p=p, qm=qm):
                off = _blk_off(j, tk)
                kb = pp.k_block(k_refs[p], off, tk)
                vb = v_refs[p][pl.ds(off, tk), :]
                out = []
                for e in range(2):
                    h = 2 * p + e
                    m, acc = carry[e]
                    s = _dot_nt(qm[e], kb)
                    if not fold:
                        s = s * scale
                    if bias is not None:
                        s = s - ft_ref[h:h + 1, pl.ds(off, tk)]
                    if masked:
                        s = jnp.where(_causal_mask(tq, False), s, NEG)
                    m_new = jnp.maximum(m, jnp.max(s, axis=1, keepdims=True))
                    pr = jnp.exp(s - m_new).astype(MXU_DTYPE)
                    out.append((m_new, jnp.exp(m - m_new) * acc + _dot(pr, vb * head_on[e] + head_off[e])))
                return tuple(out)

            carry = tuple((jnp.full((tq, 1), NEG, F32), jnp.zeros((tq, PAIR), F32)) for _ in range(2))
            if causal:
                carry = lax.fori_loop(0, i, lambda j, c, block=block: block(j, c, False), carry)
                carry = block(i, carry, True)
            else:
                for j in range(nkv):
                    carry = block(j, carry, False)
            outs = []
            for e in range(2):
                m, acc = carry[e]
                l = _col(acc, HEAD_DIM * (1 - e))
                outs.append(acc / l)
                lse_cols[2 * p + e] = m + jnp.log(l) + (_col(fcb, 2 * p + e) if bias is not None else 0.0)
            o_ref[:, p * PAIR:(p + 1) * PAIR] = _take_heads(outs, p)
        lse_ref[...] = _scatter_cols(lse_cols, tq)

    in_specs = pp.q_specs + pp.k_specs + pp.v_specs
    args = pp.q_args + pp.k_args + pp.v_args
    if bias is not None:
        in_specs += [_row_spec(tq, 0), pl.BlockSpec((8, sk), lambda i: (0, 0), pipeline_mode=pl.Buffered(1))]
        args += list(bias)
    return pl.pallas_call(
        kern, name=name, grid=(sq // tq,), in_specs=in_specs,
        out_specs=[_row_spec(tq, 0, GROUP_W), _row_spec(tq, 0)],
        out_shape=[jax.ShapeDtypeStruct((sq, GROUP_W), F32), jax.ShapeDtypeStruct((sq, PAIR), F32)],
        compiler_params=_att_params(True),
    )(*args)


def _softmax_bwd(q_parts, k_parts, v_parts, o, lse, do, do_blk, sq, sk, scale, causal, bias, name):
    tq = min(SOFTMAX_TILE, sq)
    tk = tq if causal else min(SOFTMAX_TILE, sk)
    nkv = sk // tk
    pp = _Parts(q_parts, k_parts, v_parts, tq, sk)
    quad = pp.nparts == 2
    wq = GROUP_W + (PAIR if quad else 0)

    def kern(*refs):
        q_refs, k_refs, v_refs, rest = pp.split(refs)
        if bias is not None:
            o_ref, lse_ref, do_ref, fc_ref, ft_ref, dq_ref, dk_ref, dv_ref, dfq_ref, dfk_ref = rest
            fcb = fc_ref[...]
        else:
            o_ref, lse_ref, do_ref, dq_ref, dk_ref, dv_ref = rest
        i = pl.program_id(0)
        fold = _is_pow2(scale)

        @pl.when(i == 0)
        def _():
            dk_ref[...] = jnp.zeros_like(dk_ref)
            dv_ref[...] = jnp.zeros_like(dv_ref)
            if bias is not None:
                dfk_ref[...] = jnp.zeros_like(dfk_ref)

        lse_b = lse_ref[...]
        qm, dom, delta, lse_h = [], [], [], []
        for p in range(2):
            qm += _masked_heads([r[...] for r in q_refs[p]], pp.kinds[p], p, MXU_DTYPE, scale if fold else None)
            do_p = do_ref[:, p * PAIR:(p + 1) * PAIR]
            dom += _masked_heads([do_p], ["pair"], p, MXU_DTYPE)
            prod = do_p * o_ref[:, p * PAIR:(p + 1) * PAIR]
            for e in range(2):
                h = 2 * p + e
                delta.append(jnp.sum(jnp.where(_lane_mask("pair", h, tq), prod, 0.0), axis=1, keepdims=True))
                lse_h.append(_col(lse_b, h) - (_col(fcb, h) if bias is not None else 0.0))

        def block(j, carry, masked):
            off = _blk_off(j, tk)
            out = []
            for p in range(2):
                kb = pp.k_block(k_refs[p], off, tk)
                vb = v_refs[p][pl.ds(off, tk), :]
                dk_acc = jnp.zeros((tk, pp.width), F32)
                dv_acc = jnp.zeros((tk, PAIR), F32)
                for e in range(2):
                    h = 2 * p + e
                    dq, dfq = carry[h]
                    s = _dot_nt(qm[h], kb)
                    if not fold:
                        s = s * scale
                    if bias is not None:
                        s = s - ft_ref[h:h + 1, pl.ds(off, tk)]
                    if masked:
                        s = jnp.where(_causal_mask(tq, False), s, NEG)
                    pr = jnp.exp(s - lse_h[h])
                    ds = pr * (_dot_nt(dom[h], vb) - delta[h])
                    dsb = (ds if fold else ds * scale).astype(MXU_DTYPE)
                    dv_acc = dv_acc + _dot_tn(pr.astype(MXU_DTYPE), dom[h])
                    dk_acc = dk_acc + _dot_tn(dsb, qm[h])
                    dq = dq + _dot(dsb, kb)
                    if bias is not None:
                        dfq = dfq + jnp.sum(ds, axis=1, keepdims=True)
                        dfk_ref[h:h + 1, pl.ds(off, tk)] -= jnp.sum(ds, axis=0, keepdims=True)
                    out.append((dq, dfq))
                dv_ref[pl.ds(off, tk), p * PAIR:(p + 1) * PAIR] += dv_acc
                dk_ref[pl.ds(off, tk), p * PAIR:(p + 1) * PAIR] += dk_acc[:, :PAIR]
                if quad:
                    dk_ref[pl.ds(off, tk), GROUP_W:] += dk_acc[:, PAIR:]
            return tuple(out)

        carry = tuple((jnp.zeros((tq, pp.width), F32), jnp.zeros((tq, 1), F32)) for _ in range(N_HEADS))
        if causal:
            carry = lax.fori_loop(0, i, lambda j, c: block(j, c, False), carry)
            carry = block(i, carry, True)
        else:
            for j in range(nkv):
                carry = block(j, carry, False)
        dqs = [c[0] * scale if fold else c[0] for c in carry]
        for p in range(2):
            dq_ref[:, p * PAIR:(p + 1) * PAIR] = _take_heads([dqs[2 * p + e][:, :PAIR] for e in range(2)], p)
        if quad:
            dq_ref[:, GROUP_W:] = sum(jnp.where(_lane_mask("quad", h, tq), dqs[h][:, PAIR:], 0.0)
                                      for h in range(N_HEADS))
        if bias is not None:
            dfq_ref[...] = _scatter_cols({h: carry[h][1] for h in range(N_HEADS)}, tq)

    acc_spec = lambda rows, width: pl.BlockSpec((rows, width), lambda i: (0, 0), pipeline_mode=pl.Buffered(1))
    in_specs = pp.q_specs + pp.k_specs + pp.v_specs + [_row_spec(tq, 0, GROUP_W), _row_spec(tq, 0),
                                                       _row_spec(tq, do_blk, GROUP_W)]
    args = pp.q_args + pp.k_args + pp.v_args + [o, lse, do]
    out_specs = [_row_spec(tq, 0, wq), acc_spec(sk, wq), acc_spec(sk, GROUP_W)]
    out_shape = [jax.ShapeDtypeStruct((sq, wq), F32), jax.ShapeDtypeStruct((sk, wq), F32),
                 jax.ShapeDtypeStruct((sk, GROUP_W), F32)]
    if bias is not None:
        in_specs += [_row_spec(tq, 0), pl.BlockSpec((8, sk), lambda i: (0, 0), pipeline_mode=pl.Buffered(1))]
        args += list(bias)
        out_specs += [_row_spec(tq, 0), acc_spec(8, sk)]
        out_shape += [jax.ShapeDtypeStruct((sq, PAIR), F32), jax.ShapeDtypeStruct((8, sk), F32)]
    return pl.pallas_call(
        kern, name=name, grid=(sq // tq,), in_specs=in_specs, out_specs=out_specs, out_shape=out_shape,
        compiler_params=_att_params(False),
    )(*args)


def _sb_logs(qh, kb, valid):
    z = _dot_nt(qh, kb)
    sp = _softplus(z)
    lk = -sp
    if valid is not None:
        lk = jnp.where(valid, lk, 0.0)
    return lk, z - sp


def _sb_valid(d, tq, tk):
    r = lax.broadcasted_iota(jnp.int32, (tq, tk), 0)
    c = lax.broadcasted_iota(jnp.int32, (tq, tk), 1)
    return c + d * tk < r


def _tri_sums(xs, tri):
    t = xs[0].shape[0]
    pieces = [_split2(x) for x in xs]
    hi = _dot(jnp.concatenate([pc[0] for pc in pieces], axis=0), tri)
    lo = _dot(jnp.concatenate([pc[1] for pc in pieces], axis=0), tri)
    return [hi[n * t:(n + 1) * t] + lo[n * t:(n + 1) * t] for n in range(len(xs))]


def _sb_fwd(src, q_blk, k_blk, v_blk, s, scale, name):
    assert _is_pow2(scale) and s // ATT_TILE <= SB_SLOT
    tq, t = min(SB_QUERY_TILE, s), min(ATT_TILE, s)
    band = tq // t
    pair = lambda blk: [[(src, blk + p, "pair")] for p in range(2)]
    pp = _Parts(pair(q_blk), pair(k_blk), [(src, v_blk + p) for p in range(2)], tq, s)

    def kern(*refs):
        q_refs, k_refs, v_refs, (tri_ref, o_ref, rm_ref) = pp.split(refs)
        i = pl.program_id(0)
        tri = tri_ref[...]
        lane = lax.broadcasted_iota(jnp.int32, (tq, PAIR), 1)
        qm = []
        for p in range(2):
            qm += _masked_heads([q_refs[p][0][...]], ["pair"], p, MXU_DTYPE, scale)

        def block(j, carry, valid):
            accs, rights, rm = carry
            off = _blk_off(j, t)
            kbs = [k_refs[p][0][pl.ds(off, t), :] for p in range(2)]
            vbs = [v_refs[p][pl.ds(off, t), :] for p in range(2)]
            logs = [_sb_logs(qm[h], kbs[h // 2], valid) for h in range(N_HEADS)]
            tails = _tri_sums([lg[0] for lg in logs], tri)
            new_acc, new_right = [], []
            for h in range(N_HEADS):
                lk, ls = logs[h]
                w = jnp.exp(ls + tails[h] + rights[h])
                if valid is not None:
                    w = jnp.where(valid, w, 0.0)
                new_acc.append(accs[h] + _dot(w.astype(MXU_DTYPE), vbs[h // 2]))
                rm = rm + jnp.where(lane == SB_SLOT * h + j, rights[h], 0.0)
                new_right.append(rights[h] + jnp.sum(lk, axis=1, keepdims=True))
            return tuple(new_acc), tuple(new_right), rm

        carry = (tuple(jnp.zeros((tq, PAIR), F32) for _ in range(N_HEADS)),
                 tuple(jnp.zeros((tq, 1), F32) for _ in range(N_HEADS)), jnp.zeros((tq, PAIR), F32))
        for d in reversed(range(band)):
            carry = block(band * i + d, carry, _sb_valid(d, tq, t))
        carry = lax.fori_loop(0, band * i, lambda n, c: block(band * i - 1 - n, c, None), carry)
        for p in range(2):
            o_ref[:, p * PAIR:(p + 1) * PAIR] = _take_heads([carry[0][2 * p + e] for e in range(2)], p)
        rm_ref[...] = carry[2]

    return pl.pallas_call(
        kern, name=name, grid=(s // tq,),
        in_specs=pp.q_specs + pp.k_specs + pp.v_specs + [pl.BlockSpec((t, t), lambda i: (0, 0))],
        out_specs=[_row_spec(tq, 0, GROUP_W), _row_spec(tq, 0)],
        out_shape=[jax.ShapeDtypeStruct((s, GROUP_W), F32), jax.ShapeDtypeStruct((s, PAIR), F32)],
        compiler_params=_att_params(True),
    )(*(pp.q_args + pp.k_args + pp.v_args + [_tri(t, "row_gt_col")]))


def _sb_bwd(src, q_blk, k_blk, v_blk, do, do_blk, rm, s, scale, name):
    assert _is_pow2(scale)
    tq, t = min(SB_QUERY_TILE, s), min(ATT_TILE, s)
    band = tq // t
    pair = lambda blk: [[(src, blk + p, "pair")] for p in range(2)]
    pp = _Parts(pair(q_blk), pair(k_blk), [(src, v_blk + p) for p in range(2)], tq, s)

    def kern(*refs):
        q_refs, k_refs, v_refs, (do_ref, rm_ref, tri_ref, pre_ref, dq_ref, dk_ref, dv_ref) = pp.split(refs)
        i = pl.program_id(0)

        @pl.when(i == 0)
        def _():
            dk_ref[...] = jnp.zeros_like(dk_ref)
            dv_ref[...] = jnp.zeros_like(dv_ref)

        rmb = rm_ref[...]
        tri = tri_ref[...]
        pre = pre_ref[...]
        qm, dom = [], []
        for p in range(2):
            qm += _masked_heads([q_refs[p][0][...]], ["pair"], p, MXU_DTYPE, scale)
            dom += _masked_heads([do_ref[:, p * PAIR:(p + 1) * PAIR]], ["pair"], p, MXU_DTYPE)

        def block(j, carry, valid):
            dqs, lefts = carry
            off = _blk_off(j, t)
            kbs = [k_refs[p][0][pl.ds(off, t), :] for p in range(2)]
            vbs = [v_refs[p][pl.ds(off, t), :] for p in range(2)]
            logs = [_sb_logs(qm[h], kbs[h // 2], valid) for h in range(N_HEADS)]
            tails = _tri_sums([lg[0] for lg in logs], tri)
            ws, gs = [], []
            for h in range(N_HEADS):
                lk, ls = logs[h]
                w = jnp.exp(ls + tails[h] + _col(rmb, SB_SLOT * h + j))
                if valid is not None:
                    w = jnp.where(valid, w, 0.0)
                ws.append(w)
                gs.append(_dot_nt(dom[h], vbs[h // 2]) * w)
            prefix = _tri_sums(gs, pre)
            new_dq, new_left = [], []
            dk_acc = [jnp.zeros((t, PAIR), F32) for _ in range(2)]
            dv_acc = [jnp.zeros((t, PAIR), F32) for _ in range(2)]
            for h in range(N_HEADS):
                lk, ls = logs[h]
                sig = jnp.exp(ls)
                dz = gs[h] * (1.0 - sig) - sig * (prefix[h] + lefts[h])
                if valid is not None:
                    dz = jnp.where(valid, dz, 0.0)
                dzb = dz.astype(MXU_DTYPE)
                dv_acc[h // 2] = dv_acc[h // 2] + _dot_tn(ws[h].astype(MXU_DTYPE), dom[h])
                dk_acc[h // 2] = dk_acc[h // 2] + _dot_tn(dzb, qm[h])
                new_dq.append(dqs[h] + _dot(dzb, kbs[h // 2]))
                new_left.append(lefts[h] + jnp.sum(gs[h], axis=1, keepdims=True))
            for p in range(2):
                dv_ref[pl.ds(off, t), p * PAIR:(p + 1) * PAIR] += dv_acc[p]
                dk_ref[pl.ds(off, t), p * PAIR:(p + 1) * PAIR] += dk_acc[p]
            return tuple(new_dq), tuple(new_left)

        carry = (tuple(jnp.zeros((tq, PAIR), F32) for _ in range(N_HEADS)),
                 tuple(jnp.zeros((tq, 1), F32) for _ in range(N_HEADS)))
        carry = lax.fori_loop(0, band * i, lambda j, c: block(j, c, None), carry)
        for d in range(band):
            carry = block(band * i + d, carry, _sb_valid(d, tq, t))
        for p in range(2):
            dq_ref[:, p * PAIR:(p + 1) * PAIR] = _take_heads([carry[0][2 * p + e] * scale for e in range(2)], p)

    mspec = pl.BlockSpec((t, t), lambda i: (0, 0))
    acc_spec = pl.BlockSpec((s, GROUP_W), lambda i: (0, 0), pipeline_mode=pl.Buffered(1))
    return pl.pallas_call(
        kern, name=name, grid=(s // tq,),
        in_specs=pp.q_specs + pp.k_specs + pp.v_specs + [_row_spec(tq, do_blk, GROUP_W), _row_spec(tq, 0), mspec, mspec],
        out_specs=[_row_spec(tq, 0, GROUP_W), acc_spec, acc_spec],
        out_shape=[jax.ShapeDtypeStruct((s, GROUP_W), F32)] * 3,
        compiler_params=_att_params(False),
    )(*(pp.q_args + pp.k_args + pp.v_args + [do, rm, _tri(t, "row_gt_col"), _tri(t, "row_lt_col")]))


def _split_w_in(w):
    col = lambda n: w[:, _OFF[n]:_OFF[n + 1]]
    wa = jnp.concatenate([col(0), col(1), col(2), col(4), col(5), col(6), col(10)], axis=1)
    misc = jnp.concatenate([col(3), col(9), jnp.zeros((w.shape[0], 128 - 4 - MLA_ROPE), w.dtype)], axis=1)
    wb = jnp.concatenate([col(11), col(7), col(8), misc], axis=1)
    return wa, wb


def _merge_dw_in(dwp):
    a = lambda n: dwp[:, n * GROUP_W:(n + 1) * GROUP_W]
    b0 = PA_COLS
    gate = dwp[:, b0:b0 + 1024]
    cq = dwp[:, b0 + 1024:b0 + 1280]
    ckv = dwp[:, b0 + 1280:b0 + 1408]
    flog = dwp[:, b0 + 1408:b0 + 1412]
    krot = dwp[:, b0 + 1408 + MISC_KROT:b0 + 1408 + MISC_KROT + MLA_ROPE]
    return jnp.concatenate([a(0), a(1), a(2), flog, a(3), a(4), a(5), cq, ckv, krot, a(6), gate], axis=1)


def _heads_first(w, per_head, first):
    r = w.shape[0]
    w3 = w.reshape(r, N_HEADS, per_head)
    return jnp.concatenate([w3[:, :, :first].reshape(r, -1), w3[:, :, first:].reshape(r, -1)], axis=1)


def _heads_interleaved(w, per_head, first):
    r = w.shape[0]
    a = w[:, :N_HEADS * first].reshape(r, N_HEADS, first)
    b = w[:, N_HEADS * first:].reshape(r, N_HEADS, per_head - first)
    return jnp.concatenate([a, b], axis=2).reshape(r, N_HEADS * per_head)


def _pad_rows8(a):
    return a[:, :8].T


def _local_step(x2, mem2, tgt, p):
    s = x2.shape[0]
    nm = mem2.shape[0]
    head_scale = HEAD_DIM ** -0.5
    mla_scale = (HEAD_DIM + MLA_ROPE) ** -0.5
    tabs = _rope_tables(s)
    mats = _rope_matrices()
    pairs = lambda arr, blk: [[(arr, blk + q, "pair")] for q in range(2)]
    vals = lambda arr, blk: [(arr, blk + q) for q in range(2)]

    h, hb = _ln_fwd(x2, None, p["ln_in_g"], p["ln_in_b"], "ln_in_fwd")
    _, memn_b = _ln_fwd(mem2, None, p["mem_ln_g"], p["mem_ln_b"], "ln_mem_fwd")

    saved = []
    for l in range(DEPTH):
        wa, wb = _split_w_in(p["w_in"][l])
        wp = jnp.concatenate([wa, wb], axis=1)
        wq_up = _heads_first(p["w_mla_q_up"][l], HEAD_DIM + MLA_ROPE, HEAD_DIM)
        wkv_up = _heads_first(p["w_mla_kv_up"][l], 2 * HEAD_DIM, HEAD_DIM)
        bias_row = jnp.pad(p["b_forget"][l], (0, 128 - N_HEADS)).reshape(1, 128)
        pa = _matmul(hb, wa, MXU_DTYPE, "proj_a")
        pb = _matmul(hb, wb, F32, "proj_b")

        fc = _forget_fwd(pb, bias_row)
        fbias = (fc, _pad_rows8(fc))
        o_fox, lse_fox = _softmax_fwd(pairs(pa, 0), pairs(pa, 2), vals(pa, 4), s, s, head_scale, True, fbias,
                                      "fox_fwd")
        o_sb, rm_sb = _sb_fwd(pa, 6, 8, 10, s, head_scale, "sb_fwd")

        qfull, cqn = _mla_q_fwd(pb, p["mla_q_norm_g"][l], wq_up, tabs, mats)
        kv, ckvn, kr4 = _mla_kv_fwd(pb, p["mla_kv_norm_g"][l], wkv_up, tabs, mats)
        mla_q = [[(qfull, q, "pair"), (qfull, 2, "quad")] for q in range(2)]
        mla_k = [[(kv, q, "pair"), (kr4, 0, "quad")] for q in range(2)]
        o_mla, lse_mla = _softmax_fwd(mla_q, mla_k, vals(kv, 2), s, s, mla_scale, True, None, "mla_fwd")

        mkv = _matmul(memn_b, p["w_mem_kv"][l], MXU_DTYPE, "mem_kv")
        o_mem, lse_mem = _softmax_fwd(pairs(pa, 12), pairs(mkv, 0), vals(mkv, 2), s, nm, head_scale, False, None,
                                      "mem_fwd")

        groups = (o_fox, o_sb, o_mla, o_mem)
        gated = _gate_fwd(groups, pb)
        y = _matmul(gated, p["w_out"][l], F32, "out_proj")
        saved.append(dict(h=h, hb=hb, y=y, wp=wp, wq_up=wq_up, wkv_up=wkv_up, bias_row=bias_row, pa=pa, pb=pb,
                          fbias=fbias, lse_fox=lse_fox, rm_sb=rm_sb, cqn=cqn, ckvn=ckvn, mla_q=mla_q, mla_k=mla_k,
                          kv=kv, lse_mla=lse_mla, mkv=mkv, lse_mem=lse_mem, groups=groups, gated=gated))
        h, hb = _ln_fwd(h, y, p["ln_g"][l], p["ln_b"][l], "ln_fwd")

    dh, sq_cols = _loss_grad(h, tgt)
    loss_sum = jnp.sum(sq_cols)

    grads = {k: [None] * DEPTH for k in ("w_in", "b_forget", "mla_q_norm_g", "w_mla_q_up", "mla_kv_norm_g",
                                         "w_mla_kv_up", "w_mem_kv", "w_out", "ln_g", "ln_b")}
    dmemn = []
    dy1, dy2, c1 = dh, None, 1.0
    for l in reversed(range(DEPTH)):
        r = saved[l]
        pa, pb = r["pa"], r["pb"]
        o_fox, o_sb, o_mla, o_mem = r["groups"]
        du, du_b, dg, db = _ln_bwd(r["h"], r["y"], p["ln_g"][l], dy1, dy2, c1, "ln_bwd")
        grads["ln_g"][l], grads["ln_b"][l] = dg[0], db[0]
        dgated = _matmul(du_b, p["w_out"][l].T, F32, "out_proj_dx")
        grads["w_out"][l] = _matmul(r["gated"].T, du_b, F32, "out_proj_dw")
        dmixed, dgate_b = _gate_bwd(dgated, r["groups"], pb)

        dfq, dfk, dfv, dfc_q, dfc_k = _softmax_bwd(pairs(pa, 0), pairs(pa, 2), vals(pa, 4), o_fox, r["lse_fox"],
                                                   dmixed, 0, s, s, head_scale, True, r["fbias"], "fox_bwd")
        dmisc_f, dbf = _forget_bwd(pb, r["bias_row"], dfc_q + jnp.pad(dfc_k.T, ((0, 0), (0, 128 - 8))))
        grads["b_forget"][l] = dbf[0, :N_HEADS]

        dsq, dsk, dsv = _sb_bwd(pa, 6, 8, 10, dmixed, 1, r["rm_sb"], s, head_scale, "sb_bwd")

        dqm, dkm, dvm = _softmax_bwd(r["mla_q"], r["mla_k"], vals(r["kv"], 2), o_mla, r["lse_mla"], dmixed, 2,
                                     s, s, mla_scale, True, None, "mla_bwd")
        dq_mla_b = _rope_q_bwd(dqm, tabs, mats)
        dcqn = _matmul(dq_mla_b, r["wq_up"].T, F32, "q_up_dx")
        grads["w_mla_q_up"][l] = _heads_interleaved(_matmul(r["cqn"].T, dq_mla_b, F32, "q_up_dw"),
                                                    HEAD_DIM + MLA_ROPE, HEAD_DIM)
        dcq_b, dgq = _rms_bwd(pb, MLA_Q_RANK, PB_CQ_BLK, p["mla_q_norm_g"][l], dcqn, "rms_q_bwd")
        grads["mla_q_norm_g"][l] = dgq[0]
        dkv_b = jnp.concatenate([dkm[:, :GROUP_W], dvm], axis=1).astype(MXU_DTYPE)
        dckvn = _matmul(dkv_b, r["wkv_up"].T, F32, "kv_up_dx")
        grads["w_mla_kv_up"][l] = _heads_interleaved(_matmul(r["ckvn"].T, dkv_b, F32, "kv_up_dw"),
                                                     2 * HEAD_DIM, HEAD_DIM)
        dckv_b, dgkv = _rms_bwd(pb, MLA_KV_RANK, PB_CKV_BLK, p["mla_kv_norm_g"][l], dckvn, "rms_kv_bwd")
        grads["mla_kv_norm_g"][l] = dgkv[0]
        dmisc_k = _rope_k_bwd(dkm[:, GROUP_W:], tabs, mats)

        dmq, dmk, dmv = _softmax_bwd(pairs(pa, 12), pairs(r["mkv"], 0), vals(r["mkv"], 2), o_mem, r["lse_mem"],
                                     dmixed, 3, s, nm, head_scale, False, None, "mem_bwd")
        dmkv_b = jnp.concatenate([dmk, dmv], axis=1).astype(MXU_DTYPE)
        grads["w_mem_kv"][l] = _matmul(memn_b.T, dmkv_b, F32, "mem_kv_dw")
        dmemn.append(_matmul(dmkv_b, p["w_mem_kv"][l].T, F32, "mem_kv_dx"))

        dmisc = jnp.concatenate([dmisc_f[:, :MISC_KROT], dmisc_k[:, MISC_KROT:MISC_KROT + MLA_ROPE],
                                 jnp.zeros((s, 128 - MISC_KROT - MLA_ROPE), F32)], axis=1)
        dp = jnp.concatenate([d.astype(MXU_DTYPE) for d in (dfq, dfk, dfv, dsq, dsk, dsv, dmq)]
                             + [dgate_b, dcq_b, dckv_b, dmisc.astype(MXU_DTYPE)], axis=1)
        dhproj = _matmul(dp, r["wp"].T, F32, "proj_dx")
        grads["w_in"][l] = _merge_dw_in(_matmul(r["hb"].T, dp, F32, "proj_dw"))
        dy1, dy2, c1 = du, dhproj, ALPHA

    dx, _, dg_in, db_in = _ln_bwd(x2, None, p["ln_in_g"], dy1, dy2, c1, "ln_in_bwd")
    _, _, dg_mem, db_mem = _ln_bwd(mem2, None, p["mem_ln_g"], dmemn[0], dmemn[1], 1.0, "ln_mem_bwd")
    out = {k: jnp.stack(v) for k, v in grads.items()}
    out.update(ln_in_g=dg_in[0], ln_in_b=db_in[0], mem_ln_g=dg_mem[0], mem_ln_b=db_mem[0])
    return loss_sum, dx, out


BIG_NAMES = ("w_in", "w_out", "w_mem_kv", "w_mla_q_up", "w_mla_kv_up")
BIG_AXIS = dict(w_in=2, w_out=1, w_mem_kv=1, w_mla_q_up=2, w_mla_kv_up=2)
SMALL_NAMES = ("ln_in_g", "ln_in_b", "mem_ln_g", "mem_ln_b", "ln_g", "ln_b", "b_forget", "mla_q_norm_g",
               "mla_kv_norm_g")
ALL_NAMES = ("ln_in_g", "ln_in_b", "mem_ln_g", "mem_ln_b", "w_in", "b_forget", "mla_q_norm_g", "w_mla_q_up",
             "mla_kv_norm_g", "w_mla_kv_up", "w_mem_kv", "w_out", "ln_g", "ln_b")
N_CHIPS = 4
N_DEV = 8


def _rows_of(shape):
    rows = -(-int(np.prod(shape)) // LANES)
    return -(-rows // PACK_ALIGN) * PACK_ALIGN


def _pack(arrs, rows):
    parts = []
    for a in arrs:
        f = a.reshape(-1)
        n = _rows_of(a.shape) * LANES
        parts.append(jnp.pad(f, (0, n - f.shape[0])).reshape(-1, LANES))
    used = sum(q.shape[0] for q in parts)
    if rows > used:
        parts.append(jnp.zeros((rows - used, LANES), parts[0].dtype))
    return jnp.concatenate(parts, axis=0)


def _unpack(buf, shapes):
    out, r = [], 0
    for shp in shapes:
        n = _rows_of(shp)
        out.append(buf[r:r + n].reshape(-1)[:int(np.prod(shp))].reshape(shp))
        r += n
    return out


HBM_SPEC = pl.BlockSpec(memory_space=pltpu.HBM)


def _gather_weights(shard):
    rows = shard.shape[0]

    def body(w_ref, out_ref, send_sems, recv_sems, local_sem):
        x, y, c = (lax.axis_index(a) for a in MESH_AXES)
        me = 2 * x + y
        chips = [(1 - x, y), (x, 1 - y), (1 - x, 1 - y)]

        def copy(k, block, to):
            return pltpu.make_async_remote_copy(
                src_ref=w_ref, dst_ref=out_ref.at[block], send_sem=send_sems.at[k], recv_sem=recv_sems.at[k],
                device_id=to, device_id_type=pl.DeviceIdType.MESH)

        mine = pltpu.make_async_copy(w_ref, out_ref.at[me], local_sem)
        mine.start()
        sends = [copy(k, me, (px, py, c)) for k, (px, py) in enumerate(chips)]
        for cp in sends:
            cp.start()
        for k, (px, py) in enumerate(chips):
            copy(k, 2 * px + py, (px, py, c)).wait_recv()
        for cp in sends:
            cp.wait_send()
        mine.wait()

    return pl.pallas_call(
        body, name="gather_weights", out_shape=jax.ShapeDtypeStruct((N_CHIPS, rows, LANES), shard.dtype),
        in_specs=[HBM_SPEC], out_specs=HBM_SPEC,
        scratch_shapes=[pltpu.SemaphoreType.DMA((3,)), pltpu.SemaphoreType.DMA((3,)), pltpu.SemaphoreType.DMA],
    )(shard)


def _exchange_grads(big, small):
    rb, rs = big.shape[1], small.shape[0]

    def body(big_ref, small_ref, obig_ref, osmall_ref, send_sems, recv_sems, local_sems):
        x, y, c = (lax.axis_index(a) for a in MESH_AXES)
        me = 4 * x + 2 * y + c
        flips = [(fx, fy, fc) for fx in (0, 1) for fy in (0, 1) for fc in (0, 1) if fx or fy or fc]
        peers = [(1 - x if fx else x, 1 - y if fy else y, 1 - c if fc else c) for fx, fy, fc in flips]

        def copy_big(k, src_chip, slot, to):
            return pltpu.make_async_remote_copy(
                src_ref=big_ref.at[src_chip], dst_ref=obig_ref.at[slot], send_sem=send_sems.at[k],
                recv_sem=recv_sems.at[k], device_id=to, device_id_type=pl.DeviceIdType.MESH)

        def copy_small(k, slot, to):
            return pltpu.make_async_remote_copy(
                src_ref=small_ref, dst_ref=osmall_ref.at[slot], send_sem=send_sems.at[7 + k],
                recv_sem=recv_sems.at[7 + k], device_id=to, device_id_type=pl.DeviceIdType.MESH)

        own_big = pltpu.make_async_copy(big_ref.at[2 * x + y], obig_ref.at[me], local_sems.at[0])
        own_small = pltpu.make_async_copy(small_ref, osmall_ref.at[me], local_sems.at[1])
        own_big.start()
        own_small.start()
        sends = []
        for k, (px, py, pc) in enumerate(peers):
            sends += [copy_big(k, 2 * px + py, me, (px, py, pc)), copy_small(k, me, (px, py, pc))]
        for cp in sends:
            cp.start()
        for k, (px, py, pc) in enumerate(peers):
            slot = 4 * px + 2 * py + pc
            copy_big(k, 2 * x + y, slot, (px, py, pc)).wait_recv()
            copy_small(k, slot, (px, py, pc)).wait_recv()
        for cp in sends:
            cp.wait_send()
        own_big.wait()
        own_small.wait()

    return pl.pallas_call(
        body, name="exchange_grads",
        out_shape=[jax.ShapeDtypeStruct((N_DEV, rb, LANES), big.dtype),
                   jax.ShapeDtypeStruct((N_DEV, rs, LANES), small.dtype)],
        in_specs=[HBM_SPEC, HBM_SPEC], out_specs=[HBM_SPEC, HBM_SPEC],
        scratch_shapes=[pltpu.SemaphoreType.DMA((14,)), pltpu.SemaphoreType.DMA((14,)),
                        pltpu.SemaphoreType.DMA((2,))],
    )(big, small)


def _adamw(parts, w, m, v, name):
    rows = w.shape[0]
    tile = _pick(rows, (128, 16, 8))
    bc1 = 1.0 - ADAM_B1 ** ADAM_STEP
    bc2 = 1.0 - ADAM_B2 ** ADAM_STEP

    def kern(p_ref, w_ref, m_ref, v_ref, g_ref, d_ref, nm_ref, nv_ref):
        g = p_ref[0]
        for d in range(1, N_DEV):
            g = g + p_ref[d]
        nm = ADAM_B1 * m_ref[...] + (1.0 - ADAM_B1) * g
        nv = ADAM_B2 * v_ref[...] + (1.0 - ADAM_B2) * (g * g)
        g_ref[...] = g
        nm_ref[...] = nm
        nv_ref[...] = nv
        d_ref[...] = -ADAM_LR * ((nm / bc1) / (jnp.sqrt(nv / bc2) + ADAM_EPS) + ADAM_WD * w_ref[...])

    spec = pl.BlockSpec((tile, LANES), lambda i: (i, 0))
    return pl.pallas_call(
        kern, name=name, grid=(rows // tile,),
        in_specs=[pl.BlockSpec((N_DEV, tile, LANES), lambda i: (0, i, 0)), spec, spec, spec],
        out_specs=[spec] * 4, out_shape=[jax.ShapeDtypeStruct((rows, LANES), F32)] * 4,
        compiler_params=pltpu.CompilerParams(dimension_semantics=("parallel",), vmem_limit_bytes=VMEM_LIMIT),
    )(parts, w, m, v)


def kernel(x, mem, ln_in_g, ln_in_b, mem_ln_g, mem_ln_b, w_in, b_forget, mla_q_norm_g, w_mla_q_up, mla_kv_norm_g, w_mla_kv_up, w_mem_kv, w_out, ln_g, ln_b, loss_target, m_ln_in_g, m_ln_in_b, m_mem_ln_g, m_mem_ln_b, m_w_in, m_b_forget, m_mla_q_norm_g, m_w_mla_q_up, m_mla_kv_norm_g, m_w_mla_kv_up, m_w_mem_kv, m_w_out, m_ln_g, m_ln_b, v_ln_in_g, v_ln_in_b, v_mem_ln_g, v_mem_ln_b, v_w_in, v_b_forget, v_mla_q_norm_g, v_w_mla_q_up, v_mla_kv_norm_g, v_w_mla_kv_up, v_w_mem_kv, v_w_out, v_ln_g, v_ln_b):
    w = dict(ln_in_g=ln_in_g, ln_in_b=ln_in_b, mem_ln_g=mem_ln_g, mem_ln_b=mem_ln_b, w_in=w_in, b_forget=b_forget,
             mla_q_norm_g=mla_q_norm_g, w_mla_q_up=w_mla_q_up, mla_kv_norm_g=mla_kv_norm_g,
             w_mla_kv_up=w_mla_kv_up, w_mem_kv=w_mem_kv, w_out=w_out, ln_g=ln_g, ln_b=ln_b)
    mo = dict(ln_in_g=m_ln_in_g, ln_in_b=m_ln_in_b, mem_ln_g=m_mem_ln_g, mem_ln_b=m_mem_ln_b, w_in=m_w_in,
              b_forget=m_b_forget, mla_q_norm_g=m_mla_q_norm_g, w_mla_q_up=m_w_mla_q_up,
              mla_kv_norm_g=m_mla_kv_norm_g, w_mla_kv_up=m_w_mla_kv_up, w_mem_kv=m_w_mem_kv, w_out=m_w_out,
              ln_g=m_ln_g, ln_b=m_ln_b)
    vo = dict(ln_in_g=v_ln_in_g, ln_in_b=v_ln_in_b, mem_ln_g=v_mem_ln_g, mem_ln_b=v_mem_ln_b, w_in=v_w_in,
              b_forget=v_b_forget, mla_q_norm_g=v_mla_q_norm_g, w_mla_q_up=v_w_mla_q_up,
              mla_kv_norm_g=v_mla_kv_norm_g, w_mla_kv_up=v_w_mla_kv_up, w_mem_kv=v_w_mem_kv, w_out=v_w_out,
              ln_g=v_ln_g, ln_b=v_ln_b)
    big_shapes = [w[n].shape for n in BIG_NAMES]
    small_shapes = [w[n].shape for n in SMALL_NAMES]

    gathered = _gather_weights(_pack([w[n].astype(MXU_DTYPE) for n in BIG_NAMES], BIG_ROWS))
    per_chip = [_unpack(gathered[j], big_shapes) for j in range(N_CHIPS)]
    full = dict(w)
    for idx, n in enumerate(BIG_NAMES):
        full[n] = jnp.concatenate([per_chip[j][idx] for j in range(N_CHIPS)], axis=BIG_AXIS[n])

    loss_sum, dx, g = _local_step(x[0], mem[0], loss_target[0], full)
    loss = lax.psum(loss_sum * (0.5 / D_MODEL), MESH_AXES)

    def shard_of(n, j):
        ax, size = BIG_AXIS[n], w[n].shape[BIG_AXIS[n]]
        return lax.slice_in_dim(g[n], j * size, (j + 1) * size, axis=ax)

    big = jnp.stack([_pack([shard_of(n, j) for n in BIG_NAMES], BIG_ROWS) for j in range(N_CHIPS)])
    small = _pack([g[n] for n in SMALL_NAMES], SMALL_ROWS)
    big_parts, small_parts = _exchange_grads(big, small)

    res_big = _adamw(big_parts, _pack([w[n] for n in BIG_NAMES], BIG_ROWS), _pack([mo[n] for n in BIG_NAMES], BIG_ROWS),
                     _pack([vo[n] for n in BIG_NAMES], BIG_ROWS), "adamw_sharded")
    res_small = _adamw(small_parts, _pack([w[n] for n in SMALL_NAMES], SMALL_ROWS),
                       _pack([mo[n] for n in SMALL_NAMES], SMALL_ROWS), _pack([vo[n] for n in SMALL_NAMES], SMALL_ROWS),
                       "adamw_replicated")
    outs = []
    for kind in range(4):
        vals = dict(zip(BIG_NAMES, _unpack(res_big[kind], big_shapes)))
        vals.update(zip(SMALL_NAMES, _unpack(res_small[kind], small_shapes)))
        outs += [vals[n] for n in ALL_NAMES]
    return (loss, dx[None], *outs)
```

```python
import functools

import numpy as np
import jax
import jax.numpy as jnp
from jax import lax
from jax.experimental import pallas as pl
from jax.experimental.pallas import tpu as pltpu

F32 = jnp.float32
MXU_DTYPE = jnp.bfloat16

DEPTH = 2
D_MODEL = 1024
GROUP_W = 256
N_HEADS = 4
HEAD_DIM = 64
MLA_Q_RANK = 256
MLA_KV_RANK = 128
MLA_ROPE = 32
MLA_Q_COLS = N_HEADS * (HEAD_DIM + MLA_ROPE)
MLA_KV_COLS = N_HEADS * 2 * HEAD_DIM
ROPE_THETA = 10000.0
LN_EPS = 1e-5
RMS_EPS = 1e-6
ALPHA = (2 * DEPTH) ** 0.25
ADAM_LR, ADAM_B1, ADAM_B2, ADAM_EPS, ADAM_WD, ADAM_STEP = 0.001, 0.9, 0.999, 1e-08, 0.01, 10

_SPLIT = (256, 256, 256, 4, 256, 256, 256, 256, 128, 32, 256, 1024)
_OFF = [int(o) for o in np.cumsum((0,) + _SPLIT)]
IN_COLS = _OFF[-1]
PA_COLS = 7 * GROUP_W
PB_COLS = 1024 + 256 + 128 + 128
PB_CQ_BLK, PB_CKV_BLK, PB_MISC_BLK = 4, 10, 11
MISC_KROT = 4

LANES = 1024
PACK_ALIGN = 16
BIG_ROWS = 2560
SMALL_ROWS = 144
ROW_TILE = 256
ATT_TILE = 256
SB_QUERY_TILE = 512
SOFTMAX_TILE = 512
PAIR = 128
SB_SLOT = PAIR // N_HEADS
VMEM_LIMIT = 56 * 1024 * 1024
NEG = -1e30
EXP_UNDERFLOW = -104.0
MESH_AXES = ("x", "y", "c")


def _dot(a, b):
    return jnp.dot(a, b, preferred_element_type=F32)


def _dot_nt(a, b):
    return lax.dot_general(a, b, (((1,), (1,)), ((), ())), preferred_element_type=F32)


def _dot_tn(a, b):
    return lax.dot_general(a, b, (((0,), (0,)), ((), ())), preferred_element_type=F32)


def _split2(x):
    hi = x.astype(MXU_DTYPE)
    lo = (x - hi.astype(F32)).astype(MXU_DTYPE)
    return hi, lo


def _split3(x):
    hi = x.astype(MXU_DTYPE)
    r = x - hi.astype(F32)
    mid = r.astype(MXU_DTYPE)
    lo = (r - mid.astype(F32)).astype(MXU_DTYPE)
    return hi, mid, lo


def _dot_exact_r(x, pm):
    hi, mid, lo = _split3(x)
    return _dot(hi, pm) + _dot(mid, pm) + _dot(lo, pm)


def _dot_exact_l(pm, x):
    hi, mid, lo = _split3(x)
    return _dot(pm, hi) + _dot(pm, mid) + _dot(pm, lo)


def _pick(dim, prefs):
    for p in prefs:
        if dim % p == 0:
            return p
    return dim


def _softplus(z):
    return jnp.maximum(z, 0.0) + jnp.log(1.0 + jnp.exp(-jnp.abs(z)))


def _matmul(a, b, out_dtype, name):
    m, k = a.shape
    _, n = b.shape
    tm = _pick(m, (512, 256, 128))
    tn = _pick(n, (512, 256, 128))
    tk = k if k <= 4096 else _pick(k, (1024, 512, 256, 128))
    nk = k // tk

    def kern(a_ref, b_ref, o_ref, acc_ref):
        kk = pl.program_id(2)

        @pl.when(kk == 0)
        def _():
            acc_ref[...] = jnp.zeros_like(acc_ref)

        acc_ref[...] += _dot(a_ref[...], b_ref[...])

        @pl.when(kk == nk - 1)
        def _():
            o_ref[...] = acc_ref[...].astype(o_ref.dtype)

    return pl.pallas_call(
        kern, name=name, grid=(m // tm, n // tn, nk),
        in_specs=[pl.BlockSpec((tm, tk), lambda i, j, kk: (i, kk)),
                  pl.BlockSpec((tk, tn), lambda i, j, kk: (kk, j))],
        out_specs=pl.BlockSpec((tm, tn), lambda i, j, kk: (i, j)),
        out_shape=jax.ShapeDtypeStruct((m, n), out_dtype),
        scratch_shapes=[pltpu.VMEM((tm, tn), F32)],
        compiler_params=pltpu.CompilerParams(
            dimension_semantics=("parallel", "parallel", "arbitrary"), vmem_limit_bytes=VMEM_LIMIT),
    )(a.astype(MXU_DTYPE), b.astype(MXU_DTYPE))


def _rowwise(body, name, rows, tile, row_ins, full_ins, row_outs, acc_outs=(), scratch=(),
             reverse=False, sequential=False):
    n = rows // tile

    def ridx(i):
        return (n - 1 - i) if reverse else i

    in_specs, args = [], []
    for arr, width, cb in row_ins:
        in_specs.append(pl.BlockSpec((tile, width), lambda i, cb=cb: (ridx(i), cb)))
        args.append(arr)
    for arr in full_ins:
        in_specs.append(pl.BlockSpec(arr.shape, lambda i, nd=arr.ndim: (0,) * nd))
        args.append(arr)
    out_shape = [jax.ShapeDtypeStruct((rows, w), dt) for w, dt in row_outs]
    out_shape += [jax.ShapeDtypeStruct(s, dt) for s, dt in acc_outs]
    out_specs = [pl.BlockSpec((tile, w), lambda i: (ridx(i), 0)) for w, dt in row_outs]
    out_specs += [pl.BlockSpec(s, lambda i, nd=len(s): (0,) * nd) for s, dt in acc_outs]

    def kern(*refs):
        body(pl.program_id(0), *refs)

    sem = "arbitrary" if (acc_outs or sequential) else "parallel"
    return pl.pallas_call(
        kern, name=name, grid=(n,), in_specs=in_specs, out_specs=out_specs, out_shape=out_shape,
        scratch_shapes=list(scratch),
        compiler_params=pltpu.CompilerParams(dimension_semantics=(sem,), vmem_limit_bytes=VMEM_LIMIT),
    )(*args)


def _ln_stats(u):
    mu = jnp.mean(u, axis=-1, keepdims=True)
    xc = u - mu
    var = jnp.mean(xc * xc, axis=-1, keepdims=True)
    return xc, lax.rsqrt(var + LN_EPS)


def _ln_fwd(a, b, g, beta, name):
    rows, d = a.shape
    has_b = b is not None

    def body(i, *refs):
        if has_b:
            a_ref, b_ref, g_ref, be_ref, h_ref, hb_ref = refs
            u = ALPHA * a_ref[...] + b_ref[...]
        else:
            a_ref, g_ref, be_ref, h_ref, hb_ref = refs
            u = a_ref[...]
        xc, rstd = _ln_stats(u)
        y = xc * rstd * g_ref[...] + be_ref[...]
        h_ref[...] = y
        hb_ref[...] = y.astype(hb_ref.dtype)

    row_ins = [(a, d, 0)] + ([(b, d, 0)] if has_b else [])
    return _rowwise(body, name, rows, min(ROW_TILE, rows), row_ins,
                    [g.reshape(1, d), beta.reshape(1, d)], [(d, F32), (d, MXU_DTYPE)])


def _ln_bwd(a, b, g, dy1, dy2, c1, name):
    rows, d = a.shape
    has_b = b is not None
    has_2 = dy2 is not None

    def body(i, *refs):
        refs = list(refs)
        a_ref = refs.pop(0)
        u = a_ref[...]
        if has_b:
            u = ALPHA * u + refs.pop(0)[...]
        dy = c1 * refs.pop(0)[...]
        if has_2:
            dy = dy + refs.pop(0)[...]
        g_ref, du_ref, dub_ref, dg_ref, db_ref = refs

        @pl.when(i == 0)
        def _():
            dg_ref[...] = jnp.zeros_like(dg_ref)
            db_ref[...] = jnp.zeros_like(db_ref)

        xc, rstd = _ln_stats(u)
        xhat = xc * rstd
        dxh = dy * g_ref[...]
        m1 = jnp.mean(dxh, axis=-1, keepdims=True)
        m2 = jnp.mean(dxh * xhat, axis=-1, keepdims=True)
        du = rstd * (dxh - m1 - xhat * m2)
        du_ref[...] = du
        dub_ref[...] = du.astype(dub_ref.dtype)
        dg_ref[...] += jnp.sum(dy * xhat, axis=0, keepdims=True)
        db_ref[...] += jnp.sum(dy, axis=0, keepdims=True)

    row_ins = [(a, d, 0)] + ([(b, d, 0)] if has_b else []) + [(dy1, d, 0)] + ([(dy2, d, 0)] if has_2 else [])
    return _rowwise(body, name, rows, min(ROW_TILE, rows), row_ins, [g.reshape(1, d)],
                    [(d, F32), (d, MXU_DTYPE)], [((1, d), F32), ((1, d), F32)])


def _loss_grad(h, target):
    rows, d = h.shape

    def body(i, h_ref, t_ref, dh_ref, acc_ref):
        @pl.when(i == 0)
        def _():
            acc_ref[...] = jnp.zeros_like(acc_ref)

        e = h_ref[...] - t_ref[...]
        dh_ref[...] = e * (1.0 / d)
        acc_ref[...] += jnp.sum(e * e, axis=0, keepdims=True)

    return _rowwise(body, "loss_grad", rows, ROW_TILE, [(h, d, 0), (target, d, 0)], [],
                    [(d, F32)], [((1, d), F32)])


def _gate_fwd(groups, pb):
    rows = pb.shape[0]
    w = GROUP_W * len(groups)

    def body(i, *refs):
        g = refs[4][...]
        mixed = jnp.concatenate([r[...] for r in refs[:4]], axis=1)
        refs[5][...] = (mixed * (g / (1.0 + jnp.exp(-g)))).astype(refs[5].dtype)

    return _rowwise(body, "gate_fwd", rows, ROW_TILE, [(o, GROUP_W, 0) for o in groups] + [(pb, w, 0)], [],
                    [(w, MXU_DTYPE)])[0]


def _gate_bwd(dgated, groups, pb):
    rows = pb.shape[0]
    w = GROUP_W * len(groups)

    def body(i, *refs):
        dg = refs[0][...]
        mixed = jnp.concatenate([r[...] for r in refs[1:5]], axis=1)
        g = refs[5][...]
        dm_ref, dgate_ref = refs[6], refs[7]
        sig = 1.0 / (1.0 + jnp.exp(-g))
        dm_ref[...] = dg * (g * sig)
        dgate_ref[...] = (dg * mixed * (sig * (1.0 + g * (1.0 - sig)))).astype(dgate_ref.dtype)

    return _rowwise(body, "gate_bwd", rows, ROW_TILE,
                    [(dgated, w, 0)] + [(o, GROUP_W, 0) for o in groups] + [(pb, w, 0)], [],
                    [(w, F32), (w, MXU_DTYPE)])


def _tri(n, kind):
    r = np.arange(n)[:, None]
    c = np.arange(n)[None, :]
    m = {"lower_incl": r >= c, "upper_incl": r <= c, "row_gt_col": r > c, "row_lt_col": r < c}[kind]
    return jnp.asarray(m.astype(np.float32), dtype=MXU_DTYPE)


def _forget_fwd(pb, bias_row):
    rows = pb.shape[0]
    tile = min(ROW_TILE, rows)

    def body(i, x_ref, b_ref, l_ref, o_ref, carry_ref):
        @pl.when(i == 0)
        def _():
            carry_ref[...] = jnp.zeros_like(carry_ref)

        xx = x_ref[...] + b_ref[...]
        lane = lax.broadcasted_iota(jnp.int32, xx.shape, 1)
        lf = jnp.where(lane < N_HEADS, -_softplus(-xx), 0.0)
        o_ref[...] = _dot_exact_l(l_ref[...], lf) + carry_ref[...]
        carry_ref[...] += jnp.sum(lf, axis=0, keepdims=True)

    return _rowwise(body, "forget_fwd", rows, tile, [(pb, 128, PB_MISC_BLK)],
                    [bias_row, _tri(tile, "lower_incl")], [(128, F32)],
                    scratch=[pltpu.VMEM((1, 128), F32)], sequential=True)[0]


def _forget_bwd(pb, bias_row, dfc):
    rows = pb.shape[0]
    tile = min(ROW_TILE, rows)

    def body(i, x_ref, df_ref, b_ref, u_ref, o_ref, db_ref, carry_ref):
        @pl.when(i == 0)
        def _():
            carry_ref[...] = jnp.zeros_like(carry_ref)
            db_ref[...] = jnp.zeros_like(db_ref)

        df = df_ref[...]
        sfx = _dot_exact_l(u_ref[...], df) + carry_ref[...]
        carry_ref[...] += jnp.sum(df, axis=0, keepdims=True)
        xx = x_ref[...] + b_ref[...]
        lane = lax.broadcasted_iota(jnp.int32, xx.shape, 1)
        dl = jnp.where(lane < N_HEADS, sfx / (1.0 + jnp.exp(xx)), 0.0)
        o_ref[...] = dl
        db_ref[...] += jnp.sum(dl, axis=0, keepdims=True)

    return _rowwise(body, "forget_bwd", rows, tile,
                    [(pb, 128, PB_MISC_BLK), (dfc, 128, 0)],
                    [bias_row, _tri(tile, "upper_incl")], [(128, F32)], [((1, 128), F32)],
                    scratch=[pltpu.VMEM((1, 128), F32)], reverse=True, sequential=True)


def _rope_tables(s):
    half = MLA_ROPE // 2
    inv_freq = ROPE_THETA ** (-jnp.arange(half, dtype=F32) / half)
    ang = jnp.arange(s).astype(F32)[:, None] * inv_freq[None, :]
    cos2 = jnp.tile(jnp.cos(ang), (1, 2))
    sin2 = jnp.tile(jnp.sin(ang), (1, 2))
    cx = jnp.tile(cos2, (1, N_HEADS))
    sx = jnp.tile(sin2, (1, N_HEADS))
    cq = jnp.concatenate([jnp.ones((s, GROUP_W), F32), cx], axis=1)
    sq = jnp.concatenate([jnp.zeros((s, GROUP_W), F32), sx], axis=1)
    pad = ((0, 0), (MISC_KROT, 128 - MISC_KROT - MLA_ROPE))
    ck = jnp.pad(cos2, pad)
    sk = jnp.pad(sin2, pad)
    return dict(cq=cq, sq=sq, ck=ck, sk=sk, cx=cx, sx=sx)


def _rot_matrix(width, bases):
    half = MLA_ROPE // 2
    p = np.zeros((width, width), np.float32)
    for b in bases:
        for i in range(half):
            p[b + half + i, b + i] = -1.0
            p[b + i, b + half + i] = 1.0
    return p


def _rope_matrices():
    pq = _rot_matrix(MLA_Q_COLS, [GROUP_W + h * MLA_ROPE for h in range(N_HEADS)])
    pk = _rot_matrix(128, [MISC_KROT])
    p4 = _rot_matrix(128, [h * MLA_ROPE for h in range(N_HEADS)])
    a = np.zeros((128, 128), np.float32)
    for h in range(N_HEADS):
        for r in range(MLA_ROPE):
            a[h * MLA_ROPE + r, MISC_KROT + r] = 1.0
    cast = lambda m: jnp.asarray(m, dtype=MXU_DTYPE)
    return dict(pq=cast(pq), pqt=cast(pq.T), pk=cast(pk), spread=cast(a.T), xa=cast(a), xb=cast(p4.T @ a))


def _rms(c, g):
    r = lax.rsqrt(jnp.mean(c * c, axis=-1, keepdims=True) + RMS_EPS)
    return c * r * g


def _mla_q_fwd(pb, g, w_up, tabs, mats):
    rows = pb.shape[0]

    def body(i, c_ref, cos_ref, sin_ref, g_ref, w_ref, p_ref, q_ref, cn_ref):
        cn = _rms(c_ref[...], g_ref[...]).astype(cn_ref.dtype)
        cn_ref[...] = cn
        q = _dot(cn, w_ref[...])
        q_ref[...] = (q * cos_ref[...] + _dot_exact_r(q, p_ref[...]) * sin_ref[...]).astype(q_ref.dtype)

    return _rowwise(body, "mla_q_fwd", rows, ROW_TILE,
                    [(pb, MLA_Q_RANK, PB_CQ_BLK), (tabs["cq"], MLA_Q_COLS, 0), (tabs["sq"], MLA_Q_COLS, 0)],
                    [g.reshape(1, MLA_Q_RANK), w_up.astype(MXU_DTYPE), mats["pq"]],
                    [(MLA_Q_COLS, MXU_DTYPE), (MLA_Q_RANK, MXU_DTYPE)])


def _mla_kv_fwd(pb, g, w_up, tabs, mats):
    rows = pb.shape[0]

    def body(i, c_ref, x_ref, cos_ref, sin_ref, g_ref, w_ref, p_ref, sp_ref, kv_ref, cn_ref, kr_ref):
        cn = _rms(c_ref[...], g_ref[...]).astype(cn_ref.dtype)
        cn_ref[...] = cn
        kv_ref[...] = _dot(cn, w_ref[...]).astype(kv_ref.dtype)
        xx = x_ref[...]
        kr = xx * cos_ref[...] + _dot_exact_r(xx, p_ref[...]) * sin_ref[...]
        kr_ref[...] = _dot_exact_r(kr, sp_ref[...]).astype(kr_ref.dtype)

    return _rowwise(body, "mla_kv_fwd", rows, ROW_TILE,
                    [(pb, MLA_KV_RANK, PB_CKV_BLK), (pb, 128, PB_MISC_BLK), (tabs["ck"], 128, 0), (tabs["sk"], 128, 0)],
                    [g.reshape(1, MLA_KV_RANK), w_up.astype(MXU_DTYPE), mats["pk"], mats["spread"]],
                    [(MLA_KV_COLS, MXU_DTYPE), (MLA_KV_RANK, MXU_DTYPE), (128, MXU_DTYPE)])


def _rope_q_bwd(dq_full, tabs, mats):
    rows, nq = dq_full.shape

    def body(i, d_ref, cos_ref, sin_ref, pt_ref, o_ref):
        d = d_ref[...]
        o_ref[...] = (d * cos_ref[...] + _dot_exact_r(d * sin_ref[...], pt_ref[...])).astype(o_ref.dtype)

    return _rowwise(body, "rope_q_bwd", rows, ROW_TILE,
                    [(dq_full, nq, 0), (tabs["cq"], nq, 0), (tabs["sq"], nq, 0)], [mats["pqt"]],
                    [(nq, MXU_DTYPE)])[0]


def _rope_k_bwd(dkr, tabs, mats):
    rows = dkr.shape[0]

    def body(i, d_ref, cos_ref, sin_ref, a_ref, b_ref, o_ref):
        d = d_ref[...]
        o_ref[...] = _dot_exact_r(d * cos_ref[...], a_ref[...]) + _dot_exact_r(d * sin_ref[...], b_ref[...])

    return _rowwise(body, "rope_k_bwd", rows, ROW_TILE,
                    [(dkr, 128, 0), (tabs["cx"], 128, 0), (tabs["sx"], 128, 0)], [mats["xa"], mats["xb"]],
                    [(128, F32)])[0]


def _rms_bwd(pb, width, col_blk, g, dy, name):
    rows = pb.shape[0]

    def body(i, c_ref, dy_ref, g_ref, dc_ref, dg_ref):
        @pl.when(i == 0)
        def _():
            dg_ref[...] = jnp.zeros_like(dg_ref)

        c = c_ref[...]
        dy = dy_ref[...]
        r = lax.rsqrt(jnp.mean(c * c, axis=-1, keepdims=True) + RMS_EPS)
        dyg = dy * g_ref[...]
        dc = r * dyg - c * (r * r * r) * jnp.mean(c * dyg, axis=-1, keepdims=True)
        dc_ref[...] = dc.astype(dc_ref.dtype)
        dg_ref[...] += jnp.sum(dy * c * r, axis=0, keepdims=True)

    return _rowwise(body, name, rows, ROW_TILE, [(pb, width, col_blk), (dy, width, 0)], [g.reshape(1, width)],
                    [(width, MXU_DTYPE)], [((1, width), F32)])


def _att_params(parallel):
    return pltpu.CompilerParams(dimension_semantics=("parallel" if parallel else "arbitrary",),
                                vmem_limit_bytes=VMEM_LIMIT)


def _blk_off(j, t):
    return j * t if isinstance(j, int) else pl.multiple_of(j * t, t)


def _causal_mask(t, strict):
    r = lax.broadcasted_iota(jnp.int32, (t, t), 0)
    c = lax.broadcasted_iota(jnp.int32, (t, t), 1)
    return (c < r) if strict else (c <= r)


def _lane_mask(kind, head, rows):
    lane = lax.broadcasted_iota(jnp.int32, (rows, PAIR), 1)
    if kind == "pair":
        return (lane < HEAD_DIM) if head % 2 == 0 else (lane >= HEAD_DIM)
    return (lane >= MLA_ROPE * head) & (lane < MLA_ROPE * (head + 1))


def _row_spec(t, cb, width=PAIR):
    return pl.BlockSpec((t, width), lambda i, cb=cb: (i, cb))


def _whole_spec(rows, cb, width=PAIR):
    return pl.BlockSpec((rows, width), lambda i, cb=cb: (0, cb), pipeline_mode=pl.Buffered(1))


def _is_pow2(x):
    return float(np.frexp(x)[0]) == 0.5


def _masked_heads(blocks, kinds, pair, dtype, scale=None):
    out = []
    for e in range(2):
        head = 2 * pair + e
        parts = [jnp.where(_lane_mask(k, head, b.shape[0]), b.astype(F32) * (1.0 if scale is None else scale),
                           0.0).astype(dtype)
                 for b, k in zip(blocks, kinds)]
        out.append(parts[0] if len(parts) == 1 else jnp.concatenate(parts, axis=1))
    return out


def _col(block, idx):
    lane = lax.broadcasted_iota(jnp.int32, block.shape, 1)
    return jnp.sum(jnp.where(lane == idx, block, 0.0), axis=1, keepdims=True)


def _scatter_cols(cols, t):
    lane = lax.broadcasted_iota(jnp.int32, (t, PAIR), 1)
    out = jnp.zeros((t, PAIR), F32)
    for idx, c in cols.items():
        out = out + jnp.where(lane == idx, c, 0.0)
    return out


def _take_heads(per_head, pair):
    return jnp.where(_lane_mask("pair", 0, per_head[0].shape[0]), per_head[0], per_head[1])


class _Parts:
    def __init__(self, q_parts, k_parts, v_parts, tq, sk):
        self.kinds = [[kind for _, _, kind in q_parts[p]] for p in range(2)]
        self.nparts = len(q_parts[0])
        self.q_specs = [_row_spec(tq, cb) for p in range(2) for _, cb, _ in q_parts[p]]
        self.q_args = [a for p in range(2) for a, _, _ in q_parts[p]]
        self.k_specs = [_whole_spec(sk, cb) for p in range(2) for _, cb, _ in k_parts[p]]
        self.k_args = [a for p in range(2) for a, _, _ in k_parts[p]]
        self.v_specs = [_whole_spec(sk, cb) for _, cb in v_parts]
        self.v_args = [a for a, _ in v_parts]
        self.width = PAIR * self.nparts

    def split(self, refs):
        n = self.nparts
        refs = list(refs)
        q = [refs[p * n:(p + 1) * n] for p in range(2)]
        k = [refs[2 * n + p * n:2 * n + (p + 1) * n] for p in range(2)]
        v = refs[4 * n:4 * n + 2]
        return q, k, v, refs[4 * n + 2:]

    def k_block(self, k_refs, off, t):
        blks = [r[pl.ds(off, t), :] for r in k_refs]
        return blks[0] if len(blks) == 1 else jnp.concatenate(blks, axis=1)


def _softmax_fwd(q_parts, k_parts, v_parts, sq, sk, scale, causal, bias, name):
    tq = min(SOFTMAX_TILE, sq)
    tk = tq if causal else min(SOFTMAX_TILE, sk)
    nkv = sk // tk
    pp = _Parts(q_parts, k_parts, v_parts, tq, sk)

    def kern(*refs):
        q_refs, k_refs, v_refs, rest = pp.split(refs)
        if bias is not None:
            fc_ref, ft_ref, o_ref, lse_ref = rest
            fcb = fc_ref[...]
        else:
            o_ref, lse_ref = rest
        i = pl.program_id(0)
        fold = _is_pow2(scale)
        head_on = [jnp.where(_lane_mask("pair", e, tk), 1.0, 0.0).astype(MXU_DTYPE) for e in range(2)]
        head_off = [jnp.where(_lane_mask("pair", e, tk), 0.0, 1.0).astype(MXU_DTYPE) for e in range(2)]
        lse_cols = {}
        for p in range(2):
            qm = _masked_heads([r[...] for r in q_refs[p]], pp.kinds[p], p, MXU_DTYPE, scale if fold else None)

            def block(j, carry, masked, p=p, qm=qm):
                off = _blk_off(j, tk)
                kb = pp.k_block(k_refs[p], off, tk)
                vb = v_refs[p][pl.ds(off, tk), :]
                out = []
                for e in range(2):
                    h = 2 * p + e
                    m, acc = carry[e]
                    s = _dot_nt(qm[e], kb)
                    if not fold:
                        s = s * scale
                    if bias is not None:
                        s = s - ft_ref[h:h + 1, pl.ds(off, tk)]
                    if masked:
                        s = jnp.where(_causal_mask(tq, False), s, NEG)
                    m_new = jnp.maximum(m, jnp.max(s, axis=1, keepdims=True))
                    pr = jnp.exp(s - m_new).astype(MXU_DTYPE)
                    out.append((m_new, jnp.exp(m - m_new) * acc + _dot(pr, vb * head_on[e] + head_off[e])))
                return tuple(out)

            carry = tuple((jnp.full((tq, 1), NEG, F32), jnp.zeros((tq, PAIR), F32)) for _ in range(2))
            if causal:
                carry = lax.fori_loop(0, i, lambda j, c, block=block: block(j, c, False), carry)
                carry = block(i, carry, True)
            else:
                for j in range(nkv):
                    carry = block(j, carry, False)
            outs = []
            for e in range(2):
                m, acc = carry[e]
                l = _col(acc, HEAD_DIM * (1 - e))
                outs.append(acc / l)
                lse_cols[2 * p + e] = m + jnp.log(l) + (_col(fcb, 2 * p + e) if bias is not None else 0.0)
            o_ref[:, p * PAIR:(p + 1) * PAIR] = _take_heads(outs, p)
        lse_ref[...] = _scatter_cols(lse_cols, tq)

    in_specs = pp.q_specs + pp.k_specs + pp.v_specs
    args = pp.q_args + pp.k_args + pp.v_args
    if bias is not None:
        in_specs += [_row_spec(tq, 0), pl.BlockSpec((8, sk), lambda i: (0, 0), pipeline_mode=pl.Buffered(1))]
        args += list(bias)
    return pl.pallas_call(
        kern, name=name, grid=(sq // tq,), in_specs=in_specs,
        out_specs=[_row_spec(tq, 0, GROUP_W), _row_spec(tq, 0)],
        out_shape=[jax.ShapeDtypeStruct((sq, GROUP_W), F32), jax.ShapeDtypeStruct((sq, PAIR), F32)],
        compiler_params=_att_params(True),
    )(*args)


def _softmax_bwd(q_parts, k_parts, v_parts, o, lse, do, do_blk, sq, sk, scale, causal, bias, name):
    tq = min(SOFTMAX_TILE, sq)
    tk = tq if causal else min(SOFTMAX_TILE, sk)
    nkv = sk // tk
    pp = _Parts(q_parts, k_parts, v_parts, tq, sk)
    quad = pp.nparts == 2
    wq = GROUP_W + (PAIR if quad else 0)

    def kern(*refs):
        q_refs, k_refs, v_refs, rest = pp.split(refs)
        if bias is not None:
            o_ref, lse_ref, do_ref, fc_ref, ft_ref, dq_ref, dk_ref, dv_ref, dfq_ref, dfk_ref = rest
            fcb = fc_ref[...]
        else:
            o_ref, lse_ref, do_ref, dq_ref, dk_ref, dv_ref = rest
        i = pl.program_id(0)
        fold = _is_pow2(scale)

        @pl.when(i == 0)
        def _():
            dk_ref[...] = jnp.zeros_like(dk_ref)
            dv_ref[...] = jnp.zeros_like(dv_ref)
            if bias is not None:
                dfk_ref[...] = jnp.zeros_like(dfk_ref)

        lse_b = lse_ref[...]
        qm, dom, delta, lse_h = [], [], [], []
        for p in range(2):
            qm += _masked_heads([r[...] for r in q_refs[p]], pp.kinds[p], p, MXU_DTYPE, scale if fold else None)
            do_p = do_ref[:, p * PAIR:(p + 1) * PAIR]
            dom += _masked_heads([do_p], ["pair"], p, MXU_DTYPE)
            prod = do_p * o_ref[:, p * PAIR:(p + 1) * PAIR]
            for e in range(2):
                h = 2 * p + e
                delta.append(jnp.sum(jnp.where(_lane_mask("pair", h, tq), prod, 0.0), axis=1, keepdims=True))
                lse_h.append(_col(lse_b, h) - (_col(fcb, h) if bias is not None else 0.0))

        def block(j, carry, masked):
            off = _blk_off(j, tk)
            out = []
            for p in range(2):
                kb = pp.k_block(k_refs[p], off, tk)
                vb = v_refs[p][pl.ds(off, tk), :]
                dk_acc = jnp.zeros((tk, pp.width), F32)
                dv_acc = jnp.zeros((tk, PAIR), F32)
                for e in range(2):
                    h = 2 * p + e
                    dq, dfq = carry[h]
                    s = _dot_nt(qm[h], kb)
                    if not fold:
                        s = s * scale
                    if bias is not None:
                        s = s - ft_ref[h:h + 1, pl.ds(off, tk)]
                    if masked:
                        s = jnp.where(_causal_mask(tq, False), s, NEG)
                    pr = jnp.exp(s - lse_h[h])
                    ds = pr * (_dot_nt(dom[h], vb) - delta[h])
                    dsb = (ds if fold else ds * scale).astype(MXU_DTYPE)
                    dv_acc = dv_acc + _dot_tn(pr.astype(MXU_DTYPE), dom[h])
                    dk_acc = dk_acc + _dot_tn(dsb, qm[h])
                    dq = dq + _dot(dsb, kb)
                    if bias is not None:
                        dfq = dfq + jnp.sum(ds, axis=1, keepdims=True)
                        dfk_ref[h:h + 1, pl.ds(off, tk)] -= jnp.sum(ds, axis=0, keepdims=True)
                    out.append((dq, dfq))
                dv_ref[pl.ds(off, tk), p * PAIR:(p + 1) * PAIR] += dv_acc
                dk_ref[pl.ds(off, tk), p * PAIR:(p + 1) * PAIR] += dk_acc[:, :PAIR]
                if quad:
                    dk_ref[pl.ds(off, tk), GROUP_W:] += dk_acc[:, PAIR:]
            return tuple(out)

        carry = tuple((jnp.zeros((tq, pp.width), F32), jnp.zeros((tq, 1), F32)) for _ in range(N_HEADS))
        if causal:
            carry = lax.fori_loop(0, i, lambda j, c: block(j, c, False), carry)
            carry = block(i, carry, True)
        else:
            for j in range(nkv):
                carry = block(j, carry, False)
        dqs = [c[0] * scale if fold else c[0] for c in carry]
        for p in range(2):
            dq_ref[:, p * PAIR:(p + 1) * PAIR] = _take_heads([dqs[2 * p + e][:, :PAIR] for e in range(2)], p)
        if quad:
            dq_ref[:, GROUP_W:] = sum(jnp.where(_lane_mask("quad", h, tq), dqs[h][:, PAIR:], 0.0)
                                      for h in range(N_HEADS))
        if bias is not None:
            dfq_ref[...] = _scatter_cols({h: carry[h][1] for h in range(N_HEADS)}, tq)

    acc_spec = lambda rows, width: pl.BlockSpec((rows, width), lambda i: (0, 0), pipeline_mode=pl.Buffered(1))
    in_specs = pp.q_specs + pp.k_specs + pp.v_specs + [_row_spec(tq, 0, GROUP_W), _row_spec(tq, 0),
                                                       _row_spec(tq, do_blk, GROUP_W)]
    args = pp.q_args + pp.k_args + pp.v_args + [o, lse, do]
    out_specs = [_row_spec(tq, 0, wq), acc_spec(sk, wq), acc_spec(sk, GROUP_W)]
    out_shape = [jax.ShapeDtypeStruct((sq, wq), F32), jax.ShapeDtypeStruct((sk, wq), F32),
                 jax.ShapeDtypeStruct((sk, GROUP_W), F32)]
    if bias is not None:
        in_specs += [_row_spec(tq, 0), pl.BlockSpec((8, sk), lambda i: (0, 0), pipeline_mode=pl.Buffered(1))]
        args += list(bias)
        out_specs += [_row_spec(tq, 0), acc_spec(8, sk)]
        out_shape += [jax.ShapeDtypeStruct((sq, PAIR), F32), jax.ShapeDtypeStruct((8, sk), F32)]
    return pl.pallas_call(
        kern, name=name, grid=(sq // tq,), in_specs=in_specs, out_specs=out_specs, out_shape=out_shape,
        compiler_params=_att_params(False),
    )(*args)


def _sb_logs(qh, kb, valid):
    z = _dot_nt(qh, kb)
    sp = _softplus(z)
    lk = -sp
    if valid is not None:
        lk = jnp.where(valid, lk, 0.0)
    return lk, z - sp


def _sb_valid(d, tq, tk):
    r = lax.broadcasted_iota(jnp.int32, (tq, tk), 0)
    c = lax.broadcasted_iota(jnp.int32, (tq, tk), 1)
    return c + d * tk < r


def _tri_sums(xs, tri):
    t = xs[0].shape[0]
    pieces = [_split2(x) for x in xs]
    hi = _dot(jnp.concatenate([pc[0] for pc in pieces], axis=0), tri)
    lo = _dot(jnp.concatenate([pc[1] for pc in pieces], axis=0), tri)
    return [hi[n * t:(n + 1) * t] + lo[n * t:(n + 1) * t] for n in range(len(xs))]


def _sb_fwd(src, q_blk, k_blk, v_blk, s, scale, name):
    assert _is_pow2(scale) and s // ATT_TILE <= SB_SLOT
    tq, t = min(SB_QUERY_TILE, s), min(ATT_TILE, s)
    band = tq // t
    pair = lambda blk: [[(src, blk + p, "pair")] for p in range(2)]
    pp = _Parts(pair(q_blk), pair(k_blk), [(src, v_blk + p) for p in range(2)], tq, s)

    def kern(*refs):
        q_refs, k_refs, v_refs, (tri_ref, o_ref, rm_ref, cnt_ref) = pp.split(refs)
        i = pl.program_id(0)
        tri = tri_ref[...]
        lane = lax.broadcasted_iota(jnp.int32, (tq, PAIR), 1)
        qm = []
        for p in range(2):
            qm += _masked_heads([q_refs[p][0][...]], ["pair"], p, MXU_DTYPE, scale)

        def block(j, carry, valid):
            accs, rights, rm = carry
            off = _blk_off(j, t)
            kbs = [k_refs[p][0][pl.ds(off, t), :] for p in range(2)]
            vbs = [v_refs[p][pl.ds(off, t), :] for p in range(2)]
            logs = [_sb_logs(qm[h], kbs[h // 2], valid) for h in range(N_HEADS)]
            tails = _tri_sums([lg[0] for lg in logs], tri)
            new_acc, new_right = [], []
            for h in range(N_HEADS):
                lk, ls = logs[h]
                w = jnp.exp(ls + tails[h] + rights[h])
                if valid is not None:
                    w = jnp.where(valid, w, 0.0)
                new_acc.append(accs[h] + _dot(w.astype(MXU_DTYPE), vbs[h // 2]))
                rm = rm + jnp.where(lane == SB_SLOT * h + j, rights[h], 0.0)
                new_right.append(rights[h] + jnp.sum(lk, axis=1, keepdims=True))
            return tuple(new_acc), tuple(new_right), rm

        carry = (tuple(jnp.zeros((tq, PAIR), F32) for _ in range(N_HEADS)),
                 tuple(jnp.zeros((tq, 1), F32) for _ in range(N_HEADS)), jnp.zeros((tq, PAIR), F32))
        for d in reversed(range(band)):
            carry = block(band * i + d, carry, _sb_valid(d, tq, t))

        def alive(c):
            return functools.reduce(jnp.maximum, [jnp.max(r) for r in c[1]])

        def step(state):
            n, _, c = state
            c = block(band * i - 1 - n, c, None)
            return n + 1, alive(c), c

        n_done, _, carry = lax.while_loop(lambda st: jnp.logical_and(st[0] < band * i, st[1] > EXP_UNDERFLOW),
                                          step, (jnp.int32(0), alive(carry), carry))
        cnt_ref[i] = n_done
        for p in range(2):
            o_ref[:, p * PAIR:(p + 1) * PAIR] = _take_heads([carry[0][2 * p + e] for e in range(2)], p)
        rm_ref[...] = carry[2]

    return pl.pallas_call(
        kern, name=name, grid=(s // tq,),
        in_specs=pp.q_specs + pp.k_specs + pp.v_specs + [pl.BlockSpec((t, t), lambda i: (0, 0))],
        out_specs=[_row_spec(tq, 0, GROUP_W), _row_spec(tq, 0), pl.BlockSpec(memory_space=pltpu.SMEM)],
        out_shape=[jax.ShapeDtypeStruct((s, GROUP_W), F32), jax.ShapeDtypeStruct((s, PAIR), F32),
                   jax.ShapeDtypeStruct((s // tq,), jnp.int32)],
        compiler_params=_att_params(False),
    )(*(pp.q_args + pp.k_args + pp.v_args + [_tri(t, "row_gt_col")]))


def _sb_bwd(src, q_blk, k_blk, v_blk, do, do_blk, rm, visited, s, scale, name):
    assert _is_pow2(scale)
    tq, t = min(SB_QUERY_TILE, s), min(ATT_TILE, s)
    band = tq // t
    pair = lambda blk: [[(src, blk + p, "pair")] for p in range(2)]
    pp = _Parts(pair(q_blk), pair(k_blk), [(src, v_blk + p) for p in range(2)], tq, s)

    def kern(*refs):
        q_refs, k_refs, v_refs, (do_ref, rm_ref, tri_ref, pre_ref, cnt_ref, dq_ref, dk_ref, dv_ref) = pp.split(refs)
        i = pl.program_id(0)

        @pl.when(i == 0)
        def _():
            dk_ref[...] = jnp.zeros_like(dk_ref)
            dv_ref[...] = jnp.zeros_like(dv_ref)

        rmb = rm_ref[...]
        tri = tri_ref[...]
        pre = pre_ref[...]
        qm, dom = [], []
        for p in range(2):
            qm += _masked_heads([q_refs[p][0][...]], ["pair"], p, MXU_DTYPE, scale)
            dom += _masked_heads([do_ref[:, p * PAIR:(p + 1) * PAIR]], ["pair"], p, MXU_DTYPE)

        def block(j, carry, valid):
            dqs, lefts = carry
            off = _blk_off(j, t)
            kbs = [k_refs[p][0][pl.ds(off, t), :] for p in range(2)]
            vbs = [v_refs[p][pl.ds(off, t), :] for p in range(2)]
            logs = [_sb_logs(qm[h], kbs[h // 2], valid) for h in range(N_HEADS)]
            tails = _tri_sums([lg[0] for lg in logs], tri)
            ws, gs = [], []
            for h in range(N_HEADS):
                lk, ls = logs[h]
                w = jnp.exp(ls + tails[h] + _col(rmb, SB_SLOT * h + j))
                if valid is not None:
                    w = jnp.where(valid, w, 0.0)
                ws.append(w)
                gs.append(_dot_nt(dom[h], vbs[h // 2]) * w)
            prefix = _tri_sums(gs, pre)
            new_dq, new_left = [], []
            dk_acc = [jnp.zeros((t, PAIR), F32) for _ in range(2)]
            dv_acc = [jnp.zeros((t, PAIR), F32) for _ in range(2)]
            for h in range(N_HEADS):
                lk, ls = logs[h]
                sig = jnp.exp(ls)
                dz = gs[h] * (1.0 - sig) - sig * (prefix[h] + lefts[h])
                if valid is not None:
                    dz = jnp.where(valid, dz, 0.0)
                dzb = dz.astype(MXU_DTYPE)
                dv_acc[h // 2] = dv_acc[h // 2] + _dot_tn(ws[h].astype(MXU_DTYPE), dom[h])
                dk_acc[h // 2] = dk_acc[h // 2] + _dot_tn(dzb, qm[h])
                new_dq.append(dqs[h] + _dot(dzb, kbs[h // 2]))
                new_left.append(lefts[h] + jnp.sum(gs[h], axis=1, keepdims=True))
            for p in range(2):
                dv_ref[pl.ds(off, t), p * PAIR:(p + 1) * PAIR] += dv_acc[p]
                dk_ref[pl.ds(off, t), p * PAIR:(p + 1) * PAIR] += dk_acc[p]
            return tuple(new_dq), tuple(new_left)

        carry = (tuple(jnp.zeros((tq, PAIR), F32) for _ in range(N_HEADS)),
                 tuple(jnp.zeros((tq, 1), F32) for _ in range(N_HEADS)))
        carry = lax.fori_loop(band * i - cnt_ref[i], band * i, lambda j, c: block(j, c, None), carry)
        for d in range(band):
            carry = block(band * i + d, carry, _sb_valid(d, tq, t))
        for p in range(2):
            dq_ref[:, p * PAIR:(p + 1) * PAIR] = _take_heads([carry[0][2 * p + e] * scale for e in range(2)], p)

    mspec = pl.BlockSpec((t, t), lambda i: (0, 0))
    acc_spec = pl.BlockSpec((s, GROUP_W), lambda i: (0, 0), pipeline_mode=pl.Buffered(1))
    return pl.pallas_call(
        kern, name=name, grid=(s // tq,),
        in_specs=pp.q_specs + pp.k_specs + pp.v_specs + [_row_spec(tq, do_blk, GROUP_W), _row_spec(tq, 0), mspec, mspec,
                                                         pl.BlockSpec(memory_space=pltpu.SMEM)],
        out_specs=[_row_spec(tq, 0, GROUP_W), acc_spec, acc_spec],
        out_shape=[jax.ShapeDtypeStruct((s, GROUP_W), F32)] * 3,
        compiler_params=_att_params(False),
    )(*(pp.q_args + pp.k_args + pp.v_args + [do, rm, _tri(t, "row_gt_col"), _tri(t, "row_lt_col"), visited]))


def _split_w_in(w):
    col = lambda n: w[:, _OFF[n]:_OFF[n + 1]]
    wa = jnp.concatenate([col(0), col(1), col(2), col(4), col(5), col(6), col(10)], axis=1)
    misc = jnp.concatenate([col(3), col(9), jnp.zeros((w.shape[0], 128 - 4 - MLA_ROPE), w.dtype)], axis=1)
    wb = jnp.concatenate([col(11), col(7), col(8), misc], axis=1)
    return wa, wb


def _merge_dw_in(dwp):
    a = lambda n: dwp[:, n * GROUP_W:(n + 1) * GROUP_W]
    b0 = PA_COLS
    gate = dwp[:, b0:b0 + 1024]
    cq = dwp[:, b0 + 1024:b0 + 1280]
    ckv = dwp[:, b0 + 1280:b0 + 1408]
    flog = dwp[:, b0 + 1408:b0 + 1412]
    krot = dwp[:, b0 + 1408 + MISC_KROT:b0 + 1408 + MISC_KROT + MLA_ROPE]
    return jnp.concatenate([a(0), a(1), a(2), flog, a(3), a(4), a(5), cq, ckv, krot, a(6), gate], axis=1)


def _heads_first(w, per_head, first):
    r = w.shape[0]
    w3 = w.reshape(r, N_HEADS, per_head)
    return jnp.concatenate([w3[:, :, :first].reshape(r, -1), w3[:, :, first:].reshape(r, -1)], axis=1)


def _heads_interleaved(w, per_head, first):
    r = w.shape[0]
    a = w[:, :N_HEADS * first].reshape(r, N_HEADS, first)
    b = w[:, N_HEADS * first:].reshape(r, N_HEADS, per_head - first)
    return jnp.concatenate([a, b], axis=2).reshape(r, N_HEADS * per_head)


def _pad_rows8(a):
    return a[:, :8].T


def _local_step(x2, mem2, tgt, p):
    s = x2.shape[0]
    nm = mem2.shape[0]
    head_scale = HEAD_DIM ** -0.5
    mla_scale = (HEAD_DIM + MLA_ROPE) ** -0.5
    tabs = _rope_tables(s)
    mats = _rope_matrices()
    pairs = lambda arr, blk: [[(arr, blk + q, "pair")] for q in range(2)]
    vals = lambda arr, blk: [(arr, blk + q) for q in range(2)]

    h, hb = _ln_fwd(x2, None, p["ln_in_g"], p["ln_in_b"], "ln_in_fwd")
    _, memn_b = _ln_fwd(mem2, None, p["mem_ln_g"], p["mem_ln_b"], "ln_mem_fwd")

    saved = []
    for l in range(DEPTH):
        wa, wb = _split_w_in(p["w_in"][l])
        wp = jnp.concatenate([wa, wb], axis=1)
        wq_up = _heads_first(p["w_mla_q_up"][l], HEAD_DIM + MLA_ROPE, HEAD_DIM)
        wkv_up = _heads_first(p["w_mla_kv_up"][l], 2 * HEAD_DIM, HEAD_DIM)
        bias_row = jnp.pad(p["b_forget"][l], (0, 128 - N_HEADS)).reshape(1, 128)
        pa = _matmul(hb, wa, MXU_DTYPE, "proj_a")
        pb = _matmul(hb, wb, F32, "proj_b")

        fc = _forget_fwd(pb, bias_row)
        fbias = (fc, _pad_rows8(fc))
        o_fox, lse_fox = _softmax_fwd(pairs(pa, 0), pairs(pa, 2), vals(pa, 4), s, s, head_scale, True, fbias,
                                      "fox_fwd")
        o_sb, *rm_sb = _sb_fwd(pa, 6, 8, 10, s, head_scale, "sb_fwd")

        qfull, cqn = _mla_q_fwd(pb, p["mla_q_norm_g"][l], wq_up, tabs, mats)
        kv, ckvn, kr4 = _mla_kv_fwd(pb, p["mla_kv_norm_g"][l], wkv_up, tabs, mats)
        mla_q = [[(qfull, q, "pair"), (qfull, 2, "quad")] for q in range(2)]
        mla_k = [[(kv, q, "pair"), (kr4, 0, "quad")] for q in range(2)]
        o_mla, lse_mla = _softmax_fwd(mla_q, mla_k, vals(kv, 2), s, s, mla_scale, True, None, "mla_fwd")

        mkv = _matmul(memn_b, p["w_mem_kv"][l], MXU_DTYPE, "mem_kv")
        o_mem, lse_mem = _softmax_fwd(pairs(pa, 12), pairs(mkv, 0), vals(mkv, 2), s, nm, head_scale, False, None,
                                      "mem_fwd")

        groups = (o_fox, o_sb, o_mla, o_mem)
        gated = _gate_fwd(groups, pb)
        y = _matmul(gated, p["w_out"][l], F32, "out_proj")
        saved.append(dict(h=h, hb=hb, y=y, wp=wp, wq_up=wq_up, wkv_up=wkv_up, bias_row=bias_row, pa=pa, pb=pb,
                          fbias=fbias, lse_fox=lse_fox, rm_sb=rm_sb, cqn=cqn, ckvn=ckvn, mla_q=mla_q, mla_k=mla_k,
                          kv=kv, lse_mla=lse_mla, mkv=mkv, lse_mem=lse_mem, groups=groups, gated=gated))
        h, hb = _ln_fwd(h, y, p["ln_g"][l], p["ln_b"][l], "ln_fwd")

    dh, sq_cols = _loss_grad(h, tgt)
    loss_sum = jnp.sum(sq_cols)

    grads = {k: [None] * DEPTH for k in ("w_in", "b_forget", "mla_q_norm_g", "w_mla_q_up", "mla_kv_norm_g",
                                         "w_mla_kv_up", "w_mem_kv", "w_out", "ln_g", "ln_b")}
    dmemn = []
    dy1, dy2, c1 = dh, None, 1.0
    for l in reversed(range(DEPTH)):
        r = saved[l]
        pa, pb = r["pa"], r["pb"]
        o_fox, o_sb, o_mla, o_mem = r["groups"]
        du, du_b, dg, db = _ln_bwd(r["h"], r["y"], p["ln_g"][l], dy1, dy2, c1, "ln_bwd")
        grads["ln_g"][l], grads["ln_b"][l] = dg[0], db[0]
        dgated = _matmul(du_b, p["w_out"][l].T, F32, "out_proj_dx")
        grads["w_out"][l] = _matmul(r["gated"].T, du_b, F32, "out_proj_dw")
        dmixed, dgate_b = _gate_bwd(dgated, r["groups"], pb)

        dfq, dfk, dfv, dfc_q, dfc_k = _softmax_bwd(pairs(pa, 0), pairs(pa, 2), vals(pa, 4), o_fox, r["lse_fox"],
                                                   dmixed, 0, s, s, head_scale, True, r["fbias"], "fox_bwd")
        dmisc_f, dbf = _forget_bwd(pb, r["bias_row"], dfc_q + jnp.pad(dfc_k.T, ((0, 0), (0, 128 - 8))))
        grads["b_forget"][l] = dbf[0, :N_HEADS]

        dsq, dsk, dsv = _sb_bwd(pa, 6, 8, 10, dmixed, 1, *r["rm_sb"], s, head_scale, "sb_bwd")

        dqm, dkm, dvm = _softmax_bwd(r["mla_q"], r["mla_k"], vals(r["kv"], 2), o_mla, r["lse_mla"], dmixed, 2,
                                     s, s, mla_scale, True, None, "mla_bwd")
        dq_mla_b = _rope_q_bwd(dqm, tabs, mats)
        dcqn = _matmul(dq_mla_b, r["wq_up"].T, F32, "q_up_dx")
        grads["w_mla_q_up"][l] = _heads_interleaved(_matmul(r["cqn"].T, dq_mla_b, F32, "q_up_dw"),
                                                    HEAD_DIM + MLA_ROPE, HEAD_DIM)
        dcq_b, dgq = _rms_bwd(pb, MLA_Q_RANK, PB_CQ_BLK, p["mla_q_norm_g"][l], dcqn, "rms_q_bwd")
        grads["mla_q_norm_g"][l] = dgq[0]
        dkv_b = jnp.concatenate([dkm[:, :GROUP_W], dvm], axis=1).astype(MXU_DTYPE)
        dckvn = _matmul(dkv_b, r["wkv_up"].T, F32, "kv_up_dx")
        grads["w_mla_kv_up"][l] = _heads_interleaved(_matmul(r["ckvn"].T, dkv_b, F32, "kv_up_dw"),
                                                     2 * HEAD_DIM, HEAD_DIM)
        dckv_b, dgkv = _rms_bwd(pb, MLA_KV_RANK, PB_CKV_BLK, p["mla_kv_norm_g"][l], dckvn, "rms_kv_bwd")
        grads["mla_kv_norm_g"][l] = dgkv[0]
        dmisc_k = _rope_k_bwd(dkm[:, GROUP_W:], tabs, mats)

        dmq, dmk, dmv = _softmax_bwd(pairs(pa, 12), pairs(r["mkv"], 0), vals(r["mkv"], 2), o_mem, r["lse_mem"],
                                     dmixed, 3, s, nm, head_scale, False, None, "mem_bwd")
        dmkv_b = jnp.concatenate([dmk, dmv], axis=1).astype(MXU_DTYPE)
        grads["w_mem_kv"][l] = _matmul(memn_b.T, dmkv_b, F32, "mem_kv_dw")
        dmemn.append(_matmul(dmkv_b, p["w_mem_kv"][l].T, F32, "mem_kv_dx"))

        dmisc = jnp.concatenate([dmisc_f[:, :MISC_KROT], dmisc_k[:, MISC_KROT:MISC_KROT + MLA_ROPE],
                                 jnp.zeros((s, 128 - MISC_KROT - MLA_ROPE), F32)], axis=1)
        dp = jnp.concatenate([d.astype(MXU_DTYPE) for d in (dfq, dfk, dfv, dsq, dsk, dsv, dmq)]
                             + [dgate_b, dcq_b, dckv_b, dmisc.astype(MXU_DTYPE)], axis=1)
        dhproj = _matmul(dp, r["wp"].T, F32, "proj_dx")
        grads["w_in"][l] = _merge_dw_in(_matmul(r["hb"].T, dp, F32, "proj_dw"))
        dy1, dy2, c1 = du, dhproj, ALPHA

    dx, _, dg_in, db_in = _ln_bwd(x2, None, p["ln_in_g"], dy1, dy2, c1, "ln_in_bwd")
    _, _, dg_mem, db_mem = _ln_bwd(mem2, None, p["mem_ln_g"], dmemn[0], dmemn[1], 1.0, "ln_mem_bwd")
    out = {k: jnp.stack(v) for k, v in grads.items()}
    out.update(ln_in_g=dg_in[0], ln_in_b=db_in[0], mem_ln_g=dg_mem[0], mem_ln_b=db_mem[0])
    return loss_sum, dx, out


BIG_NAMES = ("w_in", "w_out", "w_mem_kv", "w_mla_q_up", "w_mla_kv_up")
BIG_AXIS = dict(w_in=2, w_out=1, w_mem_kv=1, w_mla_q_up=2, w_mla_kv_up=2)
SMALL_NAMES = ("ln_in_g", "ln_in_b", "mem_ln_g", "mem_ln_b", "ln_g", "ln_b", "b_forget", "mla_q_norm_g",
               "mla_kv_norm_g")
ALL_NAMES = ("ln_in_g", "ln_in_b", "mem_ln_g", "mem_ln_b", "w_in", "b_forget", "mla_q_norm_g", "w_mla_q_up",
             "mla_kv_norm_g", "w_mla_kv_up", "w_mem_kv", "w_out", "ln_g", "ln_b")
N_CHIPS = 4
N_DEV = 8


def _rows_of(shape):
    rows = -(-int(np.prod(shape)) // LANES)
    return -(-rows // PACK_ALIGN) * PACK_ALIGN


def _pack(arrs, rows):
    parts = []
    for a in arrs:
        f = a.reshape(-1)
        n = _rows_of(a.shape) * LANES
        parts.append(jnp.pad(f, (0, n - f.shape[0])).reshape(-1, LANES))
    used = sum(q.shape[0] for q in parts)
    if rows > used:
        parts.append(jnp.zeros((rows - used, LANES), parts[0].dtype))
    return jnp.concatenate(parts, axis=0)


def _unpack(buf, shapes):
    out, r = [], 0
    for shp in shapes:
        n = _rows_of(shp)
        out.append(buf[r:r + n].reshape(-1)[:int(np.prod(shp))].reshape(shp))
        r += n
    return out


HBM_SPEC = pl.BlockSpec(memory_space=pltpu.HBM)


def _gather_weights(shard):
    rows = shard.shape[0]

    def body(w_ref, out_ref, send_sems, recv_sems, local_sem):
        x, y, c = (lax.axis_index(a) for a in MESH_AXES)
        me = 2 * x + y
        chips = [(1 - x, y), (x, 1 - y), (1 - x, 1 - y)]

        def copy(k, block, to):
            return pltpu.make_async_remote_copy(
                src_ref=w_ref, dst_ref=out_ref.at[block], send_sem=send_sems.at[k], recv_sem=recv_sems.at[k],
                device_id=to, device_id_type=pl.DeviceIdType.MESH)

        mine = pltpu.make_async_copy(w_ref, out_ref.at[me], local_sem)
        mine.start()
        sends = [copy(k, me, (px, py, c)) for k, (px, py) in enumerate(chips)]
        for cp in sends:
            cp.start()
        for k, (px, py) in enumerate(chips):
            copy(k, 2 * px + py, (px, py, c)).wait_recv()
        for cp in sends:
            cp.wait_send()
        mine.wait()

    return pl.pallas_call(
        body, name="gather_weights", out_shape=jax.ShapeDtypeStruct((N_CHIPS, rows, LANES), shard.dtype),
        in_specs=[HBM_SPEC], out_specs=HBM_SPEC,
        scratch_shapes=[pltpu.SemaphoreType.DMA((3,)), pltpu.SemaphoreType.DMA((3,)), pltpu.SemaphoreType.DMA],
    )(shard)


def _exchange_grads(big, small):
    rb, rs = big.shape[1], small.shape[0]

    def body(big_ref, small_ref, obig_ref, osmall_ref, send_sems, recv_sems, local_sems):
        x, y, c = (lax.axis_index(a) for a in MESH_AXES)
        me = 4 * x + 2 * y + c
        flips = [(fx, fy, fc) for fx in (0, 1) for fy in (0, 1) for fc in (0, 1) if fx or fy or fc]
        peers = [(1 - x if fx else x, 1 - y if fy else y, 1 - c if fc else c) for fx, fy, fc in flips]

        def copy_big(k, src_chip, slot, to):
            return pltpu.make_async_remote_copy(
                src_ref=big_ref.at[src_chip], dst_ref=obig_ref.at[slot], send_sem=send_sems.at[k],
                recv_sem=recv_sems.at[k], device_id=to, device_id_type=pl.DeviceIdType.MESH)

        def copy_small(k, slot, to):
            return pltpu.make_async_remote_copy(
                src_ref=small_ref, dst_ref=osmall_ref.at[slot], send_sem=send_sems.at[7 + k],
                recv_sem=recv_sems.at[7 + k], device_id=to, device_id_type=pl.DeviceIdType.MESH)

        own_big = pltpu.make_async_copy(big_ref.at[2 * x + y], obig_ref.at[me], local_sems.at[0])
        own_small = pltpu.make_async_copy(small_ref, osmall_ref.at[me], local_sems.at[1])
        own_big.start()
        own_small.start()
        sends = []
        for k, (px, py, pc) in enumerate(peers):
            sends += [copy_big(k, 2 * px + py, me, (px, py, pc)), copy_small(k, me, (px, py, pc))]
        for cp in sends:
            cp.start()
        for k, (px, py, pc) in enumerate(peers):
            slot = 4 * px + 2 * py + pc
            copy_big(k, 2 * x + y, slot, (px, py, pc)).wait_recv()
            copy_small(k, slot, (px, py, pc)).wait_recv()
        for cp in sends:
            cp.wait_send()
        own_big.wait()
        own_small.wait()

    return pl.pallas_call(
        body, name="exchange_grads",
        out_shape=[jax.ShapeDtypeStruct((N_DEV, rb, LANES), big.dtype),
                   jax.ShapeDtypeStruct((N_DEV, rs, LANES), small.dtype)],
        in_specs=[HBM_SPEC, HBM_SPEC], out_specs=[HBM_SPEC, HBM_SPEC],
        scratch_shapes=[pltpu.SemaphoreType.DMA((14,)), pltpu.SemaphoreType.DMA((14,)),
                        pltpu.SemaphoreType.DMA((2,))],
    )(big, small)


def _adamw(parts, w, m, v, name):
    rows = w.shape[0]
    tile = _pick(rows, (128, 16, 8))
    bc1 = 1.0 - ADAM_B1 ** ADAM_STEP
    bc2 = 1.0 - ADAM_B2 ** ADAM_STEP

    def kern(p_ref, w_ref, m_ref, v_ref, g_ref, d_ref, nm_ref, nv_ref):
        g = p_ref[0].astype(F32)
        for d in range(1, N_DEV):
            g = g + p_ref[d].astype(F32)
        nm = ADAM_B1 * m_ref[...] + (1.0 - ADAM_B1) * g
        nv = ADAM_B2 * v_ref[...] + (1.0 - ADAM_B2) * (g * g)
        g_ref[...] = g
        nm_ref[...] = nm
        nv_ref[...] = nv
        d_ref[...] = -ADAM_LR * ((nm / bc1) / (jnp.sqrt(nv / bc2) + ADAM_EPS) + ADAM_WD * w_ref[...])

    spec = pl.BlockSpec((tile, LANES), lambda i: (i, 0))
    return pl.pallas_call(
        kern, name=name, grid=(rows // tile,),
        in_specs=[pl.BlockSpec((N_DEV, tile, LANES), lambda i: (0, i, 0)), spec, spec, spec],
        out_specs=[spec] * 4, out_shape=[jax.ShapeDtypeStruct((rows, LANES), F32)] * 4,
        compiler_params=pltpu.CompilerParams(dimension_semantics=("parallel",), vmem_limit_bytes=VMEM_LIMIT),
    )(parts, w, m, v)


def kernel(x, mem, ln_in_g, ln_in_b, mem_ln_g, mem_ln_b, w_in, b_forget, mla_q_norm_g, w_mla_q_up, mla_kv_norm_g, w_mla_kv_up, w_mem_kv, w_out, ln_g, ln_b, loss_target, m_ln_in_g, m_ln_in_b, m_mem_ln_g, m_mem_ln_b, m_w_in, m_b_forget, m_mla_q_norm_g, m_w_mla_q_up, m_mla_kv_norm_g, m_w_mla_kv_up, m_w_mem_kv, m_w_out, m_ln_g, m_ln_b, v_ln_in_g, v_ln_in_b, v_mem_ln_g, v_mem_ln_b, v_w_in, v_b_forget, v_mla_q_norm_g, v_w_mla_q_up, v_mla_kv_norm_g, v_w_mla_kv_up, v_w_mem_kv, v_w_out, v_ln_g, v_ln_b):
    w = dict(ln_in_g=ln_in_g, ln_in_b=ln_in_b, mem_ln_g=mem_ln_g, mem_ln_b=mem_ln_b, w_in=w_in, b_forget=b_forget,
             mla_q_norm_g=mla_q_norm_g, w_mla_q_up=w_mla_q_up, mla_kv_norm_g=mla_kv_norm_g,
             w_mla_kv_up=w_mla_kv_up, w_mem_kv=w_mem_kv, w_out=w_out, ln_g=ln_g, ln_b=ln_b)
    mo = dict(ln_in_g=m_ln_in_g, ln_in_b=m_ln_in_b, mem_ln_g=m_mem_ln_g, mem_ln_b=m_mem_ln_b, w_in=m_w_in,
              b_forget=m_b_forget, mla_q_norm_g=m_mla_q_norm_g, w_mla_q_up=m_w_mla_q_up,
              mla_kv_norm_g=m_mla_kv_norm_g, w_mla_kv_up=m_w_mla_kv_up, w_mem_kv=m_w_mem_kv, w_out=m_w_out,
              ln_g=m_ln_g, ln_b=m_ln_b)
    vo = dict(ln_in_g=v_ln_in_g, ln_in_b=v_ln_in_b, mem_ln_g=v_mem_ln_g, mem_ln_b=v_mem_ln_b, w_in=v_w_in,
              b_forget=v_b_forget, mla_q_norm_g=v_mla_q_norm_g, w_mla_q_up=v_w_mla_q_up,
              mla_kv_norm_g=v_mla_kv_norm_g, w_mla_kv_up=v_w_mla_kv_up, w_mem_kv=v_w_mem_kv, w_out=v_w_out,
              ln_g=v_ln_g, ln_b=v_ln_b)
    big_shapes = [w[n].shape for n in BIG_NAMES]
    small_shapes = [w[n].shape for n in SMALL_NAMES]

    gathered = _gather_weights(_pack([w[n].astype(MXU_DTYPE) for n in BIG_NAMES], BIG_ROWS))
    per_chip = [_unpack(gathered[j], big_shapes) for j in range(N_CHIPS)]
    full = dict(w)
    for idx, n in enumerate(BIG_NAMES):
        full[n] = jnp.concatenate([per_chip[j][idx] for j in range(N_CHIPS)], axis=BIG_AXIS[n])

    loss_sum, dx, g = _local_step(x[0], mem[0], loss_target[0], full)
    loss = lax.psum(loss_sum * (0.5 / D_MODEL), MESH_AXES)

    def shard_of(n, j):
        ax, size = BIG_AXIS[n], w[n].shape[BIG_AXIS[n]]
        return lax.slice_in_dim(g[n], j * size, (j + 1) * size, axis=ax)

    big = jnp.stack([_pack([shard_of(n, j).astype(MXU_DTYPE) for n in BIG_NAMES], BIG_ROWS) for j in range(N_CHIPS)])
    small = _pack([g[n] for n in SMALL_NAMES], SMALL_ROWS)
    big_parts, small_parts = _exchange_grads(big, small)

    res_big = _adamw(big_parts, _pack([w[n] for n in BIG_NAMES], BIG_ROWS), _pack([mo[n] for n in BIG_NAMES], BIG_ROWS),
                     _pack([vo[n] for n in BIG_NAMES], BIG_ROWS), "adamw_sharded")
    res_small = _adamw(small_parts, _pack([w[n] for n in SMALL_NAMES], SMALL_ROWS),
                       _pack([mo[n] for n in SMALL_NAMES], SMALL_ROWS), _pack([vo[n] for n in SMALL_NAMES], SMALL_ROWS),
                       "adamw_replicated")
    outs = []
    for kind in range(4):
        vals = dict(zip(BIG_NAMES, _unpack(res_big[kind], big_shapes)))
        vals.update(zip(SMALL_NAMES, _unpack(res_small[kind], small_shapes)))
        outs += [vals[n] for n in ALL_NAMES]
    return (loss, dx[None], *outs)
```

```python
import functools

import numpy as np
import jax
import jax.numpy as jnp
from jax import lax
from jax.experimental import pallas as pl
from jax.experimental.pallas import tpu as pltpu

F32 = jnp.float32
MXU_DTYPE = jnp.bfloat16

DEPTH = 2
D_MODEL = 1024
GROUP_W = 256
N_HEADS = 4
HEAD_DIM = 64
MLA_Q_RANK = 256
MLA_KV_RANK = 128
MLA_ROPE = 32
MLA_Q_COLS = N_HEADS * (HEAD_DIM + MLA_ROPE)
MLA_KV_COLS = N_HEADS * 2 * HEAD_DIM
ROPE_THETA = 10000.0
LN_EPS = 1e-5
RMS_EPS = 1e-6
ALPHA = (2 * DEPTH) ** 0.25
ADAM_LR, ADAM_B1, ADAM_B2, ADAM_EPS, ADAM_WD, ADAM_STEP = 0.001, 0.9, 0.999, 1e-08, 0.01, 10

_SPLIT = (256, 256, 256, 4, 256, 256, 256, 256, 128, 32, 256, 1024)
_OFF = [int(o) for o in np.cumsum((0,) + _SPLIT)]
IN_COLS = _OFF[-1]
PA_COLS = 7 * GROUP_W
PB_COLS = 1024 + 256 + 128 + 128
PB_CQ_BLK, PB_CKV_BLK, PB_MISC_BLK = 4, 10, 11
MISC_KROT = 4

LANES = 1024
PACK_ALIGN = 16
BIG_ROWS = 2560
SMALL_ROWS = 144
ROW_TILE = 256
ATT_TILE = 256
SB_QUERY_TILE = 512
SOFTMAX_TILE = 512
PAIR = 128
SB_SLOT = PAIR // N_HEADS
VMEM_LIMIT = 56 * 1024 * 1024
MATMUL_VMEM = 30 * 1024 * 1024
NEG = -1e30
EXP_UNDERFLOW = -104.0
MESH_AXES = ("x", "y", "c")


def _dot(a, b):
    return jnp.dot(a, b, preferred_element_type=F32)


def _dot_nt(a, b):
    return lax.dot_general(a, b, (((1,), (1,)), ((), ())), preferred_element_type=F32)


def _dot_tn(a, b):
    return lax.dot_general(a, b, (((0,), (0,)), ((), ())), preferred_element_type=F32)


def _split2(x):
    hi = x.astype(MXU_DTYPE)
    lo = (x - hi.astype(F32)).astype(MXU_DTYPE)
    return hi, lo


def _split3(x):
    hi = x.astype(MXU_DTYPE)
    r = x - hi.astype(F32)
    mid = r.astype(MXU_DTYPE)
    lo = (r - mid.astype(F32)).astype(MXU_DTYPE)
    return hi, mid, lo


def _dot_exact_r(x, pm):
    hi, mid, lo = _split3(x)
    return _dot(hi, pm) + _dot(mid, pm) + _dot(lo, pm)


def _dot_exact_l(pm, x):
    hi, mid, lo = _split3(x)
    return _dot(pm, hi) + _dot(pm, mid) + _dot(pm, lo)


def _pick(dim, prefs):
    for p in prefs:
        if dim % p == 0:
            return p
    return dim


def _softplus(z):
    return jnp.maximum(z, 0.0) + jnp.log(1.0 + jnp.exp(-jnp.abs(z)))


def _tile_options(dim):
    opts = [d for d in range(128, min(dim, 2048) + 1, 128) if dim % d == 0]
    return opts or [dim]


def _matmul_tiles(m, n, k, out_bytes):
    tk = k if k <= 4096 else _pick(k, (1024, 512, 256, 128))
    best = None
    for tm in _tile_options(m):
        for tn in _tile_options(n):
            vmem = 2 * 2 * (tm * tk + tk * tn) + 4 * tm * tn + 2 * out_bytes * tm * tn
            if vmem <= MATMUL_VMEM and (best is None or tm * tn / (tm + tn) > best[0]):
                best = (tm * tn / (tm + tn), tm, tn)
    return best[1], best[2], tk


def _matmul(a, b, out_dtype, name, mode="nn"):
    m, k = (a.shape[1], a.shape[0]) if mode == "tn" else a.shape
    n = b.shape[0] if mode == "nt" else b.shape[1]
    tm, tn, tk = _matmul_tiles(m, n, k, jnp.dtype(out_dtype).itemsize)
    nk = k // tk
    dot = {"nn": _dot, "tn": _dot_tn, "nt": _dot_nt}[mode]

    def kern(a_ref, b_ref, o_ref, *acc):
        if nk == 1:
            o_ref[...] = dot(a_ref[...], b_ref[...]).astype(o_ref.dtype)
            return
        acc_ref, = acc
        kk = pl.program_id(2)

        @pl.when(kk == 0)
        def _():
            acc_ref[...] = jnp.zeros_like(acc_ref)

        acc_ref[...] += dot(a_ref[...], b_ref[...])

        @pl.when(kk == nk - 1)
        def _():
            o_ref[...] = acc_ref[...].astype(o_ref.dtype)

    a_spec = (pl.BlockSpec((tk, tm), lambda i, j, kk: (kk, i)) if mode == "tn"
              else pl.BlockSpec((tm, tk), lambda i, j, kk: (i, kk)))
    b_spec = (pl.BlockSpec((tn, tk), lambda i, j, kk: (j, kk)) if mode == "nt"
              else pl.BlockSpec((tk, tn), lambda i, j, kk: (kk, j)))
    return pl.pallas_call(
        kern, name=name, grid=(m // tm, n // tn, nk), in_specs=[a_spec, b_spec],
        out_specs=pl.BlockSpec((tm, tn), lambda i, j, kk: (i, j)),
        out_shape=jax.ShapeDtypeStruct((m, n), out_dtype),
        scratch_shapes=[pltpu.VMEM((tm, tn), F32)] if nk > 1 else [],
        compiler_params=pltpu.CompilerParams(
            dimension_semantics=("parallel", "parallel", "arbitrary"), vmem_limit_bytes=VMEM_LIMIT),
    )(a.astype(MXU_DTYPE), b.astype(MXU_DTYPE))


def _rowwise(body, name, rows, tile, row_ins, full_ins, row_outs, acc_outs=(), scratch=(),
             reverse=False, sequential=False):
    n = rows // tile

    def ridx(i):
        return (n - 1 - i) if reverse else i

    in_specs, args = [], []
    for arr, width, cb in row_ins:
        in_specs.append(pl.BlockSpec((tile, width), lambda i, cb=cb: (ridx(i), cb)))
        args.append(arr)
    for arr in full_ins:
        in_specs.append(pl.BlockSpec(arr.shape, lambda i, nd=arr.ndim: (0,) * nd))
        args.append(arr)
    out_shape = [jax.ShapeDtypeStruct((rows, w), dt) for w, dt in row_outs]
    out_shape += [jax.ShapeDtypeStruct(s, dt) for s, dt in acc_outs]
    out_specs = [pl.BlockSpec((tile, w), lambda i: (ridx(i), 0)) for w, dt in row_outs]
    out_specs += [pl.BlockSpec(s, lambda i, nd=len(s): (0,) * nd) for s, dt in acc_outs]

    def kern(*refs):
        body(pl.program_id(0), *refs)

    sem = "arbitrary" if (acc_outs or sequential) else "parallel"
    return pl.pallas_call(
        kern, name=name, grid=(n,), in_specs=in_specs, out_specs=out_specs, out_shape=out_shape,
        scratch_shapes=list(scratch),
        compiler_params=pltpu.CompilerParams(dimension_semantics=(sem,), vmem_limit_bytes=VMEM_LIMIT),
    )(*args)


def _ln_stats(u):
    mu = jnp.mean(u, axis=-1, keepdims=True)
    xc = u - mu
    var = jnp.mean(xc * xc, axis=-1, keepdims=True)
    return xc, lax.rsqrt(var + LN_EPS)


def _ln_fwd(a, b, g, beta, name):
    rows, d = a.shape
    has_b = b is not None

    def body(i, *refs):
        if has_b:
            a_ref, b_ref, g_ref, be_ref, h_ref, hb_ref = refs
            u = ALPHA * a_ref[...] + b_ref[...]
        else:
            a_ref, g_ref, be_ref, h_ref, hb_ref = refs
            u = a_ref[...]
        xc, rstd = _ln_stats(u)
        y = xc * rstd * g_ref[...] + be_ref[...]
        h_ref[...] = y
        hb_ref[...] = y.astype(hb_ref.dtype)

    row_ins = [(a, d, 0)] + ([(b, d, 0)] if has_b else [])
    return _rowwise(body, name, rows, min(ROW_TILE, rows), row_ins,
                    [g.reshape(1, d), beta.reshape(1, d)], [(d, F32), (d, MXU_DTYPE)])


def _ln_bwd(a, b, g, dy1, dy2, c1, name):
    rows, d = a.shape
    has_b = b is not None
    has_2 = dy2 is not None

    def body(i, *refs):
        refs = list(refs)
        a_ref = refs.pop(0)
        u = a_ref[...]
        if has_b:
            u = ALPHA * u + refs.pop(0)[...]
        dy = c1 * refs.pop(0)[...]
        if has_2:
            dy = dy + refs.pop(0)[...]
        g_ref, du_ref, dub_ref, dg_ref, db_ref = refs

        @pl.when(i == 0)
        def _():
            dg_ref[...] = jnp.zeros_like(dg_ref)
            db_ref[...] = jnp.zeros_like(db_ref)

        xc, rstd = _ln_stats(u)
        xhat = xc * rstd
        dxh = dy * g_ref[...]
        m1 = jnp.mean(dxh, axis=-1, keepdims=True)
        m2 = jnp.mean(dxh * xhat, axis=-1, keepdims=True)
        du = rstd * (dxh - m1 - xhat * m2)
        du_ref[...] = du
        dub_ref[...] = du.astype(dub_ref.dtype)
        dg_ref[...] += jnp.sum(dy * xhat, axis=0, keepdims=True)
        db_ref[...] += jnp.sum(dy, axis=0, keepdims=True)

    row_ins = [(a, d, 0)] + ([(b, d, 0)] if has_b else []) + [(dy1, d, 0)] + ([(dy2, d, 0)] if has_2 else [])
    return _rowwise(body, name, rows, min(ROW_TILE, rows), row_ins, [g.reshape(1, d)],
                    [(d, F32), (d, MXU_DTYPE)], [((1, d), F32), ((1, d), F32)])


def _loss_grad(h, target):
    rows, d = h.shape

    def body(i, h_ref, t_ref, dh_ref, acc_ref):
        @pl.when(i == 0)
        def _():
            acc_ref[...] = jnp.zeros_like(acc_ref)

        e = h_ref[...] - t_ref[...]
        dh_ref[...] = e * (1.0 / d)
        acc_ref[...] += jnp.sum(e * e, axis=0, keepdims=True)

    return _rowwise(body, "loss_grad", rows, ROW_TILE, [(h, d, 0), (target, d, 0)], [],
                    [(d, F32)], [((1, d), F32)])


def _gate_fwd(groups, pb):
    rows = pb.shape[0]
    w = GROUP_W * len(groups)

    def body(i, *refs):
        g = refs[4][...]
        mixed = jnp.concatenate([r[...] for r in refs[:4]], axis=1)
        refs[5][...] = (mixed * (g / (1.0 + jnp.exp(-g)))).astype(refs[5].dtype)

    return _rowwise(body, "gate_fwd", rows, ROW_TILE, [(o, GROUP_W, 0) for o in groups] + [(pb, w, 0)], [],
                    [(w, MXU_DTYPE)])[0]


def _gate_bwd(dgated, groups, pb):
    rows = pb.shape[0]
    w = GROUP_W * len(groups)

    def body(i, *refs):
        dg = refs[0][...]
        mixed = jnp.concatenate([r[...] for r in refs[1:5]], axis=1)
        g = refs[5][...]
        dm_ref, dgate_ref = refs[6], refs[7]
        sig = 1.0 / (1.0 + jnp.exp(-g))
        dm_ref[...] = dg * (g * sig)
        dgate_ref[...] = (dg * mixed * (sig * (1.0 + g * (1.0 - sig)))).astype(dgate_ref.dtype)

    return _rowwise(body, "gate_bwd", rows, ROW_TILE,
                    [(dgated, w, 0)] + [(o, GROUP_W, 0) for o in groups] + [(pb, w, 0)], [],
                    [(w, F32), (w, MXU_DTYPE)])


def _tri(n, kind):
    r = np.arange(n)[:, None]
    c = np.arange(n)[None, :]
    m = {"lower_incl": r >= c, "upper_incl": r <= c, "row_gt_col": r > c, "row_lt_col": r < c}[kind]
    return jnp.asarray(m.astype(np.float32), dtype=MXU_DTYPE)


def _forget_fwd(pb, bias_row):
    rows = pb.shape[0]
    tile = min(ROW_TILE, rows)

    def body(i, x_ref, b_ref, l_ref, o_ref, carry_ref):
        @pl.when(i == 0)
        def _():
            carry_ref[...] = jnp.zeros_like(carry_ref)

        xx = x_ref[...] + b_ref[...]
        lane = lax.broadcasted_iota(jnp.int32, xx.shape, 1)
        lf = jnp.where(lane < N_HEADS, -_softplus(-xx), 0.0)
        o_ref[...] = _dot_exact_l(l_ref[...], lf) + carry_ref[...]
        carry_ref[...] += jnp.sum(lf, axis=0, keepdims=True)

    return _rowwise(body, "forget_fwd", rows, tile, [(pb, 128, PB_MISC_BLK)],
                    [bias_row, _tri(tile, "lower_incl")], [(128, F32)],
                    scratch=[pltpu.VMEM((1, 128), F32)], sequential=True)[0]


def _forget_bwd(pb, bias_row, dfc):
    rows = pb.shape[0]
    tile = min(ROW_TILE, rows)

    def body(i, x_ref, df_ref, b_ref, u_ref, o_ref, db_ref, carry_ref):
        @pl.when(i == 0)
        def _():
            carry_ref[...] = jnp.zeros_like(carry_ref)
            db_ref[...] = jnp.zeros_like(db_ref)

        df = df_ref[...]
        sfx = _dot_exact_l(u_ref[...], df) + carry_ref[...]
        carry_ref[...] += jnp.sum(df, axis=0, keepdims=True)
        xx = x_ref[...] + b_ref[...]
        lane = lax.broadcasted_iota(jnp.int32, xx.shape, 1)
        dl = jnp.where(lane < N_HEADS, sfx / (1.0 + jnp.exp(xx)), 0.0)
        o_ref[...] = dl
        db_ref[...] += jnp.sum(dl, axis=0, keepdims=True)

    return _rowwise(body, "forget_bwd", rows, tile,
                    [(pb, 128, PB_MISC_BLK), (dfc, 128, 0)],
                    [bias_row, _tri(tile, "upper_incl")], [(128, F32)], [((1, 128), F32)],
                    scratch=[pltpu.VMEM((1, 128), F32)], reverse=True, sequential=True)


def _rope_tables(s):
    half = MLA_ROPE // 2
    inv_freq = ROPE_THETA ** (-jnp.arange(half, dtype=F32) / half)
    ang = jnp.arange(s).astype(F32)[:, None] * inv_freq[None, :]
    cos2 = jnp.tile(jnp.cos(ang), (1, 2))
    sin2 = jnp.tile(jnp.sin(ang), (1, 2))
    cx = jnp.tile(cos2, (1, N_HEADS))
    sx = jnp.tile(sin2, (1, N_HEADS))
    cq = jnp.concatenate([jnp.ones((s, GROUP_W), F32), cx], axis=1)
    sq = jnp.concatenate([jnp.zeros((s, GROUP_W), F32), sx], axis=1)
    pad = ((0, 0), (MISC_KROT, 128 - MISC_KROT - MLA_ROPE))
    ck = jnp.pad(cos2, pad)
    sk = jnp.pad(sin2, pad)
    return dict(cq=cq, sq=sq, ck=ck, sk=sk, cx=cx, sx=sx)


def _rot_matrix(width, bases):
    half = MLA_ROPE // 2
    p = np.zeros((width, width), np.float32)
    for b in bases:
        for i in range(half):
            p[b + half + i, b + i] = -1.0
            p[b + i, b + half + i] = 1.0
    return p


def _rope_matrices():
    pq = _rot_matrix(MLA_Q_COLS, [GROUP_W + h * MLA_ROPE for h in range(N_HEADS)])
    pk = _rot_matrix(128, [MISC_KROT])
    p4 = _rot_matrix(128, [h * MLA_ROPE for h in range(N_HEADS)])
    a = np.zeros((128, 128), np.float32)
    for h in range(N_HEADS):
        for r in range(MLA_ROPE):
            a[h * MLA_ROPE + r, MISC_KROT + r] = 1.0
    cast = lambda m: jnp.asarray(m, dtype=MXU_DTYPE)
    return dict(pq=cast(pq), pqt=cast(pq.T), pk=cast(pk), spread=cast(a.T), xa=cast(a), xb=cast(p4.T @ a))


def _rms(c, g):
    r = lax.rsqrt(jnp.mean(c * c, axis=-1, keepdims=True) + RMS_EPS)
    return c * r * g


def _mla_q_fwd(pb, g, w_up, tabs, mats):
    rows = pb.shape[0]

    def body(i, c_ref, cos_ref, sin_ref, g_ref, w_ref, p_ref, q_ref, cn_ref):
        cn = _rms(c_ref[...], g_ref[...]).astype(cn_ref.dtype)
        cn_ref[...] = cn
        q = _dot(cn, w_ref[...])
        q_ref[...] = (q * cos_ref[...] + _dot_exact_r(q, p_ref[...]) * sin_ref[...]).astype(q_ref.dtype)

    return _rowwise(body, "mla_q_fwd", rows, ROW_TILE,
                    [(pb, MLA_Q_RANK, PB_CQ_BLK), (tabs["cq"], MLA_Q_COLS, 0), (tabs["sq"], MLA_Q_COLS, 0)],
                    [g.reshape(1, MLA_Q_RANK), w_up.astype(MXU_DTYPE), mats["pq"]],
                    [(MLA_Q_COLS, MXU_DTYPE), (MLA_Q_RANK, MXU_DTYPE)])


def _mla_kv_fwd(pb, g, w_up, tabs, mats):
    rows = pb.shape[0]

    def body(i, c_ref, x_ref, cos_ref, sin_ref, g_ref, w_ref, p_ref, sp_ref, kv_ref, cn_ref, kr_ref):
        cn = _rms(c_ref[...], g_ref[...]).astype(cn_ref.dtype)
        cn_ref[...] = cn
        kv_ref[...] = _dot(cn, w_ref[...]).astype(kv_ref.dtype)
        xx = x_ref[...]
        kr = xx * cos_ref[...] + _dot_exact_r(xx, p_ref[...]) * sin_ref[...]
        kr_ref[...] = _dot_exact_r(kr, sp_ref[...]).astype(kr_ref.dtype)

    return _rowwise(body, "mla_kv_fwd", rows, ROW_TILE,
                    [(pb, MLA_KV_RANK, PB_CKV_BLK), (pb, 128, PB_MISC_BLK), (tabs["ck"], 128, 0), (tabs["sk"], 128, 0)],
                    [g.reshape(1, MLA_KV_RANK), w_up.astype(MXU_DTYPE), mats["pk"], mats["spread"]],
                    [(MLA_KV_COLS, MXU_DTYPE), (MLA_KV_RANK, MXU_DTYPE), (128, MXU_DTYPE)])


def _rope_q_bwd(dq_full, tabs, mats):
    rows, nq = dq_full.shape

    def body(i, d_ref, cos_ref, sin_ref, pt_ref, o_ref):
        d = d_ref[...]
        o_ref[...] = (d * cos_ref[...] + _dot_exact_r(d * sin_ref[...], pt_ref[...])).astype(o_ref.dtype)

    return _rowwise(body, "rope_q_bwd", rows, ROW_TILE,
                    [(dq_full, nq, 0), (tabs["cq"], nq, 0), (tabs["sq"], nq, 0)], [mats["pqt"]],
                    [(nq, MXU_DTYPE)])[0]


def _rope_k_bwd(dkr, tabs, mats):
    rows = dkr.shape[0]

    def body(i, d_ref, cos_ref, sin_ref, a_ref, b_ref, o_ref):
        d = d_ref[...]
        o_ref[...] = _dot_exact_r(d * cos_ref[...], a_ref[...]) + _dot_exact_r(d * sin_ref[...], b_ref[...])

    return _rowwise(body, "rope_k_bwd", rows, ROW_TILE,
                    [(dkr, 128, 0), (tabs["cx"], 128, 0), (tabs["sx"], 128, 0)], [mats["xa"], mats["xb"]],
                    [(128, F32)])[0]


def _rms_bwd(pb, width, col_blk, g, dy, name):
    rows = pb.shape[0]

    def body(i, c_ref, dy_ref, g_ref, dc_ref, dg_ref):
        @pl.when(i == 0)
        def _():
            dg_ref[...] = jnp.zeros_like(dg_ref)

        c = c_ref[...]
        dy = dy_ref[...]
        r = lax.rsqrt(jnp.mean(c * c, axis=-1, keepdims=True) + RMS_EPS)
        dyg = dy * g_ref[...]
        dc = r * dyg - c * (r * r * r) * jnp.mean(c * dyg, axis=-1, keepdims=True)
        dc_ref[...] = dc.astype(dc_ref.dtype)
        dg_ref[...] += jnp.sum(dy * c * r, axis=0, keepdims=True)

    return _rowwise(body, name, rows, ROW_TILE, [(pb, width, col_blk), (dy, width, 0)], [g.reshape(1, width)],
                    [(width, MXU_DTYPE)], [((1, width), F32)])


def _att_params(parallel):
    return pltpu.CompilerParams(dimension_semantics=("parallel" if parallel else "arbitrary",),
                                vmem_limit_bytes=VMEM_LIMIT)


def _blk_off(j, t):
    return j * t if isinstance(j, int) else pl.multiple_of(j * t, t)


def _causal_mask(t, strict):
    r = lax.broadcasted_iota(jnp.int32, (t, t), 0)
    c = lax.broadcasted_iota(jnp.int32, (t, t), 1)
    return (c < r) if strict else (c <= r)


def _lane_mask(kind, head, rows):
    lane = lax.broadcasted_iota(jnp.int32, (rows, PAIR), 1)
    if kind == "pair":
        return (lane < HEAD_DIM) if head % 2 == 0 else (lane >= HEAD_DIM)
    return (lane >= MLA_ROPE * head) & (lane < MLA_ROPE * (head + 1))


def _row_spec(t, cb, width=PAIR):
    return pl.BlockSpec((t, width), lambda i, cb=cb: (i, cb))


def _whole_spec(rows, cb, width=PAIR):
    return pl.BlockSpec((rows, width), lambda i, cb=cb: (0, cb), pipeline_mode=pl.Buffered(1))


def _is_pow2(x):
    return float(np.frexp(x)[0]) == 0.5


def _masked_heads(blocks, kinds, pair, dtype, scale=None):
    out = []
    for e in range(2):
        head = 2 * pair + e
        parts = [jnp.where(_lane_mask(k, head, b.shape[0]), b.astype(F32) * (1.0 if scale is None else scale),
                           0.0).astype(dtype)
                 for b, k in zip(blocks, kinds)]
        out.append(parts[0] if len(parts) == 1 else jnp.concatenate(parts, axis=1))
    return out


def _col(block, idx):
    lane = lax.broadcasted_iota(jnp.int32, block.shape, 1)
    return jnp.sum(jnp.where(lane == idx, block, 0.0), axis=1, keepdims=True)


def _scatter_cols(cols, t):
    lane = lax.broadcasted_iota(jnp.int32, (t, PAIR), 1)
    out = jnp.zeros((t, PAIR), F32)
    for idx, c in cols.items():
        out = out + jnp.where(lane == idx, c, 0.0)
    return out


def _take_heads(per_head, pair):
    return jnp.where(_lane_mask("pair", 0, per_head[0].shape[0]), per_head[0], per_head[1])


class _Parts:
    def __init__(self, q_parts, k_parts, v_parts, tq, sk):
        self.kinds = [[kind for _, _, kind in q_parts[p]] for p in range(2)]
        self.nparts = len(q_parts[0])
        self.q_specs = [_row_spec(tq, cb) for p in range(2) for _, cb, _ in q_parts[p]]
        self.q_args = [a for p in range(2) for a, _, _ in q_parts[p]]
        self.k_specs = [_whole_spec(sk, cb) for p in range(2) for _, cb, _ in k_parts[p]]
        self.k_args = [a for p in range(2) for a, _, _ in k_parts[p]]
        self.v_specs = [_whole_spec(sk, cb) for _, cb in v_parts]
        self.v_args = [a for a, _ in v_parts]
        self.width = PAIR * self.nparts

    def split(self, refs):
        n = self.nparts
        refs = list(refs)
        q = [refs[p * n:(p + 1) * n] for p in range(2)]
        k = [refs[2 * n + p * n:2 * n + (p + 1) * n] for p in range(2)]
        v = refs[4 * n:4 * n + 2]
        return q, k, v, refs[4 * n + 2:]

    def k_block(self, k_refs, off, t):
        blks = [r[pl.ds(off, t), :] for r in k_refs]
        return blks[0] if len(blks) == 1 else jnp.concatenate(blks, axis=1)


def _softmax_fwd(q_parts, k_parts, v_parts, sq, sk, scale, causal, bias, name):
    tq = min(SOFTMAX_TILE, sq)
    tk = tq if causal else min(SOFTMAX_TILE, sk)
    nkv = sk // tk
    pp = _Parts(q_parts, k_parts, v_parts, tq, sk)

    def kern(*refs):
        q_refs, k_refs, v_refs, rest = pp.split(refs)
        if bias is not None:
            fc_ref, ft_ref, o_ref, lse_ref = rest
            fcb = fc_ref[...]
        else:
            o_ref, lse_ref = rest
        i = pl.program_id(0)
        fold = _is_pow2(scale)
        head_on = [jnp.where(_lane_mask("pair", e, tk), 1.0, 0.0).astype(MXU_DTYPE) for e in range(2)]
        head_off = [jnp.where(_lane_mask("pair", e, tk), 0.0, 1.0).astype(MXU_DTYPE) for e in range(2)]
        lse_cols = {}
        for p in range(2):
            qm = _masked_heads([r[...] for r in q_refs[p]], pp.kinds[p], p, MXU_DTYPE, scale if fold else None)

            def block(j, carry, masked, p=p, qm=qm):
                off = _blk_off(j, tk)
                kb = pp.k_block(k_refs[p], off, tk)
                vb = v_refs[p][pl.ds(off, tk), :]
                out = []
                for e in range(2):
                    h = 2 * p + e
                    m, acc = carry[e]
                    s = _dot_nt(qm[e], kb)
                    if not fold:
                        s = s * scale
                    if bias is not None:
                        s = s - ft_ref[h:h + 1, pl.ds(off, tk)]
                    if masked:
                        s = jnp.where(_causal_mask(tq, False), s, NEG)
                    m_new = jnp.maximum(m, jnp.max(s, axis=1, keepdims=True))
                    pr = jnp.exp(s - m_new).astype(MXU_DTYPE)
                    out.append((m_new, jnp.exp(m - m_new) * acc + _dot(pr, vb * head_on[e] + head_off[e])))
                return tuple(out)

            carry = tuple((jnp.full((tq, 1), NEG, F32), jnp.zeros((tq, PAIR), F32)) for _ in range(2))
            if causal:
                carry = lax.fori_loop(0, i, lambda j, c, block=block: block(j, c, False), carry)
                carry = block(i, carry, True)
            else:
                for j in range(nkv):
                    carry = block(j, carry, False)
            outs = []
            for e in range(2):
                m, acc = carry[e]
                l = _col(acc, HEAD_DIM * (1 - e))
                outs.append(acc / l)
                lse_cols[2 * p + e] = m + jnp.log(l) + (_col(fcb, 2 * p + e) if bias is not None else 0.0)
            o_ref[:, p * PAIR:(p + 1) * PAIR] = _take_heads(outs, p)
        lse_ref[...] = _scatter_cols(lse_cols, tq)

    in_specs = pp.q_specs + pp.k_specs + pp.v_specs
    args = pp.q_args + pp.k_args + pp.v_args
    if bias is not None:
        in_specs += [_row_spec(tq, 0), pl.BlockSpec((8, sk), lambda i: (0, 0), pipeline_mode=pl.Buffered(1))]
        args += list(bias)
    return pl.pallas_call(
        kern, name=name, grid=(sq // tq,), in_specs=in_specs,
        out_specs=[_row_spec(tq, 0, GROUP_W), _row_spec(tq, 0)],
        out_shape=[jax.ShapeDtypeStruct((sq, GROUP_W), F32), jax.ShapeDtypeStruct((sq, PAIR), F32)],
        compiler_params=_att_params(True),
    )(*args)


def _softmax_bwd(q_parts, k_parts, v_parts, o, lse, do, do_blk, sq, sk, scale, causal, bias, name):
    tq = min(SOFTMAX_TILE, sq)
    tk = tq if causal else min(SOFTMAX_TILE, sk)
    nkv = sk // tk
    pp = _Parts(q_parts, k_parts, v_parts, tq, sk)
    quad = pp.nparts == 2
    wq = GROUP_W + (PAIR if quad else 0)

    def kern(*refs):
        q_refs, k_refs, v_refs, rest = pp.split(refs)
        if bias is not None:
            o_ref, lse_ref, do_ref, fc_ref, ft_ref, dq_ref, dk_ref, dv_ref, dfq_ref, dfk_ref = rest
            fcb = fc_ref[...]
        else:
            o_ref, lse_ref, do_ref, dq_ref, dk_ref, dv_ref = rest
        i = pl.program_id(0)
        fold = _is_pow2(scale)

        @pl.when(i == 0)
        def _():
            dk_ref[...] = jnp.zeros_like(dk_ref)
            dv_ref[...] = jnp.zeros_like(dv_ref)
            if bias is not None:
                dfk_ref[...] = jnp.zeros_like(dfk_ref)

        lse_b = lse_ref[...]
        qm, dom, delta, lse_h = [], [], [], []
        for p in range(2):
            qm += _masked_heads([r[...] for r in q_refs[p]], pp.kinds[p], p, MXU_DTYPE, scale if fold else None)
            do_p = do_ref[:, p * PAIR:(p + 1) * PAIR]
            dom += _masked_heads([do_p], ["pair"], p, MXU_DTYPE)
            prod = do_p * o_ref[:, p * PAIR:(p + 1) * PAIR]
            for e in range(2):
                h = 2 * p + e
                delta.append(jnp.sum(jnp.where(_lane_mask("pair", h, tq), prod, 0.0), axis=1, keepdims=True))
                lse_h.append(_col(lse_b, h) - (_col(fcb, h) if bias is not None else 0.0))

        def block(j, carry, masked):
            off = _blk_off(j, tk)
            out = []
            for p in range(2):
                kb = pp.k_block(k_refs[p], off, tk)
                vb = v_refs[p][pl.ds(off, tk), :]
                dk_acc = jnp.zeros((tk, pp.width), F32)
                dv_acc = jnp.zeros((tk, PAIR), F32)
                for e in range(2):
                    h = 2 * p + e
                    dq, dfq = carry[h]
                    s = _dot_nt(qm[h], kb)
                    if not fold:
                        s = s * scale
                    if bias is not None:
                        s = s - ft_ref[h:h + 1, pl.ds(off, tk)]
                    if masked:
                        s = jnp.where(_causal_mask(tq, False), s, NEG)
                    pr = jnp.exp(s - lse_h[h])
                    ds = pr * (_dot_nt(dom[h], vb) - delta[h])
                    dsb = (ds if fold else ds * scale).astype(MXU_DTYPE)
                    dv_acc = dv_acc + _dot_tn(pr.astype(MXU_DTYPE), dom[h])
                    dk_acc = dk_acc + _dot_tn(dsb, qm[h])
                    dq = dq + _dot(dsb, kb)
                    if bias is not None:
                        dfq = dfq + jnp.sum(ds, axis=1, keepdims=True)
                        dfk_ref[h:h + 1, pl.ds(off, tk)] -= jnp.sum(ds, axis=0, keepdims=True)
                    out.append((dq, dfq))
                dv_ref[pl.ds(off, tk), p * PAIR:(p + 1) * PAIR] += dv_acc
                dk_ref[pl.ds(off, tk), p * PAIR:(p + 1) * PAIR] += dk_acc[:, :PAIR]
                if quad:
                    dk_ref[pl.ds(off, tk), GROUP_W:] += dk_acc[:, PAIR:]
            return tuple(out)

        carry = tuple((jnp.zeros((tq, pp.width), F32), jnp.zeros((tq, 1), F32)) for _ in range(N_HEADS))
        if causal:
            carry = lax.fori_loop(0, i, lambda j, c: block(j, c, False), carry)
            carry = block(i, carry, True)
        else:
            for j in range(nkv):
                carry = block(j, carry, False)
        dqs = [c[0] * scale if fold else c[0] for c in carry]
        for p in range(2):
            dq_ref[:, p * PAIR:(p + 1) * PAIR] = _take_heads([dqs[2 * p + e][:, :PAIR] for e in range(2)], p)
        if quad:
            dq_ref[:, GROUP_W:] = sum(jnp.where(_lane_mask("quad", h, tq), dqs[h][:, PAIR:], 0.0)
                                      for h in range(N_HEADS))
        if bias is not None:
            dfq_ref[...] = _scatter_cols({h: carry[h][1] for h in range(N_HEADS)}, tq)

    acc_spec = lambda rows, width: pl.BlockSpec((rows, width), lambda i: (0, 0), pipeline_mode=pl.Buffered(1))
    in_specs = pp.q_specs + pp.k_specs + pp.v_specs + [_row_spec(tq, 0, GROUP_W), _row_spec(tq, 0),
                                                       _row_spec(tq, do_blk, GROUP_W)]
    args = pp.q_args + pp.k_args + pp.v_args + [o, lse, do]
    out_specs = [_row_spec(tq, 0, wq), acc_spec(sk, wq), acc_spec(sk, GROUP_W)]
    out_shape = [jax.ShapeDtypeStruct((sq, wq), F32), jax.ShapeDtypeStruct((sk, wq), F32),
                 jax.ShapeDtypeStruct((sk, GROUP_W), F32)]
    if bias is not None:
        in_specs += [_row_spec(tq, 0), pl.BlockSpec((8, sk), lambda i: (0, 0), pipeline_mode=pl.Buffered(1))]
        args += list(bias)
        out_specs += [_row_spec(tq, 0), acc_spec(8, sk)]
        out_shape += [jax.ShapeDtypeStruct((sq, PAIR), F32), jax.ShapeDtypeStruct((8, sk), F32)]
    return pl.pallas_call(
        kern, name=name, grid=(sq // tq,), in_specs=in_specs, out_specs=out_specs, out_shape=out_shape,
        compiler_params=_att_params(False),
    )(*args)


def _sb_logs(qh, kb, valid):
    z = _dot_nt(qh, kb)
    sp = _softplus(z)
    lk = -sp
    if valid is not None:
        lk = jnp.where(valid, lk, 0.0)
    return lk, z - sp


def _sb_valid(d, tq, tk):
    r = lax.broadcasted_iota(jnp.int32, (tq, tk), 0)
    c = lax.broadcasted_iota(jnp.int32, (tq, tk), 1)
    return c + d * tk < r


def _tri_sums(xs, tri):
    t = xs[0].shape[0]
    pieces = [_split2(x) for x in xs]
    hi = _dot(jnp.concatenate([pc[0] for pc in pieces], axis=0), tri)
    lo = _dot(jnp.concatenate([pc[1] for pc in pieces], axis=0), tri)
    return [hi[n * t:(n + 1) * t] + lo[n * t:(n + 1) * t] for n in range(len(xs))]


def _sb_fwd(src, q_blk, k_blk, v_blk, s, scale, name):
    assert _is_pow2(scale) and s // ATT_TILE <= SB_SLOT
    tq, t = min(SB_QUERY_TILE, s), min(ATT_TILE, s)
    band = tq // t
    pair = lambda blk: [[(src, blk + p, "pair")] for p in range(2)]
    pp = _Parts(pair(q_blk), pair(k_blk), [(src, v_blk + p) for p in range(2)], tq, s)

    def kern(*refs):
        q_refs, k_refs, v_refs, (tri_ref, o_ref, rm_ref, cnt_ref) = pp.split(refs)
        i = pl.program_id(0)
        tri = tri_ref[...]
        lane = lax.broadcasted_iota(jnp.int32, (tq, PAIR), 1)
        qm = []
        for p in range(2):
            qm += _masked_heads([q_refs[p][0][...]], ["pair"], p, MXU_DTYPE, scale)

        def block(j, carry, valid):
            accs, rights, rm = carry
            off = _blk_off(j, t)
            kbs = [k_refs[p][0][pl.ds(off, t), :] for p in range(2)]
            vbs = [v_refs[p][pl.ds(off, t), :] for p in range(2)]
            logs = [_sb_logs(qm[h], kbs[h // 2], valid) for h in range(N_HEADS)]
            tails = _tri_sums([lg[0] for lg in logs], tri)
            new_acc, new_right = [], []
            for h in range(N_HEADS):
                lk, ls = logs[h]
                w = jnp.exp(ls + tails[h] + rights[h])
                if valid is not None:
                    w = jnp.where(valid, w, 0.0)
                new_acc.append(accs[h] + _dot(w.astype(MXU_DTYPE), vbs[h // 2]))
                rm = rm + jnp.where(lane == SB_SLOT * h + j, rights[h], 0.0)
                new_right.append(rights[h] + jnp.sum(lk, axis=1, keepdims=True))
            return tuple(new_acc), tuple(new_right), rm

        carry = (tuple(jnp.zeros((tq, PAIR), F32) for _ in range(N_HEADS)),
                 tuple(jnp.zeros((tq, 1), F32) for _ in range(N_HEADS)), jnp.zeros((tq, PAIR), F32))
        for d in reversed(range(band)):
            carry = block(band * i + d, carry, _sb_valid(d, tq, t))

        def alive(c):
            return functools.reduce(jnp.maximum, [jnp.max(r) for r in c[1]])

        def step(state):
            n, _, c = state
            c = block(band * i - 1 - n, c, None)
            return n + 1, alive(c), c

        n_done, _, carry = lax.while_loop(lambda st: jnp.logical_and(st[0] < band * i, st[1] > EXP_UNDERFLOW),
                                          step, (jnp.int32(0), alive(carry), carry))
        cnt_ref[i] = n_done
        for p in range(2):
            o_ref[:, p * PAIR:(p + 1) * PAIR] = _take_heads([carry[0][2 * p + e] for e in range(2)], p)
        rm_ref[...] = carry[2]

    return pl.pallas_call(
        kern, name=name, grid=(s // tq,),
        in_specs=pp.q_specs + pp.k_specs + pp.v_specs + [pl.BlockSpec((t, t), lambda i: (0, 0))],
        out_specs=[_row_spec(tq, 0, GROUP_W), _row_spec(tq, 0), pl.BlockSpec(memory_space=pltpu.SMEM)],
        out_shape=[jax.ShapeDtypeStruct((s, GROUP_W), F32), jax.ShapeDtypeStruct((s, PAIR), F32),
                   jax.ShapeDtypeStruct((s // tq,), jnp.int32)],
        compiler_params=_att_params(False),
    )(*(pp.q_args + pp.k_args + pp.v_args + [_tri(t, "row_gt_col")]))


def _sb_bwd(src, q_blk, k_blk, v_blk, do, do_blk, rm, visited, s, scale, name):
    assert _is_pow2(scale)
    tq, t = min(SB_QUERY_TILE, s), min(ATT_TILE, s)
    band = tq // t
    pair = lambda blk: [[(src, blk + p, "pair")] for p in range(2)]
    pp = _Parts(pair(q_blk), pair(k_blk), [(src, v_blk + p) for p in range(2)], tq, s)

    def kern(*refs):
        q_refs, k_refs, v_refs, (do_ref, rm_ref, tri_ref, pre_ref, cnt_ref, dq_ref, dk_ref, dv_ref) = pp.split(refs)
        i = pl.program_id(0)

        @pl.when(i == 0)
        def _():
            dk_ref[...] = jnp.zeros_like(dk_ref)
            dv_ref[...] = jnp.zeros_like(dv_ref)

        rmb = rm_ref[...]
        tri = tri_ref[...]
        pre = pre_ref[...]
        qm, dom = [], []
        for p in range(2):
            qm += _masked_heads([q_refs[p][0][...]], ["pair"], p, MXU_DTYPE, scale)
            dom += _masked_heads([do_ref[:, p * PAIR:(p + 1) * PAIR]], ["pair"], p, MXU_DTYPE)

        def block(j, carry, valid):
            dqs, lefts = carry
            off = _blk_off(j, t)
            kbs = [k_refs[p][0][pl.ds(off, t), :] for p in range(2)]
            vbs = [v_refs[p][pl.ds(off, t), :] for p in range(2)]
            logs = [_sb_logs(qm[h], kbs[h // 2], valid) for h in range(N_HEADS)]
            tails = _tri_sums([lg[0] for lg in logs], tri)
            ws, gs = [], []
            for h in range(N_HEADS):
                lk, ls = logs[h]
                w = jnp.exp(ls + tails[h] + _col(rmb, SB_SLOT * h + j))
                if valid is not None:
                    w = jnp.where(valid, w, 0.0)
                ws.append(w)
                gs.append(_dot_nt(dom[h], vbs[h // 2]) * w)
            prefix = _tri_sums(gs, pre)
            new_dq, new_left = [], []
            dk_acc = [jnp.zeros((t, PAIR), F32) for _ in range(2)]
            dv_acc = [jnp.zeros((t, PAIR), F32) for _ in range(2)]
            for h in range(N_HEADS):
                lk, ls = logs[h]
                sig = jnp.exp(ls)
                dz = gs[h] * (1.0 - sig) - sig * (prefix[h] + lefts[h])
                if valid is not None:
                    dz = jnp.where(valid, dz, 0.0)
                dzb = dz.astype(MXU_DTYPE)
                dv_acc[h // 2] = dv_acc[h // 2] + _dot_tn(ws[h].astype(MXU_DTYPE), dom[h])
                dk_acc[h // 2] = dk_acc[h // 2] + _dot_tn(dzb, qm[h])
                new_dq.append(dqs[h] + _dot(dzb, kbs[h // 2]))
                new_left.append(lefts[h] + jnp.sum(gs[h], axis=1, keepdims=True))
            for p in range(2):
                dv_ref[pl.ds(off, t), p * PAIR:(p + 1) * PAIR] += dv_acc[p]
                dk_ref[pl.ds(off, t), p * PAIR:(p + 1) * PAIR] += dk_acc[p]
            return tuple(new_dq), tuple(new_left)

        carry = (tuple(jnp.zeros((tq, PAIR), F32) for _ in range(N_HEADS)),
                 tuple(jnp.zeros((tq, 1), F32) for _ in range(N_HEADS)))
        carry = lax.fori_loop(band * i - cnt_ref[i], band * i, lambda j, c: block(j, c, None), carry)
        for d in range(band):
            carry = block(band * i + d, carry, _sb_valid(d, tq, t))
        for p in range(2):
            dq_ref[:, p * PAIR:(p + 1) * PAIR] = _take_heads([carry[0][2 * p + e] * scale for e in range(2)], p)

    mspec = pl.BlockSpec((t, t), lambda i: (0, 0))
    acc_spec = pl.BlockSpec((s, GROUP_W), lambda i: (0, 0), pipeline_mode=pl.Buffered(1))
    return pl.pallas_call(
        kern, name=name, grid=(s // tq,),
        in_specs=pp.q_specs + pp.k_specs + pp.v_specs + [_row_spec(tq, do_blk, GROUP_W), _row_spec(tq, 0), mspec, mspec,
                                                         pl.BlockSpec(memory_space=pltpu.SMEM)],
        out_specs=[_row_spec(tq, 0, GROUP_W), acc_spec, acc_spec],
        out_shape=[jax.ShapeDtypeStruct((s, GROUP_W), F32)] * 3,
        compiler_params=_att_params(False),
    )(*(pp.q_args + pp.k_args + pp.v_args + [do, rm, _tri(t, "row_gt_col"), _tri(t, "row_lt_col"), visited]))


def _split_w_in(w):
    col = lambda n: w[:, _OFF[n]:_OFF[n + 1]]
    wa = jnp.concatenate([col(0), col(1), col(2), col(4), col(5), col(6), col(10)], axis=1)
    misc = jnp.concatenate([col(3), col(9), jnp.zeros((w.shape[0], 128 - 4 - MLA_ROPE), w.dtype)], axis=1)
    wb = jnp.concatenate([col(11), col(7), col(8), misc], axis=1)
    return wa, wb


def _merge_dw_in(dwp):
    a = lambda n: dwp[:, n * GROUP_W:(n + 1) * GROUP_W]
    b0 = PA_COLS
    gate = dwp[:, b0:b0 + 1024]
    cq = dwp[:, b0 + 1024:b0 + 1280]
    ckv = dwp[:, b0 + 1280:b0 + 1408]
    flog = dwp[:, b0 + 1408:b0 + 1412]
    krot = dwp[:, b0 + 1408 + MISC_KROT:b0 + 1408 + MISC_KROT + MLA_ROPE]
    return jnp.concatenate([a(0), a(1), a(2), flog, a(3), a(4), a(5), cq, ckv, krot, a(6), gate], axis=1)


def _heads_first(w, per_head, first):
    r = w.shape[0]
    w3 = w.reshape(r, N_HEADS, per_head)
    return jnp.concatenate([w3[:, :, :first].reshape(r, -1), w3[:, :, first:].reshape(r, -1)], axis=1)


def _heads_interleaved(w, per_head, first):
    r = w.shape[0]
    a = w[:, :N_HEADS * first].reshape(r, N_HEADS, first)
    b = w[:, N_HEADS * first:].reshape(r, N_HEADS, per_head - first)
    return jnp.concatenate([a, b], axis=2).reshape(r, N_HEADS * per_head)


def _pad_rows8(a):
    return a[:, :8].T


def _local_step(x2, mem2, tgt, p):
    s = x2.shape[0]
    nm = mem2.shape[0]
    head_scale = HEAD_DIM ** -0.5
    mla_scale = (HEAD_DIM + MLA_ROPE) ** -0.5
    tabs = _rope_tables(s)
    mats = _rope_matrices()
    pairs = lambda arr, blk: [[(arr, blk + q, "pair")] for q in range(2)]
    vals = lambda arr, blk: [(arr, blk + q) for q in range(2)]

    h, hb = _ln_fwd(x2, None, p["ln_in_g"], p["ln_in_b"], "ln_in_fwd")
    _, memn_b = _ln_fwd(mem2, None, p["mem_ln_g"], p["mem_ln_b"], "ln_mem_fwd")

    saved = []
    for l in range(DEPTH):
        wa, wb = _split_w_in(p["w_in"][l])
        wp = jnp.concatenate([wa, wb], axis=1)
        wq_up = _heads_first(p["w_mla_q_up"][l], HEAD_DIM + MLA_ROPE, HEAD_DIM)
        wkv_up = _heads_first(p["w_mla_kv_up"][l], 2 * HEAD_DIM, HEAD_DIM)
        bias_row = jnp.pad(p["b_forget"][l], (0, 128 - N_HEADS)).reshape(1, 128)
        pa = _matmul(hb, wa, MXU_DTYPE, "proj_a")
        pb = _matmul(hb, wb, F32, "proj_b")

        fc = _forget_fwd(pb, bias_row)
        fbias = (fc, _pad_rows8(fc))
        o_fox, lse_fox = _softmax_fwd(pairs(pa, 0), pairs(pa, 2), vals(pa, 4), s, s, head_scale, True, fbias,
                                      "fox_fwd")
        o_sb, *rm_sb = _sb_fwd(pa, 6, 8, 10, s, head_scale, "sb_fwd")

        qfull, cqn = _mla_q_fwd(pb, p["mla_q_norm_g"][l], wq_up, tabs, mats)
        kv, ckvn, kr4 = _mla_kv_fwd(pb, p["mla_kv_norm_g"][l], wkv_up, tabs, mats)
        mla_q = [[(qfull, q, "pair"), (qfull, 2, "quad")] for q in range(2)]
        mla_k = [[(kv, q, "pair"), (kr4, 0, "quad")] for q in range(2)]
        o_mla, lse_mla = _softmax_fwd(mla_q, mla_k, vals(kv, 2), s, s, mla_scale, True, None, "mla_fwd")

        mkv = _matmul(memn_b, p["w_mem_kv"][l], MXU_DTYPE, "mem_kv")
        o_mem, lse_mem = _softmax_fwd(pairs(pa, 12), pairs(mkv, 0), vals(mkv, 2), s, nm, head_scale, False, None,
                                      "mem_fwd")

        groups = (o_fox, o_sb, o_mla, o_mem)
        gated = _gate_fwd(groups, pb)
        y = _matmul(gated, p["w_out"][l], F32, "out_proj")
        saved.append(dict(h=h, hb=hb, y=y, wp=wp, wq_up=wq_up, wkv_up=wkv_up, bias_row=bias_row, pa=pa, pb=pb,
                          fbias=fbias, lse_fox=lse_fox, rm_sb=rm_sb, cqn=cqn, ckvn=ckvn, mla_q=mla_q, mla_k=mla_k,
                          kv=kv, lse_mla=lse_mla, mkv=mkv, lse_mem=lse_mem, groups=groups, gated=gated))
        h, hb = _ln_fwd(h, y, p["ln_g"][l], p["ln_b"][l], "ln_fwd")

    dh, sq_cols = _loss_grad(h, tgt)
    loss_sum = jnp.sum(sq_cols)

    grads = {k: [None] * DEPTH for k in ("w_in", "b_forget", "mla_q_norm_g", "w_mla_q_up", "mla_kv_norm_g",
                                         "w_mla_kv_up", "w_mem_kv", "w_out", "ln_g", "ln_b")}
    dmemn = []
    dy1, dy2, c1 = dh, None, 1.0
    for l in reversed(range(DEPTH)):
        r = saved[l]
        pa, pb = r["pa"], r["pb"]
        o_fox, o_sb, o_mla, o_mem = r["groups"]
        du, du_b, dg, db = _ln_bwd(r["h"], r["y"], p["ln_g"][l], dy1, dy2, c1, "ln_bwd")
        grads["ln_g"][l], grads["ln_b"][l] = dg[0], db[0]
        dgated = _matmul(du_b, p["w_out"][l], F32, "out_proj_dx", "nt")
        grads["w_out"][l] = _matmul(r["gated"], du_b, F32, "out_proj_dw", "tn")
        dmixed, dgate_b = _gate_bwd(dgated, r["groups"], pb)

        dfq, dfk, dfv, dfc_q, dfc_k = _softmax_bwd(pairs(pa, 0), pairs(pa, 2), vals(pa, 4), o_fox, r["lse_fox"],
                                                   dmixed, 0, s, s, head_scale, True, r["fbias"], "fox_bwd")
        dmisc_f, dbf = _forget_bwd(pb, r["bias_row"], dfc_q + jnp.pad(dfc_k.T, ((0, 0), (0, 128 - 8))))
        grads["b_forget"][l] = dbf[0, :N_HEADS]

        dsq, dsk, dsv = _sb_bwd(pa, 6, 8, 10, dmixed, 1, *r["rm_sb"], s, head_scale, "sb_bwd")

        dqm, dkm, dvm = _softmax_bwd(r["mla_q"], r["mla_k"], vals(r["kv"], 2), o_mla, r["lse_mla"], dmixed, 2,
                                     s, s, mla_scale, True, None, "mla_bwd")
        dq_mla_b = _rope_q_bwd(dqm, tabs, mats)
        dcqn = _matmul(dq_mla_b, r["wq_up"], F32, "q_up_dx", "nt")
        grads["w_mla_q_up"][l] = _heads_interleaved(_matmul(r["cqn"], dq_mla_b, F32, "q_up_dw", "tn"),
                                                    HEAD_DIM + MLA_ROPE, HEAD_DIM)
        dcq_b, dgq = _rms_bwd(pb, MLA_Q_RANK, PB_CQ_BLK, p["mla_q_norm_g"][l], dcqn, "rms_q_bwd")
        grads["mla_q_norm_g"][l] = dgq[0]
        dkv_b = jnp.concatenate([dkm[:, :GROUP_W], dvm], axis=1).astype(MXU_DTYPE)
        dckvn = _matmul(dkv_b, r["wkv_up"], F32, "kv_up_dx", "nt")
        grads["w_mla_kv_up"][l] = _heads_interleaved(_matmul(r["ckvn"], dkv_b, F32, "kv_up_dw", "tn"),
                                                     2 * HEAD_DIM, HEAD_DIM)
        dckv_b, dgkv = _rms_bwd(pb, MLA_KV_RANK, PB_CKV_BLK, p["mla_kv_norm_g"][l], dckvn, "rms_kv_bwd")
        grads["mla_kv_norm_g"][l] = dgkv[0]
        dmisc_k = _rope_k_bwd(dkm[:, GROUP_W:], tabs, mats)

        dmq, dmk, dmv = _softmax_bwd(pairs(pa, 12), pairs(r["mkv"], 0), vals(r["mkv"], 2), o_mem, r["lse_mem"],
                                     dmixed, 3, s, nm, head_scale, False, None, "mem_bwd")
        dmkv_b = jnp.concatenate([dmk, dmv], axis=1).astype(MXU_DTYPE)
        grads["w_mem_kv"][l] = _matmul(memn_b, dmkv_b, F32, "mem_kv_dw", "tn")
        dmemn.append(_matmul(dmkv_b, p["w_mem_kv"][l], F32, "mem_kv_dx", "nt"))

        dmisc = jnp.concatenate([dmisc_f[:, :MISC_KROT], dmisc_k[:, MISC_KROT:MISC_KROT + MLA_ROPE],
                                 jnp.zeros((s, 128 - MISC_KROT - MLA_ROPE), F32)], axis=1)
        dp = jnp.concatenate([d.astype(MXU_DTYPE) for d in (dfq, dfk, dfv, dsq, dsk, dsv, dmq)]
                             + [dgate_b, dcq_b, dckv_b, dmisc.astype(MXU_DTYPE)], axis=1)
        dhproj = _matmul(dp, r["wp"], F32, "proj_dx", "nt")
        grads["w_in"][l] = _merge_dw_in(_matmul(r["hb"], dp, F32, "proj_dw", "tn"))
        dy1, dy2, c1 = du, dhproj, ALPHA

    dx, _, dg_in, db_in = _ln_bwd(x2, None, p["ln_in_g"], dy1, dy2, c1, "ln_in_bwd")
    _, _, dg_mem, db_mem = _ln_bwd(mem2, None, p["mem_ln_g"], dmemn[0], dmemn[1], 1.0, "ln_mem_bwd")
    out = {k: jnp.stack(v) for k, v in grads.items()}
    out.update(ln_in_g=dg_in[0], ln_in_b=db_in[0], mem_ln_g=dg_mem[0], mem_ln_b=db_mem[0])
    return loss_sum, dx, out


BIG_NAMES = ("w_in", "w_out", "w_mem_kv", "w_mla_q_up", "w_mla_kv_up")
BIG_AXIS = dict(w_in=2, w_out=1, w_mem_kv=1, w_mla_q_up=2, w_mla_kv_up=2)
SMALL_NAMES = ("ln_in_g", "ln_in_b", "mem_ln_g", "mem_ln_b", "ln_g", "ln_b", "b_forget", "mla_q_norm_g",
               "mla_kv_norm_g")
ALL_NAMES = ("ln_in_g", "ln_in_b", "mem_ln_g", "mem_ln_b", "w_in", "b_forget", "mla_q_norm_g", "w_mla_q_up",
             "mla_kv_norm_g", "w_mla_kv_up", "w_mem_kv", "w_out", "ln_g", "ln_b")
N_CHIPS = 4
N_DEV = 8


def _rows_of(shape):
    rows = -(-int(np.prod(shape)) // LANES)
    return -(-rows // PACK_ALIGN) * PACK_ALIGN


def _pack(arrs, rows):
    parts = []
    for a in arrs:
        f = a.reshape(-1)
        n = _rows_of(a.shape) * LANES
        parts.append(jnp.pad(f, (0, n - f.shape[0])).reshape(-1, LANES))
    used = sum(q.shape[0] for q in parts)
    if rows > used:
        parts.append(jnp.zeros((rows - used, LANES), parts[0].dtype))
    return jnp.concatenate(parts, axis=0)


def _unpack(buf, shapes):
    out, r = [], 0
    for shp in shapes:
        n = _rows_of(shp)
        out.append(buf[r:r + n].reshape(-1)[:int(np.prod(shp))].reshape(shp))
        r += n
    return out


HBM_SPEC = pl.BlockSpec(memory_space=pltpu.HBM)


def _gather_weights(shard):
    rows = shard.shape[0]

    def body(w_ref, out_ref, send_sems, recv_sems, local_sem):
        x, y, c = (lax.axis_index(a) for a in MESH_AXES)
        me = 2 * x + y
        chips = [(1 - x, y), (x, 1 - y), (1 - x, 1 - y)]

        def copy(k, block, to):
            return pltpu.make_async_remote_copy(
                src_ref=w_ref, dst_ref=out_ref.at[block], send_sem=send_sems.at[k], recv_sem=recv_sems.at[k],
                device_id=to, device_id_type=pl.DeviceIdType.MESH)

        mine = pltpu.make_async_copy(w_ref, out_ref.at[me], local_sem)
        mine.start()
        sends = [copy(k, me, (px, py, c)) for k, (px, py) in enumerate(chips)]
        for cp in sends:
            cp.start()
        for k, (px, py) in enumerate(chips):
            copy(k, 2 * px + py, (px, py, c)).wait_recv()
        for cp in sends:
            cp.wait_send()
        mine.wait()

    return pl.pallas_call(
        body, name="gather_weights", out_shape=jax.ShapeDtypeStruct((N_CHIPS, rows, LANES), shard.dtype),
        in_specs=[HBM_SPEC], out_specs=HBM_SPEC,
        scratch_shapes=[pltpu.SemaphoreType.DMA((3,)), pltpu.SemaphoreType.DMA((3,)), pltpu.SemaphoreType.DMA],
    )(shard)


def _exchange_grads(big, small):
    rb, rs = big.shape[1], small.shape[0]

    def body(big_ref, small_ref, obig_ref, osmall_ref, send_sems, recv_sems, local_sems):
        x, y, c = (lax.axis_index(a) for a in MESH_AXES)
        me = 4 * x + 2 * y + c
        flips = [(fx, fy, fc) for fx in (0, 1) for fy in (0, 1) for fc in (0, 1) if fx or fy or fc]
        peers = [(1 - x if fx else x, 1 - y if fy else y, 1 - c if fc else c) for fx, fy, fc in flips]

        def copy_big(k, src_chip, slot, to):
            return pltpu.make_async_remote_copy(
                src_ref=big_ref.at[src_chip], dst_ref=obig_ref.at[slot], send_sem=send_sems.at[k],
                recv_sem=recv_sems.at[k], device_id=to, device_id_type=pl.DeviceIdType.MESH)

        def copy_small(k, slot, to):
            return pltpu.make_async_remote_copy(
                src_ref=small_ref, dst_ref=osmall_ref.at[slot], send_sem=send_sems.at[7 + k],
                recv_sem=recv_sems.at[7 + k], device_id=to, device_id_type=pl.DeviceIdType.MESH)

        own_big = pltpu.make_async_copy(big_ref.at[2 * x + y], obig_ref.at[me], local_sems.at[0])
        own_small = pltpu.make_async_copy(small_ref, osmall_ref.at[me], local_sems.at[1])
        own_big.start()
        own_small.start()
        sends = []
        for k, (px, py, pc) in enumerate(peers):
            sends += [copy_big(k, 2 * px + py, me, (px, py, pc)), copy_small(k, me, (px, py, pc))]
        for cp in sends:
            cp.start()
        for k, (px, py, pc) in enumerate(peers):
            slot = 4 * px + 2 * py + pc
            copy_big(k, 2 * x + y, slot, (px, py, pc)).wait_recv()
            copy_small(k, slot, (px, py, pc)).wait_recv()
        for cp in sends:
            cp.wait_send()
        own_big.wait()
        own_small.wait()

    return pl.pallas_call(
        body, name="exchange_grads",
        out_shape=[jax.ShapeDtypeStruct((N_DEV, rb, LANES), big.dtype),
                   jax.ShapeDtypeStruct((N_DEV, rs, LANES), small.dtype)],
        in_specs=[HBM_SPEC, HBM_SPEC], out_specs=[HBM_SPEC, HBM_SPEC],
        scratch_shapes=[pltpu.SemaphoreType.DMA((14,)), pltpu.SemaphoreType.DMA((14,)),
                        pltpu.SemaphoreType.DMA((2,))],
    )(big, small)


def _adamw(parts, w, m, v, name):
    rows = w.shape[0]
    tile = _pick(rows, (128, 16, 8))
    bc1 = 1.0 - ADAM_B1 ** ADAM_STEP
    bc2 = 1.0 - ADAM_B2 ** ADAM_STEP

    def kern(p_ref, w_ref, m_ref, v_ref, g_ref, d_ref, nm_ref, nv_ref):
        g = p_ref[0].astype(F32)
        for d in range(1, N_DEV):
            g = g + p_ref[d].astype(F32)
        nm = ADAM_B1 * m_ref[...] + (1.0 - ADAM_B1) * g
        nv = ADAM_B2 * v_ref[...] + (1.0 - ADAM_B2) * (g * g)
        g_ref[...] = g
        nm_ref[...] = nm
        nv_ref[...] = nv
        d_ref[...] = -ADAM_LR * ((nm / bc1) / (jnp.sqrt(nv / bc2) + ADAM_EPS) + ADAM_WD * w_ref[...])

    spec = pl.BlockSpec((tile, LANES), lambda i: (i, 0))
    return pl.pallas_call(
        kern, name=name, grid=(rows // tile,),
        in_specs=[pl.BlockSpec((N_DEV, tile, LANES), lambda i: (0, i, 0)), spec, spec, spec],
        out_specs=[spec] * 4, out_shape=[jax.ShapeDtypeStruct((rows, LANES), F32)] * 4,
        compiler_params=pltpu.CompilerParams(dimension_semantics=("parallel",), vmem_limit_bytes=VMEM_LIMIT),
    )(parts, w, m, v)


def kernel(x, mem, ln_in_g, ln_in_b, mem_ln_g, mem_ln_b, w_in, b_forget, mla_q_norm_g, w_mla_q_up, mla_kv_norm_g, w_mla_kv_up, w_mem_kv, w_out, ln_g, ln_b, loss_target, m_ln_in_g, m_ln_in_b, m_mem_ln_g, m_mem_ln_b, m_w_in, m_b_forget, m_mla_q_norm_g, m_w_mla_q_up, m_mla_kv_norm_g, m_w_mla_kv_up, m_w_mem_kv, m_w_out, m_ln_g, m_ln_b, v_ln_in_g, v_ln_in_b, v_mem_ln_g, v_mem_ln_b, v_w_in, v_b_forget, v_mla_q_norm_g, v_w_mla_q_up, v_mla_kv_norm_g, v_w_mla_kv_up, v_w_mem_kv, v_w_out, v_ln_g, v_ln_b):
    w = dict(ln_in_g=ln_in_g, ln_in_b=ln_in_b, mem_ln_g=mem_ln_g, mem_ln_b=mem_ln_b, w_in=w_in, b_forget=b_forget,
             mla_q_norm_g=mla_q_norm_g, w_mla_q_up=w_mla_q_up, mla_kv_norm_g=mla_kv_norm_g,
             w_mla_kv_up=w_mla_kv_up, w_mem_kv=w_mem_kv, w_out=w_out, ln_g=ln_g, ln_b=ln_b)
    mo = dict(ln_in_g=m_ln_in_g, ln_in_b=m_ln_in_b, mem_ln_g=m_mem_ln_g, mem_ln_b=m_mem_ln_b, w_in=m_w_in,
              b_forget=m_b_forget, mla_q_norm_g=m_mla_q_norm_g, w_mla_q_up=m_w_mla_q_up,
              mla_kv_norm_g=m_mla_kv_norm_g, w_mla_kv_up=m_w_mla_kv_up, w_mem_kv=m_w_mem_kv, w_out=m_w_out,
              ln_g=m_ln_g, ln_b=m_ln_b)
    vo = dict(ln_in_g=v_ln_in_g, ln_in_b=v_ln_in_b, mem_ln_g=v_mem_ln_g, mem_ln_b=v_mem_ln_b, w_in=v_w_in,
              b_forget=v_b_forget, mla_q_norm_g=v_mla_q_norm_g, w_mla_q_up=v_w_mla_q_up,
              mla_kv_norm_g=v_mla_kv_norm_g, w_mla_kv_up=v_w_mla_kv_up, w_mem_kv=v_w_mem_kv, w_out=v_w_out,
              ln_g=v_ln_g, ln_b=v_ln_b)
    big_shapes = [w[n].shape for n in BIG_NAMES]
    small_shapes = [w[n].shape for n in SMALL_NAMES]

    gathered = _gather_weights(_pack([w[n].astype(MXU_DTYPE) for n in BIG_NAMES], BIG_ROWS))
    per_chip = [_unpack(gathered[j], big_shapes) for j in range(N_CHIPS)]
    full = dict(w)
    for idx, n in enumerate(BIG_NAMES):
        full[n] = jnp.concatenate([per_chip[j][idx] for j in range(N_CHIPS)], axis=BIG_AXIS[n])

    loss_sum, dx, g = _local_step(x[0], mem[0], loss_target[0], full)
    loss = lax.psum(loss_sum * (0.5 / D_MODEL), MESH_AXES)

    def shard_of(n, j):
        ax, size = BIG_AXIS[n], w[n].shape[BIG_AXIS[n]]
        return lax.slice_in_dim(g[n], j * size, (j + 1) * size, axis=ax)

    big = jnp.stack([_pack([shard_of(n, j).astype(MXU_DTYPE) for n in BIG_NAMES], BIG_ROWS) for j in range(N_CHIPS)])
    small = _pack([g[n] for n in SMALL_NAMES], SMALL_ROWS)
    big_parts, small_parts = _exchange_grads(big, small)

    res_big = _adamw(big_parts, _pack([w[n] for n in BIG_NAMES], BIG_ROWS), _pack([mo[n] for n in BIG_NAMES], BIG_ROWS),
                     _pack([vo[n] for n in BIG_NAMES], BIG_ROWS), "adamw_sharded")
    res_small = _adamw(small_parts, _pack([w[n] for n in SMALL_NAMES], SMALL_ROWS),
                       _pack([mo[n] for n in SMALL_NAMES], SMALL_ROWS), _pack([vo[n] for n in SMALL_NAMES], SMALL_ROWS),
                       "adamw_replicated")
    outs = []
    for kind in range(4):
        vals = dict(zip(BIG_NAMES, _unpack(res_big[kind], big_shapes)))
        vals.update(zip(SMALL_NAMES, _unpack(res_small[kind], small_shapes)))
        outs += [vals[n] for n in ALL_NAMES]
    return (loss, dx[None], *outs)
```

```python
import functools

import numpy as np
import jax
import jax.numpy as jnp
from jax import lax
from jax.experimental import pallas as pl
from jax.experimental.pallas import tpu as pltpu

F32 = jnp.float32
MXU_DTYPE = jnp.bfloat16

DEPTH = 2
D_MODEL = 1024
GROUP_W = 256
N_HEADS = 4
HEAD_DIM = 64
MLA_Q_RANK = 256
MLA_KV_RANK = 128
MLA_ROPE = 32
MLA_Q_COLS = N_HEADS * (HEAD_DIM + MLA_ROPE)
MLA_KV_COLS = N_HEADS * 2 * HEAD_DIM
ROPE_THETA = 10000.0
LN_EPS = 1e-5
RMS_EPS = 1e-6
ALPHA = (2 * DEPTH) ** 0.25
ADAM_LR, ADAM_B1, ADAM_B2, ADAM_EPS, ADAM_WD, ADAM_STEP = 0.001, 0.9, 0.999, 1e-08, 0.01, 10

_SPLIT = (256, 256, 256, 4, 256, 256, 256, 256, 128, 32, 256, 1024)
_OFF = [int(o) for o in np.cumsum((0,) + _SPLIT)]
IN_COLS = _OFF[-1]
PA_COLS = 7 * GROUP_W
PB_COLS = 1024 + 256 + 128 + 128
PB_CQ_BLK, PB_CKV_BLK, PB_MISC_BLK = 4, 10, 11
MISC_KROT = 4

LANES = 1024
PACK_ALIGN = 16
BIG_ROWS = 2560
SMALL_ROWS = 144
ROW_TILE = 256
ATT_TILE = 256
SB_QUERY_TILE = 512
SOFTMAX_TILE = 512
PAIR = 128
SB_SLOT = PAIR // N_HEADS
VMEM_LIMIT = 56 * 1024 * 1024
MATMUL_VMEM = 30 * 1024 * 1024
NEG = -1e30
EXP_UNDERFLOW = -104.0
DEAD_LOGIT = -110.0
REACH_SLACK = 1.0 + 2.0 ** -10
MESH_AXES = ("x", "y", "c")


def _dot(a, b):
    return jnp.dot(a, b, preferred_element_type=F32)


def _dot_nt(a, b):
    return lax.dot_general(a, b, (((1,), (1,)), ((), ())), preferred_element_type=F32)


def _dot_tn(a, b):
    return lax.dot_general(a, b, (((0,), (0,)), ((), ())), preferred_element_type=F32)


def _split2(x):
    hi = x.astype(MXU_DTYPE)
    lo = (x - hi.astype(F32)).astype(MXU_DTYPE)
    return hi, lo


def _split3(x):
    hi = x.astype(MXU_DTYPE)
    r = x - hi.astype(F32)
    mid = r.astype(MXU_DTYPE)
    lo = (r - mid.astype(F32)).astype(MXU_DTYPE)
    return hi, mid, lo


def _dot_exact_r(x, pm):
    hi, mid, lo = _split3(x)
    return _dot(hi, pm) + _dot(mid, pm) + _dot(lo, pm)


def _dot_exact_l(pm, x):
    hi, mid, lo = _split3(x)
    return _dot(pm, hi) + _dot(pm, mid) + _dot(pm, lo)


def _pick(dim, prefs):
    for p in prefs:
        if dim % p == 0:
            return p
    return dim


def _softplus(z):
    return jnp.maximum(z, 0.0) + jnp.log(1.0 + jnp.exp(-jnp.abs(z)))


def _tile_options(dim):
    opts = [d for d in range(128, min(dim, 2048) + 1, 128) if dim % d == 0]
    return opts or [dim]


def _matmul_tiles(m, n, k, out_bytes):
    tk = k if k <= 4096 else _pick(k, (1024, 512, 256, 128))
    best = None
    for tm in _tile_options(m):
        for tn in _tile_options(n):
            vmem = 2 * 2 * (tm * tk + tk * tn) + 4 * tm * tn + 2 * out_bytes * tm * tn
            if vmem <= MATMUL_VMEM and (best is None or tm * tn / (tm + tn) > best[0]):
                best = (tm * tn / (tm + tn), tm, tn)
    return best[1], best[2], tk


def _matmul(a, b, out_dtype, name, mode="nn"):
    m, k = (a.shape[1], a.shape[0]) if mode == "tn" else a.shape
    n = b.shape[0] if mode == "nt" else b.shape[1]
    tm, tn, tk = _matmul_tiles(m, n, k, jnp.dtype(out_dtype).itemsize)
    nk = k // tk
    dot = {"nn": _dot, "tn": _dot_tn, "nt": _dot_nt}[mode]

    def kern(a_ref, b_ref, o_ref, *acc):
        if nk == 1:
            o_ref[...] = dot(a_ref[...], b_ref[...]).astype(o_ref.dtype)
            return
        acc_ref, = acc
        kk = pl.program_id(2)

        @pl.when(kk == 0)
        def _():
            acc_ref[...] = jnp.zeros_like(acc_ref)

        acc_ref[...] += dot(a_ref[...], b_ref[...])

        @pl.when(kk == nk - 1)
        def _():
            o_ref[...] = acc_ref[...].astype(o_ref.dtype)

    a_spec = (pl.BlockSpec((tk, tm), lambda i, j, kk: (kk, i)) if mode == "tn"
              else pl.BlockSpec((tm, tk), lambda i, j, kk: (i, kk)))
    b_spec = (pl.BlockSpec((tn, tk), lambda i, j, kk: (j, kk)) if mode == "nt"
              else pl.BlockSpec((tk, tn), lambda i, j, kk: (kk, j)))
    return pl.pallas_call(
        kern, name=name, grid=(m // tm, n // tn, nk), in_specs=[a_spec, b_spec],
        out_specs=pl.BlockSpec((tm, tn), lambda i, j, kk: (i, j)),
        out_shape=jax.ShapeDtypeStruct((m, n), out_dtype),
        scratch_shapes=[pltpu.VMEM((tm, tn), F32)] if nk > 1 else [],
        compiler_params=pltpu.CompilerParams(
            dimension_semantics=("parallel", "parallel", "arbitrary"), vmem_limit_bytes=VMEM_LIMIT),
    )(a.astype(MXU_DTYPE), b.astype(MXU_DTYPE))


def _rowwise(body, name, rows, tile, row_ins, full_ins, row_outs, acc_outs=(), scratch=(),
             reverse=False, sequential=False):
    n = rows // tile

    def ridx(i):
        return (n - 1 - i) if reverse else i

    in_specs, args = [], []
    for arr, width, cb in row_ins:
        in_specs.append(pl.BlockSpec((tile, width), lambda i, cb=cb: (ridx(i), cb)))
        args.append(arr)
    for arr in full_ins:
        in_specs.append(pl.BlockSpec(arr.shape, lambda i, nd=arr.ndim: (0,) * nd))
        args.append(arr)
    out_shape = [jax.ShapeDtypeStruct((rows, w), dt) for w, dt in row_outs]
    out_shape += [jax.ShapeDtypeStruct(s, dt) for s, dt in acc_outs]
    out_specs = [pl.BlockSpec((tile, w), lambda i: (ridx(i), 0)) for w, dt in row_outs]
    out_specs += [pl.BlockSpec(s, lambda i, nd=len(s): (0,) * nd) for s, dt in acc_outs]

    def kern(*refs):
        body(pl.program_id(0), *refs)

    sem = "arbitrary" if (acc_outs or sequential) else "parallel"
    return pl.pallas_call(
        kern, name=name, grid=(n,), in_specs=in_specs, out_specs=out_specs, out_shape=out_shape,
        scratch_shapes=list(scratch),
        compiler_params=pltpu.CompilerParams(dimension_semantics=(sem,), vmem_limit_bytes=VMEM_LIMIT),
    )(*args)


def _ln_stats(u):
    mu = jnp.mean(u, axis=-1, keepdims=True)
    xc = u - mu
    var = jnp.mean(xc * xc, axis=-1, keepdims=True)
    return xc, lax.rsqrt(var + LN_EPS)


def _ln_fwd(a, b, g, beta, name):
    rows, d = a.shape
    has_b = b is not None

    def body(i, *refs):
        if has_b:
            a_ref, b_ref, g_ref, be_ref, h_ref, hb_ref = refs
            u = ALPHA * a_ref[...] + b_ref[...]
        else:
            a_ref, g_ref, be_ref, h_ref, hb_ref = refs
            u = a_ref[...]
        xc, rstd = _ln_stats(u)
        y = xc * rstd * g_ref[...] + be_ref[...]
        h_ref[...] = y
        hb_ref[...] = y.astype(hb_ref.dtype)

    row_ins = [(a, d, 0)] + ([(b, d, 0)] if has_b else [])
    return _rowwise(body, name, rows, min(ROW_TILE, rows), row_ins,
                    [g.reshape(1, d), beta.reshape(1, d)], [(d, F32), (d, MXU_DTYPE)])


def _ln_bwd(a, b, g, dy1, dy2, c1, name):
    rows, d = a.shape
    has_b = b is not None
    has_2 = dy2 is not None

    def body(i, *refs):
        refs = list(refs)
        a_ref = refs.pop(0)
        u = a_ref[...]
        if has_b:
            u = ALPHA * u + refs.pop(0)[...]
        dy = c1 * refs.pop(0)[...]
        if has_2:
            dy = dy + refs.pop(0)[...]
        g_ref, du_ref, dub_ref, dg_ref, db_ref = refs

        @pl.when(i == 0)
        def _():
            dg_ref[...] = jnp.zeros_like(dg_ref)
            db_ref[...] = jnp.zeros_like(db_ref)

        xc, rstd = _ln_stats(u)
        xhat = xc * rstd
        dxh = dy * g_ref[...]
        m1 = jnp.mean(dxh, axis=-1, keepdims=True)
        m2 = jnp.mean(dxh * xhat, axis=-1, keepdims=True)
        du = rstd * (dxh - m1 - xhat * m2)
        du_ref[...] = du
        dub_ref[...] = du.astype(dub_ref.dtype)
        dg_ref[...] += jnp.sum(dy * xhat, axis=0, keepdims=True)
        db_ref[...] += jnp.sum(dy, axis=0, keepdims=True)

    row_ins = [(a, d, 0)] + ([(b, d, 0)] if has_b else []) + [(dy1, d, 0)] + ([(dy2, d, 0)] if has_2 else [])
    return _rowwise(body, name, rows, min(ROW_TILE, rows), row_ins, [g.reshape(1, d)],
                    [(d, F32), (d, MXU_DTYPE)], [((1, d), F32), ((1, d), F32)])


def _loss_grad(h, target):
    rows, d = h.shape

    def body(i, h_ref, t_ref, dh_ref, acc_ref):
        @pl.when(i == 0)
        def _():
            acc_ref[...] = jnp.zeros_like(acc_ref)

        e = h_ref[...] - t_ref[...]
        dh_ref[...] = e * (1.0 / d)
        acc_ref[...] += jnp.sum(e * e, axis=0, keepdims=True)

    return _rowwise(body, "loss_grad", rows, ROW_TILE, [(h, d, 0), (target, d, 0)], [],
                    [(d, F32)], [((1, d), F32)])


def _gate_fwd(groups, pb):
    rows = pb.shape[0]
    w = GROUP_W * len(groups)

    def body(i, *refs):
        g = refs[4][...]
        mixed = jnp.concatenate([r[...] for r in refs[:4]], axis=1)
        refs[5][...] = (mixed * (g / (1.0 + jnp.exp(-g)))).astype(refs[5].dtype)

    return _rowwise(body, "gate_fwd", rows, ROW_TILE, [(o, GROUP_W, 0) for o in groups] + [(pb, w, 0)], [],
                    [(w, MXU_DTYPE)])[0]


def _gate_bwd(dgated, groups, pb):
    rows = pb.shape[0]
    w = GROUP_W * len(groups)

    def body(i, *refs):
        dg = refs[0][...]
        mixed = jnp.concatenate([r[...] for r in refs[1:5]], axis=1)
        g = refs[5][...]
        dm_ref, dgate_ref = refs[6], refs[7]
        sig = 1.0 / (1.0 + jnp.exp(-g))
        dm_ref[...] = dg * (g * sig)
        dgate_ref[...] = (dg * mixed * (sig * (1.0 + g * (1.0 - sig)))).astype(dgate_ref.dtype)

    return _rowwise(body, "gate_bwd", rows, ROW_TILE,
                    [(dgated, w, 0)] + [(o, GROUP_W, 0) for o in groups] + [(pb, w, 0)], [],
                    [(w, F32), (w, MXU_DTYPE)])


def _tri(n, kind):
    r = np.arange(n)[:, None]
    c = np.arange(n)[None, :]
    m = {"lower_incl": r >= c, "upper_incl": r <= c, "row_gt_col": r > c, "row_lt_col": r < c}[kind]
    return jnp.asarray(m.astype(np.float32), dtype=MXU_DTYPE)


def _forget_fwd(pb, bias_row):
    rows = pb.shape[0]
    tile = min(ROW_TILE, rows)

    def body(i, x_ref, b_ref, l_ref, o_ref, carry_ref):
        @pl.when(i == 0)
        def _():
            carry_ref[...] = jnp.zeros_like(carry_ref)

        xx = x_ref[...] + b_ref[...]
        lane = lax.broadcasted_iota(jnp.int32, xx.shape, 1)
        lf = jnp.where(lane < N_HEADS, -_softplus(-xx), 0.0)
        o_ref[...] = _dot_exact_l(l_ref[...], lf) + carry_ref[...]
        carry_ref[...] += jnp.sum(lf, axis=0, keepdims=True)

    return _rowwise(body, "forget_fwd", rows, tile, [(pb, 128, PB_MISC_BLK)],
                    [bias_row, _tri(tile, "lower_incl")], [(128, F32)],
                    scratch=[pltpu.VMEM((1, 128), F32)], sequential=True)[0]


def _forget_bwd(pb, bias_row, dfc):
    rows = pb.shape[0]
    tile = min(ROW_TILE, rows)

    def body(i, x_ref, df_ref, b_ref, u_ref, o_ref, db_ref, carry_ref):
        @pl.when(i == 0)
        def _():
            carry_ref[...] = jnp.zeros_like(carry_ref)
            db_ref[...] = jnp.zeros_like(db_ref)

        df = df_ref[...]
        sfx = _dot_exact_l(u_ref[...], df) + carry_ref[...]
        carry_ref[...] += jnp.sum(df, axis=0, keepdims=True)
        xx = x_ref[...] + b_ref[...]
        lane = lax.broadcasted_iota(jnp.int32, xx.shape, 1)
        dl = jnp.where(lane < N_HEADS, sfx / (1.0 + jnp.exp(xx)), 0.0)
        o_ref[...] = dl
        db_ref[...] += jnp.sum(dl, axis=0, keepdims=True)

    return _rowwise(body, "forget_bwd", rows, tile,
                    [(pb, 128, PB_MISC_BLK), (dfc, 128, 0)],
                    [bias_row, _tri(tile, "upper_incl")], [(128, F32)], [((1, 128), F32)],
                    scratch=[pltpu.VMEM((1, 128), F32)], reverse=True, sequential=True)


def _rope_tables(s):
    half = MLA_ROPE // 2
    inv_freq = ROPE_THETA ** (-jnp.arange(half, dtype=F32) / half)
    ang = jnp.arange(s).astype(F32)[:, None] * inv_freq[None, :]
    cos2 = jnp.tile(jnp.cos(ang), (1, 2))
    sin2 = jnp.tile(jnp.sin(ang), (1, 2))
    cx = jnp.tile(cos2, (1, N_HEADS))
    sx = jnp.tile(sin2, (1, N_HEADS))
    cq = jnp.concatenate([jnp.ones((s, GROUP_W), F32), cx], axis=1)
    sq = jnp.concatenate([jnp.zeros((s, GROUP_W), F32), sx], axis=1)
    pad = ((0, 0), (MISC_KROT, 128 - MISC_KROT - MLA_ROPE))
    ck = jnp.pad(cos2, pad)
    sk = jnp.pad(sin2, pad)
    return dict(cq=cq, sq=sq, ck=ck, sk=sk, cx=cx, sx=sx)


def _rot_matrix(width, bases):
    half = MLA_ROPE // 2
    p = np.zeros((width, width), np.float32)
    for b in bases:
        for i in range(half):
            p[b + half + i, b + i] = -1.0
            p[b + i, b + half + i] = 1.0
    return p


def _rope_matrices():
    pq = _rot_matrix(MLA_Q_COLS, [GROUP_W + h * MLA_ROPE for h in range(N_HEADS)])
    pk = _rot_matrix(128, [MISC_KROT])
    p4 = _rot_matrix(128, [h * MLA_ROPE for h in range(N_HEADS)])
    a = np.zeros((128, 128), np.float32)
    for h in range(N_HEADS):
        for r in range(MLA_ROPE):
            a[h * MLA_ROPE + r, MISC_KROT + r] = 1.0
    cast = lambda m: jnp.asarray(m, dtype=MXU_DTYPE)
    return dict(pq=cast(pq), pqt=cast(pq.T), pk=cast(pk), spread=cast(a.T), xa=cast(a), xb=cast(p4.T @ a))


def _rms(c, g):
    r = lax.rsqrt(jnp.mean(c * c, axis=-1, keepdims=True) + RMS_EPS)
    return c * r * g


def _mla_q_fwd(pb, g, w_up, tabs, mats):
    rows = pb.shape[0]

    def body(i, c_ref, cos_ref, sin_ref, g_ref, w_ref, p_ref, q_ref, cn_ref):
        cn = _rms(c_ref[...], g_ref[...]).astype(cn_ref.dtype)
        cn_ref[...] = cn
        q = _dot(cn, w_ref[...])
        q_ref[...] = (q * cos_ref[...] + _dot_exact_r(q, p_ref[...]) * sin_ref[...]).astype(q_ref.dtype)

    return _rowwise(body, "mla_q_fwd", rows, ROW_TILE,
                    [(pb, MLA_Q_RANK, PB_CQ_BLK), (tabs["cq"], MLA_Q_COLS, 0), (tabs["sq"], MLA_Q_COLS, 0)],
                    [g.reshape(1, MLA_Q_RANK), w_up.astype(MXU_DTYPE), mats["pq"]],
                    [(MLA_Q_COLS, MXU_DTYPE), (MLA_Q_RANK, MXU_DTYPE)])


def _mla_kv_fwd(pb, g, w_up, tabs, mats):
    rows = pb.shape[0]

    def body(i, c_ref, x_ref, cos_ref, sin_ref, g_ref, w_ref, p_ref, sp_ref, kv_ref, cn_ref, kr_ref):
        cn = _rms(c_ref[...], g_ref[...]).astype(cn_ref.dtype)
        cn_ref[...] = cn
        kv_ref[...] = _dot(cn, w_ref[...]).astype(kv_ref.dtype)
        xx = x_ref[...]
        kr = xx * cos_ref[...] + _dot_exact_r(xx, p_ref[...]) * sin_ref[...]
        kr_ref[...] = _dot_exact_r(kr, sp_ref[...]).astype(kr_ref.dtype)

    return _rowwise(body, "mla_kv_fwd", rows, ROW_TILE,
                    [(pb, MLA_KV_RANK, PB_CKV_BLK), (pb, 128, PB_MISC_BLK), (tabs["ck"], 128, 0), (tabs["sk"], 128, 0)],
                    [g.reshape(1, MLA_KV_RANK), w_up.astype(MXU_DTYPE), mats["pk"], mats["spread"]],
                    [(MLA_KV_COLS, MXU_DTYPE), (MLA_KV_RANK, MXU_DTYPE), (128, MXU_DTYPE)])


def _rope_q_bwd(dq_full, tabs, mats):
    rows, nq = dq_full.shape

    def body(i, d_ref, cos_ref, sin_ref, pt_ref, o_ref):
        d = d_ref[...]
        o_ref[...] = (d * cos_ref[...] + _dot_exact_r(d * sin_ref[...], pt_ref[...])).astype(o_ref.dtype)

    return _rowwise(body, "rope_q_bwd", rows, ROW_TILE,
                    [(dq_full, nq, 0), (tabs["cq"], nq, 0), (tabs["sq"], nq, 0)], [mats["pqt"]],
                    [(nq, MXU_DTYPE)])[0]


def _rope_k_bwd(dkr, tabs, mats):
    rows = dkr.shape[0]

    def body(i, d_ref, cos_ref, sin_ref, a_ref, b_ref, o_ref):
        d = d_ref[...]
        o_ref[...] = _dot_exact_r(d * cos_ref[...], a_ref[...]) + _dot_exact_r(d * sin_ref[...], b_ref[...])

    return _rowwise(body, "rope_k_bwd", rows, ROW_TILE,
                    [(dkr, 128, 0), (tabs["cx"], 128, 0), (tabs["sx"], 128, 0)], [mats["xa"], mats["xb"]],
                    [(128, F32)])[0]


def _rms_bwd(pb, width, col_blk, g, dy, name):
    rows = pb.shape[0]

    def body(i, c_ref, dy_ref, g_ref, dc_ref, dg_ref):
        @pl.when(i == 0)
        def _():
            dg_ref[...] = jnp.zeros_like(dg_ref)

        c = c_ref[...]
        dy = dy_ref[...]
        r = lax.rsqrt(jnp.mean(c * c, axis=-1, keepdims=True) + RMS_EPS)
        dyg = dy * g_ref[...]
        dc = r * dyg - c * (r * r * r) * jnp.mean(c * dyg, axis=-1, keepdims=True)
        dc_ref[...] = dc.astype(dc_ref.dtype)
        dg_ref[...] += jnp.sum(dy * c * r, axis=0, keepdims=True)

    return _rowwise(body, name, rows, ROW_TILE, [(pb, width, col_blk), (dy, width, 0)], [g.reshape(1, width)],
                    [(width, MXU_DTYPE)], [((1, width), F32)])


def _att_params(parallel):
    return pltpu.CompilerParams(dimension_semantics=("parallel" if parallel else "arbitrary",),
                                vmem_limit_bytes=VMEM_LIMIT)


def _blk_off(j, t):
    return j * t if isinstance(j, int) else pl.multiple_of(j * t, t)


def _causal_mask(t, strict):
    r = lax.broadcasted_iota(jnp.int32, (t, t), 0)
    c = lax.broadcasted_iota(jnp.int32, (t, t), 1)
    return (c < r) if strict else (c <= r)


def _lane_mask(kind, head, rows):
    lane = lax.broadcasted_iota(jnp.int32, (rows, PAIR), 1)
    if kind == "pair":
        return (lane < HEAD_DIM) if head % 2 == 0 else (lane >= HEAD_DIM)
    return (lane >= MLA_ROPE * head) & (lane < MLA_ROPE * (head + 1))


def _row_spec(t, cb, width=PAIR):
    return pl.BlockSpec((t, width), lambda i, cb=cb: (i, cb))


def _whole_spec(rows, cb, width=PAIR):
    return pl.BlockSpec((rows, width), lambda i, cb=cb: (0, cb), pipeline_mode=pl.Buffered(1))


def _is_pow2(x):
    return float(np.frexp(x)[0]) == 0.5


def _masked_heads(blocks, kinds, pair, dtype, scale=None):
    out = []
    for e in range(2):
        head = 2 * pair + e
        parts = [jnp.where(_lane_mask(k, head, b.shape[0]), b.astype(F32) * (1.0 if scale is None else scale),
                           0.0).astype(dtype)
                 for b, k in zip(blocks, kinds)]
        out.append(parts[0] if len(parts) == 1 else jnp.concatenate(parts, axis=1))
    return out


def _logit_reach(qh, kmax2, head):
    q32 = qh.astype(F32)
    return jnp.sqrt(jnp.sum(q32 * q32, axis=1, keepdims=True) * _col(kmax2, head)) * REACH_SLACK


def _forget_top(ft_ref, head, off):
    return jnp.max(-ft_ref[head:head + 1, pl.ds(off, PAIR)])


def _col(block, idx):
    lane = lax.broadcasted_iota(jnp.int32, block.shape, 1)
    return jnp.sum(jnp.where(lane == idx, block, 0.0), axis=1, keepdims=True)


def _scatter_cols(cols, t):
    lane = lax.broadcasted_iota(jnp.int32, (t, PAIR), 1)
    out = jnp.zeros((t, PAIR), F32)
    for idx, c in cols.items():
        out = out + jnp.where(lane == idx, c, 0.0)
    return out


def _take_heads(per_head, pair):
    return jnp.where(_lane_mask("pair", 0, per_head[0].shape[0]), per_head[0], per_head[1])


class _Parts:
    def __init__(self, q_parts, k_parts, v_parts, tq, sk):
        self.kinds = [[kind for _, _, kind in q_parts[p]] for p in range(2)]
        self.nparts = len(q_parts[0])
        self.q_specs = [_row_spec(tq, cb) for p in range(2) for _, cb, _ in q_parts[p]]
        self.q_args = [a for p in range(2) for a, _, _ in q_parts[p]]
        self.k_specs = [_whole_spec(sk, cb) for p in range(2) for _, cb, _ in k_parts[p]]
        self.k_args = [a for p in range(2) for a, _, _ in k_parts[p]]
        self.v_specs = [_whole_spec(sk, cb) for _, cb in v_parts]
        self.v_args = [a for a, _ in v_parts]
        self.width = PAIR * self.nparts

    def split(self, refs):
        n = self.nparts
        refs = list(refs)
        q = [refs[p * n:(p + 1) * n] for p in range(2)]
        k = [refs[2 * n + p * n:2 * n + (p + 1) * n] for p in range(2)]
        v = refs[4 * n:4 * n + 2]
        return q, k, v, refs[4 * n + 2:]

    def k_block(self, k_refs, off, t):
        blks = [r[pl.ds(off, t), :] for r in k_refs]
        return blks[0] if len(blks) == 1 else jnp.concatenate(blks, axis=1)


def _key_norm_max(src, k_blk):
    rows = src.shape[0]

    def body(i, k0_ref, k1_ref, o_ref):
        @pl.when(i == 0)
        def _():
            o_ref[...] = jnp.zeros_like(o_ref)

        cols = {}
        for p, ref in enumerate((k0_ref, k1_ref)):
            k32 = ref[...].astype(F32)
            for e in range(2):
                sq = jnp.sum(jnp.where(_lane_mask("pair", e, k32.shape[0]), k32 * k32, 0.0), axis=1, keepdims=True)
                cols[2 * p + e] = jnp.max(sq, axis=0, keepdims=True)
        lane = lax.broadcasted_iota(jnp.int32, (1, PAIR), 1)
        o_ref[...] = jnp.maximum(o_ref[...], sum(jnp.where(lane == h, c, 0.0) for h, c in cols.items()))

    return _rowwise(body, "key_norm_max", rows, ROW_TILE, [(src, PAIR, k_blk), (src, PAIR, k_blk + 1)], [], [],
                    [((1, PAIR), F32)])[0]


def _softmax_fwd(q_parts, k_parts, v_parts, sq, sk, scale, causal, bias, name):
    tq = min(SOFTMAX_TILE, sq)
    tk = tq if causal else min(SOFTMAX_TILE, sk)
    nkv = sk // tk
    pp = _Parts(q_parts, k_parts, v_parts, tq, sk)

    def kern(*refs):
        q_refs, k_refs, v_refs, rest = pp.split(refs)
        if bias is not None:
            fc_ref, ft_ref, kmax_ref, o_ref, lse_ref = rest
            fcb = fc_ref[...]
        else:
            o_ref, lse_ref = rest
        i = pl.program_id(0)
        fold = _is_pow2(scale)
        head_on = [jnp.where(_lane_mask("pair", e, tk), 1.0, 0.0).astype(MXU_DTYPE) for e in range(2)]
        head_off = [jnp.where(_lane_mask("pair", e, tk), 0.0, 1.0).astype(MXU_DTYPE) for e in range(2)]
        lse_cols = {}
        for p in range(2):
            qm = _masked_heads([r[...] for r in q_refs[p]], pp.kinds[p], p, MXU_DTYPE, scale if fold else None)
            if bias is not None:
                reach = [_logit_reach(qm[e], kmax_ref[...], 2 * p + e) for e in range(2)]

            def block(j, carry, masked, p=p, qm=qm):
                off = _blk_off(j, tk)
                kb = pp.k_block(k_refs[p], off, tk)
                vb = v_refs[p][pl.ds(off, tk), :]
                out = []
                for e in range(2):
                    h = 2 * p + e
                    m, acc = carry[e]
                    s = _dot_nt(qm[e], kb)
                    if not fold:
                        s = s * scale
                    if bias is not None:
                        s = s - ft_ref[h:h + 1, pl.ds(off, tk)]
                    if masked:
                        s = jnp.where(_causal_mask(tq, False), s, NEG)
                    m_new = jnp.maximum(m, jnp.max(s, axis=1, keepdims=True))
                    pr = jnp.exp(s - m_new).astype(MXU_DTYPE)
                    out.append((m_new, jnp.exp(m - m_new) * acc + _dot(pr, vb * head_on[e] + head_off[e])))
                return tuple(out)

            carry = tuple((jnp.full((tq, 1), NEG, F32), jnp.zeros((tq, PAIR), F32)) for _ in range(2))
            if causal and bias is not None:
                def alive(c, j, p=p, reach=reach):
                    top = [_forget_top(ft_ref, 2 * p + e, _blk_off(j, tk)) for e in range(2)]
                    return functools.reduce(jnp.maximum, [jnp.max(reach[e] + top[e] - c[e][0]) for e in range(2)])

                def step(state, block=block, alive=alive):
                    n, _, c = state
                    c = block(i - 1 - n, c, False)
                    return n + 1, alive(c, i - 1 - n), c

                carry = block(i, carry, True)
                _, _, carry = lax.while_loop(lambda st: jnp.logical_and(st[0] < i, st[1] > DEAD_LOGIT), step,
                                             (jnp.int32(0), alive(carry, i), carry))
            elif causal:
                carry = lax.fori_loop(0, i, lambda j, c, block=block: block(j, c, False), carry)
                carry = block(i, carry, True)
            else:
                for j in range(nkv):
                    carry = block(j, carry, False)
            outs = []
            for e in range(2):
                m, acc = carry[e]
                l = _col(acc, HEAD_DIM * (1 - e))
                outs.append(acc / l)
                lse_cols[2 * p + e] = m + jnp.log(l) + (_col(fcb, 2 * p + e) if bias is not None else 0.0)
            o_ref[:, p * PAIR:(p + 1) * PAIR] = _take_heads(outs, p)
        lse_ref[...] = _scatter_cols(lse_cols, tq)

    in_specs = pp.q_specs + pp.k_specs + pp.v_specs
    args = pp.q_args + pp.k_args + pp.v_args
    if bias is not None:
        in_specs += [_row_spec(tq, 0), pl.BlockSpec((8, sk), lambda i: (0, 0), pipeline_mode=pl.Buffered(1)),
                     pl.BlockSpec((1, PAIR), lambda i: (0, 0))]
        args += list(bias)
    return pl.pallas_call(
        kern, name=name, grid=(sq // tq,), in_specs=in_specs,
        out_specs=[_row_spec(tq, 0, GROUP_W), _row_spec(tq, 0)],
        out_shape=[jax.ShapeDtypeStruct((sq, GROUP_W), F32), jax.ShapeDtypeStruct((sq, PAIR), F32)],
        compiler_params=_att_params(True),
    )(*args)


def _softmax_bwd(q_parts, k_parts, v_parts, o, lse, do, do_blk, sq, sk, scale, causal, bias, name):
    tq = min(SOFTMAX_TILE, sq)
    tk = tq if causal else min(SOFTMAX_TILE, sk)
    nkv = sk // tk
    pp = _Parts(q_parts, k_parts, v_parts, tq, sk)
    quad = pp.nparts == 2
    wq = GROUP_W + (PAIR if quad else 0)

    def kern(*refs):
        q_refs, k_refs, v_refs, rest = pp.split(refs)
        if bias is not None:
            o_ref, lse_ref, do_ref, fc_ref, ft_ref, kmax_ref, dq_ref, dk_ref, dv_ref, dfq_ref, dfk_ref = rest
            fcb = fc_ref[...]
        else:
            o_ref, lse_ref, do_ref, dq_ref, dk_ref, dv_ref = rest
        i = pl.program_id(0)
        fold = _is_pow2(scale)

        @pl.when(i == 0)
        def _():
            dk_ref[...] = jnp.zeros_like(dk_ref)
            dv_ref[...] = jnp.zeros_like(dv_ref)
            if bias is not None:
                dfk_ref[...] = jnp.zeros_like(dfk_ref)

        lse_b = lse_ref[...]
        qm, dom, delta, lse_h = [], [], [], []
        for p in range(2):
            qm += _masked_heads([r[...] for r in q_refs[p]], pp.kinds[p], p, MXU_DTYPE, scale if fold else None)
            do_p = do_ref[:, p * PAIR:(p + 1) * PAIR]
            dom += _masked_heads([do_p], ["pair"], p, MXU_DTYPE)
            prod = do_p * o_ref[:, p * PAIR:(p + 1) * PAIR]
            for e in range(2):
                h = 2 * p + e
                delta.append(jnp.sum(jnp.where(_lane_mask("pair", h, tq), prod, 0.0), axis=1, keepdims=True))
                lse_h.append(_col(lse_b, h) - (_col(fcb, h) if bias is not None else 0.0))

        def block(j, carry, masked):
            off = _blk_off(j, tk)
            out = []
            for p in range(2):
                kb = pp.k_block(k_refs[p], off, tk)
                vb = v_refs[p][pl.ds(off, tk), :]
                dk_acc = jnp.zeros((tk, pp.width), F32)
                dv_acc = jnp.zeros((tk, PAIR), F32)
                for e in range(2):
                    h = 2 * p + e
                    dq, dfq = carry[h]
                    s = _dot_nt(qm[h], kb)
                    if not fold:
                        s = s * scale
                    if bias is not None:
                        s = s - ft_ref[h:h + 1, pl.ds(off, tk)]
                    if masked:
                        s = jnp.where(_causal_mask(tq, False), s, NEG)
                    pr = jnp.exp(s - lse_h[h])
                    ds = pr * (_dot_nt(dom[h], vb) - delta[h])
                    dsb = (ds if fold else ds * scale).astype(MXU_DTYPE)
                    dv_acc = dv_acc + _dot_tn(pr.astype(MXU_DTYPE), dom[h])
                    dk_acc = dk_acc + _dot_tn(dsb, qm[h])
                    dq = dq + _dot(dsb, kb)
                    if bias is not None:
                        dfq = dfq + jnp.sum(ds, axis=1, keepdims=True)
                        dfk_ref[h:h + 1, pl.ds(off, tk)] -= jnp.sum(ds, axis=0, keepdims=True)
                    out.append((dq, dfq))
                dv_ref[pl.ds(off, tk), p * PAIR:(p + 1) * PAIR] += dv_acc
                dk_ref[pl.ds(off, tk), p * PAIR:(p + 1) * PAIR] += dk_acc[:, :PAIR]
                if quad:
                    dk_ref[pl.ds(off, tk), GROUP_W:] += dk_acc[:, PAIR:]
            return tuple(out)

        carry = tuple((jnp.zeros((tq, pp.width), F32), jnp.zeros((tq, 1), F32)) for _ in range(N_HEADS))
        if causal and bias is not None:
            reach = [_logit_reach(qm[h], kmax_ref[...], h) - lse_h[h] for h in range(N_HEADS)]

            def alive(j):
                return functools.reduce(jnp.maximum, [jnp.max(reach[h] + _forget_top(ft_ref, h, _blk_off(j, tk)))
                                                      for h in range(N_HEADS)])

            def step(state):
                n, _, c = state
                return n + 1, alive(i - 1 - n), block(i - 1 - n, c, False)

            carry = block(i, carry, True)
            _, _, carry = lax.while_loop(lambda st: jnp.logical_and(st[0] < i, st[1] > DEAD_LOGIT), step,
                                         (jnp.int32(0), alive(i), carry))
        elif causal:
            carry = lax.fori_loop(0, i, lambda j, c: block(j, c, False), carry)
            carry = block(i, carry, True)
        else:
            for j in range(nkv):
                carry = block(j, carry, False)
        dqs = [c[0] * scale if fold else c[0] for c in carry]
        for p in range(2):
            dq_ref[:, p * PAIR:(p + 1) * PAIR] = _take_heads([dqs[2 * p + e][:, :PAIR] for e in range(2)], p)
        if quad:
            dq_ref[:, GROUP_W:] = sum(jnp.where(_lane_mask("quad", h, tq), dqs[h][:, PAIR:], 0.0)
                                      for h in range(N_HEADS))
        if bias is not None:
            dfq_ref[...] = _scatter_cols({h: carry[h][1] for h in range(N_HEADS)}, tq)

    acc_spec = lambda rows, width: pl.BlockSpec((rows, width), lambda i: (0, 0), pipeline_mode=pl.Buffered(1))
    in_specs = pp.q_specs + pp.k_specs + pp.v_specs + [_row_spec(tq, 0, GROUP_W), _row_spec(tq, 0),
                                                       _row_spec(tq, do_blk, GROUP_W)]
    args = pp.q_args + pp.k_args + pp.v_args + [o, lse, do]
    out_specs = [_row_spec(tq, 0, wq), acc_spec(sk, wq), acc_spec(sk, GROUP_W)]
    out_shape = [jax.ShapeDtypeStruct((sq, wq), F32), jax.ShapeDtypeStruct((sk, wq), F32),
                 jax.ShapeDtypeStruct((sk, GROUP_W), F32)]
    if bias is not None:
        in_specs += [_row_spec(tq, 0), pl.BlockSpec((8, sk), lambda i: (0, 0), pipeline_mode=pl.Buffered(1)),
                     pl.BlockSpec((1, PAIR), lambda i: (0, 0))]
        args += list(bias)
        out_specs += [_row_spec(tq, 0), acc_spec(8, sk)]
        out_shape += [jax.ShapeDtypeStruct((sq, PAIR), F32), jax.ShapeDtypeStruct((8, sk), F32)]
    return pl.pallas_call(
        kern, name=name, grid=(sq // tq,), in_specs=in_specs, out_specs=out_specs, out_shape=out_shape,
        compiler_params=_att_params(False),
    )(*args)


def _sb_logs(qh, kb, valid):
    z = _dot_nt(qh, kb)
    sp = _softplus(z)
    lk = -sp
    if valid is not None:
        lk = jnp.where(valid, lk, 0.0)
    return lk, z - sp


def _sb_valid(d, tq, tk):
    r = lax.broadcasted_iota(jnp.int32, (tq, tk), 0)
    c = lax.broadcasted_iota(jnp.int32, (tq, tk), 1)
    return c + d * tk < r


def _tri_sums(xs, tri):
    t = xs[0].shape[0]
    pieces = [_split2(x) for x in xs]
    hi = _dot(jnp.concatenate([pc[0] for pc in pieces], axis=0), tri)
    lo = _dot(jnp.concatenate([pc[1] for pc in pieces], axis=0), tri)
    return [hi[n * t:(n + 1) * t] + lo[n * t:(n + 1) * t] for n in range(len(xs))]


def _sb_fwd(src, q_blk, k_blk, v_blk, s, scale, name):
    assert _is_pow2(scale) and s // ATT_TILE <= SB_SLOT
    tq, t = min(SB_QUERY_TILE, s), min(ATT_TILE, s)
    band = tq // t
    pair = lambda blk: [[(src, blk + p, "pair")] for p in range(2)]
    pp = _Parts(pair(q_blk), pair(k_blk), [(src, v_blk + p) for p in range(2)], tq, s)

    def kern(*refs):
        q_refs, k_refs, v_refs, (tri_ref, o_ref, rm_ref, cnt_ref) = pp.split(refs)
        i = pl.program_id(0)
        tri = tri_ref[...]
        lane = lax.broadcasted_iota(jnp.int32, (tq, PAIR), 1)
        qm = []
        for p in range(2):
            qm += _masked_heads([q_refs[p][0][...]], ["pair"], p, MXU_DTYPE, scale)

        def block(j, carry, valid):
            accs, rights, rm = carry
            off = _blk_off(j, t)
            kbs = [k_refs[p][0][pl.ds(off, t), :] for p in range(2)]
            vbs = [v_refs[p][pl.ds(off, t), :] for p in range(2)]
            logs = [_sb_logs(qm[h], kbs[h // 2], valid) for h in range(N_HEADS)]
            tails = _tri_sums([lg[0] for lg in logs], tri)
            new_acc, new_right = [], []
            for h in range(N_HEADS):
                lk, ls = logs[h]
                w = jnp.exp(ls + tails[h] + rights[h])
                if valid is not None:
                    w = jnp.where(valid, w, 0.0)
                new_acc.append(accs[h] + _dot(w.astype(MXU_DTYPE), vbs[h // 2]))
                rm = rm + jnp.where(lane == SB_SLOT * h + j, rights[h], 0.0)
                new_right.append(rights[h] + jnp.sum(lk, axis=1, keepdims=True))
            return tuple(new_acc), tuple(new_right), rm

        carry = (tuple(jnp.zeros((tq, PAIR), F32) for _ in range(N_HEADS)),
                 tuple(jnp.zeros((tq, 1), F32) for _ in range(N_HEADS)), jnp.zeros((tq, PAIR), F32))
        for d in reversed(range(band)):
            carry = block(band * i + d, carry, _sb_valid(d, tq, t))

        def alive(c):
            return functools.reduce(jnp.maximum, [jnp.max(r) for r in c[1]])

        def step(state):
            n, _, c = state
            c = block(band * i - 1 - n, c, None)
            return n + 1, alive(c), c

        n_done, _, carry = lax.while_loop(lambda st: jnp.logical_and(st[0] < band * i, st[1] > EXP_UNDERFLOW),
                                          step, (jnp.int32(0), alive(carry), carry))
        cnt_ref[i] = n_done
        for p in range(2):
            o_ref[:, p * PAIR:(p + 1) * PAIR] = _take_heads([carry[0][2 * p + e] for e in range(2)], p)
        rm_ref[...] = carry[2]

    return pl.pallas_call(
        kern, name=name, grid=(s // tq,),
        in_specs=pp.q_specs + pp.k_specs + pp.v_specs + [pl.BlockSpec((t, t), lambda i: (0, 0))],
        out_specs=[_row_spec(tq, 0, GROUP_W), _row_spec(tq, 0), pl.BlockSpec(memory_space=pltpu.SMEM)],
        out_shape=[jax.ShapeDtypeStruct((s, GROUP_W), F32), jax.ShapeDtypeStruct((s, PAIR), F32),
                   jax.ShapeDtypeStruct((s // tq,), jnp.int32)],
        compiler_params=_att_params(False),
    )(*(pp.q_args + pp.k_args + pp.v_args + [_tri(t, "row_gt_col")]))


def _sb_bwd(src, q_blk, k_blk, v_blk, do, do_blk, rm, visited, s, scale, name):
    assert _is_pow2(scale)
    tq, t = min(SB_QUERY_TILE, s), min(ATT_TILE, s)
    band = tq // t
    pair = lambda blk: [[(src, blk + p, "pair")] for p in range(2)]
    pp = _Parts(pair(q_blk), pair(k_blk), [(src, v_blk + p) for p in range(2)], tq, s)

    def kern(*refs):
        q_refs, k_refs, v_refs, (do_ref, rm_ref, tri_ref, pre_ref, cnt_ref, dq_ref, dk_ref, dv_ref) = pp.split(refs)
        i = pl.program_id(0)

        @pl.when(i == 0)
        def _():
            dk_ref[...] = jnp.zeros_like(dk_ref)
            dv_ref[...] = jnp.zeros_like(dv_ref)

        rmb = rm_ref[...]
        tri = tri_ref[...]
        pre = pre_ref[...]
        qm, dom = [], []
        for p in range(2):
            qm += _masked_heads([q_refs[p][0][...]], ["pair"], p, MXU_DTYPE, scale)
            dom += _masked_heads([do_ref[:, p * PAIR:(p + 1) * PAIR]], ["pair"], p, MXU_DTYPE)

        def block(j, carry, valid):
            dqs, lefts = carry
            off = _blk_off(j, t)
            kbs = [k_refs[p][0][pl.ds(off, t), :] for p in range(2)]
            vbs = [v_refs[p][pl.ds(off, t), :] for p in range(2)]
            logs = [_sb_logs(qm[h], kbs[h // 2], valid) for h in range(N_HEADS)]
            tails = _tri_sums([lg[0] for lg in logs], tri)
            ws, gs = [], []
            for h in range(N_HEADS):
                lk, ls = logs[h]
                w = jnp.exp(ls + tails[h] + _col(rmb, SB_SLOT * h + j))
                if valid is not None:
                    w = jnp.where(valid, w, 0.0)
                ws.append(w)
                gs.append(_dot_nt(dom[h], vbs[h // 2]) * w)
            prefix = _tri_sums(gs, pre)
            new_dq, new_left = [], []
            dk_acc = [jnp.zeros((t, PAIR), F32) for _ in range(2)]
            dv_acc = [jnp.zeros((t, PAIR), F32) for _ in range(2)]
            for h in range(N_HEADS):
                lk, ls = logs[h]
                sig = jnp.exp(ls)
                dz = gs[h] * (1.0 - sig) - sig * (prefix[h] + lefts[h])
                if valid is not None:
                    dz = jnp.where(valid, dz, 0.0)
                dzb = dz.astype(MXU_DTYPE)
                dv_acc[h // 2] = dv_acc[h // 2] + _dot_tn(ws[h].astype(MXU_DTYPE), dom[h])
                dk_acc[h // 2] = dk_acc[h // 2] + _dot_tn(dzb, qm[h])
                new_dq.append(dqs[h] + _dot(dzb, kbs[h // 2]))
                new_left.append(lefts[h] + jnp.sum(gs[h], axis=1, keepdims=True))
            for p in range(2):
                dv_ref[pl.ds(off, t), p * PAIR:(p + 1) * PAIR] += dv_acc[p]
                dk_ref[pl.ds(off, t), p * PAIR:(p + 1) * PAIR] += dk_acc[p]
            return tuple(new_dq), tuple(new_left)

        carry = (tuple(jnp.zeros((tq, PAIR), F32) for _ in range(N_HEADS)),
                 tuple(jnp.zeros((tq, 1), F32) for _ in range(N_HEADS)))
        carry = lax.fori_loop(band * i - cnt_ref[i], band * i, lambda j, c: block(j, c, None), carry)
        for d in range(band):
            carry = block(band * i + d, carry, _sb_valid(d, tq, t))
        for p in range(2):
            dq_ref[:, p * PAIR:(p + 1) * PAIR] = _take_heads([carry[0][2 * p + e] * scale for e in range(2)], p)

    mspec = pl.BlockSpec((t, t), lambda i: (0, 0))
    acc_spec = pl.BlockSpec((s, GROUP_W), lambda i: (0, 0), pipeline_mode=pl.Buffered(1))
    return pl.pallas_call(
        kern, name=name, grid=(s // tq,),
        in_specs=pp.q_specs + pp.k_specs + pp.v_specs + [_row_spec(tq, do_blk, GROUP_W), _row_spec(tq, 0), mspec, mspec,
                                                         pl.BlockSpec(memory_space=pltpu.SMEM)],
        out_specs=[_row_spec(tq, 0, GROUP_W), acc_spec, acc_spec],
        out_shape=[jax.ShapeDtypeStruct((s, GROUP_W), F32)] * 3,
        compiler_params=_att_params(False),
    )(*(pp.q_args + pp.k_args + pp.v_args + [do, rm, _tri(t, "row_gt_col"), _tri(t, "row_lt_col"), visited]))


def _split_w_in(w):
    col = lambda n: w[:, _OFF[n]:_OFF[n + 1]]
    wa = jnp.concatenate([col(0), col(1), col(2), col(4), col(5), col(6), col(10)], axis=1)
    misc = jnp.concatenate([col(3), col(9), jnp.zeros((w.shape[0], 128 - 4 - MLA_ROPE), w.dtype)], axis=1)
    wb = jnp.concatenate([col(11), col(7), col(8), misc], axis=1)
    return wa, wb


def _merge_dw_in(dwp):
    a = lambda n: dwp[:, n * GROUP_W:(n + 1) * GROUP_W]
    b0 = PA_COLS
    gate = dwp[:, b0:b0 + 1024]
    cq = dwp[:, b0 + 1024:b0 + 1280]
    ckv = dwp[:, b0 + 1280:b0 + 1408]
    flog = dwp[:, b0 + 1408:b0 + 1412]
    krot = dwp[:, b0 + 1408 + MISC_KROT:b0 + 1408 + MISC_KROT + MLA_ROPE]
    return jnp.concatenate([a(0), a(1), a(2), flog, a(3), a(4), a(5), cq, ckv, krot, a(6), gate], axis=1)


def _heads_first(w, per_head, first):
    r = w.shape[0]
    w3 = w.reshape(r, N_HEADS, per_head)
    return jnp.concatenate([w3[:, :, :first].reshape(r, -1), w3[:, :, first:].reshape(r, -1)], axis=1)


def _heads_interleaved(w, per_head, first):
    r = w.shape[0]
    a = w[:, :N_HEADS * first].reshape(r, N_HEADS, first)
    b = w[:, N_HEADS * first:].reshape(r, N_HEADS, per_head - first)
    return jnp.concatenate([a, b], axis=2).reshape(r, N_HEADS * per_head)


def _pad_rows8(a):
    return a[:, :8].T


def _local_step(x2, mem2, tgt, p):
    s = x2.shape[0]
    nm = mem2.shape[0]
    head_scale = HEAD_DIM ** -0.5
    mla_scale = (HEAD_DIM + MLA_ROPE) ** -0.5
    tabs = _rope_tables(s)
    mats = _rope_matrices()
    pairs = lambda arr, blk: [[(arr, blk + q, "pair")] for q in range(2)]
    vals = lambda arr, blk: [(arr, blk + q) for q in range(2)]

    h, hb = _ln_fwd(x2, None, p["ln_in_g"], p["ln_in_b"], "ln_in_fwd")
    _, memn_b = _ln_fwd(mem2, None, p["mem_ln_g"], p["mem_ln_b"], "ln_mem_fwd")

    saved = []
    for l in range(DEPTH):
        wa, wb = _split_w_in(p["w_in"][l])
        wp = jnp.concatenate([wa, wb], axis=1)
        wq_up = _heads_first(p["w_mla_q_up"][l], HEAD_DIM + MLA_ROPE, HEAD_DIM)
        wkv_up = _heads_first(p["w_mla_kv_up"][l], 2 * HEAD_DIM, HEAD_DIM)
        bias_row = jnp.pad(p["b_forget"][l], (0, 128 - N_HEADS)).reshape(1, 128)
        pa = _matmul(hb, wa, MXU_DTYPE, "proj_a")
        pb = _matmul(hb, wb, F32, "proj_b")

        fc = _forget_fwd(pb, bias_row)
        fbias = (fc, _pad_rows8(fc), _key_norm_max(pa, 2))
        o_fox, lse_fox = _softmax_fwd(pairs(pa, 0), pairs(pa, 2), vals(pa, 4), s, s, head_scale, True, fbias,
                                      "fox_fwd")
        o_sb, *rm_sb = _sb_fwd(pa, 6, 8, 10, s, head_scale, "sb_fwd")

        qfull, cqn = _mla_q_fwd(pb, p["mla_q_norm_g"][l], wq_up, tabs, mats)
        kv, ckvn, kr4 = _mla_kv_fwd(pb, p["mla_kv_norm_g"][l], wkv_up, tabs, mats)
        mla_q = [[(qfull, q, "pair"), (qfull, 2, "quad")] for q in range(2)]
        mla_k = [[(kv, q, "pair"), (kr4, 0, "quad")] for q in range(2)]
        o_mla, lse_mla = _softmax_fwd(mla_q, mla_k, vals(kv, 2), s, s, mla_scale, True, None, "mla_fwd")

        mkv = _matmul(memn_b, p["w_mem_kv"][l], MXU_DTYPE, "mem_kv")
        o_mem, lse_mem = _softmax_fwd(pairs(pa, 12), pairs(mkv, 0), vals(mkv, 2), s, nm, head_scale, False, None,
                                      "mem_fwd")

        groups = (o_fox, o_sb, o_mla, o_mem)
        gated = _gate_fwd(groups, pb)
        y = _matmul(gated, p["w_out"][l], F32, "out_proj")
        saved.append(dict(h=h, hb=hb, y=y, wp=wp, wq_up=wq_up, wkv_up=wkv_up, bias_row=bias_row, pa=pa, pb=pb,
                          fbias=fbias, lse_fox=lse_fox, rm_sb=rm_sb, cqn=cqn, ckvn=ckvn, mla_q=mla_q, mla_k=mla_k,
                          kv=kv, lse_mla=lse_mla, mkv=mkv, lse_mem=lse_mem, groups=groups, gated=gated))
        h, hb = _ln_fwd(h, y, p["ln_g"][l], p["ln_b"][l], "ln_fwd")

    dh, sq_cols = _loss_grad(h, tgt)
    loss_sum = jnp.sum(sq_cols)

    grads = {k: [None] * DEPTH for k in ("w_in", "b_forget", "mla_q_norm_g", "w_mla_q_up", "mla_kv_norm_g",
                                         "w_mla_kv_up", "w_mem_kv", "w_out", "ln_g", "ln_b")}
    dmemn = []
    dy1, dy2, c1 = dh, None, 1.0
    for l in reversed(range(DEPTH)):
        r = saved[l]
        pa, pb = r["pa"], r["pb"]
        o_fox, o_sb, o_mla, o_mem = r["groups"]
        du, du_b, dg, db = _ln_bwd(r["h"], r["y"], p["ln_g"][l], dy1, dy2, c1, "ln_bwd")
        grads["ln_g"][l], grads["ln_b"][l] = dg[0], db[0]
        dgated = _matmul(du_b, p["w_out"][l], F32, "out_proj_dx", "nt")
        grads["w_out"][l] = _matmul(r["gated"], du_b, F32, "out_proj_dw", "tn")
        dmixed, dgate_b = _gate_bwd(dgated, r["groups"], pb)

        dfq, dfk, dfv, dfc_q, dfc_k = _softmax_bwd(pairs(pa, 0), pairs(pa, 2), vals(pa, 4), o_fox, r["lse_fox"],
                                                   dmixed, 0, s, s, head_scale, True, r["fbias"], "fox_bwd")
        dmisc_f, dbf = _forget_bwd(pb, r["bias_row"], dfc_q + jnp.pad(dfc_k.T, ((0, 0), (0, 128 - 8))))
        grads["b_forget"][l] = dbf[0, :N_HEADS]

        dsq, dsk, dsv = _sb_bwd(pa, 6, 8, 10, dmixed, 1, *r["rm_sb"], s, head_scale, "sb_bwd")

        dqm, dkm, dvm = _softmax_bwd(r["mla_q"], r["mla_k"], vals(r["kv"], 2), o_mla, r["lse_mla"], dmixed, 2,
                                     s, s, mla_scale, True, None, "mla_bwd")
        dq_mla_b = _rope_q_bwd(dqm, tabs, mats)
        dcqn = _matmul(dq_mla_b, r["wq_up"], F32, "q_up_dx", "nt")
        grads["w_mla_q_up"][l] = _heads_interleaved(_matmul(r["cqn"], dq_mla_b, F32, "q_up_dw", "tn"),
                                                    HEAD_DIM + MLA_ROPE, HEAD_DIM)
        dcq_b, dgq = _rms_bwd(pb, MLA_Q_RANK, PB_CQ_BLK, p["mla_q_norm_g"][l], dcqn, "rms_q_bwd")
        grads["mla_q_norm_g"][l] = dgq[0]
        dkv_b = jnp.concatenate([dkm[:, :GROUP_W], dvm], axis=1).astype(MXU_DTYPE)
        dckvn = _matmul(dkv_b, r["wkv_up"], F32, "kv_up_dx", "nt")
        grads["w_mla_kv_up"][l] = _heads_interleaved(_matmul(r["ckvn"], dkv_b, F32, "kv_up_dw", "tn"),
                                                     2 * HEAD_DIM, HEAD_DIM)
        dckv_b, dgkv = _rms_bwd(pb, MLA_KV_RANK, PB_CKV_BLK, p["mla_kv_norm_g"][l], dckvn, "rms_kv_bwd")
        grads["mla_kv_norm_g"][l] = dgkv[0]
        dmisc_k = _rope_k_bwd(dkm[:, GROUP_W:], tabs, mats)

        dmq, dmk, dmv = _softmax_bwd(pairs(pa, 12), pairs(r["mkv"], 0), vals(r["mkv"], 2), o_mem, r["lse_mem"],
                                     dmixed, 3, s, nm, head_scale, False, None, "mem_bwd")
        dmkv_b = jnp.concatenate([dmk, dmv], axis=1).astype(MXU_DTYPE)
        grads["w_mem_kv"][l] = _matmul(memn_b, dmkv_b, F32, "mem_kv_dw", "tn")
        dmemn.append(_matmul(dmkv_b, p["w_mem_kv"][l], F32, "mem_kv_dx", "nt"))

        dmisc = jnp.concatenate([dmisc_f[:, :MISC_KROT], dmisc_k[:, MISC_KROT:MISC_KROT + MLA_ROPE],
                                 jnp.zeros((s, 128 - MISC_KROT - MLA_ROPE), F32)], axis=1)
        dp = jnp.concatenate([d.astype(MXU_DTYPE) for d in (dfq, dfk, dfv, dsq, dsk, dsv, dmq)]
                             + [dgate_b, dcq_b, dckv_b, dmisc.astype(MXU_DTYPE)], axis=1)
        dhproj = _matmul(dp, r["wp"], F32, "proj_dx", "nt")
        grads["w_in"][l] = _merge_dw_in(_matmul(r["hb"], dp, F32, "proj_dw", "tn"))
        dy1, dy2, c1 = du, dhproj, ALPHA

    dx, _, dg_in, db_in = _ln_bwd(x2, None, p["ln_in_g"], dy1, dy2, c1, "ln_in_bwd")
    _, _, dg_mem, db_mem = _ln_bwd(mem2, None, p["mem_ln_g"], dmemn[0], dmemn[1], 1.0, "ln_mem_bwd")
    out = {k: jnp.stack(v) for k, v in grads.items()}
    out.update(ln_in_g=dg_in[0], ln_in_b=db_in[0], mem_ln_g=dg_mem[0], mem_ln_b=db_mem[0])
    return loss_sum, dx, out


BIG_NAMES = ("w_in", "w_out", "w_mem_kv", "w_mla_q_up", "w_mla_kv_up")
BIG_AXIS = dict(w_in=2, w_out=1, w_mem_kv=1, w_mla_q_up=2, w_mla_kv_up=2)
SMALL_NAMES = ("ln_in_g", "ln_in_b", "mem_ln_g", "mem_ln_b", "ln_g", "ln_b", "b_forget", "mla_q_norm_g",
               "mla_kv_norm_g")
ALL_NAMES = ("ln_in_g", "ln_in_b", "mem_ln_g", "mem_ln_b", "w_in", "b_forget", "mla_q_norm_g", "w_mla_q_up",
             "mla_kv_norm_g", "w_mla_kv_up", "w_mem_kv", "w_out", "ln_g", "ln_b")
N_CHIPS = 4
N_DEV = 8


def _rows_of(shape):
    rows = -(-int(np.prod(shape)) // LANES)
    return -(-rows // PACK_ALIGN) * PACK_ALIGN


def _pack(arrs, rows):
    parts = []
    for a in arrs:
        f = a.reshape(-1)
        n = _rows_of(a.shape) * LANES
        parts.append(jnp.pad(f, (0, n - f.shape[0])).reshape(-1, LANES))
    used = sum(q.shape[0] for q in parts)
    if rows > used:
        parts.append(jnp.zeros((rows - used, LANES), parts[0].dtype))
    return jnp.concatenate(parts, axis=0)


def _unpack(buf, shapes):
    out, r = [], 0
    for shp in shapes:
        n = _rows_of(shp)
        out.append(buf[r:r + n].reshape(-1)[:int(np.prod(shp))].reshape(shp))
        r += n
    return out


HBM_SPEC = pl.BlockSpec(memory_space=pltpu.HBM)


def _gather_weights(shard):
    rows = shard.shape[0]

    def body(w_ref, out_ref, send_sems, recv_sems, local_sem):
        x, y, c = (lax.axis_index(a) for a in MESH_AXES)
        me = 2 * x + y
        chips = [(1 - x, y), (x, 1 - y), (1 - x, 1 - y)]

        def copy(k, block, to):
            return pltpu.make_async_remote_copy(
                src_ref=w_ref, dst_ref=out_ref.at[block], send_sem=send_sems.at[k], recv_sem=recv_sems.at[k],
                device_id=to, device_id_type=pl.DeviceIdType.MESH)

        mine = pltpu.make_async_copy(w_ref, out_ref.at[me], local_sem)
        mine.start()
        sends = [copy(k, me, (px, py, c)) for k, (px, py) in enumerate(chips)]
        for cp in sends:
            cp.start()
        for k, (px, py) in enumerate(chips):
            copy(k, 2 * px + py, (px, py, c)).wait_recv()
        for cp in sends:
            cp.wait_send()
        mine.wait()

    return pl.pallas_call(
        body, name="gather_weights", out_shape=jax.ShapeDtypeStruct((N_CHIPS, rows, LANES), shard.dtype),
        in_specs=[HBM_SPEC], out_specs=HBM_SPEC,
        scratch_shapes=[pltpu.SemaphoreType.DMA((3,)), pltpu.SemaphoreType.DMA((3,)), pltpu.SemaphoreType.DMA],
    )(shard)


def _exchange_grads(big, small):
    rb, rs = big.shape[1], small.shape[0]

    def body(big_ref, small_ref, obig_ref, osmall_ref, send_sems, recv_sems, local_sems):
        x, y, c = (lax.axis_index(a) for a in MESH_AXES)
        me = 4 * x + 2 * y + c
        flips = [(fx, fy, fc) for fx in (0, 1) for fy in (0, 1) for fc in (0, 1) if fx or fy or fc]
        peers = [(1 - x if fx else x, 1 - y if fy else y, 1 - c if fc else c) for fx, fy, fc in flips]

        def copy_big(k, src_chip, slot, to):
            return pltpu.make_async_remote_copy(
                src_ref=big_ref.at[src_chip], dst_ref=obig_ref.at[slot], send_sem=send_sems.at[k],
                recv_sem=recv_sems.at[k], device_id=to, device_id_type=pl.DeviceIdType.MESH)

        def copy_small(k, slot, to):
            return pltpu.make_async_remote_copy(
                src_ref=small_ref, dst_ref=osmall_ref.at[slot], send_sem=send_sems.at[7 + k],
                recv_sem=recv_sems.at[7 + k], device_id=to, device_id_type=pl.DeviceIdType.MESH)

        own_big = pltpu.make_async_copy(big_ref.at[2 * x + y], obig_ref.at[me], local_sems.at[0])
        own_small = pltpu.make_async_copy(small_ref, osmall_ref.at[me], local_sems.at[1])
        own_big.start()
        own_small.start()
        sends = []
        for k, (px, py, pc) in enumerate(peers):
            sends += [copy_big(k, 2 * px + py, me, (px, py, pc)), copy_small(k, me, (px, py, pc))]
        for cp in sends:
            cp.start()
        for k, (px, py, pc) in enumerate(peers):
            slot = 4 * px + 2 * py + pc
            copy_big(k, 2 * x + y, slot, (px, py, pc)).wait_recv()
            copy_small(k, slot, (px, py, pc)).wait_recv()
        for cp in sends:
            cp.wait_send()
        own_big.wait()
        own_small.wait()

    return pl.pallas_call(
        body, name="exchange_grads",
        out_shape=[jax.ShapeDtypeStruct((N_DEV, rb, LANES), big.dtype),
                   jax.ShapeDtypeStruct((N_DEV, rs, LANES), small.dtype)],
        in_specs=[HBM_SPEC, HBM_SPEC], out_specs=[HBM_SPEC, HBM_SPEC],
        scratch_shapes=[pltpu.SemaphoreType.DMA((14,)), pltpu.SemaphoreType.DMA((14,)),
                        pltpu.SemaphoreType.DMA((2,))],
    )(big, small)


def _adamw(parts, w, m, v, name):
    rows = w.shape[0]
    tile = _pick(rows, (128, 16, 8))
    bc1 = 1.0 - ADAM_B1 ** ADAM_STEP
    bc2 = 1.0 - ADAM_B2 ** ADAM_STEP

    def kern(p_ref, w_ref, m_ref, v_ref, g_ref, d_ref, nm_ref, nv_ref):
        g = p_ref[0].astype(F32)
        for d in range(1, N_DEV):
            g = g + p_ref[d].astype(F32)
        nm = ADAM_B1 * m_ref[...] + (1.0 - ADAM_B1) * g
        nv = ADAM_B2 * v_ref[...] + (1.0 - ADAM_B2) * (g * g)
        g_ref[...] = g
        nm_ref[...] = nm
        nv_ref[...] = nv
        d_ref[...] = -ADAM_LR * ((nm / bc1) / (jnp.sqrt(nv / bc2) + ADAM_EPS) + ADAM_WD * w_ref[...])

    spec = pl.BlockSpec((tile, LANES), lambda i: (i, 0))
    return pl.pallas_call(
        kern, name=name, grid=(rows // tile,),
        in_specs=[pl.BlockSpec((N_DEV, tile, LANES), lambda i: (0, i, 0)), spec, spec, spec],
        out_specs=[spec] * 4, out_shape=[jax.ShapeDtypeStruct((rows, LANES), F32)] * 4,
        compiler_params=pltpu.CompilerParams(dimension_semantics=("parallel",), vmem_limit_bytes=VMEM_LIMIT),
    )(parts, w, m, v)


def kernel(x, mem, ln_in_g, ln_in_b, mem_ln_g, mem_ln_b, w_in, b_forget, mla_q_norm_g, w_mla_q_up, mla_kv_norm_g, w_mla_kv_up, w_mem_kv, w_out, ln_g, ln_b, loss_target, m_ln_in_g, m_ln_in_b, m_mem_ln_g, m_mem_ln_b, m_w_in, m_b_forget, m_mla_q_norm_g, m_w_mla_q_up, m_mla_kv_norm_g, m_w_mla_kv_up, m_w_mem_kv, m_w_out, m_ln_g, m_ln_b, v_ln_in_g, v_ln_in_b, v_mem_ln_g, v_mem_ln_b, v_w_in, v_b_forget, v_mla_q_norm_g, v_w_mla_q_up, v_mla_kv_norm_g, v_w_mla_kv_up, v_w_mem_kv, v_w_out, v_ln_g, v_ln_b):
    w = dict(ln_in_g=ln_in_g, ln_in_b=ln_in_b, mem_ln_g=mem_ln_g, mem_ln_b=mem_ln_b, w_in=w_in, b_forget=b_forget,
             mla_q_norm_g=mla_q_norm_g, w_mla_q_up=w_mla_q_up, mla_kv_norm_g=mla_kv_norm_g,
             w_mla_kv_up=w_mla_kv_up, w_mem_kv=w_mem_kv, w_out=w_out, ln_g=ln_g, ln_b=ln_b)
    mo = dict(ln_in_g=m_ln_in_g, ln_in_b=m_ln_in_b, mem_ln_g=m_mem_ln_g, mem_ln_b=m_mem_ln_b, w_in=m_w_in,
              b_forget=m_b_forget, mla_q_norm_g=m_mla_q_norm_g, w_mla_q_up=m_w_mla_q_up,
              mla_kv_norm_g=m_mla_kv_norm_g, w_mla_kv_up=m_w_mla_kv_up, w_mem_kv=m_w_mem_kv, w_out=m_w_out,
              ln_g=m_ln_g, ln_b=m_ln_b)
    vo = dict(ln_in_g=v_ln_in_g, ln_in_b=v_ln_in_b, mem_ln_g=v_mem_ln_g, mem_ln_b=v_mem_ln_b, w_in=v_w_in,
              b_forget=v_b_forget, mla_q_norm_g=v_mla_q_norm_g, w_mla_q_up=v_w_mla_q_up,
              mla_kv_norm_g=v_mla_kv_norm_g, w_mla_kv_up=v_w_mla_kv_up, w_mem_kv=v_w_mem_kv, w_out=v_w_out,
              ln_g=v_ln_g, ln_b=v_ln_b)
    big_shapes = [w[n].shape for n in BIG_NAMES]
    small_shapes = [w[n].shape for n in SMALL_NAMES]

    gathered = _gather_weights(_pack([w[n].astype(MXU_DTYPE) for n in BIG_NAMES], BIG_ROWS))
    per_chip = [_unpack(gathered[j], big_shapes) for j in range(N_CHIPS)]
    full = dict(w)
    for idx, n in enumerate(BIG_NAMES):
        full[n] = jnp.concatenate([per_chip[j][idx] for j in range(N_CHIPS)], axis=BIG_AXIS[n])

    loss_sum, dx, g = _local_step(x[0], mem[0], loss_target[0], full)
    loss = lax.psum(loss_sum * (0.5 / D_MODEL), MESH_AXES)

    def shard_of(n, j):
        ax, size = BIG_AXIS[n], w[n].shape[BIG_AXIS[n]]
        return lax.slice_in_dim(g[n], j * size, (j + 1) * size, axis=ax)

    big = jnp.stack([_pack([shard_of(n, j).astype(MXU_DTYPE) for n in BIG_NAMES], BIG_ROWS) for j in range(N_CHIPS)])
    small = _pack([g[n] for n in SMALL_NAMES], SMALL_ROWS)
    big_parts, small_parts = _exchange_grads(big, small)

    res_big = _adamw(big_parts, _pack([w[n] for n in BIG_NAMES], BIG_ROWS), _pack([mo[n] for n in BIG_NAMES], BIG_ROWS),
                     _pack([vo[n] for n in BIG_NAMES], BIG_ROWS), "adamw_sharded")
    res_small = _adamw(small_parts, _pack([w[n] for n in SMALL_NAMES], SMALL_ROWS),
                       _pack([mo[n] for n in SMALL_NAMES], SMALL_ROWS), _pack([vo[n] for n in SMALL_NAMES], SMALL_ROWS),
                       "adamw_replicated")
    outs = []
    for kind in range(4):
        vals = dict(zip(BIG_NAMES, _unpack(res_big[kind], big_shapes)))
        vals.update(zip(SMALL_NAMES, _unpack(res_small[kind], small_shapes)))
        outs += [vals[n] for n in ALL_NAMES]
    return (loss, dx[None], *outs)
```

```python
import functools

import numpy as np
import jax
import jax.numpy as jnp
from jax import lax
from jax.experimental import pallas as pl
from jax.experimental.pallas import tpu as pltpu

F32 = jnp.float32
MXU_DTYPE = jnp.bfloat16

DEPTH = 2
D_MODEL = 1024
GROUP_W = 256
N_HEADS = 4
HEAD_DIM = 64
MLA_Q_RANK = 256
MLA_KV_RANK = 128
MLA_ROPE = 32
MLA_Q_COLS = N_HEADS * (HEAD_DIM + MLA_ROPE)
MLA_KV_COLS = N_HEADS * 2 * HEAD_DIM
ROPE_THETA = 10000.0
LN_EPS = 1e-5
RMS_EPS = 1e-6
ALPHA = (2 * DEPTH) ** 0.25
ADAM_LR, ADAM_B1, ADAM_B2, ADAM_EPS, ADAM_WD, ADAM_STEP = 0.001, 0.9, 0.999, 1e-08, 0.01, 10

_SPLIT = (256, 256, 256, 4, 256, 256, 256, 256, 128, 32, 256, 1024)
_OFF = [int(o) for o in np.cumsum((0,) + _SPLIT)]
IN_COLS = _OFF[-1]
PA_COLS = 7 * GROUP_W
PB_COLS = 1024 + 256 + 128 + 128
PB_CQ_BLK, PB_CKV_BLK, PB_MISC_BLK = 4, 10, 11
MISC_KROT = 4

LANES = 1024
PACK_ALIGN = 16
BIG_ROWS = 2560
SMALL_ROWS = 144
ROW_TILE = 256
ATT_TILE = 256
SB_QUERY_TILE = 512
SOFTMAX_TILE = 512
PAIR = 128
SB_SLOT = PAIR // N_HEADS
VMEM_LIMIT = 56 * 1024 * 1024
MATMUL_VMEM = 30 * 1024 * 1024
NEG = -1e30
EXP_UNDERFLOW = -104.0
DEAD_LOGIT = -110.0
REACH_SLACK = 1.0 + 2.0 ** -10
MESH_AXES = ("x", "y", "c")


def _dot(a, b):
    return jnp.dot(a, b, preferred_element_type=F32)


def _dot_nt(a, b):
    return lax.dot_general(a, b, (((1,), (1,)), ((), ())), preferred_element_type=F32)


def _dot_tn(a, b):
    return lax.dot_general(a, b, (((0,), (0,)), ((), ())), preferred_element_type=F32)


def _split2(x):
    hi = x.astype(MXU_DTYPE)
    lo = (x - hi.astype(F32)).astype(MXU_DTYPE)
    return hi, lo


def _split3(x):
    hi = x.astype(MXU_DTYPE)
    r = x - hi.astype(F32)
    mid = r.astype(MXU_DTYPE)
    lo = (r - mid.astype(F32)).astype(MXU_DTYPE)
    return hi, mid, lo


def _dot_exact_r(x, pm):
    hi, mid, lo = _split3(x)
    return _dot(hi, pm) + _dot(mid, pm) + _dot(lo, pm)


def _dot_exact_l(pm, x):
    hi, mid, lo = _split3(x)
    return _dot(pm, hi) + _dot(pm, mid) + _dot(pm, lo)


def _pick(dim, prefs):
    for p in prefs:
        if dim % p == 0:
            return p
    return dim


def _softplus(z):
    return jnp.maximum(z, 0.0) + jnp.log(1.0 + jnp.exp(-jnp.abs(z)))


def _tile_options(dim):
    opts = [d for d in range(128, min(dim, 2048) + 1, 128) if dim % d == 0]
    return opts or [dim]


def _matmul_tiles(m, n, k, out_bytes):
    tk = k if k <= 4096 else _pick(k, (1024, 512, 256, 128))
    best = None
    for tm in _tile_options(m):
        for tn in _tile_options(n):
            vmem = 2 * 2 * (tm * tk + tk * tn) + 4 * tm * tn + 2 * out_bytes * tm * tn
            if vmem <= MATMUL_VMEM and (best is None or tm * tn / (tm + tn) > best[0]):
                best = (tm * tn / (tm + tn), tm, tn)
    return best[1], best[2], tk


def _matmul(a, b, out_dtype, name, mode="nn"):
    m, k = (a.shape[1], a.shape[0]) if mode == "tn" else a.shape
    n = b.shape[0] if mode == "nt" else b.shape[1]
    tm, tn, tk = _matmul_tiles(m, n, k, jnp.dtype(out_dtype).itemsize)
    nk = k // tk
    dot = {"nn": _dot, "tn": _dot_tn, "nt": _dot_nt}[mode]

    def kern(a_ref, b_ref, o_ref, *acc):
        if nk == 1:
            o_ref[...] = dot(a_ref[...], b_ref[...]).astype(o_ref.dtype)
            return
        acc_ref, = acc
        kk = pl.program_id(2)

        @pl.when(kk == 0)
        def _():
            acc_ref[...] = jnp.zeros_like(acc_ref)

        acc_ref[...] += dot(a_ref[...], b_ref[...])

        @pl.when(kk == nk - 1)
        def _():
            o_ref[...] = acc_ref[...].astype(o_ref.dtype)

    a_spec = (pl.BlockSpec((tk, tm), lambda i, j, kk: (kk, i)) if mode == "tn"
              else pl.BlockSpec((tm, tk), lambda i, j, kk: (i, kk)))
    b_spec = (pl.BlockSpec((tn, tk), lambda i, j, kk: (j, kk)) if mode == "nt"
              else pl.BlockSpec((tk, tn), lambda i, j, kk: (kk, j)))
    return pl.pallas_call(
        kern, name=name, grid=(m // tm, n // tn, nk), in_specs=[a_spec, b_spec],
        out_specs=pl.BlockSpec((tm, tn), lambda i, j, kk: (i, j)),
        out_shape=jax.ShapeDtypeStruct((m, n), out_dtype),
        scratch_shapes=[pltpu.VMEM((tm, tn), F32)] if nk > 1 else [],
        compiler_params=pltpu.CompilerParams(
            dimension_semantics=("parallel", "parallel", "arbitrary"), vmem_limit_bytes=VMEM_LIMIT),
    )(a.astype(MXU_DTYPE), b.astype(MXU_DTYPE))


def _rowwise(body, name, rows, tile, row_ins, full_ins, row_outs, acc_outs=(), scratch=(),
             reverse=False, sequential=False):
    n = rows // tile

    def ridx(i):
        return (n - 1 - i) if reverse else i

    in_specs, args = [], []
    for arr, width, cb in row_ins:
        in_specs.append(pl.BlockSpec((tile, width), lambda i, cb=cb: (ridx(i), cb)))
        args.append(arr)
    for arr in full_ins:
        in_specs.append(pl.BlockSpec(arr.shape, lambda i, nd=arr.ndim: (0,) * nd))
        args.append(arr)
    out_shape = [jax.ShapeDtypeStruct((rows, w), dt) for w, dt in row_outs]
    out_shape += [jax.ShapeDtypeStruct(s, dt) for s, dt in acc_outs]
    out_specs = [pl.BlockSpec((tile, w), lambda i: (ridx(i), 0)) for w, dt in row_outs]
    out_specs += [pl.BlockSpec(s, lambda i, nd=len(s): (0,) * nd) for s, dt in acc_outs]

    def kern(*refs):
        body(pl.program_id(0), *refs)

    sem = "arbitrary" if (acc_outs or sequential) else "parallel"
    return pl.pallas_call(
        kern, name=name, grid=(n,), in_specs=in_specs, out_specs=out_specs, out_shape=out_shape,
        scratch_shapes=list(scratch),
        compiler_params=pltpu.CompilerParams(dimension_semantics=(sem,), vmem_limit_bytes=VMEM_LIMIT),
    )(*args)


def _ln_stats(u):
    mu = jnp.mean(u, axis=-1, keepdims=True)
    xc = u - mu
    var = jnp.mean(xc * xc, axis=-1, keepdims=True)
    return xc, lax.rsqrt(var + LN_EPS)


def _ln_fwd(a, b, g, beta, name):
    rows, d = a.shape
    has_b = b is not None

    def body(i, *refs):
        if has_b:
            a_ref, b_ref, g_ref, be_ref, h_ref, hb_ref = refs
            u = ALPHA * a_ref[...] + b_ref[...]
        else:
            a_ref, g_ref, be_ref, h_ref, hb_ref = refs
            u = a_ref[...]
        xc, rstd = _ln_stats(u)
        y = xc * rstd * g_ref[...] + be_ref[...]
        h_ref[...] = y
        hb_ref[...] = y.astype(hb_ref.dtype)

    row_ins = [(a, d, 0)] + ([(b, d, 0)] if has_b else [])
    return _rowwise(body, name, rows, min(ROW_TILE, rows), row_ins,
                    [g.reshape(1, d), beta.reshape(1, d)], [(d, F32), (d, MXU_DTYPE)])


def _ln_bwd(a, b, g, dy1, dy2, c1, name):
    rows, d = a.shape
    has_b = b is not None
    has_2 = dy2 is not None

    def body(i, *refs):
        refs = list(refs)
        a_ref = refs.pop(0)
        u = a_ref[...]
        if has_b:
            u = ALPHA * u + refs.pop(0)[...]
        dy = c1 * refs.pop(0)[...]
        if has_2:
            dy = dy + refs.pop(0)[...]
        g_ref, du_ref, dub_ref, dg_ref, db_ref = refs

        @pl.when(i == 0)
        def _():
            dg_ref[...] = jnp.zeros_like(dg_ref)
            db_ref[...] = jnp.zeros_like(db_ref)

        xc, rstd = _ln_stats(u)
        xhat = xc * rstd
        dxh = dy * g_ref[...]
        m1 = jnp.mean(dxh, axis=-1, keepdims=True)
        m2 = jnp.mean(dxh * xhat, axis=-1, keepdims=True)
        du = rstd * (dxh - m1 - xhat * m2)
        du_ref[...] = du
        dub_ref[...] = du.astype(dub_ref.dtype)
        dg_ref[...] += jnp.sum(dy * xhat, axis=0, keepdims=True)
        db_ref[...] += jnp.sum(dy, axis=0, keepdims=True)

    row_ins = [(a, d, 0)] + ([(b, d, 0)] if has_b else []) + [(dy1, d, 0)] + ([(dy2, d, 0)] if has_2 else [])
    return _rowwise(body, name, rows, min(ROW_TILE, rows), row_ins, [g.reshape(1, d)],
                    [(d, F32), (d, MXU_DTYPE)], [((1, d), F32), ((1, d), F32)])


def _loss_grad(h, target):
    rows, d = h.shape

    def body(i, h_ref, t_ref, dh_ref, acc_ref):
        @pl.when(i == 0)
        def _():
            acc_ref[...] = jnp.zeros_like(acc_ref)

        e = h_ref[...] - t_ref[...]
        dh_ref[...] = e * (1.0 / d)
        acc_ref[...] += jnp.sum(e * e, axis=0, keepdims=True)

    return _rowwise(body, "loss_grad", rows, ROW_TILE, [(h, d, 0), (target, d, 0)], [],
                    [(d, F32)], [((1, d), F32)])


def _gate_fwd(groups, pb):
    rows = pb.shape[0]
    w = GROUP_W * len(groups)

    def body(i, *refs):
        g = refs[4][...]
        mixed = jnp.concatenate([r[...] for r in refs[:4]], axis=1)
        refs[5][...] = (mixed * (g / (1.0 + jnp.exp(-g)))).astype(refs[5].dtype)

    return _rowwise(body, "gate_fwd", rows, ROW_TILE, [(o, GROUP_W, 0) for o in groups] + [(pb, w, 0)], [],
                    [(w, MXU_DTYPE)])[0]


def _gate_bwd(dgated, groups, pb):
    rows = pb.shape[0]
    w = GROUP_W * len(groups)

    def body(i, *refs):
        dg = refs[0][...]
        mixed = jnp.concatenate([r[...] for r in refs[1:5]], axis=1)
        g = refs[5][...]
        dm_ref, dgate_ref = refs[6], refs[7]
        sig = 1.0 / (1.0 + jnp.exp(-g))
        dm_ref[...] = dg * (g * sig)
        dgate_ref[...] = (dg * mixed * (sig * (1.0 + g * (1.0 - sig)))).astype(dgate_ref.dtype)

    return _rowwise(body, "gate_bwd", rows, ROW_TILE,
                    [(dgated, w, 0)] + [(o, GROUP_W, 0) for o in groups] + [(pb, w, 0)], [],
                    [(w, F32), (w, MXU_DTYPE)])


def _tri(n, kind):
    r = np.arange(n)[:, None]
    c = np.arange(n)[None, :]
    m = {"lower_incl": r >= c, "upper_incl": r <= c, "row_gt_col": r > c, "row_lt_col": r < c}[kind]
    return jnp.asarray(m.astype(np.float32), dtype=MXU_DTYPE)


def _forget_fwd(pb, bias_row):
    rows = pb.shape[0]
    tile = min(ROW_TILE, rows)

    def body(i, x_ref, b_ref, l_ref, o_ref, carry_ref):
        @pl.when(i == 0)
        def _():
            carry_ref[...] = jnp.zeros_like(carry_ref)

        xx = x_ref[...] + b_ref[...]
        lane = lax.broadcasted_iota(jnp.int32, xx.shape, 1)
        lf = jnp.where(lane < N_HEADS, -_softplus(-xx), 0.0)
        o_ref[...] = _dot_exact_l(l_ref[...], lf) + carry_ref[...]
        carry_ref[...] += jnp.sum(lf, axis=0, keepdims=True)

    return _rowwise(body, "forget_fwd", rows, tile, [(pb, 128, PB_MISC_BLK)],
                    [bias_row, _tri(tile, "lower_incl")], [(128, F32)],
                    scratch=[pltpu.VMEM((1, 128), F32)], sequential=True)[0]


def _forget_bwd(pb, bias_row, dfc):
    rows = pb.shape[0]
    tile = min(ROW_TILE, rows)

    def body(i, x_ref, df_ref, b_ref, u_ref, o_ref, db_ref, carry_ref):
        @pl.when(i == 0)
        def _():
            carry_ref[...] = jnp.zeros_like(carry_ref)
            db_ref[...] = jnp.zeros_like(db_ref)

        df = df_ref[...]
        sfx = _dot_exact_l(u_ref[...], df) + carry_ref[...]
        carry_ref[...] += jnp.sum(df, axis=0, keepdims=True)
        xx = x_ref[...] + b_ref[...]
        lane = lax.broadcasted_iota(jnp.int32, xx.shape, 1)
        dl = jnp.where(lane < N_HEADS, sfx / (1.0 + jnp.exp(xx)), 0.0)
        o_ref[...] = dl
        db_ref[...] += jnp.sum(dl, axis=0, keepdims=True)

    return _rowwise(body, "forget_bwd", rows, tile,
                    [(pb, 128, PB_MISC_BLK), (dfc, 128, 0)],
                    [bias_row, _tri(tile, "upper_incl")], [(128, F32)], [((1, 128), F32)],
                    scratch=[pltpu.VMEM((1, 128), F32)], reverse=True, sequential=True)


def _rope_tables(s):
    half = MLA_ROPE // 2
    inv_freq = ROPE_THETA ** (-jnp.arange(half, dtype=F32) / half)
    ang = jnp.arange(s).astype(F32)[:, None] * inv_freq[None, :]
    cos2 = jnp.tile(jnp.cos(ang), (1, 2))
    sin2 = jnp.tile(jnp.sin(ang), (1, 2))
    cx = jnp.tile(cos2, (1, N_HEADS))
    sx = jnp.tile(sin2, (1, N_HEADS))
    cq = jnp.concatenate([jnp.ones((s, GROUP_W), F32), cx], axis=1)
    sq = jnp.concatenate([jnp.zeros((s, GROUP_W), F32), sx], axis=1)
    pad = ((0, 0), (MISC_KROT, 128 - MISC_KROT - MLA_ROPE))
    ck = jnp.pad(cos2, pad)
    sk = jnp.pad(sin2, pad)
    return dict(cq=cq, sq=sq, ck=ck, sk=sk, cx=cx, sx=sx)


def _rot_matrix(width, bases):
    half = MLA_ROPE // 2
    p = np.zeros((width, width), np.float32)
    for b in bases:
        for i in range(half):
            p[b + half + i, b + i] = -1.0
            p[b + i, b + half + i] = 1.0
    return p


def _rope_matrices():
    pq = _rot_matrix(MLA_Q_COLS, [GROUP_W + h * MLA_ROPE for h in range(N_HEADS)])
    pk = _rot_matrix(128, [MISC_KROT])
    p4 = _rot_matrix(128, [h * MLA_ROPE for h in range(N_HEADS)])
    a = np.zeros((128, 128), np.float32)
    for h in range(N_HEADS):
        for r in range(MLA_ROPE):
            a[h * MLA_ROPE + r, MISC_KROT + r] = 1.0
    cast = lambda m: jnp.asarray(m, dtype=MXU_DTYPE)
    return dict(pq=cast(pq), pqt=cast(pq.T), pk=cast(pk), spread=cast(a.T), xa=cast(a), xb=cast(p4.T @ a))


def _rms(c, g):
    r = lax.rsqrt(jnp.mean(c * c, axis=-1, keepdims=True) + RMS_EPS)
    return c * r * g


def _mla_q_fwd(pb, g, w_up, tabs, mats):
    rows = pb.shape[0]

    def body(i, c_ref, cos_ref, sin_ref, g_ref, w_ref, p_ref, q_ref, cn_ref):
        cn = _rms(c_ref[...], g_ref[...]).astype(cn_ref.dtype)
        cn_ref[...] = cn
        q = _dot(cn, w_ref[...])
        q_ref[...] = (q * cos_ref[...] + _dot_exact_r(q, p_ref[...]) * sin_ref[...]).astype(q_ref.dtype)

    return _rowwise(body, "mla_q_fwd", rows, ROW_TILE,
                    [(pb, MLA_Q_RANK, PB_CQ_BLK), (tabs["cq"], MLA_Q_COLS, 0), (tabs["sq"], MLA_Q_COLS, 0)],
                    [g.reshape(1, MLA_Q_RANK), w_up.astype(MXU_DTYPE), mats["pq"]],
                    [(MLA_Q_COLS, MXU_DTYPE), (MLA_Q_RANK, MXU_DTYPE)])


def _mla_kv_fwd(pb, g, w_up, tabs, mats):
    rows = pb.shape[0]

    def body(i, c_ref, x_ref, cos_ref, sin_ref, g_ref, w_ref, p_ref, sp_ref, kv_ref, cn_ref, kr_ref):
        cn = _rms(c_ref[...], g_ref[...]).astype(cn_ref.dtype)
        cn_ref[...] = cn
        kv_ref[...] = _dot(cn, w_ref[...]).astype(kv_ref.dtype)
        xx = x_ref[...]
        kr = xx * cos_ref[...] + _dot_exact_r(xx, p_ref[...]) * sin_ref[...]
        kr_ref[...] = _dot_exact_r(kr, sp_ref[...]).astype(kr_ref.dtype)

    return _rowwise(body, "mla_kv_fwd", rows, ROW_TILE,
                    [(pb, MLA_KV_RANK, PB_CKV_BLK), (pb, 128, PB_MISC_BLK), (tabs["ck"], 128, 0), (tabs["sk"], 128, 0)],
                    [g.reshape(1, MLA_KV_RANK), w_up.astype(MXU_DTYPE), mats["pk"], mats["spread"]],
                    [(MLA_KV_COLS, MXU_DTYPE), (MLA_KV_RANK, MXU_DTYPE), (128, MXU_DTYPE)])


def _rope_q_bwd(dq_full, tabs, mats):
    rows, nq = dq_full.shape

    def body(i, d_ref, cos_ref, sin_ref, pt_ref, o_ref):
        d = d_ref[...]
        o_ref[...] = (d * cos_ref[...] + _dot_exact_r(d * sin_ref[...], pt_ref[...])).astype(o_ref.dtype)

    return _rowwise(body, "rope_q_bwd", rows, ROW_TILE,
                    [(dq_full, nq, 0), (tabs["cq"], nq, 0), (tabs["sq"], nq, 0)], [mats["pqt"]],
                    [(nq, MXU_DTYPE)])[0]


def _rope_k_bwd(dkr, tabs, mats):
    rows = dkr.shape[0]

    def body(i, d_ref, cos_ref, sin_ref, a_ref, b_ref, o_ref):
        d = d_ref[...]
        o_ref[...] = _dot_exact_r(d * cos_ref[...], a_ref[...]) + _dot_exact_r(d * sin_ref[...], b_ref[...])

    return _rowwise(body, "rope_k_bwd", rows, ROW_TILE,
                    [(dkr, 128, 0), (tabs["cx"], 128, 0), (tabs["sx"], 128, 0)], [mats["xa"], mats["xb"]],
                    [(128, F32)])[0]


def _rms_bwd(pb, width, col_blk, g, dy, name):
    rows = pb.shape[0]

    def body(i, c_ref, dy_ref, g_ref, dc_ref, dg_ref):
        @pl.when(i == 0)
        def _():
            dg_ref[...] = jnp.zeros_like(dg_ref)

        c = c_ref[...]
        dy = dy_ref[...]
        r = lax.rsqrt(jnp.mean(c * c, axis=-1, keepdims=True) + RMS_EPS)
        dyg = dy * g_ref[...]
        dc = r * dyg - c * (r * r * r) * jnp.mean(c * dyg, axis=-1, keepdims=True)
        dc_ref[...] = dc.astype(dc_ref.dtype)
        dg_ref[...] += jnp.sum(dy * c * r, axis=0, keepdims=True)

    return _rowwise(body, name, rows, ROW_TILE, [(pb, width, col_blk), (dy, width, 0)], [g.reshape(1, width)],
                    [(width, MXU_DTYPE)], [((1, width), F32)])


def _att_params(parallel):
    return pltpu.CompilerParams(dimension_semantics=("parallel" if parallel else "arbitrary",),
                                vmem_limit_bytes=VMEM_LIMIT)


def _blk_off(j, t):
    return j * t if isinstance(j, int) else pl.multiple_of(j * t, t)


def _causal_mask(t, strict):
    r = lax.broadcasted_iota(jnp.int32, (t, t), 0)
    c = lax.broadcasted_iota(jnp.int32, (t, t), 1)
    return (c < r) if strict else (c <= r)


def _lane_mask(kind, head, rows):
    lane = lax.broadcasted_iota(jnp.int32, (rows, PAIR), 1)
    if kind == "pair":
        return (lane < HEAD_DIM) if head % 2 == 0 else (lane >= HEAD_DIM)
    return (lane >= MLA_ROPE * head) & (lane < MLA_ROPE * (head + 1))


def _row_spec(t, cb, width=PAIR):
    return pl.BlockSpec((t, width), lambda i, cb=cb: (i, cb))


def _whole_spec(rows, cb, width=PAIR):
    return pl.BlockSpec((rows, width), lambda i, cb=cb: (0, cb), pipeline_mode=pl.Buffered(1))


def _is_pow2(x):
    return float(np.frexp(x)[0]) == 0.5


def _masked_heads(blocks, kinds, pair, dtype, scale=None):
    out = []
    for e in range(2):
        head = 2 * pair + e
        parts = [jnp.where(_lane_mask(k, head, b.shape[0]), b.astype(F32) * (1.0 if scale is None else scale),
                           0.0).astype(dtype)
                 for b, k in zip(blocks, kinds)]
        out.append(parts[0] if len(parts) == 1 else jnp.concatenate(parts, axis=1))
    return out


def _logit_reach(qh, kmax2, head):
    q32 = qh.astype(F32)
    return jnp.sqrt(jnp.sum(q32 * q32, axis=1, keepdims=True) * _col(kmax2, head)) * REACH_SLACK


def _forget_top(ft_ref, head, off):
    return jnp.max(-ft_ref[head:head + 1, pl.ds(off, PAIR)])


def _col(block, idx):
    lane = lax.broadcasted_iota(jnp.int32, block.shape, 1)
    return jnp.sum(jnp.where(lane == idx, block, 0.0), axis=1, keepdims=True)


def _scatter_cols(cols, t):
    lane = lax.broadcasted_iota(jnp.int32, (t, PAIR), 1)
    out = jnp.zeros((t, PAIR), F32)
    for idx, c in cols.items():
        out = out + jnp.where(lane == idx, c, 0.0)
    return out


def _take_heads(per_head, pair):
    return jnp.where(_lane_mask("pair", 0, per_head[0].shape[0]), per_head[0], per_head[1])


class _Parts:
    def __init__(self, q_parts, k_parts, v_parts, tq, sk):
        self.kinds = [[kind for _, _, kind in q_parts[p]] for p in range(2)]
        self.nparts = len(q_parts[0])
        self.q_specs = [_row_spec(tq, cb) for p in range(2) for _, cb, _ in q_parts[p]]
        self.q_args = [a for p in range(2) for a, _, _ in q_parts[p]]
        self.k_specs = [_whole_spec(sk, cb) for p in range(2) for _, cb, _ in k_parts[p]]
        self.k_args = [a for p in range(2) for a, _, _ in k_parts[p]]
        self.v_specs = [_whole_spec(sk, cb) for _, cb in v_parts]
        self.v_args = [a for a, _ in v_parts]
        self.width = PAIR * self.nparts

    def split(self, refs):
        n = self.nparts
        refs = list(refs)
        q = [refs[p * n:(p + 1) * n] for p in range(2)]
        k = [refs[2 * n + p * n:2 * n + (p + 1) * n] for p in range(2)]
        v = refs[4 * n:4 * n + 2]
        return q, k, v, refs[4 * n + 2:]

    def k_block(self, k_refs, off, t):
        blks = [r[pl.ds(off, t), :] for r in k_refs]
        return blks[0] if len(blks) == 1 else jnp.concatenate(blks, axis=1)


def _key_norm_max(src, k_blk):
    rows = src.shape[0]

    def body(i, k0_ref, k1_ref, o_ref):
        @pl.when(i == 0)
        def _():
            o_ref[...] = jnp.zeros_like(o_ref)

        cols = {}
        for p, ref in enumerate((k0_ref, k1_ref)):
            k32 = ref[...].astype(F32)
            for e in range(2):
                sq = jnp.sum(jnp.where(_lane_mask("pair", e, k32.shape[0]), k32 * k32, 0.0), axis=1, keepdims=True)
                cols[2 * p + e] = jnp.max(sq, axis=0, keepdims=True)
        lane = lax.broadcasted_iota(jnp.int32, (1, PAIR), 1)
        o_ref[...] = jnp.maximum(o_ref[...], sum(jnp.where(lane == h, c, 0.0) for h, c in cols.items()))

    return _rowwise(body, "key_norm_max", rows, ROW_TILE, [(src, PAIR, k_blk), (src, PAIR, k_blk + 1)], [], [],
                    [((1, PAIR), F32)])[0]


def _softmax_fwd(q_parts, k_parts, v_parts, sq, sk, scale, causal, bias, name):
    tq = min(SOFTMAX_TILE, sq)
    tk = tq if causal else min(SOFTMAX_TILE, sk)
    nkv = sk // tk
    pp = _Parts(q_parts, k_parts, v_parts, tq, sk)

    def kern(*refs):
        q_refs, k_refs, v_refs, rest = pp.split(refs)
        if bias is not None:
            fc_ref, ft_ref, kmax_ref, o_ref, lse_ref = rest
            fcb = fc_ref[...]
        else:
            o_ref, lse_ref = rest
        i = pl.program_id(0)
        fold = _is_pow2(scale)
        head_on = [jnp.where(_lane_mask("pair", e, tk), 1.0, 0.0).astype(MXU_DTYPE) for e in range(2)]
        head_off = [jnp.where(_lane_mask("pair", e, tk), 0.0, 1.0).astype(MXU_DTYPE) for e in range(2)]
        lse_cols = {}
        for p in range(2):
            qm = _masked_heads([r[...] for r in q_refs[p]], pp.kinds[p], p, MXU_DTYPE, scale if fold else None)
            if bias is not None:
                reach = [_logit_reach(qm[e], kmax_ref[...], 2 * p + e) for e in range(2)]

            def block(j, carry, masked, p=p, qm=qm):
                off = _blk_off(j, tk)
                kb = pp.k_block(k_refs[p], off, tk)
                vb = v_refs[p][pl.ds(off, tk), :]
                out = []
                for e in range(2):
                    h = 2 * p + e
                    m, acc = carry[e]
                    s = _dot_nt(qm[e], kb)
                    if not fold:
                        s = s * scale
                    if bias is not None:
                        s = s - ft_ref[h:h + 1, pl.ds(off, tk)]
                    if masked:
                        s = jnp.where(_causal_mask(tq, False), s, NEG)
                    m_new = jnp.maximum(m, jnp.max(s, axis=1, keepdims=True))
                    pr = jnp.exp(s - m_new).astype(MXU_DTYPE)
                    out.append((m_new, jnp.exp(m - m_new) * acc + _dot(pr, vb * head_on[e] + head_off[e])))
                return tuple(out)

            carry = tuple((jnp.full((tq, 1), NEG, F32), jnp.zeros((tq, PAIR), F32)) for _ in range(2))
            if causal and bias is not None:
                def alive(c, j, p=p, reach=reach):
                    top = [_forget_top(ft_ref, 2 * p + e, _blk_off(j, tk)) for e in range(2)]
                    return functools.reduce(jnp.maximum, [jnp.max(reach[e] + top[e] - c[e][0]) for e in range(2)])

                def step(state, block=block, alive=alive):
                    n, _, c = state
                    c = block(i - 1 - n, c, False)
                    return n + 1, alive(c, i - 1 - n), c

                carry = block(i, carry, True)
                _, _, carry = lax.while_loop(lambda st: jnp.logical_and(st[0] < i, st[1] > DEAD_LOGIT), step,
                                             (jnp.int32(0), alive(carry, i), carry))
            elif causal:
                carry = lax.fori_loop(0, i, lambda j, c, block=block: block(j, c, False), carry)
                carry = block(i, carry, True)
            else:
                for j in range(nkv):
                    carry = block(j, carry, False)
            outs = []
            for e in range(2):
                m, acc = carry[e]
                l = _col(acc, HEAD_DIM * (1 - e))
                outs.append(acc / l)
                lse_cols[2 * p + e] = m + jnp.log(l) + (_col(fcb, 2 * p + e) if bias is not None else 0.0)
            o_ref[:, p * PAIR:(p + 1) * PAIR] = _take_heads(outs, p)
        lse_ref[...] = _scatter_cols(lse_cols, tq)

    in_specs = pp.q_specs + pp.k_specs + pp.v_specs
    args = pp.q_args + pp.k_args + pp.v_args
    if bias is not None:
        in_specs += [_row_spec(tq, 0), pl.BlockSpec((8, sk), lambda i: (0, 0), pipeline_mode=pl.Buffered(1)),
                     pl.BlockSpec((1, PAIR), lambda i: (0, 0))]
        args += list(bias)
    return pl.pallas_call(
        kern, name=name, grid=(sq // tq,), in_specs=in_specs,
        out_specs=[_row_spec(tq, 0, GROUP_W), _row_spec(tq, 0)],
        out_shape=[jax.ShapeDtypeStruct((sq, GROUP_W), F32), jax.ShapeDtypeStruct((sq, PAIR), F32)],
        compiler_params=_att_params(True),
    )(*args)


def _softmax_bwd(q_parts, k_parts, v_parts, o, lse, do, do_blk, sq, sk, scale, causal, bias, name):
    tq = min(SOFTMAX_TILE, sq)
    tk = tq if causal else min(SOFTMAX_TILE, sk)
    nkv = sk // tk
    pp = _Parts(q_parts, k_parts, v_parts, tq, sk)
    quad = pp.nparts == 2
    wq = GROUP_W + (PAIR if quad else 0)

    def kern(*refs):
        q_refs, k_refs, v_refs, rest = pp.split(refs)
        if bias is not None:
            o_ref, lse_ref, do_ref, fc_ref, ft_ref, kmax_ref, dq_ref, dk_ref, dv_ref, dfq_ref, dfk_ref = rest
            fcb = fc_ref[...]
        else:
            o_ref, lse_ref, do_ref, dq_ref, dk_ref, dv_ref = rest
        i = pl.program_id(0)
        fold = _is_pow2(scale)

        @pl.when(i == 0)
        def _():
            dk_ref[...] = jnp.zeros_like(dk_ref)
            dv_ref[...] = jnp.zeros_like(dv_ref)
            if bias is not None:
                dfk_ref[...] = jnp.zeros_like(dfk_ref)

        lse_b = lse_ref[...]
        qm, dom, delta, lse_h = [], [], [], []
        for p in range(2):
            qm += _masked_heads([r[...] for r in q_refs[p]], pp.kinds[p], p, MXU_DTYPE, scale if fold else None)
            do_p = do_ref[:, p * PAIR:(p + 1) * PAIR]
            dom += _masked_heads([do_p], ["pair"], p, MXU_DTYPE)
            prod = do_p * o_ref[:, p * PAIR:(p + 1) * PAIR]
            for e in range(2):
                h = 2 * p + e
                delta.append(jnp.sum(jnp.where(_lane_mask("pair", h, tq), prod, 0.0), axis=1, keepdims=True))
                lse_h.append(_col(lse_b, h) - (_col(fcb, h) if bias is not None else 0.0))

        def block(j, carry, masked):
            off = _blk_off(j, tk)
            out = []
            for p in range(2):
                kb = pp.k_block(k_refs[p], off, tk)
                vb = v_refs[p][pl.ds(off, tk), :]
                dk_acc = jnp.zeros((tk, pp.width), F32)
                dv_acc = jnp.zeros((tk, PAIR), F32)
                for e in range(2):
                    h = 2 * p + e
                    dq, dfq = carry[h]
                    s = _dot_nt(qm[h], kb)
                    if not fold:
                        s = s * scale
                    if bias is not None:
                        s = s - ft_ref[h:h + 1, pl.ds(off, tk)]
                    if masked:
                        s = jnp.where(_causal_mask(tq, False), s, NEG)
                    pr = jnp.exp(s - lse_h[h])
                    ds = pr * (_dot_nt(dom[h], vb) - delta[h])
                    dsb = (ds if fold else ds * scale).astype(MXU_DTYPE)
                    dv_acc = dv_acc + _dot_tn(pr.astype(MXU_DTYPE), dom[h])
                    dk_acc = dk_acc + _dot_tn(dsb, qm[h])
                    dq = dq + _dot(dsb, kb)
                    if bias is not None:
                        dfq = dfq + jnp.sum(ds, axis=1, keepdims=True)
                        dfk_ref[h:h + 1, pl.ds(off, tk)] -= jnp.sum(ds, axis=0, keepdims=True)
                    out.append((dq, dfq))
                dv_ref[pl.ds(off, tk), p * PAIR:(p + 1) * PAIR] += dv_acc
                dk_ref[pl.ds(off, tk), p * PAIR:(p + 1) * PAIR] += dk_acc[:, :PAIR]
                if quad:
                    dk_ref[pl.ds(off, tk), GROUP_W:] += dk_acc[:, PAIR:]
            return tuple(out)

        carry = tuple((jnp.zeros((tq, pp.width), F32), jnp.zeros((tq, 1), F32)) for _ in range(N_HEADS))
        if causal and bias is not None:
            reach = [_logit_reach(qm[h], kmax_ref[...], h) - lse_h[h] for h in range(N_HEADS)]

            def alive(j):
                return functools.reduce(jnp.maximum, [jnp.max(reach[h] + _forget_top(ft_ref, h, _blk_off(j, tk)))
                                                      for h in range(N_HEADS)])

            def step(state):
                n, _, c = state
                return n + 1, alive(i - 1 - n), block(i - 1 - n, c, False)

            carry = block(i, carry, True)
            _, _, carry = lax.while_loop(lambda st: jnp.logical_and(st[0] < i, st[1] > DEAD_LOGIT), step,
                                         (jnp.int32(0), alive(i), carry))
        elif causal:
            carry = lax.fori_loop(0, i, lambda j, c: block(j, c, False), carry)
            carry = block(i, carry, True)
        else:
            for j in range(nkv):
                carry = block(j, carry, False)
        dqs = [c[0] * scale if fold else c[0] for c in carry]
        for p in range(2):
            dq_ref[:, p * PAIR:(p + 1) * PAIR] = _take_heads([dqs[2 * p + e][:, :PAIR] for e in range(2)], p)
        if quad:
            dq_ref[:, GROUP_W:] = sum(jnp.where(_lane_mask("quad", h, tq), dqs[h][:, PAIR:], 0.0)
                                      for h in range(N_HEADS))
        if bias is not None:
            dfq_ref[...] = _scatter_cols({h: carry[h][1] for h in range(N_HEADS)}, tq)

    acc_spec = lambda rows, width: pl.BlockSpec((rows, width), lambda i: (0, 0), pipeline_mode=pl.Buffered(1))
    in_specs = pp.q_specs + pp.k_specs + pp.v_specs + [_row_spec(tq, 0, GROUP_W), _row_spec(tq, 0),
                                                       _row_spec(tq, do_blk, GROUP_W)]
    args = pp.q_args + pp.k_args + pp.v_args + [o, lse, do]
    out_specs = [_row_spec(tq, 0, wq), acc_spec(sk, wq), acc_spec(sk, GROUP_W)]
    out_shape = [jax.ShapeDtypeStruct((sq, wq), F32), jax.ShapeDtypeStruct((sk, wq), F32),
                 jax.ShapeDtypeStruct((sk, GROUP_W), F32)]
    if bias is not None:
        in_specs += [_row_spec(tq, 0), pl.BlockSpec((8, sk), lambda i: (0, 0), pipeline_mode=pl.Buffered(1)),
                     pl.BlockSpec((1, PAIR), lambda i: (0, 0))]
        args += list(bias)
        out_specs += [_row_spec(tq, 0), acc_spec(8, sk)]
        out_shape += [jax.ShapeDtypeStruct((sq, PAIR), F32), jax.ShapeDtypeStruct((8, sk), F32)]
    return pl.pallas_call(
        kern, name=name, grid=(sq // tq,), in_specs=in_specs, out_specs=out_specs, out_shape=out_shape,
        compiler_params=_att_params(False),
    )(*args)


def _sb_logs(qh, kb, valid):
    z = _dot_nt(qh, kb)
    sp = _softplus(z)
    lk = -sp
    if valid is not None:
        lk = jnp.where(valid, lk, 0.0)
    return lk, z - sp


def _sb_valid(d, tq, tk):
    r = lax.broadcasted_iota(jnp.int32, (tq, tk), 0)
    c = lax.broadcasted_iota(jnp.int32, (tq, tk), 1)
    return c + d * tk < r


def _tri_sums(xs, tri):
    t = xs[0].shape[0]
    pieces = [_split2(x) for x in xs]
    hi = _dot(jnp.concatenate([pc[0] for pc in pieces], axis=0), tri)
    lo = _dot(jnp.concatenate([pc[1] for pc in pieces], axis=0), tri)
    return [hi[n * t:(n + 1) * t] + lo[n * t:(n + 1) * t] for n in range(len(xs))]


def _sb_fwd(src, q_blk, k_blk, v_blk, s, scale, name):
    assert _is_pow2(scale) and s // ATT_TILE <= SB_SLOT
    tq, t = min(SB_QUERY_TILE, s), min(ATT_TILE, s)
    band = tq // t
    pair = lambda blk: [[(src, blk + p, "pair")] for p in range(2)]
    pp = _Parts(pair(q_blk), pair(k_blk), [(src, v_blk + p) for p in range(2)], tq, s)

    def kern(*refs):
        q_refs, k_refs, v_refs, (tri_ref, o_ref, rm_ref, cnt_ref) = pp.split(refs)
        i = pl.program_id(0)
        tri = tri_ref[...]
        lane = lax.broadcasted_iota(jnp.int32, (tq, PAIR), 1)
        qm = []
        for p in range(2):
            qm += _masked_heads([q_refs[p][0][...]], ["pair"], p, MXU_DTYPE, scale)

        def block(j, carry, valid):
            accs, rights, rm = carry
            off = _blk_off(j, t)
            kbs = [k_refs[p][0][pl.ds(off, t), :] for p in range(2)]
            vbs = [v_refs[p][pl.ds(off, t), :] for p in range(2)]
            logs = [_sb_logs(qm[h], kbs[h // 2], valid) for h in range(N_HEADS)]
            tails = _tri_sums([lg[0] for lg in logs], tri)
            new_acc, new_right = [], []
            for h in range(N_HEADS):
                lk, ls = logs[h]
                w = jnp.exp(ls + tails[h] + rights[h])
                if valid is not None:
                    w = jnp.where(valid, w, 0.0)
                new_acc.append(accs[h] + _dot(w.astype(MXU_DTYPE), vbs[h // 2]))
                rm = rm + jnp.where(lane == SB_SLOT * h + j, rights[h], 0.0)
                new_right.append(rights[h] + jnp.sum(lk, axis=1, keepdims=True))
            return tuple(new_acc), tuple(new_right), rm

        carry = (tuple(jnp.zeros((tq, PAIR), F32) for _ in range(N_HEADS)),
                 tuple(jnp.zeros((tq, 1), F32) for _ in range(N_HEADS)), jnp.zeros((tq, PAIR), F32))
        for d in reversed(range(band)):
            carry = block(band * i + d, carry, _sb_valid(d, tq, t))

        def alive(c):
            return functools.reduce(jnp.maximum, [jnp.max(r) for r in c[1]])

        def step(state):
            n, _, c = state
            c = block(band * i - 1 - n, c, None)
            return n + 1, alive(c), c

        n_done, _, carry = lax.while_loop(lambda st: jnp.logical_and(st[0] < band * i, st[1] > EXP_UNDERFLOW),
                                          step, (jnp.int32(0), alive(carry), carry))
        cnt_ref[i] = n_done
        for p in range(2):
            o_ref[:, p * PAIR:(p + 1) * PAIR] = _take_heads([carry[0][2 * p + e] for e in range(2)], p)
        rm_ref[...] = carry[2]

    return pl.pallas_call(
        kern, name=name, grid=(s // tq,),
        in_specs=pp.q_specs + pp.k_specs + pp.v_specs + [pl.BlockSpec((t, t), lambda i: (0, 0))],
        out_specs=[_row_spec(tq, 0, GROUP_W), _row_spec(tq, 0), pl.BlockSpec(memory_space=pltpu.SMEM)],
        out_shape=[jax.ShapeDtypeStruct((s, GROUP_W), F32), jax.ShapeDtypeStruct((s, PAIR), F32),
                   jax.ShapeDtypeStruct((s // tq,), jnp.int32)],
        compiler_params=_att_params(False),
    )(*(pp.q_args + pp.k_args + pp.v_args + [_tri(t, "row_gt_col")]))


def _sb_bwd(src, q_blk, k_blk, v_blk, do, do_blk, rm, visited, s, scale, name):
    assert _is_pow2(scale)
    tq, t = min(SB_QUERY_TILE, s), min(ATT_TILE, s)
    band = tq // t
    pair = lambda blk: [[(src, blk + p, "pair")] for p in range(2)]
    pp = _Parts(pair(q_blk), pair(k_blk), [(src, v_blk + p) for p in range(2)], tq, s)

    def kern(*refs):
        q_refs, k_refs, v_refs, (do_ref, rm_ref, tri_ref, pre_ref, cnt_ref, dq_ref, dk_ref, dv_ref) = pp.split(refs)
        i = pl.program_id(0)

        @pl.when(i == 0)
        def _():
            dk_ref[...] = jnp.zeros_like(dk_ref)
            dv_ref[...] = jnp.zeros_like(dv_ref)

        rmb = rm_ref[...]
        tri = tri_ref[...]
        pre = pre_ref[...]
        qm, dom = [], []
        for p in range(2):
            qm += _masked_heads([q_refs[p][0][...]], ["pair"], p, MXU_DTYPE, scale)
            dom += _masked_heads([do_ref[:, p * PAIR:(p + 1) * PAIR]], ["pair"], p, MXU_DTYPE)

        def block(j, carry, valid):
            dqs, lefts = carry
            off = _blk_off(j, t)
            kbs = [k_refs[p][0][pl.ds(off, t), :] for p in range(2)]
            vbs = [v_refs[p][pl.ds(off, t), :] for p in range(2)]
            logs = [_sb_logs(qm[h], kbs[h // 2], valid) for h in range(N_HEADS)]
            tails = _tri_sums([lg[0] for lg in logs], tri)
            ws, gs = [], []
            for h in range(N_HEADS):
                lk, ls = logs[h]
                w = jnp.exp(ls + tails[h] + _col(rmb, SB_SLOT * h + j))
                if valid is not None:
                    w = jnp.where(valid, w, 0.0)
                ws.append(w)
                gs.append(_dot_nt(dom[h], vbs[h // 2]) * w)
            prefix = _tri_sums(gs, pre)
            new_dq, new_left = [], []
            dk_acc = [jnp.zeros((t, PAIR), F32) for _ in range(2)]
            dv_acc = [jnp.zeros((t, PAIR), F32) for _ in range(2)]
            for h in range(N_HEADS):
                lk, ls = logs[h]
                sig = jnp.exp(ls)
                dz = gs[h] * (1.0 - sig) - sig * (prefix[h] + lefts[h])
                if valid is not None:
                    dz = jnp.where(valid, dz, 0.0)
                dzb = dz.astype(MXU_DTYPE)
                dv_acc[h // 2] = dv_acc[h // 2] + _dot_tn(ws[h].astype(MXU_DTYPE), dom[h])
                dk_acc[h // 2] = dk_acc[h // 2] + _dot_tn(dzb, qm[h])
                new_dq.append(dqs[h] + _dot(dzb, kbs[h // 2]))
                new_left.append(lefts[h] + jnp.sum(gs[h], axis=1, keepdims=True))
            for p in range(2):
                dv_ref[pl.ds(off, t), p * PAIR:(p + 1) * PAIR] += dv_acc[p]
                dk_ref[pl.ds(off, t), p * PAIR:(p + 1) * PAIR] += dk_acc[p]
            return tuple(new_dq), tuple(new_left)

        carry = (tuple(jnp.zeros((tq, PAIR), F32) for _ in range(N_HEADS)),
                 tuple(jnp.zeros((tq, 1), F32) for _ in range(N_HEADS)))
        carry = lax.fori_loop(band * i - cnt_ref[i], band * i, lambda j, c: block(j, c, None), carry)
        for d in range(band):
            carry = block(band * i + d, carry, _sb_valid(d, tq, t))
        for p in range(2):
            dq_ref[:, p * PAIR:(p + 1) * PAIR] = _take_heads([carry[0][2 * p + e] * scale for e in range(2)], p)

    mspec = pl.BlockSpec((t, t), lambda i: (0, 0))
    acc_spec = pl.BlockSpec((s, GROUP_W), lambda i: (0, 0), pipeline_mode=pl.Buffered(1))
    return pl.pallas_call(
        kern, name=name, grid=(s // tq,),
        in_specs=pp.q_specs + pp.k_specs + pp.v_specs + [_row_spec(tq, do_blk, GROUP_W), _row_spec(tq, 0), mspec, mspec,
                                                         pl.BlockSpec(memory_space=pltpu.SMEM)],
        out_specs=[_row_spec(tq, 0, GROUP_W), acc_spec, acc_spec],
        out_shape=[jax.ShapeDtypeStruct((s, GROUP_W), F32)] * 3,
        compiler_params=_att_params(False),
    )(*(pp.q_args + pp.k_args + pp.v_args + [do, rm, _tri(t, "row_gt_col"), _tri(t, "row_lt_col"), visited]))


def _split_w_in(w):
    col = lambda n: w[:, _OFF[n]:_OFF[n + 1]]
    wa = jnp.concatenate([col(0), col(1), col(2), col(4), col(5), col(6), col(10)], axis=1)
    misc = jnp.concatenate([col(3), col(9), jnp.zeros((w.shape[0], 128 - 4 - MLA_ROPE), w.dtype)], axis=1)
    wb = jnp.concatenate([col(11), col(7), col(8), misc], axis=1)
    return wa, wb


def _merge_dw_in(dwp):
    a = lambda n: dwp[:, n * GROUP_W:(n + 1) * GROUP_W]
    b0 = PA_COLS
    gate = dwp[:, b0:b0 + 1024]
    cq = dwp[:, b0 + 1024:b0 + 1280]
    ckv = dwp[:, b0 + 1280:b0 + 1408]
    flog = dwp[:, b0 + 1408:b0 + 1412]
    krot = dwp[:, b0 + 1408 + MISC_KROT:b0 + 1408 + MISC_KROT + MLA_ROPE]
    return jnp.concatenate([a(0), a(1), a(2), flog, a(3), a(4), a(5), cq, ckv, krot, a(6), gate], axis=1)


def _heads_first(w, per_head, first):
    r = w.shape[0]
    w3 = w.reshape(r, N_HEADS, per_head)
    return jnp.concatenate([w3[:, :, :first].reshape(r, -1), w3[:, :, first:].reshape(r, -1)], axis=1)


def _heads_interleaved(w, per_head, first):
    r = w.shape[0]
    a = w[:, :N_HEADS * first].reshape(r, N_HEADS, first)
    b = w[:, N_HEADS * first:].reshape(r, N_HEADS, per_head - first)
    return jnp.concatenate([a, b], axis=2).reshape(r, N_HEADS * per_head)


def _pad_rows8(a):
    return a[:, :8].T


def _local_step(x2, mem2, tgt, p):
    s = x2.shape[0]
    nm = mem2.shape[0]
    head_scale = HEAD_DIM ** -0.5
    mla_scale = (HEAD_DIM + MLA_ROPE) ** -0.5
    tabs = _rope_tables(s)
    mats = _rope_matrices()
    pairs = lambda arr, blk: [[(arr, blk + q, "pair")] for q in range(2)]
    vals = lambda arr, blk: [(arr, blk + q) for q in range(2)]

    h, hb = _ln_fwd(x2, None, p["ln_in_g"], p["ln_in_b"], "ln_in_fwd")
    _, memn_b = _ln_fwd(mem2, None, p["mem_ln_g"], p["mem_ln_b"], "ln_mem_fwd")

    saved = []
    for l in range(DEPTH):
        wa, wb = _split_w_in(p["w_in"][l])
        wp = jnp.concatenate([wa, wb], axis=1)
        wq_up = _heads_first(p["w_mla_q_up"][l], HEAD_DIM + MLA_ROPE, HEAD_DIM)
        wkv_up = _heads_first(p["w_mla_kv_up"][l], 2 * HEAD_DIM, HEAD_DIM)
        bias_row = jnp.pad(p["b_forget"][l], (0, 128 - N_HEADS)).reshape(1, 128)
        pa = _matmul(hb, wa, MXU_DTYPE, "proj_a")
        pb = _matmul(hb, wb, F32, "proj_b")

        fc = _forget_fwd(pb, bias_row)
        fbias = (fc, _pad_rows8(fc), _key_norm_max(pa, 2))
        o_fox, lse_fox = _softmax_fwd(pairs(pa, 0), pairs(pa, 2), vals(pa, 4), s, s, head_scale, True, fbias,
                                      "fox_fwd")
        o_sb, *rm_sb = _sb_fwd(pa, 6, 8, 10, s, head_scale, "sb_fwd")

        qfull, cqn = _mla_q_fwd(pb, p["mla_q_norm_g"][l], wq_up, tabs, mats)
        kv, ckvn, kr4 = _mla_kv_fwd(pb, p["mla_kv_norm_g"][l], wkv_up, tabs, mats)
        mla_q = [[(qfull, q, "pair"), (qfull, 2, "quad")] for q in range(2)]
        mla_k = [[(kv, q, "pair"), (kr4, 0, "quad")] for q in range(2)]
        o_mla, lse_mla = _softmax_fwd(mla_q, mla_k, vals(kv, 2), s, s, mla_scale, True, None, "mla_fwd")

        mkv = _matmul(memn_b, p["w_mem_kv"][l], MXU_DTYPE, "mem_kv")
        o_mem, lse_mem = _softmax_fwd(pairs(pa, 12), pairs(mkv, 0), vals(mkv, 2), s, nm, head_scale, False, None,
                                      "mem_fwd")

        groups = (o_fox, o_sb, o_mla, o_mem)
        gated = _gate_fwd(groups, pb)
        y = _matmul(gated, p["w_out"][l], F32, "out_proj")
        saved.append(dict(h=h, hb=hb, y=y, wp=wp, wq_up=wq_up, wkv_up=wkv_up, bias_row=bias_row, pa=pa, pb=pb,
                          fbias=fbias, lse_fox=lse_fox, rm_sb=rm_sb, cqn=cqn, ckvn=ckvn, mla_q=mla_q, mla_k=mla_k,
                          kv=kv, lse_mla=lse_mla, mkv=mkv, lse_mem=lse_mem, groups=groups, gated=gated))
        h, hb = _ln_fwd(h, y, p["ln_g"][l], p["ln_b"][l], "ln_fwd")

    dh, sq_cols = _loss_grad(h, tgt)
    loss_sum = jnp.sum(sq_cols)

    grads = {k: [None] * DEPTH for k in ("w_in", "b_forget", "mla_q_norm_g", "w_mla_q_up", "mla_kv_norm_g",
                                         "w_mla_kv_up", "w_mem_kv", "w_out", "ln_g", "ln_b")}
    dmemn = []
    dy1, dy2, c1 = dh, None, 1.0
    for l in reversed(range(DEPTH)):
        r = saved[l]
        pa, pb = r["pa"], r["pb"]
        o_fox, o_sb, o_mla, o_mem = r["groups"]
        du, du_b, dg, db = _ln_bwd(r["h"], r["y"], p["ln_g"][l], dy1, dy2, c1, "ln_bwd")
        grads["ln_g"][l], grads["ln_b"][l] = dg[0], db[0]
        dgated = _matmul(du_b, p["w_out"][l], F32, "out_proj_dx", "nt")
        grads["w_out"][l] = _matmul(r["gated"], du_b, F32, "out_proj_dw", "tn")
        dmixed, dgate_b = _gate_bwd(dgated, r["groups"], pb)

        dfq, dfk, dfv, dfc_q, dfc_k = _softmax_bwd(pairs(pa, 0), pairs(pa, 2), vals(pa, 4), o_fox, r["lse_fox"],
                                                   dmixed, 0, s, s, head_scale, True, r["fbias"], "fox_bwd")
        dmisc_f, dbf = _forget_bwd(pb, r["bias_row"], dfc_q + jnp.pad(dfc_k.T, ((0, 0), (0, 128 - 8))))
        grads["b_forget"][l] = dbf[0, :N_HEADS]

        dsq, dsk, dsv = _sb_bwd(pa, 6, 8, 10, dmixed, 1, *r["rm_sb"], s, head_scale, "sb_bwd")

        dqm, dkm, dvm = _softmax_bwd(r["mla_q"], r["mla_k"], vals(r["kv"], 2), o_mla, r["lse_mla"], dmixed, 2,
                                     s, s, mla_scale, True, None, "mla_bwd")
        dq_mla_b = _rope_q_bwd(dqm, tabs, mats)
        dcqn = _matmul(dq_mla_b, r["wq_up"], F32, "q_up_dx", "nt")
        grads["w_mla_q_up"][l] = _heads_interleaved(_matmul(r["cqn"], dq_mla_b, F32, "q_up_dw", "tn"),
                                                    HEAD_DIM + MLA_ROPE, HEAD_DIM)
        dcq_b, dgq = _rms_bwd(pb, MLA_Q_RANK, PB_CQ_BLK, p["mla_q_norm_g"][l], dcqn, "rms_q_bwd")
        grads["mla_q_norm_g"][l] = dgq[0]
        dkv_b = jnp.concatenate([dkm[:, :GROUP_W], dvm], axis=1).astype(MXU_DTYPE)
        dckvn = _matmul(dkv_b, r["wkv_up"], F32, "kv_up_dx", "nt")
        grads["w_mla_kv_up"][l] = _heads_interleaved(_matmul(r["ckvn"], dkv_b, F32, "kv_up_dw", "tn"),
                                                     2 * HEAD_DIM, HEAD_DIM)
        dckv_b, dgkv = _rms_bwd(pb, MLA_KV_RANK, PB_CKV_BLK, p["mla_kv_norm_g"][l], dckvn, "rms_kv_bwd")
        grads["mla_kv_norm_g"][l] = dgkv[0]
        dmisc_k = _rope_k_bwd(dkm[:, GROUP_W:], tabs, mats)

        dmq, dmk, dmv = _softmax_bwd(pairs(pa, 12), pairs(r["mkv"], 0), vals(r["mkv"], 2), o_mem, r["lse_mem"],
                                     dmixed, 3, s, nm, head_scale, False, None, "mem_bwd")
        dmkv_b = jnp.concatenate([dmk, dmv], axis=1).astype(MXU_DTYPE)
        grads["w_mem_kv"][l] = _matmul(memn_b, dmkv_b, F32, "mem_kv_dw", "tn")
        dmemn.append(_matmul(dmkv_b, p["w_mem_kv"][l], F32, "mem_kv_dx", "nt"))

        dmisc = jnp.concatenate([dmisc_f[:, :MISC_KROT], dmisc_k[:, MISC_KROT:MISC_KROT + MLA_ROPE],
                                 jnp.zeros((s, 128 - MISC_KROT - MLA_ROPE), F32)], axis=1)
        dp = jnp.concatenate([d.astype(MXU_DTYPE) for d in (dfq, dfk, dfv, dsq, dsk, dsv, dmq)]
                             + [dgate_b, dcq_b, dckv_b, dmisc.astype(MXU_DTYPE)], axis=1)
        dhproj = _matmul(dp, r["wp"], F32, "proj_dx", "nt")
        grads["w_in"][l] = _merge_dw_in(_matmul(r["hb"], dp, F32, "proj_dw", "tn"))
        dy1, dy2, c1 = du, dhproj, ALPHA

    dx, _, dg_in, db_in = _ln_bwd(x2, None, p["ln_in_g"], dy1, dy2, c1, "ln_in_bwd")
    _, _, dg_mem, db_mem = _ln_bwd(mem2, None, p["mem_ln_g"], dmemn[0], dmemn[1], 1.0, "ln_mem_bwd")
    out = {k: jnp.stack(v) for k, v in grads.items()}
    out.update(ln_in_g=dg_in[0], ln_in_b=db_in[0], mem_ln_g=dg_mem[0], mem_ln_b=db_mem[0])
    return loss_sum, dx, out


BIG_NAMES = ("w_in", "w_out", "w_mem_kv", "w_mla_q_up", "w_mla_kv_up")
BIG_AXIS = dict(w_in=2, w_out=1, w_mem_kv=1, w_mla_q_up=2, w_mla_kv_up=2)
SMALL_NAMES = ("ln_in_g", "ln_in_b", "mem_ln_g", "mem_ln_b", "ln_g", "ln_b", "b_forget", "mla_q_norm_g",
               "mla_kv_norm_g")
ALL_NAMES = ("ln_in_g", "ln_in_b", "mem_ln_g", "mem_ln_b", "w_in", "b_forget", "mla_q_norm_g", "w_mla_q_up",
             "mla_kv_norm_g", "w_mla_kv_up", "w_mem_kv", "w_out", "ln_g", "ln_b")
N_CHIPS = 4
N_DEV = 8


def _rows_of(shape):
    rows = -(-int(np.prod(shape)) // LANES)
    return -(-rows // PACK_ALIGN) * PACK_ALIGN


def _pack(arrs, rows):
    parts = []
    for a in arrs:
        f = a.reshape(-1)
        n = _rows_of(a.shape) * LANES
        parts.append(jnp.pad(f, (0, n - f.shape[0])).reshape(-1, LANES))
    used = sum(q.shape[0] for q in parts)
    if rows > used:
        parts.append(jnp.zeros((rows - used, LANES), parts[0].dtype))
    return jnp.concatenate(parts, axis=0)


def _unpack(buf, shapes):
    out, r = [], 0
    for shp in shapes:
        n = _rows_of(shp)
        out.append(buf[r:r + n].reshape(-1)[:int(np.prod(shp))].reshape(shp))
        r += n
    return out


HBM_SPEC = pl.BlockSpec(memory_space=pltpu.HBM)


def _gather_weights(shard):
    rows = shard.shape[0]
    half = rows // 2

    def body(w_ref, out_ref, send_sems, recv_sems, local_sem):
        x, y, c = (lax.axis_index(a) for a in MESH_AXES)
        me, sibling = 2 * x + y, (x, y, 1 - c)
        chips = [(1 - x, y), (x, 1 - y), (1 - x, 1 - y)]

        def part(chip, core):
            return out_ref.at[chip, pl.ds(core * half, half)]

        def copy(k, src, dst, to):
            return pltpu.make_async_remote_copy(
                src_ref=src, dst_ref=dst, send_sem=send_sems.at[k], recv_sem=recv_sems.at[k],
                device_id=to, device_id_type=pl.DeviceIdType.MESH)

        mine = pltpu.make_async_copy(w_ref, out_ref.at[me], local_sem)
        mine.start()
        first = [copy(k, w_ref.at[pl.ds(c * half, half)], part(me, c), (px, py, c)) for k, (px, py) in enumerate(chips)]
        for cp in first:
            cp.start()
        passed = [copy(3 + k, part(2 * px + py, c), part(2 * px + py, c), sibling) for k, (px, py) in enumerate(chips)]
        for k, (px, py) in enumerate(chips):
            copy(k, part(me, c), part(2 * px + py, c), (px, py, c)).wait_recv()
            passed[k].start()
        for k, (px, py) in enumerate(chips):
            copy(3 + k, part(me, c), part(2 * px + py, 1 - c), sibling).wait_recv()
        for cp in first + passed:
            cp.wait_send()
        mine.wait()

    return pl.pallas_call(
        body, name="gather_weights", out_shape=jax.ShapeDtypeStruct((N_CHIPS, rows, LANES), shard.dtype),
        in_specs=[HBM_SPEC], out_specs=HBM_SPEC,
        scratch_shapes=[pltpu.SemaphoreType.DMA((6,)), pltpu.SemaphoreType.DMA((6,)), pltpu.SemaphoreType.DMA],
    )(shard)


def _exchange_grads(big, small):
    half, rs = big.shape[1] // 2, small.shape[0]

    def body(big_ref, small_ref, obig_ref, osmall_ref, send_sems, recv_sems, local_sems):
        x, y, c = (lax.axis_index(a) for a in MESH_AXES)
        me = 4 * x + 2 * y + c
        flips = [(fx, fy, fc) for fx in (0, 1) for fy in (0, 1) for fc in (0, 1) if fx or fy or fc]
        peers = [(1 - x if fx else x, 1 - y if fy else y, 1 - c if fc else c) for fx, fy, fc in flips]

        def mine_for(chip, core):
            return big_ref.at[chip, pl.ds(core * half, half)]

        def copy_big(k, src, slot, to):
            return pltpu.make_async_remote_copy(
                src_ref=src, dst_ref=obig_ref.at[slot], send_sem=send_sems.at[k],
                recv_sem=recv_sems.at[k], device_id=to, device_id_type=pl.DeviceIdType.MESH)

        def copy_small(k, slot, to):
            return pltpu.make_async_remote_copy(
                src_ref=small_ref, dst_ref=osmall_ref.at[slot], send_sem=send_sems.at[7 + k],
                recv_sem=recv_sems.at[7 + k], device_id=to, device_id_type=pl.DeviceIdType.MESH)

        own_big = pltpu.make_async_copy(mine_for(2 * x + y, c), obig_ref.at[me], local_sems.at[0])
        own_small = pltpu.make_async_copy(small_ref, osmall_ref.at[me], local_sems.at[1])
        own_big.start()
        own_small.start()
        sends = []
        for k, (px, py, pc) in enumerate(peers):
            sends += [copy_big(k, mine_for(2 * px + py, pc), me, (px, py, pc)), copy_small(k, me, (px, py, pc))]
        for cp in sends:
            cp.start()
        for k, (px, py, pc) in enumerate(peers):
            slot = 4 * px + 2 * py + pc
            copy_big(k, mine_for(2 * x + y, c), slot, (px, py, pc)).wait_recv()
            copy_small(k, slot, (px, py, pc)).wait_recv()
        for cp in sends:
            cp.wait_send()
        own_big.wait()
        own_small.wait()

    return pl.pallas_call(
        body, name="exchange_grads",
        out_shape=[jax.ShapeDtypeStruct((N_DEV, half, LANES), big.dtype),
                   jax.ShapeDtypeStruct((N_DEV, rs, LANES), small.dtype)],
        in_specs=[HBM_SPEC, HBM_SPEC], out_specs=[HBM_SPEC, HBM_SPEC],
        scratch_shapes=[pltpu.SemaphoreType.DMA((14,)), pltpu.SemaphoreType.DMA((14,)),
                        pltpu.SemaphoreType.DMA((2,))],
    )(big, small)


def _sum_parts(parts, name):
    n, rows = parts.shape[0], parts.shape[1]
    tile = _pick(rows, (128, 16, 8))

    def kern(p_ref, o_ref):
        g = p_ref[0].astype(F32)
        for d in range(1, n):
            g = g + p_ref[d].astype(F32)
        o_ref[...] = g

    return pl.pallas_call(
        kern, name=name, grid=(rows // tile,),
        in_specs=[pl.BlockSpec((n, tile, LANES), lambda i: (0, i, 0))],
        out_specs=pl.BlockSpec((tile, LANES), lambda i: (i, 0)), out_shape=jax.ShapeDtypeStruct((rows, LANES), F32),
        compiler_params=pltpu.CompilerParams(dimension_semantics=("parallel",), vmem_limit_bytes=VMEM_LIMIT),
    )(parts)


def _swap_halves(mine):
    def body(q_ref, out_ref, send_sem, recv_sem, local_sem):
        x, y, c = (lax.axis_index(a) for a in MESH_AXES)
        own = pltpu.make_async_copy(q_ref, out_ref.at[c], local_sem)
        own.start()
        swap = pltpu.make_async_remote_copy(
            src_ref=q_ref, dst_ref=out_ref.at[c], send_sem=send_sem, recv_sem=recv_sem,
            device_id=(x, y, 1 - c), device_id_type=pl.DeviceIdType.MESH)
        swap.start()
        pltpu.make_async_remote_copy(
            src_ref=q_ref, dst_ref=out_ref.at[1 - c], send_sem=send_sem, recv_sem=recv_sem,
            device_id=(x, y, 1 - c), device_id_type=pl.DeviceIdType.MESH).wait_recv()
        swap.wait_send()
        own.wait()

    return pl.pallas_call(
        body, name="swap_halves", out_shape=jax.ShapeDtypeStruct((2,) + mine.shape, mine.dtype),
        in_specs=[HBM_SPEC], out_specs=HBM_SPEC,
        scratch_shapes=[pltpu.SemaphoreType.DMA, pltpu.SemaphoreType.DMA, pltpu.SemaphoreType.DMA],
    )(mine)


def _adamw(parts, w, m, v, name):
    rows = w.shape[0]
    n_parts = parts.shape[0]
    tile = _pick(rows, (128, 16, 8))
    bc1 = 1.0 - ADAM_B1 ** ADAM_STEP
    bc2 = 1.0 - ADAM_B2 ** ADAM_STEP

    def kern(p_ref, w_ref, m_ref, v_ref, g_ref, d_ref, nm_ref, nv_ref):
        g = p_ref[0].astype(F32)
        for d in range(1, n_parts):
            g = g + p_ref[d].astype(F32)
        nm = ADAM_B1 * m_ref[...] + (1.0 - ADAM_B1) * g
        nv = ADAM_B2 * v_ref[...] + (1.0 - ADAM_B2) * (g * g)
        g_ref[...] = g
        nm_ref[...] = nm
        nv_ref[...] = nv
        d_ref[...] = -ADAM_LR * ((nm / bc1) / (jnp.sqrt(nv / bc2) + ADAM_EPS) + ADAM_WD * w_ref[...])

    spec = pl.BlockSpec((tile, LANES), lambda i: (i, 0))
    return pl.pallas_call(
        kern, name=name, grid=(rows // tile,),
        in_specs=[pl.BlockSpec((n_parts, tile, LANES), lambda i: (0, i, 0)), spec, spec, spec],
        out_specs=[spec] * 4, out_shape=[jax.ShapeDtypeStruct((rows, LANES), F32)] * 4,
        compiler_params=pltpu.CompilerParams(dimension_semantics=("parallel",), vmem_limit_bytes=VMEM_LIMIT),
    )(parts, w, m, v)


def kernel(x, mem, ln_in_g, ln_in_b, mem_ln_g, mem_ln_b, w_in, b_forget, mla_q_norm_g, w_mla_q_up, mla_kv_norm_g, w_mla_kv_up, w_mem_kv, w_out, ln_g, ln_b, loss_target, m_ln_in_g, m_ln_in_b, m_mem_ln_g, m_mem_ln_b, m_w_in, m_b_forget, m_mla_q_norm_g, m_w_mla_q_up, m_mla_kv_norm_g, m_w_mla_kv_up, m_w_mem_kv, m_w_out, m_ln_g, m_ln_b, v_ln_in_g, v_ln_in_b, v_mem_ln_g, v_mem_ln_b, v_w_in, v_b_forget, v_mla_q_norm_g, v_w_mla_q_up, v_mla_kv_norm_g, v_w_mla_kv_up, v_w_mem_kv, v_w_out, v_ln_g, v_ln_b):
    w = dict(ln_in_g=ln_in_g, ln_in_b=ln_in_b, mem_ln_g=mem_ln_g, mem_ln_b=mem_ln_b, w_in=w_in, b_forget=b_forget,
             mla_q_norm_g=mla_q_norm_g, w_mla_q_up=w_mla_q_up, mla_kv_norm_g=mla_kv_norm_g,
             w_mla_kv_up=w_mla_kv_up, w_mem_kv=w_mem_kv, w_out=w_out, ln_g=ln_g, ln_b=ln_b)
    mo = dict(ln_in_g=m_ln_in_g, ln_in_b=m_ln_in_b, mem_ln_g=m_mem_ln_g, mem_ln_b=m_mem_ln_b, w_in=m_w_in,
              b_forget=m_b_forget, mla_q_norm_g=m_mla_q_norm_g, w_mla_q_up=m_w_mla_q_up,
              mla_kv_norm_g=m_mla_kv_norm_g, w_mla_kv_up=m_w_mla_kv_up, w_mem_kv=m_w_mem_kv, w_out=m_w_out,
              ln_g=m_ln_g, ln_b=m_ln_b)
    vo = dict(ln_in_g=v_ln_in_g, ln_in_b=v_ln_in_b, mem_ln_g=v_mem_ln_g, mem_ln_b=v_mem_ln_b, w_in=v_w_in,
              b_forget=v_b_forget, mla_q_norm_g=v_mla_q_norm_g, w_mla_q_up=v_w_mla_q_up,
              mla_kv_norm_g=v_mla_kv_norm_g, w_mla_kv_up=v_w_mla_kv_up, w_mem_kv=v_w_mem_kv, w_out=v_w_out,
              ln_g=v_ln_g, ln_b=v_ln_b)
    big_shapes = [w[n].shape for n in BIG_NAMES]
    small_shapes = [w[n].shape for n in SMALL_NAMES]

    gathered = _gather_weights(_pack([w[n].astype(MXU_DTYPE) for n in BIG_NAMES], BIG_ROWS))
    per_chip = [_unpack(gathered[j], big_shapes) for j in range(N_CHIPS)]
    full = dict(w)
    for idx, n in enumerate(BIG_NAMES):
        full[n] = jnp.concatenate([per_chip[j][idx] for j in range(N_CHIPS)], axis=BIG_AXIS[n])

    loss_sum, dx, g = _local_step(x[0], mem[0], loss_target[0], full)
    loss = lax.psum(loss_sum * (0.5 / D_MODEL), MESH_AXES)

    def shard_of(n, j):
        ax, size = BIG_AXIS[n], w[n].shape[BIG_AXIS[n]]
        return lax.slice_in_dim(g[n], j * size, (j + 1) * size, axis=ax)

    big = jnp.stack([_pack([shard_of(n, j).astype(MXU_DTYPE) for n in BIG_NAMES], BIG_ROWS) for j in range(N_CHIPS)])
    small = _pack([g[n] for n in SMALL_NAMES], SMALL_ROWS)
    big_parts, small_parts = _exchange_grads(big, small)

    g_big = _swap_halves(_sum_parts(big_parts, "sum_sharded")).reshape(1, BIG_ROWS, LANES)
    res_big = _adamw(g_big, _pack([w[n] for n in BIG_NAMES], BIG_ROWS), _pack([mo[n] for n in BIG_NAMES], BIG_ROWS),
                     _pack([vo[n] for n in BIG_NAMES], BIG_ROWS), "adamw_sharded")
    res_small = _adamw(small_parts, _pack([w[n] for n in SMALL_NAMES], SMALL_ROWS),
                       _pack([mo[n] for n in SMALL_NAMES], SMALL_ROWS), _pack([vo[n] for n in SMALL_NAMES], SMALL_ROWS),
                       "adamw_replicated")
    outs = []
    for kind in range(4):
        vals = dict(zip(BIG_NAMES, _unpack(res_big[kind], big_shapes)))
        vals.update(zip(SMALL_NAMES, _unpack(res_small[kind], small_shapes)))
        outs += [vals[n] for n in ALL_NAMES]
    return (loss, dx[None], *outs)
```

```python
import functools

import numpy as np
import jax
import jax.numpy as jnp
from jax import lax
from jax.experimental import pallas as pl
from jax.experimental.pallas import tpu as pltpu

F32 = jnp.float32
MXU_DTYPE = jnp.bfloat16

DEPTH = 2
D_MODEL = 1024
GROUP_W = 256
N_HEADS = 4
HEAD_DIM = 64
MLA_Q_RANK = 256
MLA_KV_RANK = 128
MLA_ROPE = 32
MLA_Q_COLS = N_HEADS * (HEAD_DIM + MLA_ROPE)
MLA_KV_COLS = N_HEADS * 2 * HEAD_DIM
ROPE_THETA = 10000.0
LN_EPS = 1e-5
RMS_EPS = 1e-6
ALPHA = (2 * DEPTH) ** 0.25
ADAM_LR, ADAM_B1, ADAM_B2, ADAM_EPS, ADAM_WD, ADAM_STEP = 0.001, 0.9, 0.999, 1e-08, 0.01, 10

_SPLIT = (256, 256, 256, 4, 256, 256, 256, 256, 128, 32, 256, 1024)
_OFF = [int(o) for o in np.cumsum((0,) + _SPLIT)]
IN_COLS = _OFF[-1]
PA_COLS = 7 * GROUP_W
PB_COLS = 1024 + 256 + 128 + 128
PB_CQ_BLK, PB_CKV_BLK, PB_MISC_BLK = 4, 10, 11
MISC_KROT = 4

LANES = 1024
PACK_ALIGN = 16
BIG_ROWS = 2560
SMALL_ROWS = 144
ROW_TILE = 256
ATT_TILE = 256
SB_QUERY_TILE = 256
SOFTMAX_TILE = 512
PAIR = 128
SB_SLOT = PAIR // N_HEADS
VMEM_LIMIT = 56 * 1024 * 1024
MATMUL_VMEM = 30 * 1024 * 1024
NEG = -1e30
EXP_UNDERFLOW = -104.0
DEAD_LOGIT = -110.0
REACH_SLACK = 1.0 + 2.0 ** -10
MESH_AXES = ("x", "y", "c")


def _dot(a, b):
    return jnp.dot(a, b, preferred_element_type=F32)


def _dot_nt(a, b):
    return lax.dot_general(a, b, (((1,), (1,)), ((), ())), preferred_element_type=F32)


def _dot_tn(a, b):
    return lax.dot_general(a, b, (((0,), (0,)), ((), ())), preferred_element_type=F32)


def _split2(x):
    hi = x.astype(MXU_DTYPE)
    lo = (x - hi.astype(F32)).astype(MXU_DTYPE)
    return hi, lo


def _split3(x):
    hi = x.astype(MXU_DTYPE)
    r = x - hi.astype(F32)
    mid = r.astype(MXU_DTYPE)
    lo = (r - mid.astype(F32)).astype(MXU_DTYPE)
    return hi, mid, lo


def _dot_exact_r(x, pm):
    hi, mid, lo = _split3(x)
    return _dot(hi, pm) + _dot(mid, pm) + _dot(lo, pm)


def _dot_exact_l(pm, x):
    hi, mid, lo = _split3(x)
    return _dot(pm, hi) + _dot(pm, mid) + _dot(pm, lo)


def _pick(dim, prefs):
    for p in prefs:
        if dim % p == 0:
            return p
    return dim


def _softplus(z):
    return jnp.maximum(z, 0.0) + jnp.log(1.0 + jnp.exp(-jnp.abs(z)))


def _tile_options(dim):
    opts = [d for d in range(128, min(dim, 2048) + 1, 128) if dim % d == 0]
    return opts or [dim]


def _matmul_tiles(m, n, k, out_bytes):
    tk = k if k <= 4096 else _pick(k, (1024, 512, 256, 128))
    best = None
    for tm in _tile_options(m):
        for tn in _tile_options(n):
            vmem = 2 * 2 * (tm * tk + tk * tn) + 4 * tm * tn + 2 * out_bytes * tm * tn
            if vmem <= MATMUL_VMEM and (best is None or tm * tn / (tm + tn) > best[0]):
                best = (tm * tn / (tm + tn), tm, tn)
    return best[1], best[2], tk


def _matmul(a, b, out_dtype, name, mode="nn"):
    m, k = (a.shape[1], a.shape[0]) if mode == "tn" else a.shape
    n = b.shape[0] if mode == "nt" else b.shape[1]
    tm, tn, tk = _matmul_tiles(m, n, k, jnp.dtype(out_dtype).itemsize)
    nk = k // tk
    dot = {"nn": _dot, "tn": _dot_tn, "nt": _dot_nt}[mode]

    def kern(a_ref, b_ref, o_ref, *acc):
        if nk == 1:
            o_ref[...] = dot(a_ref[...], b_ref[...]).astype(o_ref.dtype)
            return
        acc_ref, = acc
        kk = pl.program_id(2)

        @pl.when(kk == 0)
        def _():
            acc_ref[...] = jnp.zeros_like(acc_ref)

        acc_ref[...] += dot(a_ref[...], b_ref[...])

        @pl.when(kk == nk - 1)
        def _():
            o_ref[...] = acc_ref[...].astype(o_ref.dtype)

    a_spec = (pl.BlockSpec((tk, tm), lambda i, j, kk: (kk, i)) if mode == "tn"
              else pl.BlockSpec((tm, tk), lambda i, j, kk: (i, kk)))
    b_spec = (pl.BlockSpec((tn, tk), lambda i, j, kk: (j, kk)) if mode == "nt"
              else pl.BlockSpec((tk, tn), lambda i, j, kk: (kk, j)))
    return pl.pallas_call(
        kern, name=name, grid=(m // tm, n // tn, nk), in_specs=[a_spec, b_spec],
        out_specs=pl.BlockSpec((tm, tn), lambda i, j, kk: (i, j)),
        out_shape=jax.ShapeDtypeStruct((m, n), out_dtype),
        scratch_shapes=[pltpu.VMEM((tm, tn), F32)] if nk > 1 else [],
        compiler_params=pltpu.CompilerParams(
            dimension_semantics=("parallel", "parallel", "arbitrary"), vmem_limit_bytes=VMEM_LIMIT),
    )(a.astype(MXU_DTYPE), b.astype(MXU_DTYPE))


def _rowwise(body, name, rows, tile, row_ins, full_ins, row_outs, acc_outs=(), scratch=(),
             reverse=False, sequential=False):
    n = rows // tile

    def ridx(i):
        return (n - 1 - i) if reverse else i

    in_specs, args = [], []
    for arr, width, cb in row_ins:
        in_specs.append(pl.BlockSpec((tile, width), lambda i, cb=cb: (ridx(i), cb)))
        args.append(arr)
    for arr in full_ins:
        in_specs.append(pl.BlockSpec(arr.shape, lambda i, nd=arr.ndim: (0,) * nd))
        args.append(arr)
    out_shape = [jax.ShapeDtypeStruct((rows, w), dt) for w, dt in row_outs]
    out_shape += [jax.ShapeDtypeStruct(s, dt) for s, dt in acc_outs]
    out_specs = [pl.BlockSpec((tile, w), lambda i: (ridx(i), 0)) for w, dt in row_outs]
    out_specs += [pl.BlockSpec(s, lambda i, nd=len(s): (0,) * nd) for s, dt in acc_outs]

    def kern(*refs):
        body(pl.program_id(0), *refs)

    sem = "arbitrary" if (acc_outs or sequential) else "parallel"
    return pl.pallas_call(
        kern, name=name, grid=(n,), in_specs=in_specs, out_specs=out_specs, out_shape=out_shape,
        scratch_shapes=list(scratch),
        compiler_params=pltpu.CompilerParams(dimension_semantics=(sem,), vmem_limit_bytes=VMEM_LIMIT),
    )(*args)


def _ln_stats(u):
    mu = jnp.mean(u, axis=-1, keepdims=True)
    xc = u - mu
    var = jnp.mean(xc * xc, axis=-1, keepdims=True)
    return xc, lax.rsqrt(var + LN_EPS)


def _ln_fwd(a, b, g, beta, name):
    rows, d = a.shape
    has_b = b is not None

    def body(i, *refs):
        if has_b:
            a_ref, b_ref, g_ref, be_ref, h_ref, hb_ref = refs
            u = ALPHA * a_ref[...] + b_ref[...]
        else:
            a_ref, g_ref, be_ref, h_ref, hb_ref = refs
            u = a_ref[...]
        xc, rstd = _ln_stats(u)
        y = xc * rstd * g_ref[...] + be_ref[...]
        h_ref[...] = y
        hb_ref[...] = y.astype(hb_ref.dtype)

    row_ins = [(a, d, 0)] + ([(b, d, 0)] if has_b else [])
    return _rowwise(body, name, rows, min(ROW_TILE, rows), row_ins,
                    [g.reshape(1, d), beta.reshape(1, d)], [(d, F32), (d, MXU_DTYPE)])


def _ln_bwd(a, b, g, dy1, dy2, c1, name):
    rows, d = a.shape
    has_b = b is not None
    has_2 = dy2 is not None

    def body(i, *refs):
        refs = list(refs)
        a_ref = refs.pop(0)
        u = a_ref[...]
        if has_b:
            u = ALPHA * u + refs.pop(0)[...]
        dy = c1 * refs.pop(0)[...]
        if has_2:
            dy = dy + refs.pop(0)[...]
        g_ref, du_ref, dub_ref, dg_ref, db_ref = refs

        @pl.when(i == 0)
        def _():
            dg_ref[...] = jnp.zeros_like(dg_ref)
            db_ref[...] = jnp.zeros_like(db_ref)

        xc, rstd = _ln_stats(u)
        xhat = xc * rstd
        dxh = dy * g_ref[...]
        m1 = jnp.mean(dxh, axis=-1, keepdims=True)
        m2 = jnp.mean(dxh * xhat, axis=-1, keepdims=True)
        du = rstd * (dxh - m1 - xhat * m2)
        du_ref[...] = du
        dub_ref[...] = du.astype(dub_ref.dtype)
        dg_ref[...] += jnp.sum(dy * xhat, axis=0, keepdims=True)
        db_ref[...] += jnp.sum(dy, axis=0, keepdims=True)

    row_ins = [(a, d, 0)] + ([(b, d, 0)] if has_b else []) + [(dy1, d, 0)] + ([(dy2, d, 0)] if has_2 else [])
    return _rowwise(body, name, rows, min(ROW_TILE, rows), row_ins, [g.reshape(1, d)],
                    [(d, F32), (d, MXU_DTYPE)], [((1, d), F32), ((1, d), F32)])


def _loss_grad(h, target):
    rows, d = h.shape

    def body(i, h_ref, t_ref, dh_ref, acc_ref):
        @pl.when(i == 0)
        def _():
            acc_ref[...] = jnp.zeros_like(acc_ref)

        e = h_ref[...] - t_ref[...]
        dh_ref[...] = e * (1.0 / d)
        acc_ref[...] += jnp.sum(e * e, axis=0, keepdims=True)

    return _rowwise(body, "loss_grad", rows, ROW_TILE, [(h, d, 0), (target, d, 0)], [],
                    [(d, F32)], [((1, d), F32)])


def _gate_fwd(groups, pb):
    rows = pb.shape[0]
    w = GROUP_W * len(groups)

    def body(i, *refs):
        g = refs[4][...]
        mixed = jnp.concatenate([r[...] for r in refs[:4]], axis=1)
        refs[5][...] = (mixed * (g / (1.0 + jnp.exp(-g)))).astype(refs[5].dtype)

    return _rowwise(body, "gate_fwd", rows, ROW_TILE, [(o, GROUP_W, 0) for o in groups] + [(pb, w, 0)], [],
                    [(w, MXU_DTYPE)])[0]


def _gate_bwd(dgated, groups, pb):
    rows = pb.shape[0]
    w = GROUP_W * len(groups)

    def body(i, *refs):
        dg = refs[0][...]
        mixed = jnp.concatenate([r[...] for r in refs[1:5]], axis=1)
        g = refs[5][...]
        dm_ref, dgate_ref = refs[6], refs[7]
        sig = 1.0 / (1.0 + jnp.exp(-g))
        dm_ref[...] = dg * (g * sig)
        dgate_ref[...] = (dg * mixed * (sig * (1.0 + g * (1.0 - sig)))).astype(dgate_ref.dtype)

    return _rowwise(body, "gate_bwd", rows, ROW_TILE,
                    [(dgated, w, 0)] + [(o, GROUP_W, 0) for o in groups] + [(pb, w, 0)], [],
                    [(w, F32), (w, MXU_DTYPE)])


def _tri(n, kind):
    r = np.arange(n)[:, None]
    c = np.arange(n)[None, :]
    m = {"lower_incl": r >= c, "upper_incl": r <= c, "row_gt_col": r > c, "row_lt_col": r < c}[kind]
    return jnp.asarray(m.astype(np.float32), dtype=MXU_DTYPE)


def _forget_fwd(pb, bias_row):
    rows = pb.shape[0]
    tile = min(ROW_TILE, rows)

    def body(i, x_ref, b_ref, l_ref, o_ref, carry_ref):
        @pl.when(i == 0)
        def _():
            carry_ref[...] = jnp.zeros_like(carry_ref)

        xx = x_ref[...] + b_ref[...]
        lane = lax.broadcasted_iota(jnp.int32, xx.shape, 1)
        lf = jnp.where(lane < N_HEADS, -_softplus(-xx), 0.0)
        o_ref[...] = _dot_exact_l(l_ref[...], lf) + carry_ref[...]
        carry_ref[...] += jnp.sum(lf, axis=0, keepdims=True)

    return _rowwise(body, "forget_fwd", rows, tile, [(pb, 128, PB_MISC_BLK)],
                    [bias_row, _tri(tile, "lower_incl")], [(128, F32)],
                    scratch=[pltpu.VMEM((1, 128), F32)], sequential=True)[0]


def _forget_bwd(pb, bias_row, dfc):
    rows = pb.shape[0]
    tile = min(ROW_TILE, rows)

    def body(i, x_ref, df_ref, b_ref, u_ref, o_ref, db_ref, carry_ref):
        @pl.when(i == 0)
        def _():
            carry_ref[...] = jnp.zeros_like(carry_ref)
            db_ref[...] = jnp.zeros_like(db_ref)

        df = df_ref[...]
        sfx = _dot_exact_l(u_ref[...], df) + carry_ref[...]
        carry_ref[...] += jnp.sum(df, axis=0, keepdims=True)
        xx = x_ref[...] + b_ref[...]
        lane = lax.broadcasted_iota(jnp.int32, xx.shape, 1)
        dl = jnp.where(lane < N_HEADS, sfx / (1.0 + jnp.exp(xx)), 0.0)
        o_ref[...] = dl
        db_ref[...] += jnp.sum(dl, axis=0, keepdims=True)

    return _rowwise(body, "forget_bwd", rows, tile,
                    [(pb, 128, PB_MISC_BLK), (dfc, 128, 0)],
                    [bias_row, _tri(tile, "upper_incl")], [(128, F32)], [((1, 128), F32)],
                    scratch=[pltpu.VMEM((1, 128), F32)], reverse=True, sequential=True)


def _rope_tables(s):
    half = MLA_ROPE // 2
    inv_freq = ROPE_THETA ** (-jnp.arange(half, dtype=F32) / half)
    ang = jnp.arange(s).astype(F32)[:, None] * inv_freq[None, :]
    cos2 = jnp.tile(jnp.cos(ang), (1, 2))
    sin2 = jnp.tile(jnp.sin(ang), (1, 2))
    cx = jnp.tile(cos2, (1, N_HEADS))
    sx = jnp.tile(sin2, (1, N_HEADS))
    cq = jnp.concatenate([jnp.ones((s, GROUP_W), F32), cx], axis=1)
    sq = jnp.concatenate([jnp.zeros((s, GROUP_W), F32), sx], axis=1)
    pad = ((0, 0), (MISC_KROT, 128 - MISC_KROT - MLA_ROPE))
    ck = jnp.pad(cos2, pad)
    sk = jnp.pad(sin2, pad)
    return dict(cq=cq, sq=sq, ck=ck, sk=sk, cx=cx, sx=sx)


def _rot_matrix(width, bases):
    half = MLA_ROPE // 2
    p = np.zeros((width, width), np.float32)
    for b in bases:
        for i in range(half):
            p[b + half + i, b + i] = -1.0
            p[b + i, b + half + i] = 1.0
    return p


def _rope_matrices():
    pq = _rot_matrix(MLA_Q_COLS, [GROUP_W + h * MLA_ROPE for h in range(N_HEADS)])
    pk = _rot_matrix(128, [MISC_KROT])
    p4 = _rot_matrix(128, [h * MLA_ROPE for h in range(N_HEADS)])
    a = np.zeros((128, 128), np.float32)
    for h in range(N_HEADS):
        for r in range(MLA_ROPE):
            a[h * MLA_ROPE + r, MISC_KROT + r] = 1.0
    cast = lambda m: jnp.asarray(m, dtype=MXU_DTYPE)
    return dict(pq=cast(pq), pqt=cast(pq.T), pk=cast(pk), spread=cast(a.T), xa=cast(a), xb=cast(p4.T @ a))


def _rms(c, g):
    r = lax.rsqrt(jnp.mean(c * c, axis=-1, keepdims=True) + RMS_EPS)
    return c * r * g


def _mla_q_fwd(pb, g, w_up, tabs, mats):
    rows = pb.shape[0]

    def body(i, c_ref, cos_ref, sin_ref, g_ref, w_ref, p_ref, q_ref, cn_ref):
        cn = _rms(c_ref[...], g_ref[...]).astype(cn_ref.dtype)
        cn_ref[...] = cn
        q = _dot(cn, w_ref[...])
        q_ref[...] = (q * cos_ref[...] + _dot_exact_r(q, p_ref[...]) * sin_ref[...]).astype(q_ref.dtype)

    return _rowwise(body, "mla_q_fwd", rows, ROW_TILE,
                    [(pb, MLA_Q_RANK, PB_CQ_BLK), (tabs["cq"], MLA_Q_COLS, 0), (tabs["sq"], MLA_Q_COLS, 0)],
                    [g.reshape(1, MLA_Q_RANK), w_up.astype(MXU_DTYPE), mats["pq"]],
                    [(MLA_Q_COLS, MXU_DTYPE), (MLA_Q_RANK, MXU_DTYPE)])


def _mla_kv_fwd(pb, g, w_up, tabs, mats):
    rows = pb.shape[0]

    def body(i, c_ref, x_ref, cos_ref, sin_ref, g_ref, w_ref, p_ref, sp_ref, kv_ref, cn_ref, kr_ref):
        cn = _rms(c_ref[...], g_ref[...]).astype(cn_ref.dtype)
        cn_ref[...] = cn
        kv_ref[...] = _dot(cn, w_ref[...]).astype(kv_ref.dtype)
        xx = x_ref[...]
        kr = xx * cos_ref[...] + _dot_exact_r(xx, p_ref[...]) * sin_ref[...]
        kr_ref[...] = _dot_exact_r(kr, sp_ref[...]).astype(kr_ref.dtype)

    return _rowwise(body, "mla_kv_fwd", rows, ROW_TILE,
                    [(pb, MLA_KV_RANK, PB_CKV_BLK), (pb, 128, PB_MISC_BLK), (tabs["ck"], 128, 0), (tabs["sk"], 128, 0)],
                    [g.reshape(1, MLA_KV_RANK), w_up.astype(MXU_DTYPE), mats["pk"], mats["spread"]],
                    [(MLA_KV_COLS, MXU_DTYPE), (MLA_KV_RANK, MXU_DTYPE), (128, MXU_DTYPE)])


def _rope_q_bwd(dq_full, tabs, mats):
    rows, nq = dq_full.shape

    def body(i, d_ref, cos_ref, sin_ref, pt_ref, o_ref):
        d = d_ref[...]
        o_ref[...] = (d * cos_ref[...] + _dot_exact_r(d * sin_ref[...], pt_ref[...])).astype(o_ref.dtype)

    return _rowwise(body, "rope_q_bwd", rows, ROW_TILE,
                    [(dq_full, nq, 0), (tabs["cq"], nq, 0), (tabs["sq"], nq, 0)], [mats["pqt"]],
                    [(nq, MXU_DTYPE)])[0]


def _rope_k_bwd(dkr, tabs, mats):
    rows = dkr.shape[0]

    def body(i, d_ref, cos_ref, sin_ref, a_ref, b_ref, o_ref):
        d = d_ref[...]
        o_ref[...] = _dot_exact_r(d * cos_ref[...], a_ref[...]) + _dot_exact_r(d * sin_ref[...], b_ref[...])

    return _rowwise(body, "rope_k_bwd", rows, ROW_TILE,
                    [(dkr, 128, 0), (tabs["cx"], 128, 0), (tabs["sx"], 128, 0)], [mats["xa"], mats["xb"]],
                    [(128, F32)])[0]


def _rms_bwd(pb, width, col_blk, g, dy, name):
    rows = pb.shape[0]

    def body(i, c_ref, dy_ref, g_ref, dc_ref, dg_ref):
        @pl.when(i == 0)
        def _():
            dg_ref[...] = jnp.zeros_like(dg_ref)

        c = c_ref[...]
        dy = dy_ref[...]
        r = lax.rsqrt(jnp.mean(c * c, axis=-1, keepdims=True) + RMS_EPS)
        dyg = dy * g_ref[...]
        dc = r * dyg - c * (r * r * r) * jnp.mean(c * dyg, axis=-1, keepdims=True)
        dc_ref[...] = dc.astype(dc_ref.dtype)
        dg_ref[...] += jnp.sum(dy * c * r, axis=0, keepdims=True)

    return _rowwise(body, name, rows, ROW_TILE, [(pb, width, col_blk), (dy, width, 0)], [g.reshape(1, width)],
                    [(width, MXU_DTYPE)], [((1, width), F32)])


def _att_params(parallel):
    return pltpu.CompilerParams(dimension_semantics=("parallel" if parallel else "arbitrary",),
                                vmem_limit_bytes=VMEM_LIMIT)


def _blk_off(j, t):
    return j * t if isinstance(j, int) else pl.multiple_of(j * t, t)


def _causal_mask(t, strict):
    r = lax.broadcasted_iota(jnp.int32, (t, t), 0)
    c = lax.broadcasted_iota(jnp.int32, (t, t), 1)
    return (c < r) if strict else (c <= r)


def _lane_mask(kind, head, rows):
    lane = lax.broadcasted_iota(jnp.int32, (rows, PAIR), 1)
    if kind == "pair":
        return (lane < HEAD_DIM) if head % 2 == 0 else (lane >= HEAD_DIM)
    return (lane >= MLA_ROPE * head) & (lane < MLA_ROPE * (head + 1))


def _row_spec(t, cb, width=PAIR):
    return pl.BlockSpec((t, width), lambda i, cb=cb: (i, cb))


def _whole_spec(rows, cb, width=PAIR):
    return pl.BlockSpec((rows, width), lambda i, cb=cb: (0, cb), pipeline_mode=pl.Buffered(1))


def _is_pow2(x):
    return float(np.frexp(x)[0]) == 0.5


def _masked_heads(blocks, kinds, pair, dtype, scale=None):
    out = []
    for e in range(2):
        head = 2 * pair + e
        parts = [jnp.where(_lane_mask(k, head, b.shape[0]), b.astype(F32) * (1.0 if scale is None else scale),
                           0.0).astype(dtype)
                 for b, k in zip(blocks, kinds)]
        out.append(parts[0] if len(parts) == 1 else jnp.concatenate(parts, axis=1))
    return out


def _logit_reach(qh, kmax2, head):
    q32 = qh.astype(F32)
    return jnp.sqrt(jnp.sum(q32 * q32, axis=1, keepdims=True) * _col(kmax2, head)) * REACH_SLACK


def _forget_top(ft_ref, head, off):
    return jnp.max(-ft_ref[head:head + 1, pl.ds(off, PAIR)])


def _col(block, idx):
    lane = lax.broadcasted_iota(jnp.int32, block.shape, 1)
    return jnp.sum(jnp.where(lane == idx, block, 0.0), axis=1, keepdims=True)


def _scatter_cols(cols, t):
    lane = lax.broadcasted_iota(jnp.int32, (t, PAIR), 1)
    out = jnp.zeros((t, PAIR), F32)
    for idx, c in cols.items():
        out = out + jnp.where(lane == idx, c, 0.0)
    return out


def _take_heads(per_head, pair):
    return jnp.where(_lane_mask("pair", 0, per_head[0].shape[0]), per_head[0], per_head[1])


class _Parts:
    def __init__(self, q_parts, k_parts, v_parts, tq, sk):
        self.kinds = [[kind for _, _, kind in q_parts[p]] for p in range(2)]
        self.nparts = len(q_parts[0])
        self.q_specs = [_row_spec(tq, cb) for p in range(2) for _, cb, _ in q_parts[p]]
        self.q_args = [a for p in range(2) for a, _, _ in q_parts[p]]
        self.k_specs = [_whole_spec(sk, cb) for p in range(2) for _, cb, _ in k_parts[p]]
        self.k_args = [a for p in range(2) for a, _, _ in k_parts[p]]
        self.v_specs = [_whole_spec(sk, cb) for _, cb in v_parts]
        self.v_args = [a for a, _ in v_parts]
        self.width = PAIR * self.nparts

    def split(self, refs):
        n = self.nparts
        refs = list(refs)
        q = [refs[p * n:(p + 1) * n] for p in range(2)]
        k = [refs[2 * n + p * n:2 * n + (p + 1) * n] for p in range(2)]
        v = refs[4 * n:4 * n + 2]
        return q, k, v, refs[4 * n + 2:]

    def k_block(self, k_refs, off, t):
        blks = [r[pl.ds(off, t), :] for r in k_refs]
        return blks[0] if len(blks) == 1 else jnp.concatenate(blks, axis=1)


def _key_norm_max(src, k_blk):
    rows = src.shape[0]

    def body(i, k0_ref, k1_ref, o_ref):
        @pl.when(i == 0)
        def _():
            o_ref[...] = jnp.zeros_like(o_ref)

        cols = {}
        for p, ref in enumerate((k0_ref, k1_ref)):
            k32 = ref[...].astype(F32)
            for e in range(2):
                sq = jnp.sum(jnp.where(_lane_mask("pair", e, k32.shape[0]), k32 * k32, 0.0), axis=1, keepdims=True)
                cols[2 * p + e] = jnp.max(sq, axis=0, keepdims=True)
        lane = lax.broadcasted_iota(jnp.int32, (1, PAIR), 1)
        o_ref[...] = jnp.maximum(o_ref[...], sum(jnp.where(lane == h, c, 0.0) for h, c in cols.items()))

    return _rowwise(body, "key_norm_max", rows, ROW_TILE, [(src, PAIR, k_blk), (src, PAIR, k_blk + 1)], [], [],
                    [((1, PAIR), F32)])[0]


def _softmax_fwd(q_parts, k_parts, v_parts, sq, sk, scale, causal, bias, name):
    tq = min(SOFTMAX_TILE, sq)
    tk = tq if causal else min(SOFTMAX_TILE, sk)
    nkv = sk // tk
    pp = _Parts(q_parts, k_parts, v_parts, tq, sk)

    def kern(*refs):
        q_refs, k_refs, v_refs, rest = pp.split(refs)
        if bias is not None:
            fc_ref, ft_ref, kmax_ref, o_ref, lse_ref = rest
            fcb = fc_ref[...]
        else:
            o_ref, lse_ref = rest
        i = pl.program_id(0)
        fold = _is_pow2(scale)
        head_on = [jnp.where(_lane_mask("pair", e, tk), 1.0, 0.0).astype(MXU_DTYPE) for e in range(2)]
        head_off = [jnp.where(_lane_mask("pair", e, tk), 0.0, 1.0).astype(MXU_DTYPE) for e in range(2)]
        lse_cols = {}
        for p in range(2):
            qm = _masked_heads([r[...] for r in q_refs[p]], pp.kinds[p], p, MXU_DTYPE, scale if fold else None)
            if bias is not None:
                reach = [_logit_reach(qm[e], kmax_ref[...], 2 * p + e) for e in range(2)]

            def block(j, carry, masked, p=p, qm=qm):
                off = _blk_off(j, tk)
                kb = pp.k_block(k_refs[p], off, tk)
                vb = v_refs[p][pl.ds(off, tk), :]
                out = []
                for e in range(2):
                    h = 2 * p + e
                    m, acc = carry[e]
                    s = _dot_nt(qm[e], kb)
                    if not fold:
                        s = s * scale
                    if bias is not None:
                        s = s - ft_ref[h:h + 1, pl.ds(off, tk)]
                    if masked:
                        s = jnp.where(_causal_mask(tq, False), s, NEG)
                    m_new = jnp.maximum(m, jnp.max(s, axis=1, keepdims=True))
                    pr = jnp.exp(s - m_new).astype(MXU_DTYPE)
                    out.append((m_new, jnp.exp(m - m_new) * acc + _dot(pr, vb * head_on[e] + head_off[e])))
                return tuple(out)

            carry = tuple((jnp.full((tq, 1), NEG, F32), jnp.zeros((tq, PAIR), F32)) for _ in range(2))
            if causal and bias is not None:
                def alive(c, j, p=p, reach=reach):
                    top = [_forget_top(ft_ref, 2 * p + e, _blk_off(j, tk)) for e in range(2)]
                    return functools.reduce(jnp.maximum, [jnp.max(reach[e] + top[e] - c[e][0]) for e in range(2)])

                def step(state, block=block, alive=alive):
                    n, _, c = state
                    c = block(i - 1 - n, c, False)
                    return n + 1, alive(c, i - 1 - n), c

                carry = block(i, carry, True)
                _, _, carry = lax.while_loop(lambda st: jnp.logical_and(st[0] < i, st[1] > DEAD_LOGIT), step,
                                             (jnp.int32(0), alive(carry, i), carry))
            elif causal:
                carry = lax.fori_loop(0, i, lambda j, c, block=block: block(j, c, False), carry)
                carry = block(i, carry, True)
            else:
                for j in range(nkv):
                    carry = block(j, carry, False)
            outs = []
            for e in range(2):
                m, acc = carry[e]
                l = _col(acc, HEAD_DIM * (1 - e))
                outs.append(acc / l)
                lse_cols[2 * p + e] = m + jnp.log(l) + (_col(fcb, 2 * p + e) if bias is not None else 0.0)
            o_ref[:, p * PAIR:(p + 1) * PAIR] = _take_heads(outs, p)
        lse_ref[...] = _scatter_cols(lse_cols, tq)

    in_specs = pp.q_specs + pp.k_specs + pp.v_specs
    args = pp.q_args + pp.k_args + pp.v_args
    if bias is not None:
        in_specs += [_row_spec(tq, 0), pl.BlockSpec((8, sk), lambda i: (0, 0), pipeline_mode=pl.Buffered(1)),
                     pl.BlockSpec((1, PAIR), lambda i: (0, 0))]
        args += list(bias)
    return pl.pallas_call(
        kern, name=name, grid=(sq // tq,), in_specs=in_specs,
        out_specs=[_row_spec(tq, 0, GROUP_W), _row_spec(tq, 0)],
        out_shape=[jax.ShapeDtypeStruct((sq, GROUP_W), F32), jax.ShapeDtypeStruct((sq, PAIR), F32)],
        compiler_params=_att_params(True),
    )(*args)


def _softmax_bwd(q_parts, k_parts, v_parts, o, lse, do, do_blk, sq, sk, scale, causal, bias, name):
    tq = min(SOFTMAX_TILE, sq)
    tk = tq if causal else min(SOFTMAX_TILE, sk)
    nkv = sk // tk
    pp = _Parts(q_parts, k_parts, v_parts, tq, sk)
    quad = pp.nparts == 2
    wq = GROUP_W + (PAIR if quad else 0)

    def kern(*refs):
        q_refs, k_refs, v_refs, rest = pp.split(refs)
        if bias is not None:
            o_ref, lse_ref, do_ref, fc_ref, ft_ref, kmax_ref, dq_ref, dk_ref, dv_ref, dfq_ref, dfk_ref = rest
            fcb = fc_ref[...]
        else:
            o_ref, lse_ref, do_ref, dq_ref, dk_ref, dv_ref = rest
        i = pl.program_id(0)
        fold = _is_pow2(scale)

        @pl.when(i == 0)
        def _():
            dk_ref[...] = jnp.zeros_like(dk_ref)
            dv_ref[...] = jnp.zeros_like(dv_ref)
            if bias is not None:
                dfk_ref[...] = jnp.zeros_like(dfk_ref)

        lse_b = lse_ref[...]
        qm, dom, delta, lse_h = [], [], [], []
        for p in range(2):
            qm += _masked_heads([r[...] for r in q_refs[p]], pp.kinds[p], p, MXU_DTYPE, scale if fold else None)
            do_p = do_ref[:, p * PAIR:(p + 1) * PAIR]
            dom += _masked_heads([do_p], ["pair"], p, MXU_DTYPE)
            prod = do_p * o_ref[:, p * PAIR:(p + 1) * PAIR]
            for e in range(2):
                h = 2 * p + e
                delta.append(jnp.sum(jnp.where(_lane_mask("pair", h, tq), prod, 0.0), axis=1, keepdims=True))
                lse_h.append(_col(lse_b, h) - (_col(fcb, h) if bias is not None else 0.0))

        def block(j, carry, masked):
            off = _blk_off(j, tk)
            out = []
            for p in range(2):
                kb = pp.k_block(k_refs[p], off, tk)
                vb = v_refs[p][pl.ds(off, tk), :]
                dk_acc = jnp.zeros((tk, pp.width), F32)
                dv_acc = jnp.zeros((tk, PAIR), F32)
                for e in range(2):
                    h = 2 * p + e
                    dq, dfq = carry[h]
                    s = _dot_nt(qm[h], kb)
                    if not fold:
                        s = s * scale
                    if bias is not None:
                        s = s - ft_ref[h:h + 1, pl.ds(off, tk)]
                    if masked:
                        s = jnp.where(_causal_mask(tq, False), s, NEG)
                    pr = jnp.exp(s - lse_h[h])
                    ds = pr * (_dot_nt(dom[h], vb) - delta[h])
                    dsb = (ds if fold else ds * scale).astype(MXU_DTYPE)
                    dv_acc = dv_acc + _dot_tn(pr.astype(MXU_DTYPE), dom[h])
                    dk_acc = dk_acc + _dot_tn(dsb, qm[h])
                    dq = dq + _dot(dsb, kb)
                    if bias is not None:
                        dfq = dfq + jnp.sum(ds, axis=1, keepdims=True)
                        dfk_ref[h:h + 1, pl.ds(off, tk)] -= jnp.sum(ds, axis=0, keepdims=True)
                    out.append((dq, dfq))
                dv_ref[pl.ds(off, tk), p * PAIR:(p + 1) * PAIR] += dv_acc
                dk_ref[pl.ds(off, tk), p * PAIR:(p + 1) * PAIR] += dk_acc[:, :PAIR]
                if quad:
                    dk_ref[pl.ds(off, tk), GROUP_W:] += dk_acc[:, PAIR:]
            return tuple(out)

        carry = tuple((jnp.zeros((tq, pp.width), F32), jnp.zeros((tq, 1), F32)) for _ in range(N_HEADS))
        if causal and bias is not None:
            reach = [_logit_reach(qm[h], kmax_ref[...], h) - lse_h[h] for h in range(N_HEADS)]

            def alive(j):
                return functools.reduce(jnp.maximum, [jnp.max(reach[h] + _forget_top(ft_ref, h, _blk_off(j, tk)))
                                                      for h in range(N_HEADS)])

            def step(state):
                n, _, c = state
                return n + 1, alive(i - 1 - n), block(i - 1 - n, c, False)

            carry = block(i, carry, True)
            _, _, carry = lax.while_loop(lambda st: jnp.logical_and(st[0] < i, st[1] > DEAD_LOGIT), step,
                                         (jnp.int32(0), alive(i), carry))
        elif causal:
            carry = lax.fori_loop(0, i, lambda j, c: block(j, c, False), carry)
            carry = block(i, carry, True)
        else:
            for j in range(nkv):
                carry = block(j, carry, False)
        dqs = [c[0] * scale if fold else c[0] for c in carry]
        for p in range(2):
            dq_ref[:, p * PAIR:(p + 1) * PAIR] = _take_heads([dqs[2 * p + e][:, :PAIR] for e in range(2)], p)
        if quad:
            dq_ref[:, GROUP_W:] = sum(jnp.where(_lane_mask("quad", h, tq), dqs[h][:, PAIR:], 0.0)
                                      for h in range(N_HEADS))
        if bias is not None:
            dfq_ref[...] = _scatter_cols({h: carry[h][1] for h in range(N_HEADS)}, tq)

    acc_spec = lambda rows, width: pl.BlockSpec((rows, width), lambda i: (0, 0), pipeline_mode=pl.Buffered(1))
    in_specs = pp.q_specs + pp.k_specs + pp.v_specs + [_row_spec(tq, 0, GROUP_W), _row_spec(tq, 0),
                                                       _row_spec(tq, do_blk, GROUP_W)]
    args = pp.q_args + pp.k_args + pp.v_args + [o, lse, do]
    out_specs = [_row_spec(tq, 0, wq), acc_spec(sk, wq), acc_spec(sk, GROUP_W)]
    out_shape = [jax.ShapeDtypeStruct((sq, wq), F32), jax.ShapeDtypeStruct((sk, wq), F32),
                 jax.ShapeDtypeStruct((sk, GROUP_W), F32)]
    if bias is not None:
        in_specs += [_row_spec(tq, 0), pl.BlockSpec((8, sk), lambda i: (0, 0), pipeline_mode=pl.Buffered(1)),
                     pl.BlockSpec((1, PAIR), lambda i: (0, 0))]
        args += list(bias)
        out_specs += [_row_spec(tq, 0), acc_spec(8, sk)]
        out_shape += [jax.ShapeDtypeStruct((sq, PAIR), F32), jax.ShapeDtypeStruct((8, sk), F32)]
    return pl.pallas_call(
        kern, name=name, grid=(sq // tq,), in_specs=in_specs, out_specs=out_specs, out_shape=out_shape,
        compiler_params=_att_params(False),
    )(*args)


def _sb_logs(qh, kb, valid):
    z = _dot_nt(qh, kb)
    sp = _softplus(z)
    lk = -sp
    if valid is not None:
        lk = jnp.where(valid, lk, 0.0)
    return lk, z - sp


def _sb_valid(d, tq, tk):
    r = lax.broadcasted_iota(jnp.int32, (tq, tk), 0)
    c = lax.broadcasted_iota(jnp.int32, (tq, tk), 1)
    return c + d * tk < r


def _tri_sums(xs, tri):
    t = xs[0].shape[0]
    pieces = [_split2(x) for x in xs]
    hi = _dot(jnp.concatenate([pc[0] for pc in pieces], axis=0), tri)
    lo = _dot(jnp.concatenate([pc[1] for pc in pieces], axis=0), tri)
    return [hi[n * t:(n + 1) * t] + lo[n * t:(n + 1) * t] for n in range(len(xs))]


def _sb_fwd(src, q_blk, k_blk, v_blk, s, scale, name):
    assert _is_pow2(scale) and s // ATT_TILE <= SB_SLOT
    tq, t = min(SB_QUERY_TILE, s), min(ATT_TILE, s)
    band = tq // t
    pair = lambda blk: [[(src, blk + p, "pair")] for p in range(2)]
    pp = _Parts(pair(q_blk), pair(k_blk), [(src, v_blk + p) for p in range(2)], tq, s)

    def kern(*refs):
        q_refs, k_refs, v_refs, (tri_ref, o_ref, rm_ref, cnt_ref) = pp.split(refs)
        i = pl.program_id(0)
        tri = tri_ref[...]
        lane = lax.broadcasted_iota(jnp.int32, (tq, PAIR), 1)
        qm = []
        for p in range(2):
            qm += _masked_heads([q_refs[p][0][...]], ["pair"], p, MXU_DTYPE, scale)

        def block(j, carry, valid):
            accs, rights, rm = carry
            off = _blk_off(j, t)
            kbs = [k_refs[p][0][pl.ds(off, t), :] for p in range(2)]
            vbs = [v_refs[p][pl.ds(off, t), :] for p in range(2)]
            logs = [_sb_logs(qm[h], kbs[h // 2], valid) for h in range(N_HEADS)]
            tails = _tri_sums([lg[0] for lg in logs], tri)
            new_acc, new_right = [], []
            for h in range(N_HEADS):
                lk, ls = logs[h]
                w = jnp.exp(ls + tails[h] + rights[h])
                if valid is not None:
                    w = jnp.where(valid, w, 0.0)
                new_acc.append(accs[h] + _dot(w.astype(MXU_DTYPE), vbs[h // 2]))
                rm = rm + jnp.where(lane == SB_SLOT * h + j, rights[h], 0.0)
                new_right.append(rights[h] + jnp.sum(lk, axis=1, keepdims=True))
            return tuple(new_acc), tuple(new_right), rm

        carry = (tuple(jnp.zeros((tq, PAIR), F32) for _ in range(N_HEADS)),
                 tuple(jnp.zeros((tq, 1), F32) for _ in range(N_HEADS)), jnp.zeros((tq, PAIR), F32))
        for d in reversed(range(band)):
            carry = block(band * i + d, carry, _sb_valid(d, tq, t))

        def alive(c):
            return functools.reduce(jnp.maximum, [jnp.max(r) for r in c[1]])

        def step(state):
            n, _, c = state
            c = block(band * i - 1 - n, c, None)
            return n + 1, alive(c), c

        n_done, _, carry = lax.while_loop(lambda st: jnp.logical_and(st[0] < band * i, st[1] > EXP_UNDERFLOW),
                                          step, (jnp.int32(0), alive(carry), carry))
        cnt_ref[i] = n_done
        for p in range(2):
            o_ref[:, p * PAIR:(p + 1) * PAIR] = _take_heads([carry[0][2 * p + e] for e in range(2)], p)
        rm_ref[...] = carry[2]

    return pl.pallas_call(
        kern, name=name, grid=(s // tq,),
        in_specs=pp.q_specs + pp.k_specs + pp.v_specs + [pl.BlockSpec((t, t), lambda i: (0, 0))],
        out_specs=[_row_spec(tq, 0, GROUP_W), _row_spec(tq, 0), pl.BlockSpec(memory_space=pltpu.SMEM)],
        out_shape=[jax.ShapeDtypeStruct((s, GROUP_W), F32), jax.ShapeDtypeStruct((s, PAIR), F32),
                   jax.ShapeDtypeStruct((s // tq,), jnp.int32)],
        compiler_params=_att_params(False),
    )(*(pp.q_args + pp.k_args + pp.v_args + [_tri(t, "row_gt_col")]))


def _sb_bwd(src, q_blk, k_blk, v_blk, do, do_blk, rm, visited, s, scale, name):
    assert _is_pow2(scale)
    tq, t = min(SB_QUERY_TILE, s), min(ATT_TILE, s)
    band = tq // t
    pair = lambda blk: [[(src, blk + p, "pair")] for p in range(2)]
    pp = _Parts(pair(q_blk), pair(k_blk), [(src, v_blk + p) for p in range(2)], tq, s)

    def kern(*refs):
        q_refs, k_refs, v_refs, (do_ref, rm_ref, tri_ref, pre_ref, cnt_ref, dq_ref, dk_ref, dv_ref) = pp.split(refs)
        i = pl.program_id(0)

        @pl.when(i == 0)
        def _():
            dk_ref[...] = jnp.zeros_like(dk_ref)
            dv_ref[...] = jnp.zeros_like(dv_ref)

        rmb = rm_ref[...]
        tri = tri_ref[...]
        pre = pre_ref[...]
        qm, dom = [], []
        for p in range(2):
            qm += _masked_heads([q_refs[p][0][...]], ["pair"], p, MXU_DTYPE, scale)
            dom += _masked_heads([do_ref[:, p * PAIR:(p + 1) * PAIR]], ["pair"], p, MXU_DTYPE)

        def block(j, carry, valid):
            dqs, lefts = carry
            off = _blk_off(j, t)
            kbs = [k_refs[p][0][pl.ds(off, t), :] for p in range(2)]
            vbs = [v_refs[p][pl.ds(off, t), :] for p in range(2)]
            logs = [_sb_logs(qm[h], kbs[h // 2], valid) for h in range(N_HEADS)]
            tails = _tri_sums([lg[0] for lg in logs], tri)
            ws, gs = [], []
            for h in range(N_HEADS):
                lk, ls = logs[h]
                w = jnp.exp(ls + tails[h] + _col(rmb, SB_SLOT * h + j))
                if valid is not None:
                    w = jnp.where(valid, w, 0.0)
                ws.append(w)
                gs.append(_dot_nt(dom[h], vbs[h // 2]) * w)
            prefix = _tri_sums(gs, pre)
            new_dq, new_left = [], []
            dk_acc = [jnp.zeros((t, PAIR), F32) for _ in range(2)]
            dv_acc = [jnp.zeros((t, PAIR), F32) for _ in range(2)]
            for h in range(N_HEADS):
                lk, ls = logs[h]
                sig = jnp.exp(ls)
                dz = gs[h] * (1.0 - sig) - sig * (prefix[h] + lefts[h])
                if valid is not None:
                    dz = jnp.where(valid, dz, 0.0)
                dzb = dz.astype(MXU_DTYPE)
                dv_acc[h // 2] = dv_acc[h // 2] + _dot_tn(ws[h].astype(MXU_DTYPE), dom[h])
                dk_acc[h // 2] = dk_acc[h // 2] + _dot_tn(dzb, qm[h])
                new_dq.append(dqs[h] + _dot(dzb, kbs[h // 2]))
                new_left.append(lefts[h] + jnp.sum(gs[h], axis=1, keepdims=True))
            for p in range(2):
                dv_ref[pl.ds(off, t), p * PAIR:(p + 1) * PAIR] += dv_acc[p]
                dk_ref[pl.ds(off, t), p * PAIR:(p + 1) * PAIR] += dk_acc[p]
            return tuple(new_dq), tuple(new_left)

        carry = (tuple(jnp.zeros((tq, PAIR), F32) for _ in range(N_HEADS)),
                 tuple(jnp.zeros((tq, 1), F32) for _ in range(N_HEADS)))
        carry = lax.fori_loop(band * i - cnt_ref[i], band * i, lambda j, c: block(j, c, None), carry)
        for d in range(band):
            carry = block(band * i + d, carry, _sb_valid(d, tq, t))
        for p in range(2):
            dq_ref[:, p * PAIR:(p + 1) * PAIR] = _take_heads([carry[0][2 * p + e] * scale for e in range(2)], p)

    mspec = pl.BlockSpec((t, t), lambda i: (0, 0))
    acc_spec = pl.BlockSpec((s, GROUP_W), lambda i: (0, 0), pipeline_mode=pl.Buffered(1))
    return pl.pallas_call(
        kern, name=name, grid=(s // tq,),
        in_specs=pp.q_specs + pp.k_specs + pp.v_specs + [_row_spec(tq, do_blk, GROUP_W), _row_spec(tq, 0), mspec, mspec,
                                                         pl.BlockSpec(memory_space=pltpu.SMEM)],
        out_specs=[_row_spec(tq, 0, GROUP_W), acc_spec, acc_spec],
        out_shape=[jax.ShapeDtypeStruct((s, GROUP_W), F32)] * 3,
        compiler_params=_att_params(False),
    )(*(pp.q_args + pp.k_args + pp.v_args + [do, rm, _tri(t, "row_gt_col"), _tri(t, "row_lt_col"), visited]))


def _split_w_in(w):
    col = lambda n: w[:, _OFF[n]:_OFF[n + 1]]
    wa = jnp.concatenate([col(0), col(1), col(2), col(4), col(5), col(6), col(10)], axis=1)
    misc = jnp.concatenate([col(3), col(9), jnp.zeros((w.shape[0], 128 - 4 - MLA_ROPE), w.dtype)], axis=1)
    wb = jnp.concatenate([col(11), col(7), col(8), misc], axis=1)
    return wa, wb


def _merge_dw_in(dwp):
    a = lambda n: dwp[:, n * GROUP_W:(n + 1) * GROUP_W]
    b0 = PA_COLS
    gate = dwp[:, b0:b0 + 1024]
    cq = dwp[:, b0 + 1024:b0 + 1280]
    ckv = dwp[:, b0 + 1280:b0 + 1408]
    flog = dwp[:, b0 + 1408:b0 + 1412]
    krot = dwp[:, b0 + 1408 + MISC_KROT:b0 + 1408 + MISC_KROT + MLA_ROPE]
    return jnp.concatenate([a(0), a(1), a(2), flog, a(3), a(4), a(5), cq, ckv, krot, a(6), gate], axis=1)


def _heads_first(w, per_head, first):
    r = w.shape[0]
    w3 = w.reshape(r, N_HEADS, per_head)
    return jnp.concatenate([w3[:, :, :first].reshape(r, -1), w3[:, :, first:].reshape(r, -1)], axis=1)


def _heads_interleaved(w, per_head, first):
    r = w.shape[0]
    a = w[:, :N_HEADS * first].reshape(r, N_HEADS, first)
    b = w[:, N_HEADS * first:].reshape(r, N_HEADS, per_head - first)
    return jnp.concatenate([a, b], axis=2).reshape(r, N_HEADS * per_head)


def _pad_rows8(a):
    return a[:, :8].T


def _local_step(x2, mem2, tgt, p):
    s = x2.shape[0]
    nm = mem2.shape[0]
    head_scale = HEAD_DIM ** -0.5
    mla_scale = (HEAD_DIM + MLA_ROPE) ** -0.5
    tabs = _rope_tables(s)
    mats = _rope_matrices()
    pairs = lambda arr, blk: [[(arr, blk + q, "pair")] for q in range(2)]
    vals = lambda arr, blk: [(arr, blk + q) for q in range(2)]

    h, hb = _ln_fwd(x2, None, p["ln_in_g"], p["ln_in_b"], "ln_in_fwd")
    _, memn_b = _ln_fwd(mem2, None, p["mem_ln_g"], p["mem_ln_b"], "ln_mem_fwd")

    saved = []
    for l in range(DEPTH):
        wa, wb = _split_w_in(p["w_in"][l])
        wp = jnp.concatenate([wa, wb], axis=1)
        wq_up = _heads_first(p["w_mla_q_up"][l], HEAD_DIM + MLA_ROPE, HEAD_DIM)
        wkv_up = _heads_first(p["w_mla_kv_up"][l], 2 * HEAD_DIM, HEAD_DIM)
        bias_row = jnp.pad(p["b_forget"][l], (0, 128 - N_HEADS)).reshape(1, 128)
        pa = _matmul(hb, wa, MXU_DTYPE, "proj_a")
        pb = _matmul(hb, wb, F32, "proj_b")

        fc = _forget_fwd(pb, bias_row)
        fbias = (fc, _pad_rows8(fc), _key_norm_max(pa, 2))
        o_fox, lse_fox = _softmax_fwd(pairs(pa, 0), pairs(pa, 2), vals(pa, 4), s, s, head_scale, True, fbias,
                                      "fox_fwd")
        o_sb, *rm_sb = _sb_fwd(pa, 6, 8, 10, s, head_scale, "sb_fwd")

        qfull, cqn = _mla_q_fwd(pb, p["mla_q_norm_g"][l], wq_up, tabs, mats)
        kv, ckvn, kr4 = _mla_kv_fwd(pb, p["mla_kv_norm_g"][l], wkv_up, tabs, mats)
        mla_q = [[(qfull, q, "pair"), (qfull, 2, "quad")] for q in range(2)]
        mla_k = [[(kv, q, "pair"), (kr4, 0, "quad")] for q in range(2)]
        o_mla, lse_mla = _softmax_fwd(mla_q, mla_k, vals(kv, 2), s, s, mla_scale, True, None, "mla_fwd")

        mkv = _matmul(memn_b, p["w_mem_kv"][l], MXU_DTYPE, "mem_kv")
        o_mem, lse_mem = _softmax_fwd(pairs(pa, 12), pairs(mkv, 0), vals(mkv, 2), s, nm, head_scale, False, None,
                                      "mem_fwd")

        groups = (o_fox, o_sb, o_mla, o_mem)
        gated = _gate_fwd(groups, pb)
        y = _matmul(gated, p["w_out"][l], F32, "out_proj")
        saved.append(dict(h=h, hb=hb, y=y, wp=wp, wq_up=wq_up, wkv_up=wkv_up, bias_row=bias_row, pa=pa, pb=pb,
                          fbias=fbias, lse_fox=lse_fox, rm_sb=rm_sb, cqn=cqn, ckvn=ckvn, mla_q=mla_q, mla_k=mla_k,
                          kv=kv, lse_mla=lse_mla, mkv=mkv, lse_mem=lse_mem, groups=groups, gated=gated))
        h, hb = _ln_fwd(h, y, p["ln_g"][l], p["ln_b"][l], "ln_fwd")

    dh, sq_cols = _loss_grad(h, tgt)
    loss_sum = jnp.sum(sq_cols)

    grads = {k: [None] * DEPTH for k in ("w_in", "b_forget", "mla_q_norm_g", "w_mla_q_up", "mla_kv_norm_g",
                                         "w_mla_kv_up", "w_mem_kv", "w_out", "ln_g", "ln_b")}
    dmemn = []
    dy1, dy2, c1 = dh, None, 1.0
    for l in reversed(range(DEPTH)):
        r = saved[l]
        pa, pb = r["pa"], r["pb"]
        o_fox, o_sb, o_mla, o_mem = r["groups"]
        du, du_b, dg, db = _ln_bwd(r["h"], r["y"], p["ln_g"][l], dy1, dy2, c1, "ln_bwd")
        grads["ln_g"][l], grads["ln_b"][l] = dg[0], db[0]
        dgated = _matmul(du_b, p["w_out"][l], F32, "out_proj_dx", "nt")
        grads["w_out"][l] = _matmul(r["gated"], du_b, F32, "out_proj_dw", "tn")
        dmixed, dgate_b = _gate_bwd(dgated, r["groups"], pb)

        dfq, dfk, dfv, dfc_q, dfc_k = _softmax_bwd(pairs(pa, 0), pairs(pa, 2), vals(pa, 4), o_fox, r["lse_fox"],
                                                   dmixed, 0, s, s, head_scale, True, r["fbias"], "fox_bwd")
        dmisc_f, dbf = _forget_bwd(pb, r["bias_row"], dfc_q + jnp.pad(dfc_k.T, ((0, 0), (0, 128 - 8))))
        grads["b_forget"][l] = dbf[0, :N_HEADS]

        dsq, dsk, dsv = _sb_bwd(pa, 6, 8, 10, dmixed, 1, *r["rm_sb"], s, head_scale, "sb_bwd")

        dqm, dkm, dvm = _softmax_bwd(r["mla_q"], r["mla_k"], vals(r["kv"], 2), o_mla, r["lse_mla"], dmixed, 2,
                                     s, s, mla_scale, True, None, "mla_bwd")
        dq_mla_b = _rope_q_bwd(dqm, tabs, mats)
        dcqn = _matmul(dq_mla_b, r["wq_up"], F32, "q_up_dx", "nt")
        grads["w_mla_q_up"][l] = _heads_interleaved(_matmul(r["cqn"], dq_mla_b, F32, "q_up_dw", "tn"),
                                                    HEAD_DIM + MLA_ROPE, HEAD_DIM)
        dcq_b, dgq = _rms_bwd(pb, MLA_Q_RANK, PB_CQ_BLK, p["mla_q_norm_g"][l], dcqn, "rms_q_bwd")
        grads["mla_q_norm_g"][l] = dgq[0]
        dkv_b = jnp.concatenate([dkm[:, :GROUP_W], dvm], axis=1).astype(MXU_DTYPE)
        dckvn = _matmul(dkv_b, r["wkv_up"], F32, "kv_up_dx", "nt")
        grads["w_mla_kv_up"][l] = _heads_interleaved(_matmul(r["ckvn"], dkv_b, F32, "kv_up_dw", "tn"),
                                                     2 * HEAD_DIM, HEAD_DIM)
        dckv_b, dgkv = _rms_bwd(pb, MLA_KV_RANK, PB_CKV_BLK, p["mla_kv_norm_g"][l], dckvn, "rms_kv_bwd")
        grads["mla_kv_norm_g"][l] = dgkv[0]
        dmisc_k = _rope_k_bwd(dkm[:, GROUP_W:], tabs, mats)

        dmq, dmk, dmv = _softmax_bwd(pairs(pa, 12), pairs(r["mkv"], 0), vals(r["mkv"], 2), o_mem, r["lse_mem"],
                                     dmixed, 3, s, nm, head_scale, False, None, "mem_bwd")
        dmkv_b = jnp.concatenate([dmk, dmv], axis=1).astype(MXU_DTYPE)
        grads["w_mem_kv"][l] = _matmul(memn_b, dmkv_b, F32, "mem_kv_dw", "tn")
        dmemn.append(_matmul(dmkv_b, p["w_mem_kv"][l], F32, "mem_kv_dx", "nt"))

        dmisc = jnp.concatenate([dmisc_f[:, :MISC_KROT], dmisc_k[:, MISC_KROT:MISC_KROT + MLA_ROPE],
                                 jnp.zeros((s, 128 - MISC_KROT - MLA_ROPE), F32)], axis=1)
        dp = jnp.concatenate([d.astype(MXU_DTYPE) for d in (dfq, dfk, dfv, dsq, dsk, dsv, dmq)]
                             + [dgate_b, dcq_b, dckv_b, dmisc.astype(MXU_DTYPE)], axis=1)
        dhproj = _matmul(dp, r["wp"], F32, "proj_dx", "nt")
        grads["w_in"][l] = _merge_dw_in(_matmul(r["hb"], dp, F32, "proj_dw", "tn"))
        dy1, dy2, c1 = du, dhproj, ALPHA

    dx, _, dg_in, db_in = _ln_bwd(x2, None, p["ln_in_g"], dy1, dy2, c1, "ln_in_bwd")
    _, _, dg_mem, db_mem = _ln_bwd(mem2, None, p["mem_ln_g"], dmemn[0], dmemn[1], 1.0, "ln_mem_bwd")
    out = {k: jnp.stack(v) for k, v in grads.items()}
    out.update(ln_in_g=dg_in[0], ln_in_b=db_in[0], mem_ln_g=dg_mem[0], mem_ln_b=db_mem[0])
    return loss_sum, dx, out


WIDE = "w_in"
FLAT_NAMES = ("w_out", "w_mem_kv", "w_mla_q_up", "w_mla_kv_up")
FLAT_ROWS = 896
BIG_NAMES = (WIDE,) + FLAT_NAMES
BIG_AXIS = dict(w_in=2, w_out=1, w_mem_kv=1, w_mla_q_up=2, w_mla_kv_up=2)
SMALL_NAMES = ("ln_in_g", "ln_in_b", "mem_ln_g", "mem_ln_b", "ln_g", "ln_b", "b_forget", "mla_q_norm_g",
               "mla_kv_norm_g")
ALL_NAMES = ("ln_in_g", "ln_in_b", "mem_ln_g", "mem_ln_b", "w_in", "b_forget", "mla_q_norm_g", "w_mla_q_up",
             "mla_kv_norm_g", "w_mla_kv_up", "w_mem_kv", "w_out", "ln_g", "ln_b")
N_CHIPS = 4
N_DEV = 8


def _rows_of(shape):
    rows = -(-int(np.prod(shape)) // LANES)
    return -(-rows // PACK_ALIGN) * PACK_ALIGN


def _pack(arrs, rows):
    parts = []
    for a in arrs:
        f = a.reshape(-1)
        n = _rows_of(a.shape) * LANES
        parts.append(jnp.pad(f, (0, n - f.shape[0])).reshape(-1, LANES))
    used = sum(q.shape[0] for q in parts)
    if rows > used:
        parts.append(jnp.zeros((rows - used, LANES), parts[0].dtype))
    return jnp.concatenate(parts, axis=0)


def _unpack(buf, shapes):
    out, r = [], 0
    for shp in shapes:
        n = _rows_of(shp)
        out.append(buf[r:r + n].reshape(-1)[:int(np.prod(shp))].reshape(shp))
        r += n
    return out


def _sharded_pair(get):
    wide = get(WIDE)
    return [wide.reshape(-1, wide.shape[-1]), _pack([get(n) for n in FLAT_NAMES], FLAT_ROWS)]


HBM_SPEC = pl.BlockSpec(memory_space=pltpu.HBM)


def _gather_weights(shards):
    n = len(shards)

    def body(*refs):
        w_refs, out_refs, (send_sems, recv_sems, local_sems) = refs[:n], refs[n:2 * n], refs[2 * n:]
        x, y, c = (lax.axis_index(a) for a in MESH_AXES)
        me, sibling = 2 * x + y, (x, y, 1 - c)
        chips = [(1 - x, y), (x, 1 - y), (1 - x, 1 - y)]
        local, first, passed = [], [], []
        for a in range(n):
            w_ref, out_ref, half = w_refs[a], out_refs[a], shards[a].shape[0] // 2

            def part(chip, core, out_ref=out_ref, half=half):
                return out_ref.at[chip, pl.ds(core * half, half)]

            def copy(k, src, dst, to, a=a):
                return pltpu.make_async_remote_copy(
                    src_ref=src, dst_ref=dst, send_sem=send_sems.at[6 * a + k], recv_sem=recv_sems.at[6 * a + k],
                    device_id=to, device_id_type=pl.DeviceIdType.MESH)

            local.append(pltpu.make_async_copy(w_ref, out_ref.at[me], local_sems.at[a]))
            local[-1].start()
            mine = [copy(k, w_ref.at[pl.ds(c * half, half)], part(me, c), (px, py, c))
                    for k, (px, py) in enumerate(chips)]
            for cp in mine:
                cp.start()
            first.append((mine, part, copy))
        for mine, part, copy in first:
            for k, (px, py) in enumerate(chips):
                copy(k, part(me, c), part(2 * px + py, c), (px, py, c)).wait_recv()
                passed.append(copy(3 + k, part(2 * px + py, c), part(2 * px + py, c), sibling))
                passed[-1].start()
        for mine, part, copy in first:
            for k, (px, py) in enumerate(chips):
                copy(3 + k, part(me, c), part(2 * px + py, 1 - c), sibling).wait_recv()
        for cp in [cp for mine, _, _ in first for cp in mine] + passed:
            cp.wait_send()
        for cp in local:
            cp.wait()

    return pl.pallas_call(
        body, name="gather_weights",
        out_shape=[jax.ShapeDtypeStruct((N_CHIPS,) + s.shape, s.dtype) for s in shards],
        in_specs=[HBM_SPEC] * n, out_specs=[HBM_SPEC] * n,
        scratch_shapes=[pltpu.SemaphoreType.DMA((6 * n,)), pltpu.SemaphoreType.DMA((6 * n,)),
                        pltpu.SemaphoreType.DMA((n,))],
    )(*shards)


def _exchange_grads(bigs, small):
    nb = len(bigs)

    def body(*refs):
        big_refs, small_ref, out_refs = refs[:nb], refs[nb], refs[nb + 1:2 * nb + 2]
        send_sems, recv_sems, local_sems = refs[2 * nb + 2:]
        x, y, c = (lax.axis_index(a) for a in MESH_AXES)
        me, my_chip = 4 * x + 2 * y + c, 2 * x + y
        flips = [(fx, fy, fc) for fx in (0, 1) for fy in (0, 1) for fc in (0, 1) if fx or fy or fc]
        peers = [(1 - x if fx else x, 1 - y if fy else y, 1 - c if fc else c) for fx, fy, fc in flips]

        sources = [lambda chip, core, r=big_refs[a], half=bigs[a].shape[1] // 2: r.at[chip, pl.ds(core * half, half)]
                   for a in range(nb)] + [lambda chip, core: small_ref]

        def copy(a, k, src, slot, to):
            return pltpu.make_async_remote_copy(
                src_ref=src, dst_ref=out_refs[a].at[slot], send_sem=send_sems.at[7 * a + k],
                recv_sem=recv_sems.at[7 * a + k], device_id=to, device_id_type=pl.DeviceIdType.MESH)

        own = [pltpu.make_async_copy(src(my_chip, c), out_refs[a].at[me], local_sems.at[a])
               for a, src in enumerate(sources)]
        for cp in own:
            cp.start()
        sends = [copy(a, k, src(2 * px + py, pc), me, (px, py, pc))
                 for a, src in enumerate(sources) for k, (px, py, pc) in enumerate(peers)]
        for cp in sends:
            cp.start()
        for a, src in enumerate(sources):
            for k, (px, py, pc) in enumerate(peers):
                copy(a, k, src(my_chip, c), 4 * px + 2 * py + pc, (px, py, pc)).wait_recv()
        for cp in sends:
            cp.wait_send()
        for cp in own:
            cp.wait()

    return pl.pallas_call(
        body, name="exchange_grads",
        out_shape=[jax.ShapeDtypeStruct((N_DEV, b.shape[1] // 2, b.shape[2]), b.dtype) for b in bigs]
        + [jax.ShapeDtypeStruct((N_DEV,) + small.shape, small.dtype)],
        in_specs=[HBM_SPEC] * (nb + 1), out_specs=[HBM_SPEC] * (nb + 1),
        scratch_shapes=[pltpu.SemaphoreType.DMA((7 * (nb + 1),)), pltpu.SemaphoreType.DMA((7 * (nb + 1),)),
                        pltpu.SemaphoreType.DMA((nb + 1,))],
    )(*bigs, small)


def _sum_parts(parts, name):
    n, rows, width = parts.shape
    tile = _pick(rows, (128, 16, 8))

    def kern(p_ref, o_ref):
        g = p_ref[0].astype(F32)
        for d in range(1, n):
            g = g + p_ref[d].astype(F32)
        o_ref[...] = g

    return pl.pallas_call(
        kern, name=name, grid=(rows // tile,),
        in_specs=[pl.BlockSpec((n, tile, width), lambda i: (0, i, 0))],
        out_specs=pl.BlockSpec((tile, width), lambda i: (i, 0)), out_shape=jax.ShapeDtypeStruct((rows, width), F32),
        compiler_params=pltpu.CompilerParams(dimension_semantics=("parallel",), vmem_limit_bytes=VMEM_LIMIT),
    )(parts)


def _swap_halves(mine):
    n = len(mine)

    def body(*refs):
        q_refs, out_refs, (send_sems, recv_sems, local_sems) = refs[:n], refs[n:2 * n], refs[2 * n:]
        x, y, c = (lax.axis_index(a) for a in MESH_AXES)

        def copy(a, dst):
            return pltpu.make_async_remote_copy(
                src_ref=q_refs[a], dst_ref=dst, send_sem=send_sems.at[a], recv_sem=recv_sems.at[a],
                device_id=(x, y, 1 - c), device_id_type=pl.DeviceIdType.MESH)

        own = [pltpu.make_async_copy(q_refs[a], out_refs[a].at[c], local_sems.at[a]) for a in range(n)]
        swaps = [copy(a, out_refs[a].at[c]) for a in range(n)]
        for cp in own + swaps:
            cp.start()
        for a in range(n):
            copy(a, out_refs[a].at[1 - c]).wait_recv()
        for cp in swaps:
            cp.wait_send()
        for cp in own:
            cp.wait()

    return pl.pallas_call(
        body, name="swap_halves", out_shape=[jax.ShapeDtypeStruct((2,) + q.shape, q.dtype) for q in mine],
        in_specs=[HBM_SPEC] * n, out_specs=[HBM_SPEC] * n,
        scratch_shapes=[pltpu.SemaphoreType.DMA((n,)), pltpu.SemaphoreType.DMA((n,)), pltpu.SemaphoreType.DMA((n,))],
    )(*mine)


def _adamw(parts, w, m, v, name):
    rows, width = w.shape
    n_parts = parts.shape[0]
    tile = _pick(rows, (128, 16, 8))
    bc1 = 1.0 - ADAM_B1 ** ADAM_STEP
    bc2 = 1.0 - ADAM_B2 ** ADAM_STEP

    def kern(p_ref, w_ref, m_ref, v_ref, g_ref, d_ref, nm_ref, nv_ref):
        g = p_ref[0].astype(F32)
        for d in range(1, n_parts):
            g = g + p_ref[d].astype(F32)
        nm = ADAM_B1 * m_ref[...] + (1.0 - ADAM_B1) * g
        nv = ADAM_B2 * v_ref[...] + (1.0 - ADAM_B2) * (g * g)
        g_ref[...] = g
        nm_ref[...] = nm
        nv_ref[...] = nv
        d_ref[...] = -ADAM_LR * ((nm / bc1) / (jnp.sqrt(nv / bc2) + ADAM_EPS) + ADAM_WD * w_ref[...])

    spec = pl.BlockSpec((tile, width), lambda i: (i, 0))
    return pl.pallas_call(
        kern, name=name, grid=(rows // tile,),
        in_specs=[pl.BlockSpec((n_parts, tile, width), lambda i: (0, i, 0)), spec, spec, spec],
        out_specs=[spec] * 4, out_shape=[jax.ShapeDtypeStruct((rows, width), F32)] * 4,
        compiler_params=pltpu.CompilerParams(dimension_semantics=("parallel",), vmem_limit_bytes=VMEM_LIMIT),
    )(parts, w, m, v)


def kernel(x, mem, ln_in_g, ln_in_b, mem_ln_g, mem_ln_b, w_in, b_forget, mla_q_norm_g, w_mla_q_up, mla_kv_norm_g, w_mla_kv_up, w_mem_kv, w_out, ln_g, ln_b, loss_target, m_ln_in_g, m_ln_in_b, m_mem_ln_g, m_mem_ln_b, m_w_in, m_b_forget, m_mla_q_norm_g, m_w_mla_q_up, m_mla_kv_norm_g, m_w_mla_kv_up, m_w_mem_kv, m_w_out, m_ln_g, m_ln_b, v_ln_in_g, v_ln_in_b, v_mem_ln_g, v_mem_ln_b, v_w_in, v_b_forget, v_mla_q_norm_g, v_w_mla_q_up, v_mla_kv_norm_g, v_w_mla_kv_up, v_w_mem_kv, v_w_out, v_ln_g, v_ln_b):
    w = dict(ln_in_g=ln_in_g, ln_in_b=ln_in_b, mem_ln_g=mem_ln_g, mem_ln_b=mem_ln_b, w_in=w_in, b_forget=b_forget,
             mla_q_norm_g=mla_q_norm_g, w_mla_q_up=w_mla_q_up, mla_kv_norm_g=mla_kv_norm_g,
             w_mla_kv_up=w_mla_kv_up, w_mem_kv=w_mem_kv, w_out=w_out, ln_g=ln_g, ln_b=ln_b)
    mo = dict(ln_in_g=m_ln_in_g, ln_in_b=m_ln_in_b, mem_ln_g=m_mem_ln_g, mem_ln_b=m_mem_ln_b, w_in=m_w_in,
              b_forget=m_b_forget, mla_q_norm_g=m_mla_q_norm_g, w_mla_q_up=m_w_mla_q_up,
              mla_kv_norm_g=m_mla_kv_norm_g, w_mla_kv_up=m_w_mla_kv_up, w_mem_kv=m_w_mem_kv, w_out=m_w_out,
              ln_g=m_ln_g, ln_b=m_ln_b)
    vo = dict(ln_in_g=v_ln_in_g, ln_in_b=v_ln_in_b, mem_ln_g=v_mem_ln_g, mem_ln_b=v_mem_ln_b, w_in=v_w_in,
              b_forget=v_b_forget, mla_q_norm_g=v_mla_q_norm_g, w_mla_q_up=v_w_mla_q_up,
              mla_kv_norm_g=v_mla_kv_norm_g, w_mla_kv_up=v_w_mla_kv_up, w_mem_kv=v_w_mem_kv, w_out=v_w_out,
              ln_g=v_ln_g, ln_b=v_ln_b)
    flat_shapes = [w[n].shape for n in FLAT_NAMES]
    small_shapes = [w[n].shape for n in SMALL_NAMES]

    got_wide, got_flat = _gather_weights(_sharded_pair(lambda n: w[n].astype(MXU_DTYPE)))
    full = dict(w)
    full[WIDE] = jnp.concatenate([got_wide[j] for j in range(N_CHIPS)], axis=1).reshape(
        w[WIDE].shape[:2] + (N_CHIPS * w[WIDE].shape[2],))
    per_chip = [_unpack(got_flat[j], flat_shapes) for j in range(N_CHIPS)]
    for idx, n in enumerate(FLAT_NAMES):
        full[n] = jnp.concatenate([per_chip[j][idx] for j in range(N_CHIPS)], axis=BIG_AXIS[n])

    loss_sum, dx, g = _local_step(x[0], mem[0], loss_target[0], full)
    loss = lax.psum(loss_sum * (0.5 / D_MODEL), MESH_AXES)

    def shard_of(n, j):
        ax, size = BIG_AXIS[n], w[n].shape[BIG_AXIS[n]]
        return lax.slice_in_dim(g[n], j * size, (j + 1) * size, axis=ax).astype(MXU_DTYPE)

    per_dest = [_sharded_pair(lambda n, j=j: shard_of(n, j)) for j in range(N_CHIPS)]
    bigs = [jnp.stack([per_dest[j][a] for j in range(N_CHIPS)]) for a in range(2)]
    *big_parts, small_parts = _exchange_grads(bigs, _pack([g[n] for n in SMALL_NAMES], SMALL_ROWS))
    halves = _swap_halves([_sum_parts(big_parts[0], "sum_wide"), _sum_parts(big_parts[1], "sum_flat")])

    res = []
    for a, (grad, nm) in enumerate(zip(halves, ("adamw_wide", "adamw_flat"))):
        state = [_sharded_pair(lambda n, src=src: src[n])[a] for src in (w, mo, vo)]
        res.append(_adamw(grad.reshape((1,) + state[0].shape), *state, nm))
    res_small = _adamw(small_parts, *[_pack([src[n] for n in SMALL_NAMES], SMALL_ROWS) for src in (w, mo, vo)],
                       "adamw_replicated")
    outs = []
    for kind in range(4):
        vals = {WIDE: res[0][kind].reshape(w[WIDE].shape)}
        vals.update(zip(FLAT_NAMES, _unpack(res[1][kind], flat_shapes)))
        vals.update(zip(SMALL_NAMES, _unpack(res_small[kind], small_shapes)))
        outs += [vals[n] for n in ALL_NAMES]
    return (loss, dx[None], *outs)
```

```python
import functools

import numpy as np
import jax
import jax.numpy as jnp
from jax import lax
from jax.experimental import pallas as pl
from jax.experimental.pallas import tpu as pltpu

F32 = jnp.float32
MXU_DTYPE = jnp.bfloat16

DEPTH = 2
D_MODEL = 1024
GROUP_W = 256
N_HEADS = 4
HEAD_DIM = 64
MLA_Q_RANK = 256
MLA_KV_RANK = 128
MLA_ROPE = 32
MLA_Q_COLS = N_HEADS * (HEAD_DIM + MLA_ROPE)
MLA_KV_COLS = N_HEADS * 2 * HEAD_DIM
ROPE_THETA = 10000.0
LN_EPS = 1e-5
RMS_EPS = 1e-6
ALPHA = (2 * DEPTH) ** 0.25
ADAM_LR, ADAM_B1, ADAM_B2, ADAM_EPS, ADAM_WD, ADAM_STEP = 0.001, 0.9, 0.999, 1e-08, 0.01, 10

_SPLIT = (256, 256, 256, 4, 256, 256, 256, 256, 128, 32, 256, 1024)
_OFF = [int(o) for o in np.cumsum((0,) + _SPLIT)]
IN_COLS = _OFF[-1]
PA_COLS = 7 * GROUP_W
PB_COLS = 1024 + 256 + 128 + 128
PB_CQ_BLK, PB_CKV_BLK, PB_MISC_BLK = 4, 10, 11
MISC_KROT = 4

LANES = 1024
PACK_ALIGN = 16
BIG_ROWS = 2560
SMALL_ROWS = 144
ROW_TILE = 256
ATT_TILE = 256
SB_QUERY_TILE = 256
SOFTMAX_TILE = 512
PAIR = 128
SB_SLOT = PAIR // N_HEADS
VMEM_LIMIT = 56 * 1024 * 1024
MATMUL_VMEM = 30 * 1024 * 1024
NEG = -1e30
EXP_UNDERFLOW = -104.0
DEAD_LOGIT = -110.0
REACH_SLACK = 1.0 + 2.0 ** -10
MESH_AXES = ("x", "y", "c")


def _dot(a, b):
    return jnp.dot(a, b, preferred_element_type=F32)


def _dot_nt(a, b):
    return lax.dot_general(a, b, (((1,), (1,)), ((), ())), preferred_element_type=F32)


def _dot_tn(a, b):
    return lax.dot_general(a, b, (((0,), (0,)), ((), ())), preferred_element_type=F32)


def _split2(x):
    hi = x.astype(MXU_DTYPE)
    lo = (x - hi.astype(F32)).astype(MXU_DTYPE)
    return hi, lo


def _split3(x):
    hi = x.astype(MXU_DTYPE)
    r = x - hi.astype(F32)
    mid = r.astype(MXU_DTYPE)
    lo = (r - mid.astype(F32)).astype(MXU_DTYPE)
    return hi, mid, lo


def _dot_exact_r(x, pm):
    hi, mid, lo = _split3(x)
    return _dot(hi, pm) + _dot(mid, pm) + _dot(lo, pm)


def _dot_exact_l(pm, x):
    hi, mid, lo = _split3(x)
    return _dot(pm, hi) + _dot(pm, mid) + _dot(pm, lo)


def _pick(dim, prefs):
    for p in prefs:
        if dim % p == 0:
            return p
    return dim


def _softplus(z):
    return jnp.maximum(z, 0.0) + jnp.log(1.0 + jnp.exp(-jnp.abs(z)))


def _tile_options(dim):
    opts = [d for d in range(128, min(dim, 2048) + 1, 128) if dim % d == 0]
    return opts or [dim]


def _matmul_tiles(m, n, k, out_bytes):
    tk = k if k <= 4096 else _pick(k, (1024, 512, 256, 128))
    best = None
    for tm in _tile_options(m):
        for tn in _tile_options(n):
            vmem = 2 * 2 * (tm * tk + tk * tn) + 4 * tm * tn + 2 * out_bytes * tm * tn
            if vmem <= MATMUL_VMEM and (best is None or tm * tn / (tm + tn) > best[0]):
                best = (tm * tn / (tm + tn), tm, tn)
    return best[1], best[2], tk


def _matmul(a, b, out_dtype, name, mode="nn"):
    m, k = (a.shape[1], a.shape[0]) if mode == "tn" else a.shape
    n = b.shape[0] if mode == "nt" else b.shape[1]
    tm, tn, tk = _matmul_tiles(m, n, k, jnp.dtype(out_dtype).itemsize)
    nk = k // tk
    dot = {"nn": _dot, "tn": _dot_tn, "nt": _dot_nt}[mode]

    def kern(a_ref, b_ref, o_ref, *acc):
        if nk == 1:
            o_ref[...] = dot(a_ref[...], b_ref[...]).astype(o_ref.dtype)
            return
        acc_ref, = acc
        kk = pl.program_id(2)

        @pl.when(kk == 0)
        def _():
            acc_ref[...] = jnp.zeros_like(acc_ref)

        acc_ref[...] += dot(a_ref[...], b_ref[...])

        @pl.when(kk == nk - 1)
        def _():
            o_ref[...] = acc_ref[...].astype(o_ref.dtype)

    a_spec = (pl.BlockSpec((tk, tm), lambda i, j, kk: (kk, i)) if mode == "tn"
              else pl.BlockSpec((tm, tk), lambda i, j, kk: (i, kk)))
    b_spec = (pl.BlockSpec((tn, tk), lambda i, j, kk: (j, kk)) if mode == "nt"
              else pl.BlockSpec((tk, tn), lambda i, j, kk: (kk, j)))
    return pl.pallas_call(
        kern, name=name, grid=(m // tm, n // tn, nk), in_specs=[a_spec, b_spec],
        out_specs=pl.BlockSpec((tm, tn), lambda i, j, kk: (i, j)),
        out_shape=jax.ShapeDtypeStruct((m, n), out_dtype),
        scratch_shapes=[pltpu.VMEM((tm, tn), F32)] if nk > 1 else [],
        compiler_params=pltpu.CompilerParams(
            dimension_semantics=("parallel", "parallel", "arbitrary"), vmem_limit_bytes=VMEM_LIMIT),
    )(a.astype(MXU_DTYPE), b.astype(MXU_DTYPE))


def _rowwise(body, name, rows, tile, row_ins, full_ins, row_outs, acc_outs=(), scratch=(),
             reverse=False, sequential=False):
    n = rows // tile

    def ridx(i):
        return (n - 1 - i) if reverse else i

    in_specs, args = [], []
    for arr, width, cb in row_ins:
        in_specs.append(pl.BlockSpec((tile, width), lambda i, cb=cb: (ridx(i), cb)))
        args.append(arr)
    for arr in full_ins:
        in_specs.append(pl.BlockSpec(arr.shape, lambda i, nd=arr.ndim: (0,) * nd))
        args.append(arr)
    out_shape = [jax.ShapeDtypeStruct((rows, w), dt) for w, dt in row_outs]
    out_shape += [jax.ShapeDtypeStruct(s, dt) for s, dt in acc_outs]
    out_specs = [pl.BlockSpec((tile, w), lambda i: (ridx(i), 0)) for w, dt in row_outs]
    out_specs += [pl.BlockSpec(s, lambda i, nd=len(s): (0,) * nd) for s, dt in acc_outs]

    def kern(*refs):
        body(pl.program_id(0), *refs)

    sem = "arbitrary" if (acc_outs or sequential) else "parallel"
    return pl.pallas_call(
        kern, name=name, grid=(n,), in_specs=in_specs, out_specs=out_specs, out_shape=out_shape,
        scratch_shapes=list(scratch),
        compiler_params=pltpu.CompilerParams(dimension_semantics=(sem,), vmem_limit_bytes=VMEM_LIMIT),
    )(*args)


def _ln_stats(u):
    mu = jnp.mean(u, axis=-1, keepdims=True)
    xc = u - mu
    var = jnp.mean(xc * xc, axis=-1, keepdims=True)
    return xc, lax.rsqrt(var + LN_EPS)


def _ln_fwd(a, b, g, beta, name):
    rows, d = a.shape
    has_b = b is not None

    def body(i, *refs):
        if has_b:
            a_ref, b_ref, g_ref, be_ref, h_ref, hb_ref = refs
            u = ALPHA * a_ref[...] + b_ref[...]
        else:
            a_ref, g_ref, be_ref, h_ref, hb_ref = refs
            u = a_ref[...]
        xc, rstd = _ln_stats(u)
        y = xc * rstd * g_ref[...] + be_ref[...]
        h_ref[...] = y
        hb_ref[...] = y.astype(hb_ref.dtype)

    row_ins = [(a, d, 0)] + ([(b, d, 0)] if has_b else [])
    return _rowwise(body, name, rows, min(ROW_TILE, rows), row_ins,
                    [g.reshape(1, d), beta.reshape(1, d)], [(d, F32), (d, MXU_DTYPE)])


def _ln_bwd(a, b, g, dy1, dy2, c1, name):
    rows, d = a.shape
    has_b = b is not None
    has_2 = dy2 is not None

    def body(i, *refs):
        refs = list(refs)
        a_ref = refs.pop(0)
        u = a_ref[...]
        if has_b:
            u = ALPHA * u + refs.pop(0)[...]
        dy = c1 * refs.pop(0)[...]
        if has_2:
            dy = dy + refs.pop(0)[...]
        g_ref, du_ref, dub_ref, dg_ref, db_ref = refs

        @pl.when(i == 0)
        def _():
            dg_ref[...] = jnp.zeros_like(dg_ref)
            db_ref[...] = jnp.zeros_like(db_ref)

        xc, rstd = _ln_stats(u)
        xhat = xc * rstd
        dxh = dy * g_ref[...]
        m1 = jnp.mean(dxh, axis=-1, keepdims=True)
        m2 = jnp.mean(dxh * xhat, axis=-1, keepdims=True)
        du = rstd * (dxh - m1 - xhat * m2)
        du_ref[...] = du
        dub_ref[...] = du.astype(dub_ref.dtype)
        dg_ref[...] += jnp.sum(dy * xhat, axis=0, keepdims=True)
        db_ref[...] += jnp.sum(dy, axis=0, keepdims=True)

    row_ins = [(a, d, 0)] + ([(b, d, 0)] if has_b else []) + [(dy1, d, 0)] + ([(dy2, d, 0)] if has_2 else [])
    return _rowwise(body, name, rows, min(ROW_TILE, rows), row_ins, [g.reshape(1, d)],
                    [(d, F32), (d, MXU_DTYPE)], [((1, d), F32), ((1, d), F32)])


def _loss_grad(h, target):
    rows, d = h.shape

    def body(i, h_ref, t_ref, dh_ref, acc_ref):
        @pl.when(i == 0)
        def _():
            acc_ref[...] = jnp.zeros_like(acc_ref)

        e = h_ref[...] - t_ref[...]
        dh_ref[...] = e * (1.0 / d)
        acc_ref[...] += jnp.sum(e * e, axis=0, keepdims=True)

    return _rowwise(body, "loss_grad", rows, ROW_TILE, [(h, d, 0), (target, d, 0)], [],
                    [(d, F32)], [((1, d), F32)])


def _gate_fwd(groups, pb):
    rows = pb.shape[0]
    w = GROUP_W * len(groups)

    def body(i, *refs):
        g = refs[4][...]
        mixed = jnp.concatenate([r[...] for r in refs[:4]], axis=1)
        refs[5][...] = (mixed * (g / (1.0 + jnp.exp(-g)))).astype(refs[5].dtype)

    return _rowwise(body, "gate_fwd", rows, ROW_TILE, [(o, GROUP_W, 0) for o in groups] + [(pb, w, 0)], [],
                    [(w, MXU_DTYPE)])[0]


def _gate_bwd(dgated, groups, pb):
    rows = pb.shape[0]
    w = GROUP_W * len(groups)

    def body(i, *refs):
        dg = refs[0][...]
        mixed = jnp.concatenate([r[...] for r in refs[1:5]], axis=1)
        g = refs[5][...]
        dm_ref, dgate_ref = refs[6], refs[7]
        sig = 1.0 / (1.0 + jnp.exp(-g))
        dm_ref[...] = dg * (g * sig)
        dgate_ref[...] = (dg * mixed * (sig * (1.0 + g * (1.0 - sig)))).astype(dgate_ref.dtype)

    return _rowwise(body, "gate_bwd", rows, ROW_TILE,
                    [(dgated, w, 0)] + [(o, GROUP_W, 0) for o in groups] + [(pb, w, 0)], [],
                    [(w, F32), (w, MXU_DTYPE)])


def _tri(n, kind):
    r = np.arange(n)[:, None]
    c = np.arange(n)[None, :]
    m = {"lower_incl": r >= c, "upper_incl": r <= c, "row_gt_col": r > c, "row_lt_col": r < c}[kind]
    return jnp.asarray(m.astype(np.float32), dtype=MXU_DTYPE)


def _forget_fwd(pb, bias_row):
    rows = pb.shape[0]
    tile = min(ROW_TILE, rows)

    def body(i, x_ref, b_ref, l_ref, o_ref, carry_ref):
        @pl.when(i == 0)
        def _():
            carry_ref[...] = jnp.zeros_like(carry_ref)

        xx = x_ref[...] + b_ref[...]
        lane = lax.broadcasted_iota(jnp.int32, xx.shape, 1)
        lf = jnp.where(lane < N_HEADS, -_softplus(-xx), 0.0)
        o_ref[...] = _dot_exact_l(l_ref[...], lf) + carry_ref[...]
        carry_ref[...] += jnp.sum(lf, axis=0, keepdims=True)

    return _rowwise(body, "forget_fwd", rows, tile, [(pb, 128, PB_MISC_BLK)],
                    [bias_row, _tri(tile, "lower_incl")], [(128, F32)],
                    scratch=[pltpu.VMEM((1, 128), F32)], sequential=True)[0]


def _forget_bwd(pb, bias_row, dfc):
    rows = pb.shape[0]
    tile = min(ROW_TILE, rows)

    def body(i, x_ref, df_ref, b_ref, u_ref, o_ref, db_ref, carry_ref):
        @pl.when(i == 0)
        def _():
            carry_ref[...] = jnp.zeros_like(carry_ref)
            db_ref[...] = jnp.zeros_like(db_ref)

        df = df_ref[...]
        sfx = _dot_exact_l(u_ref[...], df) + carry_ref[...]
        carry_ref[...] += jnp.sum(df, axis=0, keepdims=True)
        xx = x_ref[...] + b_ref[...]
        lane = lax.broadcasted_iota(jnp.int32, xx.shape, 1)
        dl = jnp.where(lane < N_HEADS, sfx / (1.0 + jnp.exp(xx)), 0.0)
        o_ref[...] = dl
        db_ref[...] += jnp.sum(dl, axis=0, keepdims=True)

    return _rowwise(body, "forget_bwd", rows, tile,
                    [(pb, 128, PB_MISC_BLK), (dfc, 128, 0)],
                    [bias_row, _tri(tile, "upper_incl")], [(128, F32)], [((1, 128), F32)],
                    scratch=[pltpu.VMEM((1, 128), F32)], reverse=True, sequential=True)


def _rope_tables(s):
    half = MLA_ROPE // 2
    inv_freq = ROPE_THETA ** (-jnp.arange(half, dtype=F32) / half)
    ang = jnp.arange(s).astype(F32)[:, None] * inv_freq[None, :]
    cos2 = jnp.tile(jnp.cos(ang), (1, 2))
    sin2 = jnp.tile(jnp.sin(ang), (1, 2))
    cx = jnp.tile(cos2, (1, N_HEADS))
    sx = jnp.tile(sin2, (1, N_HEADS))
    cq = jnp.concatenate([jnp.ones((s, GROUP_W), F32), cx], axis=1)
    sq = jnp.concatenate([jnp.zeros((s, GROUP_W), F32), sx], axis=1)
    pad = ((0, 0), (MISC_KROT, 128 - MISC_KROT - MLA_ROPE))
    ck = jnp.pad(cos2, pad)
    sk = jnp.pad(sin2, pad)
    return dict(cq=cq, sq=sq, ck=ck, sk=sk, cx=cx, sx=sx)


def _rot_matrix(width, bases):
    half = MLA_ROPE // 2
    p = np.zeros((width, width), np.float32)
    for b in bases:
        for i in range(half):
            p[b + half + i, b + i] = -1.0
            p[b + i, b + half + i] = 1.0
    return p


def _rope_matrices():
    pq = _rot_matrix(MLA_Q_COLS, [GROUP_W + h * MLA_ROPE for h in range(N_HEADS)])
    pk = _rot_matrix(128, [MISC_KROT])
    p4 = _rot_matrix(128, [h * MLA_ROPE for h in range(N_HEADS)])
    a = np.zeros((128, 128), np.float32)
    for h in range(N_HEADS):
        for r in range(MLA_ROPE):
            a[h * MLA_ROPE + r, MISC_KROT + r] = 1.0
    cast = lambda m: jnp.asarray(m, dtype=MXU_DTYPE)
    return dict(pq=cast(pq), pqt=cast(pq.T), pk=cast(pk), spread=cast(a.T), xa=cast(a), xb=cast(p4.T @ a))


def _rms(c, g):
    r = lax.rsqrt(jnp.mean(c * c, axis=-1, keepdims=True) + RMS_EPS)
    return c * r * g


def _mla_q_fwd(pb, g, w_up, tabs, mats):
    rows = pb.shape[0]

    def body(i, c_ref, cos_ref, sin_ref, g_ref, w_ref, p_ref, q_ref, cn_ref):
        cn = _rms(c_ref[...], g_ref[...]).astype(cn_ref.dtype)
        cn_ref[...] = cn
        q = _dot(cn, w_ref[...])
        q_ref[...] = (q * cos_ref[...] + _dot_exact_r(q, p_ref[...]) * sin_ref[...]).astype(q_ref.dtype)

    return _rowwise(body, "mla_q_fwd", rows, ROW_TILE,
                    [(pb, MLA_Q_RANK, PB_CQ_BLK), (tabs["cq"], MLA_Q_COLS, 0), (tabs["sq"], MLA_Q_COLS, 0)],
                    [g.reshape(1, MLA_Q_RANK), w_up.astype(MXU_DTYPE), mats["pq"]],
                    [(MLA_Q_COLS, MXU_DTYPE), (MLA_Q_RANK, MXU_DTYPE)])


def _mla_kv_fwd(pb, g, w_up, tabs, mats):
    rows = pb.shape[0]

    def body(i, c_ref, x_ref, cos_ref, sin_ref, g_ref, w_ref, p_ref, sp_ref, kv_ref, cn_ref, kr_ref):
        cn = _rms(c_ref[...], g_ref[...]).astype(cn_ref.dtype)
        cn_ref[...] = cn
        kv_ref[...] = _dot(cn, w_ref[...]).astype(kv_ref.dtype)
        xx = x_ref[...]
        kr = xx * cos_ref[...] + _dot_exact_r(xx, p_ref[...]) * sin_ref[...]
        kr_ref[...] = _dot_exact_r(kr, sp_ref[...]).astype(kr_ref.dtype)

    return _rowwise(body, "mla_kv_fwd", rows, ROW_TILE,
                    [(pb, MLA_KV_RANK, PB_CKV_BLK), (pb, 128, PB_MISC_BLK), (tabs["ck"], 128, 0), (tabs["sk"], 128, 0)],
                    [g.reshape(1, MLA_KV_RANK), w_up.astype(MXU_DTYPE), mats["pk"], mats["spread"]],
                    [(MLA_KV_COLS, MXU_DTYPE), (MLA_KV_RANK, MXU_DTYPE), (128, MXU_DTYPE)])


def _rope_q_bwd(dq_full, tabs, mats):
    rows, nq = dq_full.shape

    def body(i, d_ref, cos_ref, sin_ref, pt_ref, o_ref):
        d = d_ref[...]
        o_ref[...] = (d * cos_ref[...] + _dot_exact_r(d * sin_ref[...], pt_ref[...])).astype(o_ref.dtype)

    return _rowwise(body, "rope_q_bwd", rows, ROW_TILE,
                    [(dq_full, nq, 0), (tabs["cq"], nq, 0), (tabs["sq"], nq, 0)], [mats["pqt"]],
                    [(nq, MXU_DTYPE)])[0]


def _rope_k_bwd(dkr, tabs, mats):
    rows = dkr.shape[0]

    def body(i, d_ref, cos_ref, sin_ref, a_ref, b_ref, o_ref):
        d = d_ref[...]
        o_ref[...] = _dot_exact_r(d * cos_ref[...], a_ref[...]) + _dot_exact_r(d * sin_ref[...], b_ref[...])

    return _rowwise(body, "rope_k_bwd", rows, ROW_TILE,
                    [(dkr, 128, 0), (tabs["cx"], 128, 0), (tabs["sx"], 128, 0)], [mats["xa"], mats["xb"]],
                    [(128, F32)])[0]


def _rms_bwd(pb, width, col_blk, g, dy, name):
    rows = pb.shape[0]

    def body(i, c_ref, dy_ref, g_ref, dc_ref, dg_ref):
        @pl.when(i == 0)
        def _():
            dg_ref[...] = jnp.zeros_like(dg_ref)

        c = c_ref[...]
        dy = dy_ref[...]
        r = lax.rsqrt(jnp.mean(c * c, axis=-1, keepdims=True) + RMS_EPS)
        dyg = dy * g_ref[...]
        dc = r * dyg - c * (r * r * r) * jnp.mean(c * dyg, axis=-1, keepdims=True)
        dc_ref[...] = dc.astype(dc_ref.dtype)
        dg_ref[...] += jnp.sum(dy * c * r, axis=0, keepdims=True)

    return _rowwise(body, name, rows, ROW_TILE, [(pb, width, col_blk), (dy, width, 0)], [g.reshape(1, width)],
                    [(width, MXU_DTYPE)], [((1, width), F32)])


def _att_params(parallel):
    return pltpu.CompilerParams(dimension_semantics=("parallel" if parallel else "arbitrary",),
                                vmem_limit_bytes=VMEM_LIMIT)


def _blk_off(j, t):
    return j * t if isinstance(j, int) else pl.multiple_of(j * t, t)


def _causal_mask(t, strict):
    r = lax.broadcasted_iota(jnp.int32, (t, t), 0)
    c = lax.broadcasted_iota(jnp.int32, (t, t), 1)
    return (c < r) if strict else (c <= r)


def _lane_mask(kind, head, rows):
    lane = lax.broadcasted_iota(jnp.int32, (rows, PAIR), 1)
    if kind == "pair":
        return (lane < HEAD_DIM) if head % 2 == 0 else (lane >= HEAD_DIM)
    return (lane >= MLA_ROPE * head) & (lane < MLA_ROPE * (head + 1))


def _row_spec(t, cb, width=PAIR):
    return pl.BlockSpec((t, width), lambda i, cb=cb: (i, cb))


def _whole_spec(rows, cb, width=PAIR):
    return pl.BlockSpec((rows, width), lambda i, cb=cb: (0, cb), pipeline_mode=pl.Buffered(1))


def _is_pow2(x):
    return float(np.frexp(x)[0]) == 0.5


def _masked_heads(blocks, kinds, pair, dtype, scale=None):
    out = []
    for e in range(2):
        head = 2 * pair + e
        parts = [jnp.where(_lane_mask(k, head, b.shape[0]), b.astype(F32) * (1.0 if scale is None else scale),
                           0.0).astype(dtype)
                 for b, k in zip(blocks, kinds)]
        out.append(parts[0] if len(parts) == 1 else jnp.concatenate(parts, axis=1))
    return out


def _logit_reach(qh, kmax2, head):
    q32 = qh.astype(F32)
    return jnp.sqrt(jnp.sum(q32 * q32, axis=1, keepdims=True) * _col(kmax2, head)) * REACH_SLACK


def _forget_top(ft_ref, head, off):
    return jnp.max(-ft_ref[head:head + 1, pl.ds(off, PAIR)])


def _col(block, idx):
    lane = lax.broadcasted_iota(jnp.int32, block.shape, 1)
    return jnp.sum(jnp.where(lane == idx, block, 0.0), axis=1, keepdims=True)


def _scatter_cols(cols, t):
    lane = lax.broadcasted_iota(jnp.int32, (t, PAIR), 1)
    out = jnp.zeros((t, PAIR), F32)
    for idx, c in cols.items():
        out = out + jnp.where(lane == idx, c, 0.0)
    return out


def _take_heads(per_head, pair):
    return jnp.where(_lane_mask("pair", 0, per_head[0].shape[0]), per_head[0], per_head[1])


class _Parts:
    def __init__(self, q_parts, k_parts, v_parts, tq, sk):
        self.kinds = [[kind for _, _, kind in q_parts[p]] for p in range(2)]
        self.nparts = len(q_parts[0])
        self.q_specs = [_row_spec(tq, cb) for p in range(2) for _, cb, _ in q_parts[p]]
        self.q_args = [a for p in range(2) for a, _, _ in q_parts[p]]
        self.k_specs = [_whole_spec(sk, cb) for p in range(2) for _, cb, _ in k_parts[p]]
        self.k_args = [a for p in range(2) for a, _, _ in k_parts[p]]
        self.v_specs = [_whole_spec(sk, cb) for _, cb in v_parts]
        self.v_args = [a for a, _ in v_parts]
        self.width = PAIR * self.nparts

    def split(self, refs):
        n = self.nparts
        refs = list(refs)
        q = [refs[p * n:(p + 1) * n] for p in range(2)]
        k = [refs[2 * n + p * n:2 * n + (p + 1) * n] for p in range(2)]
        v = refs[4 * n:4 * n + 2]
        return q, k, v, refs[4 * n + 2:]

    def k_block(self, k_refs, off, t):
        blks = [r[pl.ds(off, t), :] for r in k_refs]
        return blks[0] if len(blks) == 1 else jnp.concatenate(blks, axis=1)


def _key_norm_max(src, k_blk):
    rows = src.shape[0]

    def body(i, k0_ref, k1_ref, o_ref):
        @pl.when(i == 0)
        def _():
            o_ref[...] = jnp.zeros_like(o_ref)

        cols = {}
        for p, ref in enumerate((k0_ref, k1_ref)):
            k32 = ref[...].astype(F32)
            for e in range(2):
                sq = jnp.sum(jnp.where(_lane_mask("pair", e, k32.shape[0]), k32 * k32, 0.0), axis=1, keepdims=True)
                cols[2 * p + e] = jnp.max(sq, axis=0, keepdims=True)
        lane = lax.broadcasted_iota(jnp.int32, (1, PAIR), 1)
        o_ref[...] = jnp.maximum(o_ref[...], sum(jnp.where(lane == h, c, 0.0) for h, c in cols.items()))

    return _rowwise(body, "key_norm_max", rows, ROW_TILE, [(src, PAIR, k_blk), (src, PAIR, k_blk + 1)], [], [],
                    [((1, PAIR), F32)])[0]


def _softmax_fwd(q_parts, k_parts, v_parts, sq, sk, scale, causal, bias, name):
    tq = min(SOFTMAX_TILE, sq)
    tk = tq if causal else min(SOFTMAX_TILE, sk)
    nkv = sk // tk
    pp = _Parts(q_parts, k_parts, v_parts, tq, sk)

    def kern(*refs):
        q_refs, k_refs, v_refs, rest = pp.split(refs)
        if bias is not None:
            fc_ref, ft_ref, kmax_ref, o_ref, lse_ref = rest
            fcb = fc_ref[...]
        else:
            o_ref, lse_ref = rest
        i = pl.program_id(0)
        fold = _is_pow2(scale)
        head_on = [jnp.where(_lane_mask("pair", e, tk), 1.0, 0.0).astype(MXU_DTYPE) for e in range(2)]
        head_off = [jnp.where(_lane_mask("pair", e, tk), 0.0, 1.0).astype(MXU_DTYPE) for e in range(2)]
        lse_cols = {}
        for p in range(2):
            qm = _masked_heads([r[...] for r in q_refs[p]], pp.kinds[p], p, MXU_DTYPE, scale if fold else None)
            if bias is not None:
                reach = [_logit_reach(qm[e], kmax_ref[...], 2 * p + e) for e in range(2)]

            def block(j, carry, masked, p=p, qm=qm):
                off = _blk_off(j, tk)
                kb = pp.k_block(k_refs[p], off, tk)
                vb = v_refs[p][pl.ds(off, tk), :]
                out = []
                for e in range(2):
                    h = 2 * p + e
                    m, acc = carry[e]
                    s = _dot_nt(qm[e], kb)
                    if not fold:
                        s = s * scale
                    if bias is not None:
                        s = s - ft_ref[h:h + 1, pl.ds(off, tk)]
                    if masked:
                        s = jnp.where(_causal_mask(tq, False), s, NEG)
                    m_new = jnp.maximum(m, jnp.max(s, axis=1, keepdims=True))
                    pr = jnp.exp(s - m_new).astype(MXU_DTYPE)
                    out.append((m_new, jnp.exp(m - m_new) * acc + _dot(pr, vb * head_on[e] + head_off[e])))
                return tuple(out)

            carry = tuple((jnp.full((tq, 1), NEG, F32), jnp.zeros((tq, PAIR), F32)) for _ in range(2))
            if causal and bias is not None:
                def alive(c, j, p=p, reach=reach):
                    top = [_forget_top(ft_ref, 2 * p + e, _blk_off(j, tk)) for e in range(2)]
                    return functools.reduce(jnp.maximum, [jnp.max(reach[e] + top[e] - c[e][0]) for e in range(2)])

                def step(state, block=block, alive=alive):
                    n, _, c = state
                    c = block(i - 1 - n, c, False)
                    return n + 1, alive(c, i - 1 - n), c

                carry = block(i, carry, True)
                _, _, carry = lax.while_loop(lambda st: jnp.logical_and(st[0] < i, st[1] > DEAD_LOGIT), step,
                                             (jnp.int32(0), alive(carry, i), carry))
            elif causal:
                carry = lax.fori_loop(0, i, lambda j, c, block=block: block(j, c, False), carry)
                carry = block(i, carry, True)
            else:
                for j in range(nkv):
                    carry = block(j, carry, False)
            outs = []
            for e in range(2):
                m, acc = carry[e]
                l = _col(acc, HEAD_DIM * (1 - e))
                outs.append(acc / l)
                lse_cols[2 * p + e] = m + jnp.log(l) + (_col(fcb, 2 * p + e) if bias is not None else 0.0)
            o_ref[:, p * PAIR:(p + 1) * PAIR] = _take_heads(outs, p)
        lse_ref[...] = _scatter_cols(lse_cols, tq)

    in_specs = pp.q_specs + pp.k_specs + pp.v_specs
    args = pp.q_args + pp.k_args + pp.v_args
    if bias is not None:
        in_specs += [_row_spec(tq, 0), pl.BlockSpec((8, sk), lambda i: (0, 0), pipeline_mode=pl.Buffered(1)),
                     pl.BlockSpec((1, PAIR), lambda i: (0, 0))]
        args += list(bias)
    return pl.pallas_call(
        kern, name=name, grid=(sq // tq,), in_specs=in_specs,
        out_specs=[_row_spec(tq, 0, GROUP_W), _row_spec(tq, 0)],
        out_shape=[jax.ShapeDtypeStruct((sq, GROUP_W), F32), jax.ShapeDtypeStruct((sq, PAIR), F32)],
        compiler_params=_att_params(True),
    )(*args)


def _softmax_bwd(q_parts, k_parts, v_parts, o, lse, do, do_blk, sq, sk, scale, causal, bias, name):
    tq = min(SOFTMAX_TILE, sq)
    tk = tq if causal else min(SOFTMAX_TILE, sk)
    nkv = sk // tk
    pp = _Parts(q_parts, k_parts, v_parts, tq, sk)
    quad = pp.nparts == 2
    wq = GROUP_W + (PAIR if quad else 0)

    def kern(*refs):
        q_refs, k_refs, v_refs, rest = pp.split(refs)
        if bias is not None:
            o_ref, lse_ref, do_ref, fc_ref, ft_ref, kmax_ref, dq_ref, dk_ref, dv_ref, dfq_ref, dfk_ref = rest
            fcb = fc_ref[...]
        else:
            o_ref, lse_ref, do_ref, dq_ref, dk_ref, dv_ref = rest
        i = pl.program_id(0)
        fold = _is_pow2(scale)

        @pl.when(i == 0)
        def _():
            dk_ref[...] = jnp.zeros_like(dk_ref)
            dv_ref[...] = jnp.zeros_like(dv_ref)
            if bias is not None:
                dfk_ref[...] = jnp.zeros_like(dfk_ref)

        lse_b = lse_ref[...]
        qm, dom, delta, lse_h = [], [], [], []
        for p in range(2):
            qm += _masked_heads([r[...] for r in q_refs[p]], pp.kinds[p], p, MXU_DTYPE, scale if fold else None)
            do_p = do_ref[:, p * PAIR:(p + 1) * PAIR]
            dom += _masked_heads([do_p], ["pair"], p, MXU_DTYPE)
            prod = do_p * o_ref[:, p * PAIR:(p + 1) * PAIR]
            for e in range(2):
                h = 2 * p + e
                delta.append(jnp.sum(jnp.where(_lane_mask("pair", h, tq), prod, 0.0), axis=1, keepdims=True))
                lse_h.append(_col(lse_b, h) - (_col(fcb, h) if bias is not None else 0.0))

        def block(j, carry, masked):
            off = _blk_off(j, tk)
            out = []
            for p in range(2):
                kb = pp.k_block(k_refs[p], off, tk)
                vb = v_refs[p][pl.ds(off, tk), :]
                dk_acc = jnp.zeros((tk, pp.width), F32)
                dv_acc = jnp.zeros((tk, PAIR), F32)
                for e in range(2):
                    h = 2 * p + e
                    dq, dfq = carry[h]
                    s = _dot_nt(qm[h], kb)
                    if not fold:
                        s = s * scale
                    if bias is not None:
                        s = s - ft_ref[h:h + 1, pl.ds(off, tk)]
                    if masked:
                        s = jnp.where(_causal_mask(tq, False), s, NEG)
                    pr = jnp.exp(s - lse_h[h])
                    ds = pr * (_dot_nt(dom[h], vb) - delta[h])
                    dsb = (ds if fold else ds * scale).astype(MXU_DTYPE)
                    dv_acc = dv_acc + _dot_tn(pr.astype(MXU_DTYPE), dom[h])
                    dk_acc = dk_acc + _dot_tn(dsb, qm[h])
                    dq = dq + _dot(dsb, kb)
                    if bias is not None:
                        dfq = dfq + jnp.sum(ds, axis=1, keepdims=True)
                        dfk_ref[h:h + 1, pl.ds(off, tk)] -= jnp.sum(ds, axis=0, keepdims=True)
                    out.append((dq, dfq))
                dv_ref[pl.ds(off, tk), p * PAIR:(p + 1) * PAIR] += dv_acc
                dk_ref[pl.ds(off, tk), p * PAIR:(p + 1) * PAIR] += dk_acc[:, :PAIR]
                if quad:
                    dk_ref[pl.ds(off, tk), GROUP_W:] += dk_acc[:, PAIR:]
            return tuple(out)

        carry = tuple((jnp.zeros((tq, pp.width), F32), jnp.zeros((tq, 1), F32)) for _ in range(N_HEADS))
        if causal and bias is not None:
            reach = [_logit_reach(qm[h], kmax_ref[...], h) - lse_h[h] for h in range(N_HEADS)]

            def alive(j):
                return functools.reduce(jnp.maximum, [jnp.max(reach[h] + _forget_top(ft_ref, h, _blk_off(j, tk)))
                                                      for h in range(N_HEADS)])

            def step(state):
                n, _, c = state
                return n + 1, alive(i - 1 - n), block(i - 1 - n, c, False)

            carry = block(i, carry, True)
            _, _, carry = lax.while_loop(lambda st: jnp.logical_and(st[0] < i, st[1] > DEAD_LOGIT), step,
                                         (jnp.int32(0), alive(i), carry))
        elif causal:
            carry = lax.fori_loop(0, i, lambda j, c: block(j, c, False), carry)
            carry = block(i, carry, True)
        else:
            for j in range(nkv):
                carry = block(j, carry, False)
        dqs = [c[0] * scale if fold else c[0] for c in carry]
        for p in range(2):
            dq_ref[:, p * PAIR:(p + 1) * PAIR] = _take_heads([dqs[2 * p + e][:, :PAIR] for e in range(2)], p)
        if quad:
            dq_ref[:, GROUP_W:] = sum(jnp.where(_lane_mask("quad", h, tq), dqs[h][:, PAIR:], 0.0)
                                      for h in range(N_HEADS))
        if bias is not None:
            dfq_ref[...] = _scatter_cols({h: carry[h][1] for h in range(N_HEADS)}, tq)

    acc_spec = lambda rows, width: pl.BlockSpec((rows, width), lambda i: (0, 0), pipeline_mode=pl.Buffered(1))
    in_specs = pp.q_specs + pp.k_specs + pp.v_specs + [_row_spec(tq, 0, GROUP_W), _row_spec(tq, 0),
                                                       _row_spec(tq, do_blk, GROUP_W)]
    args = pp.q_args + pp.k_args + pp.v_args + [o, lse, do]
    out_specs = [_row_spec(tq, 0, wq), acc_spec(sk, wq), acc_spec(sk, GROUP_W)]
    out_shape = [jax.ShapeDtypeStruct((sq, wq), F32), jax.ShapeDtypeStruct((sk, wq), F32),
                 jax.ShapeDtypeStruct((sk, GROUP_W), F32)]
    if bias is not None:
        in_specs += [_row_spec(tq, 0), pl.BlockSpec((8, sk), lambda i: (0, 0), pipeline_mode=pl.Buffered(1)),
                     pl.BlockSpec((1, PAIR), lambda i: (0, 0))]
        args += list(bias)
        out_specs += [_row_spec(tq, 0), acc_spec(8, sk)]
        out_shape += [jax.ShapeDtypeStruct((sq, PAIR), F32), jax.ShapeDtypeStruct((8, sk), F32)]
    return pl.pallas_call(
        kern, name=name, grid=(sq // tq,), in_specs=in_specs, out_specs=out_specs, out_shape=out_shape,
        compiler_params=_att_params(False),
    )(*args)


def _sb_logs(qh, kb, valid):
    z = _dot_nt(qh, kb)
    sp = _softplus(z)
    lk = -sp
    if valid is not None:
        lk = jnp.where(valid, lk, 0.0)
    return lk, z - sp


def _sb_valid(d, tq, tk):
    r = lax.broadcasted_iota(jnp.int32, (tq, tk), 0)
    c = lax.broadcasted_iota(jnp.int32, (tq, tk), 1)
    return c + d * tk < r


def _tri_sums(xs, tri):
    t = xs[0].shape[0]
    pieces = [_split2(x) for x in xs]
    hi = _dot(jnp.concatenate([pc[0] for pc in pieces], axis=0), tri)
    lo = _dot(jnp.concatenate([pc[1] for pc in pieces], axis=0), tri)
    return [hi[n * t:(n + 1) * t] + lo[n * t:(n + 1) * t] for n in range(len(xs))]


def _sb_fwd(src, q_blk, k_blk, v_blk, s, scale, name):
    assert _is_pow2(scale) and s // ATT_TILE <= SB_SLOT
    tq, t = min(SB_QUERY_TILE, s), min(ATT_TILE, s)
    band = tq // t
    pair = lambda blk: [[(src, blk + p, "pair")] for p in range(2)]
    pp = _Parts(pair(q_blk), pair(k_blk), [(src, v_blk + p) for p in range(2)], tq, s)

    def kern(*refs):
        q_refs, k_refs, v_refs, (tri_ref, o_ref, rm_ref, cnt_ref) = pp.split(refs)
        i = pl.program_id(0)
        tri = tri_ref[...]
        lane = lax.broadcasted_iota(jnp.int32, (tq, PAIR), 1)
        qm = []
        for p in range(2):
            qm += _masked_heads([q_refs[p][0][...]], ["pair"], p, MXU_DTYPE, scale)

        def block(j, carry, valid):
            accs, rights, rm = carry
            off = _blk_off(j, t)
            kbs = [k_refs[p][0][pl.ds(off, t), :] for p in range(2)]
            vbs = [v_refs[p][pl.ds(off, t), :] for p in range(2)]
            logs = [_sb_logs(qm[h], kbs[h // 2], valid) for h in range(N_HEADS)]
            tails = _tri_sums([lg[0] for lg in logs], tri)
            new_acc, new_right = [], []
            for h in range(N_HEADS):
                lk, ls = logs[h]
                w = jnp.exp(ls + tails[h] + rights[h])
                if valid is not None:
                    w = jnp.where(valid, w, 0.0)
                new_acc.append(accs[h] + _dot(w.astype(MXU_DTYPE), vbs[h // 2]))
                rm = rm + jnp.where(lane == SB_SLOT * h + j, rights[h], 0.0)
                new_right.append(rights[h] + jnp.sum(lk, axis=1, keepdims=True))
            return tuple(new_acc), tuple(new_right), rm

        carry = (tuple(jnp.zeros((tq, PAIR), F32) for _ in range(N_HEADS)),
                 tuple(jnp.zeros((tq, 1), F32) for _ in range(N_HEADS)), jnp.zeros((tq, PAIR), F32))
        for d in reversed(range(band)):
            carry = block(band * i + d, carry, _sb_valid(d, tq, t))

        def alive(c):
            return functools.reduce(jnp.maximum, [jnp.max(r) for r in c[1]])

        def step(state):
            n, _, c = state
            c = block(band * i - 1 - n, c, None)
            return n + 1, alive(c), c

        n_done, _, carry = lax.while_loop(lambda st: jnp.logical_and(st[0] < band * i, st[1] > EXP_UNDERFLOW),
                                          step, (jnp.int32(0), alive(carry), carry))
        cnt_ref[i] = n_done
        for p in range(2):
            o_ref[:, p * PAIR:(p + 1) * PAIR] = _take_heads([carry[0][2 * p + e] for e in range(2)], p)
        rm_ref[...] = carry[2]

    return pl.pallas_call(
        kern, name=name, grid=(s // tq,),
        in_specs=pp.q_specs + pp.k_specs + pp.v_specs + [pl.BlockSpec((t, t), lambda i: (0, 0))],
        out_specs=[_row_spec(tq, 0, GROUP_W), _row_spec(tq, 0), pl.BlockSpec(memory_space=pltpu.SMEM)],
        out_shape=[jax.ShapeDtypeStruct((s, GROUP_W), F32), jax.ShapeDtypeStruct((s, PAIR), F32),
                   jax.ShapeDtypeStruct((s // tq,), jnp.int32)],
        compiler_params=_att_params(False),
    )(*(pp.q_args + pp.k_args + pp.v_args + [_tri(t, "row_gt_col")]))


def _sb_bwd(src, q_blk, k_blk, v_blk, do, do_blk, rm, visited, s, scale, name):
    assert _is_pow2(scale)
    tq, t = min(SB_QUERY_TILE, s), min(ATT_TILE, s)
    band = tq // t
    pair = lambda blk: [[(src, blk + p, "pair")] for p in range(2)]
    pp = _Parts(pair(q_blk), pair(k_blk), [(src, v_blk + p) for p in range(2)], tq, s)

    def kern(*refs):
        q_refs, k_refs, v_refs, (do_ref, rm_ref, tri_ref, pre_ref, cnt_ref, dq_ref, dk_ref, dv_ref) = pp.split(refs)
        i = pl.program_id(0)

        @pl.when(i == 0)
        def _():
            dk_ref[...] = jnp.zeros_like(dk_ref)
            dv_ref[...] = jnp.zeros_like(dv_ref)

        rmb = rm_ref[...]
        tri = tri_ref[...]
        pre = pre_ref[...]
        qm, dom = [], []
        for p in range(2):
            qm += _masked_heads([q_refs[p][0][...]], ["pair"], p, MXU_DTYPE, scale)
            dom += _masked_heads([do_ref[:, p * PAIR:(p + 1) * PAIR]], ["pair"], p, MXU_DTYPE)

        def block(j, carry, valid):
            dqs, lefts = carry
            off = _blk_off(j, t)
            kbs = [k_refs[p][0][pl.ds(off, t), :] for p in range(2)]
            vbs = [v_refs[p][pl.ds(off, t), :] for p in range(2)]
            logs = [_sb_logs(qm[h], kbs[h // 2], valid) for h in range(N_HEADS)]
            tails = _tri_sums([lg[0] for lg in logs], tri)
            ws, gs = [], []
            for h in range(N_HEADS):
                lk, ls = logs[h]
                w = jnp.exp(ls + tails[h] + _col(rmb, SB_SLOT * h + j))
                if valid is not None:
                    w = jnp.where(valid, w, 0.0)
                ws.append(w)
                gs.append(_dot_nt(dom[h], vbs[h // 2]) * w)
            prefix = _tri_sums(gs, pre)
            new_dq, new_left = [], []
            dk_acc = [jnp.zeros((t, PAIR), F32) for _ in range(2)]
            dv_acc = [jnp.zeros((t, PAIR), F32) for _ in range(2)]
            for h in range(N_HEADS):
                lk, ls = logs[h]
                sig = jnp.exp(ls)
                dz = gs[h] * (1.0 - sig) - sig * (prefix[h] + lefts[h])
                if valid is not None:
                    dz = jnp.where(valid, dz, 0.0)
                dzb = dz.astype(MXU_DTYPE)
                dv_acc[h // 2] = dv_acc[h // 2] + _dot_tn(ws[h].astype(MXU_DTYPE), dom[h])
                dk_acc[h // 2] = dk_acc[h // 2] + _dot_tn(dzb, qm[h])
                new_dq.append(dqs[h] + _dot(dzb, kbs[h // 2]))
                new_left.append(lefts[h] + jnp.sum(gs[h], axis=1, keepdims=True))
            for p in range(2):
                dv_ref[pl.ds(off, t), p * PAIR:(p + 1) * PAIR] += dv_acc[p]
                dk_ref[pl.ds(off, t), p * PAIR:(p + 1) * PAIR] += dk_acc[p]
            return tuple(new_dq), tuple(new_left)

        carry = (tuple(jnp.zeros((tq, PAIR), F32) for _ in range(N_HEADS)),
                 tuple(jnp.zeros((tq, 1), F32) for _ in range(N_HEADS)))
        carry = lax.fori_loop(band * i - cnt_ref[i], band * i, lambda j, c: block(j, c, None), carry)
        for d in range(band):
            carry = block(band * i + d, carry, _sb_valid(d, tq, t))
        for p in range(2):
            dq_ref[:, p * PAIR:(p + 1) * PAIR] = _take_heads([carry[0][2 * p + e] * scale for e in range(2)], p)

    mspec = pl.BlockSpec((t, t), lambda i: (0, 0))
    acc_spec = pl.BlockSpec((s, GROUP_W), lambda i: (0, 0), pipeline_mode=pl.Buffered(1))
    return pl.pallas_call(
        kern, name=name, grid=(s // tq,),
        in_specs=pp.q_specs + pp.k_specs + pp.v_specs + [_row_spec(tq, do_blk, GROUP_W), _row_spec(tq, 0), mspec, mspec,
                                                         pl.BlockSpec(memory_space=pltpu.SMEM)],
        out_specs=[_row_spec(tq, 0, GROUP_W), acc_spec, acc_spec],
        out_shape=[jax.ShapeDtypeStruct((s, GROUP_W), F32)] * 3,
        compiler_params=_att_params(False),
    )(*(pp.q_args + pp.k_args + pp.v_args + [do, rm, _tri(t, "row_gt_col"), _tri(t, "row_lt_col"), visited]))


def _split_w_in(w):
    col = lambda n: w[:, _OFF[n]:_OFF[n + 1]]
    wa = jnp.concatenate([col(0), col(1), col(2), col(4), col(5), col(6), col(10)], axis=1)
    misc = jnp.concatenate([col(3), col(9), jnp.zeros((w.shape[0], 128 - 4 - MLA_ROPE), w.dtype)], axis=1)
    wb = jnp.concatenate([col(11), col(7), col(8), misc], axis=1)
    return wa, wb


def _merge_dw_in(dwp):
    a = lambda n: dwp[:, n * GROUP_W:(n + 1) * GROUP_W]
    b0 = PA_COLS
    gate = dwp[:, b0:b0 + 1024]
    cq = dwp[:, b0 + 1024:b0 + 1280]
    ckv = dwp[:, b0 + 1280:b0 + 1408]
    flog = dwp[:, b0 + 1408:b0 + 1412]
    krot = dwp[:, b0 + 1408 + MISC_KROT:b0 + 1408 + MISC_KROT + MLA_ROPE]
    return jnp.concatenate([a(0), a(1), a(2), flog, a(3), a(4), a(5), cq, ckv, krot, a(6), gate], axis=1)


def _heads_first(w, per_head, first):
    r = w.shape[0]
    w3 = w.reshape(r, N_HEADS, per_head)
    return jnp.concatenate([w3[:, :, :first].reshape(r, -1), w3[:, :, first:].reshape(r, -1)], axis=1)


def _heads_interleaved(w, per_head, first):
    r = w.shape[0]
    a = w[:, :N_HEADS * first].reshape(r, N_HEADS, first)
    b = w[:, N_HEADS * first:].reshape(r, N_HEADS, per_head - first)
    return jnp.concatenate([a, b], axis=2).reshape(r, N_HEADS * per_head)


def _pad_rows8(a):
    return a[:, :8].T


def _local_step(x2, mem2, tgt, p):
    s = x2.shape[0]
    nm = mem2.shape[0]
    head_scale = HEAD_DIM ** -0.5
    mla_scale = (HEAD_DIM + MLA_ROPE) ** -0.5
    tabs = _rope_tables(s)
    mats = _rope_matrices()
    pairs = lambda arr, blk: [[(arr, blk + q, "pair")] for q in range(2)]
    vals = lambda arr, blk: [(arr, blk + q) for q in range(2)]

    h, hb = _ln_fwd(x2, None, p["ln_in_g"], p["ln_in_b"], "ln_in_fwd")
    _, memn_b = _ln_fwd(mem2, None, p["mem_ln_g"], p["mem_ln_b"], "ln_mem_fwd")

    saved = []
    for l in range(DEPTH):
        wa, wb = _split_w_in(p["w_in"][l])
        wp = jnp.concatenate([wa, wb], axis=1)
        wq_up = _heads_first(p["w_mla_q_up"][l], HEAD_DIM + MLA_ROPE, HEAD_DIM)
        wkv_up = _heads_first(p["w_mla_kv_up"][l], 2 * HEAD_DIM, HEAD_DIM)
        bias_row = jnp.pad(p["b_forget"][l], (0, 128 - N_HEADS)).reshape(1, 128)
        pa = _matmul(hb, wa, MXU_DTYPE, "proj_a")
        pb = _matmul(hb, wb, F32, "proj_b")

        fc = _forget_fwd(pb, bias_row)
        fbias = (fc, _pad_rows8(fc), _key_norm_max(pa, 2))
        o_fox, lse_fox = _softmax_fwd(pairs(pa, 0), pairs(pa, 2), vals(pa, 4), s, s, head_scale, True, fbias,
                                      "fox_fwd")
        o_sb, *rm_sb = _sb_fwd(pa, 6, 8, 10, s, head_scale, "sb_fwd")

        qfull, cqn = _mla_q_fwd(pb, p["mla_q_norm_g"][l], wq_up, tabs, mats)
        kv, ckvn, kr4 = _mla_kv_fwd(pb, p["mla_kv_norm_g"][l], wkv_up, tabs, mats)
        mla_q = [[(qfull, q, "pair"), (qfull, 2, "quad")] for q in range(2)]
        mla_k = [[(kv, q, "pair"), (kr4, 0, "quad")] for q in range(2)]
        o_mla, lse_mla = _softmax_fwd(mla_q, mla_k, vals(kv, 2), s, s, mla_scale, True, None, "mla_fwd")

        mkv = _matmul(memn_b, p["w_mem_kv"][l], MXU_DTYPE, "mem_kv")
        o_mem, lse_mem = _softmax_fwd(pairs(pa, 12), pairs(mkv, 0), vals(mkv, 2), s, nm, head_scale, False, None,
                                      "mem_fwd")

        groups = (o_fox, o_sb, o_mla, o_mem)
        gated = _gate_fwd(groups, pb)
        y = _matmul(gated, p["w_out"][l], F32, "out_proj")
        saved.append(dict(h=h, hb=hb, y=y, wp=wp, wq_up=wq_up, wkv_up=wkv_up, bias_row=bias_row, pa=pa, pb=pb,
                          fbias=fbias, lse_fox=lse_fox, rm_sb=rm_sb, cqn=cqn, ckvn=ckvn, mla_q=mla_q, mla_k=mla_k,
                          kv=kv, lse_mla=lse_mla, mkv=mkv, lse_mem=lse_mem, groups=groups, gated=gated))
        h, hb = _ln_fwd(h, y, p["ln_g"][l], p["ln_b"][l], "ln_fwd")

    dh, sq_cols = _loss_grad(h, tgt)
    loss_sum = jnp.sum(sq_cols)

    grads = {k: [None] * DEPTH for k in ("w_in", "b_forget", "mla_q_norm_g", "w_mla_q_up", "mla_kv_norm_g",
                                         "w_mla_kv_up", "w_mem_kv", "w_out", "ln_g", "ln_b")}
    dmemn = []
    dy1, dy2, c1 = dh, None, 1.0
    for l in reversed(range(DEPTH)):
        r = saved[l]
        pa, pb = r["pa"], r["pb"]
        o_fox, o_sb, o_mla, o_mem = r["groups"]
        du, du_b, dg, db = _ln_bwd(r["h"], r["y"], p["ln_g"][l], dy1, dy2, c1, "ln_bwd")
        grads["ln_g"][l], grads["ln_b"][l] = dg[0], db[0]
        dgated = _matmul(du_b, p["w_out"][l], F32, "out_proj_dx", "nt")
        grads["w_out"][l] = _matmul(r["gated"], du_b, F32, "out_proj_dw", "tn")
        dmixed, dgate_b = _gate_bwd(dgated, r["groups"], pb)

        dfq, dfk, dfv, dfc_q, dfc_k = _softmax_bwd(pairs(pa, 0), pairs(pa, 2), vals(pa, 4), o_fox, r["lse_fox"],
                                                   dmixed, 0, s, s, head_scale, True, r["fbias"], "fox_bwd")
        dmisc_f, dbf = _forget_bwd(pb, r["bias_row"], dfc_q + jnp.pad(dfc_k.T, ((0, 0), (0, 128 - 8))))
        grads["b_forget"][l] = dbf[0, :N_HEADS]

        dsq, dsk, dsv = _sb_bwd(pa, 6, 8, 10, dmixed, 1, *r["rm_sb"], s, head_scale, "sb_bwd")

        dqm, dkm, dvm = _softmax_bwd(r["mla_q"], r["mla_k"], vals(r["kv"], 2), o_mla, r["lse_mla"], dmixed, 2,
                                     s, s, mla_scale, True, None, "mla_bwd")
        dq_mla_b = _rope_q_bwd(dqm, tabs, mats)
        dcqn = _matmul(dq_mla_b, r["wq_up"], F32, "q_up_dx", "nt")
        grads["w_mla_q_up"][l] = _heads_interleaved(_matmul(r["cqn"], dq_mla_b, F32, "q_up_dw", "tn"),
                                                    HEAD_DIM + MLA_ROPE, HEAD_DIM)
        dcq_b, dgq = _rms_bwd(pb, MLA_Q_RANK, PB_CQ_BLK, p["mla_q_norm_g"][l], dcqn, "rms_q_bwd")
        grads["mla_q_norm_g"][l] = dgq[0]
        dkv_b = jnp.concatenate([dkm[:, :GROUP_W], dvm], axis=1).astype(MXU_DTYPE)
        dckvn = _matmul(dkv_b, r["wkv_up"], F32, "kv_up_dx", "nt")
        grads["w_mla_kv_up"][l] = _heads_interleaved(_matmul(r["ckvn"], dkv_b, F32, "kv_up_dw", "tn"),
                                                     2 * HEAD_DIM, HEAD_DIM)
        dckv_b, dgkv = _rms_bwd(pb, MLA_KV_RANK, PB_CKV_BLK, p["mla_kv_norm_g"][l], dckvn, "rms_kv_bwd")
        grads["mla_kv_norm_g"][l] = dgkv[0]
        dmisc_k = _rope_k_bwd(dkm[:, GROUP_W:], tabs, mats)

        dmq, dmk, dmv = _softmax_bwd(pairs(pa, 12), pairs(r["mkv"], 0), vals(r["mkv"], 2), o_mem, r["lse_mem"],
                                     dmixed, 3, s, nm, head_scale, False, None, "mem_bwd")
        dmkv_b = jnp.concatenate([dmk, dmv], axis=1).astype(MXU_DTYPE)
        grads["w_mem_kv"][l] = _matmul(memn_b, dmkv_b, F32, "mem_kv_dw", "tn")
        dmemn.append(_matmul(dmkv_b, p["w_mem_kv"][l], F32, "mem_kv_dx", "nt"))

        dmisc = jnp.concatenate([dmisc_f[:, :MISC_KROT], dmisc_k[:, MISC_KROT:MISC_KROT + MLA_ROPE],
                                 jnp.zeros((s, 128 - MISC_KROT - MLA_ROPE), F32)], axis=1)
        dp = jnp.concatenate([d.astype(MXU_DTYPE) for d in (dfq, dfk, dfv, dsq, dsk, dsv, dmq)]
                             + [dgate_b, dcq_b, dckv_b, dmisc.astype(MXU_DTYPE)], axis=1)
        dhproj = _matmul(dp, r["wp"], F32, "proj_dx", "nt")
        grads["w_in"][l] = _merge_dw_in(_matmul(r["hb"], dp, F32, "proj_dw", "tn"))
        dy1, dy2, c1 = du, dhproj, ALPHA

    dx, _, dg_in, db_in = _ln_bwd(x2, None, p["ln_in_g"], dy1, dy2, c1, "ln_in_bwd")
    _, _, dg_mem, db_mem = _ln_bwd(mem2, None, p["mem_ln_g"], dmemn[0], dmemn[1], 1.0, "ln_mem_bwd")
    out = {k: jnp.stack(v) for k, v in grads.items()}
    out.update(ln_in_g=dg_in[0], ln_in_b=db_in[0], mem_ln_g=dg_mem[0], mem_ln_b=db_mem[0])
    return loss_sum, dx, out


WIDE = "w_in"
FLAT_NAMES = ("w_out", "w_mem_kv", "w_mla_q_up", "w_mla_kv_up")
FLAT_ROWS = 896
BIG_NAMES = (WIDE,) + FLAT_NAMES
BIG_AXIS = dict(w_in=2, w_out=1, w_mem_kv=1, w_mla_q_up=2, w_mla_kv_up=2)
SMALL_NAMES = ("ln_in_g", "ln_in_b", "mem_ln_g", "mem_ln_b", "ln_g", "ln_b", "b_forget", "mla_q_norm_g",
               "mla_kv_norm_g")
ALL_NAMES = ("ln_in_g", "ln_in_b", "mem_ln_g", "mem_ln_b", "w_in", "b_forget", "mla_q_norm_g", "w_mla_q_up",
             "mla_kv_norm_g", "w_mla_kv_up", "w_mem_kv", "w_out", "ln_g", "ln_b")
N_CHIPS = 4
N_DEV = 8


def _rows_of(shape):
    rows = -(-int(np.prod(shape)) // LANES)
    return -(-rows // PACK_ALIGN) * PACK_ALIGN


def _pack(arrs, rows):
    parts = []
    for a in arrs:
        f = a.reshape(-1)
        n = _rows_of(a.shape) * LANES
        parts.append(jnp.pad(f, (0, n - f.shape[0])).reshape(-1, LANES))
    used = sum(q.shape[0] for q in parts)
    if rows > used:
        parts.append(jnp.zeros((rows - used, LANES), parts[0].dtype))
    return jnp.concatenate(parts, axis=0)


def _unpack(buf, shapes):
    out, r = [], 0
    for shp in shapes:
        n = _rows_of(shp)
        out.append(buf[r:r + n].reshape(-1)[:int(np.prod(shp))].reshape(shp))
        r += n
    return out


def _sharded_pair(get):
    wide = get(WIDE)
    return [wide.reshape(-1, wide.shape[-1]), _pack([get(n) for n in FLAT_NAMES], FLAT_ROWS)]


HBM_SPEC = pl.BlockSpec(memory_space=pltpu.HBM)
SWAP_COLLECTIVE_ID = 0


def _gather_weights(shards):
    n = len(shards)

    def body(*refs):
        w_refs, out_refs, (send_sems, recv_sems, local_sems) = refs[:n], refs[n:2 * n], refs[2 * n:]
        x, y, c = (lax.axis_index(a) for a in MESH_AXES)
        me, sibling = 2 * x + y, (x, y, 1 - c)
        chips = [(1 - x, y), (x, 1 - y), (1 - x, 1 - y)]
        local, first, passed = [], [], []
        for a in range(n):
            w_ref, out_ref, half = w_refs[a], out_refs[a], shards[a].shape[0] // 2

            def part(chip, core, out_ref=out_ref, half=half):
                return out_ref.at[chip, pl.ds(core * half, half)]

            def copy(k, src, dst, to, a=a):
                return pltpu.make_async_remote_copy(
                    src_ref=src, dst_ref=dst, send_sem=send_sems.at[6 * a + k], recv_sem=recv_sems.at[6 * a + k],
                    device_id=to, device_id_type=pl.DeviceIdType.MESH)

            local.append(pltpu.make_async_copy(w_ref, out_ref.at[me], local_sems.at[a]))
            local[-1].start()
            mine = [copy(k, w_ref.at[pl.ds(c * half, half)], part(me, c), (px, py, c))
                    for k, (px, py) in enumerate(chips)]
            for cp in mine:
                cp.start()
            first.append((mine, part, copy))
        for mine, part, copy in first:
            for k, (px, py) in enumerate(chips):
                copy(k, part(me, c), part(2 * px + py, c), (px, py, c)).wait_recv()
                passed.append(copy(3 + k, part(2 * px + py, c), part(2 * px + py, c), sibling))
                passed[-1].start()
        for mine, part, copy in first:
            for k, (px, py) in enumerate(chips):
                copy(3 + k, part(me, c), part(2 * px + py, 1 - c), sibling).wait_recv()
        for cp in [cp for mine, _, _ in first for cp in mine] + passed:
            cp.wait_send()
        for cp in local:
            cp.wait()

    return pl.pallas_call(
        body, name="gather_weights",
        out_shape=[jax.ShapeDtypeStruct((N_CHIPS,) + s.shape, s.dtype) for s in shards],
        in_specs=[HBM_SPEC] * n, out_specs=[HBM_SPEC] * n,
        scratch_shapes=[pltpu.SemaphoreType.DMA((6 * n,)), pltpu.SemaphoreType.DMA((6 * n,)),
                        pltpu.SemaphoreType.DMA((n,))],
    )(*shards)


def _exchange_grads(bigs, small):
    nb = len(bigs)

    def body(*refs):
        big_refs, small_ref, out_refs = refs[:nb], refs[nb], refs[nb + 1:2 * nb + 2]
        send_sems, recv_sems, local_sems = refs[2 * nb + 2:]
        x, y, c = (lax.axis_index(a) for a in MESH_AXES)
        me, my_chip = 4 * x + 2 * y + c, 2 * x + y
        flips = [(fx, fy, fc) for fx in (0, 1) for fy in (0, 1) for fc in (0, 1) if fx or fy or fc]
        peers = [(1 - x if fx else x, 1 - y if fy else y, 1 - c if fc else c) for fx, fy, fc in flips]

        sources = [lambda chip, core, r=big_refs[a], half=bigs[a].shape[1] // 2: r.at[chip, pl.ds(core * half, half)]
                   for a in range(nb)] + [lambda chip, core: small_ref]

        def copy(a, k, src, slot, to):
            return pltpu.make_async_remote_copy(
                src_ref=src, dst_ref=out_refs[a].at[slot], send_sem=send_sems.at[7 * a + k],
                recv_sem=recv_sems.at[7 * a + k], device_id=to, device_id_type=pl.DeviceIdType.MESH)

        own = [pltpu.make_async_copy(src(my_chip, c), out_refs[a].at[me], local_sems.at[a])
               for a, src in enumerate(sources)]
        for cp in own:
            cp.start()
        sends = [copy(a, k, src(2 * px + py, pc), me, (px, py, pc))
                 for a, src in enumerate(sources) for k, (px, py, pc) in enumerate(peers)]
        for cp in sends:
            cp.start()
        for a, src in enumerate(sources):
            for k, (px, py, pc) in enumerate(peers):
                copy(a, k, src(my_chip, c), 4 * px + 2 * py + pc, (px, py, pc)).wait_recv()
        for cp in sends:
            cp.wait_send()
        for cp in own:
            cp.wait()

    return pl.pallas_call(
        body, name="exchange_grads",
        out_shape=[jax.ShapeDtypeStruct((N_DEV, b.shape[1] // 2, b.shape[2]), b.dtype) for b in bigs]
        + [jax.ShapeDtypeStruct((N_DEV,) + small.shape, small.dtype)],
        in_specs=[HBM_SPEC] * (nb + 1), out_specs=[HBM_SPEC] * (nb + 1),
        scratch_shapes=[pltpu.SemaphoreType.DMA((7 * (nb + 1),)), pltpu.SemaphoreType.DMA((7 * (nb + 1),)),
                        pltpu.SemaphoreType.DMA((nb + 1,))],
    )(*bigs, small)


def _sum_parts(parts, name):
    n, rows, width = parts.shape
    tile = _pick(rows, (128, 16, 8))

    def kern(p_ref, o_ref):
        g = p_ref[0].astype(F32)
        for d in range(1, n):
            g = g + p_ref[d].astype(F32)
        o_ref[...] = g

    return pl.pallas_call(
        kern, name=name, grid=(rows // tile,),
        in_specs=[pl.BlockSpec((n, tile, width), lambda i: (0, i, 0))],
        out_specs=pl.BlockSpec((tile, width), lambda i: (i, 0)), out_shape=jax.ShapeDtypeStruct((rows, width), F32),
        compiler_params=pltpu.CompilerParams(dimension_semantics=("parallel",), vmem_limit_bytes=VMEM_LIMIT),
    )(parts)


def _swap_halves(mine):
    n = len(mine)

    def body(*refs):
        q_refs, out_refs, (send_sems, recv_sems, local_sems) = refs[:n], refs[n:2 * n], refs[2 * n:]
        x, y, c = (lax.axis_index(a) for a in MESH_AXES)
        barrier = pltpu.get_barrier_semaphore()
        pl.semaphore_signal(barrier, inc=1, device_id=(x, y, 1 - c), device_id_type=pl.DeviceIdType.MESH)
        pl.semaphore_wait(barrier, 1)

        def copy(a, dst):
            return pltpu.make_async_remote_copy(
                src_ref=q_refs[a], dst_ref=dst, send_sem=send_sems.at[a], recv_sem=recv_sems.at[a],
                device_id=(x, y, 1 - c), device_id_type=pl.DeviceIdType.MESH)

        own = [pltpu.make_async_copy(q_refs[a], out_refs[a].at[c], local_sems.at[a]) for a in range(n)]
        swaps = [copy(a, out_refs[a].at[c]) for a in range(n)]
        for cp in own + swaps:
            cp.start()
        for a in range(n):
            copy(a, out_refs[a].at[1 - c]).wait_recv()
        for cp in swaps:
            cp.wait_send()
        for cp in own:
            cp.wait()

    return pl.pallas_call(
        body, name="swap_halves", out_shape=[jax.ShapeDtypeStruct((2,) + q.shape, q.dtype) for q in mine],
        in_specs=[HBM_SPEC] * n, out_specs=[HBM_SPEC] * n,
        scratch_shapes=[pltpu.SemaphoreType.DMA((n,)), pltpu.SemaphoreType.DMA((n,)), pltpu.SemaphoreType.DMA((n,))],
        compiler_params=pltpu.CompilerParams(collective_id=SWAP_COLLECTIVE_ID),
    )(*mine)


def _adamw(parts, w, m, v, name):
    rows, width = w.shape
    n_parts = parts.shape[0]
    tile = _pick(rows, (128, 16, 8))
    bc1 = 1.0 - ADAM_B1 ** ADAM_STEP
    bc2 = 1.0 - ADAM_B2 ** ADAM_STEP

    def kern(p_ref, w_ref, m_ref, v_ref, g_ref, d_ref, nm_ref, nv_ref):
        g = p_ref[0].astype(F32)
        for d in range(1, n_parts):
            g = g + p_ref[d].astype(F32)
        nm = ADAM_B1 * m_ref[...] + (1.0 - ADAM_B1) * g
        nv = ADAM_B2 * v_ref[...] + (1.0 - ADAM_B2) * (g * g)
        g_ref[...] = g
        nm_ref[...] = nm
        nv_ref[...] = nv
        d_ref[...] = -ADAM_LR * ((nm / bc1) / (jnp.sqrt(nv / bc2) + ADAM_EPS) + ADAM_WD * w_ref[...])

    spec = pl.BlockSpec((tile, width), lambda i: (i, 0))
    return pl.pallas_call(
        kern, name=name, grid=(rows // tile,),
        in_specs=[pl.BlockSpec((n_parts, tile, width), lambda i: (0, i, 0)), spec, spec, spec],
        out_specs=[spec] * 4, out_shape=[jax.ShapeDtypeStruct((rows, width), F32)] * 4,
        compiler_params=pltpu.CompilerParams(dimension_semantics=("parallel",), vmem_limit_bytes=VMEM_LIMIT),
    )(parts, w, m, v)


def kernel(x, mem, ln_in_g, ln_in_b, mem_ln_g, mem_ln_b, w_in, b_forget, mla_q_norm_g, w_mla_q_up, mla_kv_norm_g, w_mla_kv_up, w_mem_kv, w_out, ln_g, ln_b, loss_target, m_ln_in_g, m_ln_in_b, m_mem_ln_g, m_mem_ln_b, m_w_in, m_b_forget, m_mla_q_norm_g, m_w_mla_q_up, m_mla_kv_norm_g, m_w_mla_kv_up, m_w_mem_kv, m_w_out, m_ln_g, m_ln_b, v_ln_in_g, v_ln_in_b, v_mem_ln_g, v_mem_ln_b, v_w_in, v_b_forget, v_mla_q_norm_g, v_w_mla_q_up, v_mla_kv_norm_g, v_w_mla_kv_up, v_w_mem_kv, v_w_out, v_ln_g, v_ln_b):
    w = dict(ln_in_g=ln_in_g, ln_in_b=ln_in_b, mem_ln_g=mem_ln_g, mem_ln_b=mem_ln_b, w_in=w_in, b_forget=b_forget,
             mla_q_norm_g=mla_q_norm_g, w_mla_q_up=w_mla_q_up, mla_kv_norm_g=mla_kv_norm_g,
             w_mla_kv_up=w_mla_kv_up, w_mem_kv=w_mem_kv, w_out=w_out, ln_g=ln_g, ln_b=ln_b)
    mo = dict(ln_in_g=m_ln_in_g, ln_in_b=m_ln_in_b, mem_ln_g=m_mem_ln_g, mem_ln_b=m_mem_ln_b, w_in=m_w_in,
              b_forget=m_b_forget, mla_q_norm_g=m_mla_q_norm_g, w_mla_q_up=m_w_mla_q_up,
              mla_kv_norm_g=m_mla_kv_norm_g, w_mla_kv_up=m_w_mla_kv_up, w_mem_kv=m_w_mem_kv, w_out=m_w_out,
              ln_g=m_ln_g, ln_b=m_ln_b)
    vo = dict(ln_in_g=v_ln_in_g, ln_in_b=v_ln_in_b, mem_ln_g=v_mem_ln_g, mem_ln_b=v_mem_ln_b, w_in=v_w_in,
              b_forget=v_b_forget, mla_q_norm_g=v_mla_q_norm_g, w_mla_q_up=v_w_mla_q_up,
              mla_kv_norm_g=v_mla_kv_norm_g, w_mla_kv_up=v_w_mla_kv_up, w_mem_kv=v_w_mem_kv, w_out=v_w_out,
              ln_g=v_ln_g, ln_b=v_ln_b)
    flat_shapes = [w[n].shape for n in FLAT_NAMES]
    small_shapes = [w[n].shape for n in SMALL_NAMES]

    got_wide, got_flat = _gather_weights(_sharded_pair(lambda n: w[n].astype(MXU_DTYPE)))
    full = dict(w)
    full[WIDE] = jnp.concatenate([got_wide[j] for j in range(N_CHIPS)], axis=1).reshape(
        w[WIDE].shape[:2] + (N_CHIPS * w[WIDE].shape[2],))
    per_chip = [_unpack(got_flat[j], flat_shapes) for j in range(N_CHIPS)]
    for idx, n in enumerate(FLAT_NAMES):
        full[n] = jnp.concatenate([per_chip[j][idx] for j in range(N_CHIPS)], axis=BIG_AXIS[n])

    loss_sum, dx, g = _local_step(x[0], mem[0], loss_target[0], full)
    loss = lax.psum(loss_sum * (0.5 / D_MODEL), MESH_AXES)

    def shard_of(n, j):
        ax, size = BIG_AXIS[n], w[n].shape[BIG_AXIS[n]]
        return lax.slice_in_dim(g[n], j * size, (j + 1) * size, axis=ax).astype(MXU_DTYPE)

    per_dest = [_sharded_pair(lambda n, j=j: shard_of(n, j)) for j in range(N_CHIPS)]
    bigs = [jnp.stack([per_dest[j][a] for j in range(N_CHIPS)]) for a in range(2)]
    *big_parts, small_parts = _exchange_grads(bigs, _pack([g[n] for n in SMALL_NAMES], SMALL_ROWS))
    halves = _swap_halves([_sum_parts(big_parts[0], "sum_wide"), _sum_parts(big_parts[1], "sum_flat")])

    res = []
    for a, (grad, nm) in enumerate(zip(halves, ("adamw_wide", "adamw_flat"))):
        state = [_sharded_pair(lambda n, src=src: src[n])[a] for src in (w, mo, vo)]
        res.append(_adamw(grad.reshape((1,) + state[0].shape), *state, nm))
    res_small = _adamw(small_parts, *[_pack([src[n] for n in SMALL_NAMES], SMALL_ROWS) for src in (w, mo, vo)],
                       "adamw_replicated")
    outs = []
    for kind in range(4):
        vals = {WIDE: res[0][kind].reshape(w[WIDE].shape)}
        vals.update(zip(FLAT_NAMES, _unpack(res[1][kind], flat_shapes)))
        vals.update(zip(SMALL_NAMES, _unpack(res_small[kind], small_shapes)))
        outs += [vals[n] for n in ALL_NAMES]
    return (loss, dx[None], *outs)
```

```python
import functools

import numpy as np
import jax
import jax.numpy as jnp
from jax import lax
from jax.experimental import pallas as pl
from jax.experimental.pallas import tpu as pltpu

F32 = jnp.float32
MXU_DTYPE = jnp.bfloat16

DEPTH = 2
D_MODEL = 1024
GROUP_W = 256
N_HEADS = 4
HEAD_DIM = 64
MLA_Q_RANK = 256
MLA_KV_RANK = 128
MLA_ROPE = 32
MLA_Q_COLS = N_HEADS * (HEAD_DIM + MLA_ROPE)
MLA_KV_COLS = N_HEADS * 2 * HEAD_DIM
ROPE_THETA = 10000.0
LN_EPS = 1e-5
RMS_EPS = 1e-6
ALPHA = (2 * DEPTH) ** 0.25
ADAM_LR, ADAM_B1, ADAM_B2, ADAM_EPS, ADAM_WD, ADAM_STEP = 0.001, 0.9, 0.999, 1e-08, 0.01, 10

_SPLIT = (256, 256, 256, 4, 256, 256, 256, 256, 128, 32, 256, 1024)
_OFF = [int(o) for o in np.cumsum((0,) + _SPLIT)]
IN_COLS = _OFF[-1]
PA_COLS = 7 * GROUP_W
PB_COLS = 1024 + 256 + 128 + 128
PB_CQ_BLK, PB_CKV_BLK, PB_MISC_BLK = 4, 10, 11
MISC_KROT = 4

LANES = 1024
PACK_ALIGN = 16
BIG_ROWS = 2560
SMALL_ROWS = 144
ROW_TILE = 256
ATT_TILE = 256
SB_QUERY_TILE = 256
SOFTMAX_TILE = 512
PAIR = 128
SB_SLOT = PAIR // N_HEADS
VMEM_LIMIT = 56 * 1024 * 1024
MATMUL_VMEM = 30 * 1024 * 1024
NEG = -1e30
EXP_UNDERFLOW = -104.0
DEAD_LOGIT = -110.0
REACH_SLACK = 1.0 + 2.0 ** -10
MESH_AXES = ("x", "y", "c")


def _dot(a, b):
    return jnp.dot(a, b, preferred_element_type=F32)


def _dot_nt(a, b):
    return lax.dot_general(a, b, (((1,), (1,)), ((), ())), preferred_element_type=F32)


def _dot_tn(a, b):
    return lax.dot_general(a, b, (((0,), (0,)), ((), ())), preferred_element_type=F32)


def _split2(x):
    hi = x.astype(MXU_DTYPE)
    lo = (x - hi.astype(F32)).astype(MXU_DTYPE)
    return hi, lo


def _split3(x):
    hi = x.astype(MXU_DTYPE)
    r = x - hi.astype(F32)
    mid = r.astype(MXU_DTYPE)
    lo = (r - mid.astype(F32)).astype(MXU_DTYPE)
    return hi, mid, lo


def _dot_exact_r(x, pm):
    hi, mid, lo = _split3(x)
    return _dot(hi, pm) + _dot(mid, pm) + _dot(lo, pm)


def _dot_exact_l(pm, x):
    hi, mid, lo = _split3(x)
    return _dot(pm, hi) + _dot(pm, mid) + _dot(pm, lo)


def _pick(dim, prefs):
    for p in prefs:
        if dim % p == 0:
            return p
    return dim


def _softplus(z):
    return jnp.maximum(z, 0.0) + jnp.log(1.0 + jnp.exp(-jnp.abs(z)))


def _tile_options(dim):
    opts = [d for d in range(128, min(dim, 2048) + 1, 128) if dim % d == 0]
    return opts or [dim]


def _matmul_tiles(m, n, k, out_bytes):
    tk = k if k <= 4096 else _pick(k, (1024, 512, 256, 128))
    best = None
    for tm in _tile_options(m):
        for tn in _tile_options(n):
            vmem = 2 * 2 * (tm * tk + tk * tn) + 4 * tm * tn + 2 * out_bytes * tm * tn
            if vmem <= MATMUL_VMEM and (best is None or tm * tn / (tm + tn) > best[0]):
                best = (tm * tn / (tm + tn), tm, tn)
    return best[1], best[2], tk


def _matmul(a, b, out_dtype, name, mode="nn"):
    m, k = (a.shape[1], a.shape[0]) if mode == "tn" else a.shape
    n = b.shape[0] if mode == "nt" else b.shape[1]
    tm, tn, tk = _matmul_tiles(m, n, k, jnp.dtype(out_dtype).itemsize)
    nk = k // tk
    dot = {"nn": _dot, "tn": _dot_tn, "nt": _dot_nt}[mode]

    def kern(a_ref, b_ref, o_ref, *acc):
        if nk == 1:
            o_ref[...] = dot(a_ref[...], b_ref[...]).astype(o_ref.dtype)
            return
        acc_ref, = acc
        kk = pl.program_id(2)

        @pl.when(kk == 0)
        def _():
            acc_ref[...] = jnp.zeros_like(acc_ref)

        acc_ref[...] += dot(a_ref[...], b_ref[...])

        @pl.when(kk == nk - 1)
        def _():
            o_ref[...] = acc_ref[...].astype(o_ref.dtype)

    a_spec = (pl.BlockSpec((tk, tm), lambda i, j, kk: (kk, i)) if mode == "tn"
              else pl.BlockSpec((tm, tk), lambda i, j, kk: (i, kk)))
    b_spec = (pl.BlockSpec((tn, tk), lambda i, j, kk: (j, kk)) if mode == "nt"
              else pl.BlockSpec((tk, tn), lambda i, j, kk: (kk, j)))
    return pl.pallas_call(
        kern, name=name, grid=(m // tm, n // tn, nk), in_specs=[a_spec, b_spec],
        out_specs=pl.BlockSpec((tm, tn), lambda i, j, kk: (i, j)),
        out_shape=jax.ShapeDtypeStruct((m, n), out_dtype),
        scratch_shapes=[pltpu.VMEM((tm, tn), F32)] if nk > 1 else [],
        compiler_params=pltpu.CompilerParams(
            dimension_semantics=("parallel", "parallel", "arbitrary"), vmem_limit_bytes=VMEM_LIMIT),
    )(a.astype(MXU_DTYPE), b.astype(MXU_DTYPE))


def _rowwise(body, name, rows, tile, row_ins, full_ins, row_outs, acc_outs=(), scratch=(),
             reverse=False, sequential=False):
    n = rows // tile

    def ridx(i):
        return (n - 1 - i) if reverse else i

    in_specs, args = [], []
    for arr, width, cb in row_ins:
        in_specs.append(pl.BlockSpec((tile, width), lambda i, cb=cb: (ridx(i), cb)))
        args.append(arr)
    for arr in full_ins:
        in_specs.append(pl.BlockSpec(arr.shape, lambda i, nd=arr.ndim: (0,) * nd))
        args.append(arr)
    out_shape = [jax.ShapeDtypeStruct((rows, w), dt) for w, dt in row_outs]
    out_shape += [jax.ShapeDtypeStruct(s, dt) for s, dt in acc_outs]
    out_specs = [pl.BlockSpec((tile, w), lambda i: (ridx(i), 0)) for w, dt in row_outs]
    out_specs += [pl.BlockSpec(s, lambda i, nd=len(s): (0,) * nd) for s, dt in acc_outs]

    def kern(*refs):
        body(pl.program_id(0), *refs)

    sem = "arbitrary" if (acc_outs or sequential) else "parallel"
    return pl.pallas_call(
        kern, name=name, grid=(n,), in_specs=in_specs, out_specs=out_specs, out_shape=out_shape,
        scratch_shapes=list(scratch),
        compiler_params=pltpu.CompilerParams(dimension_semantics=(sem,), vmem_limit_bytes=VMEM_LIMIT),
    )(*args)


def _ln_stats(u):
    mu = jnp.mean(u, axis=-1, keepdims=True)
    xc = u - mu
    var = jnp.mean(xc * xc, axis=-1, keepdims=True)
    return xc, lax.rsqrt(var + LN_EPS)


def _ln_fwd(a, b, g, beta, name):
    rows, d = a.shape
    has_b = b is not None

    def body(i, *refs):
        if has_b:
            a_ref, b_ref, g_ref, be_ref, h_ref, hb_ref = refs
            u = ALPHA * a_ref[...] + b_ref[...]
        else:
            a_ref, g_ref, be_ref, h_ref, hb_ref = refs
            u = a_ref[...]
        xc, rstd = _ln_stats(u)
        y = xc * rstd * g_ref[...] + be_ref[...]
        h_ref[...] = y
        hb_ref[...] = y.astype(hb_ref.dtype)

    row_ins = [(a, d, 0)] + ([(b, d, 0)] if has_b else [])
    return _rowwise(body, name, rows, min(ROW_TILE, rows), row_ins,
                    [g.reshape(1, d), beta.reshape(1, d)], [(d, F32), (d, MXU_DTYPE)])


def _ln_bwd(a, b, g, dy1, dy2, c1, name):
    rows, d = a.shape
    has_b = b is not None
    has_2 = dy2 is not None

    def body(i, *refs):
        refs = list(refs)
        a_ref = refs.pop(0)
        u = a_ref[...]
        if has_b:
            u = ALPHA * u + refs.pop(0)[...]
        dy = c1 * refs.pop(0)[...]
        if has_2:
            dy = dy + refs.pop(0)[...]
        g_ref, du_ref, dub_ref, dg_ref, db_ref = refs

        @pl.when(i == 0)
        def _():
            dg_ref[...] = jnp.zeros_like(dg_ref)
            db_ref[...] = jnp.zeros_like(db_ref)

        xc, rstd = _ln_stats(u)
        xhat = xc * rstd
        dxh = dy * g_ref[...]
        m1 = jnp.mean(dxh, axis=-1, keepdims=True)
        m2 = jnp.mean(dxh * xhat, axis=-1, keepdims=True)
        du = rstd * (dxh - m1 - xhat * m2)
        du_ref[...] = du
        dub_ref[...] = du.astype(dub_ref.dtype)
        dg_ref[...] += jnp.sum(dy * xhat, axis=0, keepdims=True)
        db_ref[...] += jnp.sum(dy, axis=0, keepdims=True)

    row_ins = [(a, d, 0)] + ([(b, d, 0)] if has_b else []) + [(dy1, d, 0)] + ([(dy2, d, 0)] if has_2 else [])
    return _rowwise(body, name, rows, min(ROW_TILE, rows), row_ins, [g.reshape(1, d)],
                    [(d, F32), (d, MXU_DTYPE)], [((1, d), F32), ((1, d), F32)])


def _loss_grad(h, target):
    rows, d = h.shape

    def body(i, h_ref, t_ref, dh_ref, acc_ref):
        @pl.when(i == 0)
        def _():
            acc_ref[...] = jnp.zeros_like(acc_ref)

        e = h_ref[...] - t_ref[...]
        dh_ref[...] = e * (1.0 / d)
        acc_ref[...] += jnp.sum(e * e, axis=0, keepdims=True)

    return _rowwise(body, "loss_grad", rows, ROW_TILE, [(h, d, 0), (target, d, 0)], [],
                    [(d, F32)], [((1, d), F32)])


def _gate_fwd(groups, pb):
    rows = pb.shape[0]
    w = GROUP_W * len(groups)

    def body(i, *refs):
        g = refs[4][...]
        mixed = jnp.concatenate([r[...] for r in refs[:4]], axis=1)
        refs[5][...] = (mixed * (g / (1.0 + jnp.exp(-g)))).astype(refs[5].dtype)

    return _rowwise(body, "gate_fwd", rows, ROW_TILE, [(o, GROUP_W, 0) for o in groups] + [(pb, w, 0)], [],
                    [(w, MXU_DTYPE)])[0]


def _gate_bwd(dgated, groups, pb):
    rows = pb.shape[0]
    w = GROUP_W * len(groups)

    def body(i, *refs):
        dg = refs[0][...]
        mixed = jnp.concatenate([r[...] for r in refs[1:5]], axis=1)
        g = refs[5][...]
        dm_ref, dgate_ref = refs[6], refs[7]
        sig = 1.0 / (1.0 + jnp.exp(-g))
        dm_ref[...] = dg * (g * sig)
        dgate_ref[...] = (dg * mixed * (sig * (1.0 + g * (1.0 - sig)))).astype(dgate_ref.dtype)

    return _rowwise(body, "gate_bwd", rows, ROW_TILE,
                    [(dgated, w, 0)] + [(o, GROUP_W, 0) for o in groups] + [(pb, w, 0)], [],
                    [(w, F32), (w, MXU_DTYPE)])


def _tri(n, kind):
    r = np.arange(n)[:, None]
    c = np.arange(n)[None, :]
    m = {"lower_incl": r >= c, "upper_incl": r <= c, "row_gt_col": r > c, "row_lt_col": r < c}[kind]
    return jnp.asarray(m.astype(np.float32), dtype=MXU_DTYPE)


def _forget_fwd(pb, bias_row):
    rows = pb.shape[0]
    tile = min(ROW_TILE, rows)

    def body(i, x_ref, b_ref, l_ref, o_ref, carry_ref):
        @pl.when(i == 0)
        def _():
            carry_ref[...] = jnp.zeros_like(carry_ref)

        xx = x_ref[...] + b_ref[...]
        lane = lax.broadcasted_iota(jnp.int32, xx.shape, 1)
        lf = jnp.where(lane < N_HEADS, -_softplus(-xx), 0.0)
        o_ref[...] = _dot_exact_l(l_ref[...], lf) + carry_ref[...]
        carry_ref[...] += jnp.sum(lf, axis=0, keepdims=True)

    return _rowwise(body, "forget_fwd", rows, tile, [(pb, 128, PB_MISC_BLK)],
                    [bias_row, _tri(tile, "lower_incl")], [(128, F32)],
                    scratch=[pltpu.VMEM((1, 128), F32)], sequential=True)[0]


def _forget_bwd(pb, bias_row, dfc):
    rows = pb.shape[0]
    tile = min(ROW_TILE, rows)

    def body(i, x_ref, df_ref, b_ref, u_ref, o_ref, db_ref, carry_ref):
        @pl.when(i == 0)
        def _():
            carry_ref[...] = jnp.zeros_like(carry_ref)
            db_ref[...] = jnp.zeros_like(db_ref)

        df = df_ref[...]
        sfx = _dot_exact_l(u_ref[...], df) + carry_ref[...]
        carry_ref[...] += jnp.sum(df, axis=0, keepdims=True)
        xx = x_ref[...] + b_ref[...]
        lane = lax.broadcasted_iota(jnp.int32, xx.shape, 1)
        dl = jnp.where(lane < N_HEADS, sfx / (1.0 + jnp.exp(xx)), 0.0)
        o_ref[...] = dl
        db_ref[...] += jnp.sum(dl, axis=0, keepdims=True)

    return _rowwise(body, "forget_bwd", rows, tile,
                    [(pb, 128, PB_MISC_BLK), (dfc, 128, 0)],
                    [bias_row, _tri(tile, "upper_incl")], [(128, F32)], [((1, 128), F32)],
                    scratch=[pltpu.VMEM((1, 128), F32)], reverse=True, sequential=True)


def _rope_tables(s):
    half = MLA_ROPE // 2
    inv_freq = ROPE_THETA ** (-jnp.arange(half, dtype=F32) / half)
    ang = jnp.arange(s).astype(F32)[:, None] * inv_freq[None, :]
    cos2 = jnp.tile(jnp.cos(ang), (1, 2))
    sin2 = jnp.tile(jnp.sin(ang), (1, 2))
    cx = jnp.tile(cos2, (1, N_HEADS))
    sx = jnp.tile(sin2, (1, N_HEADS))
    cq = jnp.concatenate([jnp.ones((s, GROUP_W), F32), cx], axis=1)
    sq = jnp.concatenate([jnp.zeros((s, GROUP_W), F32), sx], axis=1)
    pad = ((0, 0), (MISC_KROT, 128 - MISC_KROT - MLA_ROPE))
    ck = jnp.pad(cos2, pad)
    sk = jnp.pad(sin2, pad)
    return dict(cq=cq, sq=sq, ck=ck, sk=sk, cx=cx, sx=sx)


def _rot_matrix(width, bases):
    half = MLA_ROPE // 2
    p = np.zeros((width, width), np.float32)
    for b in bases:
        for i in range(half):
            p[b + half + i, b + i] = -1.0
            p[b + i, b + half + i] = 1.0
    return p


def _rope_matrices():
    pq = _rot_matrix(MLA_Q_COLS, [GROUP_W + h * MLA_ROPE for h in range(N_HEADS)])
    pk = _rot_matrix(128, [MISC_KROT])
    p4 = _rot_matrix(128, [h * MLA_ROPE for h in range(N_HEADS)])
    a = np.zeros((128, 128), np.float32)
    for h in range(N_HEADS):
        for r in range(MLA_ROPE):
            a[h * MLA_ROPE + r, MISC_KROT + r] = 1.0
    cast = lambda m: jnp.asarray(m, dtype=MXU_DTYPE)
    return dict(pq=cast(pq), pqt=cast(pq.T), pk=cast(pk), spread=cast(a.T), xa=cast(a), xb=cast(p4.T @ a))


def _rms(c, g):
    r = lax.rsqrt(jnp.mean(c * c, axis=-1, keepdims=True) + RMS_EPS)
    return c * r * g


def _mla_q_fwd(pb, g, w_up, tabs, mats):
    rows = pb.shape[0]

    def body(i, c_ref, cos_ref, sin_ref, g_ref, w_ref, p_ref, q_ref, cn_ref):
        cn = _rms(c_ref[...], g_ref[...]).astype(cn_ref.dtype)
        cn_ref[...] = cn
        q = _dot(cn, w_ref[...])
        q_ref[...] = (q * cos_ref[...] + _dot_exact_r(q, p_ref[...]) * sin_ref[...]).astype(q_ref.dtype)

    return _rowwise(body, "mla_q_fwd", rows, ROW_TILE,
                    [(pb, MLA_Q_RANK, PB_CQ_BLK), (tabs["cq"], MLA_Q_COLS, 0), (tabs["sq"], MLA_Q_COLS, 0)],
                    [g.reshape(1, MLA_Q_RANK), w_up.astype(MXU_DTYPE), mats["pq"]],
                    [(MLA_Q_COLS, MXU_DTYPE), (MLA_Q_RANK, MXU_DTYPE)])


def _mla_kv_fwd(pb, g, w_up, tabs, mats):
    rows = pb.shape[0]

    def body(i, c_ref, x_ref, cos_ref, sin_ref, g_ref, w_ref, p_ref, sp_ref, kv_ref, cn_ref, kr_ref):
        cn = _rms(c_ref[...], g_ref[...]).astype(cn_ref.dtype)
        cn_ref[...] = cn
        kv_ref[...] = _dot(cn, w_ref[...]).astype(kv_ref.dtype)
        xx = x_ref[...]
        kr = xx * cos_ref[...] + _dot_exact_r(xx, p_ref[...]) * sin_ref[...]
        kr_ref[...] = _dot_exact_r(kr, sp_ref[...]).astype(kr_ref.dtype)

    return _rowwise(body, "mla_kv_fwd", rows, ROW_TILE,
                    [(pb, MLA_KV_RANK, PB_CKV_BLK), (pb, 128, PB_MISC_BLK), (tabs["ck"], 128, 0), (tabs["sk"], 128, 0)],
                    [g.reshape(1, MLA_KV_RANK), w_up.astype(MXU_DTYPE), mats["pk"], mats["spread"]],
                    [(MLA_KV_COLS, MXU_DTYPE), (MLA_KV_RANK, MXU_DTYPE), (128, MXU_DTYPE)])


def _rope_q_bwd(dq_full, tabs, mats):
    rows, nq = dq_full.shape

    def body(i, d_ref, cos_ref, sin_ref, pt_ref, o_ref):
        d = d_ref[...]
        o_ref[...] = (d * cos_ref[...] + _dot_exact_r(d * sin_ref[...], pt_ref[...])).astype(o_ref.dtype)

    return _rowwise(body, "rope_q_bwd", rows, ROW_TILE,
                    [(dq_full, nq, 0), (tabs["cq"], nq, 0), (tabs["sq"], nq, 0)], [mats["pqt"]],
                    [(nq, MXU_DTYPE)])[0]


def _rope_k_bwd(dkr, tabs, mats):
    rows = dkr.shape[0]

    def body(i, d_ref, cos_ref, sin_ref, a_ref, b_ref, o_ref):
        d = d_ref[...]
        o_ref[...] = _dot_exact_r(d * cos_ref[...], a_ref[...]) + _dot_exact_r(d * sin_ref[...], b_ref[...])

    return _rowwise(body, "rope_k_bwd", rows, ROW_TILE,
                    [(dkr, 128, 0), (tabs["cx"], 128, 0), (tabs["sx"], 128, 0)], [mats["xa"], mats["xb"]],
                    [(128, F32)])[0]


def _rms_bwd(pb, width, col_blk, g, dy, name):
    rows = pb.shape[0]

    def body(i, c_ref, dy_ref, g_ref, dc_ref, dg_ref):
        @pl.when(i == 0)
        def _():
            dg_ref[...] = jnp.zeros_like(dg_ref)

        c = c_ref[...]
        dy = dy_ref[...]
        r = lax.rsqrt(jnp.mean(c * c, axis=-1, keepdims=True) + RMS_EPS)
        dyg = dy * g_ref[...]
        dc = r * dyg - c * (r * r * r) * jnp.mean(c * dyg, axis=-1, keepdims=True)
        dc_ref[...] = dc.astype(dc_ref.dtype)
        dg_ref[...] += jnp.sum(dy * c * r, axis=0, keepdims=True)

    return _rowwise(body, name, rows, ROW_TILE, [(pb, width, col_blk), (dy, width, 0)], [g.reshape(1, width)],
                    [(width, MXU_DTYPE)], [((1, width), F32)])


def _att_params(parallel):
    return pltpu.CompilerParams(dimension_semantics=("parallel" if parallel else "arbitrary",),
                                vmem_limit_bytes=VMEM_LIMIT)


def _blk_off(j, t):
    return j * t if isinstance(j, int) else pl.multiple_of(j * t, t)


def _causal_mask(t, strict):
    r = lax.broadcasted_iota(jnp.int32, (t, t), 0)
    c = lax.broadcasted_iota(jnp.int32, (t, t), 1)
    return (c < r) if strict else (c <= r)


def _lane_mask(kind, head, rows):
    lane = lax.broadcasted_iota(jnp.int32, (rows, PAIR), 1)
    if kind == "pair":
        return (lane < HEAD_DIM) if head % 2 == 0 else (lane >= HEAD_DIM)
    return (lane >= MLA_ROPE * head) & (lane < MLA_ROPE * (head + 1))


def _row_spec(t, cb, width=PAIR):
    return pl.BlockSpec((t, width), lambda i, cb=cb: (i, cb))


def _whole_spec(rows, cb, width=PAIR):
    return pl.BlockSpec((rows, width), lambda i, cb=cb: (0, cb), pipeline_mode=pl.Buffered(1))


def _is_pow2(x):
    return float(np.frexp(x)[0]) == 0.5


def _masked_heads(blocks, kinds, pair, dtype, scale=None):
    out = []
    for e in range(2):
        head = 2 * pair + e
        parts = [jnp.where(_lane_mask(k, head, b.shape[0]), b.astype(F32) * (1.0 if scale is None else scale),
                           0.0).astype(dtype)
                 for b, k in zip(blocks, kinds)]
        out.append(parts[0] if len(parts) == 1 else jnp.concatenate(parts, axis=1))
    return out


def _logit_reach(qh, kmax2, head):
    q32 = qh.astype(F32)
    return jnp.sqrt(jnp.sum(q32 * q32, axis=1, keepdims=True) * _col(kmax2, head)) * REACH_SLACK


def _forget_top(ft_ref, head, off):
    return jnp.max(-ft_ref[head:head + 1, pl.ds(off, PAIR)])


def _col(block, idx):
    lane = lax.broadcasted_iota(jnp.int32, block.shape, 1)
    return jnp.sum(jnp.where(lane == idx, block, 0.0), axis=1, keepdims=True)


def _scatter_cols(cols, t):
    lane = lax.broadcasted_iota(jnp.int32, (t, PAIR), 1)
    out = jnp.zeros((t, PAIR), F32)
    for idx, c in cols.items():
        out = out + jnp.where(lane == idx, c, 0.0)
    return out


def _take_heads(per_head, pair):
    return jnp.where(_lane_mask("pair", 0, per_head[0].shape[0]), per_head[0], per_head[1])


class _Parts:
    def __init__(self, q_parts, k_parts, v_parts, tq, sk):
        self.kinds = [[kind for _, _, kind in q_parts[p]] for p in range(2)]
        self.nparts = len(q_parts[0])
        self.q_specs = [_row_spec(tq, cb) for p in range(2) for _, cb, _ in q_parts[p]]
        self.q_args = [a for p in range(2) for a, _, _ in q_parts[p]]
        self.k_specs = [_whole_spec(sk, cb) for p in range(2) for _, cb, _ in k_parts[p]]
        self.k_args = [a for p in range(2) for a, _, _ in k_parts[p]]
        self.v_specs = [_whole_spec(sk, cb) for _, cb in v_parts]
        self.v_args = [a for a, _ in v_parts]
        self.width = PAIR * self.nparts

    def split(self, refs):
        n = self.nparts
        refs = list(refs)
        q = [refs[p * n:(p + 1) * n] for p in range(2)]
        k = [refs[2 * n + p * n:2 * n + (p + 1) * n] for p in range(2)]
        v = refs[4 * n:4 * n + 2]
        return q, k, v, refs[4 * n + 2:]

    def k_block(self, k_refs, off, t):
        blks = [r[pl.ds(off, t), :] for r in k_refs]
        return blks[0] if len(blks) == 1 else jnp.concatenate(blks, axis=1)


def _key_norm_max(src, k_blk):
    rows = src.shape[0]

    def body(i, k0_ref, k1_ref, o_ref):
        @pl.when(i == 0)
        def _():
            o_ref[...] = jnp.zeros_like(o_ref)

        cols = {}
        for p, ref in enumerate((k0_ref, k1_ref)):
            k32 = ref[...].astype(F32)
            for e in range(2):
                sq = jnp.sum(jnp.where(_lane_mask("pair", e, k32.shape[0]), k32 * k32, 0.0), axis=1, keepdims=True)
                cols[2 * p + e] = jnp.max(sq, axis=0, keepdims=True)
        lane = lax.broadcasted_iota(jnp.int32, (1, PAIR), 1)
        o_ref[...] = jnp.maximum(o_ref[...], sum(jnp.where(lane == h, c, 0.0) for h, c in cols.items()))

    return _rowwise(body, "key_norm_max", rows, ROW_TILE, [(src, PAIR, k_blk), (src, PAIR, k_blk + 1)], [], [],
                    [((1, PAIR), F32)])[0]


def _softmax_fwd(q_parts, k_parts, v_parts, sq, sk, scale, causal, bias, name):
    tq = min(SOFTMAX_TILE, sq)
    tk = tq if causal else min(SOFTMAX_TILE, sk)
    nkv = sk // tk
    pp = _Parts(q_parts, k_parts, v_parts, tq, sk)

    def kern(*refs):
        q_refs, k_refs, v_refs, rest = pp.split(refs)
        if bias is not None:
            fc_ref, ft_ref, kmax_ref, o_ref, lse_ref = rest
            fcb = fc_ref[...]
        else:
            o_ref, lse_ref = rest
        i = pl.program_id(0)
        fold = _is_pow2(scale)
        head_on = [jnp.where(_lane_mask("pair", e, tk), 1.0, 0.0).astype(MXU_DTYPE) for e in range(2)]
        head_off = [jnp.where(_lane_mask("pair", e, tk), 0.0, 1.0).astype(MXU_DTYPE) for e in range(2)]
        lse_cols = {}
        for p in range(2):
            qm = _masked_heads([r[...] for r in q_refs[p]], pp.kinds[p], p, MXU_DTYPE, scale if fold else None)
            if bias is not None:
                reach = [_logit_reach(qm[e], kmax_ref[...], 2 * p + e) for e in range(2)]

            def block(j, carry, masked, p=p, qm=qm):
                off = _blk_off(j, tk)
                kb = pp.k_block(k_refs[p], off, tk)
                vb = v_refs[p][pl.ds(off, tk), :]
                out = []
                for e in range(2):
                    h = 2 * p + e
                    m, acc = carry[e]
                    s = _dot_nt(qm[e], kb)
                    if not fold:
                        s = s * scale
                    if bias is not None:
                        s = s - ft_ref[h:h + 1, pl.ds(off, tk)]
                    if masked:
                        s = jnp.where(_causal_mask(tq, False), s, NEG)
                    m_new = jnp.maximum(m, jnp.max(s, axis=1, keepdims=True))
                    pr = jnp.exp(s - m_new).astype(MXU_DTYPE)
                    out.append((m_new, jnp.exp(m - m_new) * acc + _dot(pr, vb * head_on[e] + head_off[e])))
                return tuple(out)

            carry = tuple((jnp.full((tq, 1), NEG, F32), jnp.zeros((tq, PAIR), F32)) for _ in range(2))
            if causal and bias is not None:
                def alive(c, j, p=p, reach=reach):
                    top = [_forget_top(ft_ref, 2 * p + e, _blk_off(j, tk)) for e in range(2)]
                    return functools.reduce(jnp.maximum, [jnp.max(reach[e] + top[e] - c[e][0]) for e in range(2)])

                def step(state, block=block, alive=alive):
                    n, _, c = state
                    c = block(i - 1 - n, c, False)
                    return n + 1, alive(c, i - 1 - n), c

                carry = block(i, carry, True)
                _, _, carry = lax.while_loop(lambda st: jnp.logical_and(st[0] < i, st[1] > DEAD_LOGIT), step,
                                             (jnp.int32(0), alive(carry, i), carry))
            elif causal:
                carry = lax.fori_loop(0, i, lambda j, c, block=block: block(j, c, False), carry)
                carry = block(i, carry, True)
            else:
                for j in range(nkv):
                    carry = block(j, carry, False)
            outs = []
            for e in range(2):
                m, acc = carry[e]
                l = _col(acc, HEAD_DIM * (1 - e))
                outs.append(acc / l)
                lse_cols[2 * p + e] = m + jnp.log(l) + (_col(fcb, 2 * p + e) if bias is not None else 0.0)
            o_ref[:, p * PAIR:(p + 1) * PAIR] = _take_heads(outs, p)
        lse_ref[...] = _scatter_cols(lse_cols, tq)

    in_specs = pp.q_specs + pp.k_specs + pp.v_specs
    args = pp.q_args + pp.k_args + pp.v_args
    if bias is not None:
        in_specs += [_row_spec(tq, 0), pl.BlockSpec((8, sk), lambda i: (0, 0), pipeline_mode=pl.Buffered(1)),
                     pl.BlockSpec((1, PAIR), lambda i: (0, 0))]
        args += list(bias)
    return pl.pallas_call(
        kern, name=name, grid=(sq // tq,), in_specs=in_specs,
        out_specs=[_row_spec(tq, 0, GROUP_W), _row_spec(tq, 0)],
        out_shape=[jax.ShapeDtypeStruct((sq, GROUP_W), F32), jax.ShapeDtypeStruct((sq, PAIR), F32)],
        compiler_params=_att_params(True),
    )(*args)


def _softmax_bwd(q_parts, k_parts, v_parts, o, lse, do, do_blk, sq, sk, scale, causal, bias, name):
    tq = min(SOFTMAX_TILE, sq)
    tk = tq if causal else min(SOFTMAX_TILE, sk)
    nkv = sk // tk
    pp = _Parts(q_parts, k_parts, v_parts, tq, sk)
    quad = pp.nparts == 2
    wq = GROUP_W + (PAIR if quad else 0)

    def kern(*refs):
        q_refs, k_refs, v_refs, rest = pp.split(refs)
        if bias is not None:
            o_ref, lse_ref, do_ref, fc_ref, ft_ref, kmax_ref, dq_ref, dk_ref, dv_ref, dfq_ref, dfk_ref = rest
            fcb = fc_ref[...]
        else:
            o_ref, lse_ref, do_ref, dq_ref, dk_ref, dv_ref = rest
        i = pl.program_id(0)
        fold = _is_pow2(scale)

        @pl.when(i == 0)
        def _():
            dk_ref[...] = jnp.zeros_like(dk_ref)
            dv_ref[...] = jnp.zeros_like(dv_ref)
            if bias is not None:
                dfk_ref[...] = jnp.zeros_like(dfk_ref)

        lse_b = lse_ref[...]
        qm, dom, delta, lse_h = [], [], [], []
        for p in range(2):
            qm += _masked_heads([r[...] for r in q_refs[p]], pp.kinds[p], p, MXU_DTYPE, scale if fold else None)
            do_p = do_ref[:, p * PAIR:(p + 1) * PAIR]
            dom += _masked_heads([do_p], ["pair"], p, MXU_DTYPE)
            prod = do_p * o_ref[:, p * PAIR:(p + 1) * PAIR]
            for e in range(2):
                h = 2 * p + e
                delta.append(jnp.sum(jnp.where(_lane_mask("pair", h, tq), prod, 0.0), axis=1, keepdims=True))
                lse_h.append(_col(lse_b, h) - (_col(fcb, h) if bias is not None else 0.0))

        def block(j, carry, masked):
            off = _blk_off(j, tk)
            out = []
            for p in range(2):
                kb = pp.k_block(k_refs[p], off, tk)
                vb = v_refs[p][pl.ds(off, tk), :]
                dk_acc = jnp.zeros((tk, pp.width), F32)
                dv_acc = jnp.zeros((tk, PAIR), F32)
                for e in range(2):
                    h = 2 * p + e
                    dq, dfq = carry[h]
                    s = _dot_nt(qm[h], kb)
                    if not fold:
                        s = s * scale
                    if bias is not None:
                        s = s - ft_ref[h:h + 1, pl.ds(off, tk)]
                    if masked:
                        s = jnp.where(_causal_mask(tq, False), s, NEG)
                    pr = jnp.exp(s - lse_h[h])
                    ds = pr * (_dot_nt(dom[h], vb) - delta[h])
                    dsb = (ds if fold else ds * scale).astype(MXU_DTYPE)
                    dv_acc = dv_acc + _dot_tn(pr.astype(MXU_DTYPE), dom[h])
                    dk_acc = dk_acc + _dot_tn(dsb, qm[h])
                    dq = dq + _dot(dsb, kb)
                    if bias is not None:
                        dfq = dfq + jnp.sum(ds, axis=1, keepdims=True)
                        dfk_ref[h:h + 1, pl.ds(off, tk)] -= jnp.sum(ds, axis=0, keepdims=True)
                    out.append((dq, dfq))
                dv_ref[pl.ds(off, tk), p * PAIR:(p + 1) * PAIR] += dv_acc
                dk_ref[pl.ds(off, tk), p * PAIR:(p + 1) * PAIR] += dk_acc[:, :PAIR]
                if quad:
                    dk_ref[pl.ds(off, tk), GROUP_W:] += dk_acc[:, PAIR:]
            return tuple(out)

        carry = tuple((jnp.zeros((tq, pp.width), F32), jnp.zeros((tq, 1), F32)) for _ in range(N_HEADS))
        if causal and bias is not None:
            reach = [_logit_reach(qm[h], kmax_ref[...], h) - lse_h[h] for h in range(N_HEADS)]

            def alive(j):
                return functools.reduce(jnp.maximum, [jnp.max(reach[h] + _forget_top(ft_ref, h, _blk_off(j, tk)))
                                                      for h in range(N_HEADS)])

            def step(state):
                n, _, c = state
                return n + 1, alive(i - 1 - n), block(i - 1 - n, c, False)

            carry = block(i, carry, True)
            _, _, carry = lax.while_loop(lambda st: jnp.logical_and(st[0] < i, st[1] > DEAD_LOGIT), step,
                                         (jnp.int32(0), alive(i), carry))
        elif causal:
            carry = lax.fori_loop(0, i, lambda j, c: block(j, c, False), carry)
            carry = block(i, carry, True)
        else:
            for j in range(nkv):
                carry = block(j, carry, False)
        dqs = [c[0] * scale if fold else c[0] for c in carry]
        for p in range(2):
            dq_ref[:, p * PAIR:(p + 1) * PAIR] = _take_heads([dqs[2 * p + e][:, :PAIR] for e in range(2)], p)
        if quad:
            dq_ref[:, GROUP_W:] = sum(jnp.where(_lane_mask("quad", h, tq), dqs[h][:, PAIR:], 0.0)
                                      for h in range(N_HEADS))
        if bias is not None:
            dfq_ref[...] = _scatter_cols({h: carry[h][1] for h in range(N_HEADS)}, tq)

    acc_spec = lambda rows, width: pl.BlockSpec((rows, width), lambda i: (0, 0), pipeline_mode=pl.Buffered(1))
    in_specs = pp.q_specs + pp.k_specs + pp.v_specs + [_row_spec(tq, 0, GROUP_W), _row_spec(tq, 0),
                                                       _row_spec(tq, do_blk, GROUP_W)]
    args = pp.q_args + pp.k_args + pp.v_args + [o, lse, do]
    out_specs = [_row_spec(tq, 0, wq), acc_spec(sk, wq), acc_spec(sk, GROUP_W)]
    out_shape = [jax.ShapeDtypeStruct((sq, wq), F32), jax.ShapeDtypeStruct((sk, wq), F32),
                 jax.ShapeDtypeStruct((sk, GROUP_W), F32)]
    if bias is not None:
        in_specs += [_row_spec(tq, 0), pl.BlockSpec((8, sk), lambda i: (0, 0), pipeline_mode=pl.Buffered(1)),
                     pl.BlockSpec((1, PAIR), lambda i: (0, 0))]
        args += list(bias)
        out_specs += [_row_spec(tq, 0), acc_spec(8, sk)]
        out_shape += [jax.ShapeDtypeStruct((sq, PAIR), F32), jax.ShapeDtypeStruct((8, sk), F32)]
    return pl.pallas_call(
        kern, name=name, grid=(sq // tq,), in_specs=in_specs, out_specs=out_specs, out_shape=out_shape,
        compiler_params=_att_params(False),
    )(*args)


def _sb_logs(qh, kb, valid):
    z = _dot_nt(qh, kb)
    sp = _softplus(z)
    lk = -sp
    if valid is not None:
        lk = jnp.where(valid, lk, 0.0)
    return lk, z - sp


def _sb_valid(d, tq, tk):
    r = lax.broadcasted_iota(jnp.int32, (tq, tk), 0)
    c = lax.broadcasted_iota(jnp.int32, (tq, tk), 1)
    return c + d * tk < r


def _tri_sums(xs, tri):
    t = xs[0].shape[0]
    pieces = [_split2(x) for x in xs]
    hi = _dot(jnp.concatenate([pc[0] for pc in pieces], axis=0), tri)
    lo = _dot(jnp.concatenate([pc[1] for pc in pieces], axis=0), tri)
    return [hi[n * t:(n + 1) * t] + lo[n * t:(n + 1) * t] for n in range(len(xs))]


def _sb_fwd(src, q_blk, k_blk, v_blk, s, scale, name):
    assert _is_pow2(scale) and s // ATT_TILE <= SB_SLOT
    tq, t = min(SB_QUERY_TILE, s), min(ATT_TILE, s)
    band = tq // t
    pair = lambda blk: [[(src, blk + p, "pair")] for p in range(2)]
    pp = _Parts(pair(q_blk), pair(k_blk), [(src, v_blk + p) for p in range(2)], tq, s)

    def kern(*refs):
        q_refs, k_refs, v_refs, (tri_ref, o_ref, rm_ref, cnt_ref) = pp.split(refs)
        i = pl.program_id(0)
        tri = tri_ref[...]
        lane = lax.broadcasted_iota(jnp.int32, (tq, PAIR), 1)
        qm = []
        for p in range(2):
            qm += _masked_heads([q_refs[p][0][...]], ["pair"], p, MXU_DTYPE, scale)

        def block(j, carry, valid):
            accs, rights, rm = carry
            off = _blk_off(j, t)
            kbs = [k_refs[p][0][pl.ds(off, t), :] for p in range(2)]
            vbs = [v_refs[p][pl.ds(off, t), :] for p in range(2)]
            logs = [_sb_logs(qm[h], kbs[h // 2], valid) for h in range(N_HEADS)]
            tails = _tri_sums([lg[0] for lg in logs], tri)
            new_acc, new_right = [], []
            for h in range(N_HEADS):
                lk, ls = logs[h]
                w = jnp.exp(ls + tails[h] + rights[h])
                if valid is not None:
                    w = jnp.where(valid, w, 0.0)
                new_acc.append(accs[h] + _dot(w.astype(MXU_DTYPE), vbs[h // 2]))
                rm = rm + jnp.where(lane == SB_SLOT * h + j, rights[h], 0.0)
                new_right.append(rights[h] + jnp.sum(lk, axis=1, keepdims=True))
            return tuple(new_acc), tuple(new_right), rm

        carry = (tuple(jnp.zeros((tq, PAIR), F32) for _ in range(N_HEADS)),
                 tuple(jnp.zeros((tq, 1), F32) for _ in range(N_HEADS)), jnp.zeros((tq, PAIR), F32))
        for d in reversed(range(band)):
            carry = block(band * i + d, carry, _sb_valid(d, tq, t))

        def alive(c):
            return functools.reduce(jnp.maximum, [jnp.max(r) for r in c[1]])

        def step(state):
            n, _, c = state
            c = block(band * i - 1 - n, c, None)
            return n + 1, alive(c), c

        n_done, _, carry = lax.while_loop(lambda st: jnp.logical_and(st[0] < band * i, st[1] > EXP_UNDERFLOW),
                                          step, (jnp.int32(0), alive(carry), carry))
        cnt_ref[i] = n_done
        for p in range(2):
            o_ref[:, p * PAIR:(p + 1) * PAIR] = _take_heads([carry[0][2 * p + e] for e in range(2)], p)
        rm_ref[...] = carry[2]

    return pl.pallas_call(
        kern, name=name, grid=(s // tq,),
        in_specs=pp.q_specs + pp.k_specs + pp.v_specs + [pl.BlockSpec((t, t), lambda i: (0, 0))],
        out_specs=[_row_spec(tq, 0, GROUP_W), _row_spec(tq, 0), pl.BlockSpec(memory_space=pltpu.SMEM)],
        out_shape=[jax.ShapeDtypeStruct((s, GROUP_W), F32), jax.ShapeDtypeStruct((s, PAIR), F32),
                   jax.ShapeDtypeStruct((s // tq,), jnp.int32)],
        compiler_params=_att_params(False),
    )(*(pp.q_args + pp.k_args + pp.v_args + [_tri(t, "row_gt_col")]))


def _sb_bwd(src, q_blk, k_blk, v_blk, do, do_blk, rm, visited, s, scale, name):
    assert _is_pow2(scale)
    tq, t = min(SB_QUERY_TILE, s), min(ATT_TILE, s)
    band = tq // t
    pair = lambda blk: [[(src, blk + p, "pair")] for p in range(2)]
    pp = _Parts(pair(q_blk), pair(k_blk), [(src, v_blk + p) for p in range(2)], tq, s)

    def kern(*refs):
        q_refs, k_refs, v_refs, (do_ref, rm_ref, tri_ref, pre_ref, cnt_ref, dq_ref, dk_ref, dv_ref) = pp.split(refs)
        i = pl.program_id(0)

        @pl.when(i == 0)
        def _():
            dk_ref[...] = jnp.zeros_like(dk_ref)
            dv_ref[...] = jnp.zeros_like(dv_ref)

        rmb = rm_ref[...]
        tri = tri_ref[...]
        pre = pre_ref[...]
        qm, dom = [], []
        for p in range(2):
            qm += _masked_heads([q_refs[p][0][...]], ["pair"], p, MXU_DTYPE, scale)
            dom += _masked_heads([do_ref[:, p * PAIR:(p + 1) * PAIR]], ["pair"], p, MXU_DTYPE)

        def block(j, carry, valid):
            dqs, lefts = carry
            off = _blk_off(j, t)
            kbs = [k_refs[p][0][pl.ds(off, t), :] for p in range(2)]
            vbs = [v_refs[p][pl.ds(off, t), :] for p in range(2)]
            logs = [_sb_logs(qm[h], kbs[h // 2], valid) for h in range(N_HEADS)]
            tails = _tri_sums([lg[0] for lg in logs], tri)
            ws, gs = [], []
            for h in range(N_HEADS):
                lk, ls = logs[h]
                w = jnp.exp(ls + tails[h] + _col(rmb, SB_SLOT * h + j))
                if valid is not None:
                    w = jnp.where(valid, w, 0.0)
                ws.append(w)
                gs.append(_dot_nt(dom[h], vbs[h // 2]) * w)
            prefix = _tri_sums(gs, pre)
            new_dq, new_left = [], []
            dk_acc = [jnp.zeros((t, PAIR), F32) for _ in range(2)]
            dv_acc = [jnp.zeros((t, PAIR), F32) for _ in range(2)]
            for h in range(N_HEADS):
                lk, ls = logs[h]
                sig = jnp.exp(ls)
                dz = gs[h] * (1.0 - sig) - sig * (prefix[h] + lefts[h])
                if valid is not None:
                    dz = jnp.where(valid, dz, 0.0)
                dzb = dz.astype(MXU_DTYPE)
                dv_acc[h // 2] = dv_acc[h // 2] + _dot_tn(ws[h].astype(MXU_DTYPE), dom[h])
                dk_acc[h // 2] = dk_acc[h // 2] + _dot_tn(dzb, qm[h])
                new_dq.append(dqs[h] + _dot(dzb, kbs[h // 2]))
                new_left.append(lefts[h] + jnp.sum(gs[h], axis=1, keepdims=True))
            for p in range(2):
                dv_ref[pl.ds(off, t), p * PAIR:(p + 1) * PAIR] += dv_acc[p]
                dk_ref[pl.ds(off, t), p * PAIR:(p + 1) * PAIR] += dk_acc[p]
            return tuple(new_dq), tuple(new_left)

        carry = (tuple(jnp.zeros((tq, PAIR), F32) for _ in range(N_HEADS)),
                 tuple(jnp.zeros((tq, 1), F32) for _ in range(N_HEADS)))
        carry = lax.fori_loop(band * i - cnt_ref[i], band * i, lambda j, c: block(j, c, None), carry)
        for d in range(band):
            carry = block(band * i + d, carry, _sb_valid(d, tq, t))
        for p in range(2):
            dq_ref[:, p * PAIR:(p + 1) * PAIR] = _take_heads([carry[0][2 * p + e] * scale for e in range(2)], p)

    mspec = pl.BlockSpec((t, t), lambda i: (0, 0))
    acc_spec = pl.BlockSpec((s, GROUP_W), lambda i: (0, 0), pipeline_mode=pl.Buffered(1))
    return pl.pallas_call(
        kern, name=name, grid=(s // tq,),
        in_specs=pp.q_specs + pp.k_specs + pp.v_specs + [_row_spec(tq, do_blk, GROUP_W), _row_spec(tq, 0), mspec, mspec,
                                                         pl.BlockSpec(memory_space=pltpu.SMEM)],
        out_specs=[_row_spec(tq, 0, GROUP_W), acc_spec, acc_spec],
        out_shape=[jax.ShapeDtypeStruct((s, GROUP_W), F32)] * 3,
        compiler_params=_att_params(False),
    )(*(pp.q_args + pp.k_args + pp.v_args + [do, rm, _tri(t, "row_gt_col"), _tri(t, "row_lt_col"), visited]))


def _split_w_in(w):
    col = lambda n: w[:, _OFF[n]:_OFF[n + 1]]
    wa = jnp.concatenate([col(0), col(1), col(2), col(4), col(5), col(6), col(10)], axis=1)
    misc = jnp.concatenate([col(3), col(9), jnp.zeros((w.shape[0], 128 - 4 - MLA_ROPE), w.dtype)], axis=1)
    wb = jnp.concatenate([col(11), col(7), col(8), misc], axis=1)
    return wa, wb


def _merge_dw_in(dwp):
    a = lambda n: dwp[:, n * GROUP_W:(n + 1) * GROUP_W]
    b0 = PA_COLS
    gate = dwp[:, b0:b0 + 1024]
    cq = dwp[:, b0 + 1024:b0 + 1280]
    ckv = dwp[:, b0 + 1280:b0 + 1408]
    flog = dwp[:, b0 + 1408:b0 + 1412]
    krot = dwp[:, b0 + 1408 + MISC_KROT:b0 + 1408 + MISC_KROT + MLA_ROPE]
    return jnp.concatenate([a(0), a(1), a(2), flog, a(3), a(4), a(5), cq, ckv, krot, a(6), gate], axis=1)


def _heads_first(w, per_head, first):
    r = w.shape[0]
    w3 = w.reshape(r, N_HEADS, per_head)
    return jnp.concatenate([w3[:, :, :first].reshape(r, -1), w3[:, :, first:].reshape(r, -1)], axis=1)


def _heads_interleaved(w, per_head, first):
    r = w.shape[0]
    a = w[:, :N_HEADS * first].reshape(r, N_HEADS, first)
    b = w[:, N_HEADS * first:].reshape(r, N_HEADS, per_head - first)
    return jnp.concatenate([a, b], axis=2).reshape(r, N_HEADS * per_head)


def _pad_rows8(a):
    return a[:, :8].T


def _local_step(x2, mem2, tgt, p):
    s = x2.shape[0]
    nm = mem2.shape[0]
    head_scale = HEAD_DIM ** -0.5
    mla_scale = (HEAD_DIM + MLA_ROPE) ** -0.5
    tabs = _rope_tables(s)
    mats = _rope_matrices()
    pairs = lambda arr, blk: [[(arr, blk + q, "pair")] for q in range(2)]
    vals = lambda arr, blk: [(arr, blk + q) for q in range(2)]

    h, hb = _ln_fwd(x2, None, p["ln_in_g"], p["ln_in_b"], "ln_in_fwd")
    _, memn_b = _ln_fwd(mem2, None, p["mem_ln_g"], p["mem_ln_b"], "ln_mem_fwd")

    saved = []
    for l in range(DEPTH):
        wa, wb = _split_w_in(p["w_in"][l])
        wp = jnp.concatenate([wa, wb], axis=1)
        wq_up = _heads_first(p["w_mla_q_up"][l], HEAD_DIM + MLA_ROPE, HEAD_DIM)
        wkv_up = _heads_first(p["w_mla_kv_up"][l], 2 * HEAD_DIM, HEAD_DIM)
        bias_row = jnp.pad(p["b_forget"][l], (0, 128 - N_HEADS)).reshape(1, 128)
        pa = _matmul(hb, wa, MXU_DTYPE, "proj_a")
        pb = _matmul(hb, wb, F32, "proj_b")

        fc = _forget_fwd(pb, bias_row)
        fbias = (fc, _pad_rows8(fc), _key_norm_max(pa, 2))
        o_fox, lse_fox = _softmax_fwd(pairs(pa, 0), pairs(pa, 2), vals(pa, 4), s, s, head_scale, True, fbias,
                                      "fox_fwd")
        o_sb, *rm_sb = _sb_fwd(pa, 6, 8, 10, s, head_scale, "sb_fwd")

        qfull, cqn = _mla_q_fwd(pb, p["mla_q_norm_g"][l], wq_up, tabs, mats)
        kv, ckvn, kr4 = _mla_kv_fwd(pb, p["mla_kv_norm_g"][l], wkv_up, tabs, mats)
        mla_q = [[(qfull, q, "pair"), (qfull, 2, "quad")] for q in range(2)]
        mla_k = [[(kv, q, "pair"), (kr4, 0, "quad")] for q in range(2)]
        o_mla, lse_mla = _softmax_fwd(mla_q, mla_k, vals(kv, 2), s, s, mla_scale, True, None, "mla_fwd")

        mkv = _matmul(memn_b, p["w_mem_kv"][l], MXU_DTYPE, "mem_kv")
        o_mem, lse_mem = _softmax_fwd(pairs(pa, 12), pairs(mkv, 0), vals(mkv, 2), s, nm, head_scale, False, None,
                                      "mem_fwd")

        groups = (o_fox, o_sb, o_mla, o_mem)
        gated = _gate_fwd(groups, pb)
        y = _matmul(gated, p["w_out"][l], F32, "out_proj")
        saved.append(dict(h=h, hb=hb, y=y, wp=wp, wq_up=wq_up, wkv_up=wkv_up, bias_row=bias_row, pa=pa, pb=pb,
                          fbias=fbias, lse_fox=lse_fox, rm_sb=rm_sb, cqn=cqn, ckvn=ckvn, mla_q=mla_q, mla_k=mla_k,
                          kv=kv, lse_mla=lse_mla, mkv=mkv, lse_mem=lse_mem, groups=groups, gated=gated))
        h, hb = _ln_fwd(h, y, p["ln_g"][l], p["ln_b"][l], "ln_fwd")

    dh, sq_cols = _loss_grad(h, tgt)
    loss_sum = jnp.sum(sq_cols)

    grads = {k: [None] * DEPTH for k in ("w_in", "b_forget", "mla_q_norm_g", "w_mla_q_up", "mla_kv_norm_g",
                                         "w_mla_kv_up", "w_mem_kv", "w_out", "ln_g", "ln_b")}
    dmemn = []
    dy1, dy2, c1 = dh, None, 1.0
    for l in reversed(range(DEPTH)):
        r = saved[l]
        pa, pb = r["pa"], r["pb"]
        o_fox, o_sb, o_mla, o_mem = r["groups"]
        du, du_b, dg, db = _ln_bwd(r["h"], r["y"], p["ln_g"][l], dy1, dy2, c1, "ln_bwd")
        grads["ln_g"][l], grads["ln_b"][l] = dg[0], db[0]
        dgated = _matmul(du_b, p["w_out"][l], F32, "out_proj_dx", "nt")
        grads["w_out"][l] = _matmul(r["gated"], du_b, F32, "out_proj_dw", "tn")
        dmixed, dgate_b = _gate_bwd(dgated, r["groups"], pb)

        dfq, dfk, dfv, dfc_q, dfc_k = _softmax_bwd(pairs(pa, 0), pairs(pa, 2), vals(pa, 4), o_fox, r["lse_fox"],
                                                   dmixed, 0, s, s, head_scale, True, r["fbias"], "fox_bwd")
        dmisc_f, dbf = _forget_bwd(pb, r["bias_row"], dfc_q + jnp.pad(dfc_k.T, ((0, 0), (0, 128 - 8))))
        grads["b_forget"][l] = dbf[0, :N_HEADS]

        dsq, dsk, dsv = _sb_bwd(pa, 6, 8, 10, dmixed, 1, *r["rm_sb"], s, head_scale, "sb_bwd")

        dqm, dkm, dvm = _softmax_bwd(r["mla_q"], r["mla_k"], vals(r["kv"], 2), o_mla, r["lse_mla"], dmixed, 2,
                                     s, s, mla_scale, True, None, "mla_bwd")
        dq_mla_b = _rope_q_bwd(dqm, tabs, mats)
        dcqn = _matmul(dq_mla_b, r["wq_up"], F32, "q_up_dx", "nt")
        grads["w_mla_q_up"][l] = _heads_interleaved(_matmul(r["cqn"], dq_mla_b, F32, "q_up_dw", "tn"),
                                                    HEAD_DIM + MLA_ROPE, HEAD_DIM)
        dcq_b, dgq = _rms_bwd(pb, MLA_Q_RANK, PB_CQ_BLK, p["mla_q_norm_g"][l], dcqn, "rms_q_bwd")
        grads["mla_q_norm_g"][l] = dgq[0]
        dkv_b = jnp.concatenate([dkm[:, :GROUP_W], dvm], axis=1).astype(MXU_DTYPE)
        dckvn = _matmul(dkv_b, r["wkv_up"], F32, "kv_up_dx", "nt")
        grads["w_mla_kv_up"][l] = _heads_interleaved(_matmul(r["ckvn"], dkv_b, F32, "kv_up_dw", "tn"),
                                                     2 * HEAD_DIM, HEAD_DIM)
        dckv_b, dgkv = _rms_bwd(pb, MLA_KV_RANK, PB_CKV_BLK, p["mla_kv_norm_g"][l], dckvn, "rms_kv_bwd")
        grads["mla_kv_norm_g"][l] = dgkv[0]
        dmisc_k = _rope_k_bwd(dkm[:, GROUP_W:], tabs, mats)

        dmq, dmk, dmv = _softmax_bwd(pairs(pa, 12), pairs(r["mkv"], 0), vals(r["mkv"], 2), o_mem, r["lse_mem"],
                                     dmixed, 3, s, nm, head_scale, False, None, "mem_bwd")
        dmkv_b = jnp.concatenate([dmk, dmv], axis=1).astype(MXU_DTYPE)
        grads["w_mem_kv"][l] = _matmul(memn_b, dmkv_b, F32, "mem_kv_dw", "tn")
        dmemn.append(_matmul(dmkv_b, p["w_mem_kv"][l], F32, "mem_kv_dx", "nt"))

        dmisc = jnp.concatenate([dmisc_f[:, :MISC_KROT], dmisc_k[:, MISC_KROT:MISC_KROT + MLA_ROPE],
                                 jnp.zeros((s, 128 - MISC_KROT - MLA_ROPE), F32)], axis=1)
        dp = jnp.concatenate([d.astype(MXU_DTYPE) for d in (dfq, dfk, dfv, dsq, dsk, dsv, dmq)]
                             + [dgate_b, dcq_b, dckv_b, dmisc.astype(MXU_DTYPE)], axis=1)
        dhproj = _matmul(dp, r["wp"], F32, "proj_dx", "nt")
        grads["w_in"][l] = _merge_dw_in(_matmul(r["hb"], dp, F32, "proj_dw", "tn"))
        dy1, dy2, c1 = du, dhproj, ALPHA

    dx, _, dg_in, db_in = _ln_bwd(x2, None, p["ln_in_g"], dy1, dy2, c1, "ln_in_bwd")
    _, _, dg_mem, db_mem = _ln_bwd(mem2, None, p["mem_ln_g"], dmemn[0], dmemn[1], 1.0, "ln_mem_bwd")
    out = {k: jnp.stack(v) for k, v in grads.items()}
    out.update(ln_in_g=dg_in[0], ln_in_b=db_in[0], mem_ln_g=dg_mem[0], mem_ln_b=db_mem[0])
    return loss_sum, dx, out


WIDE = "w_in"
FLAT_NAMES = ("w_out", "w_mem_kv", "w_mla_q_up", "w_mla_kv_up")
FLAT_ROWS = 896
BIG_NAMES = (WIDE,) + FLAT_NAMES
BIG_AXIS = dict(w_in=2, w_out=1, w_mem_kv=1, w_mla_q_up=2, w_mla_kv_up=2)
SMALL_NAMES = ("ln_in_g", "ln_in_b", "mem_ln_g", "mem_ln_b", "ln_g", "ln_b", "b_forget", "mla_q_norm_g",
               "mla_kv_norm_g")
ALL_NAMES = ("ln_in_g", "ln_in_b", "mem_ln_g", "mem_ln_b", "w_in", "b_forget", "mla_q_norm_g", "w_mla_q_up",
             "mla_kv_norm_g", "w_mla_kv_up", "w_mem_kv", "w_out", "ln_g", "ln_b")
N_CHIPS = 4
N_DEV = 8


def _rows_of(shape):
    rows = -(-int(np.prod(shape)) // LANES)
    return -(-rows // PACK_ALIGN) * PACK_ALIGN


def _pack(arrs, rows):
    parts = []
    for a in arrs:
        f = a.reshape(-1)
        n = _rows_of(a.shape) * LANES
        parts.append(jnp.pad(f, (0, n - f.shape[0])).reshape(-1, LANES))
    used = sum(q.shape[0] for q in parts)
    if rows > used:
        parts.append(jnp.zeros((rows - used, LANES), parts[0].dtype))
    return jnp.concatenate(parts, axis=0)


def _unpack(buf, shapes):
    out, r = [], 0
    for shp in shapes:
        n = _rows_of(shp)
        out.append(buf[r:r + n].reshape(-1)[:int(np.prod(shp))].reshape(shp))
        r += n
    return out


def _sharded_pair(get):
    wide = get(WIDE)
    return [wide.reshape(-1, wide.shape[-1]), _pack([get(n) for n in FLAT_NAMES], FLAT_ROWS)]


HBM_SPEC = pl.BlockSpec(memory_space=pltpu.HBM)
SWAP_PIECES = 8


def _gather_weights(shards):
    n = len(shards)

    def body(*refs):
        w_refs, out_refs, (send_sems, recv_sems, local_sems) = refs[:n], refs[n:2 * n], refs[2 * n:]
        x, y, c = (lax.axis_index(a) for a in MESH_AXES)
        me, sibling = 2 * x + y, (x, y, 1 - c)
        chips = [(1 - x, y), (x, 1 - y), (1 - x, 1 - y)]
        local, first, passed = [], [], []
        for a in range(n):
            w_ref, out_ref, half = w_refs[a], out_refs[a], shards[a].shape[0] // 2

            def part(chip, core, out_ref=out_ref, half=half):
                return out_ref.at[chip, pl.ds(core * half, half)]

            def copy(k, src, dst, to, a=a):
                return pltpu.make_async_remote_copy(
                    src_ref=src, dst_ref=dst, send_sem=send_sems.at[6 * a + k], recv_sem=recv_sems.at[6 * a + k],
                    device_id=to, device_id_type=pl.DeviceIdType.MESH)

            local.append(pltpu.make_async_copy(w_ref, out_ref.at[me], local_sems.at[a]))
            local[-1].start()
            mine = [copy(k, w_ref.at[pl.ds(c * half, half)], part(me, c), (px, py, c))
                    for k, (px, py) in enumerate(chips)]
            for cp in mine:
                cp.start()
            first.append((mine, part, copy))
        for mine, part, copy in first:
            for k, (px, py) in enumerate(chips):
                copy(k, part(me, c), part(2 * px + py, c), (px, py, c)).wait_recv()
                passed.append(copy(3 + k, part(2 * px + py, c), part(2 * px + py, c), sibling))
                passed[-1].start()
        for mine, part, copy in first:
            for k, (px, py) in enumerate(chips):
                copy(3 + k, part(me, c), part(2 * px + py, 1 - c), sibling).wait_recv()
        for cp in [cp for mine, _, _ in first for cp in mine] + passed:
            cp.wait_send()
        for cp in local:
            cp.wait()

    return pl.pallas_call(
        body, name="gather_weights",
        out_shape=[jax.ShapeDtypeStruct((N_CHIPS,) + s.shape, s.dtype) for s in shards],
        in_specs=[HBM_SPEC] * n, out_specs=[HBM_SPEC] * n,
        scratch_shapes=[pltpu.SemaphoreType.DMA((6 * n,)), pltpu.SemaphoreType.DMA((6 * n,)),
                        pltpu.SemaphoreType.DMA((n,))],
    )(*shards)


def _exchange_grads(bigs, small):
    nb = len(bigs)

    def body(*refs):
        big_refs, small_ref, out_refs = refs[:nb], refs[nb], refs[nb + 1:2 * nb + 2]
        send_sems, recv_sems, local_sems = refs[2 * nb + 2:]
        x, y, c = (lax.axis_index(a) for a in MESH_AXES)
        me, my_chip = 4 * x + 2 * y + c, 2 * x + y
        flips = [(fx, fy, fc) for fx in (0, 1) for fy in (0, 1) for fc in (0, 1) if fx or fy or fc]
        peers = [(1 - x if fx else x, 1 - y if fy else y, 1 - c if fc else c) for fx, fy, fc in flips]

        sources = [lambda chip, core, r=big_refs[a], half=bigs[a].shape[1] // 2: r.at[chip, pl.ds(core * half, half)]
                   for a in range(nb)] + [lambda chip, core: small_ref]

        def copy(a, k, src, slot, to):
            return pltpu.make_async_remote_copy(
                src_ref=src, dst_ref=out_refs[a].at[slot], send_sem=send_sems.at[7 * a + k],
                recv_sem=recv_sems.at[7 * a + k], device_id=to, device_id_type=pl.DeviceIdType.MESH)

        own = [pltpu.make_async_copy(src(my_chip, c), out_refs[a].at[me], local_sems.at[a])
               for a, src in enumerate(sources)]
        for cp in own:
            cp.start()
        sends = [copy(a, k, src(2 * px + py, pc), me, (px, py, pc))
                 for a, src in enumerate(sources) for k, (px, py, pc) in enumerate(peers)]
        for cp in sends:
            cp.start()
        for a, src in enumerate(sources):
            for k, (px, py, pc) in enumerate(peers):
                copy(a, k, src(my_chip, c), 4 * px + 2 * py + pc, (px, py, pc)).wait_recv()
        for cp in sends:
            cp.wait_send()
        for cp in own:
            cp.wait()

    return pl.pallas_call(
        body, name="exchange_grads",
        out_shape=[jax.ShapeDtypeStruct((N_DEV, b.shape[1] // 2, b.shape[2]), b.dtype) for b in bigs]
        + [jax.ShapeDtypeStruct((N_DEV,) + small.shape, small.dtype)],
        in_specs=[HBM_SPEC] * (nb + 1), out_specs=[HBM_SPEC] * (nb + 1),
        scratch_shapes=[pltpu.SemaphoreType.DMA((7 * (nb + 1),)), pltpu.SemaphoreType.DMA((7 * (nb + 1),)),
                        pltpu.SemaphoreType.DMA((nb + 1,))],
    )(*bigs, small)


def _sum_parts(parts, name):
    n, rows, width = parts.shape
    tile = _pick(rows, (128, 16, 8))

    def kern(p_ref, o_ref):
        g = p_ref[0].astype(F32)
        for d in range(1, n):
            g = g + p_ref[d].astype(F32)
        o_ref[...] = g

    return pl.pallas_call(
        kern, name=name, grid=(rows // tile,),
        in_specs=[pl.BlockSpec((n, tile, width), lambda i: (0, i, 0))],
        out_specs=pl.BlockSpec((tile, width), lambda i: (i, 0)), out_shape=jax.ShapeDtypeStruct((rows, width), F32),
        compiler_params=pltpu.CompilerParams(dimension_semantics=("parallel",), vmem_limit_bytes=VMEM_LIMIT),
    )(parts)


def _swap_halves(mine):
    n = len(mine)
    pieces = [(a, k * (q.shape[0] // SWAP_PIECES), q.shape[0] // SWAP_PIECES)
              for a, q in enumerate(mine) for k in range(SWAP_PIECES)]

    def body(*refs):
        q_refs, out_refs, (send_sems, recv_sems, local_sems) = refs[:n], refs[n:2 * n], refs[2 * n:]
        x, y, c = (lax.axis_index(a) for a in MESH_AXES)

        def copy(i, core):
            a, r0, nr = pieces[i]
            return pltpu.make_async_remote_copy(
                src_ref=q_refs[a].at[pl.ds(r0, nr)], dst_ref=out_refs[a].at[core, pl.ds(r0, nr)],
                send_sem=send_sems.at[i], recv_sem=recv_sems.at[i],
                device_id=(x, y, 1 - c), device_id_type=pl.DeviceIdType.MESH)

        own = [pltpu.make_async_copy(q_refs[a], out_refs[a].at[c], local_sems.at[a]) for a in range(n)]
        swaps = [copy(i, c) for i in range(len(pieces))]
        for cp in own + swaps:
            cp.start()
        for i in range(len(pieces)):
            copy(i, 1 - c).wait_recv()
        for cp in swaps:
            cp.wait_send()
        for cp in own:
            cp.wait()

    return pl.pallas_call(
        body, name="swap_halves", out_shape=[jax.ShapeDtypeStruct((2,) + q.shape, q.dtype) for q in mine],
        in_specs=[HBM_SPEC] * n, out_specs=[HBM_SPEC] * n,
        scratch_shapes=[pltpu.SemaphoreType.DMA((len(pieces),)), pltpu.SemaphoreType.DMA((len(pieces),)),
                        pltpu.SemaphoreType.DMA((n,))],
    )(*mine)


def _adamw(parts, w, m, v, name):
    rows, width = w.shape
    n_parts = parts.shape[0]
    tile = _pick(rows, (128, 16, 8))
    bc1 = 1.0 - ADAM_B1 ** ADAM_STEP
    bc2 = 1.0 - ADAM_B2 ** ADAM_STEP

    def kern(p_ref, w_ref, m_ref, v_ref, g_ref, d_ref, nm_ref, nv_ref):
        g = p_ref[0].astype(F32)
        for d in range(1, n_parts):
            g = g + p_ref[d].astype(F32)
        nm = ADAM_B1 * m_ref[...] + (1.0 - ADAM_B1) * g
        nv = ADAM_B2 * v_ref[...] + (1.0 - ADAM_B2) * (g * g)
        g_ref[...] = g
        nm_ref[...] = nm
        nv_ref[...] = nv
        d_ref[...] = -ADAM_LR * ((nm / bc1) / (jnp.sqrt(nv / bc2) + ADAM_EPS) + ADAM_WD * w_ref[...])

    spec = pl.BlockSpec((tile, width), lambda i: (i, 0))
    return pl.pallas_call(
        kern, name=name, grid=(rows // tile,),
        in_specs=[pl.BlockSpec((n_parts, tile, width), lambda i: (0, i, 0)), spec, spec, spec],
        out_specs=[spec] * 4, out_shape=[jax.ShapeDtypeStruct((rows, width), F32)] * 4,
        compiler_params=pltpu.CompilerParams(dimension_semantics=("parallel",), vmem_limit_bytes=VMEM_LIMIT),
    )(parts, w, m, v)


def kernel(x, mem, ln_in_g, ln_in_b, mem_ln_g, mem_ln_b, w_in, b_forget, mla_q_norm_g, w_mla_q_up, mla_kv_norm_g, w_mla_kv_up, w_mem_kv, w_out, ln_g, ln_b, loss_target, m_ln_in_g, m_ln_in_b, m_mem_ln_g, m_mem_ln_b, m_w_in, m_b_forget, m_mla_q_norm_g, m_w_mla_q_up, m_mla_kv_norm_g, m_w_mla_kv_up, m_w_mem_kv, m_w_out, m_ln_g, m_ln_b, v_ln_in_g, v_ln_in_b, v_mem_ln_g, v_mem_ln_b, v_w_in, v_b_forget, v_mla_q_norm_g, v_w_mla_q_up, v_mla_kv_norm_g, v_w_mla_kv_up, v_w_mem_kv, v_w_out, v_ln_g, v_ln_b):
    w = dict(ln_in_g=ln_in_g, ln_in_b=ln_in_b, mem_ln_g=mem_ln_g, mem_ln_b=mem_ln_b, w_in=w_in, b_forget=b_forget,
             mla_q_norm_g=mla_q_norm_g, w_mla_q_up=w_mla_q_up, mla_kv_norm_g=mla_kv_norm_g,
             w_mla_kv_up=w_mla_kv_up, w_mem_kv=w_mem_kv, w_out=w_out, ln_g=ln_g, ln_b=ln_b)
    mo = dict(ln_in_g=m_ln_in_g, ln_in_b=m_ln_in_b, mem_ln_g=m_mem_ln_g, mem_ln_b=m_mem_ln_b, w_in=m_w_in,
              b_forget=m_b_forget, mla_q_norm_g=m_mla_q_norm_g, w_mla_q_up=m_w_mla_q_up,
              mla_kv_norm_g=m_mla_kv_norm_g, w_mla_kv_up=m_w_mla_kv_up, w_mem_kv=m_w_mem_kv, w_out=m_w_out,
              ln_g=m_ln_g, ln_b=m_ln_b)
    vo = dict(ln_in_g=v_ln_in_g, ln_in_b=v_ln_in_b, mem_ln_g=v_mem_ln_g, mem_ln_b=v_mem_ln_b, w_in=v_w_in,
              b_forget=v_b_forget, mla_q_norm_g=v_mla_q_norm_g, w_mla_q_up=v_w_mla_q_up,
              mla_kv_norm_g=v_mla_kv_norm_g, w_mla_kv_up=v_w_mla_kv_up, w_mem_kv=v_w_mem_kv, w_out=v_w_out,
              ln_g=v_ln_g, ln_b=v_ln_b)
    flat_shapes = [w[n].shape for n in FLAT_NAMES]
    small_shapes = [w[n].shape for n in SMALL_NAMES]

    got_wide, got_flat = _gather_weights(_sharded_pair(lambda n: w[n].astype(MXU_DTYPE)))
    full = dict(w)
    full[WIDE] = jnp.concatenate([got_wide[j] for j in range(N_CHIPS)], axis=1).reshape(
        w[WIDE].shape[:2] + (N_CHIPS * w[WIDE].shape[2],))
    per_chip = [_unpack(got_flat[j], flat_shapes) for j in range(N_CHIPS)]
    for idx, n in enumerate(FLAT_NAMES):
        full[n] = jnp.concatenate([per_chip[j][idx] for j in range(N_CHIPS)], axis=BIG_AXIS[n])

    loss_sum, dx, g = _local_step(x[0], mem[0], loss_target[0], full)
    loss = lax.psum(loss_sum * (0.5 / D_MODEL), MESH_AXES)

    def shard_of(n, j):
        ax, size = BIG_AXIS[n], w[n].shape[BIG_AXIS[n]]
        return lax.slice_in_dim(g[n], j * size, (j + 1) * size, axis=ax).astype(MXU_DTYPE)

    per_dest = [_sharded_pair(lambda n, j=j: shard_of(n, j)) for j in range(N_CHIPS)]
    bigs = [jnp.stack([per_dest[j][a] for j in range(N_CHIPS)]) for a in range(2)]
    *big_parts, small_parts = _exchange_grads(bigs, _pack([g[n] for n in SMALL_NAMES], SMALL_ROWS))
    halves = _swap_halves([_sum_parts(big_parts[0], "sum_wide"), _sum_parts(big_parts[1], "sum_flat")])

    res = []
    for a, (grad, nm) in enumerate(zip(halves, ("adamw_wide", "adamw_flat"))):
        state = [_sharded_pair(lambda n, src=src: src[n])[a] for src in (w, mo, vo)]
        res.append(_adamw(grad.reshape((1,) + state[0].shape), *state, nm))
    res_small = _adamw(small_parts, *[_pack([src[n] for n in SMALL_NAMES], SMALL_ROWS) for src in (w, mo, vo)],
                       "adamw_replicated")
    outs = []
    for kind in range(4):
        vals = {WIDE: res[0][kind].reshape(w[WIDE].shape)}
        vals.update(zip(FLAT_NAMES, _unpack(res[1][kind], flat_shapes)))
        vals.update(zip(SMALL_NAMES, _unpack(res_small[kind], small_shapes)))
        outs += [vals[n] for n in ALL_NAMES]
    return (loss, dx[None], *outs)
```

```python
import functools

import numpy as np
import jax
import jax.numpy as jnp
from jax import lax
from jax.experimental import pallas as pl
from jax.experimental.pallas import tpu as pltpu

F32 = jnp.float32
MXU_DTYPE = jnp.bfloat16

DEPTH = 2
D_MODEL = 1024
GROUP_W = 256
N_HEADS = 4
HEAD_DIM = 64
MLA_Q_RANK = 256
MLA_KV_RANK = 128
MLA_ROPE = 32
MLA_Q_COLS = N_HEADS * (HEAD_DIM + MLA_ROPE)
MLA_KV_COLS = N_HEADS * 2 * HEAD_DIM
ROPE_THETA = 10000.0
LN_EPS = 1e-5
RMS_EPS = 1e-6
ALPHA = (2 * DEPTH) ** 0.25
ADAM_LR, ADAM_B1, ADAM_B2, ADAM_EPS, ADAM_WD, ADAM_STEP = 0.001, 0.9, 0.999, 1e-08, 0.01, 10

_SPLIT = (256, 256, 256, 4, 256, 256, 256, 256, 128, 32, 256, 1024)
_OFF = [int(o) for o in np.cumsum((0,) + _SPLIT)]
IN_COLS = _OFF[-1]
PA_COLS = 7 * GROUP_W
PB_COLS = 1024 + 256 + 128 + 128
PB_CQ_BLK, PB_CKV_BLK, PB_MISC_BLK = 4, 10, 11
MISC_KROT = 4

LANES = 1024
PACK_ALIGN = 16
BIG_ROWS = 2560
SMALL_ROWS = 144
ROW_TILE = 256
PROJ_BWD_ROWS = 512
ATT_TILE = 256
SB_QUERY_TILE = 256
SOFTMAX_TILE = 512
PAIR = 128
SB_SLOT = PAIR // N_HEADS
VMEM_LIMIT = 56 * 1024 * 1024
MATMUL_VMEM = 30 * 1024 * 1024
NEG = -1e30
EXP_UNDERFLOW = -104.0
DEAD_LOGIT = -110.0
REACH_SLACK = 1.0 + 2.0 ** -10
MESH_AXES = ("x", "y", "c")


def _dot(a, b):
    return jnp.dot(a, b, preferred_element_type=F32)


def _dot_nt(a, b):
    return lax.dot_general(a, b, (((1,), (1,)), ((), ())), preferred_element_type=F32)


def _dot_tn(a, b):
    return lax.dot_general(a, b, (((0,), (0,)), ((), ())), preferred_element_type=F32)


def _split2(x):
    hi = x.astype(MXU_DTYPE)
    lo = (x - hi.astype(F32)).astype(MXU_DTYPE)
    return hi, lo


def _split3(x):
    hi = x.astype(MXU_DTYPE)
    r = x - hi.astype(F32)
    mid = r.astype(MXU_DTYPE)
    lo = (r - mid.astype(F32)).astype(MXU_DTYPE)
    return hi, mid, lo


def _dot_exact_r(x, pm):
    hi, mid, lo = _split3(x)
    return _dot(hi, pm) + _dot(mid, pm) + _dot(lo, pm)


def _dot_exact_l(pm, x):
    hi, mid, lo = _split3(x)
    return _dot(pm, hi) + _dot(pm, mid) + _dot(pm, lo)


def _pick(dim, prefs):
    for p in prefs:
        if dim % p == 0:
            return p
    return dim


def _softplus(z):
    return jnp.maximum(z, 0.0) + jnp.log(1.0 + jnp.exp(-jnp.abs(z)))


def _tile_options(dim):
    opts = [d for d in range(128, min(dim, 2048) + 1, 128) if dim % d == 0]
    return opts or [dim]


def _matmul_tiles(m, n, k, out_bytes):
    tk = k if k <= 4096 else _pick(k, (1024, 512, 256, 128))
    best = None
    for tm in _tile_options(m):
        for tn in _tile_options(n):
            vmem = 2 * 2 * (tm * tk + tk * tn) + 4 * tm * tn + 2 * out_bytes * tm * tn
            if vmem <= MATMUL_VMEM and (best is None or tm * tn / (tm + tn) > best[0]):
                best = (tm * tn / (tm + tn), tm, tn)
    return best[1], best[2], tk


def _matmul(a, b, out_dtype, name, mode="nn"):
    m, k = (a.shape[1], a.shape[0]) if mode == "tn" else a.shape
    n = b.shape[0] if mode == "nt" else b.shape[1]
    tm, tn, tk = _matmul_tiles(m, n, k, jnp.dtype(out_dtype).itemsize)
    nk = k // tk
    dot = {"nn": _dot, "tn": _dot_tn, "nt": _dot_nt}[mode]

    def kern(a_ref, b_ref, o_ref, *acc):
        if nk == 1:
            o_ref[...] = dot(a_ref[...], b_ref[...]).astype(o_ref.dtype)
            return
        acc_ref, = acc
        kk = pl.program_id(2)

        @pl.when(kk == 0)
        def _():
            acc_ref[...] = jnp.zeros_like(acc_ref)

        acc_ref[...] += dot(a_ref[...], b_ref[...])

        @pl.when(kk == nk - 1)
        def _():
            o_ref[...] = acc_ref[...].astype(o_ref.dtype)

    a_spec = (pl.BlockSpec((tk, tm), lambda i, j, kk: (kk, i)) if mode == "tn"
              else pl.BlockSpec((tm, tk), lambda i, j, kk: (i, kk)))
    b_spec = (pl.BlockSpec((tn, tk), lambda i, j, kk: (j, kk)) if mode == "nt"
              else pl.BlockSpec((tk, tn), lambda i, j, kk: (kk, j)))
    return pl.pallas_call(
        kern, name=name, grid=(m // tm, n // tn, nk), in_specs=[a_spec, b_spec],
        out_specs=pl.BlockSpec((tm, tn), lambda i, j, kk: (i, j)),
        out_shape=jax.ShapeDtypeStruct((m, n), out_dtype),
        scratch_shapes=[pltpu.VMEM((tm, tn), F32)] if nk > 1 else [],
        compiler_params=pltpu.CompilerParams(
            dimension_semantics=("parallel", "parallel", "arbitrary"), vmem_limit_bytes=VMEM_LIMIT),
    )(a.astype(MXU_DTYPE), b.astype(MXU_DTYPE))


def _piece_arrays(pieces):
    return [a for p in pieces for a in (p if isinstance(p, tuple) else (p,))]


def _join_pieces(refs, pieces):
    refs, cols = list(refs), []
    for p in pieces:
        vals = [refs.pop(0)[...] for _ in (p if isinstance(p, tuple) else (p,))]
        cols.append(functools.reduce(jnp.add, vals).astype(MXU_DTYPE))
    return jnp.concatenate(cols, axis=1)


def _proj_dx(pieces, w):
    arrs = _piece_arrays(pieces)
    rows, n = arrs[0].shape[0], w.shape[0]
    tm = min(PROJ_BWD_ROWS, rows)

    def kern(*refs):
        refs[-1][...] = _dot_nt(_join_pieces(refs[:len(arrs)], pieces), refs[len(arrs)][...])

    return pl.pallas_call(
        kern, name="proj_dx", grid=(rows // tm,),
        in_specs=[pl.BlockSpec((tm, a.shape[1]), lambda i: (i, 0)) for a in arrs]
        + [pl.BlockSpec(w.shape, lambda i: (0, 0), pipeline_mode=pl.Buffered(1))],
        out_specs=pl.BlockSpec((tm, n), lambda i: (i, 0)), out_shape=jax.ShapeDtypeStruct((rows, n), F32),
        compiler_params=pltpu.CompilerParams(dimension_semantics=("parallel",), vmem_limit_bytes=VMEM_LIMIT),
    )(*arrs, w.astype(MXU_DTYPE))


def _proj_dw(a, pieces):
    arrs = _piece_arrays(pieces)
    rows, m = a.shape
    k = sum(x.shape[1] for x in (p[0] if isinstance(p, tuple) else p for p in pieces))
    tk = min(PROJ_BWD_ROWS, rows)

    def kern(*refs):
        o_ref = refs[-1]

        @pl.when(pl.program_id(0) == 0)
        def _():
            o_ref[...] = jnp.zeros_like(o_ref)

        o_ref[...] += _dot_tn(refs[0][...], _join_pieces(refs[1:1 + len(arrs)], pieces))

    return pl.pallas_call(
        kern, name="proj_dw", grid=(rows // tk,),
        in_specs=[pl.BlockSpec((tk, m), lambda i: (i, 0))]
        + [pl.BlockSpec((tk, x.shape[1]), lambda i: (i, 0)) for x in arrs],
        out_specs=pl.BlockSpec((m, k), lambda i: (0, 0), pipeline_mode=pl.Buffered(1)),
        out_shape=jax.ShapeDtypeStruct((m, k), F32),
        compiler_params=pltpu.CompilerParams(dimension_semantics=("arbitrary",), vmem_limit_bytes=VMEM_LIMIT),
    )(a, *arrs)


def _rowwise(body, name, rows, tile, row_ins, full_ins, row_outs, acc_outs=(), scratch=(),
             reverse=False, sequential=False):
    n = rows // tile

    def ridx(i):
        return (n - 1 - i) if reverse else i

    in_specs, args = [], []
    for arr, width, cb in row_ins:
        in_specs.append(pl.BlockSpec((tile, width), lambda i, cb=cb: (ridx(i), cb)))
        args.append(arr)
    for arr in full_ins:
        in_specs.append(pl.BlockSpec(arr.shape, lambda i, nd=arr.ndim: (0,) * nd))
        args.append(arr)
    out_shape = [jax.ShapeDtypeStruct((rows, w), dt) for w, dt in row_outs]
    out_shape += [jax.ShapeDtypeStruct(s, dt) for s, dt in acc_outs]
    out_specs = [pl.BlockSpec((tile, w), lambda i: (ridx(i), 0)) for w, dt in row_outs]
    out_specs += [pl.BlockSpec(s, lambda i, nd=len(s): (0,) * nd) for s, dt in acc_outs]

    def kern(*refs):
        body(pl.program_id(0), *refs)

    sem = "arbitrary" if (acc_outs or sequential) else "parallel"
    return pl.pallas_call(
        kern, name=name, grid=(n,), in_specs=in_specs, out_specs=out_specs, out_shape=out_shape,
        scratch_shapes=list(scratch),
        compiler_params=pltpu.CompilerParams(dimension_semantics=(sem,), vmem_limit_bytes=VMEM_LIMIT),
    )(*args)


def _ln_stats(u):
    mu = jnp.mean(u, axis=-1, keepdims=True)
    xc = u - mu
    var = jnp.mean(xc * xc, axis=-1, keepdims=True)
    return xc, lax.rsqrt(var + LN_EPS)


def _ln_fwd(a, b, g, beta, name):
    rows, d = a.shape
    has_b = b is not None

    def body(i, *refs):
        if has_b:
            a_ref, b_ref, g_ref, be_ref, h_ref, hb_ref = refs
            u = ALPHA * a_ref[...] + b_ref[...]
        else:
            a_ref, g_ref, be_ref, h_ref, hb_ref = refs
            u = a_ref[...]
        xc, rstd = _ln_stats(u)
        y = xc * rstd * g_ref[...] + be_ref[...]
        h_ref[...] = y
        hb_ref[...] = y.astype(hb_ref.dtype)

    row_ins = [(a, d, 0)] + ([(b, d, 0)] if has_b else [])
    return _rowwise(body, name, rows, min(ROW_TILE, rows), row_ins,
                    [g.reshape(1, d), beta.reshape(1, d)], [(d, F32), (d, MXU_DTYPE)])


def _ln_bwd(a, b, g, dy1, dy2, c1, name):
    rows, d = a.shape
    has_b = b is not None
    has_2 = dy2 is not None

    def body(i, *refs):
        refs = list(refs)
        a_ref = refs.pop(0)
        u = a_ref[...]
        if has_b:
            u = ALPHA * u + refs.pop(0)[...]
        dy = c1 * refs.pop(0)[...]
        if has_2:
            dy = dy + refs.pop(0)[...]
        g_ref, du_ref, dub_ref, dg_ref, db_ref = refs

        @pl.when(i == 0)
        def _():
            dg_ref[...] = jnp.zeros_like(dg_ref)
            db_ref[...] = jnp.zeros_like(db_ref)

        xc, rstd = _ln_stats(u)
        xhat = xc * rstd
        dxh = dy * g_ref[...]
        m1 = jnp.mean(dxh, axis=-1, keepdims=True)
        m2 = jnp.mean(dxh * xhat, axis=-1, keepdims=True)
        du = rstd * (dxh - m1 - xhat * m2)
        du_ref[...] = du
        dub_ref[...] = du.astype(dub_ref.dtype)
        dg_ref[...] += jnp.sum(dy * xhat, axis=0, keepdims=True)
        db_ref[...] += jnp.sum(dy, axis=0, keepdims=True)

    row_ins = [(a, d, 0)] + ([(b, d, 0)] if has_b else []) + [(dy1, d, 0)] + ([(dy2, d, 0)] if has_2 else [])
    return _rowwise(body, name, rows, min(ROW_TILE, rows), row_ins, [g.reshape(1, d)],
                    [(d, F32), (d, MXU_DTYPE)], [((1, d), F32), ((1, d), F32)])


def _loss_grad(h, target):
    rows, d = h.shape

    def body(i, h_ref, t_ref, dh_ref, acc_ref):
        @pl.when(i == 0)
        def _():
            acc_ref[...] = jnp.zeros_like(acc_ref)

        e = h_ref[...] - t_ref[...]
        dh_ref[...] = e * (1.0 / d)
        acc_ref[...] += jnp.sum(e * e, axis=0, keepdims=True)

    return _rowwise(body, "loss_grad", rows, ROW_TILE, [(h, d, 0), (target, d, 0)], [],
                    [(d, F32)], [((1, d), F32)])


def _gate_fwd(groups, pb):
    rows = pb.shape[0]
    w = GROUP_W * len(groups)

    def body(i, *refs):
        g = refs[4][...]
        mixed = jnp.concatenate([r[...] for r in refs[:4]], axis=1)
        refs[5][...] = (mixed * (g / (1.0 + jnp.exp(-g)))).astype(refs[5].dtype)

    return _rowwise(body, "gate_fwd", rows, ROW_TILE, [(o, GROUP_W, 0) for o in groups] + [(pb, w, 0)], [],
                    [(w, MXU_DTYPE)])[0]


def _gate_bwd(dgated, groups, pb):
    rows = pb.shape[0]
    w = GROUP_W * len(groups)

    def body(i, *refs):
        dg = refs[0][...]
        mixed = jnp.concatenate([r[...] for r in refs[1:5]], axis=1)
        g = refs[5][...]
        dm_ref, dgate_ref = refs[6], refs[7]
        sig = 1.0 / (1.0 + jnp.exp(-g))
        dm_ref[...] = dg * (g * sig)
        dgate_ref[...] = (dg * mixed * (sig * (1.0 + g * (1.0 - sig)))).astype(dgate_ref.dtype)

    return _rowwise(body, "gate_bwd", rows, ROW_TILE,
                    [(dgated, w, 0)] + [(o, GROUP_W, 0) for o in groups] + [(pb, w, 0)], [],
                    [(w, F32), (w, MXU_DTYPE)])


def _tri(n, kind):
    r = np.arange(n)[:, None]
    c = np.arange(n)[None, :]
    m = {"lower_incl": r >= c, "upper_incl": r <= c, "row_gt_col": r > c, "row_lt_col": r < c}[kind]
    return jnp.asarray(m.astype(np.float32), dtype=MXU_DTYPE)


def _forget_fwd(pb, bias_row):
    rows = pb.shape[0]
    tile = min(ROW_TILE, rows)

    def body(i, x_ref, b_ref, l_ref, o_ref, carry_ref):
        @pl.when(i == 0)
        def _():
            carry_ref[...] = jnp.zeros_like(carry_ref)

        xx = x_ref[...] + b_ref[...]
        lane = lax.broadcasted_iota(jnp.int32, xx.shape, 1)
        lf = jnp.where(lane < N_HEADS, -_softplus(-xx), 0.0)
        o_ref[...] = _dot_exact_l(l_ref[...], lf) + carry_ref[...]
        carry_ref[...] += jnp.sum(lf, axis=0, keepdims=True)

    return _rowwise(body, "forget_fwd", rows, tile, [(pb, 128, PB_MISC_BLK)],
                    [bias_row, _tri(tile, "lower_incl")], [(128, F32)],
                    scratch=[pltpu.VMEM((1, 128), F32)], sequential=True)[0]


def _forget_bwd(pb, bias_row, dfc):
    rows = pb.shape[0]
    tile = min(ROW_TILE, rows)

    def body(i, x_ref, df_ref, b_ref, u_ref, o_ref, db_ref, carry_ref):
        @pl.when(i == 0)
        def _():
            carry_ref[...] = jnp.zeros_like(carry_ref)
            db_ref[...] = jnp.zeros_like(db_ref)

        df = df_ref[...]
        sfx = _dot_exact_l(u_ref[...], df) + carry_ref[...]
        carry_ref[...] += jnp.sum(df, axis=0, keepdims=True)
        xx = x_ref[...] + b_ref[...]
        lane = lax.broadcasted_iota(jnp.int32, xx.shape, 1)
        dl = jnp.where(lane < N_HEADS, sfx / (1.0 + jnp.exp(xx)), 0.0)
        o_ref[...] = dl
        db_ref[...] += jnp.sum(dl, axis=0, keepdims=True)

    return _rowwise(body, "forget_bwd", rows, tile,
                    [(pb, 128, PB_MISC_BLK), (dfc, 128, 0)],
                    [bias_row, _tri(tile, "upper_incl")], [(128, F32)], [((1, 128), F32)],
                    scratch=[pltpu.VMEM((1, 128), F32)], reverse=True, sequential=True)


def _rope_tables(s):
    half = MLA_ROPE // 2
    inv_freq = ROPE_THETA ** (-jnp.arange(half, dtype=F32) / half)
    ang = jnp.arange(s).astype(F32)[:, None] * inv_freq[None, :]
    cos2 = jnp.tile(jnp.cos(ang), (1, 2))
    sin2 = jnp.tile(jnp.sin(ang), (1, 2))
    cx = jnp.tile(cos2, (1, N_HEADS))
    sx = jnp.tile(sin2, (1, N_HEADS))
    cq = jnp.concatenate([jnp.ones((s, GROUP_W), F32), cx], axis=1)
    sq = jnp.concatenate([jnp.zeros((s, GROUP_W), F32), sx], axis=1)
    pad = ((0, 0), (MISC_KROT, 128 - MISC_KROT - MLA_ROPE))
    ck = jnp.pad(cos2, pad)
    sk = jnp.pad(sin2, pad)
    return dict(cq=cq, sq=sq, ck=ck, sk=sk, cx=cx, sx=sx)


def _rot_matrix(width, bases):
    half = MLA_ROPE // 2
    p = np.zeros((width, width), np.float32)
    for b in bases:
        for i in range(half):
            p[b + half + i, b + i] = -1.0
            p[b + i, b + half + i] = 1.0
    return p


def _rope_matrices():
    pq = _rot_matrix(MLA_Q_COLS, [GROUP_W + h * MLA_ROPE for h in range(N_HEADS)])
    pk = _rot_matrix(128, [MISC_KROT])
    p4 = _rot_matrix(128, [h * MLA_ROPE for h in range(N_HEADS)])
    a = np.zeros((128, 128), np.float32)
    for h in range(N_HEADS):
        for r in range(MLA_ROPE):
            a[h * MLA_ROPE + r, MISC_KROT + r] = 1.0
    cast = lambda m: jnp.asarray(m, dtype=MXU_DTYPE)
    return dict(pq=cast(pq), pqt=cast(pq.T), pk=cast(pk), spread=cast(a.T), xa=cast(a), xb=cast(p4.T @ a))


def _rms(c, g):
    r = lax.rsqrt(jnp.mean(c * c, axis=-1, keepdims=True) + RMS_EPS)
    return c * r * g


def _mla_q_fwd(pb, g, w_up, tabs, mats):
    rows = pb.shape[0]

    def body(i, c_ref, cos_ref, sin_ref, g_ref, w_ref, p_ref, q_ref, cn_ref):
        cn = _rms(c_ref[...], g_ref[...]).astype(cn_ref.dtype)
        cn_ref[...] = cn
        q = _dot(cn, w_ref[...])
        q_ref[...] = (q * cos_ref[...] + _dot_exact_r(q, p_ref[...]) * sin_ref[...]).astype(q_ref.dtype)

    return _rowwise(body, "mla_q_fwd", rows, ROW_TILE,
                    [(pb, MLA_Q_RANK, PB_CQ_BLK), (tabs["cq"], MLA_Q_COLS, 0), (tabs["sq"], MLA_Q_COLS, 0)],
                    [g.reshape(1, MLA_Q_RANK), w_up.astype(MXU_DTYPE), mats["pq"]],
                    [(MLA_Q_COLS, MXU_DTYPE), (MLA_Q_RANK, MXU_DTYPE)])


def _mla_kv_fwd(pb, g, w_up, tabs, mats):
    rows = pb.shape[0]

    def body(i, c_ref, x_ref, cos_ref, sin_ref, g_ref, w_ref, p_ref, sp_ref, kv_ref, cn_ref, kr_ref):
        cn = _rms(c_ref[...], g_ref[...]).astype(cn_ref.dtype)
        cn_ref[...] = cn
        kv_ref[...] = _dot(cn, w_ref[...]).astype(kv_ref.dtype)
        xx = x_ref[...]
        kr = xx * cos_ref[...] + _dot_exact_r(xx, p_ref[...]) * sin_ref[...]
        kr_ref[...] = _dot_exact_r(kr, sp_ref[...]).astype(kr_ref.dtype)

    return _rowwise(body, "mla_kv_fwd", rows, ROW_TILE,
                    [(pb, MLA_KV_RANK, PB_CKV_BLK), (pb, 128, PB_MISC_BLK), (tabs["ck"], 128, 0), (tabs["sk"], 128, 0)],
                    [g.reshape(1, MLA_KV_RANK), w_up.astype(MXU_DTYPE), mats["pk"], mats["spread"]],
                    [(MLA_KV_COLS, MXU_DTYPE), (MLA_KV_RANK, MXU_DTYPE), (128, MXU_DTYPE)])


def _rope_q_bwd(dq_full, tabs, mats):
    rows, nq = dq_full.shape

    def body(i, d_ref, cos_ref, sin_ref, pt_ref, o_ref):
        d = d_ref[...]
        o_ref[...] = (d * cos_ref[...] + _dot_exact_r(d * sin_ref[...], pt_ref[...])).astype(o_ref.dtype)

    return _rowwise(body, "rope_q_bwd", rows, ROW_TILE,
                    [(dq_full, nq, 0), (tabs["cq"], nq, 0), (tabs["sq"], nq, 0)], [mats["pqt"]],
                    [(nq, MXU_DTYPE)])[0]


def _rope_k_bwd(dkr, tabs, mats):
    rows = dkr.shape[0]

    def body(i, d_ref, cos_ref, sin_ref, a_ref, b_ref, o_ref):
        d = d_ref[...]
        o_ref[...] = _dot_exact_r(d * cos_ref[...], a_ref[...]) + _dot_exact_r(d * sin_ref[...], b_ref[...])

    return _rowwise(body, "rope_k_bwd", rows, ROW_TILE,
                    [(dkr, 128, 0), (tabs["cx"], 128, 0), (tabs["sx"], 128, 0)], [mats["xa"], mats["xb"]],
                    [(128, F32)])[0]


def _rms_bwd(pb, width, col_blk, g, dy, name):
    rows = pb.shape[0]

    def body(i, c_ref, dy_ref, g_ref, dc_ref, dg_ref):
        @pl.when(i == 0)
        def _():
            dg_ref[...] = jnp.zeros_like(dg_ref)

        c = c_ref[...]
        dy = dy_ref[...]
        r = lax.rsqrt(jnp.mean(c * c, axis=-1, keepdims=True) + RMS_EPS)
        dyg = dy * g_ref[...]
        dc = r * dyg - c * (r * r * r) * jnp.mean(c * dyg, axis=-1, keepdims=True)
        dc_ref[...] = dc.astype(dc_ref.dtype)
        dg_ref[...] += jnp.sum(dy * c * r, axis=0, keepdims=True)

    return _rowwise(body, name, rows, ROW_TILE, [(pb, width, col_blk), (dy, width, 0)], [g.reshape(1, width)],
                    [(width, MXU_DTYPE)], [((1, width), F32)])


def _att_params(parallel):
    return pltpu.CompilerParams(dimension_semantics=("parallel" if parallel else "arbitrary",),
                                vmem_limit_bytes=VMEM_LIMIT)


def _blk_off(j, t):
    return j * t if isinstance(j, int) else pl.multiple_of(j * t, t)


def _causal_mask(t, strict):
    r = lax.broadcasted_iota(jnp.int32, (t, t), 0)
    c = lax.broadcasted_iota(jnp.int32, (t, t), 1)
    return (c < r) if strict else (c <= r)


def _lane_mask(kind, head, rows):
    lane = lax.broadcasted_iota(jnp.int32, (rows, PAIR), 1)
    if kind == "pair":
        return (lane < HEAD_DIM) if head % 2 == 0 else (lane >= HEAD_DIM)
    return (lane >= MLA_ROPE * head) & (lane < MLA_ROPE * (head + 1))


def _row_spec(t, cb, width=PAIR):
    return pl.BlockSpec((t, width), lambda i, cb=cb: (i, cb))


def _whole_spec(rows, cb, width=PAIR):
    return pl.BlockSpec((rows, width), lambda i, cb=cb: (0, cb), pipeline_mode=pl.Buffered(1))


def _is_pow2(x):
    return float(np.frexp(x)[0]) == 0.5


def _masked_heads(blocks, kinds, pair, dtype, scale=None):
    out = []
    for e in range(2):
        head = 2 * pair + e
        parts = [jnp.where(_lane_mask(k, head, b.shape[0]), b.astype(F32) * (1.0 if scale is None else scale),
                           0.0).astype(dtype)
                 for b, k in zip(blocks, kinds)]
        out.append(parts[0] if len(parts) == 1 else jnp.concatenate(parts, axis=1))
    return out


def _logit_reach(qh, kmax2, head):
    q32 = qh.astype(F32)
    return jnp.sqrt(jnp.sum(q32 * q32, axis=1, keepdims=True) * _col(kmax2, head)) * REACH_SLACK


def _forget_top(ft_ref, head, off):
    return jnp.max(-ft_ref[head:head + 1, pl.ds(off, PAIR)])


def _col(block, idx):
    lane = lax.broadcasted_iota(jnp.int32, block.shape, 1)
    return jnp.sum(jnp.where(lane == idx, block, 0.0), axis=1, keepdims=True)


def _scatter_cols(cols, t):
    lane = lax.broadcasted_iota(jnp.int32, (t, PAIR), 1)
    out = jnp.zeros((t, PAIR), F32)
    for idx, c in cols.items():
        out = out + jnp.where(lane == idx, c, 0.0)
    return out


def _take_heads(per_head, pair):
    return jnp.where(_lane_mask("pair", 0, per_head[0].shape[0]), per_head[0], per_head[1])


class _Parts:
    def __init__(self, q_parts, k_parts, v_parts, tq, sk):
        self.kinds = [[kind for _, _, kind in q_parts[p]] for p in range(2)]
        self.nparts = len(q_parts[0])
        self.q_specs = [_row_spec(tq, cb) for p in range(2) for _, cb, _ in q_parts[p]]
        self.q_args = [a for p in range(2) for a, _, _ in q_parts[p]]
        self.k_specs = [_whole_spec(sk, cb) for p in range(2) for _, cb, _ in k_parts[p]]
        self.k_args = [a for p in range(2) for a, _, _ in k_parts[p]]
        self.v_specs = [_whole_spec(sk, cb) for _, cb in v_parts]
        self.v_args = [a for a, _ in v_parts]
        self.width = PAIR * self.nparts

    def split(self, refs):
        n = self.nparts
        refs = list(refs)
        q = [refs[p * n:(p + 1) * n] for p in range(2)]
        k = [refs[2 * n + p * n:2 * n + (p + 1) * n] for p in range(2)]
        v = refs[4 * n:4 * n + 2]
        return q, k, v, refs[4 * n + 2:]

    def k_block(self, k_refs, off, t):
        blks = [r[pl.ds(off, t), :] for r in k_refs]
        return blks[0] if len(blks) == 1 else jnp.concatenate(blks, axis=1)


def _key_norm_max(src, k_blk):
    rows = src.shape[0]

    def body(i, k0_ref, k1_ref, o_ref):
        @pl.when(i == 0)
        def _():
            o_ref[...] = jnp.zeros_like(o_ref)

        cols = {}
        for p, ref in enumerate((k0_ref, k1_ref)):
            k32 = ref[...].astype(F32)
            for e in range(2):
                sq = jnp.sum(jnp.where(_lane_mask("pair", e, k32.shape[0]), k32 * k32, 0.0), axis=1, keepdims=True)
                cols[2 * p + e] = jnp.max(sq, axis=0, keepdims=True)
        lane = lax.broadcasted_iota(jnp.int32, (1, PAIR), 1)
        o_ref[...] = jnp.maximum(o_ref[...], sum(jnp.where(lane == h, c, 0.0) for h, c in cols.items()))

    return _rowwise(body, "key_norm_max", rows, ROW_TILE, [(src, PAIR, k_blk), (src, PAIR, k_blk + 1)], [], [],
                    [((1, PAIR), F32)])[0]


def _softmax_fwd(q_parts, k_parts, v_parts, sq, sk, scale, causal, bias, name):
    tq = min(SOFTMAX_TILE, sq)
    tk = tq if causal else min(SOFTMAX_TILE, sk)
    nkv = sk // tk
    pp = _Parts(q_parts, k_parts, v_parts, tq, sk)

    def kern(*refs):
        q_refs, k_refs, v_refs, rest = pp.split(refs)
        if bias is not None:
            fc_ref, ft_ref, kmax_ref, o_ref, lse_ref = rest
            fcb = fc_ref[...]
        else:
            o_ref, lse_ref = rest
        i = pl.program_id(0)
        fold = _is_pow2(scale)
        head_on = [jnp.where(_lane_mask("pair", e, tk), 1.0, 0.0).astype(MXU_DTYPE) for e in range(2)]
        head_off = [jnp.where(_lane_mask("pair", e, tk), 0.0, 1.0).astype(MXU_DTYPE) for e in range(2)]
        lse_cols = {}
        for p in range(2):
            qm = _masked_heads([r[...] for r in q_refs[p]], pp.kinds[p], p, MXU_DTYPE, scale if fold else None)
            if bias is not None:
                reach = [_logit_reach(qm[e], kmax_ref[...], 2 * p + e) for e in range(2)]

            def block(j, carry, masked, p=p, qm=qm):
                off = _blk_off(j, tk)
                kb = pp.k_block(k_refs[p], off, tk)
                vb = v_refs[p][pl.ds(off, tk), :]
                out = []
                for e in range(2):
                    h = 2 * p + e
                    m, acc = carry[e]
                    s = _dot_nt(qm[e], kb)
                    if not fold:
                        s = s * scale
                    if bias is not None:
                        s = s - ft_ref[h:h + 1, pl.ds(off, tk)]
                    if masked:
                        s = jnp.where(_causal_mask(tq, False), s, NEG)
                    m_new = jnp.maximum(m, jnp.max(s, axis=1, keepdims=True))
                    pr = jnp.exp(s - m_new).astype(MXU_DTYPE)
                    out.append((m_new, jnp.exp(m - m_new) * acc + _dot(pr, vb * head_on[e] + head_off[e])))
                return tuple(out)

            carry = tuple((jnp.full((tq, 1), NEG, F32), jnp.zeros((tq, PAIR), F32)) for _ in range(2))
            if causal and bias is not None:
                def alive(c, j, p=p, reach=reach):
                    top = [_forget_top(ft_ref, 2 * p + e, _blk_off(j, tk)) for e in range(2)]
                    return functools.reduce(jnp.maximum, [jnp.max(reach[e] + top[e] - c[e][0]) for e in range(2)])

                def step(state, block=block, alive=alive):
                    n, _, c = state
                    c = block(i - 1 - n, c, False)
                    return n + 1, alive(c, i - 1 - n), c

                carry = block(i, carry, True)
                _, _, carry = lax.while_loop(lambda st: jnp.logical_and(st[0] < i, st[1] > DEAD_LOGIT), step,
                                             (jnp.int32(0), alive(carry, i), carry))
            elif causal:
                carry = lax.fori_loop(0, i, lambda j, c, block=block: block(j, c, False), carry)
                carry = block(i, carry, True)
            else:
                for j in range(nkv):
                    carry = block(j, carry, False)
            outs = []
            for e in range(2):
                m, acc = carry[e]
                l = _col(acc, HEAD_DIM * (1 - e))
                outs.append(acc / l)
                lse_cols[2 * p + e] = m + jnp.log(l) + (_col(fcb, 2 * p + e) if bias is not None else 0.0)
            o_ref[:, p * PAIR:(p + 1) * PAIR] = _take_heads(outs, p)
        lse_ref[...] = _scatter_cols(lse_cols, tq)

    in_specs = pp.q_specs + pp.k_specs + pp.v_specs
    args = pp.q_args + pp.k_args + pp.v_args
    if bias is not None:
        in_specs += [_row_spec(tq, 0), pl.BlockSpec((8, sk), lambda i: (0, 0), pipeline_mode=pl.Buffered(1)),
                     pl.BlockSpec((1, PAIR), lambda i: (0, 0))]
        args += list(bias)
    return pl.pallas_call(
        kern, name=name, grid=(sq // tq,), in_specs=in_specs,
        out_specs=[_row_spec(tq, 0, GROUP_W), _row_spec(tq, 0)],
        out_shape=[jax.ShapeDtypeStruct((sq, GROUP_W), F32), jax.ShapeDtypeStruct((sq, PAIR), F32)],
        compiler_params=_att_params(True),
    )(*args)


def _softmax_bwd(q_parts, k_parts, v_parts, o, lse, do, do_blk, sq, sk, scale, causal, bias, name):
    tq = min(SOFTMAX_TILE, sq)
    tk = tq if causal else min(SOFTMAX_TILE, sk)
    nkv = sk // tk
    pp = _Parts(q_parts, k_parts, v_parts, tq, sk)
    quad = pp.nparts == 2
    wq = GROUP_W + (PAIR if quad else 0)

    def kern(*refs):
        q_refs, k_refs, v_refs, rest = pp.split(refs)
        if bias is not None:
            o_ref, lse_ref, do_ref, fc_ref, ft_ref, kmax_ref, dq_ref, dk_ref, dv_ref, dfq_ref, dfk_ref = rest
            fcb = fc_ref[...]
        else:
            o_ref, lse_ref, do_ref, dq_ref, dk_ref, dv_ref = rest
        i = pl.program_id(0)
        fold = _is_pow2(scale)

        @pl.when(i == 0)
        def _():
            dk_ref[...] = jnp.zeros_like(dk_ref)
            dv_ref[...] = jnp.zeros_like(dv_ref)
            if bias is not None:
                dfk_ref[...] = jnp.zeros_like(dfk_ref)

        lse_b = lse_ref[...]
        qm, dom, delta, lse_h = [], [], [], []
        for p in range(2):
            qm += _masked_heads([r[...] for r in q_refs[p]], pp.kinds[p], p, MXU_DTYPE, scale if fold else None)
            do_p = do_ref[:, p * PAIR:(p + 1) * PAIR]
            dom += _masked_heads([do_p], ["pair"], p, MXU_DTYPE)
            prod = do_p * o_ref[:, p * PAIR:(p + 1) * PAIR]
            for e in range(2):
                h = 2 * p + e
                delta.append(jnp.sum(jnp.where(_lane_mask("pair", h, tq), prod, 0.0), axis=1, keepdims=True))
                lse_h.append(_col(lse_b, h) - (_col(fcb, h) if bias is not None else 0.0))

        def block(j, carry, masked):
            off = _blk_off(j, tk)
            out = []
            for p in range(2):
                kb = pp.k_block(k_refs[p], off, tk)
                vb = v_refs[p][pl.ds(off, tk), :]
                dk_acc = jnp.zeros((tk, pp.width), F32)
                dv_acc = jnp.zeros((tk, PAIR), F32)
                for e in range(2):
                    h = 2 * p + e
                    dq, dfq = carry[h]
                    s = _dot_nt(qm[h], kb)
                    if not fold:
                        s = s * scale
                    if bias is not None:
                        s = s - ft_ref[h:h + 1, pl.ds(off, tk)]
                    if masked:
                        s = jnp.where(_causal_mask(tq, False), s, NEG)
                    pr = jnp.exp(s - lse_h[h])
                    ds = pr * (_dot_nt(dom[h], vb) - delta[h])
                    dsb = (ds if fold else ds * scale).astype(MXU_DTYPE)
                    dv_acc = dv_acc + _dot_tn(pr.astype(MXU_DTYPE), dom[h])
                    dk_acc = dk_acc + _dot_tn(dsb, qm[h])
                    dq = dq + _dot(dsb, kb)
                    if bias is not None:
                        dfq = dfq + jnp.sum(ds, axis=1, keepdims=True)
                        dfk_ref[h:h + 1, pl.ds(off, tk)] -= jnp.sum(ds, axis=0, keepdims=True)
                    out.append((dq, dfq))
                dv_ref[pl.ds(off, tk), p * PAIR:(p + 1) * PAIR] += dv_acc
                dk_ref[pl.ds(off, tk), p * PAIR:(p + 1) * PAIR] += dk_acc[:, :PAIR]
                if quad:
                    dk_ref[pl.ds(off, tk), GROUP_W:] += dk_acc[:, PAIR:]
            return tuple(out)

        carry = tuple((jnp.zeros((tq, pp.width), F32), jnp.zeros((tq, 1), F32)) for _ in range(N_HEADS))
        if causal and bias is not None:
            reach = [_logit_reach(qm[h], kmax_ref[...], h) - lse_h[h] for h in range(N_HEADS)]

            def alive(j):
                return functools.reduce(jnp.maximum, [jnp.max(reach[h] + _forget_top(ft_ref, h, _blk_off(j, tk)))
                                                      for h in range(N_HEADS)])

            def step(state):
                n, _, c = state
                return n + 1, alive(i - 1 - n), block(i - 1 - n, c, False)

            carry = block(i, carry, True)
            _, _, carry = lax.while_loop(lambda st: jnp.logical_and(st[0] < i, st[1] > DEAD_LOGIT), step,
                                         (jnp.int32(0), alive(i), carry))
        elif causal:
            carry = lax.fori_loop(0, i, lambda j, c: block(j, c, False), carry)
            carry = block(i, carry, True)
        else:
            for j in range(nkv):
                carry = block(j, carry, False)
        dqs = [c[0] * scale if fold else c[0] for c in carry]
        for p in range(2):
            dq_ref[:, p * PAIR:(p + 1) * PAIR] = _take_heads([dqs[2 * p + e][:, :PAIR] for e in range(2)], p)
        if quad:
            dq_ref[:, GROUP_W:] = sum(jnp.where(_lane_mask("quad", h, tq), dqs[h][:, PAIR:], 0.0)
                                      for h in range(N_HEADS))
        if bias is not None:
            dfq_ref[...] = _scatter_cols({h: carry[h][1] for h in range(N_HEADS)}, tq)

    acc_spec = lambda rows, width: pl.BlockSpec((rows, width), lambda i: (0, 0), pipeline_mode=pl.Buffered(1))
    in_specs = pp.q_specs + pp.k_specs + pp.v_specs + [_row_spec(tq, 0, GROUP_W), _row_spec(tq, 0),
                                                       _row_spec(tq, do_blk, GROUP_W)]
    args = pp.q_args + pp.k_args + pp.v_args + [o, lse, do]
    out_specs = [_row_spec(tq, 0, wq), acc_spec(sk, wq), acc_spec(sk, GROUP_W)]
    out_shape = [jax.ShapeDtypeStruct((sq, wq), F32), jax.ShapeDtypeStruct((sk, wq), F32),
                 jax.ShapeDtypeStruct((sk, GROUP_W), F32)]
    if bias is not None:
        in_specs += [_row_spec(tq, 0), pl.BlockSpec((8, sk), lambda i: (0, 0), pipeline_mode=pl.Buffered(1)),
                     pl.BlockSpec((1, PAIR), lambda i: (0, 0))]
        args += list(bias)
        out_specs += [_row_spec(tq, 0), acc_spec(8, sk)]
        out_shape += [jax.ShapeDtypeStruct((sq, PAIR), F32), jax.ShapeDtypeStruct((8, sk), F32)]
    return pl.pallas_call(
        kern, name=name, grid=(sq // tq,), in_specs=in_specs, out_specs=out_specs, out_shape=out_shape,
        compiler_params=_att_params(False),
    )(*args)


def _sb_logs(qh, kb, valid):
    z = _dot_nt(qh, kb)
    sp = _softplus(z)
    lk = -sp
    if valid is not None:
        lk = jnp.where(valid, lk, 0.0)
    return lk, z - sp


def _sb_valid(d, tq, tk):
    r = lax.broadcasted_iota(jnp.int32, (tq, tk), 0)
    c = lax.broadcasted_iota(jnp.int32, (tq, tk), 1)
    return c + d * tk < r


def _tri_sums(xs, tri):
    t = xs[0].shape[0]
    pieces = [_split2(x) for x in xs]
    hi = _dot(jnp.concatenate([pc[0] for pc in pieces], axis=0), tri)
    lo = _dot(jnp.concatenate([pc[1] for pc in pieces], axis=0), tri)
    return [hi[n * t:(n + 1) * t] + lo[n * t:(n + 1) * t] for n in range(len(xs))]


def _sb_fwd(src, q_blk, k_blk, v_blk, s, scale, name):
    assert _is_pow2(scale) and s // ATT_TILE <= SB_SLOT
    tq, t = min(SB_QUERY_TILE, s), min(ATT_TILE, s)
    band = tq // t
    pair = lambda blk: [[(src, blk + p, "pair")] for p in range(2)]
    pp = _Parts(pair(q_blk), pair(k_blk), [(src, v_blk + p) for p in range(2)], tq, s)

    def kern(*refs):
        q_refs, k_refs, v_refs, (tri_ref, o_ref, rm_ref, cnt_ref) = pp.split(refs)
        i = pl.program_id(0)
        tri = tri_ref[...]
        lane = lax.broadcasted_iota(jnp.int32, (tq, PAIR), 1)
        qm = []
        for p in range(2):
            qm += _masked_heads([q_refs[p][0][...]], ["pair"], p, MXU_DTYPE, scale)

        def block(j, carry, valid):
            accs, rights, rm = carry
            off = _blk_off(j, t)
            kbs = [k_refs[p][0][pl.ds(off, t), :] for p in range(2)]
            vbs = [v_refs[p][pl.ds(off, t), :] for p in range(2)]
            logs = [_sb_logs(qm[h], kbs[h // 2], valid) for h in range(N_HEADS)]
            tails = _tri_sums([lg[0] for lg in logs], tri)
            new_acc, new_right = [], []
            for h in range(N_HEADS):
                lk, ls = logs[h]
                w = jnp.exp(ls + tails[h] + rights[h])
                if valid is not None:
                    w = jnp.where(valid, w, 0.0)
                new_acc.append(accs[h] + _dot(w.astype(MXU_DTYPE), vbs[h // 2]))
                rm = rm + jnp.where(lane == SB_SLOT * h + j, rights[h], 0.0)
                new_right.append(rights[h] + jnp.sum(lk, axis=1, keepdims=True))
            return tuple(new_acc), tuple(new_right), rm

        carry = (tuple(jnp.zeros((tq, PAIR), F32) for _ in range(N_HEADS)),
                 tuple(jnp.zeros((tq, 1), F32) for _ in range(N_HEADS)), jnp.zeros((tq, PAIR), F32))
        for d in reversed(range(band)):
            carry = block(band * i + d, carry, _sb_valid(d, tq, t))

        def alive(c):
            return functools.reduce(jnp.maximum, [jnp.max(r) for r in c[1]])

        def step(state):
            n, _, c = state
            c = block(band * i - 1 - n, c, None)
            return n + 1, alive(c), c

        n_done, _, carry = lax.while_loop(lambda st: jnp.logical_and(st[0] < band * i, st[1] > EXP_UNDERFLOW),
                                          step, (jnp.int32(0), alive(carry), carry))
        cnt_ref[i] = n_done
        for p in range(2):
            o_ref[:, p * PAIR:(p + 1) * PAIR] = _take_heads([carry[0][2 * p + e] for e in range(2)], p)
        rm_ref[...] = carry[2]

    return pl.pallas_call(
        kern, name=name, grid=(s // tq,),
        in_specs=pp.q_specs + pp.k_specs + pp.v_specs + [pl.BlockSpec((t, t), lambda i: (0, 0))],
        out_specs=[_row_spec(tq, 0, GROUP_W), _row_spec(tq, 0), pl.BlockSpec(memory_space=pltpu.SMEM)],
        out_shape=[jax.ShapeDtypeStruct((s, GROUP_W), F32), jax.ShapeDtypeStruct((s, PAIR), F32),
                   jax.ShapeDtypeStruct((s // tq,), jnp.int32)],
        compiler_params=_att_params(False),
    )(*(pp.q_args + pp.k_args + pp.v_args + [_tri(t, "row_gt_col")]))


def _sb_bwd(src, q_blk, k_blk, v_blk, do, do_blk, rm, visited, s, scale, name):
    assert _is_pow2(scale)
    tq, t = min(SB_QUERY_TILE, s), min(ATT_TILE, s)
    band = tq // t
    pair = lambda blk: [[(src, blk + p, "pair")] for p in range(2)]
    pp = _Parts(pair(q_blk), pair(k_blk), [(src, v_blk + p) for p in range(2)], tq, s)

    def kern(*refs):
        q_refs, k_refs, v_refs, (do_ref, rm_ref, tri_ref, pre_ref, cnt_ref, dq_ref, dk_ref, dv_ref) = pp.split(refs)
        i = pl.program_id(0)

        @pl.when(i == 0)
        def _():
            dk_ref[...] = jnp.zeros_like(dk_ref)
            dv_ref[...] = jnp.zeros_like(dv_ref)

        rmb = rm_ref[...]
        tri = tri_ref[...]
        pre = pre_ref[...]
        qm, dom = [], []
        for p in range(2):
            qm += _masked_heads([q_refs[p][0][...]], ["pair"], p, MXU_DTYPE, scale)
            dom += _masked_heads([do_ref[:, p * PAIR:(p + 1) * PAIR]], ["pair"], p, MXU_DTYPE)

        def block(j, carry, valid):
            dqs, lefts = carry
            off = _blk_off(j, t)
            kbs = [k_refs[p][0][pl.ds(off, t), :] for p in range(2)]
            vbs = [v_refs[p][pl.ds(off, t), :] for p in range(2)]
            logs = [_sb_logs(qm[h], kbs[h // 2], valid) for h in range(N_HEADS)]
            tails = _tri_sums([lg[0] for lg in logs], tri)
            ws, gs = [], []
            for h in range(N_HEADS):
                lk, ls = logs[h]
                w = jnp.exp(ls + tails[h] + _col(rmb, SB_SLOT * h + j))
                if valid is not None:
                    w = jnp.where(valid, w, 0.0)
                ws.append(w)
                gs.append(_dot_nt(dom[h], vbs[h // 2]) * w)
            prefix = _tri_sums(gs, pre)
            new_dq, new_left = [], []
            dk_acc = [jnp.zeros((t, PAIR), F32) for _ in range(2)]
            dv_acc = [jnp.zeros((t, PAIR), F32) for _ in range(2)]
            for h in range(N_HEADS):
                lk, ls = logs[h]
                sig = jnp.exp(ls)
                dz = gs[h] * (1.0 - sig) - sig * (prefix[h] + lefts[h])
                if valid is not None:
                    dz = jnp.where(valid, dz, 0.0)
                dzb = dz.astype(MXU_DTYPE)
                dv_acc[h // 2] = dv_acc[h // 2] + _dot_tn(ws[h].astype(MXU_DTYPE), dom[h])
                dk_acc[h // 2] = dk_acc[h // 2] + _dot_tn(dzb, qm[h])
                new_dq.append(dqs[h] + _dot(dzb, kbs[h // 2]))
                new_left.append(lefts[h] + jnp.sum(gs[h], axis=1, keepdims=True))
            for p in range(2):
                dv_ref[pl.ds(off, t), p * PAIR:(p + 1) * PAIR] += dv_acc[p]
                dk_ref[pl.ds(off, t), p * PAIR:(p + 1) * PAIR] += dk_acc[p]
            return tuple(new_dq), tuple(new_left)

        carry = (tuple(jnp.zeros((tq, PAIR), F32) for _ in range(N_HEADS)),
                 tuple(jnp.zeros((tq, 1), F32) for _ in range(N_HEADS)))
        carry = lax.fori_loop(band * i - cnt_ref[i], band * i, lambda j, c: block(j, c, None), carry)
        for d in range(band):
            carry = block(band * i + d, carry, _sb_valid(d, tq, t))
        for p in range(2):
            dq_ref[:, p * PAIR:(p + 1) * PAIR] = _take_heads([carry[0][2 * p + e] * scale for e in range(2)], p)

    mspec = pl.BlockSpec((t, t), lambda i: (0, 0))
    acc_spec = pl.BlockSpec((s, GROUP_W), lambda i: (0, 0), pipeline_mode=pl.Buffered(1))
    return pl.pallas_call(
        kern, name=name, grid=(s // tq,),
        in_specs=pp.q_specs + pp.k_specs + pp.v_specs + [_row_spec(tq, do_blk, GROUP_W), _row_spec(tq, 0), mspec, mspec,
                                                         pl.BlockSpec(memory_space=pltpu.SMEM)],
        out_specs=[_row_spec(tq, 0, GROUP_W), acc_spec, acc_spec],
        out_shape=[jax.ShapeDtypeStruct((s, GROUP_W), F32)] * 3,
        compiler_params=_att_params(False),
    )(*(pp.q_args + pp.k_args + pp.v_args + [do, rm, _tri(t, "row_gt_col"), _tri(t, "row_lt_col"), visited]))


def _split_w_in(w):
    col = lambda n: w[:, _OFF[n]:_OFF[n + 1]]
    wa = jnp.concatenate([col(0), col(1), col(2), col(4), col(5), col(6), col(10)], axis=1)
    misc = jnp.concatenate([col(3), col(9), jnp.zeros((w.shape[0], 128 - 4 - MLA_ROPE), w.dtype)], axis=1)
    wb = jnp.concatenate([col(11), col(7), col(8), misc], axis=1)
    return wa, wb


def _merge_dw_in(dwp):
    a = lambda n: dwp[:, n * GROUP_W:(n + 1) * GROUP_W]
    b0 = PA_COLS
    gate = dwp[:, b0:b0 + 1024]
    cq = dwp[:, b0 + 1024:b0 + 1280]
    ckv = dwp[:, b0 + 1280:b0 + 1408]
    flog = dwp[:, b0 + 1408:b0 + 1412]
    krot = dwp[:, b0 + 1408 + MISC_KROT:b0 + 1408 + MISC_KROT + MLA_ROPE]
    return jnp.concatenate([a(0), a(1), a(2), flog, a(3), a(4), a(5), cq, ckv, krot, a(6), gate], axis=1)


def _heads_first(w, per_head, first):
    r = w.shape[0]
    w3 = w.reshape(r, N_HEADS, per_head)
    return jnp.concatenate([w3[:, :, :first].reshape(r, -1), w3[:, :, first:].reshape(r, -1)], axis=1)


def _heads_interleaved(w, per_head, first):
    r = w.shape[0]
    a = w[:, :N_HEADS * first].reshape(r, N_HEADS, first)
    b = w[:, N_HEADS * first:].reshape(r, N_HEADS, per_head - first)
    return jnp.concatenate([a, b], axis=2).reshape(r, N_HEADS * per_head)


def _pad_rows8(a):
    return a[:, :8].T


def _local_step(x2, mem2, tgt, p):
    s = x2.shape[0]
    nm = mem2.shape[0]
    head_scale = HEAD_DIM ** -0.5
    mla_scale = (HEAD_DIM + MLA_ROPE) ** -0.5
    tabs = _rope_tables(s)
    mats = _rope_matrices()
    pairs = lambda arr, blk: [[(arr, blk + q, "pair")] for q in range(2)]
    vals = lambda arr, blk: [(arr, blk + q) for q in range(2)]

    h, hb = _ln_fwd(x2, None, p["ln_in_g"], p["ln_in_b"], "ln_in_fwd")
    _, memn_b = _ln_fwd(mem2, None, p["mem_ln_g"], p["mem_ln_b"], "ln_mem_fwd")

    saved = []
    for l in range(DEPTH):
        wa, wb = _split_w_in(p["w_in"][l])
        wp = jnp.concatenate([wa, wb], axis=1)
        wq_up = _heads_first(p["w_mla_q_up"][l], HEAD_DIM + MLA_ROPE, HEAD_DIM)
        wkv_up = _heads_first(p["w_mla_kv_up"][l], 2 * HEAD_DIM, HEAD_DIM)
        bias_row = jnp.pad(p["b_forget"][l], (0, 128 - N_HEADS)).reshape(1, 128)
        pa = _matmul(hb, wa, MXU_DTYPE, "proj_a")
        pb = _matmul(hb, wb, F32, "proj_b")

        fc = _forget_fwd(pb, bias_row)
        fbias = (fc, _pad_rows8(fc), _key_norm_max(pa, 2))
        o_fox, lse_fox = _softmax_fwd(pairs(pa, 0), pairs(pa, 2), vals(pa, 4), s, s, head_scale, True, fbias,
                                      "fox_fwd")
        o_sb, *rm_sb = _sb_fwd(pa, 6, 8, 10, s, head_scale, "sb_fwd")

        qfull, cqn = _mla_q_fwd(pb, p["mla_q_norm_g"][l], wq_up, tabs, mats)
        kv, ckvn, kr4 = _mla_kv_fwd(pb, p["mla_kv_norm_g"][l], wkv_up, tabs, mats)
        mla_q = [[(qfull, q, "pair"), (qfull, 2, "quad")] for q in range(2)]
        mla_k = [[(kv, q, "pair"), (kr4, 0, "quad")] for q in range(2)]
        o_mla, lse_mla = _softmax_fwd(mla_q, mla_k, vals(kv, 2), s, s, mla_scale, True, None, "mla_fwd")

        mkv = _matmul(memn_b, p["w_mem_kv"][l], MXU_DTYPE, "mem_kv")
        o_mem, lse_mem = _softmax_fwd(pairs(pa, 12), pairs(mkv, 0), vals(mkv, 2), s, nm, head_scale, False, None,
                                      "mem_fwd")

        groups = (o_fox, o_sb, o_mla, o_mem)
        gated = _gate_fwd(groups, pb)
        y = _matmul(gated, p["w_out"][l], F32, "out_proj")
        saved.append(dict(h=h, hb=hb, y=y, wp=wp, wq_up=wq_up, wkv_up=wkv_up, bias_row=bias_row, pa=pa, pb=pb,
                          fbias=fbias, lse_fox=lse_fox, rm_sb=rm_sb, cqn=cqn, ckvn=ckvn, mla_q=mla_q, mla_k=mla_k,
                          kv=kv, lse_mla=lse_mla, mkv=mkv, lse_mem=lse_mem, groups=groups, gated=gated))
        h, hb = _ln_fwd(h, y, p["ln_g"][l], p["ln_b"][l], "ln_fwd")

    dh, sq_cols = _loss_grad(h, tgt)
    loss_sum = jnp.sum(sq_cols)

    grads = {k: [None] * DEPTH for k in ("w_in", "b_forget", "mla_q_norm_g", "w_mla_q_up", "mla_kv_norm_g",
                                         "w_mla_kv_up", "w_mem_kv", "w_out", "ln_g", "ln_b")}
    dmemn = []
    dy1, dy2, c1 = dh, None, 1.0
    for l in reversed(range(DEPTH)):
        r = saved[l]
        pa, pb = r["pa"], r["pb"]
        o_fox, o_sb, o_mla, o_mem = r["groups"]
        du, du_b, dg, db = _ln_bwd(r["h"], r["y"], p["ln_g"][l], dy1, dy2, c1, "ln_bwd")
        grads["ln_g"][l], grads["ln_b"][l] = dg[0], db[0]
        dgated = _matmul(du_b, p["w_out"][l], F32, "out_proj_dx", "nt")
        grads["w_out"][l] = _matmul(r["gated"], du_b, F32, "out_proj_dw", "tn")
        dmixed, dgate_b = _gate_bwd(dgated, r["groups"], pb)

        dfq, dfk, dfv, dfc_q, dfc_k = _softmax_bwd(pairs(pa, 0), pairs(pa, 2), vals(pa, 4), o_fox, r["lse_fox"],
                                                   dmixed, 0, s, s, head_scale, True, r["fbias"], "fox_bwd")
        dmisc_f, dbf = _forget_bwd(pb, r["bias_row"], dfc_q + jnp.pad(dfc_k.T, ((0, 0), (0, 128 - 8))))
        grads["b_forget"][l] = dbf[0, :N_HEADS]

        dsq, dsk, dsv = _sb_bwd(pa, 6, 8, 10, dmixed, 1, *r["rm_sb"], s, head_scale, "sb_bwd")

        dqm, dkm, dvm = _softmax_bwd(r["mla_q"], r["mla_k"], vals(r["kv"], 2), o_mla, r["lse_mla"], dmixed, 2,
                                     s, s, mla_scale, True, None, "mla_bwd")
        dq_mla_b = _rope_q_bwd(dqm, tabs, mats)
        dcqn = _matmul(dq_mla_b, r["wq_up"], F32, "q_up_dx", "nt")
        grads["w_mla_q_up"][l] = _heads_interleaved(_matmul(r["cqn"], dq_mla_b, F32, "q_up_dw", "tn"),
                                                    HEAD_DIM + MLA_ROPE, HEAD_DIM)
        dcq_b, dgq = _rms_bwd(pb, MLA_Q_RANK, PB_CQ_BLK, p["mla_q_norm_g"][l], dcqn, "rms_q_bwd")
        grads["mla_q_norm_g"][l] = dgq[0]
        dkv_b = jnp.concatenate([dkm[:, :GROUP_W], dvm], axis=1).astype(MXU_DTYPE)
        dckvn = _matmul(dkv_b, r["wkv_up"], F32, "kv_up_dx", "nt")
        grads["w_mla_kv_up"][l] = _heads_interleaved(_matmul(r["ckvn"], dkv_b, F32, "kv_up_dw", "tn"),
                                                     2 * HEAD_DIM, HEAD_DIM)
        dckv_b, dgkv = _rms_bwd(pb, MLA_KV_RANK, PB_CKV_BLK, p["mla_kv_norm_g"][l], dckvn, "rms_kv_bwd")
        grads["mla_kv_norm_g"][l] = dgkv[0]
        dmisc_k = _rope_k_bwd(dkm[:, GROUP_W:], tabs, mats)

        dmq, dmk, dmv = _softmax_bwd(pairs(pa, 12), pairs(r["mkv"], 0), vals(r["mkv"], 2), o_mem, r["lse_mem"],
                                     dmixed, 3, s, nm, head_scale, False, None, "mem_bwd")
        dmkv_b = jnp.concatenate([dmk, dmv], axis=1).astype(MXU_DTYPE)
        grads["w_mem_kv"][l] = _matmul(memn_b, dmkv_b, F32, "mem_kv_dw", "tn")
        dmemn.append(_matmul(dmkv_b, p["w_mem_kv"][l], F32, "mem_kv_dx", "nt"))

        dp = [dfq, dfk, dfv, dsq, dsk, dsv, dmq, dgate_b, dcq_b, dckv_b, (dmisc_f, dmisc_k)]
        dhproj = _proj_dx(dp, r["wp"])
        grads["w_in"][l] = _merge_dw_in(_proj_dw(r["hb"], dp))
        dy1, dy2, c1 = du, dhproj, ALPHA

    dx, _, dg_in, db_in = _ln_bwd(x2, None, p["ln_in_g"], dy1, dy2, c1, "ln_in_bwd")
    _, _, dg_mem, db_mem = _ln_bwd(mem2, None, p["mem_ln_g"], dmemn[0], dmemn[1], 1.0, "ln_mem_bwd")
    out = {k: jnp.stack(v) for k, v in grads.items()}
    out.update(ln_in_g=dg_in[0], ln_in_b=db_in[0], mem_ln_g=dg_mem[0], mem_ln_b=db_mem[0])
    return loss_sum, dx, out


WIDE = "w_in"
FLAT_NAMES = ("w_out", "w_mem_kv", "w_mla_q_up", "w_mla_kv_up")
FLAT_ROWS = 896
BIG_NAMES = (WIDE,) + FLAT_NAMES
BIG_AXIS = dict(w_in=2, w_out=1, w_mem_kv=1, w_mla_q_up=2, w_mla_kv_up=2)
SMALL_NAMES = ("ln_in_g", "ln_in_b", "mem_ln_g", "mem_ln_b", "ln_g", "ln_b", "b_forget", "mla_q_norm_g",
               "mla_kv_norm_g")
ALL_NAMES = ("ln_in_g", "ln_in_b", "mem_ln_g", "mem_ln_b", "w_in", "b_forget", "mla_q_norm_g", "w_mla_q_up",
             "mla_kv_norm_g", "w_mla_kv_up", "w_mem_kv", "w_out", "ln_g", "ln_b")
N_CHIPS = 4
N_DEV = 8


def _rows_of(shape):
    rows = -(-int(np.prod(shape)) // LANES)
    return -(-rows // PACK_ALIGN) * PACK_ALIGN


def _pack(arrs, rows):
    parts = []
    for a in arrs:
        f = a.reshape(-1)
        n = _rows_of(a.shape) * LANES
        parts.append(jnp.pad(f, (0, n - f.shape[0])).reshape(-1, LANES))
    used = sum(q.shape[0] for q in parts)
    if rows > used:
        parts.append(jnp.zeros((rows - used, LANES), parts[0].dtype))
    return jnp.concatenate(parts, axis=0)


def _unpack(buf, shapes):
    out, r = [], 0
    for shp in shapes:
        n = _rows_of(shp)
        out.append(buf[r:r + n].reshape(-1)[:int(np.prod(shp))].reshape(shp))
        r += n
    return out


def _sharded_pair(get):
    wide = get(WIDE)
    return [wide.reshape(-1, wide.shape[-1]), _pack([get(n) for n in FLAT_NAMES], FLAT_ROWS)]


HBM_SPEC = pl.BlockSpec(memory_space=pltpu.HBM)
SWAP_PIECES = 8


def _gather_weights(shards):
    n = len(shards)

    def body(*refs):
        w_refs, out_refs, (send_sems, recv_sems, local_sems) = refs[:n], refs[n:2 * n], refs[2 * n:]
        x, y, c = (lax.axis_index(a) for a in MESH_AXES)
        me, sibling = 2 * x + y, (x, y, 1 - c)
        chips = [(1 - x, y), (x, 1 - y), (1 - x, 1 - y)]
        local, first, passed = [], [], []
        for a in range(n):
            w_ref, out_ref, half = w_refs[a], out_refs[a], shards[a].shape[0] // 2

            def part(chip, core, out_ref=out_ref, half=half):
                return out_ref.at[chip, pl.ds(core * half, half)]

            def copy(k, src, dst, to, a=a):
                return pltpu.make_async_remote_copy(
                    src_ref=src, dst_ref=dst, send_sem=send_sems.at[6 * a + k], recv_sem=recv_sems.at[6 * a + k],
                    device_id=to, device_id_type=pl.DeviceIdType.MESH)

            local.append(pltpu.make_async_copy(w_ref, out_ref.at[me], local_sems.at[a]))
            local[-1].start()
            mine = [copy(k, w_ref.at[pl.ds(c * half, half)], part(me, c), (px, py, c))
                    for k, (px, py) in enumerate(chips)]
            for cp in mine:
                cp.start()
            first.append((mine, part, copy))
        for mine, part, copy in first:
            for k, (px, py) in enumerate(chips):
                copy(k, part(me, c), part(2 * px + py, c), (px, py, c)).wait_recv()
                passed.append(copy(3 + k, part(2 * px + py, c), part(2 * px + py, c), sibling))
                passed[-1].start()
        for mine, part, copy in first:
            for k, (px, py) in enumerate(chips):
                copy(3 + k, part(me, c), part(2 * px + py, 1 - c), sibling).wait_recv()
        for cp in [cp for mine, _, _ in first for cp in mine] + passed:
            cp.wait_send()
        for cp in local:
            cp.wait()

    return pl.pallas_call(
        body, name="gather_weights",
        out_shape=[jax.ShapeDtypeStruct((N_CHIPS,) + s.shape, s.dtype) for s in shards],
        in_specs=[HBM_SPEC] * n, out_specs=[HBM_SPEC] * n,
        scratch_shapes=[pltpu.SemaphoreType.DMA((6 * n,)), pltpu.SemaphoreType.DMA((6 * n,)),
                        pltpu.SemaphoreType.DMA((n,))],
    )(*shards)


def _exchange_grads(bigs, small):
    nb = len(bigs)

    def body(*refs):
        big_refs, small_ref, out_refs = refs[:nb], refs[nb], refs[nb + 1:2 * nb + 2]
        send_sems, recv_sems, local_sems = refs[2 * nb + 2:]
        x, y, c = (lax.axis_index(a) for a in MESH_AXES)
        me, my_chip = 4 * x + 2 * y + c, 2 * x + y
        flips = [(fx, fy, fc) for fx in (0, 1) for fy in (0, 1) for fc in (0, 1) if fx or fy or fc]
        peers = [(1 - x if fx else x, 1 - y if fy else y, 1 - c if fc else c) for fx, fy, fc in flips]

        sources = [lambda chip, core, r=big_refs[a], half=bigs[a].shape[1] // 2: r.at[chip, pl.ds(core * half, half)]
                   for a in range(nb)] + [lambda chip, core: small_ref]

        def copy(a, k, src, slot, to):
            return pltpu.make_async_remote_copy(
                src_ref=src, dst_ref=out_refs[a].at[slot], send_sem=send_sems.at[7 * a + k],
                recv_sem=recv_sems.at[7 * a + k], device_id=to, device_id_type=pl.DeviceIdType.MESH)

        own = [pltpu.make_async_copy(src(my_chip, c), out_refs[a].at[me], local_sems.at[a])
               for a, src in enumerate(sources)]
        for cp in own:
            cp.start()
        sends = [copy(a, k, src(2 * px + py, pc), me, (px, py, pc))
                 for a, src in enumerate(sources) for k, (px, py, pc) in enumerate(peers)]
        for cp in sends:
            cp.start()
        for a, src in enumerate(sources):
            for k, (px, py, pc) in enumerate(peers):
                copy(a, k, src(my_chip, c), 4 * px + 2 * py + pc, (px, py, pc)).wait_recv()
        for cp in sends:
            cp.wait_send()
        for cp in own:
            cp.wait()

    return pl.pallas_call(
        body, name="exchange_grads",
        out_shape=[jax.ShapeDtypeStruct((N_DEV, b.shape[1] // 2, b.shape[2]), b.dtype) for b in bigs]
        + [jax.ShapeDtypeStruct((N_DEV,) + small.shape, small.dtype)],
        in_specs=[HBM_SPEC] * (nb + 1), out_specs=[HBM_SPEC] * (nb + 1),
        scratch_shapes=[pltpu.SemaphoreType.DMA((7 * (nb + 1),)), pltpu.SemaphoreType.DMA((7 * (nb + 1),)),
                        pltpu.SemaphoreType.DMA((nb + 1,))],
    )(*bigs, small)


def _sum_parts(parts, name):
    n, rows, width = parts.shape
    tile = _pick(rows, (128, 16, 8))

    def kern(p_ref, o_ref):
        g = p_ref[0].astype(F32)
        for d in range(1, n):
            g = g + p_ref[d].astype(F32)
        o_ref[...] = g

    return pl.pallas_call(
        kern, name=name, grid=(rows // tile,),
        in_specs=[pl.BlockSpec((n, tile, width), lambda i: (0, i, 0))],
        out_specs=pl.BlockSpec((tile, width), lambda i: (i, 0)), out_shape=jax.ShapeDtypeStruct((rows, width), F32),
        compiler_params=pltpu.CompilerParams(dimension_semantics=("parallel",), vmem_limit_bytes=VMEM_LIMIT),
    )(parts)


def _swap_halves(mine):
    n = len(mine)
    pieces = [(a, k * (q.shape[0] // SWAP_PIECES), q.shape[0] // SWAP_PIECES)
              for a, q in enumerate(mine) for k in range(SWAP_PIECES)]

    def body(*refs):
        q_refs, out_refs, (send_sems, recv_sems, local_sems) = refs[:n], refs[n:2 * n], refs[2 * n:]
        x, y, c = (lax.axis_index(a) for a in MESH_AXES)

        def copy(i, core):
            a, r0, nr = pieces[i]
            return pltpu.make_async_remote_copy(
                src_ref=q_refs[a].at[pl.ds(r0, nr)], dst_ref=out_refs[a].at[core, pl.ds(r0, nr)],
                send_sem=send_sems.at[i], recv_sem=recv_sems.at[i],
                device_id=(x, y, 1 - c), device_id_type=pl.DeviceIdType.MESH)

        own = [pltpu.make_async_copy(q_refs[a], out_refs[a].at[c], local_sems.at[a]) for a in range(n)]
        swaps = [copy(i, c) for i in range(len(pieces))]
        for cp in own + swaps:
            cp.start()
        for i in range(len(pieces)):
            copy(i, 1 - c).wait_recv()
        for cp in swaps:
            cp.wait_send()
        for cp in own:
            cp.wait()

    return pl.pallas_call(
        body, name="swap_halves", out_shape=[jax.ShapeDtypeStruct((2,) + q.shape, q.dtype) for q in mine],
        in_specs=[HBM_SPEC] * n, out_specs=[HBM_SPEC] * n,
        scratch_shapes=[pltpu.SemaphoreType.DMA((len(pieces),)), pltpu.SemaphoreType.DMA((len(pieces),)),
                        pltpu.SemaphoreType.DMA((n,))],
    )(*mine)


def _adamw(parts, w, m, v, name):
    rows, width = w.shape
    n_parts = parts.shape[0]
    tile = _pick(rows, (128, 16, 8))
    bc1 = 1.0 - ADAM_B1 ** ADAM_STEP
    bc2 = 1.0 - ADAM_B2 ** ADAM_STEP

    def kern(p_ref, w_ref, m_ref, v_ref, g_ref, d_ref, nm_ref, nv_ref):
        g = p_ref[0].astype(F32)
        for d in range(1, n_parts):
            g = g + p_ref[d].astype(F32)
        nm = ADAM_B1 * m_ref[...] + (1.0 - ADAM_B1) * g
        nv = ADAM_B2 * v_ref[...] + (1.0 - ADAM_B2) * (g * g)
        g_ref[...] = g
        nm_ref[...] = nm
        nv_ref[...] = nv
        d_ref[...] = -ADAM_LR * ((nm / bc1) / (jnp.sqrt(nv / bc2) + ADAM_EPS) + ADAM_WD * w_ref[...])

    spec = pl.BlockSpec((tile, width), lambda i: (i, 0))
    return pl.pallas_call(
        kern, name=name, grid=(rows // tile,),
        in_specs=[pl.BlockSpec((n_parts, tile, width), lambda i: (0, i, 0)), spec, spec, spec],
        out_specs=[spec] * 4, out_shape=[jax.ShapeDtypeStruct((rows, width), F32)] * 4,
        compiler_params=pltpu.CompilerParams(dimension_semantics=("parallel",), vmem_limit_bytes=VMEM_LIMIT),
    )(parts, w, m, v)


def kernel(x, mem, ln_in_g, ln_in_b, mem_ln_g, mem_ln_b, w_in, b_forget, mla_q_norm_g, w_mla_q_up, mla_kv_norm_g, w_mla_kv_up, w_mem_kv, w_out, ln_g, ln_b, loss_target, m_ln_in_g, m_ln_in_b, m_mem_ln_g, m_mem_ln_b, m_w_in, m_b_forget, m_mla_q_norm_g, m_w_mla_q_up, m_mla_kv_norm_g, m_w_mla_kv_up, m_w_mem_kv, m_w_out, m_ln_g, m_ln_b, v_ln_in_g, v_ln_in_b, v_mem_ln_g, v_mem_ln_b, v_w_in, v_b_forget, v_mla_q_norm_g, v_w_mla_q_up, v_mla_kv_norm_g, v_w_mla_kv_up, v_w_mem_kv, v_w_out, v_ln_g, v_ln_b):
    w = dict(ln_in_g=ln_in_g, ln_in_b=ln_in_b, mem_ln_g=mem_ln_g, mem_ln_b=mem_ln_b, w_in=w_in, b_forget=b_forget,
             mla_q_norm_g=mla_q_norm_g, w_mla_q_up=w_mla_q_up, mla_kv_norm_g=mla_kv_norm_g,
             w_mla_kv_up=w_mla_kv_up, w_mem_kv=w_mem_kv, w_out=w_out, ln_g=ln_g, ln_b=ln_b)
    mo = dict(ln_in_g=m_ln_in_g, ln_in_b=m_ln_in_b, mem_ln_g=m_mem_ln_g, mem_ln_b=m_mem_ln_b, w_in=m_w_in,
              b_forget=m_b_forget, mla_q_norm_g=m_mla_q_norm_g, w_mla_q_up=m_w_mla_q_up,
              mla_kv_norm_g=m_mla_kv_norm_g, w_mla_kv_up=m_w_mla_kv_up, w_mem_kv=m_w_mem_kv, w_out=m_w_out,
              ln_g=m_ln_g, ln_b=m_ln_b)
    vo = dict(ln_in_g=v_ln_in_g, ln_in_b=v_ln_in_b, mem_ln_g=v_mem_ln_g, mem_ln_b=v_mem_ln_b, w_in=v_w_in,
              b_forget=v_b_forget, mla_q_norm_g=v_mla_q_norm_g, w_mla_q_up=v_w_mla_q_up,
              mla_kv_norm_g=v_mla_kv_norm_g, w_mla_kv_up=v_w_mla_kv_up, w_mem_kv=v_w_mem_kv, w_out=v_w_out,
              ln_g=v_ln_g, ln_b=v_ln_b)
    flat_shapes = [w[n].shape for n in FLAT_NAMES]
    small_shapes = [w[n].shape for n in SMALL_NAMES]

    got_wide, got_flat = _gather_weights(_sharded_pair(lambda n: w[n].astype(MXU_DTYPE)))
    full = dict(w)
    full[WIDE] = jnp.concatenate([got_wide[j] for j in range(N_CHIPS)], axis=1).reshape(
        w[WIDE].shape[:2] + (N_CHIPS * w[WIDE].shape[2],))
    per_chip = [_unpack(got_flat[j], flat_shapes) for j in range(N_CHIPS)]
    for idx, n in enumerate(FLAT_NAMES):
        full[n] = jnp.concatenate([per_chip[j][idx] for j in range(N_CHIPS)], axis=BIG_AXIS[n])

    loss_sum, dx, g = _local_step(x[0], mem[0], loss_target[0], full)
    loss = lax.psum(loss_sum * (0.5 / D_MODEL), MESH_AXES)

    def shard_of(n, j):
        ax, size = BIG_AXIS[n], w[n].shape[BIG_AXIS[n]]
        return lax.slice_in_dim(g[n], j * size, (j + 1) * size, axis=ax).astype(MXU_DTYPE)

    per_dest = [_sharded_pair(lambda n, j=j: shard_of(n, j)) for j in range(N_CHIPS)]
    bigs = [jnp.stack([per_dest[j][a] for j in range(N_CHIPS)]) for a in range(2)]
    *big_parts, small_parts = _exchange_grads(bigs, _pack([g[n] for n in SMALL_NAMES], SMALL_ROWS))
    halves = _swap_halves([_sum_parts(big_parts[0], "sum_wide"), _sum_parts(big_parts[1], "sum_flat")])

    res = []
    for a, (grad, nm) in enumerate(zip(halves, ("adamw_wide", "adamw_flat"))):
        state = [_sharded_pair(lambda n, src=src: src[n])[a] for src in (w, mo, vo)]
        res.append(_adamw(grad.reshape((1,) + state[0].shape), *state, nm))
    res_small = _adamw(small_parts, *[_pack([src[n] for n in SMALL_NAMES], SMALL_ROWS) for src in (w, mo, vo)],
                       "adamw_replicated")
    outs = []
    for kind in range(4):
        vals = {WIDE: res[0][kind].reshape(w[WIDE].shape)}
        vals.update(zip(FLAT_NAMES, _unpack(res[1][kind], flat_shapes)))
        vals.update(zip(SMALL_NAMES, _unpack(res_small[kind], small_shapes)))
        outs += [vals[n] for n in ALL_NAMES]
    return (loss, dx[None], *outs)
```

```python
import functools

import numpy as np
import jax
import jax.numpy as jnp
from jax import lax
from jax.experimental import pallas as pl
from jax.experimental.pallas import tpu as pltpu

F32 = jnp.float32
MXU_DTYPE = jnp.bfloat16

DEPTH = 2
D_MODEL = 1024
GROUP_W = 256
N_HEADS = 4
HEAD_DIM = 64
MLA_Q_RANK = 256
MLA_KV_RANK = 128
MLA_ROPE = 32
MLA_Q_COLS = N_HEADS * (HEAD_DIM + MLA_ROPE)
MLA_KV_COLS = N_HEADS * 2 * HEAD_DIM
ROPE_THETA = 10000.0
LN_EPS = 1e-5
RMS_EPS = 1e-6
ALPHA = (2 * DEPTH) ** 0.25
ADAM_LR, ADAM_B1, ADAM_B2, ADAM_EPS, ADAM_WD, ADAM_STEP = 0.001, 0.9, 0.999, 1e-08, 0.01, 10

_SPLIT = (256, 256, 256, 4, 256, 256, 256, 256, 128, 32, 256, 1024)
_OFF = [int(o) for o in np.cumsum((0,) + _SPLIT)]
IN_COLS = _OFF[-1]
PA_COLS = 7 * GROUP_W
PB_COLS = 1024 + 256 + 128 + 128
PB_CQ_BLK, PB_CKV_BLK, PB_MISC_BLK = 4, 10, 11
MISC_KROT = 4

LANES = 1024
PACK_ALIGN = 16
SMALL_ROWS = 144
ROW_TILE = 256
PROJ_BWD_ROWS = 512
ATT_TILE = 256
SB_QUERY_TILE = 256
SOFTMAX_TILE = 512
FWD_PAIRS_PER_LOOP = 1
PAIR = 128
SB_SLOT = PAIR // N_HEADS
VMEM_LIMIT = 56 * 1024 * 1024
MATMUL_VMEM = 30 * 1024 * 1024
NEG = -1e30
EXP_UNDERFLOW = -104.0
DEAD_LOGIT = -110.0
REACH_SLACK = 1.0 + 2.0 ** -10
MESH_AXES = ("x", "y", "c")


def _dot(a, b):
    return jnp.dot(a, b, preferred_element_type=F32)


def _dot_nt(a, b):
    return lax.dot_general(a, b, (((1,), (1,)), ((), ())), preferred_element_type=F32)


def _dot_tn(a, b):
    return lax.dot_general(a, b, (((0,), (0,)), ((), ())), preferred_element_type=F32)


def _split2(x):
    hi = x.astype(MXU_DTYPE)
    lo = (x - hi.astype(F32)).astype(MXU_DTYPE)
    return hi, lo


def _split3(x):
    hi = x.astype(MXU_DTYPE)
    r = x - hi.astype(F32)
    mid = r.astype(MXU_DTYPE)
    lo = (r - mid.astype(F32)).astype(MXU_DTYPE)
    return hi, mid, lo


def _dot_exact_r(x, pm):
    hi, mid, lo = _split3(x)
    return _dot(hi, pm) + _dot(mid, pm) + _dot(lo, pm)


def _dot_exact_l(pm, x):
    hi, mid, lo = _split3(x)
    return _dot(pm, hi) + _dot(pm, mid) + _dot(pm, lo)


def _pick(dim, prefs):
    for p in prefs:
        if dim % p == 0:
            return p
    return dim


def _softplus(z):
    return jnp.maximum(z, 0.0) + jnp.log(1.0 + jnp.exp(-jnp.abs(z)))


def _tile_options(dim):
    opts = [d for d in range(128, min(dim, 2048) + 1, 128) if dim % d == 0]
    return opts or [dim]


def _matmul_tiles(m, n, k, out_bytes):
    tk = k if k <= 4096 else _pick(k, (1024, 512, 256, 128))
    best = None
    for tm in _tile_options(m):
        for tn in _tile_options(n):
            vmem = 2 * 2 * (tm * tk + tk * tn) + 4 * tm * tn + 2 * out_bytes * tm * tn
            if vmem <= MATMUL_VMEM and (best is None or tm * tn / (tm + tn) > best[0]):
                best = (tm * tn / (tm + tn), tm, tn)
    return best[1], best[2], tk


def _matmul(a, b, out_dtype, name, mode="nn"):
    m, k = (a.shape[1], a.shape[0]) if mode == "tn" else a.shape
    n = b.shape[0] if mode == "nt" else b.shape[1]
    tm, tn, tk = _matmul_tiles(m, n, k, jnp.dtype(out_dtype).itemsize)
    nk = k // tk
    dot = {"nn": _dot, "tn": _dot_tn, "nt": _dot_nt}[mode]

    def kern(a_ref, b_ref, o_ref, *acc):
        if nk == 1:
            o_ref[...] = dot(a_ref[...], b_ref[...]).astype(o_ref.dtype)
            return
        acc_ref, = acc
        kk = pl.program_id(2)

        @pl.when(kk == 0)
        def _():
            acc_ref[...] = jnp.zeros_like(acc_ref)

        acc_ref[...] += dot(a_ref[...], b_ref[...])

        @pl.when(kk == nk - 1)
        def _():
            o_ref[...] = acc_ref[...].astype(o_ref.dtype)

    a_spec = (pl.BlockSpec((tk, tm), lambda i, j, kk: (kk, i)) if mode == "tn"
              else pl.BlockSpec((tm, tk), lambda i, j, kk: (i, kk)))
    b_spec = (pl.BlockSpec((tn, tk), lambda i, j, kk: (j, kk)) if mode == "nt"
              else pl.BlockSpec((tk, tn), lambda i, j, kk: (kk, j)))
    return pl.pallas_call(
        kern, name=name, grid=(m // tm, n // tn, nk), in_specs=[a_spec, b_spec],
        out_specs=pl.BlockSpec((tm, tn), lambda i, j, kk: (i, j)),
        out_shape=jax.ShapeDtypeStruct((m, n), out_dtype),
        scratch_shapes=[pltpu.VMEM((tm, tn), F32)] if nk > 1 else [],
        compiler_params=pltpu.CompilerParams(
            dimension_semantics=("parallel", "parallel", "arbitrary"), vmem_limit_bytes=VMEM_LIMIT),
    )(a.astype(MXU_DTYPE), b.astype(MXU_DTYPE))


def _piece_arrays(pieces):
    return [a for p in pieces for a in (p if isinstance(p, tuple) else (p,))]


def _join_pieces(refs, pieces):
    refs, cols = list(refs), []
    for p in pieces:
        vals = [refs.pop(0)[...] for _ in (p if isinstance(p, tuple) else (p,))]
        cols.append(functools.reduce(jnp.add, vals).astype(MXU_DTYPE))
    return jnp.concatenate(cols, axis=1)


def _proj_dx(pieces, w):
    arrs = _piece_arrays(pieces)
    rows, n = arrs[0].shape[0], w.shape[0]
    tm = min(PROJ_BWD_ROWS, rows)

    def kern(*refs):
        refs[-1][...] = _dot_nt(_join_pieces(refs[:len(arrs)], pieces), refs[len(arrs)][...])

    return pl.pallas_call(
        kern, name="proj_dx", grid=(rows // tm,),
        in_specs=[pl.BlockSpec((tm, a.shape[1]), lambda i: (i, 0)) for a in arrs]
        + [pl.BlockSpec(w.shape, lambda i: (0, 0), pipeline_mode=pl.Buffered(1))],
        out_specs=pl.BlockSpec((tm, n), lambda i: (i, 0)), out_shape=jax.ShapeDtypeStruct((rows, n), F32),
        compiler_params=pltpu.CompilerParams(dimension_semantics=("parallel",), vmem_limit_bytes=VMEM_LIMIT),
    )(*arrs, w.astype(MXU_DTYPE))


def _proj_dw(a, pieces):
    arrs = _piece_arrays(pieces)
    rows, m = a.shape
    k = sum(x.shape[1] for x in (p[0] if isinstance(p, tuple) else p for p in pieces))
    tk = min(PROJ_BWD_ROWS, rows)

    def kern(*refs):
        o_ref = refs[-1]

        @pl.when(pl.program_id(0) == 0)
        def _():
            o_ref[...] = jnp.zeros_like(o_ref)

        o_ref[...] += _dot_tn(refs[0][...], _join_pieces(refs[1:1 + len(arrs)], pieces))

    return pl.pallas_call(
        kern, name="proj_dw", grid=(rows // tk,),
        in_specs=[pl.BlockSpec((tk, m), lambda i: (i, 0))]
        + [pl.BlockSpec((tk, x.shape[1]), lambda i: (i, 0)) for x in arrs],
        out_specs=pl.BlockSpec((m, k), lambda i: (0, 0), pipeline_mode=pl.Buffered(1)),
        out_shape=jax.ShapeDtypeStruct((m, k), F32),
        compiler_params=pltpu.CompilerParams(dimension_semantics=("arbitrary",), vmem_limit_bytes=VMEM_LIMIT),
    )(a, *arrs)


def _rowwise(body, name, rows, tile, row_ins, full_ins, row_outs, acc_outs=(), scratch=(),
             reverse=False, sequential=False):
    n = rows // tile

    def ridx(i):
        return (n - 1 - i) if reverse else i

    in_specs, args = [], []
    for arr, width, cb in row_ins:
        in_specs.append(pl.BlockSpec((tile, width), lambda i, cb=cb: (ridx(i), cb)))
        args.append(arr)
    for arr in full_ins:
        in_specs.append(pl.BlockSpec(arr.shape, lambda i, nd=arr.ndim: (0,) * nd))
        args.append(arr)
    out_shape = [jax.ShapeDtypeStruct((rows, w), dt) for w, dt in row_outs]
    out_shape += [jax.ShapeDtypeStruct(s, dt) for s, dt in acc_outs]
    out_specs = [pl.BlockSpec((tile, w), lambda i: (ridx(i), 0)) for w, dt in row_outs]
    out_specs += [pl.BlockSpec(s, lambda i, nd=len(s): (0,) * nd) for s, dt in acc_outs]

    def kern(*refs):
        body(pl.program_id(0), *refs)

    sem = "arbitrary" if (acc_outs or sequential) else "parallel"
    return pl.pallas_call(
        kern, name=name, grid=(n,), in_specs=in_specs, out_specs=out_specs, out_shape=out_shape,
        scratch_shapes=list(scratch),
        compiler_params=pltpu.CompilerParams(dimension_semantics=(sem,), vmem_limit_bytes=VMEM_LIMIT),
    )(*args)


def _ln_stats(u):
    mu = jnp.mean(u, axis=-1, keepdims=True)
    xc = u - mu
    var = jnp.mean(xc * xc, axis=-1, keepdims=True)
    return xc, lax.rsqrt(var + LN_EPS)


def _ln_fwd(a, b, g, beta, name):
    rows, d = a.shape
    has_b = b is not None

    def body(i, *refs):
        if has_b:
            a_ref, b_ref, g_ref, be_ref, h_ref, hb_ref = refs
            u = ALPHA * a_ref[...] + b_ref[...]
        else:
            a_ref, g_ref, be_ref, h_ref, hb_ref = refs
            u = a_ref[...]
        xc, rstd = _ln_stats(u)
        y = xc * rstd * g_ref[...] + be_ref[...]
        h_ref[...] = y
        hb_ref[...] = y.astype(hb_ref.dtype)

    row_ins = [(a, d, 0)] + ([(b, d, 0)] if has_b else [])
    return _rowwise(body, name, rows, min(ROW_TILE, rows), row_ins,
                    [g.reshape(1, d), beta.reshape(1, d)], [(d, F32), (d, MXU_DTYPE)])


def _ln_bwd(a, b, g, dy1, dy2, c1, name):
    rows, d = a.shape
    has_b = b is not None
    has_2 = dy2 is not None

    def body(i, *refs):
        refs = list(refs)
        a_ref = refs.pop(0)
        u = a_ref[...]
        if has_b:
            u = ALPHA * u + refs.pop(0)[...]
        dy = c1 * refs.pop(0)[...]
        if has_2:
            dy = dy + refs.pop(0)[...]
        g_ref, du_ref, dub_ref, dg_ref, db_ref = refs

        @pl.when(i == 0)
        def _():
            dg_ref[...] = jnp.zeros_like(dg_ref)
            db_ref[...] = jnp.zeros_like(db_ref)

        xc, rstd = _ln_stats(u)
        xhat = xc * rstd
        dxh = dy * g_ref[...]
        m1 = jnp.mean(dxh, axis=-1, keepdims=True)
        m2 = jnp.mean(dxh * xhat, axis=-1, keepdims=True)
        du = rstd * (dxh - m1 - xhat * m2)
        du_ref[...] = du
        dub_ref[...] = du.astype(dub_ref.dtype)
        dg_ref[...] += jnp.sum(dy * xhat, axis=0, keepdims=True)
        db_ref[...] += jnp.sum(dy, axis=0, keepdims=True)

    row_ins = [(a, d, 0)] + ([(b, d, 0)] if has_b else []) + [(dy1, d, 0)] + ([(dy2, d, 0)] if has_2 else [])
    return _rowwise(body, name, rows, min(ROW_TILE, rows), row_ins, [g.reshape(1, d)],
                    [(d, F32), (d, MXU_DTYPE)], [((1, d), F32), ((1, d), F32)])


def _loss_grad(h, target):
    rows, d = h.shape

    def body(i, h_ref, t_ref, dh_ref, acc_ref):
        @pl.when(i == 0)
        def _():
            acc_ref[...] = jnp.zeros_like(acc_ref)

        e = h_ref[...] - t_ref[...]
        dh_ref[...] = e * (1.0 / d)
        acc_ref[...] += jnp.sum(e * e, axis=0, keepdims=True)

    return _rowwise(body, "loss_grad", rows, ROW_TILE, [(h, d, 0), (target, d, 0)], [],
                    [(d, F32)], [((1, d), F32)])


def _gate_fwd(groups, pb):
    rows = pb.shape[0]
    w = GROUP_W * len(groups)

    def body(i, *refs):
        g = refs[4][...]
        mixed = jnp.concatenate([r[...] for r in refs[:4]], axis=1)
        refs[5][...] = (mixed * (g / (1.0 + jnp.exp(-g)))).astype(refs[5].dtype)

    return _rowwise(body, "gate_fwd", rows, ROW_TILE, [(o, GROUP_W, 0) for o in groups] + [(pb, w, 0)], [],
                    [(w, MXU_DTYPE)])[0]


def _gate_bwd(dgated, groups, pb):
    rows = pb.shape[0]
    w = GROUP_W * len(groups)

    def body(i, *refs):
        dg = refs[0][...]
        mixed = jnp.concatenate([r[...] for r in refs[1:5]], axis=1)
        g = refs[5][...]
        dm_ref, dgate_ref = refs[6], refs[7]
        sig = 1.0 / (1.0 + jnp.exp(-g))
        dm_ref[...] = dg * (g * sig)
        dgate_ref[...] = (dg * mixed * (sig * (1.0 + g * (1.0 - sig)))).astype(dgate_ref.dtype)

    return _rowwise(body, "gate_bwd", rows, ROW_TILE,
                    [(dgated, w, 0)] + [(o, GROUP_W, 0) for o in groups] + [(pb, w, 0)], [],
                    [(w, F32), (w, MXU_DTYPE)])


def _tri(n, kind):
    r = np.arange(n)[:, None]
    c = np.arange(n)[None, :]
    m = {"lower_incl": r >= c, "upper_incl": r <= c, "row_gt_col": r > c, "row_lt_col": r < c}[kind]
    return jnp.asarray(m.astype(np.float32), dtype=MXU_DTYPE)


def _forget_fwd(pb, bias_row):
    rows = pb.shape[0]
    tile = min(ROW_TILE, rows)

    def body(i, x_ref, b_ref, l_ref, o_ref, carry_ref):
        @pl.when(i == 0)
        def _():
            carry_ref[...] = jnp.zeros_like(carry_ref)

        xx = x_ref[...] + b_ref[...]
        lane = lax.broadcasted_iota(jnp.int32, xx.shape, 1)
        lf = jnp.where(lane < N_HEADS, -_softplus(-xx), 0.0)
        o_ref[...] = _dot_exact_l(l_ref[...], lf) + carry_ref[...]
        carry_ref[...] += jnp.sum(lf, axis=0, keepdims=True)

    return _rowwise(body, "forget_fwd", rows, tile, [(pb, 128, PB_MISC_BLK)],
                    [bias_row, _tri(tile, "lower_incl")], [(128, F32)],
                    scratch=[pltpu.VMEM((1, 128), F32)], sequential=True)[0]


def _forget_bwd(pb, bias_row, dfc):
    rows = pb.shape[0]
    tile = min(ROW_TILE, rows)

    def body(i, x_ref, df_ref, b_ref, u_ref, o_ref, db_ref, carry_ref):
        @pl.when(i == 0)
        def _():
            carry_ref[...] = jnp.zeros_like(carry_ref)
            db_ref[...] = jnp.zeros_like(db_ref)

        df = df_ref[...]
        sfx = _dot_exact_l(u_ref[...], df) + carry_ref[...]
        carry_ref[...] += jnp.sum(df, axis=0, keepdims=True)
        xx = x_ref[...] + b_ref[...]
        lane = lax.broadcasted_iota(jnp.int32, xx.shape, 1)
        dl = jnp.where(lane < N_HEADS, sfx / (1.0 + jnp.exp(xx)), 0.0)
        o_ref[...] = dl
        db_ref[...] += jnp.sum(dl, axis=0, keepdims=True)

    return _rowwise(body, "forget_bwd", rows, tile,
                    [(pb, 128, PB_MISC_BLK), (dfc, 128, 0)],
                    [bias_row, _tri(tile, "upper_incl")], [(128, F32)], [((1, 128), F32)],
                    scratch=[pltpu.VMEM((1, 128), F32)], reverse=True, sequential=True)


def _rope_tables(s):
    half = MLA_ROPE // 2
    inv_freq = ROPE_THETA ** (-jnp.arange(half, dtype=F32) / half)
    ang = jnp.arange(s).astype(F32)[:, None] * inv_freq[None, :]
    cos2 = jnp.tile(jnp.cos(ang), (1, 2))
    sin2 = jnp.tile(jnp.sin(ang), (1, 2))
    cx = jnp.tile(cos2, (1, N_HEADS))
    sx = jnp.tile(sin2, (1, N_HEADS))
    pad = ((0, 0), (MISC_KROT, 128 - MISC_KROT - MLA_ROPE))
    ck = jnp.pad(cos2, pad)
    sk = jnp.pad(sin2, pad)
    return dict(ck=ck, sk=sk, cx=cx, sx=sx)


def _rot_matrix(width, bases):
    half = MLA_ROPE // 2
    p = np.zeros((width, width), np.float32)
    for b in bases:
        for i in range(half):
            p[b + half + i, b + i] = -1.0
            p[b + i, b + half + i] = 1.0
    return p


def _rope_matrices():
    pk = _rot_matrix(128, [MISC_KROT])
    p4 = _rot_matrix(128, [h * MLA_ROPE for h in range(N_HEADS)])
    a = np.zeros((128, 128), np.float32)
    for h in range(N_HEADS):
        for r in range(MLA_ROPE):
            a[h * MLA_ROPE + r, MISC_KROT + r] = 1.0
    cast = lambda m: jnp.asarray(m, dtype=MXU_DTYPE)
    return dict(p4=cast(p4), p4t=cast(p4.T), pk=cast(pk), spread=cast(a.T), xa=cast(a), xb=cast(p4.T @ a))


def _rms(c, g):
    r = lax.rsqrt(jnp.mean(c * c, axis=-1, keepdims=True) + RMS_EPS)
    return c * r * g


def _mla_q_fwd(pb, g, w_up, tabs, mats):
    rows = pb.shape[0]

    def body(i, c_ref, cos_ref, sin_ref, g_ref, w_ref, p_ref, q_ref, cn_ref):
        cn = _rms(c_ref[...], g_ref[...]).astype(cn_ref.dtype)
        cn_ref[...] = cn
        q = _dot(cn, w_ref[...])
        qr = q[:, GROUP_W:]
        q_ref[:, :GROUP_W] = q[:, :GROUP_W].astype(q_ref.dtype)
        q_ref[:, GROUP_W:] = (qr * cos_ref[...] + _dot_exact_r(qr, p_ref[...]) * sin_ref[...]).astype(q_ref.dtype)

    return _rowwise(body, "mla_q_fwd", rows, ROW_TILE,
                    [(pb, MLA_Q_RANK, PB_CQ_BLK), (tabs["cx"], PAIR, 0), (tabs["sx"], PAIR, 0)],
                    [g.reshape(1, MLA_Q_RANK), w_up.astype(MXU_DTYPE), mats["p4"]],
                    [(MLA_Q_COLS, MXU_DTYPE), (MLA_Q_RANK, MXU_DTYPE)])


def _mla_kv_fwd(pb, g, w_up, tabs, mats):
    rows = pb.shape[0]

    def body(i, c_ref, x_ref, cos_ref, sin_ref, g_ref, w_ref, p_ref, sp_ref, kv_ref, cn_ref, kr_ref):
        cn = _rms(c_ref[...], g_ref[...]).astype(cn_ref.dtype)
        cn_ref[...] = cn
        kv_ref[...] = _dot(cn, w_ref[...]).astype(kv_ref.dtype)
        xx = x_ref[...]
        kr = xx * cos_ref[...] + _dot_exact_r(xx, p_ref[...]) * sin_ref[...]
        kr_ref[...] = _dot_exact_r(kr, sp_ref[...]).astype(kr_ref.dtype)

    return _rowwise(body, "mla_kv_fwd", rows, ROW_TILE,
                    [(pb, MLA_KV_RANK, PB_CKV_BLK), (pb, 128, PB_MISC_BLK), (tabs["ck"], 128, 0), (tabs["sk"], 128, 0)],
                    [g.reshape(1, MLA_KV_RANK), w_up.astype(MXU_DTYPE), mats["pk"], mats["spread"]],
                    [(MLA_KV_COLS, MXU_DTYPE), (MLA_KV_RANK, MXU_DTYPE), (128, MXU_DTYPE)])


def _rope_q_bwd(dq_full, tabs, mats):
    rows, nq = dq_full.shape

    def body(i, d_ref, cos_ref, sin_ref, pt_ref, o_ref):
        dr = d_ref[:, GROUP_W:]
        o_ref[:, :GROUP_W] = d_ref[:, :GROUP_W].astype(o_ref.dtype)
        o_ref[:, GROUP_W:] = (dr * cos_ref[...] + _dot_exact_r(dr * sin_ref[...], pt_ref[...])).astype(o_ref.dtype)

    return _rowwise(body, "rope_q_bwd", rows, ROW_TILE,
                    [(dq_full, nq, 0), (tabs["cx"], PAIR, 0), (tabs["sx"], PAIR, 0)], [mats["p4t"]],
                    [(nq, MXU_DTYPE)])[0]


def _rope_k_bwd(dkr, tabs, mats):
    rows = dkr.shape[0]

    def body(i, d_ref, cos_ref, sin_ref, a_ref, b_ref, o_ref):
        d = d_ref[...]
        o_ref[...] = _dot_exact_r(d * cos_ref[...], a_ref[...]) + _dot_exact_r(d * sin_ref[...], b_ref[...])

    return _rowwise(body, "rope_k_bwd", rows, ROW_TILE,
                    [(dkr, 128, 0), (tabs["cx"], 128, 0), (tabs["sx"], 128, 0)], [mats["xa"], mats["xb"]],
                    [(128, F32)])[0]


def _rms_bwd(pb, width, col_blk, g, dy, name):
    rows = pb.shape[0]

    def body(i, c_ref, dy_ref, g_ref, dc_ref, dg_ref):
        @pl.when(i == 0)
        def _():
            dg_ref[...] = jnp.zeros_like(dg_ref)

        c = c_ref[...]
        dy = dy_ref[...]
        r = lax.rsqrt(jnp.mean(c * c, axis=-1, keepdims=True) + RMS_EPS)
        dyg = dy * g_ref[...]
        dc = r * dyg - c * (r * r * r) * jnp.mean(c * dyg, axis=-1, keepdims=True)
        dc_ref[...] = dc.astype(dc_ref.dtype)
        dg_ref[...] += jnp.sum(dy * c * r, axis=0, keepdims=True)

    return _rowwise(body, name, rows, ROW_TILE, [(pb, width, col_blk), (dy, width, 0)], [g.reshape(1, width)],
                    [(width, MXU_DTYPE)], [((1, width), F32)])


def _att_params(parallel):
    return pltpu.CompilerParams(dimension_semantics=("parallel" if parallel else "arbitrary",),
                                vmem_limit_bytes=VMEM_LIMIT)


def _blk_off(j, t):
    return j * t if isinstance(j, int) else pl.multiple_of(j * t, t)


def _causal_mask(t, strict):
    r = lax.broadcasted_iota(jnp.int32, (t, t), 0)
    c = lax.broadcasted_iota(jnp.int32, (t, t), 1)
    return (c < r) if strict else (c <= r)


def _lane_mask(kind, head, rows):
    lane = lax.broadcasted_iota(jnp.int32, (rows, PAIR), 1)
    if kind == "pair":
        return (lane < HEAD_DIM) if head % 2 == 0 else (lane >= HEAD_DIM)
    return (lane >= MLA_ROPE * head) & (lane < MLA_ROPE * (head + 1))


def _row_spec(t, cb, width=PAIR):
    return pl.BlockSpec((t, width), lambda i, cb=cb: (i, cb))


def _whole_spec(rows, cb, width=PAIR):
    return pl.BlockSpec((rows, width), lambda i, cb=cb: (0, cb), pipeline_mode=pl.Buffered(1))


def _is_pow2(x):
    return float(np.frexp(x)[0]) == 0.5


def _masked_heads(blocks, kinds, pair, dtype, scale=None):
    out = []
    for e in range(2):
        head = 2 * pair + e
        parts = [jnp.where(_lane_mask(k, head, b.shape[0]), b.astype(F32) * (1.0 if scale is None else scale),
                           0.0).astype(dtype)
                 for b, k in zip(blocks, kinds)]
        out.append(parts[0] if len(parts) == 1 else jnp.concatenate(parts, axis=1))
    return out


def _logit_reach(qh, kmax2, head):
    q32 = qh.astype(F32)
    return jnp.sqrt(jnp.sum(q32 * q32, axis=1, keepdims=True) * _col(kmax2, head)) * REACH_SLACK


def _forget_top(ft_ref, head, off):
    return jnp.max(-ft_ref[head:head + 1, pl.ds(off, PAIR)])


def _col(block, idx):
    lane = lax.broadcasted_iota(jnp.int32, block.shape, 1)
    return jnp.sum(jnp.where(lane == idx, block, 0.0), axis=1, keepdims=True)


def _scatter_cols(cols, t):
    lane = lax.broadcasted_iota(jnp.int32, (t, PAIR), 1)
    out = jnp.zeros((t, PAIR), F32)
    for idx, c in cols.items():
        out = out + jnp.where(lane == idx, c, 0.0)
    return out


def _take_heads(per_head, pair):
    return jnp.where(_lane_mask("pair", 0, per_head[0].shape[0]), per_head[0], per_head[1])


class _Parts:
    def __init__(self, q_parts, k_parts, v_parts, tq, sk):
        self.kinds = [[kind for _, _, kind in q_parts[p]] for p in range(2)]
        self.nparts = len(q_parts[0])
        self.q_specs = [_row_spec(tq, cb) for p in range(2) for _, cb, _ in q_parts[p]]
        self.q_args = [a for p in range(2) for a, _, _ in q_parts[p]]
        self.k_specs = [_whole_spec(sk, cb) for p in range(2) for _, cb, _ in k_parts[p]]
        self.k_args = [a for p in range(2) for a, _, _ in k_parts[p]]
        self.v_specs = [_whole_spec(sk, cb) for _, cb in v_parts]
        self.v_args = [a for a, _ in v_parts]
        self.width = PAIR * self.nparts

    def split(self, refs):
        n = self.nparts
        refs = list(refs)
        q = [refs[p * n:(p + 1) * n] for p in range(2)]
        k = [refs[2 * n + p * n:2 * n + (p + 1) * n] for p in range(2)]
        v = refs[4 * n:4 * n + 2]
        return q, k, v, refs[4 * n + 2:]

    def k_block(self, k_refs, off, t):
        blks = [r[pl.ds(off, t), :] for r in k_refs]
        return blks[0] if len(blks) == 1 else jnp.concatenate(blks, axis=1)


def _key_norm_max(src, k_blk):
    rows = src.shape[0]

    def body(i, k0_ref, k1_ref, o_ref):
        @pl.when(i == 0)
        def _():
            o_ref[...] = jnp.zeros_like(o_ref)

        cols = {}
        for p, ref in enumerate((k0_ref, k1_ref)):
            k32 = ref[...].astype(F32)
            for e in range(2):
                sq = jnp.sum(jnp.where(_lane_mask("pair", e, k32.shape[0]), k32 * k32, 0.0), axis=1, keepdims=True)
                cols[2 * p + e] = jnp.max(sq, axis=0, keepdims=True)
        lane = lax.broadcasted_iota(jnp.int32, (1, PAIR), 1)
        o_ref[...] = jnp.maximum(o_ref[...], sum(jnp.where(lane == h, c, 0.0) for h, c in cols.items()))

    return _rowwise(body, "key_norm_max", rows, ROW_TILE, [(src, PAIR, k_blk), (src, PAIR, k_blk + 1)], [], [],
                    [((1, PAIR), F32)])[0]


def _softmax_fwd(q_parts, k_parts, v_parts, sq, sk, scale, causal, bias, name):
    tq = min(SOFTMAX_TILE, sq)
    tk = tq if causal else min(SOFTMAX_TILE, sk)
    nkv = sk // tk
    pp = _Parts(q_parts, k_parts, v_parts, tq, sk)

    def kern(*refs):
        q_refs, k_refs, v_refs, rest = pp.split(refs)
        if bias is not None:
            fc_ref, ft_ref, kmax_ref, o_ref, lse_ref = rest
            fcb = fc_ref[...]
        else:
            o_ref, lse_ref = rest
        i = pl.program_id(0)
        fold = _is_pow2(scale)
        head_on = [jnp.where(_lane_mask("pair", e, tk), 1.0, 0.0).astype(MXU_DTYPE) for e in range(2)]
        head_off = [jnp.where(_lane_mask("pair", e, tk), 0.0, 1.0).astype(MXU_DTYPE) for e in range(2)]
        lse_cols = {}
        for first in range(0, 2, FWD_PAIRS_PER_LOOP):
            pairs = list(range(first, first + FWD_PAIRS_PER_LOOP))
            heads = [2 * p + e for p in pairs for e in range(2)]
            qm = {}
            for p in pairs:
                masked_q = _masked_heads([r[...] for r in q_refs[p]], pp.kinds[p], p, MXU_DTYPE, scale if fold else None)
                qm.update({2 * p + e: masked_q[e] for e in range(2)})
            if bias is not None:
                reach = {h: _logit_reach(qm[h], kmax_ref[...], h) for h in heads}

            def block(j, carry, masked, pairs=pairs, qm=qm):
                off = _blk_off(j, tk)
                out = []
                for p in pairs:
                    kb = pp.k_block(k_refs[p], off, tk)
                    vb = v_refs[p][pl.ds(off, tk), :]
                    for e in range(2):
                        h = 2 * p + e
                        m, acc = carry[len(out)]
                        s = _dot_nt(qm[h], kb)
                        if not fold:
                            s = s * scale
                        if bias is not None:
                            s = s - ft_ref[h:h + 1, pl.ds(off, tk)]
                        if masked:
                            s = jnp.where(_causal_mask(tq, False), s, NEG)
                        m_new = jnp.maximum(m, jnp.max(s, axis=1, keepdims=True))
                        pr = jnp.exp(s - m_new).astype(MXU_DTYPE)
                        out.append((m_new, jnp.exp(m - m_new) * acc + _dot(pr, vb * head_on[e] + head_off[e])))
                return tuple(out)

            carry = tuple((jnp.full((tq, 1), NEG, F32), jnp.zeros((tq, PAIR), F32)) for _ in heads)
            if causal and bias is not None:
                def alive(c, j, heads=heads, reach=reach):
                    return functools.reduce(jnp.maximum, [
                        jnp.max(reach[h] + _forget_top(ft_ref, h, _blk_off(j, tk)) - c[n][0]) for n, h in enumerate(heads)])

                def step(state, block=block, alive=alive):
                    n, _, c = state
                    c = block(i - 1 - n, c, False)
                    return n + 1, alive(c, i - 1 - n), c

                carry = block(i, carry, True)
                _, _, carry = lax.while_loop(lambda st: jnp.logical_and(st[0] < i, st[1] > DEAD_LOGIT), step,
                                             (jnp.int32(0), alive(carry, i), carry))
            elif causal:
                carry = lax.fori_loop(0, i, lambda j, c, block=block: block(j, c, False), carry)
                carry = block(i, carry, True)
            else:
                for j in range(nkv):
                    carry = block(j, carry, False)
            for n, p in enumerate(pairs):
                outs = []
                for e in range(2):
                    m, acc = carry[2 * n + e]
                    l = _col(acc, HEAD_DIM * (1 - e))
                    outs.append(acc / l)
                    lse_cols[2 * p + e] = m + jnp.log(l) + (_col(fcb, 2 * p + e) if bias is not None else 0.0)
                o_ref[:, p * PAIR:(p + 1) * PAIR] = _take_heads(outs, p)
        lse_ref[...] = _scatter_cols(lse_cols, tq)

    in_specs = pp.q_specs + pp.k_specs + pp.v_specs
    args = pp.q_args + pp.k_args + pp.v_args
    if bias is not None:
        in_specs += [_row_spec(tq, 0), pl.BlockSpec((8, sk), lambda i: (0, 0), pipeline_mode=pl.Buffered(1)),
                     pl.BlockSpec((1, PAIR), lambda i: (0, 0))]
        args += list(bias)
    return pl.pallas_call(
        kern, name=name, grid=(sq // tq,), in_specs=in_specs,
        out_specs=[_row_spec(tq, 0, GROUP_W), _row_spec(tq, 0)],
        out_shape=[jax.ShapeDtypeStruct((sq, GROUP_W), F32), jax.ShapeDtypeStruct((sq, PAIR), F32)],
        compiler_params=_att_params(True),
    )(*args)


def _softmax_bwd(q_parts, k_parts, v_parts, o, lse, do, do_blk, sq, sk, scale, causal, bias, name):
    tq = min(SOFTMAX_TILE, sq)
    tk = tq if causal else min(SOFTMAX_TILE, sk)
    nkv = sk // tk
    pp = _Parts(q_parts, k_parts, v_parts, tq, sk)
    quad = pp.nparts == 2
    wq = GROUP_W + (PAIR if quad else 0)

    def kern(*refs):
        q_refs, k_refs, v_refs, rest = pp.split(refs)
        if bias is not None:
            o_ref, lse_ref, do_ref, fc_ref, ft_ref, kmax_ref, dq_ref, dk_ref, dv_ref, dfq_ref, dfk_ref = rest
            fcb = fc_ref[...]
        else:
            o_ref, lse_ref, do_ref, dq_ref, dk_ref, dv_ref = rest
        i = pl.program_id(0)
        fold = _is_pow2(scale)

        @pl.when(i == 0)
        def _():
            dk_ref[...] = jnp.zeros_like(dk_ref)
            dv_ref[...] = jnp.zeros_like(dv_ref)
            if bias is not None:
                dfk_ref[...] = jnp.zeros_like(dfk_ref)

        lse_b = lse_ref[...]
        qm, dom, delta, lse_h = [], [], [], []
        for p in range(2):
            qm += _masked_heads([r[...] for r in q_refs[p]], pp.kinds[p], p, MXU_DTYPE, scale if fold else None)
            do_p = do_ref[:, p * PAIR:(p + 1) * PAIR]
            dom += _masked_heads([do_p], ["pair"], p, MXU_DTYPE)
            prod = do_p * o_ref[:, p * PAIR:(p + 1) * PAIR]
            for e in range(2):
                h = 2 * p + e
                delta.append(jnp.sum(jnp.where(_lane_mask("pair", h, tq), prod, 0.0), axis=1, keepdims=True))
                lse_h.append(_col(lse_b, h) - (_col(fcb, h) if bias is not None else 0.0))

        def block(j, carry, masked):
            off = _blk_off(j, tk)
            out = []
            for p in range(2):
                kb = pp.k_block(k_refs[p], off, tk)
                vb = v_refs[p][pl.ds(off, tk), :]
                dk_acc = jnp.zeros((tk, pp.width), F32)
                dv_acc = jnp.zeros((tk, PAIR), F32)
                for e in range(2):
                    h = 2 * p + e
                    dq, dfq = carry[h]
                    s = _dot_nt(qm[h], kb)
                    if not fold:
                        s = s * scale
                    if bias is not None:
                        s = s - ft_ref[h:h + 1, pl.ds(off, tk)]
                    if masked:
                        s = jnp.where(_causal_mask(tq, False), s, NEG)
                    pr = jnp.exp(s - lse_h[h])
                    ds = pr * (_dot_nt(dom[h], vb) - delta[h])
                    dsb = (ds if fold else ds * scale).astype(MXU_DTYPE)
                    dv_acc = dv_acc + _dot_tn(pr.astype(MXU_DTYPE), dom[h])
                    dk_acc = dk_acc + _dot_tn(dsb, qm[h])
                    dq = dq + _dot(dsb, kb)
                    if bias is not None:
                        dfq = dfq + jnp.sum(ds, axis=1, keepdims=True)
                        dfk_ref[h:h + 1, pl.ds(off, tk)] -= jnp.sum(ds, axis=0, keepdims=True)
                    out.append((dq, dfq))
                dv_ref[pl.ds(off, tk), p * PAIR:(p + 1) * PAIR] += dv_acc
                dk_ref[pl.ds(off, tk), p * PAIR:(p + 1) * PAIR] += dk_acc[:, :PAIR]
                if quad:
                    dk_ref[pl.ds(off, tk), GROUP_W:] += dk_acc[:, PAIR:]
            return tuple(out)

        carry = tuple((jnp.zeros((tq, pp.width), F32), jnp.zeros((tq, 1), F32)) for _ in range(N_HEADS))
        if causal and bias is not None:
            reach = [_logit_reach(qm[h], kmax_ref[...], h) - lse_h[h] for h in range(N_HEADS)]

            def alive(j):
                return functools.reduce(jnp.maximum, [jnp.max(reach[h] + _forget_top(ft_ref, h, _blk_off(j, tk)))
                                                      for h in range(N_HEADS)])

            def step(state):
                n, _, c = state
                return n + 1, alive(i - 1 - n), block(i - 1 - n, c, False)

            carry = block(i, carry, True)
            _, _, carry = lax.while_loop(lambda st: jnp.logical_and(st[0] < i, st[1] > DEAD_LOGIT), step,
                                         (jnp.int32(0), alive(i), carry))
        elif causal:
            carry = lax.fori_loop(0, i, lambda j, c: block(j, c, False), carry)
            carry = block(i, carry, True)
        else:
            for j in range(nkv):
                carry = block(j, carry, False)
        dqs = [c[0] * scale if fold else c[0] for c in carry]
        for p in range(2):
            dq_ref[:, p * PAIR:(p + 1) * PAIR] = _take_heads([dqs[2 * p + e][:, :PAIR] for e in range(2)], p)
        if quad:
            dq_ref[:, GROUP_W:] = sum(jnp.where(_lane_mask("quad", h, tq), dqs[h][:, PAIR:], 0.0)
                                      for h in range(N_HEADS))
        if bias is not None:
            dfq_ref[...] = _scatter_cols({h: carry[h][1] for h in range(N_HEADS)}, tq)

    acc_spec = lambda rows, width: pl.BlockSpec((rows, width), lambda i: (0, 0), pipeline_mode=pl.Buffered(1))
    in_specs = pp.q_specs + pp.k_specs + pp.v_specs + [_row_spec(tq, 0, GROUP_W), _row_spec(tq, 0),
                                                       _row_spec(tq, do_blk, GROUP_W)]
    args = pp.q_args + pp.k_args + pp.v_args + [o, lse, do]
    out_specs = [_row_spec(tq, 0, wq), acc_spec(sk, wq), acc_spec(sk, GROUP_W)]
    out_shape = [jax.ShapeDtypeStruct((sq, wq), F32), jax.ShapeDtypeStruct((sk, wq), F32),
                 jax.ShapeDtypeStruct((sk, GROUP_W), F32)]
    if bias is not None:
        in_specs += [_row_spec(tq, 0), pl.BlockSpec((8, sk), lambda i: (0, 0), pipeline_mode=pl.Buffered(1)),
                     pl.BlockSpec((1, PAIR), lambda i: (0, 0))]
        args += list(bias)
        out_specs += [_row_spec(tq, 0), acc_spec(8, sk)]
        out_shape += [jax.ShapeDtypeStruct((sq, PAIR), F32), jax.ShapeDtypeStruct((8, sk), F32)]
    return pl.pallas_call(
        kern, name=name, grid=(sq // tq,), in_specs=in_specs, out_specs=out_specs, out_shape=out_shape,
        compiler_params=_att_params(False),
    )(*args)


def _sb_logs(qh, kb, valid):
    z = _dot_nt(qh, kb)
    sp = _softplus(z)
    lk = -sp
    if valid is not None:
        lk = jnp.where(valid, lk, 0.0)
    return lk, z - sp


def _sb_valid(d, tq, tk):
    r = lax.broadcasted_iota(jnp.int32, (tq, tk), 0)
    c = lax.broadcasted_iota(jnp.int32, (tq, tk), 1)
    return c + d * tk < r


def _tri_sums(xs, tri):
    t = xs[0].shape[0]
    pieces = [_split2(x) for x in xs]
    hi = _dot(jnp.concatenate([pc[0] for pc in pieces], axis=0), tri)
    lo = _dot(jnp.concatenate([pc[1] for pc in pieces], axis=0), tri)
    return [hi[n * t:(n + 1) * t] + lo[n * t:(n + 1) * t] for n in range(len(xs))]


def _sb_fwd(src, q_blk, k_blk, v_blk, s, scale, name):
    assert _is_pow2(scale) and s // ATT_TILE <= SB_SLOT
    tq, t = min(SB_QUERY_TILE, s), min(ATT_TILE, s)
    band = tq // t
    pair = lambda blk: [[(src, blk + p, "pair")] for p in range(2)]
    pp = _Parts(pair(q_blk), pair(k_blk), [(src, v_blk + p) for p in range(2)], tq, s)

    def kern(*refs):
        q_refs, k_refs, v_refs, (tri_ref, o_ref, rm_ref, cnt_ref) = pp.split(refs)
        i = pl.program_id(0)
        tri = tri_ref[...]
        lane = lax.broadcasted_iota(jnp.int32, (tq, PAIR), 1)
        qm = []
        for p in range(2):
            qm += _masked_heads([q_refs[p][0][...]], ["pair"], p, MXU_DTYPE, scale)

        def block(j, carry, valid):
            accs, rights, rm = carry
            off = _blk_off(j, t)
            kbs = [k_refs[p][0][pl.ds(off, t), :] for p in range(2)]
            vbs = [v_refs[p][pl.ds(off, t), :] for p in range(2)]
            logs = [_sb_logs(qm[h], kbs[h // 2], valid) for h in range(N_HEADS)]
            tails = _tri_sums([lg[0] for lg in logs], tri)
            new_acc, new_right = [], []
            for h in range(N_HEADS):
                lk, ls = logs[h]
                w = jnp.exp(ls + tails[h] + rights[h])
                if valid is not None:
                    w = jnp.where(valid, w, 0.0)
                new_acc.append(accs[h] + _dot(w.astype(MXU_DTYPE), vbs[h // 2]))
                rm = rm + jnp.where(lane == SB_SLOT * h + j, rights[h], 0.0)
                new_right.append(rights[h] + jnp.sum(lk, axis=1, keepdims=True))
            return tuple(new_acc), tuple(new_right), rm

        carry = (tuple(jnp.zeros((tq, PAIR), F32) for _ in range(N_HEADS)),
                 tuple(jnp.zeros((tq, 1), F32) for _ in range(N_HEADS)), jnp.zeros((tq, PAIR), F32))
        for d in reversed(range(band)):
            carry = block(band * i + d, carry, _sb_valid(d, tq, t))

        def alive(c):
            return functools.reduce(jnp.maximum, [jnp.max(r) for r in c[1]])

        def step(state):
            n, _, c = state
            c = block(band * i - 1 - n, c, None)
            return n + 1, alive(c), c

        n_done, _, carry = lax.while_loop(lambda st: jnp.logical_and(st[0] < band * i, st[1] > EXP_UNDERFLOW),
                                          step, (jnp.int32(0), alive(carry), carry))
        cnt_ref[i] = n_done
        for p in range(2):
            o_ref[:, p * PAIR:(p + 1) * PAIR] = _take_heads([carry[0][2 * p + e] for e in range(2)], p)
        rm_ref[...] = carry[2]

    return pl.pallas_call(
        kern, name=name, grid=(s // tq,),
        in_specs=pp.q_specs + pp.k_specs + pp.v_specs + [pl.BlockSpec((t, t), lambda i: (0, 0))],
        out_specs=[_row_spec(tq, 0, GROUP_W), _row_spec(tq, 0), pl.BlockSpec(memory_space=pltpu.SMEM)],
        out_shape=[jax.ShapeDtypeStruct((s, GROUP_W), F32), jax.ShapeDtypeStruct((s, PAIR), F32),
                   jax.ShapeDtypeStruct((s // tq,), jnp.int32)],
        compiler_params=_att_params(False),
    )(*(pp.q_args + pp.k_args + pp.v_args + [_tri(t, "row_gt_col")]))


def _sb_bwd(src, q_blk, k_blk, v_blk, do, do_blk, rm, visited, s, scale, name):
    assert _is_pow2(scale)
    tq, t = min(SB_QUERY_TILE, s), min(ATT_TILE, s)
    band = tq // t
    pair = lambda blk: [[(src, blk + p, "pair")] for p in range(2)]
    pp = _Parts(pair(q_blk), pair(k_blk), [(src, v_blk + p) for p in range(2)], tq, s)

    def kern(*refs):
        q_refs, k_refs, v_refs, (do_ref, rm_ref, tri_ref, pre_ref, cnt_ref, dq_ref, dk_ref, dv_ref) = pp.split(refs)
        i = pl.program_id(0)

        @pl.when(i == 0)
        def _():
            dk_ref[...] = jnp.zeros_like(dk_ref)
            dv_ref[...] = jnp.zeros_like(dv_ref)

        rmb = rm_ref[...]
        tri = tri_ref[...]
        pre = pre_ref[...]
        qm, dom = [], []
        for p in range(2):
            qm += _masked_heads([q_refs[p][0][...]], ["pair"], p, MXU_DTYPE, scale)
            dom += _masked_heads([do_ref[:, p * PAIR:(p + 1) * PAIR]], ["pair"], p, MXU_DTYPE)

        def block(j, carry, valid):
            dqs, lefts = carry
            off = _blk_off(j, t)
            kbs = [k_refs[p][0][pl.ds(off, t), :] for p in range(2)]
            vbs = [v_refs[p][pl.ds(off, t), :] for p in range(2)]
            logs = [_sb_logs(qm[h], kbs[h // 2], valid) for h in range(N_HEADS)]
            tails = _tri_sums([lg[0] for lg in logs], tri)
            ws, gs = [], []
            for h in range(N_HEADS):
                lk, ls = logs[h]
                w = jnp.exp(ls + tails[h] + _col(rmb, SB_SLOT * h + j))
                if valid is not None:
                    w = jnp.where(valid, w, 0.0)
                ws.append(w)
                gs.append(_dot_nt(dom[h], vbs[h // 2]) * w)
            prefix = _tri_sums(gs, pre)
            new_dq, new_left = [], []
            dk_acc = [jnp.zeros((t, PAIR), F32) for _ in range(2)]
            dv_acc = [jnp.zeros((t, PAIR), F32) for _ in range(2)]
            for h in range(N_HEADS):
                lk, ls = logs[h]
                sig = jnp.exp(ls)
                dz = gs[h] * (1.0 - sig) - sig * (prefix[h] + lefts[h])
                if valid is not None:
                    dz = jnp.where(valid, dz, 0.0)
                dzb = dz.astype(MXU_DTYPE)
                dv_acc[h // 2] = dv_acc[h // 2] + _dot_tn(ws[h].astype(MXU_DTYPE), dom[h])
                dk_acc[h // 2] = dk_acc[h // 2] + _dot_tn(dzb, qm[h])
                new_dq.append(dqs[h] + _dot(dzb, kbs[h // 2]))
                new_left.append(lefts[h] + jnp.sum(gs[h], axis=1, keepdims=True))
            for p in range(2):
                dv_ref[pl.ds(off, t), p * PAIR:(p + 1) * PAIR] += dv_acc[p]
                dk_ref[pl.ds(off, t), p * PAIR:(p + 1) * PAIR] += dk_acc[p]
            return tuple(new_dq), tuple(new_left)

        carry = (tuple(jnp.zeros((tq, PAIR), F32) for _ in range(N_HEADS)),
                 tuple(jnp.zeros((tq, 1), F32) for _ in range(N_HEADS)))
        carry = lax.fori_loop(band * i - cnt_ref[i], band * i, lambda j, c: block(j, c, None), carry)
        for d in range(band):
            carry = block(band * i + d, carry, _sb_valid(d, tq, t))
        for p in range(2):
            dq_ref[:, p * PAIR:(p + 1) * PAIR] = _take_heads([carry[0][2 * p + e] * scale for e in range(2)], p)

    mspec = pl.BlockSpec((t, t), lambda i: (0, 0))
    acc_spec = pl.BlockSpec((s, GROUP_W), lambda i: (0, 0), pipeline_mode=pl.Buffered(1))
    return pl.pallas_call(
        kern, name=name, grid=(s // tq,),
        in_specs=pp.q_specs + pp.k_specs + pp.v_specs + [_row_spec(tq, do_blk, GROUP_W), _row_spec(tq, 0), mspec, mspec,
                                                         pl.BlockSpec(memory_space=pltpu.SMEM)],
        out_specs=[_row_spec(tq, 0, GROUP_W), acc_spec, acc_spec],
        out_shape=[jax.ShapeDtypeStruct((s, GROUP_W), F32)] * 3,
        compiler_params=_att_params(False),
    )(*(pp.q_args + pp.k_args + pp.v_args + [do, rm, _tri(t, "row_gt_col"), _tri(t, "row_lt_col"), visited]))


def _split_w_in(w):
    col = lambda n: w[:, _OFF[n]:_OFF[n + 1]]
    wa = jnp.concatenate([col(0), col(1), col(2), col(4), col(5), col(6), col(10)], axis=1)
    misc = jnp.concatenate([col(3), col(9), jnp.zeros((w.shape[0], 128 - 4 - MLA_ROPE), w.dtype)], axis=1)
    wb = jnp.concatenate([col(11), col(7), col(8), misc], axis=1)
    return wa, wb


def _merge_dw_in(dwp):
    a = lambda n: dwp[:, n * GROUP_W:(n + 1) * GROUP_W]
    b0 = PA_COLS
    gate = dwp[:, b0:b0 + 1024]
    cq = dwp[:, b0 + 1024:b0 + 1280]
    ckv = dwp[:, b0 + 1280:b0 + 1408]
    flog = dwp[:, b0 + 1408:b0 + 1412]
    krot = dwp[:, b0 + 1408 + MISC_KROT:b0 + 1408 + MISC_KROT + MLA_ROPE]
    return jnp.concatenate([a(0), a(1), a(2), flog, a(3), a(4), a(5), cq, ckv, krot, a(6), gate], axis=1)


def _heads_first(w, per_head, first):
    r = w.shape[0]
    w3 = w.reshape(r, N_HEADS, per_head)
    return jnp.concatenate([w3[:, :, :first].reshape(r, -1), w3[:, :, first:].reshape(r, -1)], axis=1)


def _heads_interleaved(w, per_head, first):
    r = w.shape[0]
    a = w[:, :N_HEADS * first].reshape(r, N_HEADS, first)
    b = w[:, N_HEADS * first:].reshape(r, N_HEADS, per_head - first)
    return jnp.concatenate([a, b], axis=2).reshape(r, N_HEADS * per_head)


def _pad_rows8(a):
    return a[:, :8].T


def _local_step(x2, mem2, tgt, p):
    s = x2.shape[0]
    nm = mem2.shape[0]
    head_scale = HEAD_DIM ** -0.5
    mla_scale = (HEAD_DIM + MLA_ROPE) ** -0.5
    tabs = _rope_tables(s)
    mats = _rope_matrices()
    pairs = lambda arr, blk: [[(arr, blk + q, "pair")] for q in range(2)]
    vals = lambda arr, blk: [(arr, blk + q) for q in range(2)]

    h, hb = _ln_fwd(x2, None, p["ln_in_g"], p["ln_in_b"], "ln_in_fwd")
    _, memn_b = _ln_fwd(mem2, None, p["mem_ln_g"], p["mem_ln_b"], "ln_mem_fwd")

    saved = []
    for l in range(DEPTH):
        wa, wb = _split_w_in(p["w_in"][l])
        wp = jnp.concatenate([wa, wb], axis=1)
        wq_up = _heads_first(p["w_mla_q_up"][l], HEAD_DIM + MLA_ROPE, HEAD_DIM)
        wkv_up = _heads_first(p["w_mla_kv_up"][l], 2 * HEAD_DIM, HEAD_DIM)
        bias_row = jnp.pad(p["b_forget"][l], (0, 128 - N_HEADS)).reshape(1, 128)
        pa = _matmul(hb, wa, MXU_DTYPE, "proj_a")
        pb = _matmul(hb, wb, F32, "proj_b")

        fc = _forget_fwd(pb, bias_row)
        fbias = (fc, _pad_rows8(fc), _key_norm_max(pa, 2))
        o_fox, lse_fox = _softmax_fwd(pairs(pa, 0), pairs(pa, 2), vals(pa, 4), s, s, head_scale, True, fbias,
                                      "fox_fwd")
        o_sb, *rm_sb = _sb_fwd(pa, 6, 8, 10, s, head_scale, "sb_fwd")

        qfull, cqn = _mla_q_fwd(pb, p["mla_q_norm_g"][l], wq_up, tabs, mats)
        kv, ckvn, kr4 = _mla_kv_fwd(pb, p["mla_kv_norm_g"][l], wkv_up, tabs, mats)
        mla_q = [[(qfull, q, "pair"), (qfull, 2, "quad")] for q in range(2)]
        mla_k = [[(kv, q, "pair"), (kr4, 0, "quad")] for q in range(2)]
        o_mla, lse_mla = _softmax_fwd(mla_q, mla_k, vals(kv, 2), s, s, mla_scale, True, None, "mla_fwd")

        mkv = _matmul(memn_b, p["w_mem_kv"][l], MXU_DTYPE, "mem_kv")
        o_mem, lse_mem = _softmax_fwd(pairs(pa, 12), pairs(mkv, 0), vals(mkv, 2), s, nm, head_scale, False, None,
                                      "mem_fwd")

        groups = (o_fox, o_sb, o_mla, o_mem)
        gated = _gate_fwd(groups, pb)
        y = _matmul(gated, p["w_out"][l], F32, "out_proj")
        saved.append(dict(h=h, hb=hb, y=y, wp=wp, wq_up=wq_up, wkv_up=wkv_up, bias_row=bias_row, pa=pa, pb=pb,
                          fbias=fbias, lse_fox=lse_fox, rm_sb=rm_sb, cqn=cqn, ckvn=ckvn, mla_q=mla_q, mla_k=mla_k,
                          kv=kv, lse_mla=lse_mla, mkv=mkv, lse_mem=lse_mem, groups=groups, gated=gated))
        h, hb = _ln_fwd(h, y, p["ln_g"][l], p["ln_b"][l], "ln_fwd")

    dh, sq_cols = _loss_grad(h, tgt)
    loss_sum = jnp.sum(sq_cols)

    grads = {k: [None] * DEPTH for k in ("w_in", "b_forget", "mla_q_norm_g", "w_mla_q_up", "mla_kv_norm_g",
                                         "w_mla_kv_up", "w_mem_kv", "w_out", "ln_g", "ln_b")}
    dmemn = []
    dy1, dy2, c1 = dh, None, 1.0
    for l in reversed(range(DEPTH)):
        r = saved[l]
        pa, pb = r["pa"], r["pb"]
        o_fox, o_sb, o_mla, o_mem = r["groups"]
        du, du_b, dg, db = _ln_bwd(r["h"], r["y"], p["ln_g"][l], dy1, dy2, c1, "ln_bwd")
        grads["ln_g"][l], grads["ln_b"][l] = dg[0], db[0]
        dgated = _matmul(du_b, p["w_out"][l], F32, "out_proj_dx", "nt")
        grads["w_out"][l] = _matmul(r["gated"], du_b, F32, "out_proj_dw", "tn")
        dmixed, dgate_b = _gate_bwd(dgated, r["groups"], pb)

        dfq, dfk, dfv, dfc_q, dfc_k = _softmax_bwd(pairs(pa, 0), pairs(pa, 2), vals(pa, 4), o_fox, r["lse_fox"],
                                                   dmixed, 0, s, s, head_scale, True, r["fbias"], "fox_bwd")
        dmisc_f, dbf = _forget_bwd(pb, r["bias_row"], dfc_q + jnp.pad(dfc_k.T, ((0, 0), (0, 128 - 8))))
        grads["b_forget"][l] = dbf[0, :N_HEADS]

        dsq, dsk, dsv = _sb_bwd(pa, 6, 8, 10, dmixed, 1, *r["rm_sb"], s, head_scale, "sb_bwd")

        dqm, dkm, dvm = _softmax_bwd(r["mla_q"], r["mla_k"], vals(r["kv"], 2), o_mla, r["lse_mla"], dmixed, 2,
                                     s, s, mla_scale, True, None, "mla_bwd")
        dq_mla_b = _rope_q_bwd(dqm, tabs, mats)
        dcqn = _matmul(dq_mla_b, r["wq_up"], F32, "q_up_dx", "nt")
        grads["w_mla_q_up"][l] = _heads_interleaved(_matmul(r["cqn"], dq_mla_b, F32, "q_up_dw", "tn"),
                                                    HEAD_DIM + MLA_ROPE, HEAD_DIM)
        dcq_b, dgq = _rms_bwd(pb, MLA_Q_RANK, PB_CQ_BLK, p["mla_q_norm_g"][l], dcqn, "rms_q_bwd")
        grads["mla_q_norm_g"][l] = dgq[0]
        dkv_b = jnp.concatenate([dkm[:, :GROUP_W], dvm], axis=1).astype(MXU_DTYPE)
        dckvn = _matmul(dkv_b, r["wkv_up"], F32, "kv_up_dx", "nt")
        grads["w_mla_kv_up"][l] = _heads_interleaved(_matmul(r["ckvn"], dkv_b, F32, "kv_up_dw", "tn"),
                                                     2 * HEAD_DIM, HEAD_DIM)
        dckv_b, dgkv = _rms_bwd(pb, MLA_KV_RANK, PB_CKV_BLK, p["mla_kv_norm_g"][l], dckvn, "rms_kv_bwd")
        grads["mla_kv_norm_g"][l] = dgkv[0]
        dmisc_k = _rope_k_bwd(dkm[:, GROUP_W:], tabs, mats)

        dmq, dmk, dmv = _softmax_bwd(pairs(pa, 12), pairs(r["mkv"], 0), vals(r["mkv"], 2), o_mem, r["lse_mem"],
                                     dmixed, 3, s, nm, head_scale, False, None, "mem_bwd")
        dmkv_b = jnp.concatenate([dmk, dmv], axis=1).astype(MXU_DTYPE)
        grads["w_mem_kv"][l] = _matmul(memn_b, dmkv_b, F32, "mem_kv_dw", "tn")
        dmemn.append(_matmul(dmkv_b, p["w_mem_kv"][l], F32, "mem_kv_dx", "nt"))

        dp = [dfq, dfk, dfv, dsq, dsk, dsv, dmq, dgate_b, dcq_b, dckv_b, (dmisc_f, dmisc_k)]
        dhproj = _proj_dx(dp, r["wp"])
        grads["w_in"][l] = _merge_dw_in(_proj_dw(r["hb"], dp))
        dy1, dy2, c1 = du, dhproj, ALPHA

    dx, _, dg_in, db_in = _ln_bwd(x2, None, p["ln_in_g"], dy1, dy2, c1, "ln_in_bwd")
    _, _, dg_mem, db_mem = _ln_bwd(mem2, None, p["mem_ln_g"], dmemn[0], dmemn[1], 1.0, "ln_mem_bwd")
    out = {k: jnp.stack(v) for k, v in grads.items()}
    out.update(ln_in_g=dg_in[0], ln_in_b=db_in[0], mem_ln_g=dg_mem[0], mem_ln_b=db_mem[0])
    return loss_sum, dx, out


WIDE = "w_in"
FLAT_NAMES = ("w_out", "w_mem_kv", "w_mla_q_up", "w_mla_kv_up")
FLAT_ROWS = 896
BIG_NAMES = (WIDE,) + FLAT_NAMES
BIG_AXIS = dict(w_in=2, w_out=1, w_mem_kv=1, w_mla_q_up=2, w_mla_kv_up=2)
SMALL_NAMES = ("ln_in_g", "ln_in_b", "mem_ln_g", "mem_ln_b", "ln_g", "ln_b", "b_forget", "mla_q_norm_g",
               "mla_kv_norm_g")
ALL_NAMES = ("ln_in_g", "ln_in_b", "mem_ln_g", "mem_ln_b", "w_in", "b_forget", "mla_q_norm_g", "w_mla_q_up",
             "mla_kv_norm_g", "w_mla_kv_up", "w_mem_kv", "w_out", "ln_g", "ln_b")
N_CHIPS = 4
N_DEV = 8


def _rows_of(shape):
    rows = -(-int(np.prod(shape)) // LANES)
    return -(-rows // PACK_ALIGN) * PACK_ALIGN


def _pack(arrs, rows):
    parts = []
    for a in arrs:
        f = a.reshape(-1)
        n = _rows_of(a.shape) * LANES
        parts.append(jnp.pad(f, (0, n - f.shape[0])).reshape(-1, LANES))
    used = sum(q.shape[0] for q in parts)
    if rows > used:
        parts.append(jnp.zeros((rows - used, LANES), parts[0].dtype))
    return jnp.concatenate(parts, axis=0)


def _unpack(buf, shapes):
    out, r = [], 0
    for shp in shapes:
        n = _rows_of(shp)
        out.append(buf[r:r + n].reshape(-1)[:int(np.prod(shp))].reshape(shp))
        r += n
    return out


def _sharded_pair(get):
    wide = get(WIDE)
    return [wide.reshape(-1, wide.shape[-1]), _pack([get(n) for n in FLAT_NAMES], FLAT_ROWS)]


HBM_SPEC = pl.BlockSpec(memory_space=pltpu.HBM)


def _gather_weights(shards):
    n = len(shards)

    def body(*refs):
        w_refs, out_refs, (send_sems, recv_sems, local_sems) = refs[:n], refs[n:2 * n], refs[2 * n:]
        x, y, c = (lax.axis_index(a) for a in MESH_AXES)
        me, sibling = 2 * x + y, (x, y, 1 - c)
        chips = [(1 - x, y), (x, 1 - y), (1 - x, 1 - y)]
        local, first, passed = [], [], []
        for a in range(n):
            w_ref, out_ref, half = w_refs[a], out_refs[a], shards[a].shape[0] // 2

            def part(chip, core, out_ref=out_ref, half=half):
                return out_ref.at[chip, pl.ds(core * half, half)]

            def copy(k, src, dst, to, a=a):
                return pltpu.make_async_remote_copy(
                    src_ref=src, dst_ref=dst, send_sem=send_sems.at[6 * a + k], recv_sem=recv_sems.at[6 * a + k],
                    device_id=to, device_id_type=pl.DeviceIdType.MESH)

            local.append(pltpu.make_async_copy(w_ref, out_ref.at[me], local_sems.at[a]))
            local[-1].start()
            mine = [copy(k, w_ref.at[pl.ds(c * half, half)], part(me, c), (px, py, c))
                    for k, (px, py) in enumerate(chips)]
            for cp in mine:
                cp.start()
            first.append((mine, part, copy))
        for mine, part, copy in first:
            for k, (px, py) in enumerate(chips):
                copy(k, part(me, c), part(2 * px + py, c), (px, py, c)).wait_recv()
                passed.append(copy(3 + k, part(2 * px + py, c), part(2 * px + py, c), sibling))
                passed[-1].start()
        for mine, part, copy in first:
            for k, (px, py) in enumerate(chips):
                copy(3 + k, part(me, c), part(2 * px + py, 1 - c), sibling).wait_recv()
        for cp in [cp for mine, _, _ in first for cp in mine] + passed:
            cp.wait_send()
        for cp in local:
            cp.wait()

    return pl.pallas_call(
        body, name="gather_weights",
        out_shape=[jax.ShapeDtypeStruct((N_CHIPS,) + s.shape, s.dtype) for s in shards],
        in_specs=[HBM_SPEC] * n, out_specs=[HBM_SPEC] * n,
        scratch_shapes=[pltpu.SemaphoreType.DMA((6 * n,)), pltpu.SemaphoreType.DMA((6 * n,)),
                        pltpu.SemaphoreType.DMA((n,))],
    )(*shards)


def _exchange_grads(bigs, small):
    nb = len(bigs)
    halves = [b.shape[1] // 2 for b in bigs]
    chunks = [_pick(h, (128, 64, 32, 16)) for h in halves]

    def body(*refs):
        big_refs, small_ref = refs[:nb], refs[nb]
        sum_refs, out_refs = refs[nb + 1:2 * nb + 1], refs[2 * nb + 1:3 * nb + 2]
        scratch = refs[3 * nb + 2:]
        land_bufs, sum_bufs = scratch[:nb], scratch[nb:2 * nb]
        send_sems, recv_sems, local_sems, load_sem, swap_send, swap_recv, keep_sems = scratch[2 * nb:]
        x, y, c = (lax.axis_index(a) for a in MESH_AXES)
        me, my_chip = 4 * x + 2 * y + c, 2 * x + y
        flips = [(fx, fy, fc) for fx in (0, 1) for fy in (0, 1) for fc in (0, 1) if fx or fy or fc]
        peers = [(1 - x if fx else x, 1 - y if fy else y, 1 - c if fc else c) for fx, fy, fc in flips]

        sources = [lambda chip, core, r=big_refs[a], half=bigs[a].shape[1] // 2: r.at[chip, pl.ds(core * half, half)]
                   for a in range(nb)] + [lambda chip, core: small_ref]

        def copy(a, k, src, slot, to):
            return pltpu.make_async_remote_copy(
                src_ref=src, dst_ref=out_refs[a].at[slot], send_sem=send_sems.at[7 * a + k],
                recv_sem=recv_sems.at[7 * a + k], device_id=to, device_id_type=pl.DeviceIdType.MESH)

        own = [pltpu.make_async_copy(src(my_chip, c), out_refs[a].at[me], local_sems.at[a])
               for a, src in enumerate(sources)]
        for cp in own:
            cp.start()
        sends = [copy(a, k, src(2 * px + py, pc), me, (px, py, pc))
                 for a, src in enumerate(sources) for k, (px, py, pc) in enumerate(peers)]
        for cp in sends:
            cp.start()
        for a, src in enumerate(sources):
            for k, (px, py, pc) in enumerate(peers):
                copy(a, k, src(my_chip, c), 4 * px + 2 * py + pc, (px, py, pc)).wait_recv()
        for cp in sends:
            cp.wait_send()
        for cp in own:
            cp.wait()

        tails = []
        for a in range(nb):
            for r0 in range(0, halves[a], chunks[a]):
                load = pltpu.make_async_copy(out_refs[a].at[:, pl.ds(r0, chunks[a])], land_bufs[a], load_sem)
                load.start()
                load.wait()
                total = land_bufs[a][0].astype(F32)
                for d in range(1, N_DEV):
                    total = total + land_bufs[a][d].astype(F32)
                sum_bufs[a][pl.ds(r0, chunks[a]), :] = total
            keep = pltpu.make_async_copy(sum_bufs[a], sum_refs[a].at[c], keep_sems.at[a])
            give = pltpu.make_async_remote_copy(
                src_ref=sum_bufs[a], dst_ref=sum_refs[a].at[c], send_sem=swap_send.at[a], recv_sem=swap_recv.at[a],
                device_id=(x, y, 1 - c), device_id_type=pl.DeviceIdType.MESH)
            keep.start()
            give.start()
            tails.append((keep, give))
        for a, (keep, give) in enumerate(tails):
            pltpu.make_async_remote_copy(
                src_ref=sum_bufs[a], dst_ref=sum_refs[a].at[1 - c], send_sem=swap_send.at[a], recv_sem=swap_recv.at[a],
                device_id=(x, y, 1 - c), device_id_type=pl.DeviceIdType.MESH).wait_recv()
            give.wait_send()
            keep.wait()

    return pl.pallas_call(
        body, name="exchange_grads",
        out_shape=[jax.ShapeDtypeStruct((2, h, b.shape[2]), F32) for h, b in zip(halves, bigs)]
        + [jax.ShapeDtypeStruct((N_DEV, h, b.shape[2]), b.dtype) for h, b in zip(halves, bigs)]
        + [jax.ShapeDtypeStruct((N_DEV,) + small.shape, small.dtype)],
        in_specs=[HBM_SPEC] * (nb + 1), out_specs=[HBM_SPEC] * (2 * nb + 1),
        scratch_shapes=[pltpu.VMEM((N_DEV, ch, b.shape[2]), b.dtype) for ch, b in zip(chunks, bigs)]
        + [pltpu.VMEM((h, b.shape[2]), F32) for h, b in zip(halves, bigs)]
        + [pltpu.SemaphoreType.DMA((7 * (nb + 1),)), pltpu.SemaphoreType.DMA((7 * (nb + 1),)),
           pltpu.SemaphoreType.DMA((nb + 1,)), pltpu.SemaphoreType.DMA, pltpu.SemaphoreType.DMA((nb,)),
           pltpu.SemaphoreType.DMA((nb,)), pltpu.SemaphoreType.DMA((nb,))],
        compiler_params=pltpu.CompilerParams(vmem_limit_bytes=VMEM_LIMIT),
    )(*bigs, small)


def _adamw(parts, w, m, v, name):
    rows, width = w.shape
    n_parts = parts.shape[0]
    tile = _pick(rows, (128, 16, 8))
    bc1 = 1.0 - ADAM_B1 ** ADAM_STEP
    bc2 = 1.0 - ADAM_B2 ** ADAM_STEP

    def kern(p_ref, w_ref, m_ref, v_ref, g_ref, d_ref, nm_ref, nv_ref):
        g = p_ref[0].astype(F32)
        for d in range(1, n_parts):
            g = g + p_ref[d].astype(F32)
        nm = ADAM_B1 * m_ref[...] + (1.0 - ADAM_B1) * g
        nv = ADAM_B2 * v_ref[...] + (1.0 - ADAM_B2) * (g * g)
        g_ref[...] = g
        nm_ref[...] = nm
        nv_ref[...] = nv
        d_ref[...] = -ADAM_LR * ((nm / bc1) / (jnp.sqrt(nv / bc2) + ADAM_EPS) + ADAM_WD * w_ref[...])

    spec = pl.BlockSpec((tile, width), lambda i: (i, 0))
    return pl.pallas_call(
        kern, name=name, grid=(rows // tile,),
        in_specs=[pl.BlockSpec((n_parts, tile, width), lambda i: (0, i, 0)), spec, spec, spec],
        out_specs=[spec] * 4, out_shape=[jax.ShapeDtypeStruct((rows, width), F32)] * 4,
        compiler_params=pltpu.CompilerParams(dimension_semantics=("parallel",), vmem_limit_bytes=VMEM_LIMIT),
    )(parts, w, m, v)


def kernel(x, mem, ln_in_g, ln_in_b, mem_ln_g, mem_ln_b, w_in, b_forget, mla_q_norm_g, w_mla_q_up, mla_kv_norm_g, w_mla_kv_up, w_mem_kv, w_out, ln_g, ln_b, loss_target, m_ln_in_g, m_ln_in_b, m_mem_ln_g, m_mem_ln_b, m_w_in, m_b_forget, m_mla_q_norm_g, m_w_mla_q_up, m_mla_kv_norm_g, m_w_mla_kv_up, m_w_mem_kv, m_w_out, m_ln_g, m_ln_b, v_ln_in_g, v_ln_in_b, v_mem_ln_g, v_mem_ln_b, v_w_in, v_b_forget, v_mla_q_norm_g, v_w_mla_q_up, v_mla_kv_norm_g, v_w_mla_kv_up, v_w_mem_kv, v_w_out, v_ln_g, v_ln_b):
    w = dict(ln_in_g=ln_in_g, ln_in_b=ln_in_b, mem_ln_g=mem_ln_g, mem_ln_b=mem_ln_b, w_in=w_in, b_forget=b_forget,
             mla_q_norm_g=mla_q_norm_g, w_mla_q_up=w_mla_q_up, mla_kv_norm_g=mla_kv_norm_g,
             w_mla_kv_up=w_mla_kv_up, w_mem_kv=w_mem_kv, w_out=w_out, ln_g=ln_g, ln_b=ln_b)
    mo = dict(ln_in_g=m_ln_in_g, ln_in_b=m_ln_in_b, mem_ln_g=m_mem_ln_g, mem_ln_b=m_mem_ln_b, w_in=m_w_in,
              b_forget=m_b_forget, mla_q_norm_g=m_mla_q_norm_g, w_mla_q_up=m_w_mla_q_up,
              mla_kv_norm_g=m_mla_kv_norm_g, w_mla_kv_up=m_w_mla_kv_up, w_mem_kv=m_w_mem_kv, w_out=m_w_out,
              ln_g=m_ln_g, ln_b=m_ln_b)
    vo = dict(ln_in_g=v_ln_in_g, ln_in_b=v_ln_in_b, mem_ln_g=v_mem_ln_g, mem_ln_b=v_mem_ln_b, w_in=v_w_in,
              b_forget=v_b_forget, mla_q_norm_g=v_mla_q_norm_g, w_mla_q_up=v_w_mla_q_up,
              mla_kv_norm_g=v_mla_kv_norm_g, w_mla_kv_up=v_w_mla_kv_up, w_mem_kv=v_w_mem_kv, w_out=v_w_out,
              ln_g=v_ln_g, ln_b=v_ln_b)
    flat_shapes = [w[n].shape for n in FLAT_NAMES]
    small_shapes = [w[n].shape for n in SMALL_NAMES]

    got_wide, got_flat = _gather_weights(_sharded_pair(lambda n: w[n].astype(MXU_DTYPE)))
    full = dict(w)
    full[WIDE] = jnp.concatenate([got_wide[j] for j in range(N_CHIPS)], axis=1).reshape(
        w[WIDE].shape[:2] + (N_CHIPS * w[WIDE].shape[2],))
    per_chip = [_unpack(got_flat[j], flat_shapes) for j in range(N_CHIPS)]
    for idx, n in enumerate(FLAT_NAMES):
        full[n] = jnp.concatenate([per_chip[j][idx] for j in range(N_CHIPS)], axis=BIG_AXIS[n])

    loss_sum, dx, g = _local_step(x[0], mem[0], loss_target[0], full)
    loss = lax.psum(loss_sum * (0.5 / D_MODEL), MESH_AXES)

    def shard_of(n, j):
        ax, size = BIG_AXIS[n], w[n].shape[BIG_AXIS[n]]
        return lax.slice_in_dim(g[n], j * size, (j + 1) * size, axis=ax).astype(MXU_DTYPE)

    per_dest = [_sharded_pair(lambda n, j=j: shard_of(n, j)) for j in range(N_CHIPS)]
    bigs = [jnp.stack([per_dest[j][a] for j in range(N_CHIPS)]) for a in range(2)]
    exchanged = _exchange_grads(bigs, _pack([g[n] for n in SMALL_NAMES], SMALL_ROWS))
    halves, small_parts = exchanged[:2], exchanged[-1]

    res = []
    for a, (grad, nm) in enumerate(zip(halves, ("adamw_wide", "adamw_flat"))):
        state = [_sharded_pair(lambda n, src=src: src[n])[a] for src in (w, mo, vo)]
        res.append(_adamw(grad.reshape((1,) + state[0].shape), *state, nm))
    res_small = _adamw(small_parts, *[_pack([src[n] for n in SMALL_NAMES], SMALL_ROWS) for src in (w, mo, vo)],
                       "adamw_replicated")
    outs = []
    for kind in range(4):
        vals = {WIDE: res[0][kind].reshape(w[WIDE].shape)}
        vals.update(zip(FLAT_NAMES, _unpack(res[1][kind], flat_shapes)))
        vals.update(zip(SMALL_NAMES, _unpack(res_small[kind], small_shapes)))
        outs += [vals[n] for n in ALL_NAMES]
    return (loss, dx[None], *outs)
```

```python
import functools

import numpy as np
import jax
import jax.numpy as jnp
from jax import lax
from jax.experimental import pallas as pl
from jax.experimental.pallas import tpu as pltpu

F32 = jnp.float32
MXU_DTYPE = jnp.bfloat16

DEPTH = 2
D_MODEL = 1024
GROUP_W = 256
N_HEADS = 4
HEAD_DIM = 64
MLA_Q_RANK = 256
MLA_KV_RANK = 128
MLA_ROPE = 32
MLA_Q_COLS = N_HEADS * (HEAD_DIM + MLA_ROPE)
MLA_KV_COLS = N_HEADS * 2 * HEAD_DIM
ROPE_THETA = 10000.0
LN_EPS = 1e-5
RMS_EPS = 1e-6
ALPHA = (2 * DEPTH) ** 0.25
ADAM_LR, ADAM_B1, ADAM_B2, ADAM_EPS, ADAM_WD, ADAM_STEP = 0.001, 0.9, 0.999, 1e-08, 0.01, 10

_SPLIT = (256, 256, 256, 4, 256, 256, 256, 256, 128, 32, 256, 1024)
_OFF = [int(o) for o in np.cumsum((0,) + _SPLIT)]
IN_COLS = _OFF[-1]
PA_COLS = 7 * GROUP_W
PB_COLS = 1024 + 256 + 128 + 128
PB_CQ_BLK, PB_CKV_BLK, PB_MISC_BLK = 4, 10, 11
MISC_KROT = 4

LANES = 1024
PACK_ALIGN = 16
SMALL_ROWS = 144
ROW_TILE = 256
PROJ_BWD_ROWS = 512
ATT_TILE = 128
SB_QUERY_TILE = 128
SOFTMAX_TILE = 512
FWD_PAIRS_PER_LOOP = 1
PAIR = 128
SB_SLOT = PAIR // N_HEADS
VMEM_LIMIT = 56 * 1024 * 1024
MATMUL_VMEM = 30 * 1024 * 1024
NEG = -1e30
EXP_UNDERFLOW = -104.0
DEAD_LOGIT = -110.0
REACH_SLACK = 1.0 + 2.0 ** -10
MESH_AXES = ("x", "y", "c")


def _dot(a, b):
    return jnp.dot(a, b, preferred_element_type=F32)


def _dot_nt(a, b):
    return lax.dot_general(a, b, (((1,), (1,)), ((), ())), preferred_element_type=F32)


def _dot_tn(a, b):
    return lax.dot_general(a, b, (((0,), (0,)), ((), ())), preferred_element_type=F32)


def _split2(x):
    hi = x.astype(MXU_DTYPE)
    lo = (x - hi.astype(F32)).astype(MXU_DTYPE)
    return hi, lo


def _split3(x):
    hi = x.astype(MXU_DTYPE)
    r = x - hi.astype(F32)
    mid = r.astype(MXU_DTYPE)
    lo = (r - mid.astype(F32)).astype(MXU_DTYPE)
    return hi, mid, lo


def _dot_exact_r(x, pm):
    hi, mid, lo = _split3(x)
    return _dot(hi, pm) + _dot(mid, pm) + _dot(lo, pm)


def _dot_exact_l(pm, x):
    hi, mid, lo = _split3(x)
    return _dot(pm, hi) + _dot(pm, mid) + _dot(pm, lo)


def _pick(dim, prefs):
    for p in prefs:
        if dim % p == 0:
            return p
    return dim


def _softplus(z):
    return jnp.maximum(z, 0.0) + jnp.log(1.0 + jnp.exp(-jnp.abs(z)))


def _tile_options(dim):
    opts = [d for d in range(128, min(dim, 2048) + 1, 128) if dim % d == 0]
    return opts or [dim]


def _matmul_tiles(m, n, k, out_bytes):
    tk = k if k <= 4096 else _pick(k, (1024, 512, 256, 128))
    best = None
    for tm in _tile_options(m):
        for tn in _tile_options(n):
            vmem = 2 * 2 * (tm * tk + tk * tn) + 4 * tm * tn + 2 * out_bytes * tm * tn
            if vmem <= MATMUL_VMEM and (best is None or tm * tn / (tm + tn) > best[0]):
                best = (tm * tn / (tm + tn), tm, tn)
    return best[1], best[2], tk


def _matmul(a, b, out_dtype, name, mode="nn"):
    m, k = (a.shape[1], a.shape[0]) if mode == "tn" else a.shape
    n = b.shape[0] if mode == "nt" else b.shape[1]
    tm, tn, tk = _matmul_tiles(m, n, k, jnp.dtype(out_dtype).itemsize)
    nk = k // tk
    dot = {"nn": _dot, "tn": _dot_tn, "nt": _dot_nt}[mode]

    def kern(a_ref, b_ref, o_ref, *acc):
        if nk == 1:
            o_ref[...] = dot(a_ref[...], b_ref[...]).astype(o_ref.dtype)
            return
        acc_ref, = acc
        kk = pl.program_id(2)

        @pl.when(kk == 0)
        def _():
            acc_ref[...] = jnp.zeros_like(acc_ref)

        acc_ref[...] += dot(a_ref[...], b_ref[...])

        @pl.when(kk == nk - 1)
        def _():
            o_ref[...] = acc_ref[...].astype(o_ref.dtype)

    a_spec = (pl.BlockSpec((tk, tm), lambda i, j, kk: (kk, i)) if mode == "tn"
              else pl.BlockSpec((tm, tk), lambda i, j, kk: (i, kk)))
    b_spec = (pl.BlockSpec((tn, tk), lambda i, j, kk: (j, kk)) if mode == "nt"
              else pl.BlockSpec((tk, tn), lambda i, j, kk: (kk, j)))
    return pl.pallas_call(
        kern, name=name, grid=(m // tm, n // tn, nk), in_specs=[a_spec, b_spec],
        out_specs=pl.BlockSpec((tm, tn), lambda i, j, kk: (i, j)),
        out_shape=jax.ShapeDtypeStruct((m, n), out_dtype),
        scratch_shapes=[pltpu.VMEM((tm, tn), F32)] if nk > 1 else [],
        compiler_params=pltpu.CompilerParams(
            dimension_semantics=("parallel", "parallel", "arbitrary"), vmem_limit_bytes=VMEM_LIMIT),
    )(a.astype(MXU_DTYPE), b.astype(MXU_DTYPE))


def _piece_arrays(pieces):
    return [a for p in pieces for a in (p if isinstance(p, tuple) else (p,))]


def _join_pieces(refs, pieces):
    refs, cols = list(refs), []
    for p in pieces:
        vals = [refs.pop(0)[...] for _ in (p if isinstance(p, tuple) else (p,))]
        cols.append(functools.reduce(jnp.add, vals).astype(MXU_DTYPE))
    return jnp.concatenate(cols, axis=1)


def _proj_dx(pieces, w):
    arrs = _piece_arrays(pieces)
    rows, n = arrs[0].shape[0], w.shape[0]
    tm = min(PROJ_BWD_ROWS, rows)

    def kern(*refs):
        refs[-1][...] = _dot_nt(_join_pieces(refs[:len(arrs)], pieces), refs[len(arrs)][...])

    return pl.pallas_call(
        kern, name="proj_dx", grid=(rows // tm,),
        in_specs=[pl.BlockSpec((tm, a.shape[1]), lambda i: (i, 0)) for a in arrs]
        + [pl.BlockSpec(w.shape, lambda i: (0, 0), pipeline_mode=pl.Buffered(1))],
        out_specs=pl.BlockSpec((tm, n), lambda i: (i, 0)), out_shape=jax.ShapeDtypeStruct((rows, n), F32),
        compiler_params=pltpu.CompilerParams(dimension_semantics=("parallel",), vmem_limit_bytes=VMEM_LIMIT),
    )(*arrs, w.astype(MXU_DTYPE))


def _proj_dw(a, pieces):
    arrs = _piece_arrays(pieces)
    rows, m = a.shape
    k = sum(x.shape[1] for x in (p[0] if isinstance(p, tuple) else p for p in pieces))
    tk = min(PROJ_BWD_ROWS, rows)

    def kern(*refs):
        o_ref = refs[-1]

        @pl.when(pl.program_id(0) == 0)
        def _():
            o_ref[...] = jnp.zeros_like(o_ref)

        o_ref[...] += _dot_tn(refs[0][...], _join_pieces(refs[1:1 + len(arrs)], pieces))

    return pl.pallas_call(
        kern, name="proj_dw", grid=(rows // tk,),
        in_specs=[pl.BlockSpec((tk, m), lambda i: (i, 0))]
        + [pl.BlockSpec((tk, x.shape[1]), lambda i: (i, 0)) for x in arrs],
        out_specs=pl.BlockSpec((m, k), lambda i: (0, 0), pipeline_mode=pl.Buffered(1)),
        out_shape=jax.ShapeDtypeStruct((m, k), F32),
        compiler_params=pltpu.CompilerParams(dimension_semantics=("arbitrary",), vmem_limit_bytes=VMEM_LIMIT),
    )(a, *arrs)


def _rowwise(body, name, rows, tile, row_ins, full_ins, row_outs, acc_outs=(), scratch=(),
             reverse=False, sequential=False):
    n = rows // tile

    def ridx(i):
        return (n - 1 - i) if reverse else i

    in_specs, args = [], []
    for arr, width, cb in row_ins:
        in_specs.append(pl.BlockSpec((tile, width), lambda i, cb=cb: (ridx(i), cb)))
        args.append(arr)
    for arr in full_ins:
        in_specs.append(pl.BlockSpec(arr.shape, lambda i, nd=arr.ndim: (0,) * nd))
        args.append(arr)
    out_shape = [jax.ShapeDtypeStruct((rows, w), dt) for w, dt in row_outs]
    out_shape += [jax.ShapeDtypeStruct(s, dt) for s, dt in acc_outs]
    out_specs = [pl.BlockSpec((tile, w), lambda i: (ridx(i), 0)) for w, dt in row_outs]
    out_specs += [pl.BlockSpec(s, lambda i, nd=len(s): (0,) * nd) for s, dt in acc_outs]

    def kern(*refs):
        body(pl.program_id(0), *refs)

    sem = "arbitrary" if (acc_outs or sequential) else "parallel"
    return pl.pallas_call(
        kern, name=name, grid=(n,), in_specs=in_specs, out_specs=out_specs, out_shape=out_shape,
        scratch_shapes=list(scratch),
        compiler_params=pltpu.CompilerParams(dimension_semantics=(sem,), vmem_limit_bytes=VMEM_LIMIT),
    )(*args)


def _ln_stats(u):
    mu = jnp.mean(u, axis=-1, keepdims=True)
    xc = u - mu
    var = jnp.mean(xc * xc, axis=-1, keepdims=True)
    return xc, lax.rsqrt(var + LN_EPS)


def _ln_fwd(a, b, g, beta, name):
    rows, d = a.shape
    has_b = b is not None

    def body(i, *refs):
        if has_b:
            a_ref, b_ref, g_ref, be_ref, h_ref, hb_ref = refs
            u = ALPHA * a_ref[...] + b_ref[...]
        else:
            a_ref, g_ref, be_ref, h_ref, hb_ref = refs
            u = a_ref[...]
        xc, rstd = _ln_stats(u)
        y = xc * rstd * g_ref[...] + be_ref[...]
        h_ref[...] = y
        hb_ref[...] = y.astype(hb_ref.dtype)

    row_ins = [(a, d, 0)] + ([(b, d, 0)] if has_b else [])
    return _rowwise(body, name, rows, min(ROW_TILE, rows), row_ins,
                    [g.reshape(1, d), beta.reshape(1, d)], [(d, F32), (d, MXU_DTYPE)])


def _ln_bwd(a, b, g, dy1, dy2, c1, name):
    rows, d = a.shape
    has_b = b is not None
    has_2 = dy2 is not None

    def body(i, *refs):
        refs = list(refs)
        a_ref = refs.pop(0)
        u = a_ref[...]
        if has_b:
            u = ALPHA * u + refs.pop(0)[...]
        dy = c1 * refs.pop(0)[...]
        if has_2:
            dy = dy + refs.pop(0)[...]
        g_ref, du_ref, dub_ref, dg_ref, db_ref = refs

        @pl.when(i == 0)
        def _():
            dg_ref[...] = jnp.zeros_like(dg_ref)
            db_ref[...] = jnp.zeros_like(db_ref)

        xc, rstd = _ln_stats(u)
        xhat = xc * rstd
        dxh = dy * g_ref[...]
        m1 = jnp.mean(dxh, axis=-1, keepdims=True)
        m2 = jnp.mean(dxh * xhat, axis=-1, keepdims=True)
        du = rstd * (dxh - m1 - xhat * m2)
        du_ref[...] = du
        dub_ref[...] = du.astype(dub_ref.dtype)
        dg_ref[...] += jnp.sum(dy * xhat, axis=0, keepdims=True)
        db_ref[...] += jnp.sum(dy, axis=0, keepdims=True)

    row_ins = [(a, d, 0)] + ([(b, d, 0)] if has_b else []) + [(dy1, d, 0)] + ([(dy2, d, 0)] if has_2 else [])
    return _rowwise(body, name, rows, min(ROW_TILE, rows), row_ins, [g.reshape(1, d)],
                    [(d, F32), (d, MXU_DTYPE)], [((1, d), F32), ((1, d), F32)])


def _loss_grad(h, target):
    rows, d = h.shape

    def body(i, h_ref, t_ref, dh_ref, acc_ref):
        @pl.when(i == 0)
        def _():
            acc_ref[...] = jnp.zeros_like(acc_ref)

        e = h_ref[...] - t_ref[...]
        dh_ref[...] = e * (1.0 / d)
        acc_ref[...] += jnp.sum(e * e, axis=0, keepdims=True)

    return _rowwise(body, "loss_grad", rows, ROW_TILE, [(h, d, 0), (target, d, 0)], [],
                    [(d, F32)], [((1, d), F32)])


def _gate_fwd(groups, pb):
    rows = pb.shape[0]
    w = GROUP_W * len(groups)

    def body(i, *refs):
        g = refs[4][...]
        mixed = jnp.concatenate([r[...] for r in refs[:4]], axis=1)
        refs[5][...] = (mixed * (g / (1.0 + jnp.exp(-g)))).astype(refs[5].dtype)

    return _rowwise(body, "gate_fwd", rows, ROW_TILE, [(o, GROUP_W, 0) for o in groups] + [(pb, w, 0)], [],
                    [(w, MXU_DTYPE)])[0]


def _gate_bwd(dgated, groups, pb):
    rows = pb.shape[0]
    w = GROUP_W * len(groups)

    def body(i, *refs):
        dg = refs[0][...]
        mixed = jnp.concatenate([r[...] for r in refs[1:5]], axis=1)
        g = refs[5][...]
        dm_ref, dgate_ref = refs[6], refs[7]
        sig = 1.0 / (1.0 + jnp.exp(-g))
        dm_ref[...] = dg * (g * sig)
        dgate_ref[...] = (dg * mixed * (sig * (1.0 + g * (1.0 - sig)))).astype(dgate_ref.dtype)

    return _rowwise(body, "gate_bwd", rows, ROW_TILE,
                    [(dgated, w, 0)] + [(o, GROUP_W, 0) for o in groups] + [(pb, w, 0)], [],
                    [(w, F32), (w, MXU_DTYPE)])


def _tri(n, kind):
    r = np.arange(n)[:, None]
    c = np.arange(n)[None, :]
    m = {"lower_incl": r >= c, "upper_incl": r <= c, "row_gt_col": r > c, "row_lt_col": r < c}[kind]
    return jnp.asarray(m.astype(np.float32), dtype=MXU_DTYPE)


def _forget_fwd(pb, bias_row):
    rows = pb.shape[0]
    tile = _pick(rows, (1024, 512, 256))

    def body(i, x_ref, b_ref, l_ref, o_ref, carry_ref):
        @pl.when(i == 0)
        def _():
            carry_ref[...] = jnp.zeros_like(carry_ref)

        xx = x_ref[...] + b_ref[...]
        lane = lax.broadcasted_iota(jnp.int32, xx.shape, 1)
        lf = jnp.where(lane < N_HEADS, -_softplus(-xx), 0.0)
        o_ref[...] = _dot_exact_l(l_ref[...], lf) + carry_ref[...]
        carry_ref[...] += jnp.sum(lf, axis=0, keepdims=True)

    return _rowwise(body, "forget_fwd", rows, tile, [(pb, 128, PB_MISC_BLK)],
                    [bias_row, _tri(tile, "lower_incl")], [(128, F32)],
                    scratch=[pltpu.VMEM((1, 128), F32)], sequential=True)[0]


def _forget_bwd(pb, bias_row, dfc):
    rows = pb.shape[0]
    tile = _pick(rows, (1024, 512, 256))

    def body(i, x_ref, df_ref, b_ref, u_ref, o_ref, db_ref, carry_ref):
        @pl.when(i == 0)
        def _():
            carry_ref[...] = jnp.zeros_like(carry_ref)
            db_ref[...] = jnp.zeros_like(db_ref)

        df = df_ref[...]
        sfx = _dot_exact_l(u_ref[...], df) + carry_ref[...]
        carry_ref[...] += jnp.sum(df, axis=0, keepdims=True)
        xx = x_ref[...] + b_ref[...]
        lane = lax.broadcasted_iota(jnp.int32, xx.shape, 1)
        dl = jnp.where(lane < N_HEADS, sfx / (1.0 + jnp.exp(xx)), 0.0)
        o_ref[...] = dl
        db_ref[...] += jnp.sum(dl, axis=0, keepdims=True)

    return _rowwise(body, "forget_bwd", rows, tile,
                    [(pb, 128, PB_MISC_BLK), (dfc, 128, 0)],
                    [bias_row, _tri(tile, "upper_incl")], [(128, F32)], [((1, 128), F32)],
                    scratch=[pltpu.VMEM((1, 128), F32)], reverse=True, sequential=True)


def _rope_tables(s):
    half = MLA_ROPE // 2
    inv_freq = ROPE_THETA ** (-jnp.arange(half, dtype=F32) / half)
    ang = jnp.arange(s).astype(F32)[:, None] * inv_freq[None, :]
    cos2 = jnp.tile(jnp.cos(ang), (1, 2))
    sin2 = jnp.tile(jnp.sin(ang), (1, 2))
    cx = jnp.tile(cos2, (1, N_HEADS))
    sx = jnp.tile(sin2, (1, N_HEADS))
    pad = ((0, 0), (MISC_KROT, 128 - MISC_KROT - MLA_ROPE))
    ck = jnp.pad(cos2, pad)
    sk = jnp.pad(sin2, pad)
    return dict(ck=ck, sk=sk, cx=cx, sx=sx)


def _rot_matrix(width, bases):
    half = MLA_ROPE // 2
    p = np.zeros((width, width), np.float32)
    for b in bases:
        for i in range(half):
            p[b + half + i, b + i] = -1.0
            p[b + i, b + half + i] = 1.0
    return p


def _rope_matrices():
    pk = _rot_matrix(128, [MISC_KROT])
    p4 = _rot_matrix(128, [h * MLA_ROPE for h in range(N_HEADS)])
    a = np.zeros((128, 128), np.float32)
    for h in range(N_HEADS):
        for r in range(MLA_ROPE):
            a[h * MLA_ROPE + r, MISC_KROT + r] = 1.0
    cast = lambda m: jnp.asarray(m, dtype=MXU_DTYPE)
    return dict(p4=cast(p4), p4t=cast(p4.T), pk=cast(pk), spread=cast(a.T), xa=cast(a), xb=cast(p4.T @ a))


def _rms(c, g):
    r = lax.rsqrt(jnp.mean(c * c, axis=-1, keepdims=True) + RMS_EPS)
    return c * r * g


def _mla_q_fwd(pb, g, w_up, tabs, mats):
    rows = pb.shape[0]

    def body(i, c_ref, cos_ref, sin_ref, g_ref, w_ref, p_ref, q_ref, cn_ref):
        cn = _rms(c_ref[...], g_ref[...]).astype(cn_ref.dtype)
        cn_ref[...] = cn
        q = _dot(cn, w_ref[...])
        qr = q[:, GROUP_W:]
        q_ref[:, :GROUP_W] = q[:, :GROUP_W].astype(q_ref.dtype)
        q_ref[:, GROUP_W:] = (qr * cos_ref[...] + _dot_exact_r(qr, p_ref[...]) * sin_ref[...]).astype(q_ref.dtype)

    return _rowwise(body, "mla_q_fwd", rows, ROW_TILE,
                    [(pb, MLA_Q_RANK, PB_CQ_BLK), (tabs["cx"], PAIR, 0), (tabs["sx"], PAIR, 0)],
                    [g.reshape(1, MLA_Q_RANK), w_up.astype(MXU_DTYPE), mats["p4"]],
                    [(MLA_Q_COLS, MXU_DTYPE), (MLA_Q_RANK, MXU_DTYPE)])


def _mla_kv_fwd(pb, g, w_up, tabs, mats):
    rows = pb.shape[0]

    def body(i, c_ref, x_ref, cos_ref, sin_ref, g_ref, w_ref, p_ref, sp_ref, kv_ref, cn_ref, kr_ref):
        cn = _rms(c_ref[...], g_ref[...]).astype(cn_ref.dtype)
        cn_ref[...] = cn
        kv_ref[...] = _dot(cn, w_ref[...]).astype(kv_ref.dtype)
        xx = x_ref[...]
        kr = xx * cos_ref[...] + _dot_exact_r(xx, p_ref[...]) * sin_ref[...]
        kr_ref[...] = _dot_exact_r(kr, sp_ref[...]).astype(kr_ref.dtype)

    return _rowwise(body, "mla_kv_fwd", rows, ROW_TILE,
                    [(pb, MLA_KV_RANK, PB_CKV_BLK), (pb, 128, PB_MISC_BLK), (tabs["ck"], 128, 0), (tabs["sk"], 128, 0)],
                    [g.reshape(1, MLA_KV_RANK), w_up.astype(MXU_DTYPE), mats["pk"], mats["spread"]],
                    [(MLA_KV_COLS, MXU_DTYPE), (MLA_KV_RANK, MXU_DTYPE), (128, MXU_DTYPE)])


def _rope_q_bwd(dq_full, tabs, mats):
    rows, nq = dq_full.shape

    def body(i, d_ref, cos_ref, sin_ref, pt_ref, o_ref):
        dr = d_ref[:, GROUP_W:]
        o_ref[:, :GROUP_W] = d_ref[:, :GROUP_W].astype(o_ref.dtype)
        o_ref[:, GROUP_W:] = (dr * cos_ref[...] + _dot_exact_r(dr * sin_ref[...], pt_ref[...])).astype(o_ref.dtype)

    return _rowwise(body, "rope_q_bwd", rows, ROW_TILE,
                    [(dq_full, nq, 0), (tabs["cx"], PAIR, 0), (tabs["sx"], PAIR, 0)], [mats["p4t"]],
                    [(nq, MXU_DTYPE)])[0]


def _rope_k_bwd(dkr, tabs, mats):
    rows = dkr.shape[0]

    def body(i, d_ref, cos_ref, sin_ref, a_ref, b_ref, o_ref):
        d = d_ref[...]
        o_ref[...] = _dot_exact_r(d * cos_ref[...], a_ref[...]) + _dot_exact_r(d * sin_ref[...], b_ref[...])

    return _rowwise(body, "rope_k_bwd", rows, ROW_TILE,
                    [(dkr, 128, 0), (tabs["cx"], 128, 0), (tabs["sx"], 128, 0)], [mats["xa"], mats["xb"]],
                    [(128, F32)])[0]


def _rms_bwd(pb, width, col_blk, g, dy, name):
    rows = pb.shape[0]

    def body(i, c_ref, dy_ref, g_ref, dc_ref, dg_ref):
        @pl.when(i == 0)
        def _():
            dg_ref[...] = jnp.zeros_like(dg_ref)

        c = c_ref[...]
        dy = dy_ref[...]
        r = lax.rsqrt(jnp.mean(c * c, axis=-1, keepdims=True) + RMS_EPS)
        dyg = dy * g_ref[...]
        dc = r * dyg - c * (r * r * r) * jnp.mean(c * dyg, axis=-1, keepdims=True)
        dc_ref[...] = dc.astype(dc_ref.dtype)
        dg_ref[...] += jnp.sum(dy * c * r, axis=0, keepdims=True)

    return _rowwise(body, name, rows, ROW_TILE, [(pb, width, col_blk), (dy, width, 0)], [g.reshape(1, width)],
                    [(width, MXU_DTYPE)], [((1, width), F32)])


def _att_params(parallel):
    return pltpu.CompilerParams(dimension_semantics=("parallel" if parallel else "arbitrary",),
                                vmem_limit_bytes=VMEM_LIMIT)


def _blk_off(j, t):
    return j * t if isinstance(j, int) else pl.multiple_of(j * t, t)


def _causal_mask(t, strict):
    r = lax.broadcasted_iota(jnp.int32, (t, t), 0)
    c = lax.broadcasted_iota(jnp.int32, (t, t), 1)
    return (c < r) if strict else (c <= r)


def _lane_mask(kind, head, rows):
    lane = lax.broadcasted_iota(jnp.int32, (rows, PAIR), 1)
    if kind == "pair":
        return (lane < HEAD_DIM) if head % 2 == 0 else (lane >= HEAD_DIM)
    return (lane >= MLA_ROPE * head) & (lane < MLA_ROPE * (head + 1))


def _row_spec(t, cb, width=PAIR):
    return pl.BlockSpec((t, width), lambda i, cb=cb: (i, cb))


def _whole_spec(rows, cb, width=PAIR):
    return pl.BlockSpec((rows, width), lambda i, cb=cb: (0, cb), pipeline_mode=pl.Buffered(1))


def _is_pow2(x):
    return float(np.frexp(x)[0]) == 0.5


def _masked_heads(blocks, kinds, pair, dtype, scale=None):
    out = []
    for e in range(2):
        head = 2 * pair + e
        parts = [jnp.where(_lane_mask(k, head, b.shape[0]), b.astype(F32) * (1.0 if scale is None else scale),
                           0.0).astype(dtype)
                 for b, k in zip(blocks, kinds)]
        out.append(parts[0] if len(parts) == 1 else jnp.concatenate(parts, axis=1))
    return out


def _logit_reach(qh, kmax2, head):
    q32 = qh.astype(F32)
    return jnp.sqrt(jnp.sum(q32 * q32, axis=1, keepdims=True) * _col(kmax2, head)) * REACH_SLACK


def _forget_top(ft_ref, head, off):
    return jnp.max(-ft_ref[head:head + 1, pl.ds(off, PAIR)])


def _col(block, idx):
    lane = lax.broadcasted_iota(jnp.int32, block.shape, 1)
    return jnp.sum(jnp.where(lane == idx, block, 0.0), axis=1, keepdims=True)


def _scatter_cols(cols, t):
    lane = lax.broadcasted_iota(jnp.int32, (t, PAIR), 1)
    out = jnp.zeros((t, PAIR), F32)
    for idx, c in cols.items():
        out = out + jnp.where(lane == idx, c, 0.0)
    return out


def _take_heads(per_head, pair):
    return jnp.where(_lane_mask("pair", 0, per_head[0].shape[0]), per_head[0], per_head[1])


class _Parts:
    def __init__(self, q_parts, k_parts, v_parts, tq, sk):
        self.kinds = [[kind for _, _, kind in q_parts[p]] for p in range(2)]
        self.nparts = len(q_parts[0])
        self.q_specs = [_row_spec(tq, cb) for p in range(2) for _, cb, _ in q_parts[p]]
        self.q_args = [a for p in range(2) for a, _, _ in q_parts[p]]
        self.k_specs = [_whole_spec(sk, cb) for p in range(2) for _, cb, _ in k_parts[p]]
        self.k_args = [a for p in range(2) for a, _, _ in k_parts[p]]
        self.v_specs = [_whole_spec(sk, cb) for _, cb in v_parts]
        self.v_args = [a for a, _ in v_parts]
        self.width = PAIR * self.nparts

    def split(self, refs):
        n = self.nparts
        refs = list(refs)
        q = [refs[p * n:(p + 1) * n] for p in range(2)]
        k = [refs[2 * n + p * n:2 * n + (p + 1) * n] for p in range(2)]
        v = refs[4 * n:4 * n + 2]
        return q, k, v, refs[4 * n + 2:]

    def k_block(self, k_refs, off, t):
        blks = [r[pl.ds(off, t), :] for r in k_refs]
        return blks[0] if len(blks) == 1 else jnp.concatenate(blks, axis=1)


def _key_norm_max(src, k_blk):
    rows = src.shape[0]

    def body(i, k0_ref, k1_ref, o_ref):
        @pl.when(i == 0)
        def _():
            o_ref[...] = jnp.zeros_like(o_ref)

        cols = {}
        for p, ref in enumerate((k0_ref, k1_ref)):
            k32 = ref[...].astype(F32)
            for e in range(2):
                sq = jnp.sum(jnp.where(_lane_mask("pair", e, k32.shape[0]), k32 * k32, 0.0), axis=1, keepdims=True)
                cols[2 * p + e] = jnp.max(sq, axis=0, keepdims=True)
        lane = lax.broadcasted_iota(jnp.int32, (1, PAIR), 1)
        o_ref[...] = jnp.maximum(o_ref[...], sum(jnp.where(lane == h, c, 0.0) for h, c in cols.items()))

    return _rowwise(body, "key_norm_max", rows, ROW_TILE, [(src, PAIR, k_blk), (src, PAIR, k_blk + 1)], [], [],
                    [((1, PAIR), F32)])[0]


def _softmax_fwd(q_parts, k_parts, v_parts, sq, sk, scale, causal, bias, name):
    tq = min(SOFTMAX_TILE, sq)
    tk = tq if causal else min(SOFTMAX_TILE, sk)
    nkv = sk // tk
    pp = _Parts(q_parts, k_parts, v_parts, tq, sk)

    def kern(*refs):
        q_refs, k_refs, v_refs, rest = pp.split(refs)
        if bias is not None:
            fc_ref, ft_ref, kmax_ref, o_ref, lse_ref = rest
            fcb = fc_ref[...]
        else:
            o_ref, lse_ref = rest
        i = pl.program_id(0)
        fold = _is_pow2(scale)
        head_on = [jnp.where(_lane_mask("pair", e, tk), 1.0, 0.0).astype(MXU_DTYPE) for e in range(2)]
        head_off = [jnp.where(_lane_mask("pair", e, tk), 0.0, 1.0).astype(MXU_DTYPE) for e in range(2)]
        lse_cols = {}
        for first in range(0, 2, FWD_PAIRS_PER_LOOP):
            pairs = list(range(first, first + FWD_PAIRS_PER_LOOP))
            heads = [2 * p + e for p in pairs for e in range(2)]
            qm = {}
            for p in pairs:
                masked_q = _masked_heads([r[...] for r in q_refs[p]], pp.kinds[p], p, MXU_DTYPE, scale if fold else None)
                qm.update({2 * p + e: masked_q[e] for e in range(2)})
            if bias is not None:
                reach = {h: _logit_reach(qm[h], kmax_ref[...], h) for h in heads}

            def block(j, carry, masked, pairs=pairs, qm=qm):
                off = _blk_off(j, tk)
                out = []
                for p in pairs:
                    kb = pp.k_block(k_refs[p], off, tk)
                    vb = v_refs[p][pl.ds(off, tk), :]
                    for e in range(2):
                        h = 2 * p + e
                        m, acc = carry[len(out)]
                        s = _dot_nt(qm[h], kb)
                        if not fold:
                            s = s * scale
                        if bias is not None:
                            s = s - ft_ref[h:h + 1, pl.ds(off, tk)]
                        if masked:
                            s = jnp.where(_causal_mask(tq, False), s, NEG)
                        m_new = jnp.maximum(m, jnp.max(s, axis=1, keepdims=True))
                        pr = jnp.exp(s - m_new).astype(MXU_DTYPE)
                        out.append((m_new, jnp.exp(m - m_new) * acc + _dot(pr, vb * head_on[e] + head_off[e])))
                return tuple(out)

            carry = tuple((jnp.full((tq, 1), NEG, F32), jnp.zeros((tq, PAIR), F32)) for _ in heads)
            if causal and bias is not None:
                def alive(c, j, heads=heads, reach=reach):
                    return functools.reduce(jnp.maximum, [
                        jnp.max(reach[h] + _forget_top(ft_ref, h, _blk_off(j, tk)) - c[n][0]) for n, h in enumerate(heads)])

                def step(state, block=block, alive=alive):
                    n, _, c = state
                    c = block(i - 1 - n, c, False)
                    return n + 1, alive(c, i - 1 - n), c

                carry = block(i, carry, True)
                _, _, carry = lax.while_loop(lambda st: jnp.logical_and(st[0] < i, st[1] > DEAD_LOGIT), step,
                                             (jnp.int32(0), alive(carry, i), carry))
            elif causal:
                carry = lax.fori_loop(0, i, lambda j, c, block=block: block(j, c, False), carry)
                carry = block(i, carry, True)
            else:
                for j in range(nkv):
                    carry = block(j, carry, False)
            for n, p in enumerate(pairs):
                outs = []
                for e in range(2):
                    m, acc = carry[2 * n + e]
                    l = _col(acc, HEAD_DIM * (1 - e))
                    outs.append(acc / l)
                    lse_cols[2 * p + e] = m + jnp.log(l) + (_col(fcb, 2 * p + e) if bias is not None else 0.0)
                o_ref[:, p * PAIR:(p + 1) * PAIR] = _take_heads(outs, p)
        lse_ref[...] = _scatter_cols(lse_cols, tq)

    in_specs = pp.q_specs + pp.k_specs + pp.v_specs
    args = pp.q_args + pp.k_args + pp.v_args
    if bias is not None:
        in_specs += [_row_spec(tq, 0), pl.BlockSpec((8, sk), lambda i: (0, 0), pipeline_mode=pl.Buffered(1)),
                     pl.BlockSpec((1, PAIR), lambda i: (0, 0))]
        args += list(bias)
    return pl.pallas_call(
        kern, name=name, grid=(sq // tq,), in_specs=in_specs,
        out_specs=[_row_spec(tq, 0, GROUP_W), _row_spec(tq, 0)],
        out_shape=[jax.ShapeDtypeStruct((sq, GROUP_W), F32), jax.ShapeDtypeStruct((sq, PAIR), F32)],
        compiler_params=_att_params(True),
    )(*args)


def _softmax_bwd(q_parts, k_parts, v_parts, o, lse, do, do_blk, sq, sk, scale, causal, bias, name):
    tq = min(SOFTMAX_TILE, sq)
    tk = tq if causal else min(SOFTMAX_TILE, sk)
    nkv = sk // tk
    pp = _Parts(q_parts, k_parts, v_parts, tq, sk)
    quad = pp.nparts == 2
    wq = GROUP_W + (PAIR if quad else 0)

    def kern(*refs):
        q_refs, k_refs, v_refs, rest = pp.split(refs)
        if bias is not None:
            o_ref, lse_ref, do_ref, fc_ref, ft_ref, kmax_ref, dq_ref, dk_ref, dv_ref, dfq_ref, dfk_ref = rest
            fcb = fc_ref[...]
        else:
            o_ref, lse_ref, do_ref, dq_ref, dk_ref, dv_ref = rest
        i = pl.program_id(0)
        fold = _is_pow2(scale)

        @pl.when(i == 0)
        def _():
            dk_ref[...] = jnp.zeros_like(dk_ref)
            dv_ref[...] = jnp.zeros_like(dv_ref)
            if bias is not None:
                dfk_ref[...] = jnp.zeros_like(dfk_ref)

        lse_b = lse_ref[...]
        qm, dom, delta, lse_h = [], [], [], []
        for p in range(2):
            qm += _masked_heads([r[...] for r in q_refs[p]], pp.kinds[p], p, MXU_DTYPE, scale if fold else None)
            do_p = do_ref[:, p * PAIR:(p + 1) * PAIR]
            dom += _masked_heads([do_p], ["pair"], p, MXU_DTYPE)
            prod = do_p * o_ref[:, p * PAIR:(p + 1) * PAIR]
            for e in range(2):
                h = 2 * p + e
                delta.append(jnp.sum(jnp.where(_lane_mask("pair", h, tq), prod, 0.0), axis=1, keepdims=True))
                lse_h.append(_col(lse_b, h) - (_col(fcb, h) if bias is not None else 0.0))

        def block(j, carry, masked):
            off = _blk_off(j, tk)
            out = []
            for p in range(2):
                kb = pp.k_block(k_refs[p], off, tk)
                vb = v_refs[p][pl.ds(off, tk), :]
                dk_acc = jnp.zeros((tk, pp.width), F32)
                dv_acc = jnp.zeros((tk, PAIR), F32)
                for e in range(2):
                    h = 2 * p + e
                    dq, dfq = carry[h]
                    s = _dot_nt(qm[h], kb)
                    if not fold:
                        s = s * scale
                    if bias is not None:
                        s = s - ft_ref[h:h + 1, pl.ds(off, tk)]
                    if masked:
                        s = jnp.where(_causal_mask(tq, False), s, NEG)
                    pr = jnp.exp(s - lse_h[h])
                    ds = pr * (_dot_nt(dom[h], vb) - delta[h])
                    dsb = (ds if fold else ds * scale).astype(MXU_DTYPE)
                    dv_acc = dv_acc + _dot_tn(pr.astype(MXU_DTYPE), dom[h])
                    dk_acc = dk_acc + _dot_tn(dsb, qm[h])
                    dq = dq + _dot(dsb, kb)
                    if bias is not None:
                        dfq = dfq + jnp.sum(ds, axis=1, keepdims=True)
                        dfk_ref[h:h + 1, pl.ds(off, tk)] -= jnp.sum(ds, axis=0, keepdims=True)
                    out.append((dq, dfq))
                dv_ref[pl.ds(off, tk), p * PAIR:(p + 1) * PAIR] += dv_acc
                dk_ref[pl.ds(off, tk), p * PAIR:(p + 1) * PAIR] += dk_acc[:, :PAIR]
                if quad:
                    dk_ref[pl.ds(off, tk), GROUP_W:] += dk_acc[:, PAIR:]
            return tuple(out)

        carry = tuple((jnp.zeros((tq, pp.width), F32), jnp.zeros((tq, 1), F32)) for _ in range(N_HEADS))
        if causal and bias is not None:
            reach = [_logit_reach(qm[h], kmax_ref[...], h) - lse_h[h] for h in range(N_HEADS)]

            def alive(j):
                return functools.reduce(jnp.maximum, [jnp.max(reach[h] + _forget_top(ft_ref, h, _blk_off(j, tk)))
                                                      for h in range(N_HEADS)])

            def step(state):
                n, _, c = state
                return n + 1, alive(i - 1 - n), block(i - 1 - n, c, False)

            carry = block(i, carry, True)
            _, _, carry = lax.while_loop(lambda st: jnp.logical_and(st[0] < i, st[1] > DEAD_LOGIT), step,
                                         (jnp.int32(0), alive(i), carry))
        elif causal:
            carry = lax.fori_loop(0, i, lambda j, c: block(j, c, False), carry)
            carry = block(i, carry, True)
        else:
            for j in range(nkv):
                carry = block(j, carry, False)
        dqs = [c[0] * scale if fold else c[0] for c in carry]
        for p in range(2):
            dq_ref[:, p * PAIR:(p + 1) * PAIR] = _take_heads([dqs[2 * p + e][:, :PAIR] for e in range(2)], p)
        if quad:
            dq_ref[:, GROUP_W:] = sum(jnp.where(_lane_mask("quad", h, tq), dqs[h][:, PAIR:], 0.0)
                                      for h in range(N_HEADS))
        if bias is not None:
            dfq_ref[...] = _scatter_cols({h: carry[h][1] for h in range(N_HEADS)}, tq)

    acc_spec = lambda rows, width: pl.BlockSpec((rows, width), lambda i: (0, 0), pipeline_mode=pl.Buffered(1))
    in_specs = pp.q_specs + pp.k_specs + pp.v_specs + [_row_spec(tq, 0, GROUP_W), _row_spec(tq, 0),
                                                       _row_spec(tq, do_blk, GROUP_W)]
    args = pp.q_args + pp.k_args + pp.v_args + [o, lse, do]
    out_specs = [_row_spec(tq, 0, wq), acc_spec(sk, wq), acc_spec(sk, GROUP_W)]
    out_shape = [jax.ShapeDtypeStruct((sq, wq), F32), jax.ShapeDtypeStruct((sk, wq), F32),
                 jax.ShapeDtypeStruct((sk, GROUP_W), F32)]
    if bias is not None:
        in_specs += [_row_spec(tq, 0), pl.BlockSpec((8, sk), lambda i: (0, 0), pipeline_mode=pl.Buffered(1)),
                     pl.BlockSpec((1, PAIR), lambda i: (0, 0))]
        args += list(bias)
        out_specs += [_row_spec(tq, 0), acc_spec(8, sk)]
        out_shape += [jax.ShapeDtypeStruct((sq, PAIR), F32), jax.ShapeDtypeStruct((8, sk), F32)]
    return pl.pallas_call(
        kern, name=name, grid=(sq // tq,), in_specs=in_specs, out_specs=out_specs, out_shape=out_shape,
        compiler_params=_att_params(False),
    )(*args)


def _sb_logs(qh, kb, valid):
    z = _dot_nt(qh, kb)
    sp = _softplus(z)
    lk = -sp
    if valid is not None:
        lk = jnp.where(valid, lk, 0.0)
    return lk, z - sp


def _sb_valid(d, tq, tk):
    r = lax.broadcasted_iota(jnp.int32, (tq, tk), 0)
    c = lax.broadcasted_iota(jnp.int32, (tq, tk), 1)
    return c + d * tk < r


def _tri_sums(xs, tri):
    t = xs[0].shape[0]
    pieces = [_split2(x) for x in xs]
    hi = _dot(jnp.concatenate([pc[0] for pc in pieces], axis=0), tri)
    lo = _dot(jnp.concatenate([pc[1] for pc in pieces], axis=0), tri)
    return [hi[n * t:(n + 1) * t] + lo[n * t:(n + 1) * t] for n in range(len(xs))]


def _sb_fwd(src, q_blk, k_blk, v_blk, s, scale, name):
    assert _is_pow2(scale)
    tq, t = min(SB_QUERY_TILE, s), min(ATT_TILE, s)
    slots = -(-(s // t) // SB_SLOT) * SB_SLOT
    width = N_HEADS * slots
    band = tq // t
    pair = lambda blk: [[(src, blk + p, "pair")] for p in range(2)]
    pp = _Parts(pair(q_blk), pair(k_blk), [(src, v_blk + p) for p in range(2)], tq, s)

    def kern(*refs):
        q_refs, k_refs, v_refs, (tri_ref, o_ref, rm_ref, cnt_ref) = pp.split(refs)
        i = pl.program_id(0)
        tri = tri_ref[...]
        lane = lax.broadcasted_iota(jnp.int32, (tq, width), 1)
        qm = []
        for p in range(2):
            qm += _masked_heads([q_refs[p][0][...]], ["pair"], p, MXU_DTYPE, scale)

        def block(j, carry, valid):
            accs, rights, rm = carry
            off = _blk_off(j, t)
            kbs = [k_refs[p][0][pl.ds(off, t), :] for p in range(2)]
            vbs = [v_refs[p][pl.ds(off, t), :] for p in range(2)]
            logs = [_sb_logs(qm[h], kbs[h // 2], valid) for h in range(N_HEADS)]
            tails = _tri_sums([lg[0] for lg in logs], tri)
            new_acc, new_right = [], []
            for h in range(N_HEADS):
                lk, ls = logs[h]
                w = jnp.exp(ls + tails[h] + rights[h])
                if valid is not None:
                    w = jnp.where(valid, w, 0.0)
                new_acc.append(accs[h] + _dot(w.astype(MXU_DTYPE), vbs[h // 2]))
                rm = rm + jnp.where(lane == slots * h + j, rights[h], 0.0)
                new_right.append(rights[h] + jnp.sum(lk, axis=1, keepdims=True))
            return tuple(new_acc), tuple(new_right), rm

        carry = (tuple(jnp.zeros((tq, PAIR), F32) for _ in range(N_HEADS)),
                 tuple(jnp.zeros((tq, 1), F32) for _ in range(N_HEADS)), jnp.zeros((tq, width), F32))
        for d in reversed(range(band)):
            carry = block(band * i + d, carry, _sb_valid(d, tq, t))

        def alive(c):
            return functools.reduce(jnp.maximum, [jnp.max(r) for r in c[1]])

        def step(state):
            n, _, c = state
            c = block(band * i - 1 - n, c, None)
            return n + 1, alive(c), c

        n_done, _, carry = lax.while_loop(lambda st: jnp.logical_and(st[0] < band * i, st[1] > EXP_UNDERFLOW),
                                          step, (jnp.int32(0), alive(carry), carry))
        cnt_ref[i] = n_done
        for p in range(2):
            o_ref[:, p * PAIR:(p + 1) * PAIR] = _take_heads([carry[0][2 * p + e] for e in range(2)], p)
        rm_ref[...] = carry[2]

    return pl.pallas_call(
        kern, name=name, grid=(s // tq,),
        in_specs=pp.q_specs + pp.k_specs + pp.v_specs + [pl.BlockSpec((t, t), lambda i: (0, 0))],
        out_specs=[_row_spec(tq, 0, GROUP_W), _row_spec(tq, 0, width), pl.BlockSpec(memory_space=pltpu.SMEM)],
        out_shape=[jax.ShapeDtypeStruct((s, GROUP_W), F32), jax.ShapeDtypeStruct((s, width), F32),
                   jax.ShapeDtypeStruct((s // tq,), jnp.int32)],
        compiler_params=_att_params(False),
    )(*(pp.q_args + pp.k_args + pp.v_args + [_tri(t, "row_gt_col")]))


def _sb_bwd(src, q_blk, k_blk, v_blk, do, do_blk, rm, visited, s, scale, name):
    assert _is_pow2(scale)
    tq, t = min(SB_QUERY_TILE, s), min(ATT_TILE, s)
    band = tq // t
    pair = lambda blk: [[(src, blk + p, "pair")] for p in range(2)]
    pp = _Parts(pair(q_blk), pair(k_blk), [(src, v_blk + p) for p in range(2)], tq, s)

    def kern(*refs):
        q_refs, k_refs, v_refs, (do_ref, rm_ref, tri_ref, pre_ref, cnt_ref, dq_ref, dk_ref, dv_ref) = pp.split(refs)
        i = pl.program_id(0)

        @pl.when(i == 0)
        def _():
            dk_ref[...] = jnp.zeros_like(dk_ref)
            dv_ref[...] = jnp.zeros_like(dv_ref)

        rmb = rm_ref[...]
        tri = tri_ref[...]
        pre = pre_ref[...]
        qm, dom = [], []
        for p in range(2):
            qm += _masked_heads([q_refs[p][0][...]], ["pair"], p, MXU_DTYPE, scale)
            dom += _masked_heads([do_ref[:, p * PAIR:(p + 1) * PAIR]], ["pair"], p, MXU_DTYPE)

        def block(j, carry, valid):
            dqs, lefts = carry
            off = _blk_off(j, t)
            kbs = [k_refs[p][0][pl.ds(off, t), :] for p in range(2)]
            vbs = [v_refs[p][pl.ds(off, t), :] for p in range(2)]
            logs = [_sb_logs(qm[h], kbs[h // 2], valid) for h in range(N_HEADS)]
            tails = _tri_sums([lg[0] for lg in logs], tri)
            ws, gs = [], []
            for h in range(N_HEADS):
                lk, ls = logs[h]
                w = jnp.exp(ls + tails[h] + _col(rmb, (rm.shape[1] // N_HEADS) * h + j))
                if valid is not None:
                    w = jnp.where(valid, w, 0.0)
                ws.append(w)
                gs.append(_dot_nt(dom[h], vbs[h // 2]) * w)
            prefix = _tri_sums(gs, pre)
            new_dq, new_left = [], []
            dk_acc = [jnp.zeros((t, PAIR), F32) for _ in range(2)]
            dv_acc = [jnp.zeros((t, PAIR), F32) for _ in range(2)]
            for h in range(N_HEADS):
                lk, ls = logs[h]
                sig = jnp.exp(ls)
                dz = gs[h] * (1.0 - sig) - sig * (prefix[h] + lefts[h])
                if valid is not None:
                    dz = jnp.where(valid, dz, 0.0)
                dzb = dz.astype(MXU_DTYPE)
                dv_acc[h // 2] = dv_acc[h // 2] + _dot_tn(ws[h].astype(MXU_DTYPE), dom[h])
                dk_acc[h // 2] = dk_acc[h // 2] + _dot_tn(dzb, qm[h])
                new_dq.append(dqs[h] + _dot(dzb, kbs[h // 2]))
                new_left.append(lefts[h] + jnp.sum(gs[h], axis=1, keepdims=True))
            for p in range(2):
                dv_ref[pl.ds(off, t), p * PAIR:(p + 1) * PAIR] += dv_acc[p]
                dk_ref[pl.ds(off, t), p * PAIR:(p + 1) * PAIR] += dk_acc[p]
            return tuple(new_dq), tuple(new_left)

        carry = (tuple(jnp.zeros((tq, PAIR), F32) for _ in range(N_HEADS)),
                 tuple(jnp.zeros((tq, 1), F32) for _ in range(N_HEADS)))
        carry = lax.fori_loop(band * i - cnt_ref[i], band * i, lambda j, c: block(j, c, None), carry)
        for d in range(band):
            carry = block(band * i + d, carry, _sb_valid(d, tq, t))
        for p in range(2):
            dq_ref[:, p * PAIR:(p + 1) * PAIR] = _take_heads([carry[0][2 * p + e] * scale for e in range(2)], p)

    mspec = pl.BlockSpec((t, t), lambda i: (0, 0))
    acc_spec = pl.BlockSpec((s, GROUP_W), lambda i: (0, 0), pipeline_mode=pl.Buffered(1))
    return pl.pallas_call(
        kern, name=name, grid=(s // tq,),
        in_specs=pp.q_specs + pp.k_specs + pp.v_specs + [_row_spec(tq, do_blk, GROUP_W), _row_spec(tq, 0, rm.shape[1]), mspec, mspec,
                                                         pl.BlockSpec(memory_space=pltpu.SMEM)],
        out_specs=[_row_spec(tq, 0, GROUP_W), acc_spec, acc_spec],
        out_shape=[jax.ShapeDtypeStruct((s, GROUP_W), F32)] * 3,
        compiler_params=_att_params(False),
    )(*(pp.q_args + pp.k_args + pp.v_args + [do, rm, _tri(t, "row_gt_col"), _tri(t, "row_lt_col"), visited]))


def _split_w_in(w):
    col = lambda n: w[:, _OFF[n]:_OFF[n + 1]]
    wa = jnp.concatenate([col(0), col(1), col(2), col(4), col(5), col(6), col(10)], axis=1)
    misc = jnp.concatenate([col(3), col(9), jnp.zeros((w.shape[0], 128 - 4 - MLA_ROPE), w.dtype)], axis=1)
    wb = jnp.concatenate([col(11), col(7), col(8), misc], axis=1)
    return wa, wb


def _merge_dw_in(dwp):
    a = lambda n: dwp[:, n * GROUP_W:(n + 1) * GROUP_W]
    b0 = PA_COLS
    gate = dwp[:, b0:b0 + 1024]
    cq = dwp[:, b0 + 1024:b0 + 1280]
    ckv = dwp[:, b0 + 1280:b0 + 1408]
    flog = dwp[:, b0 + 1408:b0 + 1412]
    krot = dwp[:, b0 + 1408 + MISC_KROT:b0 + 1408 + MISC_KROT + MLA_ROPE]
    return jnp.concatenate([a(0), a(1), a(2), flog, a(3), a(4), a(5), cq, ckv, krot, a(6), gate], axis=1)


def _heads_first(w, per_head, first):
    r = w.shape[0]
    w3 = w.reshape(r, N_HEADS, per_head)
    return jnp.concatenate([w3[:, :, :first].reshape(r, -1), w3[:, :, first:].reshape(r, -1)], axis=1)


def _heads_interleaved(w, per_head, first):
    r = w.shape[0]
    a = w[:, :N_HEADS * first].reshape(r, N_HEADS, first)
    b = w[:, N_HEADS * first:].reshape(r, N_HEADS, per_head - first)
    return jnp.concatenate([a, b], axis=2).reshape(r, N_HEADS * per_head)


def _pad_rows8(a):
    return a[:, :8].T


def _local_step(x2, mem2, tgt, p):
    s = x2.shape[0]
    nm = mem2.shape[0]
    head_scale = HEAD_DIM ** -0.5
    mla_scale = (HEAD_DIM + MLA_ROPE) ** -0.5
    tabs = _rope_tables(s)
    mats = _rope_matrices()
    pairs = lambda arr, blk: [[(arr, blk + q, "pair")] for q in range(2)]
    vals = lambda arr, blk: [(arr, blk + q) for q in range(2)]

    h, hb = _ln_fwd(x2, None, p["ln_in_g"], p["ln_in_b"], "ln_in_fwd")
    _, memn_b = _ln_fwd(mem2, None, p["mem_ln_g"], p["mem_ln_b"], "ln_mem_fwd")

    saved = []
    for l in range(DEPTH):
        wa, wb = _split_w_in(p["w_in"][l])
        wp = jnp.concatenate([wa, wb], axis=1)
        wq_up = _heads_first(p["w_mla_q_up"][l], HEAD_DIM + MLA_ROPE, HEAD_DIM)
        wkv_up = _heads_first(p["w_mla_kv_up"][l], 2 * HEAD_DIM, HEAD_DIM)
        bias_row = jnp.pad(p["b_forget"][l], (0, 128 - N_HEADS)).reshape(1, 128)
        pa = _matmul(hb, wa, MXU_DTYPE, "proj_a")
        pb = _matmul(hb, wb, F32, "proj_b")

        fc = _forget_fwd(pb, bias_row)
        fbias = (fc, _pad_rows8(fc), _key_norm_max(pa, 2))
        o_fox, lse_fox = _softmax_fwd(pairs(pa, 0), pairs(pa, 2), vals(pa, 4), s, s, head_scale, True, fbias,
                                      "fox_fwd")
        o_sb, *rm_sb = _sb_fwd(pa, 6, 8, 10, s, head_scale, "sb_fwd")

        qfull, cqn = _mla_q_fwd(pb, p["mla_q_norm_g"][l], wq_up, tabs, mats)
        kv, ckvn, kr4 = _mla_kv_fwd(pb, p["mla_kv_norm_g"][l], wkv_up, tabs, mats)
        mla_q = [[(qfull, q, "pair"), (qfull, 2, "quad")] for q in range(2)]
        mla_k = [[(kv, q, "pair"), (kr4, 0, "quad")] for q in range(2)]
        o_mla, lse_mla = _softmax_fwd(mla_q, mla_k, vals(kv, 2), s, s, mla_scale, True, None, "mla_fwd")

        mkv = _matmul(memn_b, p["w_mem_kv"][l], MXU_DTYPE, "mem_kv")
        o_mem, lse_mem = _softmax_fwd(pairs(pa, 12), pairs(mkv, 0), vals(mkv, 2), s, nm, head_scale, False, None,
                                      "mem_fwd")

        groups = (o_fox, o_sb, o_mla, o_mem)
        gated = _gate_fwd(groups, pb)
        y = _matmul(gated, p["w_out"][l], F32, "out_proj")
        saved.append(dict(h=h, hb=hb, y=y, wp=wp, wq_up=wq_up, wkv_up=wkv_up, bias_row=bias_row, pa=pa, pb=pb,
                          fbias=fbias, lse_fox=lse_fox, rm_sb=rm_sb, cqn=cqn, ckvn=ckvn, mla_q=mla_q, mla_k=mla_k,
                          kv=kv, lse_mla=lse_mla, mkv=mkv, lse_mem=lse_mem, groups=groups, gated=gated))
        h, hb = _ln_fwd(h, y, p["ln_g"][l], p["ln_b"][l], "ln_fwd")

    dh, sq_cols = _loss_grad(h, tgt)
    loss_sum = jnp.sum(sq_cols)

    grads = {k: [None] * DEPTH for k in ("w_in", "b_forget", "mla_q_norm_g", "w_mla_q_up", "mla_kv_norm_g",
                                         "w_mla_kv_up", "w_mem_kv", "w_out", "ln_g", "ln_b")}
    dmemn = []
    dy1, dy2, c1 = dh, None, 1.0
    for l in reversed(range(DEPTH)):
        r = saved[l]
        pa, pb = r["pa"], r["pb"]
        o_fox, o_sb, o_mla, o_mem = r["groups"]
        du, du_b, dg, db = _ln_bwd(r["h"], r["y"], p["ln_g"][l], dy1, dy2, c1, "ln_bwd")
        grads["ln_g"][l], grads["ln_b"][l] = dg[0], db[0]
        dgated = _matmul(du_b, p["w_out"][l], F32, "out_proj_dx", "nt")
        grads["w_out"][l] = _matmul(r["gated"], du_b, F32, "out_proj_dw", "tn")
        dmixed, dgate_b = _gate_bwd(dgated, r["groups"], pb)

        dfq, dfk, dfv, dfc_q, dfc_k = _softmax_bwd(pairs(pa, 0), pairs(pa, 2), vals(pa, 4), o_fox, r["lse_fox"],
                                                   dmixed, 0, s, s, head_scale, True, r["fbias"], "fox_bwd")
        dmisc_f, dbf = _forget_bwd(pb, r["bias_row"], dfc_q + jnp.pad(dfc_k.T, ((0, 0), (0, 128 - 8))))
        grads["b_forget"][l] = dbf[0, :N_HEADS]

        dsq, dsk, dsv = _sb_bwd(pa, 6, 8, 10, dmixed, 1, *r["rm_sb"], s, head_scale, "sb_bwd")

        dqm, dkm, dvm = _softmax_bwd(r["mla_q"], r["mla_k"], vals(r["kv"], 2), o_mla, r["lse_mla"], dmixed, 2,
                                     s, s, mla_scale, True, None, "mla_bwd")
        dq_mla_b = _rope_q_bwd(dqm, tabs, mats)
        dcqn = _matmul(dq_mla_b, r["wq_up"], F32, "q_up_dx", "nt")
        grads["w_mla_q_up"][l] = _heads_interleaved(_matmul(r["cqn"], dq_mla_b, F32, "q_up_dw", "tn"),
                                                    HEAD_DIM + MLA_ROPE, HEAD_DIM)
        dcq_b, dgq = _rms_bwd(pb, MLA_Q_RANK, PB_CQ_BLK, p["mla_q_norm_g"][l], dcqn, "rms_q_bwd")
        grads["mla_q_norm_g"][l] = dgq[0]
        dkv_b = jnp.concatenate([dkm[:, :GROUP_W], dvm], axis=1).astype(MXU_DTYPE)
        dckvn = _matmul(dkv_b, r["wkv_up"], F32, "kv_up_dx", "nt")
        grads["w_mla_kv_up"][l] = _heads_interleaved(_matmul(r["ckvn"], dkv_b, F32, "kv_up_dw", "tn"),
                                                     2 * HEAD_DIM, HEAD_DIM)
        dckv_b, dgkv = _rms_bwd(pb, MLA_KV_RANK, PB_CKV_BLK, p["mla_kv_norm_g"][l], dckvn, "rms_kv_bwd")
        grads["mla_kv_norm_g"][l] = dgkv[0]
        dmisc_k = _rope_k_bwd(dkm[:, GROUP_W:], tabs, mats)

        dmq, dmk, dmv = _softmax_bwd(pairs(pa, 12), pairs(r["mkv"], 0), vals(r["mkv"], 2), o_mem, r["lse_mem"],
                                     dmixed, 3, s, nm, head_scale, False, None, "mem_bwd")
        dmkv_b = jnp.concatenate([dmk, dmv], axis=1).astype(MXU_DTYPE)
        grads["w_mem_kv"][l] = _matmul(memn_b, dmkv_b, F32, "mem_kv_dw", "tn")
        dmemn.append(_matmul(dmkv_b, p["w_mem_kv"][l], F32, "mem_kv_dx", "nt"))

        dp = [dfq, dfk, dfv, dsq, dsk, dsv, dmq, dgate_b, dcq_b, dckv_b, (dmisc_f, dmisc_k)]
        dhproj = _proj_dx(dp, r["wp"])
        grads["w_in"][l] = _merge_dw_in(_proj_dw(r["hb"], dp))
        dy1, dy2, c1 = du, dhproj, ALPHA

    dx, _, dg_in, db_in = _ln_bwd(x2, None, p["ln_in_g"], dy1, dy2, c1, "ln_in_bwd")
    _, _, dg_mem, db_mem = _ln_bwd(mem2, None, p["mem_ln_g"], dmemn[0], dmemn[1], 1.0, "ln_mem_bwd")
    out = {k: jnp.stack(v) for k, v in grads.items()}
    out.update(ln_in_g=dg_in[0], ln_in_b=db_in[0], mem_ln_g=dg_mem[0], mem_ln_b=db_mem[0])
    return loss_sum, dx, out


WIDE = "w_in"
FLAT_NAMES = ("w_out", "w_mem_kv", "w_mla_q_up", "w_mla_kv_up")
FLAT_ROWS = 896
BIG_NAMES = (WIDE,) + FLAT_NAMES
BIG_AXIS = dict(w_in=2, w_out=1, w_mem_kv=1, w_mla_q_up=2, w_mla_kv_up=2)
SMALL_NAMES = ("ln_in_g", "ln_in_b", "mem_ln_g", "mem_ln_b", "ln_g", "ln_b", "b_forget", "mla_q_norm_g",
               "mla_kv_norm_g")
ALL_NAMES = ("ln_in_g", "ln_in_b", "mem_ln_g", "mem_ln_b", "w_in", "b_forget", "mla_q_norm_g", "w_mla_q_up",
             "mla_kv_norm_g", "w_mla_kv_up", "w_mem_kv", "w_out", "ln_g", "ln_b")
N_CHIPS = 4
N_DEV = 8


def _rows_of(shape):
    rows = -(-int(np.prod(shape)) // LANES)
    return -(-rows // PACK_ALIGN) * PACK_ALIGN


def _pack(arrs, rows):
    parts = []
    for a in arrs:
        f = a.reshape(-1)
        n = _rows_of(a.shape) * LANES
        parts.append(jnp.pad(f, (0, n - f.shape[0])).reshape(-1, LANES))
    used = sum(q.shape[0] for q in parts)
    if rows > used:
        parts.append(jnp.zeros((rows - used, LANES), parts[0].dtype))
    return jnp.concatenate(parts, axis=0)


def _unpack(buf, shapes):
    out, r = [], 0
    for shp in shapes:
        n = _rows_of(shp)
        out.append(buf[r:r + n].reshape(-1)[:int(np.prod(shp))].reshape(shp))
        r += n
    return out


def _sharded_pair(get):
    wide = get(WIDE)
    return [wide.reshape(-1, wide.shape[-1]), _pack([get(n) for n in FLAT_NAMES], FLAT_ROWS)]


HBM_SPEC = pl.BlockSpec(memory_space=pltpu.HBM)


def _gather_weights(shards):
    n = len(shards)

    def body(*refs):
        w_refs, out_refs, (send_sems, recv_sems, local_sems) = refs[:n], refs[n:2 * n], refs[2 * n:]
        x, y, c = (lax.axis_index(a) for a in MESH_AXES)
        me, sibling = 2 * x + y, (x, y, 1 - c)
        chips = [(1 - x, y), (x, 1 - y), (1 - x, 1 - y)]
        local, first, passed = [], [], []
        for a in range(n):
            w_ref, out_ref, half = w_refs[a], out_refs[a], shards[a].shape[0] // 2

            def part(chip, core, out_ref=out_ref, half=half):
                return out_ref.at[chip, pl.ds(core * half, half)]

            def copy(k, src, dst, to, a=a):
                return pltpu.make_async_remote_copy(
                    src_ref=src, dst_ref=dst, send_sem=send_sems.at[6 * a + k], recv_sem=recv_sems.at[6 * a + k],
                    device_id=to, device_id_type=pl.DeviceIdType.MESH)

            local.append(pltpu.make_async_copy(w_ref, out_ref.at[me], local_sems.at[a]))
            local[-1].start()
            mine = [copy(k, w_ref.at[pl.ds(c * half, half)], part(me, c), (px, py, c))
                    for k, (px, py) in enumerate(chips)]
            for cp in mine:
                cp.start()
            first.append((mine, part, copy))
        for mine, part, copy in first:
            for k, (px, py) in enumerate(chips):
                copy(k, part(me, c), part(2 * px + py, c), (px, py, c)).wait_recv()
                passed.append(copy(3 + k, part(2 * px + py, c), part(2 * px + py, c), sibling))
                passed[-1].start()
        for mine, part, copy in first:
            for k, (px, py) in enumerate(chips):
                copy(3 + k, part(me, c), part(2 * px + py, 1 - c), sibling).wait_recv()
        for cp in [cp for mine, _, _ in first for cp in mine] + passed:
            cp.wait_send()
        for cp in local:
            cp.wait()

    return pl.pallas_call(
        body, name="gather_weights",
        out_shape=[jax.ShapeDtypeStruct((N_CHIPS,) + s.shape, s.dtype) for s in shards],
        in_specs=[HBM_SPEC] * n, out_specs=[HBM_SPEC] * n,
        scratch_shapes=[pltpu.SemaphoreType.DMA((6 * n,)), pltpu.SemaphoreType.DMA((6 * n,)),
                        pltpu.SemaphoreType.DMA((n,))],
    )(*shards)


def _exchange_grads(bigs, small):
    nb = len(bigs)
    halves = [b.shape[1] // 2 for b in bigs]
    chunks = [_pick(h, (128, 64, 32, 16)) for h in halves]

    def body(*refs):
        big_refs, small_ref = refs[:nb], refs[nb]
        sum_refs, out_refs = refs[nb + 1:2 * nb + 1], refs[2 * nb + 1:3 * nb + 2]
        scratch = refs[3 * nb + 2:]
        land_bufs, sum_bufs = scratch[:nb], scratch[nb:2 * nb]
        send_sems, recv_sems, local_sems, load_sem, swap_send, swap_recv, keep_sems = scratch[2 * nb:]
        x, y, c = (lax.axis_index(a) for a in MESH_AXES)
        me, my_chip = 4 * x + 2 * y + c, 2 * x + y
        flips = [(fx, fy, fc) for fx in (0, 1) for fy in (0, 1) for fc in (0, 1) if fx or fy or fc]
        peers = [(1 - x if fx else x, 1 - y if fy else y, 1 - c if fc else c) for fx, fy, fc in flips]

        sources = [lambda chip, core, r=big_refs[a], half=bigs[a].shape[1] // 2: r.at[chip, pl.ds(core * half, half)]
                   for a in range(nb)] + [lambda chip, core: small_ref]

        def copy(a, k, src, slot, to):
            return pltpu.make_async_remote_copy(
                src_ref=src, dst_ref=out_refs[a].at[slot], send_sem=send_sems.at[7 * a + k],
                recv_sem=recv_sems.at[7 * a + k], device_id=to, device_id_type=pl.DeviceIdType.MESH)

        own = [pltpu.make_async_copy(src(my_chip, c), out_refs[a].at[me], local_sems.at[a])
               for a, src in enumerate(sources)]
        for cp in own:
            cp.start()
        sends = [copy(a, k, src(2 * px + py, pc), me, (px, py, pc))
                 for a, src in enumerate(sources) for k, (px, py, pc) in enumerate(peers)]
        for cp in sends:
            cp.start()
        for a, src in enumerate(sources):
            for k, (px, py, pc) in enumerate(peers):
                copy(a, k, src(my_chip, c), 4 * px + 2 * py + pc, (px, py, pc)).wait_recv()
        for cp in sends:
            cp.wait_send()
        for cp in own:
            cp.wait()

        tails = []
        for a in range(nb):
            for r0 in range(0, halves[a], chunks[a]):
                load = pltpu.make_async_copy(out_refs[a].at[:, pl.ds(r0, chunks[a])], land_bufs[a], load_sem)
                load.start()
                load.wait()
                total = land_bufs[a][0].astype(F32)
                for d in range(1, N_DEV):
                    total = total + land_bufs[a][d].astype(F32)
                sum_bufs[a][pl.ds(r0, chunks[a]), :] = total
            keep = pltpu.make_async_copy(sum_bufs[a], sum_refs[a].at[c], keep_sems.at[a])
            give = pltpu.make_async_remote_copy(
                src_ref=sum_bufs[a], dst_ref=sum_refs[a].at[c], send_sem=swap_send.at[a], recv_sem=swap_recv.at[a],
                device_id=(x, y, 1 - c), device_id_type=pl.DeviceIdType.MESH)
            keep.start()
            give.start()
            tails.append((keep, give))
        for a, (keep, give) in enumerate(tails):
            pltpu.make_async_remote_copy(
                src_ref=sum_bufs[a], dst_ref=sum_refs[a].at[1 - c], send_sem=swap_send.at[a], recv_sem=swap_recv.at[a],
                device_id=(x, y, 1 - c), device_id_type=pl.DeviceIdType.MESH).wait_recv()
            give.wait_send()
            keep.wait()

    return pl.pallas_call(
        body, name="exchange_grads",
        out_shape=[jax.ShapeDtypeStruct((2, h, b.shape[2]), F32) for h, b in zip(halves, bigs)]
        + [jax.ShapeDtypeStruct((N_DEV, h, b.shape[2]), b.dtype) for h, b in zip(halves, bigs)]
        + [jax.ShapeDtypeStruct((N_DEV,) + small.shape, small.dtype)],
        in_specs=[HBM_SPEC] * (nb + 1), out_specs=[HBM_SPEC] * (2 * nb + 1),
        scratch_shapes=[pltpu.VMEM((N_DEV, ch, b.shape[2]), b.dtype) for ch, b in zip(chunks, bigs)]
        + [pltpu.VMEM((h, b.shape[2]), F32) for h, b in zip(halves, bigs)]
        + [pltpu.SemaphoreType.DMA((7 * (nb + 1),)), pltpu.SemaphoreType.DMA((7 * (nb + 1),)),
           pltpu.SemaphoreType.DMA((nb + 1,)), pltpu.SemaphoreType.DMA, pltpu.SemaphoreType.DMA((nb,)),
           pltpu.SemaphoreType.DMA((nb,)), pltpu.SemaphoreType.DMA((nb,))],
        compiler_params=pltpu.CompilerParams(vmem_limit_bytes=VMEM_LIMIT),
    )(*bigs, small)


def _adamw(parts, w, m, v, name):
    rows, width = w.shape
    n_parts = parts.shape[0]
    tile = _pick(rows, (128, 16, 8))
    bc1 = 1.0 - ADAM_B1 ** ADAM_STEP
    bc2 = 1.0 - ADAM_B2 ** ADAM_STEP

    def kern(p_ref, w_ref, m_ref, v_ref, g_ref, d_ref, nm_ref, nv_ref):
        g = p_ref[0].astype(F32)
        for d in range(1, n_parts):
            g = g + p_ref[d].astype(F32)
        nm = ADAM_B1 * m_ref[...] + (1.0 - ADAM_B1) * g
        nv = ADAM_B2 * v_ref[...] + (1.0 - ADAM_B2) * (g * g)
        g_ref[...] = g
        nm_ref[...] = nm
        nv_ref[...] = nv
        d_ref[...] = -ADAM_LR * ((nm / bc1) / (jnp.sqrt(nv / bc2) + ADAM_EPS) + ADAM_WD * w_ref[...])

    spec = pl.BlockSpec((tile, width), lambda i: (i, 0))
    return pl.pallas_call(
        kern, name=name, grid=(rows // tile,),
        in_specs=[pl.BlockSpec((n_parts, tile, width), lambda i: (0, i, 0)), spec, spec, spec],
        out_specs=[spec] * 4, out_shape=[jax.ShapeDtypeStruct((rows, width), F32)] * 4,
        compiler_params=pltpu.CompilerParams(dimension_semantics=("parallel",), vmem_limit_bytes=VMEM_LIMIT),
    )(parts, w, m, v)


def kernel(x, mem, ln_in_g, ln_in_b, mem_ln_g, mem_ln_b, w_in, b_forget, mla_q_norm_g, w_mla_q_up, mla_kv_norm_g, w_mla_kv_up, w_mem_kv, w_out, ln_g, ln_b, loss_target, m_ln_in_g, m_ln_in_b, m_mem_ln_g, m_mem_ln_b, m_w_in, m_b_forget, m_mla_q_norm_g, m_w_mla_q_up, m_mla_kv_norm_g, m_w_mla_kv_up, m_w_mem_kv, m_w_out, m_ln_g, m_ln_b, v_ln_in_g, v_ln_in_b, v_mem_ln_g, v_mem_ln_b, v_w_in, v_b_forget, v_mla_q_norm_g, v_w_mla_q_up, v_mla_kv_norm_g, v_w_mla_kv_up, v_w_mem_kv, v_w_out, v_ln_g, v_ln_b):
    w = dict(ln_in_g=ln_in_g, ln_in_b=ln_in_b, mem_ln_g=mem_ln_g, mem_ln_b=mem_ln_b, w_in=w_in, b_forget=b_forget,
             mla_q_norm_g=mla_q_norm_g, w_mla_q_up=w_mla_q_up, mla_kv_norm_g=mla_kv_norm_g,
             w_mla_kv_up=w_mla_kv_up, w_mem_kv=w_mem_kv, w_out=w_out, ln_g=ln_g, ln_b=ln_b)
    mo = dict(ln_in_g=m_ln_in_g, ln_in_b=m_ln_in_b, mem_ln_g=m_mem_ln_g, mem_ln_b=m_mem_ln_b, w_in=m_w_in,
              b_forget=m_b_forget, mla_q_norm_g=m_mla_q_norm_g, w_mla_q_up=m_w_mla_q_up,
              mla_kv_norm_g=m_mla_kv_norm_g, w_mla_kv_up=m_w_mla_kv_up, w_mem_kv=m_w_mem_kv, w_out=m_w_out,
              ln_g=m_ln_g, ln_b=m_ln_b)
    vo = dict(ln_in_g=v_ln_in_g, ln_in_b=v_ln_in_b, mem_ln_g=v_mem_ln_g, mem_ln_b=v_mem_ln_b, w_in=v_w_in,
              b_forget=v_b_forget, mla_q_norm_g=v_mla_q_norm_g, w_mla_q_up=v_w_mla_q_up,
              mla_kv_norm_g=v_mla_kv_norm_g, w_mla_kv_up=v_w_mla_kv_up, w_mem_kv=v_w_mem_kv, w_out=v_w_out,
              ln_g=v_ln_g, ln_b=v_ln_b)
    flat_shapes = [w[n].shape for n in FLAT_NAMES]
    small_shapes = [w[n].shape for n in SMALL_NAMES]

    got_wide, got_flat = _gather_weights(_sharded_pair(lambda n: w[n].astype(MXU_DTYPE)))
    full = dict(w)
    full[WIDE] = jnp.concatenate([got_wide[j] for j in range(N_CHIPS)], axis=1).reshape(
        w[WIDE].shape[:2] + (N_CHIPS * w[WIDE].shape[2],))
    per_chip = [_unpack(got_flat[j], flat_shapes) for j in range(N_CHIPS)]
    for idx, n in enumerate(FLAT_NAMES):
        full[n] = jnp.concatenate([per_chip[j][idx] for j in range(N_CHIPS)], axis=BIG_AXIS[n])

    loss_sum, dx, g = _local_step(x[0], mem[0], loss_target[0], full)
    loss = lax.psum(loss_sum * (0.5 / D_MODEL), MESH_AXES)

    def shard_of(n, j):
        ax, size = BIG_AXIS[n], w[n].shape[BIG_AXIS[n]]
        return lax.slice_in_dim(g[n], j * size, (j + 1) * size, axis=ax).astype(MXU_DTYPE)

    per_dest = [_sharded_pair(lambda n, j=j: shard_of(n, j)) for j in range(N_CHIPS)]
    bigs = [jnp.stack([per_dest[j][a] for j in range(N_CHIPS)]) for a in range(2)]
    exchanged = _exchange_grads(bigs, _pack([g[n] for n in SMALL_NAMES], SMALL_ROWS))
    halves, small_parts = exchanged[:2], exchanged[-1]

    res = []
    for a, (grad, nm) in enumerate(zip(halves, ("adamw_wide", "adamw_flat"))):
        state = [_sharded_pair(lambda n, src=src: src[n])[a] for src in (w, mo, vo)]
        res.append(_adamw(grad.reshape((1,) + state[0].shape), *state, nm))
    res_small = _adamw(small_parts, *[_pack([src[n] for n in SMALL_NAMES], SMALL_ROWS) for src in (w, mo, vo)],
                       "adamw_replicated")
    outs = []
    for kind in range(4):
        vals = {WIDE: res[0][kind].reshape(w[WIDE].shape)}
        vals.update(zip(FLAT_NAMES, _unpack(res[1][kind], flat_shapes)))
        vals.update(zip(SMALL_NAMES, _unpack(res_small[kind], small_shapes)))
        outs += [vals[n] for n in ALL_NAMES]
    return (loss, dx[None], *outs)
```

```python
import functools

import numpy as np
import jax
import jax.numpy as jnp
from jax import lax
from jax.experimental import pallas as pl
from jax.experimental.pallas import tpu as pltpu

F32 = jnp.float32
MXU_DTYPE = jnp.bfloat16

DEPTH = 2
D_MODEL = 1024
GROUP_W = 256
N_HEADS = 4
HEAD_DIM = 64
MLA_Q_RANK = 256
MLA_KV_RANK = 128
MLA_ROPE = 32
MLA_Q_COLS = N_HEADS * (HEAD_DIM + MLA_ROPE)
MLA_KV_COLS = N_HEADS * 2 * HEAD_DIM
ROPE_THETA = 10000.0
LN_EPS = 1e-5
RMS_EPS = 1e-6
ALPHA = (2 * DEPTH) ** 0.25
ADAM_LR, ADAM_B1, ADAM_B2, ADAM_EPS, ADAM_WD, ADAM_STEP = 0.001, 0.9, 0.999, 1e-08, 0.01, 10

_SPLIT = (256, 256, 256, 4, 256, 256, 256, 256, 128, 32, 256, 1024)
_OFF = [int(o) for o in np.cumsum((0,) + _SPLIT)]
IN_COLS = _OFF[-1]
PA_COLS = 7 * GROUP_W
PB_COLS = 1024 + 256 + 128 + 128
PB_CQ_BLK, PB_CKV_BLK, PB_MISC_BLK = 4, 10, 11
MISC_KROT = 4

LANES = 1024
PACK_ALIGN = 16
SMALL_ROWS = 144
ROW_TILE = 256
PROJ_BWD_ROWS = 512
ATT_TILE = 256
SB_QUERY_TILE = 256
SOFTMAX_TILE = 512
FWD_PAIRS_PER_LOOP = 1
PAIR = 128
SB_SLOT = PAIR // N_HEADS
VMEM_LIMIT = 56 * 1024 * 1024
MATMUL_VMEM = 30 * 1024 * 1024
NEG = -1e30
LOG2E = 1.4426950408889634
LN2 = 0.6931471805599453
EXP_UNDERFLOW = -104.0
DEAD_LOGIT = -110.0
REACH_SLACK = 1.0 + 2.0 ** -10
MESH_AXES = ("x", "y", "c")


def _dot(a, b):
    return jnp.dot(a, b, preferred_element_type=F32)


def _dot_nt(a, b):
    return lax.dot_general(a, b, (((1,), (1,)), ((), ())), preferred_element_type=F32)


def _dot_tn(a, b):
    return lax.dot_general(a, b, (((0,), (0,)), ((), ())), preferred_element_type=F32)


def _split2(x):
    hi = x.astype(MXU_DTYPE)
    lo = (x - hi.astype(F32)).astype(MXU_DTYPE)
    return hi, lo


def _split3(x):
    hi = x.astype(MXU_DTYPE)
    r = x - hi.astype(F32)
    mid = r.astype(MXU_DTYPE)
    lo = (r - mid.astype(F32)).astype(MXU_DTYPE)
    return hi, mid, lo


def _dot_exact_r(x, pm):
    hi, mid, lo = _split3(x)
    return _dot(hi, pm) + _dot(mid, pm) + _dot(lo, pm)


def _dot_exact_l(pm, x):
    hi, mid, lo = _split3(x)
    return _dot(pm, hi) + _dot(pm, mid) + _dot(pm, lo)


def _pick(dim, prefs):
    for p in prefs:
        if dim % p == 0:
            return p
    return dim


def _softplus(z):
    return jnp.maximum(z, 0.0) + jnp.log(1.0 + jnp.exp(-jnp.abs(z)))


def _tile_options(dim):
    opts = [d for d in range(128, min(dim, 2048) + 1, 128) if dim % d == 0]
    return opts or [dim]


def _matmul_tiles(m, n, k, out_bytes):
    tk = k if k <= 4096 else _pick(k, (1024, 512, 256, 128))
    best = None
    for tm in _tile_options(m):
        for tn in _tile_options(n):
            vmem = 2 * 2 * (tm * tk + tk * tn) + 4 * tm * tn + 2 * out_bytes * tm * tn
            if vmem <= MATMUL_VMEM and (best is None or tm * tn / (tm + tn) > best[0]):
                best = (tm * tn / (tm + tn), tm, tn)
    return best[1], best[2], tk


def _matmul(a, b, out_dtype, name, mode="nn"):
    m, k = (a.shape[1], a.shape[0]) if mode == "tn" else a.shape
    n = b.shape[0] if mode == "nt" else b.shape[1]
    tm, tn, tk = _matmul_tiles(m, n, k, jnp.dtype(out_dtype).itemsize)
    nk = k // tk
    dot = {"nn": _dot, "tn": _dot_tn, "nt": _dot_nt}[mode]

    def kern(a_ref, b_ref, o_ref, *acc):
        if nk == 1:
            o_ref[...] = dot(a_ref[...], b_ref[...]).astype(o_ref.dtype)
            return
        acc_ref, = acc
        kk = pl.program_id(2)

        @pl.when(kk == 0)
        def _():
            acc_ref[...] = jnp.zeros_like(acc_ref)

        acc_ref[...] += dot(a_ref[...], b_ref[...])

        @pl.when(kk == nk - 1)
        def _():
            o_ref[...] = acc_ref[...].astype(o_ref.dtype)

    a_spec = (pl.BlockSpec((tk, tm), lambda i, j, kk: (kk, i)) if mode == "tn"
              else pl.BlockSpec((tm, tk), lambda i, j, kk: (i, kk)))
    b_spec = (pl.BlockSpec((tn, tk), lambda i, j, kk: (j, kk)) if mode == "nt"
              else pl.BlockSpec((tk, tn), lambda i, j, kk: (kk, j)))
    return pl.pallas_call(
        kern, name=name, grid=(m // tm, n // tn, nk), in_specs=[a_spec, b_spec],
        out_specs=pl.BlockSpec((tm, tn), lambda i, j, kk: (i, j)),
        out_shape=jax.ShapeDtypeStruct((m, n), out_dtype),
        scratch_shapes=[pltpu.VMEM((tm, tn), F32)] if nk > 1 else [],
        compiler_params=pltpu.CompilerParams(
            dimension_semantics=("parallel", "parallel", "arbitrary"), vmem_limit_bytes=VMEM_LIMIT),
    )(a.astype(MXU_DTYPE), b.astype(MXU_DTYPE))


def _piece_arrays(pieces):
    return [a for p in pieces for a in (p if isinstance(p, tuple) else (p,))]


def _join_pieces(refs, pieces):
    refs, cols = list(refs), []
    for p in pieces:
        vals = [refs.pop(0)[...] for _ in (p if isinstance(p, tuple) else (p,))]
        cols.append(functools.reduce(jnp.add, vals).astype(MXU_DTYPE))
    return jnp.concatenate(cols, axis=1)


def _proj_dx(pieces, w):
    arrs = _piece_arrays(pieces)
    rows, n = arrs[0].shape[0], w.shape[0]
    tm = min(PROJ_BWD_ROWS, rows)

    def kern(*refs):
        refs[-1][...] = _dot_nt(_join_pieces(refs[:len(arrs)], pieces), refs[len(arrs)][...])

    return pl.pallas_call(
        kern, name="proj_dx", grid=(rows // tm,),
        in_specs=[pl.BlockSpec((tm, a.shape[1]), lambda i: (i, 0)) for a in arrs]
        + [pl.BlockSpec(w.shape, lambda i: (0, 0), pipeline_mode=pl.Buffered(1))],
        out_specs=pl.BlockSpec((tm, n), lambda i: (i, 0)), out_shape=jax.ShapeDtypeStruct((rows, n), F32),
        compiler_params=pltpu.CompilerParams(dimension_semantics=("parallel",), vmem_limit_bytes=VMEM_LIMIT),
    )(*arrs, w.astype(MXU_DTYPE))


def _proj_dw(a, pieces):
    arrs = _piece_arrays(pieces)
    rows, m = a.shape
    k = sum(x.shape[1] for x in (p[0] if isinstance(p, tuple) else p for p in pieces))
    tk = min(PROJ_BWD_ROWS, rows)

    def kern(*refs):
        o_ref = refs[-1]

        @pl.when(pl.program_id(0) == 0)
        def _():
            o_ref[...] = jnp.zeros_like(o_ref)

        o_ref[...] += _dot_tn(refs[0][...], _join_pieces(refs[1:1 + len(arrs)], pieces))

    return pl.pallas_call(
        kern, name="proj_dw", grid=(rows // tk,),
        in_specs=[pl.BlockSpec((tk, m), lambda i: (i, 0))]
        + [pl.BlockSpec((tk, x.shape[1]), lambda i: (i, 0)) for x in arrs],
        out_specs=pl.BlockSpec((m, k), lambda i: (0, 0), pipeline_mode=pl.Buffered(1)),
        out_shape=jax.ShapeDtypeStruct((m, k), F32),
        compiler_params=pltpu.CompilerParams(dimension_semantics=("arbitrary",), vmem_limit_bytes=VMEM_LIMIT),
    )(a, *arrs)


def _rowwise(body, name, rows, tile, row_ins, full_ins, row_outs, acc_outs=(), scratch=(),
             reverse=False, sequential=False):
    n = rows // tile

    def ridx(i):
        return (n - 1 - i) if reverse else i

    in_specs, args = [], []
    for arr, width, cb in row_ins:
        in_specs.append(pl.BlockSpec((tile, width), lambda i, cb=cb: (ridx(i), cb)))
        args.append(arr)
    for arr in full_ins:
        in_specs.append(pl.BlockSpec(arr.shape, lambda i, nd=arr.ndim: (0,) * nd))
        args.append(arr)
    out_shape = [jax.ShapeDtypeStruct((rows, w), dt) for w, dt in row_outs]
    out_shape += [jax.ShapeDtypeStruct(s, dt) for s, dt in acc_outs]
    out_specs = [pl.BlockSpec((tile, w), lambda i: (ridx(i), 0)) for w, dt in row_outs]
    out_specs += [pl.BlockSpec(s, lambda i, nd=len(s): (0,) * nd) for s, dt in acc_outs]

    def kern(*refs):
        body(pl.program_id(0), *refs)

    sem = "arbitrary" if (acc_outs or sequential) else "parallel"
    return pl.pallas_call(
        kern, name=name, grid=(n,), in_specs=in_specs, out_specs=out_specs, out_shape=out_shape,
        scratch_shapes=list(scratch),
        compiler_params=pltpu.CompilerParams(dimension_semantics=(sem,), vmem_limit_bytes=VMEM_LIMIT),
    )(*args)


def _ln_stats(u):
    mu = jnp.mean(u, axis=-1, keepdims=True)
    xc = u - mu
    var = jnp.mean(xc * xc, axis=-1, keepdims=True)
    return xc, lax.rsqrt(var + LN_EPS)


def _ln_fwd(a, b, g, beta, name):
    rows, d = a.shape
    has_b = b is not None

    def body(i, *refs):
        if has_b:
            a_ref, b_ref, g_ref, be_ref, h_ref, hb_ref = refs
            u = ALPHA * a_ref[...] + b_ref[...]
        else:
            a_ref, g_ref, be_ref, h_ref, hb_ref = refs
            u = a_ref[...]
        xc, rstd = _ln_stats(u)
        y = xc * rstd * g_ref[...] + be_ref[...]
        h_ref[...] = y
        hb_ref[...] = y.astype(hb_ref.dtype)

    row_ins = [(a, d, 0)] + ([(b, d, 0)] if has_b else [])
    return _rowwise(body, name, rows, min(ROW_TILE, rows), row_ins,
                    [g.reshape(1, d), beta.reshape(1, d)], [(d, F32), (d, MXU_DTYPE)])


def _ln_bwd(a, b, g, dy1, dy2, c1, name):
    rows, d = a.shape
    has_b = b is not None
    has_2 = dy2 is not None

    def body(i, *refs):
        refs = list(refs)
        a_ref = refs.pop(0)
        u = a_ref[...]
        if has_b:
            u = ALPHA * u + refs.pop(0)[...]
        dy = c1 * refs.pop(0)[...]
        if has_2:
            dy = dy + refs.pop(0)[...]
        g_ref, du_ref, dub_ref, dg_ref, db_ref = refs

        @pl.when(i == 0)
        def _():
            dg_ref[...] = jnp.zeros_like(dg_ref)
            db_ref[...] = jnp.zeros_like(db_ref)

        xc, rstd = _ln_stats(u)
        xhat = xc * rstd
        dxh = dy * g_ref[...]
        m1 = jnp.mean(dxh, axis=-1, keepdims=True)
        m2 = jnp.mean(dxh * xhat, axis=-1, keepdims=True)
        du = rstd * (dxh - m1 - xhat * m2)
        du_ref[...] = du
        dub_ref[...] = du.astype(dub_ref.dtype)
        dg_ref[...] += jnp.sum(dy * xhat, axis=0, keepdims=True)
        db_ref[...] += jnp.sum(dy, axis=0, keepdims=True)

    row_ins = [(a, d, 0)] + ([(b, d, 0)] if has_b else []) + [(dy1, d, 0)] + ([(dy2, d, 0)] if has_2 else [])
    return _rowwise(body, name, rows, min(ROW_TILE, rows), row_ins, [g.reshape(1, d)],
                    [(d, F32), (d, MXU_DTYPE)], [((1, d), F32), ((1, d), F32)])


def _loss_grad(h, target):
    rows, d = h.shape

    def body(i, h_ref, t_ref, dh_ref, acc_ref):
        @pl.when(i == 0)
        def _():
            acc_ref[...] = jnp.zeros_like(acc_ref)

        e = h_ref[...] - t_ref[...]
        dh_ref[...] = e * (1.0 / d)
        acc_ref[...] += jnp.sum(e * e, axis=0, keepdims=True)

    return _rowwise(body, "loss_grad", rows, ROW_TILE, [(h, d, 0), (target, d, 0)], [],
                    [(d, F32)], [((1, d), F32)])


def _gate_fwd(groups, pb):
    rows = pb.shape[0]
    w = GROUP_W * len(groups)

    def body(i, *refs):
        g = refs[4][...]
        mixed = jnp.concatenate([r[...] for r in refs[:4]], axis=1)
        refs[5][...] = (mixed * (g / (1.0 + jnp.exp(-g)))).astype(refs[5].dtype)

    return _rowwise(body, "gate_fwd", rows, ROW_TILE, [(o, GROUP_W, 0) for o in groups] + [(pb, w, 0)], [],
                    [(w, MXU_DTYPE)])[0]


def _gate_bwd(dgated, groups, pb):
    rows = pb.shape[0]
    w = GROUP_W * len(groups)

    def body(i, *refs):
        dg = refs[0][...]
        mixed = jnp.concatenate([r[...] for r in refs[1:5]], axis=1)
        g = refs[5][...]
        dm_ref, dgate_ref = refs[6], refs[7]
        sig = 1.0 / (1.0 + jnp.exp(-g))
        dm_ref[...] = dg * (g * sig)
        dgate_ref[...] = (dg * mixed * (sig * (1.0 + g * (1.0 - sig)))).astype(dgate_ref.dtype)

    return _rowwise(body, "gate_bwd", rows, ROW_TILE,
                    [(dgated, w, 0)] + [(o, GROUP_W, 0) for o in groups] + [(pb, w, 0)], [],
                    [(w, F32), (w, MXU_DTYPE)])


def _tri(n, kind):
    r = np.arange(n)[:, None]
    c = np.arange(n)[None, :]
    m = {"lower_incl": r >= c, "upper_incl": r <= c, "row_gt_col": r > c, "row_lt_col": r < c}[kind]
    return jnp.asarray(m.astype(np.float32), dtype=MXU_DTYPE)


def _forget_fwd(pb, bias_row):
    rows = pb.shape[0]
    tile = _pick(rows, (1024, 512, 256))

    def body(i, x_ref, b_ref, l_ref, o_ref, carry_ref):
        @pl.when(i == 0)
        def _():
            carry_ref[...] = jnp.zeros_like(carry_ref)

        xx = x_ref[...] + b_ref[...]
        lane = lax.broadcasted_iota(jnp.int32, xx.shape, 1)
        lf = jnp.where(lane < N_HEADS, -_softplus(-xx), 0.0)
        o_ref[...] = _dot_exact_l(l_ref[...], lf) + carry_ref[...]
        carry_ref[...] += jnp.sum(lf, axis=0, keepdims=True)

    return _rowwise(body, "forget_fwd", rows, tile, [(pb, 128, PB_MISC_BLK)],
                    [bias_row, _tri(tile, "lower_incl")], [(128, F32)],
                    scratch=[pltpu.VMEM((1, 128), F32)], sequential=True)[0]


def _forget_bwd(pb, bias_row, dfc):
    rows = pb.shape[0]
    tile = _pick(rows, (1024, 512, 256))

    def body(i, x_ref, df_ref, b_ref, u_ref, o_ref, db_ref, carry_ref):
        @pl.when(i == 0)
        def _():
            carry_ref[...] = jnp.zeros_like(carry_ref)
            db_ref[...] = jnp.zeros_like(db_ref)

        df = df_ref[...]
        sfx = _dot_exact_l(u_ref[...], df) + carry_ref[...]
        carry_ref[...] += jnp.sum(df, axis=0, keepdims=True)
        xx = x_ref[...] + b_ref[...]
        lane = lax.broadcasted_iota(jnp.int32, xx.shape, 1)
        dl = jnp.where(lane < N_HEADS, sfx / (1.0 + jnp.exp(xx)), 0.0)
        o_ref[...] = dl
        db_ref[...] += jnp.sum(dl, axis=0, keepdims=True)

    return _rowwise(body, "forget_bwd", rows, tile,
                    [(pb, 128, PB_MISC_BLK), (dfc, 128, 0)],
                    [bias_row, _tri(tile, "upper_incl")], [(128, F32)], [((1, 128), F32)],
                    scratch=[pltpu.VMEM((1, 128), F32)], reverse=True, sequential=True)


def _rope_tables(s):
    half = MLA_ROPE // 2
    inv_freq = ROPE_THETA ** (-jnp.arange(half, dtype=F32) / half)
    ang = jnp.arange(s).astype(F32)[:, None] * inv_freq[None, :]
    cos2 = jnp.tile(jnp.cos(ang), (1, 2))
    sin2 = jnp.tile(jnp.sin(ang), (1, 2))
    cx = jnp.tile(cos2, (1, N_HEADS))
    sx = jnp.tile(sin2, (1, N_HEADS))
    pad = ((0, 0), (MISC_KROT, 128 - MISC_KROT - MLA_ROPE))
    ck = jnp.pad(cos2, pad)
    sk = jnp.pad(sin2, pad)
    return dict(ck=ck, sk=sk, cx=cx, sx=sx)


def _rot_matrix(width, bases):
    half = MLA_ROPE // 2
    p = np.zeros((width, width), np.float32)
    for b in bases:
        for i in range(half):
            p[b + half + i, b + i] = -1.0
            p[b + i, b + half + i] = 1.0
    return p


def _rope_matrices():
    pk = _rot_matrix(128, [MISC_KROT])
    p4 = _rot_matrix(128, [h * MLA_ROPE for h in range(N_HEADS)])
    a = np.zeros((128, 128), np.float32)
    for h in range(N_HEADS):
        for r in range(MLA_ROPE):
            a[h * MLA_ROPE + r, MISC_KROT + r] = 1.0
    cast = lambda m: jnp.asarray(m, dtype=MXU_DTYPE)
    return dict(p4=cast(p4), p4t=cast(p4.T), pk=cast(pk), spread=cast(a.T), xa=cast(a), xb=cast(p4.T @ a))


def _rms(c, g):
    r = lax.rsqrt(jnp.mean(c * c, axis=-1, keepdims=True) + RMS_EPS)
    return c * r * g


def _mla_q_fwd(pb, g, w_up, tabs, mats):
    rows = pb.shape[0]

    def body(i, c_ref, cos_ref, sin_ref, g_ref, w_ref, p_ref, q_ref, cn_ref):
        cn = _rms(c_ref[...], g_ref[...]).astype(cn_ref.dtype)
        cn_ref[...] = cn
        q = _dot(cn, w_ref[...])
        qr = q[:, GROUP_W:]
        q_ref[:, :GROUP_W] = q[:, :GROUP_W].astype(q_ref.dtype)
        q_ref[:, GROUP_W:] = (qr * cos_ref[...] + _dot_exact_r(qr, p_ref[...]) * sin_ref[...]).astype(q_ref.dtype)

    return _rowwise(body, "mla_q_fwd", rows, ROW_TILE,
                    [(pb, MLA_Q_RANK, PB_CQ_BLK), (tabs["cx"], PAIR, 0), (tabs["sx"], PAIR, 0)],
                    [g.reshape(1, MLA_Q_RANK), w_up.astype(MXU_DTYPE), mats["p4"]],
                    [(MLA_Q_COLS, MXU_DTYPE), (MLA_Q_RANK, MXU_DTYPE)])


def _mla_kv_fwd(pb, g, w_up, tabs, mats):
    rows = pb.shape[0]

    def body(i, c_ref, x_ref, cos_ref, sin_ref, g_ref, w_ref, p_ref, sp_ref, kv_ref, cn_ref, kr_ref):
        cn = _rms(c_ref[...], g_ref[...]).astype(cn_ref.dtype)
        cn_ref[...] = cn
        kv_ref[...] = _dot(cn, w_ref[...]).astype(kv_ref.dtype)
        xx = x_ref[...]
        kr = xx * cos_ref[...] + _dot_exact_r(xx, p_ref[...]) * sin_ref[...]
        kr_ref[...] = _dot_exact_r(kr, sp_ref[...]).astype(kr_ref.dtype)

    return _rowwise(body, "mla_kv_fwd", rows, ROW_TILE,
                    [(pb, MLA_KV_RANK, PB_CKV_BLK), (pb, 128, PB_MISC_BLK), (tabs["ck"], 128, 0), (tabs["sk"], 128, 0)],
                    [g.reshape(1, MLA_KV_RANK), w_up.astype(MXU_DTYPE), mats["pk"], mats["spread"]],
                    [(MLA_KV_COLS, MXU_DTYPE), (MLA_KV_RANK, MXU_DTYPE), (128, MXU_DTYPE)])


def _rope_q_bwd(dq_full, tabs, mats):
    rows, nq = dq_full.shape

    def body(i, d_ref, cos_ref, sin_ref, pt_ref, o_ref):
        dr = d_ref[:, GROUP_W:]
        o_ref[:, :GROUP_W] = d_ref[:, :GROUP_W].astype(o_ref.dtype)
        o_ref[:, GROUP_W:] = (dr * cos_ref[...] + _dot_exact_r(dr * sin_ref[...], pt_ref[...])).astype(o_ref.dtype)

    return _rowwise(body, "rope_q_bwd", rows, ROW_TILE,
                    [(dq_full, nq, 0), (tabs["cx"], PAIR, 0), (tabs["sx"], PAIR, 0)], [mats["p4t"]],
                    [(nq, MXU_DTYPE)])[0]


def _rope_k_bwd(dkr, tabs, mats):
    rows = dkr.shape[0]

    def body(i, d_ref, cos_ref, sin_ref, a_ref, b_ref, o_ref):
        d = d_ref[...]
        o_ref[...] = _dot_exact_r(d * cos_ref[...], a_ref[...]) + _dot_exact_r(d * sin_ref[...], b_ref[...])

    return _rowwise(body, "rope_k_bwd", rows, ROW_TILE,
                    [(dkr, 128, 0), (tabs["cx"], 128, 0), (tabs["sx"], 128, 0)], [mats["xa"], mats["xb"]],
                    [(128, F32)])[0]


def _rms_bwd(pb, width, col_blk, g, dy, name):
    rows = pb.shape[0]

    def body(i, c_ref, dy_ref, g_ref, dc_ref, dg_ref):
        @pl.when(i == 0)
        def _():
            dg_ref[...] = jnp.zeros_like(dg_ref)

        c = c_ref[...]
        dy = dy_ref[...]
        r = lax.rsqrt(jnp.mean(c * c, axis=-1, keepdims=True) + RMS_EPS)
        dyg = dy * g_ref[...]
        dc = r * dyg - c * (r * r * r) * jnp.mean(c * dyg, axis=-1, keepdims=True)
        dc_ref[...] = dc.astype(dc_ref.dtype)
        dg_ref[...] += jnp.sum(dy * c * r, axis=0, keepdims=True)

    return _rowwise(body, name, rows, ROW_TILE, [(pb, width, col_blk), (dy, width, 0)], [g.reshape(1, width)],
                    [(width, MXU_DTYPE)], [((1, width), F32)])


def _att_params(parallel):
    return pltpu.CompilerParams(dimension_semantics=("parallel" if parallel else "arbitrary",),
                                vmem_limit_bytes=VMEM_LIMIT)


def _blk_off(j, t):
    return j * t if isinstance(j, int) else pl.multiple_of(j * t, t)


def _causal_mask(t, strict):
    r = lax.broadcasted_iota(jnp.int32, (t, t), 0)
    c = lax.broadcasted_iota(jnp.int32, (t, t), 1)
    return (c < r) if strict else (c <= r)


def _lane_mask(kind, head, rows):
    lane = lax.broadcasted_iota(jnp.int32, (rows, PAIR), 1)
    if kind == "pair":
        return (lane < HEAD_DIM) if head % 2 == 0 else (lane >= HEAD_DIM)
    return (lane >= MLA_ROPE * head) & (lane < MLA_ROPE * (head + 1))


def _row_spec(t, cb, width=PAIR):
    return pl.BlockSpec((t, width), lambda i, cb=cb: (i, cb))


def _whole_spec(rows, cb, width=PAIR):
    return pl.BlockSpec((rows, width), lambda i, cb=cb: (0, cb), pipeline_mode=pl.Buffered(1))


def _is_pow2(x):
    return float(np.frexp(x)[0]) == 0.5


def _masked_heads(blocks, kinds, pair, dtype, scale=None):
    out = []
    for e in range(2):
        head = 2 * pair + e
        parts = [jnp.where(_lane_mask(k, head, b.shape[0]), b.astype(F32) * (1.0 if scale is None else scale),
                           0.0).astype(dtype)
                 for b, k in zip(blocks, kinds)]
        out.append(parts[0] if len(parts) == 1 else jnp.concatenate(parts, axis=1))
    return out


def _logit_reach(qh, kmax2, head):
    q32 = qh.astype(F32)
    return jnp.sqrt(jnp.sum(q32 * q32, axis=1, keepdims=True) * _col(kmax2, head)) * REACH_SLACK


def _forget_top(ft_ref, head, off):
    return jnp.max(-ft_ref[head:head + 1, pl.ds(off, PAIR)])


def _col(block, idx):
    lane = lax.broadcasted_iota(jnp.int32, block.shape, 1)
    return jnp.sum(jnp.where(lane == idx, block, 0.0), axis=1, keepdims=True)


def _scatter_cols(cols, t):
    lane = lax.broadcasted_iota(jnp.int32, (t, PAIR), 1)
    out = jnp.zeros((t, PAIR), F32)
    for idx, c in cols.items():
        out = out + jnp.where(lane == idx, c, 0.0)
    return out


def _take_heads(per_head, pair):
    return jnp.where(_lane_mask("pair", 0, per_head[0].shape[0]), per_head[0], per_head[1])


class _Parts:
    def __init__(self, q_parts, k_parts, v_parts, tq, sk):
        self.kinds = [[kind for _, _, kind in q_parts[p]] for p in range(2)]
        self.nparts = len(q_parts[0])
        self.q_specs = [_row_spec(tq, cb) for p in range(2) for _, cb, _ in q_parts[p]]
        self.q_args = [a for p in range(2) for a, _, _ in q_parts[p]]
        self.k_specs = [_whole_spec(sk, cb) for p in range(2) for _, cb, _ in k_parts[p]]
        self.k_args = [a for p in range(2) for a, _, _ in k_parts[p]]
        self.v_specs = [_whole_spec(sk, cb) for _, cb in v_parts]
        self.v_args = [a for a, _ in v_parts]
        self.width = PAIR * self.nparts

    def split(self, refs):
        n = self.nparts
        refs = list(refs)
        q = [refs[p * n:(p + 1) * n] for p in range(2)]
        k = [refs[2 * n + p * n:2 * n + (p + 1) * n] for p in range(2)]
        v = refs[4 * n:4 * n + 2]
        return q, k, v, refs[4 * n + 2:]

    def k_block(self, k_refs, off, t):
        blks = [r[pl.ds(off, t), :] for r in k_refs]
        return blks[0] if len(blks) == 1 else jnp.concatenate(blks, axis=1)


def _key_norm_max(src, k_blk):
    rows = src.shape[0]

    def body(i, k0_ref, k1_ref, o_ref):
        @pl.when(i == 0)
        def _():
            o_ref[...] = jnp.zeros_like(o_ref)

        cols = {}
        for p, ref in enumerate((k0_ref, k1_ref)):
            k32 = ref[...].astype(F32)
            for e in range(2):
                sq = jnp.sum(jnp.where(_lane_mask("pair", e, k32.shape[0]), k32 * k32, 0.0), axis=1, keepdims=True)
                cols[2 * p + e] = jnp.max(sq, axis=0, keepdims=True)
        lane = lax.broadcasted_iota(jnp.int32, (1, PAIR), 1)
        o_ref[...] = jnp.maximum(o_ref[...], sum(jnp.where(lane == h, c, 0.0) for h, c in cols.items()))

    return _rowwise(body, "key_norm_max", rows, ROW_TILE, [(src, PAIR, k_blk), (src, PAIR, k_blk + 1)], [], [],
                    [((1, PAIR), F32)])[0]


def _softmax_fwd(q_parts, k_parts, v_parts, sq, sk, scale, causal, bias, name):
    tq = min(SOFTMAX_TILE, sq)
    tk = tq if causal else min(SOFTMAX_TILE, sk)
    nkv = sk // tk
    pp = _Parts(q_parts, k_parts, v_parts, tq, sk)

    def kern(*refs):
        q_refs, k_refs, v_refs, rest = pp.split(refs)
        if bias is not None:
            fc_ref, ft_ref, kmax_ref, o_ref, lse_ref = rest
            fcb = fc_ref[...]
        else:
            o_ref, lse_ref = rest
        i = pl.program_id(0)
        fold = _is_pow2(scale)
        head_on = [jnp.where(_lane_mask("pair", e, tk), 1.0, 0.0).astype(MXU_DTYPE) for e in range(2)]
        head_off = [jnp.where(_lane_mask("pair", e, tk), 0.0, 1.0).astype(MXU_DTYPE) for e in range(2)]
        lse_cols = {}
        for first in range(0, 2, FWD_PAIRS_PER_LOOP):
            pairs = list(range(first, first + FWD_PAIRS_PER_LOOP))
            heads = [2 * p + e for p in pairs for e in range(2)]
            qm = {}
            for p in pairs:
                masked_q = _masked_heads([r[...] for r in q_refs[p]], pp.kinds[p], p, MXU_DTYPE, scale if fold else None)
                qm.update({2 * p + e: masked_q[e] for e in range(2)})
            if bias is not None:
                reach = {h: _logit_reach(qm[h], kmax_ref[...], h) for h in heads}

            def block(j, carry, masked, pairs=pairs, qm=qm):
                off = _blk_off(j, tk)
                out = []
                for p in pairs:
                    kb = pp.k_block(k_refs[p], off, tk)
                    vb = v_refs[p][pl.ds(off, tk), :]
                    for e in range(2):
                        h = 2 * p + e
                        m, acc = carry[len(out)]
                        s = _dot_nt(qm[h], kb) * (LOG2E if fold else scale * LOG2E)
                        if bias is not None:
                            s = s - ft_ref[h:h + 1, pl.ds(off, tk)] * LOG2E
                        if masked:
                            s = jnp.where(_causal_mask(tq, False), s, NEG)
                        m_new = jnp.maximum(m, jnp.max(s, axis=1, keepdims=True))
                        pr = jnp.exp2(s - m_new).astype(MXU_DTYPE)
                        out.append((m_new, jnp.exp2(m - m_new) * acc + _dot(pr, vb * head_on[e] + head_off[e])))
                return tuple(out)

            carry = tuple((jnp.full((tq, 1), NEG, F32), jnp.zeros((tq, PAIR), F32)) for _ in heads)
            if causal and bias is not None:
                def alive(c, j, heads=heads, reach=reach):
                    return functools.reduce(jnp.maximum, [
                        jnp.max(reach[h] + _forget_top(ft_ref, h, _blk_off(j, tk)) - c[n][0] * LN2)
                        for n, h in enumerate(heads)])

                def step(state, block=block, alive=alive):
                    n, _, c = state
                    c = block(i - 1 - n, c, False)
                    return n + 1, alive(c, i - 1 - n), c

                carry = block(i, carry, True)
                _, _, carry = lax.while_loop(lambda st: jnp.logical_and(st[0] < i, st[1] > DEAD_LOGIT), step,
                                             (jnp.int32(0), alive(carry, i), carry))
            elif causal:
                carry = lax.fori_loop(0, i, lambda j, c, block=block: block(j, c, False), carry)
                carry = block(i, carry, True)
            else:
                for j in range(nkv):
                    carry = block(j, carry, False)
            for n, p in enumerate(pairs):
                outs = []
                for e in range(2):
                    m, acc = carry[2 * n + e]
                    l = _col(acc, HEAD_DIM * (1 - e))
                    outs.append(acc / l)
                    lse_cols[2 * p + e] = m * LN2 + jnp.log(l) + (_col(fcb, 2 * p + e) if bias is not None else 0.0)
                o_ref[:, p * PAIR:(p + 1) * PAIR] = _take_heads(outs, p)
        lse_ref[...] = _scatter_cols(lse_cols, tq)

    in_specs = pp.q_specs + pp.k_specs + pp.v_specs
    args = pp.q_args + pp.k_args + pp.v_args
    if bias is not None:
        in_specs += [_row_spec(tq, 0), pl.BlockSpec((8, sk), lambda i: (0, 0), pipeline_mode=pl.Buffered(1)),
                     pl.BlockSpec((1, PAIR), lambda i: (0, 0))]
        args += list(bias)
    return pl.pallas_call(
        kern, name=name, grid=(sq // tq,), in_specs=in_specs,
        out_specs=[_row_spec(tq, 0, GROUP_W), _row_spec(tq, 0)],
        out_shape=[jax.ShapeDtypeStruct((sq, GROUP_W), F32), jax.ShapeDtypeStruct((sq, PAIR), F32)],
        compiler_params=_att_params(True),
    )(*args)


def _softmax_bwd(q_parts, k_parts, v_parts, o, lse, do, do_blk, sq, sk, scale, causal, bias, name):
    tq = min(SOFTMAX_TILE, sq)
    tk = tq if causal else min(SOFTMAX_TILE, sk)
    nkv = sk // tk
    pp = _Parts(q_parts, k_parts, v_parts, tq, sk)
    quad = pp.nparts == 2
    wq = GROUP_W + (PAIR if quad else 0)

    def kern(*refs):
        q_refs, k_refs, v_refs, rest = pp.split(refs)
        if bias is not None:
            o_ref, lse_ref, do_ref, fc_ref, ft_ref, kmax_ref, dq_ref, dk_ref, dv_ref, dfq_ref, dfk_ref = rest
            fcb = fc_ref[...]
        else:
            o_ref, lse_ref, do_ref, dq_ref, dk_ref, dv_ref = rest
        i = pl.program_id(0)
        fold = _is_pow2(scale)

        @pl.when(i == 0)
        def _():
            dk_ref[...] = jnp.zeros_like(dk_ref)
            dv_ref[...] = jnp.zeros_like(dv_ref)
            if bias is not None:
                dfk_ref[...] = jnp.zeros_like(dfk_ref)

        lse_b = lse_ref[...]
        qm, dom, delta, lse_h = [], [], [], []
        for p in range(2):
            qm += _masked_heads([r[...] for r in q_refs[p]], pp.kinds[p], p, MXU_DTYPE, scale if fold else None)
            do_p = do_ref[:, p * PAIR:(p + 1) * PAIR]
            dom += _masked_heads([do_p], ["pair"], p, MXU_DTYPE)
            prod = do_p * o_ref[:, p * PAIR:(p + 1) * PAIR]
            for e in range(2):
                h = 2 * p + e
                delta.append(jnp.sum(jnp.where(_lane_mask("pair", h, tq), prod, 0.0), axis=1, keepdims=True))
                lse_h.append(_col(lse_b, h) - (_col(fcb, h) if bias is not None else 0.0))

        def block(j, carry, masked):
            off = _blk_off(j, tk)
            out = []
            for p in range(2):
                kb = pp.k_block(k_refs[p], off, tk)
                vb = v_refs[p][pl.ds(off, tk), :]
                dk_acc = jnp.zeros((tk, pp.width), F32)
                dv_acc = jnp.zeros((tk, PAIR), F32)
                for e in range(2):
                    h = 2 * p + e
                    dq, dfq = carry[h]
                    s = _dot_nt(qm[h], kb)
                    if not fold:
                        s = s * scale
                    if bias is not None:
                        s = s - ft_ref[h:h + 1, pl.ds(off, tk)]
                    if masked:
                        s = jnp.where(_causal_mask(tq, False), s, NEG)
                    pr = jnp.exp(s - lse_h[h])
                    ds = pr * (_dot_nt(dom[h], vb) - delta[h])
                    dsb = (ds if fold else ds * scale).astype(MXU_DTYPE)
                    dv_acc = dv_acc + _dot_tn(pr.astype(MXU_DTYPE), dom[h])
                    dk_acc = dk_acc + _dot_tn(dsb, qm[h])
                    dq = dq + _dot(dsb, kb)
                    if bias is not None:
                        dfq = dfq + jnp.sum(ds, axis=1, keepdims=True)
                        dfk_ref[h:h + 1, pl.ds(off, tk)] -= jnp.sum(ds, axis=0, keepdims=True)
                    out.append((dq, dfq))
                dv_ref[pl.ds(off, tk), p * PAIR:(p + 1) * PAIR] += dv_acc
                dk_ref[pl.ds(off, tk), p * PAIR:(p + 1) * PAIR] += dk_acc[:, :PAIR]
                if quad:
                    dk_ref[pl.ds(off, tk), GROUP_W:] += dk_acc[:, PAIR:]
            return tuple(out)

        carry = tuple((jnp.zeros((tq, pp.width), F32), jnp.zeros((tq, 1), F32)) for _ in range(N_HEADS))
        if causal and bias is not None:
            reach = [_logit_reach(qm[h], kmax_ref[...], h) - lse_h[h] for h in range(N_HEADS)]

            def alive(j):
                return functools.reduce(jnp.maximum, [jnp.max(reach[h] + _forget_top(ft_ref, h, _blk_off(j, tk)))
                                                      for h in range(N_HEADS)])

            def step(state):
                n, _, c = state
                return n + 1, alive(i - 1 - n), block(i - 1 - n, c, False)

            carry = block(i, carry, True)
            _, _, carry = lax.while_loop(lambda st: jnp.logical_and(st[0] < i, st[1] > DEAD_LOGIT), step,
                                         (jnp.int32(0), alive(i), carry))
        elif causal:
            carry = lax.fori_loop(0, i, lambda j, c: block(j, c, False), carry)
            carry = block(i, carry, True)
        else:
            for j in range(nkv):
                carry = block(j, carry, False)
        dqs = [c[0] * scale if fold else c[0] for c in carry]
        for p in range(2):
            dq_ref[:, p * PAIR:(p + 1) * PAIR] = _take_heads([dqs[2 * p + e][:, :PAIR] for e in range(2)], p)
        if quad:
            dq_ref[:, GROUP_W:] = sum(jnp.where(_lane_mask("quad", h, tq), dqs[h][:, PAIR:], 0.0)
                                      for h in range(N_HEADS))
        if bias is not None:
            dfq_ref[...] = _scatter_cols({h: carry[h][1] for h in range(N_HEADS)}, tq)

    acc_spec = lambda rows, width: pl.BlockSpec((rows, width), lambda i: (0, 0), pipeline_mode=pl.Buffered(1))
    in_specs = pp.q_specs + pp.k_specs + pp.v_specs + [_row_spec(tq, 0, GROUP_W), _row_spec(tq, 0),
                                                       _row_spec(tq, do_blk, GROUP_W)]
    args = pp.q_args + pp.k_args + pp.v_args + [o, lse, do]
    out_specs = [_row_spec(tq, 0, wq), acc_spec(sk, wq), acc_spec(sk, GROUP_W)]
    out_shape = [jax.ShapeDtypeStruct((sq, wq), F32), jax.ShapeDtypeStruct((sk, wq), F32),
                 jax.ShapeDtypeStruct((sk, GROUP_W), F32)]
    if bias is not None:
        in_specs += [_row_spec(tq, 0), pl.BlockSpec((8, sk), lambda i: (0, 0), pipeline_mode=pl.Buffered(1)),
                     pl.BlockSpec((1, PAIR), lambda i: (0, 0))]
        args += list(bias)
        out_specs += [_row_spec(tq, 0), acc_spec(8, sk)]
        out_shape += [jax.ShapeDtypeStruct((sq, PAIR), F32), jax.ShapeDtypeStruct((8, sk), F32)]
    return pl.pallas_call(
        kern, name=name, grid=(sq // tq,), in_specs=in_specs, out_specs=out_specs, out_shape=out_shape,
        compiler_params=_att_params(False),
    )(*args)


def _sb_logs(qh, kb, valid):
    z = _dot_nt(qh, kb)
    sp = _softplus(z)
    lk = -sp
    if valid is not None:
        lk = jnp.where(valid, lk, 0.0)
    return lk, z - sp


def _sb_valid(d, tq, tk):
    r = lax.broadcasted_iota(jnp.int32, (tq, tk), 0)
    c = lax.broadcasted_iota(jnp.int32, (tq, tk), 1)
    return c + d * tk < r


def _tri_sums(xs, tri):
    t = xs[0].shape[0]
    pieces = [_split2(x) for x in xs]
    hi = _dot(jnp.concatenate([pc[0] for pc in pieces], axis=0), tri)
    lo = _dot(jnp.concatenate([pc[1] for pc in pieces], axis=0), tri)
    return [hi[n * t:(n + 1) * t] + lo[n * t:(n + 1) * t] for n in range(len(xs))]


def _sb_fwd(src, q_blk, k_blk, v_blk, s, scale, name):
    assert _is_pow2(scale)
    tq, t = min(SB_QUERY_TILE, s), min(ATT_TILE, s)
    slots = -(-(s // t) // SB_SLOT) * SB_SLOT
    width = N_HEADS * slots
    band = tq // t
    pair = lambda blk: [[(src, blk + p, "pair")] for p in range(2)]
    pp = _Parts(pair(q_blk), pair(k_blk), [(src, v_blk + p) for p in range(2)], tq, s)

    def kern(*refs):
        q_refs, k_refs, v_refs, (tri_ref, o_ref, rm_ref, cnt_ref) = pp.split(refs)
        i = pl.program_id(0)
        tri = tri_ref[...]
        lane = lax.broadcasted_iota(jnp.int32, (tq, width), 1)
        qm = []
        for p in range(2):
            qm += _masked_heads([q_refs[p][0][...]], ["pair"], p, MXU_DTYPE, scale)

        def block(j, carry, valid):
            accs, rights, rm = carry
            off = _blk_off(j, t)
            kbs = [k_refs[p][0][pl.ds(off, t), :] for p in range(2)]
            vbs = [v_refs[p][pl.ds(off, t), :] for p in range(2)]
            logs = [_sb_logs(qm[h], kbs[h // 2], valid) for h in range(N_HEADS)]
            tails = _tri_sums([lg[0] for lg in logs], tri)
            new_acc, new_right = [], []
            for h in range(N_HEADS):
                lk, ls = logs[h]
                w = jnp.exp(ls + tails[h] + rights[h])
                if valid is not None:
                    w = jnp.where(valid, w, 0.0)
                new_acc.append(accs[h] + _dot(w.astype(MXU_DTYPE), vbs[h // 2]))
                rm = rm + jnp.where(lane == slots * h + j, rights[h], 0.0)
                new_right.append(rights[h] + jnp.sum(lk, axis=1, keepdims=True))
            return tuple(new_acc), tuple(new_right), rm

        carry = (tuple(jnp.zeros((tq, PAIR), F32) for _ in range(N_HEADS)),
                 tuple(jnp.zeros((tq, 1), F32) for _ in range(N_HEADS)), jnp.zeros((tq, width), F32))
        for d in reversed(range(band)):
            carry = block(band * i + d, carry, _sb_valid(d, tq, t))

        def alive(c):
            return functools.reduce(jnp.maximum, [jnp.max(r) for r in c[1]])

        def step(state):
            n, _, c = state
            c = block(band * i - 1 - n, c, None)
            return n + 1, alive(c), c

        n_done, _, carry = lax.while_loop(lambda st: jnp.logical_and(st[0] < band * i, st[1] > EXP_UNDERFLOW),
                                          step, (jnp.int32(0), alive(carry), carry))
        cnt_ref[i] = n_done
        for p in range(2):
            o_ref[:, p * PAIR:(p + 1) * PAIR] = _take_heads([carry[0][2 * p + e] for e in range(2)], p)
        rm_ref[...] = carry[2]

    return pl.pallas_call(
        kern, name=name, grid=(s // tq,),
        in_specs=pp.q_specs + pp.k_specs + pp.v_specs + [pl.BlockSpec((t, t), lambda i: (0, 0))],
        out_specs=[_row_spec(tq, 0, GROUP_W), _row_spec(tq, 0, width), pl.BlockSpec(memory_space=pltpu.SMEM)],
        out_shape=[jax.ShapeDtypeStruct((s, GROUP_W), F32), jax.ShapeDtypeStruct((s, width), F32),
                   jax.ShapeDtypeStruct((s // tq,), jnp.int32)],
        compiler_params=_att_params(False),
    )(*(pp.q_args + pp.k_args + pp.v_args + [_tri(t, "row_gt_col")]))


def _sb_bwd(src, q_blk, k_blk, v_blk, do, do_blk, rm, visited, s, scale, name):
    assert _is_pow2(scale)
    tq, t = min(SB_QUERY_TILE, s), min(ATT_TILE, s)
    band = tq // t
    pair = lambda blk: [[(src, blk + p, "pair")] for p in range(2)]
    pp = _Parts(pair(q_blk), pair(k_blk), [(src, v_blk + p) for p in range(2)], tq, s)

    def kern(*refs):
        q_refs, k_refs, v_refs, (do_ref, rm_ref, tri_ref, pre_ref, cnt_ref, dq_ref, dk_ref, dv_ref) = pp.split(refs)
        i = pl.program_id(0)

        @pl.when(i == 0)
        def _():
            dk_ref[...] = jnp.zeros_like(dk_ref)
            dv_ref[...] = jnp.zeros_like(dv_ref)

        rmb = rm_ref[...]
        tri = tri_ref[...]
        pre = pre_ref[...]
        qm, dom = [], []
        for p in range(2):
            qm += _masked_heads([q_refs[p][0][...]], ["pair"], p, MXU_DTYPE, scale)
            dom += _masked_heads([do_ref[:, p * PAIR:(p + 1) * PAIR]], ["pair"], p, MXU_DTYPE)

        def block(j, carry, valid):
            dqs, lefts = carry
            off = _blk_off(j, t)
            kbs = [k_refs[p][0][pl.ds(off, t), :] for p in range(2)]
            vbs = [v_refs[p][pl.ds(off, t), :] for p in range(2)]
            logs = [_sb_logs(qm[h], kbs[h // 2], valid) for h in range(N_HEADS)]
            tails = _tri_sums([lg[0] for lg in logs], tri)
            ws, gs = [], []
            for h in range(N_HEADS):
                lk, ls = logs[h]
                w = jnp.exp(ls + tails[h] + _col(rmb, (rm.shape[1] // N_HEADS) * h + j))
                if valid is not None:
                    w = jnp.where(valid, w, 0.0)
                ws.append(w)
                gs.append(_dot_nt(dom[h], vbs[h // 2]) * w)
            prefix = _tri_sums(gs, pre)
            new_dq, new_left = [], []
            dk_acc = [jnp.zeros((t, PAIR), F32) for _ in range(2)]
            dv_acc = [jnp.zeros((t, PAIR), F32) for _ in range(2)]
            for h in range(N_HEADS):
                lk, ls = logs[h]
                sig = jnp.exp(ls)
                dz = gs[h] * (1.0 - sig) - sig * (prefix[h] + lefts[h])
                if valid is not None:
                    dz = jnp.where(valid, dz, 0.0)
                dzb = dz.astype(MXU_DTYPE)
                dv_acc[h // 2] = dv_acc[h // 2] + _dot_tn(ws[h].astype(MXU_DTYPE), dom[h])
                dk_acc[h // 2] = dk_acc[h // 2] + _dot_tn(dzb, qm[h])
                new_dq.append(dqs[h] + _dot(dzb, kbs[h // 2]))
                new_left.append(lefts[h] + jnp.sum(gs[h], axis=1, keepdims=True))
            for p in range(2):
                dv_ref[pl.ds(off, t), p * PAIR:(p + 1) * PAIR] += dv_acc[p]
                dk_ref[pl.ds(off, t), p * PAIR:(p + 1) * PAIR] += dk_acc[p]
            return tuple(new_dq), tuple(new_left)

        carry = (tuple(jnp.zeros((tq, PAIR), F32) for _ in range(N_HEADS)),
                 tuple(jnp.zeros((tq, 1), F32) for _ in range(N_HEADS)))
        carry = lax.fori_loop(band * i - cnt_ref[i], band * i, lambda j, c: block(j, c, None), carry)
        for d in range(band):
            carry = block(band * i + d, carry, _sb_valid(d, tq, t))
        for p in range(2):
            dq_ref[:, p * PAIR:(p + 1) * PAIR] = _take_heads([carry[0][2 * p + e] * scale for e in range(2)], p)

    mspec = pl.BlockSpec((t, t), lambda i: (0, 0))
    acc_spec = pl.BlockSpec((s, GROUP_W), lambda i: (0, 0), pipeline_mode=pl.Buffered(1))
    return pl.pallas_call(
        kern, name=name, grid=(s // tq,),
        in_specs=pp.q_specs + pp.k_specs + pp.v_specs + [_row_spec(tq, do_blk, GROUP_W), _row_spec(tq, 0, rm.shape[1]), mspec, mspec,
                                                         pl.BlockSpec(memory_space=pltpu.SMEM)],
        out_specs=[_row_spec(tq, 0, GROUP_W), acc_spec, acc_spec],
        out_shape=[jax.ShapeDtypeStruct((s, GROUP_W), F32)] * 3,
        compiler_params=_att_params(False),
    )(*(pp.q_args + pp.k_args + pp.v_args + [do, rm, _tri(t, "row_gt_col"), _tri(t, "row_lt_col"), visited]))


def _split_w_in(w):
    col = lambda n: w[:, _OFF[n]:_OFF[n + 1]]
    wa = jnp.concatenate([col(0), col(1), col(2), col(4), col(5), col(6), col(10)], axis=1)
    misc = jnp.concatenate([col(3), col(9), jnp.zeros((w.shape[0], 128 - 4 - MLA_ROPE), w.dtype)], axis=1)
    wb = jnp.concatenate([col(11), col(7), col(8), misc], axis=1)
    return wa, wb


def _merge_dw_in(dwp):
    a = lambda n: dwp[:, n * GROUP_W:(n + 1) * GROUP_W]
    b0 = PA_COLS
    gate = dwp[:, b0:b0 + 1024]
    cq = dwp[:, b0 + 1024:b0 + 1280]
    ckv = dwp[:, b0 + 1280:b0 + 1408]
    flog = dwp[:, b0 + 1408:b0 + 1412]
    krot = dwp[:, b0 + 1408 + MISC_KROT:b0 + 1408 + MISC_KROT + MLA_ROPE]
    return jnp.concatenate([a(0), a(1), a(2), flog, a(3), a(4), a(5), cq, ckv, krot, a(6), gate], axis=1)


def _heads_first(w, per_head, first):
    r = w.shape[0]
    w3 = w.reshape(r, N_HEADS, per_head)
    return jnp.concatenate([w3[:, :, :first].reshape(r, -1), w3[:, :, first:].reshape(r, -1)], axis=1)


def _heads_interleaved(w, per_head, first):
    r = w.shape[0]
    a = w[:, :N_HEADS * first].reshape(r, N_HEADS, first)
    b = w[:, N_HEADS * first:].reshape(r, N_HEADS, per_head - first)
    return jnp.concatenate([a, b], axis=2).reshape(r, N_HEADS * per_head)


def _pad_rows8(a):
    return a[:, :8].T


def _local_step(x2, mem2, tgt, p):
    s = x2.shape[0]
    nm = mem2.shape[0]
    head_scale = HEAD_DIM ** -0.5
    mla_scale = (HEAD_DIM + MLA_ROPE) ** -0.5
    tabs = _rope_tables(s)
    mats = _rope_matrices()
    pairs = lambda arr, blk: [[(arr, blk + q, "pair")] for q in range(2)]
    vals = lambda arr, blk: [(arr, blk + q) for q in range(2)]

    h, hb = _ln_fwd(x2, None, p["ln_in_g"], p["ln_in_b"], "ln_in_fwd")
    _, memn_b = _ln_fwd(mem2, None, p["mem_ln_g"], p["mem_ln_b"], "ln_mem_fwd")

    saved = []
    for l in range(DEPTH):
        wa, wb = _split_w_in(p["w_in"][l])
        wp = jnp.concatenate([wa, wb], axis=1)
        wq_up = _heads_first(p["w_mla_q_up"][l], HEAD_DIM + MLA_ROPE, HEAD_DIM)
        wkv_up = _heads_first(p["w_mla_kv_up"][l], 2 * HEAD_DIM, HEAD_DIM)
        bias_row = jnp.pad(p["b_forget"][l], (0, 128 - N_HEADS)).reshape(1, 128)
        pa = _matmul(hb, wa, MXU_DTYPE, "proj_a")
        pb = _matmul(hb, wb, F32, "proj_b")

        fc = _forget_fwd(pb, bias_row)
        fbias = (fc, _pad_rows8(fc), _key_norm_max(pa, 2))
        o_fox, lse_fox = _softmax_fwd(pairs(pa, 0), pairs(pa, 2), vals(pa, 4), s, s, head_scale, True, fbias,
                                      "fox_fwd")
        o_sb, *rm_sb = _sb_fwd(pa, 6, 8, 10, s, head_scale, "sb_fwd")

        qfull, cqn = _mla_q_fwd(pb, p["mla_q_norm_g"][l], wq_up, tabs, mats)
        kv, ckvn, kr4 = _mla_kv_fwd(pb, p["mla_kv_norm_g"][l], wkv_up, tabs, mats)
        mla_q = [[(qfull, q, "pair"), (qfull, 2, "quad")] for q in range(2)]
        mla_k = [[(kv, q, "pair"), (kr4, 0, "quad")] for q in range(2)]
        o_mla, lse_mla = _softmax_fwd(mla_q, mla_k, vals(kv, 2), s, s, mla_scale, True, None, "mla_fwd")

        mkv = _matmul(memn_b, p["w_mem_kv"][l], MXU_DTYPE, "mem_kv")
        o_mem, lse_mem = _softmax_fwd(pairs(pa, 12), pairs(mkv, 0), vals(mkv, 2), s, nm, head_scale, False, None,
                                      "mem_fwd")

        groups = (o_fox, o_sb, o_mla, o_mem)
        gated = _gate_fwd(groups, pb)
        y = _matmul(gated, p["w_out"][l], F32, "out_proj")
        saved.append(dict(h=h, hb=hb, y=y, wp=wp, wq_up=wq_up, wkv_up=wkv_up, bias_row=bias_row, pa=pa, pb=pb,
                          fbias=fbias, lse_fox=lse_fox, rm_sb=rm_sb, cqn=cqn, ckvn=ckvn, mla_q=mla_q, mla_k=mla_k,
                          kv=kv, lse_mla=lse_mla, mkv=mkv, lse_mem=lse_mem, groups=groups, gated=gated))
        h, hb = _ln_fwd(h, y, p["ln_g"][l], p["ln_b"][l], "ln_fwd")

    dh, sq_cols = _loss_grad(h, tgt)
    loss_sum = jnp.sum(sq_cols)

    grads = {k: [None] * DEPTH for k in ("w_in", "b_forget", "mla_q_norm_g", "w_mla_q_up", "mla_kv_norm_g",
                                         "w_mla_kv_up", "w_mem_kv", "w_out", "ln_g", "ln_b")}
    dmemn = []
    dy1, dy2, c1 = dh, None, 1.0
    for l in reversed(range(DEPTH)):
        r = saved[l]
        pa, pb = r["pa"], r["pb"]
        o_fox, o_sb, o_mla, o_mem = r["groups"]
        du, du_b, dg, db = _ln_bwd(r["h"], r["y"], p["ln_g"][l], dy1, dy2, c1, "ln_bwd")
        grads["ln_g"][l], grads["ln_b"][l] = dg[0], db[0]
        dgated = _matmul(du_b, p["w_out"][l], F32, "out_proj_dx", "nt")
        grads["w_out"][l] = _matmul(r["gated"], du_b, F32, "out_proj_dw", "tn")
        dmixed, dgate_b = _gate_bwd(dgated, r["groups"], pb)

        dfq, dfk, dfv, dfc_q, dfc_k = _softmax_bwd(pairs(pa, 0), pairs(pa, 2), vals(pa, 4), o_fox, r["lse_fox"],
                                                   dmixed, 0, s, s, head_scale, True, r["fbias"], "fox_bwd")
        dmisc_f, dbf = _forget_bwd(pb, r["bias_row"], dfc_q + jnp.pad(dfc_k.T, ((0, 0), (0, 128 - 8))))
        grads["b_forget"][l] = dbf[0, :N_HEADS]

        dsq, dsk, dsv = _sb_bwd(pa, 6, 8, 10, dmixed, 1, *r["rm_sb"], s, head_scale, "sb_bwd")

        dqm, dkm, dvm = _softmax_bwd(r["mla_q"], r["mla_k"], vals(r["kv"], 2), o_mla, r["lse_mla"], dmixed, 2,
                                     s, s, mla_scale, True, None, "mla_bwd")
        dq_mla_b = _rope_q_bwd(dqm, tabs, mats)
        dcqn = _matmul(dq_mla_b, r["wq_up"], F32, "q_up_dx", "nt")
        grads["w_mla_q_up"][l] = _heads_interleaved(_matmul(r["cqn"], dq_mla_b, F32, "q_up_dw", "tn"),
                                                    HEAD_DIM + MLA_ROPE, HEAD_DIM)
        dcq_b, dgq = _rms_bwd(pb, MLA_Q_RANK, PB_CQ_BLK, p["mla_q_norm_g"][l], dcqn, "rms_q_bwd")
        grads["mla_q_norm_g"][l] = dgq[0]
        dkv_b = jnp.concatenate([dkm[:, :GROUP_W], dvm], axis=1).astype(MXU_DTYPE)
        dckvn = _matmul(dkv_b, r["wkv_up"], F32, "kv_up_dx", "nt")
        grads["w_mla_kv_up"][l] = _heads_interleaved(_matmul(r["ckvn"], dkv_b, F32, "kv_up_dw", "tn"),
                                                     2 * HEAD_DIM, HEAD_DIM)
        dckv_b, dgkv = _rms_bwd(pb, MLA_KV_RANK, PB_CKV_BLK, p["mla_kv_norm_g"][l], dckvn, "rms_kv_bwd")
        grads["mla_kv_norm_g"][l] = dgkv[0]
        dmisc_k = _rope_k_bwd(dkm[:, GROUP_W:], tabs, mats)

        dmq, dmk, dmv = _softmax_bwd(pairs(pa, 12), pairs(r["mkv"], 0), vals(r["mkv"], 2), o_mem, r["lse_mem"],
                                     dmixed, 3, s, nm, head_scale, False, None, "mem_bwd")
        dmkv_b = jnp.concatenate([dmk, dmv], axis=1).astype(MXU_DTYPE)
        grads["w_mem_kv"][l] = _matmul(memn_b, dmkv_b, F32, "mem_kv_dw", "tn")
        dmemn.append(_matmul(dmkv_b, p["w_mem_kv"][l], F32, "mem_kv_dx", "nt"))

        dp = [dfq, dfk, dfv, dsq, dsk, dsv, dmq, dgate_b, dcq_b, dckv_b, (dmisc_f, dmisc_k)]
        dhproj = _proj_dx(dp, r["wp"])
        grads["w_in"][l] = _merge_dw_in(_proj_dw(r["hb"], dp))
        dy1, dy2, c1 = du, dhproj, ALPHA

    dx, _, dg_in, db_in = _ln_bwd(x2, None, p["ln_in_g"], dy1, dy2, c1, "ln_in_bwd")
    _, _, dg_mem, db_mem = _ln_bwd(mem2, None, p["mem_ln_g"], dmemn[0], dmemn[1], 1.0, "ln_mem_bwd")
    out = {k: jnp.stack(v) for k, v in grads.items()}
    out.update(ln_in_g=dg_in[0], ln_in_b=db_in[0], mem_ln_g=dg_mem[0], mem_ln_b=db_mem[0])
    return loss_sum, dx, out


WIDE = "w_in"
FLAT_NAMES = ("w_out", "w_mem_kv", "w_mla_q_up", "w_mla_kv_up")
FLAT_ROWS = 896
BIG_NAMES = (WIDE,) + FLAT_NAMES
BIG_AXIS = dict(w_in=2, w_out=1, w_mem_kv=1, w_mla_q_up=2, w_mla_kv_up=2)
SMALL_NAMES = ("ln_in_g", "ln_in_b", "mem_ln_g", "mem_ln_b", "ln_g", "ln_b", "b_forget", "mla_q_norm_g",
               "mla_kv_norm_g")
ALL_NAMES = ("ln_in_g", "ln_in_b", "mem_ln_g", "mem_ln_b", "w_in", "b_forget", "mla_q_norm_g", "w_mla_q_up",
             "mla_kv_norm_g", "w_mla_kv_up", "w_mem_kv", "w_out", "ln_g", "ln_b")
N_CHIPS = 4
N_DEV = 8


def _rows_of(shape):
    rows = -(-int(np.prod(shape)) // LANES)
    return -(-rows // PACK_ALIGN) * PACK_ALIGN


def _pack(arrs, rows):
    parts = []
    for a in arrs:
        f = a.reshape(-1)
        n = _rows_of(a.shape) * LANES
        parts.append(jnp.pad(f, (0, n - f.shape[0])).reshape(-1, LANES))
    used = sum(q.shape[0] for q in parts)
    if rows > used:
        parts.append(jnp.zeros((rows - used, LANES), parts[0].dtype))
    return jnp.concatenate(parts, axis=0)


def _unpack(buf, shapes):
    out, r = [], 0
    for shp in shapes:
        n = _rows_of(shp)
        out.append(buf[r:r + n].reshape(-1)[:int(np.prod(shp))].reshape(shp))
        r += n
    return out


def _sharded_pair(get):
    wide = get(WIDE)
    return [wide.reshape(-1, wide.shape[-1]), _pack([get(n) for n in FLAT_NAMES], FLAT_ROWS)]


HBM_SPEC = pl.BlockSpec(memory_space=pltpu.HBM)


def _gather_weights(shards):
    n = len(shards)

    def body(*refs):
        w_refs, out_refs, (send_sems, recv_sems, local_sems) = refs[:n], refs[n:2 * n], refs[2 * n:]
        x, y, c = (lax.axis_index(a) for a in MESH_AXES)
        me, sibling = 2 * x + y, (x, y, 1 - c)
        chips = [(1 - x, y), (x, 1 - y), (1 - x, 1 - y)]
        local, first, passed = [], [], []
        for a in range(n):
            w_ref, out_ref, half = w_refs[a], out_refs[a], shards[a].shape[0] // 2

            def part(chip, core, out_ref=out_ref, half=half):
                return out_ref.at[chip, pl.ds(core * half, half)]

            def copy(k, src, dst, to, a=a):
                return pltpu.make_async_remote_copy(
                    src_ref=src, dst_ref=dst, send_sem=send_sems.at[6 * a + k], recv_sem=recv_sems.at[6 * a + k],
                    device_id=to, device_id_type=pl.DeviceIdType.MESH)

            local.append(pltpu.make_async_copy(w_ref, out_ref.at[me], local_sems.at[a]))
            local[-1].start()
            mine = [copy(k, w_ref.at[pl.ds(c * half, half)], part(me, c), (px, py, c))
                    for k, (px, py) in enumerate(chips)]
            for cp in mine:
                cp.start()
            first.append((mine, part, copy))
        for mine, part, copy in first:
            for k, (px, py) in enumerate(chips):
                copy(k, part(me, c), part(2 * px + py, c), (px, py, c)).wait_recv()
                passed.append(copy(3 + k, part(2 * px + py, c), part(2 * px + py, c), sibling))
                passed[-1].start()
        for mine, part, copy in first:
            for k, (px, py) in enumerate(chips):
                copy(3 + k, part(me, c), part(2 * px + py, 1 - c), sibling).wait_recv()
        for cp in [cp for mine, _, _ in first for cp in mine] + passed:
            cp.wait_send()
        for cp in local:
            cp.wait()

    return pl.pallas_call(
        body, name="gather_weights",
        out_shape=[jax.ShapeDtypeStruct((N_CHIPS,) + s.shape, s.dtype) for s in shards],
        in_specs=[HBM_SPEC] * n, out_specs=[HBM_SPEC] * n,
        scratch_shapes=[pltpu.SemaphoreType.DMA((6 * n,)), pltpu.SemaphoreType.DMA((6 * n,)),
                        pltpu.SemaphoreType.DMA((n,))],
    )(*shards)


def _exchange_grads(bigs, small):
    nb = len(bigs)
    halves = [b.shape[1] // 2 for b in bigs]
    chunks = [_pick(h, (128, 64, 32, 16)) for h in halves]

    def body(*refs):
        big_refs, small_ref = refs[:nb], refs[nb]
        sum_refs, out_refs = refs[nb + 1:2 * nb + 1], refs[2 * nb + 1:3 * nb + 2]
        scratch = refs[3 * nb + 2:]
        land_bufs, sum_bufs = scratch[:nb], scratch[nb:2 * nb]
        send_sems, recv_sems, local_sems, load_sem, swap_send, swap_recv, keep_sems = scratch[2 * nb:]
        x, y, c = (lax.axis_index(a) for a in MESH_AXES)
        me, my_chip = 4 * x + 2 * y + c, 2 * x + y
        flips = [(fx, fy, fc) for fx in (0, 1) for fy in (0, 1) for fc in (0, 1) if fx or fy or fc]
        peers = [(1 - x if fx else x, 1 - y if fy else y, 1 - c if fc else c) for fx, fy, fc in flips]

        sources = [lambda chip, core, r=big_refs[a], half=bigs[a].shape[1] // 2: r.at[chip, pl.ds(core * half, half)]
                   for a in range(nb)] + [lambda chip, core: small_ref]

        def copy(a, k, src, slot, to):
            return pltpu.make_async_remote_copy(
                src_ref=src, dst_ref=out_refs[a].at[slot], send_sem=send_sems.at[7 * a + k],
                recv_sem=recv_sems.at[7 * a + k], device_id=to, device_id_type=pl.DeviceIdType.MESH)

        own = [pltpu.make_async_copy(src(my_chip, c), out_refs[a].at[me], local_sems.at[a])
               for a, src in enumerate(sources)]
        for cp in own:
            cp.start()
        sends = [copy(a, k, src(2 * px + py, pc), me, (px, py, pc))
                 for a, src in enumerate(sources) for k, (px, py, pc) in enumerate(peers)]
        for cp in sends:
            cp.start()
        for a, src in enumerate(sources):
            for k, (px, py, pc) in enumerate(peers):
                copy(a, k, src(my_chip, c), 4 * px + 2 * py + pc, (px, py, pc)).wait_recv()
        for cp in sends:
            cp.wait_send()
        for cp in own:
            cp.wait()

        tails = []
        for a in range(nb):
            for r0 in range(0, halves[a], chunks[a]):
                load = pltpu.make_async_copy(out_refs[a].at[:, pl.ds(r0, chunks[a])], land_bufs[a], load_sem)
                load.start()
                load.wait()
                total = land_bufs[a][0].astype(F32)
                for d in range(1, N_DEV):
                    total = total + land_bufs[a][d].astype(F32)
                sum_bufs[a][pl.ds(r0, chunks[a]), :] = total
            keep = pltpu.make_async_copy(sum_bufs[a], sum_refs[a].at[c], keep_sems.at[a])
            give = pltpu.make_async_remote_copy(
                src_ref=sum_bufs[a], dst_ref=sum_refs[a].at[c], send_sem=swap_send.at[a], recv_sem=swap_recv.at[a],
                device_id=(x, y, 1 - c), device_id_type=pl.DeviceIdType.MESH)
            keep.start()
            give.start()
            tails.append((keep, give))
        for a, (keep, give) in enumerate(tails):
            pltpu.make_async_remote_copy(
                src_ref=sum_bufs[a], dst_ref=sum_refs[a].at[1 - c], send_sem=swap_send.at[a], recv_sem=swap_recv.at[a],
                device_id=(x, y, 1 - c), device_id_type=pl.DeviceIdType.MESH).wait_recv()
            give.wait_send()
            keep.wait()

    return pl.pallas_call(
        body, name="exchange_grads",
        out_shape=[jax.ShapeDtypeStruct((2, h, b.shape[2]), F32) for h, b in zip(halves, bigs)]
        + [jax.ShapeDtypeStruct((N_DEV, h, b.shape[2]), b.dtype) for h, b in zip(halves, bigs)]
        + [jax.ShapeDtypeStruct((N_DEV,) + small.shape, small.dtype)],
        in_specs=[HBM_SPEC] * (nb + 1), out_specs=[HBM_SPEC] * (2 * nb + 1),
        scratch_shapes=[pltpu.VMEM((N_DEV, ch, b.shape[2]), b.dtype) for ch, b in zip(chunks, bigs)]
        + [pltpu.VMEM((h, b.shape[2]), F32) for h, b in zip(halves, bigs)]
        + [pltpu.SemaphoreType.DMA((7 * (nb + 1),)), pltpu.SemaphoreType.DMA((7 * (nb + 1),)),
           pltpu.SemaphoreType.DMA((nb + 1,)), pltpu.SemaphoreType.DMA, pltpu.SemaphoreType.DMA((nb,)),
           pltpu.SemaphoreType.DMA((nb,)), pltpu.SemaphoreType.DMA((nb,))],
        compiler_params=pltpu.CompilerParams(vmem_limit_bytes=VMEM_LIMIT),
    )(*bigs, small)


def _adamw(parts, w, m, v, name):
    rows, width = w.shape
    n_parts = parts.shape[0]
    tile = _pick(rows, (128, 16, 8))
    bc1 = 1.0 - ADAM_B1 ** ADAM_STEP
    bc2 = 1.0 - ADAM_B2 ** ADAM_STEP

    def kern(p_ref, w_ref, m_ref, v_ref, g_ref, d_ref, nm_ref, nv_ref):
        g = p_ref[0].astype(F32)
        for d in range(1, n_parts):
            g = g + p_ref[d].astype(F32)
        nm = ADAM_B1 * m_ref[...] + (1.0 - ADAM_B1) * g
        nv = ADAM_B2 * v_ref[...] + (1.0 - ADAM_B2) * (g * g)
        g_ref[...] = g
        nm_ref[...] = nm
        nv_ref[...] = nv
        d_ref[...] = -ADAM_LR * ((nm / bc1) / (jnp.sqrt(nv / bc2) + ADAM_EPS) + ADAM_WD * w_ref[...])

    spec = pl.BlockSpec((tile, width), lambda i: (i, 0))
    return pl.pallas_call(
        kern, name=name, grid=(rows // tile,),
        in_specs=[pl.BlockSpec((n_parts, tile, width), lambda i: (0, i, 0)), spec, spec, spec],
        out_specs=[spec] * 4, out_shape=[jax.ShapeDtypeStruct((rows, width), F32)] * 4,
        compiler_params=pltpu.CompilerParams(dimension_semantics=("parallel",), vmem_limit_bytes=VMEM_LIMIT),
    )(parts, w, m, v)


def kernel(x, mem, ln_in_g, ln_in_b, mem_ln_g, mem_ln_b, w_in, b_forget, mla_q_norm_g, w_mla_q_up, mla_kv_norm_g, w_mla_kv_up, w_mem_kv, w_out, ln_g, ln_b, loss_target, m_ln_in_g, m_ln_in_b, m_mem_ln_g, m_mem_ln_b, m_w_in, m_b_forget, m_mla_q_norm_g, m_w_mla_q_up, m_mla_kv_norm_g, m_w_mla_kv_up, m_w_mem_kv, m_w_out, m_ln_g, m_ln_b, v_ln_in_g, v_ln_in_b, v_mem_ln_g, v_mem_ln_b, v_w_in, v_b_forget, v_mla_q_norm_g, v_w_mla_q_up, v_mla_kv_norm_g, v_w_mla_kv_up, v_w_mem_kv, v_w_out, v_ln_g, v_ln_b):
    w = dict(ln_in_g=ln_in_g, ln_in_b=ln_in_b, mem_ln_g=mem_ln_g, mem_ln_b=mem_ln_b, w_in=w_in, b_forget=b_forget,
             mla_q_norm_g=mla_q_norm_g, w_mla_q_up=w_mla_q_up, mla_kv_norm_g=mla_kv_norm_g,
             w_mla_kv_up=w_mla_kv_up, w_mem_kv=w_mem_kv, w_out=w_out, ln_g=ln_g, ln_b=ln_b)
    mo = dict(ln_in_g=m_ln_in_g, ln_in_b=m_ln_in_b, mem_ln_g=m_mem_ln_g, mem_ln_b=m_mem_ln_b, w_in=m_w_in,
              b_forget=m_b_forget, mla_q_norm_g=m_mla_q_norm_g, w_mla_q_up=m_w_mla_q_up,
              mla_kv_norm_g=m_mla_kv_norm_g, w_mla_kv_up=m_w_mla_kv_up, w_mem_kv=m_w_mem_kv, w_out=m_w_out,
              ln_g=m_ln_g, ln_b=m_ln_b)
    vo = dict(ln_in_g=v_ln_in_g, ln_in_b=v_ln_in_b, mem_ln_g=v_mem_ln_g, mem_ln_b=v_mem_ln_b, w_in=v_w_in,
              b_forget=v_b_forget, mla_q_norm_g=v_mla_q_norm_g, w_mla_q_up=v_w_mla_q_up,
              mla_kv_norm_g=v_mla_kv_norm_g, w_mla_kv_up=v_w_mla_kv_up, w_mem_kv=v_w_mem_kv, w_out=v_w_out,
              ln_g=v_ln_g, ln_b=v_ln_b)
    flat_shapes = [w[n].shape for n in FLAT_NAMES]
    small_shapes = [w[n].shape for n in SMALL_NAMES]

    got_wide, got_flat = _gather_weights(_sharded_pair(lambda n: w[n].astype(MXU_DTYPE)))
    full = dict(w)
    full[WIDE] = jnp.concatenate([got_wide[j] for j in range(N_CHIPS)], axis=1).reshape(
        w[WIDE].shape[:2] + (N_CHIPS * w[WIDE].shape[2],))
    per_chip = [_unpack(got_flat[j], flat_shapes) for j in range(N_CHIPS)]
    for idx, n in enumerate(FLAT_NAMES):
        full[n] = jnp.concatenate([per_chip[j][idx] for j in range(N_CHIPS)], axis=BIG_AXIS[n])

    loss_sum, dx, g = _local_step(x[0], mem[0], loss_target[0], full)
    loss = lax.psum(loss_sum * (0.5 / D_MODEL), MESH_AXES)

    def shard_of(n, j):
        ax, size = BIG_AXIS[n], w[n].shape[BIG_AXIS[n]]
        return lax.slice_in_dim(g[n], j * size, (j + 1) * size, axis=ax).astype(MXU_DTYPE)

    per_dest = [_sharded_pair(lambda n, j=j: shard_of(n, j)) for j in range(N_CHIPS)]
    bigs = [jnp.stack([per_dest[j][a] for j in range(N_CHIPS)]) for a in range(2)]
    exchanged = _exchange_grads(bigs, _pack([g[n] for n in SMALL_NAMES], SMALL_ROWS))
    halves, small_parts = exchanged[:2], exchanged[-1]

    res = []
    for a, (grad, nm) in enumerate(zip(halves, ("adamw_wide", "adamw_flat"))):
        state = [_sharded_pair(lambda n, src=src: src[n])[a] for src in (w, mo, vo)]
        res.append(_adamw(grad.reshape((1,) + state[0].shape), *state, nm))
    res_small = _adamw(small_parts, *[_pack([src[n] for n in SMALL_NAMES], SMALL_ROWS) for src in (w, mo, vo)],
                       "adamw_replicated")
    outs = []
    for kind in range(4):
        vals = {WIDE: res[0][kind].reshape(w[WIDE].shape)}
        vals.update(zip(FLAT_NAMES, _unpack(res[1][kind], flat_shapes)))
        vals.update(zip(SMALL_NAMES, _unpack(res_small[kind], small_shapes)))
        outs += [vals[n] for n in ALL_NAMES]
    return (loss, dx[None], *outs)
```

```python
import functools

import numpy as np
import jax
import jax.numpy as jnp
from jax import lax
from jax.experimental import pallas as pl
from jax.experimental.pallas import tpu as pltpu

F32 = jnp.float32
MXU_DTYPE = jnp.bfloat16

DEPTH = 2
D_MODEL = 1024
GROUP_W = 256
N_HEADS = 4
HEAD_DIM = 64
MLA_Q_RANK = 256
MLA_KV_RANK = 128
MLA_ROPE = 32
MLA_Q_COLS = N_HEADS * (HEAD_DIM + MLA_ROPE)
MLA_KV_COLS = N_HEADS * 2 * HEAD_DIM
ROPE_THETA = 10000.0
LN_EPS = 1e-5
RMS_EPS = 1e-6
ALPHA = (2 * DEPTH) ** 0.25
ADAM_LR, ADAM_B1, ADAM_B2, ADAM_EPS, ADAM_WD, ADAM_STEP = 0.001, 0.9, 0.999, 1e-08, 0.01, 10

_SPLIT = (256, 256, 256, 4, 256, 256, 256, 256, 128, 32, 256, 1024)
_OFF = [int(o) for o in np.cumsum((0,) + _SPLIT)]
IN_COLS = _OFF[-1]
PA_COLS = 7 * GROUP_W
PB_COLS = 1024 + 256 + 128 + 128
PB_CQ_BLK, PB_CKV_BLK, PB_MISC_BLK = 4, 10, 11
MISC_KROT = 4

LANES = 1024
PACK_ALIGN = 16
SMALL_ROWS = 144
ROW_TILE = 256
PROJ_BWD_ROWS = 512
ATT_TILE = 256
SB_QUERY_TILE = 256
SOFTMAX_TILE = 512
FWD_PAIRS_PER_LOOP = 1
PAIR = 128
SB_SLOT = PAIR // N_HEADS
VMEM_LIMIT = 56 * 1024 * 1024
MATMUL_VMEM = 30 * 1024 * 1024
NEG = -1e30
LOG2E = 1.4426950408889634
LN2 = 0.6931471805599453
EXP_UNDERFLOW = -104.0
DEAD_LOGIT = -110.0
REACH_SLACK = 1.0 + 2.0 ** -10
MESH_AXES = ("x", "y", "c")


def _dot(a, b):
    return jnp.dot(a, b, preferred_element_type=F32)


def _dot_nt(a, b):
    return lax.dot_general(a, b, (((1,), (1,)), ((), ())), preferred_element_type=F32)


def _dot_tn(a, b):
    return lax.dot_general(a, b, (((0,), (0,)), ((), ())), preferred_element_type=F32)


def _split2(x):
    hi = x.astype(MXU_DTYPE)
    lo = (x - hi.astype(F32)).astype(MXU_DTYPE)
    return hi, lo


def _split3(x):
    hi = x.astype(MXU_DTYPE)
    r = x - hi.astype(F32)
    mid = r.astype(MXU_DTYPE)
    lo = (r - mid.astype(F32)).astype(MXU_DTYPE)
    return hi, mid, lo


def _dot_exact_r(x, pm):
    hi, mid, lo = _split3(x)
    return _dot(hi, pm) + _dot(mid, pm) + _dot(lo, pm)


def _dot_exact_l(pm, x):
    hi, mid, lo = _split3(x)
    return _dot(pm, hi) + _dot(pm, mid) + _dot(pm, lo)


def _pick(dim, prefs):
    for p in prefs:
        if dim % p == 0:
            return p
    return dim


def _softplus(z):
    return jnp.maximum(z, 0.0) + jnp.log(1.0 + jnp.exp(-jnp.abs(z)))


def _tile_options(dim):
    opts = [d for d in range(128, min(dim, 2048) + 1, 128) if dim % d == 0]
    return opts or [dim]


def _matmul_tiles(m, n, k, out_bytes):
    tk = k if k <= 4096 else _pick(k, (1024, 512, 256, 128))
    best = None
    for tm in _tile_options(m):
        for tn in _tile_options(n):
            vmem = 2 * 2 * (tm * tk + tk * tn) + 4 * tm * tn + 2 * out_bytes * tm * tn
            if vmem <= MATMUL_VMEM and (best is None or tm * tn / (tm + tn) > best[0]):
                best = (tm * tn / (tm + tn), tm, tn)
    return best[1], best[2], tk


def _matmul(a, b, out_dtype, name, mode="nn"):
    m, k = (a.shape[1], a.shape[0]) if mode == "tn" else a.shape
    n = b.shape[0] if mode == "nt" else b.shape[1]
    tm, tn, tk = _matmul_tiles(m, n, k, jnp.dtype(out_dtype).itemsize)
    nk = k // tk
    dot = {"nn": _dot, "tn": _dot_tn, "nt": _dot_nt}[mode]

    def kern(a_ref, b_ref, o_ref, *acc):
        if nk == 1:
            o_ref[...] = dot(a_ref[...], b_ref[...]).astype(o_ref.dtype)
            return
        acc_ref, = acc
        kk = pl.program_id(2)

        @pl.when(kk == 0)
        def _():
            acc_ref[...] = jnp.zeros_like(acc_ref)

        acc_ref[...] += dot(a_ref[...], b_ref[...])

        @pl.when(kk == nk - 1)
        def _():
            o_ref[...] = acc_ref[...].astype(o_ref.dtype)

    a_spec = (pl.BlockSpec((tk, tm), lambda i, j, kk: (kk, i)) if mode == "tn"
              else pl.BlockSpec((tm, tk), lambda i, j, kk: (i, kk)))
    b_spec = (pl.BlockSpec((tn, tk), lambda i, j, kk: (j, kk)) if mode == "nt"
              else pl.BlockSpec((tk, tn), lambda i, j, kk: (kk, j)))
    return pl.pallas_call(
        kern, name=name, grid=(m // tm, n // tn, nk), in_specs=[a_spec, b_spec],
        out_specs=pl.BlockSpec((tm, tn), lambda i, j, kk: (i, j)),
        out_shape=jax.ShapeDtypeStruct((m, n), out_dtype),
        scratch_shapes=[pltpu.VMEM((tm, tn), F32)] if nk > 1 else [],
        compiler_params=pltpu.CompilerParams(
            dimension_semantics=("parallel", "parallel", "arbitrary"), vmem_limit_bytes=VMEM_LIMIT),
    )(a.astype(MXU_DTYPE), b.astype(MXU_DTYPE))


def _piece_arrays(pieces):
    return [a for p in pieces for a in (p if isinstance(p, tuple) else (p,))]


def _join_pieces(refs, pieces):
    refs, cols = list(refs), []
    for p in pieces:
        vals = [refs.pop(0)[...] for _ in (p if isinstance(p, tuple) else (p,))]
        cols.append(functools.reduce(jnp.add, vals).astype(MXU_DTYPE))
    return jnp.concatenate(cols, axis=1)


def _proj_dx(pieces, w):
    arrs = _piece_arrays(pieces)
    rows, n = arrs[0].shape[0], w.shape[0]
    tm = min(PROJ_BWD_ROWS, rows)

    def kern(*refs):
        refs[-1][...] = _dot_nt(_join_pieces(refs[:len(arrs)], pieces), refs[len(arrs)][...])

    return pl.pallas_call(
        kern, name="proj_dx", grid=(rows // tm,),
        in_specs=[pl.BlockSpec((tm, a.shape[1]), lambda i: (i, 0)) for a in arrs]
        + [pl.BlockSpec(w.shape, lambda i: (0, 0), pipeline_mode=pl.Buffered(1))],
        out_specs=pl.BlockSpec((tm, n), lambda i: (i, 0)), out_shape=jax.ShapeDtypeStruct((rows, n), F32),
        compiler_params=pltpu.CompilerParams(dimension_semantics=("parallel",), vmem_limit_bytes=VMEM_LIMIT),
    )(*arrs, w.astype(MXU_DTYPE))


def _proj_dw(a, pieces):
    arrs = _piece_arrays(pieces)
    rows, m = a.shape
    k = sum(x.shape[1] for x in (p[0] if isinstance(p, tuple) else p for p in pieces))
    tk = min(PROJ_BWD_ROWS, rows)

    def kern(*refs):
        o_ref = refs[-1]

        @pl.when(pl.program_id(0) == 0)
        def _():
            o_ref[...] = jnp.zeros_like(o_ref)

        o_ref[...] += _dot_tn(refs[0][...], _join_pieces(refs[1:1 + len(arrs)], pieces))

    return pl.pallas_call(
        kern, name="proj_dw", grid=(rows // tk,),
        in_specs=[pl.BlockSpec((tk, m), lambda i: (i, 0))]
        + [pl.BlockSpec((tk, x.shape[1]), lambda i: (i, 0)) for x in arrs],
        out_specs=pl.BlockSpec((m, k), lambda i: (0, 0), pipeline_mode=pl.Buffered(1)),
        out_shape=jax.ShapeDtypeStruct((m, k), F32),
        compiler_params=pltpu.CompilerParams(dimension_semantics=("arbitrary",), vmem_limit_bytes=VMEM_LIMIT),
    )(a, *arrs)


def _rowwise(body, name, rows, tile, row_ins, full_ins, row_outs, acc_outs=(), scratch=(),
             reverse=False, sequential=False):
    n = rows // tile

    def ridx(i):
        return (n - 1 - i) if reverse else i

    in_specs, args = [], []
    for arr, width, cb in row_ins:
        in_specs.append(pl.BlockSpec((tile, width), lambda i, cb=cb: (ridx(i), cb)))
        args.append(arr)
    for arr in full_ins:
        in_specs.append(pl.BlockSpec(arr.shape, lambda i, nd=arr.ndim: (0,) * nd))
        args.append(arr)
    out_shape = [jax.ShapeDtypeStruct((rows, w), dt) for w, dt in row_outs]
    out_shape += [jax.ShapeDtypeStruct(s, dt) for s, dt in acc_outs]
    out_specs = [pl.BlockSpec((tile, w), lambda i: (ridx(i), 0)) for w, dt in row_outs]
    out_specs += [pl.BlockSpec(s, lambda i, nd=len(s): (0,) * nd) for s, dt in acc_outs]

    def kern(*refs):
        body(pl.program_id(0), *refs)

    sem = "arbitrary" if (acc_outs or sequential) else "parallel"
    return pl.pallas_call(
        kern, name=name, grid=(n,), in_specs=in_specs, out_specs=out_specs, out_shape=out_shape,
        scratch_shapes=list(scratch),
        compiler_params=pltpu.CompilerParams(dimension_semantics=(sem,), vmem_limit_bytes=VMEM_LIMIT),
    )(*args)


def _ln_stats(u):
    mu = jnp.mean(u, axis=-1, keepdims=True)
    xc = u - mu
    var = jnp.mean(xc * xc, axis=-1, keepdims=True)
    return xc, lax.rsqrt(var + LN_EPS)


def _ln_fwd(a, b, g, beta, name):
    rows, d = a.shape
    has_b = b is not None

    def body(i, *refs):
        if has_b:
            a_ref, b_ref, g_ref, be_ref, h_ref, hb_ref = refs
            u = ALPHA * a_ref[...] + b_ref[...]
        else:
            a_ref, g_ref, be_ref, h_ref, hb_ref = refs
            u = a_ref[...]
        xc, rstd = _ln_stats(u)
        y = xc * rstd * g_ref[...] + be_ref[...]
        h_ref[...] = y
        hb_ref[...] = y.astype(hb_ref.dtype)

    row_ins = [(a, d, 0)] + ([(b, d, 0)] if has_b else [])
    return _rowwise(body, name, rows, min(ROW_TILE, rows), row_ins,
                    [g.reshape(1, d), beta.reshape(1, d)], [(d, F32), (d, MXU_DTYPE)])


def _ln_bwd(a, b, g, dy1, dy2, c1, name):
    rows, d = a.shape
    has_b = b is not None
    has_2 = dy2 is not None

    def body(i, *refs):
        refs = list(refs)
        a_ref = refs.pop(0)
        u = a_ref[...]
        if has_b:
            u = ALPHA * u + refs.pop(0)[...]
        dy = c1 * refs.pop(0)[...]
        if has_2:
            dy = dy + refs.pop(0)[...]
        g_ref, du_ref, dub_ref, dg_ref, db_ref = refs

        @pl.when(i == 0)
        def _():
            dg_ref[...] = jnp.zeros_like(dg_ref)
            db_ref[...] = jnp.zeros_like(db_ref)

        xc, rstd = _ln_stats(u)
        xhat = xc * rstd
        dxh = dy * g_ref[...]
        m1 = jnp.mean(dxh, axis=-1, keepdims=True)
        m2 = jnp.mean(dxh * xhat, axis=-1, keepdims=True)
        du = rstd * (dxh - m1 - xhat * m2)
        du_ref[...] = du
        dub_ref[...] = du.astype(dub_ref.dtype)
        dg_ref[...] += jnp.sum(dy * xhat, axis=0, keepdims=True)
        db_ref[...] += jnp.sum(dy, axis=0, keepdims=True)

    row_ins = [(a, d, 0)] + ([(b, d, 0)] if has_b else []) + [(dy1, d, 0)] + ([(dy2, d, 0)] if has_2 else [])
    return _rowwise(body, name, rows, min(ROW_TILE, rows), row_ins, [g.reshape(1, d)],
                    [(d, F32), (d, MXU_DTYPE)], [((1, d), F32), ((1, d), F32)])


def _out_proj_ln(gated, w_out, h, g, beta, target=None):
    rows, d = h.shape
    last = target is not None

    def body(i, *refs):
        if last:
            a_ref, h_ref, t_ref, w_ref, g_ref, be_ref, u_ref, dh_ref, acc_ref = refs

            @pl.when(i == 0)
            def _():
                acc_ref[...] = jnp.zeros_like(acc_ref)
        else:
            a_ref, h_ref, w_ref, g_ref, be_ref, u_ref, o_ref, ob_ref = refs
        u = ALPHA * h_ref[...] + _dot(a_ref[...], w_ref[...])
        u_ref[...] = u
        xc, rstd = _ln_stats(u)
        y = xc * rstd * g_ref[...] + be_ref[...]
        if last:
            e = y - t_ref[...]
            dh_ref[...] = e * (1.0 / d)
            acc_ref[...] += jnp.sum(e * e, axis=0, keepdims=True)
        else:
            o_ref[...] = y
            ob_ref[...] = y.astype(ob_ref.dtype)

    row_ins = [(gated, gated.shape[1], 0), (h, d, 0)] + ([(target, d, 0)] if last else [])
    full_ins = [w_out.astype(MXU_DTYPE), g.reshape(1, d), beta.reshape(1, d)]
    if last:
        return _rowwise(body, "out_proj_ln_loss", rows, PROJ_BWD_ROWS, row_ins, full_ins, [(d, F32), (d, F32)],
                        [((1, d), F32)])
    return _rowwise(body, "out_proj_ln", rows, PROJ_BWD_ROWS, row_ins, full_ins,
                    [(d, F32), (d, F32), (d, MXU_DTYPE)])


def _gate_fwd(groups, pb):
    rows = pb.shape[0]
    w = GROUP_W * len(groups)

    def body(i, *refs):
        g = refs[4][...]
        mixed = jnp.concatenate([r[...] for r in refs[:4]], axis=1)
        refs[5][...] = (mixed * (g / (1.0 + jnp.exp(-g)))).astype(refs[5].dtype)

    return _rowwise(body, "gate_fwd", rows, ROW_TILE, [(o, GROUP_W, 0) for o in groups] + [(pb, w, 0)], [],
                    [(w, MXU_DTYPE)])[0]


def _out_proj_gate_bwd(du_b, w_out, groups, pb):
    rows = pb.shape[0]
    w = GROUP_W * len(groups)

    def body(i, *refs):
        dg = _dot_nt(refs[0][...], refs[6][...])
        mixed = jnp.concatenate([r[...] for r in refs[1:5]], axis=1)
        g = refs[5][...]
        dm_ref, dgate_ref = refs[7], refs[8]
        sig = 1.0 / (1.0 + jnp.exp(-g))
        dm_ref[...] = dg * (g * sig)
        dgate_ref[...] = (dg * mixed * (sig * (1.0 + g * (1.0 - sig)))).astype(dgate_ref.dtype)

    return _rowwise(body, "out_proj_gate_bwd", rows, PROJ_BWD_ROWS,
                    [(du_b, du_b.shape[1], 0)] + [(o, GROUP_W, 0) for o in groups] + [(pb, w, 0)],
                    [w_out.astype(MXU_DTYPE)], [(w, F32), (w, MXU_DTYPE)])


def _tri(n, kind):
    r = np.arange(n)[:, None]
    c = np.arange(n)[None, :]
    m = {"lower_incl": r >= c, "upper_incl": r <= c, "row_gt_col": r > c, "row_lt_col": r < c}[kind]
    return jnp.asarray(m.astype(np.float32), dtype=MXU_DTYPE)


def _forget_fwd(pb, bias_row):
    rows = pb.shape[0]
    tile = _pick(rows, (1024, 512, 256))

    def body(i, x_ref, b_ref, l_ref, o_ref, carry_ref):
        @pl.when(i == 0)
        def _():
            carry_ref[...] = jnp.zeros_like(carry_ref)

        xx = x_ref[...] + b_ref[...]
        lane = lax.broadcasted_iota(jnp.int32, xx.shape, 1)
        lf = jnp.where(lane < N_HEADS, -_softplus(-xx), 0.0)
        o_ref[...] = _dot_exact_l(l_ref[...], lf) + carry_ref[...]
        carry_ref[...] += jnp.sum(lf, axis=0, keepdims=True)

    return _rowwise(body, "forget_fwd", rows, tile, [(pb, 128, PB_MISC_BLK)],
                    [bias_row, _tri(tile, "lower_incl")], [(128, F32)],
                    scratch=[pltpu.VMEM((1, 128), F32)], sequential=True)[0]


def _forget_bwd(pb, bias_row, dfc):
    rows = pb.shape[0]
    tile = _pick(rows, (1024, 512, 256))

    def body(i, x_ref, df_ref, b_ref, u_ref, o_ref, db_ref, carry_ref):
        @pl.when(i == 0)
        def _():
            carry_ref[...] = jnp.zeros_like(carry_ref)
            db_ref[...] = jnp.zeros_like(db_ref)

        df = df_ref[...]
        sfx = _dot_exact_l(u_ref[...], df) + carry_ref[...]
        carry_ref[...] += jnp.sum(df, axis=0, keepdims=True)
        xx = x_ref[...] + b_ref[...]
        lane = lax.broadcasted_iota(jnp.int32, xx.shape, 1)
        dl = jnp.where(lane < N_HEADS, sfx / (1.0 + jnp.exp(xx)), 0.0)
        o_ref[...] = dl
        db_ref[...] += jnp.sum(dl, axis=0, keepdims=True)

    return _rowwise(body, "forget_bwd", rows, tile,
                    [(pb, 128, PB_MISC_BLK), (dfc, 128, 0)],
                    [bias_row, _tri(tile, "upper_incl")], [(128, F32)], [((1, 128), F32)],
                    scratch=[pltpu.VMEM((1, 128), F32)], reverse=True, sequential=True)


def _rope_tables(s):
    half = MLA_ROPE // 2
    inv_freq = ROPE_THETA ** (-jnp.arange(half, dtype=F32) / half)
    ang = jnp.arange(s).astype(F32)[:, None] * inv_freq[None, :]
    cos2 = jnp.tile(jnp.cos(ang), (1, 2))
    sin2 = jnp.tile(jnp.sin(ang), (1, 2))
    cx = jnp.tile(cos2, (1, N_HEADS))
    sx = jnp.tile(sin2, (1, N_HEADS))
    pad = ((0, 0), (MISC_KROT, 128 - MISC_KROT - MLA_ROPE))
    ck = jnp.pad(cos2, pad)
    sk = jnp.pad(sin2, pad)
    return dict(ck=ck, sk=sk, cx=cx, sx=sx)


def _rot_matrix(width, bases):
    half = MLA_ROPE // 2
    p = np.zeros((width, width), np.float32)
    for b in bases:
        for i in range(half):
            p[b + half + i, b + i] = -1.0
            p[b + i, b + half + i] = 1.0
    return p


def _rope_matrices():
    pk = _rot_matrix(128, [MISC_KROT])
    p4 = _rot_matrix(128, [h * MLA_ROPE for h in range(N_HEADS)])
    a = np.zeros((128, 128), np.float32)
    for h in range(N_HEADS):
        for r in range(MLA_ROPE):
            a[h * MLA_ROPE + r, MISC_KROT + r] = 1.0
    cast = lambda m: jnp.asarray(m, dtype=MXU_DTYPE)
    return dict(p4=cast(p4), p4t=cast(p4.T), pk=cast(pk), spread=cast(a.T), xa=cast(a), xb=cast(p4.T @ a))


def _rms(c, g):
    r = lax.rsqrt(jnp.mean(c * c, axis=-1, keepdims=True) + RMS_EPS)
    return c * r * g


def _mla_q_fwd(pb, g, w_up, tabs, mats):
    rows = pb.shape[0]

    def body(i, c_ref, cos_ref, sin_ref, g_ref, w_ref, p_ref, q_ref, cn_ref):
        cn = _rms(c_ref[...], g_ref[...]).astype(cn_ref.dtype)
        cn_ref[...] = cn
        q = _dot(cn, w_ref[...])
        qr = q[:, GROUP_W:]
        q_ref[:, :GROUP_W] = q[:, :GROUP_W].astype(q_ref.dtype)
        q_ref[:, GROUP_W:] = (qr * cos_ref[...] + _dot_exact_r(qr, p_ref[...]) * sin_ref[...]).astype(q_ref.dtype)

    return _rowwise(body, "mla_q_fwd", rows, ROW_TILE,
                    [(pb, MLA_Q_RANK, PB_CQ_BLK), (tabs["cx"], PAIR, 0), (tabs["sx"], PAIR, 0)],
                    [g.reshape(1, MLA_Q_RANK), w_up.astype(MXU_DTYPE), mats["p4"]],
                    [(MLA_Q_COLS, MXU_DTYPE), (MLA_Q_RANK, MXU_DTYPE)])


def _mla_kv_fwd(pb, g, w_up, tabs, mats):
    rows = pb.shape[0]

    def body(i, c_ref, x_ref, cos_ref, sin_ref, g_ref, w_ref, p_ref, sp_ref, kv_ref, cn_ref, kr_ref):
        cn = _rms(c_ref[...], g_ref[...]).astype(cn_ref.dtype)
        cn_ref[...] = cn
        kv_ref[...] = _dot(cn, w_ref[...]).astype(kv_ref.dtype)
        xx = x_ref[...]
        kr = xx * cos_ref[...] + _dot_exact_r(xx, p_ref[...]) * sin_ref[...]
        kr_ref[...] = _dot_exact_r(kr, sp_ref[...]).astype(kr_ref.dtype)

    return _rowwise(body, "mla_kv_fwd", rows, ROW_TILE,
                    [(pb, MLA_KV_RANK, PB_CKV_BLK), (pb, 128, PB_MISC_BLK), (tabs["ck"], 128, 0), (tabs["sk"], 128, 0)],
                    [g.reshape(1, MLA_KV_RANK), w_up.astype(MXU_DTYPE), mats["pk"], mats["spread"]],
                    [(MLA_KV_COLS, MXU_DTYPE), (MLA_KV_RANK, MXU_DTYPE), (128, MXU_DTYPE)])


def _rope_q_bwd(dq_full, tabs, mats):
    rows, nq = dq_full.shape

    def body(i, d_ref, cos_ref, sin_ref, pt_ref, o_ref):
        dr = d_ref[:, GROUP_W:]
        o_ref[:, :GROUP_W] = d_ref[:, :GROUP_W].astype(o_ref.dtype)
        o_ref[:, GROUP_W:] = (dr * cos_ref[...] + _dot_exact_r(dr * sin_ref[...], pt_ref[...])).astype(o_ref.dtype)

    return _rowwise(body, "rope_q_bwd", rows, ROW_TILE,
                    [(dq_full, nq, 0), (tabs["cx"], PAIR, 0), (tabs["sx"], PAIR, 0)], [mats["p4t"]],
                    [(nq, MXU_DTYPE)])[0]


def _rope_k_bwd(dkr, tabs, mats):
    rows = dkr.shape[0]

    def body(i, d_ref, cos_ref, sin_ref, a_ref, b_ref, o_ref):
        d = d_ref[...]
        o_ref[...] = _dot_exact_r(d * cos_ref[...], a_ref[...]) + _dot_exact_r(d * sin_ref[...], b_ref[...])

    return _rowwise(body, "rope_k_bwd", rows, ROW_TILE,
                    [(dkr, 128, 0), (tabs["cx"], 128, 0), (tabs["sx"], 128, 0)], [mats["xa"], mats["xb"]],
                    [(128, F32)])[0]


def _rms_bwd(pb, width, col_blk, g, dy, name):
    rows = pb.shape[0]

    def body(i, c_ref, dy_ref, g_ref, dc_ref, dg_ref):
        @pl.when(i == 0)
        def _():
            dg_ref[...] = jnp.zeros_like(dg_ref)

        c = c_ref[...]
        dy = dy_ref[...]
        r = lax.rsqrt(jnp.mean(c * c, axis=-1, keepdims=True) + RMS_EPS)
        dyg = dy * g_ref[...]
        dc = r * dyg - c * (r * r * r) * jnp.mean(c * dyg, axis=-1, keepdims=True)
        dc_ref[...] = dc.astype(dc_ref.dtype)
        dg_ref[...] += jnp.sum(dy * c * r, axis=0, keepdims=True)

    return _rowwise(body, name, rows, ROW_TILE, [(pb, width, col_blk), (dy, width, 0)], [g.reshape(1, width)],
                    [(width, MXU_DTYPE)], [((1, width), F32)])


def _att_params(parallel):
    return pltpu.CompilerParams(dimension_semantics=("parallel" if parallel else "arbitrary",),
                                vmem_limit_bytes=VMEM_LIMIT)


def _blk_off(j, t):
    return j * t if isinstance(j, int) else pl.multiple_of(j * t, t)


def _causal_mask(t, strict):
    r = lax.broadcasted_iota(jnp.int32, (t, t), 0)
    c = lax.broadcasted_iota(jnp.int32, (t, t), 1)
    return (c < r) if strict else (c <= r)


def _lane_mask(kind, head, rows):
    lane = lax.broadcasted_iota(jnp.int32, (rows, PAIR), 1)
    if kind == "pair":
        return (lane < HEAD_DIM) if head % 2 == 0 else (lane >= HEAD_DIM)
    return (lane >= MLA_ROPE * head) & (lane < MLA_ROPE * (head + 1))


def _row_spec(t, cb, width=PAIR):
    return pl.BlockSpec((t, width), lambda i, cb=cb: (i, cb))


def _whole_spec(rows, cb, width=PAIR):
    return pl.BlockSpec((rows, width), lambda i, cb=cb: (0, cb), pipeline_mode=pl.Buffered(1))


def _is_pow2(x):
    return float(np.frexp(x)[0]) == 0.5


def _masked_heads(blocks, kinds, pair, dtype, scale=None):
    out = []
    for e in range(2):
        head = 2 * pair + e
        parts = [jnp.where(_lane_mask(k, head, b.shape[0]), b.astype(F32) * (1.0 if scale is None else scale),
                           0.0).astype(dtype)
                 for b, k in zip(blocks, kinds)]
        out.append(parts[0] if len(parts) == 1 else jnp.concatenate(parts, axis=1))
    return out


def _logit_reach(qh, kmax2, head):
    q32 = qh.astype(F32)
    return jnp.sqrt(jnp.sum(q32 * q32, axis=1, keepdims=True) * _col(kmax2, head)) * REACH_SLACK


def _forget_top(ft_ref, head, off):
    return jnp.max(-ft_ref[head:head + 1, pl.ds(off, PAIR)])


def _col(block, idx):
    lane = lax.broadcasted_iota(jnp.int32, block.shape, 1)
    return jnp.sum(jnp.where(lane == idx, block, 0.0), axis=1, keepdims=True)


def _scatter_cols(cols, t):
    lane = lax.broadcasted_iota(jnp.int32, (t, PAIR), 1)
    out = jnp.zeros((t, PAIR), F32)
    for idx, c in cols.items():
        out = out + jnp.where(lane == idx, c, 0.0)
    return out


def _take_heads(per_head, pair):
    return jnp.where(_lane_mask("pair", 0, per_head[0].shape[0]), per_head[0], per_head[1])


class _Parts:
    def __init__(self, q_parts, k_parts, v_parts, tq, sk):
        self.kinds = [[kind for _, _, kind in q_parts[p]] for p in range(2)]
        self.nparts = len(q_parts[0])
        self.q_specs = [_row_spec(tq, cb) for p in range(2) for _, cb, _ in q_parts[p]]
        self.q_args = [a for p in range(2) for a, _, _ in q_parts[p]]
        self.k_specs = [_whole_spec(sk, cb) for p in range(2) for _, cb, _ in k_parts[p]]
        self.k_args = [a for p in range(2) for a, _, _ in k_parts[p]]
        self.v_specs = [_whole_spec(sk, cb) for _, cb in v_parts]
        self.v_args = [a for a, _ in v_parts]
        self.width = PAIR * self.nparts

    def split(self, refs):
        n = self.nparts
        refs = list(refs)
        q = [refs[p * n:(p + 1) * n] for p in range(2)]
        k = [refs[2 * n + p * n:2 * n + (p + 1) * n] for p in range(2)]
        v = refs[4 * n:4 * n + 2]
        return q, k, v, refs[4 * n + 2:]

    def k_block(self, k_refs, off, t):
        blks = [r[pl.ds(off, t), :] for r in k_refs]
        return blks[0] if len(blks) == 1 else jnp.concatenate(blks, axis=1)


def _key_norm_max(src, k_blk):
    rows = src.shape[0]

    def body(i, k0_ref, k1_ref, o_ref):
        @pl.when(i == 0)
        def _():
            o_ref[...] = jnp.zeros_like(o_ref)

        cols = {}
        for p, ref in enumerate((k0_ref, k1_ref)):
            k32 = ref[...].astype(F32)
            for e in range(2):
                sq = jnp.sum(jnp.where(_lane_mask("pair", e, k32.shape[0]), k32 * k32, 0.0), axis=1, keepdims=True)
                cols[2 * p + e] = jnp.max(sq, axis=0, keepdims=True)
        lane = lax.broadcasted_iota(jnp.int32, (1, PAIR), 1)
        o_ref[...] = jnp.maximum(o_ref[...], sum(jnp.where(lane == h, c, 0.0) for h, c in cols.items()))

    return _rowwise(body, "key_norm_max", rows, ROW_TILE, [(src, PAIR, k_blk), (src, PAIR, k_blk + 1)], [], [],
                    [((1, PAIR), F32)])[0]


def _softmax_fwd(q_parts, k_parts, v_parts, sq, sk, scale, causal, bias, name):
    tq = min(SOFTMAX_TILE, sq)
    tk = tq if causal else min(SOFTMAX_TILE, sk)
    nkv = sk // tk
    pp = _Parts(q_parts, k_parts, v_parts, tq, sk)

    def kern(*refs):
        q_refs, k_refs, v_refs, rest = pp.split(refs)
        if bias is not None:
            fc_ref, ft_ref, kmax_ref, o_ref, lse_ref = rest
            fcb = fc_ref[...]
        else:
            o_ref, lse_ref = rest
        i = pl.program_id(0)
        fold = _is_pow2(scale)
        head_on = [jnp.where(_lane_mask("pair", e, tk), 1.0, 0.0).astype(MXU_DTYPE) for e in range(2)]
        head_off = [jnp.where(_lane_mask("pair", e, tk), 0.0, 1.0).astype(MXU_DTYPE) for e in range(2)]
        lse_cols = {}
        for first in range(0, 2, FWD_PAIRS_PER_LOOP):
            pairs = list(range(first, first + FWD_PAIRS_PER_LOOP))
            heads = [2 * p + e for p in pairs for e in range(2)]
            qm = {}
            for p in pairs:
                masked_q = _masked_heads([r[...] for r in q_refs[p]], pp.kinds[p], p, MXU_DTYPE, scale if fold else None)
                qm.update({2 * p + e: masked_q[e] for e in range(2)})
            if bias is not None:
                reach = {h: _logit_reach(qm[h], kmax_ref[...], h) for h in heads}

            def block(j, carry, masked, pairs=pairs, qm=qm):
                off = _blk_off(j, tk)
                out = []
                for p in pairs:
                    kb = pp.k_block(k_refs[p], off, tk)
                    vb = v_refs[p][pl.ds(off, tk), :]
                    for e in range(2):
                        h = 2 * p + e
                        m, acc = carry[len(out)]
                        s = _dot_nt(qm[h], kb) * (LOG2E if fold else scale * LOG2E)
                        if bias is not None:
                            s = s - ft_ref[h:h + 1, pl.ds(off, tk)] * LOG2E
                        if masked:
                            s = jnp.where(_causal_mask(tq, False), s, NEG)
                        m_new = jnp.maximum(m, jnp.max(s, axis=1, keepdims=True))
                        pr = jnp.exp2(s - m_new).astype(MXU_DTYPE)
                        out.append((m_new, jnp.exp2(m - m_new) * acc + _dot(pr, vb * head_on[e] + head_off[e])))
                return tuple(out)

            carry = tuple((jnp.full((tq, 1), NEG, F32), jnp.zeros((tq, PAIR), F32)) for _ in heads)
            if causal and bias is not None:
                def alive(c, j, heads=heads, reach=reach):
                    return functools.reduce(jnp.maximum, [
                        jnp.max(reach[h] + _forget_top(ft_ref, h, _blk_off(j, tk)) - c[n][0] * LN2)
                        for n, h in enumerate(heads)])

                def step(state, block=block, alive=alive):
                    n, _, c = state
                    c = block(i - 1 - n, c, False)
                    return n + 1, alive(c, i - 1 - n), c

                carry = block(i, carry, True)
                _, _, carry = lax.while_loop(lambda st: jnp.logical_and(st[0] < i, st[1] > DEAD_LOGIT), step,
                                             (jnp.int32(0), alive(carry, i), carry))
            elif causal:
                carry = lax.fori_loop(0, i, lambda j, c, block=block: block(j, c, False), carry)
                carry = block(i, carry, True)
            else:
                for j in range(nkv):
                    carry = block(j, carry, False)
            for n, p in enumerate(pairs):
                outs = []
                for e in range(2):
                    m, acc = carry[2 * n + e]
                    l = _col(acc, HEAD_DIM * (1 - e))
                    outs.append(acc / l)
                    lse_cols[2 * p + e] = m * LN2 + jnp.log(l) + (_col(fcb, 2 * p + e) if bias is not None else 0.0)
                o_ref[:, p * PAIR:(p + 1) * PAIR] = _take_heads(outs, p)
        lse_ref[...] = _scatter_cols(lse_cols, tq)

    in_specs = pp.q_specs + pp.k_specs + pp.v_specs
    args = pp.q_args + pp.k_args + pp.v_args
    if bias is not None:
        in_specs += [_row_spec(tq, 0), pl.BlockSpec((8, sk), lambda i: (0, 0), pipeline_mode=pl.Buffered(1)),
                     pl.BlockSpec((1, PAIR), lambda i: (0, 0))]
        args += list(bias)
    return pl.pallas_call(
        kern, name=name, grid=(sq // tq,), in_specs=in_specs,
        out_specs=[_row_spec(tq, 0, GROUP_W), _row_spec(tq, 0)],
        out_shape=[jax.ShapeDtypeStruct((sq, GROUP_W), F32), jax.ShapeDtypeStruct((sq, PAIR), F32)],
        compiler_params=_att_params(True),
    )(*args)


def _softmax_bwd(q_parts, k_parts, v_parts, o, lse, do, do_blk, sq, sk, scale, causal, bias, name):
    tq = min(SOFTMAX_TILE, sq)
    tk = tq if causal else min(SOFTMAX_TILE, sk)
    nkv = sk // tk
    pp = _Parts(q_parts, k_parts, v_parts, tq, sk)
    quad = pp.nparts == 2
    wq = GROUP_W + (PAIR if quad else 0)

    def kern(*refs):
        q_refs, k_refs, v_refs, rest = pp.split(refs)
        if bias is not None:
            o_ref, lse_ref, do_ref, fc_ref, ft_ref, kmax_ref, dq_ref, dk_ref, dv_ref, dfq_ref, dfk_ref = rest
            fcb = fc_ref[...]
        else:
            o_ref, lse_ref, do_ref, dq_ref, dk_ref, dv_ref = rest
        i = pl.program_id(0)
        fold = _is_pow2(scale)

        @pl.when(i == 0)
        def _():
            dk_ref[...] = jnp.zeros_like(dk_ref)
            dv_ref[...] = jnp.zeros_like(dv_ref)
            if bias is not None:
                dfk_ref[...] = jnp.zeros_like(dfk_ref)

        lse_b = lse_ref[...]
        qm, dom, delta, lse_h = [], [], [], []
        for p in range(2):
            qm += _masked_heads([r[...] for r in q_refs[p]], pp.kinds[p], p, MXU_DTYPE, scale if fold else None)
            do_p = do_ref[:, p * PAIR:(p + 1) * PAIR]
            dom += _masked_heads([do_p], ["pair"], p, MXU_DTYPE)
            prod = do_p * o_ref[:, p * PAIR:(p + 1) * PAIR]
            for e in range(2):
                h = 2 * p + e
                delta.append(jnp.sum(jnp.where(_lane_mask("pair", h, tq), prod, 0.0), axis=1, keepdims=True))
                lse_h.append(_col(lse_b, h) - (_col(fcb, h) if bias is not None else 0.0))

        def block(j, carry, masked):
            off = _blk_off(j, tk)
            out = []
            for p in range(2):
                kb = pp.k_block(k_refs[p], off, tk)
                vb = v_refs[p][pl.ds(off, tk), :]
                dk_acc = jnp.zeros((tk, pp.width), F32)
                dv_acc = jnp.zeros((tk, PAIR), F32)
                for e in range(2):
                    h = 2 * p + e
                    dq, dfq = carry[h]
                    s = _dot_nt(qm[h], kb)
                    if not fold:
                        s = s * scale
                    if bias is not None:
                        s = s - ft_ref[h:h + 1, pl.ds(off, tk)]
                    if masked:
                        s = jnp.where(_causal_mask(tq, False), s, NEG)
                    pr = jnp.exp(s - lse_h[h])
                    ds = pr * (_dot_nt(dom[h], vb) - delta[h])
                    dsb = (ds if fold else ds * scale).astype(MXU_DTYPE)
                    dv_acc = dv_acc + _dot_tn(pr.astype(MXU_DTYPE), dom[h])
                    dk_acc = dk_acc + _dot_tn(dsb, qm[h])
                    dq = dq + _dot(dsb, kb)
                    if bias is not None:
                        dfq = dfq + jnp.sum(ds, axis=1, keepdims=True)
                        dfk_ref[h:h + 1, pl.ds(off, tk)] -= jnp.sum(ds, axis=0, keepdims=True)
                    out.append((dq, dfq))
                dv_ref[pl.ds(off, tk), p * PAIR:(p + 1) * PAIR] += dv_acc
                dk_ref[pl.ds(off, tk), p * PAIR:(p + 1) * PAIR] += dk_acc[:, :PAIR]
                if quad:
                    dk_ref[pl.ds(off, tk), GROUP_W:] += dk_acc[:, PAIR:]
            return tuple(out)

        carry = tuple((jnp.zeros((tq, pp.width), F32), jnp.zeros((tq, 1), F32)) for _ in range(N_HEADS))
        if causal and bias is not None:
            reach = [_logit_reach(qm[h], kmax_ref[...], h) - lse_h[h] for h in range(N_HEADS)]

            def alive(j):
                return functools.reduce(jnp.maximum, [jnp.max(reach[h] + _forget_top(ft_ref, h, _blk_off(j, tk)))
                                                      for h in range(N_HEADS)])

            def step(state):
                n, _, c = state
                return n + 1, alive(i - 1 - n), block(i - 1 - n, c, False)

            carry = block(i, carry, True)
            _, _, carry = lax.while_loop(lambda st: jnp.logical_and(st[0] < i, st[1] > DEAD_LOGIT), step,
                                         (jnp.int32(0), alive(i), carry))
        elif causal:
            carry = lax.fori_loop(0, i, lambda j, c: block(j, c, False), carry)
            carry = block(i, carry, True)
        else:
            for j in range(nkv):
                carry = block(j, carry, False)
        dqs = [c[0] * scale if fold else c[0] for c in carry]
        for p in range(2):
            dq_ref[:, p * PAIR:(p + 1) * PAIR] = _take_heads([dqs[2 * p + e][:, :PAIR] for e in range(2)], p)
        if quad:
            dq_ref[:, GROUP_W:] = sum(jnp.where(_lane_mask("quad", h, tq), dqs[h][:, PAIR:], 0.0)
                                      for h in range(N_HEADS))
        if bias is not None:
            dfq_ref[...] = _scatter_cols({h: carry[h][1] for h in range(N_HEADS)}, tq)

    acc_spec = lambda rows, width: pl.BlockSpec((rows, width), lambda i: (0, 0), pipeline_mode=pl.Buffered(1))
    in_specs = pp.q_specs + pp.k_specs + pp.v_specs + [_row_spec(tq, 0, GROUP_W), _row_spec(tq, 0),
                                                       _row_spec(tq, do_blk, GROUP_W)]
    args = pp.q_args + pp.k_args + pp.v_args + [o, lse, do]
    out_specs = [_row_spec(tq, 0, wq), acc_spec(sk, wq), acc_spec(sk, GROUP_W)]
    out_shape = [jax.ShapeDtypeStruct((sq, wq), F32), jax.ShapeDtypeStruct((sk, wq), F32),
                 jax.ShapeDtypeStruct((sk, GROUP_W), F32)]
    if bias is not None:
        in_specs += [_row_spec(tq, 0), pl.BlockSpec((8, sk), lambda i: (0, 0), pipeline_mode=pl.Buffered(1)),
                     pl.BlockSpec((1, PAIR), lambda i: (0, 0))]
        args += list(bias)
        out_specs += [_row_spec(tq, 0), acc_spec(8, sk)]
        out_shape += [jax.ShapeDtypeStruct((sq, PAIR), F32), jax.ShapeDtypeStruct((8, sk), F32)]
    return pl.pallas_call(
        kern, name=name, grid=(sq // tq,), in_specs=in_specs, out_specs=out_specs, out_shape=out_shape,
        compiler_params=_att_params(False),
    )(*args)


def _sb_logs(qh, kb, valid):
    z = _dot_nt(qh, kb)
    sp = _softplus(z)
    lk = -sp
    if valid is not None:
        lk = jnp.where(valid, lk, 0.0)
    return lk, z - sp


def _sb_valid(d, tq, tk):
    r = lax.broadcasted_iota(jnp.int32, (tq, tk), 0)
    c = lax.broadcasted_iota(jnp.int32, (tq, tk), 1)
    return c + d * tk < r


def _tri_sums(xs, tri):
    t = xs[0].shape[0]
    pieces = [_split2(x) for x in xs]
    hi = _dot(jnp.concatenate([pc[0] for pc in pieces], axis=0), tri)
    lo = _dot(jnp.concatenate([pc[1] for pc in pieces], axis=0), tri)
    return [hi[n * t:(n + 1) * t] + lo[n * t:(n + 1) * t] for n in range(len(xs))]


def _sb_fwd(src, q_blk, k_blk, v_blk, s, scale, name):
    assert _is_pow2(scale)
    tq, t = min(SB_QUERY_TILE, s), min(ATT_TILE, s)
    slots = -(-(s // t) // SB_SLOT) * SB_SLOT
    width = N_HEADS * slots
    band = tq // t
    pair = lambda blk: [[(src, blk + p, "pair")] for p in range(2)]
    pp = _Parts(pair(q_blk), pair(k_blk), [(src, v_blk + p) for p in range(2)], tq, s)

    def kern(*refs):
        q_refs, k_refs, v_refs, (tri_ref, o_ref, rm_ref, cnt_ref) = pp.split(refs)
        i = pl.program_id(0)
        tri = tri_ref[...]
        lane = lax.broadcasted_iota(jnp.int32, (tq, width), 1)
        qm = []
        for p in range(2):
            qm += _masked_heads([q_refs[p][0][...]], ["pair"], p, MXU_DTYPE, scale)

        def block(j, carry, valid):
            accs, rights, rm = carry
            off = _blk_off(j, t)
            kbs = [k_refs[p][0][pl.ds(off, t), :] for p in range(2)]
            vbs = [v_refs[p][pl.ds(off, t), :] for p in range(2)]
            logs = [_sb_logs(qm[h], kbs[h // 2], valid) for h in range(N_HEADS)]
            tails = _tri_sums([lg[0] for lg in logs], tri)
            new_acc, new_right = [], []
            for h in range(N_HEADS):
                lk, ls = logs[h]
                w = jnp.exp(ls + tails[h] + rights[h])
                if valid is not None:
                    w = jnp.where(valid, w, 0.0)
                new_acc.append(accs[h] + _dot(w.astype(MXU_DTYPE), vbs[h // 2]))
                rm = rm + jnp.where(lane == slots * h + j, rights[h], 0.0)
                new_right.append(rights[h] + jnp.sum(lk, axis=1, keepdims=True))
            return tuple(new_acc), tuple(new_right), rm

        carry = (tuple(jnp.zeros((tq, PAIR), F32) for _ in range(N_HEADS)),
                 tuple(jnp.zeros((tq, 1), F32) for _ in range(N_HEADS)), jnp.zeros((tq, width), F32))
        for d in reversed(range(band)):
            carry = block(band * i + d, carry, _sb_valid(d, tq, t))

        def alive(c):
            return functools.reduce(jnp.maximum, [jnp.max(r) for r in c[1]])

        def step(state):
            n, _, c = state
            c = block(band * i - 1 - n, c, None)
            return n + 1, alive(c), c

        n_done, _, carry = lax.while_loop(lambda st: jnp.logical_and(st[0] < band * i, st[1] > EXP_UNDERFLOW),
                                          step, (jnp.int32(0), alive(carry), carry))
        cnt_ref[i] = n_done
        for p in range(2):
            o_ref[:, p * PAIR:(p + 1) * PAIR] = _take_heads([carry[0][2 * p + e] for e in range(2)], p)
        rm_ref[...] = carry[2]

    return pl.pallas_call(
        kern, name=name, grid=(s // tq,),
        in_specs=pp.q_specs + pp.k_specs + pp.v_specs + [pl.BlockSpec((t, t), lambda i: (0, 0))],
        out_specs=[_row_spec(tq, 0, GROUP_W), _row_spec(tq, 0, width), pl.BlockSpec(memory_space=pltpu.SMEM)],
        out_shape=[jax.ShapeDtypeStruct((s, GROUP_W), F32), jax.ShapeDtypeStruct((s, width), F32),
                   jax.ShapeDtypeStruct((s // tq,), jnp.int32)],
        compiler_params=_att_params(False),
    )(*(pp.q_args + pp.k_args + pp.v_args + [_tri(t, "row_gt_col")]))


def _sb_bwd(src, q_blk, k_blk, v_blk, do, do_blk, rm, visited, s, scale, name):
    assert _is_pow2(scale)
    tq, t = min(SB_QUERY_TILE, s), min(ATT_TILE, s)
    band = tq // t
    pair = lambda blk: [[(src, blk + p, "pair")] for p in range(2)]
    pp = _Parts(pair(q_blk), pair(k_blk), [(src, v_blk + p) for p in range(2)], tq, s)

    def kern(*refs):
        q_refs, k_refs, v_refs, (do_ref, rm_ref, tri_ref, pre_ref, cnt_ref, dq_ref, dk_ref, dv_ref) = pp.split(refs)
        i = pl.program_id(0)

        @pl.when(i == 0)
        def _():
            dk_ref[...] = jnp.zeros_like(dk_ref)
            dv_ref[...] = jnp.zeros_like(dv_ref)

        rmb = rm_ref[...]
        tri = tri_ref[...]
        pre = pre_ref[...]
        qm, dom = [], []
        for p in range(2):
            qm += _masked_heads([q_refs[p][0][...]], ["pair"], p, MXU_DTYPE, scale)
            dom += _masked_heads([do_ref[:, p * PAIR:(p + 1) * PAIR]], ["pair"], p, MXU_DTYPE)

        def block(j, carry, valid):
            dqs, lefts = carry
            off = _blk_off(j, t)
            kbs = [k_refs[p][0][pl.ds(off, t), :] for p in range(2)]
            vbs = [v_refs[p][pl.ds(off, t), :] for p in range(2)]
            logs = [_sb_logs(qm[h], kbs[h // 2], valid) for h in range(N_HEADS)]
            tails = _tri_sums([lg[0] for lg in logs], tri)
            ws, gs = [], []
            for h in range(N_HEADS):
                lk, ls = logs[h]
                w = jnp.exp(ls + tails[h] + _col(rmb, (rm.shape[1] // N_HEADS) * h + j))
                if valid is not None:
                    w = jnp.where(valid, w, 0.0)
                ws.append(w)
                gs.append(_dot_nt(dom[h], vbs[h // 2]) * w)
            prefix = _tri_sums(gs, pre)
            new_dq, new_left = [], []
            dk_acc = [jnp.zeros((t, PAIR), F32) for _ in range(2)]
            dv_acc = [jnp.zeros((t, PAIR), F32) for _ in range(2)]
            for h in range(N_HEADS):
                lk, ls = logs[h]
                sig = jnp.exp(ls)
                dz = gs[h] * (1.0 - sig) - sig * (prefix[h] + lefts[h])
                if valid is not None:
                    dz = jnp.where(valid, dz, 0.0)
                dzb = dz.astype(MXU_DTYPE)
                dv_acc[h // 2] = dv_acc[h // 2] + _dot_tn(ws[h].astype(MXU_DTYPE), dom[h])
                dk_acc[h // 2] = dk_acc[h // 2] + _dot_tn(dzb, qm[h])
                new_dq.append(dqs[h] + _dot(dzb, kbs[h // 2]))
                new_left.append(lefts[h] + jnp.sum(gs[h], axis=1, keepdims=True))
            for p in range(2):
                dv_ref[pl.ds(off, t), p * PAIR:(p + 1) * PAIR] += dv_acc[p]
                dk_ref[pl.ds(off, t), p * PAIR:(p + 1) * PAIR] += dk_acc[p]
            return tuple(new_dq), tuple(new_left)

        carry = (tuple(jnp.zeros((tq, PAIR), F32) for _ in range(N_HEADS)),
                 tuple(jnp.zeros((tq, 1), F32) for _ in range(N_HEADS)))
        carry = lax.fori_loop(band * i - cnt_ref[i], band * i, lambda j, c: block(j, c, None), carry)
        for d in range(band):
            carry = block(band * i + d, carry, _sb_valid(d, tq, t))
        for p in range(2):
            dq_ref[:, p * PAIR:(p + 1) * PAIR] = _take_heads([carry[0][2 * p + e] * scale for e in range(2)], p)

    mspec = pl.BlockSpec((t, t), lambda i: (0, 0))
    acc_spec = pl.BlockSpec((s, GROUP_W), lambda i: (0, 0), pipeline_mode=pl.Buffered(1))
    return pl.pallas_call(
        kern, name=name, grid=(s // tq,),
        in_specs=pp.q_specs + pp.k_specs + pp.v_specs + [_row_spec(tq, do_blk, GROUP_W), _row_spec(tq, 0, rm.shape[1]), mspec, mspec,
                                                         pl.BlockSpec(memory_space=pltpu.SMEM)],
        out_specs=[_row_spec(tq, 0, GROUP_W), acc_spec, acc_spec],
        out_shape=[jax.ShapeDtypeStruct((s, GROUP_W), F32)] * 3,
        compiler_params=_att_params(False),
    )(*(pp.q_args + pp.k_args + pp.v_args + [do, rm, _tri(t, "row_gt_col"), _tri(t, "row_lt_col"), visited]))


def _split_w_in(w):
    col = lambda n: w[:, _OFF[n]:_OFF[n + 1]]
    wa = jnp.concatenate([col(0), col(1), col(2), col(4), col(5), col(6), col(10)], axis=1)
    misc = jnp.concatenate([col(3), col(9), jnp.zeros((w.shape[0], 128 - 4 - MLA_ROPE), w.dtype)], axis=1)
    wb = jnp.concatenate([col(11), col(7), col(8), misc], axis=1)
    return wa, wb


def _merge_dw_in(dwp):
    a = lambda n: dwp[:, n * GROUP_W:(n + 1) * GROUP_W]
    b0 = PA_COLS
    gate = dwp[:, b0:b0 + 1024]
    cq = dwp[:, b0 + 1024:b0 + 1280]
    ckv = dwp[:, b0 + 1280:b0 + 1408]
    flog = dwp[:, b0 + 1408:b0 + 1412]
    krot = dwp[:, b0 + 1408 + MISC_KROT:b0 + 1408 + MISC_KROT + MLA_ROPE]
    return jnp.concatenate([a(0), a(1), a(2), flog, a(3), a(4), a(5), cq, ckv, krot, a(6), gate], axis=1)


def _heads_first(w, per_head, first):
    r = w.shape[0]
    w3 = w.reshape(r, N_HEADS, per_head)
    return jnp.concatenate([w3[:, :, :first].reshape(r, -1), w3[:, :, first:].reshape(r, -1)], axis=1)


def _heads_interleaved(w, per_head, first):
    r = w.shape[0]
    a = w[:, :N_HEADS * first].reshape(r, N_HEADS, first)
    b = w[:, N_HEADS * first:].reshape(r, N_HEADS, per_head - first)
    return jnp.concatenate([a, b], axis=2).reshape(r, N_HEADS * per_head)


def _pad_rows8(a):
    return a[:, :8].T


def _local_step(x2, mem2, tgt, p):
    s = x2.shape[0]
    nm = mem2.shape[0]
    head_scale = HEAD_DIM ** -0.5
    mla_scale = (HEAD_DIM + MLA_ROPE) ** -0.5
    tabs = _rope_tables(s)
    mats = _rope_matrices()
    pairs = lambda arr, blk: [[(arr, blk + q, "pair")] for q in range(2)]
    vals = lambda arr, blk: [(arr, blk + q) for q in range(2)]

    h, hb = _ln_fwd(x2, None, p["ln_in_g"], p["ln_in_b"], "ln_in_fwd")
    _, memn_b = _ln_fwd(mem2, None, p["mem_ln_g"], p["mem_ln_b"], "ln_mem_fwd")

    saved = []
    for l in range(DEPTH):
        wa, wb = _split_w_in(p["w_in"][l])
        wp = jnp.concatenate([wa, wb], axis=1)
        wq_up = _heads_first(p["w_mla_q_up"][l], HEAD_DIM + MLA_ROPE, HEAD_DIM)
        wkv_up = _heads_first(p["w_mla_kv_up"][l], 2 * HEAD_DIM, HEAD_DIM)
        bias_row = jnp.pad(p["b_forget"][l], (0, 128 - N_HEADS)).reshape(1, 128)
        pa = _matmul(hb, wa, MXU_DTYPE, "proj_a")
        pb = _matmul(hb, wb, F32, "proj_b")

        fc = _forget_fwd(pb, bias_row)
        fbias = (fc, _pad_rows8(fc), _key_norm_max(pa, 2))
        o_fox, lse_fox = _softmax_fwd(pairs(pa, 0), pairs(pa, 2), vals(pa, 4), s, s, head_scale, True, fbias,
                                      "fox_fwd")
        o_sb, *rm_sb = _sb_fwd(pa, 6, 8, 10, s, head_scale, "sb_fwd")

        qfull, cqn = _mla_q_fwd(pb, p["mla_q_norm_g"][l], wq_up, tabs, mats)
        kv, ckvn, kr4 = _mla_kv_fwd(pb, p["mla_kv_norm_g"][l], wkv_up, tabs, mats)
        mla_q = [[(qfull, q, "pair"), (qfull, 2, "quad")] for q in range(2)]
        mla_k = [[(kv, q, "pair"), (kr4, 0, "quad")] for q in range(2)]
        o_mla, lse_mla = _softmax_fwd(mla_q, mla_k, vals(kv, 2), s, s, mla_scale, True, None, "mla_fwd")

        mkv = _matmul(memn_b, p["w_mem_kv"][l], MXU_DTYPE, "mem_kv")
        o_mem, lse_mem = _softmax_fwd(pairs(pa, 12), pairs(mkv, 0), vals(mkv, 2), s, nm, head_scale, False, None,
                                      "mem_fwd")

        groups = (o_fox, o_sb, o_mla, o_mem)
        gated = _gate_fwd(groups, pb)
        if l < DEPTH - 1:
            u, h_next, hb_next = _out_proj_ln(gated, p["w_out"][l], h, p["ln_g"][l], p["ln_b"][l])
        else:
            u, dh, sq_cols = _out_proj_ln(gated, p["w_out"][l], h, p["ln_g"][l], p["ln_b"][l], tgt)
        saved.append(dict(u=u, hb=hb, wp=wp, wq_up=wq_up, wkv_up=wkv_up, bias_row=bias_row, pa=pa, pb=pb,
                          fbias=fbias, lse_fox=lse_fox, rm_sb=rm_sb, cqn=cqn, ckvn=ckvn, mla_q=mla_q, mla_k=mla_k,
                          kv=kv, lse_mla=lse_mla, mkv=mkv, lse_mem=lse_mem, groups=groups, gated=gated))
        if l < DEPTH - 1:
            h, hb = h_next, hb_next

    loss_sum = jnp.sum(sq_cols)

    grads = {k: [None] * DEPTH for k in ("w_in", "b_forget", "mla_q_norm_g", "w_mla_q_up", "mla_kv_norm_g",
                                         "w_mla_kv_up", "w_mem_kv", "w_out", "ln_g", "ln_b")}
    dmemn = []
    dy1, dy2, c1 = dh, None, 1.0
    for l in reversed(range(DEPTH)):
        r = saved[l]
        pa, pb = r["pa"], r["pb"]
        o_fox, o_sb, o_mla, o_mem = r["groups"]
        du, du_b, dg, db = _ln_bwd(r["u"], None, p["ln_g"][l], dy1, dy2, c1, "ln_bwd")
        grads["ln_g"][l], grads["ln_b"][l] = dg[0], db[0]
        grads["w_out"][l] = _matmul(r["gated"], du_b, F32, "out_proj_dw", "tn")
        dmixed, dgate_b = _out_proj_gate_bwd(du_b, p["w_out"][l], r["groups"], pb)

        dfq, dfk, dfv, dfc_q, dfc_k = _softmax_bwd(pairs(pa, 0), pairs(pa, 2), vals(pa, 4), o_fox, r["lse_fox"],
                                                   dmixed, 0, s, s, head_scale, True, r["fbias"], "fox_bwd")
        dmisc_f, dbf = _forget_bwd(pb, r["bias_row"], dfc_q + jnp.pad(dfc_k.T, ((0, 0), (0, 128 - 8))))
        grads["b_forget"][l] = dbf[0, :N_HEADS]

        dsq, dsk, dsv = _sb_bwd(pa, 6, 8, 10, dmixed, 1, *r["rm_sb"], s, head_scale, "sb_bwd")

        dqm, dkm, dvm = _softmax_bwd(r["mla_q"], r["mla_k"], vals(r["kv"], 2), o_mla, r["lse_mla"], dmixed, 2,
                                     s, s, mla_scale, True, None, "mla_bwd")
        dq_mla_b = _rope_q_bwd(dqm, tabs, mats)
        dcqn = _matmul(dq_mla_b, r["wq_up"], F32, "q_up_dx", "nt")
        grads["w_mla_q_up"][l] = _heads_interleaved(_matmul(r["cqn"], dq_mla_b, F32, "q_up_dw", "tn"),
                                                    HEAD_DIM + MLA_ROPE, HEAD_DIM)
        dcq_b, dgq = _rms_bwd(pb, MLA_Q_RANK, PB_CQ_BLK, p["mla_q_norm_g"][l], dcqn, "rms_q_bwd")
        grads["mla_q_norm_g"][l] = dgq[0]
        dkv_b = jnp.concatenate([dkm[:, :GROUP_W], dvm], axis=1).astype(MXU_DTYPE)
        dckvn = _matmul(dkv_b, r["wkv_up"], F32, "kv_up_dx", "nt")
        grads["w_mla_kv_up"][l] = _heads_interleaved(_matmul(r["ckvn"], dkv_b, F32, "kv_up_dw", "tn"),
                                                     2 * HEAD_DIM, HEAD_DIM)
        dckv_b, dgkv = _rms_bwd(pb, MLA_KV_RANK, PB_CKV_BLK, p["mla_kv_norm_g"][l], dckvn, "rms_kv_bwd")
        grads["mla_kv_norm_g"][l] = dgkv[0]
        dmisc_k = _rope_k_bwd(dkm[:, GROUP_W:], tabs, mats)

        dmq, dmk, dmv = _softmax_bwd(pairs(pa, 12), pairs(r["mkv"], 0), vals(r["mkv"], 2), o_mem, r["lse_mem"],
                                     dmixed, 3, s, nm, head_scale, False, None, "mem_bwd")
        dmkv_b = jnp.concatenate([dmk, dmv], axis=1).astype(MXU_DTYPE)
        grads["w_mem_kv"][l] = _matmul(memn_b, dmkv_b, F32, "mem_kv_dw", "tn")
        dmemn.append(_matmul(dmkv_b, p["w_mem_kv"][l], F32, "mem_kv_dx", "nt"))

        dp = [dfq, dfk, dfv, dsq, dsk, dsv, dmq, dgate_b, dcq_b, dckv_b, (dmisc_f, dmisc_k)]
        dhproj = _proj_dx(dp, r["wp"])
        grads["w_in"][l] = _merge_dw_in(_proj_dw(r["hb"], dp))
        dy1, dy2, c1 = du, dhproj, ALPHA

    dx, _, dg_in, db_in = _ln_bwd(x2, None, p["ln_in_g"], dy1, dy2, c1, "ln_in_bwd")
    _, _, dg_mem, db_mem = _ln_bwd(mem2, None, p["mem_ln_g"], dmemn[0], dmemn[1], 1.0, "ln_mem_bwd")
    out = {k: jnp.stack(v) for k, v in grads.items()}
    out.update(ln_in_g=dg_in[0], ln_in_b=db_in[0], mem_ln_g=dg_mem[0], mem_ln_b=db_mem[0])
    return loss_sum, dx, out


WIDE = "w_in"
FLAT_NAMES = ("w_out", "w_mem_kv", "w_mla_q_up", "w_mla_kv_up")
FLAT_ROWS = 896
BIG_NAMES = (WIDE,) + FLAT_NAMES
BIG_AXIS = dict(w_in=2, w_out=1, w_mem_kv=1, w_mla_q_up=2, w_mla_kv_up=2)
SMALL_NAMES = ("ln_in_g", "ln_in_b", "mem_ln_g", "mem_ln_b", "ln_g", "ln_b", "b_forget", "mla_q_norm_g",
               "mla_kv_norm_g")
ALL_NAMES = ("ln_in_g", "ln_in_b", "mem_ln_g", "mem_ln_b", "w_in", "b_forget", "mla_q_norm_g", "w_mla_q_up",
             "mla_kv_norm_g", "w_mla_kv_up", "w_mem_kv", "w_out", "ln_g", "ln_b")
N_CHIPS = 4
N_DEV = 8


def _rows_of(shape):
    rows = -(-int(np.prod(shape)) // LANES)
    return -(-rows // PACK_ALIGN) * PACK_ALIGN


def _pack(arrs, rows):
    parts = []
    for a in arrs:
        f = a.reshape(-1)
        n = _rows_of(a.shape) * LANES
        parts.append(jnp.pad(f, (0, n - f.shape[0])).reshape(-1, LANES))
    used = sum(q.shape[0] for q in parts)
    if rows > used:
        parts.append(jnp.zeros((rows - used, LANES), parts[0].dtype))
    return jnp.concatenate(parts, axis=0)


def _unpack(buf, shapes):
    out, r = [], 0
    for shp in shapes:
        n = _rows_of(shp)
        out.append(buf[r:r + n].reshape(-1)[:int(np.prod(shp))].reshape(shp))
        r += n
    return out


def _sharded_pair(get):
    wide = get(WIDE)
    return [wide.reshape(-1, wide.shape[-1]), _pack([get(n) for n in FLAT_NAMES], FLAT_ROWS)]


HBM_SPEC = pl.BlockSpec(memory_space=pltpu.HBM)


def _gather_weights(shards):
    n = len(shards)

    def body(*refs):
        w_refs, out_refs, (send_sems, recv_sems, local_sems) = refs[:n], refs[n:2 * n], refs[2 * n:]
        x, y, c = (lax.axis_index(a) for a in MESH_AXES)
        me, sibling = 2 * x + y, (x, y, 1 - c)
        chips = [(1 - x, y), (x, 1 - y), (1 - x, 1 - y)]
        local, first, passed = [], [], []
        for a in range(n):
            w_ref, out_ref, half = w_refs[a], out_refs[a], shards[a].shape[0] // 2

            def part(chip, core, out_ref=out_ref, half=half):
                return out_ref.at[chip, pl.ds(core * half, half)]

            def copy(k, src, dst, to, a=a):
                return pltpu.make_async_remote_copy(
                    src_ref=src, dst_ref=dst, send_sem=send_sems.at[6 * a + k], recv_sem=recv_sems.at[6 * a + k],
                    device_id=to, device_id_type=pl.DeviceIdType.MESH)

            local.append(pltpu.make_async_copy(w_ref, out_ref.at[me], local_sems.at[a]))
            local[-1].start()
            mine = [copy(k, w_ref.at[pl.ds(c * half, half)], part(me, c), (px, py, c))
                    for k, (px, py) in enumerate(chips)]
            for cp in mine:
                cp.start()
            first.append((mine, part, copy))
        for mine, part, copy in first:
            for k, (px, py) in enumerate(chips):
                copy(k, part(me, c), part(2 * px + py, c), (px, py, c)).wait_recv()
                passed.append(copy(3 + k, part(2 * px + py, c), part(2 * px + py, c), sibling))
                passed[-1].start()
        for mine, part, copy in first:
            for k, (px, py) in enumerate(chips):
                copy(3 + k, part(me, c), part(2 * px + py, 1 - c), sibling).wait_recv()
        for cp in [cp for mine, _, _ in first for cp in mine] + passed:
            cp.wait_send()
        for cp in local:
            cp.wait()

    return pl.pallas_call(
        body, name="gather_weights",
        out_shape=[jax.ShapeDtypeStruct((N_CHIPS,) + s.shape, s.dtype) for s in shards],
        in_specs=[HBM_SPEC] * n, out_specs=[HBM_SPEC] * n,
        scratch_shapes=[pltpu.SemaphoreType.DMA((6 * n,)), pltpu.SemaphoreType.DMA((6 * n,)),
                        pltpu.SemaphoreType.DMA((n,))],
    )(*shards)


def _exchange_grads(bigs, small):
    nb = len(bigs)
    halves = [b.shape[1] // 2 for b in bigs]
    chunks = [_pick(h, (128, 64, 32, 16)) for h in halves]

    def body(*refs):
        big_refs, small_ref = refs[:nb], refs[nb]
        sum_refs, out_refs = refs[nb + 1:2 * nb + 1], refs[2 * nb + 1:3 * nb + 2]
        scratch = refs[3 * nb + 2:]
        land_bufs, sum_bufs = scratch[:nb], scratch[nb:2 * nb]
        send_sems, recv_sems, local_sems, load_sem, swap_send, swap_recv, keep_sems = scratch[2 * nb:]
        x, y, c = (lax.axis_index(a) for a in MESH_AXES)
        me, my_chip = 4 * x + 2 * y + c, 2 * x + y
        flips = [(fx, fy, fc) for fx in (0, 1) for fy in (0, 1) for fc in (0, 1) if fx or fy or fc]
        peers = [(1 - x if fx else x, 1 - y if fy else y, 1 - c if fc else c) for fx, fy, fc in flips]

        sources = [lambda chip, core, r=big_refs[a], half=bigs[a].shape[1] // 2: r.at[chip, pl.ds(core * half, half)]
                   for a in range(nb)] + [lambda chip, core: small_ref]

        def copy(a, k, src, slot, to):
            return pltpu.make_async_remote_copy(
                src_ref=src, dst_ref=out_refs[a].at[slot], send_sem=send_sems.at[7 * a + k],
                recv_sem=recv_sems.at[7 * a + k], device_id=to, device_id_type=pl.DeviceIdType.MESH)

        own = [pltpu.make_async_copy(src(my_chip, c), out_refs[a].at[me], local_sems.at[a])
               for a, src in enumerate(sources)]
        for cp in own:
            cp.start()
        sends = [copy(a, k, src(2 * px + py, pc), me, (px, py, pc))
                 for a, src in enumerate(sources) for k, (px, py, pc) in enumerate(peers)]
        for cp in sends:
            cp.start()
        for a, src in enumerate(sources):
            for k, (px, py, pc) in enumerate(peers):
                copy(a, k, src(my_chip, c), 4 * px + 2 * py + pc, (px, py, pc)).wait_recv()
        for cp in sends:
            cp.wait_send()
        for cp in own:
            cp.wait()

        tails = []
        for a in range(nb):
            for r0 in range(0, halves[a], chunks[a]):
                load = pltpu.make_async_copy(out_refs[a].at[:, pl.ds(r0, chunks[a])], land_bufs[a], load_sem)
                load.start()
                load.wait()
                total = land_bufs[a][0].astype(F32)
                for d in range(1, N_DEV):
                    total = total + land_bufs[a][d].astype(F32)
                sum_bufs[a][pl.ds(r0, chunks[a]), :] = total
            keep = pltpu.make_async_copy(sum_bufs[a], sum_refs[a].at[c], keep_sems.at[a])
            give = pltpu.make_async_remote_copy(
                src_ref=sum_bufs[a], dst_ref=sum_refs[a].at[c], send_sem=swap_send.at[a], recv_sem=swap_recv.at[a],
                device_id=(x, y, 1 - c), device_id_type=pl.DeviceIdType.MESH)
            keep.start()
            give.start()
            tails.append((keep, give))
        for a, (keep, give) in enumerate(tails):
            pltpu.make_async_remote_copy(
                src_ref=sum_bufs[a], dst_ref=sum_refs[a].at[1 - c], send_sem=swap_send.at[a], recv_sem=swap_recv.at[a],
                device_id=(x, y, 1 - c), device_id_type=pl.DeviceIdType.MESH).wait_recv()
            give.wait_send()
            keep.wait()

    return pl.pallas_call(
        body, name="exchange_grads",
        out_shape=[jax.ShapeDtypeStruct((2, h, b.shape[2]), F32) for h, b in zip(halves, bigs)]
        + [jax.ShapeDtypeStruct((N_DEV, h, b.shape[2]), b.dtype) for h, b in zip(halves, bigs)]
        + [jax.ShapeDtypeStruct((N_DEV,) + small.shape, small.dtype)],
        in_specs=[HBM_SPEC] * (nb + 1), out_specs=[HBM_SPEC] * (2 * nb + 1),
        scratch_shapes=[pltpu.VMEM((N_DEV, ch, b.shape[2]), b.dtype) for ch, b in zip(chunks, bigs)]
        + [pltpu.VMEM((h, b.shape[2]), F32) for h, b in zip(halves, bigs)]
        + [pltpu.SemaphoreType.DMA((7 * (nb + 1),)), pltpu.SemaphoreType.DMA((7 * (nb + 1),)),
           pltpu.SemaphoreType.DMA((nb + 1,)), pltpu.SemaphoreType.DMA, pltpu.SemaphoreType.DMA((nb,)),
           pltpu.SemaphoreType.DMA((nb,)), pltpu.SemaphoreType.DMA((nb,))],
        compiler_params=pltpu.CompilerParams(vmem_limit_bytes=VMEM_LIMIT),
    )(*bigs, small)


def _adamw(parts, w, m, v, name):
    rows, width = w.shape
    n_parts = parts.shape[0]
    tile = _pick(rows, (128, 16, 8))
    bc1 = 1.0 - ADAM_B1 ** ADAM_STEP
    bc2 = 1.0 - ADAM_B2 ** ADAM_STEP

    def kern(p_ref, w_ref, m_ref, v_ref, g_ref, d_ref, nm_ref, nv_ref):
        g = p_ref[0].astype(F32)
        for d in range(1, n_parts):
            g = g + p_ref[d].astype(F32)
        nm = ADAM_B1 * m_ref[...] + (1.0 - ADAM_B1) * g
        nv = ADAM_B2 * v_ref[...] + (1.0 - ADAM_B2) * (g * g)
        g_ref[...] = g
        nm_ref[...] = nm
        nv_ref[...] = nv
        d_ref[...] = -ADAM_LR * ((nm / bc1) / (jnp.sqrt(nv / bc2) + ADAM_EPS) + ADAM_WD * w_ref[...])

    spec = pl.BlockSpec((tile, width), lambda i: (i, 0))
    return pl.pallas_call(
        kern, name=name, grid=(rows // tile,),
        in_specs=[pl.BlockSpec((n_parts, tile, width), lambda i: (0, i, 0)), spec, spec, spec],
        out_specs=[spec] * 4, out_shape=[jax.ShapeDtypeStruct((rows, width), F32)] * 4,
        compiler_params=pltpu.CompilerParams(dimension_semantics=("parallel",), vmem_limit_bytes=VMEM_LIMIT),
    )(parts, w, m, v)


def kernel(x, mem, ln_in_g, ln_in_b, mem_ln_g, mem_ln_b, w_in, b_forget, mla_q_norm_g, w_mla_q_up, mla_kv_norm_g, w_mla_kv_up, w_mem_kv, w_out, ln_g, ln_b, loss_target, m_ln_in_g, m_ln_in_b, m_mem_ln_g, m_mem_ln_b, m_w_in, m_b_forget, m_mla_q_norm_g, m_w_mla_q_up, m_mla_kv_norm_g, m_w_mla_kv_up, m_w_mem_kv, m_w_out, m_ln_g, m_ln_b, v_ln_in_g, v_ln_in_b, v_mem_ln_g, v_mem_ln_b, v_w_in, v_b_forget, v_mla_q_norm_g, v_w_mla_q_up, v_mla_kv_norm_g, v_w_mla_kv_up, v_w_mem_kv, v_w_out, v_ln_g, v_ln_b):
    w = dict(ln_in_g=ln_in_g, ln_in_b=ln_in_b, mem_ln_g=mem_ln_g, mem_ln_b=mem_ln_b, w_in=w_in, b_forget=b_forget,
             mla_q_norm_g=mla_q_norm_g, w_mla_q_up=w_mla_q_up, mla_kv_norm_g=mla_kv_norm_g,
             w_mla_kv_up=w_mla_kv_up, w_mem_kv=w_mem_kv, w_out=w_out, ln_g=ln_g, ln_b=ln_b)
    mo = dict(ln_in_g=m_ln_in_g, ln_in_b=m_ln_in_b, mem_ln_g=m_mem_ln_g, mem_ln_b=m_mem_ln_b, w_in=m_w_in,
              b_forget=m_b_forget, mla_q_norm_g=m_mla_q_norm_g, w_mla_q_up=m_w_mla_q_up,
              mla_kv_norm_g=m_mla_kv_norm_g, w_mla_kv_up=m_w_mla_kv_up, w_mem_kv=m_w_mem_kv, w_out=m_w_out,
              ln_g=m_ln_g, ln_b=m_ln_b)
    vo = dict(ln_in_g=v_ln_in_g, ln_in_b=v_ln_in_b, mem_ln_g=v_mem_ln_g, mem_ln_b=v_mem_ln_b, w_in=v_w_in,
              b_forget=v_b_forget, mla_q_norm_g=v_mla_q_norm_g, w_mla_q_up=v_w_mla_q_up,
              mla_kv_norm_g=v_mla_kv_norm_g, w_mla_kv_up=v_w_mla_kv_up, w_mem_kv=v_w_mem_kv, w_out=v_w_out,
              ln_g=v_ln_g, ln_b=v_ln_b)
    flat_shapes = [w[n].shape for n in FLAT_NAMES]
    small_shapes = [w[n].shape for n in SMALL_NAMES]

    got_wide, got_flat = _gather_weights(_sharded_pair(lambda n: w[n].astype(MXU_DTYPE)))
    full = dict(w)
    full[WIDE] = jnp.concatenate([got_wide[j] for j in range(N_CHIPS)], axis=1).reshape(
        w[WIDE].shape[:2] + (N_CHIPS * w[WIDE].shape[2],))
    per_chip = [_unpack(got_flat[j], flat_shapes) for j in range(N_CHIPS)]
    for idx, n in enumerate(FLAT_NAMES):
        full[n] = jnp.concatenate([per_chip[j][idx] for j in range(N_CHIPS)], axis=BIG_AXIS[n])

    loss_sum, dx, g = _local_step(x[0], mem[0], loss_target[0], full)
    loss = lax.psum(loss_sum * (0.5 / D_MODEL), MESH_AXES)

    def shard_of(n, j):
        ax, size = BIG_AXIS[n], w[n].shape[BIG_AXIS[n]]
        return lax.slice_in_dim(g[n], j * size, (j + 1) * size, axis=ax).astype(MXU_DTYPE)

    per_dest = [_sharded_pair(lambda n, j=j: shard_of(n, j)) for j in range(N_CHIPS)]
    bigs = [jnp.stack([per_dest[j][a] for j in range(N_CHIPS)]) for a in range(2)]
    exchanged = _exchange_grads(bigs, _pack([g[n] for n in SMALL_NAMES], SMALL_ROWS))
    halves, small_parts = exchanged[:2], exchanged[-1]

    res = []
    for a, (grad, nm) in enumerate(zip(halves, ("adamw_wide", "adamw_flat"))):
        state = [_sharded_pair(lambda n, src=src: src[n])[a] for src in (w, mo, vo)]
        res.append(_adamw(grad.reshape((1,) + state[0].shape), *state, nm))
    res_small = _adamw(small_parts, *[_pack([src[n] for n in SMALL_NAMES], SMALL_ROWS) for src in (w, mo, vo)],
                       "adamw_replicated")
    outs = []
    for kind in range(4):
        vals = {WIDE: res[0][kind].reshape(w[WIDE].shape)}
        vals.update(zip(FLAT_NAMES, _unpack(res[1][kind], flat_shapes)))
        vals.update(zip(SMALL_NAMES, _unpack(res_small[kind], small_shapes)))
        outs += [vals[n] for n in ALL_NAMES]
    return (loss, dx[None], *outs)
```

```python
import functools

import numpy as np
import jax
import jax.numpy as jnp
from jax import lax
from jax.experimental import pallas as pl
from jax.experimental.pallas import tpu as pltpu

F32 = jnp.float32
MXU_DTYPE = jnp.bfloat16

DEPTH = 2
D_MODEL = 1024
GROUP_W = 256
N_HEADS = 4
HEAD_DIM = 64
MLA_Q_RANK = 256
MLA_KV_RANK = 128
MLA_ROPE = 32
MLA_Q_COLS = N_HEADS * (HEAD_DIM + MLA_ROPE)
MLA_KV_COLS = N_HEADS * 2 * HEAD_DIM
ROPE_THETA = 10000.0
LN_EPS = 1e-5
RMS_EPS = 1e-6
ALPHA = (2 * DEPTH) ** 0.25
ADAM_LR, ADAM_B1, ADAM_B2, ADAM_EPS, ADAM_WD, ADAM_STEP = 0.001, 0.9, 0.999, 1e-08, 0.01, 10

_SPLIT = (256, 256, 256, 4, 256, 256, 256, 256, 128, 32, 256, 1024)
_OFF = [int(o) for o in np.cumsum((0,) + _SPLIT)]
IN_COLS = _OFF[-1]
PA_COLS = 7 * GROUP_W
PB_COLS = 1024 + 256 + 128 + 128
PB_CQ_BLK, PB_CKV_BLK, PB_MISC_BLK = 4, 10, 11
MISC_KROT = 4

LANES = 1024
PACK_ALIGN = 16
SMALL_ROWS = 144
ROW_TILE = 256
PROJ_BWD_ROWS = 512
ATT_TILE = 256
SB_QUERY_TILE = 256
SOFTMAX_TILE = 512
FWD_PAIRS_PER_LOOP = 1
PAIR = 128
SB_SLOT = PAIR // N_HEADS
VMEM_LIMIT = 56 * 1024 * 1024
MATMUL_VMEM = 30 * 1024 * 1024
NEG = -1e30
LOG2E = 1.4426950408889634
LN2 = 0.6931471805599453
EXP_UNDERFLOW = -104.0
DEAD_LOGIT = -110.0
REACH_SLACK = 1.0 + 2.0 ** -10
MESH_AXES = ("x", "y", "c")


def _dot(a, b):
    return jnp.dot(a, b, preferred_element_type=F32)


def _dot_nt(a, b):
    return lax.dot_general(a, b, (((1,), (1,)), ((), ())), preferred_element_type=F32)


def _dot_tn(a, b):
    return lax.dot_general(a, b, (((0,), (0,)), ((), ())), preferred_element_type=F32)


def _split2(x):
    hi = x.astype(MXU_DTYPE)
    lo = (x - hi.astype(F32)).astype(MXU_DTYPE)
    return hi, lo


def _split3(x):
    hi = x.astype(MXU_DTYPE)
    r = x - hi.astype(F32)
    mid = r.astype(MXU_DTYPE)
    lo = (r - mid.astype(F32)).astype(MXU_DTYPE)
    return hi, mid, lo


def _dot_exact_r(x, pm):
    hi, mid, lo = _split3(x)
    return _dot(hi, pm) + _dot(mid, pm) + _dot(lo, pm)


def _dot_exact_l(pm, x):
    hi, mid, lo = _split3(x)
    return _dot(pm, hi) + _dot(pm, mid) + _dot(pm, lo)


def _pick(dim, prefs):
    for p in prefs:
        if dim % p == 0:
            return p
    return dim


def _softplus(z):
    return jnp.maximum(z, 0.0) + jnp.log(1.0 + jnp.exp(-jnp.abs(z)))


def _tile_options(dim):
    opts = [d for d in range(128, min(dim, 2048) + 1, 128) if dim % d == 0]
    return opts or [dim]


def _matmul_tiles(m, n, k, out_bytes):
    tk = k if k <= 4096 else _pick(k, (1024, 512, 256, 128))
    best = None
    for tm in _tile_options(m):
        for tn in _tile_options(n):
            vmem = 2 * 2 * (tm * tk + tk * tn) + 4 * tm * tn + 2 * out_bytes * tm * tn
            if vmem <= MATMUL_VMEM and (best is None or tm * tn / (tm + tn) > best[0]):
                best = (tm * tn / (tm + tn), tm, tn)
    return best[1], best[2], tk


def _matmul(a, b, out_dtype, name, mode="nn"):
    m, k = (a.shape[1], a.shape[0]) if mode == "tn" else a.shape
    n = b.shape[0] if mode == "nt" else b.shape[1]
    tm, tn, tk = _matmul_tiles(m, n, k, jnp.dtype(out_dtype).itemsize)
    nk = k // tk
    dot = {"nn": _dot, "tn": _dot_tn, "nt": _dot_nt}[mode]

    def kern(a_ref, b_ref, o_ref, *acc):
        if nk == 1:
            o_ref[...] = dot(a_ref[...], b_ref[...]).astype(o_ref.dtype)
            return
        acc_ref, = acc
        kk = pl.program_id(2)

        @pl.when(kk == 0)
        def _():
            acc_ref[...] = jnp.zeros_like(acc_ref)

        acc_ref[...] += dot(a_ref[...], b_ref[...])

        @pl.when(kk == nk - 1)
        def _():
            o_ref[...] = acc_ref[...].astype(o_ref.dtype)

    a_spec = (pl.BlockSpec((tk, tm), lambda i, j, kk: (kk, i)) if mode == "tn"
              else pl.BlockSpec((tm, tk), lambda i, j, kk: (i, kk)))
    b_spec = (pl.BlockSpec((tn, tk), lambda i, j, kk: (j, kk)) if mode == "nt"
              else pl.BlockSpec((tk, tn), lambda i, j, kk: (kk, j)))
    return pl.pallas_call(
        kern, name=name, grid=(m // tm, n // tn, nk), in_specs=[a_spec, b_spec],
        out_specs=pl.BlockSpec((tm, tn), lambda i, j, kk: (i, j)),
        out_shape=jax.ShapeDtypeStruct((m, n), out_dtype),
        scratch_shapes=[pltpu.VMEM((tm, tn), F32)] if nk > 1 else [],
        compiler_params=pltpu.CompilerParams(
            dimension_semantics=("parallel", "parallel", "arbitrary"), vmem_limit_bytes=VMEM_LIMIT),
    )(a.astype(MXU_DTYPE), b.astype(MXU_DTYPE))


def _proj_in(hb, wa, wb):
    rows, d = hb.shape
    tm = min(PROJ_BWD_ROWS, rows)

    def kern(a_ref, wa_ref, wb_ref, pa_ref, pb_ref):
        a = a_ref[...]
        pa_ref[...] = _dot(a, wa_ref[...]).astype(pa_ref.dtype)
        pb_ref[...] = _dot(a, wb_ref[...])

    whole = lambda w: pl.BlockSpec(w.shape, lambda i: (0, 0), pipeline_mode=pl.Buffered(1))
    return pl.pallas_call(
        kern, name="proj_in", grid=(rows // tm,),
        in_specs=[pl.BlockSpec((tm, d), lambda i: (i, 0)), whole(wa), whole(wb)],
        out_specs=[pl.BlockSpec((tm, wa.shape[1]), lambda i: (i, 0)), pl.BlockSpec((tm, wb.shape[1]), lambda i: (i, 0))],
        out_shape=[jax.ShapeDtypeStruct((rows, wa.shape[1]), MXU_DTYPE), jax.ShapeDtypeStruct((rows, wb.shape[1]), F32)],
        compiler_params=pltpu.CompilerParams(dimension_semantics=("parallel",), vmem_limit_bytes=VMEM_LIMIT),
    )(hb, wa.astype(MXU_DTYPE), wb.astype(MXU_DTYPE))


def _piece_arrays(pieces):
    return [a for p in pieces for a in (p if isinstance(p, tuple) else (p,))]


def _join_pieces(refs, pieces):
    refs, cols = list(refs), []
    for p in pieces:
        vals = [refs.pop(0)[...] for _ in (p if isinstance(p, tuple) else (p,))]
        cols.append(functools.reduce(jnp.add, vals).astype(MXU_DTYPE))
    return jnp.concatenate(cols, axis=1)


def _proj_dx(pieces, w):
    arrs = _piece_arrays(pieces)
    rows, n = arrs[0].shape[0], w.shape[0]
    tm = min(PROJ_BWD_ROWS, rows)

    def kern(*refs):
        refs[-1][...] = _dot_nt(_join_pieces(refs[:len(arrs)], pieces), refs[len(arrs)][...])

    return pl.pallas_call(
        kern, name="proj_dx", grid=(rows // tm,),
        in_specs=[pl.BlockSpec((tm, a.shape[1]), lambda i: (i, 0)) for a in arrs]
        + [pl.BlockSpec(w.shape, lambda i: (0, 0), pipeline_mode=pl.Buffered(1))],
        out_specs=pl.BlockSpec((tm, n), lambda i: (i, 0)), out_shape=jax.ShapeDtypeStruct((rows, n), F32),
        compiler_params=pltpu.CompilerParams(dimension_semantics=("parallel",), vmem_limit_bytes=VMEM_LIMIT),
    )(*arrs, w.astype(MXU_DTYPE))


def _proj_dw(a, pieces):
    arrs = _piece_arrays(pieces)
    rows, m = a.shape
    k = sum(x.shape[1] for x in (p[0] if isinstance(p, tuple) else p for p in pieces))
    tk = min(PROJ_BWD_ROWS, rows)

    def kern(*refs):
        o_ref = refs[-1]

        @pl.when(pl.program_id(0) == 0)
        def _():
            o_ref[...] = jnp.zeros_like(o_ref)

        o_ref[...] += _dot_tn(refs[0][...], _join_pieces(refs[1:1 + len(arrs)], pieces))

    return pl.pallas_call(
        kern, name="proj_dw", grid=(rows // tk,),
        in_specs=[pl.BlockSpec((tk, m), lambda i: (i, 0))]
        + [pl.BlockSpec((tk, x.shape[1]), lambda i: (i, 0)) for x in arrs],
        out_specs=pl.BlockSpec((m, k), lambda i: (0, 0), pipeline_mode=pl.Buffered(1)),
        out_shape=jax.ShapeDtypeStruct((m, k), F32),
        compiler_params=pltpu.CompilerParams(dimension_semantics=("arbitrary",), vmem_limit_bytes=VMEM_LIMIT),
    )(a, *arrs)


def _rowwise(body, name, rows, tile, row_ins, full_ins, row_outs, acc_outs=(), scratch=(),
             reverse=False, sequential=False):
    n = rows // tile

    def ridx(i):
        return (n - 1 - i) if reverse else i

    in_specs, args = [], []
    for arr, width, cb in row_ins:
        in_specs.append(pl.BlockSpec((tile, width), lambda i, cb=cb: (ridx(i), cb)))
        args.append(arr)
    for arr in full_ins:
        in_specs.append(pl.BlockSpec(arr.shape, lambda i, nd=arr.ndim: (0,) * nd))
        args.append(arr)
    out_shape = [jax.ShapeDtypeStruct((rows, w), dt) for w, dt in row_outs]
    out_shape += [jax.ShapeDtypeStruct(s, dt) for s, dt in acc_outs]
    out_specs = [pl.BlockSpec((tile, w), lambda i: (ridx(i), 0)) for w, dt in row_outs]
    out_specs += [pl.BlockSpec(s, lambda i, nd=len(s): (0,) * nd) for s, dt in acc_outs]

    def kern(*refs):
        body(pl.program_id(0), *refs)

    sem = "arbitrary" if (acc_outs or sequential) else "parallel"
    return pl.pallas_call(
        kern, name=name, grid=(n,), in_specs=in_specs, out_specs=out_specs, out_shape=out_shape,
        scratch_shapes=list(scratch),
        compiler_params=pltpu.CompilerParams(dimension_semantics=(sem,), vmem_limit_bytes=VMEM_LIMIT),
    )(*args)


def _ln_stats(u):
    mu = jnp.mean(u, axis=-1, keepdims=True)
    xc = u - mu
    var = jnp.mean(xc * xc, axis=-1, keepdims=True)
    return xc, lax.rsqrt(var + LN_EPS)


def _ln_fwd(a, b, g, beta, name):
    rows, d = a.shape
    has_b = b is not None

    def body(i, *refs):
        if has_b:
            a_ref, b_ref, g_ref, be_ref, h_ref, hb_ref = refs
            u = ALPHA * a_ref[...] + b_ref[...]
        else:
            a_ref, g_ref, be_ref, h_ref, hb_ref = refs
            u = a_ref[...]
        xc, rstd = _ln_stats(u)
        y = xc * rstd * g_ref[...] + be_ref[...]
        h_ref[...] = y
        hb_ref[...] = y.astype(hb_ref.dtype)

    row_ins = [(a, d, 0)] + ([(b, d, 0)] if has_b else [])
    return _rowwise(body, name, rows, min(ROW_TILE, rows), row_ins,
                    [g.reshape(1, d), beta.reshape(1, d)], [(d, F32), (d, MXU_DTYPE)])


def _ln_bwd(a, b, g, dy1, dy2, c1, name):
    rows, d = a.shape
    has_b = b is not None
    has_2 = dy2 is not None

    def body(i, *refs):
        refs = list(refs)
        a_ref = refs.pop(0)
        u = a_ref[...]
        if has_b:
            u = ALPHA * u + refs.pop(0)[...]
        dy = c1 * refs.pop(0)[...]
        if has_2:
            dy = dy + refs.pop(0)[...]
        g_ref, du_ref, dub_ref, dg_ref, db_ref = refs

        @pl.when(i == 0)
        def _():
            dg_ref[...] = jnp.zeros_like(dg_ref)
            db_ref[...] = jnp.zeros_like(db_ref)

        xc, rstd = _ln_stats(u)
        xhat = xc * rstd
        dxh = dy * g_ref[...]
        m1 = jnp.mean(dxh, axis=-1, keepdims=True)
        m2 = jnp.mean(dxh * xhat, axis=-1, keepdims=True)
        du = rstd * (dxh - m1 - xhat * m2)
        du_ref[...] = du
        dub_ref[...] = du.astype(dub_ref.dtype)
        dg_ref[...] += jnp.sum(dy * xhat, axis=0, keepdims=True)
        db_ref[...] += jnp.sum(dy, axis=0, keepdims=True)

    row_ins = [(a, d, 0)] + ([(b, d, 0)] if has_b else []) + [(dy1, d, 0)] + ([(dy2, d, 0)] if has_2 else [])
    return _rowwise(body, name, rows, min(ROW_TILE, rows), row_ins, [g.reshape(1, d)],
                    [(d, F32), (d, MXU_DTYPE)], [((1, d), F32), ((1, d), F32)])


def _gate_out_proj_ln(groups, pb, w_out, h, g, beta, target=None):
    rows, d = h.shape
    last = target is not None
    w = GROUP_W * len(groups)

    def body(i, *refs):
        refs = list(refs)
        mixed = jnp.concatenate([refs.pop(0)[...] for _ in groups], axis=1)
        gate = refs.pop(0)[...]
        if last:
            h_ref, t_ref, w_ref, g_ref, be_ref, a_ref, u_ref, dh_ref, acc_ref = refs

            @pl.when(i == 0)
            def _():
                acc_ref[...] = jnp.zeros_like(acc_ref)
        else:
            h_ref, w_ref, g_ref, be_ref, a_ref, u_ref, o_ref, ob_ref = refs
        gated = (mixed * (gate / (1.0 + jnp.exp(-gate)))).astype(a_ref.dtype)
        a_ref[...] = gated
        u = ALPHA * h_ref[...] + _dot(gated, w_ref[...])
        u_ref[...] = u
        xc, rstd = _ln_stats(u)
        y = xc * rstd * g_ref[...] + be_ref[...]
        if last:
            e = y - t_ref[...]
            dh_ref[...] = e * (1.0 / d)
            acc_ref[...] += jnp.sum(e * e, axis=0, keepdims=True)
        else:
            o_ref[...] = y
            ob_ref[...] = y.astype(ob_ref.dtype)

    row_ins = [(o, GROUP_W, 0) for o in groups] + [(pb, w, 0), (h, d, 0)] + ([(target, d, 0)] if last else [])
    full_ins = [w_out.astype(MXU_DTYPE), g.reshape(1, d), beta.reshape(1, d)]
    if last:
        return _rowwise(body, "gate_out_proj_ln_loss", rows, PROJ_BWD_ROWS, row_ins, full_ins,
                        [(w, MXU_DTYPE), (d, F32), (d, F32)], [((1, d), F32)])
    return _rowwise(body, "gate_out_proj_ln", rows, PROJ_BWD_ROWS, row_ins, full_ins,
                    [(w, MXU_DTYPE), (d, F32), (d, F32), (d, MXU_DTYPE)])


def _out_proj_gate_bwd(du_b, w_out, groups, pb):
    rows = pb.shape[0]
    w = GROUP_W * len(groups)

    def body(i, *refs):
        dg = _dot_nt(refs[0][...], refs[6][...])
        mixed = jnp.concatenate([r[...] for r in refs[1:5]], axis=1)
        g = refs[5][...]
        dm_ref, dgate_ref = refs[7], refs[8]
        sig = 1.0 / (1.0 + jnp.exp(-g))
        dm_ref[...] = dg * (g * sig)
        dgate_ref[...] = (dg * mixed * (sig * (1.0 + g * (1.0 - sig)))).astype(dgate_ref.dtype)

    return _rowwise(body, "out_proj_gate_bwd", rows, PROJ_BWD_ROWS,
                    [(du_b, du_b.shape[1], 0)] + [(o, GROUP_W, 0) for o in groups] + [(pb, w, 0)],
                    [w_out.astype(MXU_DTYPE)], [(w, F32), (w, MXU_DTYPE)])


def _tri(n, kind):
    r = np.arange(n)[:, None]
    c = np.arange(n)[None, :]
    m = {"lower_incl": r >= c, "upper_incl": r <= c, "row_gt_col": r > c, "row_lt_col": r < c}[kind]
    return jnp.asarray(m.astype(np.float32), dtype=MXU_DTYPE)


def _forget_fwd(pb, bias_row):
    rows = pb.shape[0]
    tile = _pick(rows, (1024, 512, 256))

    def body(i, x_ref, b_ref, l_ref, o_ref, carry_ref):
        @pl.when(i == 0)
        def _():
            carry_ref[...] = jnp.zeros_like(carry_ref)

        xx = x_ref[...] + b_ref[...]
        lane = lax.broadcasted_iota(jnp.int32, xx.shape, 1)
        lf = jnp.where(lane < N_HEADS, -_softplus(-xx), 0.0)
        o_ref[...] = _dot_exact_l(l_ref[...], lf) + carry_ref[...]
        carry_ref[...] += jnp.sum(lf, axis=0, keepdims=True)

    return _rowwise(body, "forget_fwd", rows, tile, [(pb, 128, PB_MISC_BLK)],
                    [bias_row, _tri(tile, "lower_incl")], [(128, F32)],
                    scratch=[pltpu.VMEM((1, 128), F32)], sequential=True)[0]


def _forget_bwd(pb, bias_row, dfc):
    rows = pb.shape[0]
    tile = _pick(rows, (1024, 512, 256))

    def body(i, x_ref, df_ref, b_ref, u_ref, o_ref, db_ref, carry_ref):
        @pl.when(i == 0)
        def _():
            carry_ref[...] = jnp.zeros_like(carry_ref)
            db_ref[...] = jnp.zeros_like(db_ref)

        df = df_ref[...]
        sfx = _dot_exact_l(u_ref[...], df) + carry_ref[...]
        carry_ref[...] += jnp.sum(df, axis=0, keepdims=True)
        xx = x_ref[...] + b_ref[...]
        lane = lax.broadcasted_iota(jnp.int32, xx.shape, 1)
        dl = jnp.where(lane < N_HEADS, sfx / (1.0 + jnp.exp(xx)), 0.0)
        o_ref[...] = dl
        db_ref[...] += jnp.sum(dl, axis=0, keepdims=True)

    return _rowwise(body, "forget_bwd", rows, tile,
                    [(pb, 128, PB_MISC_BLK), (dfc, 128, 0)],
                    [bias_row, _tri(tile, "upper_incl")], [(128, F32)], [((1, 128), F32)],
                    scratch=[pltpu.VMEM((1, 128), F32)], reverse=True, sequential=True)


def _rope_tables(s):
    half = MLA_ROPE // 2
    inv_freq = ROPE_THETA ** (-jnp.arange(half, dtype=F32) / half)
    ang = jnp.arange(s).astype(F32)[:, None] * inv_freq[None, :]
    cos2 = jnp.tile(jnp.cos(ang), (1, 2))
    sin2 = jnp.tile(jnp.sin(ang), (1, 2))
    cx = jnp.tile(cos2, (1, N_HEADS))
    sx = jnp.tile(sin2, (1, N_HEADS))
    pad = ((0, 0), (MISC_KROT, 128 - MISC_KROT - MLA_ROPE))
    ck = jnp.pad(cos2, pad)
    sk = jnp.pad(sin2, pad)
    return dict(ck=ck, sk=sk, cx=cx, sx=sx)


def _rot_matrix(width, bases):
    half = MLA_ROPE // 2
    p = np.zeros((width, width), np.float32)
    for b in bases:
        for i in range(half):
            p[b + half + i, b + i] = -1.0
            p[b + i, b + half + i] = 1.0
    return p


def _rope_matrices():
    pk = _rot_matrix(128, [MISC_KROT])
    p4 = _rot_matrix(128, [h * MLA_ROPE for h in range(N_HEADS)])
    a = np.zeros((128, 128), np.float32)
    for h in range(N_HEADS):
        for r in range(MLA_ROPE):
            a[h * MLA_ROPE + r, MISC_KROT + r] = 1.0
    cast = lambda m: jnp.asarray(m, dtype=MXU_DTYPE)
    return dict(p4=cast(p4), p4t=cast(p4.T), pk=cast(pk), spread=cast(a.T), xa=cast(a), xb=cast(p4.T @ a))


def _rms(c, g):
    r = lax.rsqrt(jnp.mean(c * c, axis=-1, keepdims=True) + RMS_EPS)
    return c * r * g


def _mla_q_fwd(pb, g, w_up, tabs, mats):
    rows = pb.shape[0]

    def body(i, c_ref, cos_ref, sin_ref, g_ref, w_ref, p_ref, q_ref, cn_ref):
        cn = _rms(c_ref[...], g_ref[...]).astype(cn_ref.dtype)
        cn_ref[...] = cn
        q = _dot(cn, w_ref[...])
        qr = q[:, GROUP_W:]
        q_ref[:, :GROUP_W] = q[:, :GROUP_W].astype(q_ref.dtype)
        q_ref[:, GROUP_W:] = (qr * cos_ref[...] + _dot_exact_r(qr, p_ref[...]) * sin_ref[...]).astype(q_ref.dtype)

    return _rowwise(body, "mla_q_fwd", rows, ROW_TILE,
                    [(pb, MLA_Q_RANK, PB_CQ_BLK), (tabs["cx"], PAIR, 0), (tabs["sx"], PAIR, 0)],
                    [g.reshape(1, MLA_Q_RANK), w_up.astype(MXU_DTYPE), mats["p4"]],
                    [(MLA_Q_COLS, MXU_DTYPE), (MLA_Q_RANK, MXU_DTYPE)])


def _mla_kv_fwd(pb, g, w_up, tabs, mats):
    rows = pb.shape[0]

    def body(i, c_ref, x_ref, cos_ref, sin_ref, g_ref, w_ref, p_ref, sp_ref, kv_ref, cn_ref, kr_ref):
        cn = _rms(c_ref[...], g_ref[...]).astype(cn_ref.dtype)
        cn_ref[...] = cn
        kv_ref[...] = _dot(cn, w_ref[...]).astype(kv_ref.dtype)
        xx = x_ref[...]
        kr = xx * cos_ref[...] + _dot_exact_r(xx, p_ref[...]) * sin_ref[...]
        kr_ref[...] = _dot_exact_r(kr, sp_ref[...]).astype(kr_ref.dtype)

    return _rowwise(body, "mla_kv_fwd", rows, ROW_TILE,
                    [(pb, MLA_KV_RANK, PB_CKV_BLK), (pb, 128, PB_MISC_BLK), (tabs["ck"], 128, 0), (tabs["sk"], 128, 0)],
                    [g.reshape(1, MLA_KV_RANK), w_up.astype(MXU_DTYPE), mats["pk"], mats["spread"]],
                    [(MLA_KV_COLS, MXU_DTYPE), (MLA_KV_RANK, MXU_DTYPE), (128, MXU_DTYPE)])


def _rope_q_bwd(dq_full, tabs, mats):
    rows, nq = dq_full.shape

    def body(i, d_ref, cos_ref, sin_ref, pt_ref, o_ref):
        dr = d_ref[:, GROUP_W:]
        o_ref[:, :GROUP_W] = d_ref[:, :GROUP_W].astype(o_ref.dtype)
        o_ref[:, GROUP_W:] = (dr * cos_ref[...] + _dot_exact_r(dr * sin_ref[...], pt_ref[...])).astype(o_ref.dtype)

    return _rowwise(body, "rope_q_bwd", rows, ROW_TILE,
                    [(dq_full, nq, 0), (tabs["cx"], PAIR, 0), (tabs["sx"], PAIR, 0)], [mats["p4t"]],
                    [(nq, MXU_DTYPE)])[0]


def _rope_k_bwd(dkr, tabs, mats):
    rows = dkr.shape[0]

    def body(i, d_ref, cos_ref, sin_ref, a_ref, b_ref, o_ref):
        d = d_ref[...]
        o_ref[...] = _dot_exact_r(d * cos_ref[...], a_ref[...]) + _dot_exact_r(d * sin_ref[...], b_ref[...])

    return _rowwise(body, "rope_k_bwd", rows, ROW_TILE,
                    [(dkr, 128, 0), (tabs["cx"], 128, 0), (tabs["sx"], 128, 0)], [mats["xa"], mats["xb"]],
                    [(128, F32)])[0]


def _rms_bwd(pb, width, col_blk, g, dy, name):
    rows = pb.shape[0]

    def body(i, c_ref, dy_ref, g_ref, dc_ref, dg_ref):
        @pl.when(i == 0)
        def _():
            dg_ref[...] = jnp.zeros_like(dg_ref)

        c = c_ref[...]
        dy = dy_ref[...]
        r = lax.rsqrt(jnp.mean(c * c, axis=-1, keepdims=True) + RMS_EPS)
        dyg = dy * g_ref[...]
        dc = r * dyg - c * (r * r * r) * jnp.mean(c * dyg, axis=-1, keepdims=True)
        dc_ref[...] = dc.astype(dc_ref.dtype)
        dg_ref[...] += jnp.sum(dy * c * r, axis=0, keepdims=True)

    return _rowwise(body, name, rows, ROW_TILE, [(pb, width, col_blk), (dy, width, 0)], [g.reshape(1, width)],
                    [(width, MXU_DTYPE)], [((1, width), F32)])


def _att_params(parallel):
    return pltpu.CompilerParams(dimension_semantics=("parallel" if parallel else "arbitrary",),
                                vmem_limit_bytes=VMEM_LIMIT)


def _blk_off(j, t):
    return j * t if isinstance(j, int) else pl.multiple_of(j * t, t)


def _causal_mask(t, strict):
    r = lax.broadcasted_iota(jnp.int32, (t, t), 0)
    c = lax.broadcasted_iota(jnp.int32, (t, t), 1)
    return (c < r) if strict else (c <= r)


def _lane_mask(kind, head, rows):
    lane = lax.broadcasted_iota(jnp.int32, (rows, PAIR), 1)
    if kind == "pair":
        return (lane < HEAD_DIM) if head % 2 == 0 else (lane >= HEAD_DIM)
    return (lane >= MLA_ROPE * head) & (lane < MLA_ROPE * (head + 1))


def _row_spec(t, cb, width=PAIR):
    return pl.BlockSpec((t, width), lambda i, cb=cb: (i, cb))


def _whole_spec(rows, cb, width=PAIR):
    return pl.BlockSpec((rows, width), lambda i, cb=cb: (0, cb), pipeline_mode=pl.Buffered(1))


def _is_pow2(x):
    return float(np.frexp(x)[0]) == 0.5


def _masked_heads(blocks, kinds, pair, dtype, scale=None):
    out = []
    for e in range(2):
        head = 2 * pair + e
        parts = [jnp.where(_lane_mask(k, head, b.shape[0]), b.astype(F32) * (1.0 if scale is None else scale),
                           0.0).astype(dtype)
                 for b, k in zip(blocks, kinds)]
        out.append(parts[0] if len(parts) == 1 else jnp.concatenate(parts, axis=1))
    return out


def _logit_reach(qh, kmax2, head):
    q32 = qh.astype(F32)
    return jnp.sqrt(jnp.sum(q32 * q32, axis=1, keepdims=True) * _col(kmax2, head)) * REACH_SLACK


def _forget_top(ft_ref, head, off):
    return jnp.max(-ft_ref[head:head + 1, pl.ds(off, PAIR)])


def _col(block, idx):
    lane = lax.broadcasted_iota(jnp.int32, block.shape, 1)
    return jnp.sum(jnp.where(lane == idx, block, 0.0), axis=1, keepdims=True)


def _scatter_cols(cols, t):
    lane = lax.broadcasted_iota(jnp.int32, (t, PAIR), 1)
    out = jnp.zeros((t, PAIR), F32)
    for idx, c in cols.items():
        out = out + jnp.where(lane == idx, c, 0.0)
    return out


def _take_heads(per_head, pair):
    return jnp.where(_lane_mask("pair", 0, per_head[0].shape[0]), per_head[0], per_head[1])


class _Parts:
    def __init__(self, q_parts, k_parts, v_parts, tq, sk):
        self.kinds = [[kind for _, _, kind in q_parts[p]] for p in range(2)]
        self.nparts = len(q_parts[0])
        self.q_specs = [_row_spec(tq, cb) for p in range(2) for _, cb, _ in q_parts[p]]
        self.q_args = [a for p in range(2) for a, _, _ in q_parts[p]]
        self.k_specs = [_whole_spec(sk, cb) for p in range(2) for _, cb, _ in k_parts[p]]
        self.k_args = [a for p in range(2) for a, _, _ in k_parts[p]]
        self.v_specs = [_whole_spec(sk, cb) for _, cb in v_parts]
        self.v_args = [a for a, _ in v_parts]
        self.width = PAIR * self.nparts

    def split(self, refs):
        n = self.nparts
        refs = list(refs)
        q = [refs[p * n:(p + 1) * n] for p in range(2)]
        k = [refs[2 * n + p * n:2 * n + (p + 1) * n] for p in range(2)]
        v = refs[4 * n:4 * n + 2]
        return q, k, v, refs[4 * n + 2:]

    def k_block(self, k_refs, off, t):
        blks = [r[pl.ds(off, t), :] for r in k_refs]
        return blks[0] if len(blks) == 1 else jnp.concatenate(blks, axis=1)


def _key_norm_max(src, k_blk):
    rows = src.shape[0]

    def body(i, k0_ref, k1_ref, o_ref):
        @pl.when(i == 0)
        def _():
            o_ref[...] = jnp.zeros_like(o_ref)

        cols = {}
        for p, ref in enumerate((k0_ref, k1_ref)):
            k32 = ref[...].astype(F32)
            for e in range(2):
                sq = jnp.sum(jnp.where(_lane_mask("pair", e, k32.shape[0]), k32 * k32, 0.0), axis=1, keepdims=True)
                cols[2 * p + e] = jnp.max(sq, axis=0, keepdims=True)
        lane = lax.broadcasted_iota(jnp.int32, (1, PAIR), 1)
        o_ref[...] = jnp.maximum(o_ref[...], sum(jnp.where(lane == h, c, 0.0) for h, c in cols.items()))

    return _rowwise(body, "key_norm_max", rows, ROW_TILE, [(src, PAIR, k_blk), (src, PAIR, k_blk + 1)], [], [],
                    [((1, PAIR), F32)])[0]


def _softmax_fwd(q_parts, k_parts, v_parts, sq, sk, scale, causal, bias, name):
    tq = min(SOFTMAX_TILE, sq)
    tk = tq if causal else min(SOFTMAX_TILE, sk)
    nkv = sk // tk
    pp = _Parts(q_parts, k_parts, v_parts, tq, sk)

    def kern(*refs):
        q_refs, k_refs, v_refs, rest = pp.split(refs)
        if bias is not None:
            fc_ref, ft_ref, kmax_ref, o_ref, lse_ref = rest
            fcb = fc_ref[...]
        else:
            o_ref, lse_ref = rest
        i = pl.program_id(0)
        fold = _is_pow2(scale)
        head_on = [jnp.where(_lane_mask("pair", e, tk), 1.0, 0.0).astype(MXU_DTYPE) for e in range(2)]
        head_off = [jnp.where(_lane_mask("pair", e, tk), 0.0, 1.0).astype(MXU_DTYPE) for e in range(2)]
        lse_cols = {}
        for first in range(0, 2, FWD_PAIRS_PER_LOOP):
            pairs = list(range(first, first + FWD_PAIRS_PER_LOOP))
            heads = [2 * p + e for p in pairs for e in range(2)]
            qm = {}
            for p in pairs:
                masked_q = _masked_heads([r[...] for r in q_refs[p]], pp.kinds[p], p, MXU_DTYPE, scale if fold else None)
                qm.update({2 * p + e: masked_q[e] for e in range(2)})
            if bias is not None:
                reach = {h: _logit_reach(qm[h], kmax_ref[...], h) for h in heads}

            def block(j, carry, masked, pairs=pairs, qm=qm):
                off = _blk_off(j, tk)
                out = []
                for p in pairs:
                    kb = pp.k_block(k_refs[p], off, tk)
                    vb = v_refs[p][pl.ds(off, tk), :]
                    for e in range(2):
                        h = 2 * p + e
                        m, acc = carry[len(out)]
                        s = _dot_nt(qm[h], kb) * (LOG2E if fold else scale * LOG2E)
                        if bias is not None:
                            s = s - ft_ref[h:h + 1, pl.ds(off, tk)] * LOG2E
                        if masked:
                            s = jnp.where(_causal_mask(tq, False), s, NEG)
                        m_new = jnp.maximum(m, jnp.max(s, axis=1, keepdims=True))
                        pr = jnp.exp2(s - m_new).astype(MXU_DTYPE)
                        out.append((m_new, jnp.exp2(m - m_new) * acc + _dot(pr, vb * head_on[e] + head_off[e])))
                return tuple(out)

            carry = tuple((jnp.full((tq, 1), NEG, F32), jnp.zeros((tq, PAIR), F32)) for _ in heads)
            if causal and bias is not None:
                def alive(c, j, heads=heads, reach=reach):
                    return functools.reduce(jnp.maximum, [
                        jnp.max(reach[h] + _forget_top(ft_ref, h, _blk_off(j, tk)) - c[n][0] * LN2)
                        for n, h in enumerate(heads)])

                def step(state, block=block, alive=alive):
                    n, _, c = state
                    c = block(i - 1 - n, c, False)
                    return n + 1, alive(c, i - 1 - n), c

                carry = block(i, carry, True)
                _, _, carry = lax.while_loop(lambda st: jnp.logical_and(st[0] < i, st[1] > DEAD_LOGIT), step,
                                             (jnp.int32(0), alive(carry, i), carry))
            elif causal:
                carry = lax.fori_loop(0, i, lambda j, c, block=block: block(j, c, False), carry)
                carry = block(i, carry, True)
            else:
                for j in range(nkv):
                    carry = block(j, carry, False)
            for n, p in enumerate(pairs):
                outs = []
                for e in range(2):
                    m, acc = carry[2 * n + e]
                    l = _col(acc, HEAD_DIM * (1 - e))
                    outs.append(acc / l)
                    lse_cols[2 * p + e] = m * LN2 + jnp.log(l) + (_col(fcb, 2 * p + e) if bias is not None else 0.0)
                o_ref[:, p * PAIR:(p + 1) * PAIR] = _take_heads(outs, p)
        lse_ref[...] = _scatter_cols(lse_cols, tq)

    in_specs = pp.q_specs + pp.k_specs + pp.v_specs
    args = pp.q_args + pp.k_args + pp.v_args
    if bias is not None:
        in_specs += [_row_spec(tq, 0), pl.BlockSpec((8, sk), lambda i: (0, 0), pipeline_mode=pl.Buffered(1)),
                     pl.BlockSpec((1, PAIR), lambda i: (0, 0))]
        args += list(bias)
    return pl.pallas_call(
        kern, name=name, grid=(sq // tq,), in_specs=in_specs,
        out_specs=[_row_spec(tq, 0, GROUP_W), _row_spec(tq, 0)],
        out_shape=[jax.ShapeDtypeStruct((sq, GROUP_W), F32), jax.ShapeDtypeStruct((sq, PAIR), F32)],
        compiler_params=_att_params(True),
    )(*args)


def _softmax_bwd(q_parts, k_parts, v_parts, o, lse, do, do_blk, sq, sk, scale, causal, bias, name):
    tq = min(SOFTMAX_TILE, sq)
    tk = tq if causal else min(SOFTMAX_TILE, sk)
    nkv = sk // tk
    pp = _Parts(q_parts, k_parts, v_parts, tq, sk)
    quad = pp.nparts == 2
    wq = GROUP_W + (PAIR if quad else 0)

    def kern(*refs):
        q_refs, k_refs, v_refs, rest = pp.split(refs)
        if bias is not None:
            o_ref, lse_ref, do_ref, fc_ref, ft_ref, kmax_ref, dq_ref, dk_ref, dv_ref, dfq_ref, dfk_ref = rest
            fcb = fc_ref[...]
        else:
            o_ref, lse_ref, do_ref, dq_ref, dk_ref, dv_ref = rest
        i = pl.program_id(0)
        fold = _is_pow2(scale)

        @pl.when(i == 0)
        def _():
            dk_ref[...] = jnp.zeros_like(dk_ref)
            dv_ref[...] = jnp.zeros_like(dv_ref)
            if bias is not None:
                dfk_ref[...] = jnp.zeros_like(dfk_ref)

        lse_b = lse_ref[...]
        qm, dom, delta, lse_h = [], [], [], []
        for p in range(2):
            qm += _masked_heads([r[...] for r in q_refs[p]], pp.kinds[p], p, MXU_DTYPE, scale if fold else None)
            do_p = do_ref[:, p * PAIR:(p + 1) * PAIR]
            dom += _masked_heads([do_p], ["pair"], p, MXU_DTYPE)
            prod = do_p * o_ref[:, p * PAIR:(p + 1) * PAIR]
            for e in range(2):
                h = 2 * p + e
                delta.append(jnp.sum(jnp.where(_lane_mask("pair", h, tq), prod, 0.0), axis=1, keepdims=True))
                lse_h.append(_col(lse_b, h) - (_col(fcb, h) if bias is not None else 0.0))

        def block(j, carry, masked):
            off = _blk_off(j, tk)
            out = []
            for p in range(2):
                kb = pp.k_block(k_refs[p], off, tk)
                vb = v_refs[p][pl.ds(off, tk), :]
                dk_acc = jnp.zeros((tk, pp.width), F32)
                dv_acc = jnp.zeros((tk, PAIR), F32)
                for e in range(2):
                    h = 2 * p + e
                    dq, dfq = carry[h]
                    s = _dot_nt(qm[h], kb)
                    if not fold:
                        s = s * scale
                    if bias is not None:
                        s = s - ft_ref[h:h + 1, pl.ds(off, tk)]
                    if masked:
                        s = jnp.where(_causal_mask(tq, False), s, NEG)
                    pr = jnp.exp(s - lse_h[h])
                    ds = pr * (_dot_nt(dom[h], vb) - delta[h])
                    dsb = (ds if fold else ds * scale).astype(MXU_DTYPE)
                    dv_acc = dv_acc + _dot_tn(pr.astype(MXU_DTYPE), dom[h])
                    dk_acc = dk_acc + _dot_tn(dsb, qm[h])
                    dq = dq + _dot(dsb, kb)
                    if bias is not None:
                        dfq = dfq + jnp.sum(ds, axis=1, keepdims=True)
                        dfk_ref[h:h + 1, pl.ds(off, tk)] -= jnp.sum(ds, axis=0, keepdims=True)
                    out.append((dq, dfq))
                dv_ref[pl.ds(off, tk), p * PAIR:(p + 1) * PAIR] += dv_acc
                dk_ref[pl.ds(off, tk), p * PAIR:(p + 1) * PAIR] += dk_acc[:, :PAIR]
                if quad:
                    dk_ref[pl.ds(off, tk), GROUP_W:] += dk_acc[:, PAIR:]
            return tuple(out)

        carry = tuple((jnp.zeros((tq, pp.width), F32), jnp.zeros((tq, 1), F32)) for _ in range(N_HEADS))
        if causal and bias is not None:
            reach = [_logit_reach(qm[h], kmax_ref[...], h) - lse_h[h] for h in range(N_HEADS)]

            def alive(j):
                return functools.reduce(jnp.maximum, [jnp.max(reach[h] + _forget_top(ft_ref, h, _blk_off(j, tk)))
                                                      for h in range(N_HEADS)])

            def step(state):
                n, _, c = state
                return n + 1, alive(i - 1 - n), block(i - 1 - n, c, False)

            carry = block(i, carry, True)
            _, _, carry = lax.while_loop(lambda st: jnp.logical_and(st[0] < i, st[1] > DEAD_LOGIT), step,
                                         (jnp.int32(0), alive(i), carry))
        elif causal:
            carry = lax.fori_loop(0, i, lambda j, c: block(j, c, False), carry)
            carry = block(i, carry, True)
        else:
            for j in range(nkv):
                carry = block(j, carry, False)
        dqs = [c[0] * scale if fold else c[0] for c in carry]
        for p in range(2):
            dq_ref[:, p * PAIR:(p + 1) * PAIR] = _take_heads([dqs[2 * p + e][:, :PAIR] for e in range(2)], p)
        if quad:
            dq_ref[:, GROUP_W:] = sum(jnp.where(_lane_mask("quad", h, tq), dqs[h][:, PAIR:], 0.0)
                                      for h in range(N_HEADS))
        if bias is not None:
            dfq_ref[...] = _scatter_cols({h: carry[h][1] for h in range(N_HEADS)}, tq)

    acc_spec = lambda rows, width: pl.BlockSpec((rows, width), lambda i: (0, 0), pipeline_mode=pl.Buffered(1))
    in_specs = pp.q_specs + pp.k_specs + pp.v_specs + [_row_spec(tq, 0, GROUP_W), _row_spec(tq, 0),
                                                       _row_spec(tq, do_blk, GROUP_W)]
    args = pp.q_args + pp.k_args + pp.v_args + [o, lse, do]
    out_specs = [_row_spec(tq, 0, wq), acc_spec(sk, wq), acc_spec(sk, GROUP_W)]
    out_shape = [jax.ShapeDtypeStruct((sq, wq), F32), jax.ShapeDtypeStruct((sk, wq), F32),
                 jax.ShapeDtypeStruct((sk, GROUP_W), F32)]
    if bias is not None:
        in_specs += [_row_spec(tq, 0), pl.BlockSpec((8, sk), lambda i: (0, 0), pipeline_mode=pl.Buffered(1)),
                     pl.BlockSpec((1, PAIR), lambda i: (0, 0))]
        args += list(bias)
        out_specs += [_row_spec(tq, 0), acc_spec(8, sk)]
        out_shape += [jax.ShapeDtypeStruct((sq, PAIR), F32), jax.ShapeDtypeStruct((8, sk), F32)]
    return pl.pallas_call(
        kern, name=name, grid=(sq // tq,), in_specs=in_specs, out_specs=out_specs, out_shape=out_shape,
        compiler_params=_att_params(False),
    )(*args)


def _sb_logs(qh, kb, valid):
    z = _dot_nt(qh, kb)
    sp = _softplus(z)
    lk = -sp
    if valid is not None:
        lk = jnp.where(valid, lk, 0.0)
    return lk, z - sp


def _sb_valid(d, tq, tk):
    r = lax.broadcasted_iota(jnp.int32, (tq, tk), 0)
    c = lax.broadcasted_iota(jnp.int32, (tq, tk), 1)
    return c + d * tk < r


def _tri_sums(xs, tri):
    t = xs[0].shape[0]
    pieces = [_split2(x) for x in xs]
    hi = _dot(jnp.concatenate([pc[0] for pc in pieces], axis=0), tri)
    lo = _dot(jnp.concatenate([pc[1] for pc in pieces], axis=0), tri)
    return [hi[n * t:(n + 1) * t] + lo[n * t:(n + 1) * t] for n in range(len(xs))]


def _sb_fwd(src, q_blk, k_blk, v_blk, s, scale, name):
    assert _is_pow2(scale)
    tq, t = min(SB_QUERY_TILE, s), min(ATT_TILE, s)
    slots = -(-(s // t) // SB_SLOT) * SB_SLOT
    width = N_HEADS * slots
    band = tq // t
    pair = lambda blk: [[(src, blk + p, "pair")] for p in range(2)]
    pp = _Parts(pair(q_blk), pair(k_blk), [(src, v_blk + p) for p in range(2)], tq, s)

    def kern(*refs):
        q_refs, k_refs, v_refs, (tri_ref, o_ref, rm_ref, cnt_ref) = pp.split(refs)
        i = pl.program_id(0)
        tri = tri_ref[...]
        lane = lax.broadcasted_iota(jnp.int32, (tq, width), 1)
        qm = []
        for p in range(2):
            qm += _masked_heads([q_refs[p][0][...]], ["pair"], p, MXU_DTYPE, scale)

        def block(j, carry, valid):
            accs, rights, rm = carry
            off = _blk_off(j, t)
            kbs = [k_refs[p][0][pl.ds(off, t), :] for p in range(2)]
            vbs = [v_refs[p][pl.ds(off, t), :] for p in range(2)]
            logs = [_sb_logs(qm[h], kbs[h // 2], valid) for h in range(N_HEADS)]
            tails = _tri_sums([lg[0] for lg in logs], tri)
            new_acc, new_right = [], []
            for h in range(N_HEADS):
                lk, ls = logs[h]
                w = jnp.exp(ls + tails[h] + rights[h])
                if valid is not None:
                    w = jnp.where(valid, w, 0.0)
                new_acc.append(accs[h] + _dot(w.astype(MXU_DTYPE), vbs[h // 2]))
                rm = rm + jnp.where(lane == slots * h + j, rights[h], 0.0)
                new_right.append(rights[h] + jnp.sum(lk, axis=1, keepdims=True))
            return tuple(new_acc), tuple(new_right), rm

        carry = (tuple(jnp.zeros((tq, PAIR), F32) for _ in range(N_HEADS)),
                 tuple(jnp.zeros((tq, 1), F32) for _ in range(N_HEADS)), jnp.zeros((tq, width), F32))
        for d in reversed(range(band)):
            carry = block(band * i + d, carry, _sb_valid(d, tq, t))

        def alive(c):
            return functools.reduce(jnp.maximum, [jnp.max(r) for r in c[1]])

        def step(state):
            n, _, c = state
            c = block(band * i - 1 - n, c, None)
            return n + 1, alive(c), c

        n_done, _, carry = lax.while_loop(lambda st: jnp.logical_and(st[0] < band * i, st[1] > EXP_UNDERFLOW),
                                          step, (jnp.int32(0), alive(carry), carry))
        cnt_ref[i] = n_done
        for p in range(2):
            o_ref[:, p * PAIR:(p + 1) * PAIR] = _take_heads([carry[0][2 * p + e] for e in range(2)], p)
        rm_ref[...] = carry[2]

    return pl.pallas_call(
        kern, name=name, grid=(s // tq,),
        in_specs=pp.q_specs + pp.k_specs + pp.v_specs + [pl.BlockSpec((t, t), lambda i: (0, 0))],
        out_specs=[_row_spec(tq, 0, GROUP_W), _row_spec(tq, 0, width), pl.BlockSpec(memory_space=pltpu.SMEM)],
        out_shape=[jax.ShapeDtypeStruct((s, GROUP_W), F32), jax.ShapeDtypeStruct((s, width), F32),
                   jax.ShapeDtypeStruct((s // tq,), jnp.int32)],
        compiler_params=_att_params(False),
    )(*(pp.q_args + pp.k_args + pp.v_args + [_tri(t, "row_gt_col")]))


def _sb_bwd(src, q_blk, k_blk, v_blk, do, do_blk, rm, visited, s, scale, name):
    assert _is_pow2(scale)
    tq, t = min(SB_QUERY_TILE, s), min(ATT_TILE, s)
    band = tq // t
    pair = lambda blk: [[(src, blk + p, "pair")] for p in range(2)]
    pp = _Parts(pair(q_blk), pair(k_blk), [(src, v_blk + p) for p in range(2)], tq, s)

    def kern(*refs):
        q_refs, k_refs, v_refs, (do_ref, rm_ref, tri_ref, pre_ref, cnt_ref, dq_ref, dk_ref, dv_ref) = pp.split(refs)
        i = pl.program_id(0)

        @pl.when(i == 0)
        def _():
            dk_ref[...] = jnp.zeros_like(dk_ref)
            dv_ref[...] = jnp.zeros_like(dv_ref)

        rmb = rm_ref[...]
        tri = tri_ref[...]
        pre = pre_ref[...]
        qm, dom = [], []
        for p in range(2):
            qm += _masked_heads([q_refs[p][0][...]], ["pair"], p, MXU_DTYPE, scale)
            dom += _masked_heads([do_ref[:, p * PAIR:(p + 1) * PAIR]], ["pair"], p, MXU_DTYPE)

        def block(j, carry, valid):
            dqs, lefts = carry
            off = _blk_off(j, t)
            kbs = [k_refs[p][0][pl.ds(off, t), :] for p in range(2)]
            vbs = [v_refs[p][pl.ds(off, t), :] for p in range(2)]
            logs = [_sb_logs(qm[h], kbs[h // 2], valid) for h in range(N_HEADS)]
            tails = _tri_sums([lg[0] for lg in logs], tri)
            ws, gs = [], []
            for h in range(N_HEADS):
                lk, ls = logs[h]
                w = jnp.exp(ls + tails[h] + _col(rmb, (rm.shape[1] // N_HEADS) * h + j))
                if valid is not None:
                    w = jnp.where(valid, w, 0.0)
                ws.append(w)
                gs.append(_dot_nt(dom[h], vbs[h // 2]) * w)
            prefix = _tri_sums(gs, pre)
            new_dq, new_left = [], []
            dk_acc = [jnp.zeros((t, PAIR), F32) for _ in range(2)]
            dv_acc = [jnp.zeros((t, PAIR), F32) for _ in range(2)]
            for h in range(N_HEADS):
                lk, ls = logs[h]
                sig = jnp.exp(ls)
                dz = gs[h] * (1.0 - sig) - sig * (prefix[h] + lefts[h])
                if valid is not None:
                    dz = jnp.where(valid, dz, 0.0)
                dzb = dz.astype(MXU_DTYPE)
                dv_acc[h // 2] = dv_acc[h // 2] + _dot_tn(ws[h].astype(MXU_DTYPE), dom[h])
                dk_acc[h // 2] = dk_acc[h // 2] + _dot_tn(dzb, qm[h])
                new_dq.append(dqs[h] + _dot(dzb, kbs[h // 2]))
                new_left.append(lefts[h] + jnp.sum(gs[h], axis=1, keepdims=True))
            for p in range(2):
                dv_ref[pl.ds(off, t), p * PAIR:(p + 1) * PAIR] += dv_acc[p]
                dk_ref[pl.ds(off, t), p * PAIR:(p + 1) * PAIR] += dk_acc[p]
            return tuple(new_dq), tuple(new_left)

        carry = (tuple(jnp.zeros((tq, PAIR), F32) for _ in range(N_HEADS)),
                 tuple(jnp.zeros((tq, 1), F32) for _ in range(N_HEADS)))
        carry = lax.fori_loop(band * i - cnt_ref[i], band * i, lambda j, c: block(j, c, None), carry)
        for d in range(band):
            carry = block(band * i + d, carry, _sb_valid(d, tq, t))
        for p in range(2):
            dq_ref[:, p * PAIR:(p + 1) * PAIR] = _take_heads([carry[0][2 * p + e] * scale for e in range(2)], p)

    mspec = pl.BlockSpec((t, t), lambda i: (0, 0))
    acc_spec = pl.BlockSpec((s, GROUP_W), lambda i: (0, 0), pipeline_mode=pl.Buffered(1))
    return pl.pallas_call(
        kern, name=name, grid=(s // tq,),
        in_specs=pp.q_specs + pp.k_specs + pp.v_specs + [_row_spec(tq, do_blk, GROUP_W), _row_spec(tq, 0, rm.shape[1]), mspec, mspec,
                                                         pl.BlockSpec(memory_space=pltpu.SMEM)],
        out_specs=[_row_spec(tq, 0, GROUP_W), acc_spec, acc_spec],
        out_shape=[jax.ShapeDtypeStruct((s, GROUP_W), F32)] * 3,
        compiler_params=_att_params(False),
    )(*(pp.q_args + pp.k_args + pp.v_args + [do, rm, _tri(t, "row_gt_col"), _tri(t, "row_lt_col"), visited]))


def _split_w_in(w):
    col = lambda n: w[:, _OFF[n]:_OFF[n + 1]]
    wa = jnp.concatenate([col(0), col(1), col(2), col(4), col(5), col(6), col(10)], axis=1)
    misc = jnp.concatenate([col(3), col(9), jnp.zeros((w.shape[0], 128 - 4 - MLA_ROPE), w.dtype)], axis=1)
    wb = jnp.concatenate([col(11), col(7), col(8), misc], axis=1)
    return wa, wb


def _merge_dw_in(dwp):
    a = lambda n: dwp[:, n * GROUP_W:(n + 1) * GROUP_W]
    b0 = PA_COLS
    gate = dwp[:, b0:b0 + 1024]
    cq = dwp[:, b0 + 1024:b0 + 1280]
    ckv = dwp[:, b0 + 1280:b0 + 1408]
    flog = dwp[:, b0 + 1408:b0 + 1412]
    krot = dwp[:, b0 + 1408 + MISC_KROT:b0 + 1408 + MISC_KROT + MLA_ROPE]
    return jnp.concatenate([a(0), a(1), a(2), flog, a(3), a(4), a(5), cq, ckv, krot, a(6), gate], axis=1)


def _heads_first(w, per_head, first):
    r = w.shape[0]
    w3 = w.reshape(r, N_HEADS, per_head)
    return jnp.concatenate([w3[:, :, :first].reshape(r, -1), w3[:, :, first:].reshape(r, -1)], axis=1)


def _heads_interleaved(w, per_head, first):
    r = w.shape[0]
    a = w[:, :N_HEADS * first].reshape(r, N_HEADS, first)
    b = w[:, N_HEADS * first:].reshape(r, N_HEADS, per_head - first)
    return jnp.concatenate([a, b], axis=2).reshape(r, N_HEADS * per_head)


def _pad_rows8(a):
    return a[:, :8].T


def _local_step(x2, mem2, tgt, p):
    s = x2.shape[0]
    nm = mem2.shape[0]
    head_scale = HEAD_DIM ** -0.5
    mla_scale = (HEAD_DIM + MLA_ROPE) ** -0.5
    tabs = _rope_tables(s)
    mats = _rope_matrices()
    pairs = lambda arr, blk: [[(arr, blk + q, "pair")] for q in range(2)]
    vals = lambda arr, blk: [(arr, blk + q) for q in range(2)]

    h, hb = _ln_fwd(x2, None, p["ln_in_g"], p["ln_in_b"], "ln_in_fwd")
    _, memn_b = _ln_fwd(mem2, None, p["mem_ln_g"], p["mem_ln_b"], "ln_mem_fwd")

    saved = []
    for l in range(DEPTH):
        wa, wb = _split_w_in(p["w_in"][l])
        wp = jnp.concatenate([wa, wb], axis=1)
        wq_up = _heads_first(p["w_mla_q_up"][l], HEAD_DIM + MLA_ROPE, HEAD_DIM)
        wkv_up = _heads_first(p["w_mla_kv_up"][l], 2 * HEAD_DIM, HEAD_DIM)
        bias_row = jnp.pad(p["b_forget"][l], (0, 128 - N_HEADS)).reshape(1, 128)
        pa, pb = _proj_in(hb, wa, wb)

        fc = _forget_fwd(pb, bias_row)
        fbias = (fc, _pad_rows8(fc), _key_norm_max(pa, 2))
        o_fox, lse_fox = _softmax_fwd(pairs(pa, 0), pairs(pa, 2), vals(pa, 4), s, s, head_scale, True, fbias,
                                      "fox_fwd")
        o_sb, *rm_sb = _sb_fwd(pa, 6, 8, 10, s, head_scale, "sb_fwd")

        qfull, cqn = _mla_q_fwd(pb, p["mla_q_norm_g"][l], wq_up, tabs, mats)
        kv, ckvn, kr4 = _mla_kv_fwd(pb, p["mla_kv_norm_g"][l], wkv_up, tabs, mats)
        mla_q = [[(qfull, q, "pair"), (qfull, 2, "quad")] for q in range(2)]
        mla_k = [[(kv, q, "pair"), (kr4, 0, "quad")] for q in range(2)]
        o_mla, lse_mla = _softmax_fwd(mla_q, mla_k, vals(kv, 2), s, s, mla_scale, True, None, "mla_fwd")

        mkv = _matmul(memn_b, p["w_mem_kv"][l], MXU_DTYPE, "mem_kv")
        o_mem, lse_mem = _softmax_fwd(pairs(pa, 12), pairs(mkv, 0), vals(mkv, 2), s, nm, head_scale, False, None,
                                      "mem_fwd")

        groups = (o_fox, o_sb, o_mla, o_mem)
        if l < DEPTH - 1:
            gated, u, h_next, hb_next = _gate_out_proj_ln(groups, pb, p["w_out"][l], h, p["ln_g"][l], p["ln_b"][l])
        else:
            gated, u, dh, sq_cols = _gate_out_proj_ln(groups, pb, p["w_out"][l], h, p["ln_g"][l], p["ln_b"][l], tgt)
        saved.append(dict(u=u, hb=hb, wp=wp, wq_up=wq_up, wkv_up=wkv_up, bias_row=bias_row, pa=pa, pb=pb,
                          fbias=fbias, lse_fox=lse_fox, rm_sb=rm_sb, cqn=cqn, ckvn=ckvn, mla_q=mla_q, mla_k=mla_k,
                          kv=kv, lse_mla=lse_mla, mkv=mkv, lse_mem=lse_mem, groups=groups, gated=gated))
        if l < DEPTH - 1:
            h, hb = h_next, hb_next

    loss_sum = jnp.sum(sq_cols)

    grads = {k: [None] * DEPTH for k in ("w_in", "b_forget", "mla_q_norm_g", "w_mla_q_up", "mla_kv_norm_g",
                                         "w_mla_kv_up", "w_mem_kv", "w_out", "ln_g", "ln_b")}
    dmemn = []
    dy1, dy2, c1 = dh, None, 1.0
    for l in reversed(range(DEPTH)):
        r = saved[l]
        pa, pb = r["pa"], r["pb"]
        o_fox, o_sb, o_mla, o_mem = r["groups"]
        du, du_b, dg, db = _ln_bwd(r["u"], None, p["ln_g"][l], dy1, dy2, c1, "ln_bwd")
        grads["ln_g"][l], grads["ln_b"][l] = dg[0], db[0]
        grads["w_out"][l] = _matmul(r["gated"], du_b, F32, "out_proj_dw", "tn")
        dmixed, dgate_b = _out_proj_gate_bwd(du_b, p["w_out"][l], r["groups"], pb)

        dfq, dfk, dfv, dfc_q, dfc_k = _softmax_bwd(pairs(pa, 0), pairs(pa, 2), vals(pa, 4), o_fox, r["lse_fox"],
                                                   dmixed, 0, s, s, head_scale, True, r["fbias"], "fox_bwd")
        dmisc_f, dbf = _forget_bwd(pb, r["bias_row"], dfc_q + jnp.pad(dfc_k.T, ((0, 0), (0, 128 - 8))))
        grads["b_forget"][l] = dbf[0, :N_HEADS]

        dsq, dsk, dsv = _sb_bwd(pa, 6, 8, 10, dmixed, 1, *r["rm_sb"], s, head_scale, "sb_bwd")

        dqm, dkm, dvm = _softmax_bwd(r["mla_q"], r["mla_k"], vals(r["kv"], 2), o_mla, r["lse_mla"], dmixed, 2,
                                     s, s, mla_scale, True, None, "mla_bwd")
        dq_mla_b = _rope_q_bwd(dqm, tabs, mats)
        dcqn = _matmul(dq_mla_b, r["wq_up"], F32, "q_up_dx", "nt")
        grads["w_mla_q_up"][l] = _heads_interleaved(_matmul(r["cqn"], dq_mla_b, F32, "q_up_dw", "tn"),
                                                    HEAD_DIM + MLA_ROPE, HEAD_DIM)
        dcq_b, dgq = _rms_bwd(pb, MLA_Q_RANK, PB_CQ_BLK, p["mla_q_norm_g"][l], dcqn, "rms_q_bwd")
        grads["mla_q_norm_g"][l] = dgq[0]
        dkv_b = jnp.concatenate([dkm[:, :GROUP_W], dvm], axis=1).astype(MXU_DTYPE)
        dckvn = _matmul(dkv_b, r["wkv_up"], F32, "kv_up_dx", "nt")
        grads["w_mla_kv_up"][l] = _heads_interleaved(_matmul(r["ckvn"], dkv_b, F32, "kv_up_dw", "tn"),
                                                     2 * HEAD_DIM, HEAD_DIM)
        dckv_b, dgkv = _rms_bwd(pb, MLA_KV_RANK, PB_CKV_BLK, p["mla_kv_norm_g"][l], dckvn, "rms_kv_bwd")
        grads["mla_kv_norm_g"][l] = dgkv[0]
        dmisc_k = _rope_k_bwd(dkm[:, GROUP_W:], tabs, mats)

        dmq, dmk, dmv = _softmax_bwd(pairs(pa, 12), pairs(r["mkv"], 0), vals(r["mkv"], 2), o_mem, r["lse_mem"],
                                     dmixed, 3, s, nm, head_scale, False, None, "mem_bwd")
        dmkv_b = jnp.concatenate([dmk, dmv], axis=1).astype(MXU_DTYPE)
        grads["w_mem_kv"][l] = _matmul(memn_b, dmkv_b, F32, "mem_kv_dw", "tn")
        dmemn.append(_matmul(dmkv_b, p["w_mem_kv"][l], F32, "mem_kv_dx", "nt"))

        dp = [dfq, dfk, dfv, dsq, dsk, dsv, dmq, dgate_b, dcq_b, dckv_b, (dmisc_f, dmisc_k)]
        dhproj = _proj_dx(dp, r["wp"])
        grads["w_in"][l] = _merge_dw_in(_proj_dw(r["hb"], dp))
        dy1, dy2, c1 = du, dhproj, ALPHA

    dx, _, dg_in, db_in = _ln_bwd(x2, None, p["ln_in_g"], dy1, dy2, c1, "ln_in_bwd")
    _, _, dg_mem, db_mem = _ln_bwd(mem2, None, p["mem_ln_g"], dmemn[0], dmemn[1], 1.0, "ln_mem_bwd")
    out = {k: jnp.stack(v) for k, v in grads.items()}
    out.update(ln_in_g=dg_in[0], ln_in_b=db_in[0], mem_ln_g=dg_mem[0], mem_ln_b=db_mem[0])
    return loss_sum, dx, out


WIDE = "w_in"
FLAT_NAMES = ("w_out", "w_mem_kv", "w_mla_q_up", "w_mla_kv_up")
FLAT_ROWS = 896
BIG_NAMES = (WIDE,) + FLAT_NAMES
BIG_AXIS = dict(w_in=2, w_out=1, w_mem_kv=1, w_mla_q_up=2, w_mla_kv_up=2)
SMALL_NAMES = ("ln_in_g", "ln_in_b", "mem_ln_g", "mem_ln_b", "ln_g", "ln_b", "b_forget", "mla_q_norm_g",
               "mla_kv_norm_g")
ALL_NAMES = ("ln_in_g", "ln_in_b", "mem_ln_g", "mem_ln_b", "w_in", "b_forget", "mla_q_norm_g", "w_mla_q_up",
             "mla_kv_norm_g", "w_mla_kv_up", "w_mem_kv", "w_out", "ln_g", "ln_b")
N_CHIPS = 4
N_DEV = 8


def _rows_of(shape):
    rows = -(-int(np.prod(shape)) // LANES)
    return -(-rows // PACK_ALIGN) * PACK_ALIGN


def _pack(arrs, rows):
    parts = []
    for a in arrs:
        f = a.reshape(-1)
        n = _rows_of(a.shape) * LANES
        parts.append(jnp.pad(f, (0, n - f.shape[0])).reshape(-1, LANES))
    used = sum(q.shape[0] for q in parts)
    if rows > used:
        parts.append(jnp.zeros((rows - used, LANES), parts[0].dtype))
    return jnp.concatenate(parts, axis=0)


def _unpack(buf, shapes):
    out, r = [], 0
    for shp in shapes:
        n = _rows_of(shp)
        out.append(buf[r:r + n].reshape(-1)[:int(np.prod(shp))].reshape(shp))
        r += n
    return out


def _sharded_pair(get):
    wide = get(WIDE)
    return [wide.reshape(-1, wide.shape[-1]), _pack([get(n) for n in FLAT_NAMES], FLAT_ROWS)]


HBM_SPEC = pl.BlockSpec(memory_space=pltpu.HBM)


def _gather_weights(shards):
    n = len(shards)

    def body(*refs):
        w_refs, out_refs, (send_sems, recv_sems, local_sems) = refs[:n], refs[n:2 * n], refs[2 * n:]
        x, y, c = (lax.axis_index(a) for a in MESH_AXES)
        me, sibling = 2 * x + y, (x, y, 1 - c)
        chips = [(1 - x, y), (x, 1 - y), (1 - x, 1 - y)]
        local, first, passed = [], [], []
        for a in range(n):
            w_ref, out_ref, half = w_refs[a], out_refs[a], shards[a].shape[0] // 2

            def part(chip, core, out_ref=out_ref, half=half):
                return out_ref.at[chip, pl.ds(core * half, half)]

            def copy(k, src, dst, to, a=a):
                return pltpu.make_async_remote_copy(
                    src_ref=src, dst_ref=dst, send_sem=send_sems.at[6 * a + k], recv_sem=recv_sems.at[6 * a + k],
                    device_id=to, device_id_type=pl.DeviceIdType.MESH)

            local.append(pltpu.make_async_copy(w_ref, out_ref.at[me], local_sems.at[a]))
            local[-1].start()
            mine = [copy(k, w_ref.at[pl.ds(c * half, half)], part(me, c), (px, py, c))
                    for k, (px, py) in enumerate(chips)]
            for cp in mine:
                cp.start()
            first.append((mine, part, copy))
        for mine, part, copy in first:
            for k, (px, py) in enumerate(chips):
                copy(k, part(me, c), part(2 * px + py, c), (px, py, c)).wait_recv()
                passed.append(copy(3 + k, part(2 * px + py, c), part(2 * px + py, c), sibling))
                passed[-1].start()
        for mine, part, copy in first:
            for k, (px, py) in enumerate(chips):
                copy(3 + k, part(me, c), part(2 * px + py, 1 - c), sibling).wait_recv()
        for cp in [cp for mine, _, _ in first for cp in mine] + passed:
            cp.wait_send()
        for cp in local:
            cp.wait()

    return pl.pallas_call(
        body, name="gather_weights",
        out_shape=[jax.ShapeDtypeStruct((N_CHIPS,) + s.shape, s.dtype) for s in shards],
        in_specs=[HBM_SPEC] * n, out_specs=[HBM_SPEC] * n,
        scratch_shapes=[pltpu.SemaphoreType.DMA((6 * n,)), pltpu.SemaphoreType.DMA((6 * n,)),
                        pltpu.SemaphoreType.DMA((n,))],
    )(*shards)


def _exchange_grads(bigs, small):
    nb = len(bigs)
    halves = [b.shape[1] // 2 for b in bigs]
    chunks = [_pick(h, (128, 64, 32, 16)) for h in halves]

    def body(*refs):
        big_refs, small_ref = refs[:nb], refs[nb]
        sum_refs, out_refs = refs[nb + 1:2 * nb + 1], refs[2 * nb + 1:3 * nb + 2]
        scratch = refs[3 * nb + 2:]
        land_bufs, sum_bufs = scratch[:nb], scratch[nb:2 * nb]
        send_sems, recv_sems, local_sems, load_sem, swap_send, swap_recv, keep_sems = scratch[2 * nb:]
        x, y, c = (lax.axis_index(a) for a in MESH_AXES)
        me, my_chip = 4 * x + 2 * y + c, 2 * x + y
        flips = [(fx, fy, fc) for fx in (0, 1) for fy in (0, 1) for fc in (0, 1) if fx or fy or fc]
        peers = [(1 - x if fx else x, 1 - y if fy else y, 1 - c if fc else c) for fx, fy, fc in flips]

        sources = [lambda chip, core, r=big_refs[a], half=bigs[a].shape[1] // 2: r.at[chip, pl.ds(core * half, half)]
                   for a in range(nb)] + [lambda chip, core: small_ref]

        def copy(a, k, src, slot, to):
            return pltpu.make_async_remote_copy(
                src_ref=src, dst_ref=out_refs[a].at[slot], send_sem=send_sems.at[7 * a + k],
                recv_sem=recv_sems.at[7 * a + k], device_id=to, device_id_type=pl.DeviceIdType.MESH)

        own = [pltpu.make_async_copy(src(my_chip, c), out_refs[a].at[me], local_sems.at[a])
               for a, src in enumerate(sources)]
        for cp in own:
            cp.start()
        sends = [copy(a, k, src(2 * px + py, pc), me, (px, py, pc))
                 for a, src in enumerate(sources) for k, (px, py, pc) in enumerate(peers)]
        for cp in sends:
            cp.start()
        for a, src in enumerate(sources):
            for k, (px, py, pc) in enumerate(peers):
                copy(a, k, src(my_chip, c), 4 * px + 2 * py + pc, (px, py, pc)).wait_recv()
        for cp in sends:
            cp.wait_send()
        for cp in own:
            cp.wait()

        tails = []
        for a in range(nb):
            for r0 in range(0, halves[a], chunks[a]):
                load = pltpu.make_async_copy(out_refs[a].at[:, pl.ds(r0, chunks[a])], land_bufs[a], load_sem)
                load.start()
                load.wait()
                total = land_bufs[a][0].astype(F32)
                for d in range(1, N_DEV):
                    total = total + land_bufs[a][d].astype(F32)
                sum_bufs[a][pl.ds(r0, chunks[a]), :] = total
            keep = pltpu.make_async_copy(sum_bufs[a], sum_refs[a].at[c], keep_sems.at[a])
            give = pltpu.make_async_remote_copy(
                src_ref=sum_bufs[a], dst_ref=sum_refs[a].at[c], send_sem=swap_send.at[a], recv_sem=swap_recv.at[a],
                device_id=(x, y, 1 - c), device_id_type=pl.DeviceIdType.MESH)
            keep.start()
            give.start()
            tails.append((keep, give))
        for a, (keep, give) in enumerate(tails):
            pltpu.make_async_remote_copy(
                src_ref=sum_bufs[a], dst_ref=sum_refs[a].at[1 - c], send_sem=swap_send.at[a], recv_sem=swap_recv.at[a],
                device_id=(x, y, 1 - c), device_id_type=pl.DeviceIdType.MESH).wait_recv()
            give.wait_send()
            keep.wait()

    return pl.pallas_call(
        body, name="exchange_grads",
        out_shape=[jax.ShapeDtypeStruct((2, h, b.shape[2]), F32) for h, b in zip(halves, bigs)]
        + [jax.ShapeDtypeStruct((N_DEV, h, b.shape[2]), b.dtype) for h, b in zip(halves, bigs)]
        + [jax.ShapeDtypeStruct((N_DEV,) + small.shape, small.dtype)],
        in_specs=[HBM_SPEC] * (nb + 1), out_specs=[HBM_SPEC] * (2 * nb + 1),
        scratch_shapes=[pltpu.VMEM((N_DEV, ch, b.shape[2]), b.dtype) for ch, b in zip(chunks, bigs)]
        + [pltpu.VMEM((h, b.shape[2]), F32) for h, b in zip(halves, bigs)]
        + [pltpu.SemaphoreType.DMA((7 * (nb + 1),)), pltpu.SemaphoreType.DMA((7 * (nb + 1),)),
           pltpu.SemaphoreType.DMA((nb + 1,)), pltpu.SemaphoreType.DMA, pltpu.SemaphoreType.DMA((nb,)),
           pltpu.SemaphoreType.DMA((nb,)), pltpu.SemaphoreType.DMA((nb,))],
        compiler_params=pltpu.CompilerParams(vmem_limit_bytes=VMEM_LIMIT),
    )(*bigs, small)


def _adamw(parts, w, m, v, name):
    rows, width = w.shape
    n_parts = parts.shape[0]
    tile = _pick(rows, (128, 16, 8))
    bc1 = 1.0 - ADAM_B1 ** ADAM_STEP
    bc2 = 1.0 - ADAM_B2 ** ADAM_STEP

    def kern(p_ref, w_ref, m_ref, v_ref, g_ref, d_ref, nm_ref, nv_ref):
        g = p_ref[0].astype(F32)
        for d in range(1, n_parts):
            g = g + p_ref[d].astype(F32)
        nm = ADAM_B1 * m_ref[...] + (1.0 - ADAM_B1) * g
        nv = ADAM_B2 * v_ref[...] + (1.0 - ADAM_B2) * (g * g)
        g_ref[...] = g
        nm_ref[...] = nm
        nv_ref[...] = nv
        d_ref[...] = -ADAM_LR * ((nm / bc1) / (jnp.sqrt(nv / bc2) + ADAM_EPS) + ADAM_WD * w_ref[...])

    spec = pl.BlockSpec((tile, width), lambda i: (i, 0))
    return pl.pallas_call(
        kern, name=name, grid=(rows // tile,),
        in_specs=[pl.BlockSpec((n_parts, tile, width), lambda i: (0, i, 0)), spec, spec, spec],
        out_specs=[spec] * 4, out_shape=[jax.ShapeDtypeStruct((rows, width), F32)] * 4,
        compiler_params=pltpu.CompilerParams(dimension_semantics=("parallel",), vmem_limit_bytes=VMEM_LIMIT),
    )(parts, w, m, v)


def kernel(x, mem, ln_in_g, ln_in_b, mem_ln_g, mem_ln_b, w_in, b_forget, mla_q_norm_g, w_mla_q_up, mla_kv_norm_g, w_mla_kv_up, w_mem_kv, w_out, ln_g, ln_b, loss_target, m_ln_in_g, m_ln_in_b, m_mem_ln_g, m_mem_ln_b, m_w_in, m_b_forget, m_mla_q_norm_g, m_w_mla_q_up, m_mla_kv_norm_g, m_w_mla_kv_up, m_w_mem_kv, m_w_out, m_ln_g, m_ln_b, v_ln_in_g, v_ln_in_b, v_mem_ln_g, v_mem_ln_b, v_w_in, v_b_forget, v_mla_q_norm_g, v_w_mla_q_up, v_mla_kv_norm_g, v_w_mla_kv_up, v_w_mem_kv, v_w_out, v_ln_g, v_ln_b):
    w = dict(ln_in_g=ln_in_g, ln_in_b=ln_in_b, mem_ln_g=mem_ln_g, mem_ln_b=mem_ln_b, w_in=w_in, b_forget=b_forget,
             mla_q_norm_g=mla_q_norm_g, w_mla_q_up=w_mla_q_up, mla_kv_norm_g=mla_kv_norm_g,
             w_mla_kv_up=w_mla_kv_up, w_mem_kv=w_mem_kv, w_out=w_out, ln_g=ln_g, ln_b=ln_b)
    mo = dict(ln_in_g=m_ln_in_g, ln_in_b=m_ln_in_b, mem_ln_g=m_mem_ln_g, mem_ln_b=m_mem_ln_b, w_in=m_w_in,
              b_forget=m_b_forget, mla_q_norm_g=m_mla_q_norm_g, w_mla_q_up=m_w_mla_q_up,
              mla_kv_norm_g=m_mla_kv_norm_g, w_mla_kv_up=m_w_mla_kv_up, w_mem_kv=m_w_mem_kv, w_out=m_w_out,
              ln_g=m_ln_g, ln_b=m_ln_b)
    vo = dict(ln_in_g=v_ln_in_g, ln_in_b=v_ln_in_b, mem_ln_g=v_mem_ln_g, mem_ln_b=v_mem_ln_b, w_in=v_w_in,
              b_forget=v_b_forget, mla_q_norm_g=v_mla_q_norm_g, w_mla_q_up=v_w_mla_q_up,
              mla_kv_norm_g=v_mla_kv_norm_g, w_mla_kv_up=v_w_mla_kv_up, w_mem_kv=v_w_mem_kv, w_out=v_w_out,
              ln_g=v_ln_g, ln_b=v_ln_b)
    flat_shapes = [w[n].shape for n in FLAT_NAMES]
    small_shapes = [w[n].shape for n in SMALL_NAMES]

    got_wide, got_flat = _gather_weights(_sharded_pair(lambda n: w[n].astype(MXU_DTYPE)))
    full = dict(w)
    full[WIDE] = jnp.concatenate([got_wide[j] for j in range(N_CHIPS)], axis=1).reshape(
        w[WIDE].shape[:2] + (N_CHIPS * w[WIDE].shape[2],))
    per_chip = [_unpack(got_flat[j], flat_shapes) for j in range(N_CHIPS)]
    for idx, n in enumerate(FLAT_NAMES):
        full[n] = jnp.concatenate([per_chip[j][idx] for j in range(N_CHIPS)], axis=BIG_AXIS[n])

    loss_sum, dx, g = _local_step(x[0], mem[0], loss_target[0], full)
    loss = lax.psum(loss_sum * (0.5 / D_MODEL), MESH_AXES)

    def shard_of(n, j):
        ax, size = BIG_AXIS[n], w[n].shape[BIG_AXIS[n]]
        return lax.slice_in_dim(g[n], j * size, (j + 1) * size, axis=ax).astype(MXU_DTYPE)

    per_dest = [_sharded_pair(lambda n, j=j: shard_of(n, j)) for j in range(N_CHIPS)]
    bigs = [jnp.stack([per_dest[j][a] for j in range(N_CHIPS)]) for a in range(2)]
    exchanged = _exchange_grads(bigs, _pack([g[n] for n in SMALL_NAMES], SMALL_ROWS))
    halves, small_parts = exchanged[:2], exchanged[-1]

    res = []
    for a, (grad, nm) in enumerate(zip(halves, ("adamw_wide", "adamw_flat"))):
        state = [_sharded_pair(lambda n, src=src: src[n])[a] for src in (w, mo, vo)]
        res.append(_adamw(grad.reshape((1,) + state[0].shape), *state, nm))
    res_small = _adamw(small_parts, *[_pack([src[n] for n in SMALL_NAMES], SMALL_ROWS) for src in (w, mo, vo)],
                       "adamw_replicated")
    outs = []
    for kind in range(4):
        vals = {WIDE: res[0][kind].reshape(w[WIDE].shape)}
        vals.update(zip(FLAT_NAMES, _unpack(res[1][kind], flat_shapes)))
        vals.update(zip(SMALL_NAMES, _unpack(res_small[kind], small_shapes)))
        outs += [vals[n] for n in ALL_NAMES]
    return (loss, dx[None], *outs)
```

```python
import functools

import numpy as np
import jax
import jax.numpy as jnp
from jax import lax
from jax.experimental import pallas as pl
from jax.experimental.pallas import tpu as pltpu

F32 = jnp.float32
MXU_DTYPE = jnp.bfloat16

DEPTH = 2
D_MODEL = 1024
GROUP_W = 256
N_HEADS = 4
HEAD_DIM = 64
MLA_Q_RANK = 256
MLA_KV_RANK = 128
MLA_ROPE = 32
MLA_Q_COLS = N_HEADS * (HEAD_DIM + MLA_ROPE)
MLA_KV_COLS = N_HEADS * 2 * HEAD_DIM
ROPE_THETA = 10000.0
LN_EPS = 1e-5
RMS_EPS = 1e-6
ALPHA = (2 * DEPTH) ** 0.25
ADAM_LR, ADAM_B1, ADAM_B2, ADAM_EPS, ADAM_WD, ADAM_STEP = 0.001, 0.9, 0.999, 1e-08, 0.01, 10

_SPLIT = (256, 256, 256, 4, 256, 256, 256, 256, 128, 32, 256, 1024)
_OFF = [int(o) for o in np.cumsum((0,) + _SPLIT)]
IN_COLS = _OFF[-1]
PA_COLS = 7 * GROUP_W
PB_COLS = 1024 + 256 + 128 + 128
PB_CQ_BLK, PB_CKV_BLK, PB_MISC_BLK = 4, 10, 11
MISC_KROT = 4

LANES = 1024
PACK_ALIGN = 16
SMALL_ROWS = 144
ROW_TILE = 256
PROJ_BWD_ROWS = 512
ATT_TILE = 256
SB_QUERY_TILE = 256
SOFTMAX_TILE = 512
FWD_PAIRS_PER_LOOP = 1
BWD_PAIRS_PER_LOOP = 2
PAIR = 128
SB_SLOT = PAIR // N_HEADS
VMEM_LIMIT = 56 * 1024 * 1024
MATMUL_VMEM = 30 * 1024 * 1024
NEG = -1e30
LOG2E = 1.4426950408889634
LN2 = 0.6931471805599453
EXP_UNDERFLOW = -104.0
DEAD_LOGIT = -110.0
REACH_SLACK = 1.0 + 2.0 ** -10
MESH_AXES = ("x", "y", "c")


def _dot(a, b):
    return jnp.dot(a, b, preferred_element_type=F32)


def _dot_nt(a, b):
    return lax.dot_general(a, b, (((1,), (1,)), ((), ())), preferred_element_type=F32)


def _dot_tn(a, b):
    return lax.dot_general(a, b, (((0,), (0,)), ((), ())), preferred_element_type=F32)


def _split2(x):
    hi = x.astype(MXU_DTYPE)
    lo = (x - hi.astype(F32)).astype(MXU_DTYPE)
    return hi, lo


def _split3(x):
    hi = x.astype(MXU_DTYPE)
    r = x - hi.astype(F32)
    mid = r.astype(MXU_DTYPE)
    lo = (r - mid.astype(F32)).astype(MXU_DTYPE)
    return hi, mid, lo


def _dot_exact_r(x, pm):
    hi, mid, lo = _split3(x)
    return _dot(hi, pm) + _dot(mid, pm) + _dot(lo, pm)


def _dot_exact_l(pm, x):
    hi, mid, lo = _split3(x)
    return _dot(pm, hi) + _dot(pm, mid) + _dot(pm, lo)


def _pick(dim, prefs):
    for p in prefs:
        if dim % p == 0:
            return p
    return dim


def _softplus(z):
    return jnp.maximum(z, 0.0) + jnp.log(1.0 + jnp.exp(-jnp.abs(z)))


def _tile_options(dim):
    opts = [d for d in range(128, min(dim, 2048) + 1, 128) if dim % d == 0]
    return opts or [dim]


def _matmul_tiles(m, n, k, out_bytes):
    tk = k if k <= 4096 else _pick(k, (1024, 512, 256, 128))
    best = None
    for tm in _tile_options(m):
        for tn in _tile_options(n):
            vmem = 2 * 2 * (tm * tk + tk * tn) + 4 * tm * tn + 2 * out_bytes * tm * tn
            if vmem <= MATMUL_VMEM and (best is None or tm * tn / (tm + tn) > best[0]):
                best = (tm * tn / (tm + tn), tm, tn)
    return best[1], best[2], tk


def _matmul(a, b, out_dtype, name, mode="nn"):
    m, k = (a.shape[1], a.shape[0]) if mode == "tn" else a.shape
    n = b.shape[0] if mode == "nt" else b.shape[1]
    tm, tn, tk = _matmul_tiles(m, n, k, jnp.dtype(out_dtype).itemsize)
    nk = k // tk
    dot = {"nn": _dot, "tn": _dot_tn, "nt": _dot_nt}[mode]

    def kern(a_ref, b_ref, o_ref, *acc):
        if nk == 1:
            o_ref[...] = dot(a_ref[...], b_ref[...]).astype(o_ref.dtype)
            return
        acc_ref, = acc
        kk = pl.program_id(2)

        @pl.when(kk == 0)
        def _():
            acc_ref[...] = jnp.zeros_like(acc_ref)

        acc_ref[...] += dot(a_ref[...], b_ref[...])

        @pl.when(kk == nk - 1)
        def _():
            o_ref[...] = acc_ref[...].astype(o_ref.dtype)

    a_spec = (pl.BlockSpec((tk, tm), lambda i, j, kk: (kk, i)) if mode == "tn"
              else pl.BlockSpec((tm, tk), lambda i, j, kk: (i, kk)))
    b_spec = (pl.BlockSpec((tn, tk), lambda i, j, kk: (j, kk)) if mode == "nt"
              else pl.BlockSpec((tk, tn), lambda i, j, kk: (kk, j)))
    return pl.pallas_call(
        kern, name=name, grid=(m // tm, n // tn, nk), in_specs=[a_spec, b_spec],
        out_specs=pl.BlockSpec((tm, tn), lambda i, j, kk: (i, j)),
        out_shape=jax.ShapeDtypeStruct((m, n), out_dtype),
        scratch_shapes=[pltpu.VMEM((tm, tn), F32)] if nk > 1 else [],
        compiler_params=pltpu.CompilerParams(
            dimension_semantics=("parallel", "parallel", "arbitrary"), vmem_limit_bytes=VMEM_LIMIT),
    )(a.astype(MXU_DTYPE), b.astype(MXU_DTYPE))


def _proj_in(hb, wa, wb, key_cols):
    rows, d = hb.shape
    tm = min(PROJ_BWD_ROWS, rows)

    def kern(a_ref, wa_ref, wb_ref, pa_ref, pb_ref, kmax_ref):
        @pl.when(pl.program_id(0) == 0)
        def _():
            kmax_ref[...] = jnp.zeros_like(kmax_ref)

        a = a_ref[...]
        pa = _dot(a, wa_ref[...]).astype(pa_ref.dtype)
        pa_ref[...] = pa
        pb_ref[...] = _dot(a, wb_ref[...])
        cols = {}
        for p in range(2):
            k32 = pa[:, key_cols + p * PAIR:key_cols + (p + 1) * PAIR].astype(F32)
            for e in range(2):
                sq = jnp.sum(jnp.where(_lane_mask("pair", e, tm), k32 * k32, 0.0), axis=1, keepdims=True)
                cols[2 * p + e] = jnp.max(sq, axis=0, keepdims=True)
        lane = lax.broadcasted_iota(jnp.int32, (1, PAIR), 1)
        kmax_ref[...] = jnp.maximum(kmax_ref[...], sum(jnp.where(lane == h, c, 0.0) for h, c in cols.items()))

    whole = lambda w: pl.BlockSpec(w.shape, lambda i: (0, 0), pipeline_mode=pl.Buffered(1))
    return pl.pallas_call(
        kern, name="proj_in", grid=(rows // tm,),
        in_specs=[pl.BlockSpec((tm, d), lambda i: (i, 0)), whole(wa), whole(wb)],
        out_specs=[pl.BlockSpec((tm, wa.shape[1]), lambda i: (i, 0)), pl.BlockSpec((tm, wb.shape[1]), lambda i: (i, 0)),
                   pl.BlockSpec((1, PAIR), lambda i: (0, 0))],
        out_shape=[jax.ShapeDtypeStruct((rows, wa.shape[1]), MXU_DTYPE), jax.ShapeDtypeStruct((rows, wb.shape[1]), F32),
                   jax.ShapeDtypeStruct((1, PAIR), F32)],
        compiler_params=pltpu.CompilerParams(dimension_semantics=("arbitrary",), vmem_limit_bytes=VMEM_LIMIT),
    )(hb, wa.astype(MXU_DTYPE), wb.astype(MXU_DTYPE))


def _piece_arrays(pieces):
    return [a for p in pieces for a in (p if isinstance(p, tuple) else (p,))]


def _join_pieces(refs, pieces):
    refs, cols = list(refs), []
    for p in pieces:
        vals = [refs.pop(0)[...] for _ in (p if isinstance(p, tuple) else (p,))]
        cols.append(functools.reduce(jnp.add, vals).astype(MXU_DTYPE))
    return jnp.concatenate(cols, axis=1)


def _proj_dx(pieces, w):
    arrs = _piece_arrays(pieces)
    rows, n = arrs[0].shape[0], w.shape[0]
    tm = min(PROJ_BWD_ROWS, rows)

    def kern(*refs):
        refs[-1][...] = _dot_nt(_join_pieces(refs[:len(arrs)], pieces), refs[len(arrs)][...])

    return pl.pallas_call(
        kern, name="proj_dx", grid=(rows // tm,),
        in_specs=[pl.BlockSpec((tm, a.shape[1]), lambda i: (i, 0)) for a in arrs]
        + [pl.BlockSpec(w.shape, lambda i: (0, 0), pipeline_mode=pl.Buffered(1))],
        out_specs=pl.BlockSpec((tm, n), lambda i: (i, 0)), out_shape=jax.ShapeDtypeStruct((rows, n), F32),
        compiler_params=pltpu.CompilerParams(dimension_semantics=("parallel",), vmem_limit_bytes=VMEM_LIMIT),
    )(*arrs, w.astype(MXU_DTYPE))


def _proj_dw(a, pieces):
    arrs = _piece_arrays(pieces)
    rows, m = a.shape
    k = sum(x.shape[1] for x in (p[0] if isinstance(p, tuple) else p for p in pieces))
    tk = min(PROJ_BWD_ROWS, rows)

    def kern(*refs):
        o_ref = refs[-1]

        @pl.when(pl.program_id(0) == 0)
        def _():
            o_ref[...] = jnp.zeros_like(o_ref)

        o_ref[...] += _dot_tn(refs[0][...], _join_pieces(refs[1:1 + len(arrs)], pieces))

    return pl.pallas_call(
        kern, name="proj_dw", grid=(rows // tk,),
        in_specs=[pl.BlockSpec((tk, m), lambda i: (i, 0))]
        + [pl.BlockSpec((tk, x.shape[1]), lambda i: (i, 0)) for x in arrs],
        out_specs=pl.BlockSpec((m, k), lambda i: (0, 0), pipeline_mode=pl.Buffered(1)),
        out_shape=jax.ShapeDtypeStruct((m, k), F32),
        compiler_params=pltpu.CompilerParams(dimension_semantics=("arbitrary",), vmem_limit_bytes=VMEM_LIMIT),
    )(a, *arrs)


def _rowwise(body, name, rows, tile, row_ins, full_ins, row_outs, acc_outs=(), scratch=(),
             reverse=False, sequential=False):
    n = rows // tile

    def ridx(i):
        return (n - 1 - i) if reverse else i

    in_specs, args = [], []
    for arr, width, cb in row_ins:
        in_specs.append(pl.BlockSpec((tile, width), lambda i, cb=cb: (ridx(i), cb)))
        args.append(arr)
    for arr in full_ins:
        in_specs.append(pl.BlockSpec(arr.shape, lambda i, nd=arr.ndim: (0,) * nd))
        args.append(arr)
    out_shape = [jax.ShapeDtypeStruct((rows, w), dt) for w, dt in row_outs]
    out_shape += [jax.ShapeDtypeStruct(s, dt) for s, dt in acc_outs]
    out_specs = [pl.BlockSpec((tile, w), lambda i: (ridx(i), 0)) for w, dt in row_outs]
    out_specs += [pl.BlockSpec(s, lambda i, nd=len(s): (0,) * nd) for s, dt in acc_outs]

    def kern(*refs):
        body(pl.program_id(0), *refs)

    sem = "arbitrary" if (acc_outs or sequential) else "parallel"
    return pl.pallas_call(
        kern, name=name, grid=(n,), in_specs=in_specs, out_specs=out_specs, out_shape=out_shape,
        scratch_shapes=list(scratch),
        compiler_params=pltpu.CompilerParams(dimension_semantics=(sem,), vmem_limit_bytes=VMEM_LIMIT),
    )(*args)


def _ln_stats(u):
    mu = jnp.mean(u, axis=-1, keepdims=True)
    xc = u - mu
    var = jnp.mean(xc * xc, axis=-1, keepdims=True)
    return xc, lax.rsqrt(var + LN_EPS)


def _ln_fwd(a, b, g, beta, name):
    rows, d = a.shape
    has_b = b is not None

    def body(i, *refs):
        if has_b:
            a_ref, b_ref, g_ref, be_ref, h_ref, hb_ref = refs
            u = ALPHA * a_ref[...] + b_ref[...]
        else:
            a_ref, g_ref, be_ref, h_ref, hb_ref = refs
            u = a_ref[...]
        xc, rstd = _ln_stats(u)
        y = xc * rstd * g_ref[...] + be_ref[...]
        h_ref[...] = y
        hb_ref[...] = y.astype(hb_ref.dtype)

    row_ins = [(a, d, 0)] + ([(b, d, 0)] if has_b else [])
    return _rowwise(body, name, rows, min(ROW_TILE, rows), row_ins,
                    [g.reshape(1, d), beta.reshape(1, d)], [(d, F32), (d, MXU_DTYPE)])


def _ln_bwd(a, b, g, dy1, dy2, c1, name):
    rows, d = a.shape
    has_b = b is not None
    has_2 = dy2 is not None

    def body(i, *refs):
        refs = list(refs)
        a_ref = refs.pop(0)
        u = a_ref[...]
        if has_b:
            u = ALPHA * u + refs.pop(0)[...]
        dy = c1 * refs.pop(0)[...]
        if has_2:
            dy = dy + refs.pop(0)[...]
        g_ref, du_ref, dub_ref, dg_ref, db_ref = refs

        @pl.when(i == 0)
        def _():
            dg_ref[...] = jnp.zeros_like(dg_ref)
            db_ref[...] = jnp.zeros_like(db_ref)

        xc, rstd = _ln_stats(u)
        xhat = xc * rstd
        dxh = dy * g_ref[...]
        m1 = jnp.mean(dxh, axis=-1, keepdims=True)
        m2 = jnp.mean(dxh * xhat, axis=-1, keepdims=True)
        du = rstd * (dxh - m1 - xhat * m2)
        du_ref[...] = du
        dub_ref[...] = du.astype(dub_ref.dtype)
        dg_ref[...] += jnp.sum(dy * xhat, axis=0, keepdims=True)
        db_ref[...] += jnp.sum(dy, axis=0, keepdims=True)

    row_ins = [(a, d, 0)] + ([(b, d, 0)] if has_b else []) + [(dy1, d, 0)] + ([(dy2, d, 0)] if has_2 else [])
    return _rowwise(body, name, rows, min(ROW_TILE, rows), row_ins, [g.reshape(1, d)],
                    [(d, F32), (d, MXU_DTYPE)], [((1, d), F32), ((1, d), F32)])


def _gate_out_proj_ln(groups, pb, w_out, h, g, beta, target=None):
    rows, d = h.shape
    last = target is not None
    w = GROUP_W * len(groups)

    def body(i, *refs):
        refs = list(refs)
        mixed = jnp.concatenate([refs.pop(0)[...] for _ in groups], axis=1)
        gate = refs.pop(0)[...]
        if last:
            h_ref, t_ref, w_ref, g_ref, be_ref, a_ref, u_ref, dh_ref, acc_ref = refs

            @pl.when(i == 0)
            def _():
                acc_ref[...] = jnp.zeros_like(acc_ref)
        else:
            h_ref, w_ref, g_ref, be_ref, a_ref, u_ref, o_ref, ob_ref = refs
        gated = (mixed * (gate / (1.0 + jnp.exp(-gate)))).astype(a_ref.dtype)
        a_ref[...] = gated
        u = ALPHA * h_ref[...] + _dot(gated, w_ref[...])
        u_ref[...] = u
        xc, rstd = _ln_stats(u)
        y = xc * rstd * g_ref[...] + be_ref[...]
        if last:
            e = y - t_ref[...]
            dh_ref[...] = e * (1.0 / d)
            acc_ref[...] += jnp.sum(e * e, axis=0, keepdims=True)
        else:
            o_ref[...] = y
            ob_ref[...] = y.astype(ob_ref.dtype)

    row_ins = [(o, GROUP_W, 0) for o in groups] + [(pb, w, 0), (h, d, 0)] + ([(target, d, 0)] if last else [])
    full_ins = [w_out.astype(MXU_DTYPE), g.reshape(1, d), beta.reshape(1, d)]
    if last:
        return _rowwise(body, "gate_out_proj_ln_loss", rows, PROJ_BWD_ROWS, row_ins, full_ins,
                        [(w, MXU_DTYPE), (d, F32), (d, F32)], [((1, d), F32)])
    return _rowwise(body, "gate_out_proj_ln", rows, PROJ_BWD_ROWS, row_ins, full_ins,
                    [(w, MXU_DTYPE), (d, F32), (d, F32), (d, MXU_DTYPE)])


def _out_proj_gate_bwd(du_b, w_out, groups, pb):
    rows = pb.shape[0]
    w = GROUP_W * len(groups)

    def body(i, *refs):
        dg = _dot_nt(refs[0][...], refs[6][...])
        mixed = jnp.concatenate([r[...] for r in refs[1:5]], axis=1)
        g = refs[5][...]
        dm_ref, dgate_ref = refs[7], refs[8]
        sig = 1.0 / (1.0 + jnp.exp(-g))
        dm_ref[...] = dg * (g * sig)
        dgate_ref[...] = (dg * mixed * (sig * (1.0 + g * (1.0 - sig)))).astype(dgate_ref.dtype)

    return _rowwise(body, "out_proj_gate_bwd", rows, PROJ_BWD_ROWS,
                    [(du_b, du_b.shape[1], 0)] + [(o, GROUP_W, 0) for o in groups] + [(pb, w, 0)],
                    [w_out.astype(MXU_DTYPE)], [(w, F32), (w, MXU_DTYPE)])


def _tri(n, kind):
    r = np.arange(n)[:, None]
    c = np.arange(n)[None, :]
    m = {"lower_incl": r >= c, "upper_incl": r <= c, "row_gt_col": r > c, "row_lt_col": r < c}[kind]
    return jnp.asarray(m.astype(np.float32), dtype=MXU_DTYPE)


def _forget_fwd(pb, bias_row):
    rows = pb.shape[0]
    tile = _pick(rows, (1024, 512, 256))

    def body(i, x_ref, b_ref, l_ref, o_ref, carry_ref):
        @pl.when(i == 0)
        def _():
            carry_ref[...] = jnp.zeros_like(carry_ref)

        xx = x_ref[...] + b_ref[...]
        lane = lax.broadcasted_iota(jnp.int32, xx.shape, 1)
        lf = jnp.where(lane < N_HEADS, -_softplus(-xx), 0.0)
        o_ref[...] = _dot_exact_l(l_ref[...], lf) + carry_ref[...]
        carry_ref[...] += jnp.sum(lf, axis=0, keepdims=True)

    return _rowwise(body, "forget_fwd", rows, tile, [(pb, 128, PB_MISC_BLK)],
                    [bias_row, _tri(tile, "lower_incl")], [(128, F32)],
                    scratch=[pltpu.VMEM((1, 128), F32)], sequential=True)[0]


def _forget_bwd(pb, bias_row, dfc):
    rows = pb.shape[0]
    tile = _pick(rows, (1024, 512, 256))

    def body(i, x_ref, df_ref, b_ref, u_ref, o_ref, db_ref, carry_ref):
        @pl.when(i == 0)
        def _():
            carry_ref[...] = jnp.zeros_like(carry_ref)
            db_ref[...] = jnp.zeros_like(db_ref)

        df = df_ref[...]
        sfx = _dot_exact_l(u_ref[...], df) + carry_ref[...]
        carry_ref[...] += jnp.sum(df, axis=0, keepdims=True)
        xx = x_ref[...] + b_ref[...]
        lane = lax.broadcasted_iota(jnp.int32, xx.shape, 1)
        dl = jnp.where(lane < N_HEADS, sfx / (1.0 + jnp.exp(xx)), 0.0)
        o_ref[...] = dl
        db_ref[...] += jnp.sum(dl, axis=0, keepdims=True)

    return _rowwise(body, "forget_bwd", rows, tile,
                    [(pb, 128, PB_MISC_BLK), (dfc, 128, 0)],
                    [bias_row, _tri(tile, "upper_incl")], [(128, F32)], [((1, 128), F32)],
                    scratch=[pltpu.VMEM((1, 128), F32)], reverse=True, sequential=True)


def _rope_tables(s):
    half = MLA_ROPE // 2
    inv_freq = ROPE_THETA ** (-jnp.arange(half, dtype=F32) / half)
    ang = jnp.arange(s).astype(F32)[:, None] * inv_freq[None, :]
    cos2 = jnp.tile(jnp.cos(ang), (1, 2))
    sin2 = jnp.tile(jnp.sin(ang), (1, 2))
    cx = jnp.tile(cos2, (1, N_HEADS))
    sx = jnp.tile(sin2, (1, N_HEADS))
    pad = ((0, 0), (MISC_KROT, 128 - MISC_KROT - MLA_ROPE))
    ck = jnp.pad(cos2, pad)
    sk = jnp.pad(sin2, pad)
    return dict(ck=ck, sk=sk, cx=cx, sx=sx)


def _rot_matrix(width, bases):
    half = MLA_ROPE // 2
    p = np.zeros((width, width), np.float32)
    for b in bases:
        for i in range(half):
            p[b + half + i, b + i] = -1.0
            p[b + i, b + half + i] = 1.0
    return p


def _rope_matrices():
    pk = _rot_matrix(128, [MISC_KROT])
    p4 = _rot_matrix(128, [h * MLA_ROPE for h in range(N_HEADS)])
    a = np.zeros((128, 128), np.float32)
    for h in range(N_HEADS):
        for r in range(MLA_ROPE):
            a[h * MLA_ROPE + r, MISC_KROT + r] = 1.0
    cast = lambda m: jnp.asarray(m, dtype=MXU_DTYPE)
    return dict(p4=cast(p4), p4t=cast(p4.T), pk=cast(pk), spread=cast(a.T), xa=cast(a), xb=cast(p4.T @ a))


def _rms(c, g):
    r = lax.rsqrt(jnp.mean(c * c, axis=-1, keepdims=True) + RMS_EPS)
    return c * r * g


def _mla_q_fwd(pb, g, w_up, tabs, mats):
    rows = pb.shape[0]

    def body(i, c_ref, cos_ref, sin_ref, g_ref, w_ref, p_ref, q_ref, cn_ref):
        cn = _rms(c_ref[...], g_ref[...]).astype(cn_ref.dtype)
        cn_ref[...] = cn
        q = _dot(cn, w_ref[...])
        qr = q[:, GROUP_W:]
        q_ref[:, :GROUP_W] = q[:, :GROUP_W].astype(q_ref.dtype)
        q_ref[:, GROUP_W:] = (qr * cos_ref[...] + _dot_exact_r(qr, p_ref[...]) * sin_ref[...]).astype(q_ref.dtype)

    return _rowwise(body, "mla_q_fwd", rows, ROW_TILE,
                    [(pb, MLA_Q_RANK, PB_CQ_BLK), (tabs["cx"], PAIR, 0), (tabs["sx"], PAIR, 0)],
                    [g.reshape(1, MLA_Q_RANK), w_up.astype(MXU_DTYPE), mats["p4"]],
                    [(MLA_Q_COLS, MXU_DTYPE), (MLA_Q_RANK, MXU_DTYPE)])


def _mla_kv_fwd(pb, g, w_up, tabs, mats):
    rows = pb.shape[0]

    def body(i, c_ref, x_ref, cos_ref, sin_ref, g_ref, w_ref, p_ref, sp_ref, kv_ref, cn_ref, kr_ref):
        cn = _rms(c_ref[...], g_ref[...]).astype(cn_ref.dtype)
        cn_ref[...] = cn
        kv_ref[...] = _dot(cn, w_ref[...]).astype(kv_ref.dtype)
        xx = x_ref[...]
        kr = xx * cos_ref[...] + _dot_exact_r(xx, p_ref[...]) * sin_ref[...]
        kr_ref[...] = _dot_exact_r(kr, sp_ref[...]).astype(kr_ref.dtype)

    return _rowwise(body, "mla_kv_fwd", rows, ROW_TILE,
                    [(pb, MLA_KV_RANK, PB_CKV_BLK), (pb, 128, PB_MISC_BLK), (tabs["ck"], 128, 0), (tabs["sk"], 128, 0)],
                    [g.reshape(1, MLA_KV_RANK), w_up.astype(MXU_DTYPE), mats["pk"], mats["spread"]],
                    [(MLA_KV_COLS, MXU_DTYPE), (MLA_KV_RANK, MXU_DTYPE), (128, MXU_DTYPE)])


def _rms_bwd(c, g, dy):
    r = lax.rsqrt(jnp.mean(c * c, axis=-1, keepdims=True) + RMS_EPS)
    dyg = dy * g
    dc = r * dyg - c * (r * r * r) * jnp.mean(c * dyg, axis=-1, keepdims=True)
    return dc, jnp.sum(dy * c * r, axis=0, keepdims=True)


def _mla_q_bwd(dq_full, pb, g, w_up, tabs, mats):
    rows, nq = dq_full.shape

    def body(i, d_ref, c_ref, cos_ref, sin_ref, g_ref, w_ref, pt_ref, dq_ref, dc_ref, dg_ref):
        @pl.when(i == 0)
        def _():
            dg_ref[...] = jnp.zeros_like(dg_ref)

        dr = d_ref[:, GROUP_W:]
        dq = jnp.concatenate([d_ref[:, :GROUP_W], dr * cos_ref[...] + _dot_exact_r(dr * sin_ref[...], pt_ref[...])],
                             axis=1).astype(dq_ref.dtype)
        dq_ref[...] = dq
        dc, dg = _rms_bwd(c_ref[...], g_ref[...], _dot_nt(dq, w_ref[...]))
        dc_ref[...] = dc.astype(dc_ref.dtype)
        dg_ref[...] += dg

    return _rowwise(body, "mla_q_bwd", rows, ROW_TILE,
                    [(dq_full, nq, 0), (pb, MLA_Q_RANK, PB_CQ_BLK), (tabs["cx"], PAIR, 0), (tabs["sx"], PAIR, 0)],
                    [g.reshape(1, MLA_Q_RANK), w_up.astype(MXU_DTYPE), mats["p4t"]],
                    [(nq, MXU_DTYPE), (MLA_Q_RANK, MXU_DTYPE)], [((1, MLA_Q_RANK), F32)])


def _mla_kv_bwd(dk, dv, pb, g, w_up, tabs, mats):
    rows = dk.shape[0]

    def body(i, dk_ref, dr_ref, dv_ref, c_ref, cos_ref, sin_ref, g_ref, w_ref, a_ref, b_ref,
             dkv_ref, dc_ref, dm_ref, dg_ref):
        @pl.when(i == 0)
        def _():
            dg_ref[...] = jnp.zeros_like(dg_ref)

        dkv = jnp.concatenate([dk_ref[...], dv_ref[...]], axis=1).astype(dkv_ref.dtype)
        dkv_ref[...] = dkv
        dc, dg = _rms_bwd(c_ref[...], g_ref[...], _dot_nt(dkv, w_ref[...]))
        dc_ref[...] = dc.astype(dc_ref.dtype)
        dg_ref[...] += dg
        dr = dr_ref[...]
        dm_ref[...] = _dot_exact_r(dr * cos_ref[...], a_ref[...]) + _dot_exact_r(dr * sin_ref[...], b_ref[...])

    return _rowwise(body, "mla_kv_bwd", rows, ROW_TILE,
                    [(dk, GROUP_W, 0), (dk, PAIR, 2), (dv, GROUP_W, 0), (pb, MLA_KV_RANK, PB_CKV_BLK),
                     (tabs["cx"], PAIR, 0), (tabs["sx"], PAIR, 0)],
                    [g.reshape(1, MLA_KV_RANK), w_up.astype(MXU_DTYPE), mats["xa"], mats["xb"]],
                    [(MLA_KV_COLS, MXU_DTYPE), (MLA_KV_RANK, MXU_DTYPE), (PAIR, F32)], [((1, MLA_KV_RANK), F32)])


def _att_params(parallel):
    return pltpu.CompilerParams(dimension_semantics=("parallel" if parallel else "arbitrary",),
                                vmem_limit_bytes=VMEM_LIMIT)


def _blk_off(j, t):
    return j * t if isinstance(j, int) else pl.multiple_of(j * t, t)


def _causal_mask(t, strict):
    r = lax.broadcasted_iota(jnp.int32, (t, t), 0)
    c = lax.broadcasted_iota(jnp.int32, (t, t), 1)
    return (c < r) if strict else (c <= r)


def _lane_mask(kind, head, rows):
    lane = lax.broadcasted_iota(jnp.int32, (rows, PAIR), 1)
    if kind == "pair":
        return (lane < HEAD_DIM) if head % 2 == 0 else (lane >= HEAD_DIM)
    return (lane >= MLA_ROPE * head) & (lane < MLA_ROPE * (head + 1))


def _row_spec(t, cb, width=PAIR):
    return pl.BlockSpec((t, width), lambda i, cb=cb: (i, cb))


def _whole_spec(rows, cb, width=PAIR):
    return pl.BlockSpec((rows, width), lambda i, cb=cb: (0, cb), pipeline_mode=pl.Buffered(1))


def _is_pow2(x):
    return float(np.frexp(x)[0]) == 0.5


def _masked_heads(blocks, kinds, pair, dtype, scale=None):
    out = []
    for e in range(2):
        head = 2 * pair + e
        parts = [jnp.where(_lane_mask(k, head, b.shape[0]), b.astype(F32) * (1.0 if scale is None else scale),
                           0.0).astype(dtype)
                 for b, k in zip(blocks, kinds)]
        out.append(parts[0] if len(parts) == 1 else jnp.concatenate(parts, axis=1))
    return out


def _logit_reach(qh, kmax2, head):
    q32 = qh.astype(F32)
    return jnp.sqrt(jnp.sum(q32 * q32, axis=1, keepdims=True) * _col(kmax2, head)) * REACH_SLACK


def _forget_top(ft_ref, head, off):
    return jnp.max(-ft_ref[head:head + 1, pl.ds(off, PAIR)])


def _col(block, idx):
    lane = lax.broadcasted_iota(jnp.int32, block.shape, 1)
    return jnp.sum(jnp.where(lane == idx, block, 0.0), axis=1, keepdims=True)


def _scatter_cols(cols, t):
    lane = lax.broadcasted_iota(jnp.int32, (t, PAIR), 1)
    out = jnp.zeros((t, PAIR), F32)
    for idx, c in cols.items():
        out = out + jnp.where(lane == idx, c, 0.0)
    return out


def _take_heads(per_head, pair):
    return jnp.where(_lane_mask("pair", 0, per_head[0].shape[0]), per_head[0], per_head[1])


class _Parts:
    def __init__(self, q_parts, k_parts, v_parts, tq, sk):
        self.kinds = [[kind for _, _, kind in q_parts[p]] for p in range(2)]
        self.nparts = len(q_parts[0])
        self.q_specs = [_row_spec(tq, cb) for p in range(2) for _, cb, _ in q_parts[p]]
        self.q_args = [a for p in range(2) for a, _, _ in q_parts[p]]
        self.k_specs = [_whole_spec(sk, cb) for p in range(2) for _, cb, _ in k_parts[p]]
        self.k_args = [a for p in range(2) for a, _, _ in k_parts[p]]
        self.v_specs = [_whole_spec(sk, cb) for _, cb in v_parts]
        self.v_args = [a for a, _ in v_parts]
        self.width = PAIR * self.nparts

    def split(self, refs):
        n = self.nparts
        refs = list(refs)
        q = [refs[p * n:(p + 1) * n] for p in range(2)]
        k = [refs[2 * n + p * n:2 * n + (p + 1) * n] for p in range(2)]
        v = refs[4 * n:4 * n + 2]
        return q, k, v, refs[4 * n + 2:]

    def k_block(self, k_refs, off, t):
        blks = [r[pl.ds(off, t), :] for r in k_refs]
        return blks[0] if len(blks) == 1 else jnp.concatenate(blks, axis=1)


def _softmax_fwd(q_parts, k_parts, v_parts, sq, sk, scale, causal, bias, name):
    tq = min(SOFTMAX_TILE, sq)
    tk = tq if causal else min(SOFTMAX_TILE, sk)
    nkv = sk // tk
    pp = _Parts(q_parts, k_parts, v_parts, tq, sk)

    def kern(*refs):
        q_refs, k_refs, v_refs, rest = pp.split(refs)
        if bias is not None:
            fc_ref, ft_ref, kmax_ref, o_ref, lse_ref = rest
            fcb = fc_ref[...]
        else:
            o_ref, lse_ref = rest
        i = pl.program_id(0)
        fold = _is_pow2(scale)
        head_on = [jnp.where(_lane_mask("pair", e, tk), 1.0, 0.0).astype(MXU_DTYPE) for e in range(2)]
        head_off = [jnp.where(_lane_mask("pair", e, tk), 0.0, 1.0).astype(MXU_DTYPE) for e in range(2)]
        lse_cols = {}
        for first in range(0, 2, FWD_PAIRS_PER_LOOP):
            pairs = list(range(first, first + FWD_PAIRS_PER_LOOP))
            heads = [2 * p + e for p in pairs for e in range(2)]
            qm = {}
            for p in pairs:
                masked_q = _masked_heads([r[...] for r in q_refs[p]], pp.kinds[p], p, MXU_DTYPE, scale if fold else None)
                qm.update({2 * p + e: masked_q[e] for e in range(2)})
            if bias is not None:
                reach = {h: _logit_reach(qm[h], kmax_ref[...], h) for h in heads}

            def block(j, carry, masked, pairs=pairs, qm=qm):
                off = _blk_off(j, tk)
                out = []
                for p in pairs:
                    kb = pp.k_block(k_refs[p], off, tk)
                    vb = v_refs[p][pl.ds(off, tk), :]
                    for e in range(2):
                        h = 2 * p + e
                        m, acc = carry[len(out)]
                        s = _dot_nt(qm[h], kb) * (LOG2E if fold else scale * LOG2E)
                        if bias is not None:
                            s = s - ft_ref[h:h + 1, pl.ds(off, tk)] * LOG2E
                        if masked:
                            s = jnp.where(_causal_mask(tq, False), s, NEG)
                        m_new = jnp.maximum(m, jnp.max(s, axis=1, keepdims=True))
                        pr = jnp.exp2(s - m_new).astype(MXU_DTYPE)
                        out.append((m_new, jnp.exp2(m - m_new) * acc + _dot(pr, vb * head_on[e] + head_off[e])))
                return tuple(out)

            carry = tuple((jnp.full((tq, 1), NEG, F32), jnp.zeros((tq, PAIR), F32)) for _ in heads)
            if causal and bias is not None:
                def alive(c, j, heads=heads, reach=reach):
                    return functools.reduce(jnp.maximum, [
                        jnp.max(reach[h] + _forget_top(ft_ref, h, _blk_off(j, tk)) - c[n][0] * LN2)
                        for n, h in enumerate(heads)])

                def step(state, block=block, alive=alive):
                    n, _, c = state
                    c = block(i - 1 - n, c, False)
                    return n + 1, alive(c, i - 1 - n), c

                carry = block(i, carry, True)
                _, _, carry = lax.while_loop(lambda st: jnp.logical_and(st[0] < i, st[1] > DEAD_LOGIT), step,
                                             (jnp.int32(0), alive(carry, i), carry))
            elif causal:
                carry = lax.fori_loop(0, i, lambda j, c, block=block: block(j, c, False), carry)
                carry = block(i, carry, True)
            else:
                for j in range(nkv):
                    carry = block(j, carry, False)
            for n, p in enumerate(pairs):
                outs = []
                for e in range(2):
                    m, acc = carry[2 * n + e]
                    l = _col(acc, HEAD_DIM * (1 - e))
                    outs.append(acc / l)
                    lse_cols[2 * p + e] = m * LN2 + jnp.log(l) + (_col(fcb, 2 * p + e) if bias is not None else 0.0)
                o_ref[:, p * PAIR:(p + 1) * PAIR] = _take_heads(outs, p)
        lse_ref[...] = _scatter_cols(lse_cols, tq)

    in_specs = pp.q_specs + pp.k_specs + pp.v_specs
    args = pp.q_args + pp.k_args + pp.v_args
    if bias is not None:
        in_specs += [_row_spec(tq, 0), pl.BlockSpec((8, sk), lambda i: (0, 0), pipeline_mode=pl.Buffered(1)),
                     pl.BlockSpec((1, PAIR), lambda i: (0, 0))]
        args += list(bias)
    return pl.pallas_call(
        kern, name=name, grid=(sq // tq,), in_specs=in_specs,
        out_specs=[_row_spec(tq, 0, GROUP_W), _row_spec(tq, 0)],
        out_shape=[jax.ShapeDtypeStruct((sq, GROUP_W), F32), jax.ShapeDtypeStruct((sq, PAIR), F32)],
        compiler_params=_att_params(True),
    )(*args)


def _softmax_bwd(q_parts, k_parts, v_parts, o, lse, do, do_blk, sq, sk, scale, causal, bias, name):
    tq = min(SOFTMAX_TILE, sq)
    tk = tq if causal else min(SOFTMAX_TILE, sk)
    nkv = sk // tk
    pp = _Parts(q_parts, k_parts, v_parts, tq, sk)
    quad = pp.nparts == 2
    wq = GROUP_W + (PAIR if quad else 0)

    def kern(*refs):
        q_refs, k_refs, v_refs, rest = pp.split(refs)
        if bias is not None:
            o_ref, lse_ref, do_ref, fc_ref, ft_ref, kmax_ref, dq_ref, dk_ref, dv_ref, dfq_ref, dfk_ref = rest
            fcb = fc_ref[...]
        else:
            o_ref, lse_ref, do_ref, dq_ref, dk_ref, dv_ref = rest
        i = pl.program_id(0)
        fold = _is_pow2(scale)

        @pl.when(i == 0)
        def _():
            dk_ref[...] = jnp.zeros_like(dk_ref)
            dv_ref[...] = jnp.zeros_like(dv_ref)
            if bias is not None:
                dfk_ref[...] = jnp.zeros_like(dfk_ref)

        lse_b = lse_ref[...]
        qm, dom, delta, lse_h = [], [], [], []
        for p in range(2):
            qm += _masked_heads([r[...] for r in q_refs[p]], pp.kinds[p], p, MXU_DTYPE, scale if fold else None)
            do_p = do_ref[:, p * PAIR:(p + 1) * PAIR]
            dom += _masked_heads([do_p], ["pair"], p, MXU_DTYPE)
            prod = do_p * o_ref[:, p * PAIR:(p + 1) * PAIR]
            for e in range(2):
                h = 2 * p + e
                delta.append(jnp.sum(jnp.where(_lane_mask("pair", h, tq), prod, 0.0), axis=1, keepdims=True))
                lse_h.append(_col(lse_b, h) - (_col(fcb, h) if bias is not None else 0.0))

        def block(j, carry, masked, pairs):
            off = _blk_off(j, tk)
            out = []
            for p in pairs:
                kb = pp.k_block(k_refs[p], off, tk)
                vb = v_refs[p][pl.ds(off, tk), :]
                dk_acc = jnp.zeros((tk, pp.width), F32)
                dv_acc = jnp.zeros((tk, PAIR), F32)
                for e in range(2):
                    h = 2 * p + e
                    dq, dfq = carry[len(out)]
                    s = _dot_nt(qm[h], kb)
                    if not fold:
                        s = s * scale
                    if bias is not None:
                        s = s - ft_ref[h:h + 1, pl.ds(off, tk)]
                    if masked:
                        s = jnp.where(_causal_mask(tq, False), s, NEG)
                    pr = jnp.exp(s - lse_h[h])
                    ds = pr * (_dot_nt(dom[h], vb) - delta[h])
                    dsb = (ds if fold else ds * scale).astype(MXU_DTYPE)
                    dv_acc = dv_acc + _dot_tn(pr.astype(MXU_DTYPE), dom[h])
                    dk_acc = dk_acc + _dot_tn(dsb, qm[h])
                    dq = dq + _dot(dsb, kb)
                    if bias is not None:
                        dfq = dfq + jnp.sum(ds, axis=1, keepdims=True)
                        dfk_ref[h:h + 1, pl.ds(off, tk)] -= jnp.sum(ds, axis=0, keepdims=True)
                    out.append((dq, dfq))
                dv_ref[pl.ds(off, tk), p * PAIR:(p + 1) * PAIR] += dv_acc
                dk_ref[pl.ds(off, tk), p * PAIR:(p + 1) * PAIR] += dk_acc[:, :PAIR]
                if quad:
                    dk_ref[pl.ds(off, tk), GROUP_W:] += dk_acc[:, PAIR:]
            return tuple(out)

        done = {}
        for first in range(0, 2, BWD_PAIRS_PER_LOOP):
            pairs = list(range(first, first + BWD_PAIRS_PER_LOOP))
            heads = [2 * p + e for p in pairs for e in range(2)]
            carry = tuple((jnp.zeros((tq, pp.width), F32), jnp.zeros((tq, 1), F32)) for _ in heads)
            if causal and bias is not None:
                reach = {h: _logit_reach(qm[h], kmax_ref[...], h) - lse_h[h] for h in heads}

                def alive(j, heads=heads, reach=reach):
                    return functools.reduce(jnp.maximum, [
                        jnp.max(reach[h] + _forget_top(ft_ref, h, _blk_off(j, tk))) for h in heads])

                def step(state, pairs=pairs, alive=alive):
                    n, _, c = state
                    return n + 1, alive(i - 1 - n), block(i - 1 - n, c, False, pairs)

                carry = block(i, carry, True, pairs)
                _, _, carry = lax.while_loop(lambda st: jnp.logical_and(st[0] < i, st[1] > DEAD_LOGIT), step,
                                             (jnp.int32(0), alive(i), carry))
            elif causal:
                carry = lax.fori_loop(0, i, lambda j, c, pairs=pairs: block(j, c, False, pairs), carry)
                carry = block(i, carry, True, pairs)
            else:
                for j in range(nkv):
                    carry = block(j, carry, False, pairs)
            done.update(zip(heads, carry))
        carry = [done[h] for h in range(N_HEADS)]
        dqs = [c[0] * scale if fold else c[0] for c in carry]
        for p in range(2):
            dq_ref[:, p * PAIR:(p + 1) * PAIR] = _take_heads([dqs[2 * p + e][:, :PAIR] for e in range(2)], p)
        if quad:
            dq_ref[:, GROUP_W:] = sum(jnp.where(_lane_mask("quad", h, tq), dqs[h][:, PAIR:], 0.0)
                                      for h in range(N_HEADS))
        if bias is not None:
            dfq_ref[...] = _scatter_cols({h: carry[h][1] for h in range(N_HEADS)}, tq)

    acc_spec = lambda rows, width: pl.BlockSpec((rows, width), lambda i: (0, 0), pipeline_mode=pl.Buffered(1))
    in_specs = pp.q_specs + pp.k_specs + pp.v_specs + [_row_spec(tq, 0, GROUP_W), _row_spec(tq, 0),
                                                       _row_spec(tq, do_blk, GROUP_W)]
    args = pp.q_args + pp.k_args + pp.v_args + [o, lse, do]
    out_specs = [_row_spec(tq, 0, wq), acc_spec(sk, wq), acc_spec(sk, GROUP_W)]
    out_shape = [jax.ShapeDtypeStruct((sq, wq), F32), jax.ShapeDtypeStruct((sk, wq), F32),
                 jax.ShapeDtypeStruct((sk, GROUP_W), F32)]
    if bias is not None:
        in_specs += [_row_spec(tq, 0), pl.BlockSpec((8, sk), lambda i: (0, 0), pipeline_mode=pl.Buffered(1)),
                     pl.BlockSpec((1, PAIR), lambda i: (0, 0))]
        args += list(bias)
        out_specs += [_row_spec(tq, 0), acc_spec(8, sk)]
        out_shape += [jax.ShapeDtypeStruct((sq, PAIR), F32), jax.ShapeDtypeStruct((8, sk), F32)]
    return pl.pallas_call(
        kern, name=name, grid=(sq // tq,), in_specs=in_specs, out_specs=out_specs, out_shape=out_shape,
        compiler_params=_att_params(False),
    )(*args)


def _sb_logs(qh, kb, valid):
    z = _dot_nt(qh, kb)
    sp = _softplus(z)
    lk = -sp
    if valid is not None:
        lk = jnp.where(valid, lk, 0.0)
    return lk, z - sp


def _sb_valid(d, tq, tk):
    r = lax.broadcasted_iota(jnp.int32, (tq, tk), 0)
    c = lax.broadcasted_iota(jnp.int32, (tq, tk), 1)
    return c + d * tk < r


def _tri_sums(xs, tri):
    t = xs[0].shape[0]
    pieces = [_split2(x) for x in xs]
    hi = _dot(jnp.concatenate([pc[0] for pc in pieces], axis=0), tri)
    lo = _dot(jnp.concatenate([pc[1] for pc in pieces], axis=0), tri)
    return [hi[n * t:(n + 1) * t] + lo[n * t:(n + 1) * t] for n in range(len(xs))]


def _sb_fwd(src, q_blk, k_blk, v_blk, s, scale, name):
    assert _is_pow2(scale)
    tq, t = min(SB_QUERY_TILE, s), min(ATT_TILE, s)
    slots = -(-(s // t) // SB_SLOT) * SB_SLOT
    width = N_HEADS * slots
    band = tq // t
    pair = lambda blk: [[(src, blk + p, "pair")] for p in range(2)]
    pp = _Parts(pair(q_blk), pair(k_blk), [(src, v_blk + p) for p in range(2)], tq, s)

    def kern(*refs):
        q_refs, k_refs, v_refs, (tri_ref, o_ref, rm_ref, cnt_ref) = pp.split(refs)
        i = pl.program_id(0)
        tri = tri_ref[...]
        lane = lax.broadcasted_iota(jnp.int32, (tq, width), 1)
        qm = []
        for p in range(2):
            qm += _masked_heads([q_refs[p][0][...]], ["pair"], p, MXU_DTYPE, scale)

        def block(j, carry, valid):
            accs, rights, rm = carry
            off = _blk_off(j, t)
            kbs = [k_refs[p][0][pl.ds(off, t), :] for p in range(2)]
            vbs = [v_refs[p][pl.ds(off, t), :] for p in range(2)]
            logs = [_sb_logs(qm[h], kbs[h // 2], valid) for h in range(N_HEADS)]
            tails = _tri_sums([lg[0] for lg in logs], tri)
            new_acc, new_right = [], []
            for h in range(N_HEADS):
                lk, ls = logs[h]
                w = jnp.exp(ls + tails[h] + rights[h])
                if valid is not None:
                    w = jnp.where(valid, w, 0.0)
                new_acc.append(accs[h] + _dot(w.astype(MXU_DTYPE), vbs[h // 2]))
                rm = rm + jnp.where(lane == slots * h + j, rights[h], 0.0)
                new_right.append(rights[h] + jnp.sum(lk, axis=1, keepdims=True))
            return tuple(new_acc), tuple(new_right), rm

        carry = (tuple(jnp.zeros((tq, PAIR), F32) for _ in range(N_HEADS)),
                 tuple(jnp.zeros((tq, 1), F32) for _ in range(N_HEADS)), jnp.zeros((tq, width), F32))
        for d in reversed(range(band)):
            carry = block(band * i + d, carry, _sb_valid(d, tq, t))

        def alive(c):
            return functools.reduce(jnp.maximum, [jnp.max(r) for r in c[1]])

        def step(state):
            n, _, c = state
            c = block(band * i - 1 - n, c, None)
            return n + 1, alive(c), c

        n_done, _, carry = lax.while_loop(lambda st: jnp.logical_and(st[0] < band * i, st[1] > EXP_UNDERFLOW),
                                          step, (jnp.int32(0), alive(carry), carry))
        cnt_ref[i] = n_done
        for p in range(2):
            o_ref[:, p * PAIR:(p + 1) * PAIR] = _take_heads([carry[0][2 * p + e] for e in range(2)], p)
        rm_ref[...] = carry[2]

    return pl.pallas_call(
        kern, name=name, grid=(s // tq,),
        in_specs=pp.q_specs + pp.k_specs + pp.v_specs + [pl.BlockSpec((t, t), lambda i: (0, 0))],
        out_specs=[_row_spec(tq, 0, GROUP_W), _row_spec(tq, 0, width), pl.BlockSpec(memory_space=pltpu.SMEM)],
        out_shape=[jax.ShapeDtypeStruct((s, GROUP_W), F32), jax.ShapeDtypeStruct((s, width), F32),
                   jax.ShapeDtypeStruct((s // tq,), jnp.int32)],
        compiler_params=_att_params(False),
    )(*(pp.q_args + pp.k_args + pp.v_args + [_tri(t, "row_gt_col")]))


def _sb_bwd(src, q_blk, k_blk, v_blk, do, do_blk, rm, visited, s, scale, name):
    assert _is_pow2(scale)
    tq, t = min(SB_QUERY_TILE, s), min(ATT_TILE, s)
    band = tq // t
    pair = lambda blk: [[(src, blk + p, "pair")] for p in range(2)]
    pp = _Parts(pair(q_blk), pair(k_blk), [(src, v_blk + p) for p in range(2)], tq, s)

    def kern(*refs):
        q_refs, k_refs, v_refs, (do_ref, rm_ref, tri_ref, pre_ref, cnt_ref, dq_ref, dk_ref, dv_ref) = pp.split(refs)
        i = pl.program_id(0)

        @pl.when(i == 0)
        def _():
            dk_ref[...] = jnp.zeros_like(dk_ref)
            dv_ref[...] = jnp.zeros_like(dv_ref)

        rmb = rm_ref[...]
        tri = tri_ref[...]
        pre = pre_ref[...]
        qm, dom = [], []
        for p in range(2):
            qm += _masked_heads([q_refs[p][0][...]], ["pair"], p, MXU_DTYPE, scale)
            dom += _masked_heads([do_ref[:, p * PAIR:(p + 1) * PAIR]], ["pair"], p, MXU_DTYPE)

        def block(j, carry, valid):
            dqs, lefts = carry
            off = _blk_off(j, t)
            kbs = [k_refs[p][0][pl.ds(off, t), :] for p in range(2)]
            vbs = [v_refs[p][pl.ds(off, t), :] for p in range(2)]
            logs = [_sb_logs(qm[h], kbs[h // 2], valid) for h in range(N_HEADS)]
            tails = _tri_sums([lg[0] for lg in logs], tri)
            ws, gs = [], []
            for h in range(N_HEADS):
                lk, ls = logs[h]
                w = jnp.exp(ls + tails[h] + _col(rmb, (rm.shape[1] // N_HEADS) * h + j))
                if valid is not None:
                    w = jnp.where(valid, w, 0.0)
                ws.append(w)
                gs.append(_dot_nt(dom[h], vbs[h // 2]) * w)
            prefix = _tri_sums(gs, pre)
            new_dq, new_left = [], []
            dk_acc = [jnp.zeros((t, PAIR), F32) for _ in range(2)]
            dv_acc = [jnp.zeros((t, PAIR), F32) for _ in range(2)]
            for h in range(N_HEADS):
                lk, ls = logs[h]
                sig = jnp.exp(ls)
                dz = gs[h] * (1.0 - sig) - sig * (prefix[h] + lefts[h])
                if valid is not None:
                    dz = jnp.where(valid, dz, 0.0)
                dzb = dz.astype(MXU_DTYPE)
                dv_acc[h // 2] = dv_acc[h // 2] + _dot_tn(ws[h].astype(MXU_DTYPE), dom[h])
                dk_acc[h // 2] = dk_acc[h // 2] + _dot_tn(dzb, qm[h])
                new_dq.append(dqs[h] + _dot(dzb, kbs[h // 2]))
                new_left.append(lefts[h] + jnp.sum(gs[h], axis=1, keepdims=True))
            for p in range(2):
                dv_ref[pl.ds(off, t), p * PAIR:(p + 1) * PAIR] += dv_acc[p]
                dk_ref[pl.ds(off, t), p * PAIR:(p + 1) * PAIR] += dk_acc[p]
            return tuple(new_dq), tuple(new_left)

        carry = (tuple(jnp.zeros((tq, PAIR), F32) for _ in range(N_HEADS)),
                 tuple(jnp.zeros((tq, 1), F32) for _ in range(N_HEADS)))
        carry = lax.fori_loop(band * i - cnt_ref[i], band * i, lambda j, c: block(j, c, None), carry)
        for d in range(band):
            carry = block(band * i + d, carry, _sb_valid(d, tq, t))
        for p in range(2):
            dq_ref[:, p * PAIR:(p + 1) * PAIR] = _take_heads([carry[0][2 * p + e] * scale for e in range(2)], p)

    mspec = pl.BlockSpec((t, t), lambda i: (0, 0))
    acc_spec = pl.BlockSpec((s, GROUP_W), lambda i: (0, 0), pipeline_mode=pl.Buffered(1))
    return pl.pallas_call(
        kern, name=name, grid=(s // tq,),
        in_specs=pp.q_specs + pp.k_specs + pp.v_specs + [_row_spec(tq, do_blk, GROUP_W), _row_spec(tq, 0, rm.shape[1]), mspec, mspec,
                                                         pl.BlockSpec(memory_space=pltpu.SMEM)],
        out_specs=[_row_spec(tq, 0, GROUP_W), acc_spec, acc_spec],
        out_shape=[jax.ShapeDtypeStruct((s, GROUP_W), F32)] * 3,
        compiler_params=_att_params(False),
    )(*(pp.q_args + pp.k_args + pp.v_args + [do, rm, _tri(t, "row_gt_col"), _tri(t, "row_lt_col"), visited]))


def _split_w_in(w):
    col = lambda n: w[:, _OFF[n]:_OFF[n + 1]]
    wa = jnp.concatenate([col(0), col(1), col(2), col(4), col(5), col(6), col(10)], axis=1)
    misc = jnp.concatenate([col(3), col(9), jnp.zeros((w.shape[0], 128 - 4 - MLA_ROPE), w.dtype)], axis=1)
    wb = jnp.concatenate([col(11), col(7), col(8), misc], axis=1)
    return wa, wb


def _merge_dw_in(dwp):
    a = lambda n: dwp[:, n * GROUP_W:(n + 1) * GROUP_W]
    b0 = PA_COLS
    gate = dwp[:, b0:b0 + 1024]
    cq = dwp[:, b0 + 1024:b0 + 1280]
    ckv = dwp[:, b0 + 1280:b0 + 1408]
    flog = dwp[:, b0 + 1408:b0 + 1412]
    krot = dwp[:, b0 + 1408 + MISC_KROT:b0 + 1408 + MISC_KROT + MLA_ROPE]
    return jnp.concatenate([a(0), a(1), a(2), flog, a(3), a(4), a(5), cq, ckv, krot, a(6), gate], axis=1)


def _heads_first(w, per_head, first):
    r = w.shape[0]
    w3 = w.reshape(r, N_HEADS, per_head)
    return jnp.concatenate([w3[:, :, :first].reshape(r, -1), w3[:, :, first:].reshape(r, -1)], axis=1)


def _heads_interleaved(w, per_head, first):
    r = w.shape[0]
    a = w[:, :N_HEADS * first].reshape(r, N_HEADS, first)
    b = w[:, N_HEADS * first:].reshape(r, N_HEADS, per_head - first)
    return jnp.concatenate([a, b], axis=2).reshape(r, N_HEADS * per_head)


def _pad_rows8(a):
    return a[:, :8].T


def _local_step(x2, mem2, tgt, p):
    s = x2.shape[0]
    nm = mem2.shape[0]
    head_scale = HEAD_DIM ** -0.5
    mla_scale = (HEAD_DIM + MLA_ROPE) ** -0.5
    tabs = _rope_tables(s)
    mats = _rope_matrices()
    pairs = lambda arr, blk: [[(arr, blk + q, "pair")] for q in range(2)]
    vals = lambda arr, blk: [(arr, blk + q) for q in range(2)]

    h, hb = _ln_fwd(x2, None, p["ln_in_g"], p["ln_in_b"], "ln_in_fwd")
    _, memn_b = _ln_fwd(mem2, None, p["mem_ln_g"], p["mem_ln_b"], "ln_mem_fwd")

    saved = []
    for l in range(DEPTH):
        wa, wb = _split_w_in(p["w_in"][l])
        wp = jnp.concatenate([wa, wb], axis=1)
        wq_up = _heads_first(p["w_mla_q_up"][l], HEAD_DIM + MLA_ROPE, HEAD_DIM)
        wkv_up = _heads_first(p["w_mla_kv_up"][l], 2 * HEAD_DIM, HEAD_DIM)
        bias_row = jnp.pad(p["b_forget"][l], (0, 128 - N_HEADS)).reshape(1, 128)
        pa, pb, fox_kmax = _proj_in(hb, wa, wb, GROUP_W)

        fc = _forget_fwd(pb, bias_row)
        fbias = (fc, _pad_rows8(fc), fox_kmax)
        o_fox, lse_fox = _softmax_fwd(pairs(pa, 0), pairs(pa, 2), vals(pa, 4), s, s, head_scale, True, fbias,
                                      "fox_fwd")
        o_sb, *rm_sb = _sb_fwd(pa, 6, 8, 10, s, head_scale, "sb_fwd")

        qfull, cqn = _mla_q_fwd(pb, p["mla_q_norm_g"][l], wq_up, tabs, mats)
        kv, ckvn, kr4 = _mla_kv_fwd(pb, p["mla_kv_norm_g"][l], wkv_up, tabs, mats)
        mla_q = [[(qfull, q, "pair"), (qfull, 2, "quad")] for q in range(2)]
        mla_k = [[(kv, q, "pair"), (kr4, 0, "quad")] for q in range(2)]
        o_mla, lse_mla = _softmax_fwd(mla_q, mla_k, vals(kv, 2), s, s, mla_scale, True, None, "mla_fwd")

        mkv = _matmul(memn_b, p["w_mem_kv"][l], MXU_DTYPE, "mem_kv")
        o_mem, lse_mem = _softmax_fwd(pairs(pa, 12), pairs(mkv, 0), vals(mkv, 2), s, nm, head_scale, False, None,
                                      "mem_fwd")

        groups = (o_fox, o_sb, o_mla, o_mem)
        if l < DEPTH - 1:
            gated, u, h_next, hb_next = _gate_out_proj_ln(groups, pb, p["w_out"][l], h, p["ln_g"][l], p["ln_b"][l])
        else:
            gated, u, dh, sq_cols = _gate_out_proj_ln(groups, pb, p["w_out"][l], h, p["ln_g"][l], p["ln_b"][l], tgt)
        saved.append(dict(u=u, hb=hb, wp=wp, wq_up=wq_up, wkv_up=wkv_up, bias_row=bias_row, pa=pa, pb=pb,
                          fbias=fbias, lse_fox=lse_fox, rm_sb=rm_sb, cqn=cqn, ckvn=ckvn, mla_q=mla_q, mla_k=mla_k,
                          kv=kv, lse_mla=lse_mla, mkv=mkv, lse_mem=lse_mem, groups=groups, gated=gated))
        if l < DEPTH - 1:
            h, hb = h_next, hb_next

    loss_sum = jnp.sum(sq_cols)

    grads = {k: [None] * DEPTH for k in ("w_in", "b_forget", "mla_q_norm_g", "w_mla_q_up", "mla_kv_norm_g",
                                         "w_mla_kv_up", "w_mem_kv", "w_out", "ln_g", "ln_b")}
    dmemn = []
    dy1, dy2, c1 = dh, None, 1.0
    for l in reversed(range(DEPTH)):
        r = saved[l]
        pa, pb = r["pa"], r["pb"]
        o_fox, o_sb, o_mla, o_mem = r["groups"]
        du, du_b, dg, db = _ln_bwd(r["u"], None, p["ln_g"][l], dy1, dy2, c1, "ln_bwd")
        grads["ln_g"][l], grads["ln_b"][l] = dg[0], db[0]
        grads["w_out"][l] = _matmul(r["gated"], du_b, F32, "out_proj_dw", "tn")
        dmixed, dgate_b = _out_proj_gate_bwd(du_b, p["w_out"][l], r["groups"], pb)

        dfq, dfk, dfv, dfc_q, dfc_k = _softmax_bwd(pairs(pa, 0), pairs(pa, 2), vals(pa, 4), o_fox, r["lse_fox"],
                                                   dmixed, 0, s, s, head_scale, True, r["fbias"], "fox_bwd")
        dmisc_f, dbf = _forget_bwd(pb, r["bias_row"], dfc_q + jnp.pad(dfc_k.T, ((0, 0), (0, 128 - 8))))
        grads["b_forget"][l] = dbf[0, :N_HEADS]

        dsq, dsk, dsv = _sb_bwd(pa, 6, 8, 10, dmixed, 1, *r["rm_sb"], s, head_scale, "sb_bwd")

        dqm, dkm, dvm = _softmax_bwd(r["mla_q"], r["mla_k"], vals(r["kv"], 2), o_mla, r["lse_mla"], dmixed, 2,
                                     s, s, mla_scale, True, None, "mla_bwd")
        dq_mla_b, dcq_b, dgq = _mla_q_bwd(dqm, pb, p["mla_q_norm_g"][l], r["wq_up"], tabs, mats)
        grads["w_mla_q_up"][l] = _heads_interleaved(_matmul(r["cqn"], dq_mla_b, F32, "q_up_dw", "tn"),
                                                    HEAD_DIM + MLA_ROPE, HEAD_DIM)
        grads["mla_q_norm_g"][l] = dgq[0]
        dkv_b, dckv_b, dmisc_k, dgkv = _mla_kv_bwd(dkm, dvm, pb, p["mla_kv_norm_g"][l], r["wkv_up"], tabs, mats)
        grads["w_mla_kv_up"][l] = _heads_interleaved(_matmul(r["ckvn"], dkv_b, F32, "kv_up_dw", "tn"),
                                                     2 * HEAD_DIM, HEAD_DIM)
        grads["mla_kv_norm_g"][l] = dgkv[0]

        dmq, dmk, dmv = _softmax_bwd(pairs(pa, 12), pairs(r["mkv"], 0), vals(r["mkv"], 2), o_mem, r["lse_mem"],
                                     dmixed, 3, s, nm, head_scale, False, None, "mem_bwd")
        dmkv_b = jnp.concatenate([dmk, dmv], axis=1).astype(MXU_DTYPE)
        grads["w_mem_kv"][l] = _matmul(memn_b, dmkv_b, F32, "mem_kv_dw", "tn")
        dmemn.append(_matmul(dmkv_b, p["w_mem_kv"][l], F32, "mem_kv_dx", "nt"))

        dp = [dfq, dfk, dfv, dsq, dsk, dsv, dmq, dgate_b, dcq_b, dckv_b, (dmisc_f, dmisc_k)]
        dhproj = _proj_dx(dp, r["wp"])
        grads["w_in"][l] = _merge_dw_in(_proj_dw(r["hb"], dp))
        dy1, dy2, c1 = du, dhproj, ALPHA

    dx, _, dg_in, db_in = _ln_bwd(x2, None, p["ln_in_g"], dy1, dy2, c1, "ln_in_bwd")
    _, _, dg_mem, db_mem = _ln_bwd(mem2, None, p["mem_ln_g"], dmemn[0], dmemn[1], 1.0, "ln_mem_bwd")
    out = {k: jnp.stack(v) for k, v in grads.items()}
    out.update(ln_in_g=dg_in[0], ln_in_b=db_in[0], mem_ln_g=dg_mem[0], mem_ln_b=db_mem[0])
    return loss_sum, dx, out


WIDE = "w_in"
FLAT_NAMES = ("w_out", "w_mem_kv", "w_mla_q_up", "w_mla_kv_up")
FLAT_ROWS = 896
BIG_NAMES = (WIDE,) + FLAT_NAMES
BIG_AXIS = dict(w_in=2, w_out=1, w_mem_kv=1, w_mla_q_up=2, w_mla_kv_up=2)
SMALL_NAMES = ("ln_in_g", "ln_in_b", "mem_ln_g", "mem_ln_b", "ln_g", "ln_b", "b_forget", "mla_q_norm_g",
               "mla_kv_norm_g")
ALL_NAMES = ("ln_in_g", "ln_in_b", "mem_ln_g", "mem_ln_b", "w_in", "b_forget", "mla_q_norm_g", "w_mla_q_up",
             "mla_kv_norm_g", "w_mla_kv_up", "w_mem_kv", "w_out", "ln_g", "ln_b")
N_CHIPS = 4
N_DEV = 8


def _rows_of(shape):
    rows = -(-int(np.prod(shape)) // LANES)
    return -(-rows // PACK_ALIGN) * PACK_ALIGN


def _pack(arrs, rows):
    parts = []
    for a in arrs:
        f = a.reshape(-1)
        n = _rows_of(a.shape) * LANES
        parts.append(jnp.pad(f, (0, n - f.shape[0])).reshape(-1, LANES))
    used = sum(q.shape[0] for q in parts)
    if rows > used:
        parts.append(jnp.zeros((rows - used, LANES), parts[0].dtype))
    return jnp.concatenate(parts, axis=0)


def _unpack(buf, shapes):
    out, r = [], 0
    for shp in shapes:
        n = _rows_of(shp)
        out.append(buf[r:r + n].reshape(-1)[:int(np.prod(shp))].reshape(shp))
        r += n
    return out


def _sharded_pair(get):
    wide = get(WIDE)
    return [wide.reshape(-1, wide.shape[-1]), _pack([get(n) for n in FLAT_NAMES], FLAT_ROWS)]


HBM_SPEC = pl.BlockSpec(memory_space=pltpu.HBM)


def _gather_weights(shards):
    n = len(shards)

    def body(*refs):
        w_refs, out_refs, (send_sems, recv_sems, local_sems) = refs[:n], refs[n:2 * n], refs[2 * n:]
        x, y, c = (lax.axis_index(a) for a in MESH_AXES)
        me, sibling = 2 * x + y, (x, y, 1 - c)
        chips = [(1 - x, y), (x, 1 - y), (1 - x, 1 - y)]
        local, first, passed = [], [], []
        for a in range(n):
            w_ref, out_ref, half = w_refs[a], out_refs[a], shards[a].shape[0] // 2

            def part(chip, core, out_ref=out_ref, half=half):
                return out_ref.at[chip, pl.ds(core * half, half)]

            def copy(k, src, dst, to, a=a):
                return pltpu.make_async_remote_copy(
                    src_ref=src, dst_ref=dst, send_sem=send_sems.at[6 * a + k], recv_sem=recv_sems.at[6 * a + k],
                    device_id=to, device_id_type=pl.DeviceIdType.MESH)

            local.append(pltpu.make_async_copy(w_ref, out_ref.at[me], local_sems.at[a]))
            local[-1].start()
            mine = [copy(k, w_ref.at[pl.ds(c * half, half)], part(me, c), (px, py, c))
                    for k, (px, py) in enumerate(chips)]
            for cp in mine:
                cp.start()
            first.append((mine, part, copy))
        for mine, part, copy in first:
            for k, (px, py) in enumerate(chips):
                copy(k, part(me, c), part(2 * px + py, c), (px, py, c)).wait_recv()
                passed.append(copy(3 + k, part(2 * px + py, c), part(2 * px + py, c), sibling))
                passed[-1].start()
        for mine, part, copy in first:
            for k, (px, py) in enumerate(chips):
                copy(3 + k, part(me, c), part(2 * px + py, 1 - c), sibling).wait_recv()
        for cp in [cp for mine, _, _ in first for cp in mine] + passed:
            cp.wait_send()
        for cp in local:
            cp.wait()

    return pl.pallas_call(
        body, name="gather_weights",
        out_shape=[jax.ShapeDtypeStruct((N_CHIPS,) + s.shape, s.dtype) for s in shards],
        in_specs=[HBM_SPEC] * n, out_specs=[HBM_SPEC] * n,
        scratch_shapes=[pltpu.SemaphoreType.DMA((6 * n,)), pltpu.SemaphoreType.DMA((6 * n,)),
                        pltpu.SemaphoreType.DMA((n,))],
    )(*shards)


def _exchange_grads(bigs, small):
    nb = len(bigs)
    halves = [b.shape[1] // 2 for b in bigs]
    chunks = [_pick(h, (128, 64, 32, 16)) for h in halves]

    def body(*refs):
        big_refs, small_ref = refs[:nb], refs[nb]
        sum_refs, out_refs = refs[nb + 1:2 * nb + 1], refs[2 * nb + 1:3 * nb + 2]
        scratch = refs[3 * nb + 2:]
        land_bufs, sum_bufs = scratch[:nb], scratch[nb:2 * nb]
        send_sems, recv_sems, local_sems, load_sem, swap_send, swap_recv, keep_sems = scratch[2 * nb:]
        x, y, c = (lax.axis_index(a) for a in MESH_AXES)
        me, my_chip = 4 * x + 2 * y + c, 2 * x + y
        flips = [(fx, fy, fc) for fx in (0, 1) for fy in (0, 1) for fc in (0, 1) if fx or fy or fc]
        peers = [(1 - x if fx else x, 1 - y if fy else y, 1 - c if fc else c) for fx, fy, fc in flips]

        sources = [lambda chip, core, r=big_refs[a], half=bigs[a].shape[1] // 2: r.at[chip, pl.ds(core * half, half)]
                   for a in range(nb)] + [lambda chip, core: small_ref]

        def copy(a, k, src, slot, to):
            return pltpu.make_async_remote_copy(
                src_ref=src, dst_ref=out_refs[a].at[slot], send_sem=send_sems.at[7 * a + k],
                recv_sem=recv_sems.at[7 * a + k], device_id=to, device_id_type=pl.DeviceIdType.MESH)

        own = [pltpu.make_async_copy(src(my_chip, c), out_refs[a].at[me], local_sems.at[a])
               for a, src in enumerate(sources)]
        for cp in own:
            cp.start()
        sends = [copy(a, k, src(2 * px + py, pc), me, (px, py, pc))
                 for a, src in enumerate(sources) for k, (px, py, pc) in enumerate(peers)]
        for cp in sends:
            cp.start()
        for a, src in enumerate(sources):
            for k, (px, py, pc) in enumerate(peers):
                copy(a, k, src(my_chip, c), 4 * px + 2 * py + pc, (px, py, pc)).wait_recv()
        for cp in sends:
            cp.wait_send()
        for cp in own:
            cp.wait()

        tails = []
        for a in range(nb):
            for r0 in range(0, halves[a], chunks[a]):
                load = pltpu.make_async_copy(out_refs[a].at[:, pl.ds(r0, chunks[a])], land_bufs[a], load_sem)
                load.start()
                load.wait()
                total = land_bufs[a][0].astype(F32)
                for d in range(1, N_DEV):
                    total = total + land_bufs[a][d].astype(F32)
                sum_bufs[a][pl.ds(r0, chunks[a]), :] = total
            keep = pltpu.make_async_copy(sum_bufs[a], sum_refs[a].at[c], keep_sems.at[a])
            give = pltpu.make_async_remote_copy(
                src_ref=sum_bufs[a], dst_ref=sum_refs[a].at[c], send_sem=swap_send.at[a], recv_sem=swap_recv.at[a],
                device_id=(x, y, 1 - c), device_id_type=pl.DeviceIdType.MESH)
            keep.start()
            give.start()
            tails.append((keep, give))
        for a, (keep, give) in enumerate(tails):
            pltpu.make_async_remote_copy(
                src_ref=sum_bufs[a], dst_ref=sum_refs[a].at[1 - c], send_sem=swap_send.at[a], recv_sem=swap_recv.at[a],
                device_id=(x, y, 1 - c), device_id_type=pl.DeviceIdType.MESH).wait_recv()
            give.wait_send()
            keep.wait()

    return pl.pallas_call(
        body, name="exchange_grads",
        out_shape=[jax.ShapeDtypeStruct((2, h, b.shape[2]), F32) for h, b in zip(halves, bigs)]
        + [jax.ShapeDtypeStruct((N_DEV, h, b.shape[2]), b.dtype) for h, b in zip(halves, bigs)]
        + [jax.ShapeDtypeStruct((N_DEV,) + small.shape, small.dtype)],
        in_specs=[HBM_SPEC] * (nb + 1), out_specs=[HBM_SPEC] * (2 * nb + 1),
        scratch_shapes=[pltpu.VMEM((N_DEV, ch, b.shape[2]), b.dtype) for ch, b in zip(chunks, bigs)]
        + [pltpu.VMEM((h, b.shape[2]), F32) for h, b in zip(halves, bigs)]
        + [pltpu.SemaphoreType.DMA((7 * (nb + 1),)), pltpu.SemaphoreType.DMA((7 * (nb + 1),)),
           pltpu.SemaphoreType.DMA((nb + 1,)), pltpu.SemaphoreType.DMA, pltpu.SemaphoreType.DMA((nb,)),
           pltpu.SemaphoreType.DMA((nb,)), pltpu.SemaphoreType.DMA((nb,))],
        compiler_params=pltpu.CompilerParams(vmem_limit_bytes=VMEM_LIMIT),
    )(*bigs, small)


def _adamw(parts, w, m, v, name):
    rows, width = w.shape
    n_parts = parts.shape[0]
    tile = _pick(rows, (128, 16, 8))
    bc1 = 1.0 - ADAM_B1 ** ADAM_STEP
    bc2 = 1.0 - ADAM_B2 ** ADAM_STEP

    def kern(p_ref, w_ref, m_ref, v_ref, g_ref, d_ref, nm_ref, nv_ref):
        g = p_ref[0].astype(F32)
        for d in range(1, n_parts):
            g = g + p_ref[d].astype(F32)
        nm = ADAM_B1 * m_ref[...] + (1.0 - ADAM_B1) * g
        nv = ADAM_B2 * v_ref[...] + (1.0 - ADAM_B2) * (g * g)
        g_ref[...] = g
        nm_ref[...] = nm
        nv_ref[...] = nv
        d_ref[...] = -ADAM_LR * ((nm / bc1) / (jnp.sqrt(nv / bc2) + ADAM_EPS) + ADAM_WD * w_ref[...])

    spec = pl.BlockSpec((tile, width), lambda i: (i, 0))
    return pl.pallas_call(
        kern, name=name, grid=(rows // tile,),
        in_specs=[pl.BlockSpec((n_parts, tile, width), lambda i: (0, i, 0)), spec, spec, spec],
        out_specs=[spec] * 4, out_shape=[jax.ShapeDtypeStruct((rows, width), F32)] * 4,
        compiler_params=pltpu.CompilerParams(dimension_semantics=("parallel",), vmem_limit_bytes=VMEM_LIMIT),
    )(parts, w, m, v)


def kernel(x, mem, ln_in_g, ln_in_b, mem_ln_g, mem_ln_b, w_in, b_forget, mla_q_norm_g, w_mla_q_up, mla_kv_norm_g, w_mla_kv_up, w_mem_kv, w_out, ln_g, ln_b, loss_target, m_ln_in_g, m_ln_in_b, m_mem_ln_g, m_mem_ln_b, m_w_in, m_b_forget, m_mla_q_norm_g, m_w_mla_q_up, m_mla_kv_norm_g, m_w_mla_kv_up, m_w_mem_kv, m_w_out, m_ln_g, m_ln_b, v_ln_in_g, v_ln_in_b, v_mem_ln_g, v_mem_ln_b, v_w_in, v_b_forget, v_mla_q_norm_g, v_w_mla_q_up, v_mla_kv_norm_g, v_w_mla_kv_up, v_w_mem_kv, v_w_out, v_ln_g, v_ln_b):
    w = dict(ln_in_g=ln_in_g, ln_in_b=ln_in_b, mem_ln_g=mem_ln_g, mem_ln_b=mem_ln_b, w_in=w_in, b_forget=b_forget,
             mla_q_norm_g=mla_q_norm_g, w_mla_q_up=w_mla_q_up, mla_kv_norm_g=mla_kv_norm_g,
             w_mla_kv_up=w_mla_kv_up, w_mem_kv=w_mem_kv, w_out=w_out, ln_g=ln_g, ln_b=ln_b)
    mo = dict(ln_in_g=m_ln_in_g, ln_in_b=m_ln_in_b, mem_ln_g=m_mem_ln_g, mem_ln_b=m_mem_ln_b, w_in=m_w_in,
              b_forget=m_b_forget, mla_q_norm_g=m_mla_q_norm_g, w_mla_q_up=m_w_mla_q_up,
              mla_kv_norm_g=m_mla_kv_norm_g, w_mla_kv_up=m_w_mla_kv_up, w_mem_kv=m_w_mem_kv, w_out=m_w_out,
              ln_g=m_ln_g, ln_b=m_ln_b)
    vo = dict(ln_in_g=v_ln_in_g, ln_in_b=v_ln_in_b, mem_ln_g=v_mem_ln_g, mem_ln_b=v_mem_ln_b, w_in=v_w_in,
              b_forget=v_b_forget, mla_q_norm_g=v_mla_q_norm_g, w_mla_q_up=v_w_mla_q_up,
              mla_kv_norm_g=v_mla_kv_norm_g, w_mla_kv_up=v_w_mla_kv_up, w_mem_kv=v_w_mem_kv, w_out=v_w_out,
              ln_g=v_ln_g, ln_b=v_ln_b)
    flat_shapes = [w[n].shape for n in FLAT_NAMES]
    small_shapes = [w[n].shape for n in SMALL_NAMES]

    got_wide, got_flat = _gather_weights(_sharded_pair(lambda n: w[n].astype(MXU_DTYPE)))
    full = dict(w)
    full[WIDE] = jnp.concatenate([got_wide[j] for j in range(N_CHIPS)], axis=1).reshape(
        w[WIDE].shape[:2] + (N_CHIPS * w[WIDE].shape[2],))
    per_chip = [_unpack(got_flat[j], flat_shapes) for j in range(N_CHIPS)]
    for idx, n in enumerate(FLAT_NAMES):
        full[n] = jnp.concatenate([per_chip[j][idx] for j in range(N_CHIPS)], axis=BIG_AXIS[n])

    loss_sum, dx, g = _local_step(x[0], mem[0], loss_target[0], full)
    loss = lax.psum(loss_sum * (0.5 / D_MODEL), MESH_AXES)

    def shard_of(n, j):
        ax, size = BIG_AXIS[n], w[n].shape[BIG_AXIS[n]]
        return lax.slice_in_dim(g[n], j * size, (j + 1) * size, axis=ax).astype(MXU_DTYPE)

    per_dest = [_sharded_pair(lambda n, j=j: shard_of(n, j)) for j in range(N_CHIPS)]
    bigs = [jnp.stack([per_dest[j][a] for j in range(N_CHIPS)]) for a in range(2)]
    exchanged = _exchange_grads(bigs, _pack([g[n] for n in SMALL_NAMES], SMALL_ROWS))
    halves, small_parts = exchanged[:2], exchanged[-1]

    res = []
    for a, (grad, nm) in enumerate(zip(halves, ("adamw_wide", "adamw_flat"))):
        state = [_sharded_pair(lambda n, src=src: src[n])[a] for src in (w, mo, vo)]
        res.append(_adamw(grad.reshape((1,) + state[0].shape), *state, nm))
    res_small = _adamw(small_parts, *[_pack([src[n] for n in SMALL_NAMES], SMALL_ROWS) for src in (w, mo, vo)],
                       "adamw_replicated")
    outs = []
    for kind in range(4):
        vals = {WIDE: res[0][kind].reshape(w[WIDE].shape)}
        vals.update(zip(FLAT_NAMES, _unpack(res[1][kind], flat_shapes)))
        vals.update(zip(SMALL_NAMES, _unpack(res_small[kind], small_shapes)))
        outs += [vals[n] for n in ALL_NAMES]
    return (loss, dx[None], *outs)
```

```python
import functools

import numpy as np
import jax
import jax.numpy as jnp
from jax import lax
from jax.experimental import pallas as pl
from jax.experimental.pallas import tpu as pltpu

F32 = jnp.float32
MXU_DTYPE = jnp.bfloat16

DEPTH = 2
D_MODEL = 1024
GROUP_W = 256
N_HEADS = 4
HEAD_DIM = 64
MLA_Q_RANK = 256
MLA_KV_RANK = 128
MLA_ROPE = 32
MLA_Q_COLS = N_HEADS * (HEAD_DIM + MLA_ROPE)
MLA_KV_COLS = N_HEADS * 2 * HEAD_DIM
ROPE_THETA = 10000.0
LN_EPS = 1e-5
RMS_EPS = 1e-6
ALPHA = (2 * DEPTH) ** 0.25
ADAM_LR, ADAM_B1, ADAM_B2, ADAM_EPS, ADAM_WD, ADAM_STEP = 0.001, 0.9, 0.999, 1e-08, 0.01, 10

_SPLIT = (256, 256, 256, 4, 256, 256, 256, 256, 128, 32, 256, 1024)
_OFF = [int(o) for o in np.cumsum((0,) + _SPLIT)]
IN_COLS = _OFF[-1]
PA_COLS = 7 * GROUP_W
PB_COLS = 1024 + 256 + 128 + 128
PB_CQ_BLK, PB_CKV_BLK, PB_MISC_BLK = 4, 10, 11
MISC_KROT = 4

LANES = 1024
PACK_ALIGN = 16
SMALL_ROWS = 144
ROW_TILE = 256
PROJ_BWD_ROWS = 512
ATT_TILE = 256
SB_QUERY_TILE = 256
SOFTMAX_TILE = 512
DENSE_QUERY_TILE = 1024
FWD_PAIRS_PER_LOOP = 1
BWD_PAIRS_PER_LOOP = 2
PAIR = 128
SB_SLOT = PAIR // N_HEADS
VMEM_LIMIT = 56 * 1024 * 1024
MATMUL_VMEM = 30 * 1024 * 1024
NEG = -1e30
LOG2E = 1.4426950408889634
LN2 = 0.6931471805599453
EXP_UNDERFLOW = -104.0
DEAD_LOGIT = -110.0
REACH_SLACK = 1.0 + 2.0 ** -10
MESH_AXES = ("x", "y", "c")


def _dot(a, b):
    return jnp.dot(a, b, preferred_element_type=F32)


def _dot_nt(a, b):
    return lax.dot_general(a, b, (((1,), (1,)), ((), ())), preferred_element_type=F32)


def _dot_tn(a, b):
    return lax.dot_general(a, b, (((0,), (0,)), ((), ())), preferred_element_type=F32)


def _split2(x):
    hi = x.astype(MXU_DTYPE)
    lo = (x - hi.astype(F32)).astype(MXU_DTYPE)
    return hi, lo


def _split3(x):
    hi = x.astype(MXU_DTYPE)
    r = x - hi.astype(F32)
    mid = r.astype(MXU_DTYPE)
    lo = (r - mid.astype(F32)).astype(MXU_DTYPE)
    return hi, mid, lo


def _dot_exact_r(x, pm):
    hi, mid, lo = _split3(x)
    return _dot(hi, pm) + _dot(mid, pm) + _dot(lo, pm)


def _dot_exact_l(pm, x):
    hi, mid, lo = _split3(x)
    return _dot(pm, hi) + _dot(pm, mid) + _dot(pm, lo)


def _pick(dim, prefs):
    for p in prefs:
        if dim % p == 0:
            return p
    return dim


def _softplus(z):
    return jnp.maximum(z, 0.0) + jnp.log(1.0 + jnp.exp(-jnp.abs(z)))


def _tile_options(dim):
    opts = [d for d in range(128, min(dim, 2048) + 1, 128) if dim % d == 0]
    return opts or [dim]


def _matmul_tiles(m, n, k, out_bytes):
    tk = k if k <= 4096 else _pick(k, (1024, 512, 256, 128))
    best = None
    for tm in _tile_options(m):
        for tn in _tile_options(n):
            vmem = 2 * 2 * (tm * tk + tk * tn) + 4 * tm * tn + 2 * out_bytes * tm * tn
            if vmem <= MATMUL_VMEM and (best is None or tm * tn / (tm + tn) > best[0]):
                best = (tm * tn / (tm + tn), tm, tn)
    return best[1], best[2], tk


def _matmul(a, b, out_dtype, name, mode="nn"):
    m, k = (a.shape[1], a.shape[0]) if mode == "tn" else a.shape
    n = b.shape[0] if mode == "nt" else b.shape[1]
    tm, tn, tk = _matmul_tiles(m, n, k, jnp.dtype(out_dtype).itemsize)
    nk = k // tk
    dot = {"nn": _dot, "tn": _dot_tn, "nt": _dot_nt}[mode]

    def kern(a_ref, b_ref, o_ref, *acc):
        if nk == 1:
            o_ref[...] = dot(a_ref[...], b_ref[...]).astype(o_ref.dtype)
            return
        acc_ref, = acc
        kk = pl.program_id(2)

        @pl.when(kk == 0)
        def _():
            acc_ref[...] = jnp.zeros_like(acc_ref)

        acc_ref[...] += dot(a_ref[...], b_ref[...])

        @pl.when(kk == nk - 1)
        def _():
            o_ref[...] = acc_ref[...].astype(o_ref.dtype)

    a_spec = (pl.BlockSpec((tk, tm), lambda i, j, kk: (kk, i)) if mode == "tn"
              else pl.BlockSpec((tm, tk), lambda i, j, kk: (i, kk)))
    b_spec = (pl.BlockSpec((tn, tk), lambda i, j, kk: (j, kk)) if mode == "nt"
              else pl.BlockSpec((tk, tn), lambda i, j, kk: (kk, j)))
    return pl.pallas_call(
        kern, name=name, grid=(m // tm, n // tn, nk), in_specs=[a_spec, b_spec],
        out_specs=pl.BlockSpec((tm, tn), lambda i, j, kk: (i, j)),
        out_shape=jax.ShapeDtypeStruct((m, n), out_dtype),
        scratch_shapes=[pltpu.VMEM((tm, tn), F32)] if nk > 1 else [],
        compiler_params=pltpu.CompilerParams(
            dimension_semantics=("parallel", "parallel", "arbitrary"), vmem_limit_bytes=VMEM_LIMIT),
    )(a.astype(MXU_DTYPE), b.astype(MXU_DTYPE))


def _proj_in(hb, wa, wb, key_cols):
    rows, d = hb.shape
    tm = min(PROJ_BWD_ROWS, rows)

    def kern(a_ref, wa_ref, wb_ref, pa_ref, pb_ref, kmax_ref):
        @pl.when(pl.program_id(0) == 0)
        def _():
            kmax_ref[...] = jnp.zeros_like(kmax_ref)

        a = a_ref[...]
        pa = _dot(a, wa_ref[...]).astype(pa_ref.dtype)
        pa_ref[...] = pa
        pb_ref[...] = _dot(a, wb_ref[...])
        cols = {}
        for p in range(2):
            k32 = pa[:, key_cols + p * PAIR:key_cols + (p + 1) * PAIR].astype(F32)
            for e in range(2):
                sq = jnp.sum(jnp.where(_lane_mask("pair", e, tm), k32 * k32, 0.0), axis=1, keepdims=True)
                cols[2 * p + e] = jnp.max(sq, axis=0, keepdims=True)
        lane = lax.broadcasted_iota(jnp.int32, (1, PAIR), 1)
        kmax_ref[...] = jnp.maximum(kmax_ref[...], sum(jnp.where(lane == h, c, 0.0) for h, c in cols.items()))

    whole = lambda w: pl.BlockSpec(w.shape, lambda i: (0, 0), pipeline_mode=pl.Buffered(1))
    return pl.pallas_call(
        kern, name="proj_in", grid=(rows // tm,),
        in_specs=[pl.BlockSpec((tm, d), lambda i: (i, 0)), whole(wa), whole(wb)],
        out_specs=[pl.BlockSpec((tm, wa.shape[1]), lambda i: (i, 0)), pl.BlockSpec((tm, wb.shape[1]), lambda i: (i, 0)),
                   pl.BlockSpec((1, PAIR), lambda i: (0, 0))],
        out_shape=[jax.ShapeDtypeStruct((rows, wa.shape[1]), MXU_DTYPE), jax.ShapeDtypeStruct((rows, wb.shape[1]), F32),
                   jax.ShapeDtypeStruct((1, PAIR), F32)],
        compiler_params=pltpu.CompilerParams(dimension_semantics=("arbitrary",), vmem_limit_bytes=VMEM_LIMIT),
    )(hb, wa.astype(MXU_DTYPE), wb.astype(MXU_DTYPE))


def _piece_arrays(pieces):
    return [a for p in pieces for a in (p if isinstance(p, tuple) else (p,))]


def _join_pieces(refs, pieces):
    refs, cols = list(refs), []
    for p in pieces:
        vals = [refs.pop(0)[...] for _ in (p if isinstance(p, tuple) else (p,))]
        cols.append(functools.reduce(jnp.add, vals).astype(MXU_DTYPE))
    return jnp.concatenate(cols, axis=1)


def _proj_dx(pieces, w):
    arrs = _piece_arrays(pieces)
    rows, n = arrs[0].shape[0], w.shape[0]
    tm = min(PROJ_BWD_ROWS, rows)

    def kern(*refs):
        refs[-1][...] = _dot_nt(_join_pieces(refs[:len(arrs)], pieces), refs[len(arrs)][...])

    return pl.pallas_call(
        kern, name="proj_dx", grid=(rows // tm,),
        in_specs=[pl.BlockSpec((tm, a.shape[1]), lambda i: (i, 0)) for a in arrs]
        + [pl.BlockSpec(w.shape, lambda i: (0, 0), pipeline_mode=pl.Buffered(1))],
        out_specs=pl.BlockSpec((tm, n), lambda i: (i, 0)), out_shape=jax.ShapeDtypeStruct((rows, n), F32),
        compiler_params=pltpu.CompilerParams(dimension_semantics=("parallel",), vmem_limit_bytes=VMEM_LIMIT),
    )(*arrs, w.astype(MXU_DTYPE))


def _proj_dw(a, pieces):
    arrs = _piece_arrays(pieces)
    rows, m = a.shape
    k = sum(x.shape[1] for x in (p[0] if isinstance(p, tuple) else p for p in pieces))
    tk = min(PROJ_BWD_ROWS, rows)

    def kern(*refs):
        o_ref = refs[-1]

        @pl.when(pl.program_id(0) == 0)
        def _():
            o_ref[...] = jnp.zeros_like(o_ref)

        o_ref[...] += _dot_tn(refs[0][...], _join_pieces(refs[1:1 + len(arrs)], pieces))

    return pl.pallas_call(
        kern, name="proj_dw", grid=(rows // tk,),
        in_specs=[pl.BlockSpec((tk, m), lambda i: (i, 0))]
        + [pl.BlockSpec((tk, x.shape[1]), lambda i: (i, 0)) for x in arrs],
        out_specs=pl.BlockSpec((m, k), lambda i: (0, 0), pipeline_mode=pl.Buffered(1)),
        out_shape=jax.ShapeDtypeStruct((m, k), F32),
        compiler_params=pltpu.CompilerParams(dimension_semantics=("arbitrary",), vmem_limit_bytes=VMEM_LIMIT),
    )(a, *arrs)


def _rowwise(body, name, rows, tile, row_ins, full_ins, row_outs, acc_outs=(), scratch=(),
             reverse=False, sequential=False):
    n = rows // tile

    def ridx(i):
        return (n - 1 - i) if reverse else i

    in_specs, args = [], []
    for arr, width, cb in row_ins:
        in_specs.append(pl.BlockSpec((tile, width), lambda i, cb=cb: (ridx(i), cb)))
        args.append(arr)
    for arr in full_ins:
        in_specs.append(pl.BlockSpec(arr.shape, lambda i, nd=arr.ndim: (0,) * nd))
        args.append(arr)
    out_shape = [jax.ShapeDtypeStruct((rows, w), dt) for w, dt in row_outs]
    out_shape += [jax.ShapeDtypeStruct(s, dt) for s, dt in acc_outs]
    out_specs = [pl.BlockSpec((tile, w), lambda i: (ridx(i), 0)) for w, dt in row_outs]
    out_specs += [pl.BlockSpec(s, lambda i, nd=len(s): (0,) * nd) for s, dt in acc_outs]

    def kern(*refs):
        body(pl.program_id(0), *refs)

    sem = "arbitrary" if (acc_outs or sequential) else "parallel"
    return pl.pallas_call(
        kern, name=name, grid=(n,), in_specs=in_specs, out_specs=out_specs, out_shape=out_shape,
        scratch_shapes=list(scratch),
        compiler_params=pltpu.CompilerParams(dimension_semantics=(sem,), vmem_limit_bytes=VMEM_LIMIT),
    )(*args)


def _ln_stats(u):
    mu = jnp.mean(u, axis=-1, keepdims=True)
    xc = u - mu
    var = jnp.mean(xc * xc, axis=-1, keepdims=True)
    return xc, lax.rsqrt(var + LN_EPS)


def _ln_fwd(a, b, g, beta, name):
    rows, d = a.shape
    has_b = b is not None

    def body(i, *refs):
        if has_b:
            a_ref, b_ref, g_ref, be_ref, h_ref, hb_ref = refs
            u = ALPHA * a_ref[...] + b_ref[...]
        else:
            a_ref, g_ref, be_ref, h_ref, hb_ref = refs
            u = a_ref[...]
        xc, rstd = _ln_stats(u)
        y = xc * rstd * g_ref[...] + be_ref[...]
        h_ref[...] = y
        hb_ref[...] = y.astype(hb_ref.dtype)

    row_ins = [(a, d, 0)] + ([(b, d, 0)] if has_b else [])
    return _rowwise(body, name, rows, min(ROW_TILE, rows), row_ins,
                    [g.reshape(1, d), beta.reshape(1, d)], [(d, F32), (d, MXU_DTYPE)])


def _ln_bwd(u, g, dy1, dy2, c1, name, gate=None):
    rows, d = u.shape
    has_2 = dy2 is not None
    n_groups = len(gate[1]) if gate else 0

    def body(i, *refs):
        refs = list(refs)
        uu = refs.pop(0)[...]
        dy = c1 * refs.pop(0)[...]
        if has_2:
            dy = dy + refs.pop(0)[...]
        if gate:
            mixed = jnp.concatenate([refs.pop(0)[...] for _ in range(n_groups)], axis=1)
            gt = refs.pop(0)[...]
            g_ref, w_ref, du_ref, dub_ref, dm_ref, dgate_ref, dg_ref, db_ref = refs
        else:
            g_ref, du_ref, dg_ref, db_ref = refs

        @pl.when(i == 0)
        def _():
            dg_ref[...] = jnp.zeros_like(dg_ref)
            db_ref[...] = jnp.zeros_like(db_ref)

        xc, rstd = _ln_stats(uu)
        xhat = xc * rstd
        dxh = dy * g_ref[...]
        m1 = jnp.mean(dxh, axis=-1, keepdims=True)
        m2 = jnp.mean(dxh * xhat, axis=-1, keepdims=True)
        du = rstd * (dxh - m1 - xhat * m2)
        du_ref[...] = du
        dg_ref[...] += jnp.sum(dy * xhat, axis=0, keepdims=True)
        db_ref[...] += jnp.sum(dy, axis=0, keepdims=True)
        if gate:
            dub = du.astype(dub_ref.dtype)
            dub_ref[...] = dub
            dgated = _dot_nt(dub, w_ref[...])
            sig = 1.0 / (1.0 + jnp.exp(-gt))
            dm_ref[...] = dgated * (gt * sig)
            dgate_ref[...] = (dgated * mixed * (sig * (1.0 + gt * (1.0 - sig)))).astype(dgate_ref.dtype)

    row_ins = [(u, d, 0), (dy1, d, 0)] + ([(dy2, d, 0)] if has_2 else [])
    full_ins = [g.reshape(1, d)]
    row_outs = [(d, F32)]
    if gate:
        w_out, groups, pb = gate
        w = GROUP_W * n_groups
        row_ins += [(o, GROUP_W, 0) for o in groups] + [(pb, w, 0)]
        full_ins += [w_out.astype(MXU_DTYPE)]
        row_outs += [(d, MXU_DTYPE), (w, F32), (w, MXU_DTYPE)]
    return _rowwise(body, name, rows, min(ROW_TILE, rows), row_ins, full_ins, row_outs,
                    [((1, d), F32), ((1, d), F32)])


def _gate_out_proj_ln(groups, pb, w_out, h, g, beta, target=None):
    rows, d = h.shape
    last = target is not None
    w = GROUP_W * len(groups)

    def body(i, *refs):
        refs = list(refs)
        mixed = jnp.concatenate([refs.pop(0)[...] for _ in groups], axis=1)
        gate = refs.pop(0)[...]
        if last:
            h_ref, t_ref, w_ref, g_ref, be_ref, a_ref, u_ref, dh_ref, acc_ref = refs

            @pl.when(i == 0)
            def _():
                acc_ref[...] = jnp.zeros_like(acc_ref)
        else:
            h_ref, w_ref, g_ref, be_ref, a_ref, u_ref, o_ref, ob_ref = refs
        gated = (mixed * (gate / (1.0 + jnp.exp(-gate)))).astype(a_ref.dtype)
        a_ref[...] = gated
        u = ALPHA * h_ref[...] + _dot(gated, w_ref[...])
        u_ref[...] = u
        xc, rstd = _ln_stats(u)
        y = xc * rstd * g_ref[...] + be_ref[...]
        if last:
            e = y - t_ref[...]
            dh_ref[...] = e * (1.0 / d)
            acc_ref[...] += jnp.sum(e * e, axis=0, keepdims=True)
        else:
            o_ref[...] = y
            ob_ref[...] = y.astype(ob_ref.dtype)

    row_ins = [(o, GROUP_W, 0) for o in groups] + [(pb, w, 0), (h, d, 0)] + ([(target, d, 0)] if last else [])
    full_ins = [w_out.astype(MXU_DTYPE), g.reshape(1, d), beta.reshape(1, d)]
    if last:
        return _rowwise(body, "gate_out_proj_ln_loss", rows, PROJ_BWD_ROWS, row_ins, full_ins,
                        [(w, MXU_DTYPE), (d, F32), (d, F32)], [((1, d), F32)])
    return _rowwise(body, "gate_out_proj_ln", rows, PROJ_BWD_ROWS, row_ins, full_ins,
                    [(w, MXU_DTYPE), (d, F32), (d, F32), (d, MXU_DTYPE)])


def _tri(n, kind):
    r = np.arange(n)[:, None]
    c = np.arange(n)[None, :]
    m = {"lower_incl": r >= c, "upper_incl": r <= c, "row_gt_col": r > c, "row_lt_col": r < c}[kind]
    return jnp.asarray(m.astype(np.float32), dtype=MXU_DTYPE)


def _forget_fwd(pb, bias_row):
    rows = pb.shape[0]
    tile = _pick(rows, (1024, 512, 256))

    def body(i, x_ref, b_ref, l_ref, o_ref, carry_ref):
        @pl.when(i == 0)
        def _():
            carry_ref[...] = jnp.zeros_like(carry_ref)

        xx = x_ref[...] + b_ref[...]
        lane = lax.broadcasted_iota(jnp.int32, xx.shape, 1)
        lf = jnp.where(lane < N_HEADS, -_softplus(-xx), 0.0)
        o_ref[...] = _dot_exact_l(l_ref[...], lf) + carry_ref[...]
        carry_ref[...] += jnp.sum(lf, axis=0, keepdims=True)

    return _rowwise(body, "forget_fwd", rows, tile, [(pb, 128, PB_MISC_BLK)],
                    [bias_row, _tri(tile, "lower_incl")], [(128, F32)],
                    scratch=[pltpu.VMEM((1, 128), F32)], sequential=True)[0]


def _forget_bwd(pb, bias_row, dfc):
    rows = pb.shape[0]
    tile = _pick(rows, (1024, 512, 256))

    def body(i, x_ref, df_ref, b_ref, u_ref, o_ref, db_ref, carry_ref):
        @pl.when(i == 0)
        def _():
            carry_ref[...] = jnp.zeros_like(carry_ref)
            db_ref[...] = jnp.zeros_like(db_ref)

        df = df_ref[...]
        sfx = _dot_exact_l(u_ref[...], df) + carry_ref[...]
        carry_ref[...] += jnp.sum(df, axis=0, keepdims=True)
        xx = x_ref[...] + b_ref[...]
        lane = lax.broadcasted_iota(jnp.int32, xx.shape, 1)
        dl = jnp.where(lane < N_HEADS, sfx / (1.0 + jnp.exp(xx)), 0.0)
        o_ref[...] = dl
        db_ref[...] += jnp.sum(dl, axis=0, keepdims=True)

    return _rowwise(body, "forget_bwd", rows, tile,
                    [(pb, 128, PB_MISC_BLK), (dfc, 128, 0)],
                    [bias_row, _tri(tile, "upper_incl")], [(128, F32)], [((1, 128), F32)],
                    scratch=[pltpu.VMEM((1, 128), F32)], reverse=True, sequential=True)


def _rope_tables(s):
    half = MLA_ROPE // 2
    inv_freq = ROPE_THETA ** (-jnp.arange(half, dtype=F32) / half)
    ang = jnp.arange(s).astype(F32)[:, None] * inv_freq[None, :]
    cos2 = jnp.tile(jnp.cos(ang), (1, 2))
    sin2 = jnp.tile(jnp.sin(ang), (1, 2))
    cx = jnp.tile(cos2, (1, N_HEADS))
    sx = jnp.tile(sin2, (1, N_HEADS))
    pad = ((0, 0), (MISC_KROT, 128 - MISC_KROT - MLA_ROPE))
    ck = jnp.pad(cos2, pad)
    sk = jnp.pad(sin2, pad)
    return dict(ck=ck, sk=sk, cx=cx, sx=sx)


def _rot_matrix(width, bases):
    half = MLA_ROPE // 2
    p = np.zeros((width, width), np.float32)
    for b in bases:
        for i in range(half):
            p[b + half + i, b + i] = -1.0
            p[b + i, b + half + i] = 1.0
    return p


def _rope_matrices():
    pk = _rot_matrix(128, [MISC_KROT])
    p4 = _rot_matrix(128, [h * MLA_ROPE for h in range(N_HEADS)])
    a = np.zeros((128, 128), np.float32)
    for h in range(N_HEADS):
        for r in range(MLA_ROPE):
            a[h * MLA_ROPE + r, MISC_KROT + r] = 1.0
    cast = lambda m: jnp.asarray(m, dtype=MXU_DTYPE)
    return dict(p4=cast(p4), p4t=cast(p4.T), pk=cast(pk), spread=cast(a.T), xa=cast(a), xb=cast(p4.T @ a))


def _rms(c, g):
    r = lax.rsqrt(jnp.mean(c * c, axis=-1, keepdims=True) + RMS_EPS)
    return c * r * g


def _mla_q_fwd(pb, g, w_up, tabs, mats):
    rows = pb.shape[0]

    def body(i, c_ref, cos_ref, sin_ref, g_ref, w_ref, p_ref, q_ref, cn_ref):
        cn = _rms(c_ref[...], g_ref[...]).astype(cn_ref.dtype)
        cn_ref[...] = cn
        q = _dot(cn, w_ref[...])
        qr = q[:, GROUP_W:]
        q_ref[:, :GROUP_W] = q[:, :GROUP_W].astype(q_ref.dtype)
        q_ref[:, GROUP_W:] = (qr * cos_ref[...] + _dot_exact_r(qr, p_ref[...]) * sin_ref[...]).astype(q_ref.dtype)

    return _rowwise(body, "mla_q_fwd", rows, ROW_TILE,
                    [(pb, MLA_Q_RANK, PB_CQ_BLK), (tabs["cx"], PAIR, 0), (tabs["sx"], PAIR, 0)],
                    [g.reshape(1, MLA_Q_RANK), w_up.astype(MXU_DTYPE), mats["p4"]],
                    [(MLA_Q_COLS, MXU_DTYPE), (MLA_Q_RANK, MXU_DTYPE)])


def _mla_kv_fwd(pb, g, w_up, tabs, mats):
    rows = pb.shape[0]

    def body(i, c_ref, x_ref, cos_ref, sin_ref, g_ref, w_ref, p_ref, sp_ref, kv_ref, cn_ref, kr_ref):
        cn = _rms(c_ref[...], g_ref[...]).astype(cn_ref.dtype)
        cn_ref[...] = cn
        kv_ref[...] = _dot(cn, w_ref[...]).astype(kv_ref.dtype)
        xx = x_ref[...]
        kr = xx * cos_ref[...] + _dot_exact_r(xx, p_ref[...]) * sin_ref[...]
        kr_ref[...] = _dot_exact_r(kr, sp_ref[...]).astype(kr_ref.dtype)

    return _rowwise(body, "mla_kv_fwd", rows, ROW_TILE,
                    [(pb, MLA_KV_RANK, PB_CKV_BLK), (pb, 128, PB_MISC_BLK), (tabs["ck"], 128, 0), (tabs["sk"], 128, 0)],
                    [g.reshape(1, MLA_KV_RANK), w_up.astype(MXU_DTYPE), mats["pk"], mats["spread"]],
                    [(MLA_KV_COLS, MXU_DTYPE), (MLA_KV_RANK, MXU_DTYPE), (128, MXU_DTYPE)])


def _rms_bwd(c, g, dy):
    r = lax.rsqrt(jnp.mean(c * c, axis=-1, keepdims=True) + RMS_EPS)
    dyg = dy * g
    dc = r * dyg - c * (r * r * r) * jnp.mean(c * dyg, axis=-1, keepdims=True)
    return dc, jnp.sum(dy * c * r, axis=0, keepdims=True)


def _mla_q_bwd(dq_full, pb, g, w_up, tabs, mats):
    rows, nq = dq_full.shape

    def body(i, d_ref, c_ref, cos_ref, sin_ref, g_ref, w_ref, pt_ref, dq_ref, dc_ref, dg_ref):
        @pl.when(i == 0)
        def _():
            dg_ref[...] = jnp.zeros_like(dg_ref)

        dr = d_ref[:, GROUP_W:]
        dq = jnp.concatenate([d_ref[:, :GROUP_W], dr * cos_ref[...] + _dot_exact_r(dr * sin_ref[...], pt_ref[...])],
                             axis=1).astype(dq_ref.dtype)
        dq_ref[...] = dq
        dc, dg = _rms_bwd(c_ref[...], g_ref[...], _dot_nt(dq, w_ref[...]))
        dc_ref[...] = dc.astype(dc_ref.dtype)
        dg_ref[...] += dg

    return _rowwise(body, "mla_q_bwd", rows, ROW_TILE,
                    [(dq_full, nq, 0), (pb, MLA_Q_RANK, PB_CQ_BLK), (tabs["cx"], PAIR, 0), (tabs["sx"], PAIR, 0)],
                    [g.reshape(1, MLA_Q_RANK), w_up.astype(MXU_DTYPE), mats["p4t"]],
                    [(nq, MXU_DTYPE), (MLA_Q_RANK, MXU_DTYPE)], [((1, MLA_Q_RANK), F32)])


def _mla_kv_bwd(dk, dv, pb, g, w_up, tabs, mats):
    rows = dk.shape[0]

    def body(i, dk_ref, dr_ref, dv_ref, c_ref, cos_ref, sin_ref, g_ref, w_ref, a_ref, b_ref,
             dkv_ref, dc_ref, dm_ref, dg_ref):
        @pl.when(i == 0)
        def _():
            dg_ref[...] = jnp.zeros_like(dg_ref)

        dkv = jnp.concatenate([dk_ref[...], dv_ref[...]], axis=1).astype(dkv_ref.dtype)
        dkv_ref[...] = dkv
        dc, dg = _rms_bwd(c_ref[...], g_ref[...], _dot_nt(dkv, w_ref[...]))
        dc_ref[...] = dc.astype(dc_ref.dtype)
        dg_ref[...] += dg
        dr = dr_ref[...]
        dm_ref[...] = _dot_exact_r(dr * cos_ref[...], a_ref[...]) + _dot_exact_r(dr * sin_ref[...], b_ref[...])

    return _rowwise(body, "mla_kv_bwd", rows, ROW_TILE,
                    [(dk, GROUP_W, 0), (dk, PAIR, 2), (dv, GROUP_W, 0), (pb, MLA_KV_RANK, PB_CKV_BLK),
                     (tabs["cx"], PAIR, 0), (tabs["sx"], PAIR, 0)],
                    [g.reshape(1, MLA_KV_RANK), w_up.astype(MXU_DTYPE), mats["xa"], mats["xb"]],
                    [(MLA_KV_COLS, MXU_DTYPE), (MLA_KV_RANK, MXU_DTYPE), (PAIR, F32)], [((1, MLA_KV_RANK), F32)])


def _att_params(parallel):
    return pltpu.CompilerParams(dimension_semantics=("parallel" if parallel else "arbitrary",),
                                vmem_limit_bytes=VMEM_LIMIT)


def _blk_off(j, t):
    return j * t if isinstance(j, int) else pl.multiple_of(j * t, t)


def _causal_mask(t, strict):
    r = lax.broadcasted_iota(jnp.int32, (t, t), 0)
    c = lax.broadcasted_iota(jnp.int32, (t, t), 1)
    return (c < r) if strict else (c <= r)


def _lane_mask(kind, head, rows):
    lane = lax.broadcasted_iota(jnp.int32, (rows, PAIR), 1)
    if kind == "pair":
        return (lane < HEAD_DIM) if head % 2 == 0 else (lane >= HEAD_DIM)
    return (lane >= MLA_ROPE * head) & (lane < MLA_ROPE * (head + 1))


def _row_spec(t, cb, width=PAIR):
    return pl.BlockSpec((t, width), lambda i, cb=cb: (i, cb))


def _whole_spec(rows, cb, width=PAIR):
    return pl.BlockSpec((rows, width), lambda i, cb=cb: (0, cb), pipeline_mode=pl.Buffered(1))


def _is_pow2(x):
    return float(np.frexp(x)[0]) == 0.5


def _masked_heads(blocks, kinds, pair, dtype, scale=None):
    out = []
    for e in range(2):
        head = 2 * pair + e
        parts = [jnp.where(_lane_mask(k, head, b.shape[0]), b.astype(F32) * (1.0 if scale is None else scale),
                           0.0).astype(dtype)
                 for b, k in zip(blocks, kinds)]
        out.append(parts[0] if len(parts) == 1 else jnp.concatenate(parts, axis=1))
    return out


def _logit_reach(qh, kmax2, head):
    q32 = qh.astype(F32)
    return jnp.sqrt(jnp.sum(q32 * q32, axis=1, keepdims=True) * _col(kmax2, head)) * REACH_SLACK


def _forget_top(ft_ref, head, off):
    return jnp.max(-ft_ref[head:head + 1, pl.ds(off, PAIR)])


def _col(block, idx):
    lane = lax.broadcasted_iota(jnp.int32, block.shape, 1)
    return jnp.sum(jnp.where(lane == idx, block, 0.0), axis=1, keepdims=True)


def _scatter_cols(cols, t):
    lane = lax.broadcasted_iota(jnp.int32, (t, PAIR), 1)
    out = jnp.zeros((t, PAIR), F32)
    for idx, c in cols.items():
        out = out + jnp.where(lane == idx, c, 0.0)
    return out


def _take_heads(per_head, pair):
    return jnp.where(_lane_mask("pair", 0, per_head[0].shape[0]), per_head[0], per_head[1])


class _Parts:
    def __init__(self, q_parts, k_parts, v_parts, tq, sk):
        self.kinds = [[kind for _, _, kind in q_parts[p]] for p in range(2)]
        self.nparts = len(q_parts[0])
        self.q_specs = [_row_spec(tq, cb) for p in range(2) for _, cb, _ in q_parts[p]]
        self.q_args = [a for p in range(2) for a, _, _ in q_parts[p]]
        self.k_specs = [_whole_spec(sk, cb) for p in range(2) for _, cb, _ in k_parts[p]]
        self.k_args = [a for p in range(2) for a, _, _ in k_parts[p]]
        self.v_specs = [_whole_spec(sk, cb) for _, cb in v_parts]
        self.v_args = [a for a, _ in v_parts]
        self.width = PAIR * self.nparts

    def split(self, refs):
        n = self.nparts
        refs = list(refs)
        q = [refs[p * n:(p + 1) * n] for p in range(2)]
        k = [refs[2 * n + p * n:2 * n + (p + 1) * n] for p in range(2)]
        v = refs[4 * n:4 * n + 2]
        return q, k, v, refs[4 * n + 2:]

    def k_block(self, k_refs, off, t):
        blks = [r[pl.ds(off, t), :] for r in k_refs]
        return blks[0] if len(blks) == 1 else jnp.concatenate(blks, axis=1)


def _softmax_fwd(q_parts, k_parts, v_parts, sq, sk, scale, causal, bias, name):
    tq = min(SOFTMAX_TILE if causal else DENSE_QUERY_TILE, sq)
    tk = tq if causal else min(SOFTMAX_TILE, sk)
    nkv = sk // tk
    pp = _Parts(q_parts, k_parts, v_parts, tq, sk)

    def kern(*refs):
        q_refs, k_refs, v_refs, rest = pp.split(refs)
        if bias is not None:
            fc_ref, ft_ref, kmax_ref, o_ref, lse_ref = rest
            fcb = fc_ref[...]
        else:
            o_ref, lse_ref = rest
        i = pl.program_id(0)
        fold = _is_pow2(scale)
        head_on = [jnp.where(_lane_mask("pair", e, tk), 1.0, 0.0).astype(MXU_DTYPE) for e in range(2)]
        head_off = [jnp.where(_lane_mask("pair", e, tk), 0.0, 1.0).astype(MXU_DTYPE) for e in range(2)]
        lse_cols = {}
        for first in range(0, 2, FWD_PAIRS_PER_LOOP):
            pairs = list(range(first, first + FWD_PAIRS_PER_LOOP))
            heads = [2 * p + e for p in pairs for e in range(2)]
            qm = {}
            for p in pairs:
                masked_q = _masked_heads([r[...] for r in q_refs[p]], pp.kinds[p], p, MXU_DTYPE, scale if fold else None)
                qm.update({2 * p + e: masked_q[e] for e in range(2)})
            if bias is not None:
                reach = {h: _logit_reach(qm[h], kmax_ref[...], h) for h in heads}

            def block(j, carry, masked, pairs=pairs, qm=qm):
                off = _blk_off(j, tk)
                out = []
                for p in pairs:
                    kb = pp.k_block(k_refs[p], off, tk)
                    vb = v_refs[p][pl.ds(off, tk), :]
                    for e in range(2):
                        h = 2 * p + e
                        m, acc = carry[len(out)]
                        s = _dot_nt(qm[h], kb) * (LOG2E if fold else scale * LOG2E)
                        if bias is not None:
                            s = s - ft_ref[h:h + 1, pl.ds(off, tk)] * LOG2E
                        if masked:
                            s = jnp.where(_causal_mask(tq, False), s, NEG)
                        m_new = jnp.maximum(m, jnp.max(s, axis=1, keepdims=True))
                        pr = jnp.exp2(s - m_new).astype(MXU_DTYPE)
                        out.append((m_new, jnp.exp2(m - m_new) * acc + _dot(pr, vb * head_on[e] + head_off[e])))
                return tuple(out)

            carry = tuple((jnp.full((tq, 1), NEG, F32), jnp.zeros((tq, PAIR), F32)) for _ in heads)
            if causal and bias is not None:
                def alive(c, j, heads=heads, reach=reach):
                    return functools.reduce(jnp.maximum, [
                        jnp.max(reach[h] + _forget_top(ft_ref, h, _blk_off(j, tk)) - c[n][0] * LN2)
                        for n, h in enumerate(heads)])

                def step(state, block=block, alive=alive):
                    n, _, c = state
                    c = block(i - 1 - n, c, False)
                    return n + 1, alive(c, i - 1 - n), c

                carry = block(i, carry, True)
                _, _, carry = lax.while_loop(lambda st: jnp.logical_and(st[0] < i, st[1] > DEAD_LOGIT), step,
                                             (jnp.int32(0), alive(carry, i), carry))
            elif causal:
                carry = lax.fori_loop(0, i, lambda j, c, block=block: block(j, c, False), carry)
                carry = block(i, carry, True)
            else:
                for j in range(nkv):
                    carry = block(j, carry, False)
            for n, p in enumerate(pairs):
                outs = []
                for e in range(2):
                    m, acc = carry[2 * n + e]
                    l = _col(acc, HEAD_DIM * (1 - e))
                    outs.append(acc / l)
                    lse_cols[2 * p + e] = m * LN2 + jnp.log(l) + (_col(fcb, 2 * p + e) if bias is not None else 0.0)
                o_ref[:, p * PAIR:(p + 1) * PAIR] = _take_heads(outs, p)
        lse_ref[...] = _scatter_cols(lse_cols, tq)

    in_specs = pp.q_specs + pp.k_specs + pp.v_specs
    args = pp.q_args + pp.k_args + pp.v_args
    if bias is not None:
        in_specs += [_row_spec(tq, 0), pl.BlockSpec((8, sk), lambda i: (0, 0), pipeline_mode=pl.Buffered(1)),
                     pl.BlockSpec((1, PAIR), lambda i: (0, 0))]
        args += list(bias)
    return pl.pallas_call(
        kern, name=name, grid=(sq // tq,), in_specs=in_specs,
        out_specs=[_row_spec(tq, 0, GROUP_W), _row_spec(tq, 0)],
        out_shape=[jax.ShapeDtypeStruct((sq, GROUP_W), F32), jax.ShapeDtypeStruct((sq, PAIR), F32)],
        compiler_params=_att_params(True),
    )(*args)


def _softmax_bwd(q_parts, k_parts, v_parts, o, lse, do, do_blk, sq, sk, scale, causal, bias, name):
    tq = min(SOFTMAX_TILE if causal else DENSE_QUERY_TILE, sq)
    tk = tq if causal else min(SOFTMAX_TILE, sk)
    nkv = sk // tk
    pp = _Parts(q_parts, k_parts, v_parts, tq, sk)
    quad = pp.nparts == 2
    wq = GROUP_W + (PAIR if quad else 0)

    def kern(*refs):
        q_refs, k_refs, v_refs, rest = pp.split(refs)
        if bias is not None:
            o_ref, lse_ref, do_ref, fc_ref, ft_ref, kmax_ref, dq_ref, dk_ref, dv_ref, dfq_ref, dfk_ref = rest
            fcb = fc_ref[...]
        else:
            o_ref, lse_ref, do_ref, dq_ref, dk_ref, dv_ref = rest
        i = pl.program_id(0)
        fold = _is_pow2(scale)

        @pl.when(i == 0)
        def _():
            dk_ref[...] = jnp.zeros_like(dk_ref)
            dv_ref[...] = jnp.zeros_like(dv_ref)
            if bias is not None:
                dfk_ref[...] = jnp.zeros_like(dfk_ref)

        lse_b = lse_ref[...]
        qm, dom, delta, lse_h = [], [], [], []
        for p in range(2):
            qm += _masked_heads([r[...] for r in q_refs[p]], pp.kinds[p], p, MXU_DTYPE, scale if fold else None)
            do_p = do_ref[:, p * PAIR:(p + 1) * PAIR]
            dom += _masked_heads([do_p], ["pair"], p, MXU_DTYPE)
            prod = do_p * o_ref[:, p * PAIR:(p + 1) * PAIR]
            for e in range(2):
                h = 2 * p + e
                delta.append(jnp.sum(jnp.where(_lane_mask("pair", h, tq), prod, 0.0), axis=1, keepdims=True))
                lse_h.append(_col(lse_b, h) - (_col(fcb, h) if bias is not None else 0.0))

        def block(j, carry, masked, pairs):
            off = _blk_off(j, tk)
            out = []
            for p in pairs:
                kb = pp.k_block(k_refs[p], off, tk)
                vb = v_refs[p][pl.ds(off, tk), :]
                dk_acc = jnp.zeros((tk, pp.width), F32)
                dv_acc = jnp.zeros((tk, PAIR), F32)
                for e in range(2):
                    h = 2 * p + e
                    dq, dfq = carry[len(out)]
                    s = _dot_nt(qm[h], kb)
                    if not fold:
                        s = s * scale
                    if bias is not None:
                        s = s - ft_ref[h:h + 1, pl.ds(off, tk)]
                    if masked:
                        s = jnp.where(_causal_mask(tq, False), s, NEG)
                    pr = jnp.exp(s - lse_h[h])
                    ds = pr * (_dot_nt(dom[h], vb) - delta[h])
                    dsb = (ds if fold else ds * scale).astype(MXU_DTYPE)
                    dv_acc = dv_acc + _dot_tn(pr.astype(MXU_DTYPE), dom[h])
                    dk_acc = dk_acc + _dot_tn(dsb, qm[h])
                    dq = dq + _dot(dsb, kb)
                    if bias is not None:
                        dfq = dfq + jnp.sum(ds, axis=1, keepdims=True)
                        dfk_ref[h:h + 1, pl.ds(off, tk)] -= jnp.sum(ds, axis=0, keepdims=True)
                    out.append((dq, dfq))
                dv_ref[pl.ds(off, tk), p * PAIR:(p + 1) * PAIR] += dv_acc
                dk_ref[pl.ds(off, tk), p * PAIR:(p + 1) * PAIR] += dk_acc[:, :PAIR]
                if quad:
                    dk_ref[pl.ds(off, tk), GROUP_W:] += dk_acc[:, PAIR:]
            return tuple(out)

        done = {}
        for first in range(0, 2, BWD_PAIRS_PER_LOOP):
            pairs = list(range(first, first + BWD_PAIRS_PER_LOOP))
            heads = [2 * p + e for p in pairs for e in range(2)]
            carry = tuple((jnp.zeros((tq, pp.width), F32), jnp.zeros((tq, 1), F32)) for _ in heads)
            if causal and bias is not None:
                reach = {h: _logit_reach(qm[h], kmax_ref[...], h) - lse_h[h] for h in heads}

                def alive(j, heads=heads, reach=reach):
                    return functools.reduce(jnp.maximum, [
                        jnp.max(reach[h] + _forget_top(ft_ref, h, _blk_off(j, tk))) for h in heads])

                def step(state, pairs=pairs, alive=alive):
                    n, _, c = state
                    return n + 1, alive(i - 1 - n), block(i - 1 - n, c, False, pairs)

                carry = block(i, carry, True, pairs)
                _, _, carry = lax.while_loop(lambda st: jnp.logical_and(st[0] < i, st[1] > DEAD_LOGIT), step,
                                             (jnp.int32(0), alive(i), carry))
            elif causal:
                carry = lax.fori_loop(0, i, lambda j, c, pairs=pairs: block(j, c, False, pairs), carry)
                carry = block(i, carry, True, pairs)
            else:
                for j in range(nkv):
                    carry = block(j, carry, False, pairs)
            done.update(zip(heads, carry))
        carry = [done[h] for h in range(N_HEADS)]
        dqs = [c[0] * scale if fold else c[0] for c in carry]
        for p in range(2):
            dq_ref[:, p * PAIR:(p + 1) * PAIR] = _take_heads([dqs[2 * p + e][:, :PAIR] for e in range(2)], p)
        if quad:
            dq_ref[:, GROUP_W:] = sum(jnp.where(_lane_mask("quad", h, tq), dqs[h][:, PAIR:], 0.0)
                                      for h in range(N_HEADS))
        if bias is not None:
            dfq_ref[...] = _scatter_cols({h: carry[h][1] for h in range(N_HEADS)}, tq)

    acc_spec = lambda rows, width: pl.BlockSpec((rows, width), lambda i: (0, 0), pipeline_mode=pl.Buffered(1))
    in_specs = pp.q_specs + pp.k_specs + pp.v_specs + [_row_spec(tq, 0, GROUP_W), _row_spec(tq, 0),
                                                       _row_spec(tq, do_blk, GROUP_W)]
    args = pp.q_args + pp.k_args + pp.v_args + [o, lse, do]
    out_specs = [_row_spec(tq, 0, wq), acc_spec(sk, wq), acc_spec(sk, GROUP_W)]
    out_shape = [jax.ShapeDtypeStruct((sq, wq), F32), jax.ShapeDtypeStruct((sk, wq), F32),
                 jax.ShapeDtypeStruct((sk, GROUP_W), F32)]
    if bias is not None:
        in_specs += [_row_spec(tq, 0), pl.BlockSpec((8, sk), lambda i: (0, 0), pipeline_mode=pl.Buffered(1)),
                     pl.BlockSpec((1, PAIR), lambda i: (0, 0))]
        args += list(bias)
        out_specs += [_row_spec(tq, 0), acc_spec(8, sk)]
        out_shape += [jax.ShapeDtypeStruct((sq, PAIR), F32), jax.ShapeDtypeStruct((8, sk), F32)]
    return pl.pallas_call(
        kern, name=name, grid=(sq // tq,), in_specs=in_specs, out_specs=out_specs, out_shape=out_shape,
        compiler_params=_att_params(False),
    )(*args)


def _sb_logs(qh, kb, valid):
    z = _dot_nt(qh, kb)
    sp = _softplus(z)
    lk = -sp
    if valid is not None:
        lk = jnp.where(valid, lk, 0.0)
    return lk, z - sp


def _sb_valid(d, tq, tk):
    r = lax.broadcasted_iota(jnp.int32, (tq, tk), 0)
    c = lax.broadcasted_iota(jnp.int32, (tq, tk), 1)
    return c + d * tk < r


def _tri_sums(xs, tri):
    t = xs[0].shape[0]
    pieces = [_split2(x) for x in xs]
    hi = _dot(jnp.concatenate([pc[0] for pc in pieces], axis=0), tri)
    lo = _dot(jnp.concatenate([pc[1] for pc in pieces], axis=0), tri)
    return [hi[n * t:(n + 1) * t] + lo[n * t:(n + 1) * t] for n in range(len(xs))]


def _sb_fwd(src, q_blk, k_blk, v_blk, s, scale, name):
    assert _is_pow2(scale)
    tq, t = min(SB_QUERY_TILE, s), min(ATT_TILE, s)
    slots = -(-(s // t) // SB_SLOT) * SB_SLOT
    width = N_HEADS * slots
    band = tq // t
    pair = lambda blk: [[(src, blk + p, "pair")] for p in range(2)]
    pp = _Parts(pair(q_blk), pair(k_blk), [(src, v_blk + p) for p in range(2)], tq, s)

    def kern(*refs):
        q_refs, k_refs, v_refs, (tri_ref, o_ref, rm_ref, cnt_ref) = pp.split(refs)
        i = pl.program_id(0)
        tri = tri_ref[...]
        lane = lax.broadcasted_iota(jnp.int32, (tq, width), 1)
        qm = []
        for p in range(2):
            qm += _masked_heads([q_refs[p][0][...]], ["pair"], p, MXU_DTYPE, scale)

        def block(j, carry, valid):
            accs, rights, rm = carry
            off = _blk_off(j, t)
            kbs = [k_refs[p][0][pl.ds(off, t), :] for p in range(2)]
            vbs = [v_refs[p][pl.ds(off, t), :] for p in range(2)]
            logs = [_sb_logs(qm[h], kbs[h // 2], valid) for h in range(N_HEADS)]
            tails = _tri_sums([lg[0] for lg in logs], tri)
            new_acc, new_right = [], []
            for h in range(N_HEADS):
                lk, ls = logs[h]
                w = jnp.exp(ls + tails[h] + rights[h])
                if valid is not None:
                    w = jnp.where(valid, w, 0.0)
                new_acc.append(accs[h] + _dot(w.astype(MXU_DTYPE), vbs[h // 2]))
                rm = rm + jnp.where(lane == slots * h + j, rights[h], 0.0)
                new_right.append(rights[h] + jnp.sum(lk, axis=1, keepdims=True))
            return tuple(new_acc), tuple(new_right), rm

        carry = (tuple(jnp.zeros((tq, PAIR), F32) for _ in range(N_HEADS)),
                 tuple(jnp.zeros((tq, 1), F32) for _ in range(N_HEADS)), jnp.zeros((tq, width), F32))
        for d in reversed(range(band)):
            carry = block(band * i + d, carry, _sb_valid(d, tq, t))

        def alive(c):
            return functools.reduce(jnp.maximum, [jnp.max(r) for r in c[1]])

        def step(state):
            n, _, c = state
            c = block(band * i - 1 - n, c, None)
            return n + 1, alive(c), c

        n_done, _, carry = lax.while_loop(lambda st: jnp.logical_and(st[0] < band * i, st[1] > EXP_UNDERFLOW),
                                          step, (jnp.int32(0), alive(carry), carry))
        cnt_ref[i] = n_done
        for p in range(2):
            o_ref[:, p * PAIR:(p + 1) * PAIR] = _take_heads([carry[0][2 * p + e] for e in range(2)], p)
        rm_ref[...] = carry[2]

    return pl.pallas_call(
        kern, name=name, grid=(s // tq,),
        in_specs=pp.q_specs + pp.k_specs + pp.v_specs + [pl.BlockSpec((t, t), lambda i: (0, 0))],
        out_specs=[_row_spec(tq, 0, GROUP_W), _row_spec(tq, 0, width), pl.BlockSpec(memory_space=pltpu.SMEM)],
        out_shape=[jax.ShapeDtypeStruct((s, GROUP_W), F32), jax.ShapeDtypeStruct((s, width), F32),
                   jax.ShapeDtypeStruct((s // tq,), jnp.int32)],
        compiler_params=_att_params(False),
    )(*(pp.q_args + pp.k_args + pp.v_args + [_tri(t, "row_gt_col")]))


def _sb_bwd(src, q_blk, k_blk, v_blk, do, do_blk, rm, visited, s, scale, name):
    assert _is_pow2(scale)
    tq, t = min(SB_QUERY_TILE, s), min(ATT_TILE, s)
    band = tq // t
    pair = lambda blk: [[(src, blk + p, "pair")] for p in range(2)]
    pp = _Parts(pair(q_blk), pair(k_blk), [(src, v_blk + p) for p in range(2)], tq, s)

    def kern(*refs):
        q_refs, k_refs, v_refs, (do_ref, rm_ref, tri_ref, pre_ref, cnt_ref, dq_ref, dk_ref, dv_ref) = pp.split(refs)
        i = pl.program_id(0)

        @pl.when(i == 0)
        def _():
            dk_ref[...] = jnp.zeros_like(dk_ref)
            dv_ref[...] = jnp.zeros_like(dv_ref)

        rmb = rm_ref[...]
        tri = tri_ref[...]
        pre = pre_ref[...]
        qm, dom = [], []
        for p in range(2):
            qm += _masked_heads([q_refs[p][0][...]], ["pair"], p, MXU_DTYPE, scale)
            dom += _masked_heads([do_ref[:, p * PAIR:(p + 1) * PAIR]], ["pair"], p, MXU_DTYPE)

        def block(j, carry, valid):
            dqs, lefts = carry
            off = _blk_off(j, t)
            kbs = [k_refs[p][0][pl.ds(off, t), :] for p in range(2)]
            vbs = [v_refs[p][pl.ds(off, t), :] for p in range(2)]
            logs = [_sb_logs(qm[h], kbs[h // 2], valid) for h in range(N_HEADS)]
            tails = _tri_sums([lg[0] for lg in logs], tri)
            ws, gs = [], []
            for h in range(N_HEADS):
                lk, ls = logs[h]
                w = jnp.exp(ls + tails[h] + _col(rmb, (rm.shape[1] // N_HEADS) * h + j))
                if valid is not None:
                    w = jnp.where(valid, w, 0.0)
                ws.append(w)
                gs.append(_dot_nt(dom[h], vbs[h // 2]) * w)
            prefix = _tri_sums(gs, pre)
            new_dq, new_left = [], []
            dk_acc = [jnp.zeros((t, PAIR), F32) for _ in range(2)]
            dv_acc = [jnp.zeros((t, PAIR), F32) for _ in range(2)]
            for h in range(N_HEADS):
                lk, ls = logs[h]
                sig = jnp.exp(ls)
                dz = gs[h] * (1.0 - sig) - sig * (prefix[h] + lefts[h])
                if valid is not None:
                    dz = jnp.where(valid, dz, 0.0)
                dzb = dz.astype(MXU_DTYPE)
                dv_acc[h // 2] = dv_acc[h // 2] + _dot_tn(ws[h].astype(MXU_DTYPE), dom[h])
                dk_acc[h // 2] = dk_acc[h // 2] + _dot_tn(dzb, qm[h])
                new_dq.append(dqs[h] + _dot(dzb, kbs[h // 2]))
                new_left.append(lefts[h] + jnp.sum(gs[h], axis=1, keepdims=True))
            for p in range(2):
                dv_ref[pl.ds(off, t), p * PAIR:(p + 1) * PAIR] += dv_acc[p]
                dk_ref[pl.ds(off, t), p * PAIR:(p + 1) * PAIR] += dk_acc[p]
            return tuple(new_dq), tuple(new_left)

        carry = (tuple(jnp.zeros((tq, PAIR), F32) for _ in range(N_HEADS)),
                 tuple(jnp.zeros((tq, 1), F32) for _ in range(N_HEADS)))
        carry = lax.fori_loop(band * i - cnt_ref[i], band * i, lambda j, c: block(j, c, None), carry)
        for d in range(band):
            carry = block(band * i + d, carry, _sb_valid(d, tq, t))
        for p in range(2):
            dq_ref[:, p * PAIR:(p + 1) * PAIR] = _take_heads([carry[0][2 * p + e] * scale for e in range(2)], p)

    mspec = pl.BlockSpec((t, t), lambda i: (0, 0))
    acc_spec = pl.BlockSpec((s, GROUP_W), lambda i: (0, 0), pipeline_mode=pl.Buffered(1))
    return pl.pallas_call(
        kern, name=name, grid=(s // tq,),
        in_specs=pp.q_specs + pp.k_specs + pp.v_specs + [_row_spec(tq, do_blk, GROUP_W), _row_spec(tq, 0, rm.shape[1]), mspec, mspec,
                                                         pl.BlockSpec(memory_space=pltpu.SMEM)],
        out_specs=[_row_spec(tq, 0, GROUP_W), acc_spec, acc_spec],
        out_shape=[jax.ShapeDtypeStruct((s, GROUP_W), F32)] * 3,
        compiler_params=_att_params(False),
    )(*(pp.q_args + pp.k_args + pp.v_args + [do, rm, _tri(t, "row_gt_col"), _tri(t, "row_lt_col"), visited]))


def _split_w_in(w):
    col = lambda n: w[:, _OFF[n]:_OFF[n + 1]]
    wa = jnp.concatenate([col(0), col(1), col(2), col(4), col(5), col(6), col(10)], axis=1)
    misc = jnp.concatenate([col(3), col(9), jnp.zeros((w.shape[0], 128 - 4 - MLA_ROPE), w.dtype)], axis=1)
    wb = jnp.concatenate([col(11), col(7), col(8), misc], axis=1)
    return wa, wb


def _merge_dw_in(dwp):
    a = lambda n: dwp[:, n * GROUP_W:(n + 1) * GROUP_W]
    b0 = PA_COLS
    gate = dwp[:, b0:b0 + 1024]
    cq = dwp[:, b0 + 1024:b0 + 1280]
    ckv = dwp[:, b0 + 1280:b0 + 1408]
    flog = dwp[:, b0 + 1408:b0 + 1412]
    krot = dwp[:, b0 + 1408 + MISC_KROT:b0 + 1408 + MISC_KROT + MLA_ROPE]
    return jnp.concatenate([a(0), a(1), a(2), flog, a(3), a(4), a(5), cq, ckv, krot, a(6), gate], axis=1)


def _heads_first(w, per_head, first):
    r = w.shape[0]
    w3 = w.reshape(r, N_HEADS, per_head)
    return jnp.concatenate([w3[:, :, :first].reshape(r, -1), w3[:, :, first:].reshape(r, -1)], axis=1)


def _heads_interleaved(w, per_head, first):
    r = w.shape[0]
    a = w[:, :N_HEADS * first].reshape(r, N_HEADS, first)
    b = w[:, N_HEADS * first:].reshape(r, N_HEADS, per_head - first)
    return jnp.concatenate([a, b], axis=2).reshape(r, N_HEADS * per_head)


def _pad_rows8(a):
    return a[:, :8].T


def _local_step(x2, mem2, tgt, p):
    s = x2.shape[0]
    nm = mem2.shape[0]
    head_scale = HEAD_DIM ** -0.5
    mla_scale = (HEAD_DIM + MLA_ROPE) ** -0.5
    tabs = _rope_tables(s)
    mats = _rope_matrices()
    pairs = lambda arr, blk: [[(arr, blk + q, "pair")] for q in range(2)]
    vals = lambda arr, blk: [(arr, blk + q) for q in range(2)]

    h, hb = _ln_fwd(x2, None, p["ln_in_g"], p["ln_in_b"], "ln_in_fwd")
    _, memn_b = _ln_fwd(mem2, None, p["mem_ln_g"], p["mem_ln_b"], "ln_mem_fwd")

    saved = []
    for l in range(DEPTH):
        wa, wb = _split_w_in(p["w_in"][l])
        wp = jnp.concatenate([wa, wb], axis=1)
        wq_up = _heads_first(p["w_mla_q_up"][l], HEAD_DIM + MLA_ROPE, HEAD_DIM)
        wkv_up = _heads_first(p["w_mla_kv_up"][l], 2 * HEAD_DIM, HEAD_DIM)
        bias_row = jnp.pad(p["b_forget"][l], (0, 128 - N_HEADS)).reshape(1, 128)
        pa, pb, fox_kmax = _proj_in(hb, wa, wb, GROUP_W)

        fc = _forget_fwd(pb, bias_row)
        fbias = (fc, _pad_rows8(fc), fox_kmax)
        o_fox, lse_fox = _softmax_fwd(pairs(pa, 0), pairs(pa, 2), vals(pa, 4), s, s, head_scale, True, fbias,
                                      "fox_fwd")
        o_sb, *rm_sb = _sb_fwd(pa, 6, 8, 10, s, head_scale, "sb_fwd")

        qfull, cqn = _mla_q_fwd(pb, p["mla_q_norm_g"][l], wq_up, tabs, mats)
        kv, ckvn, kr4 = _mla_kv_fwd(pb, p["mla_kv_norm_g"][l], wkv_up, tabs, mats)
        mla_q = [[(qfull, q, "pair"), (qfull, 2, "quad")] for q in range(2)]
        mla_k = [[(kv, q, "pair"), (kr4, 0, "quad")] for q in range(2)]
        o_mla, lse_mla = _softmax_fwd(mla_q, mla_k, vals(kv, 2), s, s, mla_scale, True, None, "mla_fwd")

        mkv = _matmul(memn_b, p["w_mem_kv"][l], MXU_DTYPE, "mem_kv")
        o_mem, lse_mem = _softmax_fwd(pairs(pa, 12), pairs(mkv, 0), vals(mkv, 2), s, nm, head_scale, False, None,
                                      "mem_fwd")

        groups = (o_fox, o_sb, o_mla, o_mem)
        if l < DEPTH - 1:
            gated, u, h_next, hb_next = _gate_out_proj_ln(groups, pb, p["w_out"][l], h, p["ln_g"][l], p["ln_b"][l])
        else:
            gated, u, dh, sq_cols = _gate_out_proj_ln(groups, pb, p["w_out"][l], h, p["ln_g"][l], p["ln_b"][l], tgt)
        saved.append(dict(u=u, hb=hb, wp=wp, wq_up=wq_up, wkv_up=wkv_up, bias_row=bias_row, pa=pa, pb=pb,
                          fbias=fbias, lse_fox=lse_fox, rm_sb=rm_sb, cqn=cqn, ckvn=ckvn, mla_q=mla_q, mla_k=mla_k,
                          kv=kv, lse_mla=lse_mla, mkv=mkv, lse_mem=lse_mem, groups=groups, gated=gated))
        if l < DEPTH - 1:
            h, hb = h_next, hb_next

    loss_sum = jnp.sum(sq_cols)

    grads = {k: [None] * DEPTH for k in ("w_in", "b_forget", "mla_q_norm_g", "w_mla_q_up", "mla_kv_norm_g",
                                         "w_mla_kv_up", "w_mem_kv", "w_out", "ln_g", "ln_b")}
    dmemn = []
    dy1, dy2, c1 = dh, None, 1.0
    for l in reversed(range(DEPTH)):
        r = saved[l]
        pa, pb = r["pa"], r["pb"]
        o_fox, o_sb, o_mla, o_mem = r["groups"]
        du, du_b, dmixed, dgate_b, dg, db = _ln_bwd(r["u"], p["ln_g"][l], dy1, dy2, c1, "ln_gate_bwd",
                                                    (p["w_out"][l], r["groups"], pb))
        grads["ln_g"][l], grads["ln_b"][l] = dg[0], db[0]
        grads["w_out"][l] = _matmul(r["gated"], du_b, F32, "out_proj_dw", "tn")

        dfq, dfk, dfv, dfc_q, dfc_k = _softmax_bwd(pairs(pa, 0), pairs(pa, 2), vals(pa, 4), o_fox, r["lse_fox"],
                                                   dmixed, 0, s, s, head_scale, True, r["fbias"], "fox_bwd")
        dmisc_f, dbf = _forget_bwd(pb, r["bias_row"], dfc_q + jnp.pad(dfc_k.T, ((0, 0), (0, 128 - 8))))
        grads["b_forget"][l] = dbf[0, :N_HEADS]

        dsq, dsk, dsv = _sb_bwd(pa, 6, 8, 10, dmixed, 1, *r["rm_sb"], s, head_scale, "sb_bwd")

        dqm, dkm, dvm = _softmax_bwd(r["mla_q"], r["mla_k"], vals(r["kv"], 2), o_mla, r["lse_mla"], dmixed, 2,
                                     s, s, mla_scale, True, None, "mla_bwd")
        dq_mla_b, dcq_b, dgq = _mla_q_bwd(dqm, pb, p["mla_q_norm_g"][l], r["wq_up"], tabs, mats)
        grads["w_mla_q_up"][l] = _heads_interleaved(_matmul(r["cqn"], dq_mla_b, F32, "q_up_dw", "tn"),
                                                    HEAD_DIM + MLA_ROPE, HEAD_DIM)
        grads["mla_q_norm_g"][l] = dgq[0]
        dkv_b, dckv_b, dmisc_k, dgkv = _mla_kv_bwd(dkm, dvm, pb, p["mla_kv_norm_g"][l], r["wkv_up"], tabs, mats)
        grads["w_mla_kv_up"][l] = _heads_interleaved(_matmul(r["ckvn"], dkv_b, F32, "kv_up_dw", "tn"),
                                                     2 * HEAD_DIM, HEAD_DIM)
        grads["mla_kv_norm_g"][l] = dgkv[0]

        dmq, dmk, dmv = _softmax_bwd(pairs(pa, 12), pairs(r["mkv"], 0), vals(r["mkv"], 2), o_mem, r["lse_mem"],
                                     dmixed, 3, s, nm, head_scale, False, None, "mem_bwd")
        dmkv_b = jnp.concatenate([dmk, dmv], axis=1).astype(MXU_DTYPE)
        grads["w_mem_kv"][l] = _matmul(memn_b, dmkv_b, F32, "mem_kv_dw", "tn")
        dmemn.append(_matmul(dmkv_b, p["w_mem_kv"][l], F32, "mem_kv_dx", "nt"))

        dp = [dfq, dfk, dfv, dsq, dsk, dsv, dmq, dgate_b, dcq_b, dckv_b, (dmisc_f, dmisc_k)]
        dhproj = _proj_dx(dp, r["wp"])
        grads["w_in"][l] = _merge_dw_in(_proj_dw(r["hb"], dp))
        dy1, dy2, c1 = du, dhproj, ALPHA

    dx, dg_in, db_in = _ln_bwd(x2, p["ln_in_g"], dy1, dy2, c1, "ln_in_bwd")
    _, dg_mem, db_mem = _ln_bwd(mem2, p["mem_ln_g"], dmemn[0], dmemn[1], 1.0, "ln_mem_bwd")
    out = {k: jnp.stack(v) for k, v in grads.items()}
    out.update(ln_in_g=dg_in[0], ln_in_b=db_in[0], mem_ln_g=dg_mem[0], mem_ln_b=db_mem[0])
    return loss_sum, dx, out


WIDE = "w_in"
FLAT_NAMES = ("w_out", "w_mem_kv", "w_mla_q_up", "w_mla_kv_up")
FLAT_ROWS = 896
BIG_NAMES = (WIDE,) + FLAT_NAMES
BIG_AXIS = dict(w_in=2, w_out=1, w_mem_kv=1, w_mla_q_up=2, w_mla_kv_up=2)
SMALL_NAMES = ("ln_in_g", "ln_in_b", "mem_ln_g", "mem_ln_b", "ln_g", "ln_b", "b_forget", "mla_q_norm_g",
               "mla_kv_norm_g")
ALL_NAMES = ("ln_in_g", "ln_in_b", "mem_ln_g", "mem_ln_b", "w_in", "b_forget", "mla_q_norm_g", "w_mla_q_up",
             "mla_kv_norm_g", "w_mla_kv_up", "w_mem_kv", "w_out", "ln_g", "ln_b")
N_CHIPS = 4
N_DEV = 8


def _rows_of(shape):
    rows = -(-int(np.prod(shape)) // LANES)
    return -(-rows // PACK_ALIGN) * PACK_ALIGN


def _pack(arrs, rows):
    parts = []
    for a in arrs:
        f = a.reshape(-1)
        n = _rows_of(a.shape) * LANES
        parts.append(jnp.pad(f, (0, n - f.shape[0])).reshape(-1, LANES))
    used = sum(q.shape[0] for q in parts)
    if rows > used:
        parts.append(jnp.zeros((rows - used, LANES), parts[0].dtype))
    return jnp.concatenate(parts, axis=0)


def _unpack(buf, shapes):
    out, r = [], 0
    for shp in shapes:
        n = _rows_of(shp)
        out.append(buf[r:r + n].reshape(-1)[:int(np.prod(shp))].reshape(shp))
        r += n
    return out


def _sharded_pair(get):
    wide = get(WIDE)
    return [wide.reshape(-1, wide.shape[-1]), _pack([get(n) for n in FLAT_NAMES], FLAT_ROWS)]


HBM_SPEC = pl.BlockSpec(memory_space=pltpu.HBM)


def _gather_weights(shards):
    n = len(shards)

    def body(*refs):
        w_refs, out_refs, (send_sems, recv_sems, local_sems) = refs[:n], refs[n:2 * n], refs[2 * n:]
        x, y, c = (lax.axis_index(a) for a in MESH_AXES)
        me, sibling = 2 * x + y, (x, y, 1 - c)
        chips = [(1 - x, y), (x, 1 - y), (1 - x, 1 - y)]
        local, first, passed = [], [], []
        for a in range(n):
            w_ref, out_ref, half = w_refs[a], out_refs[a], shards[a].shape[0] // 2

            def part(chip, core, out_ref=out_ref, half=half):
                return out_ref.at[chip, pl.ds(core * half, half)]

            def copy(k, src, dst, to, a=a):
                return pltpu.make_async_remote_copy(
                    src_ref=src, dst_ref=dst, send_sem=send_sems.at[6 * a + k], recv_sem=recv_sems.at[6 * a + k],
                    device_id=to, device_id_type=pl.DeviceIdType.MESH)

            local.append(pltpu.make_async_copy(w_ref, out_ref.at[me], local_sems.at[a]))
            local[-1].start()
            mine = [copy(k, w_ref.at[pl.ds(c * half, half)], part(me, c), (px, py, c))
                    for k, (px, py) in enumerate(chips)]
            for cp in mine:
                cp.start()
            first.append((mine, part, copy))
        for mine, part, copy in first:
            for k, (px, py) in enumerate(chips):
                copy(k, part(me, c), part(2 * px + py, c), (px, py, c)).wait_recv()
                passed.append(copy(3 + k, part(2 * px + py, c), part(2 * px + py, c), sibling))
                passed[-1].start()
        for mine, part, copy in first:
            for k, (px, py) in enumerate(chips):
                copy(3 + k, part(me, c), part(2 * px + py, 1 - c), sibling).wait_recv()
        for cp in [cp for mine, _, _ in first for cp in mine] + passed:
            cp.wait_send()
        for cp in local:
            cp.wait()

    return pl.pallas_call(
        body, name="gather_weights",
        out_shape=[jax.ShapeDtypeStruct((N_CHIPS,) + s.shape, s.dtype) for s in shards],
        in_specs=[HBM_SPEC] * n, out_specs=[HBM_SPEC] * n,
        scratch_shapes=[pltpu.SemaphoreType.DMA((6 * n,)), pltpu.SemaphoreType.DMA((6 * n,)),
                        pltpu.SemaphoreType.DMA((n,))],
    )(*shards)


def _exchange_grads(bigs, small):
    nb = len(bigs)
    halves = [b.shape[1] // 2 for b in bigs]
    chunks = [_pick(h, (128, 64, 32, 16)) for h in halves]

    def body(*refs):
        big_refs, small_ref = refs[:nb], refs[nb]
        sum_refs, out_refs = refs[nb + 1:2 * nb + 1], refs[2 * nb + 1:3 * nb + 2]
        scratch = refs[3 * nb + 2:]
        land_bufs, sum_bufs = scratch[:nb], scratch[nb:2 * nb]
        send_sems, recv_sems, local_sems, load_sem, swap_send, swap_recv, keep_sems = scratch[2 * nb:]
        x, y, c = (lax.axis_index(a) for a in MESH_AXES)
        me, my_chip = 4 * x + 2 * y + c, 2 * x + y
        flips = [(fx, fy, fc) for fx in (0, 1) for fy in (0, 1) for fc in (0, 1) if fx or fy or fc]
        peers = [(1 - x if fx else x, 1 - y if fy else y, 1 - c if fc else c) for fx, fy, fc in flips]

        sources = [lambda chip, core, r=big_refs[a], half=bigs[a].shape[1] // 2: r.at[chip, pl.ds(core * half, half)]
                   for a in range(nb)] + [lambda chip, core: small_ref]

        def copy(a, k, src, slot, to):
            return pltpu.make_async_remote_copy(
                src_ref=src, dst_ref=out_refs[a].at[slot], send_sem=send_sems.at[7 * a + k],
                recv_sem=recv_sems.at[7 * a + k], device_id=to, device_id_type=pl.DeviceIdType.MESH)

        own = [pltpu.make_async_copy(src(my_chip, c), out_refs[a].at[me], local_sems.at[a])
               for a, src in enumerate(sources)]
        for cp in own:
            cp.start()
        sends = [copy(a, k, src(2 * px + py, pc), me, (px, py, pc))
                 for a, src in enumerate(sources) for k, (px, py, pc) in enumerate(peers)]
        for cp in sends:
            cp.start()
        for a, src in enumerate(sources):
            for k, (px, py, pc) in enumerate(peers):
                copy(a, k, src(my_chip, c), 4 * px + 2 * py + pc, (px, py, pc)).wait_recv()
        for cp in sends:
            cp.wait_send()
        for cp in own:
            cp.wait()

        tails = []
        for a in range(nb):
            for r0 in range(0, halves[a], chunks[a]):
                load = pltpu.make_async_copy(out_refs[a].at[:, pl.ds(r0, chunks[a])], land_bufs[a], load_sem)
                load.start()
                load.wait()
                total = land_bufs[a][0].astype(F32)
                for d in range(1, N_DEV):
                    total = total + land_bufs[a][d].astype(F32)
                sum_bufs[a][pl.ds(r0, chunks[a]), :] = total
            keep = pltpu.make_async_copy(sum_bufs[a], sum_refs[a].at[c], keep_sems.at[a])
            give = pltpu.make_async_remote_copy(
                src_ref=sum_bufs[a], dst_ref=sum_refs[a].at[c], send_sem=swap_send.at[a], recv_sem=swap_recv.at[a],
                device_id=(x, y, 1 - c), device_id_type=pl.DeviceIdType.MESH)
            keep.start()
            give.start()
            tails.append((keep, give))
        for a, (keep, give) in enumerate(tails):
            pltpu.make_async_remote_copy(
                src_ref=sum_bufs[a], dst_ref=sum_refs[a].at[1 - c], send_sem=swap_send.at[a], recv_sem=swap_recv.at[a],
                device_id=(x, y, 1 - c), device_id_type=pl.DeviceIdType.MESH).wait_recv()
            give.wait_send()
            keep.wait()

    return pl.pallas_call(
        body, name="exchange_grads",
        out_shape=[jax.ShapeDtypeStruct((2, h, b.shape[2]), F32) for h, b in zip(halves, bigs)]
        + [jax.ShapeDtypeStruct((N_DEV, h, b.shape[2]), b.dtype) for h, b in zip(halves, bigs)]
        + [jax.ShapeDtypeStruct((N_DEV,) + small.shape, small.dtype)],
        in_specs=[HBM_SPEC] * (nb + 1), out_specs=[HBM_SPEC] * (2 * nb + 1),
        scratch_shapes=[pltpu.VMEM((N_DEV, ch, b.shape[2]), b.dtype) for ch, b in zip(chunks, bigs)]
        + [pltpu.VMEM((h, b.shape[2]), F32) for h, b in zip(halves, bigs)]
        + [pltpu.SemaphoreType.DMA((7 * (nb + 1),)), pltpu.SemaphoreType.DMA((7 * (nb + 1),)),
           pltpu.SemaphoreType.DMA((nb + 1,)), pltpu.SemaphoreType.DMA, pltpu.SemaphoreType.DMA((nb,)),
           pltpu.SemaphoreType.DMA((nb,)), pltpu.SemaphoreType.DMA((nb,))],
        compiler_params=pltpu.CompilerParams(vmem_limit_bytes=VMEM_LIMIT),
    )(*bigs, small)


def _adamw(parts, w, m, v, name):
    rows, width = w.shape
    n_parts = parts.shape[0]
    tile = _pick(rows, (128, 16, 8))
    bc1 = 1.0 - ADAM_B1 ** ADAM_STEP
    bc2 = 1.0 - ADAM_B2 ** ADAM_STEP

    def kern(p_ref, w_ref, m_ref, v_ref, g_ref, d_ref, nm_ref, nv_ref):
        g = p_ref[0].astype(F32)
        for d in range(1, n_parts):
            g = g + p_ref[d].astype(F32)
        nm = ADAM_B1 * m_ref[...] + (1.0 - ADAM_B1) * g
        nv = ADAM_B2 * v_ref[...] + (1.0 - ADAM_B2) * (g * g)
        g_ref[...] = g
        nm_ref[...] = nm
        nv_ref[...] = nv
        d_ref[...] = -ADAM_LR * ((nm / bc1) / (jnp.sqrt(nv / bc2) + ADAM_EPS) + ADAM_WD * w_ref[...])

    spec = pl.BlockSpec((tile, width), lambda i: (i, 0))
    return pl.pallas_call(
        kern, name=name, grid=(rows // tile,),
        in_specs=[pl.BlockSpec((n_parts, tile, width), lambda i: (0, i, 0)), spec, spec, spec],
        out_specs=[spec] * 4, out_shape=[jax.ShapeDtypeStruct((rows, width), F32)] * 4,
        compiler_params=pltpu.CompilerParams(dimension_semantics=("parallel",), vmem_limit_bytes=VMEM_LIMIT),
    )(parts, w, m, v)


def kernel(x, mem, ln_in_g, ln_in_b, mem_ln_g, mem_ln_b, w_in, b_forget, mla_q_norm_g, w_mla_q_up, mla_kv_norm_g, w_mla_kv_up, w_mem_kv, w_out, ln_g, ln_b, loss_target, m_ln_in_g, m_ln_in_b, m_mem_ln_g, m_mem_ln_b, m_w_in, m_b_forget, m_mla_q_norm_g, m_w_mla_q_up, m_mla_kv_norm_g, m_w_mla_kv_up, m_w_mem_kv, m_w_out, m_ln_g, m_ln_b, v_ln_in_g, v_ln_in_b, v_mem_ln_g, v_mem_ln_b, v_w_in, v_b_forget, v_mla_q_norm_g, v_w_mla_q_up, v_mla_kv_norm_g, v_w_mla_kv_up, v_w_mem_kv, v_w_out, v_ln_g, v_ln_b):
    w = dict(ln_in_g=ln_in_g, ln_in_b=ln_in_b, mem_ln_g=mem_ln_g, mem_ln_b=mem_ln_b, w_in=w_in, b_forget=b_forget,
             mla_q_norm_g=mla_q_norm_g, w_mla_q_up=w_mla_q_up, mla_kv_norm_g=mla_kv_norm_g,
             w_mla_kv_up=w_mla_kv_up, w_mem_kv=w_mem_kv, w_out=w_out, ln_g=ln_g, ln_b=ln_b)
    mo = dict(ln_in_g=m_ln_in_g, ln_in_b=m_ln_in_b, mem_ln_g=m_mem_ln_g, mem_ln_b=m_mem_ln_b, w_in=m_w_in,
              b_forget=m_b_forget, mla_q_norm_g=m_mla_q_norm_g, w_mla_q_up=m_w_mla_q_up,
              mla_kv_norm_g=m_mla_kv_norm_g, w_mla_kv_up=m_w_mla_kv_up, w_mem_kv=m_w_mem_kv, w_out=m_w_out,
              ln_g=m_ln_g, ln_b=m_ln_b)
    vo = dict(ln_in_g=v_ln_in_g, ln_in_b=v_ln_in_b, mem_ln_g=v_mem_ln_g, mem_ln_b=v_mem_ln_b, w_in=v_w_in,
              b_forget=v_b_forget, mla_q_norm_g=v_mla_q_norm_g, w_mla_q_up=v_w_mla_q_up,
              mla_kv_norm_g=v_mla_kv_norm_g, w_mla_kv_up=v_w_mla_kv_up, w_mem_kv=v_w_mem_kv, w_out=v_w_out,
              ln_g=v_ln_g, ln_b=v_ln_b)
    flat_shapes = [w[n].shape for n in FLAT_NAMES]
    small_shapes = [w[n].shape for n in SMALL_NAMES]

    got_wide, got_flat = _gather_weights(_sharded_pair(lambda n: w[n].astype(MXU_DTYPE)))
    full = dict(w)
    full[WIDE] = jnp.concatenate([got_wide[j] for j in range(N_CHIPS)], axis=1).reshape(
        w[WIDE].shape[:2] + (N_CHIPS * w[WIDE].shape[2],))
    per_chip = [_unpack(got_flat[j], flat_shapes) for j in range(N_CHIPS)]
    for idx, n in enumerate(FLAT_NAMES):
        full[n] = jnp.concatenate([per_chip[j][idx] for j in range(N_CHIPS)], axis=BIG_AXIS[n])

    loss_sum, dx, g = _local_step(x[0], mem[0], loss_target[0], full)
    loss = lax.psum(loss_sum * (0.5 / D_MODEL), MESH_AXES)

    def shard_of(n, j):
        ax, size = BIG_AXIS[n], w[n].shape[BIG_AXIS[n]]
        return lax.slice_in_dim(g[n], j * size, (j + 1) * size, axis=ax).astype(MXU_DTYPE)

    per_dest = [_sharded_pair(lambda n, j=j: shard_of(n, j)) for j in range(N_CHIPS)]
    bigs = [jnp.stack([per_dest[j][a] for j in range(N_CHIPS)]) for a in range(2)]
    exchanged = _exchange_grads(bigs, _pack([g[n] for n in SMALL_NAMES], SMALL_ROWS))
    halves, small_parts = exchanged[:2], exchanged[-1]

    res = []
    for a, (grad, nm) in enumerate(zip(halves, ("adamw_wide", "adamw_flat"))):
        state = [_sharded_pair(lambda n, src=src: src[n])[a] for src in (w, mo, vo)]
        res.append(_adamw(grad.reshape((1,) + state[0].shape), *state, nm))
    res_small = _adamw(small_parts, *[_pack([src[n] for n in SMALL_NAMES], SMALL_ROWS) for src in (w, mo, vo)],
                       "adamw_replicated")
    outs = []
    for kind in range(4):
        vals = {WIDE: res[0][kind].reshape(w[WIDE].shape)}
        vals.update(zip(FLAT_NAMES, _unpack(res[1][kind], flat_shapes)))
        vals.update(zip(SMALL_NAMES, _unpack(res_small[kind], small_shapes)))
        outs += [vals[n] for n in ALL_NAMES]
    return (loss, dx[None], *outs)
```

```python
import functools

import numpy as np
import jax
import jax.numpy as jnp
from jax import lax
from jax.experimental import pallas as pl
from jax.experimental.pallas import tpu as pltpu

F32 = jnp.float32
MXU_DTYPE = jnp.bfloat16

DEPTH = 2
D_MODEL = 1024
GROUP_W = 256
N_HEADS = 4
HEAD_DIM = 64
MLA_Q_RANK = 256
MLA_KV_RANK = 128
MLA_ROPE = 32
MLA_Q_COLS = N_HEADS * (HEAD_DIM + MLA_ROPE)
MLA_KV_COLS = N_HEADS * 2 * HEAD_DIM
ROPE_THETA = 10000.0
LN_EPS = 1e-5
RMS_EPS = 1e-6
ALPHA = (2 * DEPTH) ** 0.25
ADAM_LR, ADAM_B1, ADAM_B2, ADAM_EPS, ADAM_WD, ADAM_STEP = 0.001, 0.9, 0.999, 1e-08, 0.01, 10

_SPLIT = (256, 256, 256, 4, 256, 256, 256, 256, 128, 32, 256, 1024)
_OFF = [int(o) for o in np.cumsum((0,) + _SPLIT)]
IN_COLS = _OFF[-1]
PA_COLS = 7 * GROUP_W
PB_COLS = 1024 + 256 + 128 + 128
PB_CQ_BLK, PB_CKV_BLK, PB_MISC_BLK = 4, 10, 11
MISC_KROT = 4

LANES = 1024
PACK_ALIGN = 16
SMALL_ROWS = 144
ROW_TILE = 256
PROJ_BWD_ROWS = 512
ATT_TILE = 256
SB_QUERY_TILE = 256
SOFTMAX_TILE = 512
DENSE_QUERY_TILE = 1024
FWD_PAIRS_PER_LOOP = 1
BWD_PAIRS_PER_LOOP = 2
PAIR = 128
SB_SLOT = PAIR // N_HEADS
VMEM_LIMIT = 56 * 1024 * 1024
MATMUL_VMEM = 30 * 1024 * 1024
NEG = -1e30
LOG2E = 1.4426950408889634
LN2 = 0.6931471805599453
EXP_UNDERFLOW = -104.0
DEAD_LOGIT = -110.0
REACH_SLACK = 1.0 + 2.0 ** -10
MESH_AXES = ("x", "y", "c")


def _dot(a, b):
    return jnp.dot(a, b, preferred_element_type=F32)


def _dot_nt(a, b):
    return lax.dot_general(a, b, (((1,), (1,)), ((), ())), preferred_element_type=F32)


def _dot_tn(a, b):
    return lax.dot_general(a, b, (((0,), (0,)), ((), ())), preferred_element_type=F32)


def _split2(x):
    hi = x.astype(MXU_DTYPE)
    lo = (x - hi.astype(F32)).astype(MXU_DTYPE)
    return hi, lo


def _split3(x):
    hi = x.astype(MXU_DTYPE)
    r = x - hi.astype(F32)
    mid = r.astype(MXU_DTYPE)
    lo = (r - mid.astype(F32)).astype(MXU_DTYPE)
    return hi, mid, lo


def _dot_exact_r(x, pm):
    hi, mid, lo = _split3(x)
    return _dot(hi, pm) + _dot(mid, pm) + _dot(lo, pm)


def _dot_exact_l(pm, x):
    hi, mid, lo = _split3(x)
    return _dot(pm, hi) + _dot(pm, mid) + _dot(pm, lo)


def _pick(dim, prefs):
    for p in prefs:
        if dim % p == 0:
            return p
    return dim


def _softplus(z):
    return jnp.maximum(z, 0.0) + jnp.log(1.0 + jnp.exp(-jnp.abs(z)))


def _tile_options(dim):
    opts = [d for d in range(128, min(dim, 2048) + 1, 128) if dim % d == 0]
    return opts or [dim]


def _matmul_tiles(m, n, k, out_bytes):
    tk = k if k <= 4096 else _pick(k, (1024, 512, 256, 128))
    best = None
    for tm in _tile_options(m):
        for tn in _tile_options(n):
            vmem = 2 * 2 * (tm * tk + tk * tn) + 4 * tm * tn + 2 * out_bytes * tm * tn
            if vmem <= MATMUL_VMEM and (best is None or tm * tn / (tm + tn) > best[0]):
                best = (tm * tn / (tm + tn), tm, tn)
    return best[1], best[2], tk


def _matmul(a, b, out_dtype, name, mode="nn"):
    m, k = (a.shape[1], a.shape[0]) if mode == "tn" else a.shape
    n = b.shape[0] if mode == "nt" else b.shape[1]
    tm, tn, tk = _matmul_tiles(m, n, k, jnp.dtype(out_dtype).itemsize)
    nk = k // tk
    dot = {"nn": _dot, "tn": _dot_tn, "nt": _dot_nt}[mode]

    def kern(a_ref, b_ref, o_ref, *acc):
        if nk == 1:
            o_ref[...] = dot(a_ref[...], b_ref[...]).astype(o_ref.dtype)
            return
        acc_ref, = acc
        kk = pl.program_id(2)

        @pl.when(kk == 0)
        def _():
            acc_ref[...] = jnp.zeros_like(acc_ref)

        acc_ref[...] += dot(a_ref[...], b_ref[...])

        @pl.when(kk == nk - 1)
        def _():
            o_ref[...] = acc_ref[...].astype(o_ref.dtype)

    a_spec = (pl.BlockSpec((tk, tm), lambda i, j, kk: (kk, i)) if mode == "tn"
              else pl.BlockSpec((tm, tk), lambda i, j, kk: (i, kk)))
    b_spec = (pl.BlockSpec((tn, tk), lambda i, j, kk: (j, kk)) if mode == "nt"
              else pl.BlockSpec((tk, tn), lambda i, j, kk: (kk, j)))
    return pl.pallas_call(
        kern, name=name, grid=(m // tm, n // tn, nk), in_specs=[a_spec, b_spec],
        out_specs=pl.BlockSpec((tm, tn), lambda i, j, kk: (i, j)),
        out_shape=jax.ShapeDtypeStruct((m, n), out_dtype),
        scratch_shapes=[pltpu.VMEM((tm, tn), F32)] if nk > 1 else [],
        compiler_params=pltpu.CompilerParams(
            dimension_semantics=("parallel", "parallel", "arbitrary"), vmem_limit_bytes=VMEM_LIMIT),
    )(a.astype(MXU_DTYPE), b.astype(MXU_DTYPE))


def _proj_in(hb, wa, wb, key_cols):
    rows, d = hb.shape
    tm = min(PROJ_BWD_ROWS, rows)

    def kern(a_ref, wa_ref, wb_ref, pa_ref, pb_ref, kmax_ref):
        @pl.when(pl.program_id(0) == 0)
        def _():
            kmax_ref[...] = jnp.zeros_like(kmax_ref)

        a = a_ref[...]
        pa = _dot(a, wa_ref[...]).astype(pa_ref.dtype)
        pa_ref[...] = pa
        pb_ref[...] = _dot(a, wb_ref[...])
        cols = {}
        for p in range(2):
            k32 = pa[:, key_cols + p * PAIR:key_cols + (p + 1) * PAIR].astype(F32)
            for e in range(2):
                sq = jnp.sum(jnp.where(_lane_mask("pair", e, tm), k32 * k32, 0.0), axis=1, keepdims=True)
                cols[2 * p + e] = jnp.max(sq, axis=0, keepdims=True)
        lane = lax.broadcasted_iota(jnp.int32, (1, PAIR), 1)
        kmax_ref[...] = jnp.maximum(kmax_ref[...], sum(jnp.where(lane == h, c, 0.0) for h, c in cols.items()))

    whole = lambda w: pl.BlockSpec(w.shape, lambda i: (0, 0), pipeline_mode=pl.Buffered(1))
    return pl.pallas_call(
        kern, name="proj_in", grid=(rows // tm,),
        in_specs=[pl.BlockSpec((tm, d), lambda i: (i, 0)), whole(wa), whole(wb)],
        out_specs=[pl.BlockSpec((tm, wa.shape[1]), lambda i: (i, 0)), pl.BlockSpec((tm, wb.shape[1]), lambda i: (i, 0)),
                   pl.BlockSpec((1, PAIR), lambda i: (0, 0))],
        out_shape=[jax.ShapeDtypeStruct((rows, wa.shape[1]), MXU_DTYPE), jax.ShapeDtypeStruct((rows, wb.shape[1]), F32),
                   jax.ShapeDtypeStruct((1, PAIR), F32)],
        compiler_params=pltpu.CompilerParams(dimension_semantics=("arbitrary",), vmem_limit_bytes=VMEM_LIMIT),
    )(hb, wa.astype(MXU_DTYPE), wb.astype(MXU_DTYPE))


def _piece_arrays(pieces):
    return [a for p in pieces for a in (p if isinstance(p, tuple) else (p,))]


def _join_pieces(refs, pieces):
    refs, cols = list(refs), []
    for p in pieces:
        vals = [refs.pop(0)[...] for _ in (p if isinstance(p, tuple) else (p,))]
        cols.append(functools.reduce(jnp.add, vals).astype(MXU_DTYPE))
    return jnp.concatenate(cols, axis=1)


def _proj_dx(pieces, w):
    arrs = _piece_arrays(pieces)
    rows, n = arrs[0].shape[0], w.shape[0]
    tm = min(PROJ_BWD_ROWS, rows)

    def kern(*refs):
        refs[-1][...] = _dot_nt(_join_pieces(refs[:len(arrs)], pieces), refs[len(arrs)][...])

    return pl.pallas_call(
        kern, name="proj_dx", grid=(rows // tm,),
        in_specs=[pl.BlockSpec((tm, a.shape[1]), lambda i: (i, 0)) for a in arrs]
        + [pl.BlockSpec(w.shape, lambda i: (0, 0), pipeline_mode=pl.Buffered(1))],
        out_specs=pl.BlockSpec((tm, n), lambda i: (i, 0)), out_shape=jax.ShapeDtypeStruct((rows, n), F32),
        compiler_params=pltpu.CompilerParams(dimension_semantics=("parallel",), vmem_limit_bytes=VMEM_LIMIT),
    )(*arrs, w.astype(MXU_DTYPE))


def _proj_dw(a, pieces):
    arrs = _piece_arrays(pieces)
    rows, m = a.shape
    k = sum(x.shape[1] for x in (p[0] if isinstance(p, tuple) else p for p in pieces))
    tk = min(PROJ_BWD_ROWS, rows)

    def kern(*refs):
        o_ref = refs[-1]

        @pl.when(pl.program_id(0) == 0)
        def _():
            o_ref[...] = jnp.zeros_like(o_ref)

        o_ref[...] += _dot_tn(refs[0][...], _join_pieces(refs[1:1 + len(arrs)], pieces))

    return pl.pallas_call(
        kern, name="proj_dw", grid=(rows // tk,),
        in_specs=[pl.BlockSpec((tk, m), lambda i: (i, 0))]
        + [pl.BlockSpec((tk, x.shape[1]), lambda i: (i, 0)) for x in arrs],
        out_specs=pl.BlockSpec((m, k), lambda i: (0, 0), pipeline_mode=pl.Buffered(1)),
        out_shape=jax.ShapeDtypeStruct((m, k), F32),
        compiler_params=pltpu.CompilerParams(dimension_semantics=("arbitrary",), vmem_limit_bytes=VMEM_LIMIT),
    )(a, *arrs)


def _rowwise(body, name, rows, tile, row_ins, full_ins, row_outs, acc_outs=(), scratch=(),
             reverse=False, sequential=False):
    n = rows // tile

    def ridx(i):
        return (n - 1 - i) if reverse else i

    in_specs, args = [], []
    for arr, width, cb in row_ins:
        in_specs.append(pl.BlockSpec((tile, width), lambda i, cb=cb: (ridx(i), cb)))
        args.append(arr)
    for arr in full_ins:
        in_specs.append(pl.BlockSpec(arr.shape, lambda i, nd=arr.ndim: (0,) * nd))
        args.append(arr)
    out_shape = [jax.ShapeDtypeStruct((rows, w), dt) for w, dt in row_outs]
    out_shape += [jax.ShapeDtypeStruct(s, dt) for s, dt in acc_outs]
    out_specs = [pl.BlockSpec((tile, w), lambda i: (ridx(i), 0)) for w, dt in row_outs]
    out_specs += [pl.BlockSpec(s, lambda i, nd=len(s): (0,) * nd) for s, dt in acc_outs]

    def kern(*refs):
        body(pl.program_id(0), *refs)

    sem = "arbitrary" if (acc_outs or sequential) else "parallel"
    return pl.pallas_call(
        kern, name=name, grid=(n,), in_specs=in_specs, out_specs=out_specs, out_shape=out_shape,
        scratch_shapes=list(scratch),
        compiler_params=pltpu.CompilerParams(dimension_semantics=(sem,), vmem_limit_bytes=VMEM_LIMIT),
    )(*args)


def _ln_stats(u):
    mu = jnp.mean(u, axis=-1, keepdims=True)
    xc = u - mu
    var = jnp.mean(xc * xc, axis=-1, keepdims=True)
    return xc, lax.rsqrt(var + LN_EPS)


def _ln_fwd(a, b, g, beta, name):
    rows, d = a.shape
    has_b = b is not None

    def body(i, *refs):
        if has_b:
            a_ref, b_ref, g_ref, be_ref, h_ref, hb_ref = refs
            u = ALPHA * a_ref[...] + b_ref[...]
        else:
            a_ref, g_ref, be_ref, h_ref, hb_ref = refs
            u = a_ref[...]
        xc, rstd = _ln_stats(u)
        y = xc * rstd * g_ref[...] + be_ref[...]
        h_ref[...] = y
        hb_ref[...] = y.astype(hb_ref.dtype)

    row_ins = [(a, d, 0)] + ([(b, d, 0)] if has_b else [])
    return _rowwise(body, name, rows, min(ROW_TILE, rows), row_ins,
                    [g.reshape(1, d), beta.reshape(1, d)], [(d, F32), (d, MXU_DTYPE)])


def _ln_bwd(u, g, dy1, dy2, c1, name, gate=None):
    rows, d = u.shape
    has_2 = dy2 is not None
    n_groups = len(gate[1]) if gate else 0

    def body(i, *refs):
        refs = list(refs)
        uu = refs.pop(0)[...]
        dy = c1 * refs.pop(0)[...]
        if has_2:
            dy = dy + refs.pop(0)[...]
        if gate:
            mixed = jnp.concatenate([refs.pop(0)[...] for _ in range(n_groups)], axis=1)
            gt = refs.pop(0)[...]
            g_ref, w_ref, du_ref, dub_ref, dm_ref, dgate_ref, dg_ref, db_ref = refs
        else:
            g_ref, du_ref, dg_ref, db_ref = refs

        @pl.when(i == 0)
        def _():
            dg_ref[...] = jnp.zeros_like(dg_ref)
            db_ref[...] = jnp.zeros_like(db_ref)

        xc, rstd = _ln_stats(uu)
        xhat = xc * rstd
        dxh = dy * g_ref[...]
        m1 = jnp.mean(dxh, axis=-1, keepdims=True)
        m2 = jnp.mean(dxh * xhat, axis=-1, keepdims=True)
        du = rstd * (dxh - m1 - xhat * m2)
        du_ref[...] = du
        dg_ref[...] += jnp.sum(dy * xhat, axis=0, keepdims=True)
        db_ref[...] += jnp.sum(dy, axis=0, keepdims=True)
        if gate:
            dub = du.astype(dub_ref.dtype)
            dub_ref[...] = dub
            dgated = _dot_nt(dub, w_ref[...])
            sig = 1.0 / (1.0 + jnp.exp(-gt))
            dm_ref[...] = dgated * (gt * sig)
            dgate_ref[...] = (dgated * mixed * (sig * (1.0 + gt * (1.0 - sig)))).astype(dgate_ref.dtype)

    row_ins = [(u, d, 0), (dy1, d, 0)] + ([(dy2, d, 0)] if has_2 else [])
    full_ins = [g.reshape(1, d)]
    row_outs = [(d, F32)]
    if gate:
        w_out, groups, pb = gate
        w = GROUP_W * n_groups
        row_ins += [(o, GROUP_W, 0) for o in groups] + [(pb, w, 0)]
        full_ins += [w_out.astype(MXU_DTYPE)]
        row_outs += [(d, MXU_DTYPE), (w, F32), (w, MXU_DTYPE)]
    return _rowwise(body, name, rows, min(ROW_TILE, rows), row_ins, full_ins, row_outs,
                    [((1, d), F32), ((1, d), F32)])


def _gate_out_proj_ln(groups, pb, w_out, h, g, beta, target=None):
    rows, d = h.shape
    last = target is not None
    w = GROUP_W * len(groups)

    def body(i, *refs):
        refs = list(refs)
        mixed = jnp.concatenate([refs.pop(0)[...] for _ in groups], axis=1)
        gate = refs.pop(0)[...]
        if last:
            h_ref, t_ref, w_ref, g_ref, be_ref, a_ref, u_ref, dh_ref, acc_ref = refs

            @pl.when(i == 0)
            def _():
                acc_ref[...] = jnp.zeros_like(acc_ref)
        else:
            h_ref, w_ref, g_ref, be_ref, a_ref, u_ref, o_ref, ob_ref = refs
        gated = (mixed * (gate / (1.0 + jnp.exp(-gate)))).astype(a_ref.dtype)
        a_ref[...] = gated
        u = ALPHA * h_ref[...] + _dot(gated, w_ref[...])
        u_ref[...] = u
        xc, rstd = _ln_stats(u)
        y = xc * rstd * g_ref[...] + be_ref[...]
        if last:
            e = y - t_ref[...]
            dh_ref[...] = e * (1.0 / d)
            acc_ref[...] += jnp.sum(e * e, axis=0, keepdims=True)
        else:
            o_ref[...] = y
            ob_ref[...] = y.astype(ob_ref.dtype)

    row_ins = [(o, GROUP_W, 0) for o in groups] + [(pb, w, 0), (h, d, 0)] + ([(target, d, 0)] if last else [])
    full_ins = [w_out.astype(MXU_DTYPE), g.reshape(1, d), beta.reshape(1, d)]
    if last:
        return _rowwise(body, "gate_out_proj_ln_loss", rows, PROJ_BWD_ROWS, row_ins, full_ins,
                        [(w, MXU_DTYPE), (d, F32), (d, F32)], [((1, d), F32)])
    return _rowwise(body, "gate_out_proj_ln", rows, PROJ_BWD_ROWS, row_ins, full_ins,
                    [(w, MXU_DTYPE), (d, F32), (d, F32), (d, MXU_DTYPE)])


def _tri(n, kind):
    r = np.arange(n)[:, None]
    c = np.arange(n)[None, :]
    m = {"lower_incl": r >= c, "upper_incl": r <= c, "row_gt_col": r > c, "row_lt_col": r < c}[kind]
    return jnp.asarray(m.astype(np.float32), dtype=MXU_DTYPE)


def _forget_fwd(pb, bias_row):
    rows = pb.shape[0]
    tile = _pick(rows, (1024, 512, 256))

    def body(i, x_ref, b_ref, l_ref, o_ref, carry_ref):
        @pl.when(i == 0)
        def _():
            carry_ref[...] = jnp.zeros_like(carry_ref)

        xx = x_ref[...] + b_ref[...]
        lane = lax.broadcasted_iota(jnp.int32, xx.shape, 1)
        lf = jnp.where(lane < N_HEADS, -_softplus(-xx), 0.0)
        o_ref[...] = _dot_exact_l(l_ref[...], lf) + carry_ref[...]
        carry_ref[...] += jnp.sum(lf, axis=0, keepdims=True)

    return _rowwise(body, "forget_fwd", rows, tile, [(pb, 128, PB_MISC_BLK)],
                    [bias_row, _tri(tile, "lower_incl")], [(128, F32)],
                    scratch=[pltpu.VMEM((1, 128), F32)], sequential=True)[0]


def _forget_bwd(pb, bias_row, dfc):
    rows = pb.shape[0]
    tile = _pick(rows, (1024, 512, 256))

    def body(i, x_ref, df_ref, b_ref, u_ref, o_ref, db_ref, carry_ref):
        @pl.when(i == 0)
        def _():
            carry_ref[...] = jnp.zeros_like(carry_ref)
            db_ref[...] = jnp.zeros_like(db_ref)

        df = df_ref[...]
        sfx = _dot_exact_l(u_ref[...], df) + carry_ref[...]
        carry_ref[...] += jnp.sum(df, axis=0, keepdims=True)
        xx = x_ref[...] + b_ref[...]
        lane = lax.broadcasted_iota(jnp.int32, xx.shape, 1)
        dl = jnp.where(lane < N_HEADS, sfx / (1.0 + jnp.exp(xx)), 0.0)
        o_ref[...] = dl
        db_ref[...] += jnp.sum(dl, axis=0, keepdims=True)

    return _rowwise(body, "forget_bwd", rows, tile,
                    [(pb, 128, PB_MISC_BLK), (dfc, 128, 0)],
                    [bias_row, _tri(tile, "upper_incl")], [(128, F32)], [((1, 128), F32)],
                    scratch=[pltpu.VMEM((1, 128), F32)], reverse=True, sequential=True)


def _rope_tables(s):
    half = MLA_ROPE // 2
    inv_freq = ROPE_THETA ** (-jnp.arange(half, dtype=F32) / half)
    ang = jnp.arange(s).astype(F32)[:, None] * inv_freq[None, :]
    cos2 = jnp.tile(jnp.cos(ang), (1, 2))
    sin2 = jnp.tile(jnp.sin(ang), (1, 2))
    cx = jnp.tile(cos2, (1, N_HEADS))
    sx = jnp.tile(sin2, (1, N_HEADS))
    pad = ((0, 0), (MISC_KROT, 128 - MISC_KROT - MLA_ROPE))
    ck = jnp.pad(cos2, pad)
    sk = jnp.pad(sin2, pad)
    return dict(ck=ck, sk=sk, cx=cx, sx=sx)


def _rot_matrix(width, bases):
    half = MLA_ROPE // 2
    p = np.zeros((width, width), np.float32)
    for b in bases:
        for i in range(half):
            p[b + half + i, b + i] = -1.0
            p[b + i, b + half + i] = 1.0
    return p


def _rope_matrices():
    pk = _rot_matrix(128, [MISC_KROT])
    p4 = _rot_matrix(128, [h * MLA_ROPE for h in range(N_HEADS)])
    a = np.zeros((128, 128), np.float32)
    for h in range(N_HEADS):
        for r in range(MLA_ROPE):
            a[h * MLA_ROPE + r, MISC_KROT + r] = 1.0
    cast = lambda m: jnp.asarray(m, dtype=MXU_DTYPE)
    return dict(p4=cast(p4), p4t=cast(p4.T), pk=cast(pk), spread=cast(a.T), xa=cast(a), xb=cast(p4.T @ a))


def _rms(c, g):
    r = lax.rsqrt(jnp.mean(c * c, axis=-1, keepdims=True) + RMS_EPS)
    return c * r * g


def _mla_q_fwd(pb, g, w_up, tabs, mats):
    rows = pb.shape[0]

    def body(i, c_ref, cos_ref, sin_ref, g_ref, w_ref, p_ref, q_ref, cn_ref):
        cn = _rms(c_ref[...], g_ref[...]).astype(cn_ref.dtype)
        cn_ref[...] = cn
        q = _dot(cn, w_ref[...])
        qr = q[:, GROUP_W:]
        q_ref[:, :GROUP_W] = q[:, :GROUP_W].astype(q_ref.dtype)
        q_ref[:, GROUP_W:] = (qr * cos_ref[...] + _dot_exact_r(qr, p_ref[...]) * sin_ref[...]).astype(q_ref.dtype)

    return _rowwise(body, "mla_q_fwd", rows, ROW_TILE,
                    [(pb, MLA_Q_RANK, PB_CQ_BLK), (tabs["cx"], PAIR, 0), (tabs["sx"], PAIR, 0)],
                    [g.reshape(1, MLA_Q_RANK), w_up.astype(MXU_DTYPE), mats["p4"]],
                    [(MLA_Q_COLS, MXU_DTYPE), (MLA_Q_RANK, MXU_DTYPE)])


def _mla_kv_fwd(pb, g, w_up, tabs, mats):
    rows = pb.shape[0]

    def body(i, c_ref, x_ref, cos_ref, sin_ref, g_ref, w_ref, p_ref, sp_ref, kv_ref, cn_ref, kr_ref):
        cn = _rms(c_ref[...], g_ref[...]).astype(cn_ref.dtype)
        cn_ref[...] = cn
        kv_ref[...] = _dot(cn, w_ref[...]).astype(kv_ref.dtype)
        xx = x_ref[...]
        kr = xx * cos_ref[...] + _dot_exact_r(xx, p_ref[...]) * sin_ref[...]
        kr_ref[...] = _dot_exact_r(kr, sp_ref[...]).astype(kr_ref.dtype)

    return _rowwise(body, "mla_kv_fwd", rows, ROW_TILE,
                    [(pb, MLA_KV_RANK, PB_CKV_BLK), (pb, 128, PB_MISC_BLK), (tabs["ck"], 128, 0), (tabs["sk"], 128, 0)],
                    [g.reshape(1, MLA_KV_RANK), w_up.astype(MXU_DTYPE), mats["pk"], mats["spread"]],
                    [(MLA_KV_COLS, MXU_DTYPE), (MLA_KV_RANK, MXU_DTYPE), (128, MXU_DTYPE)])


def _rms_bwd(c, g, dy):
    r = lax.rsqrt(jnp.mean(c * c, axis=-1, keepdims=True) + RMS_EPS)
    dyg = dy * g
    dc = r * dyg - c * (r * r * r) * jnp.mean(c * dyg, axis=-1, keepdims=True)
    return dc, jnp.sum(dy * c * r, axis=0, keepdims=True)


def _mla_q_bwd(dq_full, pb, g, w_up, tabs, mats):
    rows, nq = dq_full.shape

    def body(i, d_ref, c_ref, cos_ref, sin_ref, g_ref, w_ref, pt_ref, dq_ref, dc_ref, dg_ref):
        @pl.when(i == 0)
        def _():
            dg_ref[...] = jnp.zeros_like(dg_ref)

        dr = d_ref[:, GROUP_W:]
        dq = jnp.concatenate([d_ref[:, :GROUP_W], dr * cos_ref[...] + _dot_exact_r(dr * sin_ref[...], pt_ref[...])],
                             axis=1).astype(dq_ref.dtype)
        dq_ref[...] = dq
        dc, dg = _rms_bwd(c_ref[...], g_ref[...], _dot_nt(dq, w_ref[...]))
        dc_ref[...] = dc.astype(dc_ref.dtype)
        dg_ref[...] += dg

    return _rowwise(body, "mla_q_bwd", rows, ROW_TILE,
                    [(dq_full, nq, 0), (pb, MLA_Q_RANK, PB_CQ_BLK), (tabs["cx"], PAIR, 0), (tabs["sx"], PAIR, 0)],
                    [g.reshape(1, MLA_Q_RANK), w_up.astype(MXU_DTYPE), mats["p4t"]],
                    [(nq, MXU_DTYPE), (MLA_Q_RANK, MXU_DTYPE)], [((1, MLA_Q_RANK), F32)])


def _mla_kv_bwd(dk, dv, pb, g, w_up, tabs, mats):
    rows = dk.shape[0]

    def body(i, dk_ref, dr_ref, dv_ref, c_ref, cos_ref, sin_ref, g_ref, w_ref, a_ref, b_ref,
             dkv_ref, dc_ref, dm_ref, dg_ref):
        @pl.when(i == 0)
        def _():
            dg_ref[...] = jnp.zeros_like(dg_ref)

        dkv = jnp.concatenate([dk_ref[...], dv_ref[...]], axis=1).astype(dkv_ref.dtype)
        dkv_ref[...] = dkv
        dc, dg = _rms_bwd(c_ref[...], g_ref[...], _dot_nt(dkv, w_ref[...]))
        dc_ref[...] = dc.astype(dc_ref.dtype)
        dg_ref[...] += dg
        dr = dr_ref[...]
        dm_ref[...] = _dot_exact_r(dr * cos_ref[...], a_ref[...]) + _dot_exact_r(dr * sin_ref[...], b_ref[...])

    return _rowwise(body, "mla_kv_bwd", rows, ROW_TILE,
                    [(dk, GROUP_W, 0), (dk, PAIR, 2), (dv, GROUP_W, 0), (pb, MLA_KV_RANK, PB_CKV_BLK),
                     (tabs["cx"], PAIR, 0), (tabs["sx"], PAIR, 0)],
                    [g.reshape(1, MLA_KV_RANK), w_up.astype(MXU_DTYPE), mats["xa"], mats["xb"]],
                    [(MLA_KV_COLS, MXU_DTYPE), (MLA_KV_RANK, MXU_DTYPE), (PAIR, F32)], [((1, MLA_KV_RANK), F32)])


def _att_params(parallel):
    return pltpu.CompilerParams(dimension_semantics=("parallel" if parallel else "arbitrary",),
                                vmem_limit_bytes=VMEM_LIMIT)


def _blk_off(j, t):
    return j * t if isinstance(j, int) else pl.multiple_of(j * t, t)


def _causal_mask(t, strict):
    r = lax.broadcasted_iota(jnp.int32, (t, t), 0)
    c = lax.broadcasted_iota(jnp.int32, (t, t), 1)
    return (c < r) if strict else (c <= r)


def _lane_mask(kind, head, rows):
    lane = lax.broadcasted_iota(jnp.int32, (rows, PAIR), 1)
    if kind == "pair":
        return (lane < HEAD_DIM) if head % 2 == 0 else (lane >= HEAD_DIM)
    return (lane >= MLA_ROPE * head) & (lane < MLA_ROPE * (head + 1))


def _row_spec(t, cb, width=PAIR):
    return pl.BlockSpec((t, width), lambda i, cb=cb: (i, cb))


def _whole_spec(rows, cb, width=PAIR):
    return pl.BlockSpec((rows, width), lambda i, cb=cb: (0, cb), pipeline_mode=pl.Buffered(1))


def _is_pow2(x):
    return float(np.frexp(x)[0]) == 0.5


def _masked_heads(blocks, kinds, pair, dtype, scale=None):
    out = []
    for e in range(2):
        head = 2 * pair + e
        parts = [jnp.where(_lane_mask(k, head, b.shape[0]), b.astype(F32) * (1.0 if scale is None else scale),
                           0.0).astype(dtype)
                 for b, k in zip(blocks, kinds)]
        out.append(parts[0] if len(parts) == 1 else jnp.concatenate(parts, axis=1))
    return out


def _logit_reach(qh, kmax2, head):
    q32 = qh.astype(F32)
    return jnp.sqrt(jnp.sum(q32 * q32, axis=1, keepdims=True) * _col(kmax2, head)) * REACH_SLACK


def _forget_top(ft_ref, head, off):
    return jnp.max(-ft_ref[head:head + 1, pl.ds(off, PAIR)])


def _col(block, idx):
    lane = lax.broadcasted_iota(jnp.int32, block.shape, 1)
    return jnp.sum(jnp.where(lane == idx, block, 0.0), axis=1, keepdims=True)


def _scatter_cols(cols, t):
    lane = lax.broadcasted_iota(jnp.int32, (t, PAIR), 1)
    out = jnp.zeros((t, PAIR), F32)
    for idx, c in cols.items():
        out = out + jnp.where(lane == idx, c, 0.0)
    return out


def _take_heads(per_head, pair):
    return jnp.where(_lane_mask("pair", 0, per_head[0].shape[0]), per_head[0], per_head[1])


class _Parts:
    def __init__(self, q_parts, k_parts, v_parts, tq, sk):
        self.kinds = [[kind for _, _, kind in q_parts[p]] for p in range(2)]
        self.nparts = len(q_parts[0])
        self.q_specs = [_row_spec(tq, cb) for p in range(2) for _, cb, _ in q_parts[p]]
        self.q_args = [a for p in range(2) for a, _, _ in q_parts[p]]
        self.k_specs = [_whole_spec(sk, cb) for p in range(2) for _, cb, _ in k_parts[p]]
        self.k_args = [a for p in range(2) for a, _, _ in k_parts[p]]
        self.v_specs = [_whole_spec(sk, cb) for _, cb in v_parts]
        self.v_args = [a for a, _ in v_parts]
        self.width = PAIR * self.nparts

    def split(self, refs):
        n = self.nparts
        refs = list(refs)
        q = [refs[p * n:(p + 1) * n] for p in range(2)]
        k = [refs[2 * n + p * n:2 * n + (p + 1) * n] for p in range(2)]
        v = refs[4 * n:4 * n + 2]
        return q, k, v, refs[4 * n + 2:]

    def k_block(self, k_refs, off, t):
        blks = [r[pl.ds(off, t), :] for r in k_refs]
        return blks[0] if len(blks) == 1 else jnp.concatenate(blks, axis=1)


def _softmax_fwd(q_parts, k_parts, v_parts, sq, sk, scale, causal, bias, name):
    tq = min(1024 if bias is None else SOFTMAX_TILE, sq) if causal else _pick(sq, (DENSE_QUERY_TILE, SOFTMAX_TILE))
    tk = tq if causal else min(SOFTMAX_TILE, sk)
    nkv = sk // tk
    pp = _Parts(q_parts, k_parts, v_parts, tq, sk)

    def kern(*refs):
        q_refs, k_refs, v_refs, rest = pp.split(refs)
        if bias is not None:
            fc_ref, ft_ref, kmax_ref, o_ref, lse_ref = rest
            fcb = fc_ref[...]
        else:
            o_ref, lse_ref = rest
        i = pl.program_id(0)
        fold = _is_pow2(scale)
        head_on = [jnp.where(_lane_mask("pair", e, tk), 1.0, 0.0).astype(MXU_DTYPE) for e in range(2)]
        head_off = [jnp.where(_lane_mask("pair", e, tk), 0.0, 1.0).astype(MXU_DTYPE) for e in range(2)]
        lse_cols = {}
        for first in range(0, 2, FWD_PAIRS_PER_LOOP):
            pairs = list(range(first, first + FWD_PAIRS_PER_LOOP))
            heads = [2 * p + e for p in pairs for e in range(2)]
            qm = {}
            for p in pairs:
                masked_q = _masked_heads([r[...] for r in q_refs[p]], pp.kinds[p], p, MXU_DTYPE, scale if fold else None)
                qm.update({2 * p + e: masked_q[e] for e in range(2)})
            if bias is not None:
                reach = {h: _logit_reach(qm[h], kmax_ref[...], h) for h in heads}

            def block(j, carry, masked, pairs=pairs, qm=qm):
                off = _blk_off(j, tk)
                out = []
                for p in pairs:
                    kb = pp.k_block(k_refs[p], off, tk)
                    vb = v_refs[p][pl.ds(off, tk), :]
                    for e in range(2):
                        h = 2 * p + e
                        m, acc = carry[len(out)]
                        s = _dot_nt(qm[h], kb) * (LOG2E if fold else scale * LOG2E)
                        if bias is not None:
                            s = s - ft_ref[h:h + 1, pl.ds(off, tk)] * LOG2E
                        if masked:
                            s = jnp.where(_causal_mask(tq, False), s, NEG)
                        m_new = jnp.maximum(m, jnp.max(s, axis=1, keepdims=True))
                        pr = jnp.exp2(s - m_new).astype(MXU_DTYPE)
                        out.append((m_new, jnp.exp2(m - m_new) * acc + _dot(pr, vb * head_on[e] + head_off[e])))
                return tuple(out)

            carry = tuple((jnp.full((tq, 1), NEG, F32), jnp.zeros((tq, PAIR), F32)) for _ in heads)
            if causal and bias is not None:
                def alive(c, j, heads=heads, reach=reach):
                    return functools.reduce(jnp.maximum, [
                        jnp.max(reach[h] + _forget_top(ft_ref, h, _blk_off(j, tk)) - c[n][0] * LN2)
                        for n, h in enumerate(heads)])

                def step(state, block=block, alive=alive):
                    n, _, c = state
                    c = block(i - 1 - n, c, False)
                    return n + 1, alive(c, i - 1 - n), c

                carry = block(i, carry, True)
                _, _, carry = lax.while_loop(lambda st: jnp.logical_and(st[0] < i, st[1] > DEAD_LOGIT), step,
                                             (jnp.int32(0), alive(carry, i), carry))
            elif causal:
                carry = lax.fori_loop(0, i, lambda j, c, block=block: block(j, c, False), carry)
                carry = block(i, carry, True)
            else:
                for j in range(nkv):
                    carry = block(j, carry, False)
            for n, p in enumerate(pairs):
                outs = []
                for e in range(2):
                    m, acc = carry[2 * n + e]
                    l = _col(acc, HEAD_DIM * (1 - e))
                    outs.append(acc / l)
                    lse_cols[2 * p + e] = m * LN2 + jnp.log(l) + (_col(fcb, 2 * p + e) if bias is not None else 0.0)
                o_ref[:, p * PAIR:(p + 1) * PAIR] = _take_heads(outs, p)
        lse_ref[...] = _scatter_cols(lse_cols, tq)

    in_specs = pp.q_specs + pp.k_specs + pp.v_specs
    args = pp.q_args + pp.k_args + pp.v_args
    if bias is not None:
        in_specs += [_row_spec(tq, 0), pl.BlockSpec((8, sk), lambda i: (0, 0), pipeline_mode=pl.Buffered(1)),
                     pl.BlockSpec((1, PAIR), lambda i: (0, 0))]
        args += list(bias)
    return pl.pallas_call(
        kern, name=name, grid=(sq // tq,), in_specs=in_specs,
        out_specs=[_row_spec(tq, 0, GROUP_W), _row_spec(tq, 0)],
        out_shape=[jax.ShapeDtypeStruct((sq, GROUP_W), F32), jax.ShapeDtypeStruct((sq, PAIR), F32)],
        compiler_params=_att_params(True),
    )(*args)


def _softmax_bwd(q_parts, k_parts, v_parts, o, lse, do, do_blk, sq, sk, scale, causal, bias, name):
    tq = min(SOFTMAX_TILE, sq) if causal else _pick(sq, (DENSE_QUERY_TILE, SOFTMAX_TILE))
    tk = tq if causal else min(SOFTMAX_TILE, sk)
    nkv = sk // tk
    pp = _Parts(q_parts, k_parts, v_parts, tq, sk)
    quad = pp.nparts == 2
    wq = GROUP_W + (PAIR if quad else 0)

    def kern(*refs):
        q_refs, k_refs, v_refs, rest = pp.split(refs)
        if bias is not None:
            o_ref, lse_ref, do_ref, fc_ref, ft_ref, kmax_ref, dq_ref, dk_ref, dv_ref, dfq_ref, dfk_ref = rest
            fcb = fc_ref[...]
        else:
            o_ref, lse_ref, do_ref, dq_ref, dk_ref, dv_ref = rest
        i = pl.program_id(0)
        fold = _is_pow2(scale)

        @pl.when(i == 0)
        def _():
            dk_ref[...] = jnp.zeros_like(dk_ref)
            dv_ref[...] = jnp.zeros_like(dv_ref)
            if bias is not None:
                dfk_ref[...] = jnp.zeros_like(dfk_ref)

        lse_b = lse_ref[...]
        qm, dom, delta, lse_h = [], [], [], []
        for p in range(2):
            qm += _masked_heads([r[...] for r in q_refs[p]], pp.kinds[p], p, MXU_DTYPE, scale if fold else None)
            do_p = do_ref[:, p * PAIR:(p + 1) * PAIR]
            dom += _masked_heads([do_p], ["pair"], p, MXU_DTYPE)
            prod = do_p * o_ref[:, p * PAIR:(p + 1) * PAIR]
            for e in range(2):
                h = 2 * p + e
                delta.append(jnp.sum(jnp.where(_lane_mask("pair", h, tq), prod, 0.0), axis=1, keepdims=True))
                lse_h.append(_col(lse_b, h) - (_col(fcb, h) if bias is not None else 0.0))

        def block(j, carry, masked, pairs):
            off = _blk_off(j, tk)
            out = []
            for p in pairs:
                kb = pp.k_block(k_refs[p], off, tk)
                vb = v_refs[p][pl.ds(off, tk), :]
                dk_acc = jnp.zeros((tk, pp.width), F32)
                dv_acc = jnp.zeros((tk, PAIR), F32)
                for e in range(2):
                    h = 2 * p + e
                    dq, dfq = carry[len(out)]
                    s = _dot_nt(qm[h], kb)
                    if not fold:
                        s = s * scale
                    if bias is not None:
                        s = s - ft_ref[h:h + 1, pl.ds(off, tk)]
                    if masked:
                        s = jnp.where(_causal_mask(tq, False), s, NEG)
                    pr = jnp.exp(s - lse_h[h])
                    ds = pr * (_dot_nt(dom[h], vb) - delta[h])
                    dsb = (ds if fold else ds * scale).astype(MXU_DTYPE)
                    dv_acc = dv_acc + _dot_tn(pr.astype(MXU_DTYPE), dom[h])
                    dk_acc = dk_acc + _dot_tn(dsb, qm[h])
                    dq = dq + _dot(dsb, kb)
                    if bias is not None:
                        dfq = dfq + jnp.sum(ds, axis=1, keepdims=True)
                        dfk_ref[h:h + 1, pl.ds(off, tk)] -= jnp.sum(ds, axis=0, keepdims=True)
                    out.append((dq, dfq))
                dv_ref[pl.ds(off, tk), p * PAIR:(p + 1) * PAIR] += dv_acc
                dk_ref[pl.ds(off, tk), p * PAIR:(p + 1) * PAIR] += dk_acc[:, :PAIR]
                if quad:
                    dk_ref[pl.ds(off, tk), GROUP_W:] += dk_acc[:, PAIR:]
            return tuple(out)

        done = {}
        for first in range(0, 2, BWD_PAIRS_PER_LOOP):
            pairs = list(range(first, first + BWD_PAIRS_PER_LOOP))
            heads = [2 * p + e for p in pairs for e in range(2)]
            carry = tuple((jnp.zeros((tq, pp.width), F32), jnp.zeros((tq, 1), F32)) for _ in heads)
            if causal and bias is not None:
                reach = {h: _logit_reach(qm[h], kmax_ref[...], h) - lse_h[h] for h in heads}

                def alive(j, heads=heads, reach=reach):
                    return functools.reduce(jnp.maximum, [
                        jnp.max(reach[h] + _forget_top(ft_ref, h, _blk_off(j, tk))) for h in heads])

                def step(state, pairs=pairs, alive=alive):
                    n, _, c = state
                    return n + 1, alive(i - 1 - n), block(i - 1 - n, c, False, pairs)

                carry = block(i, carry, True, pairs)
                _, _, carry = lax.while_loop(lambda st: jnp.logical_and(st[0] < i, st[1] > DEAD_LOGIT), step,
                                             (jnp.int32(0), alive(i), carry))
            elif causal:
                carry = lax.fori_loop(0, i, lambda j, c, pairs=pairs: block(j, c, False, pairs), carry)
                carry = block(i, carry, True, pairs)
            else:
                for j in range(nkv):
                    carry = block(j, carry, False, pairs)
            done.update(zip(heads, carry))
        carry = [done[h] for h in range(N_HEADS)]
        dqs = [c[0] * scale if fold else c[0] for c in carry]
        for p in range(2):
            dq_ref[:, p * PAIR:(p + 1) * PAIR] = _take_heads([dqs[2 * p + e][:, :PAIR] for e in range(2)], p)
        if quad:
            dq_ref[:, GROUP_W:] = sum(jnp.where(_lane_mask("quad", h, tq), dqs[h][:, PAIR:], 0.0)
                                      for h in range(N_HEADS))
        if bias is not None:
            dfq_ref[...] = _scatter_cols({h: carry[h][1] for h in range(N_HEADS)}, tq)

    acc_spec = lambda rows, width: pl.BlockSpec((rows, width), lambda i: (0, 0), pipeline_mode=pl.Buffered(1))
    in_specs = pp.q_specs + pp.k_specs + pp.v_specs + [_row_spec(tq, 0, GROUP_W), _row_spec(tq, 0),
                                                       _row_spec(tq, do_blk, GROUP_W)]
    args = pp.q_args + pp.k_args + pp.v_args + [o, lse, do]
    out_specs = [_row_spec(tq, 0, wq), acc_spec(sk, wq), acc_spec(sk, GROUP_W)]
    out_shape = [jax.ShapeDtypeStruct((sq, wq), F32), jax.ShapeDtypeStruct((sk, wq), F32),
                 jax.ShapeDtypeStruct((sk, GROUP_W), F32)]
    if bias is not None:
        in_specs += [_row_spec(tq, 0), pl.BlockSpec((8, sk), lambda i: (0, 0), pipeline_mode=pl.Buffered(1)),
                     pl.BlockSpec((1, PAIR), lambda i: (0, 0))]
        args += list(bias)
        out_specs += [_row_spec(tq, 0), acc_spec(8, sk)]
        out_shape += [jax.ShapeDtypeStruct((sq, PAIR), F32), jax.ShapeDtypeStruct((8, sk), F32)]
    return pl.pallas_call(
        kern, name=name, grid=(sq // tq,), in_specs=in_specs, out_specs=out_specs, out_shape=out_shape,
        compiler_params=_att_params(False),
    )(*args)


def _sb_logs(qh, kb, valid):
    z = _dot_nt(qh, kb)
    sp = _softplus(z)
    lk = -sp
    if valid is not None:
        lk = jnp.where(valid, lk, 0.0)
    return lk, z - sp


def _sb_valid(d, tq, tk):
    r = lax.broadcasted_iota(jnp.int32, (tq, tk), 0)
    c = lax.broadcasted_iota(jnp.int32, (tq, tk), 1)
    return c + d * tk < r


def _tri_sums(xs, tri):
    t = xs[0].shape[0]
    pieces = [_split2(x) for x in xs]
    hi = _dot(jnp.concatenate([pc[0] for pc in pieces], axis=0), tri)
    lo = _dot(jnp.concatenate([pc[1] for pc in pieces], axis=0), tri)
    return [hi[n * t:(n + 1) * t] + lo[n * t:(n + 1) * t] for n in range(len(xs))]


def _sb_fwd(src, q_blk, k_blk, v_blk, s, scale, name):
    assert _is_pow2(scale)
    tq, t = min(SB_QUERY_TILE, s), min(ATT_TILE, s)
    slots = -(-(s // t) // SB_SLOT) * SB_SLOT
    width = N_HEADS * slots
    band = tq // t
    pair = lambda blk: [[(src, blk + p, "pair")] for p in range(2)]
    pp = _Parts(pair(q_blk), pair(k_blk), [(src, v_blk + p) for p in range(2)], tq, s)

    def kern(*refs):
        q_refs, k_refs, v_refs, (tri_ref, o_ref, rm_ref, cnt_ref) = pp.split(refs)
        i = pl.program_id(0)
        tri = tri_ref[...]
        lane = lax.broadcasted_iota(jnp.int32, (tq, width), 1)
        qm = []
        for p in range(2):
            qm += _masked_heads([q_refs[p][0][...]], ["pair"], p, MXU_DTYPE, scale)

        def block(j, carry, valid):
            accs, rights, rm = carry
            off = _blk_off(j, t)
            kbs = [k_refs[p][0][pl.ds(off, t), :] for p in range(2)]
            vbs = [v_refs[p][pl.ds(off, t), :] for p in range(2)]
            logs = [_sb_logs(qm[h], kbs[h // 2], valid) for h in range(N_HEADS)]
            tails = _tri_sums([lg[0] for lg in logs], tri)
            new_acc, new_right = [], []
            for h in range(N_HEADS):
                lk, ls = logs[h]
                w = jnp.exp(ls + tails[h] + rights[h])
                if valid is not None:
                    w = jnp.where(valid, w, 0.0)
                new_acc.append(accs[h] + _dot(w.astype(MXU_DTYPE), vbs[h // 2]))
                rm = rm + jnp.where(lane == slots * h + j, rights[h], 0.0)
                new_right.append(rights[h] + jnp.sum(lk, axis=1, keepdims=True))
            return tuple(new_acc), tuple(new_right), rm

        carry = (tuple(jnp.zeros((tq, PAIR), F32) for _ in range(N_HEADS)),
                 tuple(jnp.zeros((tq, 1), F32) for _ in range(N_HEADS)), jnp.zeros((tq, width), F32))
        for d in reversed(range(band)):
            carry = block(band * i + d, carry, _sb_valid(d, tq, t))

        def alive(c):
            return functools.reduce(jnp.maximum, [jnp.max(r) for r in c[1]])

        def step(state):
            n, _, c = state
            c = block(band * i - 1 - n, c, None)
            return n + 1, alive(c), c

        n_done, _, carry = lax.while_loop(lambda st: jnp.logical_and(st[0] < band * i, st[1] > EXP_UNDERFLOW),
                                          step, (jnp.int32(0), alive(carry), carry))
        cnt_ref[i] = n_done
        for p in range(2):
            o_ref[:, p * PAIR:(p + 1) * PAIR] = _take_heads([carry[0][2 * p + e] for e in range(2)], p)
        rm_ref[...] = carry[2]

    return pl.pallas_call(
        kern, name=name, grid=(s // tq,),
        in_specs=pp.q_specs + pp.k_specs + pp.v_specs + [pl.BlockSpec((t, t), lambda i: (0, 0))],
        out_specs=[_row_spec(tq, 0, GROUP_W), _row_spec(tq, 0, width), pl.BlockSpec(memory_space=pltpu.SMEM)],
        out_shape=[jax.ShapeDtypeStruct((s, GROUP_W), F32), jax.ShapeDtypeStruct((s, width), F32),
                   jax.ShapeDtypeStruct((s // tq,), jnp.int32)],
        compiler_params=_att_params(False),
    )(*(pp.q_args + pp.k_args + pp.v_args + [_tri(t, "row_gt_col")]))


def _sb_bwd(src, q_blk, k_blk, v_blk, do, do_blk, rm, visited, s, scale, name):
    assert _is_pow2(scale)
    tq, t = min(SB_QUERY_TILE, s), min(ATT_TILE, s)
    band = tq // t
    pair = lambda blk: [[(src, blk + p, "pair")] for p in range(2)]
    pp = _Parts(pair(q_blk), pair(k_blk), [(src, v_blk + p) for p in range(2)], tq, s)

    def kern(*refs):
        q_refs, k_refs, v_refs, (do_ref, rm_ref, tri_ref, pre_ref, cnt_ref, dq_ref, dk_ref, dv_ref) = pp.split(refs)
        i = pl.program_id(0)

        @pl.when(i == 0)
        def _():
            dk_ref[...] = jnp.zeros_like(dk_ref)
            dv_ref[...] = jnp.zeros_like(dv_ref)

        rmb = rm_ref[...]
        tri = tri_ref[...]
        pre = pre_ref[...]
        qm, dom = [], []
        for p in range(2):
            qm += _masked_heads([q_refs[p][0][...]], ["pair"], p, MXU_DTYPE, scale)
            dom += _masked_heads([do_ref[:, p * PAIR:(p + 1) * PAIR]], ["pair"], p, MXU_DTYPE)

        def block(j, carry, valid):
            dqs, lefts = carry
            off = _blk_off(j, t)
            kbs = [k_refs[p][0][pl.ds(off, t), :] for p in range(2)]
            vbs = [v_refs[p][pl.ds(off, t), :] for p in range(2)]
            logs = [_sb_logs(qm[h], kbs[h // 2], valid) for h in range(N_HEADS)]
            tails = _tri_sums([lg[0] for lg in logs], tri)
            ws, gs = [], []
            for h in range(N_HEADS):
                lk, ls = logs[h]
                w = jnp.exp(ls + tails[h] + _col(rmb, (rm.shape[1] // N_HEADS) * h + j))
                if valid is not None:
                    w = jnp.where(valid, w, 0.0)
                ws.append(w)
                gs.append(_dot_nt(dom[h], vbs[h // 2]) * w)
            prefix = _tri_sums(gs, pre)
            new_dq, new_left = [], []
            dk_acc = [jnp.zeros((t, PAIR), F32) for _ in range(2)]
            dv_acc = [jnp.zeros((t, PAIR), F32) for _ in range(2)]
            for h in range(N_HEADS):
                lk, ls = logs[h]
                sig = jnp.exp(ls)
                dz = gs[h] * (1.0 - sig) - sig * (prefix[h] + lefts[h])
                if valid is not None:
                    dz = jnp.where(valid, dz, 0.0)
                dzb = dz.astype(MXU_DTYPE)
                dv_acc[h // 2] = dv_acc[h // 2] + _dot_tn(ws[h].astype(MXU_DTYPE), dom[h])
                dk_acc[h // 2] = dk_acc[h // 2] + _dot_tn(dzb, qm[h])
                new_dq.append(dqs[h] + _dot(dzb, kbs[h // 2]))
                new_left.append(lefts[h] + jnp.sum(gs[h], axis=1, keepdims=True))
            for p in range(2):
                dv_ref[pl.ds(off, t), p * PAIR:(p + 1) * PAIR] += dv_acc[p]
                dk_ref[pl.ds(off, t), p * PAIR:(p + 1) * PAIR] += dk_acc[p]
            return tuple(new_dq), tuple(new_left)

        carry = (tuple(jnp.zeros((tq, PAIR), F32) for _ in range(N_HEADS)),
                 tuple(jnp.zeros((tq, 1), F32) for _ in range(N_HEADS)))
        carry = lax.fori_loop(band * i - cnt_ref[i], band * i, lambda j, c: block(j, c, None), carry)
        for d in range(band):
            carry = block(band * i + d, carry, _sb_valid(d, tq, t))
        for p in range(2):
            dq_ref[:, p * PAIR:(p + 1) * PAIR] = _take_heads([carry[0][2 * p + e] * scale for e in range(2)], p)

    mspec = pl.BlockSpec((t, t), lambda i: (0, 0))
    acc_spec = pl.BlockSpec((s, GROUP_W), lambda i: (0, 0), pipeline_mode=pl.Buffered(1))
    return pl.pallas_call(
        kern, name=name, grid=(s // tq,),
        in_specs=pp.q_specs + pp.k_specs + pp.v_specs + [_row_spec(tq, do_blk, GROUP_W), _row_spec(tq, 0, rm.shape[1]), mspec, mspec,
                                                         pl.BlockSpec(memory_space=pltpu.SMEM)],
        out_specs=[_row_spec(tq, 0, GROUP_W), acc_spec, acc_spec],
        out_shape=[jax.ShapeDtypeStruct((s, GROUP_W), F32)] * 3,
        compiler_params=_att_params(False),
    )(*(pp.q_args + pp.k_args + pp.v_args + [do, rm, _tri(t, "row_gt_col"), _tri(t, "row_lt_col"), visited]))


def _split_w_in(w):
    col = lambda n: w[:, _OFF[n]:_OFF[n + 1]]
    wa = jnp.concatenate([col(0), col(1), col(2), col(4), col(5), col(6), col(10)], axis=1)
    misc = jnp.concatenate([col(3), col(9), jnp.zeros((w.shape[0], 128 - 4 - MLA_ROPE), w.dtype)], axis=1)
    wb = jnp.concatenate([col(11), col(7), col(8), misc], axis=1)
    return wa, wb


def _merge_dw_in(dwp):
    a = lambda n: dwp[:, n * GROUP_W:(n + 1) * GROUP_W]
    b0 = PA_COLS
    gate = dwp[:, b0:b0 + 1024]
    cq = dwp[:, b0 + 1024:b0 + 1280]
    ckv = dwp[:, b0 + 1280:b0 + 1408]
    flog = dwp[:, b0 + 1408:b0 + 1412]
    krot = dwp[:, b0 + 1408 + MISC_KROT:b0 + 1408 + MISC_KROT + MLA_ROPE]
    return jnp.concatenate([a(0), a(1), a(2), flog, a(3), a(4), a(5), cq, ckv, krot, a(6), gate], axis=1)


def _heads_first(w, per_head, first):
    r = w.shape[0]
    w3 = w.reshape(r, N_HEADS, per_head)
    return jnp.concatenate([w3[:, :, :first].reshape(r, -1), w3[:, :, first:].reshape(r, -1)], axis=1)


def _heads_interleaved(w, per_head, first):
    r = w.shape[0]
    a = w[:, :N_HEADS * first].reshape(r, N_HEADS, first)
    b = w[:, N_HEADS * first:].reshape(r, N_HEADS, per_head - first)
    return jnp.concatenate([a, b], axis=2).reshape(r, N_HEADS * per_head)


def _pad_rows8(a):
    return a[:, :8].T


def _local_step(x2, mem2, tgt, p):
    s = x2.shape[0]
    nm = mem2.shape[0]
    head_scale = HEAD_DIM ** -0.5
    mla_scale = (HEAD_DIM + MLA_ROPE) ** -0.5
    tabs = _rope_tables(s)
    mats = _rope_matrices()
    pairs = lambda arr, blk: [[(arr, blk + q, "pair")] for q in range(2)]
    vals = lambda arr, blk: [(arr, blk + q) for q in range(2)]

    h, hb = _ln_fwd(x2, None, p["ln_in_g"], p["ln_in_b"], "ln_in_fwd")
    _, memn_b = _ln_fwd(mem2, None, p["mem_ln_g"], p["mem_ln_b"], "ln_mem_fwd")

    saved = []
    for l in range(DEPTH):
        wa, wb = _split_w_in(p["w_in"][l])
        wp = jnp.concatenate([wa, wb], axis=1)
        wq_up = _heads_first(p["w_mla_q_up"][l], HEAD_DIM + MLA_ROPE, HEAD_DIM)
        wkv_up = _heads_first(p["w_mla_kv_up"][l], 2 * HEAD_DIM, HEAD_DIM)
        bias_row = jnp.pad(p["b_forget"][l], (0, 128 - N_HEADS)).reshape(1, 128)
        pa, pb, fox_kmax = _proj_in(hb, wa, wb, GROUP_W)

        fc = _forget_fwd(pb, bias_row)
        fbias = (fc, _pad_rows8(fc), fox_kmax)
        o_fox, lse_fox = _softmax_fwd(pairs(pa, 0), pairs(pa, 2), vals(pa, 4), s, s, head_scale, True, fbias,
                                      "fox_fwd")
        o_sb, *rm_sb = _sb_fwd(pa, 6, 8, 10, s, head_scale, "sb_fwd")

        qfull, cqn = _mla_q_fwd(pb, p["mla_q_norm_g"][l], wq_up, tabs, mats)
        kv, ckvn, kr4 = _mla_kv_fwd(pb, p["mla_kv_norm_g"][l], wkv_up, tabs, mats)
        mla_q = [[(qfull, q, "pair"), (qfull, 2, "quad")] for q in range(2)]
        mla_k = [[(kv, q, "pair"), (kr4, 0, "quad")] for q in range(2)]
        o_mla, lse_mla = _softmax_fwd(mla_q, mla_k, vals(kv, 2), s, s, mla_scale, True, None, "mla_fwd")

        mkv = _matmul(memn_b, p["w_mem_kv"][l], MXU_DTYPE, "mem_kv")
        o_mem, lse_mem = _softmax_fwd(pairs(pa, 12), pairs(mkv, 0), vals(mkv, 2), s, nm, head_scale, False, None,
                                      "mem_fwd")

        groups = (o_fox, o_sb, o_mla, o_mem)
        if l < DEPTH - 1:
            gated, u, h_next, hb_next = _gate_out_proj_ln(groups, pb, p["w_out"][l], h, p["ln_g"][l], p["ln_b"][l])
        else:
            gated, u, dh, sq_cols = _gate_out_proj_ln(groups, pb, p["w_out"][l], h, p["ln_g"][l], p["ln_b"][l], tgt)
        saved.append(dict(u=u, hb=hb, wp=wp, wq_up=wq_up, wkv_up=wkv_up, bias_row=bias_row, pa=pa, pb=pb,
                          fbias=fbias, lse_fox=lse_fox, rm_sb=rm_sb, cqn=cqn, ckvn=ckvn, mla_q=mla_q, mla_k=mla_k,
                          kv=kv, lse_mla=lse_mla, mkv=mkv, lse_mem=lse_mem, groups=groups, gated=gated))
        if l < DEPTH - 1:
            h, hb = h_next, hb_next

    loss_sum = jnp.sum(sq_cols)

    grads = {k: [None] * DEPTH for k in ("w_in", "b_forget", "mla_q_norm_g", "w_mla_q_up", "mla_kv_norm_g",
                                         "w_mla_kv_up", "w_mem_kv", "w_out", "ln_g", "ln_b")}
    dmemn = []
    dy1, dy2, c1 = dh, None, 1.0
    for l in reversed(range(DEPTH)):
        r = saved[l]
        pa, pb = r["pa"], r["pb"]
        o_fox, o_sb, o_mla, o_mem = r["groups"]
        du, du_b, dmixed, dgate_b, dg, db = _ln_bwd(r["u"], p["ln_g"][l], dy1, dy2, c1, "ln_gate_bwd",
                                                    (p["w_out"][l], r["groups"], pb))
        grads["ln_g"][l], grads["ln_b"][l] = dg[0], db[0]
        grads["w_out"][l] = _matmul(r["gated"], du_b, F32, "out_proj_dw", "tn")

        dfq, dfk, dfv, dfc_q, dfc_k = _softmax_bwd(pairs(pa, 0), pairs(pa, 2), vals(pa, 4), o_fox, r["lse_fox"],
                                                   dmixed, 0, s, s, head_scale, True, r["fbias"], "fox_bwd")
        dmisc_f, dbf = _forget_bwd(pb, r["bias_row"], dfc_q + jnp.pad(dfc_k.T, ((0, 0), (0, 128 - 8))))
        grads["b_forget"][l] = dbf[0, :N_HEADS]

        dsq, dsk, dsv = _sb_bwd(pa, 6, 8, 10, dmixed, 1, *r["rm_sb"], s, head_scale, "sb_bwd")

        dqm, dkm, dvm = _softmax_bwd(r["mla_q"], r["mla_k"], vals(r["kv"], 2), o_mla, r["lse_mla"], dmixed, 2,
                                     s, s, mla_scale, True, None, "mla_bwd")
        dq_mla_b, dcq_b, dgq = _mla_q_bwd(dqm, pb, p["mla_q_norm_g"][l], r["wq_up"], tabs, mats)
        grads["w_mla_q_up"][l] = _heads_interleaved(_matmul(r["cqn"], dq_mla_b, F32, "q_up_dw", "tn"),
                                                    HEAD_DIM + MLA_ROPE, HEAD_DIM)
        grads["mla_q_norm_g"][l] = dgq[0]
        dkv_b, dckv_b, dmisc_k, dgkv = _mla_kv_bwd(dkm, dvm, pb, p["mla_kv_norm_g"][l], r["wkv_up"], tabs, mats)
        grads["w_mla_kv_up"][l] = _heads_interleaved(_matmul(r["ckvn"], dkv_b, F32, "kv_up_dw", "tn"),
                                                     2 * HEAD_DIM, HEAD_DIM)
        grads["mla_kv_norm_g"][l] = dgkv[0]

        dmq, dmk, dmv = _softmax_bwd(pairs(pa, 12), pairs(r["mkv"], 0), vals(r["mkv"], 2), o_mem, r["lse_mem"],
                                     dmixed, 3, s, nm, head_scale, False, None, "mem_bwd")
        dmkv_b = jnp.concatenate([dmk, dmv], axis=1).astype(MXU_DTYPE)
        grads["w_mem_kv"][l] = _matmul(memn_b, dmkv_b, F32, "mem_kv_dw", "tn")
        dmemn.append(_matmul(dmkv_b, p["w_mem_kv"][l], F32, "mem_kv_dx", "nt"))

        dp = [dfq, dfk, dfv, dsq, dsk, dsv, dmq, dgate_b, dcq_b, dckv_b, (dmisc_f, dmisc_k)]
        dhproj = _proj_dx(dp, r["wp"])
        grads["w_in"][l] = _merge_dw_in(_proj_dw(r["hb"], dp))
        dy1, dy2, c1 = du, dhproj, ALPHA

    dx, dg_in, db_in = _ln_bwd(x2, p["ln_in_g"], dy1, dy2, c1, "ln_in_bwd")
    _, dg_mem, db_mem = _ln_bwd(mem2, p["mem_ln_g"], dmemn[0], dmemn[1], 1.0, "ln_mem_bwd")
    out = {k: jnp.stack(v) for k, v in grads.items()}
    out.update(ln_in_g=dg_in[0], ln_in_b=db_in[0], mem_ln_g=dg_mem[0], mem_ln_b=db_mem[0])
    return loss_sum, dx, out


WIDE = "w_in"
FLAT_NAMES = ("w_out", "w_mem_kv", "w_mla_q_up", "w_mla_kv_up")
FLAT_ROWS = 896
BIG_NAMES = (WIDE,) + FLAT_NAMES
BIG_AXIS = dict(w_in=2, w_out=1, w_mem_kv=1, w_mla_q_up=2, w_mla_kv_up=2)
SMALL_NAMES = ("ln_in_g", "ln_in_b", "mem_ln_g", "mem_ln_b", "ln_g", "ln_b", "b_forget", "mla_q_norm_g",
               "mla_kv_norm_g")
ALL_NAMES = ("ln_in_g", "ln_in_b", "mem_ln_g", "mem_ln_b", "w_in", "b_forget", "mla_q_norm_g", "w_mla_q_up",
             "mla_kv_norm_g", "w_mla_kv_up", "w_mem_kv", "w_out", "ln_g", "ln_b")
N_CHIPS = 4
N_DEV = 8


def _rows_of(shape):
    rows = -(-int(np.prod(shape)) // LANES)
    return -(-rows // PACK_ALIGN) * PACK_ALIGN


def _pack(arrs, rows):
    parts = []
    for a in arrs:
        f = a.reshape(-1)
        n = _rows_of(a.shape) * LANES
        parts.append(jnp.pad(f, (0, n - f.shape[0])).reshape(-1, LANES))
    used = sum(q.shape[0] for q in parts)
    if rows > used:
        parts.append(jnp.zeros((rows - used, LANES), parts[0].dtype))
    return jnp.concatenate(parts, axis=0)


def _unpack(buf, shapes):
    out, r = [], 0
    for shp in shapes:
        n = _rows_of(shp)
        out.append(buf[r:r + n].reshape(-1)[:int(np.prod(shp))].reshape(shp))
        r += n
    return out


def _sharded_pair(get):
    wide = get(WIDE)
    return [wide.reshape(-1, wide.shape[-1]), _pack([get(n) for n in FLAT_NAMES], FLAT_ROWS)]


HBM_SPEC = pl.BlockSpec(memory_space=pltpu.HBM)


def _gather_weights(shards):
    n = len(shards)

    def body(*refs):
        w_refs, out_refs, (send_sems, recv_sems, local_sems) = refs[:n], refs[n:2 * n], refs[2 * n:]
        x, y, c = (lax.axis_index(a) for a in MESH_AXES)
        me, sibling = 2 * x + y, (x, y, 1 - c)
        chips = [(1 - x, y), (x, 1 - y), (1 - x, 1 - y)]
        local, first, passed = [], [], []
        for a in range(n):
            w_ref, out_ref, half = w_refs[a], out_refs[a], shards[a].shape[0] // 2

            def part(chip, core, out_ref=out_ref, half=half):
                return out_ref.at[chip, pl.ds(core * half, half)]

            def copy(k, src, dst, to, a=a):
                return pltpu.make_async_remote_copy(
                    src_ref=src, dst_ref=dst, send_sem=send_sems.at[6 * a + k], recv_sem=recv_sems.at[6 * a + k],
                    device_id=to, device_id_type=pl.DeviceIdType.MESH)

            local.append(pltpu.make_async_copy(w_ref, out_ref.at[me], local_sems.at[a]))
            local[-1].start()
            mine = [copy(k, w_ref.at[pl.ds(c * half, half)], part(me, c), (px, py, c))
                    for k, (px, py) in enumerate(chips)]
            for cp in mine:
                cp.start()
            first.append((mine, part, copy))
        for mine, part, copy in first:
            for k, (px, py) in enumerate(chips):
                copy(k, part(me, c), part(2 * px + py, c), (px, py, c)).wait_recv()
                passed.append(copy(3 + k, part(2 * px + py, c), part(2 * px + py, c), sibling))
                passed[-1].start()
        for mine, part, copy in first:
            for k, (px, py) in enumerate(chips):
                copy(3 + k, part(me, c), part(2 * px + py, 1 - c), sibling).wait_recv()
        for cp in [cp for mine, _, _ in first for cp in mine] + passed:
            cp.wait_send()
        for cp in local:
            cp.wait()

    return pl.pallas_call(
        body, name="gather_weights",
        out_shape=[jax.ShapeDtypeStruct((N_CHIPS,) + s.shape, s.dtype) for s in shards],
        in_specs=[HBM_SPEC] * n, out_specs=[HBM_SPEC] * n,
        scratch_shapes=[pltpu.SemaphoreType.DMA((6 * n,)), pltpu.SemaphoreType.DMA((6 * n,)),
                        pltpu.SemaphoreType.DMA((n,))],
    )(*shards)


def _exchange_grads(bigs, small):
    nb = len(bigs)
    halves = [b.shape[1] // 2 for b in bigs]
    chunks = [_pick(h, (128, 64, 32, 16)) for h in halves]

    def body(*refs):
        big_refs, small_ref = refs[:nb], refs[nb]
        sum_refs, out_refs = refs[nb + 1:2 * nb + 1], refs[2 * nb + 1:3 * nb + 2]
        scratch = refs[3 * nb + 2:]
        land_bufs, sum_bufs = scratch[:nb], scratch[nb:2 * nb]
        send_sems, recv_sems, local_sems, load_sem, swap_send, swap_recv, keep_sems = scratch[2 * nb:]
        x, y, c = (lax.axis_index(a) for a in MESH_AXES)
        me, my_chip = 4 * x + 2 * y + c, 2 * x + y
        flips = [(fx, fy, fc) for fx in (0, 1) for fy in (0, 1) for fc in (0, 1) if fx or fy or fc]
        peers = [(1 - x if fx else x, 1 - y if fy else y, 1 - c if fc else c) for fx, fy, fc in flips]

        sources = [lambda chip, core, r=big_refs[a], half=bigs[a].shape[1] // 2: r.at[chip, pl.ds(core * half, half)]
                   for a in range(nb)] + [lambda chip, core: small_ref]

        def copy(a, k, src, slot, to):
            return pltpu.make_async_remote_copy(
                src_ref=src, dst_ref=out_refs[a].at[slot], send_sem=send_sems.at[7 * a + k],
                recv_sem=recv_sems.at[7 * a + k], device_id=to, device_id_type=pl.DeviceIdType.MESH)

        own = [pltpu.make_async_copy(src(my_chip, c), out_refs[a].at[me], local_sems.at[a])
               for a, src in enumerate(sources)]
        for cp in own:
            cp.start()
        sends = [copy(a, k, src(2 * px + py, pc), me, (px, py, pc))
                 for a, src in enumerate(sources) for k, (px, py, pc) in enumerate(peers)]
        for cp in sends:
            cp.start()
        for a, src in enumerate(sources):
            for k, (px, py, pc) in enumerate(peers):
                copy(a, k, src(my_chip, c), 4 * px + 2 * py + pc, (px, py, pc)).wait_recv()
        for cp in sends:
            cp.wait_send()
        for cp in own:
            cp.wait()

        tails = []
        for a in range(nb):
            for r0 in range(0, halves[a], chunks[a]):
                load = pltpu.make_async_copy(out_refs[a].at[:, pl.ds(r0, chunks[a])], land_bufs[a], load_sem)
                load.start()
                load.wait()
                total = land_bufs[a][0].astype(F32)
                for d in range(1, N_DEV):
                    total = total + land_bufs[a][d].astype(F32)
                sum_bufs[a][pl.ds(r0, chunks[a]), :] = total
            keep = pltpu.make_async_copy(sum_bufs[a], sum_refs[a].at[c], keep_sems.at[a])
            give = pltpu.make_async_remote_copy(
                src_ref=sum_bufs[a], dst_ref=sum_refs[a].at[c], send_sem=swap_send.at[a], recv_sem=swap_recv.at[a],
                device_id=(x, y, 1 - c), device_id_type=pl.DeviceIdType.MESH)
            keep.start()
            give.start()
            tails.append((keep, give))
        for a, (keep, give) in enumerate(tails):
            pltpu.make_async_remote_copy(
                src_ref=sum_bufs[a], dst_ref=sum_refs[a].at[1 - c], send_sem=swap_send.at[a], recv_sem=swap_recv.at[a],
                device_id=(x, y, 1 - c), device_id_type=pl.DeviceIdType.MESH).wait_recv()
            give.wait_send()
            keep.wait()

    return pl.pallas_call(
        body, name="exchange_grads",
        out_shape=[jax.ShapeDtypeStruct((2, h, b.shape[2]), F32) for h, b in zip(halves, bigs)]
        + [jax.ShapeDtypeStruct((N_DEV, h, b.shape[2]), b.dtype) for h, b in zip(halves, bigs)]
        + [jax.ShapeDtypeStruct((N_DEV,) + small.shape, small.dtype)],
        in_specs=[HBM_SPEC] * (nb + 1), out_specs=[HBM_SPEC] * (2 * nb + 1),
        scratch_shapes=[pltpu.VMEM((N_DEV, ch, b.shape[2]), b.dtype) for ch, b in zip(chunks, bigs)]
        + [pltpu.VMEM((h, b.shape[2]), F32) for h, b in zip(halves, bigs)]
        + [pltpu.SemaphoreType.DMA((7 * (nb + 1),)), pltpu.SemaphoreType.DMA((7 * (nb + 1),)),
           pltpu.SemaphoreType.DMA((nb + 1,)), pltpu.SemaphoreType.DMA, pltpu.SemaphoreType.DMA((nb,)),
           pltpu.SemaphoreType.DMA((nb,)), pltpu.SemaphoreType.DMA((nb,))],
        compiler_params=pltpu.CompilerParams(vmem_limit_bytes=VMEM_LIMIT),
    )(*bigs, small)


def _adamw(parts, w, m, v, name):
    rows, width = w.shape
    n_parts = parts.shape[0]
    tile = _pick(rows, (128, 16, 8))
    bc1 = 1.0 - ADAM_B1 ** ADAM_STEP
    bc2 = 1.0 - ADAM_B2 ** ADAM_STEP

    def kern(p_ref, w_ref, m_ref, v_ref, g_ref, d_ref, nm_ref, nv_ref):
        g = p_ref[0].astype(F32)
        for d in range(1, n_parts):
            g = g + p_ref[d].astype(F32)
        nm = ADAM_B1 * m_ref[...] + (1.0 - ADAM_B1) * g
        nv = ADAM_B2 * v_ref[...] + (1.0 - ADAM_B2) * (g * g)
        g_ref[...] = g
        nm_ref[...] = nm
        nv_ref[...] = nv
        d_ref[...] = -ADAM_LR * ((nm / bc1) / (jnp.sqrt(nv / bc2) + ADAM_EPS) + ADAM_WD * w_ref[...])

    spec = pl.BlockSpec((tile, width), lambda i: (i, 0))
    return pl.pallas_call(
        kern, name=name, grid=(rows // tile,),
        in_specs=[pl.BlockSpec((n_parts, tile, width), lambda i: (0, i, 0)), spec, spec, spec],
        out_specs=[spec] * 4, out_shape=[jax.ShapeDtypeStruct((rows, width), F32)] * 4,
        compiler_params=pltpu.CompilerParams(dimension_semantics=("parallel",), vmem_limit_bytes=VMEM_LIMIT),
    )(parts, w, m, v)


def kernel(x, mem, ln_in_g, ln_in_b, mem_ln_g, mem_ln_b, w_in, b_forget, mla_q_norm_g, w_mla_q_up, mla_kv_norm_g, w_mla_kv_up, w_mem_kv, w_out, ln_g, ln_b, loss_target, m_ln_in_g, m_ln_in_b, m_mem_ln_g, m_mem_ln_b, m_w_in, m_b_forget, m_mla_q_norm_g, m_w_mla_q_up, m_mla_kv_norm_g, m_w_mla_kv_up, m_w_mem_kv, m_w_out, m_ln_g, m_ln_b, v_ln_in_g, v_ln_in_b, v_mem_ln_g, v_mem_ln_b, v_w_in, v_b_forget, v_mla_q_norm_g, v_w_mla_q_up, v_mla_kv_norm_g, v_w_mla_kv_up, v_w_mem_kv, v_w_out, v_ln_g, v_ln_b):
    w = dict(ln_in_g=ln_in_g, ln_in_b=ln_in_b, mem_ln_g=mem_ln_g, mem_ln_b=mem_ln_b, w_in=w_in, b_forget=b_forget,
             mla_q_norm_g=mla_q_norm_g, w_mla_q_up=w_mla_q_up, mla_kv_norm_g=mla_kv_norm_g,
             w_mla_kv_up=w_mla_kv_up, w_mem_kv=w_mem_kv, w_out=w_out, ln_g=ln_g, ln_b=ln_b)
    mo = dict(ln_in_g=m_ln_in_g, ln_in_b=m_ln_in_b, mem_ln_g=m_mem_ln_g, mem_ln_b=m_mem_ln_b, w_in=m_w_in,
              b_forget=m_b_forget, mla_q_norm_g=m_mla_q_norm_g, w_mla_q_up=m_w_mla_q_up,
              mla_kv_norm_g=m_mla_kv_norm_g, w_mla_kv_up=m_w_mla_kv_up, w_mem_kv=m_w_mem_kv, w_out=m_w_out,
              ln_g=m_ln_g, ln_b=m_ln_b)
    vo = dict(ln_in_g=v_ln_in_g, ln_in_b=v_ln_in_b, mem_ln_g=v_mem_ln_g, mem_ln_b=v_mem_ln_b, w_in=v_w_in,
              b_forget=v_b_forget, mla_q_norm_g=v_mla_q_norm_g, w_mla_q_up=v_w_mla_q_up,
              mla_kv_norm_g=v_mla_kv_norm_g, w_mla_kv_up=v_w_mla_kv_up, w_mem_kv=v_w_mem_kv, w_out=v_w_out,
              ln_g=v_ln_g, ln_b=v_ln_b)
    flat_shapes = [w[n].shape for n in FLAT_NAMES]
    small_shapes = [w[n].shape for n in SMALL_NAMES]

    got_wide, got_flat = _gather_weights(_sharded_pair(lambda n: w[n].astype(MXU_DTYPE)))
    full = dict(w)
    full[WIDE] = jnp.concatenate([got_wide[j] for j in range(N_CHIPS)], axis=1).reshape(
        w[WIDE].shape[:2] + (N_CHIPS * w[WIDE].shape[2],))
    per_chip = [_unpack(got_flat[j], flat_shapes) for j in range(N_CHIPS)]
    for idx, n in enumerate(FLAT_NAMES):
        full[n] = jnp.concatenate([per_chip[j][idx] for j in range(N_CHIPS)], axis=BIG_AXIS[n])

    loss_sum, dx, g = _local_step(x[0], mem[0], loss_target[0], full)
    loss = lax.psum(loss_sum * (0.5 / D_MODEL), MESH_AXES)

    def shard_of(n, j):
        ax, size = BIG_AXIS[n], w[n].shape[BIG_AXIS[n]]
        return lax.slice_in_dim(g[n], j * size, (j + 1) * size, axis=ax).astype(MXU_DTYPE)

    per_dest = [_sharded_pair(lambda n, j=j: shard_of(n, j)) for j in range(N_CHIPS)]
    bigs = [jnp.stack([per_dest[j][a] for j in range(N_CHIPS)]) for a in range(2)]
    exchanged = _exchange_grads(bigs, _pack([g[n] for n in SMALL_NAMES], SMALL_ROWS))
    halves, small_parts = exchanged[:2], exchanged[-1]

    res = []
    for a, (grad, nm) in enumerate(zip(halves, ("adamw_wide", "adamw_flat"))):
        state = [_sharded_pair(lambda n, src=src: src[n])[a] for src in (w, mo, vo)]
        res.append(_adamw(grad.reshape((1,) + state[0].shape), *state, nm))
    res_small = _adamw(small_parts, *[_pack([src[n] for n in SMALL_NAMES], SMALL_ROWS) for src in (w, mo, vo)],
                       "adamw_replicated")
    outs = []
    for kind in range(4):
        vals = {WIDE: res[0][kind].reshape(w[WIDE].shape)}
        vals.update(zip(FLAT_NAMES, _unpack(res[1][kind], flat_shapes)))
        vals.update(zip(SMALL_NAMES, _unpack(res_small[kind], small_shapes)))
        outs += [vals[n] for n in ALL_NAMES]
    return (loss, dx[None], *outs)
```

```python
import functools

import numpy as np
import jax
import jax.numpy as jnp
from jax import lax
from jax.experimental import pallas as pl
from jax.experimental.pallas import tpu as pltpu

F32 = jnp.float32
MXU_DTYPE = jnp.bfloat16

DEPTH = 2
D_MODEL = 1024
GROUP_W = 256
N_HEADS = 4
HEAD_DIM = 64
MLA_Q_RANK = 256
MLA_KV_RANK = 128
MLA_ROPE = 32
MLA_Q_COLS = N_HEADS * (HEAD_DIM + MLA_ROPE)
MLA_KV_COLS = N_HEADS * 2 * HEAD_DIM
ROPE_THETA = 10000.0
LN_EPS = 1e-5
RMS_EPS = 1e-6
ALPHA = (2 * DEPTH) ** 0.25
ADAM_LR, ADAM_B1, ADAM_B2, ADAM_EPS, ADAM_WD, ADAM_STEP = 0.001, 0.9, 0.999, 1e-08, 0.01, 10

_SPLIT = (256, 256, 256, 4, 256, 256, 256, 256, 128, 32, 256, 1024)
_OFF = [int(o) for o in np.cumsum((0,) + _SPLIT)]
IN_COLS = _OFF[-1]
PA_COLS = 7 * GROUP_W
PB_COLS = 1024 + 256 + 128 + 128
PB_CQ_BLK, PB_CKV_BLK, PB_MISC_BLK = 4, 10, 11
MISC_KROT = 4

LANES = 1024
PACK_ALIGN = 16
SMALL_ROWS = 144
ROW_TILE = 256
PROJ_BWD_ROWS = 512
ATT_TILE = 256
SB_QUERY_TILE = 256
SOFTMAX_TILE = 512
SOFTMAX_FWD_TILE = 1024
DENSE_QUERY_TILE = 1024
FWD_PAIRS_PER_LOOP = 1
BWD_PAIRS_PER_LOOP = 2
PAIR = 128
SB_SLOT = PAIR // N_HEADS
VMEM_LIMIT = 56 * 1024 * 1024
MATMUL_VMEM = 30 * 1024 * 1024
NEG = -1e30
LOG2E = 1.4426950408889634
LN2 = 0.6931471805599453
EXP_UNDERFLOW = -104.0
DEAD_LOGIT = -110.0
REACH_SLACK = 1.0 + 2.0 ** -10
MESH_AXES = ("x", "y", "c")


def _dot(a, b):
    return jnp.dot(a, b, preferred_element_type=F32)


def _dot_nt(a, b):
    return lax.dot_general(a, b, (((1,), (1,)), ((), ())), preferred_element_type=F32)


def _dot_tn(a, b):
    return lax.dot_general(a, b, (((0,), (0,)), ((), ())), preferred_element_type=F32)


def _split2(x):
    hi = x.astype(MXU_DTYPE)
    lo = (x - hi.astype(F32)).astype(MXU_DTYPE)
    return hi, lo


def _split3(x):
    hi = x.astype(MXU_DTYPE)
    r = x - hi.astype(F32)
    mid = r.astype(MXU_DTYPE)
    lo = (r - mid.astype(F32)).astype(MXU_DTYPE)
    return hi, mid, lo


def _dot_exact_r(x, pm):
    hi, mid, lo = _split3(x)
    return _dot(hi, pm) + _dot(mid, pm) + _dot(lo, pm)


def _dot_exact_l(pm, x):
    hi, mid, lo = _split3(x)
    return _dot(pm, hi) + _dot(pm, mid) + _dot(pm, lo)


def _pick(dim, prefs):
    for p in prefs:
        if dim % p == 0:
            return p
    return dim


def _softplus(z):
    return jnp.maximum(z, 0.0) + jnp.log(1.0 + jnp.exp(-jnp.abs(z)))


def _tile_options(dim):
    opts = [d for d in range(128, min(dim, 2048) + 1, 128) if dim % d == 0]
    return opts or [dim]


def _matmul_tiles(m, n, k, out_bytes):
    tk = k if k <= 4096 else _pick(k, (1024, 512, 256, 128))
    best = None
    for tm in _tile_options(m):
        for tn in _tile_options(n):
            vmem = 2 * 2 * (tm * tk + tk * tn) + 4 * tm * tn + 2 * out_bytes * tm * tn
            if vmem <= MATMUL_VMEM and (best is None or tm * tn / (tm + tn) > best[0]):
                best = (tm * tn / (tm + tn), tm, tn)
    return best[1], best[2], tk


def _matmul(a, b, out_dtype, name, mode="nn"):
    m, k = (a.shape[1], a.shape[0]) if mode == "tn" else a.shape
    n = b.shape[0] if mode == "nt" else b.shape[1]
    tm, tn, tk = _matmul_tiles(m, n, k, jnp.dtype(out_dtype).itemsize)
    nk = k // tk
    dot = {"nn": _dot, "tn": _dot_tn, "nt": _dot_nt}[mode]

    def kern(a_ref, b_ref, o_ref, *acc):
        if nk == 1:
            o_ref[...] = dot(a_ref[...], b_ref[...]).astype(o_ref.dtype)
            return
        acc_ref, = acc
        kk = pl.program_id(2)

        @pl.when(kk == 0)
        def _():
            acc_ref[...] = jnp.zeros_like(acc_ref)

        acc_ref[...] += dot(a_ref[...], b_ref[...])

        @pl.when(kk == nk - 1)
        def _():
            o_ref[...] = acc_ref[...].astype(o_ref.dtype)

    a_spec = (pl.BlockSpec((tk, tm), lambda i, j, kk: (kk, i)) if mode == "tn"
              else pl.BlockSpec((tm, tk), lambda i, j, kk: (i, kk)))
    b_spec = (pl.BlockSpec((tn, tk), lambda i, j, kk: (j, kk)) if mode == "nt"
              else pl.BlockSpec((tk, tn), lambda i, j, kk: (kk, j)))
    return pl.pallas_call(
        kern, name=name, grid=(m // tm, n // tn, nk), in_specs=[a_spec, b_spec],
        out_specs=pl.BlockSpec((tm, tn), lambda i, j, kk: (i, j)),
        out_shape=jax.ShapeDtypeStruct((m, n), out_dtype),
        scratch_shapes=[pltpu.VMEM((tm, tn), F32)] if nk > 1 else [],
        compiler_params=pltpu.CompilerParams(
            dimension_semantics=("parallel", "parallel", "arbitrary"), vmem_limit_bytes=VMEM_LIMIT),
    )(a.astype(MXU_DTYPE), b.astype(MXU_DTYPE))


def _proj_in(hb, wa, wb, key_cols):
    rows, d = hb.shape
    tm = min(PROJ_BWD_ROWS, rows)

    def kern(a_ref, wa_ref, wb_ref, pa_ref, pb_ref, kmax_ref):
        @pl.when(pl.program_id(0) == 0)
        def _():
            kmax_ref[...] = jnp.zeros_like(kmax_ref)

        a = a_ref[...]
        pa = _dot(a, wa_ref[...]).astype(pa_ref.dtype)
        pa_ref[...] = pa
        pb_ref[...] = _dot(a, wb_ref[...])
        cols = {}
        for p in range(2):
            k32 = pa[:, key_cols + p * PAIR:key_cols + (p + 1) * PAIR].astype(F32)
            for e in range(2):
                sq = jnp.sum(jnp.where(_lane_mask("pair", e, tm), k32 * k32, 0.0), axis=1, keepdims=True)
                cols[2 * p + e] = jnp.max(sq, axis=0, keepdims=True)
        lane = lax.broadcasted_iota(jnp.int32, (1, PAIR), 1)
        kmax_ref[...] = jnp.maximum(kmax_ref[...], sum(jnp.where(lane == h, c, 0.0) for h, c in cols.items()))

    whole = lambda w: pl.BlockSpec(w.shape, lambda i: (0, 0), pipeline_mode=pl.Buffered(1))
    return pl.pallas_call(
        kern, name="proj_in", grid=(rows // tm,),
        in_specs=[pl.BlockSpec((tm, d), lambda i: (i, 0)), whole(wa), whole(wb)],
        out_specs=[pl.BlockSpec((tm, wa.shape[1]), lambda i: (i, 0)), pl.BlockSpec((tm, wb.shape[1]), lambda i: (i, 0)),
                   pl.BlockSpec((1, PAIR), lambda i: (0, 0))],
        out_shape=[jax.ShapeDtypeStruct((rows, wa.shape[1]), MXU_DTYPE), jax.ShapeDtypeStruct((rows, wb.shape[1]), F32),
                   jax.ShapeDtypeStruct((1, PAIR), F32)],
        compiler_params=pltpu.CompilerParams(dimension_semantics=("arbitrary",), vmem_limit_bytes=VMEM_LIMIT),
    )(hb, wa.astype(MXU_DTYPE), wb.astype(MXU_DTYPE))


def _piece_arrays(pieces):
    return [a for p in pieces for a in (p if isinstance(p, tuple) else (p,))]


def _join_pieces(refs, pieces):
    refs, cols = list(refs), []
    for p in pieces:
        vals = [refs.pop(0)[...] for _ in (p if isinstance(p, tuple) else (p,))]
        cols.append(functools.reduce(jnp.add, vals).astype(MXU_DTYPE))
    return jnp.concatenate(cols, axis=1)


def _proj_dx(pieces, w):
    arrs = _piece_arrays(pieces)
    rows, n = arrs[0].shape[0], w.shape[0]
    tm = min(PROJ_BWD_ROWS, rows)

    def kern(*refs):
        refs[-1][...] = _dot_nt(_join_pieces(refs[:len(arrs)], pieces), refs[len(arrs)][...])

    return pl.pallas_call(
        kern, name="proj_dx", grid=(rows // tm,),
        in_specs=[pl.BlockSpec((tm, a.shape[1]), lambda i: (i, 0)) for a in arrs]
        + [pl.BlockSpec(w.shape, lambda i: (0, 0), pipeline_mode=pl.Buffered(1))],
        out_specs=pl.BlockSpec((tm, n), lambda i: (i, 0)), out_shape=jax.ShapeDtypeStruct((rows, n), F32),
        compiler_params=pltpu.CompilerParams(dimension_semantics=("parallel",), vmem_limit_bytes=VMEM_LIMIT),
    )(*arrs, w.astype(MXU_DTYPE))


def _proj_dw(a, pieces):
    arrs = _piece_arrays(pieces)
    rows, m = a.shape
    k = sum(x.shape[1] for x in (p[0] if isinstance(p, tuple) else p for p in pieces))
    tk = min(PROJ_BWD_ROWS, rows)

    def kern(*refs):
        o_ref = refs[-1]

        @pl.when(pl.program_id(0) == 0)
        def _():
            o_ref[...] = jnp.zeros_like(o_ref)

        o_ref[...] += _dot_tn(refs[0][...], _join_pieces(refs[1:1 + len(arrs)], pieces))

    return pl.pallas_call(
        kern, name="proj_dw", grid=(rows // tk,),
        in_specs=[pl.BlockSpec((tk, m), lambda i: (i, 0))]
        + [pl.BlockSpec((tk, x.shape[1]), lambda i: (i, 0)) for x in arrs],
        out_specs=pl.BlockSpec((m, k), lambda i: (0, 0), pipeline_mode=pl.Buffered(1)),
        out_shape=jax.ShapeDtypeStruct((m, k), F32),
        compiler_params=pltpu.CompilerParams(dimension_semantics=("arbitrary",), vmem_limit_bytes=VMEM_LIMIT),
    )(a, *arrs)


def _rowwise(body, name, rows, tile, row_ins, full_ins, row_outs, acc_outs=(), scratch=(),
             reverse=False, sequential=False):
    n = rows // tile

    def ridx(i):
        return (n - 1 - i) if reverse else i

    in_specs, args = [], []
    for arr, width, cb in row_ins:
        in_specs.append(pl.BlockSpec((tile, width), lambda i, cb=cb: (ridx(i), cb)))
        args.append(arr)
    for arr in full_ins:
        in_specs.append(pl.BlockSpec(arr.shape, lambda i, nd=arr.ndim: (0,) * nd))
        args.append(arr)
    out_shape = [jax.ShapeDtypeStruct((rows, w), dt) for w, dt in row_outs]
    out_shape += [jax.ShapeDtypeStruct(s, dt) for s, dt in acc_outs]
    out_specs = [pl.BlockSpec((tile, w), lambda i: (ridx(i), 0)) for w, dt in row_outs]
    out_specs += [pl.BlockSpec(s, lambda i, nd=len(s): (0,) * nd) for s, dt in acc_outs]

    def kern(*refs):
        body(pl.program_id(0), *refs)

    sem = "arbitrary" if (acc_outs or sequential) else "parallel"
    return pl.pallas_call(
        kern, name=name, grid=(n,), in_specs=in_specs, out_specs=out_specs, out_shape=out_shape,
        scratch_shapes=list(scratch),
        compiler_params=pltpu.CompilerParams(dimension_semantics=(sem,), vmem_limit_bytes=VMEM_LIMIT),
    )(*args)


def _ln_stats(u):
    mu = jnp.mean(u, axis=-1, keepdims=True)
    xc = u - mu
    var = jnp.mean(xc * xc, axis=-1, keepdims=True)
    return xc, lax.rsqrt(var + LN_EPS)


def _ln_fwd(a, g, beta, name):
    rows, d = a.shape

    def body(i, a_ref, g_ref, be_ref, h_ref, hb_ref):
        xc, rstd = _ln_stats(a_ref[...])
        y = xc * rstd * g_ref[...] + be_ref[...]
        h_ref[...] = y
        hb_ref[...] = y.astype(hb_ref.dtype)

    return _rowwise(body, name, rows, min(ROW_TILE, rows), [(a, d, 0)],
                    [g.reshape(1, d), beta.reshape(1, d)], [(d, F32), (d, MXU_DTYPE)])


def _ln_bwd(u, g, dy1, dy2, c1, name, gate=None):
    rows, d = u.shape
    has_2 = dy2 is not None
    n_groups = len(gate[1]) if gate else 0

    def body(i, *refs):
        refs = list(refs)
        uu = refs.pop(0)[...]
        dy = c1 * refs.pop(0)[...]
        if has_2:
            dy = dy + refs.pop(0)[...]
        if gate:
            mixed = jnp.concatenate([refs.pop(0)[...] for _ in range(n_groups)], axis=1)
            gt = refs.pop(0)[...]
            g_ref, w_ref, du_ref, dub_ref, dm_ref, dgate_ref, dg_ref, db_ref = refs
        else:
            g_ref, du_ref, dg_ref, db_ref = refs

        @pl.when(i == 0)
        def _():
            dg_ref[...] = jnp.zeros_like(dg_ref)
            db_ref[...] = jnp.zeros_like(db_ref)

        xc, rstd = _ln_stats(uu)
        xhat = xc * rstd
        dxh = dy * g_ref[...]
        m1 = jnp.mean(dxh, axis=-1, keepdims=True)
        m2 = jnp.mean(dxh * xhat, axis=-1, keepdims=True)
        du = rstd * (dxh - m1 - xhat * m2)
        du_ref[...] = du
        dg_ref[...] += jnp.sum(dy * xhat, axis=0, keepdims=True)
        db_ref[...] += jnp.sum(dy, axis=0, keepdims=True)
        if gate:
            dub = du.astype(dub_ref.dtype)
            dub_ref[...] = dub
            dgated = _dot_nt(dub, w_ref[...])
            sig = 1.0 / (1.0 + jnp.exp(-gt))
            dm_ref[...] = dgated * (gt * sig)
            dgate_ref[...] = (dgated * mixed * (sig * (1.0 + gt * (1.0 - sig)))).astype(dgate_ref.dtype)

    row_ins = [(u, d, 0), (dy1, d, 0)] + ([(dy2, d, 0)] if has_2 else [])
    full_ins = [g.reshape(1, d)]
    row_outs = [(d, F32)]
    if gate:
        w_out, groups, pb = gate
        w = GROUP_W * n_groups
        row_ins += [(o, GROUP_W, 0) for o in groups] + [(pb, w, 0)]
        full_ins += [w_out.astype(MXU_DTYPE)]
        row_outs += [(d, MXU_DTYPE), (w, F32), (w, MXU_DTYPE)]
    return _rowwise(body, name, rows, min(ROW_TILE, rows), row_ins, full_ins, row_outs,
                    [((1, d), F32), ((1, d), F32)])


def _gate_out_proj_ln(groups, pb, w_out, h, g, beta, target=None):
    rows, d = h.shape
    last = target is not None
    w = GROUP_W * len(groups)

    def body(i, *refs):
        refs = list(refs)
        mixed = jnp.concatenate([refs.pop(0)[...] for _ in groups], axis=1)
        gate = refs.pop(0)[...]
        if last:
            h_ref, t_ref, w_ref, g_ref, be_ref, a_ref, u_ref, dh_ref, acc_ref = refs

            @pl.when(i == 0)
            def _():
                acc_ref[...] = jnp.zeros_like(acc_ref)
        else:
            h_ref, w_ref, g_ref, be_ref, a_ref, u_ref, o_ref, ob_ref = refs
        gated = (mixed * (gate / (1.0 + jnp.exp(-gate)))).astype(a_ref.dtype)
        a_ref[...] = gated
        u = ALPHA * h_ref[...] + _dot(gated, w_ref[...])
        u_ref[...] = u
        xc, rstd = _ln_stats(u)
        y = xc * rstd * g_ref[...] + be_ref[...]
        if last:
            e = y - t_ref[...]
            dh_ref[...] = e * (1.0 / d)
            acc_ref[...] += jnp.sum(e * e, axis=0, keepdims=True)
        else:
            o_ref[...] = y
            ob_ref[...] = y.astype(ob_ref.dtype)

    row_ins = [(o, GROUP_W, 0) for o in groups] + [(pb, w, 0), (h, d, 0)] + ([(target, d, 0)] if last else [])
    full_ins = [w_out.astype(MXU_DTYPE), g.reshape(1, d), beta.reshape(1, d)]
    if last:
        return _rowwise(body, "gate_out_proj_ln_loss", rows, PROJ_BWD_ROWS, row_ins, full_ins,
                        [(w, MXU_DTYPE), (d, F32), (d, F32)], [((1, d), F32)])
    return _rowwise(body, "gate_out_proj_ln", rows, PROJ_BWD_ROWS, row_ins, full_ins,
                    [(w, MXU_DTYPE), (d, F32), (d, F32), (d, MXU_DTYPE)])


def _tri(n, kind):
    r = np.arange(n)[:, None]
    c = np.arange(n)[None, :]
    m = {"lower_incl": r >= c, "upper_incl": r <= c, "row_gt_col": r > c, "row_lt_col": r < c}[kind]
    return jnp.asarray(m.astype(np.float32), dtype=MXU_DTYPE)


def _forget_fwd(pb, bias_row):
    rows = pb.shape[0]
    tile = _pick(rows, (1024, 512, 256))

    def body(i, x_ref, b_ref, l_ref, o_ref, carry_ref):
        @pl.when(i == 0)
        def _():
            carry_ref[...] = jnp.zeros_like(carry_ref)

        xx = x_ref[...] + b_ref[...]
        lane = lax.broadcasted_iota(jnp.int32, xx.shape, 1)
        lf = jnp.where(lane < N_HEADS, -_softplus(-xx), 0.0)
        o_ref[...] = _dot_exact_l(l_ref[...], lf) + carry_ref[...]
        carry_ref[...] += jnp.sum(lf, axis=0, keepdims=True)

    return _rowwise(body, "forget_fwd", rows, tile, [(pb, 128, PB_MISC_BLK)],
                    [bias_row, _tri(tile, "lower_incl")], [(128, F32)],
                    scratch=[pltpu.VMEM((1, 128), F32)], sequential=True)[0]


def _forget_bwd(pb, bias_row, dfc):
    rows = pb.shape[0]
    tile = _pick(rows, (1024, 512, 256))

    def body(i, x_ref, df_ref, b_ref, u_ref, o_ref, db_ref, carry_ref):
        @pl.when(i == 0)
        def _():
            carry_ref[...] = jnp.zeros_like(carry_ref)
            db_ref[...] = jnp.zeros_like(db_ref)

        df = df_ref[...]
        sfx = _dot_exact_l(u_ref[...], df) + carry_ref[...]
        carry_ref[...] += jnp.sum(df, axis=0, keepdims=True)
        xx = x_ref[...] + b_ref[...]
        lane = lax.broadcasted_iota(jnp.int32, xx.shape, 1)
        dl = jnp.where(lane < N_HEADS, sfx / (1.0 + jnp.exp(xx)), 0.0)
        o_ref[...] = dl
        db_ref[...] += jnp.sum(dl, axis=0, keepdims=True)

    return _rowwise(body, "forget_bwd", rows, tile,
                    [(pb, 128, PB_MISC_BLK), (dfc, 128, 0)],
                    [bias_row, _tri(tile, "upper_incl")], [(128, F32)], [((1, 128), F32)],
                    scratch=[pltpu.VMEM((1, 128), F32)], reverse=True, sequential=True)


def _rope_tables(s):
    half = MLA_ROPE // 2
    inv_freq = ROPE_THETA ** (-jnp.arange(half, dtype=F32) / half)
    ang = jnp.arange(s).astype(F32)[:, None] * inv_freq[None, :]
    cos2 = jnp.tile(jnp.cos(ang), (1, 2))
    sin2 = jnp.tile(jnp.sin(ang), (1, 2))
    cx = jnp.tile(cos2, (1, N_HEADS))
    sx = jnp.tile(sin2, (1, N_HEADS))
    pad = ((0, 0), (MISC_KROT, 128 - MISC_KROT - MLA_ROPE))
    ck = jnp.pad(cos2, pad)
    sk = jnp.pad(sin2, pad)
    return dict(ck=ck, sk=sk, cx=cx, sx=sx)


def _rot_matrix(width, bases):
    half = MLA_ROPE // 2
    p = np.zeros((width, width), np.float32)
    for b in bases:
        for i in range(half):
            p[b + half + i, b + i] = -1.0
            p[b + i, b + half + i] = 1.0
    return p


def _rope_matrices():
    pk = _rot_matrix(128, [MISC_KROT])
    p4 = _rot_matrix(128, [h * MLA_ROPE for h in range(N_HEADS)])
    a = np.zeros((128, 128), np.float32)
    for h in range(N_HEADS):
        for r in range(MLA_ROPE):
            a[h * MLA_ROPE + r, MISC_KROT + r] = 1.0
    cast = lambda m: jnp.asarray(m, dtype=MXU_DTYPE)
    return dict(p4=cast(p4), p4t=cast(p4.T), pk=cast(pk), spread=cast(a.T), xa=cast(a), xb=cast(p4.T @ a))


def _rms(c, g):
    r = lax.rsqrt(jnp.mean(c * c, axis=-1, keepdims=True) + RMS_EPS)
    return c * r * g


def _mla_q_fwd(pb, g, w_up, tabs, mats):
    rows = pb.shape[0]

    def body(i, c_ref, cos_ref, sin_ref, g_ref, w_ref, p_ref, q_ref, cn_ref):
        cn = _rms(c_ref[...], g_ref[...]).astype(cn_ref.dtype)
        cn_ref[...] = cn
        q = _dot(cn, w_ref[...])
        qr = q[:, GROUP_W:]
        q_ref[:, :GROUP_W] = q[:, :GROUP_W].astype(q_ref.dtype)
        q_ref[:, GROUP_W:] = (qr * cos_ref[...] + _dot_exact_r(qr, p_ref[...]) * sin_ref[...]).astype(q_ref.dtype)

    return _rowwise(body, "mla_q_fwd", rows, ROW_TILE,
                    [(pb, MLA_Q_RANK, PB_CQ_BLK), (tabs["cx"], PAIR, 0), (tabs["sx"], PAIR, 0)],
                    [g.reshape(1, MLA_Q_RANK), w_up.astype(MXU_DTYPE), mats["p4"]],
                    [(MLA_Q_COLS, MXU_DTYPE), (MLA_Q_RANK, MXU_DTYPE)])


def _mla_kv_fwd(pb, g, w_up, tabs, mats):
    rows = pb.shape[0]

    def body(i, c_ref, x_ref, cos_ref, sin_ref, g_ref, w_ref, p_ref, sp_ref, kv_ref, cn_ref, kr_ref):
        cn = _rms(c_ref[...], g_ref[...]).astype(cn_ref.dtype)
        cn_ref[...] = cn
        kv_ref[...] = _dot(cn, w_ref[...]).astype(kv_ref.dtype)
        xx = x_ref[...]
        kr = xx * cos_ref[...] + _dot_exact_r(xx, p_ref[...]) * sin_ref[...]
        kr_ref[...] = _dot_exact_r(kr, sp_ref[...]).astype(kr_ref.dtype)

    return _rowwise(body, "mla_kv_fwd", rows, ROW_TILE,
                    [(pb, MLA_KV_RANK, PB_CKV_BLK), (pb, 128, PB_MISC_BLK), (tabs["ck"], 128, 0), (tabs["sk"], 128, 0)],
                    [g.reshape(1, MLA_KV_RANK), w_up.astype(MXU_DTYPE), mats["pk"], mats["spread"]],
                    [(MLA_KV_COLS, MXU_DTYPE), (MLA_KV_RANK, MXU_DTYPE), (128, MXU_DTYPE)])


def _rms_bwd(c, g, dy):
    r = lax.rsqrt(jnp.mean(c * c, axis=-1, keepdims=True) + RMS_EPS)
    dyg = dy * g
    dc = r * dyg - c * (r * r * r) * jnp.mean(c * dyg, axis=-1, keepdims=True)
    return dc, jnp.sum(dy * c * r, axis=0, keepdims=True)


def _mla_q_bwd(dq_full, pb, g, w_up, tabs, mats):
    rows, nq = dq_full.shape

    def body(i, d_ref, c_ref, cos_ref, sin_ref, g_ref, w_ref, pt_ref, dq_ref, dc_ref, dg_ref):
        @pl.when(i == 0)
        def _():
            dg_ref[...] = jnp.zeros_like(dg_ref)

        dr = d_ref[:, GROUP_W:]
        dq = jnp.concatenate([d_ref[:, :GROUP_W], dr * cos_ref[...] + _dot_exact_r(dr * sin_ref[...], pt_ref[...])],
                             axis=1).astype(dq_ref.dtype)
        dq_ref[...] = dq
        dc, dg = _rms_bwd(c_ref[...], g_ref[...], _dot_nt(dq, w_ref[...]))
        dc_ref[...] = dc.astype(dc_ref.dtype)
        dg_ref[...] += dg

    return _rowwise(body, "mla_q_bwd", rows, ROW_TILE,
                    [(dq_full, nq, 0), (pb, MLA_Q_RANK, PB_CQ_BLK), (tabs["cx"], PAIR, 0), (tabs["sx"], PAIR, 0)],
                    [g.reshape(1, MLA_Q_RANK), w_up.astype(MXU_DTYPE), mats["p4t"]],
                    [(nq, MXU_DTYPE), (MLA_Q_RANK, MXU_DTYPE)], [((1, MLA_Q_RANK), F32)])


def _mla_kv_bwd(dk, dv, pb, g, w_up, tabs, mats):
    rows = dk.shape[0]

    def body(i, dk_ref, dr_ref, dv_ref, c_ref, cos_ref, sin_ref, g_ref, w_ref, a_ref, b_ref,
             dkv_ref, dc_ref, dm_ref, dg_ref):
        @pl.when(i == 0)
        def _():
            dg_ref[...] = jnp.zeros_like(dg_ref)

        dkv = jnp.concatenate([dk_ref[...], dv_ref[...]], axis=1).astype(dkv_ref.dtype)
        dkv_ref[...] = dkv
        dc, dg = _rms_bwd(c_ref[...], g_ref[...], _dot_nt(dkv, w_ref[...]))
        dc_ref[...] = dc.astype(dc_ref.dtype)
        dg_ref[...] += dg
        dr = dr_ref[...]
        dm_ref[...] = _dot_exact_r(dr * cos_ref[...], a_ref[...]) + _dot_exact_r(dr * sin_ref[...], b_ref[...])

    return _rowwise(body, "mla_kv_bwd", rows, ROW_TILE,
                    [(dk, GROUP_W, 0), (dk, PAIR, 2), (dv, GROUP_W, 0), (pb, MLA_KV_RANK, PB_CKV_BLK),
                     (tabs["cx"], PAIR, 0), (tabs["sx"], PAIR, 0)],
                    [g.reshape(1, MLA_KV_RANK), w_up.astype(MXU_DTYPE), mats["xa"], mats["xb"]],
                    [(MLA_KV_COLS, MXU_DTYPE), (MLA_KV_RANK, MXU_DTYPE), (PAIR, F32)], [((1, MLA_KV_RANK), F32)])


def _att_params(parallel):
    return pltpu.CompilerParams(dimension_semantics=("parallel" if parallel else "arbitrary",),
                                vmem_limit_bytes=VMEM_LIMIT)


def _blk_off(j, t):
    return j * t if isinstance(j, int) else pl.multiple_of(j * t, t)


def _causal_mask(t, strict):
    r = lax.broadcasted_iota(jnp.int32, (t, t), 0)
    c = lax.broadcasted_iota(jnp.int32, (t, t), 1)
    return (c < r) if strict else (c <= r)


def _lane_mask(kind, head, rows):
    lane = lax.broadcasted_iota(jnp.int32, (rows, PAIR), 1)
    if kind == "pair":
        return (lane < HEAD_DIM) if head % 2 == 0 else (lane >= HEAD_DIM)
    return (lane >= MLA_ROPE * head) & (lane < MLA_ROPE * (head + 1))


def _row_spec(t, cb, width=PAIR):
    return pl.BlockSpec((t, width), lambda i, cb=cb: (i, cb))


def _whole_spec(rows, cb, width=PAIR):
    return pl.BlockSpec((rows, width), lambda i, cb=cb: (0, cb), pipeline_mode=pl.Buffered(1))


def _is_pow2(x):
    return float(np.frexp(x)[0]) == 0.5


def _masked_heads(blocks, kinds, pair, dtype, scale=None):
    out = []
    for e in range(2):
        head = 2 * pair + e
        parts = [jnp.where(_lane_mask(k, head, b.shape[0]), b.astype(F32) * (1.0 if scale is None else scale),
                           0.0).astype(dtype)
                 for b, k in zip(blocks, kinds)]
        out.append(parts[0] if len(parts) == 1 else jnp.concatenate(parts, axis=1))
    return out


def _logit_reach(qh, kmax2, head):
    q32 = qh.astype(F32)
    return jnp.sqrt(jnp.sum(q32 * q32, axis=1, keepdims=True) * _col(kmax2, head)) * REACH_SLACK


def _forget_top(ft_ref, head, off):
    return jnp.max(-ft_ref[head:head + 1, pl.ds(off, PAIR)])


def _col(block, idx):
    lane = lax.broadcasted_iota(jnp.int32, block.shape, 1)
    return jnp.sum(jnp.where(lane == idx, block, 0.0), axis=1, keepdims=True)


def _scatter_cols(cols, t):
    lane = lax.broadcasted_iota(jnp.int32, (t, PAIR), 1)
    out = jnp.zeros((t, PAIR), F32)
    for idx, c in cols.items():
        out = out + jnp.where(lane == idx, c, 0.0)
    return out


def _take_heads(per_head, pair):
    return jnp.where(_lane_mask("pair", 0, per_head[0].shape[0]), per_head[0], per_head[1])


class _Parts:
    def __init__(self, q_parts, k_parts, v_parts, tq, sk):
        self.kinds = [[kind for _, _, kind in q_parts[p]] for p in range(2)]
        self.nparts = len(q_parts[0])
        self.q_specs = [_row_spec(tq, cb) for p in range(2) for _, cb, _ in q_parts[p]]
        self.q_args = [a for p in range(2) for a, _, _ in q_parts[p]]
        self.k_specs = [_whole_spec(sk, cb) for p in range(2) for _, cb, _ in k_parts[p]]
        self.k_args = [a for p in range(2) for a, _, _ in k_parts[p]]
        self.v_specs = [_whole_spec(sk, cb) for _, cb in v_parts]
        self.v_args = [a for a, _ in v_parts]
        self.width = PAIR * self.nparts

    def split(self, refs):
        n = self.nparts
        refs = list(refs)
        q = [refs[p * n:(p + 1) * n] for p in range(2)]
        k = [refs[2 * n + p * n:2 * n + (p + 1) * n] for p in range(2)]
        v = refs[4 * n:4 * n + 2]
        return q, k, v, refs[4 * n + 2:]

    def k_block(self, k_refs, off, t):
        blks = [r[pl.ds(off, t), :] for r in k_refs]
        return blks[0] if len(blks) == 1 else jnp.concatenate(blks, axis=1)


def _softmax_fwd(q_parts, k_parts, v_parts, sq, sk, scale, causal, bias, name):
    tq = min(SOFTMAX_FWD_TILE, sq) if causal else _pick(sq, (DENSE_QUERY_TILE, SOFTMAX_TILE))
    tk = tq if causal else min(SOFTMAX_TILE, sk)
    nkv = sk // tk
    pp = _Parts(q_parts, k_parts, v_parts, tq, sk)

    def kern(*refs):
        q_refs, k_refs, v_refs, rest = pp.split(refs)
        if bias is not None:
            fc_ref, ft_ref, kmax_ref, o_ref, lse_ref = rest
            fcb = fc_ref[...]
        else:
            o_ref, lse_ref = rest
        i = pl.program_id(0)
        fold = _is_pow2(scale)
        head_on = [jnp.where(_lane_mask("pair", e, tk), 1.0, 0.0).astype(MXU_DTYPE) for e in range(2)]
        head_off = [jnp.where(_lane_mask("pair", e, tk), 0.0, 1.0).astype(MXU_DTYPE) for e in range(2)]
        lse_cols = {}
        for first in range(0, 2, FWD_PAIRS_PER_LOOP):
            pairs = list(range(first, first + FWD_PAIRS_PER_LOOP))
            heads = [2 * p + e for p in pairs for e in range(2)]
            qm = {}
            for p in pairs:
                masked_q = _masked_heads([r[...] for r in q_refs[p]], pp.kinds[p], p, MXU_DTYPE, scale if fold else None)
                qm.update({2 * p + e: masked_q[e] for e in range(2)})
            if bias is not None:
                reach = {h: _logit_reach(qm[h], kmax_ref[...], h) for h in heads}

            def block(j, carry, masked, pairs=pairs, qm=qm):
                off = _blk_off(j, tk)
                out = []
                for p in pairs:
                    kb = pp.k_block(k_refs[p], off, tk)
                    vb = v_refs[p][pl.ds(off, tk), :]
                    for e in range(2):
                        h = 2 * p + e
                        m, acc = carry[len(out)]
                        s = _dot_nt(qm[h], kb) * (LOG2E if fold else scale * LOG2E)
                        if bias is not None:
                            s = s - ft_ref[h:h + 1, pl.ds(off, tk)] * LOG2E
                        if masked:
                            s = jnp.where(_causal_mask(tq, False), s, NEG)
                        m_new = jnp.maximum(m, jnp.max(s, axis=1, keepdims=True))
                        pr = jnp.exp2(s - m_new).astype(MXU_DTYPE)
                        out.append((m_new, jnp.exp2(m - m_new) * acc + _dot(pr, vb * head_on[e] + head_off[e])))
                return tuple(out)

            carry = tuple((jnp.full((tq, 1), NEG, F32), jnp.zeros((tq, PAIR), F32)) for _ in heads)
            if causal and bias is not None:
                def alive(c, j, heads=heads, reach=reach):
                    return functools.reduce(jnp.maximum, [
                        jnp.max(reach[h] + _forget_top(ft_ref, h, _blk_off(j, tk)) - c[n][0] * LN2)
                        for n, h in enumerate(heads)])

                def step(state, block=block, alive=alive):
                    n, _, c = state
                    c = block(i - 1 - n, c, False)
                    return n + 1, alive(c, i - 1 - n), c

                carry = block(i, carry, True)
                _, _, carry = lax.while_loop(lambda st: jnp.logical_and(st[0] < i, st[1] > DEAD_LOGIT), step,
                                             (jnp.int32(0), alive(carry, i), carry))
            elif causal:
                carry = lax.fori_loop(0, i, lambda j, c, block=block: block(j, c, False), carry)
                carry = block(i, carry, True)
            else:
                for j in range(nkv):
                    carry = block(j, carry, False)
            for n, p in enumerate(pairs):
                outs = []
                for e in range(2):
                    m, acc = carry[2 * n + e]
                    l = _col(acc, HEAD_DIM * (1 - e))
                    outs.append(acc / l)
                    lse_cols[2 * p + e] = m * LN2 + jnp.log(l) + (_col(fcb, 2 * p + e) if bias is not None else 0.0)
                o_ref[:, p * PAIR:(p + 1) * PAIR] = _take_heads(outs, p)
        lse_ref[...] = _scatter_cols(lse_cols, tq)

    in_specs = pp.q_specs + pp.k_specs + pp.v_specs
    args = pp.q_args + pp.k_args + pp.v_args
    if bias is not None:
        in_specs += [_row_spec(tq, 0), pl.BlockSpec((8, sk), lambda i: (0, 0), pipeline_mode=pl.Buffered(1)),
                     pl.BlockSpec((1, PAIR), lambda i: (0, 0))]
        args += list(bias)
    return pl.pallas_call(
        kern, name=name, grid=(sq // tq,), in_specs=in_specs,
        out_specs=[_row_spec(tq, 0, GROUP_W), _row_spec(tq, 0)],
        out_shape=[jax.ShapeDtypeStruct((sq, GROUP_W), F32), jax.ShapeDtypeStruct((sq, PAIR), F32)],
        compiler_params=_att_params(True),
    )(*args)


def _softmax_bwd(q_parts, k_parts, v_parts, o, lse, do, do_blk, sq, sk, scale, causal, bias, name):
    tq = min(SOFTMAX_TILE, sq) if causal else _pick(sq, (DENSE_QUERY_TILE, SOFTMAX_TILE))
    tk = tq if causal else min(SOFTMAX_TILE, sk)
    nkv = sk // tk
    pp = _Parts(q_parts, k_parts, v_parts, tq, sk)
    quad = pp.nparts == 2
    wq = GROUP_W + (PAIR if quad else 0)

    def kern(*refs):
        q_refs, k_refs, v_refs, rest = pp.split(refs)
        if bias is not None:
            o_ref, lse_ref, do_ref, fc_ref, ft_ref, kmax_ref, dq_ref, dk_ref, dv_ref, dfq_ref, dfk_ref = rest
            fcb = fc_ref[...]
        else:
            o_ref, lse_ref, do_ref, dq_ref, dk_ref, dv_ref = rest
        i = pl.program_id(0)
        fold = _is_pow2(scale)

        @pl.when(i == 0)
        def _():
            dk_ref[...] = jnp.zeros_like(dk_ref)
            dv_ref[...] = jnp.zeros_like(dv_ref)
            if bias is not None:
                dfk_ref[...] = jnp.zeros_like(dfk_ref)

        lse_b = lse_ref[...]
        qm, dom, delta, lse_h = [], [], [], []
        for p in range(2):
            qm += _masked_heads([r[...] for r in q_refs[p]], pp.kinds[p], p, MXU_DTYPE, scale if fold else None)
            do_p = do_ref[:, p * PAIR:(p + 1) * PAIR]
            dom += _masked_heads([do_p], ["pair"], p, MXU_DTYPE)
            prod = do_p * o_ref[:, p * PAIR:(p + 1) * PAIR]
            for e in range(2):
                h = 2 * p + e
                delta.append(jnp.sum(jnp.where(_lane_mask("pair", h, tq), prod, 0.0), axis=1, keepdims=True))
                lse_h.append(_col(lse_b, h) - (_col(fcb, h) if bias is not None else 0.0))

        def block(j, carry, masked, pairs):
            off = _blk_off(j, tk)
            out = []
            for p in pairs:
                kb = pp.k_block(k_refs[p], off, tk)
                vb = v_refs[p][pl.ds(off, tk), :]
                dk_acc = jnp.zeros((tk, pp.width), F32)
                dv_acc = jnp.zeros((tk, PAIR), F32)
                for e in range(2):
                    h = 2 * p + e
                    dq, dfq = carry[len(out)]
                    s = _dot_nt(qm[h], kb)
                    if not fold:
                        s = s * scale
                    if bias is not None:
                        s = s - ft_ref[h:h + 1, pl.ds(off, tk)]
                    if masked:
                        s = jnp.where(_causal_mask(tq, False), s, NEG)
                    pr = jnp.exp(s - lse_h[h])
                    ds = pr * (_dot_nt(dom[h], vb) - delta[h])
                    dsb = (ds if fold else ds * scale).astype(MXU_DTYPE)
                    dv_acc = dv_acc + _dot_tn(pr.astype(MXU_DTYPE), dom[h])
                    dk_acc = dk_acc + _dot_tn(dsb, qm[h])
                    dq = dq + _dot(dsb, kb)
                    if bias is not None:
                        dfq = dfq + jnp.sum(ds, axis=1, keepdims=True)
                        dfk_ref[h:h + 1, pl.ds(off, tk)] -= jnp.sum(ds, axis=0, keepdims=True)
                    out.append((dq, dfq))
                dv_ref[pl.ds(off, tk), p * PAIR:(p + 1) * PAIR] += dv_acc
                dk_ref[pl.ds(off, tk), p * PAIR:(p + 1) * PAIR] += dk_acc[:, :PAIR]
                if quad:
                    dk_ref[pl.ds(off, tk), GROUP_W:] += dk_acc[:, PAIR:]
            return tuple(out)

        done = {}
        for first in range(0, 2, BWD_PAIRS_PER_LOOP):
            pairs = list(range(first, first + BWD_PAIRS_PER_LOOP))
            heads = [2 * p + e for p in pairs for e in range(2)]
            carry = tuple((jnp.zeros((tq, pp.width), F32), jnp.zeros((tq, 1), F32)) for _ in heads)
            if causal and bias is not None:
                reach = {h: _logit_reach(qm[h], kmax_ref[...], h) - lse_h[h] for h in heads}

                def alive(j, heads=heads, reach=reach):
                    return functools.reduce(jnp.maximum, [
                        jnp.max(reach[h] + _forget_top(ft_ref, h, _blk_off(j, tk))) for h in heads])

                def step(state, pairs=pairs, alive=alive):
                    n, _, c = state
                    return n + 1, alive(i - 1 - n), block(i - 1 - n, c, False, pairs)

                carry = block(i, carry, True, pairs)
                _, _, carry = lax.while_loop(lambda st: jnp.logical_and(st[0] < i, st[1] > DEAD_LOGIT), step,
                                             (jnp.int32(0), alive(i), carry))
            elif causal:
                carry = lax.fori_loop(0, i, lambda j, c, pairs=pairs: block(j, c, False, pairs), carry)
                carry = block(i, carry, True, pairs)
            else:
                for j in range(nkv):
                    carry = block(j, carry, False, pairs)
            done.update(zip(heads, carry))
        carry = [done[h] for h in range(N_HEADS)]
        dqs = [c[0] * scale if fold else c[0] for c in carry]
        for p in range(2):
            dq_ref[:, p * PAIR:(p + 1) * PAIR] = _take_heads([dqs[2 * p + e][:, :PAIR] for e in range(2)], p)
        if quad:
            dq_ref[:, GROUP_W:] = sum(jnp.where(_lane_mask("quad", h, tq), dqs[h][:, PAIR:], 0.0)
                                      for h in range(N_HEADS))
        if bias is not None:
            dfq_ref[...] = _scatter_cols({h: carry[h][1] for h in range(N_HEADS)}, tq)

    acc_spec = lambda rows, width: pl.BlockSpec((rows, width), lambda i: (0, 0), pipeline_mode=pl.Buffered(1))
    in_specs = pp.q_specs + pp.k_specs + pp.v_specs + [_row_spec(tq, 0, GROUP_W), _row_spec(tq, 0),
                                                       _row_spec(tq, do_blk, GROUP_W)]
    args = pp.q_args + pp.k_args + pp.v_args + [o, lse, do]
    out_specs = [_row_spec(tq, 0, wq), acc_spec(sk, wq), acc_spec(sk, GROUP_W)]
    out_shape = [jax.ShapeDtypeStruct((sq, wq), F32), jax.ShapeDtypeStruct((sk, wq), F32),
                 jax.ShapeDtypeStruct((sk, GROUP_W), F32)]
    if bias is not None:
        in_specs += [_row_spec(tq, 0), pl.BlockSpec((8, sk), lambda i: (0, 0), pipeline_mode=pl.Buffered(1)),
                     pl.BlockSpec((1, PAIR), lambda i: (0, 0))]
        args += list(bias)
        out_specs += [_row_spec(tq, 0), acc_spec(8, sk)]
        out_shape += [jax.ShapeDtypeStruct((sq, PAIR), F32), jax.ShapeDtypeStruct((8, sk), F32)]
    return pl.pallas_call(
        kern, name=name, grid=(sq // tq,), in_specs=in_specs, out_specs=out_specs, out_shape=out_shape,
        compiler_params=_att_params(False),
    )(*args)


def _sb_logs(qh, kb, valid):
    z = _dot_nt(qh, kb)
    sp = _softplus(z)
    lk = -sp
    if valid is not None:
        lk = jnp.where(valid, lk, 0.0)
    return lk, z - sp


def _sb_valid(d, tq, tk):
    r = lax.broadcasted_iota(jnp.int32, (tq, tk), 0)
    c = lax.broadcasted_iota(jnp.int32, (tq, tk), 1)
    return c + d * tk < r


def _tri_sums(xs, tri):
    t = xs[0].shape[0]
    pieces = [_split2(x) for x in xs]
    hi = _dot(jnp.concatenate([pc[0] for pc in pieces], axis=0), tri)
    lo = _dot(jnp.concatenate([pc[1] for pc in pieces], axis=0), tri)
    return [hi[n * t:(n + 1) * t] + lo[n * t:(n + 1) * t] for n in range(len(xs))]


def _sb_fwd(src, q_blk, k_blk, v_blk, s, scale, name):
    assert _is_pow2(scale)
    tq, t = min(SB_QUERY_TILE, s), min(ATT_TILE, s)
    slots = -(-(s // t) // SB_SLOT) * SB_SLOT
    width = N_HEADS * slots
    band = tq // t
    pair = lambda blk: [[(src, blk + p, "pair")] for p in range(2)]
    pp = _Parts(pair(q_blk), pair(k_blk), [(src, v_blk + p) for p in range(2)], tq, s)

    def kern(*refs):
        q_refs, k_refs, v_refs, (tri_ref, o_ref, rm_ref, cnt_ref) = pp.split(refs)
        i = pl.program_id(0)
        tri = tri_ref[...]
        lane = lax.broadcasted_iota(jnp.int32, (tq, width), 1)
        qm = []
        for p in range(2):
            qm += _masked_heads([q_refs[p][0][...]], ["pair"], p, MXU_DTYPE, scale)

        def block(j, carry, valid):
            accs, rights, rm = carry
            off = _blk_off(j, t)
            kbs = [k_refs[p][0][pl.ds(off, t), :] for p in range(2)]
            vbs = [v_refs[p][pl.ds(off, t), :] for p in range(2)]
            logs = [_sb_logs(qm[h], kbs[h // 2], valid) for h in range(N_HEADS)]
            tails = _tri_sums([lg[0] for lg in logs], tri)
            new_acc, new_right = [], []
            for h in range(N_HEADS):
                lk, ls = logs[h]
                w = jnp.exp(ls + tails[h] + rights[h])
                if valid is not None:
                    w = jnp.where(valid, w, 0.0)
                new_acc.append(accs[h] + _dot(w.astype(MXU_DTYPE), vbs[h // 2]))
                rm = rm + jnp.where(lane == slots * h + j, rights[h], 0.0)
                new_right.append(rights[h] + jnp.sum(lk, axis=1, keepdims=True))
            return tuple(new_acc), tuple(new_right), rm

        carry = (tuple(jnp.zeros((tq, PAIR), F32) for _ in range(N_HEADS)),
                 tuple(jnp.zeros((tq, 1), F32) for _ in range(N_HEADS)), jnp.zeros((tq, width), F32))
        for d in reversed(range(band)):
            carry = block(band * i + d, carry, _sb_valid(d, tq, t))

        def alive(c):
            return functools.reduce(jnp.maximum, [jnp.max(r) for r in c[1]])

        def step(state):
            n, _, c = state
            c = block(band * i - 1 - n, c, None)
            return n + 1, alive(c), c

        n_done, _, carry = lax.while_loop(lambda st: jnp.logical_and(st[0] < band * i, st[1] > EXP_UNDERFLOW),
                                          step, (jnp.int32(0), alive(carry), carry))
        cnt_ref[i] = n_done
        for p in range(2):
            o_ref[:, p * PAIR:(p + 1) * PAIR] = _take_heads([carry[0][2 * p + e] for e in range(2)], p)
        rm_ref[...] = carry[2]

    return pl.pallas_call(
        kern, name=name, grid=(s // tq,),
        in_specs=pp.q_specs + pp.k_specs + pp.v_specs + [pl.BlockSpec((t, t), lambda i: (0, 0))],
        out_specs=[_row_spec(tq, 0, GROUP_W), _row_spec(tq, 0, width), pl.BlockSpec(memory_space=pltpu.SMEM)],
        out_shape=[jax.ShapeDtypeStruct((s, GROUP_W), F32), jax.ShapeDtypeStruct((s, width), F32),
                   jax.ShapeDtypeStruct((s // tq,), jnp.int32)],
        compiler_params=_att_params(False),
    )(*(pp.q_args + pp.k_args + pp.v_args + [_tri(t, "row_gt_col")]))


def _sb_bwd(src, q_blk, k_blk, v_blk, do, do_blk, rm, visited, s, scale, name):
    assert _is_pow2(scale)
    tq, t = min(SB_QUERY_TILE, s), min(ATT_TILE, s)
    band = tq // t
    pair = lambda blk: [[(src, blk + p, "pair")] for p in range(2)]
    pp = _Parts(pair(q_blk), pair(k_blk), [(src, v_blk + p) for p in range(2)], tq, s)

    def kern(*refs):
        q_refs, k_refs, v_refs, (do_ref, rm_ref, tri_ref, pre_ref, cnt_ref, dq_ref, dk_ref, dv_ref) = pp.split(refs)
        i = pl.program_id(0)

        @pl.when(i == 0)
        def _():
            dk_ref[...] = jnp.zeros_like(dk_ref)
            dv_ref[...] = jnp.zeros_like(dv_ref)

        rmb = rm_ref[...]
        tri = tri_ref[...]
        pre = pre_ref[...]
        qm, dom = [], []
        for p in range(2):
            qm += _masked_heads([q_refs[p][0][...]], ["pair"], p, MXU_DTYPE, scale)
            dom += _masked_heads([do_ref[:, p * PAIR:(p + 1) * PAIR]], ["pair"], p, MXU_DTYPE)

        def block(j, carry, valid):
            dqs, lefts = carry
            off = _blk_off(j, t)
            kbs = [k_refs[p][0][pl.ds(off, t), :] for p in range(2)]
            vbs = [v_refs[p][pl.ds(off, t), :] for p in range(2)]
            logs = [_sb_logs(qm[h], kbs[h // 2], valid) for h in range(N_HEADS)]
            tails = _tri_sums([lg[0] for lg in logs], tri)
            ws, gs = [], []
            for h in range(N_HEADS):
                lk, ls = logs[h]
                w = jnp.exp(ls + tails[h] + _col(rmb, (rm.shape[1] // N_HEADS) * h + j))
                if valid is not None:
                    w = jnp.where(valid, w, 0.0)
                ws.append(w)
                gs.append(_dot_nt(dom[h], vbs[h // 2]) * w)
            prefix = _tri_sums(gs, pre)
            new_dq, new_left = [], []
            dk_acc = [jnp.zeros((t, PAIR), F32) for _ in range(2)]
            dv_acc = [jnp.zeros((t, PAIR), F32) for _ in range(2)]
            for h in range(N_HEADS):
                lk, ls = logs[h]
                sig = jnp.exp(ls)
                dz = gs[h] * (1.0 - sig) - sig * (prefix[h] + lefts[h])
                if valid is not None:
                    dz = jnp.where(valid, dz, 0.0)
                dzb = dz.astype(MXU_DTYPE)
                dv_acc[h // 2] = dv_acc[h // 2] + _dot_tn(ws[h].astype(MXU_DTYPE), dom[h])
                dk_acc[h // 2] = dk_acc[h // 2] + _dot_tn(dzb, qm[h])
                new_dq.append(dqs[h] + _dot(dzb, kbs[h // 2]))
                new_left.append(lefts[h] + jnp.sum(gs[h], axis=1, keepdims=True))
            for p in range(2):
                dv_ref[pl.ds(off, t), p * PAIR:(p + 1) * PAIR] += dv_acc[p]
                dk_ref[pl.ds(off, t), p * PAIR:(p + 1) * PAIR] += dk_acc[p]
            return tuple(new_dq), tuple(new_left)

        carry = (tuple(jnp.zeros((tq, PAIR), F32) for _ in range(N_HEADS)),
                 tuple(jnp.zeros((tq, 1), F32) for _ in range(N_HEADS)))
        carry = lax.fori_loop(band * i - cnt_ref[i], band * i, lambda j, c: block(j, c, None), carry)
        for d in range(band):
            carry = block(band * i + d, carry, _sb_valid(d, tq, t))
        for p in range(2):
            dq_ref[:, p * PAIR:(p + 1) * PAIR] = _take_heads([carry[0][2 * p + e] * scale for e in range(2)], p)

    mspec = pl.BlockSpec((t, t), lambda i: (0, 0))
    acc_spec = pl.BlockSpec((s, GROUP_W), lambda i: (0, 0), pipeline_mode=pl.Buffered(1))
    return pl.pallas_call(
        kern, name=name, grid=(s // tq,),
        in_specs=pp.q_specs + pp.k_specs + pp.v_specs + [_row_spec(tq, do_blk, GROUP_W), _row_spec(tq, 0, rm.shape[1]), mspec, mspec,
                                                         pl.BlockSpec(memory_space=pltpu.SMEM)],
        out_specs=[_row_spec(tq, 0, GROUP_W), acc_spec, acc_spec],
        out_shape=[jax.ShapeDtypeStruct((s, GROUP_W), F32)] * 3,
        compiler_params=_att_params(False),
    )(*(pp.q_args + pp.k_args + pp.v_args + [do, rm, _tri(t, "row_gt_col"), _tri(t, "row_lt_col"), visited]))


def _split_w_in(w):
    col = lambda n: w[:, _OFF[n]:_OFF[n + 1]]
    wa = jnp.concatenate([col(0), col(1), col(2), col(4), col(5), col(6), col(10)], axis=1)
    misc = jnp.concatenate([col(3), col(9), jnp.zeros((w.shape[0], 128 - 4 - MLA_ROPE), w.dtype)], axis=1)
    wb = jnp.concatenate([col(11), col(7), col(8), misc], axis=1)
    return wa, wb


def _merge_dw_in(dwp):
    a = lambda n: dwp[:, n * GROUP_W:(n + 1) * GROUP_W]
    b0 = PA_COLS
    gate = dwp[:, b0:b0 + 1024]
    cq = dwp[:, b0 + 1024:b0 + 1280]
    ckv = dwp[:, b0 + 1280:b0 + 1408]
    flog = dwp[:, b0 + 1408:b0 + 1412]
    krot = dwp[:, b0 + 1408 + MISC_KROT:b0 + 1408 + MISC_KROT + MLA_ROPE]
    return jnp.concatenate([a(0), a(1), a(2), flog, a(3), a(4), a(5), cq, ckv, krot, a(6), gate], axis=1)


def _heads_first(w, per_head, first):
    r = w.shape[0]
    w3 = w.reshape(r, N_HEADS, per_head)
    return jnp.concatenate([w3[:, :, :first].reshape(r, -1), w3[:, :, first:].reshape(r, -1)], axis=1)


def _heads_interleaved(w, per_head, first):
    r = w.shape[0]
    a = w[:, :N_HEADS * first].reshape(r, N_HEADS, first)
    b = w[:, N_HEADS * first:].reshape(r, N_HEADS, per_head - first)
    return jnp.concatenate([a, b], axis=2).reshape(r, N_HEADS * per_head)


def _pad_rows8(a):
    return a[:, :8].T


def _local_step(x2, mem2, tgt, p):
    s = x2.shape[0]
    nm = mem2.shape[0]
    head_scale = HEAD_DIM ** -0.5
    mla_scale = (HEAD_DIM + MLA_ROPE) ** -0.5
    tabs = _rope_tables(s)
    mats = _rope_matrices()
    pairs = lambda arr, blk: [[(arr, blk + q, "pair")] for q in range(2)]
    vals = lambda arr, blk: [(arr, blk + q) for q in range(2)]

    h, hb = _ln_fwd(x2, p["ln_in_g"], p["ln_in_b"], "ln_in_fwd")
    _, memn_b = _ln_fwd(mem2, p["mem_ln_g"], p["mem_ln_b"], "ln_mem_fwd")

    saved = []
    for l in range(DEPTH):
        wa, wb = _split_w_in(p["w_in"][l])
        wp = jnp.concatenate([wa, wb], axis=1)
        wq_up = _heads_first(p["w_mla_q_up"][l], HEAD_DIM + MLA_ROPE, HEAD_DIM)
        wkv_up = _heads_first(p["w_mla_kv_up"][l], 2 * HEAD_DIM, HEAD_DIM)
        bias_row = jnp.pad(p["b_forget"][l], (0, 128 - N_HEADS)).reshape(1, 128)
        pa, pb, fox_kmax = _proj_in(hb, wa, wb, GROUP_W)

        fc = _forget_fwd(pb, bias_row)
        fbias = (fc, _pad_rows8(fc), fox_kmax)
        o_fox, lse_fox = _softmax_fwd(pairs(pa, 0), pairs(pa, 2), vals(pa, 4), s, s, head_scale, True, fbias,
                                      "fox_fwd")
        o_sb, *rm_sb = _sb_fwd(pa, 6, 8, 10, s, head_scale, "sb_fwd")

        qfull, cqn = _mla_q_fwd(pb, p["mla_q_norm_g"][l], wq_up, tabs, mats)
        kv, ckvn, kr4 = _mla_kv_fwd(pb, p["mla_kv_norm_g"][l], wkv_up, tabs, mats)
        mla_q = [[(qfull, q, "pair"), (qfull, 2, "quad")] for q in range(2)]
        mla_k = [[(kv, q, "pair"), (kr4, 0, "quad")] for q in range(2)]
        o_mla, lse_mla = _softmax_fwd(mla_q, mla_k, vals(kv, 2), s, s, mla_scale, True, None, "mla_fwd")

        mkv = _matmul(memn_b, p["w_mem_kv"][l], MXU_DTYPE, "mem_kv")
        o_mem, lse_mem = _softmax_fwd(pairs(pa, 12), pairs(mkv, 0), vals(mkv, 2), s, nm, head_scale, False, None,
                                      "mem_fwd")

        groups = (o_fox, o_sb, o_mla, o_mem)
        if l < DEPTH - 1:
            gated, u, h_next, hb_next = _gate_out_proj_ln(groups, pb, p["w_out"][l], h, p["ln_g"][l], p["ln_b"][l])
        else:
            gated, u, dh, sq_cols = _gate_out_proj_ln(groups, pb, p["w_out"][l], h, p["ln_g"][l], p["ln_b"][l], tgt)
        saved.append(dict(u=u, hb=hb, wp=wp, wq_up=wq_up, wkv_up=wkv_up, bias_row=bias_row, pa=pa, pb=pb,
                          fbias=fbias, lse_fox=lse_fox, rm_sb=rm_sb, cqn=cqn, ckvn=ckvn, mla_q=mla_q, mla_k=mla_k,
                          kv=kv, lse_mla=lse_mla, mkv=mkv, lse_mem=lse_mem, groups=groups, gated=gated))
        if l < DEPTH - 1:
            h, hb = h_next, hb_next

    loss_sum = jnp.sum(sq_cols)

    grads = {k: [None] * DEPTH for k in ("w_in", "b_forget", "mla_q_norm_g", "w_mla_q_up", "mla_kv_norm_g",
                                         "w_mla_kv_up", "w_mem_kv", "w_out", "ln_g", "ln_b")}
    dmemn = []
    dy1, dy2, c1 = dh, None, 1.0
    for l in reversed(range(DEPTH)):
        r = saved[l]
        pa, pb = r["pa"], r["pb"]
        o_fox, o_sb, o_mla, o_mem = r["groups"]
        du, du_b, dmixed, dgate_b, dg, db = _ln_bwd(r["u"], p["ln_g"][l], dy1, dy2, c1, "ln_gate_bwd",
                                                    (p["w_out"][l], r["groups"], pb))
        grads["ln_g"][l], grads["ln_b"][l] = dg[0], db[0]
        grads["w_out"][l] = _matmul(r["gated"], du_b, F32, "out_proj_dw", "tn")

        dfq, dfk, dfv, dfc_q, dfc_k = _softmax_bwd(pairs(pa, 0), pairs(pa, 2), vals(pa, 4), o_fox, r["lse_fox"],
                                                   dmixed, 0, s, s, head_scale, True, r["fbias"], "fox_bwd")
        dmisc_f, dbf = _forget_bwd(pb, r["bias_row"], dfc_q + jnp.pad(dfc_k.T, ((0, 0), (0, 128 - 8))))
        grads["b_forget"][l] = dbf[0, :N_HEADS]

        dsq, dsk, dsv = _sb_bwd(pa, 6, 8, 10, dmixed, 1, *r["rm_sb"], s, head_scale, "sb_bwd")

        dqm, dkm, dvm = _softmax_bwd(r["mla_q"], r["mla_k"], vals(r["kv"], 2), o_mla, r["lse_mla"], dmixed, 2,
                                     s, s, mla_scale, True, None, "mla_bwd")
        dq_mla_b, dcq_b, dgq = _mla_q_bwd(dqm, pb, p["mla_q_norm_g"][l], r["wq_up"], tabs, mats)
        grads["w_mla_q_up"][l] = _heads_interleaved(_matmul(r["cqn"], dq_mla_b, F32, "q_up_dw", "tn"),
                                                    HEAD_DIM + MLA_ROPE, HEAD_DIM)
        grads["mla_q_norm_g"][l] = dgq[0]
        dkv_b, dckv_b, dmisc_k, dgkv = _mla_kv_bwd(dkm, dvm, pb, p["mla_kv_norm_g"][l], r["wkv_up"], tabs, mats)
        grads["w_mla_kv_up"][l] = _heads_interleaved(_matmul(r["ckvn"], dkv_b, F32, "kv_up_dw", "tn"),
                                                     2 * HEAD_DIM, HEAD_DIM)
        grads["mla_kv_norm_g"][l] = dgkv[0]

        dmq, dmk, dmv = _softmax_bwd(pairs(pa, 12), pairs(r["mkv"], 0), vals(r["mkv"], 2), o_mem, r["lse_mem"],
                                     dmixed, 3, s, nm, head_scale, False, None, "mem_bwd")
        dmkv_b = jnp.concatenate([dmk, dmv], axis=1).astype(MXU_DTYPE)
        grads["w_mem_kv"][l] = _matmul(memn_b, dmkv_b, F32, "mem_kv_dw", "tn")
        dmemn.append(_matmul(dmkv_b, p["w_mem_kv"][l], F32, "mem_kv_dx", "nt"))

        dp = [dfq, dfk, dfv, dsq, dsk, dsv, dmq, dgate_b, dcq_b, dckv_b, (dmisc_f, dmisc_k)]
        dhproj = _proj_dx(dp, r["wp"])
        grads["w_in"][l] = _merge_dw_in(_proj_dw(r["hb"], dp))
        dy1, dy2, c1 = du, dhproj, ALPHA

    dx, dg_in, db_in = _ln_bwd(x2, p["ln_in_g"], dy1, dy2, c1, "ln_in_bwd")
    _, dg_mem, db_mem = _ln_bwd(mem2, p["mem_ln_g"], dmemn[0], dmemn[1], 1.0, "ln_mem_bwd")
    out = {k: jnp.stack(v) for k, v in grads.items()}
    out.update(ln_in_g=dg_in[0], ln_in_b=db_in[0], mem_ln_g=dg_mem[0], mem_ln_b=db_mem[0])
    return loss_sum, dx, out


WIDE = "w_in"
FLAT_NAMES = ("w_out", "w_mem_kv", "w_mla_q_up", "w_mla_kv_up")
FLAT_ROWS = 896
BIG_NAMES = (WIDE,) + FLAT_NAMES
BIG_AXIS = dict(w_in=2, w_out=1, w_mem_kv=1, w_mla_q_up=2, w_mla_kv_up=2)
SMALL_NAMES = ("ln_in_g", "ln_in_b", "mem_ln_g", "mem_ln_b", "ln_g", "ln_b", "b_forget", "mla_q_norm_g",
               "mla_kv_norm_g")
ALL_NAMES = ("ln_in_g", "ln_in_b", "mem_ln_g", "mem_ln_b", "w_in", "b_forget", "mla_q_norm_g", "w_mla_q_up",
             "mla_kv_norm_g", "w_mla_kv_up", "w_mem_kv", "w_out", "ln_g", "ln_b")
N_CHIPS = 4
N_DEV = 8


def _rows_of(shape):
    rows = -(-int(np.prod(shape)) // LANES)
    return -(-rows // PACK_ALIGN) * PACK_ALIGN


def _pack(arrs, rows):
    parts = []
    for a in arrs:
        f = a.reshape(-1)
        n = _rows_of(a.shape) * LANES
        parts.append(jnp.pad(f, (0, n - f.shape[0])).reshape(-1, LANES))
    used = sum(q.shape[0] for q in parts)
    if rows > used:
        parts.append(jnp.zeros((rows - used, LANES), parts[0].dtype))
    return jnp.concatenate(parts, axis=0)


def _unpack(buf, shapes):
    out, r = [], 0
    for shp in shapes:
        n = _rows_of(shp)
        out.append(buf[r:r + n].reshape(-1)[:int(np.prod(shp))].reshape(shp))
        r += n
    return out


def _sharded_pair(get):
    wide = get(WIDE)
    return [wide.reshape(-1, wide.shape[-1]), _pack([get(n) for n in FLAT_NAMES], FLAT_ROWS)]


HBM_SPEC = pl.BlockSpec(memory_space=pltpu.HBM)


def _gather_weights(shards):
    n = len(shards)

    def body(*refs):
        w_refs, out_refs, (send_sems, recv_sems, local_sems) = refs[:n], refs[n:2 * n], refs[2 * n:]
        x, y, c = (lax.axis_index(a) for a in MESH_AXES)
        me, sibling = 2 * x + y, (x, y, 1 - c)
        chips = [(1 - x, y), (x, 1 - y), (1 - x, 1 - y)]
        local, first, passed = [], [], []
        for a in range(n):
            w_ref, out_ref, half = w_refs[a], out_refs[a], shards[a].shape[0] // 2

            def part(chip, core, out_ref=out_ref, half=half):
                return out_ref.at[chip, pl.ds(core * half, half)]

            def copy(k, src, dst, to, a=a):
                return pltpu.make_async_remote_copy(
                    src_ref=src, dst_ref=dst, send_sem=send_sems.at[6 * a + k], recv_sem=recv_sems.at[6 * a + k],
                    device_id=to, device_id_type=pl.DeviceIdType.MESH)

            local.append(pltpu.make_async_copy(w_ref, out_ref.at[me], local_sems.at[a]))
            local[-1].start()
            mine = [copy(k, w_ref.at[pl.ds(c * half, half)], part(me, c), (px, py, c))
                    for k, (px, py) in enumerate(chips)]
            for cp in mine:
                cp.start()
            first.append((mine, part, copy))
        for mine, part, copy in first:
            for k, (px, py) in enumerate(chips):
                copy(k, part(me, c), part(2 * px + py, c), (px, py, c)).wait_recv()
                passed.append(copy(3 + k, part(2 * px + py, c), part(2 * px + py, c), sibling))
                passed[-1].start()
        for mine, part, copy in first:
            for k, (px, py) in enumerate(chips):
                copy(3 + k, part(me, c), part(2 * px + py, 1 - c), sibling).wait_recv()
        for cp in [cp for mine, _, _ in first for cp in mine] + passed:
            cp.wait_send()
        for cp in local:
            cp.wait()

    return pl.pallas_call(
        body, name="gather_weights",
        out_shape=[jax.ShapeDtypeStruct((N_CHIPS,) + s.shape, s.dtype) for s in shards],
        in_specs=[HBM_SPEC] * n, out_specs=[HBM_SPEC] * n,
        scratch_shapes=[pltpu.SemaphoreType.DMA((6 * n,)), pltpu.SemaphoreType.DMA((6 * n,)),
                        pltpu.SemaphoreType.DMA((n,))],
    )(*shards)


def _exchange_grads(bigs, small):
    nb = len(bigs)
    halves = [b.shape[1] // 2 for b in bigs]
    chunks = [_pick(h, (128, 64, 32, 16)) for h in halves]

    def body(*refs):
        big_refs, small_ref = refs[:nb], refs[nb]
        sum_refs, out_refs = refs[nb + 1:2 * nb + 1], refs[2 * nb + 1:3 * nb + 2]
        scratch = refs[3 * nb + 2:]
        land_bufs, sum_bufs = scratch[:nb], scratch[nb:2 * nb]
        send_sems, recv_sems, local_sems, load_sem, swap_send, swap_recv, keep_sems = scratch[2 * nb:]
        x, y, c = (lax.axis_index(a) for a in MESH_AXES)
        me, my_chip = 4 * x + 2 * y + c, 2 * x + y
        flips = [(fx, fy, fc) for fx in (0, 1) for fy in (0, 1) for fc in (0, 1) if fx or fy or fc]
        peers = [(1 - x if fx else x, 1 - y if fy else y, 1 - c if fc else c) for fx, fy, fc in flips]

        sources = [lambda chip, core, r=big_refs[a], half=bigs[a].shape[1] // 2: r.at[chip, pl.ds(core * half, half)]
                   for a in range(nb)] + [lambda chip, core: small_ref]

        def copy(a, k, src, slot, to):
            return pltpu.make_async_remote_copy(
                src_ref=src, dst_ref=out_refs[a].at[slot], send_sem=send_sems.at[7 * a + k],
                recv_sem=recv_sems.at[7 * a + k], device_id=to, device_id_type=pl.DeviceIdType.MESH)

        own = [pltpu.make_async_copy(src(my_chip, c), out_refs[a].at[me], local_sems.at[a])
               for a, src in enumerate(sources)]
        for cp in own:
            cp.start()
        sends = [copy(a, k, src(2 * px + py, pc), me, (px, py, pc))
                 for a, src in enumerate(sources) for k, (px, py, pc) in enumerate(peers)]
        for cp in sends:
            cp.start()
        for a, src in enumerate(sources):
            for k, (px, py, pc) in enumerate(peers):
                copy(a, k, src(my_chip, c), 4 * px + 2 * py + pc, (px, py, pc)).wait_recv()
        for cp in sends:
            cp.wait_send()
        for cp in own:
            cp.wait()

        tails = []
        for a in range(nb):
            for r0 in range(0, halves[a], chunks[a]):
                load = pltpu.make_async_copy(out_refs[a].at[:, pl.ds(r0, chunks[a])], land_bufs[a], load_sem)
                load.start()
                load.wait()
                total = land_bufs[a][0].astype(F32)
                for d in range(1, N_DEV):
                    total = total + land_bufs[a][d].astype(F32)
                sum_bufs[a][pl.ds(r0, chunks[a]), :] = total
            keep = pltpu.make_async_copy(sum_bufs[a], sum_refs[a].at[c], keep_sems.at[a])
            give = pltpu.make_async_remote_copy(
                src_ref=sum_bufs[a], dst_ref=sum_refs[a].at[c], send_sem=swap_send.at[a], recv_sem=swap_recv.at[a],
                device_id=(x, y, 1 - c), device_id_type=pl.DeviceIdType.MESH)
            keep.start()
            give.start()
            tails.append((keep, give))
        for a, (keep, give) in enumerate(tails):
            pltpu.make_async_remote_copy(
                src_ref=sum_bufs[a], dst_ref=sum_refs[a].at[1 - c], send_sem=swap_send.at[a], recv_sem=swap_recv.at[a],
                device_id=(x, y, 1 - c), device_id_type=pl.DeviceIdType.MESH).wait_recv()
            give.wait_send()
            keep.wait()

    return pl.pallas_call(
        body, name="exchange_grads",
        out_shape=[jax.ShapeDtypeStruct((2, h, b.shape[2]), F32) for h, b in zip(halves, bigs)]
        + [jax.ShapeDtypeStruct((N_DEV, h, b.shape[2]), b.dtype) for h, b in zip(halves, bigs)]
        + [jax.ShapeDtypeStruct((N_DEV,) + small.shape, small.dtype)],
        in_specs=[HBM_SPEC] * (nb + 1), out_specs=[HBM_SPEC] * (2 * nb + 1),
        scratch_shapes=[pltpu.VMEM((N_DEV, ch, b.shape[2]), b.dtype) for ch, b in zip(chunks, bigs)]
        + [pltpu.VMEM((h, b.shape[2]), F32) for h, b in zip(halves, bigs)]
        + [pltpu.SemaphoreType.DMA((7 * (nb + 1),)), pltpu.SemaphoreType.DMA((7 * (nb + 1),)),
           pltpu.SemaphoreType.DMA((nb + 1,)), pltpu.SemaphoreType.DMA, pltpu.SemaphoreType.DMA((nb,)),
           pltpu.SemaphoreType.DMA((nb,)), pltpu.SemaphoreType.DMA((nb,))],
        compiler_params=pltpu.CompilerParams(vmem_limit_bytes=VMEM_LIMIT),
    )(*bigs, small)


def _adamw(parts, w, m, v, name):
    rows, width = w.shape
    n_parts = parts.shape[0]
    tile = _pick(rows, (128, 16, 8))
    bc1 = 1.0 - ADAM_B1 ** ADAM_STEP
    bc2 = 1.0 - ADAM_B2 ** ADAM_STEP

    def kern(p_ref, w_ref, m_ref, v_ref, g_ref, d_ref, nm_ref, nv_ref):
        g = p_ref[0].astype(F32)
        for d in range(1, n_parts):
            g = g + p_ref[d].astype(F32)
        nm = ADAM_B1 * m_ref[...] + (1.0 - ADAM_B1) * g
        nv = ADAM_B2 * v_ref[...] + (1.0 - ADAM_B2) * (g * g)
        g_ref[...] = g
        nm_ref[...] = nm
        nv_ref[...] = nv
        d_ref[...] = -ADAM_LR * ((nm / bc1) / (jnp.sqrt(nv / bc2) + ADAM_EPS) + ADAM_WD * w_ref[...])

    spec = pl.BlockSpec((tile, width), lambda i: (i, 0))
    return pl.pallas_call(
        kern, name=name, grid=(rows // tile,),
        in_specs=[pl.BlockSpec((n_parts, tile, width), lambda i: (0, i, 0)), spec, spec, spec],
        out_specs=[spec] * 4, out_shape=[jax.ShapeDtypeStruct((rows, width), F32)] * 4,
        compiler_params=pltpu.CompilerParams(dimension_semantics=("parallel",), vmem_limit_bytes=VMEM_LIMIT),
    )(parts, w, m, v)


def kernel(x, mem, ln_in_g, ln_in_b, mem_ln_g, mem_ln_b, w_in, b_forget, mla_q_norm_g, w_mla_q_up, mla_kv_norm_g, w_mla_kv_up, w_mem_kv, w_out, ln_g, ln_b, loss_target, m_ln_in_g, m_ln_in_b, m_mem_ln_g, m_mem_ln_b, m_w_in, m_b_forget, m_mla_q_norm_g, m_w_mla_q_up, m_mla_kv_norm_g, m_w_mla_kv_up, m_w_mem_kv, m_w_out, m_ln_g, m_ln_b, v_ln_in_g, v_ln_in_b, v_mem_ln_g, v_mem_ln_b, v_w_in, v_b_forget, v_mla_q_norm_g, v_w_mla_q_up, v_mla_kv_norm_g, v_w_mla_kv_up, v_w_mem_kv, v_w_out, v_ln_g, v_ln_b):
    w = dict(ln_in_g=ln_in_g, ln_in_b=ln_in_b, mem_ln_g=mem_ln_g, mem_ln_b=mem_ln_b, w_in=w_in, b_forget=b_forget,
             mla_q_norm_g=mla_q_norm_g, w_mla_q_up=w_mla_q_up, mla_kv_norm_g=mla_kv_norm_g,
             w_mla_kv_up=w_mla_kv_up, w_mem_kv=w_mem_kv, w_out=w_out, ln_g=ln_g, ln_b=ln_b)
    mo = dict(ln_in_g=m_ln_in_g, ln_in_b=m_ln_in_b, mem_ln_g=m_mem_ln_g, mem_ln_b=m_mem_ln_b, w_in=m_w_in,
              b_forget=m_b_forget, mla_q_norm_g=m_mla_q_norm_g, w_mla_q_up=m_w_mla_q_up,
              mla_kv_norm_g=m_mla_kv_norm_g, w_mla_kv_up=m_w_mla_kv_up, w_mem_kv=m_w_mem_kv, w_out=m_w_out,
              ln_g=m_ln_g, ln_b=m_ln_b)
    vo = dict(ln_in_g=v_ln_in_g, ln_in_b=v_ln_in_b, mem_ln_g=v_mem_ln_g, mem_ln_b=v_mem_ln_b, w_in=v_w_in,
              b_forget=v_b_forget, mla_q_norm_g=v_mla_q_norm_g, w_mla_q_up=v_w_mla_q_up,
              mla_kv_norm_g=v_mla_kv_norm_g, w_mla_kv_up=v_w_mla_kv_up, w_mem_kv=v_w_mem_kv, w_out=v_w_out,
              ln_g=v_ln_g, ln_b=v_ln_b)
    flat_shapes = [w[n].shape for n in FLAT_NAMES]
    small_shapes = [w[n].shape for n in SMALL_NAMES]

    got_wide, got_flat = _gather_weights(_sharded_pair(lambda n: w[n].astype(MXU_DTYPE)))
    full = dict(w)
    full[WIDE] = jnp.concatenate([got_wide[j] for j in range(N_CHIPS)], axis=1).reshape(
        w[WIDE].shape[:2] + (N_CHIPS * w[WIDE].shape[2],))
    per_chip = [_unpack(got_flat[j], flat_shapes) for j in range(N_CHIPS)]
    for idx, n in enumerate(FLAT_NAMES):
        full[n] = jnp.concatenate([per_chip[j][idx] for j in range(N_CHIPS)], axis=BIG_AXIS[n])

    loss_sum, dx, g = _local_step(x[0], mem[0], loss_target[0], full)
    loss = lax.psum(loss_sum * (0.5 / D_MODEL), MESH_AXES)

    def shard_of(n, j):
        ax, size = BIG_AXIS[n], w[n].shape[BIG_AXIS[n]]
        return lax.slice_in_dim(g[n], j * size, (j + 1) * size, axis=ax).astype(MXU_DTYPE)

    per_dest = [_sharded_pair(lambda n, j=j: shard_of(n, j)) for j in range(N_CHIPS)]
    bigs = [jnp.stack([per_dest[j][a] for j in range(N_CHIPS)]) for a in range(2)]
    exchanged = _exchange_grads(bigs, _pack([g[n] for n in SMALL_NAMES], SMALL_ROWS))
    halves, small_parts = exchanged[:2], exchanged[-1]

    res = []
    for a, (grad, nm) in enumerate(zip(halves, ("adamw_wide", "adamw_flat"))):
        state = [_sharded_pair(lambda n, src=src: src[n])[a] for src in (w, mo, vo)]
        res.append(_adamw(grad.reshape((1,) + state[0].shape), *state, nm))
    res_small = _adamw(small_parts, *[_pack([src[n] for n in SMALL_NAMES], SMALL_ROWS) for src in (w, mo, vo)],
                       "adamw_replicated")
    outs = []
    for kind in range(4):
        vals = {WIDE: res[0][kind].reshape(w[WIDE].shape)}
        vals.update(zip(FLAT_NAMES, _unpack(res[1][kind], flat_shapes)))
        vals.update(zip(SMALL_NAMES, _unpack(res_small[kind], small_shapes)))
        outs += [vals[n] for n in ALL_NAMES]
    return (loss, dx[None], *outs)
```

```python
import functools

import numpy as np
import jax
import jax.numpy as jnp
from jax import lax
from jax.experimental import pallas as pl
from jax.experimental.pallas import tpu as pltpu

F32 = jnp.float32
MXU_DTYPE = jnp.bfloat16

DEPTH = 2
D_MODEL = 1024
GROUP_W = 256
N_HEADS = 4
HEAD_DIM = 64
MLA_Q_RANK = 256
MLA_KV_RANK = 128
MLA_ROPE = 32
MLA_Q_COLS = N_HEADS * (HEAD_DIM + MLA_ROPE)
MLA_KV_COLS = N_HEADS * 2 * HEAD_DIM
ROPE_THETA = 10000.0
LN_EPS = 1e-5
RMS_EPS = 1e-6
ALPHA = (2 * DEPTH) ** 0.25
ADAM_LR, ADAM_B1, ADAM_B2, ADAM_EPS, ADAM_WD, ADAM_STEP = 0.001, 0.9, 0.999, 1e-08, 0.01, 10

_SPLIT = (256, 256, 256, 4, 256, 256, 256, 256, 128, 32, 256, 1024)
_OFF = [int(o) for o in np.cumsum((0,) + _SPLIT)]
IN_COLS = _OFF[-1]
PA_COLS = 7 * GROUP_W
PB_COLS = 1024 + 256 + 128 + 128
PB_CQ_BLK, PB_CKV_BLK, PB_MISC_BLK = 4, 10, 11
MISC_KROT = 4

LANES = 1024
PACK_ALIGN = 16
SMALL_ROWS = 144
ROW_TILE = 256
PROJ_BWD_ROWS = 512
ATT_TILE = 256
SB_QUERY_TILE = 256
SOFTMAX_TILE = 512
SOFTMAX_FWD_TILE = 1024
DENSE_QUERY_TILE = 1024
FWD_PAIRS_PER_LOOP = 1
BWD_PAIRS_PER_LOOP = 2
PAIR = 128
SB_SLOT = PAIR // N_HEADS
VMEM_LIMIT = 56 * 1024 * 1024
MATMUL_VMEM = 30 * 1024 * 1024
NEG = -1e30
LOG2E = 1.4426950408889634
LN2 = 0.6931471805599453
EXP_UNDERFLOW = -104.0
DEAD_LOGIT = -110.0
REACH_SLACK = 1.0 + 2.0 ** -10
MESH_AXES = ("x", "y", "c")


def _dot(a, b):
    return jnp.dot(a, b, preferred_element_type=F32)


def _dot_nt(a, b):
    return lax.dot_general(a, b, (((1,), (1,)), ((), ())), preferred_element_type=F32)


def _dot_tn(a, b):
    return lax.dot_general(a, b, (((0,), (0,)), ((), ())), preferred_element_type=F32)


def _split2(x):
    hi = x.astype(MXU_DTYPE)
    lo = (x - hi.astype(F32)).astype(MXU_DTYPE)
    return hi, lo


def _split3(x):
    hi = x.astype(MXU_DTYPE)
    r = x - hi.astype(F32)
    mid = r.astype(MXU_DTYPE)
    lo = (r - mid.astype(F32)).astype(MXU_DTYPE)
    return hi, mid, lo


def _dot_exact_r(x, pm):
    hi, mid, lo = _split3(x)
    return _dot(hi, pm) + _dot(mid, pm) + _dot(lo, pm)


def _dot_exact_l(pm, x):
    hi, mid, lo = _split3(x)
    return _dot(pm, hi) + _dot(pm, mid) + _dot(pm, lo)


def _pick(dim, prefs):
    for p in prefs:
        if dim % p == 0:
            return p
    return dim


def _softplus(z):
    return jnp.maximum(z, 0.0) + jnp.log(1.0 + jnp.exp(-jnp.abs(z)))


def _tile_options(dim):
    opts = [d for d in range(128, min(dim, 2048) + 1, 128) if dim % d == 0]
    return opts or [dim]


def _matmul_tiles(m, n, k, out_bytes):
    tk = k if k <= 4096 else _pick(k, (1024, 512, 256, 128))
    best = None
    for tm in _tile_options(m):
        for tn in _tile_options(n):
            vmem = 2 * 2 * (tm * tk + tk * tn) + 4 * tm * tn + 2 * out_bytes * tm * tn
            if vmem <= MATMUL_VMEM and (best is None or tm * tn / (tm + tn) > best[0]):
                best = (tm * tn / (tm + tn), tm, tn)
    return best[1], best[2], tk


def _matmul(a, b, out_dtype, name, mode="nn"):
    m, k = (a.shape[1], a.shape[0]) if mode == "tn" else a.shape
    n = b.shape[0] if mode == "nt" else b.shape[1]
    tm, tn, tk = _matmul_tiles(m, n, k, jnp.dtype(out_dtype).itemsize)
    nk = k // tk
    dot = {"nn": _dot, "tn": _dot_tn, "nt": _dot_nt}[mode]

    def kern(a_ref, b_ref, o_ref, *acc):
        if nk == 1:
            o_ref[...] = dot(a_ref[...], b_ref[...]).astype(o_ref.dtype)
            return
        acc_ref, = acc
        kk = pl.program_id(2)

        @pl.when(kk == 0)
        def _():
            acc_ref[...] = jnp.zeros_like(acc_ref)

        acc_ref[...] += dot(a_ref[...], b_ref[...])

        @pl.when(kk == nk - 1)
        def _():
            o_ref[...] = acc_ref[...].astype(o_ref.dtype)

    a_spec = (pl.BlockSpec((tk, tm), lambda i, j, kk: (kk, i)) if mode == "tn"
              else pl.BlockSpec((tm, tk), lambda i, j, kk: (i, kk)))
    b_spec = (pl.BlockSpec((tn, tk), lambda i, j, kk: (j, kk)) if mode == "nt"
              else pl.BlockSpec((tk, tn), lambda i, j, kk: (kk, j)))
    return pl.pallas_call(
        kern, name=name, grid=(m // tm, n // tn, nk), in_specs=[a_spec, b_spec],
        out_specs=pl.BlockSpec((tm, tn), lambda i, j, kk: (i, j)),
        out_shape=jax.ShapeDtypeStruct((m, n), out_dtype),
        scratch_shapes=[pltpu.VMEM((tm, tn), F32)] if nk > 1 else [],
        compiler_params=pltpu.CompilerParams(
            dimension_semantics=("parallel", "parallel", "arbitrary"), vmem_limit_bytes=VMEM_LIMIT),
    )(a.astype(MXU_DTYPE), b.astype(MXU_DTYPE))


def _proj_in(hb, wa, wb, key_cols):
    rows, d = hb.shape
    tm = min(PROJ_BWD_ROWS, rows)

    def kern(a_ref, wa_ref, wb_ref, pa_ref, pb_ref, kmax_ref):
        @pl.when(pl.program_id(0) == 0)
        def _():
            kmax_ref[...] = jnp.zeros_like(kmax_ref)

        a = a_ref[...]
        pa = _dot(a, wa_ref[...]).astype(pa_ref.dtype)
        pa_ref[...] = pa
        pb_ref[...] = _dot(a, wb_ref[...])
        cols = {}
        for p in range(2):
            k32 = pa[:, key_cols + p * PAIR:key_cols + (p + 1) * PAIR].astype(F32)
            for e in range(2):
                sq = jnp.sum(jnp.where(_lane_mask("pair", e, tm), k32 * k32, 0.0), axis=1, keepdims=True)
                cols[2 * p + e] = jnp.max(sq, axis=0, keepdims=True)
        lane = lax.broadcasted_iota(jnp.int32, (1, PAIR), 1)
        kmax_ref[...] = jnp.maximum(kmax_ref[...], sum(jnp.where(lane == h, c, 0.0) for h, c in cols.items()))

    whole = lambda w: pl.BlockSpec(w.shape, lambda i: (0, 0), pipeline_mode=pl.Buffered(1))
    return pl.pallas_call(
        kern, name="proj_in", grid=(rows // tm,),
        in_specs=[pl.BlockSpec((tm, d), lambda i: (i, 0)), whole(wa), whole(wb)],
        out_specs=[pl.BlockSpec((tm, wa.shape[1]), lambda i: (i, 0)), pl.BlockSpec((tm, wb.shape[1]), lambda i: (i, 0)),
                   pl.BlockSpec((1, PAIR), lambda i: (0, 0))],
        out_shape=[jax.ShapeDtypeStruct((rows, wa.shape[1]), MXU_DTYPE), jax.ShapeDtypeStruct((rows, wb.shape[1]), F32),
                   jax.ShapeDtypeStruct((1, PAIR), F32)],
        compiler_params=pltpu.CompilerParams(dimension_semantics=("arbitrary",), vmem_limit_bytes=VMEM_LIMIT),
    )(hb, wa.astype(MXU_DTYPE), wb.astype(MXU_DTYPE))


def _piece_arrays(pieces):
    return [a for p in pieces for a in (p if isinstance(p, tuple) else (p,))]


def _join_pieces(refs, pieces):
    refs, cols = list(refs), []
    for p in pieces:
        vals = [refs.pop(0)[...] for _ in (p if isinstance(p, tuple) else (p,))]
        cols.append(functools.reduce(jnp.add, vals).astype(MXU_DTYPE))
    return jnp.concatenate(cols, axis=1)


def _proj_dx(pieces, w):
    arrs = _piece_arrays(pieces)
    rows, n = arrs[0].shape[0], w.shape[0]
    tm = min(PROJ_BWD_ROWS, rows)

    def kern(*refs):
        refs[-1][...] = _dot_nt(_join_pieces(refs[:len(arrs)], pieces), refs[len(arrs)][...])

    return pl.pallas_call(
        kern, name="proj_dx", grid=(rows // tm,),
        in_specs=[pl.BlockSpec((tm, a.shape[1]), lambda i: (i, 0)) for a in arrs]
        + [pl.BlockSpec(w.shape, lambda i: (0, 0), pipeline_mode=pl.Buffered(1))],
        out_specs=pl.BlockSpec((tm, n), lambda i: (i, 0)), out_shape=jax.ShapeDtypeStruct((rows, n), F32),
        compiler_params=pltpu.CompilerParams(dimension_semantics=("parallel",), vmem_limit_bytes=VMEM_LIMIT),
    )(*arrs, w.astype(MXU_DTYPE))


def _proj_dw(a, pieces):
    arrs = _piece_arrays(pieces)
    rows, m = a.shape
    k = sum(x.shape[1] for x in (p[0] if isinstance(p, tuple) else p for p in pieces))
    tk = min(PROJ_BWD_ROWS, rows)

    def kern(*refs):
        o_ref = refs[-1]

        @pl.when(pl.program_id(0) == 0)
        def _():
            o_ref[...] = jnp.zeros_like(o_ref)

        o_ref[...] += _dot_tn(refs[0][...], _join_pieces(refs[1:1 + len(arrs)], pieces))

    return pl.pallas_call(
        kern, name="proj_dw", grid=(rows // tk,),
        in_specs=[pl.BlockSpec((tk, m), lambda i: (i, 0))]
        + [pl.BlockSpec((tk, x.shape[1]), lambda i: (i, 0)) for x in arrs],
        out_specs=pl.BlockSpec((m, k), lambda i: (0, 0), pipeline_mode=pl.Buffered(1)),
        out_shape=jax.ShapeDtypeStruct((m, k), F32),
        compiler_params=pltpu.CompilerParams(dimension_semantics=("arbitrary",), vmem_limit_bytes=VMEM_LIMIT),
    )(a, *arrs)


def _rowwise(body, name, rows, tile, row_ins, full_ins, row_outs, acc_outs=(), scratch=(),
             reverse=False, sequential=False):
    n = rows // tile

    def ridx(i):
        return (n - 1 - i) if reverse else i

    in_specs, args = [], []
    for arr, width, cb in row_ins:
        in_specs.append(pl.BlockSpec((tile, width), lambda i, cb=cb: (ridx(i), cb)))
        args.append(arr)
    for arr in full_ins:
        in_specs.append(pl.BlockSpec(arr.shape, lambda i, nd=arr.ndim: (0,) * nd))
        args.append(arr)
    out_shape = [jax.ShapeDtypeStruct((rows, w), dt) for w, dt in row_outs]
    out_shape += [jax.ShapeDtypeStruct(s, dt) for s, dt in acc_outs]
    out_specs = [pl.BlockSpec((tile, w), lambda i: (ridx(i), 0)) for w, dt in row_outs]
    out_specs += [pl.BlockSpec(s, lambda i, nd=len(s): (0,) * nd) for s, dt in acc_outs]

    def kern(*refs):
        body(pl.program_id(0), *refs)

    sem = "arbitrary" if (acc_outs or sequential) else "parallel"
    return pl.pallas_call(
        kern, name=name, grid=(n,), in_specs=in_specs, out_specs=out_specs, out_shape=out_shape,
        scratch_shapes=list(scratch),
        compiler_params=pltpu.CompilerParams(dimension_semantics=(sem,), vmem_limit_bytes=VMEM_LIMIT),
    )(*args)


def _ln_stats(u):
    mu = jnp.mean(u, axis=-1, keepdims=True)
    xc = u - mu
    var = jnp.mean(xc * xc, axis=-1, keepdims=True)
    return xc, lax.rsqrt(var + LN_EPS)


def _ln_fwd(a, g, beta, name):
    rows, d = a.shape

    def body(i, a_ref, g_ref, be_ref, h_ref, hb_ref):
        xc, rstd = _ln_stats(a_ref[...])
        y = xc * rstd * g_ref[...] + be_ref[...]
        h_ref[...] = y
        hb_ref[...] = y.astype(hb_ref.dtype)

    return _rowwise(body, name, rows, min(ROW_TILE, rows), [(a, d, 0)],
                    [g.reshape(1, d), beta.reshape(1, d)], [(d, F32), (d, MXU_DTYPE)])


def _ln_bwd(u, g, dy1, dy2, c1, name, gate=None):
    rows, d = u.shape
    has_2 = dy2 is not None
    n_groups = len(gate[1]) if gate else 0

    def body(i, *refs):
        refs = list(refs)
        uu = refs.pop(0)[...]
        dy = c1 * refs.pop(0)[...]
        if has_2:
            dy = dy + refs.pop(0)[...]
        if gate:
            mixed = jnp.concatenate([refs.pop(0)[...] for _ in range(n_groups)], axis=1)
            gt = refs.pop(0)[...]
            g_ref, w_ref, du_ref, dub_ref, dm_ref, dgate_ref, dg_ref, db_ref = refs
        else:
            g_ref, du_ref, dg_ref, db_ref = refs

        @pl.when(i == 0)
        def _():
            dg_ref[...] = jnp.zeros_like(dg_ref)
            db_ref[...] = jnp.zeros_like(db_ref)

        xc, rstd = _ln_stats(uu)
        xhat = xc * rstd
        dxh = dy * g_ref[...]
        m1 = jnp.mean(dxh, axis=-1, keepdims=True)
        m2 = jnp.mean(dxh * xhat, axis=-1, keepdims=True)
        du = rstd * (dxh - m1 - xhat * m2)
        du_ref[...] = du
        dg_ref[...] += jnp.sum(dy * xhat, axis=0, keepdims=True)
        db_ref[...] += jnp.sum(dy, axis=0, keepdims=True)
        if gate:
            dub = du.astype(dub_ref.dtype)
            dub_ref[...] = dub
            dgated = _dot_nt(dub, w_ref[...])
            sig = 1.0 / (1.0 + jnp.exp(-gt))
            dm_ref[...] = dgated * (gt * sig)
            dgate_ref[...] = (dgated * mixed * (sig * (1.0 + gt * (1.0 - sig)))).astype(dgate_ref.dtype)

    row_ins = [(u, d, 0), (dy1, d, 0)] + ([(dy2, d, 0)] if has_2 else [])
    full_ins = [g.reshape(1, d)]
    row_outs = [(d, F32)]
    if gate:
        w_out, groups, pb = gate
        w = GROUP_W * n_groups
        row_ins += [(o, GROUP_W, 0) for o in groups] + [(pb, w, 0)]
        full_ins += [w_out.astype(MXU_DTYPE)]
        row_outs += [(d, MXU_DTYPE), (w, F32), (w, MXU_DTYPE)]
    return _rowwise(body, name, rows, min(PROJ_BWD_ROWS if gate else ROW_TILE, rows), row_ins, full_ins, row_outs,
                    [((1, d), F32), ((1, d), F32)])


def _gate_out_proj_ln(groups, pb, w_out, h, g, beta, target=None):
    rows, d = h.shape
    last = target is not None
    w = GROUP_W * len(groups)

    def body(i, *refs):
        refs = list(refs)
        mixed = jnp.concatenate([refs.pop(0)[...] for _ in groups], axis=1)
        gate = refs.pop(0)[...]
        if last:
            h_ref, t_ref, w_ref, g_ref, be_ref, a_ref, u_ref, dh_ref, acc_ref = refs

            @pl.when(i == 0)
            def _():
                acc_ref[...] = jnp.zeros_like(acc_ref)
        else:
            h_ref, w_ref, g_ref, be_ref, a_ref, u_ref, o_ref, ob_ref = refs
        gated = (mixed * (gate / (1.0 + jnp.exp(-gate)))).astype(a_ref.dtype)
        a_ref[...] = gated
        u = ALPHA * h_ref[...] + _dot(gated, w_ref[...])
        u_ref[...] = u
        xc, rstd = _ln_stats(u)
        y = xc * rstd * g_ref[...] + be_ref[...]
        if last:
            e = y - t_ref[...]
            dh_ref[...] = e * (1.0 / d)
            acc_ref[...] += jnp.sum(e * e, axis=0, keepdims=True)
        else:
            o_ref[...] = y
            ob_ref[...] = y.astype(ob_ref.dtype)

    row_ins = [(o, GROUP_W, 0) for o in groups] + [(pb, w, 0), (h, d, 0)] + ([(target, d, 0)] if last else [])
    full_ins = [w_out.astype(MXU_DTYPE), g.reshape(1, d), beta.reshape(1, d)]
    if last:
        return _rowwise(body, "gate_out_proj_ln_loss", rows, PROJ_BWD_ROWS, row_ins, full_ins,
                        [(w, MXU_DTYPE), (d, F32), (d, F32)], [((1, d), F32)])
    return _rowwise(body, "gate_out_proj_ln", rows, PROJ_BWD_ROWS, row_ins, full_ins,
                    [(w, MXU_DTYPE), (d, F32), (d, F32), (d, MXU_DTYPE)])


def _tri(n, kind):
    r = np.arange(n)[:, None]
    c = np.arange(n)[None, :]
    m = {"lower_incl": r >= c, "upper_incl": r <= c, "row_gt_col": r > c, "row_lt_col": r < c}[kind]
    return jnp.asarray(m.astype(np.float32), dtype=MXU_DTYPE)


def _forget_fwd(pb, bias_row):
    rows = pb.shape[0]
    tile = _pick(rows, (1024, 512, 256))

    def body(i, x_ref, b_ref, l_ref, o_ref, carry_ref):
        @pl.when(i == 0)
        def _():
            carry_ref[...] = jnp.zeros_like(carry_ref)

        xx = x_ref[...] + b_ref[...]
        lane = lax.broadcasted_iota(jnp.int32, xx.shape, 1)
        lf = jnp.where(lane < N_HEADS, -_softplus(-xx), 0.0)
        o_ref[...] = _dot_exact_l(l_ref[...], lf) + carry_ref[...]
        carry_ref[...] += jnp.sum(lf, axis=0, keepdims=True)

    return _rowwise(body, "forget_fwd", rows, tile, [(pb, 128, PB_MISC_BLK)],
                    [bias_row, _tri(tile, "lower_incl")], [(128, F32)],
                    scratch=[pltpu.VMEM((1, 128), F32)], sequential=True)[0]


def _forget_bwd(pb, bias_row, dfc):
    rows = pb.shape[0]
    tile = _pick(rows, (1024, 512, 256))

    def body(i, x_ref, df_ref, b_ref, u_ref, o_ref, db_ref, carry_ref):
        @pl.when(i == 0)
        def _():
            carry_ref[...] = jnp.zeros_like(carry_ref)
            db_ref[...] = jnp.zeros_like(db_ref)

        df = df_ref[...]
        sfx = _dot_exact_l(u_ref[...], df) + carry_ref[...]
        carry_ref[...] += jnp.sum(df, axis=0, keepdims=True)
        xx = x_ref[...] + b_ref[...]
        lane = lax.broadcasted_iota(jnp.int32, xx.shape, 1)
        dl = jnp.where(lane < N_HEADS, sfx / (1.0 + jnp.exp(xx)), 0.0)
        o_ref[...] = dl
        db_ref[...] += jnp.sum(dl, axis=0, keepdims=True)

    return _rowwise(body, "forget_bwd", rows, tile,
                    [(pb, 128, PB_MISC_BLK), (dfc, 128, 0)],
                    [bias_row, _tri(tile, "upper_incl")], [(128, F32)], [((1, 128), F32)],
                    scratch=[pltpu.VMEM((1, 128), F32)], reverse=True, sequential=True)


def _rope_tables(s):
    half = MLA_ROPE // 2
    inv_freq = ROPE_THETA ** (-jnp.arange(half, dtype=F32) / half)
    ang = jnp.arange(s).astype(F32)[:, None] * inv_freq[None, :]
    cos2 = jnp.tile(jnp.cos(ang), (1, 2))
    sin2 = jnp.tile(jnp.sin(ang), (1, 2))
    cx = jnp.tile(cos2, (1, N_HEADS))
    sx = jnp.tile(sin2, (1, N_HEADS))
    pad = ((0, 0), (MISC_KROT, 128 - MISC_KROT - MLA_ROPE))
    ck = jnp.pad(cos2, pad)
    sk = jnp.pad(sin2, pad)
    return dict(ck=ck, sk=sk, cx=cx, sx=sx)


def _rot_matrix(width, bases):
    half = MLA_ROPE // 2
    p = np.zeros((width, width), np.float32)
    for b in bases:
        for i in range(half):
            p[b + half + i, b + i] = -1.0
            p[b + i, b + half + i] = 1.0
    return p


def _rope_matrices():
    pk = _rot_matrix(128, [MISC_KROT])
    p4 = _rot_matrix(128, [h * MLA_ROPE for h in range(N_HEADS)])
    a = np.zeros((128, 128), np.float32)
    for h in range(N_HEADS):
        for r in range(MLA_ROPE):
            a[h * MLA_ROPE + r, MISC_KROT + r] = 1.0
    cast = lambda m: jnp.asarray(m, dtype=MXU_DTYPE)
    return dict(p4=cast(p4), p4t=cast(p4.T), pk=cast(pk), spread=cast(a.T), xa=cast(a), xb=cast(p4.T @ a))


def _rms(c, g):
    r = lax.rsqrt(jnp.mean(c * c, axis=-1, keepdims=True) + RMS_EPS)
    return c * r * g


def _mla_q_fwd(pb, g, w_up, tabs, mats):
    rows = pb.shape[0]

    def body(i, c_ref, cos_ref, sin_ref, g_ref, w_ref, p_ref, q_ref, cn_ref):
        cn = _rms(c_ref[...], g_ref[...]).astype(cn_ref.dtype)
        cn_ref[...] = cn
        q = _dot(cn, w_ref[...])
        qr = q[:, GROUP_W:]
        q_ref[:, :GROUP_W] = q[:, :GROUP_W].astype(q_ref.dtype)
        q_ref[:, GROUP_W:] = (qr * cos_ref[...] + _dot_exact_r(qr, p_ref[...]) * sin_ref[...]).astype(q_ref.dtype)

    return _rowwise(body, "mla_q_fwd", rows, ROW_TILE,
                    [(pb, MLA_Q_RANK, PB_CQ_BLK), (tabs["cx"], PAIR, 0), (tabs["sx"], PAIR, 0)],
                    [g.reshape(1, MLA_Q_RANK), w_up.astype(MXU_DTYPE), mats["p4"]],
                    [(MLA_Q_COLS, MXU_DTYPE), (MLA_Q_RANK, MXU_DTYPE)])


def _mla_kv_fwd(pb, g, w_up, tabs, mats):
    rows = pb.shape[0]

    def body(i, c_ref, x_ref, cos_ref, sin_ref, g_ref, w_ref, p_ref, sp_ref, kv_ref, cn_ref, kr_ref):
        cn = _rms(c_ref[...], g_ref[...]).astype(cn_ref.dtype)
        cn_ref[...] = cn
        kv_ref[...] = _dot(cn, w_ref[...]).astype(kv_ref.dtype)
        xx = x_ref[...]
        kr = xx * cos_ref[...] + _dot_exact_r(xx, p_ref[...]) * sin_ref[...]
        kr_ref[...] = _dot_exact_r(kr, sp_ref[...]).astype(kr_ref.dtype)

    return _rowwise(body, "mla_kv_fwd", rows, ROW_TILE,
                    [(pb, MLA_KV_RANK, PB_CKV_BLK), (pb, 128, PB_MISC_BLK), (tabs["ck"], 128, 0), (tabs["sk"], 128, 0)],
                    [g.reshape(1, MLA_KV_RANK), w_up.astype(MXU_DTYPE), mats["pk"], mats["spread"]],
                    [(MLA_KV_COLS, MXU_DTYPE), (MLA_KV_RANK, MXU_DTYPE), (128, MXU_DTYPE)])


def _rms_bwd(c, g, dy):
    r = lax.rsqrt(jnp.mean(c * c, axis=-1, keepdims=True) + RMS_EPS)
    dyg = dy * g
    dc = r * dyg - c * (r * r * r) * jnp.mean(c * dyg, axis=-1, keepdims=True)
    return dc, jnp.sum(dy * c * r, axis=0, keepdims=True)


def _mla_q_bwd(dq_full, pb, g, w_up, tabs, mats):
    rows, nq = dq_full.shape

    def body(i, d_ref, c_ref, cos_ref, sin_ref, g_ref, w_ref, pt_ref, dq_ref, dc_ref, dg_ref):
        @pl.when(i == 0)
        def _():
            dg_ref[...] = jnp.zeros_like(dg_ref)

        dr = d_ref[:, GROUP_W:]
        dq = jnp.concatenate([d_ref[:, :GROUP_W], dr * cos_ref[...] + _dot_exact_r(dr * sin_ref[...], pt_ref[...])],
                             axis=1).astype(dq_ref.dtype)
        dq_ref[...] = dq
        dc, dg = _rms_bwd(c_ref[...], g_ref[...], _dot_nt(dq, w_ref[...]))
        dc_ref[...] = dc.astype(dc_ref.dtype)
        dg_ref[...] += dg

    return _rowwise(body, "mla_q_bwd", rows, ROW_TILE,
                    [(dq_full, nq, 0), (pb, MLA_Q_RANK, PB_CQ_BLK), (tabs["cx"], PAIR, 0), (tabs["sx"], PAIR, 0)],
                    [g.reshape(1, MLA_Q_RANK), w_up.astype(MXU_DTYPE), mats["p4t"]],
                    [(nq, MXU_DTYPE), (MLA_Q_RANK, MXU_DTYPE)], [((1, MLA_Q_RANK), F32)])


def _mla_kv_bwd(dk, dv, pb, g, w_up, tabs, mats):
    rows = dk.shape[0]

    def body(i, dk_ref, dr_ref, dv_ref, c_ref, cos_ref, sin_ref, g_ref, w_ref, a_ref, b_ref,
             dkv_ref, dc_ref, dm_ref, dg_ref):
        @pl.when(i == 0)
        def _():
            dg_ref[...] = jnp.zeros_like(dg_ref)

        dkv = jnp.concatenate([dk_ref[...], dv_ref[...]], axis=1).astype(dkv_ref.dtype)
        dkv_ref[...] = dkv
        dc, dg = _rms_bwd(c_ref[...], g_ref[...], _dot_nt(dkv, w_ref[...]))
        dc_ref[...] = dc.astype(dc_ref.dtype)
        dg_ref[...] += dg
        dr = dr_ref[...]
        dm_ref[...] = _dot_exact_r(dr * cos_ref[...], a_ref[...]) + _dot_exact_r(dr * sin_ref[...], b_ref[...])

    return _rowwise(body, "mla_kv_bwd", rows, ROW_TILE,
                    [(dk, GROUP_W, 0), (dk, PAIR, 2), (dv, GROUP_W, 0), (pb, MLA_KV_RANK, PB_CKV_BLK),
                     (tabs["cx"], PAIR, 0), (tabs["sx"], PAIR, 0)],
                    [g.reshape(1, MLA_KV_RANK), w_up.astype(MXU_DTYPE), mats["xa"], mats["xb"]],
                    [(MLA_KV_COLS, MXU_DTYPE), (MLA_KV_RANK, MXU_DTYPE), (PAIR, F32)], [((1, MLA_KV_RANK), F32)])


def _att_params(parallel):
    return pltpu.CompilerParams(dimension_semantics=("parallel" if parallel else "arbitrary",),
                                vmem_limit_bytes=VMEM_LIMIT)


def _blk_off(j, t):
    return j * t if isinstance(j, int) else pl.multiple_of(j * t, t)


def _causal_mask(t, strict):
    r = lax.broadcasted_iota(jnp.int32, (t, t), 0)
    c = lax.broadcasted_iota(jnp.int32, (t, t), 1)
    return (c < r) if strict else (c <= r)


def _lane_mask(kind, head, rows):
    lane = lax.broadcasted_iota(jnp.int32, (rows, PAIR), 1)
    if kind == "pair":
        return (lane < HEAD_DIM) if head % 2 == 0 else (lane >= HEAD_DIM)
    return (lane >= MLA_ROPE * head) & (lane < MLA_ROPE * (head + 1))


def _row_spec(t, cb, width=PAIR):
    return pl.BlockSpec((t, width), lambda i, cb=cb: (i, cb))


def _whole_spec(rows, cb, width=PAIR):
    return pl.BlockSpec((rows, width), lambda i, cb=cb: (0, cb), pipeline_mode=pl.Buffered(1))


def _is_pow2(x):
    return float(np.frexp(x)[0]) == 0.5


def _masked_heads(blocks, kinds, pair, dtype, scale=None):
    out = []
    for e in range(2):
        head = 2 * pair + e
        parts = [jnp.where(_lane_mask(k, head, b.shape[0]), b.astype(F32) * (1.0 if scale is None else scale),
                           0.0).astype(dtype)
                 for b, k in zip(blocks, kinds)]
        out.append(parts[0] if len(parts) == 1 else jnp.concatenate(parts, axis=1))
    return out


def _logit_reach(qh, kmax2, head):
    q32 = qh.astype(F32)
    return jnp.sqrt(jnp.sum(q32 * q32, axis=1, keepdims=True) * _col(kmax2, head)) * REACH_SLACK


def _forget_top(ft_ref, head, off):
    return jnp.max(-ft_ref[head:head + 1, pl.ds(off, PAIR)])


def _col(block, idx):
    lane = lax.broadcasted_iota(jnp.int32, block.shape, 1)
    return jnp.sum(jnp.where(lane == idx, block, 0.0), axis=1, keepdims=True)


def _scatter_cols(cols, t):
    lane = lax.broadcasted_iota(jnp.int32, (t, PAIR), 1)
    out = jnp.zeros((t, PAIR), F32)
    for idx, c in cols.items():
        out = out + jnp.where(lane == idx, c, 0.0)
    return out


def _take_heads(per_head, pair):
    return jnp.where(_lane_mask("pair", 0, per_head[0].shape[0]), per_head[0], per_head[1])


class _Parts:
    def __init__(self, q_parts, k_parts, v_parts, tq, sk):
        self.kinds = [[kind for _, _, kind in q_parts[p]] for p in range(2)]
        self.nparts = len(q_parts[0])
        self.q_specs = [_row_spec(tq, cb) for p in range(2) for _, cb, _ in q_parts[p]]
        self.q_args = [a for p in range(2) for a, _, _ in q_parts[p]]
        self.k_specs = [_whole_spec(sk, cb) for p in range(2) for _, cb, _ in k_parts[p]]
        self.k_args = [a for p in range(2) for a, _, _ in k_parts[p]]
        self.v_specs = [_whole_spec(sk, cb) for _, cb in v_parts]
        self.v_args = [a for a, _ in v_parts]
        self.width = PAIR * self.nparts

    def split(self, refs):
        n = self.nparts
        refs = list(refs)
        q = [refs[p * n:(p + 1) * n] for p in range(2)]
        k = [refs[2 * n + p * n:2 * n + (p + 1) * n] for p in range(2)]
        v = refs[4 * n:4 * n + 2]
        return q, k, v, refs[4 * n + 2:]

    def k_block(self, k_refs, off, t):
        blks = [r[pl.ds(off, t), :] for r in k_refs]
        return blks[0] if len(blks) == 1 else jnp.concatenate(blks, axis=1)


def _softmax_fwd(q_parts, k_parts, v_parts, sq, sk, scale, causal, bias, name):
    tq = min(SOFTMAX_FWD_TILE, sq) if causal else _pick(sq, (DENSE_QUERY_TILE, SOFTMAX_TILE))
    tk = tq if causal else min(SOFTMAX_TILE, sk)
    nkv = sk // tk
    pp = _Parts(q_parts, k_parts, v_parts, tq, sk)

    def kern(*refs):
        q_refs, k_refs, v_refs, rest = pp.split(refs)
        if bias is not None:
            fc_ref, ft_ref, kmax_ref, o_ref, lse_ref = rest
            fcb = fc_ref[...]
        else:
            o_ref, lse_ref = rest
        i = pl.program_id(0)
        fold = _is_pow2(scale)
        head_on = [jnp.where(_lane_mask("pair", e, tk), 1.0, 0.0).astype(MXU_DTYPE) for e in range(2)]
        head_off = [jnp.where(_lane_mask("pair", e, tk), 0.0, 1.0).astype(MXU_DTYPE) for e in range(2)]
        lse_cols = {}
        for first in range(0, 2, FWD_PAIRS_PER_LOOP):
            pairs = list(range(first, first + FWD_PAIRS_PER_LOOP))
            heads = [2 * p + e for p in pairs for e in range(2)]
            qm = {}
            for p in pairs:
                masked_q = _masked_heads([r[...] for r in q_refs[p]], pp.kinds[p], p, MXU_DTYPE, scale if fold else None)
                qm.update({2 * p + e: masked_q[e] for e in range(2)})
            if bias is not None:
                reach = {h: _logit_reach(qm[h], kmax_ref[...], h) for h in heads}

            def block(j, carry, masked, pairs=pairs, qm=qm):
                off = _blk_off(j, tk)
                out = []
                for p in pairs:
                    kb = pp.k_block(k_refs[p], off, tk)
                    vb = v_refs[p][pl.ds(off, tk), :]
                    for e in range(2):
                        h = 2 * p + e
                        m, acc = carry[len(out)]
                        s = _dot_nt(qm[h], kb) * (LOG2E if fold else scale * LOG2E)
                        if bias is not None:
                            s = s - ft_ref[h:h + 1, pl.ds(off, tk)] * LOG2E
                        if masked:
                            s = jnp.where(_causal_mask(tq, False), s, NEG)
                        m_new = jnp.maximum(m, jnp.max(s, axis=1, keepdims=True))
                        pr = jnp.exp2(s - m_new).astype(MXU_DTYPE)
                        out.append((m_new, jnp.exp2(m - m_new) * acc + _dot(pr, vb * head_on[e] + head_off[e])))
                return tuple(out)

            carry = tuple((jnp.full((tq, 1), NEG, F32), jnp.zeros((tq, PAIR), F32)) for _ in heads)
            if causal and bias is not None:
                def alive(c, j, heads=heads, reach=reach):
                    return functools.reduce(jnp.maximum, [
                        jnp.max(reach[h] + _forget_top(ft_ref, h, _blk_off(j, tk)) - c[n][0] * LN2)
                        for n, h in enumerate(heads)])

                def step(state, block=block, alive=alive):
                    n, _, c = state
                    c = block(i - 1 - n, c, False)
                    return n + 1, alive(c, i - 1 - n), c

                carry = block(i, carry, True)
                _, _, carry = lax.while_loop(lambda st: jnp.logical_and(st[0] < i, st[1] > DEAD_LOGIT), step,
                                             (jnp.int32(0), alive(carry, i), carry))
            elif causal:
                carry = lax.fori_loop(0, i, lambda j, c, block=block: block(j, c, False), carry)
                carry = block(i, carry, True)
            else:
                for j in range(nkv):
                    carry = block(j, carry, False)
            for n, p in enumerate(pairs):
                outs = []
                for e in range(2):
                    m, acc = carry[2 * n + e]
                    l = _col(acc, HEAD_DIM * (1 - e))
                    outs.append(acc / l)
                    lse_cols[2 * p + e] = m * LN2 + jnp.log(l) + (_col(fcb, 2 * p + e) if bias is not None else 0.0)
                o_ref[:, p * PAIR:(p + 1) * PAIR] = _take_heads(outs, p)
        lse_ref[...] = _scatter_cols(lse_cols, tq)

    in_specs = pp.q_specs + pp.k_specs + pp.v_specs
    args = pp.q_args + pp.k_args + pp.v_args
    if bias is not None:
        in_specs += [_row_spec(tq, 0), pl.BlockSpec((8, sk), lambda i: (0, 0), pipeline_mode=pl.Buffered(1)),
                     pl.BlockSpec((1, PAIR), lambda i: (0, 0))]
        args += list(bias)
    return pl.pallas_call(
        kern, name=name, grid=(sq // tq,), in_specs=in_specs,
        out_specs=[_row_spec(tq, 0, GROUP_W), _row_spec(tq, 0)],
        out_shape=[jax.ShapeDtypeStruct((sq, GROUP_W), F32), jax.ShapeDtypeStruct((sq, PAIR), F32)],
        compiler_params=_att_params(True),
    )(*args)


def _softmax_bwd(q_parts, k_parts, v_parts, o, lse, do, do_blk, sq, sk, scale, causal, bias, name):
    tq = min(SOFTMAX_TILE, sq) if causal else _pick(sq, (DENSE_QUERY_TILE, SOFTMAX_TILE))
    tk = tq if causal else min(SOFTMAX_TILE, sk)
    nkv = sk // tk
    pp = _Parts(q_parts, k_parts, v_parts, tq, sk)
    quad = pp.nparts == 2
    wq = GROUP_W + (PAIR if quad else 0)

    def kern(*refs):
        q_refs, k_refs, v_refs, rest = pp.split(refs)
        if bias is not None:
            o_ref, lse_ref, do_ref, fc_ref, ft_ref, kmax_ref, dq_ref, dk_ref, dv_ref, dfq_ref, dfk_ref = rest
            fcb = fc_ref[...]
        else:
            o_ref, lse_ref, do_ref, dq_ref, dk_ref, dv_ref = rest
        i = pl.program_id(0)
        fold = _is_pow2(scale)

        @pl.when(i == 0)
        def _():
            dk_ref[...] = jnp.zeros_like(dk_ref)
            dv_ref[...] = jnp.zeros_like(dv_ref)
            if bias is not None:
                dfk_ref[...] = jnp.zeros_like(dfk_ref)

        lse_b = lse_ref[...]
        qm, dom, delta, lse_h = [], [], [], []
        for p in range(2):
            qm += _masked_heads([r[...] for r in q_refs[p]], pp.kinds[p], p, MXU_DTYPE, scale if fold else None)
            do_p = do_ref[:, p * PAIR:(p + 1) * PAIR]
            dom += _masked_heads([do_p], ["pair"], p, MXU_DTYPE)
            prod = do_p * o_ref[:, p * PAIR:(p + 1) * PAIR]
            for e in range(2):
                h = 2 * p + e
                delta.append(jnp.sum(jnp.where(_lane_mask("pair", h, tq), prod, 0.0), axis=1, keepdims=True))
                lse_h.append(_col(lse_b, h) - (_col(fcb, h) if bias is not None else 0.0))

        def block(j, carry, masked, pairs):
            off = _blk_off(j, tk)
            out = []
            for p in pairs:
                kb = pp.k_block(k_refs[p], off, tk)
                vb = v_refs[p][pl.ds(off, tk), :]
                dk_acc = jnp.zeros((tk, pp.width), F32)
                dv_acc = jnp.zeros((tk, PAIR), F32)
                for e in range(2):
                    h = 2 * p + e
                    dq, dfq = carry[len(out)]
                    s = _dot_nt(qm[h], kb)
                    if not fold:
                        s = s * scale
                    if bias is not None:
                        s = s - ft_ref[h:h + 1, pl.ds(off, tk)]
                    if masked:
                        s = jnp.where(_causal_mask(tq, False), s, NEG)
                    pr = jnp.exp(s - lse_h[h])
                    ds = pr * (_dot_nt(dom[h], vb) - delta[h])
                    dsb = (ds if fold else ds * scale).astype(MXU_DTYPE)
                    dv_acc = dv_acc + _dot_tn(pr.astype(MXU_DTYPE), dom[h])
                    dk_acc = dk_acc + _dot_tn(dsb, qm[h])
                    dq = dq + _dot(dsb, kb)
                    if bias is not None:
                        dfq = dfq + jnp.sum(ds, axis=1, keepdims=True)
                        dfk_ref[h:h + 1, pl.ds(off, tk)] -= jnp.sum(ds, axis=0, keepdims=True)
                    out.append((dq, dfq))
                dv_ref[pl.ds(off, tk), p * PAIR:(p + 1) * PAIR] += dv_acc
                dk_ref[pl.ds(off, tk), p * PAIR:(p + 1) * PAIR] += dk_acc[:, :PAIR]
                if quad:
                    dk_ref[pl.ds(off, tk), GROUP_W:] += dk_acc[:, PAIR:]
            return tuple(out)

        done = {}
        for first in range(0, 2, BWD_PAIRS_PER_LOOP):
            pairs = list(range(first, first + BWD_PAIRS_PER_LOOP))
            heads = [2 * p + e for p in pairs for e in range(2)]
            carry = tuple((jnp.zeros((tq, pp.width), F32), jnp.zeros((tq, 1), F32)) for _ in heads)
            if causal and bias is not None:
                reach = {h: _logit_reach(qm[h], kmax_ref[...], h) - lse_h[h] for h in heads}

                def alive(j, heads=heads, reach=reach):
                    return functools.reduce(jnp.maximum, [
                        jnp.max(reach[h] + _forget_top(ft_ref, h, _blk_off(j, tk))) for h in heads])

                def step(state, pairs=pairs, alive=alive):
                    n, _, c = state
                    return n + 1, alive(i - 1 - n), block(i - 1 - n, c, False, pairs)

                carry = block(i, carry, True, pairs)
                _, _, carry = lax.while_loop(lambda st: jnp.logical_and(st[0] < i, st[1] > DEAD_LOGIT), step,
                                             (jnp.int32(0), alive(i), carry))
            elif causal:
                carry = lax.fori_loop(0, i, lambda j, c, pairs=pairs: block(j, c, False, pairs), carry)
                carry = block(i, carry, True, pairs)
            else:
                for j in range(nkv):
                    carry = block(j, carry, False, pairs)
            done.update(zip(heads, carry))
        carry = [done[h] for h in range(N_HEADS)]
        dqs = [c[0] * scale if fold else c[0] for c in carry]
        for p in range(2):
            dq_ref[:, p * PAIR:(p + 1) * PAIR] = _take_heads([dqs[2 * p + e][:, :PAIR] for e in range(2)], p)
        if quad:
            dq_ref[:, GROUP_W:] = sum(jnp.where(_lane_mask("quad", h, tq), dqs[h][:, PAIR:], 0.0)
                                      for h in range(N_HEADS))
        if bias is not None:
            dfq_ref[...] = _scatter_cols({h: carry[h][1] for h in range(N_HEADS)}, tq)

    acc_spec = lambda rows, width: pl.BlockSpec((rows, width), lambda i: (0, 0), pipeline_mode=pl.Buffered(1))
    in_specs = pp.q_specs + pp.k_specs + pp.v_specs + [_row_spec(tq, 0, GROUP_W), _row_spec(tq, 0),
                                                       _row_spec(tq, do_blk, GROUP_W)]
    args = pp.q_args + pp.k_args + pp.v_args + [o, lse, do]
    out_specs = [_row_spec(tq, 0, wq), acc_spec(sk, wq), acc_spec(sk, GROUP_W)]
    out_shape = [jax.ShapeDtypeStruct((sq, wq), F32), jax.ShapeDtypeStruct((sk, wq), F32),
                 jax.ShapeDtypeStruct((sk, GROUP_W), F32)]
    if bias is not None:
        in_specs += [_row_spec(tq, 0), pl.BlockSpec((8, sk), lambda i: (0, 0), pipeline_mode=pl.Buffered(1)),
                     pl.BlockSpec((1, PAIR), lambda i: (0, 0))]
        args += list(bias)
        out_specs += [_row_spec(tq, 0), acc_spec(8, sk)]
        out_shape += [jax.ShapeDtypeStruct((sq, PAIR), F32), jax.ShapeDtypeStruct((8, sk), F32)]
    return pl.pallas_call(
        kern, name=name, grid=(sq // tq,), in_specs=in_specs, out_specs=out_specs, out_shape=out_shape,
        compiler_params=_att_params(False),
    )(*args)


def _sb_logs(qh, kb, valid):
    z = _dot_nt(qh, kb)
    sp = _softplus(z)
    lk = -sp
    if valid is not None:
        lk = jnp.where(valid, lk, 0.0)
    return lk, z - sp


def _sb_valid(d, tq, tk):
    r = lax.broadcasted_iota(jnp.int32, (tq, tk), 0)
    c = lax.broadcasted_iota(jnp.int32, (tq, tk), 1)
    return c + d * tk < r


def _tri_sums(xs, tri):
    t = xs[0].shape[0]
    pieces = [_split2(x) for x in xs]
    hi = _dot(jnp.concatenate([pc[0] for pc in pieces], axis=0), tri)
    lo = _dot(jnp.concatenate([pc[1] for pc in pieces], axis=0), tri)
    return [hi[n * t:(n + 1) * t] + lo[n * t:(n + 1) * t] for n in range(len(xs))]


def _sb_fwd(src, q_blk, k_blk, v_blk, s, scale, name):
    assert _is_pow2(scale)
    tq, t = min(SB_QUERY_TILE, s), min(ATT_TILE, s)
    slots = -(-(s // t) // SB_SLOT) * SB_SLOT
    width = N_HEADS * slots
    band = tq // t
    pair = lambda blk: [[(src, blk + p, "pair")] for p in range(2)]
    pp = _Parts(pair(q_blk), pair(k_blk), [(src, v_blk + p) for p in range(2)], tq, s)

    def kern(*refs):
        q_refs, k_refs, v_refs, (tri_ref, o_ref, rm_ref, cnt_ref) = pp.split(refs)
        i = pl.program_id(0)
        tri = tri_ref[...]
        lane = lax.broadcasted_iota(jnp.int32, (tq, width), 1)
        qm = []
        for p in range(2):
            qm += _masked_heads([q_refs[p][0][...]], ["pair"], p, MXU_DTYPE, scale)

        def block(j, carry, valid):
            accs, rights, rm = carry
            off = _blk_off(j, t)
            kbs = [k_refs[p][0][pl.ds(off, t), :] for p in range(2)]
            vbs = [v_refs[p][pl.ds(off, t), :] for p in range(2)]
            logs = [_sb_logs(qm[h], kbs[h // 2], valid) for h in range(N_HEADS)]
            tails = _tri_sums([lg[0] for lg in logs], tri)
            new_acc, new_right = [], []
            for h in range(N_HEADS):
                lk, ls = logs[h]
                w = jnp.exp(ls + tails[h] + rights[h])
                if valid is not None:
                    w = jnp.where(valid, w, 0.0)
                new_acc.append(accs[h] + _dot(w.astype(MXU_DTYPE), vbs[h // 2]))
                rm = rm + jnp.where(lane == slots * h + j, rights[h], 0.0)
                new_right.append(rights[h] + jnp.sum(lk, axis=1, keepdims=True))
            return tuple(new_acc), tuple(new_right), rm

        carry = (tuple(jnp.zeros((tq, PAIR), F32) for _ in range(N_HEADS)),
                 tuple(jnp.zeros((tq, 1), F32) for _ in range(N_HEADS)), jnp.zeros((tq, width), F32))
        for d in reversed(range(band)):
            carry = block(band * i + d, carry, _sb_valid(d, tq, t))

        def alive(c):
            return functools.reduce(jnp.maximum, [jnp.max(r) for r in c[1]])

        def step(state):
            n, _, c = state
            c = block(band * i - 1 - n, c, None)
            return n + 1, alive(c), c

        n_done, _, carry = lax.while_loop(lambda st: jnp.logical_and(st[0] < band * i, st[1] > EXP_UNDERFLOW),
                                          step, (jnp.int32(0), alive(carry), carry))
        cnt_ref[i] = n_done
        for p in range(2):
            o_ref[:, p * PAIR:(p + 1) * PAIR] = _take_heads([carry[0][2 * p + e] for e in range(2)], p)
        rm_ref[...] = carry[2]

    return pl.pallas_call(
        kern, name=name, grid=(s // tq,),
        in_specs=pp.q_specs + pp.k_specs + pp.v_specs + [pl.BlockSpec((t, t), lambda i: (0, 0))],
        out_specs=[_row_spec(tq, 0, GROUP_W), _row_spec(tq, 0, width), pl.BlockSpec(memory_space=pltpu.SMEM)],
        out_shape=[jax.ShapeDtypeStruct((s, GROUP_W), F32), jax.ShapeDtypeStruct((s, width), F32),
                   jax.ShapeDtypeStruct((s // tq,), jnp.int32)],
        compiler_params=_att_params(False),
    )(*(pp.q_args + pp.k_args + pp.v_args + [_tri(t, "row_gt_col")]))


def _sb_bwd(src, q_blk, k_blk, v_blk, do, do_blk, rm, visited, s, scale, name):
    assert _is_pow2(scale)
    tq, t = min(SB_QUERY_TILE, s), min(ATT_TILE, s)
    band = tq // t
    pair = lambda blk: [[(src, blk + p, "pair")] for p in range(2)]
    pp = _Parts(pair(q_blk), pair(k_blk), [(src, v_blk + p) for p in range(2)], tq, s)

    def kern(*refs):
        q_refs, k_refs, v_refs, (do_ref, rm_ref, tri_ref, pre_ref, cnt_ref, dq_ref, dk_ref, dv_ref) = pp.split(refs)
        i = pl.program_id(0)

        @pl.when(i == 0)
        def _():
            dk_ref[...] = jnp.zeros_like(dk_ref)
            dv_ref[...] = jnp.zeros_like(dv_ref)

        rmb = rm_ref[...]
        tri = tri_ref[...]
        pre = pre_ref[...]
        qm, dom = [], []
        for p in range(2):
            qm += _masked_heads([q_refs[p][0][...]], ["pair"], p, MXU_DTYPE, scale)
            dom += _masked_heads([do_ref[:, p * PAIR:(p + 1) * PAIR]], ["pair"], p, MXU_DTYPE)

        def block(j, carry, valid):
            dqs, lefts = carry
            off = _blk_off(j, t)
            kbs = [k_refs[p][0][pl.ds(off, t), :] for p in range(2)]
            vbs = [v_refs[p][pl.ds(off, t), :] for p in range(2)]
            logs = [_sb_logs(qm[h], kbs[h // 2], valid) for h in range(N_HEADS)]
            tails = _tri_sums([lg[0] for lg in logs], tri)
            ws, gs = [], []
            for h in range(N_HEADS):
                lk, ls = logs[h]
                w = jnp.exp(ls + tails[h] + _col(rmb, (rm.shape[1] // N_HEADS) * h + j))
                if valid is not None:
                    w = jnp.where(valid, w, 0.0)
                ws.append(w)
                gs.append(_dot_nt(dom[h], vbs[h // 2]) * w)
            prefix = _tri_sums(gs, pre)
            new_dq, new_left = [], []
            dk_acc = [jnp.zeros((t, PAIR), F32) for _ in range(2)]
            dv_acc = [jnp.zeros((t, PAIR), F32) for _ in range(2)]
            for h in range(N_HEADS):
                lk, ls = logs[h]
                sig = jnp.exp(ls)
                dz = gs[h] * (1.0 - sig) - sig * (prefix[h] + lefts[h])
                if valid is not None:
                    dz = jnp.where(valid, dz, 0.0)
                dzb = dz.astype(MXU_DTYPE)
                dv_acc[h // 2] = dv_acc[h // 2] + _dot_tn(ws[h].astype(MXU_DTYPE), dom[h])
                dk_acc[h // 2] = dk_acc[h // 2] + _dot_tn(dzb, qm[h])
                new_dq.append(dqs[h] + _dot(dzb, kbs[h // 2]))
                new_left.append(lefts[h] + jnp.sum(gs[h], axis=1, keepdims=True))
            for p in range(2):
                dv_ref[pl.ds(off, t), p * PAIR:(p + 1) * PAIR] += dv_acc[p]
                dk_ref[pl.ds(off, t), p * PAIR:(p + 1) * PAIR] += dk_acc[p]
            return tuple(new_dq), tuple(new_left)

        carry = (tuple(jnp.zeros((tq, PAIR), F32) for _ in range(N_HEADS)),
                 tuple(jnp.zeros((tq, 1), F32) for _ in range(N_HEADS)))
        carry = lax.fori_loop(band * i - cnt_ref[i], band * i, lambda j, c: block(j, c, None), carry)
        for d in range(band):
            carry = block(band * i + d, carry, _sb_valid(d, tq, t))
        for p in range(2):
            dq_ref[:, p * PAIR:(p + 1) * PAIR] = _take_heads([carry[0][2 * p + e] * scale for e in range(2)], p)

    mspec = pl.BlockSpec((t, t), lambda i: (0, 0))
    acc_spec = pl.BlockSpec((s, GROUP_W), lambda i: (0, 0), pipeline_mode=pl.Buffered(1))
    return pl.pallas_call(
        kern, name=name, grid=(s // tq,),
        in_specs=pp.q_specs + pp.k_specs + pp.v_specs + [_row_spec(tq, do_blk, GROUP_W), _row_spec(tq, 0, rm.shape[1]), mspec, mspec,
                                                         pl.BlockSpec(memory_space=pltpu.SMEM)],
        out_specs=[_row_spec(tq, 0, GROUP_W), acc_spec, acc_spec],
        out_shape=[jax.ShapeDtypeStruct((s, GROUP_W), F32)] * 3,
        compiler_params=_att_params(False),
    )(*(pp.q_args + pp.k_args + pp.v_args + [do, rm, _tri(t, "row_gt_col"), _tri(t, "row_lt_col"), visited]))


def _split_w_in(w):
    col = lambda n: w[:, _OFF[n]:_OFF[n + 1]]
    wa = jnp.concatenate([col(0), col(1), col(2), col(4), col(5), col(6), col(10)], axis=1)
    misc = jnp.concatenate([col(3), col(9), jnp.zeros((w.shape[0], 128 - 4 - MLA_ROPE), w.dtype)], axis=1)
    wb = jnp.concatenate([col(11), col(7), col(8), misc], axis=1)
    return wa, wb


def _merge_dw_in(dwp):
    a = lambda n: dwp[:, n * GROUP_W:(n + 1) * GROUP_W]
    b0 = PA_COLS
    gate = dwp[:, b0:b0 + 1024]
    cq = dwp[:, b0 + 1024:b0 + 1280]
    ckv = dwp[:, b0 + 1280:b0 + 1408]
    flog = dwp[:, b0 + 1408:b0 + 1412]
    krot = dwp[:, b0 + 1408 + MISC_KROT:b0 + 1408 + MISC_KROT + MLA_ROPE]
    return jnp.concatenate([a(0), a(1), a(2), flog, a(3), a(4), a(5), cq, ckv, krot, a(6), gate], axis=1)


def _heads_first(w, per_head, first):
    r = w.shape[0]
    w3 = w.reshape(r, N_HEADS, per_head)
    return jnp.concatenate([w3[:, :, :first].reshape(r, -1), w3[:, :, first:].reshape(r, -1)], axis=1)


def _heads_interleaved(w, per_head, first):
    r = w.shape[0]
    a = w[:, :N_HEADS * first].reshape(r, N_HEADS, first)
    b = w[:, N_HEADS * first:].reshape(r, N_HEADS, per_head - first)
    return jnp.concatenate([a, b], axis=2).reshape(r, N_HEADS * per_head)


def _pad_rows8(a):
    return a[:, :8].T


def _local_step(x2, mem2, tgt, p):
    s = x2.shape[0]
    nm = mem2.shape[0]
    head_scale = HEAD_DIM ** -0.5
    mla_scale = (HEAD_DIM + MLA_ROPE) ** -0.5
    tabs = _rope_tables(s)
    mats = _rope_matrices()
    pairs = lambda arr, blk: [[(arr, blk + q, "pair")] for q in range(2)]
    vals = lambda arr, blk: [(arr, blk + q) for q in range(2)]

    h, hb = _ln_fwd(x2, p["ln_in_g"], p["ln_in_b"], "ln_in_fwd")
    _, memn_b = _ln_fwd(mem2, p["mem_ln_g"], p["mem_ln_b"], "ln_mem_fwd")

    saved = []
    for l in range(DEPTH):
        wa, wb = _split_w_in(p["w_in"][l])
        wp = jnp.concatenate([wa, wb], axis=1)
        wq_up = _heads_first(p["w_mla_q_up"][l], HEAD_DIM + MLA_ROPE, HEAD_DIM)
        wkv_up = _heads_first(p["w_mla_kv_up"][l], 2 * HEAD_DIM, HEAD_DIM)
        bias_row = jnp.pad(p["b_forget"][l], (0, 128 - N_HEADS)).reshape(1, 128)
        pa, pb, fox_kmax = _proj_in(hb, wa, wb, GROUP_W)

        fc = _forget_fwd(pb, bias_row)
        fbias = (fc, _pad_rows8(fc), fox_kmax)
        o_fox, lse_fox = _softmax_fwd(pairs(pa, 0), pairs(pa, 2), vals(pa, 4), s, s, head_scale, True, fbias,
                                      "fox_fwd")
        o_sb, *rm_sb = _sb_fwd(pa, 6, 8, 10, s, head_scale, "sb_fwd")

        qfull, cqn = _mla_q_fwd(pb, p["mla_q_norm_g"][l], wq_up, tabs, mats)
        kv, ckvn, kr4 = _mla_kv_fwd(pb, p["mla_kv_norm_g"][l], wkv_up, tabs, mats)
        mla_q = [[(qfull, q, "pair"), (qfull, 2, "quad")] for q in range(2)]
        mla_k = [[(kv, q, "pair"), (kr4, 0, "quad")] for q in range(2)]
        o_mla, lse_mla = _softmax_fwd(mla_q, mla_k, vals(kv, 2), s, s, mla_scale, True, None, "mla_fwd")

        mkv = _matmul(memn_b, p["w_mem_kv"][l], MXU_DTYPE, "mem_kv")
        o_mem, lse_mem = _softmax_fwd(pairs(pa, 12), pairs(mkv, 0), vals(mkv, 2), s, nm, head_scale, False, None,
                                      "mem_fwd")

        groups = (o_fox, o_sb, o_mla, o_mem)
        if l < DEPTH - 1:
            gated, u, h_next, hb_next = _gate_out_proj_ln(groups, pb, p["w_out"][l], h, p["ln_g"][l], p["ln_b"][l])
        else:
            gated, u, dh, sq_cols = _gate_out_proj_ln(groups, pb, p["w_out"][l], h, p["ln_g"][l], p["ln_b"][l], tgt)
        saved.append(dict(u=u, hb=hb, wp=wp, wq_up=wq_up, wkv_up=wkv_up, bias_row=bias_row, pa=pa, pb=pb,
                          fbias=fbias, lse_fox=lse_fox, rm_sb=rm_sb, cqn=cqn, ckvn=ckvn, mla_q=mla_q, mla_k=mla_k,
                          kv=kv, lse_mla=lse_mla, mkv=mkv, lse_mem=lse_mem, groups=groups, gated=gated))
        if l < DEPTH - 1:
            h, hb = h_next, hb_next

    loss_sum = jnp.sum(sq_cols)

    grads = {k: [None] * DEPTH for k in ("w_in", "b_forget", "mla_q_norm_g", "w_mla_q_up", "mla_kv_norm_g",
                                         "w_mla_kv_up", "w_mem_kv", "w_out", "ln_g", "ln_b")}
    dmemn = []
    dy1, dy2, c1 = dh, None, 1.0
    for l in reversed(range(DEPTH)):
        r = saved[l]
        pa, pb = r["pa"], r["pb"]
        o_fox, o_sb, o_mla, o_mem = r["groups"]
        du, du_b, dmixed, dgate_b, dg, db = _ln_bwd(r["u"], p["ln_g"][l], dy1, dy2, c1, "ln_gate_bwd",
                                                    (p["w_out"][l], r["groups"], pb))
        grads["ln_g"][l], grads["ln_b"][l] = dg[0], db[0]
        grads["w_out"][l] = _matmul(r["gated"], du_b, F32, "out_proj_dw", "tn")

        dfq, dfk, dfv, dfc_q, dfc_k = _softmax_bwd(pairs(pa, 0), pairs(pa, 2), vals(pa, 4), o_fox, r["lse_fox"],
                                                   dmixed, 0, s, s, head_scale, True, r["fbias"], "fox_bwd")
        dmisc_f, dbf = _forget_bwd(pb, r["bias_row"], dfc_q + jnp.pad(dfc_k.T, ((0, 0), (0, 128 - 8))))
        grads["b_forget"][l] = dbf[0, :N_HEADS]

        dsq, dsk, dsv = _sb_bwd(pa, 6, 8, 10, dmixed, 1, *r["rm_sb"], s, head_scale, "sb_bwd")

        dqm, dkm, dvm = _softmax_bwd(r["mla_q"], r["mla_k"], vals(r["kv"], 2), o_mla, r["lse_mla"], dmixed, 2,
                                     s, s, mla_scale, True, None, "mla_bwd")
        dq_mla_b, dcq_b, dgq = _mla_q_bwd(dqm, pb, p["mla_q_norm_g"][l], r["wq_up"], tabs, mats)
        grads["w_mla_q_up"][l] = _heads_interleaved(_matmul(r["cqn"], dq_mla_b, F32, "q_up_dw", "tn"),
                                                    HEAD_DIM + MLA_ROPE, HEAD_DIM)
        grads["mla_q_norm_g"][l] = dgq[0]
        dkv_b, dckv_b, dmisc_k, dgkv = _mla_kv_bwd(dkm, dvm, pb, p["mla_kv_norm_g"][l], r["wkv_up"], tabs, mats)
        grads["w_mla_kv_up"][l] = _heads_interleaved(_matmul(r["ckvn"], dkv_b, F32, "kv_up_dw", "tn"),
                                                     2 * HEAD_DIM, HEAD_DIM)
        grads["mla_kv_norm_g"][l] = dgkv[0]

        dmq, dmk, dmv = _softmax_bwd(pairs(pa, 12), pairs(r["mkv"], 0), vals(r["mkv"], 2), o_mem, r["lse_mem"],
                                     dmixed, 3, s, nm, head_scale, False, None, "mem_bwd")
        dmkv_b = jnp.concatenate([dmk, dmv], axis=1).astype(MXU_DTYPE)
        grads["w_mem_kv"][l] = _matmul(memn_b, dmkv_b, F32, "mem_kv_dw", "tn")
        dmemn.append(_matmul(dmkv_b, p["w_mem_kv"][l], F32, "mem_kv_dx", "nt"))

        dp = [dfq, dfk, dfv, dsq, dsk, dsv, dmq, dgate_b, dcq_b, dckv_b, (dmisc_f, dmisc_k)]
        dhproj = _proj_dx(dp, r["wp"])
        grads["w_in"][l] = _merge_dw_in(_proj_dw(r["hb"], dp))
        dy1, dy2, c1 = du, dhproj, ALPHA

    dx, dg_in, db_in = _ln_bwd(x2, p["ln_in_g"], dy1, dy2, c1, "ln_in_bwd")
    _, dg_mem, db_mem = _ln_bwd(mem2, p["mem_ln_g"], dmemn[0], dmemn[1], 1.0, "ln_mem_bwd")
    out = {k: jnp.stack(v) for k, v in grads.items()}
    out.update(ln_in_g=dg_in[0], ln_in_b=db_in[0], mem_ln_g=dg_mem[0], mem_ln_b=db_mem[0])
    return loss_sum, dx, out


WIDE = "w_in"
FLAT_NAMES = ("w_out", "w_mem_kv", "w_mla_q_up", "w_mla_kv_up")
FLAT_ROWS = 896
BIG_NAMES = (WIDE,) + FLAT_NAMES
BIG_AXIS = dict(w_in=2, w_out=1, w_mem_kv=1, w_mla_q_up=2, w_mla_kv_up=2)
SMALL_NAMES = ("ln_in_g", "ln_in_b", "mem_ln_g", "mem_ln_b", "ln_g", "ln_b", "b_forget", "mla_q_norm_g",
               "mla_kv_norm_g")
ALL_NAMES = ("ln_in_g", "ln_in_b", "mem_ln_g", "mem_ln_b", "w_in", "b_forget", "mla_q_norm_g", "w_mla_q_up",
             "mla_kv_norm_g", "w_mla_kv_up", "w_mem_kv", "w_out", "ln_g", "ln_b")
N_CHIPS = 4
N_DEV = 8


def _rows_of(shape):
    rows = -(-int(np.prod(shape)) // LANES)
    return -(-rows // PACK_ALIGN) * PACK_ALIGN


def _pack(arrs, rows):
    parts = []
    for a in arrs:
        f = a.reshape(-1)
        n = _rows_of(a.shape) * LANES
        parts.append(jnp.pad(f, (0, n - f.shape[0])).reshape(-1, LANES))
    used = sum(q.shape[0] for q in parts)
    if rows > used:
        parts.append(jnp.zeros((rows - used, LANES), parts[0].dtype))
    return jnp.concatenate(parts, axis=0)


def _unpack(buf, shapes):
    out, r = [], 0
    for shp in shapes:
        n = _rows_of(shp)
        out.append(buf[r:r + n].reshape(-1)[:int(np.prod(shp))].reshape(shp))
        r += n
    return out


def _sharded_pair(get):
    wide = get(WIDE)
    return [wide.reshape(-1, wide.shape[-1]), _pack([get(n) for n in FLAT_NAMES], FLAT_ROWS)]


HBM_SPEC = pl.BlockSpec(memory_space=pltpu.HBM)


def _gather_weights(shards):
    n = len(shards)

    def body(*refs):
        w_refs, out_refs, (send_sems, recv_sems, local_sems) = refs[:n], refs[n:2 * n], refs[2 * n:]
        x, y, c = (lax.axis_index(a) for a in MESH_AXES)
        me, sibling = 2 * x + y, (x, y, 1 - c)
        chips = [(1 - x, y), (x, 1 - y), (1 - x, 1 - y)]
        local, first, passed = [], [], []
        for a in range(n):
            w_ref, out_ref, half = w_refs[a], out_refs[a], shards[a].shape[0] // 2

            def part(chip, core, out_ref=out_ref, half=half):
                return out_ref.at[chip, pl.ds(core * half, half)]

            def copy(k, src, dst, to, a=a):
                return pltpu.make_async_remote_copy(
                    src_ref=src, dst_ref=dst, send_sem=send_sems.at[6 * a + k], recv_sem=recv_sems.at[6 * a + k],
                    device_id=to, device_id_type=pl.DeviceIdType.MESH)

            local.append(pltpu.make_async_copy(w_ref, out_ref.at[me], local_sems.at[a]))
            local[-1].start()
            mine = [copy(k, w_ref.at[pl.ds(c * half, half)], part(me, c), (px, py, c))
                    for k, (px, py) in enumerate(chips)]
            for cp in mine:
                cp.start()
            first.append((mine, part, copy))
        for mine, part, copy in first:
            for k, (px, py) in enumerate(chips):
                copy(k, part(me, c), part(2 * px + py, c), (px, py, c)).wait_recv()
                passed.append(copy(3 + k, part(2 * px + py, c), part(2 * px + py, c), sibling))
                passed[-1].start()
        for mine, part, copy in first:
            for k, (px, py) in enumerate(chips):
                copy(3 + k, part(me, c), part(2 * px + py, 1 - c), sibling).wait_recv()
        for cp in [cp for mine, _, _ in first for cp in mine] + passed:
            cp.wait_send()
        for cp in local:
            cp.wait()

    return pl.pallas_call(
        body, name="gather_weights",
        out_shape=[jax.ShapeDtypeStruct((N_CHIPS,) + s.shape, s.dtype) for s in shards],
        in_specs=[HBM_SPEC] * n, out_specs=[HBM_SPEC] * n,
        scratch_shapes=[pltpu.SemaphoreType.DMA((6 * n,)), pltpu.SemaphoreType.DMA((6 * n,)),
                        pltpu.SemaphoreType.DMA((n,))],
    )(*shards)


def _exchange_grads(bigs, small):
    nb = len(bigs)
    halves = [b.shape[1] // 2 for b in bigs]
    chunks = [_pick(h, (128, 64, 32, 16)) for h in halves]

    def body(*refs):
        big_refs, small_ref = refs[:nb], refs[nb]
        sum_refs, out_refs = refs[nb + 1:2 * nb + 1], refs[2 * nb + 1:3 * nb + 2]
        scratch = refs[3 * nb + 2:]
        land_bufs, sum_bufs = scratch[:nb], scratch[nb:2 * nb]
        send_sems, recv_sems, local_sems, load_sems, swap_send, swap_recv, keep_sems = scratch[2 * nb:]
        x, y, c = (lax.axis_index(a) for a in MESH_AXES)
        me, my_chip = 4 * x + 2 * y + c, 2 * x + y
        flips = [(fx, fy, fc) for fx in (0, 1) for fy in (0, 1) for fc in (0, 1) if fx or fy or fc]
        peers = [(1 - x if fx else x, 1 - y if fy else y, 1 - c if fc else c) for fx, fy, fc in flips]

        sources = [lambda chip, core, r=big_refs[a], half=bigs[a].shape[1] // 2: r.at[chip, pl.ds(core * half, half)]
                   for a in range(nb)] + [lambda chip, core: small_ref]

        def copy(a, k, src, slot, to):
            return pltpu.make_async_remote_copy(
                src_ref=src, dst_ref=out_refs[a].at[slot], send_sem=send_sems.at[7 * a + k],
                recv_sem=recv_sems.at[7 * a + k], device_id=to, device_id_type=pl.DeviceIdType.MESH)

        own = [pltpu.make_async_copy(src(my_chip, c), out_refs[a].at[me], local_sems.at[a])
               for a, src in enumerate(sources)]
        for cp in own:
            cp.start()
        sends = [copy(a, k, src(2 * px + py, pc), me, (px, py, pc))
                 for a, src in enumerate(sources) for k, (px, py, pc) in enumerate(peers)]
        for cp in sends:
            cp.start()
        for a, src in enumerate(sources):
            for k, (px, py, pc) in enumerate(peers):
                copy(a, k, src(my_chip, c), 4 * px + 2 * py + pc, (px, py, pc)).wait_recv()
        for cp in sends:
            cp.wait_send()
        for cp in own:
            cp.wait()

        tails = []
        loads = [pltpu.make_async_copy(out_refs[a], land_bufs[a], load_sems.at[a]) for a in range(nb)]
        for load in loads:
            load.start()
        for a in range(nb):
            loads[a].wait()
            for r0 in range(0, halves[a], chunks[a]):
                rows = pl.ds(r0, chunks[a])
                total = land_bufs[a][0, rows, :].astype(F32)
                for d in range(1, N_DEV):
                    total = total + land_bufs[a][d, rows, :].astype(F32)
                sum_bufs[a][rows, :] = total
            keep = pltpu.make_async_copy(sum_bufs[a], sum_refs[a].at[c], keep_sems.at[a])
            give = pltpu.make_async_remote_copy(
                src_ref=sum_bufs[a], dst_ref=sum_refs[a].at[c], send_sem=swap_send.at[a], recv_sem=swap_recv.at[a],
                device_id=(x, y, 1 - c), device_id_type=pl.DeviceIdType.MESH)
            keep.start()
            give.start()
            tails.append((keep, give))
        for a, (keep, give) in enumerate(tails):
            pltpu.make_async_remote_copy(
                src_ref=sum_bufs[a], dst_ref=sum_refs[a].at[1 - c], send_sem=swap_send.at[a], recv_sem=swap_recv.at[a],
                device_id=(x, y, 1 - c), device_id_type=pl.DeviceIdType.MESH).wait_recv()
            give.wait_send()
            keep.wait()

    return pl.pallas_call(
        body, name="exchange_grads",
        out_shape=[jax.ShapeDtypeStruct((2, h, b.shape[2]), F32) for h, b in zip(halves, bigs)]
        + [jax.ShapeDtypeStruct((N_DEV, h, b.shape[2]), b.dtype) for h, b in zip(halves, bigs)]
        + [jax.ShapeDtypeStruct((N_DEV,) + small.shape, small.dtype)],
        in_specs=[HBM_SPEC] * (nb + 1), out_specs=[HBM_SPEC] * (2 * nb + 1),
        scratch_shapes=[pltpu.VMEM((N_DEV, h, b.shape[2]), b.dtype) for h, b in zip(halves, bigs)]
        + [pltpu.VMEM((h, b.shape[2]), F32) for h, b in zip(halves, bigs)]
        + [pltpu.SemaphoreType.DMA((7 * (nb + 1),)), pltpu.SemaphoreType.DMA((7 * (nb + 1),)),
           pltpu.SemaphoreType.DMA((nb + 1,)), pltpu.SemaphoreType.DMA((nb,)), pltpu.SemaphoreType.DMA((nb,)),
           pltpu.SemaphoreType.DMA((nb,)), pltpu.SemaphoreType.DMA((nb,))],
        compiler_params=pltpu.CompilerParams(vmem_limit_bytes=VMEM_LIMIT),
    )(*bigs, small)


def _adamw(parts, w, m, v, name):
    rows, width = w.shape
    n_parts = parts.shape[0]
    tile = _pick(rows, (128, 16, 8))
    bc1 = 1.0 - ADAM_B1 ** ADAM_STEP
    bc2 = 1.0 - ADAM_B2 ** ADAM_STEP

    def kern(p_ref, w_ref, m_ref, v_ref, g_ref, d_ref, nm_ref, nv_ref):
        g = p_ref[0].astype(F32)
        for d in range(1, n_parts):
            g = g + p_ref[d].astype(F32)
        nm = ADAM_B1 * m_ref[...] + (1.0 - ADAM_B1) * g
        nv = ADAM_B2 * v_ref[...] + (1.0 - ADAM_B2) * (g * g)
        g_ref[...] = g
        nm_ref[...] = nm
        nv_ref[...] = nv
        d_ref[...] = -ADAM_LR * ((nm / bc1) / (jnp.sqrt(nv / bc2) + ADAM_EPS) + ADAM_WD * w_ref[...])

    spec = pl.BlockSpec((tile, width), lambda i: (i, 0))
    return pl.pallas_call(
        kern, name=name, grid=(rows // tile,),
        in_specs=[pl.BlockSpec((n_parts, tile, width), lambda i: (0, i, 0)), spec, spec, spec],
        out_specs=[spec] * 4, out_shape=[jax.ShapeDtypeStruct((rows, width), F32)] * 4,
        compiler_params=pltpu.CompilerParams(dimension_semantics=("parallel",), vmem_limit_bytes=VMEM_LIMIT),
    )(parts, w, m, v)


def kernel(x, mem, ln_in_g, ln_in_b, mem_ln_g, mem_ln_b, w_in, b_forget, mla_q_norm_g, w_mla_q_up, mla_kv_norm_g, w_mla_kv_up, w_mem_kv, w_out, ln_g, ln_b, loss_target, m_ln_in_g, m_ln_in_b, m_mem_ln_g, m_mem_ln_b, m_w_in, m_b_forget, m_mla_q_norm_g, m_w_mla_q_up, m_mla_kv_norm_g, m_w_mla_kv_up, m_w_mem_kv, m_w_out, m_ln_g, m_ln_b, v_ln_in_g, v_ln_in_b, v_mem_ln_g, v_mem_ln_b, v_w_in, v_b_forget, v_mla_q_norm_g, v_w_mla_q_up, v_mla_kv_norm_g, v_w_mla_kv_up, v_w_mem_kv, v_w_out, v_ln_g, v_ln_b):
    w = dict(ln_in_g=ln_in_g, ln_in_b=ln_in_b, mem_ln_g=mem_ln_g, mem_ln_b=mem_ln_b, w_in=w_in, b_forget=b_forget,
             mla_q_norm_g=mla_q_norm_g, w_mla_q_up=w_mla_q_up, mla_kv_norm_g=mla_kv_norm_g,
             w_mla_kv_up=w_mla_kv_up, w_mem_kv=w_mem_kv, w_out=w_out, ln_g=ln_g, ln_b=ln_b)
    mo = dict(ln_in_g=m_ln_in_g, ln_in_b=m_ln_in_b, mem_ln_g=m_mem_ln_g, mem_ln_b=m_mem_ln_b, w_in=m_w_in,
              b_forget=m_b_forget, mla_q_norm_g=m_mla_q_norm_g, w_mla_q_up=m_w_mla_q_up,
              mla_kv_norm_g=m_mla_kv_norm_g, w_mla_kv_up=m_w_mla_kv_up, w_mem_kv=m_w_mem_kv, w_out=m_w_out,
              ln_g=m_ln_g, ln_b=m_ln_b)
    vo = dict(ln_in_g=v_ln_in_g, ln_in_b=v_ln_in_b, mem_ln_g=v_mem_ln_g, mem_ln_b=v_mem_ln_b, w_in=v_w_in,
              b_forget=v_b_forget, mla_q_norm_g=v_mla_q_norm_g, w_mla_q_up=v_w_mla_q_up,
              mla_kv_norm_g=v_mla_kv_norm_g, w_mla_kv_up=v_w_mla_kv_up, w_mem_kv=v_w_mem_kv, w_out=v_w_out,
              ln_g=v_ln_g, ln_b=v_ln_b)
    flat_shapes = [w[n].shape for n in FLAT_NAMES]
    small_shapes = [w[n].shape for n in SMALL_NAMES]

    got_wide, got_flat = _gather_weights(_sharded_pair(lambda n: w[n].astype(MXU_DTYPE)))
    full = dict(w)
    full[WIDE] = jnp.concatenate([got_wide[j] for j in range(N_CHIPS)], axis=1).reshape(
        w[WIDE].shape[:2] + (N_CHIPS * w[WIDE].shape[2],))
    per_chip = [_unpack(got_flat[j], flat_shapes) for j in range(N_CHIPS)]
    for idx, n in enumerate(FLAT_NAMES):
        full[n] = jnp.concatenate([per_chip[j][idx] for j in range(N_CHIPS)], axis=BIG_AXIS[n])

    loss_sum, dx, g = _local_step(x[0], mem[0], loss_target[0], full)
    loss = lax.psum(loss_sum * (0.5 / D_MODEL), MESH_AXES)

    def shard_of(n, j):
        ax, size = BIG_AXIS[n], w[n].shape[BIG_AXIS[n]]
        return lax.slice_in_dim(g[n], j * size, (j + 1) * size, axis=ax).astype(MXU_DTYPE)

    per_dest = [_sharded_pair(lambda n, j=j: shard_of(n, j)) for j in range(N_CHIPS)]
    bigs = [jnp.stack([per_dest[j][a] for j in range(N_CHIPS)]) for a in range(2)]
    exchanged = _exchange_grads(bigs, _pack([g[n] for n in SMALL_NAMES], SMALL_ROWS))
    halves, small_parts = exchanged[:2], exchanged[-1]

    res = []
    for a, (grad, nm) in enumerate(zip(halves, ("adamw_wide", "adamw_flat"))):
        state = [_sharded_pair(lambda n, src=src: src[n])[a] for src in (w, mo, vo)]
        res.append(_adamw(grad.reshape((1,) + state[0].shape), *state, nm))
    res_small = _adamw(small_parts, *[_pack([src[n] for n in SMALL_NAMES], SMALL_ROWS) for src in (w, mo, vo)],
                       "adamw_replicated")
    outs = []
    for kind in range(4):
        vals = {WIDE: res[0][kind].reshape(w[WIDE].shape)}
        vals.update(zip(FLAT_NAMES, _unpack(res[1][kind], flat_shapes)))
        vals.update(zip(SMALL_NAMES, _unpack(res_small[kind], small_shapes)))
        outs += [vals[n] for n in ALL_NAMES]
    return (loss, dx[None], *outs)
```

```python
import functools

import numpy as np
import jax
import jax.numpy as jnp
from jax import lax
from jax.experimental import pallas as pl
from jax.experimental.pallas import tpu as pltpu

F32 = jnp.float32
MXU_DTYPE = jnp.bfloat16

DEPTH = 2
D_MODEL = 1024
GROUP_W = 256
N_HEADS = 4
HEAD_DIM = 64
MLA_Q_RANK = 256
MLA_KV_RANK = 128
MLA_ROPE = 32
MLA_Q_COLS = N_HEADS * (HEAD_DIM + MLA_ROPE)
MLA_KV_COLS = N_HEADS * 2 * HEAD_DIM
ROPE_THETA = 10000.0
LN_EPS = 1e-5
RMS_EPS = 1e-6
ALPHA = (2 * DEPTH) ** 0.25
ADAM_LR, ADAM_B1, ADAM_B2, ADAM_EPS, ADAM_WD, ADAM_STEP = 0.001, 0.9, 0.999, 1e-08, 0.01, 10

_SPLIT = (256, 256, 256, 4, 256, 256, 256, 256, 128, 32, 256, 1024)
_OFF = [int(o) for o in np.cumsum((0,) + _SPLIT)]
IN_COLS = _OFF[-1]
PA_COLS = 7 * GROUP_W
PB_COLS = 1024 + 256 + 128 + 128
PB_CQ_BLK, PB_CKV_BLK, PB_MISC_BLK = 4, 10, 11
MISC_KROT = 4

LANES = 1024
PACK_ALIGN = 16
SMALL_ROWS = 144
ROW_TILE = 512
PROJ_BWD_ROWS = 512
ATT_TILE = 256
SB_QUERY_TILE = 256
SOFTMAX_TILE = 512
SOFTMAX_FWD_TILE = 1024
DENSE_QUERY_TILE = 1024
FWD_PAIRS_PER_LOOP = 1
BWD_PAIRS_PER_LOOP = 2
PAIR = 128
SB_SLOT = PAIR // N_HEADS
VMEM_LIMIT = 56 * 1024 * 1024
MATMUL_VMEM = 30 * 1024 * 1024
NEG = -1e30
LOG2E = 1.4426950408889634
LN2 = 0.6931471805599453
EXP_UNDERFLOW = -104.0
DEAD_LOGIT = -110.0
REACH_SLACK = 1.0 + 2.0 ** -10
MESH_AXES = ("x", "y", "c")


def _dot(a, b):
    return jnp.dot(a, b, preferred_element_type=F32)


def _dot_nt(a, b):
    return lax.dot_general(a, b, (((1,), (1,)), ((), ())), preferred_element_type=F32)


def _dot_tn(a, b):
    return lax.dot_general(a, b, (((0,), (0,)), ((), ())), preferred_element_type=F32)


def _split2(x):
    hi = x.astype(MXU_DTYPE)
    lo = (x - hi.astype(F32)).astype(MXU_DTYPE)
    return hi, lo


def _split3(x):
    hi = x.astype(MXU_DTYPE)
    r = x - hi.astype(F32)
    mid = r.astype(MXU_DTYPE)
    lo = (r - mid.astype(F32)).astype(MXU_DTYPE)
    return hi, mid, lo


def _dot_exact_r(x, pm):
    hi, mid, lo = _split3(x)
    return _dot(hi, pm) + _dot(mid, pm) + _dot(lo, pm)


def _dot_exact_l(pm, x):
    hi, mid, lo = _split3(x)
    return _dot(pm, hi) + _dot(pm, mid) + _dot(pm, lo)


def _pick(dim, prefs):
    for p in prefs:
        if dim % p == 0:
            return p
    return dim


def _softplus(z):
    return jnp.maximum(z, 0.0) + jnp.log(1.0 + jnp.exp(-jnp.abs(z)))


def _tile_options(dim):
    opts = [d for d in range(128, min(dim, 2048) + 1, 128) if dim % d == 0]
    return opts or [dim]


def _matmul_tiles(m, n, k, out_bytes):
    tk = k if k <= 4096 else _pick(k, (1024, 512, 256, 128))
    best = None
    for tm in _tile_options(m):
        for tn in _tile_options(n):
            vmem = 2 * 2 * (tm * tk + tk * tn) + 4 * tm * tn + 2 * out_bytes * tm * tn
            if vmem <= MATMUL_VMEM and (best is None or tm * tn / (tm + tn) > best[0]):
                best = (tm * tn / (tm + tn), tm, tn)
    return best[1], best[2], tk


def _matmul(a, b, out_dtype, name, mode="nn"):
    m, k = (a.shape[1], a.shape[0]) if mode == "tn" else a.shape
    n = b.shape[0] if mode == "nt" else b.shape[1]
    tm, tn, tk = _matmul_tiles(m, n, k, jnp.dtype(out_dtype).itemsize)
    nk = k // tk
    dot = {"nn": _dot, "tn": _dot_tn, "nt": _dot_nt}[mode]

    def kern(a_ref, b_ref, o_ref, *acc):
        if nk == 1:
            o_ref[...] = dot(a_ref[...], b_ref[...]).astype(o_ref.dtype)
            return
        acc_ref, = acc
        kk = pl.program_id(2)

        @pl.when(kk == 0)
        def _():
            acc_ref[...] = jnp.zeros_like(acc_ref)

        acc_ref[...] += dot(a_ref[...], b_ref[...])

        @pl.when(kk == nk - 1)
        def _():
            o_ref[...] = acc_ref[...].astype(o_ref.dtype)

    a_spec = (pl.BlockSpec((tk, tm), lambda i, j, kk: (kk, i)) if mode == "tn"
              else pl.BlockSpec((tm, tk), lambda i, j, kk: (i, kk)))
    b_spec = (pl.BlockSpec((tn, tk), lambda i, j, kk: (j, kk)) if mode == "nt"
              else pl.BlockSpec((tk, tn), lambda i, j, kk: (kk, j)))
    return pl.pallas_call(
        kern, name=name, grid=(m // tm, n // tn, nk), in_specs=[a_spec, b_spec],
        out_specs=pl.BlockSpec((tm, tn), lambda i, j, kk: (i, j)),
        out_shape=jax.ShapeDtypeStruct((m, n), out_dtype),
        scratch_shapes=[pltpu.VMEM((tm, tn), F32)] if nk > 1 else [],
        compiler_params=pltpu.CompilerParams(
            dimension_semantics=("parallel", "parallel", "arbitrary"), vmem_limit_bytes=VMEM_LIMIT),
    )(a.astype(MXU_DTYPE), b.astype(MXU_DTYPE))


def _proj_in(hb, wa, wb, key_cols):
    rows, d = hb.shape
    tm = min(PROJ_BWD_ROWS, rows)

    def kern(a_ref, wa_ref, wb_ref, pa_ref, pb_ref, kmax_ref):
        @pl.when(pl.program_id(0) == 0)
        def _():
            kmax_ref[...] = jnp.zeros_like(kmax_ref)

        a = a_ref[...]
        pa = _dot(a, wa_ref[...]).astype(pa_ref.dtype)
        pa_ref[...] = pa
        pb_ref[...] = _dot(a, wb_ref[...])
        cols = {}
        for p in range(2):
            k32 = pa[:, key_cols + p * PAIR:key_cols + (p + 1) * PAIR].astype(F32)
            for e in range(2):
                sq = jnp.sum(jnp.where(_lane_mask("pair", e, tm), k32 * k32, 0.0), axis=1, keepdims=True)
                cols[2 * p + e] = jnp.max(sq, axis=0, keepdims=True)
        lane = lax.broadcasted_iota(jnp.int32, (1, PAIR), 1)
        kmax_ref[...] = jnp.maximum(kmax_ref[...], sum(jnp.where(lane == h, c, 0.0) for h, c in cols.items()))

    whole = lambda w: pl.BlockSpec(w.shape, lambda i: (0, 0), pipeline_mode=pl.Buffered(1))
    return pl.pallas_call(
        kern, name="proj_in", grid=(rows // tm,),
        in_specs=[pl.BlockSpec((tm, d), lambda i: (i, 0)), whole(wa), whole(wb)],
        out_specs=[pl.BlockSpec((tm, wa.shape[1]), lambda i: (i, 0)), pl.BlockSpec((tm, wb.shape[1]), lambda i: (i, 0)),
                   pl.BlockSpec((1, PAIR), lambda i: (0, 0))],
        out_shape=[jax.ShapeDtypeStruct((rows, wa.shape[1]), MXU_DTYPE), jax.ShapeDtypeStruct((rows, wb.shape[1]), F32),
                   jax.ShapeDtypeStruct((1, PAIR), F32)],
        compiler_params=pltpu.CompilerParams(dimension_semantics=("arbitrary",), vmem_limit_bytes=VMEM_LIMIT),
    )(hb, wa.astype(MXU_DTYPE), wb.astype(MXU_DTYPE))


def _piece_arrays(pieces):
    return [a for p in pieces for a in (p if isinstance(p, tuple) else (p,))]


def _join_pieces(refs, pieces):
    refs, cols = list(refs), []
    for p in pieces:
        vals = [refs.pop(0)[...] for _ in (p if isinstance(p, tuple) else (p,))]
        cols.append(functools.reduce(jnp.add, vals).astype(MXU_DTYPE))
    return jnp.concatenate(cols, axis=1)


def _proj_dx(pieces, w):
    arrs = _piece_arrays(pieces)
    rows, n = arrs[0].shape[0], w.shape[0]
    tm = min(PROJ_BWD_ROWS, rows)

    def kern(*refs):
        refs[-1][...] = _dot_nt(_join_pieces(refs[:len(arrs)], pieces), refs[len(arrs)][...])

    return pl.pallas_call(
        kern, name="proj_dx", grid=(rows // tm,),
        in_specs=[pl.BlockSpec((tm, a.shape[1]), lambda i: (i, 0)) for a in arrs]
        + [pl.BlockSpec(w.shape, lambda i: (0, 0), pipeline_mode=pl.Buffered(1))],
        out_specs=pl.BlockSpec((tm, n), lambda i: (i, 0)), out_shape=jax.ShapeDtypeStruct((rows, n), F32),
        compiler_params=pltpu.CompilerParams(dimension_semantics=("parallel",), vmem_limit_bytes=VMEM_LIMIT),
    )(*arrs, w.astype(MXU_DTYPE))


def _proj_dw(a, pieces):
    arrs = _piece_arrays(pieces)
    rows, m = a.shape
    k = sum(x.shape[1] for x in (p[0] if isinstance(p, tuple) else p for p in pieces))
    tk = min(PROJ_BWD_ROWS, rows)

    def kern(*refs):
        o_ref = refs[-1]

        @pl.when(pl.program_id(0) == 0)
        def _():
            o_ref[...] = jnp.zeros_like(o_ref)

        o_ref[...] += _dot_tn(refs[0][...], _join_pieces(refs[1:1 + len(arrs)], pieces))

    return pl.pallas_call(
        kern, name="proj_dw", grid=(rows // tk,),
        in_specs=[pl.BlockSpec((tk, m), lambda i: (i, 0))]
        + [pl.BlockSpec((tk, x.shape[1]), lambda i: (i, 0)) for x in arrs],
        out_specs=pl.BlockSpec((m, k), lambda i: (0, 0), pipeline_mode=pl.Buffered(1)),
        out_shape=jax.ShapeDtypeStruct((m, k), F32),
        compiler_params=pltpu.CompilerParams(dimension_semantics=("arbitrary",), vmem_limit_bytes=VMEM_LIMIT),
    )(a, *arrs)


def _rowwise(body, name, rows, tile, row_ins, full_ins, row_outs, acc_outs=(), scratch=(),
             reverse=False, sequential=False):
    n = rows // tile

    def ridx(i):
        return (n - 1 - i) if reverse else i

    in_specs, args = [], []
    for arr, width, cb in row_ins:
        in_specs.append(pl.BlockSpec((tile, width), lambda i, cb=cb: (ridx(i), cb)))
        args.append(arr)
    for arr in full_ins:
        in_specs.append(pl.BlockSpec(arr.shape, lambda i, nd=arr.ndim: (0,) * nd))
        args.append(arr)
    out_shape = [jax.ShapeDtypeStruct((rows, w), dt) for w, dt in row_outs]
    out_shape += [jax.ShapeDtypeStruct(s, dt) for s, dt in acc_outs]
    out_specs = [pl.BlockSpec((tile, w), lambda i: (ridx(i), 0)) for w, dt in row_outs]
    out_specs += [pl.BlockSpec(s, lambda i, nd=len(s): (0,) * nd) for s, dt in acc_outs]

    def kern(*refs):
        body(pl.program_id(0), *refs)

    sem = "arbitrary" if (acc_outs or sequential) else "parallel"
    return pl.pallas_call(
        kern, name=name, grid=(n,), in_specs=in_specs, out_specs=out_specs, out_shape=out_shape,
        scratch_shapes=list(scratch),
        compiler_params=pltpu.CompilerParams(dimension_semantics=(sem,), vmem_limit_bytes=VMEM_LIMIT),
    )(*args)


def _ln_stats(u):
    mu = jnp.mean(u, axis=-1, keepdims=True)
    xc = u - mu
    var = jnp.mean(xc * xc, axis=-1, keepdims=True)
    return xc, lax.rsqrt(var + LN_EPS)


def _ln_fwd(a, g, beta, name):
    rows, d = a.shape

    def body(i, a_ref, g_ref, be_ref, h_ref, hb_ref):
        xc, rstd = _ln_stats(a_ref[...])
        y = xc * rstd * g_ref[...] + be_ref[...]
        h_ref[...] = y
        hb_ref[...] = y.astype(hb_ref.dtype)

    return _rowwise(body, name, rows, min(ROW_TILE, rows), [(a, d, 0)],
                    [g.reshape(1, d), beta.reshape(1, d)], [(d, F32), (d, MXU_DTYPE)])


def _ln_bwd(u, g, dy1, dy2, c1, name, gate=None):
    rows, d = u.shape
    has_2 = dy2 is not None
    n_groups = len(gate[1]) if gate else 0

    def body(i, *refs):
        refs = list(refs)
        uu = refs.pop(0)[...]
        dy = c1 * refs.pop(0)[...]
        if has_2:
            dy = dy + refs.pop(0)[...]
        if gate:
            mixed = jnp.concatenate([refs.pop(0)[...] for _ in range(n_groups)], axis=1)
            gt = refs.pop(0)[...]
            g_ref, w_ref, du_ref, dub_ref, dm_ref, dgate_ref, dg_ref, db_ref = refs
        else:
            g_ref, du_ref, dg_ref, db_ref = refs

        @pl.when(i == 0)
        def _():
            dg_ref[...] = jnp.zeros_like(dg_ref)
            db_ref[...] = jnp.zeros_like(db_ref)

        xc, rstd = _ln_stats(uu)
        xhat = xc * rstd
        dxh = dy * g_ref[...]
        m1 = jnp.mean(dxh, axis=-1, keepdims=True)
        m2 = jnp.mean(dxh * xhat, axis=-1, keepdims=True)
        du = rstd * (dxh - m1 - xhat * m2)
        du_ref[...] = du
        dg_ref[...] += jnp.sum(dy * xhat, axis=0, keepdims=True)
        db_ref[...] += jnp.sum(dy, axis=0, keepdims=True)
        if gate:
            dub = du.astype(dub_ref.dtype)
            dub_ref[...] = dub
            dgated = _dot_nt(dub, w_ref[...])
            sig = 1.0 / (1.0 + jnp.exp(-gt))
            dm_ref[...] = dgated * (gt * sig)
            dgate_ref[...] = (dgated * mixed * (sig * (1.0 + gt * (1.0 - sig)))).astype(dgate_ref.dtype)

    row_ins = [(u, d, 0), (dy1, d, 0)] + ([(dy2, d, 0)] if has_2 else [])
    full_ins = [g.reshape(1, d)]
    row_outs = [(d, F32)]
    if gate:
        w_out, groups, pb = gate
        w = GROUP_W * n_groups
        row_ins += [(o, GROUP_W, 0) for o in groups] + [(pb, w, 0)]
        full_ins += [w_out.astype(MXU_DTYPE)]
        row_outs += [(d, MXU_DTYPE), (w, F32), (w, MXU_DTYPE)]
    return _rowwise(body, name, rows, min(PROJ_BWD_ROWS if gate else ROW_TILE, rows), row_ins, full_ins, row_outs,
                    [((1, d), F32), ((1, d), F32)])


def _gate_out_proj_ln(groups, pb, w_out, h, g, beta, target=None):
    rows, d = h.shape
    last = target is not None
    w = GROUP_W * len(groups)

    def body(i, *refs):
        refs = list(refs)
        mixed = jnp.concatenate([refs.pop(0)[...] for _ in groups], axis=1)
        gate = refs.pop(0)[...]
        if last:
            h_ref, t_ref, w_ref, g_ref, be_ref, a_ref, u_ref, dh_ref, acc_ref = refs

            @pl.when(i == 0)
            def _():
                acc_ref[...] = jnp.zeros_like(acc_ref)
        else:
            h_ref, w_ref, g_ref, be_ref, a_ref, u_ref, o_ref, ob_ref = refs
        gated = (mixed * (gate / (1.0 + jnp.exp(-gate)))).astype(a_ref.dtype)
        a_ref[...] = gated
        u = ALPHA * h_ref[...] + _dot(gated, w_ref[...])
        u_ref[...] = u
        xc, rstd = _ln_stats(u)
        y = xc * rstd * g_ref[...] + be_ref[...]
        if last:
            e = y - t_ref[...]
            dh_ref[...] = e * (1.0 / d)
            acc_ref[...] += jnp.sum(e * e, axis=0, keepdims=True)
        else:
            o_ref[...] = y
            ob_ref[...] = y.astype(ob_ref.dtype)

    row_ins = [(o, GROUP_W, 0) for o in groups] + [(pb, w, 0), (h, d, 0)] + ([(target, d, 0)] if last else [])
    full_ins = [w_out.astype(MXU_DTYPE), g.reshape(1, d), beta.reshape(1, d)]
    if last:
        return _rowwise(body, "gate_out_proj_ln_loss", rows, PROJ_BWD_ROWS, row_ins, full_ins,
                        [(w, MXU_DTYPE), (d, F32), (d, F32)], [((1, d), F32)])
    return _rowwise(body, "gate_out_proj_ln", rows, PROJ_BWD_ROWS, row_ins, full_ins,
                    [(w, MXU_DTYPE), (d, F32), (d, F32), (d, MXU_DTYPE)])


def _tri(n, kind):
    r = np.arange(n)[:, None]
    c = np.arange(n)[None, :]
    m = {"lower_incl": r >= c, "upper_incl": r <= c, "row_gt_col": r > c, "row_lt_col": r < c}[kind]
    return jnp.asarray(m.astype(np.float32), dtype=MXU_DTYPE)


def _forget_fwd(pb, bias_row):
    rows = pb.shape[0]
    tile = _pick(rows, (1024, 512, 256))

    def body(i, x_ref, b_ref, l_ref, o_ref, carry_ref):
        @pl.when(i == 0)
        def _():
            carry_ref[...] = jnp.zeros_like(carry_ref)

        xx = x_ref[...] + b_ref[...]
        lane = lax.broadcasted_iota(jnp.int32, xx.shape, 1)
        lf = jnp.where(lane < N_HEADS, -_softplus(-xx), 0.0)
        o_ref[...] = _dot_exact_l(l_ref[...], lf) + carry_ref[...]
        carry_ref[...] += jnp.sum(lf, axis=0, keepdims=True)

    return _rowwise(body, "forget_fwd", rows, tile, [(pb, 128, PB_MISC_BLK)],
                    [bias_row, _tri(tile, "lower_incl")], [(128, F32)],
                    scratch=[pltpu.VMEM((1, 128), F32)], sequential=True)[0]


def _forget_bwd(pb, bias_row, dfc):
    rows = pb.shape[0]
    tile = _pick(rows, (1024, 512, 256))

    def body(i, x_ref, df_ref, b_ref, u_ref, o_ref, db_ref, carry_ref):
        @pl.when(i == 0)
        def _():
            carry_ref[...] = jnp.zeros_like(carry_ref)
            db_ref[...] = jnp.zeros_like(db_ref)

        df = df_ref[...]
        sfx = _dot_exact_l(u_ref[...], df) + carry_ref[...]
        carry_ref[...] += jnp.sum(df, axis=0, keepdims=True)
        xx = x_ref[...] + b_ref[...]
        lane = lax.broadcasted_iota(jnp.int32, xx.shape, 1)
        dl = jnp.where(lane < N_HEADS, sfx / (1.0 + jnp.exp(xx)), 0.0)
        o_ref[...] = dl
        db_ref[...] += jnp.sum(dl, axis=0, keepdims=True)

    return _rowwise(body, "forget_bwd", rows, tile,
                    [(pb, 128, PB_MISC_BLK), (dfc, 128, 0)],
                    [bias_row, _tri(tile, "upper_incl")], [(128, F32)], [((1, 128), F32)],
                    scratch=[pltpu.VMEM((1, 128), F32)], reverse=True, sequential=True)


def _rope_tables(s):
    half = MLA_ROPE // 2
    inv_freq = ROPE_THETA ** (-jnp.arange(half, dtype=F32) / half)
    ang = jnp.arange(s).astype(F32)[:, None] * inv_freq[None, :]
    cos2 = jnp.tile(jnp.cos(ang), (1, 2))
    sin2 = jnp.tile(jnp.sin(ang), (1, 2))
    cx = jnp.tile(cos2, (1, N_HEADS))
    sx = jnp.tile(sin2, (1, N_HEADS))
    pad = ((0, 0), (MISC_KROT, 128 - MISC_KROT - MLA_ROPE))
    ck = jnp.pad(cos2, pad)
    sk = jnp.pad(sin2, pad)
    return dict(ck=ck, sk=sk, cx=cx, sx=sx)


def _rot_matrix(width, bases):
    half = MLA_ROPE // 2
    p = np.zeros((width, width), np.float32)
    for b in bases:
        for i in range(half):
            p[b + half + i, b + i] = -1.0
            p[b + i, b + half + i] = 1.0
    return p


def _rope_matrices():
    pk = _rot_matrix(128, [MISC_KROT])
    p4 = _rot_matrix(128, [h * MLA_ROPE for h in range(N_HEADS)])
    a = np.zeros((128, 128), np.float32)
    for h in range(N_HEADS):
        for r in range(MLA_ROPE):
            a[h * MLA_ROPE + r, MISC_KROT + r] = 1.0
    cast = lambda m: jnp.asarray(m, dtype=MXU_DTYPE)
    return dict(p4=cast(p4), p4t=cast(p4.T), pk=cast(pk), spread=cast(a.T), xa=cast(a), xb=cast(p4.T @ a))


def _rms(c, g):
    r = lax.rsqrt(jnp.mean(c * c, axis=-1, keepdims=True) + RMS_EPS)
    return c * r * g


def _mla_q_fwd(pb, g, w_up, tabs, mats):
    rows = pb.shape[0]

    def body(i, c_ref, cos_ref, sin_ref, g_ref, w_ref, p_ref, q_ref, cn_ref):
        cn = _rms(c_ref[...], g_ref[...]).astype(cn_ref.dtype)
        cn_ref[...] = cn
        q = _dot(cn, w_ref[...])
        qr = q[:, GROUP_W:]
        q_ref[:, :GROUP_W] = q[:, :GROUP_W].astype(q_ref.dtype)
        q_ref[:, GROUP_W:] = (qr * cos_ref[...] + _dot_exact_r(qr, p_ref[...]) * sin_ref[...]).astype(q_ref.dtype)

    return _rowwise(body, "mla_q_fwd", rows, ROW_TILE,
                    [(pb, MLA_Q_RANK, PB_CQ_BLK), (tabs["cx"], PAIR, 0), (tabs["sx"], PAIR, 0)],
                    [g.reshape(1, MLA_Q_RANK), w_up.astype(MXU_DTYPE), mats["p4"]],
                    [(MLA_Q_COLS, MXU_DTYPE), (MLA_Q_RANK, MXU_DTYPE)])


def _mla_kv_fwd(pb, g, w_up, tabs, mats):
    rows = pb.shape[0]

    def body(i, c_ref, x_ref, cos_ref, sin_ref, g_ref, w_ref, p_ref, sp_ref, kv_ref, cn_ref, kr_ref):
        cn = _rms(c_ref[...], g_ref[...]).astype(cn_ref.dtype)
        cn_ref[...] = cn
        kv_ref[...] = _dot(cn, w_ref[...]).astype(kv_ref.dtype)
        xx = x_ref[...]
        kr = xx * cos_ref[...] + _dot_exact_r(xx, p_ref[...]) * sin_ref[...]
        kr_ref[...] = _dot_exact_r(kr, sp_ref[...]).astype(kr_ref.dtype)

    return _rowwise(body, "mla_kv_fwd", rows, ROW_TILE,
                    [(pb, MLA_KV_RANK, PB_CKV_BLK), (pb, 128, PB_MISC_BLK), (tabs["ck"], 128, 0), (tabs["sk"], 128, 0)],
                    [g.reshape(1, MLA_KV_RANK), w_up.astype(MXU_DTYPE), mats["pk"], mats["spread"]],
                    [(MLA_KV_COLS, MXU_DTYPE), (MLA_KV_RANK, MXU_DTYPE), (128, MXU_DTYPE)])


def _rms_bwd(c, g, dy):
    r = lax.rsqrt(jnp.mean(c * c, axis=-1, keepdims=True) + RMS_EPS)
    dyg = dy * g
    dc = r * dyg - c * (r * r * r) * jnp.mean(c * dyg, axis=-1, keepdims=True)
    return dc, jnp.sum(dy * c * r, axis=0, keepdims=True)


def _mla_q_bwd(dq_full, pb, g, w_up, tabs, mats):
    rows, nq = dq_full.shape

    def body(i, d_ref, c_ref, cos_ref, sin_ref, g_ref, w_ref, pt_ref, dq_ref, dc_ref, dg_ref):
        @pl.when(i == 0)
        def _():
            dg_ref[...] = jnp.zeros_like(dg_ref)

        dr = d_ref[:, GROUP_W:]
        dq = jnp.concatenate([d_ref[:, :GROUP_W], dr * cos_ref[...] + _dot_exact_r(dr * sin_ref[...], pt_ref[...])],
                             axis=1).astype(dq_ref.dtype)
        dq_ref[...] = dq
        dc, dg = _rms_bwd(c_ref[...], g_ref[...], _dot_nt(dq, w_ref[...]))
        dc_ref[...] = dc.astype(dc_ref.dtype)
        dg_ref[...] += dg

    return _rowwise(body, "mla_q_bwd", rows, ROW_TILE,
                    [(dq_full, nq, 0), (pb, MLA_Q_RANK, PB_CQ_BLK), (tabs["cx"], PAIR, 0), (tabs["sx"], PAIR, 0)],
                    [g.reshape(1, MLA_Q_RANK), w_up.astype(MXU_DTYPE), mats["p4t"]],
                    [(nq, MXU_DTYPE), (MLA_Q_RANK, MXU_DTYPE)], [((1, MLA_Q_RANK), F32)])


def _mla_kv_bwd(dk, dv, pb, g, w_up, tabs, mats):
    rows = dk.shape[0]

    def body(i, dk_ref, dr_ref, dv_ref, c_ref, cos_ref, sin_ref, g_ref, w_ref, a_ref, b_ref,
             dkv_ref, dc_ref, dm_ref, dg_ref):
        @pl.when(i == 0)
        def _():
            dg_ref[...] = jnp.zeros_like(dg_ref)

        dkv = jnp.concatenate([dk_ref[...], dv_ref[...]], axis=1).astype(dkv_ref.dtype)
        dkv_ref[...] = dkv
        dc, dg = _rms_bwd(c_ref[...], g_ref[...], _dot_nt(dkv, w_ref[...]))
        dc_ref[...] = dc.astype(dc_ref.dtype)
        dg_ref[...] += dg
        dr = dr_ref[...]
        dm_ref[...] = _dot_exact_r(dr * cos_ref[...], a_ref[...]) + _dot_exact_r(dr * sin_ref[...], b_ref[...])

    return _rowwise(body, "mla_kv_bwd", rows, ROW_TILE,
                    [(dk, GROUP_W, 0), (dk, PAIR, 2), (dv, GROUP_W, 0), (pb, MLA_KV_RANK, PB_CKV_BLK),
                     (tabs["cx"], PAIR, 0), (tabs["sx"], PAIR, 0)],
                    [g.reshape(1, MLA_KV_RANK), w_up.astype(MXU_DTYPE), mats["xa"], mats["xb"]],
                    [(MLA_KV_COLS, MXU_DTYPE), (MLA_KV_RANK, MXU_DTYPE), (PAIR, F32)], [((1, MLA_KV_RANK), F32)])


def _att_params(parallel):
    return pltpu.CompilerParams(dimension_semantics=("parallel" if parallel else "arbitrary",),
                                vmem_limit_bytes=VMEM_LIMIT)


def _blk_off(j, t):
    return j * t if isinstance(j, int) else pl.multiple_of(j * t, t)


def _causal_mask(t, strict):
    r = lax.broadcasted_iota(jnp.int32, (t, t), 0)
    c = lax.broadcasted_iota(jnp.int32, (t, t), 1)
    return (c < r) if strict else (c <= r)


def _lane_mask(kind, head, rows):
    lane = lax.broadcasted_iota(jnp.int32, (rows, PAIR), 1)
    if kind == "pair":
        return (lane < HEAD_DIM) if head % 2 == 0 else (lane >= HEAD_DIM)
    return (lane >= MLA_ROPE * head) & (lane < MLA_ROPE * (head + 1))


def _row_spec(t, cb, width=PAIR):
    return pl.BlockSpec((t, width), lambda i, cb=cb: (i, cb))


def _whole_spec(rows, cb, width=PAIR):
    return pl.BlockSpec((rows, width), lambda i, cb=cb: (0, cb), pipeline_mode=pl.Buffered(1))


def _is_pow2(x):
    return float(np.frexp(x)[0]) == 0.5


def _masked_heads(blocks, kinds, pair, dtype, scale=None):
    out = []
    for e in range(2):
        head = 2 * pair + e
        parts = [jnp.where(_lane_mask(k, head, b.shape[0]), b.astype(F32) * (1.0 if scale is None else scale),
                           0.0).astype(dtype)
                 for b, k in zip(blocks, kinds)]
        out.append(parts[0] if len(parts) == 1 else jnp.concatenate(parts, axis=1))
    return out


def _logit_reach(qh, kmax2, head):
    q32 = qh.astype(F32)
    return jnp.sqrt(jnp.sum(q32 * q32, axis=1, keepdims=True) * _col(kmax2, head)) * REACH_SLACK


def _forget_top(ft_ref, head, off):
    return jnp.max(-ft_ref[head:head + 1, pl.ds(off, PAIR)])


def _col(block, idx):
    lane = lax.broadcasted_iota(jnp.int32, block.shape, 1)
    return jnp.sum(jnp.where(lane == idx, block, 0.0), axis=1, keepdims=True)


def _scatter_cols(cols, t):
    lane = lax.broadcasted_iota(jnp.int32, (t, PAIR), 1)
    out = jnp.zeros((t, PAIR), F32)
    for idx, c in cols.items():
        out = out + jnp.where(lane == idx, c, 0.0)
    return out


def _take_heads(per_head, pair):
    return jnp.where(_lane_mask("pair", 0, per_head[0].shape[0]), per_head[0], per_head[1])


class _Parts:
    def __init__(self, q_parts, k_parts, v_parts, tq, sk):
        self.kinds = [[kind for _, _, kind in q_parts[p]] for p in range(2)]
        self.nparts = len(q_parts[0])
        self.q_specs = [_row_spec(tq, cb) for p in range(2) for _, cb, _ in q_parts[p]]
        self.q_args = [a for p in range(2) for a, _, _ in q_parts[p]]
        self.k_specs = [_whole_spec(sk, cb) for p in range(2) for _, cb, _ in k_parts[p]]
        self.k_args = [a for p in range(2) for a, _, _ in k_parts[p]]
        self.v_specs = [_whole_spec(sk, cb) for _, cb in v_parts]
        self.v_args = [a for a, _ in v_parts]
        self.width = PAIR * self.nparts

    def split(self, refs):
        n = self.nparts
        refs = list(refs)
        q = [refs[p * n:(p + 1) * n] for p in range(2)]
        k = [refs[2 * n + p * n:2 * n + (p + 1) * n] for p in range(2)]
        v = refs[4 * n:4 * n + 2]
        return q, k, v, refs[4 * n + 2:]

    def k_block(self, k_refs, off, t):
        blks = [r[pl.ds(off, t), :] for r in k_refs]
        return blks[0] if len(blks) == 1 else jnp.concatenate(blks, axis=1)


def _softmax_fwd(q_parts, k_parts, v_parts, sq, sk, scale, causal, bias, name):
    tq = min(SOFTMAX_FWD_TILE, sq) if causal else _pick(sq, (DENSE_QUERY_TILE, SOFTMAX_TILE))
    tk = tq if causal else min(SOFTMAX_TILE, sk)
    nkv = sk // tk
    pp = _Parts(q_parts, k_parts, v_parts, tq, sk)

    def kern(*refs):
        q_refs, k_refs, v_refs, rest = pp.split(refs)
        if bias is not None:
            fc_ref, ft_ref, kmax_ref, o_ref, lse_ref = rest
            fcb = fc_ref[...]
        else:
            o_ref, lse_ref = rest
        i = pl.program_id(0)
        fold = _is_pow2(scale)
        head_on = [jnp.where(_lane_mask("pair", e, tk), 1.0, 0.0).astype(MXU_DTYPE) for e in range(2)]
        head_off = [jnp.where(_lane_mask("pair", e, tk), 0.0, 1.0).astype(MXU_DTYPE) for e in range(2)]
        lse_cols = {}
        for first in range(0, 2, FWD_PAIRS_PER_LOOP):
            pairs = list(range(first, first + FWD_PAIRS_PER_LOOP))
            heads = [2 * p + e for p in pairs for e in range(2)]
            qm = {}
            for p in pairs:
                masked_q = _masked_heads([r[...] for r in q_refs[p]], pp.kinds[p], p, MXU_DTYPE, scale if fold else None)
                qm.update({2 * p + e: masked_q[e] for e in range(2)})
            if bias is not None:
                reach = {h: _logit_reach(qm[h], kmax_ref[...], h) for h in heads}

            def block(j, carry, masked, pairs=pairs, qm=qm):
                off = _blk_off(j, tk)
                out = []
                for p in pairs:
                    kb = pp.k_block(k_refs[p], off, tk)
                    vb = v_refs[p][pl.ds(off, tk), :]
                    for e in range(2):
                        h = 2 * p + e
                        m, acc = carry[len(out)]
                        s = _dot_nt(qm[h], kb) * (LOG2E if fold else scale * LOG2E)
                        if bias is not None:
                            s = s - ft_ref[h:h + 1, pl.ds(off, tk)] * LOG2E
                        if masked:
                            s = jnp.where(_causal_mask(tq, False), s, NEG)
                        m_new = jnp.maximum(m, jnp.max(s, axis=1, keepdims=True))
                        pr = jnp.exp2(s - m_new).astype(MXU_DTYPE)
                        out.append((m_new, jnp.exp2(m - m_new) * acc + _dot(pr, vb * head_on[e] + head_off[e])))
                return tuple(out)

            carry = tuple((jnp.full((tq, 1), NEG, F32), jnp.zeros((tq, PAIR), F32)) for _ in heads)
            if causal and bias is not None:
                def alive(c, j, heads=heads, reach=reach):
                    return functools.reduce(jnp.maximum, [
                        jnp.max(reach[h] + _forget_top(ft_ref, h, _blk_off(j, tk)) - c[n][0] * LN2)
                        for n, h in enumerate(heads)])

                def step(state, block=block, alive=alive):
                    n, _, c = state
                    c = block(i - 1 - n, c, False)
                    return n + 1, alive(c, i - 1 - n), c

                carry = block(i, carry, True)
                _, _, carry = lax.while_loop(lambda st: jnp.logical_and(st[0] < i, st[1] > DEAD_LOGIT), step,
                                             (jnp.int32(0), alive(carry, i), carry))
            elif causal:
                carry = lax.fori_loop(0, i, lambda j, c, block=block: block(j, c, False), carry)
                carry = block(i, carry, True)
            else:
                for j in range(nkv):
                    carry = block(j, carry, False)
            for n, p in enumerate(pairs):
                outs = []
                for e in range(2):
                    m, acc = carry[2 * n + e]
                    l = _col(acc, HEAD_DIM * (1 - e))
                    outs.append(acc / l)
                    lse_cols[2 * p + e] = m * LN2 + jnp.log(l) + (_col(fcb, 2 * p + e) if bias is not None else 0.0)
                o_ref[:, p * PAIR:(p + 1) * PAIR] = _take_heads(outs, p)
        lse_ref[...] = _scatter_cols(lse_cols, tq)

    in_specs = pp.q_specs + pp.k_specs + pp.v_specs
    args = pp.q_args + pp.k_args + pp.v_args
    if bias is not None:
        in_specs += [_row_spec(tq, 0), pl.BlockSpec((8, sk), lambda i: (0, 0), pipeline_mode=pl.Buffered(1)),
                     pl.BlockSpec((1, PAIR), lambda i: (0, 0))]
        args += list(bias)
    return pl.pallas_call(
        kern, name=name, grid=(sq // tq,), in_specs=in_specs,
        out_specs=[_row_spec(tq, 0, GROUP_W), _row_spec(tq, 0)],
        out_shape=[jax.ShapeDtypeStruct((sq, GROUP_W), F32), jax.ShapeDtypeStruct((sq, PAIR), F32)],
        compiler_params=_att_params(True),
    )(*args)


def _softmax_bwd(q_parts, k_parts, v_parts, o, lse, do, do_blk, sq, sk, scale, causal, bias, name):
    tq = min(SOFTMAX_TILE, sq) if causal else _pick(sq, (DENSE_QUERY_TILE, SOFTMAX_TILE))
    tk = tq if causal else min(SOFTMAX_TILE, sk)
    nkv = sk // tk
    pp = _Parts(q_parts, k_parts, v_parts, tq, sk)
    quad = pp.nparts == 2
    wq = GROUP_W + (PAIR if quad else 0)

    def kern(*refs):
        q_refs, k_refs, v_refs, rest = pp.split(refs)
        if bias is not None:
            o_ref, lse_ref, do_ref, fc_ref, ft_ref, kmax_ref, dq_ref, dk_ref, dv_ref, dfq_ref, dfk_ref = rest
            fcb = fc_ref[...]
        else:
            o_ref, lse_ref, do_ref, dq_ref, dk_ref, dv_ref = rest
        i = pl.program_id(0)
        fold = _is_pow2(scale)

        @pl.when(i == 0)
        def _():
            dk_ref[...] = jnp.zeros_like(dk_ref)
            dv_ref[...] = jnp.zeros_like(dv_ref)
            if bias is not None:
                dfk_ref[...] = jnp.zeros_like(dfk_ref)

        lse_b = lse_ref[...]
        qm, dom, delta, lse_h = [], [], [], []
        for p in range(2):
            qm += _masked_heads([r[...] for r in q_refs[p]], pp.kinds[p], p, MXU_DTYPE, scale if fold else None)
            do_p = do_ref[:, p * PAIR:(p + 1) * PAIR]
            dom += _masked_heads([do_p], ["pair"], p, MXU_DTYPE)
            prod = do_p * o_ref[:, p * PAIR:(p + 1) * PAIR]
            for e in range(2):
                h = 2 * p + e
                delta.append(jnp.sum(jnp.where(_lane_mask("pair", h, tq), prod, 0.0), axis=1, keepdims=True))
                lse_h.append(_col(lse_b, h) - (_col(fcb, h) if bias is not None else 0.0))

        def block(j, carry, masked, pairs):
            off = _blk_off(j, tk)
            out = []
            for p in pairs:
                kb = pp.k_block(k_refs[p], off, tk)
                vb = v_refs[p][pl.ds(off, tk), :]
                dk_acc = jnp.zeros((tk, pp.width), F32)
                dv_acc = jnp.zeros((tk, PAIR), F32)
                for e in range(2):
                    h = 2 * p + e
                    dq, dfq = carry[len(out)]
                    s = _dot_nt(qm[h], kb)
                    if not fold:
                        s = s * scale
                    if bias is not None:
                        s = s - ft_ref[h:h + 1, pl.ds(off, tk)]
                    if masked:
                        s = jnp.where(_causal_mask(tq, False), s, NEG)
                    pr = jnp.exp(s - lse_h[h])
                    ds = pr * (_dot_nt(dom[h], vb) - delta[h])
                    dsb = (ds if fold else ds * scale).astype(MXU_DTYPE)
                    dv_acc = dv_acc + _dot_tn(pr.astype(MXU_DTYPE), dom[h])
                    dk_acc = dk_acc + _dot_tn(dsb, qm[h])
                    dq = dq + _dot(dsb, kb)
                    if bias is not None:
                        dfq = dfq + jnp.sum(ds, axis=1, keepdims=True)
                        dfk_ref[h:h + 1, pl.ds(off, tk)] -= jnp.sum(ds, axis=0, keepdims=True)
                    out.append((dq, dfq))
                dv_ref[pl.ds(off, tk), p * PAIR:(p + 1) * PAIR] += dv_acc
                dk_ref[pl.ds(off, tk), p * PAIR:(p + 1) * PAIR] += dk_acc[:, :PAIR]
                if quad:
                    dk_ref[pl.ds(off, tk), GROUP_W:] += dk_acc[:, PAIR:]
            return tuple(out)

        done = {}
        for first in range(0, 2, BWD_PAIRS_PER_LOOP):
            pairs = list(range(first, first + BWD_PAIRS_PER_LOOP))
            heads = [2 * p + e for p in pairs for e in range(2)]
            carry = tuple((jnp.zeros((tq, pp.width), F32), jnp.zeros((tq, 1), F32)) for _ in heads)
            if causal and bias is not None:
                reach = {h: _logit_reach(qm[h], kmax_ref[...], h) - lse_h[h] for h in heads}

                def alive(j, heads=heads, reach=reach):
                    return functools.reduce(jnp.maximum, [
                        jnp.max(reach[h] + _forget_top(ft_ref, h, _blk_off(j, tk))) for h in heads])

                def step(state, pairs=pairs, alive=alive):
                    n, _, c = state
                    return n + 1, alive(i - 1 - n), block(i - 1 - n, c, False, pairs)

                carry = block(i, carry, True, pairs)
                _, _, carry = lax.while_loop(lambda st: jnp.logical_and(st[0] < i, st[1] > DEAD_LOGIT), step,
                                             (jnp.int32(0), alive(i), carry))
            elif causal:
                carry = lax.fori_loop(0, i, lambda j, c, pairs=pairs: block(j, c, False, pairs), carry)
                carry = block(i, carry, True, pairs)
            else:
                for j in range(nkv):
                    carry = block(j, carry, False, pairs)
            done.update(zip(heads, carry))
        carry = [done[h] for h in range(N_HEADS)]
        dqs = [c[0] * scale if fold else c[0] for c in carry]
        for p in range(2):
            dq_ref[:, p * PAIR:(p + 1) * PAIR] = _take_heads([dqs[2 * p + e][:, :PAIR] for e in range(2)], p)
        if quad:
            dq_ref[:, GROUP_W:] = sum(jnp.where(_lane_mask("quad", h, tq), dqs[h][:, PAIR:], 0.0)
                                      for h in range(N_HEADS))
        if bias is not None:
            dfq_ref[...] = _scatter_cols({h: carry[h][1] for h in range(N_HEADS)}, tq)

    acc_spec = lambda rows, width: pl.BlockSpec((rows, width), lambda i: (0, 0), pipeline_mode=pl.Buffered(1))
    in_specs = pp.q_specs + pp.k_specs + pp.v_specs + [_row_spec(tq, 0, GROUP_W), _row_spec(tq, 0),
                                                       _row_spec(tq, do_blk, GROUP_W)]
    args = pp.q_args + pp.k_args + pp.v_args + [o, lse, do]
    out_specs = [_row_spec(tq, 0, wq), acc_spec(sk, wq), acc_spec(sk, GROUP_W)]
    out_shape = [jax.ShapeDtypeStruct((sq, wq), F32), jax.ShapeDtypeStruct((sk, wq), F32),
                 jax.ShapeDtypeStruct((sk, GROUP_W), F32)]
    if bias is not None:
        in_specs += [_row_spec(tq, 0), pl.BlockSpec((8, sk), lambda i: (0, 0), pipeline_mode=pl.Buffered(1)),
                     pl.BlockSpec((1, PAIR), lambda i: (0, 0))]
        args += list(bias)
        out_specs += [_row_spec(tq, 0), acc_spec(8, sk)]
        out_shape += [jax.ShapeDtypeStruct((sq, PAIR), F32), jax.ShapeDtypeStruct((8, sk), F32)]
    return pl.pallas_call(
        kern, name=name, grid=(sq // tq,), in_specs=in_specs, out_specs=out_specs, out_shape=out_shape,
        compiler_params=_att_params(False),
    )(*args)


def _sb_logs(qh, kb, valid):
    z = _dot_nt(qh, kb)
    sp = _softplus(z)
    lk = -sp
    if valid is not None:
        lk = jnp.where(valid, lk, 0.0)
    return lk, z - sp


def _sb_valid(d, tq, tk):
    r = lax.broadcasted_iota(jnp.int32, (tq, tk), 0)
    c = lax.broadcasted_iota(jnp.int32, (tq, tk), 1)
    return c + d * tk < r


def _tri_sums(xs, tri):
    t = xs[0].shape[0]
    pieces = [_split2(x) for x in xs]
    hi = _dot(jnp.concatenate([pc[0] for pc in pieces], axis=0), tri)
    lo = _dot(jnp.concatenate([pc[1] for pc in pieces], axis=0), tri)
    return [hi[n * t:(n + 1) * t] + lo[n * t:(n + 1) * t] for n in range(len(xs))]


def _sb_fwd(src, q_blk, k_blk, v_blk, s, scale, name):
    assert _is_pow2(scale)
    tq, t = min(SB_QUERY_TILE, s), min(ATT_TILE, s)
    slots = -(-(s // t) // SB_SLOT) * SB_SLOT
    width = N_HEADS * slots
    band = tq // t
    pair = lambda blk: [[(src, blk + p, "pair")] for p in range(2)]
    pp = _Parts(pair(q_blk), pair(k_blk), [(src, v_blk + p) for p in range(2)], tq, s)

    def kern(*refs):
        q_refs, k_refs, v_refs, (tri_ref, o_ref, rm_ref, cnt_ref) = pp.split(refs)
        i = pl.program_id(0)
        tri = tri_ref[...]
        lane = lax.broadcasted_iota(jnp.int32, (tq, width), 1)
        qm = []
        for p in range(2):
            qm += _masked_heads([q_refs[p][0][...]], ["pair"], p, MXU_DTYPE, scale)

        def block(j, carry, valid):
            accs, rights, rm = carry
            off = _blk_off(j, t)
            kbs = [k_refs[p][0][pl.ds(off, t), :] for p in range(2)]
            vbs = [v_refs[p][pl.ds(off, t), :] for p in range(2)]
            logs = [_sb_logs(qm[h], kbs[h // 2], valid) for h in range(N_HEADS)]
            tails = _tri_sums([lg[0] for lg in logs], tri)
            new_acc, new_right = [], []
            for h in range(N_HEADS):
                lk, ls = logs[h]
                w = jnp.exp(ls + tails[h] + rights[h])
                if valid is not None:
                    w = jnp.where(valid, w, 0.0)
                new_acc.append(accs[h] + _dot(w.astype(MXU_DTYPE), vbs[h // 2]))
                rm = rm + jnp.where(lane == slots * h + j, rights[h], 0.0)
                new_right.append(rights[h] + jnp.sum(lk, axis=1, keepdims=True))
            return tuple(new_acc), tuple(new_right), rm

        carry = (tuple(jnp.zeros((tq, PAIR), F32) for _ in range(N_HEADS)),
                 tuple(jnp.zeros((tq, 1), F32) for _ in range(N_HEADS)), jnp.zeros((tq, width), F32))
        for d in reversed(range(band)):
            carry = block(band * i + d, carry, _sb_valid(d, tq, t))

        def alive(c):
            return functools.reduce(jnp.maximum, [jnp.max(r) for r in c[1]])

        def step(state):
            n, _, c = state
            c = block(band * i - 1 - n, c, None)
            return n + 1, alive(c), c

        n_done, _, carry = lax.while_loop(lambda st: jnp.logical_and(st[0] < band * i, st[1] > EXP_UNDERFLOW),
                                          step, (jnp.int32(0), alive(carry), carry))
        cnt_ref[i] = n_done
        for p in range(2):
            o_ref[:, p * PAIR:(p + 1) * PAIR] = _take_heads([carry[0][2 * p + e] for e in range(2)], p)
        rm_ref[...] = carry[2]

    return pl.pallas_call(
        kern, name=name, grid=(s // tq,),
        in_specs=pp.q_specs + pp.k_specs + pp.v_specs + [pl.BlockSpec((t, t), lambda i: (0, 0))],
        out_specs=[_row_spec(tq, 0, GROUP_W), _row_spec(tq, 0, width), pl.BlockSpec(memory_space=pltpu.SMEM)],
        out_shape=[jax.ShapeDtypeStruct((s, GROUP_W), F32), jax.ShapeDtypeStruct((s, width), F32),
                   jax.ShapeDtypeStruct((s // tq,), jnp.int32)],
        compiler_params=_att_params(False),
    )(*(pp.q_args + pp.k_args + pp.v_args + [_tri(t, "row_gt_col")]))


def _sb_bwd(src, q_blk, k_blk, v_blk, do, do_blk, rm, visited, s, scale, name):
    assert _is_pow2(scale)
    tq, t = min(SB_QUERY_TILE, s), min(ATT_TILE, s)
    band = tq // t
    pair = lambda blk: [[(src, blk + p, "pair")] for p in range(2)]
    pp = _Parts(pair(q_blk), pair(k_blk), [(src, v_blk + p) for p in range(2)], tq, s)

    def kern(*refs):
        q_refs, k_refs, v_refs, (do_ref, rm_ref, tri_ref, pre_ref, cnt_ref, dq_ref, dk_ref, dv_ref) = pp.split(refs)
        i = pl.program_id(0)

        @pl.when(i == 0)
        def _():
            dk_ref[...] = jnp.zeros_like(dk_ref)
            dv_ref[...] = jnp.zeros_like(dv_ref)

        rmb = rm_ref[...]
        tri = tri_ref[...]
        pre = pre_ref[...]
        qm, dom = [], []
        for p in range(2):
            qm += _masked_heads([q_refs[p][0][...]], ["pair"], p, MXU_DTYPE, scale)
            dom += _masked_heads([do_ref[:, p * PAIR:(p + 1) * PAIR]], ["pair"], p, MXU_DTYPE)

        def block(j, carry, valid):
            dqs, lefts = carry
            off = _blk_off(j, t)
            kbs = [k_refs[p][0][pl.ds(off, t), :] for p in range(2)]
            vbs = [v_refs[p][pl.ds(off, t), :] for p in range(2)]
            logs = [_sb_logs(qm[h], kbs[h // 2], valid) for h in range(N_HEADS)]
            tails = _tri_sums([lg[0] for lg in logs], tri)
            ws, gs = [], []
            for h in range(N_HEADS):
                lk, ls = logs[h]
                w = jnp.exp(ls + tails[h] + _col(rmb, (rm.shape[1] // N_HEADS) * h + j))
                if valid is not None:
                    w = jnp.where(valid, w, 0.0)
                ws.append(w)
                gs.append(_dot_nt(dom[h], vbs[h // 2]) * w)
            prefix = _tri_sums(gs, pre)
            new_dq, new_left = [], []
            dk_acc = [jnp.zeros((t, PAIR), F32) for _ in range(2)]
            dv_acc = [jnp.zeros((t, PAIR), F32) for _ in range(2)]
            for h in range(N_HEADS):
                lk, ls = logs[h]
                sig = jnp.exp(ls)
                dz = gs[h] * (1.0 - sig) - sig * (prefix[h] + lefts[h])
                if valid is not None:
                    dz = jnp.where(valid, dz, 0.0)
                dzb = dz.astype(MXU_DTYPE)
                dv_acc[h // 2] = dv_acc[h // 2] + _dot_tn(ws[h].astype(MXU_DTYPE), dom[h])
                dk_acc[h // 2] = dk_acc[h // 2] + _dot_tn(dzb, qm[h])
                new_dq.append(dqs[h] + _dot(dzb, kbs[h // 2]))
                new_left.append(lefts[h] + jnp.sum(gs[h], axis=1, keepdims=True))
            for p in range(2):
                dv_ref[pl.ds(off, t), p * PAIR:(p + 1) * PAIR] += dv_acc[p]
                dk_ref[pl.ds(off, t), p * PAIR:(p + 1) * PAIR] += dk_acc[p]
            return tuple(new_dq), tuple(new_left)

        carry = (tuple(jnp.zeros((tq, PAIR), F32) for _ in range(N_HEADS)),
                 tuple(jnp.zeros((tq, 1), F32) for _ in range(N_HEADS)))
        carry = lax.fori_loop(band * i - cnt_ref[i], band * i, lambda j, c: block(j, c, None), carry)
        for d in range(band):
            carry = block(band * i + d, carry, _sb_valid(d, tq, t))
        for p in range(2):
            dq_ref[:, p * PAIR:(p + 1) * PAIR] = _take_heads([carry[0][2 * p + e] * scale for e in range(2)], p)

    mspec = pl.BlockSpec((t, t), lambda i: (0, 0))
    acc_spec = pl.BlockSpec((s, GROUP_W), lambda i: (0, 0), pipeline_mode=pl.Buffered(1))
    return pl.pallas_call(
        kern, name=name, grid=(s // tq,),
        in_specs=pp.q_specs + pp.k_specs + pp.v_specs + [_row_spec(tq, do_blk, GROUP_W), _row_spec(tq, 0, rm.shape[1]), mspec, mspec,
                                                         pl.BlockSpec(memory_space=pltpu.SMEM)],
        out_specs=[_row_spec(tq, 0, GROUP_W), acc_spec, acc_spec],
        out_shape=[jax.ShapeDtypeStruct((s, GROUP_W), F32)] * 3,
        compiler_params=_att_params(False),
    )(*(pp.q_args + pp.k_args + pp.v_args + [do, rm, _tri(t, "row_gt_col"), _tri(t, "row_lt_col"), visited]))


def _split_w_in(w):
    col = lambda n: w[:, _OFF[n]:_OFF[n + 1]]
    wa = jnp.concatenate([col(0), col(1), col(2), col(4), col(5), col(6), col(10)], axis=1)
    misc = jnp.concatenate([col(3), col(9), jnp.zeros((w.shape[0], 128 - 4 - MLA_ROPE), w.dtype)], axis=1)
    wb = jnp.concatenate([col(11), col(7), col(8), misc], axis=1)
    return wa, wb


def _merge_dw_in(dwp):
    a = lambda n: dwp[:, n * GROUP_W:(n + 1) * GROUP_W]
    b0 = PA_COLS
    gate = dwp[:, b0:b0 + 1024]
    cq = dwp[:, b0 + 1024:b0 + 1280]
    ckv = dwp[:, b0 + 1280:b0 + 1408]
    flog = dwp[:, b0 + 1408:b0 + 1412]
    krot = dwp[:, b0 + 1408 + MISC_KROT:b0 + 1408 + MISC_KROT + MLA_ROPE]
    return jnp.concatenate([a(0), a(1), a(2), flog, a(3), a(4), a(5), cq, ckv, krot, a(6), gate], axis=1)


def _heads_first(w, per_head, first):
    r = w.shape[0]
    w3 = w.reshape(r, N_HEADS, per_head)
    return jnp.concatenate([w3[:, :, :first].reshape(r, -1), w3[:, :, first:].reshape(r, -1)], axis=1)


def _heads_interleaved(w, per_head, first):
    r = w.shape[0]
    a = w[:, :N_HEADS * first].reshape(r, N_HEADS, first)
    b = w[:, N_HEADS * first:].reshape(r, N_HEADS, per_head - first)
    return jnp.concatenate([a, b], axis=2).reshape(r, N_HEADS * per_head)


def _pad_rows8(a):
    return a[:, :8].T


def _local_step(x2, mem2, tgt, p):
    s = x2.shape[0]
    nm = mem2.shape[0]
    head_scale = HEAD_DIM ** -0.5
    mla_scale = (HEAD_DIM + MLA_ROPE) ** -0.5
    tabs = _rope_tables(s)
    mats = _rope_matrices()
    pairs = lambda arr, blk: [[(arr, blk + q, "pair")] for q in range(2)]
    vals = lambda arr, blk: [(arr, blk + q) for q in range(2)]

    h, hb = _ln_fwd(x2, p["ln_in_g"], p["ln_in_b"], "ln_in_fwd")
    _, memn_b = _ln_fwd(mem2, p["mem_ln_g"], p["mem_ln_b"], "ln_mem_fwd")

    saved = []
    for l in range(DEPTH):
        wa, wb = _split_w_in(p["w_in"][l])
        wp = jnp.concatenate([wa, wb], axis=1)
        wq_up = _heads_first(p["w_mla_q_up"][l], HEAD_DIM + MLA_ROPE, HEAD_DIM)
        wkv_up = _heads_first(p["w_mla_kv_up"][l], 2 * HEAD_DIM, HEAD_DIM)
        bias_row = jnp.pad(p["b_forget"][l], (0, 128 - N_HEADS)).reshape(1, 128)
        pa, pb, fox_kmax = _proj_in(hb, wa, wb, GROUP_W)

        fc = _forget_fwd(pb, bias_row)
        fbias = (fc, _pad_rows8(fc), fox_kmax)
        o_fox, lse_fox = _softmax_fwd(pairs(pa, 0), pairs(pa, 2), vals(pa, 4), s, s, head_scale, True, fbias,
                                      "fox_fwd")
        o_sb, *rm_sb = _sb_fwd(pa, 6, 8, 10, s, head_scale, "sb_fwd")

        qfull, cqn = _mla_q_fwd(pb, p["mla_q_norm_g"][l], wq_up, tabs, mats)
        kv, ckvn, kr4 = _mla_kv_fwd(pb, p["mla_kv_norm_g"][l], wkv_up, tabs, mats)
        mla_q = [[(qfull, q, "pair"), (qfull, 2, "quad")] for q in range(2)]
        mla_k = [[(kv, q, "pair"), (kr4, 0, "quad")] for q in range(2)]
        o_mla, lse_mla = _softmax_fwd(mla_q, mla_k, vals(kv, 2), s, s, mla_scale, True, None, "mla_fwd")

        mkv = _matmul(memn_b, p["w_mem_kv"][l], MXU_DTYPE, "mem_kv")
        o_mem, lse_mem = _softmax_fwd(pairs(pa, 12), pairs(mkv, 0), vals(mkv, 2), s, nm, head_scale, False, None,
                                      "mem_fwd")

        groups = (o_fox, o_sb, o_mla, o_mem)
        if l < DEPTH - 1:
            gated, u, h_next, hb_next = _gate_out_proj_ln(groups, pb, p["w_out"][l], h, p["ln_g"][l], p["ln_b"][l])
        else:
            gated, u, dh, sq_cols = _gate_out_proj_ln(groups, pb, p["w_out"][l], h, p["ln_g"][l], p["ln_b"][l], tgt)
        saved.append(dict(u=u, hb=hb, wp=wp, wq_up=wq_up, wkv_up=wkv_up, bias_row=bias_row, pa=pa, pb=pb,
                          fbias=fbias, lse_fox=lse_fox, rm_sb=rm_sb, cqn=cqn, ckvn=ckvn, mla_q=mla_q, mla_k=mla_k,
                          kv=kv, lse_mla=lse_mla, mkv=mkv, lse_mem=lse_mem, groups=groups, gated=gated))
        if l < DEPTH - 1:
            h, hb = h_next, hb_next

    loss_sum = jnp.sum(sq_cols)

    grads = {k: [None] * DEPTH for k in ("w_in", "b_forget", "mla_q_norm_g", "w_mla_q_up", "mla_kv_norm_g",
                                         "w_mla_kv_up", "w_mem_kv", "w_out", "ln_g", "ln_b")}
    dmemn = []
    dy1, dy2, c1 = dh, None, 1.0
    for l in reversed(range(DEPTH)):
        r = saved[l]
        pa, pb = r["pa"], r["pb"]
        o_fox, o_sb, o_mla, o_mem = r["groups"]
        du, du_b, dmixed, dgate_b, dg, db = _ln_bwd(r["u"], p["ln_g"][l], dy1, dy2, c1, "ln_gate_bwd",
                                                    (p["w_out"][l], r["groups"], pb))
        grads["ln_g"][l], grads["ln_b"][l] = dg[0], db[0]
        grads["w_out"][l] = _matmul(r["gated"], du_b, F32, "out_proj_dw", "tn")

        dfq, dfk, dfv, dfc_q, dfc_k = _softmax_bwd(pairs(pa, 0), pairs(pa, 2), vals(pa, 4), o_fox, r["lse_fox"],
                                                   dmixed, 0, s, s, head_scale, True, r["fbias"], "fox_bwd")
        dmisc_f, dbf = _forget_bwd(pb, r["bias_row"], dfc_q + jnp.pad(dfc_k.T, ((0, 0), (0, 128 - 8))))
        grads["b_forget"][l] = dbf[0, :N_HEADS]

        dsq, dsk, dsv = _sb_bwd(pa, 6, 8, 10, dmixed, 1, *r["rm_sb"], s, head_scale, "sb_bwd")

        dqm, dkm, dvm = _softmax_bwd(r["mla_q"], r["mla_k"], vals(r["kv"], 2), o_mla, r["lse_mla"], dmixed, 2,
                                     s, s, mla_scale, True, None, "mla_bwd")
        dq_mla_b, dcq_b, dgq = _mla_q_bwd(dqm, pb, p["mla_q_norm_g"][l], r["wq_up"], tabs, mats)
        grads["w_mla_q_up"][l] = _heads_interleaved(_matmul(r["cqn"], dq_mla_b, F32, "q_up_dw", "tn"),
                                                    HEAD_DIM + MLA_ROPE, HEAD_DIM)
        grads["mla_q_norm_g"][l] = dgq[0]
        dkv_b, dckv_b, dmisc_k, dgkv = _mla_kv_bwd(dkm, dvm, pb, p["mla_kv_norm_g"][l], r["wkv_up"], tabs, mats)
        grads["w_mla_kv_up"][l] = _heads_interleaved(_matmul(r["ckvn"], dkv_b, F32, "kv_up_dw", "tn"),
                                                     2 * HEAD_DIM, HEAD_DIM)
        grads["mla_kv_norm_g"][l] = dgkv[0]

        dmq, dmk, dmv = _softmax_bwd(pairs(pa, 12), pairs(r["mkv"], 0), vals(r["mkv"], 2), o_mem, r["lse_mem"],
                                     dmixed, 3, s, nm, head_scale, False, None, "mem_bwd")
        dmkv_b = jnp.concatenate([dmk, dmv], axis=1).astype(MXU_DTYPE)
        grads["w_mem_kv"][l] = _matmul(memn_b, dmkv_b, F32, "mem_kv_dw", "tn")
        dmemn.append(_matmul(dmkv_b, p["w_mem_kv"][l], F32, "mem_kv_dx", "nt"))

        dp = [dfq, dfk, dfv, dsq, dsk, dsv, dmq, dgate_b, dcq_b, dckv_b, (dmisc_f, dmisc_k)]
        dhproj = _proj_dx(dp, r["wp"])
        grads["w_in"][l] = _merge_dw_in(_proj_dw(r["hb"], dp))
        dy1, dy2, c1 = du, dhproj, ALPHA

    dx, dg_in, db_in = _ln_bwd(x2, p["ln_in_g"], dy1, dy2, c1, "ln_in_bwd")
    _, dg_mem, db_mem = _ln_bwd(mem2, p["mem_ln_g"], dmemn[0], dmemn[1], 1.0, "ln_mem_bwd")
    out = {k: jnp.stack(v) for k, v in grads.items()}
    out.update(ln_in_g=dg_in[0], ln_in_b=db_in[0], mem_ln_g=dg_mem[0], mem_ln_b=db_mem[0])
    return loss_sum, dx, out


WIDE = "w_in"
FLAT_NAMES = ("w_out", "w_mem_kv", "w_mla_q_up", "w_mla_kv_up")
FLAT_ROWS = 896
BIG_NAMES = (WIDE,) + FLAT_NAMES
BIG_AXIS = dict(w_in=2, w_out=1, w_mem_kv=1, w_mla_q_up=2, w_mla_kv_up=2)
SMALL_NAMES = ("ln_in_g", "ln_in_b", "mem_ln_g", "mem_ln_b", "ln_g", "ln_b", "b_forget", "mla_q_norm_g",
               "mla_kv_norm_g")
ALL_NAMES = ("ln_in_g", "ln_in_b", "mem_ln_g", "mem_ln_b", "w_in", "b_forget", "mla_q_norm_g", "w_mla_q_up",
             "mla_kv_norm_g", "w_mla_kv_up", "w_mem_kv", "w_out", "ln_g", "ln_b")
N_CHIPS = 4
N_DEV = 8


def _rows_of(shape):
    rows = -(-int(np.prod(shape)) // LANES)
    return -(-rows // PACK_ALIGN) * PACK_ALIGN


def _pack(arrs, rows):
    parts = []
    for a in arrs:
        f = a.reshape(-1)
        n = _rows_of(a.shape) * LANES
        parts.append(jnp.pad(f, (0, n - f.shape[0])).reshape(-1, LANES))
    used = sum(q.shape[0] for q in parts)
    if rows > used:
        parts.append(jnp.zeros((rows - used, LANES), parts[0].dtype))
    return jnp.concatenate(parts, axis=0)


def _unpack(buf, shapes):
    out, r = [], 0
    for shp in shapes:
        n = _rows_of(shp)
        out.append(buf[r:r + n].reshape(-1)[:int(np.prod(shp))].reshape(shp))
        r += n
    return out


def _sharded_pair(get):
    wide = get(WIDE)
    return [wide.reshape(-1, wide.shape[-1]), _pack([get(n) for n in FLAT_NAMES], FLAT_ROWS)]


HBM_SPEC = pl.BlockSpec(memory_space=pltpu.HBM)


def _gather_weights(shards):
    n = len(shards)

    def body(*refs):
        w_refs, out_refs, (send_sems, recv_sems, local_sems) = refs[:n], refs[n:2 * n], refs[2 * n:]
        x, y, c = (lax.axis_index(a) for a in MESH_AXES)
        me, sibling = 2 * x + y, (x, y, 1 - c)
        chips = [(1 - x, y), (x, 1 - y), (1 - x, 1 - y)]
        local, first, passed = [], [], []
        for a in range(n):
            w_ref, out_ref, half = w_refs[a], out_refs[a], shards[a].shape[0] // 2

            def part(chip, core, out_ref=out_ref, half=half):
                return out_ref.at[chip, pl.ds(core * half, half)]

            def copy(k, src, dst, to, a=a):
                return pltpu.make_async_remote_copy(
                    src_ref=src, dst_ref=dst, send_sem=send_sems.at[6 * a + k], recv_sem=recv_sems.at[6 * a + k],
                    device_id=to, device_id_type=pl.DeviceIdType.MESH)

            local.append(pltpu.make_async_copy(w_ref, out_ref.at[me], local_sems.at[a]))
            local[-1].start()
            mine = [copy(k, w_ref.at[pl.ds(c * half, half)], part(me, c), (px, py, c))
                    for k, (px, py) in enumerate(chips)]
            for cp in mine:
                cp.start()
            first.append((mine, part, copy))
        for mine, part, copy in first:
            for k, (px, py) in enumerate(chips):
                copy(k, part(me, c), part(2 * px + py, c), (px, py, c)).wait_recv()
                passed.append(copy(3 + k, part(2 * px + py, c), part(2 * px + py, c), sibling))
                passed[-1].start()
        for mine, part, copy in first:
            for k, (px, py) in enumerate(chips):
                copy(3 + k, part(me, c), part(2 * px + py, 1 - c), sibling).wait_recv()
        for cp in [cp for mine, _, _ in first for cp in mine] + passed:
            cp.wait_send()
        for cp in local:
            cp.wait()

    return pl.pallas_call(
        body, name="gather_weights",
        out_shape=[jax.ShapeDtypeStruct((N_CHIPS,) + s.shape, s.dtype) for s in shards],
        in_specs=[HBM_SPEC] * n, out_specs=[HBM_SPEC] * n,
        scratch_shapes=[pltpu.SemaphoreType.DMA((6 * n,)), pltpu.SemaphoreType.DMA((6 * n,)),
                        pltpu.SemaphoreType.DMA((n,))],
    )(*shards)


def _exchange_grads(bigs, small):
    nb = len(bigs)
    halves = [b.shape[1] // 2 for b in bigs]
    chunks = [_pick(h, (128, 64, 32, 16)) for h in halves]

    def body(*refs):
        big_refs, small_ref = refs[:nb], refs[nb]
        sum_refs, out_refs = refs[nb + 1:2 * nb + 1], refs[2 * nb + 1:3 * nb + 2]
        scratch = refs[3 * nb + 2:]
        land_bufs, sum_bufs = scratch[:nb], scratch[nb:2 * nb]
        send_sems, recv_sems, local_sems, load_sems, swap_send, swap_recv, keep_sems = scratch[2 * nb:]
        x, y, c = (lax.axis_index(a) for a in MESH_AXES)
        me, my_chip = 4 * x + 2 * y + c, 2 * x + y
        flips = [(fx, fy, fc) for fx in (0, 1) for fy in (0, 1) for fc in (0, 1) if fx or fy or fc]
        peers = [(1 - x if fx else x, 1 - y if fy else y, 1 - c if fc else c) for fx, fy, fc in flips]

        sources = [lambda chip, core, r=big_refs[a], half=bigs[a].shape[1] // 2: r.at[chip, pl.ds(core * half, half)]
                   for a in range(nb)] + [lambda chip, core: small_ref]

        def copy(a, k, src, slot, to):
            return pltpu.make_async_remote_copy(
                src_ref=src, dst_ref=out_refs[a].at[slot], send_sem=send_sems.at[7 * a + k],
                recv_sem=recv_sems.at[7 * a + k], device_id=to, device_id_type=pl.DeviceIdType.MESH)

        own = [pltpu.make_async_copy(src(my_chip, c), out_refs[a].at[me], local_sems.at[a])
               for a, src in enumerate(sources)]
        for cp in own:
            cp.start()
        sends = [copy(a, k, src(2 * px + py, pc), me, (px, py, pc))
                 for a, src in enumerate(sources) for k, (px, py, pc) in enumerate(peers)]
        for cp in sends:
            cp.start()
        for a, src in enumerate(sources):
            for k, (px, py, pc) in enumerate(peers):
                copy(a, k, src(my_chip, c), 4 * px + 2 * py + pc, (px, py, pc)).wait_recv()
        for cp in sends:
            cp.wait_send()
        for cp in own:
            cp.wait()

        tails = []
        loads = [pltpu.make_async_copy(out_refs[a], land_bufs[a], load_sems.at[a]) for a in range(nb)]
        for load in loads:
            load.start()
        for a in range(nb):
            loads[a].wait()
            for r0 in range(0, halves[a], chunks[a]):
                rows = pl.ds(r0, chunks[a])
                total = land_bufs[a][0, rows, :].astype(F32)
                for d in range(1, N_DEV):
                    total = total + land_bufs[a][d, rows, :].astype(F32)
                sum_bufs[a][rows, :] = total
            keep = pltpu.make_async_copy(sum_bufs[a], sum_refs[a].at[c], keep_sems.at[a])
            give = pltpu.make_async_remote_copy(
                src_ref=sum_bufs[a], dst_ref=sum_refs[a].at[c], send_sem=swap_send.at[a], recv_sem=swap_recv.at[a],
                device_id=(x, y, 1 - c), device_id_type=pl.DeviceIdType.MESH)
            keep.start()
            give.start()
            tails.append((keep, give))
        for a, (keep, give) in enumerate(tails):
            pltpu.make_async_remote_copy(
                src_ref=sum_bufs[a], dst_ref=sum_refs[a].at[1 - c], send_sem=swap_send.at[a], recv_sem=swap_recv.at[a],
                device_id=(x, y, 1 - c), device_id_type=pl.DeviceIdType.MESH).wait_recv()
            give.wait_send()
            keep.wait()

    return pl.pallas_call(
        body, name="exchange_grads",
        out_shape=[jax.ShapeDtypeStruct((2, h, b.shape[2]), F32) for h, b in zip(halves, bigs)]
        + [jax.ShapeDtypeStruct((N_DEV, h, b.shape[2]), b.dtype) for h, b in zip(halves, bigs)]
        + [jax.ShapeDtypeStruct((N_DEV,) + small.shape, small.dtype)],
        in_specs=[HBM_SPEC] * (nb + 1), out_specs=[HBM_SPEC] * (2 * nb + 1),
        scratch_shapes=[pltpu.VMEM((N_DEV, h, b.shape[2]), b.dtype) for h, b in zip(halves, bigs)]
        + [pltpu.VMEM((h, b.shape[2]), F32) for h, b in zip(halves, bigs)]
        + [pltpu.SemaphoreType.DMA((7 * (nb + 1),)), pltpu.SemaphoreType.DMA((7 * (nb + 1),)),
           pltpu.SemaphoreType.DMA((nb + 1,)), pltpu.SemaphoreType.DMA((nb,)), pltpu.SemaphoreType.DMA((nb,)),
           pltpu.SemaphoreType.DMA((nb,)), pltpu.SemaphoreType.DMA((nb,))],
        compiler_params=pltpu.CompilerParams(vmem_limit_bytes=VMEM_LIMIT),
    )(*bigs, small)


def _adamw(parts, w, m, v, name):
    rows, width = w.shape
    n_parts = parts.shape[0]
    tile = _pick(rows, (128, 16, 8))
    bc1 = 1.0 - ADAM_B1 ** ADAM_STEP
    bc2 = 1.0 - ADAM_B2 ** ADAM_STEP

    def kern(p_ref, w_ref, m_ref, v_ref, g_ref, d_ref, nm_ref, nv_ref):
        g = p_ref[0].astype(F32)
        for d in range(1, n_parts):
            g = g + p_ref[d].astype(F32)
        nm = ADAM_B1 * m_ref[...] + (1.0 - ADAM_B1) * g
        nv = ADAM_B2 * v_ref[...] + (1.0 - ADAM_B2) * (g * g)
        g_ref[...] = g
        nm_ref[...] = nm
        nv_ref[...] = nv
        d_ref[...] = -ADAM_LR * ((nm / bc1) / (jnp.sqrt(nv / bc2) + ADAM_EPS) + ADAM_WD * w_ref[...])

    spec = pl.BlockSpec((tile, width), lambda i: (i, 0))
    return pl.pallas_call(
        kern, name=name, grid=(rows // tile,),
        in_specs=[pl.BlockSpec((n_parts, tile, width), lambda i: (0, i, 0)), spec, spec, spec],
        out_specs=[spec] * 4, out_shape=[jax.ShapeDtypeStruct((rows, width), F32)] * 4,
        compiler_params=pltpu.CompilerParams(dimension_semantics=("parallel",), vmem_limit_bytes=VMEM_LIMIT),
    )(parts, w, m, v)


def kernel(x, mem, ln_in_g, ln_in_b, mem_ln_g, mem_ln_b, w_in, b_forget, mla_q_norm_g, w_mla_q_up, mla_kv_norm_g, w_mla_kv_up, w_mem_kv, w_out, ln_g, ln_b, loss_target, m_ln_in_g, m_ln_in_b, m_mem_ln_g, m_mem_ln_b, m_w_in, m_b_forget, m_mla_q_norm_g, m_w_mla_q_up, m_mla_kv_norm_g, m_w_mla_kv_up, m_w_mem_kv, m_w_out, m_ln_g, m_ln_b, v_ln_in_g, v_ln_in_b, v_mem_ln_g, v_mem_ln_b, v_w_in, v_b_forget, v_mla_q_norm_g, v_w_mla_q_up, v_mla_kv_norm_g, v_w_mla_kv_up, v_w_mem_kv, v_w_out, v_ln_g, v_ln_b):
    w = dict(ln_in_g=ln_in_g, ln_in_b=ln_in_b, mem_ln_g=mem_ln_g, mem_ln_b=mem_ln_b, w_in=w_in, b_forget=b_forget,
             mla_q_norm_g=mla_q_norm_g, w_mla_q_up=w_mla_q_up, mla_kv_norm_g=mla_kv_norm_g,
             w_mla_kv_up=w_mla_kv_up, w_mem_kv=w_mem_kv, w_out=w_out, ln_g=ln_g, ln_b=ln_b)
    mo = dict(ln_in_g=m_ln_in_g, ln_in_b=m_ln_in_b, mem_ln_g=m_mem_ln_g, mem_ln_b=m_mem_ln_b, w_in=m_w_in,
              b_forget=m_b_forget, mla_q_norm_g=m_mla_q_norm_g, w_mla_q_up=m_w_mla_q_up,
              mla_kv_norm_g=m_mla_kv_norm_g, w_mla_kv_up=m_w_mla_kv_up, w_mem_kv=m_w_mem_kv, w_out=m_w_out,
              ln_g=m_ln_g, ln_b=m_ln_b)
    vo = dict(ln_in_g=v_ln_in_g, ln_in_b=v_ln_in_b, mem_ln_g=v_mem_ln_g, mem_ln_b=v_mem_ln_b, w_in=v_w_in,
              b_forget=v_b_forget, mla_q_norm_g=v_mla_q_norm_g, w_mla_q_up=v_w_mla_q_up,
              mla_kv_norm_g=v_mla_kv_norm_g, w_mla_kv_up=v_w_mla_kv_up, w_mem_kv=v_w_mem_kv, w_out=v_w_out,
              ln_g=v_ln_g, ln_b=v_ln_b)
    flat_shapes = [w[n].shape for n in FLAT_NAMES]
    small_shapes = [w[n].shape for n in SMALL_NAMES]

    got_wide, got_flat = _gather_weights(_sharded_pair(lambda n: w[n].astype(MXU_DTYPE)))
    full = dict(w)
    full[WIDE] = jnp.concatenate([got_wide[j] for j in range(N_CHIPS)], axis=1).reshape(
        w[WIDE].shape[:2] + (N_CHIPS * w[WIDE].shape[2],))
    per_chip = [_unpack(got_flat[j], flat_shapes) for j in range(N_CHIPS)]
    for idx, n in enumerate(FLAT_NAMES):
        full[n] = jnp.concatenate([per_chip[j][idx] for j in range(N_CHIPS)], axis=BIG_AXIS[n])

    loss_sum, dx, g = _local_step(x[0], mem[0], loss_target[0], full)
    loss = lax.psum(loss_sum * (0.5 / D_MODEL), MESH_AXES)

    def shard_of(n, j):
        ax, size = BIG_AXIS[n], w[n].shape[BIG_AXIS[n]]
        return lax.slice_in_dim(g[n], j * size, (j + 1) * size, axis=ax).astype(MXU_DTYPE)

    per_dest = [_sharded_pair(lambda n, j=j: shard_of(n, j)) for j in range(N_CHIPS)]
    bigs = [jnp.stack([per_dest[j][a] for j in range(N_CHIPS)]) for a in range(2)]
    exchanged = _exchange_grads(bigs, _pack([g[n] for n in SMALL_NAMES], SMALL_ROWS))
    halves, small_parts = exchanged[:2], exchanged[-1]

    res = []
    for a, (grad, nm) in enumerate(zip(halves, ("adamw_wide", "adamw_flat"))):
        state = [_sharded_pair(lambda n, src=src: src[n])[a] for src in (w, mo, vo)]
        res.append(_adamw(grad.reshape((1,) + state[0].shape), *state, nm))
    res_small = _adamw(small_parts, *[_pack([src[n] for n in SMALL_NAMES], SMALL_ROWS) for src in (w, mo, vo)],
                       "adamw_replicated")
    outs = []
    for kind in range(4):
        vals = {WIDE: res[0][kind].reshape(w[WIDE].shape)}
        vals.update(zip(FLAT_NAMES, _unpack(res[1][kind], flat_shapes)))
        vals.update(zip(SMALL_NAMES, _unpack(res_small[kind], small_shapes)))
        outs += [vals[n] for n in ALL_NAMES]
    return (loss, dx[None], *outs)
```

```python
import functools

import numpy as np
import jax
import jax.numpy as jnp
from jax import lax
from jax.experimental import pallas as pl
from jax.experimental.pallas import tpu as pltpu

F32 = jnp.float32
MXU_DTYPE = jnp.bfloat16

DEPTH = 2
D_MODEL = 1024
GROUP_W = 256
N_HEADS = 4
HEAD_DIM = 64
MLA_Q_RANK = 256
MLA_KV_RANK = 128
MLA_ROPE = 32
MLA_Q_COLS = N_HEADS * (HEAD_DIM + MLA_ROPE)
MLA_KV_COLS = N_HEADS * 2 * HEAD_DIM
ROPE_THETA = 10000.0
LN_EPS = 1e-5
RMS_EPS = 1e-6
ALPHA = (2 * DEPTH) ** 0.25
ADAM_LR, ADAM_B1, ADAM_B2, ADAM_EPS, ADAM_WD, ADAM_STEP = 0.001, 0.9, 0.999, 1e-08, 0.01, 10

_SPLIT = (256, 256, 256, 4, 256, 256, 256, 256, 128, 32, 256, 1024)
_OFF = [int(o) for o in np.cumsum((0,) + _SPLIT)]
IN_COLS = _OFF[-1]
PA_COLS = 7 * GROUP_W
PB_COLS = 1024 + 256 + 128 + 128
PB_CQ_BLK, PB_CKV_BLK, PB_MISC_BLK = 4, 10, 11
MISC_KROT = 4

LANES = 1024
PACK_ALIGN = 16
SMALL_ROWS = 144
ROW_TILE = 1024
PROJ_BWD_ROWS = 512
ATT_TILE = 256
SB_QUERY_TILE = 256
SOFTMAX_TILE = 512
SOFTMAX_FWD_TILE = 1024
DENSE_QUERY_TILE = 1024
FWD_PAIRS_PER_LOOP = 1
BWD_PAIRS_PER_LOOP = 2
PAIR = 128
SB_SLOT = PAIR // N_HEADS
VMEM_LIMIT = 56 * 1024 * 1024
MATMUL_VMEM = 30 * 1024 * 1024
NEG = -1e30
LOG2E = 1.4426950408889634
LN2 = 0.6931471805599453
EXP_UNDERFLOW = -104.0
DEAD_LOGIT = -110.0
REACH_SLACK = 1.0 + 2.0 ** -10
MESH_AXES = ("x", "y", "c")


def _dot(a, b):
    return jnp.dot(a, b, preferred_element_type=F32)


def _dot_nt(a, b):
    return lax.dot_general(a, b, (((1,), (1,)), ((), ())), preferred_element_type=F32)


def _dot_tn(a, b):
    return lax.dot_general(a, b, (((0,), (0,)), ((), ())), preferred_element_type=F32)


def _split2(x):
    hi = x.astype(MXU_DTYPE)
    lo = (x - hi.astype(F32)).astype(MXU_DTYPE)
    return hi, lo


def _split3(x):
    hi = x.astype(MXU_DTYPE)
    r = x - hi.astype(F32)
    mid = r.astype(MXU_DTYPE)
    lo = (r - mid.astype(F32)).astype(MXU_DTYPE)
    return hi, mid, lo


def _dot_exact_r(x, pm):
    hi, mid, lo = _split3(x)
    return _dot(hi, pm) + _dot(mid, pm) + _dot(lo, pm)


def _dot_exact_l(pm, x):
    hi, mid, lo = _split3(x)
    return _dot(pm, hi) + _dot(pm, mid) + _dot(pm, lo)


def _pick(dim, prefs):
    for p in prefs:
        if dim % p == 0:
            return p
    return dim


def _softplus(z):
    return jnp.maximum(z, 0.0) + jnp.log(1.0 + jnp.exp(-jnp.abs(z)))


def _tile_options(dim):
    opts = [d for d in range(128, min(dim, 2048) + 1, 128) if dim % d == 0]
    return opts or [dim]


def _matmul_tiles(m, n, k, out_bytes):
    tk = k if k <= 4096 else _pick(k, (1024, 512, 256, 128))
    best = None
    for tm in _tile_options(m):
        for tn in _tile_options(n):
            vmem = 2 * 2 * (tm * tk + tk * tn) + 4 * tm * tn + 2 * out_bytes * tm * tn
            if vmem <= MATMUL_VMEM and (best is None or tm * tn / (tm + tn) > best[0]):
                best = (tm * tn / (tm + tn), tm, tn)
    return best[1], best[2], tk


def _matmul(a, b, out_dtype, name, mode="nn"):
    m, k = (a.shape[1], a.shape[0]) if mode == "tn" else a.shape
    n = b.shape[0] if mode == "nt" else b.shape[1]
    tm, tn, tk = _matmul_tiles(m, n, k, jnp.dtype(out_dtype).itemsize)
    nk = k // tk
    dot = {"nn": _dot, "tn": _dot_tn, "nt": _dot_nt}[mode]

    def kern(a_ref, b_ref, o_ref, *acc):
        if nk == 1:
            o_ref[...] = dot(a_ref[...], b_ref[...]).astype(o_ref.dtype)
            return
        acc_ref, = acc
        kk = pl.program_id(2)

        @pl.when(kk == 0)
        def _():
            acc_ref[...] = jnp.zeros_like(acc_ref)

        acc_ref[...] += dot(a_ref[...], b_ref[...])

        @pl.when(kk == nk - 1)
        def _():
            o_ref[...] = acc_ref[...].astype(o_ref.dtype)

    a_spec = (pl.BlockSpec((tk, tm), lambda i, j, kk: (kk, i)) if mode == "tn"
              else pl.BlockSpec((tm, tk), lambda i, j, kk: (i, kk)))
    b_spec = (pl.BlockSpec((tn, tk), lambda i, j, kk: (j, kk)) if mode == "nt"
              else pl.BlockSpec((tk, tn), lambda i, j, kk: (kk, j)))
    return pl.pallas_call(
        kern, name=name, grid=(m // tm, n // tn, nk), in_specs=[a_spec, b_spec],
        out_specs=pl.BlockSpec((tm, tn), lambda i, j, kk: (i, j)),
        out_shape=jax.ShapeDtypeStruct((m, n), out_dtype),
        scratch_shapes=[pltpu.VMEM((tm, tn), F32)] if nk > 1 else [],
        compiler_params=pltpu.CompilerParams(
            dimension_semantics=("parallel", "parallel", "arbitrary"), vmem_limit_bytes=VMEM_LIMIT),
    )(a.astype(MXU_DTYPE), b.astype(MXU_DTYPE))


def _proj_in(hb, wa, wb, key_cols):
    rows, d = hb.shape
    tm = min(PROJ_BWD_ROWS, rows)

    def kern(a_ref, wa_ref, wb_ref, pa_ref, pb_ref, kmax_ref):
        @pl.when(pl.program_id(0) == 0)
        def _():
            kmax_ref[...] = jnp.zeros_like(kmax_ref)

        a = a_ref[...]
        pa = _dot(a, wa_ref[...]).astype(pa_ref.dtype)
        pa_ref[...] = pa
        pb_ref[...] = _dot(a, wb_ref[...])
        cols = {}
        for p in range(2):
            k32 = pa[:, key_cols + p * PAIR:key_cols + (p + 1) * PAIR].astype(F32)
            for e in range(2):
                sq = jnp.sum(jnp.where(_lane_mask("pair", e, tm), k32 * k32, 0.0), axis=1, keepdims=True)
                cols[2 * p + e] = jnp.max(sq, axis=0, keepdims=True)
        lane = lax.broadcasted_iota(jnp.int32, (1, PAIR), 1)
        kmax_ref[...] = jnp.maximum(kmax_ref[...], sum(jnp.where(lane == h, c, 0.0) for h, c in cols.items()))

    whole = lambda w: pl.BlockSpec(w.shape, lambda i: (0, 0), pipeline_mode=pl.Buffered(1))
    return pl.pallas_call(
        kern, name="proj_in", grid=(rows // tm,),
        in_specs=[pl.BlockSpec((tm, d), lambda i: (i, 0)), whole(wa), whole(wb)],
        out_specs=[pl.BlockSpec((tm, wa.shape[1]), lambda i: (i, 0)), pl.BlockSpec((tm, wb.shape[1]), lambda i: (i, 0)),
                   pl.BlockSpec((1, PAIR), lambda i: (0, 0))],
        out_shape=[jax.ShapeDtypeStruct((rows, wa.shape[1]), MXU_DTYPE), jax.ShapeDtypeStruct((rows, wb.shape[1]), F32),
                   jax.ShapeDtypeStruct((1, PAIR), F32)],
        compiler_params=pltpu.CompilerParams(dimension_semantics=("arbitrary",), vmem_limit_bytes=VMEM_LIMIT),
    )(hb, wa.astype(MXU_DTYPE), wb.astype(MXU_DTYPE))


def _piece_arrays(pieces):
    return [a for p in pieces for a in (p if isinstance(p, tuple) else (p,))]


def _join_pieces(refs, pieces):
    refs, cols = list(refs), []
    for p in pieces:
        vals = [refs.pop(0)[...] for _ in (p if isinstance(p, tuple) else (p,))]
        cols.append(functools.reduce(jnp.add, vals).astype(MXU_DTYPE))
    return jnp.concatenate(cols, axis=1)


def _proj_dx(pieces, w):
    arrs = _piece_arrays(pieces)
    rows, n = arrs[0].shape[0], w.shape[0]
    tm = min(PROJ_BWD_ROWS, rows)

    def kern(*refs):
        refs[-1][...] = _dot_nt(_join_pieces(refs[:len(arrs)], pieces), refs[len(arrs)][...])

    return pl.pallas_call(
        kern, name="proj_dx", grid=(rows // tm,),
        in_specs=[pl.BlockSpec((tm, a.shape[1]), lambda i: (i, 0)) for a in arrs]
        + [pl.BlockSpec(w.shape, lambda i: (0, 0), pipeline_mode=pl.Buffered(1))],
        out_specs=pl.BlockSpec((tm, n), lambda i: (i, 0)), out_shape=jax.ShapeDtypeStruct((rows, n), F32),
        compiler_params=pltpu.CompilerParams(dimension_semantics=("parallel",), vmem_limit_bytes=VMEM_LIMIT),
    )(*arrs, w.astype(MXU_DTYPE))


def _proj_dw(a, pieces):
    arrs = _piece_arrays(pieces)
    rows, m = a.shape
    k = sum(x.shape[1] for x in (p[0] if isinstance(p, tuple) else p for p in pieces))
    tk = min(PROJ_BWD_ROWS, rows)

    def kern(*refs):
        o_ref = refs[-1]

        @pl.when(pl.program_id(0) == 0)
        def _():
            o_ref[...] = jnp.zeros_like(o_ref)

        o_ref[...] += _dot_tn(refs[0][...], _join_pieces(refs[1:1 + len(arrs)], pieces))

    return pl.pallas_call(
        kern, name="proj_dw", grid=(rows // tk,),
        in_specs=[pl.BlockSpec((tk, m), lambda i: (i, 0))]
        + [pl.BlockSpec((tk, x.shape[1]), lambda i: (i, 0)) for x in arrs],
        out_specs=pl.BlockSpec((m, k), lambda i: (0, 0), pipeline_mode=pl.Buffered(1)),
        out_shape=jax.ShapeDtypeStruct((m, k), F32),
        compiler_params=pltpu.CompilerParams(dimension_semantics=("arbitrary",), vmem_limit_bytes=VMEM_LIMIT),
    )(a, *arrs)


def _rowwise(body, name, rows, tile, row_ins, full_ins, row_outs, acc_outs=(), scratch=(),
             reverse=False, sequential=False):
    n = rows // tile

    def ridx(i):
        return (n - 1 - i) if reverse else i

    in_specs, args = [], []
    for arr, width, cb in row_ins:
        in_specs.append(pl.BlockSpec((tile, width), lambda i, cb=cb: (ridx(i), cb)))
        args.append(arr)
    for arr in full_ins:
        in_specs.append(pl.BlockSpec(arr.shape, lambda i, nd=arr.ndim: (0,) * nd))
        args.append(arr)
    out_shape = [jax.ShapeDtypeStruct((rows, w), dt) for w, dt in row_outs]
    out_shape += [jax.ShapeDtypeStruct(s, dt) for s, dt in acc_outs]
    out_specs = [pl.BlockSpec((tile, w), lambda i: (ridx(i), 0)) for w, dt in row_outs]
    out_specs += [pl.BlockSpec(s, lambda i, nd=len(s): (0,) * nd) for s, dt in acc_outs]

    def kern(*refs):
        body(pl.program_id(0), *refs)

    sem = "arbitrary" if (acc_outs or sequential) else "parallel"
    return pl.pallas_call(
        kern, name=name, grid=(n,), in_specs=in_specs, out_specs=out_specs, out_shape=out_shape,
        scratch_shapes=list(scratch),
        compiler_params=pltpu.CompilerParams(dimension_semantics=(sem,), vmem_limit_bytes=VMEM_LIMIT),
    )(*args)


def _ln_stats(u):
    mu = jnp.mean(u, axis=-1, keepdims=True)
    xc = u - mu
    var = jnp.mean(xc * xc, axis=-1, keepdims=True)
    return xc, lax.rsqrt(var + LN_EPS)


def _ln_fwd(a, g, beta, name):
    rows, d = a.shape

    def body(i, a_ref, g_ref, be_ref, h_ref, hb_ref):
        xc, rstd = _ln_stats(a_ref[...])
        y = xc * rstd * g_ref[...] + be_ref[...]
        h_ref[...] = y
        hb_ref[...] = y.astype(hb_ref.dtype)

    return _rowwise(body, name, rows, min(ROW_TILE, rows), [(a, d, 0)],
                    [g.reshape(1, d), beta.reshape(1, d)], [(d, F32), (d, MXU_DTYPE)])


def _ln_bwd(u, g, dy1, dy2, c1, name, gate=None):
    rows, d = u.shape
    has_2 = dy2 is not None
    n_groups = len(gate[1]) if gate else 0

    def body(i, *refs):
        refs = list(refs)
        uu = refs.pop(0)[...]
        dy = c1 * refs.pop(0)[...]
        if has_2:
            dy = dy + refs.pop(0)[...]
        if gate:
            mixed = jnp.concatenate([refs.pop(0)[...] for _ in range(n_groups)], axis=1)
            gt = refs.pop(0)[...]
            g_ref, w_ref, du_ref, dub_ref, dm_ref, dgate_ref, dg_ref, db_ref = refs
        else:
            g_ref, du_ref, dg_ref, db_ref = refs

        @pl.when(i == 0)
        def _():
            dg_ref[...] = jnp.zeros_like(dg_ref)
            db_ref[...] = jnp.zeros_like(db_ref)

        xc, rstd = _ln_stats(uu)
        xhat = xc * rstd
        dxh = dy * g_ref[...]
        m1 = jnp.mean(dxh, axis=-1, keepdims=True)
        m2 = jnp.mean(dxh * xhat, axis=-1, keepdims=True)
        du = rstd * (dxh - m1 - xhat * m2)
        du_ref[...] = du
        dg_ref[...] += jnp.sum(dy * xhat, axis=0, keepdims=True)
        db_ref[...] += jnp.sum(dy, axis=0, keepdims=True)
        if gate:
            dub = du.astype(dub_ref.dtype)
            dub_ref[...] = dub
            dgated = _dot_nt(dub, w_ref[...])
            sig = 1.0 / (1.0 + jnp.exp(-gt))
            dm_ref[...] = dgated * (gt * sig)
            dgate_ref[...] = (dgated * mixed * (sig * (1.0 + gt * (1.0 - sig)))).astype(dgate_ref.dtype)

    row_ins = [(u, d, 0), (dy1, d, 0)] + ([(dy2, d, 0)] if has_2 else [])
    full_ins = [g.reshape(1, d)]
    row_outs = [(d, F32)]
    if gate:
        w_out, groups, pb = gate
        w = GROUP_W * n_groups
        row_ins += [(o, GROUP_W, 0) for o in groups] + [(pb, w, 0)]
        full_ins += [w_out.astype(MXU_DTYPE)]
        row_outs += [(d, MXU_DTYPE), (w, F32), (w, MXU_DTYPE)]
    return _rowwise(body, name, rows, min(PROJ_BWD_ROWS if gate else ROW_TILE, rows), row_ins, full_ins, row_outs,
                    [((1, d), F32), ((1, d), F32)])


def _gate_out_proj_ln(groups, pb, w_out, h, g, beta, target=None):
    rows, d = h.shape
    last = target is not None
    w = GROUP_W * len(groups)

    def body(i, *refs):
        refs = list(refs)
        mixed = jnp.concatenate([refs.pop(0)[...] for _ in groups], axis=1)
        gate = refs.pop(0)[...]
        if last:
            h_ref, t_ref, w_ref, g_ref, be_ref, a_ref, u_ref, dh_ref, acc_ref = refs

            @pl.when(i == 0)
            def _():
                acc_ref[...] = jnp.zeros_like(acc_ref)
        else:
            h_ref, w_ref, g_ref, be_ref, a_ref, u_ref, o_ref, ob_ref = refs
        gated = (mixed * (gate / (1.0 + jnp.exp(-gate)))).astype(a_ref.dtype)
        a_ref[...] = gated
        u = ALPHA * h_ref[...] + _dot(gated, w_ref[...])
        u_ref[...] = u
        xc, rstd = _ln_stats(u)
        y = xc * rstd * g_ref[...] + be_ref[...]
        if last:
            e = y - t_ref[...]
            dh_ref[...] = e * (1.0 / d)
            acc_ref[...] += jnp.sum(e * e, axis=0, keepdims=True)
        else:
            o_ref[...] = y
            ob_ref[...] = y.astype(ob_ref.dtype)

    row_ins = [(o, GROUP_W, 0) for o in groups] + [(pb, w, 0), (h, d, 0)] + ([(target, d, 0)] if last else [])
    full_ins = [w_out.astype(MXU_DTYPE), g.reshape(1, d), beta.reshape(1, d)]
    if last:
        return _rowwise(body, "gate_out_proj_ln_loss", rows, PROJ_BWD_ROWS, row_ins, full_ins,
                        [(w, MXU_DTYPE), (d, F32), (d, F32)], [((1, d), F32)])
    return _rowwise(body, "gate_out_proj_ln", rows, PROJ_BWD_ROWS, row_ins, full_ins,
                    [(w, MXU_DTYPE), (d, F32), (d, F32), (d, MXU_DTYPE)])


def _tri(n, kind):
    r = np.arange(n)[:, None]
    c = np.arange(n)[None, :]
    m = {"lower_incl": r >= c, "upper_incl": r <= c, "row_gt_col": r > c, "row_lt_col": r < c}[kind]
    return jnp.asarray(m.astype(np.float32), dtype=MXU_DTYPE)


def _forget_fwd(pb, bias_row):
    rows = pb.shape[0]
    tile = _pick(rows, (1024, 512, 256))

    def body(i, x_ref, b_ref, l_ref, o_ref, carry_ref):
        @pl.when(i == 0)
        def _():
            carry_ref[...] = jnp.zeros_like(carry_ref)

        xx = x_ref[...] + b_ref[...]
        lane = lax.broadcasted_iota(jnp.int32, xx.shape, 1)
        lf = jnp.where(lane < N_HEADS, -_softplus(-xx), 0.0)
        o_ref[...] = _dot_exact_l(l_ref[...], lf) + carry_ref[...]
        carry_ref[...] += jnp.sum(lf, axis=0, keepdims=True)

    return _rowwise(body, "forget_fwd", rows, tile, [(pb, 128, PB_MISC_BLK)],
                    [bias_row, _tri(tile, "lower_incl")], [(128, F32)],
                    scratch=[pltpu.VMEM((1, 128), F32)], sequential=True)[0]


def _forget_bwd(pb, bias_row, dfc):
    rows = pb.shape[0]
    tile = _pick(rows, (1024, 512, 256))

    def body(i, x_ref, df_ref, b_ref, u_ref, o_ref, db_ref, carry_ref):
        @pl.when(i == 0)
        def _():
            carry_ref[...] = jnp.zeros_like(carry_ref)
            db_ref[...] = jnp.zeros_like(db_ref)

        df = df_ref[...]
        sfx = _dot_exact_l(u_ref[...], df) + carry_ref[...]
        carry_ref[...] += jnp.sum(df, axis=0, keepdims=True)
        xx = x_ref[...] + b_ref[...]
        lane = lax.broadcasted_iota(jnp.int32, xx.shape, 1)
        dl = jnp.where(lane < N_HEADS, sfx / (1.0 + jnp.exp(xx)), 0.0)
        o_ref[...] = dl
        db_ref[...] += jnp.sum(dl, axis=0, keepdims=True)

    return _rowwise(body, "forget_bwd", rows, tile,
                    [(pb, 128, PB_MISC_BLK), (dfc, 128, 0)],
                    [bias_row, _tri(tile, "upper_incl")], [(128, F32)], [((1, 128), F32)],
                    scratch=[pltpu.VMEM((1, 128), F32)], reverse=True, sequential=True)


def _rope_tables(s):
    half = MLA_ROPE // 2
    inv_freq = ROPE_THETA ** (-jnp.arange(half, dtype=F32) / half)
    ang = jnp.arange(s).astype(F32)[:, None] * inv_freq[None, :]
    cos2 = jnp.tile(jnp.cos(ang), (1, 2))
    sin2 = jnp.tile(jnp.sin(ang), (1, 2))
    cx = jnp.tile(cos2, (1, N_HEADS))
    sx = jnp.tile(sin2, (1, N_HEADS))
    pad = ((0, 0), (MISC_KROT, 128 - MISC_KROT - MLA_ROPE))
    ck = jnp.pad(cos2, pad)
    sk = jnp.pad(sin2, pad)
    return dict(ck=ck, sk=sk, cx=cx, sx=sx)


def _rot_matrix(width, bases):
    half = MLA_ROPE // 2
    p = np.zeros((width, width), np.float32)
    for b in bases:
        for i in range(half):
            p[b + half + i, b + i] = -1.0
            p[b + i, b + half + i] = 1.0
    return p


def _rope_matrices():
    pk = _rot_matrix(128, [MISC_KROT])
    p4 = _rot_matrix(128, [h * MLA_ROPE for h in range(N_HEADS)])
    a = np.zeros((128, 128), np.float32)
    for h in range(N_HEADS):
        for r in range(MLA_ROPE):
            a[h * MLA_ROPE + r, MISC_KROT + r] = 1.0
    cast = lambda m: jnp.asarray(m, dtype=MXU_DTYPE)
    return dict(p4=cast(p4), p4t=cast(p4.T), pk=cast(pk), spread=cast(a.T), xa=cast(a), xb=cast(p4.T @ a))


def _rms(c, g):
    r = lax.rsqrt(jnp.mean(c * c, axis=-1, keepdims=True) + RMS_EPS)
    return c * r * g


def _mla_q_fwd(pb, g, w_up, tabs, mats):
    rows = pb.shape[0]

    def body(i, c_ref, cos_ref, sin_ref, g_ref, w_ref, p_ref, q_ref, cn_ref):
        cn = _rms(c_ref[...], g_ref[...]).astype(cn_ref.dtype)
        cn_ref[...] = cn
        q = _dot(cn, w_ref[...])
        qr = q[:, GROUP_W:]
        q_ref[:, :GROUP_W] = q[:, :GROUP_W].astype(q_ref.dtype)
        q_ref[:, GROUP_W:] = (qr * cos_ref[...] + _dot_exact_r(qr, p_ref[...]) * sin_ref[...]).astype(q_ref.dtype)

    return _rowwise(body, "mla_q_fwd", rows, ROW_TILE,
                    [(pb, MLA_Q_RANK, PB_CQ_BLK), (tabs["cx"], PAIR, 0), (tabs["sx"], PAIR, 0)],
                    [g.reshape(1, MLA_Q_RANK), w_up.astype(MXU_DTYPE), mats["p4"]],
                    [(MLA_Q_COLS, MXU_DTYPE), (MLA_Q_RANK, MXU_DTYPE)])


def _mla_kv_fwd(pb, g, w_up, tabs, mats):
    rows = pb.shape[0]

    def body(i, c_ref, x_ref, cos_ref, sin_ref, g_ref, w_ref, p_ref, sp_ref, kv_ref, cn_ref, kr_ref):
        cn = _rms(c_ref[...], g_ref[...]).astype(cn_ref.dtype)
        cn_ref[...] = cn
        kv_ref[...] = _dot(cn, w_ref[...]).astype(kv_ref.dtype)
        xx = x_ref[...]
        kr = xx * cos_ref[...] + _dot_exact_r(xx, p_ref[...]) * sin_ref[...]
        kr_ref[...] = _dot_exact_r(kr, sp_ref[...]).astype(kr_ref.dtype)

    return _rowwise(body, "mla_kv_fwd", rows, ROW_TILE,
                    [(pb, MLA_KV_RANK, PB_CKV_BLK), (pb, 128, PB_MISC_BLK), (tabs["ck"], 128, 0), (tabs["sk"], 128, 0)],
                    [g.reshape(1, MLA_KV_RANK), w_up.astype(MXU_DTYPE), mats["pk"], mats["spread"]],
                    [(MLA_KV_COLS, MXU_DTYPE), (MLA_KV_RANK, MXU_DTYPE), (128, MXU_DTYPE)])


def _rms_bwd(c, g, dy):
    r = lax.rsqrt(jnp.mean(c * c, axis=-1, keepdims=True) + RMS_EPS)
    dyg = dy * g
    dc = r * dyg - c * (r * r * r) * jnp.mean(c * dyg, axis=-1, keepdims=True)
    return dc, jnp.sum(dy * c * r, axis=0, keepdims=True)


def _mla_q_bwd(dq_full, pb, g, w_up, tabs, mats):
    rows, nq = dq_full.shape

    def body(i, d_ref, c_ref, cos_ref, sin_ref, g_ref, w_ref, pt_ref, dq_ref, dc_ref, dg_ref):
        @pl.when(i == 0)
        def _():
            dg_ref[...] = jnp.zeros_like(dg_ref)

        dr = d_ref[:, GROUP_W:]
        dq = jnp.concatenate([d_ref[:, :GROUP_W], dr * cos_ref[...] + _dot_exact_r(dr * sin_ref[...], pt_ref[...])],
                             axis=1).astype(dq_ref.dtype)
        dq_ref[...] = dq
        dc, dg = _rms_bwd(c_ref[...], g_ref[...], _dot_nt(dq, w_ref[...]))
        dc_ref[...] = dc.astype(dc_ref.dtype)
        dg_ref[...] += dg

    return _rowwise(body, "mla_q_bwd", rows, ROW_TILE,
                    [(dq_full, nq, 0), (pb, MLA_Q_RANK, PB_CQ_BLK), (tabs["cx"], PAIR, 0), (tabs["sx"], PAIR, 0)],
                    [g.reshape(1, MLA_Q_RANK), w_up.astype(MXU_DTYPE), mats["p4t"]],
                    [(nq, MXU_DTYPE), (MLA_Q_RANK, MXU_DTYPE)], [((1, MLA_Q_RANK), F32)])


def _mla_kv_bwd(dk, dv, pb, g, w_up, tabs, mats):
    rows = dk.shape[0]

    def body(i, dk_ref, dr_ref, dv_ref, c_ref, cos_ref, sin_ref, g_ref, w_ref, a_ref, b_ref,
             dkv_ref, dc_ref, dm_ref, dg_ref):
        @pl.when(i == 0)
        def _():
            dg_ref[...] = jnp.zeros_like(dg_ref)

        dkv = jnp.concatenate([dk_ref[...], dv_ref[...]], axis=1).astype(dkv_ref.dtype)
        dkv_ref[...] = dkv
        dc, dg = _rms_bwd(c_ref[...], g_ref[...], _dot_nt(dkv, w_ref[...]))
        dc_ref[...] = dc.astype(dc_ref.dtype)
        dg_ref[...] += dg
        dr = dr_ref[...]
        dm_ref[...] = _dot_exact_r(dr * cos_ref[...], a_ref[...]) + _dot_exact_r(dr * sin_ref[...], b_ref[...])

    return _rowwise(body, "mla_kv_bwd", rows, ROW_TILE,
                    [(dk, GROUP_W, 0), (dk, PAIR, 2), (dv, GROUP_W, 0), (pb, MLA_KV_RANK, PB_CKV_BLK),
                     (tabs["cx"], PAIR, 0), (tabs["sx"], PAIR, 0)],
                    [g.reshape(1, MLA_KV_RANK), w_up.astype(MXU_DTYPE), mats["xa"], mats["xb"]],
                    [(MLA_KV_COLS, MXU_DTYPE), (MLA_KV_RANK, MXU_DTYPE), (PAIR, F32)], [((1, MLA_KV_RANK), F32)])


def _att_params(parallel):
    return pltpu.CompilerParams(dimension_semantics=("parallel" if parallel else "arbitrary",),
                                vmem_limit_bytes=VMEM_LIMIT)


def _blk_off(j, t):
    return j * t if isinstance(j, int) else pl.multiple_of(j * t, t)


def _causal_mask(t, strict):
    r = lax.broadcasted_iota(jnp.int32, (t, t), 0)
    c = lax.broadcasted_iota(jnp.int32, (t, t), 1)
    return (c < r) if strict else (c <= r)


def _lane_mask(kind, head, rows):
    lane = lax.broadcasted_iota(jnp.int32, (rows, PAIR), 1)
    if kind == "pair":
        return (lane < HEAD_DIM) if head % 2 == 0 else (lane >= HEAD_DIM)
    return (lane >= MLA_ROPE * head) & (lane < MLA_ROPE * (head + 1))


def _row_spec(t, cb, width=PAIR):
    return pl.BlockSpec((t, width), lambda i, cb=cb: (i, cb))


def _whole_spec(rows, cb, width=PAIR):
    return pl.BlockSpec((rows, width), lambda i, cb=cb: (0, cb), pipeline_mode=pl.Buffered(1))


def _is_pow2(x):
    return float(np.frexp(x)[0]) == 0.5


def _masked_heads(blocks, kinds, pair, dtype, scale=None):
    out = []
    for e in range(2):
        head = 2 * pair + e
        parts = [jnp.where(_lane_mask(k, head, b.shape[0]), b.astype(F32) * (1.0 if scale is None else scale),
                           0.0).astype(dtype)
                 for b, k in zip(blocks, kinds)]
        out.append(parts[0] if len(parts) == 1 else jnp.concatenate(parts, axis=1))
    return out


def _logit_reach(qh, kmax2, head):
    q32 = qh.astype(F32)
    return jnp.sqrt(jnp.sum(q32 * q32, axis=1, keepdims=True) * _col(kmax2, head)) * REACH_SLACK


def _forget_top(ft_ref, head, off):
    return jnp.max(-ft_ref[head:head + 1, pl.ds(off, PAIR)])


def _col(block, idx):
    lane = lax.broadcasted_iota(jnp.int32, block.shape, 1)
    return jnp.sum(jnp.where(lane == idx, block, 0.0), axis=1, keepdims=True)


def _scatter_cols(cols, t):
    lane = lax.broadcasted_iota(jnp.int32, (t, PAIR), 1)
    out = jnp.zeros((t, PAIR), F32)
    for idx, c in cols.items():
        out = out + jnp.where(lane == idx, c, 0.0)
    return out


def _take_heads(per_head, pair):
    return jnp.where(_lane_mask("pair", 0, per_head[0].shape[0]), per_head[0], per_head[1])


class _Parts:
    def __init__(self, q_parts, k_parts, v_parts, tq, sk):
        self.kinds = [[kind for _, _, kind in q_parts[p]] for p in range(2)]
        self.nparts = len(q_parts[0])
        self.q_specs = [_row_spec(tq, cb) for p in range(2) for _, cb, _ in q_parts[p]]
        self.q_args = [a for p in range(2) for a, _, _ in q_parts[p]]
        self.k_specs = [_whole_spec(sk, cb) for p in range(2) for _, cb, _ in k_parts[p]]
        self.k_args = [a for p in range(2) for a, _, _ in k_parts[p]]
        self.v_specs = [_whole_spec(sk, cb) for _, cb in v_parts]
        self.v_args = [a for a, _ in v_parts]
        self.width = PAIR * self.nparts

    def split(self, refs):
        n = self.nparts
        refs = list(refs)
        q = [refs[p * n:(p + 1) * n] for p in range(2)]
        k = [refs[2 * n + p * n:2 * n + (p + 1) * n] for p in range(2)]
        v = refs[4 * n:4 * n + 2]
        return q, k, v, refs[4 * n + 2:]

    def k_block(self, k_refs, off, t):
        blks = [r[pl.ds(off, t), :] for r in k_refs]
        return blks[0] if len(blks) == 1 else jnp.concatenate(blks, axis=1)


def _softmax_fwd(q_parts, k_parts, v_parts, sq, sk, scale, causal, bias, name):
    tq = min(SOFTMAX_FWD_TILE, sq) if causal else _pick(sq, (DENSE_QUERY_TILE, SOFTMAX_TILE))
    tk = tq if causal else min(SOFTMAX_TILE, sk)
    nkv = sk // tk
    pp = _Parts(q_parts, k_parts, v_parts, tq, sk)

    def kern(*refs):
        q_refs, k_refs, v_refs, rest = pp.split(refs)
        if bias is not None:
            fc_ref, ft_ref, kmax_ref, o_ref, lse_ref = rest
            fcb = fc_ref[...]
        else:
            o_ref, lse_ref = rest
        i = pl.program_id(0)
        fold = _is_pow2(scale)
        head_on = [jnp.where(_lane_mask("pair", e, tk), 1.0, 0.0).astype(MXU_DTYPE) for e in range(2)]
        head_off = [jnp.where(_lane_mask("pair", e, tk), 0.0, 1.0).astype(MXU_DTYPE) for e in range(2)]
        lse_cols = {}
        for first in range(0, 2, FWD_PAIRS_PER_LOOP):
            pairs = list(range(first, first + FWD_PAIRS_PER_LOOP))
            heads = [2 * p + e for p in pairs for e in range(2)]
            qm = {}
            for p in pairs:
                masked_q = _masked_heads([r[...] for r in q_refs[p]], pp.kinds[p], p, MXU_DTYPE, scale if fold else None)
                qm.update({2 * p + e: masked_q[e] for e in range(2)})
            if bias is not None:
                reach = {h: _logit_reach(qm[h], kmax_ref[...], h) for h in heads}

            def block(j, carry, masked, pairs=pairs, qm=qm):
                off = _blk_off(j, tk)
                out = []
                for p in pairs:
                    kb = pp.k_block(k_refs[p], off, tk)
                    vb = v_refs[p][pl.ds(off, tk), :]
                    for e in range(2):
                        h = 2 * p + e
                        m, acc = carry[len(out)]
                        s = _dot_nt(qm[h], kb) * (LOG2E if fold else scale * LOG2E)
                        if bias is not None:
                            s = s - ft_ref[h:h + 1, pl.ds(off, tk)] * LOG2E
                        if masked:
                            s = jnp.where(_causal_mask(tq, False), s, NEG)
                        m_new = jnp.maximum(m, jnp.max(s, axis=1, keepdims=True))
                        pr = jnp.exp2(s - m_new).astype(MXU_DTYPE)
                        out.append((m_new, jnp.exp2(m - m_new) * acc + _dot(pr, vb * head_on[e] + head_off[e])))
                return tuple(out)

            carry = tuple((jnp.full((tq, 1), NEG, F32), jnp.zeros((tq, PAIR), F32)) for _ in heads)
            if causal and bias is not None:
                def alive(c, j, heads=heads, reach=reach):
                    return functools.reduce(jnp.maximum, [
                        jnp.max(reach[h] + _forget_top(ft_ref, h, _blk_off(j, tk)) - c[n][0] * LN2)
                        for n, h in enumerate(heads)])

                def step(state, block=block, alive=alive):
                    n, _, c = state
                    c = block(i - 1 - n, c, False)
                    return n + 1, alive(c, i - 1 - n), c

                carry = block(i, carry, True)
                _, _, carry = lax.while_loop(lambda st: jnp.logical_and(st[0] < i, st[1] > DEAD_LOGIT), step,
                                             (jnp.int32(0), alive(carry, i), carry))
            elif causal:
                carry = lax.fori_loop(0, i, lambda j, c, block=block: block(j, c, False), carry)
                carry = block(i, carry, True)
            else:
                for j in range(nkv):
                    carry = block(j, carry, False)
            for n, p in enumerate(pairs):
                outs = []
                for e in range(2):
                    m, acc = carry[2 * n + e]
                    l = _col(acc, HEAD_DIM * (1 - e))
                    outs.append(acc / l)
                    lse_cols[2 * p + e] = m * LN2 + jnp.log(l) + (_col(fcb, 2 * p + e) if bias is not None else 0.0)
                o_ref[:, p * PAIR:(p + 1) * PAIR] = _take_heads(outs, p)
        lse_ref[...] = _scatter_cols(lse_cols, tq)

    in_specs = pp.q_specs + pp.k_specs + pp.v_specs
    args = pp.q_args + pp.k_args + pp.v_args
    if bias is not None:
        in_specs += [_row_spec(tq, 0), pl.BlockSpec((8, sk), lambda i: (0, 0), pipeline_mode=pl.Buffered(1)),
                     pl.BlockSpec((1, PAIR), lambda i: (0, 0))]
        args += list(bias)
    return pl.pallas_call(
        kern, name=name, grid=(sq // tq,), in_specs=in_specs,
        out_specs=[_row_spec(tq, 0, GROUP_W), _row_spec(tq, 0)],
        out_shape=[jax.ShapeDtypeStruct((sq, GROUP_W), F32), jax.ShapeDtypeStruct((sq, PAIR), F32)],
        compiler_params=_att_params(True),
    )(*args)


def _softmax_bwd(q_parts, k_parts, v_parts, o, lse, do, do_blk, sq, sk, scale, causal, bias, name):
    tq = min(SOFTMAX_TILE, sq) if causal else _pick(sq, (DENSE_QUERY_TILE, SOFTMAX_TILE))
    tk = tq if causal else min(SOFTMAX_TILE, sk)
    nkv = sk // tk
    pp = _Parts(q_parts, k_parts, v_parts, tq, sk)
    quad = pp.nparts == 2
    wq = GROUP_W + (PAIR if quad else 0)

    def kern(*refs):
        q_refs, k_refs, v_refs, rest = pp.split(refs)
        if bias is not None:
            o_ref, lse_ref, do_ref, fc_ref, ft_ref, kmax_ref, dq_ref, dk_ref, dv_ref, dfq_ref, dfk_ref = rest
            fcb = fc_ref[...]
        else:
            o_ref, lse_ref, do_ref, dq_ref, dk_ref, dv_ref = rest
        i = pl.program_id(0)
        fold = _is_pow2(scale)

        @pl.when(i == 0)
        def _():
            dk_ref[...] = jnp.zeros_like(dk_ref)
            dv_ref[...] = jnp.zeros_like(dv_ref)
            if bias is not None:
                dfk_ref[...] = jnp.zeros_like(dfk_ref)

        lse_b = lse_ref[...]
        qm, dom, delta, lse_h = [], [], [], []
        for p in range(2):
            qm += _masked_heads([r[...] for r in q_refs[p]], pp.kinds[p], p, MXU_DTYPE, scale if fold else None)
            do_p = do_ref[:, p * PAIR:(p + 1) * PAIR]
            dom += _masked_heads([do_p], ["pair"], p, MXU_DTYPE)
            prod = do_p * o_ref[:, p * PAIR:(p + 1) * PAIR]
            for e in range(2):
                h = 2 * p + e
                delta.append(jnp.sum(jnp.where(_lane_mask("pair", h, tq), prod, 0.0), axis=1, keepdims=True))
                lse_h.append(_col(lse_b, h) - (_col(fcb, h) if bias is not None else 0.0))

        def block(j, carry, masked, pairs):
            off = _blk_off(j, tk)
            out = []
            for p in pairs:
                kb = pp.k_block(k_refs[p], off, tk)
                vb = v_refs[p][pl.ds(off, tk), :]
                dk_acc = jnp.zeros((tk, pp.width), F32)
                dv_acc = jnp.zeros((tk, PAIR), F32)
                for e in range(2):
                    h = 2 * p + e
                    dq, dfq = carry[len(out)]
                    s = _dot_nt(qm[h], kb)
                    if not fold:
                        s = s * scale
                    if bias is not None:
                        s = s - ft_ref[h:h + 1, pl.ds(off, tk)]
                    if masked:
                        s = jnp.where(_causal_mask(tq, False), s, NEG)
                    pr = jnp.exp(s - lse_h[h])
                    ds = pr * (_dot_nt(dom[h], vb) - delta[h])
                    dsb = (ds if fold else ds * scale).astype(MXU_DTYPE)
                    dv_acc = dv_acc + _dot_tn(pr.astype(MXU_DTYPE), dom[h])
                    dk_acc = dk_acc + _dot_tn(dsb, qm[h])
                    dq = dq + _dot(dsb, kb)
                    if bias is not None:
                        dfq = dfq + jnp.sum(ds, axis=1, keepdims=True)
                        dfk_ref[h:h + 1, pl.ds(off, tk)] -= jnp.sum(ds, axis=0, keepdims=True)
                    out.append((dq, dfq))
                dv_ref[pl.ds(off, tk), p * PAIR:(p + 1) * PAIR] += dv_acc
                dk_ref[pl.ds(off, tk), p * PAIR:(p + 1) * PAIR] += dk_acc[:, :PAIR]
                if quad:
                    dk_ref[pl.ds(off, tk), GROUP_W:] += dk_acc[:, PAIR:]
            return tuple(out)

        done = {}
        for first in range(0, 2, BWD_PAIRS_PER_LOOP):
            pairs = list(range(first, first + BWD_PAIRS_PER_LOOP))
            heads = [2 * p + e for p in pairs for e in range(2)]
            carry = tuple((jnp.zeros((tq, pp.width), F32), jnp.zeros((tq, 1), F32)) for _ in heads)
            if causal and bias is not None:
                reach = {h: _logit_reach(qm[h], kmax_ref[...], h) - lse_h[h] for h in heads}

                def alive(j, heads=heads, reach=reach):
                    return functools.reduce(jnp.maximum, [
                        jnp.max(reach[h] + _forget_top(ft_ref, h, _blk_off(j, tk))) for h in heads])

                def step(state, pairs=pairs, alive=alive):
                    n, _, c = state
                    return n + 1, alive(i - 1 - n), block(i - 1 - n, c, False, pairs)

                carry = block(i, carry, True, pairs)
                _, _, carry = lax.while_loop(lambda st: jnp.logical_and(st[0] < i, st[1] > DEAD_LOGIT), step,
                                             (jnp.int32(0), alive(i), carry))
            elif causal:
                carry = lax.fori_loop(0, i, lambda j, c, pairs=pairs: block(j, c, False, pairs), carry)
                carry = block(i, carry, True, pairs)
            else:
                for j in range(nkv):
                    carry = block(j, carry, False, pairs)
            done.update(zip(heads, carry))
        carry = [done[h] for h in range(N_HEADS)]
        dqs = [c[0] * scale if fold else c[0] for c in carry]
        for p in range(2):
            dq_ref[:, p * PAIR:(p + 1) * PAIR] = _take_heads([dqs[2 * p + e][:, :PAIR] for e in range(2)], p)
        if quad:
            dq_ref[:, GROUP_W:] = sum(jnp.where(_lane_mask("quad", h, tq), dqs[h][:, PAIR:], 0.0)
                                      for h in range(N_HEADS))
        if bias is not None:
            dfq_ref[...] = _scatter_cols({h: carry[h][1] for h in range(N_HEADS)}, tq)

    acc_spec = lambda rows, width: pl.BlockSpec((rows, width), lambda i: (0, 0), pipeline_mode=pl.Buffered(1))
    in_specs = pp.q_specs + pp.k_specs + pp.v_specs + [_row_spec(tq, 0, GROUP_W), _row_spec(tq, 0),
                                                       _row_spec(tq, do_blk, GROUP_W)]
    args = pp.q_args + pp.k_args + pp.v_args + [o, lse, do]
    out_specs = [_row_spec(tq, 0, wq), acc_spec(sk, wq), acc_spec(sk, GROUP_W)]
    out_shape = [jax.ShapeDtypeStruct((sq, wq), F32), jax.ShapeDtypeStruct((sk, wq), F32),
                 jax.ShapeDtypeStruct((sk, GROUP_W), F32)]
    if bias is not None:
        in_specs += [_row_spec(tq, 0), pl.BlockSpec((8, sk), lambda i: (0, 0), pipeline_mode=pl.Buffered(1)),
                     pl.BlockSpec((1, PAIR), lambda i: (0, 0))]
        args += list(bias)
        out_specs += [_row_spec(tq, 0), acc_spec(8, sk)]
        out_shape += [jax.ShapeDtypeStruct((sq, PAIR), F32), jax.ShapeDtypeStruct((8, sk), F32)]
    return pl.pallas_call(
        kern, name=name, grid=(sq // tq,), in_specs=in_specs, out_specs=out_specs, out_shape=out_shape,
        compiler_params=_att_params(False),
    )(*args)


def _sb_logs(qh, kb, valid):
    z = _dot_nt(qh, kb)
    sp = _softplus(z)
    lk = -sp
    if valid is not None:
        lk = jnp.where(valid, lk, 0.0)
    return lk, z - sp


def _sb_valid(d, tq, tk):
    r = lax.broadcasted_iota(jnp.int32, (tq, tk), 0)
    c = lax.broadcasted_iota(jnp.int32, (tq, tk), 1)
    return c + d * tk < r


def _tri_sums(xs, tri):
    t = xs[0].shape[0]
    pieces = [_split2(x) for x in xs]
    hi = _dot(jnp.concatenate([pc[0] for pc in pieces], axis=0), tri)
    lo = _dot(jnp.concatenate([pc[1] for pc in pieces], axis=0), tri)
    return [hi[n * t:(n + 1) * t] + lo[n * t:(n + 1) * t] for n in range(len(xs))]


def _sb_fwd(src, q_blk, k_blk, v_blk, s, scale, name):
    assert _is_pow2(scale)
    tq, t = min(SB_QUERY_TILE, s), min(ATT_TILE, s)
    slots = -(-(s // t) // SB_SLOT) * SB_SLOT
    width = N_HEADS * slots
    band = tq // t
    pair = lambda blk: [[(src, blk + p, "pair")] for p in range(2)]
    pp = _Parts(pair(q_blk), pair(k_blk), [(src, v_blk + p) for p in range(2)], tq, s)

    def kern(*refs):
        q_refs, k_refs, v_refs, (tri_ref, o_ref, rm_ref, cnt_ref) = pp.split(refs)
        i = pl.program_id(0)
        tri = tri_ref[...]
        lane = lax.broadcasted_iota(jnp.int32, (tq, width), 1)
        qm = []
        for p in range(2):
            qm += _masked_heads([q_refs[p][0][...]], ["pair"], p, MXU_DTYPE, scale)

        def block(j, carry, valid):
            accs, rights, rm = carry
            off = _blk_off(j, t)
            kbs = [k_refs[p][0][pl.ds(off, t), :] for p in range(2)]
            vbs = [v_refs[p][pl.ds(off, t), :] for p in range(2)]
            logs = [_sb_logs(qm[h], kbs[h // 2], valid) for h in range(N_HEADS)]
            tails = _tri_sums([lg[0] for lg in logs], tri)
            new_acc, new_right = [], []
            for h in range(N_HEADS):
                lk, ls = logs[h]
                w = jnp.exp(ls + tails[h] + rights[h])
                if valid is not None:
                    w = jnp.where(valid, w, 0.0)
                new_acc.append(accs[h] + _dot(w.astype(MXU_DTYPE), vbs[h // 2]))
                rm = rm + jnp.where(lane == slots * h + j, rights[h], 0.0)
                new_right.append(rights[h] + jnp.sum(lk, axis=1, keepdims=True))
            return tuple(new_acc), tuple(new_right), rm

        carry = (tuple(jnp.zeros((tq, PAIR), F32) for _ in range(N_HEADS)),
                 tuple(jnp.zeros((tq, 1), F32) for _ in range(N_HEADS)), jnp.zeros((tq, width), F32))
        for d in reversed(range(band)):
            carry = block(band * i + d, carry, _sb_valid(d, tq, t))

        def alive(c):
            return functools.reduce(jnp.maximum, [jnp.max(r) for r in c[1]])

        def step(state):
            n, _, c = state
            c = block(band * i - 1 - n, c, None)
            return n + 1, alive(c), c

        n_done, _, carry = lax.while_loop(lambda st: jnp.logical_and(st[0] < band * i, st[1] > EXP_UNDERFLOW),
                                          step, (jnp.int32(0), alive(carry), carry))
        cnt_ref[i] = n_done
        for p in range(2):
            o_ref[:, p * PAIR:(p + 1) * PAIR] = _take_heads([carry[0][2 * p + e] for e in range(2)], p)
        rm_ref[...] = carry[2]

    return pl.pallas_call(
        kern, name=name, grid=(s // tq,),
        in_specs=pp.q_specs + pp.k_specs + pp.v_specs + [pl.BlockSpec((t, t), lambda i: (0, 0))],
        out_specs=[_row_spec(tq, 0, GROUP_W), _row_spec(tq, 0, width), pl.BlockSpec(memory_space=pltpu.SMEM)],
        out_shape=[jax.ShapeDtypeStruct((s, GROUP_W), F32), jax.ShapeDtypeStruct((s, width), F32),
                   jax.ShapeDtypeStruct((s // tq,), jnp.int32)],
        compiler_params=_att_params(False),
    )(*(pp.q_args + pp.k_args + pp.v_args + [_tri(t, "row_gt_col")]))


def _sb_bwd(src, q_blk, k_blk, v_blk, do, do_blk, rm, visited, s, scale, name):
    assert _is_pow2(scale)
    tq, t = min(SB_QUERY_TILE, s), min(ATT_TILE, s)
    band = tq // t
    pair = lambda blk: [[(src, blk + p, "pair")] for p in range(2)]
    pp = _Parts(pair(q_blk), pair(k_blk), [(src, v_blk + p) for p in range(2)], tq, s)

    def kern(*refs):
        q_refs, k_refs, v_refs, (do_ref, rm_ref, tri_ref, pre_ref, cnt_ref, dq_ref, dk_ref, dv_ref) = pp.split(refs)
        i = pl.program_id(0)

        @pl.when(i == 0)
        def _():
            dk_ref[...] = jnp.zeros_like(dk_ref)
            dv_ref[...] = jnp.zeros_like(dv_ref)

        rmb = rm_ref[...]
        tri = tri_ref[...]
        pre = pre_ref[...]
        qm, dom = [], []
        for p in range(2):
            qm += _masked_heads([q_refs[p][0][...]], ["pair"], p, MXU_DTYPE, scale)
            dom += _masked_heads([do_ref[:, p * PAIR:(p + 1) * PAIR]], ["pair"], p, MXU_DTYPE)

        def block(j, carry, valid):
            dqs, lefts = carry
            off = _blk_off(j, t)
            kbs = [k_refs[p][0][pl.ds(off, t), :] for p in range(2)]
            vbs = [v_refs[p][pl.ds(off, t), :] for p in range(2)]
            logs = [_sb_logs(qm[h], kbs[h // 2], valid) for h in range(N_HEADS)]
            tails = _tri_sums([lg[0] for lg in logs], tri)
            ws, gs = [], []
            for h in range(N_HEADS):
                lk, ls = logs[h]
                w = jnp.exp(ls + tails[h] + _col(rmb, (rm.shape[1] // N_HEADS) * h + j))
                if valid is not None:
                    w = jnp.where(valid, w, 0.0)
                ws.append(w)
                gs.append(_dot_nt(dom[h], vbs[h // 2]) * w)
            prefix = _tri_sums(gs, pre)
            new_dq, new_left = [], []
            dk_acc = [jnp.zeros((t, PAIR), F32) for _ in range(2)]
            dv_acc = [jnp.zeros((t, PAIR), F32) for _ in range(2)]
            for h in range(N_HEADS):
                lk, ls = logs[h]
                sig = jnp.exp(ls)
                dz = gs[h] * (1.0 - sig) - sig * (prefix[h] + lefts[h])
                if valid is not None:
                    dz = jnp.where(valid, dz, 0.0)
                dzb = dz.astype(MXU_DTYPE)
                dv_acc[h // 2] = dv_acc[h // 2] + _dot_tn(ws[h].astype(MXU_DTYPE), dom[h])
                dk_acc[h // 2] = dk_acc[h // 2] + _dot_tn(dzb, qm[h])
                new_dq.append(dqs[h] + _dot(dzb, kbs[h // 2]))
                new_left.append(lefts[h] + jnp.sum(gs[h], axis=1, keepdims=True))
            for p in range(2):
                dv_ref[pl.ds(off, t), p * PAIR:(p + 1) * PAIR] += dv_acc[p]
                dk_ref[pl.ds(off, t), p * PAIR:(p + 1) * PAIR] += dk_acc[p]
            return tuple(new_dq), tuple(new_left)

        carry = (tuple(jnp.zeros((tq, PAIR), F32) for _ in range(N_HEADS)),
                 tuple(jnp.zeros((tq, 1), F32) for _ in range(N_HEADS)))
        carry = lax.fori_loop(band * i - cnt_ref[i], band * i, lambda j, c: block(j, c, None), carry)
        for d in range(band):
            carry = block(band * i + d, carry, _sb_valid(d, tq, t))
        for p in range(2):
            dq_ref[:, p * PAIR:(p + 1) * PAIR] = _take_heads([carry[0][2 * p + e] * scale for e in range(2)], p)

    mspec = pl.BlockSpec((t, t), lambda i: (0, 0))
    acc_spec = pl.BlockSpec((s, GROUP_W), lambda i: (0, 0), pipeline_mode=pl.Buffered(1))
    return pl.pallas_call(
        kern, name=name, grid=(s // tq,),
        in_specs=pp.q_specs + pp.k_specs + pp.v_specs + [_row_spec(tq, do_blk, GROUP_W), _row_spec(tq, 0, rm.shape[1]), mspec, mspec,
                                                         pl.BlockSpec(memory_space=pltpu.SMEM)],
        out_specs=[_row_spec(tq, 0, GROUP_W), acc_spec, acc_spec],
        out_shape=[jax.ShapeDtypeStruct((s, GROUP_W), F32)] * 3,
        compiler_params=_att_params(False),
    )(*(pp.q_args + pp.k_args + pp.v_args + [do, rm, _tri(t, "row_gt_col"), _tri(t, "row_lt_col"), visited]))


def _split_w_in(w):
    col = lambda n: w[:, _OFF[n]:_OFF[n + 1]]
    wa = jnp.concatenate([col(0), col(1), col(2), col(4), col(5), col(6), col(10)], axis=1)
    misc = jnp.concatenate([col(3), col(9), jnp.zeros((w.shape[0], 128 - 4 - MLA_ROPE), w.dtype)], axis=1)
    wb = jnp.concatenate([col(11), col(7), col(8), misc], axis=1)
    return wa, wb


def _merge_dw_in(dwp):
    a = lambda n: dwp[:, n * GROUP_W:(n + 1) * GROUP_W]
    b0 = PA_COLS
    gate = dwp[:, b0:b0 + 1024]
    cq = dwp[:, b0 + 1024:b0 + 1280]
    ckv = dwp[:, b0 + 1280:b0 + 1408]
    flog = dwp[:, b0 + 1408:b0 + 1412]
    krot = dwp[:, b0 + 1408 + MISC_KROT:b0 + 1408 + MISC_KROT + MLA_ROPE]
    return jnp.concatenate([a(0), a(1), a(2), flog, a(3), a(4), a(5), cq, ckv, krot, a(6), gate], axis=1)


def _heads_first(w, per_head, first):
    r = w.shape[0]
    w3 = w.reshape(r, N_HEADS, per_head)
    return jnp.concatenate([w3[:, :, :first].reshape(r, -1), w3[:, :, first:].reshape(r, -1)], axis=1)


def _heads_interleaved(w, per_head, first):
    r = w.shape[0]
    a = w[:, :N_HEADS * first].reshape(r, N_HEADS, first)
    b = w[:, N_HEADS * first:].reshape(r, N_HEADS, per_head - first)
    return jnp.concatenate([a, b], axis=2).reshape(r, N_HEADS * per_head)


def _pad_rows8(a):
    return a[:, :8].T


def _local_step(x2, mem2, tgt, p):
    s = x2.shape[0]
    nm = mem2.shape[0]
    head_scale = HEAD_DIM ** -0.5
    mla_scale = (HEAD_DIM + MLA_ROPE) ** -0.5
    tabs = _rope_tables(s)
    mats = _rope_matrices()
    pairs = lambda arr, blk: [[(arr, blk + q, "pair")] for q in range(2)]
    vals = lambda arr, blk: [(arr, blk + q) for q in range(2)]

    h, hb = _ln_fwd(x2, p["ln_in_g"], p["ln_in_b"], "ln_in_fwd")
    _, memn_b = _ln_fwd(mem2, p["mem_ln_g"], p["mem_ln_b"], "ln_mem_fwd")

    saved = []
    for l in range(DEPTH):
        wa, wb = _split_w_in(p["w_in"][l])
        wp = jnp.concatenate([wa, wb], axis=1)
        wq_up = _heads_first(p["w_mla_q_up"][l], HEAD_DIM + MLA_ROPE, HEAD_DIM)
        wkv_up = _heads_first(p["w_mla_kv_up"][l], 2 * HEAD_DIM, HEAD_DIM)
        bias_row = jnp.pad(p["b_forget"][l], (0, 128 - N_HEADS)).reshape(1, 128)
        pa, pb, fox_kmax = _proj_in(hb, wa, wb, GROUP_W)

        fc = _forget_fwd(pb, bias_row)
        fbias = (fc, _pad_rows8(fc), fox_kmax)
        o_fox, lse_fox = _softmax_fwd(pairs(pa, 0), pairs(pa, 2), vals(pa, 4), s, s, head_scale, True, fbias,
                                      "fox_fwd")
        o_sb, *rm_sb = _sb_fwd(pa, 6, 8, 10, s, head_scale, "sb_fwd")

        qfull, cqn = _mla_q_fwd(pb, p["mla_q_norm_g"][l], wq_up, tabs, mats)
        kv, ckvn, kr4 = _mla_kv_fwd(pb, p["mla_kv_norm_g"][l], wkv_up, tabs, mats)
        mla_q = [[(qfull, q, "pair"), (qfull, 2, "quad")] for q in range(2)]
        mla_k = [[(kv, q, "pair"), (kr4, 0, "quad")] for q in range(2)]
        o_mla, lse_mla = _softmax_fwd(mla_q, mla_k, vals(kv, 2), s, s, mla_scale, True, None, "mla_fwd")

        mkv = _matmul(memn_b, p["w_mem_kv"][l], MXU_DTYPE, "mem_kv")
        o_mem, lse_mem = _softmax_fwd(pairs(pa, 12), pairs(mkv, 0), vals(mkv, 2), s, nm, head_scale, False, None,
                                      "mem_fwd")

        groups = (o_fox, o_sb, o_mla, o_mem)
        if l < DEPTH - 1:
            gated, u, h_next, hb_next = _gate_out_proj_ln(groups, pb, p["w_out"][l], h, p["ln_g"][l], p["ln_b"][l])
        else:
            gated, u, dh, sq_cols = _gate_out_proj_ln(groups, pb, p["w_out"][l], h, p["ln_g"][l], p["ln_b"][l], tgt)
        saved.append(dict(u=u, hb=hb, wp=wp, wq_up=wq_up, wkv_up=wkv_up, bias_row=bias_row, pa=pa, pb=pb,
                          fbias=fbias, lse_fox=lse_fox, rm_sb=rm_sb, cqn=cqn, ckvn=ckvn, mla_q=mla_q, mla_k=mla_k,
                          kv=kv, lse_mla=lse_mla, mkv=mkv, lse_mem=lse_mem, groups=groups, gated=gated))
        if l < DEPTH - 1:
            h, hb = h_next, hb_next

    loss_sum = jnp.sum(sq_cols)

    grads = {k: [None] * DEPTH for k in ("w_in", "b_forget", "mla_q_norm_g", "w_mla_q_up", "mla_kv_norm_g",
                                         "w_mla_kv_up", "w_mem_kv", "w_out", "ln_g", "ln_b")}
    dmemn = []
    dy1, dy2, c1 = dh, None, 1.0
    for l in reversed(range(DEPTH)):
        r = saved[l]
        pa, pb = r["pa"], r["pb"]
        o_fox, o_sb, o_mla, o_mem = r["groups"]
        du, du_b, dmixed, dgate_b, dg, db = _ln_bwd(r["u"], p["ln_g"][l], dy1, dy2, c1, "ln_gate_bwd",
                                                    (p["w_out"][l], r["groups"], pb))
        grads["ln_g"][l], grads["ln_b"][l] = dg[0], db[0]
        grads["w_out"][l] = _matmul(r["gated"], du_b, F32, "out_proj_dw", "tn")

        dfq, dfk, dfv, dfc_q, dfc_k = _softmax_bwd(pairs(pa, 0), pairs(pa, 2), vals(pa, 4), o_fox, r["lse_fox"],
                                                   dmixed, 0, s, s, head_scale, True, r["fbias"], "fox_bwd")
        dmisc_f, dbf = _forget_bwd(pb, r["bias_row"], dfc_q + jnp.pad(dfc_k.T, ((0, 0), (0, 128 - 8))))
        grads["b_forget"][l] = dbf[0, :N_HEADS]

        dsq, dsk, dsv = _sb_bwd(pa, 6, 8, 10, dmixed, 1, *r["rm_sb"], s, head_scale, "sb_bwd")

        dqm, dkm, dvm = _softmax_bwd(r["mla_q"], r["mla_k"], vals(r["kv"], 2), o_mla, r["lse_mla"], dmixed, 2,
                                     s, s, mla_scale, True, None, "mla_bwd")
        dq_mla_b, dcq_b, dgq = _mla_q_bwd(dqm, pb, p["mla_q_norm_g"][l], r["wq_up"], tabs, mats)
        grads["w_mla_q_up"][l] = _heads_interleaved(_matmul(r["cqn"], dq_mla_b, F32, "q_up_dw", "tn"),
                                                    HEAD_DIM + MLA_ROPE, HEAD_DIM)
        grads["mla_q_norm_g"][l] = dgq[0]
        dkv_b, dckv_b, dmisc_k, dgkv = _mla_kv_bwd(dkm, dvm, pb, p["mla_kv_norm_g"][l], r["wkv_up"], tabs, mats)
        grads["w_mla_kv_up"][l] = _heads_interleaved(_matmul(r["ckvn"], dkv_b, F32, "kv_up_dw", "tn"),
                                                     2 * HEAD_DIM, HEAD_DIM)
        grads["mla_kv_norm_g"][l] = dgkv[0]

        dmq, dmk, dmv = _softmax_bwd(pairs(pa, 12), pairs(r["mkv"], 0), vals(r["mkv"], 2), o_mem, r["lse_mem"],
                                     dmixed, 3, s, nm, head_scale, False, None, "mem_bwd")
        dmkv_b = jnp.concatenate([dmk, dmv], axis=1).astype(MXU_DTYPE)
        grads["w_mem_kv"][l] = _matmul(memn_b, dmkv_b, F32, "mem_kv_dw", "tn")
        dmemn.append(_matmul(dmkv_b, p["w_mem_kv"][l], F32, "mem_kv_dx", "nt"))

        dp = [dfq, dfk, dfv, dsq, dsk, dsv, dmq, dgate_b, dcq_b, dckv_b, (dmisc_f, dmisc_k)]
        dhproj = _proj_dx(dp, r["wp"])
        grads["w_in"][l] = _merge_dw_in(_proj_dw(r["hb"], dp))
        dy1, dy2, c1 = du, dhproj, ALPHA

    dx, dg_in, db_in = _ln_bwd(x2, p["ln_in_g"], dy1, dy2, c1, "ln_in_bwd")
    _, dg_mem, db_mem = _ln_bwd(mem2, p["mem_ln_g"], dmemn[0], dmemn[1], 1.0, "ln_mem_bwd")
    out = {k: jnp.stack(v) for k, v in grads.items()}
    out.update(ln_in_g=dg_in[0], ln_in_b=db_in[0], mem_ln_g=dg_mem[0], mem_ln_b=db_mem[0])
    return loss_sum, dx, out


WIDE = "w_in"
FLAT_NAMES = ("w_out", "w_mem_kv", "w_mla_q_up", "w_mla_kv_up")
FLAT_ROWS = 896
BIG_NAMES = (WIDE,) + FLAT_NAMES
BIG_AXIS = dict(w_in=2, w_out=1, w_mem_kv=1, w_mla_q_up=2, w_mla_kv_up=2)
SMALL_NAMES = ("ln_in_g", "ln_in_b", "mem_ln_g", "mem_ln_b", "ln_g", "ln_b", "b_forget", "mla_q_norm_g",
               "mla_kv_norm_g")
ALL_NAMES = ("ln_in_g", "ln_in_b", "mem_ln_g", "mem_ln_b", "w_in", "b_forget", "mla_q_norm_g", "w_mla_q_up",
             "mla_kv_norm_g", "w_mla_kv_up", "w_mem_kv", "w_out", "ln_g", "ln_b")
N_CHIPS = 4
N_DEV = 8


def _rows_of(shape):
    rows = -(-int(np.prod(shape)) // LANES)
    return -(-rows // PACK_ALIGN) * PACK_ALIGN


def _pack(arrs, rows):
    parts = []
    for a in arrs:
        f = a.reshape(-1)
        n = _rows_of(a.shape) * LANES
        parts.append(jnp.pad(f, (0, n - f.shape[0])).reshape(-1, LANES))
    used = sum(q.shape[0] for q in parts)
    if rows > used:
        parts.append(jnp.zeros((rows - used, LANES), parts[0].dtype))
    return jnp.concatenate(parts, axis=0)


def _unpack(buf, shapes):
    out, r = [], 0
    for shp in shapes:
        n = _rows_of(shp)
        out.append(buf[r:r + n].reshape(-1)[:int(np.prod(shp))].reshape(shp))
        r += n
    return out


def _sharded_pair(get):
    wide = get(WIDE)
    return [wide.reshape(-1, wide.shape[-1]), _pack([get(n) for n in FLAT_NAMES], FLAT_ROWS)]


HBM_SPEC = pl.BlockSpec(memory_space=pltpu.HBM)


def _gather_weights(shards):
    n = len(shards)

    def body(*refs):
        w_refs, out_refs, (send_sems, recv_sems, local_sems) = refs[:n], refs[n:2 * n], refs[2 * n:]
        x, y, c = (lax.axis_index(a) for a in MESH_AXES)
        me, sibling = 2 * x + y, (x, y, 1 - c)
        chips = [(1 - x, y), (x, 1 - y), (1 - x, 1 - y)]
        local, first, passed = [], [], []
        for a in range(n):
            w_ref, out_ref, half = w_refs[a], out_refs[a], shards[a].shape[0] // 2

            def part(chip, core, out_ref=out_ref, half=half):
                return out_ref.at[chip, pl.ds(core * half, half)]

            def copy(k, src, dst, to, a=a):
                return pltpu.make_async_remote_copy(
                    src_ref=src, dst_ref=dst, send_sem=send_sems.at[6 * a + k], recv_sem=recv_sems.at[6 * a + k],
                    device_id=to, device_id_type=pl.DeviceIdType.MESH)

            local.append(pltpu.make_async_copy(w_ref, out_ref.at[me], local_sems.at[a]))
            local[-1].start()
            mine = [copy(k, w_ref.at[pl.ds(c * half, half)], part(me, c), (px, py, c))
                    for k, (px, py) in enumerate(chips)]
            for cp in mine:
                cp.start()
            first.append((mine, part, copy))
        for mine, part, copy in first:
            for k, (px, py) in enumerate(chips):
                copy(k, part(me, c), part(2 * px + py, c), (px, py, c)).wait_recv()
                passed.append(copy(3 + k, part(2 * px + py, c), part(2 * px + py, c), sibling))
                passed[-1].start()
        for mine, part, copy in first:
            for k, (px, py) in enumerate(chips):
                copy(3 + k, part(me, c), part(2 * px + py, 1 - c), sibling).wait_recv()
        for cp in [cp for mine, _, _ in first for cp in mine] + passed:
            cp.wait_send()
        for cp in local:
            cp.wait()

    return pl.pallas_call(
        body, name="gather_weights",
        out_shape=[jax.ShapeDtypeStruct((N_CHIPS,) + s.shape, s.dtype) for s in shards],
        in_specs=[HBM_SPEC] * n, out_specs=[HBM_SPEC] * n,
        scratch_shapes=[pltpu.SemaphoreType.DMA((6 * n,)), pltpu.SemaphoreType.DMA((6 * n,)),
                        pltpu.SemaphoreType.DMA((n,))],
    )(*shards)


def _exchange_grads(bigs, small):
    nb = len(bigs)
    halves = [b.shape[1] // 2 for b in bigs]
    chunks = [_pick(h, (128, 64, 32, 16)) for h in halves]

    def body(*refs):
        big_refs, small_ref = refs[:nb], refs[nb]
        sum_refs, out_refs = refs[nb + 1:2 * nb + 1], refs[2 * nb + 1:3 * nb + 2]
        scratch = refs[3 * nb + 2:]
        land_bufs, sum_bufs = scratch[:nb], scratch[nb:2 * nb]
        send_sems, recv_sems, local_sems, load_sems, swap_send, swap_recv, keep_sems = scratch[2 * nb:]
        x, y, c = (lax.axis_index(a) for a in MESH_AXES)
        me, my_chip = 4 * x + 2 * y + c, 2 * x + y
        flips = [(fx, fy, fc) for fx in (0, 1) for fy in (0, 1) for fc in (0, 1) if fx or fy or fc]
        peers = [(1 - x if fx else x, 1 - y if fy else y, 1 - c if fc else c) for fx, fy, fc in flips]

        sources = [lambda chip, core, r=big_refs[a], half=bigs[a].shape[1] // 2: r.at[chip, pl.ds(core * half, half)]
                   for a in range(nb)] + [lambda chip, core: small_ref]

        def copy(a, k, src, slot, to):
            return pltpu.make_async_remote_copy(
                src_ref=src, dst_ref=out_refs[a].at[slot], send_sem=send_sems.at[7 * a + k],
                recv_sem=recv_sems.at[7 * a + k], device_id=to, device_id_type=pl.DeviceIdType.MESH)

        own = [pltpu.make_async_copy(src(my_chip, c), out_refs[a].at[me], local_sems.at[a])
               for a, src in enumerate(sources)]
        for cp in own:
            cp.start()
        sends = [copy(a, k, src(2 * px + py, pc), me, (px, py, pc))
                 for a, src in enumerate(sources) for k, (px, py, pc) in enumerate(peers)]
        for cp in sends:
            cp.start()
        for a, src in enumerate(sources):
            for k, (px, py, pc) in enumerate(peers):
                copy(a, k, src(my_chip, c), 4 * px + 2 * py + pc, (px, py, pc)).wait_recv()
        for cp in sends:
            cp.wait_send()
        for cp in own:
            cp.wait()

        tails = []
        loads = [pltpu.make_async_copy(out_refs[a], land_bufs[a], load_sems.at[a]) for a in range(nb)]
        for load in loads:
            load.start()
        for a in range(nb):
            loads[a].wait()
            for r0 in range(0, halves[a], chunks[a]):
                rows = pl.ds(r0, chunks[a])
                total = land_bufs[a][0, rows, :].astype(F32)
                for d in range(1, N_DEV):
                    total = total + land_bufs[a][d, rows, :].astype(F32)
                sum_bufs[a][rows, :] = total
            keep = pltpu.make_async_copy(sum_bufs[a], sum_refs[a].at[c], keep_sems.at[a])
            give = pltpu.make_async_remote_copy(
                src_ref=sum_bufs[a], dst_ref=sum_refs[a].at[c], send_sem=swap_send.at[a], recv_sem=swap_recv.at[a],
                device_id=(x, y, 1 - c), device_id_type=pl.DeviceIdType.MESH)
            keep.start()
            give.start()
            tails.append((keep, give))
        for a, (keep, give) in enumerate(tails):
            pltpu.make_async_remote_copy(
                src_ref=sum_bufs[a], dst_ref=sum_refs[a].at[1 - c], send_sem=swap_send.at[a], recv_sem=swap_recv.at[a],
                device_id=(x, y, 1 - c), device_id_type=pl.DeviceIdType.MESH).wait_recv()
            give.wait_send()
            keep.wait()

    return pl.pallas_call(
        body, name="exchange_grads",
        out_shape=[jax.ShapeDtypeStruct((2, h, b.shape[2]), F32) for h, b in zip(halves, bigs)]
        + [jax.ShapeDtypeStruct((N_DEV, h, b.shape[2]), b.dtype) for h, b in zip(halves, bigs)]
        + [jax.ShapeDtypeStruct((N_DEV,) + small.shape, small.dtype)],
        in_specs=[HBM_SPEC] * (nb + 1), out_specs=[HBM_SPEC] * (2 * nb + 1),
        scratch_shapes=[pltpu.VMEM((N_DEV, h, b.shape[2]), b.dtype) for h, b in zip(halves, bigs)]
        + [pltpu.VMEM((h, b.shape[2]), F32) for h, b in zip(halves, bigs)]
        + [pltpu.SemaphoreType.DMA((7 * (nb + 1),)), pltpu.SemaphoreType.DMA((7 * (nb + 1),)),
           pltpu.SemaphoreType.DMA((nb + 1,)), pltpu.SemaphoreType.DMA((nb,)), pltpu.SemaphoreType.DMA((nb,)),
           pltpu.SemaphoreType.DMA((nb,)), pltpu.SemaphoreType.DMA((nb,))],
        compiler_params=pltpu.CompilerParams(vmem_limit_bytes=VMEM_LIMIT),
    )(*bigs, small)


def _adamw(parts, w, m, v, name):
    rows, width = w.shape
    n_parts = parts.shape[0]
    tile = _pick(rows, (128, 16, 8))
    bc1 = 1.0 - ADAM_B1 ** ADAM_STEP
    bc2 = 1.0 - ADAM_B2 ** ADAM_STEP

    def kern(p_ref, w_ref, m_ref, v_ref, g_ref, d_ref, nm_ref, nv_ref):
        g = p_ref[0].astype(F32)
        for d in range(1, n_parts):
            g = g + p_ref[d].astype(F32)
        nm = ADAM_B1 * m_ref[...] + (1.0 - ADAM_B1) * g
        nv = ADAM_B2 * v_ref[...] + (1.0 - ADAM_B2) * (g * g)
        g_ref[...] = g
        nm_ref[...] = nm
        nv_ref[...] = nv
        d_ref[...] = -ADAM_LR * ((nm / bc1) / (jnp.sqrt(nv / bc2) + ADAM_EPS) + ADAM_WD * w_ref[...])

    spec = pl.BlockSpec((tile, width), lambda i: (i, 0))
    return pl.pallas_call(
        kern, name=name, grid=(rows // tile,),
        in_specs=[pl.BlockSpec((n_parts, tile, width), lambda i: (0, i, 0)), spec, spec, spec],
        out_specs=[spec] * 4, out_shape=[jax.ShapeDtypeStruct((rows, width), F32)] * 4,
        compiler_params=pltpu.CompilerParams(dimension_semantics=("parallel",), vmem_limit_bytes=VMEM_LIMIT),
    )(parts, w, m, v)


def kernel(x, mem, ln_in_g, ln_in_b, mem_ln_g, mem_ln_b, w_in, b_forget, mla_q_norm_g, w_mla_q_up, mla_kv_norm_g, w_mla_kv_up, w_mem_kv, w_out, ln_g, ln_b, loss_target, m_ln_in_g, m_ln_in_b, m_mem_ln_g, m_mem_ln_b, m_w_in, m_b_forget, m_mla_q_norm_g, m_w_mla_q_up, m_mla_kv_norm_g, m_w_mla_kv_up, m_w_mem_kv, m_w_out, m_ln_g, m_ln_b, v_ln_in_g, v_ln_in_b, v_mem_ln_g, v_mem_ln_b, v_w_in, v_b_forget, v_mla_q_norm_g, v_w_mla_q_up, v_mla_kv_norm_g, v_w_mla_kv_up, v_w_mem_kv, v_w_out, v_ln_g, v_ln_b):
    w = dict(ln_in_g=ln_in_g, ln_in_b=ln_in_b, mem_ln_g=mem_ln_g, mem_ln_b=mem_ln_b, w_in=w_in, b_forget=b_forget,
             mla_q_norm_g=mla_q_norm_g, w_mla_q_up=w_mla_q_up, mla_kv_norm_g=mla_kv_norm_g,
             w_mla_kv_up=w_mla_kv_up, w_mem_kv=w_mem_kv, w_out=w_out, ln_g=ln_g, ln_b=ln_b)
    mo = dict(ln_in_g=m_ln_in_g, ln_in_b=m_ln_in_b, mem_ln_g=m_mem_ln_g, mem_ln_b=m_mem_ln_b, w_in=m_w_in,
              b_forget=m_b_forget, mla_q_norm_g=m_mla_q_norm_g, w_mla_q_up=m_w_mla_q_up,
              mla_kv_norm_g=m_mla_kv_norm_g, w_mla_kv_up=m_w_mla_kv_up, w_mem_kv=m_w_mem_kv, w_out=m_w_out,
              ln_g=m_ln_g, ln_b=m_ln_b)
    vo = dict(ln_in_g=v_ln_in_g, ln_in_b=v_ln_in_b, mem_ln_g=v_mem_ln_g, mem_ln_b=v_mem_ln_b, w_in=v_w_in,
              b_forget=v_b_forget, mla_q_norm_g=v_mla_q_norm_g, w_mla_q_up=v_w_mla_q_up,
              mla_kv_norm_g=v_mla_kv_norm_g, w_mla_kv_up=v_w_mla_kv_up, w_mem_kv=v_w_mem_kv, w_out=v_w_out,
              ln_g=v_ln_g, ln_b=v_ln_b)
    flat_shapes = [w[n].shape for n in FLAT_NAMES]
    small_shapes = [w[n].shape for n in SMALL_NAMES]

    got_wide, got_flat = _gather_weights(_sharded_pair(lambda n: w[n].astype(MXU_DTYPE)))
    full = dict(w)
    full[WIDE] = jnp.concatenate([got_wide[j] for j in range(N_CHIPS)], axis=1).reshape(
        w[WIDE].shape[:2] + (N_CHIPS * w[WIDE].shape[2],))
    per_chip = [_unpack(got_flat[j], flat_shapes) for j in range(N_CHIPS)]
    for idx, n in enumerate(FLAT_NAMES):
        full[n] = jnp.concatenate([per_chip[j][idx] for j in range(N_CHIPS)], axis=BIG_AXIS[n])

    loss_sum, dx, g = _local_step(x[0], mem[0], loss_target[0], full)
    loss = lax.psum(loss_sum * (0.5 / D_MODEL), MESH_AXES)

    def shard_of(n, j):
        ax, size = BIG_AXIS[n], w[n].shape[BIG_AXIS[n]]
        return lax.slice_in_dim(g[n], j * size, (j + 1) * size, axis=ax).astype(MXU_DTYPE)

    per_dest = [_sharded_pair(lambda n, j=j: shard_of(n, j)) for j in range(N_CHIPS)]
    bigs = [jnp.stack([per_dest[j][a] for j in range(N_CHIPS)]) for a in range(2)]
    exchanged = _exchange_grads(bigs, _pack([g[n] for n in SMALL_NAMES], SMALL_ROWS))
    halves, small_parts = exchanged[:2], exchanged[-1]

    res = []
    for a, (grad, nm) in enumerate(zip(halves, ("adamw_wide", "adamw_flat"))):
        state = [_sharded_pair(lambda n, src=src: src[n])[a] for src in (w, mo, vo)]
        res.append(_adamw(grad.reshape((1,) + state[0].shape), *state, nm))
    res_small = _adamw(small_parts, *[_pack([src[n] for n in SMALL_NAMES], SMALL_ROWS) for src in (w, mo, vo)],
                       "adamw_replicated")
    outs = []
    for kind in range(4):
        vals = {WIDE: res[0][kind].reshape(w[WIDE].shape)}
        vals.update(zip(FLAT_NAMES, _unpack(res[1][kind], flat_shapes)))
        vals.update(zip(SMALL_NAMES, _unpack(res_small[kind], small_shapes)))
        outs += [vals[n] for n in ALL_NAMES]
    return (loss, dx[None], *outs)
```

```python
import functools

import numpy as np
import jax
import jax.numpy as jnp
from jax import lax
from jax.experimental import pallas as pl
from jax.experimental.pallas import tpu as pltpu

F32 = jnp.float32
MXU_DTYPE = jnp.bfloat16

DEPTH = 2
D_MODEL = 1024
GROUP_W = 256
N_HEADS = 4
HEAD_DIM = 64
MLA_Q_RANK = 256
MLA_KV_RANK = 128
MLA_ROPE = 32
MLA_Q_COLS = N_HEADS * (HEAD_DIM + MLA_ROPE)
MLA_KV_COLS = N_HEADS * 2 * HEAD_DIM
ROPE_THETA = 10000.0
LN_EPS = 1e-5
RMS_EPS = 1e-6
ALPHA = (2 * DEPTH) ** 0.25
ADAM_LR, ADAM_B1, ADAM_B2, ADAM_EPS, ADAM_WD, ADAM_STEP = 0.001, 0.9, 0.999, 1e-08, 0.01, 10

_SPLIT = (256, 256, 256, 4, 256, 256, 256, 256, 128, 32, 256, 1024)
_OFF = [int(o) for o in np.cumsum((0,) + _SPLIT)]
IN_COLS = _OFF[-1]
PA_COLS = 7 * GROUP_W
PB_COLS = 1024 + 256 + 128 + 128
PB_CQ_BLK, PB_CKV_BLK, PB_MISC_BLK = 4, 10, 11
MISC_KROT = 4

LANES = 1024
PACK_ALIGN = 16
SMALL_ROWS = 144
ROW_TILE = 1024
PROJ_BWD_ROWS = 512
ATT_TILE = 256
SB_QUERY_TILE = 256
SOFTMAX_TILE = 512
SOFTMAX_FWD_TILE = 1024
DENSE_QUERY_TILE = 1024
FWD_PAIRS_PER_LOOP = 1
BWD_PAIRS_PER_LOOP = 2
PAIR = 128
SB_SLOT = PAIR // N_HEADS
VMEM_LIMIT = 56 * 1024 * 1024
MATMUL_VMEM = 30 * 1024 * 1024
NEG = -1e30
LOG2E = 1.4426950408889634
LN2 = 0.6931471805599453
EXP_UNDERFLOW = -104.0
DEAD_LOGIT = -110.0
REACH_SLACK = 1.0 + 2.0 ** -10
MESH_AXES = ("x", "y", "c")


def _dot(a, b):
    return jnp.dot(a, b, preferred_element_type=F32)


def _dot_nt(a, b):
    return lax.dot_general(a, b, (((1,), (1,)), ((), ())), preferred_element_type=F32)


def _dot_tn(a, b):
    return lax.dot_general(a, b, (((0,), (0,)), ((), ())), preferred_element_type=F32)


def _split2(x):
    hi = x.astype(MXU_DTYPE)
    lo = (x - hi.astype(F32)).astype(MXU_DTYPE)
    return hi, lo


def _split3(x):
    hi = x.astype(MXU_DTYPE)
    r = x - hi.astype(F32)
    mid = r.astype(MXU_DTYPE)
    lo = (r - mid.astype(F32)).astype(MXU_DTYPE)
    return hi, mid, lo


def _dot_exact_r(x, pm):
    hi, mid, lo = _split3(x)
    return _dot(hi, pm) + _dot(mid, pm) + _dot(lo, pm)


def _dot_exact_l(pm, x):
    hi, mid, lo = _split3(x)
    return _dot(pm, hi) + _dot(pm, mid) + _dot(pm, lo)


def _pick(dim, prefs):
    for p in prefs:
        if dim % p == 0:
            return p
    return dim


def _softplus(z):
    return jnp.maximum(z, 0.0) + jnp.log(1.0 + jnp.exp(-jnp.abs(z)))


def _tile_options(dim):
    opts = [d for d in range(128, min(dim, 2048) + 1, 128) if dim % d == 0]
    return opts or [dim]


def _matmul_tiles(m, n, k, out_bytes):
    tk = k if k <= 4096 else _pick(k, (1024, 512, 256, 128))
    best = None
    for tm in _tile_options(m):
        for tn in _tile_options(n):
            vmem = 2 * 2 * (tm * tk + tk * tn) + 4 * tm * tn + 2 * out_bytes * tm * tn
            if vmem <= MATMUL_VMEM and (best is None or tm * tn / (tm + tn) > best[0]):
                best = (tm * tn / (tm + tn), tm, tn)
    return best[1], best[2], tk


def _matmul(a, b, out_dtype, name, mode="nn"):
    m, k = (a.shape[1], a.shape[0]) if mode == "tn" else a.shape
    n = b.shape[0] if mode == "nt" else b.shape[1]
    tm, tn, tk = _matmul_tiles(m, n, k, jnp.dtype(out_dtype).itemsize)
    nk = k // tk
    dot = {"nn": _dot, "tn": _dot_tn, "nt": _dot_nt}[mode]

    def kern(a_ref, b_ref, o_ref, *acc):
        if nk == 1:
            o_ref[...] = dot(a_ref[...], b_ref[...]).astype(o_ref.dtype)
            return
        acc_ref, = acc
        kk = pl.program_id(2)

        @pl.when(kk == 0)
        def _():
            acc_ref[...] = jnp.zeros_like(acc_ref)

        acc_ref[...] += dot(a_ref[...], b_ref[...])

        @pl.when(kk == nk - 1)
        def _():
            o_ref[...] = acc_ref[...].astype(o_ref.dtype)

    a_spec = (pl.BlockSpec((tk, tm), lambda i, j, kk: (kk, i)) if mode == "tn"
              else pl.BlockSpec((tm, tk), lambda i, j, kk: (i, kk)))
    b_spec = (pl.BlockSpec((tn, tk), lambda i, j, kk: (j, kk)) if mode == "nt"
              else pl.BlockSpec((tk, tn), lambda i, j, kk: (kk, j)))
    return pl.pallas_call(
        kern, name=name, grid=(m // tm, n // tn, nk), in_specs=[a_spec, b_spec],
        out_specs=pl.BlockSpec((tm, tn), lambda i, j, kk: (i, j)),
        out_shape=jax.ShapeDtypeStruct((m, n), out_dtype),
        scratch_shapes=[pltpu.VMEM((tm, tn), F32)] if nk > 1 else [],
        compiler_params=pltpu.CompilerParams(
            dimension_semantics=("parallel", "parallel", "arbitrary"), vmem_limit_bytes=VMEM_LIMIT),
    )(a.astype(MXU_DTYPE), b.astype(MXU_DTYPE))


def _proj_in(hb, wa, wb, key_cols):
    rows, d = hb.shape
    tm = min(PROJ_BWD_ROWS, rows)

    def kern(a_ref, wa_ref, wb_ref, pa_ref, pb_ref, kmax_ref):
        @pl.when(pl.program_id(0) == 0)
        def _():
            kmax_ref[...] = jnp.zeros_like(kmax_ref)

        a = a_ref[...]
        pa = _dot(a, wa_ref[...]).astype(pa_ref.dtype)
        pa_ref[...] = pa
        pb_ref[...] = _dot(a, wb_ref[...])
        cols = {}
        for p in range(2):
            k32 = pa[:, key_cols + p * PAIR:key_cols + (p + 1) * PAIR].astype(F32)
            for e in range(2):
                sq = jnp.sum(jnp.where(_lane_mask("pair", e, tm), k32 * k32, 0.0), axis=1, keepdims=True)
                cols[2 * p + e] = jnp.max(sq, axis=0, keepdims=True)
        lane = lax.broadcasted_iota(jnp.int32, (1, PAIR), 1)
        kmax_ref[...] = jnp.maximum(kmax_ref[...], sum(jnp.where(lane == h, c, 0.0) for h, c in cols.items()))

    whole = lambda w: pl.BlockSpec(w.shape, lambda i: (0, 0), pipeline_mode=pl.Buffered(1))
    return pl.pallas_call(
        kern, name="proj_in", grid=(rows // tm,),
        in_specs=[pl.BlockSpec((tm, d), lambda i: (i, 0)), whole(wa), whole(wb)],
        out_specs=[pl.BlockSpec((tm, wa.shape[1]), lambda i: (i, 0)), pl.BlockSpec((tm, wb.shape[1]), lambda i: (i, 0)),
                   pl.BlockSpec((1, PAIR), lambda i: (0, 0))],
        out_shape=[jax.ShapeDtypeStruct((rows, wa.shape[1]), MXU_DTYPE), jax.ShapeDtypeStruct((rows, wb.shape[1]), F32),
                   jax.ShapeDtypeStruct((1, PAIR), F32)],
        compiler_params=pltpu.CompilerParams(dimension_semantics=("arbitrary",), vmem_limit_bytes=VMEM_LIMIT),
    )(hb, wa.astype(MXU_DTYPE), wb.astype(MXU_DTYPE))


def _piece_arrays(pieces):
    return [a for p in pieces for a in (p if isinstance(p, tuple) else (p,))]


def _join_pieces(refs, pieces):
    refs, cols = list(refs), []
    for p in pieces:
        vals = [refs.pop(0)[...] for _ in (p if isinstance(p, tuple) else (p,))]
        cols.append(functools.reduce(jnp.add, vals).astype(MXU_DTYPE))
    return jnp.concatenate(cols, axis=1)


def _proj_dx(pieces, w):
    arrs = _piece_arrays(pieces)
    rows, n = arrs[0].shape[0], w.shape[0]
    tm = min(PROJ_BWD_ROWS, rows)

    def kern(*refs):
        refs[-1][...] = _dot_nt(_join_pieces(refs[:len(arrs)], pieces), refs[len(arrs)][...])

    return pl.pallas_call(
        kern, name="proj_dx", grid=(rows // tm,),
        in_specs=[pl.BlockSpec((tm, a.shape[1]), lambda i: (i, 0)) for a in arrs]
        + [pl.BlockSpec(w.shape, lambda i: (0, 0), pipeline_mode=pl.Buffered(1))],
        out_specs=pl.BlockSpec((tm, n), lambda i: (i, 0)), out_shape=jax.ShapeDtypeStruct((rows, n), F32),
        compiler_params=pltpu.CompilerParams(dimension_semantics=("parallel",), vmem_limit_bytes=VMEM_LIMIT),
    )(*arrs, w.astype(MXU_DTYPE))


def _proj_dw(a, pieces):
    arrs = _piece_arrays(pieces)
    rows, m = a.shape
    k = sum(x.shape[1] for x in (p[0] if isinstance(p, tuple) else p for p in pieces))
    tk = min(PROJ_BWD_ROWS, rows)

    def kern(*refs):
        o_ref = refs[-1]

        @pl.when(pl.program_id(0) == 0)
        def _():
            o_ref[...] = jnp.zeros_like(o_ref)

        o_ref[...] += _dot_tn(refs[0][...], _join_pieces(refs[1:1 + len(arrs)], pieces))

    return pl.pallas_call(
        kern, name="proj_dw", grid=(rows // tk,),
        in_specs=[pl.BlockSpec((tk, m), lambda i: (i, 0))]
        + [pl.BlockSpec((tk, x.shape[1]), lambda i: (i, 0)) for x in arrs],
        out_specs=pl.BlockSpec((m, k), lambda i: (0, 0), pipeline_mode=pl.Buffered(1)),
        out_shape=jax.ShapeDtypeStruct((m, k), F32),
        compiler_params=pltpu.CompilerParams(dimension_semantics=("arbitrary",), vmem_limit_bytes=VMEM_LIMIT),
    )(a, *arrs)


def _rowwise(body, name, rows, tile, row_ins, full_ins, row_outs, acc_outs=(), scratch=(),
             reverse=False, sequential=False):
    n = rows // tile

    def ridx(i):
        return (n - 1 - i) if reverse else i

    in_specs, args = [], []
    for arr, width, cb in row_ins:
        in_specs.append(pl.BlockSpec((tile, width), lambda i, cb=cb: (ridx(i), cb)))
        args.append(arr)
    for arr in full_ins:
        in_specs.append(pl.BlockSpec(arr.shape, lambda i, nd=arr.ndim: (0,) * nd))
        args.append(arr)
    out_shape = [jax.ShapeDtypeStruct((rows, w), dt) for w, dt in row_outs]
    out_shape += [jax.ShapeDtypeStruct(s, dt) for s, dt in acc_outs]
    out_specs = [pl.BlockSpec((tile, w), lambda i: (ridx(i), 0)) for w, dt in row_outs]
    out_specs += [pl.BlockSpec(s, lambda i, nd=len(s): (0,) * nd) for s, dt in acc_outs]

    def kern(*refs):
        body(pl.program_id(0), *refs)

    sem = "arbitrary" if (acc_outs or sequential) else "parallel"
    return pl.pallas_call(
        kern, name=name, grid=(n,), in_specs=in_specs, out_specs=out_specs, out_shape=out_shape,
        scratch_shapes=list(scratch),
        compiler_params=pltpu.CompilerParams(dimension_semantics=(sem,), vmem_limit_bytes=VMEM_LIMIT),
    )(*args)


def _ln_stats(u):
    mu = jnp.mean(u, axis=-1, keepdims=True)
    xc = u - mu
    var = jnp.mean(xc * xc, axis=-1, keepdims=True)
    return xc, lax.rsqrt(var + LN_EPS)


def _ln_fwd(a, g, beta, name):
    rows, d = a.shape

    def body(i, a_ref, g_ref, be_ref, h_ref, hb_ref):
        xc, rstd = _ln_stats(a_ref[...])
        y = xc * rstd * g_ref[...] + be_ref[...]
        h_ref[...] = y
        hb_ref[...] = y.astype(hb_ref.dtype)

    return _rowwise(body, name, rows, min(ROW_TILE, rows), [(a, d, 0)],
                    [g.reshape(1, d), beta.reshape(1, d)], [(d, F32), (d, MXU_DTYPE)])


def _ln_bwd(u, g, dy1, dy2, c1, name, gate=None):
    rows, d = u.shape
    has_2 = dy2 is not None
    n_groups = len(gate[1]) if gate else 0

    def body(i, *refs):
        refs = list(refs)
        uu = refs.pop(0)[...]
        dy = c1 * refs.pop(0)[...]
        if has_2:
            dy = dy + refs.pop(0)[...]
        if gate:
            mixed = jnp.concatenate([refs.pop(0)[...] for _ in range(n_groups)], axis=1)
            gt = refs.pop(0)[...]
            g_ref, w_ref, du_ref, dub_ref, dm_ref, dgate_ref, dg_ref, db_ref = refs
        else:
            g_ref, du_ref, dg_ref, db_ref = refs

        @pl.when(i == 0)
        def _():
            dg_ref[...] = jnp.zeros_like(dg_ref)
            db_ref[...] = jnp.zeros_like(db_ref)

        xc, rstd = _ln_stats(uu)
        xhat = xc * rstd
        dxh = dy * g_ref[...]
        m1 = jnp.mean(dxh, axis=-1, keepdims=True)
        m2 = jnp.mean(dxh * xhat, axis=-1, keepdims=True)
        du = rstd * (dxh - m1 - xhat * m2)
        du_ref[...] = du
        dg_ref[...] += jnp.sum(dy * xhat, axis=0, keepdims=True)
        db_ref[...] += jnp.sum(dy, axis=0, keepdims=True)
        if gate:
            dub = du.astype(dub_ref.dtype)
            dub_ref[...] = dub
            dgated = _dot_nt(dub, w_ref[...])
            sig = 1.0 / (1.0 + jnp.exp(-gt))
            dm_ref[...] = dgated * (gt * sig)
            dgate_ref[...] = (dgated * mixed * (sig * (1.0 + gt * (1.0 - sig)))).astype(dgate_ref.dtype)

    row_ins = [(u, d, 0), (dy1, d, 0)] + ([(dy2, d, 0)] if has_2 else [])
    full_ins = [g.reshape(1, d)]
    row_outs = [(d, F32)]
    if gate:
        w_out, groups, pb = gate
        w = GROUP_W * n_groups
        row_ins += [(o, GROUP_W, 0) for o in groups] + [(pb, w, 0)]
        full_ins += [w_out.astype(MXU_DTYPE)]
        row_outs += [(d, MXU_DTYPE), (w, F32), (w, MXU_DTYPE)]
    return _rowwise(body, name, rows, min(PROJ_BWD_ROWS if gate else ROW_TILE, rows), row_ins, full_ins, row_outs,
                    [((1, d), F32), ((1, d), F32)])


def _gate_out_proj_ln(groups, pb, w_out, h, g, beta, target=None):
    rows, d = h.shape
    last = target is not None
    w = GROUP_W * len(groups)

    def body(i, *refs):
        refs = list(refs)
        mixed = jnp.concatenate([refs.pop(0)[...] for _ in groups], axis=1)
        gate = refs.pop(0)[...]
        if last:
            h_ref, t_ref, w_ref, g_ref, be_ref, a_ref, u_ref, dh_ref, acc_ref = refs

            @pl.when(i == 0)
            def _():
                acc_ref[...] = jnp.zeros_like(acc_ref)
        else:
            h_ref, w_ref, g_ref, be_ref, a_ref, u_ref, o_ref, ob_ref = refs
        gated = (mixed * (gate / (1.0 + jnp.exp(-gate)))).astype(a_ref.dtype)
        a_ref[...] = gated
        u = ALPHA * h_ref[...] + _dot(gated, w_ref[...])
        u_ref[...] = u
        xc, rstd = _ln_stats(u)
        y = xc * rstd * g_ref[...] + be_ref[...]
        if last:
            e = y - t_ref[...]
            dh_ref[...] = e * (1.0 / d)
            acc_ref[...] += jnp.sum(e * e, axis=0, keepdims=True)
        else:
            o_ref[...] = y
            ob_ref[...] = y.astype(ob_ref.dtype)

    row_ins = [(o, GROUP_W, 0) for o in groups] + [(pb, w, 0), (h, d, 0)] + ([(target, d, 0)] if last else [])
    full_ins = [w_out.astype(MXU_DTYPE), g.reshape(1, d), beta.reshape(1, d)]
    if last:
        return _rowwise(body, "gate_out_proj_ln_loss", rows, PROJ_BWD_ROWS, row_ins, full_ins,
                        [(w, MXU_DTYPE), (d, F32), (d, F32)], [((1, d), F32)])
    return _rowwise(body, "gate_out_proj_ln", rows, PROJ_BWD_ROWS, row_ins, full_ins,
                    [(w, MXU_DTYPE), (d, F32), (d, F32), (d, MXU_DTYPE)])


def _tri(n, kind):
    r = np.arange(n)[:, None]
    c = np.arange(n)[None, :]
    m = {"lower_incl": r >= c, "upper_incl": r <= c, "row_gt_col": r > c, "row_lt_col": r < c}[kind]
    return jnp.asarray(m.astype(np.float32), dtype=MXU_DTYPE)


def _forget_fwd(pb, bias_row):
    rows = pb.shape[0]
    tile = _pick(rows, (1024, 512, 256))

    def body(i, x_ref, b_ref, l_ref, o_ref, carry_ref):
        @pl.when(i == 0)
        def _():
            carry_ref[...] = jnp.zeros_like(carry_ref)

        xx = x_ref[...] + b_ref[...]
        lane = lax.broadcasted_iota(jnp.int32, xx.shape, 1)
        lf = jnp.where(lane < N_HEADS, -_softplus(-xx), 0.0)
        o_ref[...] = _dot_exact_l(l_ref[...], lf) + carry_ref[...]
        carry_ref[...] += jnp.sum(lf, axis=0, keepdims=True)

    return _rowwise(body, "forget_fwd", rows, tile, [(pb, 128, PB_MISC_BLK)],
                    [bias_row, _tri(tile, "lower_incl")], [(128, F32)],
                    scratch=[pltpu.VMEM((1, 128), F32)], sequential=True)[0]


def _forget_bwd(pb, bias_row, dfc):
    rows = pb.shape[0]
    tile = _pick(rows, (1024, 512, 256))

    def body(i, x_ref, df_ref, b_ref, u_ref, o_ref, db_ref, carry_ref):
        @pl.when(i == 0)
        def _():
            carry_ref[...] = jnp.zeros_like(carry_ref)
            db_ref[...] = jnp.zeros_like(db_ref)

        df = df_ref[...]
        sfx = _dot_exact_l(u_ref[...], df) + carry_ref[...]
        carry_ref[...] += jnp.sum(df, axis=0, keepdims=True)
        xx = x_ref[...] + b_ref[...]
        lane = lax.broadcasted_iota(jnp.int32, xx.shape, 1)
        dl = jnp.where(lane < N_HEADS, sfx / (1.0 + jnp.exp(xx)), 0.0)
        o_ref[...] = dl
        db_ref[...] += jnp.sum(dl, axis=0, keepdims=True)

    return _rowwise(body, "forget_bwd", rows, tile,
                    [(pb, 128, PB_MISC_BLK), (dfc, 128, 0)],
                    [bias_row, _tri(tile, "upper_incl")], [(128, F32)], [((1, 128), F32)],
                    scratch=[pltpu.VMEM((1, 128), F32)], reverse=True, sequential=True)


def _rope_tables(s):
    half = MLA_ROPE // 2
    inv_freq = ROPE_THETA ** (-jnp.arange(half, dtype=F32) / half)
    ang = jnp.arange(s).astype(F32)[:, None] * inv_freq[None, :]
    cos2 = jnp.tile(jnp.cos(ang), (1, 2))
    sin2 = jnp.tile(jnp.sin(ang), (1, 2))
    cx = jnp.tile(cos2, (1, N_HEADS))
    sx = jnp.tile(sin2, (1, N_HEADS))
    pad = ((0, 0), (MISC_KROT, 128 - MISC_KROT - MLA_ROPE))
    ck = jnp.pad(cos2, pad)
    sk = jnp.pad(sin2, pad)
    return dict(ck=ck, sk=sk, cx=cx, sx=sx)


def _rot_matrix(width, bases):
    half = MLA_ROPE // 2
    p = np.zeros((width, width), np.float32)
    for b in bases:
        for i in range(half):
            p[b + half + i, b + i] = -1.0
            p[b + i, b + half + i] = 1.0
    return p


def _rope_matrices():
    pk = _rot_matrix(128, [MISC_KROT])
    p4 = _rot_matrix(128, [h * MLA_ROPE for h in range(N_HEADS)])
    a = np.zeros((128, 128), np.float32)
    for h in range(N_HEADS):
        for r in range(MLA_ROPE):
            a[h * MLA_ROPE + r, MISC_KROT + r] = 1.0
    cast = lambda m: jnp.asarray(m, dtype=MXU_DTYPE)
    return dict(p4=cast(p4), p4t=cast(p4.T), pk=cast(pk), spread=cast(a.T), xa=cast(a), xb=cast(p4.T @ a))


def _rms(c, g):
    r = lax.rsqrt(jnp.mean(c * c, axis=-1, keepdims=True) + RMS_EPS)
    return c * r * g


def _mla_q_fwd(pb, g, w_up, tabs, mats):
    rows = pb.shape[0]

    def body(i, c_ref, cos_ref, sin_ref, g_ref, w_ref, p_ref, q_ref, cn_ref):
        cn = _rms(c_ref[...], g_ref[...]).astype(cn_ref.dtype)
        cn_ref[...] = cn
        q = _dot(cn, w_ref[...])
        qr = q[:, GROUP_W:]
        q_ref[:, :GROUP_W] = q[:, :GROUP_W].astype(q_ref.dtype)
        q_ref[:, GROUP_W:] = (qr * cos_ref[...] + _dot_exact_r(qr, p_ref[...]) * sin_ref[...]).astype(q_ref.dtype)

    return _rowwise(body, "mla_q_fwd", rows, ROW_TILE,
                    [(pb, MLA_Q_RANK, PB_CQ_BLK), (tabs["cx"], PAIR, 0), (tabs["sx"], PAIR, 0)],
                    [g.reshape(1, MLA_Q_RANK), w_up.astype(MXU_DTYPE), mats["p4"]],
                    [(MLA_Q_COLS, MXU_DTYPE), (MLA_Q_RANK, MXU_DTYPE)])


def _mla_kv_fwd(pb, g, w_up, tabs, mats):
    rows = pb.shape[0]

    def body(i, c_ref, x_ref, cos_ref, sin_ref, g_ref, w_ref, p_ref, sp_ref, kv_ref, cn_ref, kr_ref):
        cn = _rms(c_ref[...], g_ref[...]).astype(cn_ref.dtype)
        cn_ref[...] = cn
        kv_ref[...] = _dot(cn, w_ref[...]).astype(kv_ref.dtype)
        xx = x_ref[...]
        kr = xx * cos_ref[...] + _dot_exact_r(xx, p_ref[...]) * sin_ref[...]
        kr_ref[...] = _dot_exact_r(kr, sp_ref[...]).astype(kr_ref.dtype)

    return _rowwise(body, "mla_kv_fwd", rows, ROW_TILE,
                    [(pb, MLA_KV_RANK, PB_CKV_BLK), (pb, 128, PB_MISC_BLK), (tabs["ck"], 128, 0), (tabs["sk"], 128, 0)],
                    [g.reshape(1, MLA_KV_RANK), w_up.astype(MXU_DTYPE), mats["pk"], mats["spread"]],
                    [(MLA_KV_COLS, MXU_DTYPE), (MLA_KV_RANK, MXU_DTYPE), (128, MXU_DTYPE)])


def _rms_bwd(c, g, dy):
    r = lax.rsqrt(jnp.mean(c * c, axis=-1, keepdims=True) + RMS_EPS)
    dyg = dy * g
    dc = r * dyg - c * (r * r * r) * jnp.mean(c * dyg, axis=-1, keepdims=True)
    return dc, jnp.sum(dy * c * r, axis=0, keepdims=True)


def _mla_q_bwd(dq_full, pb, g, w_up, tabs, mats):
    rows, nq = dq_full.shape

    def body(i, d_ref, c_ref, cos_ref, sin_ref, g_ref, w_ref, pt_ref, dq_ref, dc_ref, dg_ref):
        @pl.when(i == 0)
        def _():
            dg_ref[...] = jnp.zeros_like(dg_ref)

        dr = d_ref[:, GROUP_W:]
        dq = jnp.concatenate([d_ref[:, :GROUP_W], dr * cos_ref[...] + _dot_exact_r(dr * sin_ref[...], pt_ref[...])],
                             axis=1).astype(dq_ref.dtype)
        dq_ref[...] = dq
        dc, dg = _rms_bwd(c_ref[...], g_ref[...], _dot_nt(dq, w_ref[...]))
        dc_ref[...] = dc.astype(dc_ref.dtype)
        dg_ref[...] += dg

    return _rowwise(body, "mla_q_bwd", rows, ROW_TILE,
                    [(dq_full, nq, 0), (pb, MLA_Q_RANK, PB_CQ_BLK), (tabs["cx"], PAIR, 0), (tabs["sx"], PAIR, 0)],
                    [g.reshape(1, MLA_Q_RANK), w_up.astype(MXU_DTYPE), mats["p4t"]],
                    [(nq, MXU_DTYPE), (MLA_Q_RANK, MXU_DTYPE)], [((1, MLA_Q_RANK), F32)])


def _mla_kv_bwd(dk, dv, pb, g, w_up, tabs, mats):
    rows = dk.shape[0]

    def body(i, dk_ref, dr_ref, dv_ref, c_ref, cos_ref, sin_ref, g_ref, w_ref, a_ref, b_ref,
             dkv_ref, dc_ref, dm_ref, dg_ref):
        @pl.when(i == 0)
        def _():
            dg_ref[...] = jnp.zeros_like(dg_ref)

        dkv = jnp.concatenate([dk_ref[...], dv_ref[...]], axis=1).astype(dkv_ref.dtype)
        dkv_ref[...] = dkv
        dc, dg = _rms_bwd(c_ref[...], g_ref[...], _dot_nt(dkv, w_ref[...]))
        dc_ref[...] = dc.astype(dc_ref.dtype)
        dg_ref[...] += dg
        dr = dr_ref[...]
        dm_ref[...] = _dot_exact_r(dr * cos_ref[...], a_ref[...]) + _dot_exact_r(dr * sin_ref[...], b_ref[...])

    return _rowwise(body, "mla_kv_bwd", rows, ROW_TILE,
                    [(dk, GROUP_W, 0), (dk, PAIR, 2), (dv, GROUP_W, 0), (pb, MLA_KV_RANK, PB_CKV_BLK),
                     (tabs["cx"], PAIR, 0), (tabs["sx"], PAIR, 0)],
                    [g.reshape(1, MLA_KV_RANK), w_up.astype(MXU_DTYPE), mats["xa"], mats["xb"]],
                    [(MLA_KV_COLS, MXU_DTYPE), (MLA_KV_RANK, MXU_DTYPE), (PAIR, F32)], [((1, MLA_KV_RANK), F32)])


def _att_params(parallel):
    return pltpu.CompilerParams(dimension_semantics=("parallel" if parallel else "arbitrary",),
                                vmem_limit_bytes=VMEM_LIMIT)


def _blk_off(j, t):
    return j * t if isinstance(j, int) else pl.multiple_of(j * t, t)


def _causal_mask(t, strict):
    r = lax.broadcasted_iota(jnp.int32, (t, t), 0)
    c = lax.broadcasted_iota(jnp.int32, (t, t), 1)
    return (c < r) if strict else (c <= r)


def _lane_mask(kind, head, rows):
    lane = lax.broadcasted_iota(jnp.int32, (rows, PAIR), 1)
    if kind == "pair":
        return (lane < HEAD_DIM) if head % 2 == 0 else (lane >= HEAD_DIM)
    return (lane >= MLA_ROPE * head) & (lane < MLA_ROPE * (head + 1))


def _row_spec(t, cb, width=PAIR):
    return pl.BlockSpec((t, width), lambda i, cb=cb: (i, cb))


def _whole_spec(rows, cb, width=PAIR):
    return pl.BlockSpec((rows, width), lambda i, cb=cb: (0, cb), pipeline_mode=pl.Buffered(1))


def _is_pow2(x):
    return float(np.frexp(x)[0]) == 0.5


def _masked_heads(blocks, kinds, pair, dtype, scale=None):
    out = []
    for e in range(2):
        head = 2 * pair + e
        parts = [jnp.where(_lane_mask(k, head, b.shape[0]), b.astype(F32) * (1.0 if scale is None else scale),
                           0.0).astype(dtype)
                 for b, k in zip(blocks, kinds)]
        out.append(parts[0] if len(parts) == 1 else jnp.concatenate(parts, axis=1))
    return out


def _logit_reach(qh, kmax2, head):
    q32 = qh.astype(F32)
    return jnp.sqrt(jnp.sum(q32 * q32, axis=1, keepdims=True) * _col(kmax2, head)) * REACH_SLACK


def _forget_top(ft_ref, head, off):
    return jnp.max(-ft_ref[head:head + 1, pl.ds(off, PAIR)])


def _col(block, idx):
    lane = lax.broadcasted_iota(jnp.int32, block.shape, 1)
    return jnp.sum(jnp.where(lane == idx, block, 0.0), axis=1, keepdims=True)


def _scatter_cols(cols, t):
    lane = lax.broadcasted_iota(jnp.int32, (t, PAIR), 1)
    out = jnp.zeros((t, PAIR), F32)
    for idx, c in cols.items():
        out = out + jnp.where(lane == idx, c, 0.0)
    return out


def _take_heads(per_head, pair):
    return jnp.where(_lane_mask("pair", 0, per_head[0].shape[0]), per_head[0], per_head[1])


class _Parts:
    def __init__(self, q_parts, k_parts, v_parts, tq, sk):
        self.kinds = [[kind for _, _, kind in q_parts[p]] for p in range(2)]
        self.nparts = len(q_parts[0])
        self.q_specs = [_row_spec(tq, cb) for p in range(2) for _, cb, _ in q_parts[p]]
        self.q_args = [a for p in range(2) for a, _, _ in q_parts[p]]
        self.k_specs = [_whole_spec(sk, cb) for p in range(2) for _, cb, _ in k_parts[p]]
        self.k_args = [a for p in range(2) for a, _, _ in k_parts[p]]
        self.v_specs = [_whole_spec(sk, cb) for _, cb in v_parts]
        self.v_args = [a for a, _ in v_parts]
        self.width = PAIR * self.nparts

    def split(self, refs):
        n = self.nparts
        refs = list(refs)
        q = [refs[p * n:(p + 1) * n] for p in range(2)]
        k = [refs[2 * n + p * n:2 * n + (p + 1) * n] for p in range(2)]
        v = refs[4 * n:4 * n + 2]
        return q, k, v, refs[4 * n + 2:]

    def k_block(self, k_refs, off, t):
        blks = [r[pl.ds(off, t), :] for r in k_refs]
        return blks[0] if len(blks) == 1 else jnp.concatenate(blks, axis=1)


def _softmax_fwd(q_parts, k_parts, v_parts, sq, sk, scale, causal, bias, name):
    tq = min(SOFTMAX_FWD_TILE, sq) if causal else _pick(sq, (DENSE_QUERY_TILE, SOFTMAX_TILE))
    tk = tq if causal else min(SOFTMAX_TILE, sk)
    nkv = sk // tk
    pp = _Parts(q_parts, k_parts, v_parts, tq, sk)

    def kern(*refs):
        q_refs, k_refs, v_refs, rest = pp.split(refs)
        if bias is not None:
            fc_ref, ft_ref, kmax_ref, o_ref, lse_ref = rest
            fcb = fc_ref[...]
        else:
            o_ref, lse_ref = rest
        i = pl.program_id(0)
        fold = _is_pow2(scale)
        head_on = [jnp.where(_lane_mask("pair", e, tk), 1.0, 0.0).astype(MXU_DTYPE) for e in range(2)]
        head_off = [jnp.where(_lane_mask("pair", e, tk), 0.0, 1.0).astype(MXU_DTYPE) for e in range(2)]
        lse_cols = {}
        for first in range(0, 2, FWD_PAIRS_PER_LOOP):
            pairs = list(range(first, first + FWD_PAIRS_PER_LOOP))
            heads = [2 * p + e for p in pairs for e in range(2)]
            qm = {}
            for p in pairs:
                masked_q = _masked_heads([r[...] for r in q_refs[p]], pp.kinds[p], p, MXU_DTYPE, scale if fold else None)
                qm.update({2 * p + e: masked_q[e] for e in range(2)})
            if bias is not None:
                reach = {h: _logit_reach(qm[h], kmax_ref[...], h) for h in heads}

            def block(j, carry, masked, pairs=pairs, qm=qm):
                off = _blk_off(j, tk)
                out = []
                for p in pairs:
                    kb = pp.k_block(k_refs[p], off, tk)
                    vb = v_refs[p][pl.ds(off, tk), :]
                    for e in range(2):
                        h = 2 * p + e
                        m, acc = carry[len(out)]
                        s = _dot_nt(qm[h], kb) * (LOG2E if fold else scale * LOG2E)
                        if bias is not None:
                            s = s - ft_ref[h:h + 1, pl.ds(off, tk)] * LOG2E
                        if masked:
                            s = jnp.where(_causal_mask(tq, False), s, NEG)
                        m_new = jnp.maximum(m, jnp.max(s, axis=1, keepdims=True))
                        pr = jnp.exp2(s - m_new).astype(MXU_DTYPE)
                        out.append((m_new, jnp.exp2(m - m_new) * acc + _dot(pr, vb * head_on[e] + head_off[e])))
                return tuple(out)

            carry = tuple((jnp.full((tq, 1), NEG, F32), jnp.zeros((tq, PAIR), F32)) for _ in heads)
            if causal and bias is not None:
                def alive(c, j, heads=heads, reach=reach):
                    return functools.reduce(jnp.maximum, [
                        jnp.max(reach[h] + _forget_top(ft_ref, h, _blk_off(j, tk)) - c[n][0] * LN2)
                        for n, h in enumerate(heads)])

                def step(state, block=block, alive=alive):
                    n, _, c = state
                    c = block(i - 1 - n, c, False)
                    return n + 1, alive(c, i - 1 - n), c

                carry = block(i, carry, True)
                _, _, carry = lax.while_loop(lambda st: jnp.logical_and(st[0] < i, st[1] > DEAD_LOGIT), step,
                                             (jnp.int32(0), alive(carry, i), carry))
            elif causal:
                carry = lax.fori_loop(0, i, lambda j, c, block=block: block(j, c, False), carry)
                carry = block(i, carry, True)
            else:
                for j in range(nkv):
                    carry = block(j, carry, False)
            for n, p in enumerate(pairs):
                outs = []
                for e in range(2):
                    m, acc = carry[2 * n + e]
                    l = _col(acc, HEAD_DIM * (1 - e))
                    outs.append(acc / l)
                    lse_cols[2 * p + e] = m * LN2 + jnp.log(l) + (_col(fcb, 2 * p + e) if bias is not None else 0.0)
                o_ref[:, p * PAIR:(p + 1) * PAIR] = _take_heads(outs, p)
        lse_ref[...] = _scatter_cols(lse_cols, tq)

    in_specs = pp.q_specs + pp.k_specs + pp.v_specs
    args = pp.q_args + pp.k_args + pp.v_args
    if bias is not None:
        in_specs += [_row_spec(tq, 0), pl.BlockSpec((8, sk), lambda i: (0, 0), pipeline_mode=pl.Buffered(1)),
                     pl.BlockSpec((1, PAIR), lambda i: (0, 0))]
        args += list(bias)
    return pl.pallas_call(
        kern, name=name, grid=(sq // tq,), in_specs=in_specs,
        out_specs=[_row_spec(tq, 0, GROUP_W), _row_spec(tq, 0)],
        out_shape=[jax.ShapeDtypeStruct((sq, GROUP_W), F32), jax.ShapeDtypeStruct((sq, PAIR), F32)],
        compiler_params=_att_params(True),
    )(*args)


def _softmax_bwd(q_parts, k_parts, v_parts, o, lse, do, do_blk, sq, sk, scale, causal, bias, name):
    tq = min(SOFTMAX_TILE, sq) if causal else _pick(sq, (DENSE_QUERY_TILE, SOFTMAX_TILE))
    tk = tq if causal else min(SOFTMAX_TILE, sk)
    nkv = sk // tk
    pp = _Parts(q_parts, k_parts, v_parts, tq, sk)
    quad = pp.nparts == 2
    wq = GROUP_W + (PAIR if quad else 0)

    def kern(*refs):
        q_refs, k_refs, v_refs, rest = pp.split(refs)
        if bias is not None:
            o_ref, lse_ref, do_ref, fc_ref, ft_ref, kmax_ref, dq_ref, dk_ref, dv_ref, dfq_ref, dfk_ref = rest
            fcb = fc_ref[...]
        else:
            o_ref, lse_ref, do_ref, dq_ref, dk_ref, dv_ref = rest
        i = pl.program_id(0)
        fold = _is_pow2(scale)

        @pl.when(i == 0)
        def _():
            dk_ref[...] = jnp.zeros_like(dk_ref)
            dv_ref[...] = jnp.zeros_like(dv_ref)
            if bias is not None:
                dfk_ref[...] = jnp.zeros_like(dfk_ref)

        lse_b = lse_ref[...]
        qm, dom, delta, lse_h = [], [], [], []
        for p in range(2):
            qm += _masked_heads([r[...] for r in q_refs[p]], pp.kinds[p], p, MXU_DTYPE, scale if fold else None)
            do_p = do_ref[:, p * PAIR:(p + 1) * PAIR]
            dom += _masked_heads([do_p], ["pair"], p, MXU_DTYPE)
            prod = do_p * o_ref[:, p * PAIR:(p + 1) * PAIR]
            for e in range(2):
                h = 2 * p + e
                delta.append(jnp.sum(jnp.where(_lane_mask("pair", h, tq), prod, 0.0), axis=1, keepdims=True))
                lse_h.append(_col(lse_b, h) - (_col(fcb, h) if bias is not None else 0.0))

        def block(j, carry, masked, pairs):
            off = _blk_off(j, tk)
            out = []
            for p in pairs:
                kb = pp.k_block(k_refs[p], off, tk)
                vb = v_refs[p][pl.ds(off, tk), :]
                dk_acc = jnp.zeros((tk, pp.width), F32)
                dv_acc = jnp.zeros((tk, PAIR), F32)
                for e in range(2):
                    h = 2 * p + e
                    dq, dfq = carry[len(out)]
                    s = _dot_nt(qm[h], kb)
                    if not fold:
                        s = s * scale
                    if bias is not None:
                        s = s - ft_ref[h:h + 1, pl.ds(off, tk)]
                    if masked:
                        s = jnp.where(_causal_mask(tq, False), s, NEG)
                    pr = jnp.exp(s - lse_h[h])
                    ds = pr * (_dot_nt(dom[h], vb) - delta[h])
                    dsb = (ds if fold else ds * scale).astype(MXU_DTYPE)
                    dv_acc = dv_acc + _dot_tn(pr.astype(MXU_DTYPE), dom[h])
                    dk_acc = dk_acc + _dot_tn(dsb, qm[h])
                    dq = dq + _dot(dsb, kb)
                    if bias is not None:
                        dfq = dfq + jnp.sum(ds, axis=1, keepdims=True)
                        dfk_ref[h:h + 1, pl.ds(off, tk)] -= jnp.sum(ds, axis=0, keepdims=True)
                    out.append((dq, dfq))
                dv_ref[pl.ds(off, tk), p * PAIR:(p + 1) * PAIR] += dv_acc
                dk_ref[pl.ds(off, tk), p * PAIR:(p + 1) * PAIR] += dk_acc[:, :PAIR]
                if quad:
                    dk_ref[pl.ds(off, tk), GROUP_W:] += dk_acc[:, PAIR:]
            return tuple(out)

        done = {}
        for first in range(0, 2, BWD_PAIRS_PER_LOOP):
            pairs = list(range(first, first + BWD_PAIRS_PER_LOOP))
            heads = [2 * p + e for p in pairs for e in range(2)]
            carry = tuple((jnp.zeros((tq, pp.width), F32), jnp.zeros((tq, 1), F32)) for _ in heads)
            if causal and bias is not None:
                reach = {h: _logit_reach(qm[h], kmax_ref[...], h) - lse_h[h] for h in heads}

                def alive(j, heads=heads, reach=reach):
                    return functools.reduce(jnp.maximum, [
                        jnp.max(reach[h] + _forget_top(ft_ref, h, _blk_off(j, tk))) for h in heads])

                def step(state, pairs=pairs, alive=alive):
                    n, _, c = state
                    return n + 1, alive(i - 1 - n), block(i - 1 - n, c, False, pairs)

                carry = block(i, carry, True, pairs)
                _, _, carry = lax.while_loop(lambda st: jnp.logical_and(st[0] < i, st[1] > DEAD_LOGIT), step,
                                             (jnp.int32(0), alive(i), carry))
            elif causal:
                carry = lax.fori_loop(0, i, lambda j, c, pairs=pairs: block(j, c, False, pairs), carry)
                carry = block(i, carry, True, pairs)
            else:
                for j in range(nkv):
                    carry = block(j, carry, False, pairs)
            done.update(zip(heads, carry))
        carry = [done[h] for h in range(N_HEADS)]
        dqs = [c[0] * scale if fold else c[0] for c in carry]
        for p in range(2):
            dq_ref[:, p * PAIR:(p + 1) * PAIR] = _take_heads([dqs[2 * p + e][:, :PAIR] for e in range(2)], p)
        if quad:
            dq_ref[:, GROUP_W:] = sum(jnp.where(_lane_mask("quad", h, tq), dqs[h][:, PAIR:], 0.0)
                                      for h in range(N_HEADS))
        if bias is not None:
            dfq_ref[...] = _scatter_cols({h: carry[h][1] for h in range(N_HEADS)}, tq)

    acc_spec = lambda rows, width: pl.BlockSpec((rows, width), lambda i: (0, 0), pipeline_mode=pl.Buffered(1))
    in_specs = pp.q_specs + pp.k_specs + pp.v_specs + [_row_spec(tq, 0, GROUP_W), _row_spec(tq, 0),
                                                       _row_spec(tq, do_blk, GROUP_W)]
    args = pp.q_args + pp.k_args + pp.v_args + [o, lse, do]
    out_specs = [_row_spec(tq, 0, wq), acc_spec(sk, wq), acc_spec(sk, GROUP_W)]
    out_shape = [jax.ShapeDtypeStruct((sq, wq), F32), jax.ShapeDtypeStruct((sk, wq), F32),
                 jax.ShapeDtypeStruct((sk, GROUP_W), F32)]
    if bias is not None:
        in_specs += [_row_spec(tq, 0), pl.BlockSpec((8, sk), lambda i: (0, 0), pipeline_mode=pl.Buffered(1)),
                     pl.BlockSpec((1, PAIR), lambda i: (0, 0))]
        args += list(bias)
        out_specs += [_row_spec(tq, 0), acc_spec(8, sk)]
        out_shape += [jax.ShapeDtypeStruct((sq, PAIR), F32), jax.ShapeDtypeStruct((8, sk), F32)]
    return pl.pallas_call(
        kern, name=name, grid=(sq // tq,), in_specs=in_specs, out_specs=out_specs, out_shape=out_shape,
        compiler_params=_att_params(False),
    )(*args)


def _sb_logs(qh, kb, valid):
    z = _dot_nt(qh, kb)
    sp = _softplus(z)
    lk = -sp
    if valid is not None:
        lk = jnp.where(valid, lk, 0.0)
    return lk, z - sp


def _sb_valid(d, tq, tk):
    r = lax.broadcasted_iota(jnp.int32, (tq, tk), 0)
    c = lax.broadcasted_iota(jnp.int32, (tq, tk), 1)
    return c + d * tk < r


def _tri_sums(xs, tri):
    t = xs[0].shape[0]
    pieces = [_split2(x) for x in xs]
    hi = _dot(jnp.concatenate([pc[0] for pc in pieces], axis=0), tri)
    lo = _dot(jnp.concatenate([pc[1] for pc in pieces], axis=0), tri)
    return [hi[n * t:(n + 1) * t] + lo[n * t:(n + 1) * t] for n in range(len(xs))]


def _sb_fwd(src, q_blk, k_blk, v_blk, s, scale, name):
    assert _is_pow2(scale)
    tq, t = min(SB_QUERY_TILE, s), min(ATT_TILE, s)
    slots = -(-(s // t) // SB_SLOT) * SB_SLOT
    width = N_HEADS * slots
    band = tq // t
    pair = lambda blk: [[(src, blk + p, "pair")] for p in range(2)]
    pp = _Parts(pair(q_blk), pair(k_blk), [(src, v_blk + p) for p in range(2)], tq, s)

    def kern(*refs):
        q_refs, k_refs, v_refs, (tri_ref, o_ref, rm_ref, cnt_ref) = pp.split(refs)
        i = pl.program_id(0)
        tri = tri_ref[...]
        lane = lax.broadcasted_iota(jnp.int32, (tq, width), 1)
        qm = []
        for p in range(2):
            qm += _masked_heads([q_refs[p][0][...]], ["pair"], p, MXU_DTYPE, scale)

        def block(j, carry, valid):
            accs, rights, rm = carry
            off = _blk_off(j, t)
            kbs = [k_refs[p][0][pl.ds(off, t), :] for p in range(2)]
            vbs = [v_refs[p][pl.ds(off, t), :] for p in range(2)]
            logs = [_sb_logs(qm[h], kbs[h // 2], valid) for h in range(N_HEADS)]
            tails = _tri_sums([lg[0] for lg in logs], tri)
            new_acc, new_right = [], []
            for h in range(N_HEADS):
                lk, ls = logs[h]
                w = jnp.exp(ls + tails[h] + rights[h])
                if valid is not None:
                    w = jnp.where(valid, w, 0.0)
                new_acc.append(accs[h] + _dot(w.astype(MXU_DTYPE), vbs[h // 2]))
                rm = rm + jnp.where(lane == slots * h + j, rights[h], 0.0)
                new_right.append(rights[h] + jnp.sum(lk, axis=1, keepdims=True))
            return tuple(new_acc), tuple(new_right), rm

        carry = (tuple(jnp.zeros((tq, PAIR), F32) for _ in range(N_HEADS)),
                 tuple(jnp.zeros((tq, 1), F32) for _ in range(N_HEADS)), jnp.zeros((tq, width), F32))
        for d in reversed(range(band)):
            carry = block(band * i + d, carry, _sb_valid(d, tq, t))

        def alive(c):
            return functools.reduce(jnp.maximum, [jnp.max(r) for r in c[1]])

        def step(state):
            n, _, c = state
            c = block(band * i - 1 - n, c, None)
            return n + 1, alive(c), c

        n_done, _, carry = lax.while_loop(lambda st: jnp.logical_and(st[0] < band * i, st[1] > EXP_UNDERFLOW),
                                          step, (jnp.int32(0), alive(carry), carry))
        cnt_ref[i] = n_done
        for p in range(2):
            o_ref[:, p * PAIR:(p + 1) * PAIR] = _take_heads([carry[0][2 * p + e] for e in range(2)], p)
        rm_ref[...] = carry[2]

    return pl.pallas_call(
        kern, name=name, grid=(s // tq,),
        in_specs=pp.q_specs + pp.k_specs + pp.v_specs + [pl.BlockSpec((t, t), lambda i: (0, 0))],
        out_specs=[_row_spec(tq, 0, GROUP_W), _row_spec(tq, 0, width), pl.BlockSpec(memory_space=pltpu.SMEM)],
        out_shape=[jax.ShapeDtypeStruct((s, GROUP_W), F32), jax.ShapeDtypeStruct((s, width), F32),
                   jax.ShapeDtypeStruct((s // tq,), jnp.int32)],
        compiler_params=_att_params(False),
    )(*(pp.q_args + pp.k_args + pp.v_args + [_tri(t, "row_gt_col")]))


def _sb_bwd(src, q_blk, k_blk, v_blk, do, do_blk, rm, visited, s, scale, name):
    assert _is_pow2(scale)
    tq, t = min(SB_QUERY_TILE, s), min(ATT_TILE, s)
    band = tq // t
    pair = lambda blk: [[(src, blk + p, "pair")] for p in range(2)]
    pp = _Parts(pair(q_blk), pair(k_blk), [(src, v_blk + p) for p in range(2)], tq, s)

    def kern(*refs):
        q_refs, k_refs, v_refs, (do_ref, rm_ref, tri_ref, pre_ref, cnt_ref, dq_ref, dk_ref, dv_ref) = pp.split(refs)
        i = pl.program_id(0)

        @pl.when(i == 0)
        def _():
            dk_ref[...] = jnp.zeros_like(dk_ref)
            dv_ref[...] = jnp.zeros_like(dv_ref)

        rmb = rm_ref[...]
        tri = tri_ref[...]
        pre = pre_ref[...]
        qm, dom = [], []
        for p in range(2):
            qm += _masked_heads([q_refs[p][0][...]], ["pair"], p, MXU_DTYPE, scale)
            dom += _masked_heads([do_ref[:, p * PAIR:(p + 1) * PAIR]], ["pair"], p, MXU_DTYPE)

        def block(j, carry, valid):
            dqs, lefts = carry
            off = _blk_off(j, t)
            kbs = [k_refs[p][0][pl.ds(off, t), :] for p in range(2)]
            vbs = [v_refs[p][pl.ds(off, t), :] for p in range(2)]
            logs = [_sb_logs(qm[h], kbs[h // 2], valid) for h in range(N_HEADS)]
            tails = _tri_sums([lg[0] for lg in logs], tri)
            ws, gs = [], []
            for h in range(N_HEADS):
                lk, ls = logs[h]
                w = jnp.exp(ls + tails[h] + _col(rmb, (rm.shape[1] // N_HEADS) * h + j))
                if valid is not None:
                    w = jnp.where(valid, w, 0.0)
                ws.append(w)
                gs.append(_dot_nt(dom[h], vbs[h // 2]) * w)
            prefix = _tri_sums(gs, pre)
            new_dq, new_left = [], []
            dk_acc = [jnp.zeros((t, PAIR), F32) for _ in range(2)]
            dv_acc = [jnp.zeros((t, PAIR), F32) for _ in range(2)]
            for h in range(N_HEADS):
                lk, ls = logs[h]
                sig = jnp.exp(ls)
                dz = gs[h] * (1.0 - sig) - sig * (prefix[h] + lefts[h])
                if valid is not None:
                    dz = jnp.where(valid, dz, 0.0)
                dzb = dz.astype(MXU_DTYPE)
                dv_acc[h // 2] = dv_acc[h // 2] + _dot_tn(ws[h].astype(MXU_DTYPE), dom[h])
                dk_acc[h // 2] = dk_acc[h // 2] + _dot_tn(dzb, qm[h])
                new_dq.append(dqs[h] + _dot(dzb, kbs[h // 2]))
                new_left.append(lefts[h] + jnp.sum(gs[h], axis=1, keepdims=True))
            for p in range(2):
                dv_ref[pl.ds(off, t), p * PAIR:(p + 1) * PAIR] += dv_acc[p]
                dk_ref[pl.ds(off, t), p * PAIR:(p + 1) * PAIR] += dk_acc[p]
            return tuple(new_dq), tuple(new_left)

        carry = (tuple(jnp.zeros((tq, PAIR), F32) for _ in range(N_HEADS)),
                 tuple(jnp.zeros((tq, 1), F32) for _ in range(N_HEADS)))
        carry = lax.fori_loop(band * i - cnt_ref[i], band * i, lambda j, c: block(j, c, None), carry)
        for d in range(band):
            carry = block(band * i + d, carry, _sb_valid(d, tq, t))
        for p in range(2):
            dq_ref[:, p * PAIR:(p + 1) * PAIR] = _take_heads([carry[0][2 * p + e] * scale for e in range(2)], p)

    mspec = pl.BlockSpec((t, t), lambda i: (0, 0))
    acc_spec = pl.BlockSpec((s, GROUP_W), lambda i: (0, 0), pipeline_mode=pl.Buffered(1))
    return pl.pallas_call(
        kern, name=name, grid=(s // tq,),
        in_specs=pp.q_specs + pp.k_specs + pp.v_specs + [_row_spec(tq, do_blk, GROUP_W), _row_spec(tq, 0, rm.shape[1]), mspec, mspec,
                                                         pl.BlockSpec(memory_space=pltpu.SMEM)],
        out_specs=[_row_spec(tq, 0, GROUP_W), acc_spec, acc_spec],
        out_shape=[jax.ShapeDtypeStruct((s, GROUP_W), F32)] * 3,
        compiler_params=_att_params(False),
    )(*(pp.q_args + pp.k_args + pp.v_args + [do, rm, _tri(t, "row_gt_col"), _tri(t, "row_lt_col"), visited]))


def _split_w_in(w):
    col = lambda n: w[:, _OFF[n]:_OFF[n + 1]]
    wa = jnp.concatenate([col(0), col(1), col(2), col(4), col(5), col(6), col(10)], axis=1)
    misc = jnp.concatenate([col(3), col(9), jnp.zeros((w.shape[0], 128 - 4 - MLA_ROPE), w.dtype)], axis=1)
    wb = jnp.concatenate([col(11), col(7), col(8), misc], axis=1)
    return wa, wb


def _merge_dw_in(dwp):
    a = lambda n: dwp[:, n * GROUP_W:(n + 1) * GROUP_W]
    b0 = PA_COLS
    gate = dwp[:, b0:b0 + 1024]
    cq = dwp[:, b0 + 1024:b0 + 1280]
    ckv = dwp[:, b0 + 1280:b0 + 1408]
    flog = dwp[:, b0 + 1408:b0 + 1412]
    krot = dwp[:, b0 + 1408 + MISC_KROT:b0 + 1408 + MISC_KROT + MLA_ROPE]
    return jnp.concatenate([a(0), a(1), a(2), flog, a(3), a(4), a(5), cq, ckv, krot, a(6), gate], axis=1)


def _heads_first(w, per_head, first):
    r = w.shape[0]
    w3 = w.reshape(r, N_HEADS, per_head)
    return jnp.concatenate([w3[:, :, :first].reshape(r, -1), w3[:, :, first:].reshape(r, -1)], axis=1)


def _heads_interleaved(w, per_head, first):
    r = w.shape[0]
    a = w[:, :N_HEADS * first].reshape(r, N_HEADS, first)
    b = w[:, N_HEADS * first:].reshape(r, N_HEADS, per_head - first)
    return jnp.concatenate([a, b], axis=2).reshape(r, N_HEADS * per_head)


def _pad_rows8(a):
    return a[:, :8].T


def _local_step(x2, mem2, tgt, p):
    s = x2.shape[0]
    nm = mem2.shape[0]
    head_scale = HEAD_DIM ** -0.5
    mla_scale = (HEAD_DIM + MLA_ROPE) ** -0.5
    tabs = _rope_tables(s)
    mats = _rope_matrices()
    pairs = lambda arr, blk: [[(arr, blk + q, "pair")] for q in range(2)]
    vals = lambda arr, blk: [(arr, blk + q) for q in range(2)]

    h, hb = _ln_fwd(x2, p["ln_in_g"], p["ln_in_b"], "ln_in_fwd")
    _, memn_b = _ln_fwd(mem2, p["mem_ln_g"], p["mem_ln_b"], "ln_mem_fwd")

    saved = []
    for l in range(DEPTH):
        wa, wb = _split_w_in(p["w_in"][l])
        wp = jnp.concatenate([wa, wb], axis=1)
        wq_up = _heads_first(p["w_mla_q_up"][l], HEAD_DIM + MLA_ROPE, HEAD_DIM)
        wkv_up = _heads_first(p["w_mla_kv_up"][l], 2 * HEAD_DIM, HEAD_DIM)
        bias_row = jnp.pad(p["b_forget"][l], (0, 128 - N_HEADS)).reshape(1, 128)
        pa, pb, fox_kmax = _proj_in(hb, wa, wb, GROUP_W)

        fc = _forget_fwd(pb, bias_row)
        fbias = (fc, _pad_rows8(fc), fox_kmax)
        o_fox, lse_fox = _softmax_fwd(pairs(pa, 0), pairs(pa, 2), vals(pa, 4), s, s, head_scale, True, fbias,
                                      "fox_fwd")
        o_sb, *rm_sb = _sb_fwd(pa, 6, 8, 10, s, head_scale, "sb_fwd")

        qfull, cqn = _mla_q_fwd(pb, p["mla_q_norm_g"][l], wq_up, tabs, mats)
        kv, ckvn, kr4 = _mla_kv_fwd(pb, p["mla_kv_norm_g"][l], wkv_up, tabs, mats)
        mla_q = [[(qfull, q, "pair"), (qfull, 2, "quad")] for q in range(2)]
        mla_k = [[(kv, q, "pair"), (kr4, 0, "quad")] for q in range(2)]
        o_mla, lse_mla = _softmax_fwd(mla_q, mla_k, vals(kv, 2), s, s, mla_scale, True, None, "mla_fwd")

        mkv = _matmul(memn_b, p["w_mem_kv"][l], MXU_DTYPE, "mem_kv")
        o_mem, lse_mem = _softmax_fwd(pairs(pa, 12), pairs(mkv, 0), vals(mkv, 2), s, nm, head_scale, False, None,
                                      "mem_fwd")

        groups = (o_fox, o_sb, o_mla, o_mem)
        if l < DEPTH - 1:
            gated, u, h_next, hb_next = _gate_out_proj_ln(groups, pb, p["w_out"][l], h, p["ln_g"][l], p["ln_b"][l])
        else:
            gated, u, dh, sq_cols = _gate_out_proj_ln(groups, pb, p["w_out"][l], h, p["ln_g"][l], p["ln_b"][l], tgt)
        saved.append(dict(u=u, hb=hb, wp=wp, wq_up=wq_up, wkv_up=wkv_up, bias_row=bias_row, pa=pa, pb=pb,
                          fbias=fbias, lse_fox=lse_fox, rm_sb=rm_sb, cqn=cqn, ckvn=ckvn, mla_q=mla_q, mla_k=mla_k,
                          kv=kv, lse_mla=lse_mla, mkv=mkv, lse_mem=lse_mem, groups=groups, gated=gated))
        if l < DEPTH - 1:
            h, hb = h_next, hb_next

    loss_sum = jnp.sum(sq_cols)

    grads = {k: [None] * DEPTH for k in ("w_in", "b_forget", "mla_q_norm_g", "w_mla_q_up", "mla_kv_norm_g",
                                         "w_mla_kv_up", "w_mem_kv", "w_out", "ln_g", "ln_b")}
    dmemn = []
    dy1, dy2, c1 = dh, None, 1.0
    for l in reversed(range(DEPTH)):
        r = saved[l]
        pa, pb = r["pa"], r["pb"]
        o_fox, o_sb, o_mla, o_mem = r["groups"]
        du, du_b, dmixed, dgate_b, dg, db = _ln_bwd(r["u"], p["ln_g"][l], dy1, dy2, c1, "ln_gate_bwd",
                                                    (p["w_out"][l], r["groups"], pb))
        grads["ln_g"][l], grads["ln_b"][l] = dg[0], db[0]
        grads["w_out"][l] = _matmul(r["gated"], du_b, F32, "out_proj_dw", "tn")

        dfq, dfk, dfv, dfc_q, dfc_k = _softmax_bwd(pairs(pa, 0), pairs(pa, 2), vals(pa, 4), o_fox, r["lse_fox"],
                                                   dmixed, 0, s, s, head_scale, True, r["fbias"], "fox_bwd")
        dmisc_f, dbf = _forget_bwd(pb, r["bias_row"], dfc_q + jnp.pad(dfc_k.T, ((0, 0), (0, 128 - 8))))
        grads["b_forget"][l] = dbf[0, :N_HEADS]

        dsq, dsk, dsv = _sb_bwd(pa, 6, 8, 10, dmixed, 1, *r["rm_sb"], s, head_scale, "sb_bwd")

        dqm, dkm, dvm = _softmax_bwd(r["mla_q"], r["mla_k"], vals(r["kv"], 2), o_mla, r["lse_mla"], dmixed, 2,
                                     s, s, mla_scale, True, None, "mla_bwd")
        dq_mla_b, dcq_b, dgq = _mla_q_bwd(dqm, pb, p["mla_q_norm_g"][l], r["wq_up"], tabs, mats)
        grads["w_mla_q_up"][l] = _heads_interleaved(_matmul(r["cqn"], dq_mla_b, F32, "q_up_dw", "tn"),
                                                    HEAD_DIM + MLA_ROPE, HEAD_DIM)
        grads["mla_q_norm_g"][l] = dgq[0]
        dkv_b, dckv_b, dmisc_k, dgkv = _mla_kv_bwd(dkm, dvm, pb, p["mla_kv_norm_g"][l], r["wkv_up"], tabs, mats)
        grads["w_mla_kv_up"][l] = _heads_interleaved(_matmul(r["ckvn"], dkv_b, F32, "kv_up_dw", "tn"),
                                                     2 * HEAD_DIM, HEAD_DIM)
        grads["mla_kv_norm_g"][l] = dgkv[0]

        dmq, dmk, dmv = _softmax_bwd(pairs(pa, 12), pairs(r["mkv"], 0), vals(r["mkv"], 2), o_mem, r["lse_mem"],
                                     dmixed, 3, s, nm, head_scale, False, None, "mem_bwd")
        dmkv_b = jnp.concatenate([dmk, dmv], axis=1).astype(MXU_DTYPE)
        grads["w_mem_kv"][l] = _matmul(memn_b, dmkv_b, F32, "mem_kv_dw", "tn")
        dmemn.append(_matmul(dmkv_b, p["w_mem_kv"][l], F32, "mem_kv_dx", "nt"))

        dp = [dfq, dfk, dfv, dsq, dsk, dsv, dmq, dgate_b, dcq_b, dckv_b, (dmisc_f, dmisc_k)]
        dhproj = _proj_dx(dp, r["wp"])
        grads["w_in"][l] = _merge_dw_in(_proj_dw(r["hb"], dp))
        dy1, dy2, c1 = du, dhproj, ALPHA

    dx, dg_in, db_in = _ln_bwd(x2, p["ln_in_g"], dy1, dy2, c1, "ln_in_bwd")
    _, dg_mem, db_mem = _ln_bwd(mem2, p["mem_ln_g"], dmemn[0], dmemn[1], 1.0, "ln_mem_bwd")
    out = {k: jnp.stack(v) for k, v in grads.items()}
    out.update(ln_in_g=dg_in[0], ln_in_b=db_in[0], mem_ln_g=dg_mem[0], mem_ln_b=db_mem[0])
    return loss_sum, dx, out


WIDE = "w_in"
FLAT_NAMES = ("w_out", "w_mem_kv", "w_mla_q_up", "w_mla_kv_up")
FLAT_ROWS = 896
BIG_NAMES = (WIDE,) + FLAT_NAMES
BIG_AXIS = dict(w_in=2, w_out=1, w_mem_kv=1, w_mla_q_up=2, w_mla_kv_up=2)
SMALL_NAMES = ("ln_in_g", "ln_in_b", "mem_ln_g", "mem_ln_b", "ln_g", "ln_b", "b_forget", "mla_q_norm_g",
               "mla_kv_norm_g")
ALL_NAMES = ("ln_in_g", "ln_in_b", "mem_ln_g", "mem_ln_b", "w_in", "b_forget", "mla_q_norm_g", "w_mla_q_up",
             "mla_kv_norm_g", "w_mla_kv_up", "w_mem_kv", "w_out", "ln_g", "ln_b")
N_CHIPS = 4
N_DEV = 8


def _rows_of(shape):
    rows = -(-int(np.prod(shape)) // LANES)
    return -(-rows // PACK_ALIGN) * PACK_ALIGN


def _pack(arrs, rows):
    parts = []
    for a in arrs:
        f = a.reshape(-1)
        n = _rows_of(a.shape) * LANES
        parts.append(jnp.pad(f, (0, n - f.shape[0])).reshape(-1, LANES))
    used = sum(q.shape[0] for q in parts)
    if rows > used:
        parts.append(jnp.zeros((rows - used, LANES), parts[0].dtype))
    return jnp.concatenate(parts, axis=0)


def _unpack(buf, shapes):
    out, r = [], 0
    for shp in shapes:
        n = _rows_of(shp)
        out.append(buf[r:r + n].reshape(-1)[:int(np.prod(shp))].reshape(shp))
        r += n
    return out


def _sharded_pair(get):
    wide = get(WIDE)
    return [wide.reshape(-1, wide.shape[-1]), _pack([get(n) for n in FLAT_NAMES], FLAT_ROWS)]


HBM_SPEC = pl.BlockSpec(memory_space=pltpu.HBM)


def _gather_weights(shards):
    n = len(shards)

    def body(*refs):
        w_refs, out_refs, (send_sems, recv_sems, local_sems) = refs[:n], refs[n:2 * n], refs[2 * n:]
        x, y, c = (lax.axis_index(a) for a in MESH_AXES)
        me, sibling = 2 * x + y, (x, y, 1 - c)
        chips = [(1 - x, y), (x, 1 - y), (1 - x, 1 - y)]
        local, first, passed = [], [], []
        for a in range(n):
            w_ref, out_ref, half = w_refs[a], out_refs[a], shards[a].shape[0] // 2

            def part(chip, core, out_ref=out_ref, half=half):
                return out_ref.at[chip, pl.ds(core * half, half)]

            def copy(k, src, dst, to, a=a):
                return pltpu.make_async_remote_copy(
                    src_ref=src, dst_ref=dst, send_sem=send_sems.at[6 * a + k], recv_sem=recv_sems.at[6 * a + k],
                    device_id=to, device_id_type=pl.DeviceIdType.MESH)

            local.append(pltpu.make_async_copy(w_ref, out_ref.at[me], local_sems.at[a]))
            local[-1].start()
            mine = [copy(k, w_ref.at[pl.ds(c * half, half)], part(me, c), (px, py, c))
                    for k, (px, py) in enumerate(chips)]
            for cp in mine:
                cp.start()
            first.append((mine, part, copy))
        for mine, part, copy in first:
            for k, (px, py) in enumerate(chips):
                copy(k, part(me, c), part(2 * px + py, c), (px, py, c)).wait_recv()
                passed.append(copy(3 + k, part(2 * px + py, c), part(2 * px + py, c), sibling))
                passed[-1].start()
        for mine, part, copy in first:
            for k, (px, py) in enumerate(chips):
                copy(3 + k, part(me, c), part(2 * px + py, 1 - c), sibling).wait_recv()
        for cp in [cp for mine, _, _ in first for cp in mine] + passed:
            cp.wait_send()
        for cp in local:
            cp.wait()

    return pl.pallas_call(
        body, name="gather_weights",
        out_shape=[jax.ShapeDtypeStruct((N_CHIPS,) + s.shape, s.dtype) for s in shards],
        in_specs=[HBM_SPEC] * n, out_specs=[HBM_SPEC] * n,
        scratch_shapes=[pltpu.SemaphoreType.DMA((6 * n,)), pltpu.SemaphoreType.DMA((6 * n,)),
                        pltpu.SemaphoreType.DMA((n,))],
    )(*shards)


def _exchange_grads(bigs, small):
    nb = len(bigs)
    halves = [b.shape[1] // 2 for b in bigs]
    chunks = [_pick(h, (128, 64, 32, 16)) for h in halves]

    def body(*refs):
        big_refs, small_ref = refs[:nb], refs[nb]
        sum_refs, out_refs = refs[nb + 1:2 * nb + 1], refs[2 * nb + 1:3 * nb + 2]
        scratch = refs[3 * nb + 2:]
        land_bufs, sum_bufs = scratch[:nb], scratch[nb:2 * nb]
        send_sems, recv_sems, local_sems, load_sems, swap_send, swap_recv, keep_sems = scratch[2 * nb:]
        x, y, c = (lax.axis_index(a) for a in MESH_AXES)
        me, my_chip = 4 * x + 2 * y + c, 2 * x + y
        flips = [(fx, fy, fc) for fx in (0, 1) for fy in (0, 1) for fc in (0, 1) if fx or fy or fc]
        peers = [(1 - x if fx else x, 1 - y if fy else y, 1 - c if fc else c) for fx, fy, fc in flips]

        sources = [lambda chip, core, r=big_refs[a], half=bigs[a].shape[1] // 2: r.at[chip, pl.ds(core * half, half)]
                   for a in range(nb)] + [lambda chip, core: small_ref]

        def copy(a, k, src, slot, to):
            return pltpu.make_async_remote_copy(
                src_ref=src, dst_ref=out_refs[a].at[slot], send_sem=send_sems.at[7 * a + k],
                recv_sem=recv_sems.at[7 * a + k], device_id=to, device_id_type=pl.DeviceIdType.MESH)

        own = [pltpu.make_async_copy(src(my_chip, c), out_refs[a].at[me], local_sems.at[a])
               for a, src in enumerate(sources)]
        for cp in own:
            cp.start()
        sends = [copy(a, k, src(2 * px + py, pc), me, (px, py, pc))
                 for a, src in enumerate(sources) for k, (px, py, pc) in enumerate(peers)]
        for cp in sends:
            cp.start()
        for a, src in enumerate(sources):
            for k, (px, py, pc) in enumerate(peers):
                copy(a, k, src(my_chip, c), 4 * px + 2 * py + pc, (px, py, pc)).wait_recv()
        for cp in sends:
            cp.wait_send()
        for cp in own:
            cp.wait()

        tails = []
        loads = [pltpu.make_async_copy(out_refs[a], land_bufs[a], load_sems.at[a]) for a in range(nb)]
        for load in loads:
            load.start()
        for a in range(nb):
            loads[a].wait()
            for r0 in range(0, halves[a], chunks[a]):
                rows = pl.ds(r0, chunks[a])
                total = land_bufs[a][0, rows, :].astype(F32)
                for d in range(1, N_DEV):
                    total = total + land_bufs[a][d, rows, :].astype(F32)
                sum_bufs[a][rows, :] = total
            keep = pltpu.make_async_copy(sum_bufs[a], sum_refs[a].at[c], keep_sems.at[a])
            give = pltpu.make_async_remote_copy(
                src_ref=sum_bufs[a], dst_ref=sum_refs[a].at[c], send_sem=swap_send.at[a], recv_sem=swap_recv.at[a],
                device_id=(x, y, 1 - c), device_id_type=pl.DeviceIdType.MESH)
            keep.start()
            give.start()
            tails.append((keep, give))
        for a, (keep, give) in enumerate(tails):
            pltpu.make_async_remote_copy(
                src_ref=sum_bufs[a], dst_ref=sum_refs[a].at[1 - c], send_sem=swap_send.at[a], recv_sem=swap_recv.at[a],
                device_id=(x, y, 1 - c), device_id_type=pl.DeviceIdType.MESH).wait_recv()
            give.wait_send()
            keep.wait()

    return pl.pallas_call(
        body, name="exchange_grads",
        out_shape=[jax.ShapeDtypeStruct((2, h, b.shape[2]), F32) for h, b in zip(halves, bigs)]
        + [jax.ShapeDtypeStruct((N_DEV, h, b.shape[2]), b.dtype) for h, b in zip(halves, bigs)]
        + [jax.ShapeDtypeStruct((N_DEV,) + small.shape, small.dtype)],
        in_specs=[HBM_SPEC] * (nb + 1), out_specs=[HBM_SPEC] * (2 * nb + 1),
        scratch_shapes=[pltpu.VMEM((N_DEV, h, b.shape[2]), b.dtype) for h, b in zip(halves, bigs)]
        + [pltpu.VMEM((h, b.shape[2]), F32) for h, b in zip(halves, bigs)]
        + [pltpu.SemaphoreType.DMA((7 * (nb + 1),)), pltpu.SemaphoreType.DMA((7 * (nb + 1),)),
           pltpu.SemaphoreType.DMA((nb + 1,)), pltpu.SemaphoreType.DMA((nb,)), pltpu.SemaphoreType.DMA((nb,)),
           pltpu.SemaphoreType.DMA((nb,)), pltpu.SemaphoreType.DMA((nb,))],
        compiler_params=pltpu.CompilerParams(vmem_limit_bytes=VMEM_LIMIT),
    )(*bigs, small)


def _adamw(parts, w, m, v, name):
    rows, width = w.shape
    n_parts = parts.shape[0]
    tile = _pick(rows, (512, 448, 144, 128, 16, 8))
    bc1 = 1.0 - ADAM_B1 ** ADAM_STEP
    bc2 = 1.0 - ADAM_B2 ** ADAM_STEP

    def kern(p_ref, w_ref, m_ref, v_ref, g_ref, d_ref, nm_ref, nv_ref):
        g = p_ref[0].astype(F32)
        for d in range(1, n_parts):
            g = g + p_ref[d].astype(F32)
        nm = ADAM_B1 * m_ref[...] + (1.0 - ADAM_B1) * g
        nv = ADAM_B2 * v_ref[...] + (1.0 - ADAM_B2) * (g * g)
        g_ref[...] = g
        nm_ref[...] = nm
        nv_ref[...] = nv
        d_ref[...] = -ADAM_LR * ((nm / bc1) / (jnp.sqrt(nv / bc2) + ADAM_EPS) + ADAM_WD * w_ref[...])

    spec = pl.BlockSpec((tile, width), lambda i: (i, 0))
    return pl.pallas_call(
        kern, name=name, grid=(rows // tile,),
        in_specs=[pl.BlockSpec((n_parts, tile, width), lambda i: (0, i, 0)), spec, spec, spec],
        out_specs=[spec] * 4, out_shape=[jax.ShapeDtypeStruct((rows, width), F32)] * 4,
        compiler_params=pltpu.CompilerParams(dimension_semantics=("parallel",), vmem_limit_bytes=VMEM_LIMIT),
    )(parts, w, m, v)


def kernel(x, mem, ln_in_g, ln_in_b, mem_ln_g, mem_ln_b, w_in, b_forget, mla_q_norm_g, w_mla_q_up, mla_kv_norm_g, w_mla_kv_up, w_mem_kv, w_out, ln_g, ln_b, loss_target, m_ln_in_g, m_ln_in_b, m_mem_ln_g, m_mem_ln_b, m_w_in, m_b_forget, m_mla_q_norm_g, m_w_mla_q_up, m_mla_kv_norm_g, m_w_mla_kv_up, m_w_mem_kv, m_w_out, m_ln_g, m_ln_b, v_ln_in_g, v_ln_in_b, v_mem_ln_g, v_mem_ln_b, v_w_in, v_b_forget, v_mla_q_norm_g, v_w_mla_q_up, v_mla_kv_norm_g, v_w_mla_kv_up, v_w_mem_kv, v_w_out, v_ln_g, v_ln_b):
    w = dict(ln_in_g=ln_in_g, ln_in_b=ln_in_b, mem_ln_g=mem_ln_g, mem_ln_b=mem_ln_b, w_in=w_in, b_forget=b_forget,
             mla_q_norm_g=mla_q_norm_g, w_mla_q_up=w_mla_q_up, mla_kv_norm_g=mla_kv_norm_g,
             w_mla_kv_up=w_mla_kv_up, w_mem_kv=w_mem_kv, w_out=w_out, ln_g=ln_g, ln_b=ln_b)
    mo = dict(ln_in_g=m_ln_in_g, ln_in_b=m_ln_in_b, mem_ln_g=m_mem_ln_g, mem_ln_b=m_mem_ln_b, w_in=m_w_in,
              b_forget=m_b_forget, mla_q_norm_g=m_mla_q_norm_g, w_mla_q_up=m_w_mla_q_up,
              mla_kv_norm_g=m_mla_kv_norm_g, w_mla_kv_up=m_w_mla_kv_up, w_mem_kv=m_w_mem_kv, w_out=m_w_out,
              ln_g=m_ln_g, ln_b=m_ln_b)
    vo = dict(ln_in_g=v_ln_in_g, ln_in_b=v_ln_in_b, mem_ln_g=v_mem_ln_g, mem_ln_b=v_mem_ln_b, w_in=v_w_in,
              b_forget=v_b_forget, mla_q_norm_g=v_mla_q_norm_g, w_mla_q_up=v_w_mla_q_up,
              mla_kv_norm_g=v_mla_kv_norm_g, w_mla_kv_up=v_w_mla_kv_up, w_mem_kv=v_w_mem_kv, w_out=v_w_out,
              ln_g=v_ln_g, ln_b=v_ln_b)
    flat_shapes = [w[n].shape for n in FLAT_NAMES]
    small_shapes = [w[n].shape for n in SMALL_NAMES]

    got_wide, got_flat = _gather_weights(_sharded_pair(lambda n: w[n].astype(MXU_DTYPE)))
    full = dict(w)
    full[WIDE] = jnp.concatenate([got_wide[j] for j in range(N_CHIPS)], axis=1).reshape(
        w[WIDE].shape[:2] + (N_CHIPS * w[WIDE].shape[2],))
    per_chip = [_unpack(got_flat[j], flat_shapes) for j in range(N_CHIPS)]
    for idx, n in enumerate(FLAT_NAMES):
        full[n] = jnp.concatenate([per_chip[j][idx] for j in range(N_CHIPS)], axis=BIG_AXIS[n])

    loss_sum, dx, g = _local_step(x[0], mem[0], loss_target[0], full)
    loss = lax.psum(loss_sum * (0.5 / D_MODEL), MESH_AXES)

    def shard_of(n, j):
        ax, size = BIG_AXIS[n], w[n].shape[BIG_AXIS[n]]
        return lax.slice_in_dim(g[n], j * size, (j + 1) * size, axis=ax).astype(MXU_DTYPE)

    per_dest = [_sharded_pair(lambda n, j=j: shard_of(n, j)) for j in range(N_CHIPS)]
    bigs = [jnp.stack([per_dest[j][a] for j in range(N_CHIPS)]) for a in range(2)]
    exchanged = _exchange_grads(bigs, _pack([g[n] for n in SMALL_NAMES], SMALL_ROWS))
    halves, small_parts = exchanged[:2], exchanged[-1]

    res = []
    for a, (grad, nm) in enumerate(zip(halves, ("adamw_wide", "adamw_flat"))):
        state = [_sharded_pair(lambda n, src=src: src[n])[a] for src in (w, mo, vo)]
        res.append(_adamw(grad.reshape((1,) + state[0].shape), *state, nm))
    res_small = _adamw(small_parts, *[_pack([src[n] for n in SMALL_NAMES], SMALL_ROWS) for src in (w, mo, vo)],
                       "adamw_replicated")
    outs = []
    for kind in range(4):
        vals = {WIDE: res[0][kind].reshape(w[WIDE].shape)}
        vals.update(zip(FLAT_NAMES, _unpack(res[1][kind], flat_shapes)))
        vals.update(zip(SMALL_NAMES, _unpack(res_small[kind], small_shapes)))
        outs += [vals[n] for n in ALL_NAMES]
    return (loss, dx[None], *outs)
```

```python
import functools

import numpy as np
import jax
import jax.numpy as jnp
from jax import lax
from jax.experimental import pallas as pl
from jax.experimental.pallas import tpu as pltpu

F32 = jnp.float32
MXU_DTYPE = jnp.bfloat16

DEPTH = 2
D_MODEL = 1024
GROUP_W = 256
N_HEADS = 4
HEAD_DIM = 64
MLA_Q_RANK = 256
MLA_KV_RANK = 128
MLA_ROPE = 32
MLA_Q_COLS = N_HEADS * (HEAD_DIM + MLA_ROPE)
MLA_KV_COLS = N_HEADS * 2 * HEAD_DIM
ROPE_THETA = 10000.0
LN_EPS = 1e-5
RMS_EPS = 1e-6
ALPHA = (2 * DEPTH) ** 0.25
ADAM_LR, ADAM_B1, ADAM_B2, ADAM_EPS, ADAM_WD, ADAM_STEP = 0.001, 0.9, 0.999, 1e-08, 0.01, 10

_SPLIT = (256, 256, 256, 4, 256, 256, 256, 256, 128, 32, 256, 1024)
_OFF = [int(o) for o in np.cumsum((0,) + _SPLIT)]
IN_COLS = _OFF[-1]
PA_COLS = 7 * GROUP_W
PB_COLS = 1024 + 256 + 128 + 128
PB_CQ_BLK, PB_CKV_BLK, PB_MISC_BLK = 4, 10, 11
MISC_KROT = 4

LANES = 1024
PACK_ALIGN = 16
SMALL_ROWS = 144
ROW_TILE = 1024
GATHER_LN_ROWS = 512
PROJ_BWD_ROWS = 512
ATT_TILE = 256
SB_QUERY_TILE = 256
SOFTMAX_TILE = 512
SOFTMAX_FWD_TILE = 1024
DENSE_QUERY_TILE = 1024
FWD_PAIRS_PER_LOOP = 1
BWD_PAIRS_PER_LOOP = 2
PAIR = 128
SB_SLOT = PAIR // N_HEADS
VMEM_LIMIT = 56 * 1024 * 1024
MATMUL_VMEM = 30 * 1024 * 1024
NEG = -1e30
LOG2E = 1.4426950408889634
LN2 = 0.6931471805599453
EXP_UNDERFLOW = -104.0
DEAD_LOGIT = -110.0
REACH_SLACK = 1.0 + 2.0 ** -10
MESH_AXES = ("x", "y", "c")


def _dot(a, b):
    return jnp.dot(a, b, preferred_element_type=F32)


def _dot_nt(a, b):
    return lax.dot_general(a, b, (((1,), (1,)), ((), ())), preferred_element_type=F32)


def _dot_tn(a, b):
    return lax.dot_general(a, b, (((0,), (0,)), ((), ())), preferred_element_type=F32)


def _split2(x):
    hi = x.astype(MXU_DTYPE)
    lo = (x - hi.astype(F32)).astype(MXU_DTYPE)
    return hi, lo


def _split3(x):
    hi = x.astype(MXU_DTYPE)
    r = x - hi.astype(F32)
    mid = r.astype(MXU_DTYPE)
    lo = (r - mid.astype(F32)).astype(MXU_DTYPE)
    return hi, mid, lo


def _dot_exact_r(x, pm):
    hi, mid, lo = _split3(x)
    return _dot(hi, pm) + _dot(mid, pm) + _dot(lo, pm)


def _dot_exact_l(pm, x):
    hi, mid, lo = _split3(x)
    return _dot(pm, hi) + _dot(pm, mid) + _dot(pm, lo)


def _pick(dim, prefs):
    for p in prefs:
        if dim % p == 0:
            return p
    return dim


def _softplus(z):
    return jnp.maximum(z, 0.0) + jnp.log(1.0 + jnp.exp(-jnp.abs(z)))


def _tile_options(dim):
    opts = [d for d in range(128, min(dim, 2048) + 1, 128) if dim % d == 0]
    return opts or [dim]


def _matmul_tiles(m, n, k, out_bytes):
    tk = k if k <= 4096 else _pick(k, (1024, 512, 256, 128))
    best = None
    for tm in _tile_options(m):
        for tn in _tile_options(n):
            vmem = 2 * 2 * (tm * tk + tk * tn) + 4 * tm * tn + 2 * out_bytes * tm * tn
            if vmem <= MATMUL_VMEM and (best is None or tm * tn / (tm + tn) > best[0]):
                best = (tm * tn / (tm + tn), tm, tn)
    return best[1], best[2], tk


def _matmul(a, b, out_dtype, name, mode="nn"):
    m, k = (a.shape[1], a.shape[0]) if mode == "tn" else a.shape
    n = b.shape[0] if mode == "nt" else b.shape[1]
    tm, tn, tk = _matmul_tiles(m, n, k, jnp.dtype(out_dtype).itemsize)
    nk = k // tk
    dot = {"nn": _dot, "tn": _dot_tn, "nt": _dot_nt}[mode]

    def kern(a_ref, b_ref, o_ref, *acc):
        if nk == 1:
            o_ref[...] = dot(a_ref[...], b_ref[...]).astype(o_ref.dtype)
            return
        acc_ref, = acc
        kk = pl.program_id(2)

        @pl.when(kk == 0)
        def _():
            acc_ref[...] = jnp.zeros_like(acc_ref)

        acc_ref[...] += dot(a_ref[...], b_ref[...])

        @pl.when(kk == nk - 1)
        def _():
            o_ref[...] = acc_ref[...].astype(o_ref.dtype)

    a_spec = (pl.BlockSpec((tk, tm), lambda i, j, kk: (kk, i)) if mode == "tn"
              else pl.BlockSpec((tm, tk), lambda i, j, kk: (i, kk)))
    b_spec = (pl.BlockSpec((tn, tk), lambda i, j, kk: (j, kk)) if mode == "nt"
              else pl.BlockSpec((tk, tn), lambda i, j, kk: (kk, j)))
    return pl.pallas_call(
        kern, name=name, grid=(m // tm, n // tn, nk), in_specs=[a_spec, b_spec],
        out_specs=pl.BlockSpec((tm, tn), lambda i, j, kk: (i, j)),
        out_shape=jax.ShapeDtypeStruct((m, n), out_dtype),
        scratch_shapes=[pltpu.VMEM((tm, tn), F32)] if nk > 1 else [],
        compiler_params=pltpu.CompilerParams(
            dimension_semantics=("parallel", "parallel", "arbitrary"), vmem_limit_bytes=VMEM_LIMIT),
    )(a.astype(MXU_DTYPE), b.astype(MXU_DTYPE))


def _proj_in(hb, wa, wb, key_cols):
    rows, d = hb.shape
    tm = min(PROJ_BWD_ROWS, rows)

    def kern(a_ref, wa_ref, wb_ref, pa_ref, pb_ref, kmax_ref):
        @pl.when(pl.program_id(0) == 0)
        def _():
            kmax_ref[...] = jnp.zeros_like(kmax_ref)

        a = a_ref[...]
        pa = _dot(a, wa_ref[...]).astype(pa_ref.dtype)
        pa_ref[...] = pa
        pb_ref[...] = _dot(a, wb_ref[...])
        cols = {}
        for p in range(2):
            k32 = pa[:, key_cols + p * PAIR:key_cols + (p + 1) * PAIR].astype(F32)
            for e in range(2):
                sq = jnp.sum(jnp.where(_lane_mask("pair", e, tm), k32 * k32, 0.0), axis=1, keepdims=True)
                cols[2 * p + e] = jnp.max(sq, axis=0, keepdims=True)
        lane = lax.broadcasted_iota(jnp.int32, (1, PAIR), 1)
        kmax_ref[...] = jnp.maximum(kmax_ref[...], sum(jnp.where(lane == h, c, 0.0) for h, c in cols.items()))

    whole = lambda w: pl.BlockSpec(w.shape, lambda i: (0, 0), pipeline_mode=pl.Buffered(1))
    return pl.pallas_call(
        kern, name="proj_in", grid=(rows // tm,),
        in_specs=[pl.BlockSpec((tm, d), lambda i: (i, 0)), whole(wa), whole(wb)],
        out_specs=[pl.BlockSpec((tm, wa.shape[1]), lambda i: (i, 0)), pl.BlockSpec((tm, wb.shape[1]), lambda i: (i, 0)),
                   pl.BlockSpec((1, PAIR), lambda i: (0, 0))],
        out_shape=[jax.ShapeDtypeStruct((rows, wa.shape[1]), MXU_DTYPE), jax.ShapeDtypeStruct((rows, wb.shape[1]), F32),
                   jax.ShapeDtypeStruct((1, PAIR), F32)],
        compiler_params=pltpu.CompilerParams(dimension_semantics=("arbitrary",), vmem_limit_bytes=VMEM_LIMIT),
    )(hb, wa.astype(MXU_DTYPE), wb.astype(MXU_DTYPE))


def _piece_arrays(pieces):
    return [a for p in pieces for a in (p if isinstance(p, tuple) else (p,))]


def _join_pieces(refs, pieces):
    refs, cols = list(refs), []
    for p in pieces:
        vals = [refs.pop(0)[...] for _ in (p if isinstance(p, tuple) else (p,))]
        cols.append(functools.reduce(jnp.add, vals).astype(MXU_DTYPE))
    return jnp.concatenate(cols, axis=1)


def _proj_dx(pieces, w):
    arrs = _piece_arrays(pieces)
    rows, n = arrs[0].shape[0], w.shape[0]
    tm = min(PROJ_BWD_ROWS, rows)

    def kern(*refs):
        refs[-1][...] = _dot_nt(_join_pieces(refs[:len(arrs)], pieces), refs[len(arrs)][...])

    return pl.pallas_call(
        kern, name="proj_dx", grid=(rows // tm,),
        in_specs=[pl.BlockSpec((tm, a.shape[1]), lambda i: (i, 0)) for a in arrs]
        + [pl.BlockSpec(w.shape, lambda i: (0, 0), pipeline_mode=pl.Buffered(1))],
        out_specs=pl.BlockSpec((tm, n), lambda i: (i, 0)), out_shape=jax.ShapeDtypeStruct((rows, n), F32),
        compiler_params=pltpu.CompilerParams(dimension_semantics=("parallel",), vmem_limit_bytes=VMEM_LIMIT),
    )(*arrs, w.astype(MXU_DTYPE))


def _proj_dw(a, pieces):
    arrs = _piece_arrays(pieces)
    rows, m = a.shape
    k = sum(x.shape[1] for x in (p[0] if isinstance(p, tuple) else p for p in pieces))
    tk = min(PROJ_BWD_ROWS, rows)

    def kern(*refs):
        o_ref = refs[-1]

        @pl.when(pl.program_id(0) == 0)
        def _():
            o_ref[...] = jnp.zeros_like(o_ref)

        o_ref[...] += _dot_tn(refs[0][...], _join_pieces(refs[1:1 + len(arrs)], pieces))

    return pl.pallas_call(
        kern, name="proj_dw", grid=(rows // tk,),
        in_specs=[pl.BlockSpec((tk, m), lambda i: (i, 0))]
        + [pl.BlockSpec((tk, x.shape[1]), lambda i: (i, 0)) for x in arrs],
        out_specs=pl.BlockSpec((m, k), lambda i: (0, 0), pipeline_mode=pl.Buffered(1)),
        out_shape=jax.ShapeDtypeStruct((m, k), F32),
        compiler_params=pltpu.CompilerParams(dimension_semantics=("arbitrary",), vmem_limit_bytes=VMEM_LIMIT),
    )(a, *arrs)


def _rowwise(body, name, rows, tile, row_ins, full_ins, row_outs, acc_outs=(), scratch=(),
             reverse=False, sequential=False):
    n = rows // tile

    def ridx(i):
        return (n - 1 - i) if reverse else i

    in_specs, args = [], []
    for arr, width, cb in row_ins:
        in_specs.append(pl.BlockSpec((tile, width), lambda i, cb=cb: (ridx(i), cb)))
        args.append(arr)
    for arr in full_ins:
        in_specs.append(pl.BlockSpec(arr.shape, lambda i, nd=arr.ndim: (0,) * nd))
        args.append(arr)
    out_shape = [jax.ShapeDtypeStruct((rows, w), dt) for w, dt in row_outs]
    out_shape += [jax.ShapeDtypeStruct(s, dt) for s, dt in acc_outs]
    out_specs = [pl.BlockSpec((tile, w), lambda i: (ridx(i), 0)) for w, dt in row_outs]
    out_specs += [pl.BlockSpec(s, lambda i, nd=len(s): (0,) * nd) for s, dt in acc_outs]

    def kern(*refs):
        body(pl.program_id(0), *refs)

    sem = "arbitrary" if (acc_outs or sequential) else "parallel"
    return pl.pallas_call(
        kern, name=name, grid=(n,), in_specs=in_specs, out_specs=out_specs, out_shape=out_shape,
        scratch_shapes=list(scratch),
        compiler_params=pltpu.CompilerParams(dimension_semantics=(sem,), vmem_limit_bytes=VMEM_LIMIT),
    )(*args)


def _ln_stats(u):
    mu = jnp.mean(u, axis=-1, keepdims=True)
    xc = u - mu
    var = jnp.mean(xc * xc, axis=-1, keepdims=True)
    return xc, lax.rsqrt(var + LN_EPS)


def _ln_fwd(a, g, beta, name):
    rows, d = a.shape

    def body(i, a_ref, g_ref, be_ref, h_ref, hb_ref):
        xc, rstd = _ln_stats(a_ref[...])
        y = xc * rstd * g_ref[...] + be_ref[...]
        h_ref[...] = y
        hb_ref[...] = y.astype(hb_ref.dtype)

    return _rowwise(body, name, rows, min(ROW_TILE, rows), [(a, d, 0)],
                    [g.reshape(1, d), beta.reshape(1, d)], [(d, F32), (d, MXU_DTYPE)])


def _ln_bwd(u, g, dy1, dy2, c1, name, gate=None):
    rows, d = u.shape
    has_2 = dy2 is not None
    n_groups = len(gate[1]) if gate else 0

    def body(i, *refs):
        refs = list(refs)
        uu = refs.pop(0)[...]
        dy = c1 * refs.pop(0)[...]
        if has_2:
            dy = dy + refs.pop(0)[...]
        if gate:
            mixed = jnp.concatenate([refs.pop(0)[...] for _ in range(n_groups)], axis=1)
            gt = refs.pop(0)[...]
            g_ref, w_ref, du_ref, dub_ref, dm_ref, dgate_ref, dg_ref, db_ref = refs
        else:
            g_ref, du_ref, dg_ref, db_ref = refs

        @pl.when(i == 0)
        def _():
            dg_ref[...] = jnp.zeros_like(dg_ref)
            db_ref[...] = jnp.zeros_like(db_ref)

        xc, rstd = _ln_stats(uu)
        xhat = xc * rstd
        dxh = dy * g_ref[...]
        m1 = jnp.mean(dxh, axis=-1, keepdims=True)
        m2 = jnp.mean(dxh * xhat, axis=-1, keepdims=True)
        du = rstd * (dxh - m1 - xhat * m2)
        du_ref[...] = du
        dg_ref[...] += jnp.sum(dy * xhat, axis=0, keepdims=True)
        db_ref[...] += jnp.sum(dy, axis=0, keepdims=True)
        if gate:
            dub = du.astype(dub_ref.dtype)
            dub_ref[...] = dub
            dgated = _dot_nt(dub, w_ref[...])
            sig = 1.0 / (1.0 + jnp.exp(-gt))
            dm_ref[...] = dgated * (gt * sig)
            dgate_ref[...] = (dgated * mixed * (sig * (1.0 + gt * (1.0 - sig)))).astype(dgate_ref.dtype)

    row_ins = [(u, d, 0), (dy1, d, 0)] + ([(dy2, d, 0)] if has_2 else [])
    full_ins = [g.reshape(1, d)]
    row_outs = [(d, F32)]
    if gate:
        w_out, groups, pb = gate
        w = GROUP_W * n_groups
        row_ins += [(o, GROUP_W, 0) for o in groups] + [(pb, w, 0)]
        full_ins += [w_out.astype(MXU_DTYPE)]
        row_outs += [(d, MXU_DTYPE), (w, F32), (w, MXU_DTYPE)]
    return _rowwise(body, name, rows, min(PROJ_BWD_ROWS if gate else ROW_TILE, rows), row_ins, full_ins, row_outs,
                    [((1, d), F32), ((1, d), F32)])


def _gate_out_proj_ln(groups, pb, w_out, h, g, beta, target=None):
    rows, d = h.shape
    last = target is not None
    w = GROUP_W * len(groups)

    def body(i, *refs):
        refs = list(refs)
        mixed = jnp.concatenate([refs.pop(0)[...] for _ in groups], axis=1)
        gate = refs.pop(0)[...]
        if last:
            h_ref, t_ref, w_ref, g_ref, be_ref, a_ref, u_ref, dh_ref, acc_ref = refs

            @pl.when(i == 0)
            def _():
                acc_ref[...] = jnp.zeros_like(acc_ref)
        else:
            h_ref, w_ref, g_ref, be_ref, a_ref, u_ref, o_ref, ob_ref = refs
        gated = (mixed * (gate / (1.0 + jnp.exp(-gate)))).astype(a_ref.dtype)
        a_ref[...] = gated
        u = ALPHA * h_ref[...] + _dot(gated, w_ref[...])
        u_ref[...] = u
        xc, rstd = _ln_stats(u)
        y = xc * rstd * g_ref[...] + be_ref[...]
        if last:
            e = y - t_ref[...]
            dh_ref[...] = e * (1.0 / d)
            acc_ref[...] += jnp.sum(e * e, axis=0, keepdims=True)
        else:
            o_ref[...] = y
            ob_ref[...] = y.astype(ob_ref.dtype)

    row_ins = [(o, GROUP_W, 0) for o in groups] + [(pb, w, 0), (h, d, 0)] + ([(target, d, 0)] if last else [])
    full_ins = [w_out.astype(MXU_DTYPE), g.reshape(1, d), beta.reshape(1, d)]
    if last:
        return _rowwise(body, "gate_out_proj_ln_loss", rows, PROJ_BWD_ROWS, row_ins, full_ins,
                        [(w, MXU_DTYPE), (d, F32), (d, F32)], [((1, d), F32)])
    return _rowwise(body, "gate_out_proj_ln", rows, PROJ_BWD_ROWS, row_ins, full_ins,
                    [(w, MXU_DTYPE), (d, F32), (d, F32), (d, MXU_DTYPE)])


def _tri(n, kind):
    r = np.arange(n)[:, None]
    c = np.arange(n)[None, :]
    m = {"lower_incl": r >= c, "upper_incl": r <= c, "row_gt_col": r > c, "row_lt_col": r < c}[kind]
    return jnp.asarray(m.astype(np.float32), dtype=MXU_DTYPE)


def _forget_fwd(pb, bias_row):
    rows = pb.shape[0]
    tile = _pick(rows, (1024, 512, 256))

    def body(i, x_ref, b_ref, l_ref, o_ref, carry_ref):
        @pl.when(i == 0)
        def _():
            carry_ref[...] = jnp.zeros_like(carry_ref)

        xx = x_ref[...] + b_ref[...]
        lane = lax.broadcasted_iota(jnp.int32, xx.shape, 1)
        lf = jnp.where(lane < N_HEADS, -_softplus(-xx), 0.0)
        o_ref[...] = _dot_exact_l(l_ref[...], lf) + carry_ref[...]
        carry_ref[...] += jnp.sum(lf, axis=0, keepdims=True)

    return _rowwise(body, "forget_fwd", rows, tile, [(pb, 128, PB_MISC_BLK)],
                    [bias_row, _tri(tile, "lower_incl")], [(128, F32)],
                    scratch=[pltpu.VMEM((1, 128), F32)], sequential=True)[0]


def _forget_bwd(pb, bias_row, dfc):
    rows = pb.shape[0]
    tile = _pick(rows, (1024, 512, 256))

    def body(i, x_ref, df_ref, b_ref, u_ref, o_ref, db_ref, carry_ref):
        @pl.when(i == 0)
        def _():
            carry_ref[...] = jnp.zeros_like(carry_ref)
            db_ref[...] = jnp.zeros_like(db_ref)

        df = df_ref[...]
        sfx = _dot_exact_l(u_ref[...], df) + carry_ref[...]
        carry_ref[...] += jnp.sum(df, axis=0, keepdims=True)
        xx = x_ref[...] + b_ref[...]
        lane = lax.broadcasted_iota(jnp.int32, xx.shape, 1)
        dl = jnp.where(lane < N_HEADS, sfx / (1.0 + jnp.exp(xx)), 0.0)
        o_ref[...] = dl
        db_ref[...] += jnp.sum(dl, axis=0, keepdims=True)

    return _rowwise(body, "forget_bwd", rows, tile,
                    [(pb, 128, PB_MISC_BLK), (dfc, 128, 0)],
                    [bias_row, _tri(tile, "upper_incl")], [(128, F32)], [((1, 128), F32)],
                    scratch=[pltpu.VMEM((1, 128), F32)], reverse=True, sequential=True)


def _rope_tables(s):
    half = MLA_ROPE // 2
    inv_freq = ROPE_THETA ** (-jnp.arange(half, dtype=F32) / half)
    ang = jnp.arange(s).astype(F32)[:, None] * inv_freq[None, :]
    cos2 = jnp.tile(jnp.cos(ang), (1, 2))
    sin2 = jnp.tile(jnp.sin(ang), (1, 2))
    cx = jnp.tile(cos2, (1, N_HEADS))
    sx = jnp.tile(sin2, (1, N_HEADS))
    pad = ((0, 0), (MISC_KROT, 128 - MISC_KROT - MLA_ROPE))
    ck = jnp.pad(cos2, pad)
    sk = jnp.pad(sin2, pad)
    return dict(ck=ck, sk=sk, cx=cx, sx=sx)


def _rot_matrix(width, bases):
    half = MLA_ROPE // 2
    p = np.zeros((width, width), np.float32)
    for b in bases:
        for i in range(half):
            p[b + half + i, b + i] = -1.0
            p[b + i, b + half + i] = 1.0
    return p


def _rope_matrices():
    pk = _rot_matrix(128, [MISC_KROT])
    p4 = _rot_matrix(128, [h * MLA_ROPE for h in range(N_HEADS)])
    a = np.zeros((128, 128), np.float32)
    for h in range(N_HEADS):
        for r in range(MLA_ROPE):
            a[h * MLA_ROPE + r, MISC_KROT + r] = 1.0
    cast = lambda m: jnp.asarray(m, dtype=MXU_DTYPE)
    return dict(p4=cast(p4), p4t=cast(p4.T), pk=cast(pk), spread=cast(a.T), xa=cast(a), xb=cast(p4.T @ a))


def _rms(c, g):
    r = lax.rsqrt(jnp.mean(c * c, axis=-1, keepdims=True) + RMS_EPS)
    return c * r * g


def _mla_q_fwd(pb, g, w_up, tabs, mats):
    rows = pb.shape[0]

    def body(i, c_ref, cos_ref, sin_ref, g_ref, w_ref, p_ref, q_ref, cn_ref):
        cn = _rms(c_ref[...], g_ref[...]).astype(cn_ref.dtype)
        cn_ref[...] = cn
        q = _dot(cn, w_ref[...])
        qr = q[:, GROUP_W:]
        q_ref[:, :GROUP_W] = q[:, :GROUP_W].astype(q_ref.dtype)
        q_ref[:, GROUP_W:] = (qr * cos_ref[...] + _dot_exact_r(qr, p_ref[...]) * sin_ref[...]).astype(q_ref.dtype)

    return _rowwise(body, "mla_q_fwd", rows, ROW_TILE,
                    [(pb, MLA_Q_RANK, PB_CQ_BLK), (tabs["cx"], PAIR, 0), (tabs["sx"], PAIR, 0)],
                    [g.reshape(1, MLA_Q_RANK), w_up.astype(MXU_DTYPE), mats["p4"]],
                    [(MLA_Q_COLS, MXU_DTYPE), (MLA_Q_RANK, MXU_DTYPE)])


def _mla_kv_fwd(pb, g, w_up, tabs, mats):
    rows = pb.shape[0]

    def body(i, c_ref, x_ref, cos_ref, sin_ref, g_ref, w_ref, p_ref, sp_ref, kv_ref, cn_ref, kr_ref):
        cn = _rms(c_ref[...], g_ref[...]).astype(cn_ref.dtype)
        cn_ref[...] = cn
        kv_ref[...] = _dot(cn, w_ref[...]).astype(kv_ref.dtype)
        xx = x_ref[...]
        kr = xx * cos_ref[...] + _dot_exact_r(xx, p_ref[...]) * sin_ref[...]
        kr_ref[...] = _dot_exact_r(kr, sp_ref[...]).astype(kr_ref.dtype)

    return _rowwise(body, "mla_kv_fwd", rows, ROW_TILE,
                    [(pb, MLA_KV_RANK, PB_CKV_BLK), (pb, 128, PB_MISC_BLK), (tabs["ck"], 128, 0), (tabs["sk"], 128, 0)],
                    [g.reshape(1, MLA_KV_RANK), w_up.astype(MXU_DTYPE), mats["pk"], mats["spread"]],
                    [(MLA_KV_COLS, MXU_DTYPE), (MLA_KV_RANK, MXU_DTYPE), (128, MXU_DTYPE)])


def _rms_bwd(c, g, dy):
    r = lax.rsqrt(jnp.mean(c * c, axis=-1, keepdims=True) + RMS_EPS)
    dyg = dy * g
    dc = r * dyg - c * (r * r * r) * jnp.mean(c * dyg, axis=-1, keepdims=True)
    return dc, jnp.sum(dy * c * r, axis=0, keepdims=True)


def _mla_q_bwd(dq_full, pb, g, w_up, tabs, mats):
    rows, nq = dq_full.shape

    def body(i, d_ref, c_ref, cos_ref, sin_ref, g_ref, w_ref, pt_ref, dq_ref, dc_ref, dg_ref):
        @pl.when(i == 0)
        def _():
            dg_ref[...] = jnp.zeros_like(dg_ref)

        dr = d_ref[:, GROUP_W:]
        dq = jnp.concatenate([d_ref[:, :GROUP_W], dr * cos_ref[...] + _dot_exact_r(dr * sin_ref[...], pt_ref[...])],
                             axis=1).astype(dq_ref.dtype)
        dq_ref[...] = dq
        dc, dg = _rms_bwd(c_ref[...], g_ref[...], _dot_nt(dq, w_ref[...]))
        dc_ref[...] = dc.astype(dc_ref.dtype)
        dg_ref[...] += dg

    return _rowwise(body, "mla_q_bwd", rows, ROW_TILE,
                    [(dq_full, nq, 0), (pb, MLA_Q_RANK, PB_CQ_BLK), (tabs["cx"], PAIR, 0), (tabs["sx"], PAIR, 0)],
                    [g.reshape(1, MLA_Q_RANK), w_up.astype(MXU_DTYPE), mats["p4t"]],
                    [(nq, MXU_DTYPE), (MLA_Q_RANK, MXU_DTYPE)], [((1, MLA_Q_RANK), F32)])


def _mla_kv_bwd(dk, dv, pb, g, w_up, tabs, mats):
    rows = dk.shape[0]

    def body(i, dk_ref, dr_ref, dv_ref, c_ref, cos_ref, sin_ref, g_ref, w_ref, a_ref, b_ref,
             dkv_ref, dc_ref, dm_ref, dg_ref):
        @pl.when(i == 0)
        def _():
            dg_ref[...] = jnp.zeros_like(dg_ref)

        dkv = jnp.concatenate([dk_ref[...], dv_ref[...]], axis=1).astype(dkv_ref.dtype)
        dkv_ref[...] = dkv
        dc, dg = _rms_bwd(c_ref[...], g_ref[...], _dot_nt(dkv, w_ref[...]))
        dc_ref[...] = dc.astype(dc_ref.dtype)
        dg_ref[...] += dg
        dr = dr_ref[...]
        dm_ref[...] = _dot_exact_r(dr * cos_ref[...], a_ref[...]) + _dot_exact_r(dr * sin_ref[...], b_ref[...])

    return _rowwise(body, "mla_kv_bwd", rows, ROW_TILE,
                    [(dk, GROUP_W, 0), (dk, PAIR, 2), (dv, GROUP_W, 0), (pb, MLA_KV_RANK, PB_CKV_BLK),
                     (tabs["cx"], PAIR, 0), (tabs["sx"], PAIR, 0)],
                    [g.reshape(1, MLA_KV_RANK), w_up.astype(MXU_DTYPE), mats["xa"], mats["xb"]],
                    [(MLA_KV_COLS, MXU_DTYPE), (MLA_KV_RANK, MXU_DTYPE), (PAIR, F32)], [((1, MLA_KV_RANK), F32)])


def _att_params(parallel):
    return pltpu.CompilerParams(dimension_semantics=("parallel" if parallel else "arbitrary",),
                                vmem_limit_bytes=VMEM_LIMIT)


def _blk_off(j, t):
    return j * t if isinstance(j, int) else pl.multiple_of(j * t, t)


def _causal_mask(t, strict):
    r = lax.broadcasted_iota(jnp.int32, (t, t), 0)
    c = lax.broadcasted_iota(jnp.int32, (t, t), 1)
    return (c < r) if strict else (c <= r)


def _lane_mask(kind, head, rows):
    lane = lax.broadcasted_iota(jnp.int32, (rows, PAIR), 1)
    if kind == "pair":
        return (lane < HEAD_DIM) if head % 2 == 0 else (lane >= HEAD_DIM)
    return (lane >= MLA_ROPE * head) & (lane < MLA_ROPE * (head + 1))


def _row_spec(t, cb, width=PAIR):
    return pl.BlockSpec((t, width), lambda i, cb=cb: (i, cb))


def _whole_spec(rows, cb, width=PAIR):
    return pl.BlockSpec((rows, width), lambda i, cb=cb: (0, cb), pipeline_mode=pl.Buffered(1))


def _is_pow2(x):
    return float(np.frexp(x)[0]) == 0.5


def _masked_heads(blocks, kinds, pair, dtype, scale=None):
    out = []
    for e in range(2):
        head = 2 * pair + e
        parts = [jnp.where(_lane_mask(k, head, b.shape[0]), b.astype(F32) * (1.0 if scale is None else scale),
                           0.0).astype(dtype)
                 for b, k in zip(blocks, kinds)]
        out.append(parts[0] if len(parts) == 1 else jnp.concatenate(parts, axis=1))
    return out


def _logit_reach(qh, kmax2, head):
    q32 = qh.astype(F32)
    return jnp.sqrt(jnp.sum(q32 * q32, axis=1, keepdims=True) * _col(kmax2, head)) * REACH_SLACK


def _forget_top(ft_ref, head, off):
    return jnp.max(-ft_ref[head:head + 1, pl.ds(off, PAIR)])


def _col(block, idx):
    lane = lax.broadcasted_iota(jnp.int32, block.shape, 1)
    return jnp.sum(jnp.where(lane == idx, block, 0.0), axis=1, keepdims=True)


def _scatter_cols(cols, t):
    lane = lax.broadcasted_iota(jnp.int32, (t, PAIR), 1)
    out = jnp.zeros((t, PAIR), F32)
    for idx, c in cols.items():
        out = out + jnp.where(lane == idx, c, 0.0)
    return out


def _take_heads(per_head, pair):
    return jnp.where(_lane_mask("pair", 0, per_head[0].shape[0]), per_head[0], per_head[1])


class _Parts:
    def __init__(self, q_parts, k_parts, v_parts, tq, sk):
        self.kinds = [[kind for _, _, kind in q_parts[p]] for p in range(2)]
        self.nparts = len(q_parts[0])
        self.q_specs = [_row_spec(tq, cb) for p in range(2) for _, cb, _ in q_parts[p]]
        self.q_args = [a for p in range(2) for a, _, _ in q_parts[p]]
        self.k_specs = [_whole_spec(sk, cb) for p in range(2) for _, cb, _ in k_parts[p]]
        self.k_args = [a for p in range(2) for a, _, _ in k_parts[p]]
        self.v_specs = [_whole_spec(sk, cb) for _, cb in v_parts]
        self.v_args = [a for a, _ in v_parts]
        self.width = PAIR * self.nparts

    def split(self, refs):
        n = self.nparts
        refs = list(refs)
        q = [refs[p * n:(p + 1) * n] for p in range(2)]
        k = [refs[2 * n + p * n:2 * n + (p + 1) * n] for p in range(2)]
        v = refs[4 * n:4 * n + 2]
        return q, k, v, refs[4 * n + 2:]

    def k_block(self, k_refs, off, t):
        blks = [r[pl.ds(off, t), :] for r in k_refs]
        return blks[0] if len(blks) == 1 else jnp.concatenate(blks, axis=1)


def _softmax_fwd(q_parts, k_parts, v_parts, sq, sk, scale, causal, bias, name):
    tq = min(SOFTMAX_FWD_TILE, sq) if causal else _pick(sq, (DENSE_QUERY_TILE, SOFTMAX_TILE))
    tk = tq if causal else min(SOFTMAX_TILE, sk)
    nkv = sk // tk
    pp = _Parts(q_parts, k_parts, v_parts, tq, sk)

    def kern(*refs):
        q_refs, k_refs, v_refs, rest = pp.split(refs)
        if bias is not None:
            fc_ref, ft_ref, kmax_ref, o_ref, lse_ref = rest
            fcb = fc_ref[...]
        else:
            o_ref, lse_ref = rest
        i = pl.program_id(0)
        fold = _is_pow2(scale)
        head_on = [jnp.where(_lane_mask("pair", e, tk), 1.0, 0.0).astype(MXU_DTYPE) for e in range(2)]
        head_off = [jnp.where(_lane_mask("pair", e, tk), 0.0, 1.0).astype(MXU_DTYPE) for e in range(2)]
        lse_cols = {}
        for first in range(0, 2, FWD_PAIRS_PER_LOOP):
            pairs = list(range(first, first + FWD_PAIRS_PER_LOOP))
            heads = [2 * p + e for p in pairs for e in range(2)]
            qm = {}
            for p in pairs:
                masked_q = _masked_heads([r[...] for r in q_refs[p]], pp.kinds[p], p, MXU_DTYPE, scale if fold else None)
                qm.update({2 * p + e: masked_q[e] for e in range(2)})
            if bias is not None:
                reach = {h: _logit_reach(qm[h], kmax_ref[...], h) for h in heads}

            def block(j, carry, masked, pairs=pairs, qm=qm):
                off = _blk_off(j, tk)
                out = []
                for p in pairs:
                    kb = pp.k_block(k_refs[p], off, tk)
                    vb = v_refs[p][pl.ds(off, tk), :]
                    for e in range(2):
                        h = 2 * p + e
                        m, acc = carry[len(out)]
                        s = _dot_nt(qm[h], kb) * (LOG2E if fold else scale * LOG2E)
                        if bias is not None:
                            s = s - ft_ref[h:h + 1, pl.ds(off, tk)] * LOG2E
                        if masked:
                            s = jnp.where(_causal_mask(tq, False), s, NEG)
                        m_new = jnp.maximum(m, jnp.max(s, axis=1, keepdims=True))
                        pr = jnp.exp2(s - m_new).astype(MXU_DTYPE)
                        out.append((m_new, jnp.exp2(m - m_new) * acc + _dot(pr, vb * head_on[e] + head_off[e])))
                return tuple(out)

            carry = tuple((jnp.full((tq, 1), NEG, F32), jnp.zeros((tq, PAIR), F32)) for _ in heads)
            if causal and bias is not None:
                def alive(c, j, heads=heads, reach=reach):
                    return functools.reduce(jnp.maximum, [
                        jnp.max(reach[h] + _forget_top(ft_ref, h, _blk_off(j, tk)) - c[n][0] * LN2)
                        for n, h in enumerate(heads)])

                def step(state, block=block, alive=alive):
                    n, _, c = state
                    c = block(i - 1 - n, c, False)
                    return n + 1, alive(c, i - 1 - n), c

                carry = block(i, carry, True)
                _, _, carry = lax.while_loop(lambda st: jnp.logical_and(st[0] < i, st[1] > DEAD_LOGIT), step,
                                             (jnp.int32(0), alive(carry, i), carry))
            elif causal:
                carry = lax.fori_loop(0, i, lambda j, c, block=block: block(j, c, False), carry)
                carry = block(i, carry, True)
            else:
                for j in range(nkv):
                    carry = block(j, carry, False)
            for n, p in enumerate(pairs):
                outs = []
                for e in range(2):
                    m, acc = carry[2 * n + e]
                    l = _col(acc, HEAD_DIM * (1 - e))
                    outs.append(acc / l)
                    lse_cols[2 * p + e] = m * LN2 + jnp.log(l) + (_col(fcb, 2 * p + e) if bias is not None else 0.0)
                o_ref[:, p * PAIR:(p + 1) * PAIR] = _take_heads(outs, p)
        lse_ref[...] = _scatter_cols(lse_cols, tq)

    in_specs = pp.q_specs + pp.k_specs + pp.v_specs
    args = pp.q_args + pp.k_args + pp.v_args
    if bias is not None:
        in_specs += [_row_spec(tq, 0), pl.BlockSpec((8, sk), lambda i: (0, 0), pipeline_mode=pl.Buffered(1)),
                     pl.BlockSpec((1, PAIR), lambda i: (0, 0))]
        args += list(bias)
    return pl.pallas_call(
        kern, name=name, grid=(sq // tq,), in_specs=in_specs,
        out_specs=[_row_spec(tq, 0, GROUP_W), _row_spec(tq, 0)],
        out_shape=[jax.ShapeDtypeStruct((sq, GROUP_W), F32), jax.ShapeDtypeStruct((sq, PAIR), F32)],
        compiler_params=_att_params(True),
    )(*args)


def _softmax_bwd(q_parts, k_parts, v_parts, o, lse, do, do_blk, sq, sk, scale, causal, bias, name):
    tq = min(SOFTMAX_TILE, sq) if causal else _pick(sq, (DENSE_QUERY_TILE, SOFTMAX_TILE))
    tk = tq if causal else min(SOFTMAX_TILE, sk)
    nkv = sk // tk
    pp = _Parts(q_parts, k_parts, v_parts, tq, sk)
    quad = pp.nparts == 2
    wq = GROUP_W + (PAIR if quad else 0)

    def kern(*refs):
        q_refs, k_refs, v_refs, rest = pp.split(refs)
        if bias is not None:
            o_ref, lse_ref, do_ref, fc_ref, ft_ref, kmax_ref, dq_ref, dk_ref, dv_ref, dfq_ref, dfk_ref = rest
            fcb = fc_ref[...]
        else:
            o_ref, lse_ref, do_ref, dq_ref, dk_ref, dv_ref = rest
        i = pl.program_id(0)
        fold = _is_pow2(scale)

        @pl.when(i == 0)
        def _():
            dk_ref[...] = jnp.zeros_like(dk_ref)
            dv_ref[...] = jnp.zeros_like(dv_ref)
            if bias is not None:
                dfk_ref[...] = jnp.zeros_like(dfk_ref)

        lse_b = lse_ref[...]
        qm, dom, delta, lse_h = [], [], [], []
        for p in range(2):
            qm += _masked_heads([r[...] for r in q_refs[p]], pp.kinds[p], p, MXU_DTYPE, scale if fold else None)
            do_p = do_ref[:, p * PAIR:(p + 1) * PAIR]
            dom += _masked_heads([do_p], ["pair"], p, MXU_DTYPE)
            prod = do_p * o_ref[:, p * PAIR:(p + 1) * PAIR]
            for e in range(2):
                h = 2 * p + e
                delta.append(jnp.sum(jnp.where(_lane_mask("pair", h, tq), prod, 0.0), axis=1, keepdims=True))
                lse_h.append(_col(lse_b, h) - (_col(fcb, h) if bias is not None else 0.0))

        def block(j, carry, masked, pairs):
            off = _blk_off(j, tk)
            out = []
            for p in pairs:
                kb = pp.k_block(k_refs[p], off, tk)
                vb = v_refs[p][pl.ds(off, tk), :]
                dk_acc = jnp.zeros((tk, pp.width), F32)
                dv_acc = jnp.zeros((tk, PAIR), F32)
                for e in range(2):
                    h = 2 * p + e
                    dq, dfq = carry[len(out)]
                    s = _dot_nt(qm[h], kb)
                    if not fold:
                        s = s * scale
                    if bias is not None:
                        s = s - ft_ref[h:h + 1, pl.ds(off, tk)]
                    if masked:
                        s = jnp.where(_causal_mask(tq, False), s, NEG)
                    pr = jnp.exp(s - lse_h[h])
                    ds = pr * (_dot_nt(dom[h], vb) - delta[h])
                    dsb = (ds if fold else ds * scale).astype(MXU_DTYPE)
                    dv_acc = dv_acc + _dot_tn(pr.astype(MXU_DTYPE), dom[h])
                    dk_acc = dk_acc + _dot_tn(dsb, qm[h])
                    dq = dq + _dot(dsb, kb)
                    if bias is not None:
                        dfq = dfq + jnp.sum(ds, axis=1, keepdims=True)
                        dfk_ref[h:h + 1, pl.ds(off, tk)] -= jnp.sum(ds, axis=0, keepdims=True)
                    out.append((dq, dfq))
                dv_ref[pl.ds(off, tk), p * PAIR:(p + 1) * PAIR] += dv_acc
                dk_ref[pl.ds(off, tk), p * PAIR:(p + 1) * PAIR] += dk_acc[:, :PAIR]
                if quad:
                    dk_ref[pl.ds(off, tk), GROUP_W:] += dk_acc[:, PAIR:]
            return tuple(out)

        done = {}
        for first in range(0, 2, BWD_PAIRS_PER_LOOP):
            pairs = list(range(first, first + BWD_PAIRS_PER_LOOP))
            heads = [2 * p + e for p in pairs for e in range(2)]
            carry = tuple((jnp.zeros((tq, pp.width), F32), jnp.zeros((tq, 1), F32)) for _ in heads)
            if causal and bias is not None:
                reach = {h: _logit_reach(qm[h], kmax_ref[...], h) - lse_h[h] for h in heads}

                def alive(j, heads=heads, reach=reach):
                    return functools.reduce(jnp.maximum, [
                        jnp.max(reach[h] + _forget_top(ft_ref, h, _blk_off(j, tk))) for h in heads])

                def step(state, pairs=pairs, alive=alive):
                    n, _, c = state
                    return n + 1, alive(i - 1 - n), block(i - 1 - n, c, False, pairs)

                carry = block(i, carry, True, pairs)
                _, _, carry = lax.while_loop(lambda st: jnp.logical_and(st[0] < i, st[1] > DEAD_LOGIT), step,
                                             (jnp.int32(0), alive(i), carry))
            elif causal:
                carry = lax.fori_loop(0, i, lambda j, c, pairs=pairs: block(j, c, False, pairs), carry)
                carry = block(i, carry, True, pairs)
            else:
                for j in range(nkv):
                    carry = block(j, carry, False, pairs)
            done.update(zip(heads, carry))
        carry = [done[h] for h in range(N_HEADS)]
        dqs = [c[0] * scale if fold else c[0] for c in carry]
        for p in range(2):
            dq_ref[:, p * PAIR:(p + 1) * PAIR] = _take_heads([dqs[2 * p + e][:, :PAIR] for e in range(2)], p)
        if quad:
            dq_ref[:, GROUP_W:] = sum(jnp.where(_lane_mask("quad", h, tq), dqs[h][:, PAIR:], 0.0)
                                      for h in range(N_HEADS))
        if bias is not None:
            dfq_ref[...] = _scatter_cols({h: carry[h][1] for h in range(N_HEADS)}, tq)

    acc_spec = lambda rows, width: pl.BlockSpec((rows, width), lambda i: (0, 0), pipeline_mode=pl.Buffered(1))
    in_specs = pp.q_specs + pp.k_specs + pp.v_specs + [_row_spec(tq, 0, GROUP_W), _row_spec(tq, 0),
                                                       _row_spec(tq, do_blk, GROUP_W)]
    args = pp.q_args + pp.k_args + pp.v_args + [o, lse, do]
    out_specs = [_row_spec(tq, 0, wq), acc_spec(sk, wq), acc_spec(sk, GROUP_W)]
    out_shape = [jax.ShapeDtypeStruct((sq, wq), F32), jax.ShapeDtypeStruct((sk, wq), F32),
                 jax.ShapeDtypeStruct((sk, GROUP_W), F32)]
    if bias is not None:
        in_specs += [_row_spec(tq, 0), pl.BlockSpec((8, sk), lambda i: (0, 0), pipeline_mode=pl.Buffered(1)),
                     pl.BlockSpec((1, PAIR), lambda i: (0, 0))]
        args += list(bias)
        out_specs += [_row_spec(tq, 0), acc_spec(8, sk)]
        out_shape += [jax.ShapeDtypeStruct((sq, PAIR), F32), jax.ShapeDtypeStruct((8, sk), F32)]
    return pl.pallas_call(
        kern, name=name, grid=(sq // tq,), in_specs=in_specs, out_specs=out_specs, out_shape=out_shape,
        compiler_params=_att_params(False),
    )(*args)


def _sb_logs(qh, kb, valid):
    z = _dot_nt(qh, kb)
    sp = _softplus(z)
    lk = -sp
    if valid is not None:
        lk = jnp.where(valid, lk, 0.0)
    return lk, z - sp


def _sb_valid(d, tq, tk):
    r = lax.broadcasted_iota(jnp.int32, (tq, tk), 0)
    c = lax.broadcasted_iota(jnp.int32, (tq, tk), 1)
    return c + d * tk < r


def _tri_sums(xs, tri):
    t = xs[0].shape[0]
    pieces = [_split2(x) for x in xs]
    hi = _dot(jnp.concatenate([pc[0] for pc in pieces], axis=0), tri)
    lo = _dot(jnp.concatenate([pc[1] for pc in pieces], axis=0), tri)
    return [hi[n * t:(n + 1) * t] + lo[n * t:(n + 1) * t] for n in range(len(xs))]


def _sb_fwd(src, q_blk, k_blk, v_blk, s, scale, name):
    assert _is_pow2(scale)
    tq, t = min(SB_QUERY_TILE, s), min(ATT_TILE, s)
    slots = -(-(s // t) // SB_SLOT) * SB_SLOT
    width = N_HEADS * slots
    band = tq // t
    pair = lambda blk: [[(src, blk + p, "pair")] for p in range(2)]
    pp = _Parts(pair(q_blk), pair(k_blk), [(src, v_blk + p) for p in range(2)], tq, s)

    def kern(*refs):
        q_refs, k_refs, v_refs, (tri_ref, o_ref, rm_ref, cnt_ref) = pp.split(refs)
        i = pl.program_id(0)
        tri = tri_ref[...]
        lane = lax.broadcasted_iota(jnp.int32, (tq, width), 1)
        qm = []
        for p in range(2):
            qm += _masked_heads([q_refs[p][0][...]], ["pair"], p, MXU_DTYPE, scale)

        def block(j, carry, valid):
            accs, rights, rm = carry
            off = _blk_off(j, t)
            kbs = [k_refs[p][0][pl.ds(off, t), :] for p in range(2)]
            vbs = [v_refs[p][pl.ds(off, t), :] for p in range(2)]
            logs = [_sb_logs(qm[h], kbs[h // 2], valid) for h in range(N_HEADS)]
            tails = _tri_sums([lg[0] for lg in logs], tri)
            new_acc, new_right = [], []
            for h in range(N_HEADS):
                lk, ls = logs[h]
                w = jnp.exp(ls + tails[h] + rights[h])
                if valid is not None:
                    w = jnp.where(valid, w, 0.0)
                new_acc.append(accs[h] + _dot(w.astype(MXU_DTYPE), vbs[h // 2]))
                rm = rm + jnp.where(lane == slots * h + j, rights[h], 0.0)
                new_right.append(rights[h] + jnp.sum(lk, axis=1, keepdims=True))
            return tuple(new_acc), tuple(new_right), rm

        carry = (tuple(jnp.zeros((tq, PAIR), F32) for _ in range(N_HEADS)),
                 tuple(jnp.zeros((tq, 1), F32) for _ in range(N_HEADS)), jnp.zeros((tq, width), F32))
        for d in reversed(range(band)):
            carry = block(band * i + d, carry, _sb_valid(d, tq, t))

        def alive(c):
            return functools.reduce(jnp.maximum, [jnp.max(r) for r in c[1]])

        def step(state):
            n, _, c = state
            c = block(band * i - 1 - n, c, None)
            return n + 1, alive(c), c

        n_done, _, carry = lax.while_loop(lambda st: jnp.logical_and(st[0] < band * i, st[1] > EXP_UNDERFLOW),
                                          step, (jnp.int32(0), alive(carry), carry))
        cnt_ref[i] = n_done
        for p in range(2):
            o_ref[:, p * PAIR:(p + 1) * PAIR] = _take_heads([carry[0][2 * p + e] for e in range(2)], p)
        rm_ref[...] = carry[2]

    return pl.pallas_call(
        kern, name=name, grid=(s // tq,),
        in_specs=pp.q_specs + pp.k_specs + pp.v_specs + [pl.BlockSpec((t, t), lambda i: (0, 0))],
        out_specs=[_row_spec(tq, 0, GROUP_W), _row_spec(tq, 0, width), pl.BlockSpec(memory_space=pltpu.SMEM)],
        out_shape=[jax.ShapeDtypeStruct((s, GROUP_W), F32), jax.ShapeDtypeStruct((s, width), F32),
                   jax.ShapeDtypeStruct((s // tq,), jnp.int32)],
        compiler_params=_att_params(False),
    )(*(pp.q_args + pp.k_args + pp.v_args + [_tri(t, "row_gt_col")]))


def _sb_bwd(src, q_blk, k_blk, v_blk, do, do_blk, rm, visited, s, scale, name):
    assert _is_pow2(scale)
    tq, t = min(SB_QUERY_TILE, s), min(ATT_TILE, s)
    band = tq // t
    pair = lambda blk: [[(src, blk + p, "pair")] for p in range(2)]
    pp = _Parts(pair(q_blk), pair(k_blk), [(src, v_blk + p) for p in range(2)], tq, s)

    def kern(*refs):
        q_refs, k_refs, v_refs, (do_ref, rm_ref, tri_ref, pre_ref, cnt_ref, dq_ref, dk_ref, dv_ref) = pp.split(refs)
        i = pl.program_id(0)

        @pl.when(i == 0)
        def _():
            dk_ref[...] = jnp.zeros_like(dk_ref)
            dv_ref[...] = jnp.zeros_like(dv_ref)

        rmb = rm_ref[...]
        tri = tri_ref[...]
        pre = pre_ref[...]
        qm, dom = [], []
        for p in range(2):
            qm += _masked_heads([q_refs[p][0][...]], ["pair"], p, MXU_DTYPE, scale)
            dom += _masked_heads([do_ref[:, p * PAIR:(p + 1) * PAIR]], ["pair"], p, MXU_DTYPE)

        def block(j, carry, valid):
            dqs, lefts = carry
            off = _blk_off(j, t)
            kbs = [k_refs[p][0][pl.ds(off, t), :] for p in range(2)]
            vbs = [v_refs[p][pl.ds(off, t), :] for p in range(2)]
            logs = [_sb_logs(qm[h], kbs[h // 2], valid) for h in range(N_HEADS)]
            tails = _tri_sums([lg[0] for lg in logs], tri)
            ws, gs = [], []
            for h in range(N_HEADS):
                lk, ls = logs[h]
                w = jnp.exp(ls + tails[h] + _col(rmb, (rm.shape[1] // N_HEADS) * h + j))
                if valid is not None:
                    w = jnp.where(valid, w, 0.0)
                ws.append(w)
                gs.append(_dot_nt(dom[h], vbs[h // 2]) * w)
            prefix = _tri_sums(gs, pre)
            new_dq, new_left = [], []
            dk_acc = [jnp.zeros((t, PAIR), F32) for _ in range(2)]
            dv_acc = [jnp.zeros((t, PAIR), F32) for _ in range(2)]
            for h in range(N_HEADS):
                lk, ls = logs[h]
                sig = jnp.exp(ls)
                dz = gs[h] * (1.0 - sig) - sig * (prefix[h] + lefts[h])
                if valid is not None:
                    dz = jnp.where(valid, dz, 0.0)
                dzb = dz.astype(MXU_DTYPE)
                dv_acc[h // 2] = dv_acc[h // 2] + _dot_tn(ws[h].astype(MXU_DTYPE), dom[h])
                dk_acc[h // 2] = dk_acc[h // 2] + _dot_tn(dzb, qm[h])
                new_dq.append(dqs[h] + _dot(dzb, kbs[h // 2]))
                new_left.append(lefts[h] + jnp.sum(gs[h], axis=1, keepdims=True))
            for p in range(2):
                dv_ref[pl.ds(off, t), p * PAIR:(p + 1) * PAIR] += dv_acc[p]
                dk_ref[pl.ds(off, t), p * PAIR:(p + 1) * PAIR] += dk_acc[p]
            return tuple(new_dq), tuple(new_left)

        carry = (tuple(jnp.zeros((tq, PAIR), F32) for _ in range(N_HEADS)),
                 tuple(jnp.zeros((tq, 1), F32) for _ in range(N_HEADS)))
        carry = lax.fori_loop(band * i - cnt_ref[i], band * i, lambda j, c: block(j, c, None), carry)
        for d in range(band):
            carry = block(band * i + d, carry, _sb_valid(d, tq, t))
        for p in range(2):
            dq_ref[:, p * PAIR:(p + 1) * PAIR] = _take_heads([carry[0][2 * p + e] * scale for e in range(2)], p)

    mspec = pl.BlockSpec((t, t), lambda i: (0, 0))
    acc_spec = pl.BlockSpec((s, GROUP_W), lambda i: (0, 0), pipeline_mode=pl.Buffered(1))
    return pl.pallas_call(
        kern, name=name, grid=(s // tq,),
        in_specs=pp.q_specs + pp.k_specs + pp.v_specs + [_row_spec(tq, do_blk, GROUP_W), _row_spec(tq, 0, rm.shape[1]), mspec, mspec,
                                                         pl.BlockSpec(memory_space=pltpu.SMEM)],
        out_specs=[_row_spec(tq, 0, GROUP_W), acc_spec, acc_spec],
        out_shape=[jax.ShapeDtypeStruct((s, GROUP_W), F32)] * 3,
        compiler_params=_att_params(False),
    )(*(pp.q_args + pp.k_args + pp.v_args + [do, rm, _tri(t, "row_gt_col"), _tri(t, "row_lt_col"), visited]))


def _split_w_in(w):
    col = lambda n: w[:, _OFF[n]:_OFF[n + 1]]
    wa = jnp.concatenate([col(0), col(1), col(2), col(4), col(5), col(6), col(10)], axis=1)
    misc = jnp.concatenate([col(3), col(9), jnp.zeros((w.shape[0], 128 - 4 - MLA_ROPE), w.dtype)], axis=1)
    wb = jnp.concatenate([col(11), col(7), col(8), misc], axis=1)
    return wa, wb


def _merge_dw_in(dwp):
    a = lambda n: dwp[:, n * GROUP_W:(n + 1) * GROUP_W]
    b0 = PA_COLS
    gate = dwp[:, b0:b0 + 1024]
    cq = dwp[:, b0 + 1024:b0 + 1280]
    ckv = dwp[:, b0 + 1280:b0 + 1408]
    flog = dwp[:, b0 + 1408:b0 + 1412]
    krot = dwp[:, b0 + 1408 + MISC_KROT:b0 + 1408 + MISC_KROT + MLA_ROPE]
    return jnp.concatenate([a(0), a(1), a(2), flog, a(3), a(4), a(5), cq, ckv, krot, a(6), gate], axis=1)


def _heads_first(w, per_head, first):
    r = w.shape[0]
    w3 = w.reshape(r, N_HEADS, per_head)
    return jnp.concatenate([w3[:, :, :first].reshape(r, -1), w3[:, :, first:].reshape(r, -1)], axis=1)


def _heads_interleaved(w, per_head, first):
    r = w.shape[0]
    a = w[:, :N_HEADS * first].reshape(r, N_HEADS, first)
    b = w[:, N_HEADS * first:].reshape(r, N_HEADS, per_head - first)
    return jnp.concatenate([a, b], axis=2).reshape(r, N_HEADS * per_head)


def _pad_rows8(a):
    return a[:, :8].T


def _local_step(x2, mem2, tgt, p, normed_x=None):
    s = x2.shape[0]
    nm = mem2.shape[0]
    head_scale = HEAD_DIM ** -0.5
    mla_scale = (HEAD_DIM + MLA_ROPE) ** -0.5
    tabs = _rope_tables(s)
    mats = _rope_matrices()
    pairs = lambda arr, blk: [[(arr, blk + q, "pair")] for q in range(2)]
    vals = lambda arr, blk: [(arr, blk + q) for q in range(2)]

    h, hb = normed_x if normed_x is not None else _ln_fwd(x2, p["ln_in_g"], p["ln_in_b"], "ln_in_fwd")
    _, memn_b = _ln_fwd(mem2, p["mem_ln_g"], p["mem_ln_b"], "ln_mem_fwd")

    saved = []
    for l in range(DEPTH):
        wa, wb = _split_w_in(p["w_in"][l])
        wp = jnp.concatenate([wa, wb], axis=1)
        wq_up = _heads_first(p["w_mla_q_up"][l], HEAD_DIM + MLA_ROPE, HEAD_DIM)
        wkv_up = _heads_first(p["w_mla_kv_up"][l], 2 * HEAD_DIM, HEAD_DIM)
        bias_row = jnp.pad(p["b_forget"][l], (0, 128 - N_HEADS)).reshape(1, 128)
        pa, pb, fox_kmax = _proj_in(hb, wa, wb, GROUP_W)

        fc = _forget_fwd(pb, bias_row)
        fbias = (fc, _pad_rows8(fc), fox_kmax)
        o_fox, lse_fox = _softmax_fwd(pairs(pa, 0), pairs(pa, 2), vals(pa, 4), s, s, head_scale, True, fbias,
                                      "fox_fwd")
        o_sb, *rm_sb = _sb_fwd(pa, 6, 8, 10, s, head_scale, "sb_fwd")

        qfull, cqn = _mla_q_fwd(pb, p["mla_q_norm_g"][l], wq_up, tabs, mats)
        kv, ckvn, kr4 = _mla_kv_fwd(pb, p["mla_kv_norm_g"][l], wkv_up, tabs, mats)
        mla_q = [[(qfull, q, "pair"), (qfull, 2, "quad")] for q in range(2)]
        mla_k = [[(kv, q, "pair"), (kr4, 0, "quad")] for q in range(2)]
        o_mla, lse_mla = _softmax_fwd(mla_q, mla_k, vals(kv, 2), s, s, mla_scale, True, None, "mla_fwd")

        mkv = _matmul(memn_b, p["w_mem_kv"][l], MXU_DTYPE, "mem_kv")
        o_mem, lse_mem = _softmax_fwd(pairs(pa, 12), pairs(mkv, 0), vals(mkv, 2), s, nm, head_scale, False, None,
                                      "mem_fwd")

        groups = (o_fox, o_sb, o_mla, o_mem)
        if l < DEPTH - 1:
            gated, u, h_next, hb_next = _gate_out_proj_ln(groups, pb, p["w_out"][l], h, p["ln_g"][l], p["ln_b"][l])
        else:
            gated, u, dh, sq_cols = _gate_out_proj_ln(groups, pb, p["w_out"][l], h, p["ln_g"][l], p["ln_b"][l], tgt)
        saved.append(dict(u=u, hb=hb, wp=wp, wq_up=wq_up, wkv_up=wkv_up, bias_row=bias_row, pa=pa, pb=pb,
                          fbias=fbias, lse_fox=lse_fox, rm_sb=rm_sb, cqn=cqn, ckvn=ckvn, mla_q=mla_q, mla_k=mla_k,
                          kv=kv, lse_mla=lse_mla, mkv=mkv, lse_mem=lse_mem, groups=groups, gated=gated))
        if l < DEPTH - 1:
            h, hb = h_next, hb_next

    loss_sum = jnp.sum(sq_cols)

    grads = {k: [None] * DEPTH for k in ("w_in", "b_forget", "mla_q_norm_g", "w_mla_q_up", "mla_kv_norm_g",
                                         "w_mla_kv_up", "w_mem_kv", "w_out", "ln_g", "ln_b")}
    dmemn = []
    dy1, dy2, c1 = dh, None, 1.0
    for l in reversed(range(DEPTH)):
        r = saved[l]
        pa, pb = r["pa"], r["pb"]
        o_fox, o_sb, o_mla, o_mem = r["groups"]
        du, du_b, dmixed, dgate_b, dg, db = _ln_bwd(r["u"], p["ln_g"][l], dy1, dy2, c1, "ln_gate_bwd",
                                                    (p["w_out"][l], r["groups"], pb))
        grads["ln_g"][l], grads["ln_b"][l] = dg[0], db[0]
        grads["w_out"][l] = _matmul(r["gated"], du_b, F32, "out_proj_dw", "tn")

        dfq, dfk, dfv, dfc_q, dfc_k = _softmax_bwd(pairs(pa, 0), pairs(pa, 2), vals(pa, 4), o_fox, r["lse_fox"],
                                                   dmixed, 0, s, s, head_scale, True, r["fbias"], "fox_bwd")
        dmisc_f, dbf = _forget_bwd(pb, r["bias_row"], dfc_q + jnp.pad(dfc_k.T, ((0, 0), (0, 128 - 8))))
        grads["b_forget"][l] = dbf[0, :N_HEADS]

        dsq, dsk, dsv = _sb_bwd(pa, 6, 8, 10, dmixed, 1, *r["rm_sb"], s, head_scale, "sb_bwd")

        dqm, dkm, dvm = _softmax_bwd(r["mla_q"], r["mla_k"], vals(r["kv"], 2), o_mla, r["lse_mla"], dmixed, 2,
                                     s, s, mla_scale, True, None, "mla_bwd")
        dq_mla_b, dcq_b, dgq = _mla_q_bwd(dqm, pb, p["mla_q_norm_g"][l], r["wq_up"], tabs, mats)
        grads["w_mla_q_up"][l] = _heads_interleaved(_matmul(r["cqn"], dq_mla_b, F32, "q_up_dw", "tn"),
                                                    HEAD_DIM + MLA_ROPE, HEAD_DIM)
        grads["mla_q_norm_g"][l] = dgq[0]
        dkv_b, dckv_b, dmisc_k, dgkv = _mla_kv_bwd(dkm, dvm, pb, p["mla_kv_norm_g"][l], r["wkv_up"], tabs, mats)
        grads["w_mla_kv_up"][l] = _heads_interleaved(_matmul(r["ckvn"], dkv_b, F32, "kv_up_dw", "tn"),
                                                     2 * HEAD_DIM, HEAD_DIM)
        grads["mla_kv_norm_g"][l] = dgkv[0]

        dmq, dmk, dmv = _softmax_bwd(pairs(pa, 12), pairs(r["mkv"], 0), vals(r["mkv"], 2), o_mem, r["lse_mem"],
                                     dmixed, 3, s, nm, head_scale, False, None, "mem_bwd")
        dmkv_b = jnp.concatenate([dmk, dmv], axis=1).astype(MXU_DTYPE)
        grads["w_mem_kv"][l] = _matmul(memn_b, dmkv_b, F32, "mem_kv_dw", "tn")
        dmemn.append(_matmul(dmkv_b, p["w_mem_kv"][l], F32, "mem_kv_dx", "nt"))

        dp = [dfq, dfk, dfv, dsq, dsk, dsv, dmq, dgate_b, dcq_b, dckv_b, (dmisc_f, dmisc_k)]
        dhproj = _proj_dx(dp, r["wp"])
        grads["w_in"][l] = _merge_dw_in(_proj_dw(r["hb"], dp))
        dy1, dy2, c1 = du, dhproj, ALPHA

    dx, dg_in, db_in = _ln_bwd(x2, p["ln_in_g"], dy1, dy2, c1, "ln_in_bwd")
    _, dg_mem, db_mem = _ln_bwd(mem2, p["mem_ln_g"], dmemn[0], dmemn[1], 1.0, "ln_mem_bwd")
    out = {k: jnp.stack(v) for k, v in grads.items()}
    out.update(ln_in_g=dg_in[0], ln_in_b=db_in[0], mem_ln_g=dg_mem[0], mem_ln_b=db_mem[0])
    return loss_sum, dx, out


WIDE = "w_in"
FLAT_NAMES = ("w_out", "w_mem_kv", "w_mla_q_up", "w_mla_kv_up")
FLAT_ROWS = 896
BIG_NAMES = (WIDE,) + FLAT_NAMES
BIG_AXIS = dict(w_in=2, w_out=1, w_mem_kv=1, w_mla_q_up=2, w_mla_kv_up=2)
SMALL_NAMES = ("ln_in_g", "ln_in_b", "mem_ln_g", "mem_ln_b", "ln_g", "ln_b", "b_forget", "mla_q_norm_g",
               "mla_kv_norm_g")
ALL_NAMES = ("ln_in_g", "ln_in_b", "mem_ln_g", "mem_ln_b", "w_in", "b_forget", "mla_q_norm_g", "w_mla_q_up",
             "mla_kv_norm_g", "w_mla_kv_up", "w_mem_kv", "w_out", "ln_g", "ln_b")
N_CHIPS = 4
N_DEV = 8


def _rows_of(shape):
    rows = -(-int(np.prod(shape)) // LANES)
    return -(-rows // PACK_ALIGN) * PACK_ALIGN


def _pack(arrs, rows):
    parts = []
    for a in arrs:
        f = a.reshape(-1)
        n = _rows_of(a.shape) * LANES
        parts.append(jnp.pad(f, (0, n - f.shape[0])).reshape(-1, LANES))
    used = sum(q.shape[0] for q in parts)
    if rows > used:
        parts.append(jnp.zeros((rows - used, LANES), parts[0].dtype))
    return jnp.concatenate(parts, axis=0)


def _unpack(buf, shapes):
    out, r = [], 0
    for shp in shapes:
        n = _rows_of(shp)
        out.append(buf[r:r + n].reshape(-1)[:int(np.prod(shp))].reshape(shp))
        r += n
    return out


def _sharded_pair(get):
    wide = get(WIDE)
    return [wide.reshape(-1, wide.shape[-1]), _pack([get(n) for n in FLAT_NAMES], FLAT_ROWS)]


HBM_SPEC = pl.BlockSpec(memory_space=pltpu.HBM)


def _gather_weights(shards, x2, g, beta):
    n = len(shards)
    rows, d = x2.shape
    ch = min(GATHER_LN_ROWS, rows)
    steps = rows // ch

    def body(*refs):
        w_refs, (x_ref, g_ref, be_ref), out_refs = refs[:n], refs[n:n + 3], refs[n + 3:2 * n + 3]
        h_ref, hb_ref = refs[2 * n + 3:2 * n + 5]
        (xbuf, hbuf, hbbuf, send_sems, recv_sems, local_sems, in_sems, h_sems, hb_sems) = refs[2 * n + 5:]
        x, y, c = (lax.axis_index(a) for a in MESH_AXES)
        me, sibling = 2 * x + y, (x, y, 1 - c)
        chips = [(1 - x, y), (x, 1 - y), (1 - x, 1 - y)]
        local, first, passed = [], [], []
        for a in range(n):
            w_ref, out_ref, half = w_refs[a], out_refs[a], shards[a].shape[0] // 2

            def part(chip, core, out_ref=out_ref, half=half):
                return out_ref.at[chip, pl.ds(core * half, half)]

            def copy(k, src, dst, to, a=a):
                return pltpu.make_async_remote_copy(
                    src_ref=src, dst_ref=dst, send_sem=send_sems.at[6 * a + k], recv_sem=recv_sems.at[6 * a + k],
                    device_id=to, device_id_type=pl.DeviceIdType.MESH)

            local.append(pltpu.make_async_copy(w_ref, out_ref.at[me], local_sems.at[a]))
            local[-1].start()
            mine = [copy(k, w_ref.at[pl.ds(c * half, half)], part(me, c), (px, py, c))
                    for k, (px, py) in enumerate(chips)]
            for cp in mine:
                cp.start()
            first.append((mine, part, copy))

        def fetch(k):
            return pltpu.make_async_copy(x_ref.at[pl.ds(k * ch, ch)], xbuf.at[k % 2], in_sems.at[k % 2])

        def put(k):
            return (pltpu.make_async_copy(hbuf.at[k % 2], h_ref.at[pl.ds(k * ch, ch)], h_sems.at[k % 2]),
                    pltpu.make_async_copy(hbbuf.at[k % 2], hb_ref.at[pl.ds(k * ch, ch)], hb_sems.at[k % 2]))

        fetch(0).start()
        for k in range(steps):
            if k + 1 < steps:
                fetch(k + 1).start()
            fetch(k).wait()
            if k >= 2:
                for cp in put(k - 2):
                    cp.wait()
            xc, rstd = _ln_stats(xbuf[k % 2])
            yy = xc * rstd * g_ref[...] + be_ref[...]
            hbuf[k % 2] = yy
            hbbuf[k % 2] = yy.astype(hbbuf.dtype)
            for cp in put(k):
                cp.start()
        for k in range(max(steps - 2, 0), steps):
            for cp in put(k):
                cp.wait()

        for mine, part, copy in first:
            for k, (px, py) in enumerate(chips):
                copy(k, part(me, c), part(2 * px + py, c), (px, py, c)).wait_recv()
                passed.append(copy(3 + k, part(2 * px + py, c), part(2 * px + py, c), sibling))
                passed[-1].start()
        for mine, part, copy in first:
            for k, (px, py) in enumerate(chips):
                copy(3 + k, part(me, c), part(2 * px + py, 1 - c), sibling).wait_recv()
        for cp in [cp for mine, _, _ in first for cp in mine] + passed:
            cp.wait_send()
        for cp in local:
            cp.wait()

    vmem = pl.BlockSpec(memory_space=pltpu.VMEM)
    return pl.pallas_call(
        body, name="gather_weights",
        out_shape=[jax.ShapeDtypeStruct((N_CHIPS,) + s.shape, s.dtype) for s in shards]
        + [jax.ShapeDtypeStruct((rows, d), F32), jax.ShapeDtypeStruct((rows, d), MXU_DTYPE)],
        in_specs=[HBM_SPEC] * (n + 1) + [vmem, vmem], out_specs=[HBM_SPEC] * (n + 2),
        scratch_shapes=[pltpu.VMEM((2, ch, d), F32), pltpu.VMEM((2, ch, d), F32), pltpu.VMEM((2, ch, d), MXU_DTYPE),
                        pltpu.SemaphoreType.DMA((6 * n,)), pltpu.SemaphoreType.DMA((6 * n,)),
                        pltpu.SemaphoreType.DMA((n,)), pltpu.SemaphoreType.DMA((2,)), pltpu.SemaphoreType.DMA((2,)),
                        pltpu.SemaphoreType.DMA((2,))],
        compiler_params=pltpu.CompilerParams(vmem_limit_bytes=VMEM_LIMIT),
    )(*shards, x2, g.reshape(1, d), beta.reshape(1, d))


def _exchange_grads(bigs, small):
    nb = len(bigs)
    halves = [b.shape[1] // 2 for b in bigs]
    chunks = [_pick(h, (128, 64, 32, 16)) for h in halves]

    def body(*refs):
        big_refs, small_ref = refs[:nb], refs[nb]
        sum_refs, out_refs = refs[nb + 1:2 * nb + 1], refs[2 * nb + 1:3 * nb + 2]
        scratch = refs[3 * nb + 2:]
        land_bufs, sum_bufs = scratch[:nb], scratch[nb:2 * nb]
        send_sems, recv_sems, local_sems, load_sems, swap_send, swap_recv, keep_sems = scratch[2 * nb:]
        x, y, c = (lax.axis_index(a) for a in MESH_AXES)
        me, my_chip = 4 * x + 2 * y + c, 2 * x + y
        flips = [(fx, fy, fc) for fx in (0, 1) for fy in (0, 1) for fc in (0, 1) if fx or fy or fc]
        peers = [(1 - x if fx else x, 1 - y if fy else y, 1 - c if fc else c) for fx, fy, fc in flips]

        sources = [lambda chip, core, r=big_refs[a], half=bigs[a].shape[1] // 2: r.at[chip, pl.ds(core * half, half)]
                   for a in range(nb)] + [lambda chip, core: small_ref]

        def copy(a, k, src, slot, to):
            return pltpu.make_async_remote_copy(
                src_ref=src, dst_ref=out_refs[a].at[slot], send_sem=send_sems.at[7 * a + k],
                recv_sem=recv_sems.at[7 * a + k], device_id=to, device_id_type=pl.DeviceIdType.MESH)

        own = [pltpu.make_async_copy(src(my_chip, c), out_refs[a].at[me], local_sems.at[a])
               for a, src in enumerate(sources)]
        for cp in own:
            cp.start()
        sends = [copy(a, k, src(2 * px + py, pc), me, (px, py, pc))
                 for a, src in enumerate(sources) for k, (px, py, pc) in enumerate(peers)]
        for cp in sends:
            cp.start()
        for a, src in enumerate(sources):
            for k, (px, py, pc) in enumerate(peers):
                copy(a, k, src(my_chip, c), 4 * px + 2 * py + pc, (px, py, pc)).wait_recv()
        for cp in sends:
            cp.wait_send()
        for cp in own:
            cp.wait()

        tails = []
        loads = [pltpu.make_async_copy(out_refs[a], land_bufs[a], load_sems.at[a]) for a in range(nb)]
        for load in loads:
            load.start()
        for a in range(nb):
            loads[a].wait()
            for r0 in range(0, halves[a], chunks[a]):
                rows = pl.ds(r0, chunks[a])
                total = land_bufs[a][0, rows, :].astype(F32)
                for d in range(1, N_DEV):
                    total = total + land_bufs[a][d, rows, :].astype(F32)
                sum_bufs[a][rows, :] = total
            keep = pltpu.make_async_copy(sum_bufs[a], sum_refs[a].at[c], keep_sems.at[a])
            give = pltpu.make_async_remote_copy(
                src_ref=sum_bufs[a], dst_ref=sum_refs[a].at[c], send_sem=swap_send.at[a], recv_sem=swap_recv.at[a],
                device_id=(x, y, 1 - c), device_id_type=pl.DeviceIdType.MESH)
            keep.start()
            give.start()
            tails.append((keep, give))
        for a, (keep, give) in enumerate(tails):
            pltpu.make_async_remote_copy(
                src_ref=sum_bufs[a], dst_ref=sum_refs[a].at[1 - c], send_sem=swap_send.at[a], recv_sem=swap_recv.at[a],
                device_id=(x, y, 1 - c), device_id_type=pl.DeviceIdType.MESH).wait_recv()
            give.wait_send()
            keep.wait()

    return pl.pallas_call(
        body, name="exchange_grads",
        out_shape=[jax.ShapeDtypeStruct((2, h, b.shape[2]), F32) for h, b in zip(halves, bigs)]
        + [jax.ShapeDtypeStruct((N_DEV, h, b.shape[2]), b.dtype) for h, b in zip(halves, bigs)]
        + [jax.ShapeDtypeStruct((N_DEV,) + small.shape, small.dtype)],
        in_specs=[HBM_SPEC] * (nb + 1), out_specs=[HBM_SPEC] * (2 * nb + 1),
        scratch_shapes=[pltpu.VMEM((N_DEV, h, b.shape[2]), b.dtype) for h, b in zip(halves, bigs)]
        + [pltpu.VMEM((h, b.shape[2]), F32) for h, b in zip(halves, bigs)]
        + [pltpu.SemaphoreType.DMA((7 * (nb + 1),)), pltpu.SemaphoreType.DMA((7 * (nb + 1),)),
           pltpu.SemaphoreType.DMA((nb + 1,)), pltpu.SemaphoreType.DMA((nb,)), pltpu.SemaphoreType.DMA((nb,)),
           pltpu.SemaphoreType.DMA((nb,)), pltpu.SemaphoreType.DMA((nb,))],
        compiler_params=pltpu.CompilerParams(vmem_limit_bytes=VMEM_LIMIT),
    )(*bigs, small)


def _adamw(parts, w, m, v, name):
    rows, width = w.shape
    n_parts = parts.shape[0]
    tile = _pick(rows, (512, 448, 144, 128, 16, 8))
    bc1 = 1.0 - ADAM_B1 ** ADAM_STEP
    bc2 = 1.0 - ADAM_B2 ** ADAM_STEP

    def kern(p_ref, w_ref, m_ref, v_ref, g_ref, d_ref, nm_ref, nv_ref):
        g = p_ref[0].astype(F32)
        for d in range(1, n_parts):
            g = g + p_ref[d].astype(F32)
        nm = ADAM_B1 * m_ref[...] + (1.0 - ADAM_B1) * g
        nv = ADAM_B2 * v_ref[...] + (1.0 - ADAM_B2) * (g * g)
        g_ref[...] = g
        nm_ref[...] = nm
        nv_ref[...] = nv
        d_ref[...] = -ADAM_LR * ((nm / bc1) / (jnp.sqrt(nv / bc2) + ADAM_EPS) + ADAM_WD * w_ref[...])

    spec = pl.BlockSpec((tile, width), lambda i: (i, 0))
    return pl.pallas_call(
        kern, name=name, grid=(rows // tile,),
        in_specs=[pl.BlockSpec((n_parts, tile, width), lambda i: (0, i, 0)), spec, spec, spec],
        out_specs=[spec] * 4, out_shape=[jax.ShapeDtypeStruct((rows, width), F32)] * 4,
        compiler_params=pltpu.CompilerParams(dimension_semantics=("parallel",), vmem_limit_bytes=VMEM_LIMIT),
    )(parts, w, m, v)


def kernel(x, mem, ln_in_g, ln_in_b, mem_ln_g, mem_ln_b, w_in, b_forget, mla_q_norm_g, w_mla_q_up, mla_kv_norm_g, w_mla_kv_up, w_mem_kv, w_out, ln_g, ln_b, loss_target, m_ln_in_g, m_ln_in_b, m_mem_ln_g, m_mem_ln_b, m_w_in, m_b_forget, m_mla_q_norm_g, m_w_mla_q_up, m_mla_kv_norm_g, m_w_mla_kv_up, m_w_mem_kv, m_w_out, m_ln_g, m_ln_b, v_ln_in_g, v_ln_in_b, v_mem_ln_g, v_mem_ln_b, v_w_in, v_b_forget, v_mla_q_norm_g, v_w_mla_q_up, v_mla_kv_norm_g, v_w_mla_kv_up, v_w_mem_kv, v_w_out, v_ln_g, v_ln_b):
    w = dict(ln_in_g=ln_in_g, ln_in_b=ln_in_b, mem_ln_g=mem_ln_g, mem_ln_b=mem_ln_b, w_in=w_in, b_forget=b_forget,
             mla_q_norm_g=mla_q_norm_g, w_mla_q_up=w_mla_q_up, mla_kv_norm_g=mla_kv_norm_g,
             w_mla_kv_up=w_mla_kv_up, w_mem_kv=w_mem_kv, w_out=w_out, ln_g=ln_g, ln_b=ln_b)
    mo = dict(ln_in_g=m_ln_in_g, ln_in_b=m_ln_in_b, mem_ln_g=m_mem_ln_g, mem_ln_b=m_mem_ln_b, w_in=m_w_in,
              b_forget=m_b_forget, mla_q_norm_g=m_mla_q_norm_g, w_mla_q_up=m_w_mla_q_up,
              mla_kv_norm_g=m_mla_kv_norm_g, w_mla_kv_up=m_w_mla_kv_up, w_mem_kv=m_w_mem_kv, w_out=m_w_out,
              ln_g=m_ln_g, ln_b=m_ln_b)
    vo = dict(ln_in_g=v_ln_in_g, ln_in_b=v_ln_in_b, mem_ln_g=v_mem_ln_g, mem_ln_b=v_mem_ln_b, w_in=v_w_in,
              b_forget=v_b_forget, mla_q_norm_g=v_mla_q_norm_g, w_mla_q_up=v_w_mla_q_up,
              mla_kv_norm_g=v_mla_kv_norm_g, w_mla_kv_up=v_w_mla_kv_up, w_mem_kv=v_w_mem_kv, w_out=v_w_out,
              ln_g=v_ln_g, ln_b=v_ln_b)
    flat_shapes = [w[n].shape for n in FLAT_NAMES]
    small_shapes = [w[n].shape for n in SMALL_NAMES]

    got_wide, got_flat, h0, hb0 = _gather_weights(_sharded_pair(lambda n: w[n].astype(MXU_DTYPE)), x[0], ln_in_g,
                                                  ln_in_b)
    full = dict(w)
    full[WIDE] = jnp.concatenate([got_wide[j] for j in range(N_CHIPS)], axis=1).reshape(
        w[WIDE].shape[:2] + (N_CHIPS * w[WIDE].shape[2],))
    per_chip = [_unpack(got_flat[j], flat_shapes) for j in range(N_CHIPS)]
    for idx, n in enumerate(FLAT_NAMES):
        full[n] = jnp.concatenate([per_chip[j][idx] for j in range(N_CHIPS)], axis=BIG_AXIS[n])

    loss_sum, dx, g = _local_step(x[0], mem[0], loss_target[0], full, (h0, hb0))
    loss = lax.psum(loss_sum * (0.5 / D_MODEL), MESH_AXES)

    def shard_of(n, j):
        ax, size = BIG_AXIS[n], w[n].shape[BIG_AXIS[n]]
        return lax.slice_in_dim(g[n], j * size, (j + 1) * size, axis=ax).astype(MXU_DTYPE)

    per_dest = [_sharded_pair(lambda n, j=j: shard_of(n, j)) for j in range(N_CHIPS)]
    bigs = [jnp.stack([per_dest[j][a] for j in range(N_CHIPS)]) for a in range(2)]
    exchanged = _exchange_grads(bigs, _pack([g[n] for n in SMALL_NAMES], SMALL_ROWS))
    halves, small_parts = exchanged[:2], exchanged[-1]

    res = []
    for a, (grad, nm) in enumerate(zip(halves, ("adamw_wide", "adamw_flat"))):
        state = [_sharded_pair(lambda n, src=src: src[n])[a] for src in (w, mo, vo)]
        res.append(_adamw(grad.reshape((1,) + state[0].shape), *state, nm))
    res_small = _adamw(small_parts, *[_pack([src[n] for n in SMALL_NAMES], SMALL_ROWS) for src in (w, mo, vo)],
                       "adamw_replicated")
    outs = []
    for kind in range(4):
        vals = {WIDE: res[0][kind].reshape(w[WIDE].shape)}
        vals.update(zip(FLAT_NAMES, _unpack(res[1][kind], flat_shapes)))
        vals.update(zip(SMALL_NAMES, _unpack(res_small[kind], small_shapes)))
        outs += [vals[n] for n in ALL_NAMES]
    return (loss, dx[None], *outs)
```
